```python
import jax, jax.numpy as jnp
from jax import lax
import numpy as np

D_MODEL = 1024
BATCH = 8
SEQ = 4096
DEPTH = 1

N_META = 16
D_RNN = 1024
N_RNN_HEADS = 4
RNN_HEAD_DIM = D_RNN // N_RNN_HEADS
RNN_CONV_WIDTH = 4
RG_LRU_C = 8.0
D_CONV = 1024
CONV_WIDTH = 31
D_FF = 2816
FFN_RESIDUAL_WEIGHT = 0.5
EPS = 1e-6
IN_SIZES = (D_RNN, D_RNN, D_CONV, D_CONV, D_MODEL, D_MODEL)
IN_TOTAL = sum(IN_SIZES)
IN_SPLITS = tuple(int(v) for v in np.cumsum(IN_SIZES)[:-1])

kernel_name = "hybrid_rglru_conformer_macaron"


def rmsnorm(x, g):
    xf = x.astype(jnp.float32)
    y = xf * lax.rsqrt(jnp.mean(xf * xf, axis=-1, keepdims=True) + EPS)
    return (y * g.astype(jnp.float32)).astype(x.dtype)


def layernorm(x, g, b):
    xf = x.astype(jnp.float32)
    mu = jnp.mean(xf, axis=-1, keepdims=True)
    var = jnp.mean(jnp.square(xf - mu), axis=-1, keepdims=True)
    y = (xf - mu) * lax.rsqrt(var + EPS)
    return (y * g.astype(jnp.float32) + b.astype(jnp.float32)).astype(x.dtype)


def swiglu_ffn(h, w_gu, w_down):
    gate, up = jnp.split(h @ w_gu, 2, axis=-1)
    return (jax.nn.silu(gate) * up) @ w_down


def causal_depthwise_conv(x, w, b):
    k_width, channels = w.shape
    out = lax.conv_general_dilated(
        x, w[:, None, :].astype(x.dtype), window_strides=(1,), padding=((k_width - 1, 0),),
        dimension_numbers=("NWC", "WIO", "NWC"), feature_group_count=channels)
    return out + b


def rg_lru(x, w_a, b_a, w_x, b_x, lam):
    bsz, t_len, _ = x.shape
    xb = x.reshape(bsz, t_len, N_RNN_HEADS, RNN_HEAD_DIM)
    r = jax.nn.sigmoid((jnp.einsum("bthi,hij->bthj", xb, w_a).reshape(bsz, t_len, D_RNN) + b_a).astype(jnp.float32))
    i = jax.nn.sigmoid((jnp.einsum("bthi,hij->bthj", xb, w_x).reshape(bsz, t_len, D_RNN) + b_x).astype(jnp.float32))
    log_a = -RG_LRU_C * r * jax.nn.softplus(-lam.astype(jnp.float32))
    a = jnp.exp(log_a)
    u = jnp.sqrt(-jnp.expm1(2.0 * log_a)) * (i * x.astype(jnp.float32))

    def combine(left, right):
        a_l, h_l = left
        a_r, h_r = right
        return a_l * a_r, a_r * h_l + h_r

    _, h = lax.associative_scan(combine, (a, u), axis=1)
    return h.astype(x.dtype)


def token_mixer(h, w_in, b_in, rnn_conv_w, rnn_conv_b, rg_w_a, rg_b_a, rg_w_x, rg_b_x, rg_lambda,
                rnn_w_proj, conv_dw_w, conv_dw_b, conv_ln_g, conv_ln_b, conv_w_proj, conv_b_proj, w_out):
    proj = h @ w_in + b_in
    x_rnn, y_rnn, glu_v, glu_g, gate_a, gate_b = jnp.split(proj, IN_SPLITS, axis=-1)
    xr = causal_depthwise_conv(x_rnn, rnn_conv_w, rnn_conv_b)
    xr = rg_lru(xr, rg_w_a, rg_b_a, rg_w_x, rg_b_x, rg_lambda)
    y_a = (xr * jax.nn.gelu(y_rnn)) @ rnn_w_proj
    v = glu_v * jax.nn.sigmoid(glu_g)
    v = causal_depthwise_conv(v, conv_dw_w, conv_dw_b)
    v = jax.nn.silu(layernorm(v, conv_ln_g, conv_ln_b))
    y_b = v @ conv_w_proj + conv_b_proj
    merged = jax.nn.sigmoid(gate_a) * y_a + jax.nn.sigmoid(gate_b) * y_b
    return merged @ w_out


def _fwd_setup_inputs(seed: int = 0) -> dict:
    key = jax.random.key(seed)
    ks = iter(jax.random.split(key, 40))
    L = DEPTH
    f32 = jnp.float32

    def nrm(shape, fan_in):
        return jax.random.normal(next(ks), shape, f32) * (fan_in ** -0.5)

    def gain(shape):
        return 1.0 + 0.02 * jax.random.normal(next(ks), shape, f32)

    def bias(shape):
        return 0.01 * jax.random.normal(next(ks), shape, f32)

    x = jax.random.normal(next(ks), (BATCH, SEQ, D_MODEL), f32)
    meta_tokens = jax.random.normal(next(ks), (N_META, D_MODEL), f32)
    u = jax.random.uniform(next(ks), (L, D_RNN), f32, minval=0.9, maxval=0.999)
    s = u ** (1.0 / RG_LRU_C)
    rg_lambda = jnp.log(s) - jnp.log1p(-s)
    return {
        "x": x,
        "meta_tokens": meta_tokens,
        "ffn1_norm": gain((L, D_MODEL)),
        "ffn1_w_gu": nrm((L, D_MODEL, 2 * D_FF), D_MODEL),
        "ffn1_w_down": nrm((L, D_FF, D_MODEL), D_FF),
        "mix_norm": gain((L, D_MODEL)),
        "w_in": nrm((L, D_MODEL, IN_TOTAL), D_MODEL),
        "b_in": bias((L, IN_TOTAL)),
        "rnn_conv_w": nrm((L, RNN_CONV_WIDTH, D_RNN), RNN_CONV_WIDTH),
        "rnn_conv_b": bias((L, D_RNN)),
        "rg_w_a": nrm((L, N_RNN_HEADS, RNN_HEAD_DIM, RNN_HEAD_DIM), RNN_HEAD_DIM),
        "rg_b_a": bias((L, D_RNN)),
        "rg_w_x": nrm((L, N_RNN_HEADS, RNN_HEAD_DIM, RNN_HEAD_DIM), RNN_HEAD_DIM),
        "rg_b_x": bias((L, D_RNN)),
        "rg_lambda": rg_lambda,
        "rnn_w_proj": nrm((L, D_RNN, D_MODEL), D_RNN),
        "conv_dw_w": nrm((L, CONV_WIDTH, D_CONV), CONV_WIDTH),
        "conv_dw_b": bias((L, D_CONV)),
        "conv_ln_g": gain((L, D_CONV)),
        "conv_ln_b": bias((L, D_CONV)),
        "conv_w_proj": nrm((L, D_CONV, D_MODEL), D_CONV),
        "conv_b_proj": bias((L, D_MODEL)),
        "w_out": nrm((L, D_MODEL, D_MODEL), D_MODEL),
        "ffn2_norm": gain((L, D_MODEL)),
        "ffn2_w_gu": nrm((L, D_MODEL, 2 * D_FF), D_MODEL),
        "ffn2_w_down": nrm((L, D_FF, D_MODEL), D_FF),
        "final_norm": gain((D_MODEL,)),
    }


def _fwd_reference(x, meta_tokens, ffn1_norm, ffn1_w_gu, ffn1_w_down, mix_norm, w_in, b_in,
              rnn_conv_w, rnn_conv_b, rg_w_a, rg_b_a, rg_w_x, rg_b_x, rg_lambda, rnn_w_proj,
              conv_dw_w, conv_dw_b, conv_ln_g, conv_ln_b, conv_w_proj, conv_b_proj, w_out,
              ffn2_norm, ffn2_w_gu, ffn2_w_down, final_norm):
    bsz = x.shape[0]
    meta = jnp.broadcast_to(meta_tokens.astype(x.dtype)[None], (bsz, N_META, x.shape[-1]))
    h = jnp.concatenate([meta, x], axis=1)
    for l in range(DEPTH):
        h = h + FFN_RESIDUAL_WEIGHT * swiglu_ffn(rmsnorm(h, ffn1_norm[l]), ffn1_w_gu[l], ffn1_w_down[l])
        h = h + token_mixer(rmsnorm(h, mix_norm[l]), w_in[l], b_in[l], rnn_conv_w[l], rnn_conv_b[l],
                            rg_w_a[l], rg_b_a[l], rg_w_x[l], rg_b_x[l], rg_lambda[l], rnn_w_proj[l],
                            conv_dw_w[l], conv_dw_b[l], conv_ln_g[l], conv_ln_b[l], conv_w_proj[l],
                            conv_b_proj[l], w_out[l])
        h = h + FFN_RESIDUAL_WEIGHT * swiglu_ffn(rmsnorm(h, ffn2_norm[l]), ffn2_w_gu[l], ffn2_w_down[l])
    return rmsnorm(h, final_norm)[:, N_META:, :]


import jax as _jax
import jax.numpy as _jnp

TWIN_FORMAT = 'train_step'
FWD_PARAMS = ['x', 'meta_tokens', 'ffn1_norm', 'ffn1_w_gu', 'ffn1_w_down', 'mix_norm', 'w_in', 'b_in', 'rnn_conv_w', 'rnn_conv_b', 'rg_w_a', 'rg_b_a', 'rg_w_x', 'rg_b_x', 'rg_lambda', 'rnn_w_proj', 'conv_dw_w', 'conv_dw_b', 'conv_ln_g', 'conv_ln_b', 'conv_w_proj', 'conv_b_proj', 'w_out', 'ffn2_norm', 'ffn2_w_gu', 'ffn2_w_down', 'final_norm']
TWIN_WEIGHTS = ['meta_tokens', 'ffn1_norm', 'ffn1_w_gu', 'ffn1_w_down', 'mix_norm', 'w_in', 'b_in', 'rnn_conv_w', 'rnn_conv_b', 'rg_w_a', 'rg_b_a', 'rg_w_x', 'rg_b_x', 'rg_lambda', 'rnn_w_proj', 'conv_dw_w', 'conv_dw_b', 'conv_ln_g', 'conv_ln_b', 'conv_w_proj', 'conv_b_proj', 'w_out', 'ffn2_norm', 'ffn2_w_gu', 'ffn2_w_down', 'final_norm']
TWIN_DIFF_INPUT = 'x'
TWIN_INPUTS = ['x', 'meta_tokens', 'ffn1_norm', 'ffn1_w_gu', 'ffn1_w_down', 'mix_norm', 'w_in', 'b_in', 'rnn_conv_w', 'rnn_conv_b', 'rg_w_a', 'rg_b_a', 'rg_w_x', 'rg_b_x', 'rg_lambda', 'rnn_w_proj', 'conv_dw_w', 'conv_dw_b', 'conv_ln_g', 'conv_ln_b', 'conv_w_proj', 'conv_b_proj', 'w_out', 'ffn2_norm', 'ffn2_w_gu', 'ffn2_w_down', 'final_norm', 'loss_target', 'm_meta_tokens', 'm_ffn1_norm', 'm_ffn1_w_gu', 'm_ffn1_w_down', 'm_mix_norm', 'm_w_in', 'm_b_in', 'm_rnn_conv_w', 'm_rnn_conv_b', 'm_rg_w_a', 'm_rg_b_a', 'm_rg_w_x', 'm_rg_b_x', 'm_rg_lambda', 'm_rnn_w_proj', 'm_conv_dw_w', 'm_conv_dw_b', 'm_conv_ln_g', 'm_conv_ln_b', 'm_conv_w_proj', 'm_conv_b_proj', 'm_w_out', 'm_ffn2_norm', 'm_ffn2_w_gu', 'm_ffn2_w_down', 'm_final_norm', 'v_meta_tokens', 'v_ffn1_norm', 'v_ffn1_w_gu', 'v_ffn1_w_down', 'v_mix_norm', 'v_w_in', 'v_b_in', 'v_rnn_conv_w', 'v_rnn_conv_b', 'v_rg_w_a', 'v_rg_b_a', 'v_rg_w_x', 'v_rg_b_x', 'v_rg_lambda', 'v_rnn_w_proj', 'v_conv_dw_w', 'v_conv_dw_b', 'v_conv_ln_g', 'v_conv_ln_b', 'v_conv_w_proj', 'v_conv_b_proj', 'v_w_out', 'v_ffn2_norm', 'v_ffn2_w_gu', 'v_ffn2_w_down', 'v_final_norm']
TWIN_OUTPUTS = ['loss', 'grad_x', 'grad_meta_tokens', 'grad_ffn1_norm', 'grad_ffn1_w_gu', 'grad_ffn1_w_down', 'grad_mix_norm', 'grad_w_in', 'grad_b_in', 'grad_rnn_conv_w', 'grad_rnn_conv_b', 'grad_rg_w_a', 'grad_rg_b_a', 'grad_rg_w_x', 'grad_rg_b_x', 'grad_rg_lambda', 'grad_rnn_w_proj', 'grad_conv_dw_w', 'grad_conv_dw_b', 'grad_conv_ln_g', 'grad_conv_ln_b', 'grad_conv_w_proj', 'grad_conv_b_proj', 'grad_w_out', 'grad_ffn2_norm', 'grad_ffn2_w_gu', 'grad_ffn2_w_down', 'grad_final_norm', 'delta_meta_tokens', 'delta_ffn1_norm', 'delta_ffn1_w_gu', 'delta_ffn1_w_down', 'delta_mix_norm', 'delta_w_in', 'delta_b_in', 'delta_rnn_conv_w', 'delta_rnn_conv_b', 'delta_rg_w_a', 'delta_rg_b_a', 'delta_rg_w_x', 'delta_rg_b_x', 'delta_rg_lambda', 'delta_rnn_w_proj', 'delta_conv_dw_w', 'delta_conv_dw_b', 'delta_conv_ln_g', 'delta_conv_ln_b', 'delta_conv_w_proj', 'delta_conv_b_proj', 'delta_w_out', 'delta_ffn2_norm', 'delta_ffn2_w_gu', 'delta_ffn2_w_down', 'delta_final_norm', 'new_m_meta_tokens', 'new_m_ffn1_norm', 'new_m_ffn1_w_gu', 'new_m_ffn1_w_down', 'new_m_mix_norm', 'new_m_w_in', 'new_m_b_in', 'new_m_rnn_conv_w', 'new_m_rnn_conv_b', 'new_m_rg_w_a', 'new_m_rg_b_a', 'new_m_rg_w_x', 'new_m_rg_b_x', 'new_m_rg_lambda', 'new_m_rnn_w_proj', 'new_m_conv_dw_w', 'new_m_conv_dw_b', 'new_m_conv_ln_g', 'new_m_conv_ln_b', 'new_m_conv_w_proj', 'new_m_conv_b_proj', 'new_m_w_out', 'new_m_ffn2_norm', 'new_m_ffn2_w_gu', 'new_m_ffn2_w_down', 'new_m_final_norm', 'new_v_meta_tokens', 'new_v_ffn1_norm', 'new_v_ffn1_w_gu', 'new_v_ffn1_w_down', 'new_v_mix_norm', 'new_v_w_in', 'new_v_b_in', 'new_v_rnn_conv_w', 'new_v_rnn_conv_b', 'new_v_rg_w_a', 'new_v_rg_b_a', 'new_v_rg_w_x', 'new_v_rg_b_x', 'new_v_rg_lambda', 'new_v_rnn_w_proj', 'new_v_conv_dw_w', 'new_v_conv_dw_b', 'new_v_conv_ln_g', 'new_v_conv_ln_b', 'new_v_conv_w_proj', 'new_v_conv_b_proj', 'new_v_w_out', 'new_v_ffn2_norm', 'new_v_ffn2_w_gu', 'new_v_ffn2_w_down', 'new_v_final_norm']
TWIN_LEAF_KINDS = {'loss': 'loss', 'grad_x': 'grad_x', 'grad_meta_tokens': 'grad_w', 'grad_ffn1_norm': 'grad_w', 'grad_ffn1_w_gu': 'grad_w', 'grad_ffn1_w_down': 'grad_w', 'grad_mix_norm': 'grad_w', 'grad_w_in': 'grad_w', 'grad_b_in': 'grad_w', 'grad_rnn_conv_w': 'grad_w', 'grad_rnn_conv_b': 'grad_w', 'grad_rg_w_a': 'grad_w', 'grad_rg_b_a': 'grad_w', 'grad_rg_w_x': 'grad_w', 'grad_rg_b_x': 'grad_w', 'grad_rg_lambda': 'grad_w', 'grad_rnn_w_proj': 'grad_w', 'grad_conv_dw_w': 'grad_w', 'grad_conv_dw_b': 'grad_w', 'grad_conv_ln_g': 'grad_w', 'grad_conv_ln_b': 'grad_w', 'grad_conv_w_proj': 'grad_w', 'grad_conv_b_proj': 'grad_w', 'grad_w_out': 'grad_w', 'grad_ffn2_norm': 'grad_w', 'grad_ffn2_w_gu': 'grad_w', 'grad_ffn2_w_down': 'grad_w', 'grad_final_norm': 'grad_w', 'delta_meta_tokens': 'delta_w', 'delta_ffn1_norm': 'delta_w', 'delta_ffn1_w_gu': 'delta_w', 'delta_ffn1_w_down': 'delta_w', 'delta_mix_norm': 'delta_w', 'delta_w_in': 'delta_w', 'delta_b_in': 'delta_w', 'delta_rnn_conv_w': 'delta_w', 'delta_rnn_conv_b': 'delta_w', 'delta_rg_w_a': 'delta_w', 'delta_rg_b_a': 'delta_w', 'delta_rg_w_x': 'delta_w', 'delta_rg_b_x': 'delta_w', 'delta_rg_lambda': 'delta_w', 'delta_rnn_w_proj': 'delta_w', 'delta_conv_dw_w': 'delta_w', 'delta_conv_dw_b': 'delta_w', 'delta_conv_ln_g': 'delta_w', 'delta_conv_ln_b': 'delta_w', 'delta_conv_w_proj': 'delta_w', 'delta_conv_b_proj': 'delta_w', 'delta_w_out': 'delta_w', 'delta_ffn2_norm': 'delta_w', 'delta_ffn2_w_gu': 'delta_w', 'delta_ffn2_w_down': 'delta_w', 'delta_final_norm': 'delta_w', 'new_m_meta_tokens': 'new_m', 'new_m_ffn1_norm': 'new_m', 'new_m_ffn1_w_gu': 'new_m', 'new_m_ffn1_w_down': 'new_m', 'new_m_mix_norm': 'new_m', 'new_m_w_in': 'new_m', 'new_m_b_in': 'new_m', 'new_m_rnn_conv_w': 'new_m', 'new_m_rnn_conv_b': 'new_m', 'new_m_rg_w_a': 'new_m', 'new_m_rg_b_a': 'new_m', 'new_m_rg_w_x': 'new_m', 'new_m_rg_b_x': 'new_m', 'new_m_rg_lambda': 'new_m', 'new_m_rnn_w_proj': 'new_m', 'new_m_conv_dw_w': 'new_m', 'new_m_conv_dw_b': 'new_m', 'new_m_conv_ln_g': 'new_m', 'new_m_conv_ln_b': 'new_m', 'new_m_conv_w_proj': 'new_m', 'new_m_conv_b_proj': 'new_m', 'new_m_w_out': 'new_m', 'new_m_ffn2_norm': 'new_m', 'new_m_ffn2_w_gu': 'new_m', 'new_m_ffn2_w_down': 'new_m', 'new_m_final_norm': 'new_m', 'new_v_meta_tokens': 'new_v', 'new_v_ffn1_norm': 'new_v', 'new_v_ffn1_w_gu': 'new_v', 'new_v_ffn1_w_down': 'new_v', 'new_v_mix_norm': 'new_v', 'new_v_w_in': 'new_v', 'new_v_b_in': 'new_v', 'new_v_rnn_conv_w': 'new_v', 'new_v_rnn_conv_b': 'new_v', 'new_v_rg_w_a': 'new_v', 'new_v_rg_b_a': 'new_v', 'new_v_rg_w_x': 'new_v', 'new_v_rg_b_x': 'new_v', 'new_v_rg_lambda': 'new_v', 'new_v_rnn_w_proj': 'new_v', 'new_v_conv_dw_w': 'new_v', 'new_v_conv_dw_b': 'new_v', 'new_v_conv_ln_g': 'new_v', 'new_v_conv_ln_b': 'new_v', 'new_v_conv_w_proj': 'new_v', 'new_v_conv_b_proj': 'new_v', 'new_v_w_out': 'new_v', 'new_v_ffn2_norm': 'new_v', 'new_v_ffn2_w_gu': 'new_v', 'new_v_ffn2_w_down': 'new_v', 'new_v_final_norm': 'new_v'}


def _forward(args):
    return _fwd_reference(*[args[k] for k in FWD_PARAMS])


def _output_shape():
    def fwd():
        inp = _fwd_setup_inputs(0)
        return _fwd_reference(*[inp[k] for k in FWD_PARAMS])
    out = _jax.eval_shape(fwd)
    return out.shape, out.dtype

N_MICROBATCH = 1
ADAM_LR = 0.001
ADAM_B1 = 0.9
ADAM_B2 = 0.999
ADAM_EPS = 1e-08
ADAM_WD = 0.01
ADAM_STEP = 10
PER_EXAMPLE_BATCH_AXIS = {'x': 0, 'loss_target': 0}
SHARED_INPUTS = []
_WEIGHT_DTYPES = {'meta_tokens': _jnp.float32, 'ffn1_norm': _jnp.float32, 'ffn1_w_gu': _jnp.float32, 'ffn1_w_down': _jnp.float32, 'mix_norm': _jnp.float32, 'w_in': _jnp.float32, 'b_in': _jnp.float32, 'rnn_conv_w': _jnp.float32, 'rnn_conv_b': _jnp.float32, 'rg_w_a': _jnp.float32, 'rg_b_a': _jnp.float32, 'rg_w_x': _jnp.float32, 'rg_b_x': _jnp.float32, 'rg_lambda': _jnp.float32, 'rnn_w_proj': _jnp.float32, 'conv_dw_w': _jnp.float32, 'conv_dw_b': _jnp.float32, 'conv_ln_g': _jnp.float32, 'conv_ln_b': _jnp.float32, 'conv_w_proj': _jnp.float32, 'conv_b_proj': _jnp.float32, 'w_out': _jnp.float32, 'ffn2_norm': _jnp.float32, 'ffn2_w_gu': _jnp.float32, 'ffn2_w_down': _jnp.float32, 'final_norm': _jnp.float32}
MOMENT_SCALE = {'meta_tokens': 4.244933e-03, 'ffn1_norm': 8.218522e-02, 'ffn1_w_gu': 3.341580e-02, 'ffn1_w_down': 5.454053e-02, 'mix_norm': 8.570998e-02, 'w_in': 3.355456e-02, 'b_in': 1.811543e-01, 'rnn_conv_w': 3.735296e-02, 'rnn_conv_b': 3.787497e-01, 'rg_w_a': 1.072149e-02, 'rg_b_a': 8.127993e-03, 'rg_w_x': 1.878891e-02, 'rg_b_x': 1.419597e-02, 'rg_lambda': 1.714702e-02, 'rnn_w_proj': 3.573344e-02, 'conv_dw_w': 5.570479e-02, 'conv_dw_b': 1.052751e-01, 'conv_ln_g': 6.567040e-02, 'conv_ln_b': 5.726210e-02, 'conv_w_proj': 5.453467e-02, 'conv_b_proj': 1.036455e-01, 'w_out': 6.491592e-02, 'ffn2_norm': 6.463988e-02, 'ffn2_w_gu': 2.818630e-02, 'ffn2_w_down': 4.595606e-02, 'final_norm': 3.198118e+01}


def _to_microbatches(a, axis):
    t = _jnp.moveaxis(a, axis, 0)
    t = t.reshape((N_MICROBATCH, t.shape[0] // N_MICROBATCH) + t.shape[1:])
    return _jnp.moveaxis(t, 1, axis + 1)


def setup_inputs(seed: int = 0) -> dict:
    inp = _fwd_setup_inputs(seed)
    key = _jax.random.fold_in(_jax.random.key(seed), 7919)
    shape, _ = _output_shape()
    out = dict(inp)
    out["loss_target"] = _jax.random.normal(_jax.random.fold_in(key, 0), shape, _jnp.float32)
    for i, name in enumerate(TWIN_WEIGHTS):
        w = inp[name].astype(_jnp.float32)
        if MOMENT_SCALE is None:
            s = _jnp.sqrt(_jnp.mean(_jnp.square(w)) + 1e-30)
        else:
            s = MOMENT_SCALE[name]
        km, kv = _jax.random.split(_jax.random.fold_in(key, i + 1))
        out[name] = w
        out["m_" + name] = s * _jax.random.normal(km, w.shape, _jnp.float32)
        out["v_" + name] = (s * s) * _jax.random.uniform(kv, w.shape, _jnp.float32, 0.5, 1.5)
    if N_MICROBATCH > 1:
        for name, axis in PER_EXAMPLE_BATCH_AXIS.items():
            out[name] = _to_microbatches(out[name], axis)
    return {'x': out['x'], 'meta_tokens': out['meta_tokens'], 'ffn1_norm': out['ffn1_norm'], 'ffn1_w_gu': out['ffn1_w_gu'], 'ffn1_w_down': out['ffn1_w_down'], 'mix_norm': out['mix_norm'], 'w_in': out['w_in'], 'b_in': out['b_in'], 'rnn_conv_w': out['rnn_conv_w'], 'rnn_conv_b': out['rnn_conv_b'], 'rg_w_a': out['rg_w_a'], 'rg_b_a': out['rg_b_a'], 'rg_w_x': out['rg_w_x'], 'rg_b_x': out['rg_b_x'], 'rg_lambda': out['rg_lambda'], 'rnn_w_proj': out['rnn_w_proj'], 'conv_dw_w': out['conv_dw_w'], 'conv_dw_b': out['conv_dw_b'], 'conv_ln_g': out['conv_ln_g'], 'conv_ln_b': out['conv_ln_b'], 'conv_w_proj': out['conv_w_proj'], 'conv_b_proj': out['conv_b_proj'], 'w_out': out['w_out'], 'ffn2_norm': out['ffn2_norm'], 'ffn2_w_gu': out['ffn2_w_gu'], 'ffn2_w_down': out['ffn2_w_down'], 'final_norm': out['final_norm'], 'loss_target': out['loss_target'], 'm_meta_tokens': out['m_meta_tokens'], 'm_ffn1_norm': out['m_ffn1_norm'], 'm_ffn1_w_gu': out['m_ffn1_w_gu'], 'm_ffn1_w_down': out['m_ffn1_w_down'], 'm_mix_norm': out['m_mix_norm'], 'm_w_in': out['m_w_in'], 'm_b_in': out['m_b_in'], 'm_rnn_conv_w': out['m_rnn_conv_w'], 'm_rnn_conv_b': out['m_rnn_conv_b'], 'm_rg_w_a': out['m_rg_w_a'], 'm_rg_b_a': out['m_rg_b_a'], 'm_rg_w_x': out['m_rg_w_x'], 'm_rg_b_x': out['m_rg_b_x'], 'm_rg_lambda': out['m_rg_lambda'], 'm_rnn_w_proj': out['m_rnn_w_proj'], 'm_conv_dw_w': out['m_conv_dw_w'], 'm_conv_dw_b': out['m_conv_dw_b'], 'm_conv_ln_g': out['m_conv_ln_g'], 'm_conv_ln_b': out['m_conv_ln_b'], 'm_conv_w_proj': out['m_conv_w_proj'], 'm_conv_b_proj': out['m_conv_b_proj'], 'm_w_out': out['m_w_out'], 'm_ffn2_norm': out['m_ffn2_norm'], 'm_ffn2_w_gu': out['m_ffn2_w_gu'], 'm_ffn2_w_down': out['m_ffn2_w_down'], 'm_final_norm': out['m_final_norm'], 'v_meta_tokens': out['v_meta_tokens'], 'v_ffn1_norm': out['v_ffn1_norm'], 'v_ffn1_w_gu': out['v_ffn1_w_gu'], 'v_ffn1_w_down': out['v_ffn1_w_down'], 'v_mix_norm': out['v_mix_norm'], 'v_w_in': out['v_w_in'], 'v_b_in': out['v_b_in'], 'v_rnn_conv_w': out['v_rnn_conv_w'], 'v_rnn_conv_b': out['v_rnn_conv_b'], 'v_rg_w_a': out['v_rg_w_a'], 'v_rg_b_a': out['v_rg_b_a'], 'v_rg_w_x': out['v_rg_w_x'], 'v_rg_b_x': out['v_rg_b_x'], 'v_rg_lambda': out['v_rg_lambda'], 'v_rnn_w_proj': out['v_rnn_w_proj'], 'v_conv_dw_w': out['v_conv_dw_w'], 'v_conv_dw_b': out['v_conv_dw_b'], 'v_conv_ln_g': out['v_conv_ln_g'], 'v_conv_ln_b': out['v_conv_ln_b'], 'v_conv_w_proj': out['v_conv_w_proj'], 'v_conv_b_proj': out['v_conv_b_proj'], 'v_w_out': out['v_w_out'], 'v_ffn2_norm': out['v_ffn2_norm'], 'v_ffn2_w_gu': out['v_ffn2_w_gu'], 'v_ffn2_w_down': out['v_ffn2_w_down'], 'v_final_norm': out['v_final_norm']}


def _loss(weights, diff, rest, loss_target):
    with _jax.named_scope("forward"):
        args = {**rest, TWIN_DIFF_INPUT: diff, **{k: w.astype(_WEIGHT_DTYPES[k]) for k, w in weights.items()}}
        y = _forward(args)
    with _jax.named_scope("loss_head"):
        err = _jnp.square(y.astype(_jnp.float32) - loss_target)
        return 0.5 * _jnp.sum(_jnp.mean(err, axis=-1)) if err.ndim else 0.5 * err


def _adamw(w, g, m, v):
    m = ADAM_B1 * m + (1.0 - ADAM_B1) * g
    v = ADAM_B2 * v + (1.0 - ADAM_B2) * _jnp.square(g)
    m_hat = m / (1.0 - ADAM_B1 ** ADAM_STEP)
    v_hat = v / (1.0 - ADAM_B2 ** ADAM_STEP)
    delta = -ADAM_LR * (m_hat / (_jnp.sqrt(v_hat) + ADAM_EPS) + ADAM_WD * w)
    return delta, m, v


def reference(x, meta_tokens, ffn1_norm, ffn1_w_gu, ffn1_w_down, mix_norm, w_in, b_in, rnn_conv_w, rnn_conv_b, rg_w_a, rg_b_a, rg_w_x, rg_b_x, rg_lambda, rnn_w_proj, conv_dw_w, conv_dw_b, conv_ln_g, conv_ln_b, conv_w_proj, conv_b_proj, w_out, ffn2_norm, ffn2_w_gu, ffn2_w_down, final_norm, loss_target, m_meta_tokens, m_ffn1_norm, m_ffn1_w_gu, m_ffn1_w_down, m_mix_norm, m_w_in, m_b_in, m_rnn_conv_w, m_rnn_conv_b, m_rg_w_a, m_rg_b_a, m_rg_w_x, m_rg_b_x, m_rg_lambda, m_rnn_w_proj, m_conv_dw_w, m_conv_dw_b, m_conv_ln_g, m_conv_ln_b, m_conv_w_proj, m_conv_b_proj, m_w_out, m_ffn2_norm, m_ffn2_w_gu, m_ffn2_w_down, m_final_norm, v_meta_tokens, v_ffn1_norm, v_ffn1_w_gu, v_ffn1_w_down, v_mix_norm, v_w_in, v_b_in, v_rnn_conv_w, v_rnn_conv_b, v_rg_w_a, v_rg_b_a, v_rg_w_x, v_rg_b_x, v_rg_lambda, v_rnn_w_proj, v_conv_dw_w, v_conv_dw_b, v_conv_ln_g, v_conv_ln_b, v_conv_w_proj, v_conv_b_proj, v_w_out, v_ffn2_norm, v_ffn2_w_gu, v_ffn2_w_down, v_final_norm):
    given = dict(x=x, meta_tokens=meta_tokens, ffn1_norm=ffn1_norm, ffn1_w_gu=ffn1_w_gu, ffn1_w_down=ffn1_w_down, mix_norm=mix_norm, w_in=w_in, b_in=b_in, rnn_conv_w=rnn_conv_w, rnn_conv_b=rnn_conv_b, rg_w_a=rg_w_a, rg_b_a=rg_b_a, rg_w_x=rg_w_x, rg_b_x=rg_b_x, rg_lambda=rg_lambda, rnn_w_proj=rnn_w_proj, conv_dw_w=conv_dw_w, conv_dw_b=conv_dw_b, conv_ln_g=conv_ln_g, conv_ln_b=conv_ln_b, conv_w_proj=conv_w_proj, conv_b_proj=conv_b_proj, w_out=w_out, ffn2_norm=ffn2_norm, ffn2_w_gu=ffn2_w_gu, ffn2_w_down=ffn2_w_down, final_norm=final_norm, loss_target=loss_target, m_meta_tokens=m_meta_tokens, m_ffn1_norm=m_ffn1_norm, m_ffn1_w_gu=m_ffn1_w_gu, m_ffn1_w_down=m_ffn1_w_down, m_mix_norm=m_mix_norm, m_w_in=m_w_in, m_b_in=m_b_in, m_rnn_conv_w=m_rnn_conv_w, m_rnn_conv_b=m_rnn_conv_b, m_rg_w_a=m_rg_w_a, m_rg_b_a=m_rg_b_a, m_rg_w_x=m_rg_w_x, m_rg_b_x=m_rg_b_x, m_rg_lambda=m_rg_lambda, m_rnn_w_proj=m_rnn_w_proj, m_conv_dw_w=m_conv_dw_w, m_conv_dw_b=m_conv_dw_b, m_conv_ln_g=m_conv_ln_g, m_conv_ln_b=m_conv_ln_b, m_conv_w_proj=m_conv_w_proj, m_conv_b_proj=m_conv_b_proj, m_w_out=m_w_out, m_ffn2_norm=m_ffn2_norm, m_ffn2_w_gu=m_ffn2_w_gu, m_ffn2_w_down=m_ffn2_w_down, m_final_norm=m_final_norm, v_meta_tokens=v_meta_tokens, v_ffn1_norm=v_ffn1_norm, v_ffn1_w_gu=v_ffn1_w_gu, v_ffn1_w_down=v_ffn1_w_down, v_mix_norm=v_mix_norm, v_w_in=v_w_in, v_b_in=v_b_in, v_rnn_conv_w=v_rnn_conv_w, v_rnn_conv_b=v_rnn_conv_b, v_rg_w_a=v_rg_w_a, v_rg_b_a=v_rg_b_a, v_rg_w_x=v_rg_w_x, v_rg_b_x=v_rg_b_x, v_rg_lambda=v_rg_lambda, v_rnn_w_proj=v_rnn_w_proj, v_conv_dw_w=v_conv_dw_w, v_conv_dw_b=v_conv_dw_b, v_conv_ln_g=v_conv_ln_g, v_conv_ln_b=v_conv_ln_b, v_conv_w_proj=v_conv_w_proj, v_conv_b_proj=v_conv_b_proj, v_w_out=v_w_out, v_ffn2_norm=v_ffn2_norm, v_ffn2_w_gu=v_ffn2_w_gu, v_ffn2_w_down=v_ffn2_w_down, v_final_norm=v_final_norm)
    weights = {n: given[n] for n in TWIN_WEIGHTS}
    shared = {n: given[n] for n in SHARED_INPUTS}
    per_example = {n: given[n] for n in ['x']}
    grad_fn = _jax.value_and_grad(_loss, argnums=(0, 1))

    def one_microbatch(ex, loss_target):
        ex = dict(ex)
        diff = ex.pop(TWIN_DIFF_INPUT)
        return grad_fn(weights, diff, {**shared, **ex}, loss_target)

    if N_MICROBATCH == 1:
        loss, (grad_w, grad_x) = one_microbatch(per_example, given["loss_target"])
    else:
        def body(carry, xs):
            loss_sum, grad_sum = carry
            l_k, (gw_k, gx_k) = one_microbatch(xs[0], xs[1])
            with _jax.named_scope("update"):
                return (loss_sum + l_k, _jax.tree.map(_jnp.add, grad_sum, gw_k)), gx_k

        init = (_jnp.zeros((), _jnp.float32), _jax.tree.map(_jnp.zeros_like, weights))
        (loss, grad_w), grad_x = _jax.lax.scan(body, init, (per_example, given["loss_target"]))
    with _jax.named_scope("update"):
        delta_w, new_m, new_v = {}, {}, {}
        for n in TWIN_WEIGHTS:
            delta_w[n], new_m[n], new_v[n] = _adamw(weights[n], grad_w[n], given["m_" + n], given["v_" + n])
    return (loss, grad_x, *[grad_w[n] for n in TWIN_WEIGHTS], *[delta_w[n] for n in TWIN_WEIGHTS],
            *[new_m[n] for n in TWIN_WEIGHTS], *[new_v[n] for n in TWIN_WEIGHTS])
```

```python
import functools
import math

import jax
import jax.numpy as jnp
from jax import lax
from jax.experimental import pallas as pl
from jax.experimental.pallas import tpu as pltpu

F32 = jnp.float32
BF16 = jnp.bfloat16
MESH = pl.DeviceIdType.MESH
N_DEV = 8
N_HEADS = 4
RG_LRU_C = 8.0
EPS = 1e-6
FFN_RES = 0.5
ADAM_LR, ADAM_B1, ADAM_B2, ADAM_EPS, ADAM_WD, ADAM_STEP = 0.001, 0.9, 0.999, 1e-08, 0.01, 10
V7X_VMEM_LIMIT = 56 * 1024 * 1024
CONV4_HALO = 8
CONV31_HALO = 32
SUBLANES = 8
GELU_C = math.sqrt(2.0 / math.pi)
GELU_K = 0.044715


def _any():
    return pl.BlockSpec(memory_space=pl.ANY)


def _params(n_grid):
    return pltpu.CompilerParams(dimension_semantics=("arbitrary",) * n_grid, vmem_limit_bytes=V7X_VMEM_LIMIT)


def _nn(a, b):
    return jnp.dot(a, b, preferred_element_type=F32)


def _nt(a, b):
    return lax.dot_general(a, b, (((1,), (1,)), ((), ())), preferred_element_type=F32)


def _tn(a, b):
    return lax.dot_general(a, b, (((0,), (0,)), ((), ())), preferred_element_type=F32)


def _sigmoid(x):
    return jax.nn.sigmoid(x)


def _rowsum(x):
    return jnp.sum(x, axis=0, keepdims=True)


def _rms_fwd(x, g):
    r = lax.rsqrt(jnp.mean(x * x, axis=-1, keepdims=True) + EPS)
    return x * r * g, r


def _rms_bwd(dn, x, r, g):
    xr = x * r
    gy = dn * g
    dx = r * (gy - xr * jnp.mean(gy * xr, axis=-1, keepdims=True))
    return dx, _rowsum(dn * xr)


def _gelu(y):
    t = jnp.tanh(GELU_C * (y + GELU_K * y * y * y))
    return 0.5 * y * (1.0 + t), t


def _gelu_grad(y, t):
    return 0.5 * (1.0 + t) + 0.5 * y * (1.0 - t * t) * GELU_C * (1.0 + 3.0 * GELU_K * y * y)


def _softplus(x):
    return jnp.maximum(x, 0.0) + jnp.log(1.0 + jnp.exp(-jnp.abs(x)))


def _one_minus_exp(z):
    series = -z * (1.0 + 0.5 * z * (1.0 + z * (1.0 / 3.0) * (1.0 + 0.25 * z)))
    return jnp.where(z > -0.05, series, 1.0 - jnp.exp(z))


def _tiles(t_real):
    if t_real > 2048:
        tm = 384
        tp = -(-t_real // tm) * tm
        return tp, tm, tm // 2, tp // 4
    tm = 128
    tp = -(-t_real // tm) * tm
    return tp, tm, tm // 2, tm


def _load_weights(copies, sems):
    cps = [pltpu.make_async_copy(s, d, sems.at[k]) for k, (s, d) in enumerate(copies)]
    for cp in cps:
        cp.start()
    for cp in cps:
        cp.wait()


def _position():
    x, y, c = lax.axis_index("x"), lax.axis_index("y"), lax.axis_index("c")
    chips = [(1 - x, y), (x, 1 - y), (1 - x, 1 - y)]
    return x, y, c, chips


def _all_gather(shards):
    n = len(shards)

    def body(*refs):
        ins, outs = refs[:n], refs[n:2 * n]
        send_sems, recv_sems, local_sems = refs[2 * n:]
        x, y, c, chips = _position()
        me, sibling = (x, y, c), (x, y, 1 - c)

        def slot(p):
            return 4 * p[0] + 2 * p[1] + p[2]

        def copy(a, k, block, to, src=None):
            dst = outs[a].at[slot(block)]
            return pltpu.make_async_remote_copy(
                src_ref=dst if src is None else src, dst_ref=dst,
                send_sem=send_sems.at[7 * a + k], recv_sem=recv_sems.at[7 * a + k],
                device_id=to, device_id_type=MESH)

        mine = [pltpu.make_async_copy(ins[a], outs[a].at[slot(me)], local_sems.at[a]) for a in range(n)]
        for cp in mine:
            cp.start()
        first = []
        for a in range(n):
            first.append(copy(a, 0, me, sibling, src=ins[a]))
            first += [copy(a, 1 + j, me, (*chip, c), src=ins[a]) for j, chip in enumerate(chips)]
        for cp in first:
            cp.start()
        passed = []
        for j, chip in enumerate(chips):
            for a in range(n):
                copy(a, 1 + j, (*chip, c), me).wait_recv()
                fwd = copy(a, 4 + j, (*chip, c), sibling)
                fwd.start()
                passed.append(fwd)
        for a in range(n):
            copy(a, 0, sibling, me).wait_recv()
            for j, chip in enumerate(chips):
                copy(a, 4 + j, (*chip, 1 - c), me).wait_recv()
        for cp in first + passed:
            cp.wait_send()
        for cp in mine:
            cp.wait()

    return pl.pallas_call(
        body, name="weights_all_gather",
        out_shape=[jax.ShapeDtypeStruct((N_DEV,) + s.shape, s.dtype) for s in shards],
        in_specs=[_any()] * n, out_specs=[_any()] * n,
        scratch_shapes=[pltpu.SemaphoreType.DMA((7 * n,)), pltpu.SemaphoreType.DMA((7 * n,)),
                        pltpu.SemaphoreType.DMA((n,))],
    )(*shards)


def _pair_exchange(grads):
    n = len(grads)

    def body(*refs):
        ins, outs = refs[:n], refs[n:2 * n]
        send_sems, recv_sems = refs[2 * n:]
        x, y, c, _ = _position()
        cps = [pltpu.make_async_remote_copy(
            src_ref=ins[a].at[:, 1 - c], dst_ref=outs[a],
            send_sem=send_sems.at[a], recv_sem=recv_sems.at[a],
            device_id=(x, y, 1 - c), device_id_type=MESH) for a in range(n)]
        for cp in cps:
            cp.start()
        for cp in cps:
            cp.wait()

    return pl.pallas_call(
        body, name="grads_pair_exchange",
        out_shape=[jax.ShapeDtypeStruct((4,) + g.shape[2:], g.dtype) for g in grads],
        in_specs=[_any()] * n, out_specs=[_any()] * n,
        scratch_shapes=[pltpu.SemaphoreType.DMA((n,)), pltpu.SemaphoreType.DMA((n,))],
    )(*grads)


def _chip_exchange(combs):
    n = len(combs)

    def body(*refs):
        ins, outs = refs[:n], refs[n:2 * n]
        send_sems, recv_sems = refs[2 * n:]
        x, y, c, chips = _position()
        cps = []
        for a in range(n):
            for j, (cx, cy) in enumerate(chips):
                cps.append(pltpu.make_async_remote_copy(
                    src_ref=ins[a].at[2 * cx + cy], dst_ref=outs[a].at[j],
                    send_sem=send_sems.at[3 * a + j], recv_sem=recv_sems.at[3 * a + j],
                    device_id=(cx, cy, c), device_id_type=MESH))
        for cp in cps:
            cp.start()
        for cp in cps:
            cp.wait()

    return pl.pallas_call(
        body, name="grads_chip_exchange",
        out_shape=[jax.ShapeDtypeStruct((3,) + g.shape[1:], g.dtype) for g in combs],
        in_specs=[_any()] * n, out_specs=[_any()] * n,
        scratch_shapes=[pltpu.SemaphoreType.DMA((3 * n,)), pltpu.SemaphoreType.DMA((3 * n,))],
    )(*combs)


def _pair_add(grad, recv, core):
    blk = grad.shape[2:]
    zeros = (0,) * len(blk)

    def body(core_ref, g_ref, r_ref, o_ref):
        del core_ref
        o_ref[...] = (g_ref[...].astype(F32) + r_ref[...].astype(F32)).astype(BF16)

    return pl.pallas_call(
        body, name="grads_pair_add",
        out_shape=jax.ShapeDtypeStruct((4,) + blk, BF16),
        grid_spec=pltpu.PrefetchScalarGridSpec(
            num_scalar_prefetch=1, grid=(4,),
            in_specs=[pl.BlockSpec((None, None) + blk, lambda i, cr: (i, cr[0]) + zeros),
                      pl.BlockSpec((None,) + blk, lambda i, cr: (i,) + zeros)],
            out_specs=pl.BlockSpec((None,) + blk, lambda i, cr: (i,) + zeros)),
        compiler_params=_params(1),
    )(core, grad, recv)


def _adamw(w, g, m, v):
    m2 = ADAM_B1 * m + (1.0 - ADAM_B1) * g
    v2 = ADAM_B2 * v + (1.0 - ADAM_B2) * (g * g)
    m_hat = m2 / (1.0 - ADAM_B1 ** ADAM_STEP)
    v_hat = v2 / (1.0 - ADAM_B2 ** ADAM_STEP)
    delta = -ADAM_LR * (m_hat / (jnp.sqrt(v_hat) + ADAM_EPS) + ADAM_WD * w)
    return delta, m2, v2


def _final_adamw(comb, recv, chip, parts):
    blk = comb.shape[1:]
    n_parts = len(parts)
    per = blk[0] // n_parts if n_parts > 1 else None
    n_chunks = 4 if (n_parts == 1 and blk[-2] % 64 == 0 and blk[-2] >= 512) else (2 if (n_parts == 1 and blk[-2] % 32 == 0) else 1)
    cblk = blk[:-2] + (blk[-2] // n_chunks, blk[-1])
    lead = (0,) * (len(blk) - 2)

    def body(chip_ref, c_ref, r_ref, *refs):
        del chip_ref
        ins, outs = refs[:3 * n_parts], refs[3 * n_parts:]
        g = c_ref[...].astype(F32) + r_ref[0].astype(F32) + r_ref[1].astype(F32) + r_ref[2].astype(F32)
        for p in range(n_parts):
            w_ref, m_ref, v_ref = ins[3 * p:3 * p + 3]
            if n_parts == 1:
                gp = g
            elif per == 1:
                gp = g[p]
            else:
                gp = g[p * per:(p + 1) * per]
            delta, m2, v2 = _adamw(w_ref[0], gp, m_ref[0], v_ref[0])
            o = outs[4 * p:4 * p + 4]
            o[0][0] = gp
            o[1][0] = delta
            o[2][0] = m2
            o[3][0] = v2

    flat = [a for wmv in parts for a in wmv]

    def part_spec(a):
        shape = a.shape[:-2] + (a.shape[-2] // n_chunks, a.shape[-1])
        return pl.BlockSpec(shape, lambda i, cr, nd=a.ndim: (0,) * (nd - 2) + (i, 0))

    outs = pl.pallas_call(
        body, name="grads_sum_adamw",
        out_shape=[jax.ShapeDtypeStruct(wmv[0].shape, F32) for wmv in parts for _ in range(4)],
        grid_spec=pltpu.PrefetchScalarGridSpec(
            num_scalar_prefetch=1, grid=(n_chunks,),
            in_specs=[pl.BlockSpec((None,) + cblk, lambda i, cr: (cr[0],) + lead + (i, 0)),
                      pl.BlockSpec((3,) + cblk, lambda i, cr: (0,) + lead + (i, 0))]
                     + [part_spec(a) for a in flat],
            out_specs=[part_spec(wmv[0]) for wmv in parts for _ in range(4)]),
        compiler_params=_params(1),
    )(chip, comb, recv, *flat)
    return [tuple(outs[4 * p:4 * p + 4]) for p in range(n_parts)]


def _small_allreduce_adamw(partial, layout, me_index):
    rows, d = partial.shape
    n = len(layout)
    cw = d // N_DEV

    def body(me_ref, p_ref, *refs):
        ins = refs[:3 * n]
        loss_ref = refs[3 * n]
        outs = refs[3 * n + 1:3 * n + 1 + 4 * n]
        buf, send_sems, recv_sems = refs[3 * n + 1 + 4 * n:]
        x, y, c, _ = _position()
        me = me_ref[0]
        buf[me] = p_ref[...]
        cps = []
        for k in range(1, N_DEV):
            peer = (x ^ (k >> 2), y ^ ((k >> 1) & 1), c ^ (k & 1))
            cps.append(pltpu.make_async_remote_copy(
                src_ref=p_ref, dst_ref=buf.at[me],
                send_sem=send_sems.at[k - 1], recv_sem=recv_sems.at[k - 1],
                device_id=peer, device_id_type=MESH))
        for cp in cps:
            cp.start()
        for cp in cps:
            cp.wait()
        total = buf[0]
        for j in range(1, N_DEV):
            total = total + buf[j]
        loss_ref[...] = total[rows - 1:rows, :]
        for e, (kind, r0, nr, _, _, _) in enumerate(layout):
            w_ref, m_ref, v_ref = ins[3 * e:3 * e + 3]
            o = outs[4 * e:4 * e + 4]
            if kind == "rep":
                g = total[r0:r0 + nr, :]
                delta, m2, v2 = _adamw(w_ref[...], g, m_ref[...], v_ref[...])
                for ref, val in zip(o, (g, delta, m2, v2)):
                    ref[...] = val
            elif kind == "wide":
                for q in range(nr):
                    sl = slice(q * d, (q + 1) * d)
                    g = total[r0 + q:r0 + q + 1, :]
                    delta, m2, v2 = _adamw(w_ref[:, sl], g, m_ref[:, sl], v_ref[:, sl])
                    for ref, val in zip(o, (g, delta, m2, v2)):
                        ref[:, sl] = val
            else:
                for j in range(N_DEV):
                    @pl.when(me == j)
                    def _(j=j, o=o, w_ref=w_ref, m_ref=m_ref, v_ref=v_ref, r0=r0, nr=nr):
                        g = total[r0:r0 + nr, j * cw:(j + 1) * cw]
                        delta, m2, v2 = _adamw(w_ref[...], g, m_ref[...], v_ref[...])
                        for ref, val in zip(o, (g, delta, m2, v2)):
                            ref[...] = val

    flat = [a for ent in layout for a in ent[3:]]
    vm = pl.BlockSpec(memory_space=pltpu.VMEM)
    outs = pl.pallas_call(
        body, name="small_allreduce_adamw",
        out_shape=[jax.ShapeDtypeStruct((1, d), F32)]
                  + [jax.ShapeDtypeStruct(ent[3].shape, F32) for ent in layout for _ in range(4)],
        in_specs=[pl.BlockSpec(memory_space=pltpu.SMEM), vm] + [vm] * len(flat),
        out_specs=[vm] * (1 + 4 * n),
        scratch_shapes=[pltpu.VMEM((N_DEV, rows, d), F32),
                        pltpu.SemaphoreType.DMA((N_DEV - 1,)), pltpu.SemaphoreType.DMA((N_DEV - 1,))],
        compiler_params=pltpu.CompilerParams(vmem_limit_bytes=V7X_VMEM_LIMIT),
    )(me_index, partial, *flat)
    return outs[0], [tuple(outs[1 + 4 * e:5 + 4 * e]) for e in range(n)]


def _wd_copies(pa_hbm, row0, wr, wd_v):
    return [(pa_hbm.at[k, pl.ds(row0, wr)], wd_v.at[pl.ds(k * wr, wr)]) for k in range(N_DEV)]


def _ffn_fwd(h, g, wgu_all, which, pa, wd_row0, wr, tm, loss=None):
    tp, d = h.shape
    fb = wgu_all.shape[-1]
    nt = tp // tm
    with_loss = loss is not None
    if with_loss:
        tgt, gf, n_meta, t_real = loss

    def body(*refs):
        if with_loss:
            (h_ref, g_ref, wgu_hbm, pa_hbm, tgt_ref, gf_ref, out_ref, gu_ref, n_ref, loss_ref, dgf_ref,
             wgu_v, wd_v, sems) = refs
        else:
            h_ref, g_ref, wgu_hbm, pa_hbm, out_ref, gu_ref, n_ref, wgu_v, wd_v, sems = refs
        i = pl.program_id(0)

        @pl.when(i == 0)
        def _():
            _load_weights([(wgu_hbm.at[:, which], wgu_v)] + _wd_copies(pa_hbm, wd_row0, wr, wd_v), sems)
            if with_loss:
                loss_ref[...] = jnp.zeros_like(loss_ref)
                dgf_ref[...] = jnp.zeros_like(dgf_ref)

        x = h_ref[...]
        n, _ = _rms_fwd(x, g_ref[...])
        nb = n.astype(BF16)
        n_ref[...] = nb
        acc = jnp.zeros((tm, d), F32)
        for j in range(4):
            gate = _nn(nb, wgu_v[j])
            up = _nn(nb, wgu_v[4 + j])
            gu_ref[0, j] = gate.astype(BF16)
            gu_ref[1, j] = up.astype(BF16)
            act = (gate * _sigmoid(gate) * up).astype(BF16)
            acc = acc + _nn(act, wd_v[pl.ds(2 * j * wr, 2 * wr), :])
        hn = x + FFN_RES * acc
        if not with_loss:
            out_ref[...] = hn
        else:
            gfv = gf_ref[...]
            r = lax.rsqrt(jnp.mean(hn * hn, axis=-1, keepdims=True) + EPS)
            xr = hn * r
            rows = i * tm + lax.broadcasted_iota(jnp.int32, (tm, 1), 0)
            mask = jnp.logical_and(rows >= n_meta, rows < t_real)
            diff = jnp.where(mask, xr * gfv - tgt_ref[...], 0.0)
            loss_ref[...] += jnp.zeros_like(loss_ref) + 0.5 * jnp.sum(diff * diff) / d
            dy = diff / d
            gy = dy * gfv
            out_ref[...] = r * (gy - xr * jnp.mean(gy * xr, axis=-1, keepdims=True))
            dgf_ref[...] += _rowsum(dy * xr)

    row = pl.BlockSpec((tm, d), lambda i: (i, 0))
    vec = pl.BlockSpec((1, d), lambda i: (0, 0))
    in_specs = [row, vec, _any(), _any()]
    out_shape = [jax.ShapeDtypeStruct((tp, d), F32), jax.ShapeDtypeStruct((2, 4, tp, fb), BF16),
                 jax.ShapeDtypeStruct((tp, d), BF16)]
    out_specs = [row, pl.BlockSpec((2, 4, tm, fb), lambda i: (0, 0, i, 0)), row]
    args = [h, g, wgu_all, pa]
    if with_loss:
        in_specs += [row, vec]
        out_shape += [jax.ShapeDtypeStruct((1, d), F32), jax.ShapeDtypeStruct((1, d), F32)]
        out_specs += [vec, vec]
        args += [tgt, gf]
    return pl.pallas_call(
        body, name="ffn_fwd_loss" if with_loss else "ffn_fwd", grid=(nt,),
        in_specs=in_specs, out_specs=out_specs, out_shape=out_shape,
        scratch_shapes=[pltpu.VMEM((N_DEV, d, fb), BF16), pltpu.VMEM((N_DEV * wr, d), BF16),
                        pltpu.SemaphoreType.DMA((1 + N_DEV,))],
        compiler_params=_params(1),
    )(*args)


def _ffn_bwd(dh, h, gu, g, wgu_all, which, pa, wd_row0, wr, tm):
    tp, d = h.shape
    fb = wgu_all.shape[-1]
    nt = tp // tm

    def body(dh_ref, h_ref, gate_ref, up_ref, g_ref, wgu_hbm, pa_hbm,
             dhin_ref, dgu_ref, act_ref, df_ref, dg_ref, wgu_v, wd_v, dn_v, sems):
        i, j = pl.program_id(0), pl.program_id(1)

        @pl.when(jnp.logical_and(i == 0, j == 0))
        def _():
            _load_weights([(wgu_hbm.at[:, which], wgu_v)] + _wd_copies(pa_hbm, wd_row0, wr, wd_v), sems)
            dg_ref[...] = jnp.zeros_like(dg_ref)

        dfb = (FFN_RES * dh_ref[...]).astype(BF16)

        @pl.when(j == 0)
        def _():
            df_ref[...] = dfb
            dn_v[...] = jnp.zeros_like(dn_v)

        wd_j = wd_v[pl.ds(pl.multiple_of(j * (2 * wr), 16), 2 * wr), :]
        dact = _nt(dfb, wd_j)
        gate = gate_ref[...].astype(F32)
        up = up_ref[...].astype(F32)
        sg = _sigmoid(gate)
        silu = gate * sg
        act_ref[...] = (silu * up).astype(BF16)
        dgate = (dact * up * (sg * (1.0 + gate * (1.0 - sg)))).astype(BF16)
        dup = (dact * silu).astype(BF16)
        dgu_ref[0] = dgate
        dgu_ref[1] = dup
        dn_v[...] += _nt(dgate, wgu_v[j]) + _nt(dup, wgu_v[4 + j])

        @pl.when(j == 3)
        def _():
            x = h_ref[...]
            gv = g_ref[...]
            r = lax.rsqrt(jnp.mean(x * x, axis=-1, keepdims=True) + EPS)
            dx, dgp = _rms_bwd(dn_v[...], x, r, gv)
            dhin_ref[...] = dh_ref[...] + dx
            dg_ref[...] += dgp

    row = pl.BlockSpec((tm, d), lambda i, j: (i, 0))
    vec = pl.BlockSpec((1, d), lambda i, j: (0, 0))
    return pl.pallas_call(
        body, name="ffn_bwd", grid=(nt, 4),
        in_specs=[row, row,
                  pl.BlockSpec((None, None, tm, fb), lambda i, j: (0, j, i, 0)),
                  pl.BlockSpec((None, None, tm, fb), lambda i, j: (1, j, i, 0)),
                  vec, _any(), _any()],
        out_specs=[row, pl.BlockSpec((2, None, tm, fb), lambda i, j: (0, j, i, 0)),
                   pl.BlockSpec((None, tm, fb), lambda i, j: (j, i, 0)), row, vec],
        out_shape=[jax.ShapeDtypeStruct((tp, d), F32), jax.ShapeDtypeStruct((2, 4, tp, fb), BF16),
                   jax.ShapeDtypeStruct((4, tp, fb), BF16), jax.ShapeDtypeStruct((tp, d), BF16),
                   jax.ShapeDtypeStruct((1, d), F32)],
        scratch_shapes=[pltpu.VMEM((N_DEV, d, fb), BF16), pltpu.VMEM((N_DEV * wr, d), BF16),
                        pltpu.VMEM((tm, d), F32), pltpu.SemaphoreType.DMA((1 + N_DEV,))],
        compiler_params=_params(2),
    )(dh, h, gu, gu, g, wgu_all, pa)


def _mixer_in_fwd(h, g, win_all, b_in, tm):
    tp, d = h.shape
    nb_cols = win_all.shape[-1]
    n_in = N_DEV * nb_cols

    def body(h_ref, g_ref, b_ref, win_hbm, proj_ref, n_ref, win_v, sems):
        @pl.when(pl.program_id(0) == 0)
        def _():
            _load_weights([(win_hbm, win_v)], sems)

        n, _ = _rms_fwd(h_ref[...], g_ref[...])
        nb = n.astype(BF16)
        n_ref[...] = nb
        for j in range(N_DEV):
            sl = slice(j * nb_cols, (j + 1) * nb_cols)
            proj_ref[:, sl] = (_nn(nb, win_v[j]) + b_ref[:, sl]).astype(BF16)

    row = pl.BlockSpec((tm, d), lambda i: (i, 0))
    return pl.pallas_call(
        body, name="mixer_in_fwd", grid=(tp // tm,),
        in_specs=[row, pl.BlockSpec((1, d), lambda i: (0, 0)), pl.BlockSpec((1, n_in), lambda i: (0, 0)), _any()],
        out_specs=[pl.BlockSpec((tm, n_in), lambda i: (i, 0)), row],
        out_shape=[jax.ShapeDtypeStruct((tp, n_in), BF16), jax.ShapeDtypeStruct((tp, d), BF16)],
        scratch_shapes=[pltpu.VMEM(win_all.shape, BF16), pltpu.SemaphoreType.DMA((1,))],
        compiler_params=_params(1),
    )(h, g, b_in, win_all)


def _mixer_in_bwd(dh, h, dproj, g, win_all, tm):
    tp, d = h.shape
    nb_cols = win_all.shape[-1]
    n_in = N_DEV * nb_cols

    def body(dh_ref, h_ref, dp_ref, g_ref, win_hbm, dhin_ref, dg_ref, win_v, sems):
        @pl.when(pl.program_id(0) == 0)
        def _():
            _load_weights([(win_hbm, win_v)], sems)
            dg_ref[...] = jnp.zeros_like(dg_ref)

        dn = jnp.zeros((tm, d), F32)
        for j in range(N_DEV):
            dn = dn + _nt(dp_ref[:, j * nb_cols:(j + 1) * nb_cols], win_v[j])
        x = h_ref[...]
        r = lax.rsqrt(jnp.mean(x * x, axis=-1, keepdims=True) + EPS)
        dx, dgp = _rms_bwd(dn, x, r, g_ref[...])
        dhin_ref[...] = dh_ref[...] + dx
        dg_ref[...] += dgp

    row = pl.BlockSpec((tm, d), lambda i: (i, 0))
    vec = pl.BlockSpec((1, d), lambda i: (0, 0))
    return pl.pallas_call(
        body, name="mixer_in_bwd", grid=(tp // tm,),
        in_specs=[row, row, pl.BlockSpec((tm, n_in), lambda i: (i, 0)), vec, _any()],
        out_specs=[row, vec],
        out_shape=[jax.ShapeDtypeStruct((tp, d), F32), jax.ShapeDtypeStruct((1, d), F32)],
        scratch_shapes=[pltpu.VMEM(win_all.shape, BF16), pltpu.SemaphoreType.DMA((1,))],
        compiler_params=_params(1),
    )(dh, h, dproj, g, win_all)


def _w3_copies(pa_hbm, row0s, rows, w3_v):
    return [(pa_hbm.at[k, pl.ds(row0s[q], rows)], w3_v.at[q, pl.ds(k * rows, rows)])
            for q in range(3) for k in range(N_DEV)]


def _gates(xrb, wg_ref, ba, bx, lam, hd):
    pre_r, pre_i = [], []
    for hh in range(N_HEADS):
        xh = xrb[:, hh * hd:(hh + 1) * hd]
        pre_r.append(_nn(xh, wg_ref[0, hh]))
        pre_i.append(_nn(xh, wg_ref[1, hh]))
    r = _sigmoid(jnp.concatenate(pre_r, axis=1) + ba)
    ig = _sigmoid(jnp.concatenate(pre_i, axis=1) + bx)
    sp = _softplus(-lam)
    log_a = -RG_LRU_C * r * sp
    a = jnp.exp(log_a)
    s = jnp.sqrt(_one_minus_exp(2.0 * log_a))
    return r, ig, sp, a, s


def _scan_fwd(a, u, h_prev):
    tm = a.shape[0]
    rows = lax.broadcasted_iota(jnp.int32, a.shape, 0)
    d = 1
    while d < tm:
        keep = rows >= d
        u = jnp.where(keep, a * pltpu.roll(u, d, 0) + u, u)
        a = jnp.where(keep, a * pltpu.roll(a, d, 0), a)
        d *= 2
    return u + a * h_prev


def _scan_bwd(b, v, g_next):
    tm = b.shape[0]
    rows = lax.broadcasted_iota(jnp.int32, b.shape, 0)
    d = 1
    while d < tm:
        keep = rows < tm - d
        v = jnp.where(keep, v + b * pltpu.roll(v, tm - d, 0), v)
        b = jnp.where(keep, b * pltpu.roll(b, tm - d, 0), b)
        d *= 2
    return v + b * g_next


def _shifted_copies(ext_ref, es_ref, n_rows):
    for s in range(1, SUBLANES):
        es_ref[s, pl.ds(0, n_rows), :] = ext_ref[pl.ds(s, n_rows), :]


def _tap(ext_ref, es_ref, off, tm):
    q, s = divmod(off, SUBLANES)
    if s == 0:
        return ext_ref[pl.ds(SUBLANES * q, tm), :]
    return es_ref[s, pl.ds(SUBLANES * q, tm), :]


def _mixer_core_fwd(proj, h, cw4, cb4, wg, ba, bx, lam, cw31, cb31, lng, lnb, bcp, pa, w3_row0s, tm):
    tp, d = h.shape
    n_in = proj.shape[1]
    hd = wg.shape[-1]
    k4, k31 = cw4.shape[0], cw31.shape[0]
    w3_rows = d // N_DEV

    def body(p_ref, h_ref, cw4_ref, cb4_ref, wg_ref, ba_ref, bx_ref, lam_ref, cw31_ref, cb31_ref,
             lng_ref, lnb_ref, bcp_ref, pa_hbm,
             h2_ref, xr_ref, hs_ref, v1_ref, ya_ref, yb_ref,
             w3_v, ext4, ext31, es31, hcar, sems):
        @pl.when(pl.program_id(0) == 0)
        def _():
            _load_weights(_w3_copies(pa_hbm, w3_row0s, w3_rows, w3_v), sems)
            ext4[pl.ds(0, CONV4_HALO), :] = jnp.zeros((CONV4_HALO, d), F32)
            ext31[pl.ds(0, CONV31_HALO), :] = jnp.zeros((CONV31_HALO, d), F32)
            hcar[...] = jnp.zeros_like(hcar)

        x_rnn = p_ref[:, 0:d].astype(F32)
        y_rnn = p_ref[:, d:2 * d].astype(F32)
        glu_v = p_ref[:, 2 * d:3 * d].astype(F32)
        glu_g = p_ref[:, 3 * d:4 * d].astype(F32)
        gate_a = p_ref[:, 4 * d:5 * d].astype(F32)
        gate_b = p_ref[:, 5 * d:6 * d].astype(F32)

        ext4[pl.ds(CONV4_HALO, tm), :] = x_rnn
        xr = cb4_ref[...] + jnp.zeros((tm, d), F32)
        for k in range(k4):
            xr = xr + cw4_ref[k:k + 1, :] * ext4[pl.ds(CONV4_HALO - (k4 - 1) + k, tm), :]
        ext4[pl.ds(0, CONV4_HALO), :] = ext4[pl.ds(tm, CONV4_HALO), :]
        xrb = xr.astype(BF16)
        xr_ref[...] = xrb
        xr = xrb.astype(F32)
        _, ig, _, a, s = _gates(xrb, wg_ref, ba_ref[...], bx_ref[...], lam_ref[...], hd)
        hseq = _scan_fwd(a, s * (ig * xr), hcar[0:1, :])
        hcar[0:1, :] = hseq[tm - 1:tm, :]
        hs_ref[...] = hseq.astype(BF16)
        gl, _ = _gelu(y_rnn)
        ya = _nn((hseq * gl).astype(BF16), w3_v[0])
        ya_ref[...] = ya.astype(BF16)

        ext31[pl.ds(CONV31_HALO, tm), :] = glu_v * _sigmoid(glu_g)
        _shifted_copies(ext31, es31, tm + CONV31_HALO - SUBLANES)
        v1 = cb31_ref[...] + jnp.zeros((tm, d), F32)
        for k in range(k31):
            v1 = v1 + cw31_ref[k:k + 1, :] * _tap(ext31, es31, CONV31_HALO - (k31 - 1) + k, tm)
        ext31[pl.ds(0, CONV31_HALO), :] = ext31[pl.ds(tm, CONV31_HALO), :]
        v1b = v1.astype(BF16)
        v1_ref[...] = v1b
        v1 = v1b.astype(F32)
        xc = v1 - jnp.mean(v1, axis=-1, keepdims=True)
        rstd = lax.rsqrt(jnp.mean(xc * xc, axis=-1, keepdims=True) + EPS)
        v2 = xc * rstd * lng_ref[...] + lnb_ref[...]
        yb = _nn((v2 * _sigmoid(v2)).astype(BF16), w3_v[1]) + bcp_ref[...]
        yb_ref[...] = yb.astype(BF16)

        merged = _sigmoid(gate_a) * ya + _sigmoid(gate_b) * yb
        h2_ref[...] = h_ref[...] + _nn(merged.astype(BF16), w3_v[2])

    row = pl.BlockSpec((tm, d), lambda i: (i, 0))
    full = lambda a: pl.BlockSpec(a.shape, lambda i, nd=a.ndim: (0,) * nd)
    smalls = [cw4, cb4, wg, ba, bx, lam, cw31, cb31, lng, lnb, bcp]
    return pl.pallas_call(
        body, name="mixer_core_fwd", grid=(tp // tm,),
        in_specs=[pl.BlockSpec((tm, n_in), lambda i: (i, 0)), row] + [full(a) for a in smalls] + [_any()],
        out_specs=[row] * 6,
        out_shape=[jax.ShapeDtypeStruct((tp, d), F32)] + [jax.ShapeDtypeStruct((tp, d), BF16)] * 5,
        scratch_shapes=[pltpu.VMEM((3, d, d), BF16),
                        pltpu.VMEM((tm + CONV4_HALO, d), F32),
                        pltpu.VMEM((tm + CONV31_HALO, d), F32),
                        pltpu.VMEM((SUBLANES, tm + CONV31_HALO, d), F32),
                        pltpu.VMEM((SUBLANES, d), F32),
                        pltpu.SemaphoreType.DMA((3 * N_DEV,))],
        compiler_params=_params(1),
    )(proj, h, *smalls, pa)


SG_BIN, SG_CW4, SG_CB4, SG_BA, SG_BX, SG_LAM, SG_CB31, SG_LNG, SG_LNB, SG_BCP, SG_CW31 = 0, 6, 10, 11, 12, 13, 14, 15, 16, 17, 18


def _mixer_core_bwd(dh2, proj, xr_s, hs_s, v1_s, ya_s, yb_s, cw4, wg, ba, bx, lam, cw31, lng, lnb, pa, w3_row0s, tm):
    tp, d = dh2.shape
    n_in = proj.shape[1]
    hd = wg.shape[-1]
    k4, k31 = cw4.shape[0], cw31.shape[0]
    nt = tp // tm
    w3_rows = d // N_DEV
    sg_rows = -(-(SG_CW31 + k31) // SUBLANES) * SUBLANES
    halo_rows = 16
    per = tm // halo_rows

    def body(dh_ref, p_ref, xr_ref, hs_ref, hh_ref, v1_ref, ya_ref, yb_ref,
             cw4_ref, wg_ref, ba_ref, bx_ref, lam_ref, cw31_ref, lng_ref, lnb_ref, pa_hbm,
             dp_ref, x3_ref, y3_ref, yg_ref, sg_ref,
             w3_v, extd4, extd31, es31, gcar, sems):
        i = pl.program_id(0)
        tile = nt - 1 - i

        @pl.when(i == 0)
        def _():
            _load_weights(_w3_copies(pa_hbm, w3_row0s, w3_rows, w3_v), sems)
            extd4[pl.ds(tm, CONV4_HALO), :] = jnp.zeros((CONV4_HALO, d), F32)
            extd31[pl.ds(tm, CONV31_HALO), :] = jnp.zeros((CONV31_HALO, d), F32)
            gcar[...] = jnp.zeros_like(gcar)
            sg_ref[...] = jnp.zeros_like(sg_ref)

        def acc(row, val):
            sg_ref[row:row + 1, :] += _rowsum(val)

        rows = lax.broadcasted_iota(jnp.int32, (tm, d), 0)
        x_rnn = p_ref[:, 0:d].astype(F32)
        y_rnn = p_ref[:, d:2 * d].astype(F32)
        glu_v = p_ref[:, 2 * d:3 * d].astype(F32)
        glu_g = p_ref[:, 3 * d:4 * d].astype(F32)
        sga = _sigmoid(p_ref[:, 4 * d:5 * d].astype(F32))
        sgb = _sigmoid(p_ref[:, 5 * d:6 * d].astype(F32))
        ya = ya_ref[...].astype(F32)
        yb = yb_ref[...].astype(F32)

        dmob = dh_ref[...].astype(BF16)
        dmerged = _nt(dmob, w3_v[2])
        x3_ref[:, 0:d] = (sga * ya + sgb * yb).astype(BF16)
        y3_ref[:, 0:d] = dmob
        dya = sga * dmerged
        dyb = sgb * dmerged
        pieces = [None, None, None, None, dmerged * ya * sga * (1.0 - sga), dmerged * yb * sgb * (1.0 - sgb)]

        dyab = dya.astype(BF16)
        y3_ref[:, d:2 * d] = dyab
        dza = _nt(dyab, w3_v[0])
        hsv = hs_ref[...].astype(F32)
        gl, th = _gelu(y_rnn)
        x3_ref[:, d:2 * d] = (hsv * gl).astype(BF16)
        pieces[1] = dza * hsv * _gelu_grad(y_rnn, th)
        dhs = dza * gl
        xrb = xr_ref[...]
        xr = xrb.astype(F32)
        lam_v = lam_ref[...]
        r, ig, sp, a, s = _gates(xrb, wg_ref, ba_ref[...], bx_ref[...], lam_v, hd)
        b = jnp.where(rows == tm - 1, gcar[1:2, :], pltpu.roll(a, tm - 1, 0))
        big_g = _scan_bwd(b, dhs, gcar[0:1, :])
        gcar[0:1, :] = big_g[0:1, :]
        gcar[1:2, :] = a[0:1, :]
        h_before = jnp.where(tile > 0, hh_ref[halo_rows - 1:halo_rows, :].astype(F32), 0.0)
        h_prev = jnp.where(rows == 0, h_before, pltpu.roll(hsv, 1, 0))
        ds = big_g * ig * xr
        dla = big_g * h_prev * a - ds * (a * a) / jnp.maximum(s, 1e-20)
        acc(SG_LAM, dla * r * (RG_LRU_C * _sigmoid(-lam_v)))
        dpr = dla * (-RG_LRU_C * sp) * r * (1.0 - r)
        dpi = big_g * s * xr * ig * (1.0 - ig)
        acc(SG_BA, dpr)
        acc(SG_BX, dpi)
        dprb = dpr.astype(BF16)
        dpib = dpi.astype(BF16)
        yg_ref[:, 0:d] = dprb
        yg_ref[:, d:2 * d] = dpib
        back = []
        for hh in range(N_HEADS):
            sl = slice(hh * hd, (hh + 1) * hd)
            back.append(_nt(dprb[:, sl], wg_ref[0, hh]) + _nt(dpib[:, sl], wg_ref[1, hh]))
        dxr = big_g * s * ig + jnp.concatenate(back, axis=1)
        acc(SG_CB4, dxr)
        extd4[pl.ds(0, tm), :] = dxr
        dx_rnn = jnp.zeros((tm, d), F32)
        for k in range(k4):
            term = extd4[pl.ds(k4 - 1 - k, tm), :]
            dx_rnn = dx_rnn + cw4_ref[k:k + 1, :] * term
            acc(SG_CW4 + k, x_rnn * term)
        extd4[pl.ds(tm, CONV4_HALO), :] = extd4[pl.ds(0, CONV4_HALO), :]
        pieces[0] = dx_rnn

        dybb = dyb.astype(BF16)
        y3_ref[:, 2 * d:3 * d] = dybb
        acc(SG_BCP, dyb)
        dv3 = _nt(dybb, w3_v[1])
        v1 = v1_ref[...].astype(F32)
        xc = v1 - jnp.mean(v1, axis=-1, keepdims=True)
        rstd = lax.rsqrt(jnp.mean(xc * xc, axis=-1, keepdims=True) + EPS)
        xhat = xc * rstd
        lng_v = lng_ref[...]
        v2 = xhat * lng_v + lnb_ref[...]
        s2 = _sigmoid(v2)
        x3_ref[:, 2 * d:3 * d] = (v2 * s2).astype(BF16)
        dv2 = dv3 * (s2 * (1.0 + v2 * (1.0 - s2)))
        acc(SG_LNG, dv2 * xhat)
        acc(SG_LNB, dv2)
        dxh = dv2 * lng_v
        dv1 = rstd * (dxh - jnp.mean(dxh, axis=-1, keepdims=True)
                      - xhat * jnp.mean(dxh * xhat, axis=-1, keepdims=True))
        acc(SG_CB31, dv1)
        extd31[pl.ds(0, tm), :] = dv1
        _shifted_copies(extd31, es31, tm + CONV31_HALO - SUBLANES)
        sgg = _sigmoid(glu_g)
        v0 = glu_v * sgg
        dv0 = jnp.zeros((tm, d), F32)
        for k in range(k31):
            term = _tap(extd31, es31, k31 - 1 - k, tm)
            dv0 = dv0 + cw31_ref[k:k + 1, :] * term
            acc(SG_CW31 + k, v0 * term)
        extd31[pl.ds(tm, CONV31_HALO), :] = extd31[pl.ds(0, CONV31_HALO), :]
        pieces[2] = dv0 * sgg
        pieces[3] = dv0 * glu_v * sgg * (1.0 - sgg)

        for q, piece in enumerate(pieces):
            dp_ref[:, q * d:(q + 1) * d] = piece.astype(BF16)
            acc(SG_BIN + q, piece)

    rev = lambda i: (nt - 1 - i, 0)
    row = pl.BlockSpec((tm, d), rev)
    full = lambda a: pl.BlockSpec(a.shape, lambda i, nd=a.ndim: (0,) * nd)
    halo = pl.BlockSpec((halo_rows, d), lambda i: (jnp.maximum((nt - 1 - i) * per - 1, 0), 0))
    smalls = [cw4, wg, ba, bx, lam, cw31, lng, lnb]
    return pl.pallas_call(
        body, name="mixer_core_bwd", grid=(nt,),
        in_specs=[row, pl.BlockSpec((tm, n_in), rev), row, row, halo, row, row, row]
                 + [full(a) for a in smalls] + [_any()],
        out_specs=[pl.BlockSpec((tm, n_in), rev), pl.BlockSpec((tm, 3 * d), rev), pl.BlockSpec((tm, 3 * d), rev),
                   pl.BlockSpec((tm, 2 * d), rev), pl.BlockSpec((sg_rows, d), lambda i: (0, 0))],
        out_shape=[jax.ShapeDtypeStruct((tp, n_in), BF16), jax.ShapeDtypeStruct((tp, 3 * d), BF16),
                   jax.ShapeDtypeStruct((tp, 3 * d), BF16), jax.ShapeDtypeStruct((tp, 2 * d), BF16),
                   jax.ShapeDtypeStruct((sg_rows, d), F32)],
        scratch_shapes=[pltpu.VMEM((3, d, d), BF16),
                        pltpu.VMEM((tm + CONV4_HALO, d), F32),
                        pltpu.VMEM((tm + CONV31_HALO, d), F32),
                        pltpu.VMEM((SUBLANES, tm + CONV31_HALO, d), F32),
                        pltpu.VMEM((SUBLANES, d), F32),
                        pltpu.SemaphoreType.DMA((3 * N_DEV,))],
        compiler_params=_params(1),
    )(dh2, proj, xr_s, hs_s, hs_s, v1_s, ya_s, yb_s, *smalls, pa)


def _tn_matmul(name, x, y, x_spec, y_spec, n_blocks, kb, nb, tm, tp, out_shape, out_spec, out_view):
    nt = tp // tm

    def body(x_ref, y_ref, o_ref, acc):
        i = pl.program_id(1)

        @pl.when(i == 0)
        def _():
            acc[...] = jnp.zeros_like(acc)

        acc[...] += _tn(x_ref[...], y_ref[...])

        @pl.when(i == nt - 1)
        def _():
            o_ref[...] = acc[...].astype(BF16).reshape(out_view)

    return pl.pallas_call(
        body, name=name, grid=(n_blocks, nt),
        in_specs=[x_spec, y_spec], out_specs=out_spec,
        out_shape=jax.ShapeDtypeStruct(out_shape, BF16),
        scratch_shapes=[pltpu.VMEM((kb, nb), F32)],
        compiler_params=_params(2),
    )(x, y)


def kernel(x, meta_tokens, ffn1_norm, ffn1_w_gu, ffn1_w_down, mix_norm, w_in, b_in, rnn_conv_w, rnn_conv_b, rg_w_a, rg_b_a, rg_w_x, rg_b_x, rg_lambda, rnn_w_proj, conv_dw_w, conv_dw_b, conv_ln_g, conv_ln_b, conv_w_proj, conv_b_proj, w_out, ffn2_norm, ffn2_w_gu, ffn2_w_down, final_norm, loss_target, m_meta_tokens, m_ffn1_norm, m_ffn1_w_gu, m_ffn1_w_down, m_mix_norm, m_w_in, m_b_in, m_rnn_conv_w, m_rnn_conv_b, m_rg_w_a, m_rg_b_a, m_rg_w_x, m_rg_b_x, m_rg_lambda, m_rnn_w_proj, m_conv_dw_w, m_conv_dw_b, m_conv_ln_g, m_conv_ln_b, m_conv_w_proj, m_conv_b_proj, m_w_out, m_ffn2_norm, m_ffn2_w_gu, m_ffn2_w_down, m_final_norm, v_meta_tokens, v_ffn1_norm, v_ffn1_w_gu, v_ffn1_w_down, v_mix_norm, v_w_in, v_b_in, v_rnn_conv_w, v_rnn_conv_b, v_rg_w_a, v_rg_b_a, v_rg_w_x, v_rg_b_x, v_rg_lambda, v_rnn_w_proj, v_conv_dw_w, v_conv_dw_b, v_conv_ln_g, v_conv_ln_b, v_conv_w_proj, v_conv_b_proj, v_w_out, v_ffn2_norm, v_ffn2_w_gu, v_ffn2_w_down, v_final_norm):
    w = dict(locals())
    seq, d = x.shape[1], x.shape[2]
    n_meta = meta_tokens.shape[0]
    t_real = n_meta + seq
    tp, tm, tmx, tmt = _tiles(t_real)
    fb = ffn1_w_gu.shape[-1]
    wr = ffn1_w_down.shape[1]
    nbc = w_in.shape[-1]
    n_in = N_DEV * nbc
    pr = rnn_w_proj.shape[1]
    hd = rg_w_a.shape[-1]
    gr = rg_w_a.shape[2]
    cw = meta_tokens.shape[1]
    k4, k31 = rnn_conv_w.shape[1], conv_dw_w.shape[1]
    assert n_in == 6 * d and 2 * wr == fb and N_HEADS * hd == d and pr * N_DEV == d

    xi, yi, ci = lax.axis_index("x"), lax.axis_index("y"), lax.axis_index("c")
    core = ci.astype(jnp.int32).reshape(1)
    chip = (2 * xi + yi).astype(jnp.int32).reshape(1)
    me_index = (4 * xi + 2 * yi + ci).astype(jnp.int32).reshape(1)

    row0 = {"wd1": 0, "rp": wr, "cp": wr + pr, "out": wr + 2 * pr, "wd2": wr + 3 * pr}
    pa_loc = jnp.concatenate([ffn1_w_down[0], rnn_w_proj[0], conv_w_proj[0], w_out[0], ffn2_w_down[0]],
                             axis=0).astype(BF16)
    wgu_loc = jnp.stack([ffn1_w_gu[0], ffn2_w_gu[0]]).astype(BF16)
    win_loc = w_in[0].astype(BF16)
    wg_loc = jnp.stack([rg_w_a[0], rg_w_x[0]]).astype(BF16)
    n_small = n_meta + k4 + k31
    small_rows = -(-n_small // SUBLANES) * SUBLANES
    small_loc = jnp.concatenate([meta_tokens, rnn_conv_w[0], conv_dw_w[0],
                                 jnp.zeros((small_rows - n_small, cw), F32)], axis=0)
    pa, wgu_all, win_all, wg_all, small_all = _all_gather([pa_loc, wgu_loc, win_loc, wg_loc, small_loc])
    wg = wg_all.transpose(1, 2, 0, 3, 4).reshape(2, N_HEADS, hd, hd)
    small_full = small_all.transpose(1, 0, 2).reshape(small_rows, d)
    meta_full = small_full[:n_meta]
    cw4 = small_full[n_meta:n_meta + k4]
    cw31 = small_full[n_meta + k4:n_meta + k4 + k31]
    w3_row0s = (row0["rp"], row0["cp"], row0["out"])

    pad = jnp.zeros((tp - t_real, d), F32)
    h0 = jnp.concatenate([meta_full, x[0], pad], axis=0)
    tgt = jnp.concatenate([jnp.zeros((n_meta, d), F32), loss_target[0], pad], axis=0)
    h1, gu1, n1 = _ffn_fwd(h0, ffn1_norm, wgu_all, 0, pa, row0["wd1"], wr, tm)
    proj, n2 = _mixer_in_fwd(h1, mix_norm, win_all, b_in, tm)
    h2, xr_s, hs_s, v1_s, ya_s, yb_s = _mixer_core_fwd(
        proj, h1, cw4, rnn_conv_b, wg, rg_b_a, rg_b_x, rg_lambda, cw31, conv_dw_b, conv_ln_g, conv_ln_b,
        conv_b_proj, pa, w3_row0s, tmx)
    dh3, gu2, n3, loss_part, dgf = _ffn_fwd(h2, ffn2_norm, wgu_all, 1, pa, row0["wd2"], wr, tm,
                                            loss=(tgt, final_norm.reshape(1, d), n_meta, t_real))

    def ffn_weight_grads(tag, n_s, dgu, act, df):
        g_wgu = _tn_matmul(
            "d_w_gu" + tag, n_s, dgu.reshape(N_DEV, tp, fb),
            pl.BlockSpec((tmt, d), lambda b, i: (i, 0)), pl.BlockSpec((None, tmt, fb), lambda b, i: (b, i, 0)),
            N_DEV, d, fb, tmt, tp, (N_DEV, d, fb), pl.BlockSpec((None, d, fb), lambda b, i: (b, 0, 0)), (d, fb))
        g_wd = _tn_matmul(
            "d_w_down" + tag, act, df,
            pl.BlockSpec((None, tmt, fb), lambda b, i: (b, i, 0)), pl.BlockSpec((tmt, d), lambda b, i: (i, 0)),
            4, fb, d, tmt, tp, (4, fb, d), pl.BlockSpec((None, fb, d), lambda b, i: (b, 0, 0)), (fb, d))
        return g_wgu, g_wd.reshape(N_DEV, wr, d)

    dh2, dgu2, act2, df2, dg_ffn2 = _ffn_bwd(dh3, h2, gu2, ffn2_norm, wgu_all, 1, pa, row0["wd2"], wr, tm)
    g_wgu2, g_wd2 = ffn_weight_grads("2", n3, dgu2, act2, df2)
    dproj, x3, y3, yg, sg = _mixer_core_bwd(
        dh2, proj, xr_s, hs_s, v1_s, ya_s, yb_s, cw4, wg, rg_b_a, rg_b_x, rg_lambda, cw31, conv_ln_g, conv_ln_b,
        pa, w3_row0s, tmx)
    g_w3 = _tn_matmul(
        "d_w_proj3", x3, y3,
        pl.BlockSpec((tmt, d), lambda b, i: (i, b)), pl.BlockSpec((tmt, d), lambda b, i: (i, b)),
        3, d, d, tmt, tp, (N_DEV, 3, pr, d), pl.BlockSpec((N_DEV, None, pr, d), lambda b, i: (0, b, 0, 0)),
        (N_DEV, pr, d))
    g_wg = _tn_matmul(
        "d_w_gates", xr_s, yg,
        pl.BlockSpec((tmt, hd), lambda b, i: (i, b % N_HEADS)), pl.BlockSpec((tmt, hd), lambda b, i: (i, b)),
        2 * N_HEADS, hd, hd, tmt, tp, (N_DEV, 2 * N_HEADS, gr, hd),
        pl.BlockSpec((N_DEV, None, gr, hd), lambda b, i: (0, b, 0, 0)), (N_DEV, gr, hd))
    g_win = _tn_matmul(
        "d_w_in", n2, dproj,
        pl.BlockSpec((tmt, d), lambda b, i: (i, 0)), pl.BlockSpec((tmt, nbc), lambda b, i: (i, b)),
        N_DEV, d, nbc, tmt, tp, (N_DEV, d, nbc), pl.BlockSpec((None, d, nbc), lambda b, i: (b, 0, 0)), (d, nbc))
    dh1, dg_mix = _mixer_in_bwd(dh2, h1, dproj, mix_norm, win_all, tm)
    dh0, dgu1, act1, df1, dg_ffn1 = _ffn_bwd(dh1, h0, gu1, ffn1_norm, wgu_all, 0, pa, row0["wd1"], wr, tm)
    g_wgu1, g_wd1 = ffn_weight_grads("1", n1, dgu1, act1, df1)
    grad_x = dh0[n_meta:t_real][None]

    big = [g_wd1, g_wgu1, g_wd2, g_wgu2, g_win, g_w3, g_wg]
    big = [g.reshape((4, 2) + g.shape[1:]) for g in big]
    from_sibling = _pair_exchange(big)
    combs = [_pair_add(g, r, core) for g, r in zip(big, from_sibling)]
    from_chips = _chip_exchange(combs)
    names = [["ffn1_w_down"], ["ffn1_w_gu"], ["ffn2_w_down"], ["ffn2_w_gu"], ["w_in"],
             ["w_out", "rnn_w_proj", "conv_w_proj"], ["rg_w_a", "rg_w_x"]]
    res = {}
    for comb, recv, group in zip(combs, from_chips, names):
        outs = _final_adamw(comb, recv, chip, [(w[nm], w["m_" + nm], w["v_" + nm]) for nm in group])
        for nm, o in zip(group, outs):
            res[nm] = o

    rep_rows = [("ffn1_norm", dg_ffn1), ("mix_norm", dg_mix), ("b_in", sg[SG_BIN:SG_BIN + 6]),
                ("rnn_conv_b", sg[SG_CB4:SG_CB4 + 1]), ("rg_b_a", sg[SG_BA:SG_BA + 1]),
                ("rg_b_x", sg[SG_BX:SG_BX + 1]), ("rg_lambda", sg[SG_LAM:SG_LAM + 1]),
                ("conv_dw_b", sg[SG_CB31:SG_CB31 + 1]), ("conv_ln_g", sg[SG_LNG:SG_LNG + 1]),
                ("conv_ln_b", sg[SG_LNB:SG_LNB + 1]), ("conv_b_proj", sg[SG_BCP:SG_BCP + 1]),
                ("ffn2_norm", dg_ffn2), ("final_norm", dgf)]
    col_rows = [("meta_tokens", dh0[:n_meta]), ("rnn_conv_w", sg[SG_CW4:SG_CW4 + k4]),
                ("conv_dw_w", sg[SG_CW31:SG_CW31 + k31])]
    layout, pieces, r0 = [], [], 0
    for nm, part in rep_rows:
        nr = part.shape[0]
        kind = "wide" if nm == "b_in" else "rep"
        as2d = lambda a: a.reshape(1, -1) if a.ndim == 1 else a
        layout.append((kind, r0, nr, as2d(w[nm]), as2d(w["m_" + nm]), as2d(w["v_" + nm])))
        pieces.append(part)
        r0 += nr
    for nm, part in col_rows:
        nr = part.shape[0]
        sq = lambda a: a.reshape(a.shape[-2], a.shape[-1])
        layout.append(("col", r0, nr, sq(w[nm]), sq(w["m_" + nm]), sq(w["v_" + nm])))
        pieces.append(part)
        r0 += nr
    total_rows = -(-(r0 + 1) // SUBLANES) * SUBLANES
    pieces.append(jnp.zeros((total_rows - 1 - r0, d), F32))
    pieces.append(loss_part)
    loss_row, small_out = _small_allreduce_adamw(jnp.concatenate(pieces, axis=0), layout, me_index)
    for (nm, _), o in zip(rep_rows + col_rows, small_out):
        res[nm] = tuple(a.reshape(w[nm].shape) for a in o)

    order = ["meta_tokens", "ffn1_norm", "ffn1_w_gu", "ffn1_w_down", "mix_norm", "w_in", "b_in", "rnn_conv_w",
             "rnn_conv_b", "rg_w_a", "rg_b_a", "rg_w_x", "rg_b_x", "rg_lambda", "rnn_w_proj", "conv_dw_w",
             "conv_dw_b", "conv_ln_g", "conv_ln_b", "conv_w_proj", "conv_b_proj", "w_out", "ffn2_norm",
             "ffn2_w_gu", "ffn2_w_down", "final_norm"]
    return (loss_row[0, 0], grad_x, *[res[nm][0] for nm in order], *[res[nm][1] for nm in order],
            *[res[nm][2] for nm in order], *[res[nm][3] for nm in order])
```

```python
import functools
import math

import jax
import jax.numpy as jnp
from jax import lax
from jax.experimental import pallas as pl
from jax.experimental.pallas import tpu as pltpu

F32 = jnp.float32
BF16 = jnp.bfloat16
MESH = pl.DeviceIdType.MESH
N_DEV = 8
N_HEADS = 4
RG_LRU_C = 8.0
EPS = 1e-6
FFN_RES = 0.5
ADAM_LR, ADAM_B1, ADAM_B2, ADAM_EPS, ADAM_WD, ADAM_STEP = 0.001, 0.9, 0.999, 1e-08, 0.01, 10
V7X_VMEM_LIMIT = 56 * 1024 * 1024
CONV4_HALO = 8
CONV31_HALO = 32
SUBLANES = 8
FFN_CHUNKS = 2
GELU_C = math.sqrt(2.0 / math.pi)
GELU_K = 0.044715


def _any():
    return pl.BlockSpec(memory_space=pl.ANY)


def _params(n_grid):
    return pltpu.CompilerParams(dimension_semantics=("arbitrary",) * n_grid, vmem_limit_bytes=V7X_VMEM_LIMIT)


def _nn(a, b):
    return jnp.dot(a, b, preferred_element_type=F32)


def _nt(a, b):
    return lax.dot_general(a, b, (((1,), (1,)), ((), ())), preferred_element_type=F32)


def _tn(a, b):
    return lax.dot_general(a, b, (((0,), (0,)), ((), ())), preferred_element_type=F32)


def _sigmoid(x):
    return jax.nn.sigmoid(x)


def _rowsum(x):
    return jnp.sum(x, axis=0, keepdims=True)


def _rms_fwd(x, g):
    r = lax.rsqrt(jnp.mean(x * x, axis=-1, keepdims=True) + EPS)
    return x * r * g, r


def _rms_bwd(dn, x, r, g):
    xr = x * r
    gy = dn * g
    dx = r * (gy - xr * jnp.mean(gy * xr, axis=-1, keepdims=True))
    return dx, _rowsum(dn * xr)


def _gelu(y):
    t = jnp.tanh(GELU_C * (y + GELU_K * y * y * y))
    return 0.5 * y * (1.0 + t), t


def _gelu_grad(y, t):
    return 0.5 * (1.0 + t) + 0.5 * y * (1.0 - t * t) * GELU_C * (1.0 + 3.0 * GELU_K * y * y)


def _softplus(x):
    return jnp.maximum(x, 0.0) + jnp.log(1.0 + jnp.exp(-jnp.abs(x)))


def _one_minus_exp(z):
    series = -z * (1.0 + 0.5 * z * (1.0 + z * (1.0 / 3.0) * (1.0 + 0.25 * z)))
    return jnp.where(z > -0.05, series, 1.0 - jnp.exp(z))


def _tiles(t_real):
    if t_real > 2048:
        tm = 384
        tp = -(-t_real // tm) * tm
        return tp, tm, tm // 2, tp // 4
    tm = 128
    tp = -(-t_real // tm) * tm
    return tp, tm, tm // 2, tm


def _load_weights(copies, sems):
    cps = [pltpu.make_async_copy(s, d, sems.at[k]) for k, (s, d) in enumerate(copies)]
    for cp in cps:
        cp.start()
    for cp in cps:
        cp.wait()


def _position():
    x, y, c = lax.axis_index("x"), lax.axis_index("y"), lax.axis_index("c")
    chips = [(1 - x, y), (x, 1 - y), (1 - x, 1 - y)]
    return x, y, c, chips


def _slot(p):
    return 4 * p[0] + 2 * p[1] + p[2]


class _Gather:
    def __init__(self, shards):
        self.shards = list(shards)
        self.n = len(self.shards)

    def inputs(self):
        return self.shards

    def out_shape(self):
        return [jax.ShapeDtypeStruct((N_DEV,) + s.shape, s.dtype) for s in self.shards]

    def scratch(self):
        return [pltpu.SemaphoreType.DMA((7 * self.n,)), pltpu.SemaphoreType.DMA((7 * self.n,)),
                pltpu.SemaphoreType.DMA((self.n,))]

    def _plan(self, ins, outs, sems):
        send_sems, recv_sems, local_sems = sems
        x, y, c, chips = _position()
        me, sibling = (x, y, c), (x, y, 1 - c)

        def copy(a, k, block, to, src=None):
            dst = outs[a].at[_slot(block)]
            return pltpu.make_async_remote_copy(
                src_ref=dst if src is None else src, dst_ref=dst,
                send_sem=send_sems.at[7 * a + k], recv_sem=recv_sems.at[7 * a + k],
                device_id=to, device_id_type=MESH)

        mine = [pltpu.make_async_copy(ins[a], outs[a].at[_slot(me)], local_sems.at[a]) for a in range(self.n)]
        first = []
        for a in range(self.n):
            first.append(copy(a, 0, me, sibling, src=ins[a]))
            first += [copy(a, 1 + j, me, (*chip, c), src=ins[a]) for j, chip in enumerate(chips)]
        return copy, mine, first, me, sibling, c, chips

    def start(self, ins, outs, sems):
        _, mine, first, *_ = self._plan(ins, outs, sems)
        for cp in mine + first:
            cp.start()

    def finish(self, ins, outs, sems):
        copy, mine, first, me, sibling, c, chips = self._plan(ins, outs, sems)
        passed = []
        for j, chip in enumerate(chips):
            for a in range(self.n):
                copy(a, 1 + j, (*chip, c), me).wait_recv()
                fwd = copy(a, 4 + j, (*chip, c), sibling)
                fwd.start()
                passed.append(fwd)
        for a in range(self.n):
            copy(a, 0, sibling, me).wait_recv()
            for j, chip in enumerate(chips):
                copy(a, 4 + j, (*chip, 1 - c), me).wait_recv()
        for cp in first + passed:
            cp.wait_send()
        for cp in mine:
            cp.wait()


class _Scatter:
    def __init__(self, grads):
        self.grads = list(grads)
        self.n = len(self.grads)

    def inputs(self):
        return self.grads

    def out_shape(self):
        return [jax.ShapeDtypeStruct((N_DEV - 1,) + g.shape[1:], g.dtype) for g in self.grads]

    def scratch(self):
        return [pltpu.SemaphoreType.DMA((7 * self.n,)), pltpu.SemaphoreType.DMA((7 * self.n,))]

    def _plan(self, ins, outs, sems):
        send_sems, recv_sems = sems
        x, y, c, _ = _position()
        cps = []
        for a in range(self.n):
            for k in range(1, N_DEV):
                peer = (x ^ (k >> 2), y ^ ((k >> 1) & 1), c ^ (k & 1))
                cps.append(pltpu.make_async_remote_copy(
                    src_ref=ins[a].at[_slot(peer)], dst_ref=outs[a].at[k - 1],
                    send_sem=send_sems.at[7 * a + k - 1], recv_sem=recv_sems.at[7 * a + k - 1],
                    device_id=peer, device_id_type=MESH))
        return cps

    def start(self, ins, outs, sems):
        for cp in self._plan(ins, outs, sems):
            cp.start()

    def finish(self, ins, outs, sems):
        for cp in self._plan(ins, outs, sems):
            cp.wait()


def _hosted(inner, n_in, n_out, comm, grid):
    if comm is None:
        return inner
    nc_in, nc_out, ns = len(comm.inputs()), len(comm.out_shape()), len(comm.scratch())

    def body(*refs):
        o0 = n_in + nc_in
        s0 = o0 + n_out + nc_out
        main = refs[:n_in] + refs[o0:o0 + n_out] + refs[s0:len(refs) - ns]
        c_in, c_out, c_sems = refs[n_in:o0], refs[o0 + n_out:s0], refs[len(refs) - ns:]
        ids = [pl.program_id(ax) for ax in range(len(grid))]
        first = functools.reduce(jnp.logical_and, [i == 0 for i in ids])
        last = functools.reduce(jnp.logical_and, [i == g - 1 for i, g in zip(ids, grid)])

        @pl.when(first)
        def _():
            comm.start(c_in, c_out, c_sems)

        inner(*main)

        @pl.when(last)
        def _():
            comm.finish(c_in, c_out, c_sems)

    return body


def _call(inner, name, grid, in_specs, out_specs, out_shape, scratch, args, comm=None):
    n_in, n_out = len(args), len(out_shape)
    body = _hosted(inner, n_in, n_out, comm, grid)
    if comm is not None:
        in_specs = list(in_specs) + [_any()] * len(comm.inputs())
        args = list(args) + comm.inputs()
        out_specs = list(out_specs) + [_any()] * len(comm.out_shape())
        out_shape = list(out_shape) + comm.out_shape()
        scratch = list(scratch) + comm.scratch()
    outs = pl.pallas_call(
        body, name=name, grid=grid, in_specs=list(in_specs), out_specs=list(out_specs), out_shape=list(out_shape),
        scratch_shapes=list(scratch), compiler_params=_params(len(grid)))(*args)
    return list(outs[:n_out]), list(outs[n_out:])


def _all_gather(shards):
    comm = _Gather(shards)
    n = comm.n

    def body(*refs):
        comm.start(refs[:n], refs[n:2 * n], refs[2 * n:])
        comm.finish(refs[:n], refs[n:2 * n], refs[2 * n:])

    return pl.pallas_call(
        body, name="weights_all_gather", out_shape=comm.out_shape(),
        in_specs=[_any()] * n, out_specs=[_any()] * n, scratch_shapes=comm.scratch(),
    )(*shards)


def _pair_exchange(grads):
    n = len(grads)

    def body(*refs):
        ins, outs = refs[:n], refs[n:2 * n]
        send_sems, recv_sems = refs[2 * n:]
        x, y, c, _ = _position()
        cps = [pltpu.make_async_remote_copy(
            src_ref=ins[a].at[:, 1 - c], dst_ref=outs[a],
            send_sem=send_sems.at[a], recv_sem=recv_sems.at[a],
            device_id=(x, y, 1 - c), device_id_type=MESH) for a in range(n)]
        for cp in cps:
            cp.start()
        for cp in cps:
            cp.wait()

    return pl.pallas_call(
        body, name="grads_pair_exchange",
        out_shape=[jax.ShapeDtypeStruct((4,) + g.shape[2:], g.dtype) for g in grads],
        in_specs=[_any()] * n, out_specs=[_any()] * n,
        scratch_shapes=[pltpu.SemaphoreType.DMA((n,)), pltpu.SemaphoreType.DMA((n,))],
    )(*grads)


def _chip_exchange(combs):
    n = len(combs)

    def body(*refs):
        ins, outs = refs[:n], refs[n:2 * n]
        send_sems, recv_sems = refs[2 * n:]
        x, y, c, chips = _position()
        cps = []
        for a in range(n):
            for j, (cx, cy) in enumerate(chips):
                cps.append(pltpu.make_async_remote_copy(
                    src_ref=ins[a].at[2 * cx + cy], dst_ref=outs[a].at[j],
                    send_sem=send_sems.at[3 * a + j], recv_sem=recv_sems.at[3 * a + j],
                    device_id=(cx, cy, c), device_id_type=MESH))
        for cp in cps:
            cp.start()
        for cp in cps:
            cp.wait()

    return pl.pallas_call(
        body, name="grads_chip_exchange",
        out_shape=[jax.ShapeDtypeStruct((3,) + g.shape[1:], g.dtype) for g in combs],
        in_specs=[_any()] * n, out_specs=[_any()] * n,
        scratch_shapes=[pltpu.SemaphoreType.DMA((3 * n,)), pltpu.SemaphoreType.DMA((3 * n,))],
    )(*combs)


def _pair_add(grad, recv, core):
    blk = grad.shape[2:]
    zeros = (0,) * len(blk)

    def body(core_ref, g_ref, r_ref, o_ref):
        del core_ref
        o_ref[...] = (g_ref[...].astype(F32) + r_ref[...].astype(F32)).astype(BF16)

    return pl.pallas_call(
        body, name="grads_pair_add",
        out_shape=jax.ShapeDtypeStruct((4,) + blk, BF16),
        grid_spec=pltpu.PrefetchScalarGridSpec(
            num_scalar_prefetch=1, grid=(4,),
            in_specs=[pl.BlockSpec((None, None) + blk, lambda i, cr: (i, cr[0]) + zeros),
                      pl.BlockSpec((None,) + blk, lambda i, cr: (i,) + zeros)],
            out_specs=pl.BlockSpec((None,) + blk, lambda i, cr: (i,) + zeros)),
        compiler_params=_params(1),
    )(core, grad, recv)


def _adamw(w, g, m, v):
    m2 = ADAM_B1 * m + (1.0 - ADAM_B1) * g
    v2 = ADAM_B2 * v + (1.0 - ADAM_B2) * (g * g)
    m_hat = m2 / (1.0 - ADAM_B1 ** ADAM_STEP)
    v_hat = v2 / (1.0 - ADAM_B2 ** ADAM_STEP)
    delta = -ADAM_LR * (m_hat / (jnp.sqrt(v_hat) + ADAM_EPS) + ADAM_WD * w)
    return delta, m2, v2


def _final_adamw(own, recv, idx, parts):
    blk = own.shape[1:]
    n_recv = recv.shape[0]
    n_parts = len(parts)
    per = blk[0] // n_parts if n_parts > 1 else None
    rows = blk[-2]
    n_chunks = 1 if n_parts > 1 else (4 if rows % 64 == 0 and rows >= 512 else (2 if rows % 32 == 0 else 1))
    cblk = blk[:-2] + (rows // n_chunks, blk[-1])
    lead = (0,) * (len(blk) - 2)

    def body(idx_ref, c_ref, r_ref, *refs):
        del idx_ref
        ins, outs = refs[:3 * n_parts], refs[3 * n_parts:]
        g = c_ref[...].astype(F32)
        for k in range(n_recv):
            g = g + r_ref[k].astype(F32)
        for p in range(n_parts):
            w_ref, m_ref, v_ref = ins[3 * p:3 * p + 3]
            if n_parts == 1:
                gp = g
            elif per == 1:
                gp = g[p]
            else:
                gp = g[p * per:(p + 1) * per]
            delta, m2, v2 = _adamw(w_ref[0], gp, m_ref[0], v_ref[0])
            o = outs[4 * p:4 * p + 4]
            o[0][0] = gp
            o[1][0] = delta
            o[2][0] = m2
            o[3][0] = v2

    flat = [a for wmv in parts for a in wmv]

    def part_spec(a):
        shape = a.shape[:-2] + (a.shape[-2] // n_chunks, a.shape[-1])
        return pl.BlockSpec(shape, lambda i, cr, nd=a.ndim: (0,) * (nd - 2) + (i, 0))

    outs = pl.pallas_call(
        body, name="grads_sum_adamw",
        out_shape=[jax.ShapeDtypeStruct(wmv[0].shape, F32) for wmv in parts for _ in range(4)],
        grid_spec=pltpu.PrefetchScalarGridSpec(
            num_scalar_prefetch=1, grid=(n_chunks,),
            in_specs=[pl.BlockSpec((None,) + cblk, lambda i, cr: (cr[0],) + lead + (i, 0)),
                      pl.BlockSpec((n_recv,) + cblk, lambda i, cr: (0,) + lead + (i, 0))]
                     + [part_spec(a) for a in flat],
            out_specs=[part_spec(wmv[0]) for wmv in parts for _ in range(4)]),
        compiler_params=_params(1),
    )(idx, own, recv, *flat)
    return [tuple(outs[4 * p:4 * p + 4]) for p in range(n_parts)]


def _small_allreduce(partial, me_index):
    rows, d = partial.shape

    def body(me_ref, p_ref, out_ref, buf, send_sems, recv_sems):
        x, y, c, _ = _position()
        me = me_ref[0]
        buf[me] = p_ref[...]
        cps = []
        for k in range(1, N_DEV):
            peer = (x ^ (k >> 2), y ^ ((k >> 1) & 1), c ^ (k & 1))
            cps.append(pltpu.make_async_remote_copy(
                src_ref=p_ref, dst_ref=buf.at[me],
                send_sem=send_sems.at[k - 1], recv_sem=recv_sems.at[k - 1],
                device_id=peer, device_id_type=MESH))
        for cp in cps:
            cp.start()
        for cp in cps:
            cp.wait()
        total = buf[0]
        for j in range(1, N_DEV):
            total = total + buf[j]
        out_ref[...] = total

    vm = pl.BlockSpec(memory_space=pltpu.VMEM)
    return pl.pallas_call(
        body, name="small_allreduce",
        out_shape=jax.ShapeDtypeStruct((rows, d), F32),
        in_specs=[pl.BlockSpec(memory_space=pltpu.SMEM), vm], out_specs=vm,
        scratch_shapes=[pltpu.VMEM((N_DEV, rows, d), F32),
                        pltpu.SemaphoreType.DMA((N_DEV - 1,)), pltpu.SemaphoreType.DMA((N_DEV - 1,))],
        compiler_params=pltpu.CompilerParams(vmem_limit_bytes=V7X_VMEM_LIMIT),
    )(me_index, partial)


def _small_adamw(total, layout, me_index):
    rows, d = total.shape
    n = len(layout)
    cw = d // N_DEV

    def body(me_ref, t_ref, *refs):
        ins, outs = refs[:3 * n], refs[3 * n:]
        me = me_ref[0]
        for e, (kind, r0, nr, _, _, _) in enumerate(layout):
            w_ref, m_ref, v_ref = ins[3 * e:3 * e + 3]
            o = outs[4 * e:4 * e + 4]
            if kind == "rep":
                g = t_ref[r0:r0 + nr, :]
                delta, m2, v2 = _adamw(w_ref[...], g, m_ref[...], v_ref[...])
                for ref, val in zip(o, (g, delta, m2, v2)):
                    ref[...] = val
            elif kind == "wide":
                for q in range(nr):
                    sl = slice(q * d, (q + 1) * d)
                    g = t_ref[r0 + q:r0 + q + 1, :]
                    delta, m2, v2 = _adamw(w_ref[:, sl], g, m_ref[:, sl], v_ref[:, sl])
                    for ref, val in zip(o, (g, delta, m2, v2)):
                        ref[:, sl] = val
            else:
                for j in range(N_DEV):
                    @pl.when(me == j)
                    def _(j=j, o=o, w_ref=w_ref, m_ref=m_ref, v_ref=v_ref, r0=r0, nr=nr):
                        g = t_ref[r0:r0 + nr, j * cw:(j + 1) * cw]
                        delta, m2, v2 = _adamw(w_ref[...], g, m_ref[...], v_ref[...])
                        for ref, val in zip(o, (g, delta, m2, v2)):
                            ref[...] = val

    flat = [a for ent in layout for a in ent[3:]]
    vm = pl.BlockSpec(memory_space=pltpu.VMEM)
    outs = pl.pallas_call(
        body, name="small_adamw",
        out_shape=[jax.ShapeDtypeStruct(ent[3].shape, F32) for ent in layout for _ in range(4)],
        in_specs=[pl.BlockSpec(memory_space=pltpu.SMEM), vm] + [vm] * len(flat),
        out_specs=[vm] * (4 * n),
        compiler_params=pltpu.CompilerParams(vmem_limit_bytes=V7X_VMEM_LIMIT),
    )(me_index, total, *flat)
    return [tuple(outs[4 * e:4 * e + 4]) for e in range(n)]


def _ffn_fwd(h, g, wgu, wd, tm, loss=None, comm=None):
    tp, d = h.shape
    f = wd.shape[0]
    fc = f // FFN_CHUNKS
    nt = tp // tm
    with_loss = loss is not None
    if with_loss:
        tgt, gf, n_meta, t_real = loss

    def body(*refs):
        if with_loss:
            (h_ref, g_ref, wgu_hbm, wd_hbm, tgt_ref, gf_ref, out_ref, gu_ref, n_ref, loss_ref, dgf_ref,
             wgu_v, wd_v, sems) = refs
        else:
            h_ref, g_ref, wgu_hbm, wd_hbm, out_ref, gu_ref, n_ref, wgu_v, wd_v, sems = refs
        i = pl.program_id(0)

        @pl.when(i == 0)
        def _():
            _load_weights([(wgu_hbm, wgu_v), (wd_hbm, wd_v)], sems)
            if with_loss:
                loss_ref[...] = jnp.zeros_like(loss_ref)
                dgf_ref[...] = jnp.zeros_like(dgf_ref)

        x = h_ref[...]
        n, _ = _rms_fwd(x, g_ref[...])
        nb = n.astype(BF16)
        n_ref[...] = nb
        acc = jnp.zeros((tm, d), F32)
        for j in range(FFN_CHUNKS):
            cols = slice(j * fc, (j + 1) * fc)
            gate = _nt(nb, wgu_v[pl.ds(j * fc, fc), :])
            up = _nt(nb, wgu_v[pl.ds(f + j * fc, fc), :])
            gu_ref[0, :, cols] = gate.astype(BF16)
            gu_ref[1, :, cols] = up.astype(BF16)
            act = (gate * _sigmoid(gate) * up).astype(BF16)
            acc = acc + _nn(act, wd_v[pl.ds(j * fc, fc), :])
        hn = x + FFN_RES * acc
        if not with_loss:
            out_ref[...] = hn
        else:
            gfv = gf_ref[...]
            r = lax.rsqrt(jnp.mean(hn * hn, axis=-1, keepdims=True) + EPS)
            xr = hn * r
            rows = i * tm + lax.broadcasted_iota(jnp.int32, (tm, 1), 0)
            mask = jnp.logical_and(rows >= n_meta, rows < t_real)
            diff = jnp.where(mask, xr * gfv - tgt_ref[...], 0.0)
            loss_ref[...] += jnp.zeros_like(loss_ref) + 0.5 * jnp.sum(diff * diff) / d
            dy = diff / d
            gy = dy * gfv
            out_ref[...] = r * (gy - xr * jnp.mean(gy * xr, axis=-1, keepdims=True))
            dgf_ref[...] += _rowsum(dy * xr)

    row = pl.BlockSpec((tm, d), lambda i: (i, 0))
    vec = pl.BlockSpec((1, d), lambda i: (0, 0))
    in_specs = [row, vec, _any(), _any()]
    out_shape = [jax.ShapeDtypeStruct((tp, d), F32), jax.ShapeDtypeStruct((2, tp, f), BF16),
                 jax.ShapeDtypeStruct((tp, d), BF16)]
    out_specs = [row, pl.BlockSpec((2, tm, f), lambda i: (0, i, 0)), row]
    args = [h, g, wgu, wd]
    if with_loss:
        in_specs += [row, vec]
        out_shape += [jax.ShapeDtypeStruct((1, d), F32), jax.ShapeDtypeStruct((1, d), F32)]
        out_specs += [vec, vec]
        args += [tgt, gf]
    return _call(body, "ffn_fwd_loss" if with_loss else "ffn_fwd", (nt,), in_specs, out_specs, out_shape,
                 [pltpu.VMEM((2 * f, d), BF16), pltpu.VMEM((f, d), BF16), pltpu.SemaphoreType.DMA((2,))],
                 args, comm)


def _ffn_bwd(dh, h, gu, g, wgu, wd, tm, comm=None):
    tp, d = h.shape
    f = wd.shape[0]
    fc = f // FFN_CHUNKS
    nt = tp // tm

    def body(dh_ref, h_ref, gu_ref, g_ref, wgu_hbm, wd_hbm,
             dhin_ref, dgu_ref, act_ref, df_ref, dg_ref, wgu_v, wd_v, dn_v, sems):
        i, j = pl.program_id(0), pl.program_id(1)

        @pl.when(jnp.logical_and(i == 0, j == 0))
        def _():
            _load_weights([(wgu_hbm, wgu_v), (wd_hbm, wd_v)], sems)
            dg_ref[...] = jnp.zeros_like(dg_ref)

        dfb = (FFN_RES * dh_ref[...]).astype(BF16)

        @pl.when(j == 0)
        def _():
            df_ref[...] = dfb
            dn_v[...] = jnp.zeros_like(dn_v)

        lo = pl.multiple_of(j * fc, 16)
        dact = _nt(dfb, wd_v[pl.ds(lo, fc), :])
        gate = gu_ref[0].astype(F32)
        up = gu_ref[1].astype(F32)
        sg = _sigmoid(gate)
        silu = gate * sg
        act_ref[...] = (silu * up).astype(BF16)
        dgate = (dact * up * (sg * (1.0 + gate * (1.0 - sg)))).astype(BF16)
        dup = (dact * silu).astype(BF16)
        dgu_ref[0] = dgate
        dgu_ref[1] = dup
        dn_v[...] += _nn(dgate, wgu_v[pl.ds(lo, fc), :]) + _nn(dup, wgu_v[pl.ds(pl.multiple_of(f + j * fc, 16), fc), :])

        @pl.when(j == FFN_CHUNKS - 1)
        def _():
            x = h_ref[...]
            r = lax.rsqrt(jnp.mean(x * x, axis=-1, keepdims=True) + EPS)
            dx, dgp = _rms_bwd(dn_v[...], x, r, g_ref[...])
            dhin_ref[...] = dh_ref[...] + dx
            dg_ref[...] += dgp

    row = pl.BlockSpec((tm, d), lambda i, j: (i, 0))
    vec = pl.BlockSpec((1, d), lambda i, j: (0, 0))
    hid2 = pl.BlockSpec((2, tm, fc), lambda i, j: (0, i, j))
    return _call(
        body, "ffn_bwd", (nt, FFN_CHUNKS),
        [row, row, hid2, vec, _any(), _any()],
        [row, hid2, pl.BlockSpec((tm, fc), lambda i, j: (i, j)), row, vec],
        [jax.ShapeDtypeStruct((tp, d), F32), jax.ShapeDtypeStruct((2, tp, f), BF16),
         jax.ShapeDtypeStruct((tp, f), BF16), jax.ShapeDtypeStruct((tp, d), BF16),
         jax.ShapeDtypeStruct((1, d), F32)],
        [pltpu.VMEM((2 * f, d), BF16), pltpu.VMEM((f, d), BF16), pltpu.VMEM((tm, d), F32),
         pltpu.SemaphoreType.DMA((2,))],
        [dh, h, gu, g, wgu, wd], comm)


def _mixer_in_fwd(h, g, win_all, b_in, tm, comm=None):
    tp, d = h.shape
    nb_cols = win_all.shape[-1]
    n_in = N_DEV * nb_cols

    def body(h_ref, g_ref, b_ref, win_hbm, proj_ref, n_ref, win_v, sems):
        @pl.when(pl.program_id(0) == 0)
        def _():
            _load_weights([(win_hbm, win_v)], sems)

        n, _ = _rms_fwd(h_ref[...], g_ref[...])
        nb = n.astype(BF16)
        n_ref[...] = nb
        for j in range(N_DEV):
            sl = slice(j * nb_cols, (j + 1) * nb_cols)
            proj_ref[:, sl] = (_nn(nb, win_v[j]) + b_ref[:, sl]).astype(BF16)

    row = pl.BlockSpec((tm, d), lambda i: (i, 0))
    return _call(
        body, "mixer_in_fwd", (tp // tm,),
        [row, pl.BlockSpec((1, d), lambda i: (0, 0)), pl.BlockSpec((1, n_in), lambda i: (0, 0)), _any()],
        [pl.BlockSpec((tm, n_in), lambda i: (i, 0)), row],
        [jax.ShapeDtypeStruct((tp, n_in), BF16), jax.ShapeDtypeStruct((tp, d), BF16)],
        [pltpu.VMEM(win_all.shape, BF16), pltpu.SemaphoreType.DMA((1,))],
        [h, g, b_in, win_all], comm)


def _mixer_in_bwd(dh, h, dproj, g, win_all, tm, comm=None):
    tp, d = h.shape
    nb_cols = win_all.shape[-1]
    n_in = N_DEV * nb_cols

    def body(dh_ref, h_ref, dp_ref, g_ref, win_hbm, dhin_ref, dg_ref, win_v, sems):
        @pl.when(pl.program_id(0) == 0)
        def _():
            _load_weights([(win_hbm, win_v)], sems)
            dg_ref[...] = jnp.zeros_like(dg_ref)

        dn = jnp.zeros((tm, d), F32)
        for j in range(N_DEV):
            dn = dn + _nt(dp_ref[:, j * nb_cols:(j + 1) * nb_cols], win_v[j])
        x = h_ref[...]
        r = lax.rsqrt(jnp.mean(x * x, axis=-1, keepdims=True) + EPS)
        dx, dgp = _rms_bwd(dn, x, r, g_ref[...])
        dhin_ref[...] = dh_ref[...] + dx
        dg_ref[...] += dgp

    row = pl.BlockSpec((tm, d), lambda i: (i, 0))
    vec = pl.BlockSpec((1, d), lambda i: (0, 0))
    return _call(
        body, "mixer_in_bwd", (tp // tm,),
        [row, row, pl.BlockSpec((tm, n_in), lambda i: (i, 0)), vec, _any()],
        [row, vec],
        [jax.ShapeDtypeStruct((tp, d), F32), jax.ShapeDtypeStruct((1, d), F32)],
        [pltpu.VMEM(win_all.shape, BF16), pltpu.SemaphoreType.DMA((1,))],
        [dh, h, dproj, g, win_all], comm)


def _w3_copies(w3_hbm, rows, w3_v):
    return [(w3_hbm.at[k, pl.ds(q * rows, rows)], w3_v.at[q, pl.ds(k * rows, rows)])
            for q in range(3) for k in range(N_DEV)]


def _gates(xrb, wg_ref, ba, bx, lam, hd):
    pre_r, pre_i = [], []
    for hh in range(N_HEADS):
        xh = xrb[:, hh * hd:(hh + 1) * hd]
        pre_r.append(_nn(xh, wg_ref[0, hh]))
        pre_i.append(_nn(xh, wg_ref[1, hh]))
    r = _sigmoid(jnp.concatenate(pre_r, axis=1) + ba)
    ig = _sigmoid(jnp.concatenate(pre_i, axis=1) + bx)
    sp = _softplus(-lam)
    log_a = -RG_LRU_C * r * sp
    a = jnp.exp(log_a)
    s = jnp.sqrt(_one_minus_exp(2.0 * log_a))
    return r, ig, sp, a, s


def _scan_fwd(a, u, h_prev):
    tm = a.shape[0]
    rows = lax.broadcasted_iota(jnp.int32, a.shape, 0)
    d = 1
    while d < tm:
        keep = rows >= d
        u = jnp.where(keep, a * pltpu.roll(u, d, 0) + u, u)
        a = jnp.where(keep, a * pltpu.roll(a, d, 0), a)
        d *= 2
    return u + a * h_prev


def _scan_bwd(b, v, g_next):
    tm = b.shape[0]
    rows = lax.broadcasted_iota(jnp.int32, b.shape, 0)
    d = 1
    while d < tm:
        keep = rows < tm - d
        v = jnp.where(keep, v + b * pltpu.roll(v, tm - d, 0), v)
        b = jnp.where(keep, b * pltpu.roll(b, tm - d, 0), b)
        d *= 2
    return v + b * g_next


def _shifted_copies(ext_ref, es_ref, n_rows):
    for s in range(1, SUBLANES):
        es_ref[s, pl.ds(0, n_rows), :] = ext_ref[pl.ds(s, n_rows), :]


def _tap(ext_ref, es_ref, off, tm):
    q, s = divmod(off, SUBLANES)
    if s == 0:
        return ext_ref[pl.ds(SUBLANES * q, tm), :]
    return es_ref[s, pl.ds(SUBLANES * q, tm), :]


def _mixer_core_fwd(proj, h, cw4, cb4, wg, ba, bx, lam, cw31, cb31, lng, lnb, bcp, w3_all, tm, comm=None):
    tp, d = h.shape
    n_in = proj.shape[1]
    hd = wg.shape[-1]
    k4, k31 = cw4.shape[0], cw31.shape[0]
    w3_rows = d // N_DEV

    def body(p_ref, h_ref, cw4_ref, cb4_ref, wg_ref, ba_ref, bx_ref, lam_ref, cw31_ref, cb31_ref,
             lng_ref, lnb_ref, bcp_ref, w3_hbm,
             h2_ref, xr_ref, hs_ref, v1_ref, ya_ref, yb_ref,
             w3_v, ext4, ext31, es31, hcar, sems):
        @pl.when(pl.program_id(0) == 0)
        def _():
            _load_weights(_w3_copies(w3_hbm, w3_rows, w3_v), sems)
            ext4[pl.ds(0, CONV4_HALO), :] = jnp.zeros((CONV4_HALO, d), F32)
            ext31[pl.ds(0, CONV31_HALO), :] = jnp.zeros((CONV31_HALO, d), F32)
            hcar[...] = jnp.zeros_like(hcar)

        x_rnn = p_ref[:, 0:d].astype(F32)
        y_rnn = p_ref[:, d:2 * d].astype(F32)
        glu_v = p_ref[:, 2 * d:3 * d].astype(F32)
        glu_g = p_ref[:, 3 * d:4 * d].astype(F32)
        gate_a = p_ref[:, 4 * d:5 * d].astype(F32)
        gate_b = p_ref[:, 5 * d:6 * d].astype(F32)

        ext4[pl.ds(CONV4_HALO, tm), :] = x_rnn
        xr = cb4_ref[...] + jnp.zeros((tm, d), F32)
        for k in range(k4):
            xr = xr + cw4_ref[k:k + 1, :] * ext4[pl.ds(CONV4_HALO - (k4 - 1) + k, tm), :]
        ext4[pl.ds(0, CONV4_HALO), :] = ext4[pl.ds(tm, CONV4_HALO), :]
        xrb = xr.astype(BF16)
        xr_ref[...] = xrb
        xr = xrb.astype(F32)
        _, ig, _, a, s = _gates(xrb, wg_ref, ba_ref[...], bx_ref[...], lam_ref[...], hd)
        hseq = _scan_fwd(a, s * (ig * xr), hcar[0:1, :])
        hcar[0:1, :] = hseq[tm - 1:tm, :]
        hs_ref[...] = hseq.astype(BF16)
        gl, _ = _gelu(y_rnn)
        ya = _nn((hseq * gl).astype(BF16), w3_v[0])
        ya_ref[...] = ya.astype(BF16)

        ext31[pl.ds(CONV31_HALO, tm), :] = glu_v * _sigmoid(glu_g)
        _shifted_copies(ext31, es31, tm + CONV31_HALO - SUBLANES)
        v1 = cb31_ref[...] + jnp.zeros((tm, d), F32)
        for k in range(k31):
            v1 = v1 + cw31_ref[k:k + 1, :] * _tap(ext31, es31, CONV31_HALO - (k31 - 1) + k, tm)
        ext31[pl.ds(0, CONV31_HALO), :] = ext31[pl.ds(tm, CONV31_HALO), :]
        v1b = v1.astype(BF16)
        v1_ref[...] = v1b
        v1 = v1b.astype(F32)
        xc = v1 - jnp.mean(v1, axis=-1, keepdims=True)
        rstd = lax.rsqrt(jnp.mean(xc * xc, axis=-1, keepdims=True) + EPS)
        v2 = xc * rstd * lng_ref[...] + lnb_ref[...]
        yb = _nn((v2 * _sigmoid(v2)).astype(BF16), w3_v[1]) + bcp_ref[...]
        yb_ref[...] = yb.astype(BF16)

        merged = _sigmoid(gate_a) * ya + _sigmoid(gate_b) * yb
        h2_ref[...] = h_ref[...] + _nn(merged.astype(BF16), w3_v[2])

    row = pl.BlockSpec((tm, d), lambda i: (i, 0))
    full = lambda a: pl.BlockSpec(a.shape, lambda i, nd=a.ndim: (0,) * nd)
    smalls = [cw4, cb4, wg, ba, bx, lam, cw31, cb31, lng, lnb, bcp]
    return _call(
        body, "mixer_core_fwd", (tp // tm,),
        [pl.BlockSpec((tm, n_in), lambda i: (i, 0)), row] + [full(a) for a in smalls] + [_any()],
        [row] * 6,
        [jax.ShapeDtypeStruct((tp, d), F32)] + [jax.ShapeDtypeStruct((tp, d), BF16)] * 5,
        [pltpu.VMEM((3, d, d), BF16),
         pltpu.VMEM((tm + CONV4_HALO, d), F32),
         pltpu.VMEM((tm + CONV31_HALO, d), F32),
         pltpu.VMEM((SUBLANES, tm + CONV31_HALO, d), F32),
         pltpu.VMEM((SUBLANES, d), F32),
         pltpu.SemaphoreType.DMA((3 * N_DEV,))],
        [proj, h, *smalls, w3_all], comm)


SG_BIN, SG_CW4, SG_CB4, SG_BA, SG_BX, SG_LAM, SG_CB31, SG_LNG, SG_LNB, SG_BCP, SG_CW31 = 0, 6, 10, 11, 12, 13, 14, 15, 16, 17, 18


def _mixer_core_bwd(dh2, proj, xr_s, hs_s, v1_s, ya_s, yb_s, cw4, wg, ba, bx, lam, cw31, lng, lnb, w3_all, tm, comm=None):
    tp, d = dh2.shape
    n_in = proj.shape[1]
    hd = wg.shape[-1]
    k4, k31 = cw4.shape[0], cw31.shape[0]
    nt = tp // tm
    w3_rows = d // N_DEV
    sg_rows = -(-(SG_CW31 + k31) // SUBLANES) * SUBLANES
    halo_rows = 16
    per = tm // halo_rows

    def body(dh_ref, p_ref, xr_ref, hs_ref, hh_ref, v1_ref, ya_ref, yb_ref,
             cw4_ref, wg_ref, ba_ref, bx_ref, lam_ref, cw31_ref, lng_ref, lnb_ref, w3_hbm,
             dp_ref, x3_ref, y3_ref, yg_ref, sg_ref,
             w3_v, extd4, extd31, es31, gcar, sems):
        i = pl.program_id(0)
        tile = nt - 1 - i

        @pl.when(i == 0)
        def _():
            _load_weights(_w3_copies(w3_hbm, w3_rows, w3_v), sems)
            extd4[pl.ds(tm, CONV4_HALO), :] = jnp.zeros((CONV4_HALO, d), F32)
            extd31[pl.ds(tm, CONV31_HALO), :] = jnp.zeros((CONV31_HALO, d), F32)
            gcar[...] = jnp.zeros_like(gcar)
            sg_ref[...] = jnp.zeros_like(sg_ref)

        def acc(row, val):
            sg_ref[row:row + 1, :] += _rowsum(val)

        rows = lax.broadcasted_iota(jnp.int32, (tm, d), 0)
        x_rnn = p_ref[:, 0:d].astype(F32)
        y_rnn = p_ref[:, d:2 * d].astype(F32)
        glu_v = p_ref[:, 2 * d:3 * d].astype(F32)
        glu_g = p_ref[:, 3 * d:4 * d].astype(F32)
        sga = _sigmoid(p_ref[:, 4 * d:5 * d].astype(F32))
        sgb = _sigmoid(p_ref[:, 5 * d:6 * d].astype(F32))
        ya = ya_ref[...].astype(F32)
        yb = yb_ref[...].astype(F32)

        dmob = dh_ref[...].astype(BF16)
        dmerged = _nt(dmob, w3_v[2])
        x3_ref[:, 0:d] = (sga * ya + sgb * yb).astype(BF16)
        y3_ref[:, 0:d] = dmob
        dya = sga * dmerged
        dyb = sgb * dmerged
        pieces = [None, None, None, None, dmerged * ya * sga * (1.0 - sga), dmerged * yb * sgb * (1.0 - sgb)]

        dyab = dya.astype(BF16)
        y3_ref[:, d:2 * d] = dyab
        dza = _nt(dyab, w3_v[0])
        hsv = hs_ref[...].astype(F32)
        gl, th = _gelu(y_rnn)
        x3_ref[:, d:2 * d] = (hsv * gl).astype(BF16)
        pieces[1] = dza * hsv * _gelu_grad(y_rnn, th)
        dhs = dza * gl
        xrb = xr_ref[...]
        xr = xrb.astype(F32)
        lam_v = lam_ref[...]
        r, ig, sp, a, s = _gates(xrb, wg_ref, ba_ref[...], bx_ref[...], lam_v, hd)
        b = jnp.where(rows == tm - 1, gcar[1:2, :], pltpu.roll(a, tm - 1, 0))
        big_g = _scan_bwd(b, dhs, gcar[0:1, :])
        gcar[0:1, :] = big_g[0:1, :]
        gcar[1:2, :] = a[0:1, :]
        h_before = jnp.where(tile > 0, hh_ref[halo_rows - 1:halo_rows, :].astype(F32), 0.0)
        h_prev = jnp.where(rows == 0, h_before, pltpu.roll(hsv, 1, 0))
        ds = big_g * ig * xr
        dla = big_g * h_prev * a - ds * (a * a) / jnp.maximum(s, 1e-20)
        acc(SG_LAM, dla * r * (RG_LRU_C * _sigmoid(-lam_v)))
        dpr = dla * (-RG_LRU_C * sp) * r * (1.0 - r)
        dpi = big_g * s * xr * ig * (1.0 - ig)
        acc(SG_BA, dpr)
        acc(SG_BX, dpi)
        dprb = dpr.astype(BF16)
        dpib = dpi.astype(BF16)
        yg_ref[:, 0:d] = dprb
        yg_ref[:, d:2 * d] = dpib
        back = []
        for hh in range(N_HEADS):
            sl = slice(hh * hd, (hh + 1) * hd)
            back.append(_nt(dprb[:, sl], wg_ref[0, hh]) + _nt(dpib[:, sl], wg_ref[1, hh]))
        dxr = big_g * s * ig + jnp.concatenate(back, axis=1)
        acc(SG_CB4, dxr)
        extd4[pl.ds(0, tm), :] = dxr
        dx_rnn = jnp.zeros((tm, d), F32)
        for k in range(k4):
            term = extd4[pl.ds(k4 - 1 - k, tm), :]
            dx_rnn = dx_rnn + cw4_ref[k:k + 1, :] * term
            acc(SG_CW4 + k, x_rnn * term)
        extd4[pl.ds(tm, CONV4_HALO), :] = extd4[pl.ds(0, CONV4_HALO), :]
        pieces[0] = dx_rnn

        dybb = dyb.astype(BF16)
        y3_ref[:, 2 * d:3 * d] = dybb
        acc(SG_BCP, dyb)
        dv3 = _nt(dybb, w3_v[1])
        v1 = v1_ref[...].astype(F32)
        xc = v1 - jnp.mean(v1, axis=-1, keepdims=True)
        rstd = lax.rsqrt(jnp.mean(xc * xc, axis=-1, keepdims=True) + EPS)
        xhat = xc * rstd
        lng_v = lng_ref[...]
        v2 = xhat * lng_v + lnb_ref[...]
        s2 = _sigmoid(v2)
        x3_ref[:, 2 * d:3 * d] = (v2 * s2).astype(BF16)
        dv2 = dv3 * (s2 * (1.0 + v2 * (1.0 - s2)))
        acc(SG_LNG, dv2 * xhat)
        acc(SG_LNB, dv2)
        dxh = dv2 * lng_v
        dv1 = rstd * (dxh - jnp.mean(dxh, axis=-1, keepdims=True)
                      - xhat * jnp.mean(dxh * xhat, axis=-1, keepdims=True))
        acc(SG_CB31, dv1)
        extd31[pl.ds(0, tm), :] = dv1
        _shifted_copies(extd31, es31, tm + CONV31_HALO - SUBLANES)
        sgg = _sigmoid(glu_g)
        v0 = glu_v * sgg
        dv0 = jnp.zeros((tm, d), F32)
        for k in range(k31):
            term = _tap(extd31, es31, k31 - 1 - k, tm)
            dv0 = dv0 + cw31_ref[k:k + 1, :] * term
            acc(SG_CW31 + k, v0 * term)
        extd31[pl.ds(tm, CONV31_HALO), :] = extd31[pl.ds(0, CONV31_HALO), :]
        pieces[2] = dv0 * sgg
        pieces[3] = dv0 * glu_v * sgg * (1.0 - sgg)

        for q, piece in enumerate(pieces):
            dp_ref[:, q * d:(q + 1) * d] = piece.astype(BF16)
            acc(SG_BIN + q, piece)

    rev = lambda i: (nt - 1 - i, 0)
    row = pl.BlockSpec((tm, d), rev)
    full = lambda a: pl.BlockSpec(a.shape, lambda i, nd=a.ndim: (0,) * nd)
    halo = pl.BlockSpec((halo_rows, d), lambda i: (jnp.maximum((nt - 1 - i) * per - 1, 0), 0))
    smalls = [cw4, wg, ba, bx, lam, cw31, lng, lnb]
    return _call(
        body, "mixer_core_bwd", (nt,),
        [row, pl.BlockSpec((tm, n_in), rev), row, row, halo, row, row, row]
        + [full(a) for a in smalls] + [_any()],
        [pl.BlockSpec((tm, n_in), rev), pl.BlockSpec((tm, 3 * d), rev), pl.BlockSpec((tm, 3 * d), rev),
         pl.BlockSpec((tm, 2 * d), rev), pl.BlockSpec((sg_rows, d), lambda i: (0, 0))],
        [jax.ShapeDtypeStruct((tp, n_in), BF16), jax.ShapeDtypeStruct((tp, 3 * d), BF16),
         jax.ShapeDtypeStruct((tp, 3 * d), BF16), jax.ShapeDtypeStruct((tp, 2 * d), BF16),
         jax.ShapeDtypeStruct((sg_rows, d), F32)],
        [pltpu.VMEM((3, d, d), BF16),
         pltpu.VMEM((tm + CONV4_HALO, d), F32),
         pltpu.VMEM((tm + CONV31_HALO, d), F32),
         pltpu.VMEM((SUBLANES, tm + CONV31_HALO, d), F32),
         pltpu.VMEM((SUBLANES, d), F32),
         pltpu.SemaphoreType.DMA((3 * N_DEV,))],
        [dh2, proj, xr_s, hs_s, hs_s, v1_s, ya_s, yb_s, *smalls, w3_all], comm)


def _tn_matmul(name, x, y, x_spec, y_spec, n_blocks, kb, nb, tm, tp, out_shape, out_spec, out_view, comm=None):
    nt = tp // tm

    def body(x_ref, y_ref, o_ref, acc):
        i = pl.program_id(1)

        @pl.when(i == 0)
        def _():
            acc[...] = jnp.zeros_like(acc)

        acc[...] += _tn(x_ref[...], y_ref[...])

        @pl.when(i == nt - 1)
        def _():
            o_ref[...] = acc[...].astype(BF16).reshape(out_view)

    outs, extra = _call(body, name, (n_blocks, nt), [x_spec, y_spec], [out_spec],
                        [jax.ShapeDtypeStruct(out_shape, BF16)], [pltpu.VMEM((kb, nb), F32)], [x, y], comm)
    return outs[0], extra


def kernel(x, meta_tokens, ffn1_norm, ffn1_w_gu, ffn1_w_down, mix_norm, w_in, b_in, rnn_conv_w, rnn_conv_b, rg_w_a, rg_b_a, rg_w_x, rg_b_x, rg_lambda, rnn_w_proj, conv_dw_w, conv_dw_b, conv_ln_g, conv_ln_b, conv_w_proj, conv_b_proj, w_out, ffn2_norm, ffn2_w_gu, ffn2_w_down, final_norm, loss_target, m_meta_tokens, m_ffn1_norm, m_ffn1_w_gu, m_ffn1_w_down, m_mix_norm, m_w_in, m_b_in, m_rnn_conv_w, m_rnn_conv_b, m_rg_w_a, m_rg_b_a, m_rg_w_x, m_rg_b_x, m_rg_lambda, m_rnn_w_proj, m_conv_dw_w, m_conv_dw_b, m_conv_ln_g, m_conv_ln_b, m_conv_w_proj, m_conv_b_proj, m_w_out, m_ffn2_norm, m_ffn2_w_gu, m_ffn2_w_down, m_final_norm, v_meta_tokens, v_ffn1_norm, v_ffn1_w_gu, v_ffn1_w_down, v_mix_norm, v_w_in, v_b_in, v_rnn_conv_w, v_rnn_conv_b, v_rg_w_a, v_rg_b_a, v_rg_w_x, v_rg_b_x, v_rg_lambda, v_rnn_w_proj, v_conv_dw_w, v_conv_dw_b, v_conv_ln_g, v_conv_ln_b, v_conv_w_proj, v_conv_b_proj, v_w_out, v_ffn2_norm, v_ffn2_w_gu, v_ffn2_w_down, v_final_norm):
    w = dict(locals())
    seq, d = x.shape[1], x.shape[2]
    n_meta = meta_tokens.shape[0]
    t_real = n_meta + seq
    tp, tm, tmx, tmt = _tiles(t_real)
    fb = ffn1_w_gu.shape[-1]
    wr = ffn1_w_down.shape[1]
    f = N_DEV * wr
    fc = f // FFN_CHUNKS
    nbc = w_in.shape[-1]
    n_in = N_DEV * nbc
    pr = rnn_w_proj.shape[1]
    hd = rg_w_a.shape[-1]
    gr = rg_w_a.shape[2]
    cw = meta_tokens.shape[1]
    k4, k31 = rnn_conv_w.shape[1], conv_dw_w.shape[1]
    assert n_in == 6 * d and 2 * wr == fb and N_HEADS * hd == d and pr * N_DEV == d

    xi, yi, ci = lax.axis_index("x"), lax.axis_index("y"), lax.axis_index("c")
    core = ci.astype(jnp.int32).reshape(1)
    chip = (2 * xi + yi).astype(jnp.int32).reshape(1)
    me_index = (4 * xi + 2 * yi + ci).astype(jnp.int32).reshape(1)

    for nm in ("ffn1_w_gu", "ffn2_w_gu"):
        for pre in ("", "m_", "v_"):
            w[pre + nm] = jnp.swapaxes(w[pre + nm], 1, 2)

    wgut1 = w["ffn1_w_gu"][0].astype(BF16)
    wgut2 = w["ffn2_w_gu"][0].astype(BF16)
    wd1 = ffn1_w_down[0].astype(BF16)
    wd2 = ffn2_w_down[0].astype(BF16)
    win_loc = w_in[0].astype(BF16)
    w3_loc = jnp.concatenate([rnn_w_proj[0], conv_w_proj[0], w_out[0]], axis=0).astype(BF16)
    wg_loc = jnp.stack([rg_w_a[0], rg_w_x[0]]).astype(BF16)
    n_small = n_meta + k4 + k31
    small_rows = -(-n_small // SUBLANES) * SUBLANES
    small_loc = jnp.concatenate([meta_tokens, rnn_conv_w[0], conv_dw_w[0],
                                 jnp.zeros((small_rows - n_small, cw), F32)], axis=0)
    wgut1_all, wd1_all, wg_all, small_all = _all_gather([wgut1, wd1, wg_loc, small_loc])
    wg = wg_all.transpose(1, 2, 0, 3, 4).reshape(2, N_HEADS, hd, hd)
    small_full = small_all.transpose(1, 0, 2).reshape(small_rows, d)
    meta_full = small_full[:n_meta]
    cw4 = small_full[n_meta:n_meta + k4]
    cw31 = small_full[n_meta + k4:n_meta + k4 + k31]

    pad = jnp.zeros((tp - t_real, d), F32)
    h0 = jnp.concatenate([meta_full, x[0], pad], axis=0)
    tgt = jnp.concatenate([jnp.zeros((n_meta, d), F32), loss_target[0], pad], axis=0)
    wgu1, wdn1 = wgut1_all.reshape(2 * f, d), wd1_all.reshape(f, d)
    (h1, gu1, n1), (win_all,) = _ffn_fwd(h0, ffn1_norm, wgu1, wdn1, tm, comm=_Gather([win_loc]))
    (proj, n2), (w3_all,) = _mixer_in_fwd(h1, mix_norm, win_all, b_in, tm, comm=_Gather([w3_loc]))
    (h2, xr_s, hs_s, v1_s, ya_s, yb_s), (wgut2_all, wd2_all) = _mixer_core_fwd(
        proj, h1, cw4, rnn_conv_b, wg, rg_b_a, rg_b_x, rg_lambda, cw31, conv_dw_b, conv_ln_g, conv_ln_b,
        conv_b_proj, w3_all, tmx, comm=_Gather([wgut2, wd2]))
    wgu2, wdn2 = wgut2_all.reshape(2 * f, d), wd2_all.reshape(f, d)
    (dh3, gu2, n3, loss_part, dgf), _ = _ffn_fwd(h2, ffn2_norm, wgu2, wdn2, tm,
                                                 loss=(tgt, final_norm.reshape(1, d), n_meta, t_real))

    def d_w_gu(tag, dgu, n_s, comm=None):
        g, extra = _tn_matmul(
            "d_w_gu" + tag, dgu, n_s,
            pl.BlockSpec((None, tmt, fc), lambda b, i: (b // FFN_CHUNKS, i, b % FFN_CHUNKS)),
            pl.BlockSpec((tmt, d), lambda b, i: (i, 0)),
            2 * FFN_CHUNKS, fc, d, tmt, tp, (2 * FFN_CHUNKS, fc, d),
            pl.BlockSpec((None, fc, d), lambda b, i: (b, 0, 0)), (fc, d), comm)
        return g.reshape(N_DEV, fb, d), extra

    def d_w_down(tag, act, df):
        g, _ = _tn_matmul(
            "d_w_down" + tag, act, df,
            pl.BlockSpec((tmt, fc), lambda b, i: (i, b)), pl.BlockSpec((tmt, d), lambda b, i: (i, 0)),
            FFN_CHUNKS, fc, d, tmt, tp, (FFN_CHUNKS, fc, d),
            pl.BlockSpec((None, fc, d), lambda b, i: (b, 0, 0)), (fc, d))
        return g.reshape(N_DEV, wr, d)

    (dh2, dgu2, act2, df2, dg_ffn2), _ = _ffn_bwd(dh3, h2, gu2, ffn2_norm, wgu2, wdn2, tm)
    g_wgu2, _ = d_w_gu("2", dgu2, n3)
    g_wd2 = d_w_down("2", act2, df2)
    (dproj, x3, y3, yg, sg), (r_wd2, r_wgu2) = _mixer_core_bwd(
        dh2, proj, xr_s, hs_s, v1_s, ya_s, yb_s, cw4, wg, rg_b_a, rg_b_x, rg_lambda, cw31, conv_ln_g, conv_ln_b,
        w3_all, tmx, comm=_Scatter([g_wd2, g_wgu2]))
    g_w3, _ = _tn_matmul(
        "d_w_proj3", x3, y3,
        pl.BlockSpec((tmt, d), lambda b, i: (i, b)), pl.BlockSpec((tmt, d), lambda b, i: (i, b)),
        3, d, d, tmt, tp, (N_DEV, 3, pr, d), pl.BlockSpec((N_DEV, None, pr, d), lambda b, i: (0, b, 0, 0)),
        (N_DEV, pr, d))
    g_wg, _ = _tn_matmul(
        "d_w_gates", xr_s, yg,
        pl.BlockSpec((tmt, hd), lambda b, i: (i, b % N_HEADS)), pl.BlockSpec((tmt, hd), lambda b, i: (i, b)),
        2 * N_HEADS, hd, hd, tmt, tp, (N_DEV, 2 * N_HEADS, gr, hd),
        pl.BlockSpec((N_DEV, None, gr, hd), lambda b, i: (0, b, 0, 0)), (N_DEV, gr, hd))
    (dh1, dg_mix), (r_w3, r_wg) = _mixer_in_bwd(dh2, h1, dproj, mix_norm, win_all, tm, comm=_Scatter([g_w3, g_wg]))
    g_win, _ = _tn_matmul(
        "d_w_in", n2, dproj,
        pl.BlockSpec((tmt, d), lambda b, i: (i, 0)), pl.BlockSpec((tmt, nbc), lambda b, i: (i, b)),
        N_DEV, d, nbc, tmt, tp, (N_DEV, d, nbc), pl.BlockSpec((None, d, nbc), lambda b, i: (b, 0, 0)), (d, nbc))
    (dh0, dgu1, act1, df1, dg_ffn1), (r_win,) = _ffn_bwd(dh1, h0, gu1, ffn1_norm, wgu1, wdn1, tm,
                                                         comm=_Scatter([g_win]))
    g_wd1 = d_w_down("1", act1, df1)
    g_wgu1, (r_wd1,) = d_w_gu("1", dgu1, n1, comm=_Scatter([g_wd1]))
    grad_x = dh0[n_meta:t_real][None]

    g_last = g_wgu1.reshape((4, 2) + g_wgu1.shape[1:])
    (from_sibling,) = _pair_exchange([g_last])
    comb_wgu1 = _pair_add(g_last, from_sibling, core)
    (r_wgu1,) = _chip_exchange([comb_wgu1])

    groups = [(g_wd1, r_wd1, me_index, ["ffn1_w_down"]), (comb_wgu1, r_wgu1, chip, ["ffn1_w_gu"]),
              (g_wd2, r_wd2, me_index, ["ffn2_w_down"]), (g_wgu2, r_wgu2, me_index, ["ffn2_w_gu"]),
              (g_win, r_win, me_index, ["w_in"]), (g_w3, r_w3, me_index, ["w_out", "rnn_w_proj", "conv_w_proj"]),
              (g_wg, r_wg, me_index, ["rg_w_a", "rg_w_x"])]
    res = {}
    for own, recv, idx, group in groups:
        outs = _final_adamw(own, recv, idx, [(w[nm], w["m_" + nm], w["v_" + nm]) for nm in group])
        for nm, o in zip(group, outs):
            res[nm] = o
    for nm in ("ffn1_w_gu", "ffn2_w_gu"):
        res[nm] = tuple(jnp.swapaxes(a, 1, 2) for a in res[nm])

    rep_rows = [("ffn1_norm", dg_ffn1), ("mix_norm", dg_mix), ("b_in", sg[SG_BIN:SG_BIN + 6]),
                ("rnn_conv_b", sg[SG_CB4:SG_CB4 + 1]), ("rg_b_a", sg[SG_BA:SG_BA + 1]),
                ("rg_b_x", sg[SG_BX:SG_BX + 1]), ("rg_lambda", sg[SG_LAM:SG_LAM + 1]),
                ("conv_dw_b", sg[SG_CB31:SG_CB31 + 1]), ("conv_ln_g", sg[SG_LNG:SG_LNG + 1]),
                ("conv_ln_b", sg[SG_LNB:SG_LNB + 1]), ("conv_b_proj", sg[SG_BCP:SG_BCP + 1]),
                ("ffn2_norm", dg_ffn2), ("final_norm", dgf)]
    col_rows = [("meta_tokens", dh0[:n_meta]), ("rnn_conv_w", sg[SG_CW4:SG_CW4 + k4]),
                ("conv_dw_w", sg[SG_CW31:SG_CW31 + k31])]
    layout, pieces, r0 = [], [], 0
    for nm, part in rep_rows:
        nr = part.shape[0]
        kind = "wide" if nm == "b_in" else "rep"
        as2d = lambda a: a.reshape(1, -1) if a.ndim == 1 else a
        layout.append((kind, r0, nr, as2d(w[nm]), as2d(w["m_" + nm]), as2d(w["v_" + nm])))
        pieces.append(part)
        r0 += nr
    for nm, part in col_rows:
        nr = part.shape[0]
        sq = lambda a: a.reshape(a.shape[-2], a.shape[-1])
        layout.append(("col", r0, nr, sq(w[nm]), sq(w["m_" + nm]), sq(w["v_" + nm])))
        pieces.append(part)
        r0 += nr
    total_rows = -(-(r0 + 1) // SUBLANES) * SUBLANES
    pieces.append(jnp.zeros((total_rows - 1 - r0, d), F32))
    pieces.append(loss_part)
    total = _small_allreduce(jnp.concatenate(pieces, axis=0), me_index)
    small_out = _small_adamw(total, layout, me_index)
    for (nm, _), o in zip(rep_rows + col_rows, small_out):
        res[nm] = tuple(a.reshape(w[nm].shape) for a in o)

    order = ["meta_tokens", "ffn1_norm", "ffn1_w_gu", "ffn1_w_down", "mix_norm", "w_in", "b_in", "rnn_conv_w",
             "rnn_conv_b", "rg_w_a", "rg_b_a", "rg_w_x", "rg_b_x", "rg_lambda", "rnn_w_proj", "conv_dw_w",
             "conv_dw_b", "conv_ln_g", "conv_ln_b", "conv_w_proj", "conv_b_proj", "w_out", "ffn2_norm",
             "ffn2_w_gu", "ffn2_w_down", "final_norm"]
    return (total[total_rows - 1, 0], grad_x, *[res[nm][0] for nm in order], *[res[nm][1] for nm in order],
            *[res[nm][2] for nm in order], *[res[nm][3] for nm in order])
```

```python
import functools
import math

import jax
import jax.numpy as jnp
from jax import lax
from jax.experimental import pallas as pl
from jax.experimental.pallas import tpu as pltpu

F32 = jnp.float32
BF16 = jnp.bfloat16
MESH = pl.DeviceIdType.MESH
N_DEV = 8
N_HEADS = 4
RG_LRU_C = 8.0
EPS = 1e-6
FFN_RES = 0.5
ADAM_LR, ADAM_B1, ADAM_B2, ADAM_EPS, ADAM_WD, ADAM_STEP = 0.001, 0.9, 0.999, 1e-08, 0.01, 10
V7X_VMEM_LIMIT = 56 * 1024 * 1024
CONV4_HALO = 8
CONV31_HALO = 32
SUBLANES = 8
FFN_CHUNKS = 2
GELU_C = math.sqrt(2.0 / math.pi)
GELU_K = 0.044715


def _any():
    return pl.BlockSpec(memory_space=pl.ANY)


def _params(n_grid):
    return pltpu.CompilerParams(dimension_semantics=("arbitrary",) * n_grid, vmem_limit_bytes=V7X_VMEM_LIMIT)


def _nn(a, b):
    return jnp.dot(a, b, preferred_element_type=F32)


def _nt(a, b):
    return lax.dot_general(a, b, (((1,), (1,)), ((), ())), preferred_element_type=F32)


def _tn(a, b):
    return lax.dot_general(a, b, (((0,), (0,)), ((), ())), preferred_element_type=F32)


def _sigmoid(x):
    return jax.nn.sigmoid(x)


def _rowsum(x):
    return jnp.sum(x, axis=0, keepdims=True)


def _rms_fwd(x, g):
    r = lax.rsqrt(jnp.mean(x * x, axis=-1, keepdims=True) + EPS)
    return x * r * g, r


def _rms_bwd(dn, x, r, g):
    xr = x * r
    gy = dn * g
    dx = r * (gy - xr * jnp.mean(gy * xr, axis=-1, keepdims=True))
    return dx, _rowsum(dn * xr)


def _gelu(y):
    t = jnp.tanh(GELU_C * (y + GELU_K * y * y * y))
    return 0.5 * y * (1.0 + t), t


def _gelu_grad(y, t):
    return 0.5 * (1.0 + t) + 0.5 * y * (1.0 - t * t) * GELU_C * (1.0 + 3.0 * GELU_K * y * y)


def _softplus(x):
    return jnp.maximum(x, 0.0) + jnp.log(1.0 + jnp.exp(-jnp.abs(x)))


def _one_minus_exp(z):
    series = -z * (1.0 + 0.5 * z * (1.0 + z * (1.0 / 3.0) * (1.0 + 0.25 * z)))
    return jnp.where(z > -0.05, series, 1.0 - jnp.exp(z))


def _tiles(t_real):
    if t_real > 2048:
        tm = 384
        tp = -(-t_real // tm) * tm
        return tp, tm, tm // 3, tp // 2
    tm = 128
    tp = -(-t_real // tm) * tm
    return tp, tm, tm // 2, tm


def _load_weights(copies, sems):
    cps = [pltpu.make_async_copy(s, d, sems.at[k]) for k, (s, d) in enumerate(copies)]
    for cp in cps:
        cp.start()
    for cp in cps:
        cp.wait()


def _position():
    x, y, c = lax.axis_index("x"), lax.axis_index("y"), lax.axis_index("c")
    chips = [(1 - x, y), (x, 1 - y), (1 - x, 1 - y)]
    return x, y, c, chips


def _slot(p):
    return 4 * p[0] + 2 * p[1] + p[2]


class _Gather:
    def __init__(self, shards):
        self.shards = list(shards)
        self.n = len(self.shards)

    def inputs(self):
        return self.shards

    def out_shape(self):
        return [jax.ShapeDtypeStruct((N_DEV,) + s.shape, s.dtype) for s in self.shards]

    def scratch(self):
        return [pltpu.SemaphoreType.DMA((7 * self.n,)), pltpu.SemaphoreType.DMA((7 * self.n,)),
                pltpu.SemaphoreType.DMA((self.n,))]

    def _plan(self, ins, outs, sems):
        send_sems, recv_sems, local_sems = sems
        x, y, c, chips = _position()
        me, sibling = (x, y, c), (x, y, 1 - c)

        def copy(a, k, block, to, src=None):
            dst = outs[a].at[_slot(block)]
            return pltpu.make_async_remote_copy(
                src_ref=dst if src is None else src, dst_ref=dst,
                send_sem=send_sems.at[7 * a + k], recv_sem=recv_sems.at[7 * a + k],
                device_id=to, device_id_type=MESH)

        mine = [pltpu.make_async_copy(ins[a], outs[a].at[_slot(me)], local_sems.at[a]) for a in range(self.n)]
        first = []
        for a in range(self.n):
            first.append(copy(a, 0, me, sibling, src=ins[a]))
            first += [copy(a, 1 + j, me, (*chip, c), src=ins[a]) for j, chip in enumerate(chips)]
        return copy, mine, first, me, sibling, c, chips

    def start(self, ins, outs, sems):
        _, mine, first, *_ = self._plan(ins, outs, sems)
        for cp in mine + first:
            cp.start()

    def finish(self, ins, outs, sems):
        copy, mine, first, me, sibling, c, chips = self._plan(ins, outs, sems)
        passed = []
        for j, chip in enumerate(chips):
            for a in range(self.n):
                copy(a, 1 + j, (*chip, c), me).wait_recv()
                fwd = copy(a, 4 + j, (*chip, c), sibling)
                fwd.start()
                passed.append(fwd)
        for a in range(self.n):
            copy(a, 0, sibling, me).wait_recv()
            for j, chip in enumerate(chips):
                copy(a, 4 + j, (*chip, 1 - c), me).wait_recv()
        for cp in first + passed:
            cp.wait_send()
        for cp in mine:
            cp.wait()


class _Scatter:
    def __init__(self, grads):
        self.grads = list(grads)
        self.n = len(self.grads)

    def inputs(self):
        return self.grads

    def out_shape(self):
        return [jax.ShapeDtypeStruct((N_DEV - 1,) + g.shape[1:], g.dtype) for g in self.grads]

    def scratch(self):
        return [pltpu.SemaphoreType.DMA((7 * self.n,)), pltpu.SemaphoreType.DMA((7 * self.n,))]

    def _plan(self, ins, outs, sems):
        send_sems, recv_sems = sems
        x, y, c, _ = _position()
        cps = []
        for a in range(self.n):
            for k in range(1, N_DEV):
                peer = (x ^ (k >> 2), y ^ ((k >> 1) & 1), c ^ (k & 1))
                cps.append(pltpu.make_async_remote_copy(
                    src_ref=ins[a].at[_slot(peer)], dst_ref=outs[a].at[k - 1],
                    send_sem=send_sems.at[7 * a + k - 1], recv_sem=recv_sems.at[7 * a + k - 1],
                    device_id=peer, device_id_type=MESH))
        return cps

    def start(self, ins, outs, sems):
        for cp in self._plan(ins, outs, sems):
            cp.start()

    def finish(self, ins, outs, sems):
        for cp in self._plan(ins, outs, sems):
            cp.wait()


def _hosted(inner, n_in, n_out, comm, grid):
    if comm is None:
        return inner
    nc_in, nc_out, ns = len(comm.inputs()), len(comm.out_shape()), len(comm.scratch())

    def body(*refs):
        o0 = n_in + nc_in
        s0 = o0 + n_out + nc_out
        main = refs[:n_in] + refs[o0:o0 + n_out] + refs[s0:len(refs) - ns]
        c_in, c_out, c_sems = refs[n_in:o0], refs[o0 + n_out:s0], refs[len(refs) - ns:]
        ids = [pl.program_id(ax) for ax in range(len(grid))]
        first = functools.reduce(jnp.logical_and, [i == 0 for i in ids])
        last = functools.reduce(jnp.logical_and, [i == g - 1 for i, g in zip(ids, grid)])

        @pl.when(first)
        def _():
            comm.start(c_in, c_out, c_sems)

        inner(*main)

        @pl.when(last)
        def _():
            comm.finish(c_in, c_out, c_sems)

    return body


def _call(inner, name, grid, in_specs, out_specs, out_shape, scratch, args, comm=None):
    n_in, n_out = len(args), len(out_shape)
    body = _hosted(inner, n_in, n_out, comm, grid)
    if comm is not None:
        in_specs = list(in_specs) + [_any()] * len(comm.inputs())
        args = list(args) + comm.inputs()
        out_specs = list(out_specs) + [_any()] * len(comm.out_shape())
        out_shape = list(out_shape) + comm.out_shape()
        scratch = list(scratch) + comm.scratch()
    outs = pl.pallas_call(
        body, name=name, grid=grid, in_specs=list(in_specs), out_specs=list(out_specs), out_shape=list(out_shape),
        scratch_shapes=list(scratch), compiler_params=_params(len(grid)))(*args)
    return list(outs[:n_out]), list(outs[n_out:])


def _all_gather(shards):
    comm = _Gather(shards)
    n = comm.n

    def body(*refs):
        comm.start(refs[:n], refs[n:2 * n], refs[2 * n:])
        comm.finish(refs[:n], refs[n:2 * n], refs[2 * n:])

    return pl.pallas_call(
        body, name="weights_all_gather", out_shape=comm.out_shape(),
        in_specs=[_any()] * n, out_specs=[_any()] * n, scratch_shapes=comm.scratch(),
    )(*shards)


def _pair_exchange(grads):
    n = len(grads)

    def body(*refs):
        ins, outs = refs[:n], refs[n:2 * n]
        send_sems, recv_sems = refs[2 * n:]
        x, y, c, _ = _position()
        cps = [pltpu.make_async_remote_copy(
            src_ref=ins[a].at[:, 1 - c], dst_ref=outs[a],
            send_sem=send_sems.at[a], recv_sem=recv_sems.at[a],
            device_id=(x, y, 1 - c), device_id_type=MESH) for a in range(n)]
        for cp in cps:
            cp.start()
        for cp in cps:
            cp.wait()

    return pl.pallas_call(
        body, name="grads_pair_exchange",
        out_shape=[jax.ShapeDtypeStruct((4,) + g.shape[2:], g.dtype) for g in grads],
        in_specs=[_any()] * n, out_specs=[_any()] * n,
        scratch_shapes=[pltpu.SemaphoreType.DMA((n,)), pltpu.SemaphoreType.DMA((n,))],
    )(*grads)


def _chip_exchange(combs):
    n = len(combs)

    def body(*refs):
        ins, outs = refs[:n], refs[n:2 * n]
        send_sems, recv_sems = refs[2 * n:]
        x, y, c, chips = _position()
        cps = []
        for a in range(n):
            for j, (cx, cy) in enumerate(chips):
                cps.append(pltpu.make_async_remote_copy(
                    src_ref=ins[a].at[2 * cx + cy], dst_ref=outs[a].at[j],
                    send_sem=send_sems.at[3 * a + j], recv_sem=recv_sems.at[3 * a + j],
                    device_id=(cx, cy, c), device_id_type=MESH))
        for cp in cps:
            cp.start()
        for cp in cps:
            cp.wait()

    return pl.pallas_call(
        body, name="grads_chip_exchange",
        out_shape=[jax.ShapeDtypeStruct((3,) + g.shape[1:], g.dtype) for g in combs],
        in_specs=[_any()] * n, out_specs=[_any()] * n,
        scratch_shapes=[pltpu.SemaphoreType.DMA((3 * n,)), pltpu.SemaphoreType.DMA((3 * n,))],
    )(*combs)


def _pair_add(grad, recv, core):
    blk = grad.shape[2:]
    zeros = (0,) * len(blk)

    def body(core_ref, g_ref, r_ref, o_ref):
        del core_ref
        o_ref[...] = (g_ref[...].astype(F32) + r_ref[...].astype(F32)).astype(BF16)

    return pl.pallas_call(
        body, name="grads_pair_add",
        out_shape=jax.ShapeDtypeStruct((4,) + blk, BF16),
        grid_spec=pltpu.PrefetchScalarGridSpec(
            num_scalar_prefetch=1, grid=(4,),
            in_specs=[pl.BlockSpec((None, None) + blk, lambda i, cr: (i, cr[0]) + zeros),
                      pl.BlockSpec((None,) + blk, lambda i, cr: (i,) + zeros)],
            out_specs=pl.BlockSpec((None,) + blk, lambda i, cr: (i,) + zeros)),
        compiler_params=_params(1),
    )(core, grad, recv)


def _adamw(w, g, m, v):
    m2 = ADAM_B1 * m + (1.0 - ADAM_B1) * g
    v2 = ADAM_B2 * v + (1.0 - ADAM_B2) * (g * g)
    m_hat = m2 / (1.0 - ADAM_B1 ** ADAM_STEP)
    v_hat = v2 / (1.0 - ADAM_B2 ** ADAM_STEP)
    delta = -ADAM_LR * (m_hat / (jnp.sqrt(v_hat) + ADAM_EPS) + ADAM_WD * w)
    return delta, m2, v2


def _final_adamw(own, recv, idx, parts):
    blk = own.shape[1:]
    n_recv = recv.shape[0]
    n_parts = len(parts)
    per = blk[0] // n_parts if n_parts > 1 else None
    rows = blk[-2]
    n_chunks = 1 if n_parts > 1 else (4 if rows % 64 == 0 and rows >= 512 else (2 if rows % 32 == 0 else 1))
    cblk = blk[:-2] + (rows // n_chunks, blk[-1])
    lead = (0,) * (len(blk) - 2)

    def body(idx_ref, c_ref, r_ref, *refs):
        del idx_ref
        ins, outs = refs[:3 * n_parts], refs[3 * n_parts:]
        g = c_ref[...].astype(F32)
        for k in range(n_recv):
            g = g + r_ref[k].astype(F32)
        for p in range(n_parts):
            w_ref, m_ref, v_ref = ins[3 * p:3 * p + 3]
            if n_parts == 1:
                gp = g
            elif per == 1:
                gp = g[p]
            else:
                gp = g[p * per:(p + 1) * per]
            delta, m2, v2 = _adamw(w_ref[0], gp, m_ref[0], v_ref[0])
            o = outs[4 * p:4 * p + 4]
            o[0][0] = gp
            o[1][0] = delta
            o[2][0] = m2
            o[3][0] = v2

    flat = [a for wmv in parts for a in wmv]

    def part_spec(a):
        shape = a.shape[:-2] + (a.shape[-2] // n_chunks, a.shape[-1])
        return pl.BlockSpec(shape, lambda i, cr, nd=a.ndim: (0,) * (nd - 2) + (i, 0))

    outs = pl.pallas_call(
        body, name="grads_sum_adamw",
        out_shape=[jax.ShapeDtypeStruct(wmv[0].shape, F32) for wmv in parts for _ in range(4)],
        grid_spec=pltpu.PrefetchScalarGridSpec(
            num_scalar_prefetch=1, grid=(n_chunks,),
            in_specs=[pl.BlockSpec((None,) + cblk, lambda i, cr: (cr[0],) + lead + (i, 0)),
                      pl.BlockSpec((n_recv,) + cblk, lambda i, cr: (0,) + lead + (i, 0))]
                     + [part_spec(a) for a in flat],
            out_specs=[part_spec(wmv[0]) for wmv in parts for _ in range(4)]),
        compiler_params=_params(1),
    )(idx, own, recv, *flat)
    return [tuple(outs[4 * p:4 * p + 4]) for p in range(n_parts)]


def _small_allreduce(partial, me_index):
    rows, d = partial.shape

    def body(me_ref, p_ref, out_ref, buf, send_sems, recv_sems):
        x, y, c, _ = _position()
        me = me_ref[0]
        buf[me] = p_ref[...]
        cps = []
        for k in range(1, N_DEV):
            peer = (x ^ (k >> 2), y ^ ((k >> 1) & 1), c ^ (k & 1))
            cps.append(pltpu.make_async_remote_copy(
                src_ref=p_ref, dst_ref=buf.at[me],
                send_sem=send_sems.at[k - 1], recv_sem=recv_sems.at[k - 1],
                device_id=peer, device_id_type=MESH))
        for cp in cps:
            cp.start()
        for cp in cps:
            cp.wait()
        total = buf[0]
        for j in range(1, N_DEV):
            total = total + buf[j]
        out_ref[...] = total

    vm = pl.BlockSpec(memory_space=pltpu.VMEM)
    return pl.pallas_call(
        body, name="small_allreduce",
        out_shape=jax.ShapeDtypeStruct((rows, d), F32),
        in_specs=[pl.BlockSpec(memory_space=pltpu.SMEM), vm], out_specs=vm,
        scratch_shapes=[pltpu.VMEM((N_DEV, rows, d), F32),
                        pltpu.SemaphoreType.DMA((N_DEV - 1,)), pltpu.SemaphoreType.DMA((N_DEV - 1,))],
        compiler_params=pltpu.CompilerParams(vmem_limit_bytes=V7X_VMEM_LIMIT),
    )(me_index, partial)


def _small_adamw(total, layout, me_index):
    rows, d = total.shape
    n = len(layout)
    cw = d // N_DEV

    def body(me_ref, t_ref, *refs):
        ins, outs = refs[:3 * n], refs[3 * n:]
        me = me_ref[0]
        for e, (kind, r0, nr, _, _, _) in enumerate(layout):
            w_ref, m_ref, v_ref = ins[3 * e:3 * e + 3]
            o = outs[4 * e:4 * e + 4]
            if kind == "rep":
                g = t_ref[r0:r0 + nr, :]
                delta, m2, v2 = _adamw(w_ref[...], g, m_ref[...], v_ref[...])
                for ref, val in zip(o, (g, delta, m2, v2)):
                    ref[...] = val
            elif kind == "wide":
                for q in range(nr):
                    sl = slice(q * d, (q + 1) * d)
                    g = t_ref[r0 + q:r0 + q + 1, :]
                    delta, m2, v2 = _adamw(w_ref[:, sl], g, m_ref[:, sl], v_ref[:, sl])
                    for ref, val in zip(o, (g, delta, m2, v2)):
                        ref[:, sl] = val
            else:
                for j in range(N_DEV):
                    @pl.when(me == j)
                    def _(j=j, o=o, w_ref=w_ref, m_ref=m_ref, v_ref=v_ref, r0=r0, nr=nr):
                        g = t_ref[r0:r0 + nr, j * cw:(j + 1) * cw]
                        delta, m2, v2 = _adamw(w_ref[...], g, m_ref[...], v_ref[...])
                        for ref, val in zip(o, (g, delta, m2, v2)):
                            ref[...] = val

    flat = [a for ent in layout for a in ent[3:]]
    vm = pl.BlockSpec(memory_space=pltpu.VMEM)
    outs = pl.pallas_call(
        body, name="small_adamw",
        out_shape=[jax.ShapeDtypeStruct(ent[3].shape, F32) for ent in layout for _ in range(4)],
        in_specs=[pl.BlockSpec(memory_space=pltpu.SMEM), vm] + [vm] * len(flat),
        out_specs=[vm] * (4 * n),
        compiler_params=pltpu.CompilerParams(vmem_limit_bytes=V7X_VMEM_LIMIT),
    )(me_index, total, *flat)
    return [tuple(outs[4 * e:4 * e + 4]) for e in range(n)]


def _ffn_fwd(h, g, wgu, wd, tm, loss=None, comm=None):
    tp, d = h.shape
    f = wd.shape[0]
    fc = f // FFN_CHUNKS
    nt = tp // tm
    with_loss = loss is not None
    if with_loss:
        tgt, gf, n_meta, t_real = loss

    def body(*refs):
        if with_loss:
            (h_ref, g_ref, wgu_hbm, wd_hbm, tgt_ref, gf_ref, out_ref, gu_ref, n_ref, loss_ref, dgf_ref,
             wgu_v, wd_v, sems) = refs
        else:
            h_ref, g_ref, wgu_hbm, wd_hbm, out_ref, gu_ref, n_ref, wgu_v, wd_v, sems = refs
        i = pl.program_id(0)

        @pl.when(i == 0)
        def _():
            _load_weights([(wgu_hbm, wgu_v), (wd_hbm, wd_v)], sems)
            if with_loss:
                loss_ref[...] = jnp.zeros_like(loss_ref)
                dgf_ref[...] = jnp.zeros_like(dgf_ref)

        x = h_ref[...]
        n, _ = _rms_fwd(x, g_ref[...])
        nb = n.astype(BF16)
        n_ref[...] = nb
        acc = jnp.zeros((tm, d), F32)
        for j in range(FFN_CHUNKS):
            cols = slice(j * fc, (j + 1) * fc)
            gate = _nt(nb, wgu_v[pl.ds(j * fc, fc), :])
            up = _nt(nb, wgu_v[pl.ds(f + j * fc, fc), :])
            gu_ref[0, :, cols] = gate.astype(BF16)
            gu_ref[1, :, cols] = up.astype(BF16)
            act = (gate * _sigmoid(gate) * up).astype(BF16)
            acc = acc + _nn(act, wd_v[pl.ds(j * fc, fc), :])
        hn = x + FFN_RES * acc
        if not with_loss:
            out_ref[...] = hn
        else:
            gfv = gf_ref[...]
            r = lax.rsqrt(jnp.mean(hn * hn, axis=-1, keepdims=True) + EPS)
            xr = hn * r
            rows = i * tm + lax.broadcasted_iota(jnp.int32, (tm, 1), 0)
            mask = jnp.logical_and(rows >= n_meta, rows < t_real)
            diff = jnp.where(mask, xr * gfv - tgt_ref[...], 0.0)
            loss_ref[...] += jnp.zeros_like(loss_ref) + 0.5 * jnp.sum(diff * diff) / d
            dy = diff / d
            gy = dy * gfv
            out_ref[...] = r * (gy - xr * jnp.mean(gy * xr, axis=-1, keepdims=True))
            dgf_ref[...] += _rowsum(dy * xr)

    row = pl.BlockSpec((tm, d), lambda i: (i, 0))
    vec = pl.BlockSpec((1, d), lambda i: (0, 0))
    in_specs = [row, vec, _any(), _any()]
    out_shape = [jax.ShapeDtypeStruct((tp, d), F32), jax.ShapeDtypeStruct((2, tp, f), BF16),
                 jax.ShapeDtypeStruct((tp, d), BF16)]
    out_specs = [row, pl.BlockSpec((2, tm, f), lambda i: (0, i, 0)), row]
    args = [h, g, wgu, wd]
    if with_loss:
        in_specs += [row, vec]
        out_shape += [jax.ShapeDtypeStruct((1, d), F32), jax.ShapeDtypeStruct((1, d), F32)]
        out_specs += [vec, vec]
        args += [tgt, gf]
    return _call(body, "ffn_fwd_loss" if with_loss else "ffn_fwd", (nt,), in_specs, out_specs, out_shape,
                 [pltpu.VMEM((2 * f, d), BF16), pltpu.VMEM((f, d), BF16), pltpu.SemaphoreType.DMA((2,))],
                 args, comm)


def _ffn_bwd(dh, h, gu, g, wgu, wd, tm, comm=None):
    tp, d = h.shape
    f = wd.shape[0]
    fc = f // FFN_CHUNKS
    nt = tp // tm

    def body(dh_ref, h_ref, gu_ref, g_ref, wgu_hbm, wd_hbm,
             dhin_ref, dgu_ref, act_ref, df_ref, dg_ref, wgu_v, wd_v, dn_v, sems):
        i, j = pl.program_id(0), pl.program_id(1)

        @pl.when(jnp.logical_and(i == 0, j == 0))
        def _():
            _load_weights([(wgu_hbm, wgu_v), (wd_hbm, wd_v)], sems)
            dg_ref[...] = jnp.zeros_like(dg_ref)

        dfb = (FFN_RES * dh_ref[...]).astype(BF16)

        @pl.when(j == 0)
        def _():
            df_ref[...] = dfb
            dn_v[...] = jnp.zeros_like(dn_v)

        lo = pl.multiple_of(j * fc, 16)
        dact = _nt(dfb, wd_v[pl.ds(lo, fc), :])
        gate = gu_ref[0].astype(F32)
        up = gu_ref[1].astype(F32)
        sg = _sigmoid(gate)
        silu = gate * sg
        act_ref[...] = (silu * up).astype(BF16)
        dgate = (dact * up * (sg * (1.0 + gate * (1.0 - sg)))).astype(BF16)
        dup = (dact * silu).astype(BF16)
        dgu_ref[0] = dgate
        dgu_ref[1] = dup
        dn_v[...] += _nn(dgate, wgu_v[pl.ds(lo, fc), :]) + _nn(dup, wgu_v[pl.ds(pl.multiple_of(f + j * fc, 16), fc), :])

        @pl.when(j == FFN_CHUNKS - 1)
        def _():
            x = h_ref[...]
            r = lax.rsqrt(jnp.mean(x * x, axis=-1, keepdims=True) + EPS)
            dx, dgp = _rms_bwd(dn_v[...], x, r, g_ref[...])
            dhin_ref[...] = dh_ref[...] + dx
            dg_ref[...] += dgp

    row = pl.BlockSpec((tm, d), lambda i, j: (i, 0))
    vec = pl.BlockSpec((1, d), lambda i, j: (0, 0))
    hid2 = pl.BlockSpec((2, tm, fc), lambda i, j: (0, i, j))
    return _call(
        body, "ffn_bwd", (nt, FFN_CHUNKS),
        [row, row, hid2, vec, _any(), _any()],
        [row, hid2, pl.BlockSpec((tm, fc), lambda i, j: (i, j)), row, vec],
        [jax.ShapeDtypeStruct((tp, d), F32), jax.ShapeDtypeStruct((2, tp, f), BF16),
         jax.ShapeDtypeStruct((tp, f), BF16), jax.ShapeDtypeStruct((tp, d), BF16),
         jax.ShapeDtypeStruct((1, d), F32)],
        [pltpu.VMEM((2 * f, d), BF16), pltpu.VMEM((f, d), BF16), pltpu.VMEM((tm, d), F32),
         pltpu.SemaphoreType.DMA((2,))],
        [dh, h, gu, g, wgu, wd], comm)


def _piece_segments(q, d, nb_cols):
    segs = []
    for j in range(N_DEV):
        lo, hi = max(q * d, j * nb_cols), min((q + 1) * d, (j + 1) * nb_cols)
        if lo < hi:
            segs.append((j, lo - q * d, hi - q * d, lo - j * nb_cols, hi - j * nb_cols))
    return segs


def _w3_copies(w3_hbm, rows, w3_v):
    return [(w3_hbm.at[k, pl.ds(q * rows, rows)], w3_v.at[q, pl.ds(k * rows, rows)])
            for q in range(3) for k in range(N_DEV)]


def _gates(xrb, wg_ref, ba, bx, lam, hd):
    pre_r, pre_i = [], []
    for hh in range(N_HEADS):
        xh = xrb[:, hh * hd:(hh + 1) * hd]
        pre_r.append(_nn(xh, wg_ref[0, hh]))
        pre_i.append(_nn(xh, wg_ref[1, hh]))
    r = _sigmoid(jnp.concatenate(pre_r, axis=1) + ba)
    ig = _sigmoid(jnp.concatenate(pre_i, axis=1) + bx)
    sp = _softplus(-lam)
    log_a = -RG_LRU_C * r * sp
    a = jnp.exp(log_a)
    s = jnp.sqrt(_one_minus_exp(2.0 * log_a))
    return r, ig, sp, a, s


def _scan_fwd(a, u, h_prev):
    tm = a.shape[0]
    rows = lax.broadcasted_iota(jnp.int32, a.shape, 0)
    d = 1
    while d < tm:
        keep = rows >= d
        u = jnp.where(keep, a * pltpu.roll(u, d, 0) + u, u)
        a = jnp.where(keep, a * pltpu.roll(a, d, 0), a)
        d *= 2
    return u + a * h_prev


def _scan_bwd(b, v, g_next):
    tm = b.shape[0]
    rows = lax.broadcasted_iota(jnp.int32, b.shape, 0)
    d = 1
    while d < tm:
        keep = rows < tm - d
        v = jnp.where(keep, v + b * pltpu.roll(v, tm - d, 0), v)
        b = jnp.where(keep, b * pltpu.roll(b, tm - d, 0), b)
        d *= 2
    return v + b * g_next


def _shifted_copies(ext_ref, es_ref, n_rows):
    for s in range(1, SUBLANES):
        es_ref[s, pl.ds(0, n_rows), :] = ext_ref[pl.ds(s, n_rows), :]


def _tap(ext_ref, es_ref, off, tm):
    q, s = divmod(off, SUBLANES)
    if s == 0:
        return ext_ref[pl.ds(SUBLANES * q, tm), :]
    return es_ref[s, pl.ds(SUBLANES * q, tm), :]


def _mixer_fwd(h, g, b_in, win_all, cw4, cb4, wg, ba, bx, lam, cw31, cb31, lng, lnb, bcp, w3_all, tm, comm=None):
    tp, d = h.shape
    nb_cols = win_all.shape[-1]
    n_in = N_DEV * nb_cols
    hd = wg.shape[-1]
    k4, k31 = cw4.shape[0], cw31.shape[0]
    w3_rows = d // N_DEV

    def body(h_ref, g_ref, b_ref, win_hbm, cw4_ref, cb4_ref, wg_ref, ba_ref, bx_ref, lam_ref, cw31_ref, cb31_ref,
             lng_ref, lnb_ref, bcp_ref, w3_hbm,
             h2_ref, p_ref, n_ref, xr_ref, hs_ref, v1_ref, ya_ref, yb_ref,
             win_v, w3_v, ext4, ext31, es31, hcar, sems):
        @pl.when(pl.program_id(0) == 0)
        def _():
            _load_weights([(win_hbm, win_v)] + _w3_copies(w3_hbm, w3_rows, w3_v), sems)
            ext4[pl.ds(0, CONV4_HALO), :] = jnp.zeros((CONV4_HALO, d), F32)
            ext31[pl.ds(0, CONV31_HALO), :] = jnp.zeros((CONV31_HALO, d), F32)
            hcar[...] = jnp.zeros_like(hcar)

        n, _ = _rms_fwd(h_ref[...], g_ref[...])
        nb = n.astype(BF16)
        n_ref[...] = nb

        def piece(q):
            parts = [_nn(nb, win_v[j, :, bl:bh]) for j, _, _, bl, bh in _piece_segments(q, d, nb_cols)]
            pq = (jnp.concatenate(parts, axis=1) + b_ref[:, q * d:(q + 1) * d]).astype(BF16)
            p_ref[:, q * d:(q + 1) * d] = pq
            return pq.astype(F32)

        x_rnn, y_rnn, glu_v, glu_g, gate_a, gate_b = [piece(q) for q in range(6)]

        ext4[pl.ds(CONV4_HALO, tm), :] = x_rnn
        xr = cb4_ref[...] + jnp.zeros((tm, d), F32)
        for k in range(k4):
            xr = xr + cw4_ref[k:k + 1, :] * ext4[pl.ds(CONV4_HALO - (k4 - 1) + k, tm), :]
        ext4[pl.ds(0, CONV4_HALO), :] = ext4[pl.ds(tm, CONV4_HALO), :]
        xrb = xr.astype(BF16)
        xr_ref[...] = xrb
        xr = xrb.astype(F32)
        _, ig, _, a, s = _gates(xrb, wg_ref, ba_ref[...], bx_ref[...], lam_ref[...], hd)
        hseq = _scan_fwd(a, s * (ig * xr), hcar[0:1, :])
        hcar[0:1, :] = hseq[tm - 1:tm, :]
        hs_ref[...] = hseq.astype(BF16)
        gl, _ = _gelu(y_rnn)
        ya = _nn((hseq * gl).astype(BF16), w3_v[0])
        ya_ref[...] = ya.astype(BF16)

        ext31[pl.ds(CONV31_HALO, tm), :] = glu_v * _sigmoid(glu_g)
        _shifted_copies(ext31, es31, tm + CONV31_HALO - SUBLANES)
        v1 = cb31_ref[...] + jnp.zeros((tm, d), F32)
        for k in range(k31):
            v1 = v1 + cw31_ref[k:k + 1, :] * _tap(ext31, es31, CONV31_HALO - (k31 - 1) + k, tm)
        ext31[pl.ds(0, CONV31_HALO), :] = ext31[pl.ds(tm, CONV31_HALO), :]
        v1b = v1.astype(BF16)
        v1_ref[...] = v1b
        v1 = v1b.astype(F32)
        xc = v1 - jnp.mean(v1, axis=-1, keepdims=True)
        rstd = lax.rsqrt(jnp.mean(xc * xc, axis=-1, keepdims=True) + EPS)
        v2 = xc * rstd * lng_ref[...] + lnb_ref[...]
        yb = _nn((v2 * _sigmoid(v2)).astype(BF16), w3_v[1]) + bcp_ref[...]
        yb_ref[...] = yb.astype(BF16)

        merged = _sigmoid(gate_a) * ya + _sigmoid(gate_b) * yb
        h2_ref[...] = h_ref[...] + _nn(merged.astype(BF16), w3_v[2])

    row = pl.BlockSpec((tm, d), lambda i: (i, 0))
    wide = pl.BlockSpec((tm, n_in), lambda i: (i, 0))
    full = lambda a: pl.BlockSpec(a.shape, lambda i, nd=a.ndim: (0,) * nd)
    smalls = [cw4, cb4, wg, ba, bx, lam, cw31, cb31, lng, lnb, bcp]
    return _call(
        body, "mixer_fwd", (tp // tm,),
        [row, full(g), full(b_in), _any()] + [full(a) for a in smalls] + [_any()],
        [row, wide] + [row] * 6,
        [jax.ShapeDtypeStruct((tp, d), F32), jax.ShapeDtypeStruct((tp, n_in), BF16)]
        + [jax.ShapeDtypeStruct((tp, d), BF16)] * 6,
        [pltpu.VMEM(win_all.shape, BF16),
         pltpu.VMEM((3, d, d), BF16),
         pltpu.VMEM((tm + CONV4_HALO, d), F32),
         pltpu.VMEM((tm + CONV31_HALO, d), F32),
         pltpu.VMEM((SUBLANES, tm + CONV31_HALO, d), F32),
         pltpu.VMEM((SUBLANES, d), F32),
         pltpu.SemaphoreType.DMA((1 + 3 * N_DEV,))],
        [h, g, b_in, win_all, *smalls, w3_all], comm)


SG_BIN, SG_CW4, SG_CB4, SG_BA, SG_BX, SG_LAM, SG_CB31, SG_LNG, SG_LNB, SG_BCP, SG_MIX, SG_CW31 = 0, 6, 10, 11, 12, 13, 14, 15, 16, 17, 18, 19


def _mixer_bwd(dh2, h, g, proj, xr_s, hs_s, v1_s, ya_s, yb_s, win_all, cw4, wg, ba, bx, lam, cw31, lng, lnb, w3_all, tm,
               comm=None):
    tp, d = dh2.shape
    nb_cols = win_all.shape[-1]
    n_in = proj.shape[1]
    hd = wg.shape[-1]
    k4, k31 = cw4.shape[0], cw31.shape[0]
    nt = tp // tm
    w3_rows = d // N_DEV
    sg_rows = -(-(SG_CW31 + k31) // SUBLANES) * SUBLANES
    halo_rows = 16
    per = tm // halo_rows

    def body(dh_ref, h_ref, g_ref, p_ref, xr_ref, hs_ref, hh_ref, v1_ref, ya_ref, yb_ref, win_hbm,
             cw4_ref, wg_ref, ba_ref, bx_ref, lam_ref, cw31_ref, lng_ref, lnb_ref, w3_hbm,
             dh1_ref, dp_ref, x3_ref, y3_ref, yg_ref, sg_ref,
             win_v, w3_v, extd4, extd31, es31, gcar, sems):
        i = pl.program_id(0)
        tile = nt - 1 - i

        @pl.when(i == 0)
        def _():
            _load_weights([(win_hbm, win_v)] + _w3_copies(w3_hbm, w3_rows, w3_v), sems)
            extd4[pl.ds(tm, CONV4_HALO), :] = jnp.zeros((CONV4_HALO, d), F32)
            extd31[pl.ds(tm, CONV31_HALO), :] = jnp.zeros((CONV31_HALO, d), F32)
            gcar[...] = jnp.zeros_like(gcar)
            sg_ref[...] = jnp.zeros_like(sg_ref)

        def acc(row, val):
            sg_ref[row:row + 1, :] += _rowsum(val)

        rows = lax.broadcasted_iota(jnp.int32, (tm, d), 0)
        x_rnn = p_ref[:, 0:d].astype(F32)
        y_rnn = p_ref[:, d:2 * d].astype(F32)
        glu_v = p_ref[:, 2 * d:3 * d].astype(F32)
        glu_g = p_ref[:, 3 * d:4 * d].astype(F32)
        sga = _sigmoid(p_ref[:, 4 * d:5 * d].astype(F32))
        sgb = _sigmoid(p_ref[:, 5 * d:6 * d].astype(F32))
        ya = ya_ref[...].astype(F32)
        yb = yb_ref[...].astype(F32)

        dmob = dh_ref[...].astype(BF16)
        dmerged = _nt(dmob, w3_v[2])
        x3_ref[:, 0:d] = (sga * ya + sgb * yb).astype(BF16)
        y3_ref[:, 0:d] = dmob
        dya = sga * dmerged
        dyb = sgb * dmerged
        dn_parts = []

        def emit(q, val):
            vb = val.astype(BF16)
            dp_ref[:, q * d:(q + 1) * d] = vb
            acc(SG_BIN + q, val)
            for j, lo, hi, bl, bh in _piece_segments(q, d, nb_cols):
                term = _nt(vb[:, lo:hi], win_v[j, :, bl:bh])
                dn_parts[:] = [term if not dn_parts else dn_parts[0] + term]

        emit(4, dmerged * ya * sga * (1.0 - sga))
        emit(5, dmerged * yb * sgb * (1.0 - sgb))

        dyab = dya.astype(BF16)
        y3_ref[:, d:2 * d] = dyab
        dza = _nt(dyab, w3_v[0])
        hsv = hs_ref[...].astype(F32)
        gl, th = _gelu(y_rnn)
        x3_ref[:, d:2 * d] = (hsv * gl).astype(BF16)
        emit(1, dza * hsv * _gelu_grad(y_rnn, th))
        dhs = dza * gl
        xrb = xr_ref[...]
        xr = xrb.astype(F32)
        lam_v = lam_ref[...]
        r, ig, sp, a, s = _gates(xrb, wg_ref, ba_ref[...], bx_ref[...], lam_v, hd)
        b = jnp.where(rows == tm - 1, gcar[1:2, :], pltpu.roll(a, tm - 1, 0))
        big_g = _scan_bwd(b, dhs, gcar[0:1, :])
        gcar[0:1, :] = big_g[0:1, :]
        gcar[1:2, :] = a[0:1, :]
        h_before = jnp.where(tile > 0, hh_ref[halo_rows - 1:halo_rows, :].astype(F32), 0.0)
        h_prev = jnp.where(rows == 0, h_before, pltpu.roll(hsv, 1, 0))
        ds = big_g * ig * xr
        dla = big_g * h_prev * a - ds * (a * a) / jnp.maximum(s, 1e-20)
        acc(SG_LAM, dla * r * (RG_LRU_C * _sigmoid(-lam_v)))
        dpr = dla * (-RG_LRU_C * sp) * r * (1.0 - r)
        dpi = big_g * s * xr * ig * (1.0 - ig)
        acc(SG_BA, dpr)
        acc(SG_BX, dpi)
        dprb = dpr.astype(BF16)
        dpib = dpi.astype(BF16)
        yg_ref[:, 0:d] = dprb
        yg_ref[:, d:2 * d] = dpib
        back = []
        for hh in range(N_HEADS):
            sl = slice(hh * hd, (hh + 1) * hd)
            back.append(_nt(dprb[:, sl], wg_ref[0, hh]) + _nt(dpib[:, sl], wg_ref[1, hh]))
        dxr = big_g * s * ig + jnp.concatenate(back, axis=1)
        acc(SG_CB4, dxr)
        extd4[pl.ds(0, tm), :] = dxr
        dx_rnn = jnp.zeros((tm, d), F32)
        for k in range(k4):
            term = extd4[pl.ds(k4 - 1 - k, tm), :]
            dx_rnn = dx_rnn + cw4_ref[k:k + 1, :] * term
            acc(SG_CW4 + k, x_rnn * term)
        extd4[pl.ds(tm, CONV4_HALO), :] = extd4[pl.ds(0, CONV4_HALO), :]
        emit(0, dx_rnn)

        dybb = dyb.astype(BF16)
        y3_ref[:, 2 * d:3 * d] = dybb
        acc(SG_BCP, dyb)
        dv3 = _nt(dybb, w3_v[1])
        v1 = v1_ref[...].astype(F32)
        xc = v1 - jnp.mean(v1, axis=-1, keepdims=True)
        rstd = lax.rsqrt(jnp.mean(xc * xc, axis=-1, keepdims=True) + EPS)
        xhat = xc * rstd
        lng_v = lng_ref[...]
        v2 = xhat * lng_v + lnb_ref[...]
        s2 = _sigmoid(v2)
        x3_ref[:, 2 * d:3 * d] = (v2 * s2).astype(BF16)
        dv2 = dv3 * (s2 * (1.0 + v2 * (1.0 - s2)))
        acc(SG_LNG, dv2 * xhat)
        acc(SG_LNB, dv2)
        dxh = dv2 * lng_v
        dv1 = rstd * (dxh - jnp.mean(dxh, axis=-1, keepdims=True)
                      - xhat * jnp.mean(dxh * xhat, axis=-1, keepdims=True))
        acc(SG_CB31, dv1)
        extd31[pl.ds(0, tm), :] = dv1
        _shifted_copies(extd31, es31, tm + CONV31_HALO - SUBLANES)
        sgg = _sigmoid(glu_g)
        v0 = glu_v * sgg
        dv0 = jnp.zeros((tm, d), F32)
        for k in range(k31):
            term = _tap(extd31, es31, k31 - 1 - k, tm)
            dv0 = dv0 + cw31_ref[k:k + 1, :] * term
            acc(SG_CW31 + k, v0 * term)
        extd31[pl.ds(tm, CONV31_HALO), :] = extd31[pl.ds(0, CONV31_HALO), :]
        emit(2, dv0 * sgg)
        emit(3, dv0 * glu_v * sgg * (1.0 - sgg))

        dn = dn_parts[0]
        x = h_ref[...]
        rr = lax.rsqrt(jnp.mean(x * x, axis=-1, keepdims=True) + EPS)
        dx, dgp = _rms_bwd(dn, x, rr, g_ref[...])
        dh1_ref[...] = dh_ref[...] + dx
        sg_ref[SG_MIX:SG_MIX + 1, :] += dgp

    rev = lambda i: (nt - 1 - i, 0)
    row = pl.BlockSpec((tm, d), rev)
    wide = pl.BlockSpec((tm, n_in), rev)
    full = lambda a: pl.BlockSpec(a.shape, lambda i, nd=a.ndim: (0,) * nd)
    halo = pl.BlockSpec((halo_rows, d), lambda i: (jnp.maximum((nt - 1 - i) * per - 1, 0), 0))
    smalls = [cw4, wg, ba, bx, lam, cw31, lng, lnb]
    return _call(
        body, "mixer_bwd", (nt,),
        [row, row, full(g), wide, row, row, halo, row, row, row, _any()]
        + [full(a) for a in smalls] + [_any()],
        [row, wide, pl.BlockSpec((tm, 3 * d), rev), pl.BlockSpec((tm, 3 * d), rev),
         pl.BlockSpec((tm, 2 * d), rev), pl.BlockSpec((sg_rows, d), lambda i: (0, 0))],
        [jax.ShapeDtypeStruct((tp, d), F32), jax.ShapeDtypeStruct((tp, n_in), BF16),
         jax.ShapeDtypeStruct((tp, 3 * d), BF16), jax.ShapeDtypeStruct((tp, 3 * d), BF16),
         jax.ShapeDtypeStruct((tp, 2 * d), BF16), jax.ShapeDtypeStruct((sg_rows, d), F32)],
        [pltpu.VMEM(win_all.shape, BF16),
         pltpu.VMEM((3, d, d), BF16),
         pltpu.VMEM((tm + CONV4_HALO, d), F32),
         pltpu.VMEM((tm + CONV31_HALO, d), F32),
         pltpu.VMEM((SUBLANES, tm + CONV31_HALO, d), F32),
         pltpu.VMEM((SUBLANES, d), F32),
         pltpu.SemaphoreType.DMA((1 + 3 * N_DEV,))],
        [dh2, h, g, proj, xr_s, hs_s, hs_s, v1_s, ya_s, yb_s, win_all, *smalls, w3_all], comm)


def _tn_matmul(name, x, y, x_spec, y_spec, n_blocks, kb, nb, tm, tp, out_shape, out_spec, out_view, comm=None):
    nt = tp // tm

    def body(x_ref, y_ref, o_ref, acc):
        i = pl.program_id(1)

        @pl.when(i == 0)
        def _():
            acc[...] = jnp.zeros_like(acc)

        acc[...] += _tn(x_ref[...], y_ref[...])

        @pl.when(i == nt - 1)
        def _():
            o_ref[...] = acc[...].astype(BF16).reshape(out_view)

    outs, extra = _call(body, name, (n_blocks, nt), [x_spec, y_spec], [out_spec],
                        [jax.ShapeDtypeStruct(out_shape, BF16)], [pltpu.VMEM((kb, nb), F32)], [x, y], comm)
    return outs[0], extra


def kernel(x, meta_tokens, ffn1_norm, ffn1_w_gu, ffn1_w_down, mix_norm, w_in, b_in, rnn_conv_w, rnn_conv_b, rg_w_a, rg_b_a, rg_w_x, rg_b_x, rg_lambda, rnn_w_proj, conv_dw_w, conv_dw_b, conv_ln_g, conv_ln_b, conv_w_proj, conv_b_proj, w_out, ffn2_norm, ffn2_w_gu, ffn2_w_down, final_norm, loss_target, m_meta_tokens, m_ffn1_norm, m_ffn1_w_gu, m_ffn1_w_down, m_mix_norm, m_w_in, m_b_in, m_rnn_conv_w, m_rnn_conv_b, m_rg_w_a, m_rg_b_a, m_rg_w_x, m_rg_b_x, m_rg_lambda, m_rnn_w_proj, m_conv_dw_w, m_conv_dw_b, m_conv_ln_g, m_conv_ln_b, m_conv_w_proj, m_conv_b_proj, m_w_out, m_ffn2_norm, m_ffn2_w_gu, m_ffn2_w_down, m_final_norm, v_meta_tokens, v_ffn1_norm, v_ffn1_w_gu, v_ffn1_w_down, v_mix_norm, v_w_in, v_b_in, v_rnn_conv_w, v_rnn_conv_b, v_rg_w_a, v_rg_b_a, v_rg_w_x, v_rg_b_x, v_rg_lambda, v_rnn_w_proj, v_conv_dw_w, v_conv_dw_b, v_conv_ln_g, v_conv_ln_b, v_conv_w_proj, v_conv_b_proj, v_w_out, v_ffn2_norm, v_ffn2_w_gu, v_ffn2_w_down, v_final_norm):
    w = dict(locals())
    seq, d = x.shape[1], x.shape[2]
    n_meta = meta_tokens.shape[0]
    t_real = n_meta + seq
    tp, tm, tmx, tmt = _tiles(t_real)
    fb = ffn1_w_gu.shape[-1]
    wr = ffn1_w_down.shape[1]
    f = N_DEV * wr
    fc = f // FFN_CHUNKS
    nbc = w_in.shape[-1]
    n_in = N_DEV * nbc
    pr = rnn_w_proj.shape[1]
    hd = rg_w_a.shape[-1]
    gr = rg_w_a.shape[2]
    cw = meta_tokens.shape[1]
    k4, k31 = rnn_conv_w.shape[1], conv_dw_w.shape[1]
    assert n_in == 6 * d and 2 * wr == fb and N_HEADS * hd == d and pr * N_DEV == d

    xi, yi, ci = lax.axis_index("x"), lax.axis_index("y"), lax.axis_index("c")
    core = ci.astype(jnp.int32).reshape(1)
    chip = (2 * xi + yi).astype(jnp.int32).reshape(1)
    me_index = (4 * xi + 2 * yi + ci).astype(jnp.int32).reshape(1)

    for nm in ("ffn1_w_gu", "ffn2_w_gu"):
        for pre in ("", "m_", "v_"):
            w[pre + nm] = jnp.swapaxes(w[pre + nm], 1, 2)

    wgut1 = w["ffn1_w_gu"][0].astype(BF16)
    wgut2 = w["ffn2_w_gu"][0].astype(BF16)
    wd1 = ffn1_w_down[0].astype(BF16)
    wd2 = ffn2_w_down[0].astype(BF16)
    win_loc = w_in[0].astype(BF16)
    w3_loc = jnp.concatenate([rnn_w_proj[0], conv_w_proj[0], w_out[0]], axis=0).astype(BF16)
    wg_loc = jnp.stack([rg_w_a[0], rg_w_x[0]]).astype(BF16)
    n_small = n_meta + k4 + k31
    small_rows = -(-n_small // SUBLANES) * SUBLANES
    small_loc = jnp.concatenate([meta_tokens, rnn_conv_w[0], conv_dw_w[0],
                                 jnp.zeros((small_rows - n_small, cw), F32)], axis=0)
    wgut1_all, wd1_all, wg_all, small_all = _all_gather([wgut1, wd1, wg_loc, small_loc])
    wg = wg_all.transpose(1, 2, 0, 3, 4).reshape(2, N_HEADS, hd, hd)
    small_full = small_all.transpose(1, 0, 2).reshape(small_rows, d)
    meta_full = small_full[:n_meta]
    cw4 = small_full[n_meta:n_meta + k4]
    cw31 = small_full[n_meta + k4:n_meta + k4 + k31]

    pad = jnp.zeros((tp - t_real, d), F32)
    h0 = jnp.concatenate([meta_full, x[0], pad], axis=0)
    tgt = jnp.concatenate([jnp.zeros((n_meta, d), F32), loss_target[0], pad], axis=0)
    wgu1, wdn1 = wgut1_all.reshape(2 * f, d), wd1_all.reshape(f, d)
    (h1, gu1, n1), (win_all, w3_all) = _ffn_fwd(h0, ffn1_norm, wgu1, wdn1, tm, comm=_Gather([win_loc, w3_loc]))
    (h2, proj, n2, xr_s, hs_s, v1_s, ya_s, yb_s), (wgut2_all, wd2_all) = _mixer_fwd(
        h1, mix_norm, b_in, win_all, cw4, rnn_conv_b, wg, rg_b_a, rg_b_x, rg_lambda, cw31, conv_dw_b, conv_ln_g,
        conv_ln_b, conv_b_proj, w3_all, tmx, comm=_Gather([wgut2, wd2]))
    wgu2, wdn2 = wgut2_all.reshape(2 * f, d), wd2_all.reshape(f, d)
    (dh3, gu2, n3, loss_part, dgf), _ = _ffn_fwd(h2, ffn2_norm, wgu2, wdn2, tm,
                                                 loss=(tgt, final_norm.reshape(1, d), n_meta, t_real))

    def d_w_gu(tag, dgu, n_s, comm=None):
        g, extra = _tn_matmul(
            "d_w_gu" + tag, dgu, n_s,
            pl.BlockSpec((None, tmt, fc), lambda b, i: (b // FFN_CHUNKS, i, b % FFN_CHUNKS)),
            pl.BlockSpec((tmt, d), lambda b, i: (i, 0)),
            2 * FFN_CHUNKS, fc, d, tmt, tp, (2 * FFN_CHUNKS, fc, d),
            pl.BlockSpec((None, fc, d), lambda b, i: (b, 0, 0)), (fc, d), comm)
        return g.reshape(N_DEV, fb, d), extra

    def d_w_down(tag, act, df):
        g, _ = _tn_matmul(
            "d_w_down" + tag, act, df,
            pl.BlockSpec((tmt, fc), lambda b, i: (i, b)), pl.BlockSpec((tmt, d), lambda b, i: (i, 0)),
            FFN_CHUNKS, fc, d, tmt, tp, (FFN_CHUNKS, fc, d),
            pl.BlockSpec((None, fc, d), lambda b, i: (b, 0, 0)), (fc, d))
        return g.reshape(N_DEV, wr, d)

    (dh2, dgu2, act2, df2, dg_ffn2), _ = _ffn_bwd(dh3, h2, gu2, ffn2_norm, wgu2, wdn2, tm)
    g_wgu2, _ = d_w_gu("2", dgu2, n3)
    g_wd2 = d_w_down("2", act2, df2)
    (dh1, dproj, x3, y3, yg, sg), (r_wd2, r_wgu2) = _mixer_bwd(
        dh2, h1, mix_norm, proj, xr_s, hs_s, v1_s, ya_s, yb_s, win_all, cw4, wg, rg_b_a, rg_b_x, rg_lambda, cw31,
        conv_ln_g, conv_ln_b, w3_all, tmx, comm=_Scatter([g_wd2, g_wgu2]))
    g_w3, _ = _tn_matmul(
        "d_w_proj3", x3, y3,
        pl.BlockSpec((tmt, d), lambda b, i: (i, b)), pl.BlockSpec((tmt, d), lambda b, i: (i, b)),
        3, d, d, tmt, tp, (N_DEV, 3, pr, d), pl.BlockSpec((N_DEV, None, pr, d), lambda b, i: (0, b, 0, 0)),
        (N_DEV, pr, d))
    g_wg, _ = _tn_matmul(
        "d_w_gates", xr_s, yg,
        pl.BlockSpec((tmt, hd), lambda b, i: (i, b % N_HEADS)), pl.BlockSpec((tmt, hd), lambda b, i: (i, b)),
        2 * N_HEADS, hd, hd, tmt, tp, (N_DEV, 2 * N_HEADS, gr, hd),
        pl.BlockSpec((N_DEV, None, gr, hd), lambda b, i: (0, b, 0, 0)), (N_DEV, gr, hd))
    g_win, (r_w3, r_wg) = _tn_matmul(
        "d_w_in", n2, dproj,
        pl.BlockSpec((tmt, d), lambda b, i: (i, 0)), pl.BlockSpec((tmt, nbc), lambda b, i: (i, b)),
        N_DEV, d, nbc, tmt, tp, (N_DEV, d, nbc), pl.BlockSpec((None, d, nbc), lambda b, i: (b, 0, 0)), (d, nbc),
        comm=_Scatter([g_w3, g_wg]))
    dg_mix = sg[SG_MIX:SG_MIX + 1]
    (dh0, dgu1, act1, df1, dg_ffn1), (r_win,) = _ffn_bwd(dh1, h0, gu1, ffn1_norm, wgu1, wdn1, tm,
                                                         comm=_Scatter([g_win]))
    g_wd1 = d_w_down("1", act1, df1)
    g_wgu1, (r_wd1,) = d_w_gu("1", dgu1, n1, comm=_Scatter([g_wd1]))
    grad_x = dh0[n_meta:t_real][None]

    g_last = g_wgu1.reshape((4, 2) + g_wgu1.shape[1:])
    (from_sibling,) = _pair_exchange([g_last])
    comb_wgu1 = _pair_add(g_last, from_sibling, core)
    (r_wgu1,) = _chip_exchange([comb_wgu1])

    groups = [(g_wd1, r_wd1, me_index, ["ffn1_w_down"]), (comb_wgu1, r_wgu1, chip, ["ffn1_w_gu"]),
              (g_wd2, r_wd2, me_index, ["ffn2_w_down"]), (g_wgu2, r_wgu2, me_index, ["ffn2_w_gu"]),
              (g_win, r_win, me_index, ["w_in"]), (g_w3, r_w3, me_index, ["w_out", "rnn_w_proj", "conv_w_proj"]),
              (g_wg, r_wg, me_index, ["rg_w_a", "rg_w_x"])]
    res = {}
    for own, recv, idx, group in groups:
        outs = _final_adamw(own, recv, idx, [(w[nm], w["m_" + nm], w["v_" + nm]) for nm in group])
        for nm, o in zip(group, outs):
            res[nm] = o
    for nm in ("ffn1_w_gu", "ffn2_w_gu"):
        res[nm] = tuple(jnp.swapaxes(a, 1, 2) for a in res[nm])

    rep_rows = [("ffn1_norm", dg_ffn1), ("mix_norm", dg_mix), ("b_in", sg[SG_BIN:SG_BIN + 6]),
                ("rnn_conv_b", sg[SG_CB4:SG_CB4 + 1]), ("rg_b_a", sg[SG_BA:SG_BA + 1]),
                ("rg_b_x", sg[SG_BX:SG_BX + 1]), ("rg_lambda", sg[SG_LAM:SG_LAM + 1]),
                ("conv_dw_b", sg[SG_CB31:SG_CB31 + 1]), ("conv_ln_g", sg[SG_LNG:SG_LNG + 1]),
                ("conv_ln_b", sg[SG_LNB:SG_LNB + 1]), ("conv_b_proj", sg[SG_BCP:SG_BCP + 1]),
                ("ffn2_norm", dg_ffn2), ("final_norm", dgf)]
    col_rows = [("meta_tokens", dh0[:n_meta]), ("rnn_conv_w", sg[SG_CW4:SG_CW4 + k4]),
                ("conv_dw_w", sg[SG_CW31:SG_CW31 + k31])]
    layout, pieces, r0 = [], [], 0
    for nm, part in rep_rows:
        nr = part.shape[0]
        kind = "wide" if nm == "b_in" else "rep"
        as2d = lambda a: a.reshape(1, -1) if a.ndim == 1 else a
        layout.append((kind, r0, nr, as2d(w[nm]), as2d(w["m_" + nm]), as2d(w["v_" + nm])))
        pieces.append(part)
        r0 += nr
    for nm, part in col_rows:
        nr = part.shape[0]
        sq = lambda a: a.reshape(a.shape[-2], a.shape[-1])
        layout.append(("col", r0, nr, sq(w[nm]), sq(w["m_" + nm]), sq(w["v_" + nm])))
        pieces.append(part)
        r0 += nr
    total_rows = -(-(r0 + 1) // SUBLANES) * SUBLANES
    pieces.append(jnp.zeros((total_rows - 1 - r0, d), F32))
    pieces.append(loss_part)
    total = _small_allreduce(jnp.concatenate(pieces, axis=0), me_index)
    small_out = _small_adamw(total, layout, me_index)
    for (nm, _), o in zip(rep_rows + col_rows, small_out):
        res[nm] = tuple(a.reshape(w[nm].shape) for a in o)

    order = ["meta_tokens", "ffn1_norm", "ffn1_w_gu", "ffn1_w_down", "mix_norm", "w_in", "b_in", "rnn_conv_w",
             "rnn_conv_b", "rg_w_a", "rg_b_a", "rg_w_x", "rg_b_x", "rg_lambda", "rnn_w_proj", "conv_dw_w",
             "conv_dw_b", "conv_ln_g", "conv_ln_b", "conv_w_proj", "conv_b_proj", "w_out", "ffn2_norm",
             "ffn2_w_gu", "ffn2_w_down", "final_norm"]
    return (total[total_rows - 1, 0], grad_x, *[res[nm][0] for nm in order], *[res[nm][1] for nm in order],
            *[res[nm][2] for nm in order], *[res[nm][3] for nm in order])
```

```python
import functools
import math

import jax
import jax.numpy as jnp
from jax import lax
from jax.experimental import pallas as pl
from jax.experimental.pallas import tpu as pltpu

F32 = jnp.float32
BF16 = jnp.bfloat16
MESH = pl.DeviceIdType.MESH
N_DEV = 8
N_HEADS = 4
RG_LRU_C = 8.0
EPS = 1e-6
FFN_RES = 0.5
ADAM_LR, ADAM_B1, ADAM_B2, ADAM_EPS, ADAM_WD, ADAM_STEP = 0.001, 0.9, 0.999, 1e-08, 0.01, 10
V7X_VMEM_LIMIT = 56 * 1024 * 1024
CONV4_HALO = 8
CONV31_HALO = 32
SUBLANES = 8
FFN_CHUNKS = 2
GELU_C = math.sqrt(2.0 / math.pi)
GELU_K = 0.044715


def _any():
    return pl.BlockSpec(memory_space=pl.ANY)


def _params(n_grid):
    return pltpu.CompilerParams(dimension_semantics=("arbitrary",) * n_grid, vmem_limit_bytes=V7X_VMEM_LIMIT)


def _nn(a, b):
    return jnp.dot(a, b, preferred_element_type=F32)


def _nt(a, b):
    return lax.dot_general(a, b, (((1,), (1,)), ((), ())), preferred_element_type=F32)


def _tn(a, b):
    return lax.dot_general(a, b, (((0,), (0,)), ((), ())), preferred_element_type=F32)


def _sigmoid(x):
    return 0.5 * jnp.tanh(0.5 * x) + 0.5


def _rowsum(x):
    return jnp.sum(x, axis=0, keepdims=True)


def _rms_fwd(x, g):
    r = lax.rsqrt(jnp.mean(x * x, axis=-1, keepdims=True) + EPS)
    return x * r * g, r


def _rms_bwd(dn, x, r, g):
    xr = x * r
    gy = dn * g
    dx = r * (gy - xr * jnp.mean(gy * xr, axis=-1, keepdims=True))
    return dx, _rowsum(dn * xr)


def _gelu(y):
    t = jnp.tanh(GELU_C * (y + GELU_K * y * y * y))
    return 0.5 * y * (1.0 + t), t


def _gelu_grad(y, t):
    return 0.5 * (1.0 + t) + 0.5 * y * (1.0 - t * t) * GELU_C * (1.0 + 3.0 * GELU_K * y * y)


def _softplus(x):
    return jnp.maximum(x, 0.0) + jnp.log(1.0 + jnp.exp(-jnp.abs(x)))


def _one_minus_exp(z):
    series = -z * (1.0 + 0.5 * z * (1.0 + z * (1.0 / 3.0) * (1.0 + 0.25 * z)))
    return jnp.where(z > -0.05, series, 1.0 - jnp.exp(z))


def _tiles(t_real):
    if t_real > 2048:
        tm = 384
        tp = -(-t_real // tm) * tm
        return tp, tm, tm // 3, tp // 2
    tm = 128
    tp = -(-t_real // tm) * tm
    return tp, tm, tm // 2, tm


def _load_weights(copies, sems):
    cps = [pltpu.make_async_copy(s, d, sems.at[k]) for k, (s, d) in enumerate(copies)]
    for cp in cps:
        cp.start()
    for cp in cps:
        cp.wait()


def _position():
    x, y, c = lax.axis_index("x"), lax.axis_index("y"), lax.axis_index("c")
    chips = [(1 - x, y), (x, 1 - y), (1 - x, 1 - y)]
    return x, y, c, chips


def _slot(p):
    return 4 * p[0] + 2 * p[1] + p[2]


class _Gather:
    def __init__(self, shards):
        self.shards = list(shards)
        self.n = len(self.shards)

    def inputs(self):
        return self.shards

    def out_shape(self):
        return [jax.ShapeDtypeStruct((N_DEV,) + s.shape, s.dtype) for s in self.shards]

    def scratch(self):
        return [pltpu.SemaphoreType.DMA((7 * self.n,)), pltpu.SemaphoreType.DMA((7 * self.n,)),
                pltpu.SemaphoreType.DMA((self.n,))]

    def _plan(self, ins, outs, sems):
        send_sems, recv_sems, local_sems = sems
        x, y, c, chips = _position()
        me, sibling = (x, y, c), (x, y, 1 - c)

        def copy(a, k, block, to, src=None):
            dst = outs[a].at[_slot(block)]
            return pltpu.make_async_remote_copy(
                src_ref=dst if src is None else src, dst_ref=dst,
                send_sem=send_sems.at[7 * a + k], recv_sem=recv_sems.at[7 * a + k],
                device_id=to, device_id_type=MESH)

        mine = [pltpu.make_async_copy(ins[a], outs[a].at[_slot(me)], local_sems.at[a]) for a in range(self.n)]
        first = []
        for a in range(self.n):
            first.append(copy(a, 0, me, sibling, src=ins[a]))
            first += [copy(a, 1 + j, me, (*chip, c), src=ins[a]) for j, chip in enumerate(chips)]
        return copy, mine, first, me, sibling, c, chips

    def start(self, ins, outs, sems):
        _, mine, first, *_ = self._plan(ins, outs, sems)
        for cp in mine + first:
            cp.start()

    def finish(self, ins, outs, sems):
        copy, mine, first, me, sibling, c, chips = self._plan(ins, outs, sems)
        passed = []
        for j, chip in enumerate(chips):
            for a in range(self.n):
                copy(a, 1 + j, (*chip, c), me).wait_recv()
                fwd = copy(a, 4 + j, (*chip, c), sibling)
                fwd.start()
                passed.append(fwd)
        for a in range(self.n):
            copy(a, 0, sibling, me).wait_recv()
            for j, chip in enumerate(chips):
                copy(a, 4 + j, (*chip, 1 - c), me).wait_recv()
        for cp in first + passed:
            cp.wait_send()
        for cp in mine:
            cp.wait()


class _Scatter:
    def __init__(self, grads):
        self.grads = list(grads)
        self.n = len(self.grads)

    def inputs(self):
        return self.grads

    def out_shape(self):
        return [jax.ShapeDtypeStruct((N_DEV - 1,) + g.shape[1:], g.dtype) for g in self.grads]

    def scratch(self):
        return [pltpu.SemaphoreType.DMA((7 * self.n,)), pltpu.SemaphoreType.DMA((7 * self.n,))]

    def _plan(self, ins, outs, sems):
        send_sems, recv_sems = sems
        x, y, c, _ = _position()
        cps = []
        for a in range(self.n):
            for k in range(1, N_DEV):
                peer = (x ^ (k >> 2), y ^ ((k >> 1) & 1), c ^ (k & 1))
                cps.append(pltpu.make_async_remote_copy(
                    src_ref=ins[a].at[_slot(peer)], dst_ref=outs[a].at[k - 1],
                    send_sem=send_sems.at[7 * a + k - 1], recv_sem=recv_sems.at[7 * a + k - 1],
                    device_id=peer, device_id_type=MESH))
        return cps

    def start(self, ins, outs, sems):
        for cp in self._plan(ins, outs, sems):
            cp.start()

    def finish(self, ins, outs, sems):
        for cp in self._plan(ins, outs, sems):
            cp.wait()


def _hosted(inner, n_in, n_out, comm, grid):
    if comm is None:
        return inner
    nc_in, nc_out, ns = len(comm.inputs()), len(comm.out_shape()), len(comm.scratch())

    def body(*refs):
        o0 = n_in + nc_in
        s0 = o0 + n_out + nc_out
        main = refs[:n_in] + refs[o0:o0 + n_out] + refs[s0:len(refs) - ns]
        c_in, c_out, c_sems = refs[n_in:o0], refs[o0 + n_out:s0], refs[len(refs) - ns:]
        ids = [pl.program_id(ax) for ax in range(len(grid))]
        first = functools.reduce(jnp.logical_and, [i == 0 for i in ids])
        last = functools.reduce(jnp.logical_and, [i == g - 1 for i, g in zip(ids, grid)])

        @pl.when(first)
        def _():
            comm.start(c_in, c_out, c_sems)

        inner(*main)

        @pl.when(last)
        def _():
            comm.finish(c_in, c_out, c_sems)

    return body


def _call(inner, name, grid, in_specs, out_specs, out_shape, scratch, args, comm=None):
    n_in, n_out = len(args), len(out_shape)
    body = _hosted(inner, n_in, n_out, comm, grid)
    if comm is not None:
        in_specs = list(in_specs) + [_any()] * len(comm.inputs())
        args = list(args) + comm.inputs()
        out_specs = list(out_specs) + [_any()] * len(comm.out_shape())
        out_shape = list(out_shape) + comm.out_shape()
        scratch = list(scratch) + comm.scratch()
    outs = pl.pallas_call(
        body, name=name, grid=grid, in_specs=list(in_specs), out_specs=list(out_specs), out_shape=list(out_shape),
        scratch_shapes=list(scratch), compiler_params=_params(len(grid)))(*args)
    return list(outs[:n_out]), list(outs[n_out:])


class _Bcast:
    def __init__(self, block):
        self.block = block

    def inputs(self):
        return [self.block]

    def out_shape(self):
        return [jax.ShapeDtypeStruct((N_DEV,) + self.block.shape, self.block.dtype)]

    def scratch(self):
        return [pltpu.SemaphoreType.DMA((N_DEV - 1,)), pltpu.SemaphoreType.DMA((N_DEV - 1,)),
                pltpu.SemaphoreType.DMA((1,))]

    def _plan(self, ins, outs, sems):
        send_sems, recv_sems, local_sem = sems
        x, y, c, _ = _position()
        mine = outs[0].at[_slot((x, y, c))]
        cps = []
        for k in range(1, N_DEV):
            peer = (x ^ (k >> 2), y ^ ((k >> 1) & 1), c ^ (k & 1))
            cps.append(pltpu.make_async_remote_copy(
                src_ref=ins[0], dst_ref=mine, send_sem=send_sems.at[k - 1], recv_sem=recv_sems.at[k - 1],
                device_id=peer, device_id_type=MESH))
        return pltpu.make_async_copy(ins[0], mine, local_sem.at[0]), cps

    def start(self, ins, outs, sems):
        own, cps = self._plan(ins, outs, sems)
        own.start()
        for cp in cps:
            cp.start()

    def finish(self, ins, outs, sems):
        own, cps = self._plan(ins, outs, sems)
        for cp in cps:
            cp.wait()
        own.wait()


def _first_gather(shards, small_idx, x2, t2, n_meta, tp):
    comm = _Gather(shards)
    n = comm.n
    seq, d = x2.shape
    t_real = n_meta + seq
    n_pad = tp - t_real
    cw = d // N_DEV
    zeros = jnp.zeros((max(n_pad, n_meta), d), F32)

    def body(*refs):
        ins, (x_ref, t_ref, z_ref) = refs[:n], refs[n:n + 3]
        outs, (h0_ref, tg_ref) = refs[n + 3:2 * n + 3], refs[2 * n + 3:2 * n + 5]
        sems, lsem = refs[2 * n + 5:2 * n + 8], refs[2 * n + 8]
        comm.start(ins, outs, sems)
        local = [(x_ref, h0_ref.at[pl.ds(n_meta, seq)]), (t_ref, tg_ref.at[pl.ds(n_meta, seq)]),
                 (z_ref.at[pl.ds(0, n_pad)], h0_ref.at[pl.ds(t_real, n_pad)]),
                 (z_ref.at[pl.ds(0, n_pad)], tg_ref.at[pl.ds(t_real, n_pad)]),
                 (z_ref.at[pl.ds(0, n_meta)], tg_ref.at[pl.ds(0, n_meta)])]
        cps = [pltpu.make_async_copy(s, t, lsem.at[k]) for k, (s, t) in enumerate(local)]
        for cp in cps:
            cp.start()
        comm.finish(ins, outs, sems)
        meta = [pltpu.make_async_copy(outs[small_idx].at[k, pl.ds(0, n_meta)],
                                      h0_ref.at[pl.ds(0, n_meta), pl.ds(k * cw, cw)], lsem.at[len(local) + k])
                for k in range(N_DEV)]
        for cp in meta:
            cp.start()
        for cp in cps + meta:
            cp.wait()

    staged = [jax.ShapeDtypeStruct((tp, d), F32)] * 2
    outs = pl.pallas_call(
        body, name="weights_all_gather", out_shape=comm.out_shape() + staged,
        in_specs=[_any()] * (n + 3), out_specs=[_any()] * (n + 2),
        scratch_shapes=comm.scratch() + [pltpu.SemaphoreType.DMA((5 + N_DEV,))],
    )(*shards, x2, t2, zeros)
    return outs[:n], outs[n], outs[n + 1]


def _pair_exchange(grads):
    n = len(grads)

    def body(*refs):
        ins, outs = refs[:n], refs[n:2 * n]
        send_sems, recv_sems = refs[2 * n:]
        x, y, c, _ = _position()
        cps = [pltpu.make_async_remote_copy(
            src_ref=ins[a].at[:, 1 - c], dst_ref=outs[a],
            send_sem=send_sems.at[a], recv_sem=recv_sems.at[a],
            device_id=(x, y, 1 - c), device_id_type=MESH) for a in range(n)]
        for cp in cps:
            cp.start()
        for cp in cps:
            cp.wait()

    return pl.pallas_call(
        body, name="grads_pair_exchange",
        out_shape=[jax.ShapeDtypeStruct((4,) + g.shape[2:], g.dtype) for g in grads],
        in_specs=[_any()] * n, out_specs=[_any()] * n,
        scratch_shapes=[pltpu.SemaphoreType.DMA((n,)), pltpu.SemaphoreType.DMA((n,))],
    )(*grads)


def _chip_exchange(combs):
    n = len(combs)

    def body(*refs):
        ins, outs = refs[:n], refs[n:2 * n]
        send_sems, recv_sems = refs[2 * n:]
        x, y, c, chips = _position()
        cps = []
        for a in range(n):
            for j, (cx, cy) in enumerate(chips):
                cps.append(pltpu.make_async_remote_copy(
                    src_ref=ins[a].at[2 * cx + cy], dst_ref=outs[a].at[j],
                    send_sem=send_sems.at[3 * a + j], recv_sem=recv_sems.at[3 * a + j],
                    device_id=(cx, cy, c), device_id_type=MESH))
        for cp in cps:
            cp.start()
        for cp in cps:
            cp.wait()

    return pl.pallas_call(
        body, name="grads_chip_exchange",
        out_shape=[jax.ShapeDtypeStruct((3,) + g.shape[1:], g.dtype) for g in combs],
        in_specs=[_any()] * n, out_specs=[_any()] * n,
        scratch_shapes=[pltpu.SemaphoreType.DMA((3 * n,)), pltpu.SemaphoreType.DMA((3 * n,))],
    )(*combs)


def _pair_add(grad, recv, core):
    blk = grad.shape[2:]
    zeros = (0,) * len(blk)

    def body(core_ref, g_ref, r_ref, o_ref):
        del core_ref
        o_ref[...] = (g_ref[...].astype(F32) + r_ref[...].astype(F32)).astype(BF16)

    return pl.pallas_call(
        body, name="grads_pair_add",
        out_shape=jax.ShapeDtypeStruct((4,) + blk, BF16),
        grid_spec=pltpu.PrefetchScalarGridSpec(
            num_scalar_prefetch=1, grid=(4,),
            in_specs=[pl.BlockSpec((None, None) + blk, lambda i, cr: (i, cr[0]) + zeros),
                      pl.BlockSpec((None,) + blk, lambda i, cr: (i,) + zeros)],
            out_specs=pl.BlockSpec((None,) + blk, lambda i, cr: (i,) + zeros)),
        compiler_params=_params(1),
    )(core, grad, recv)


def _adamw(w, g, m, v):
    m2 = ADAM_B1 * m + (1.0 - ADAM_B1) * g
    v2 = ADAM_B2 * v + (1.0 - ADAM_B2) * (g * g)
    m_hat = m2 / (1.0 - ADAM_B1 ** ADAM_STEP)
    v_hat = v2 / (1.0 - ADAM_B2 ** ADAM_STEP)
    delta = -ADAM_LR * (m_hat / (jnp.sqrt(v_hat) + ADAM_EPS) + ADAM_WD * w)
    return delta, m2, v2


def _final_adamw(own, recv, idx, parts):
    blk = own.shape[1:]
    n_recv = recv.shape[0]
    n_parts = len(parts)
    per = blk[0] // n_parts if n_parts > 1 else None
    rows = blk[-2]
    n_chunks = 1 if n_parts > 1 else (4 if rows % 64 == 0 and rows >= 512 else (2 if rows % 32 == 0 else 1))
    cblk = blk[:-2] + (rows // n_chunks, blk[-1])
    lead = (0,) * (len(blk) - 2)

    def body(idx_ref, c_ref, r_ref, *refs):
        del idx_ref
        ins, outs = refs[:3 * n_parts], refs[3 * n_parts:]
        g = c_ref[...].astype(F32)
        for k in range(n_recv):
            g = g + r_ref[k].astype(F32)
        for p in range(n_parts):
            w_ref, m_ref, v_ref = ins[3 * p:3 * p + 3]
            if n_parts == 1:
                gp = g
            elif per == 1:
                gp = g[p]
            else:
                gp = g[p * per:(p + 1) * per]
            delta, m2, v2 = _adamw(w_ref[0], gp, m_ref[0], v_ref[0])
            o = outs[4 * p:4 * p + 4]
            o[0][0] = gp
            o[1][0] = delta
            o[2][0] = m2
            o[3][0] = v2

    flat = [a for wmv in parts for a in wmv]

    def part_spec(a):
        shape = a.shape[:-2] + (a.shape[-2] // n_chunks, a.shape[-1])
        return pl.BlockSpec(shape, lambda i, cr, nd=a.ndim: (0,) * (nd - 2) + (i, 0))

    outs = pl.pallas_call(
        body, name="grads_sum_adamw",
        out_shape=[jax.ShapeDtypeStruct(wmv[0].shape, F32) for wmv in parts for _ in range(4)],
        grid_spec=pltpu.PrefetchScalarGridSpec(
            num_scalar_prefetch=1, grid=(n_chunks,),
            in_specs=[pl.BlockSpec((None,) + cblk, lambda i, cr: (cr[0],) + lead + (i, 0)),
                      pl.BlockSpec((n_recv,) + cblk, lambda i, cr: (0,) + lead + (i, 0))]
                     + [part_spec(a) for a in flat],
            out_specs=[part_spec(wmv[0]) for wmv in parts for _ in range(4)]),
        compiler_params=_params(1),
    )(idx, own, recv, *flat)
    return [tuple(outs[4 * p:4 * p + 4]) for p in range(n_parts)]


def _small_adamw(partials, layout, me_index):
    _, rows, d = partials.shape
    n = len(layout)
    cw = d // N_DEV

    def body(me_ref, p_ref, *refs):
        ins, t_ref, outs = refs[:3 * n], refs[3 * n], refs[3 * n + 1:]
        me = me_ref[0]
        total = p_ref[0]
        for j in range(1, N_DEV):
            total = total + p_ref[j]
        t_ref[...] = total
        for e, (kind, r0, nr, _, _, _) in enumerate(layout):
            w_ref, m_ref, v_ref = ins[3 * e:3 * e + 3]
            o = outs[4 * e:4 * e + 4]
            if kind == "rep":
                g = t_ref[r0:r0 + nr, :]
                delta, m2, v2 = _adamw(w_ref[...], g, m_ref[...], v_ref[...])
                for ref, val in zip(o, (g, delta, m2, v2)):
                    ref[...] = val
            elif kind == "wide":
                for q in range(nr):
                    sl = slice(q * d, (q + 1) * d)
                    g = t_ref[r0 + q:r0 + q + 1, :]
                    delta, m2, v2 = _adamw(w_ref[:, sl], g, m_ref[:, sl], v_ref[:, sl])
                    for ref, val in zip(o, (g, delta, m2, v2)):
                        ref[:, sl] = val
            else:
                for j in range(N_DEV):
                    @pl.when(me == j)
                    def _(j=j, o=o, w_ref=w_ref, m_ref=m_ref, v_ref=v_ref, r0=r0, nr=nr):
                        g = t_ref[r0:r0 + nr, j * cw:(j + 1) * cw]
                        delta, m2, v2 = _adamw(w_ref[...], g, m_ref[...], v_ref[...])
                        for ref, val in zip(o, (g, delta, m2, v2)):
                            ref[...] = val

    flat = [a for ent in layout for a in ent[3:]]
    vm = pl.BlockSpec(memory_space=pltpu.VMEM)
    outs = pl.pallas_call(
        body, name="small_adamw",
        out_shape=[jax.ShapeDtypeStruct((rows, d), F32)]
                  + [jax.ShapeDtypeStruct(ent[3].shape, F32) for ent in layout for _ in range(4)],
        in_specs=[pl.BlockSpec(memory_space=pltpu.SMEM), vm] + [vm] * len(flat),
        out_specs=[vm] * (1 + 4 * n),
        compiler_params=pltpu.CompilerParams(vmem_limit_bytes=V7X_VMEM_LIMIT),
    )(me_index, partials, *flat)
    return outs[0], [tuple(outs[1 + 4 * e:5 + 4 * e]) for e in range(n)]


def _ffn_fwd(h, g, wgu, wd, tm, loss=None, comm=None):
    tp, d = h.shape
    f = wd.shape[0]
    fc = f // FFN_CHUNKS
    nt = tp // tm
    with_loss = loss is not None
    if with_loss:
        tgt, gf, n_meta, t_real = loss

    def body(*refs):
        if with_loss:
            (h_ref, g_ref, wgu_hbm, wd_hbm, tgt_ref, gf_ref, out_ref, gu_ref, n_ref, loss_ref, dgf_ref,
             wgu_v, wd_v, sems) = refs
        else:
            h_ref, g_ref, wgu_hbm, wd_hbm, out_ref, gu_ref, n_ref, wgu_v, wd_v, sems = refs
        i = pl.program_id(0)

        @pl.when(i == 0)
        def _():
            _load_weights([(wgu_hbm, wgu_v), (wd_hbm, wd_v)], sems)
            if with_loss:
                loss_ref[...] = jnp.zeros_like(loss_ref)
                dgf_ref[...] = jnp.zeros_like(dgf_ref)

        x = h_ref[...]
        n, _ = _rms_fwd(x, g_ref[...])
        nb = n.astype(BF16)
        n_ref[...] = nb
        acc = jnp.zeros((tm, d), F32)
        for j in range(FFN_CHUNKS):
            cols = slice(j * fc, (j + 1) * fc)
            gate = _nt(nb, wgu_v[pl.ds(j * fc, fc), :])
            up = _nt(nb, wgu_v[pl.ds(f + j * fc, fc), :])
            gu_ref[0, :, cols] = gate.astype(BF16)
            gu_ref[1, :, cols] = up.astype(BF16)
            act = (gate * _sigmoid(gate) * up).astype(BF16)
            acc = acc + _nn(act, wd_v[pl.ds(j * fc, fc), :])
        hn = x + FFN_RES * acc
        if not with_loss:
            out_ref[...] = hn
        else:
            gfv = gf_ref[...]
            r = lax.rsqrt(jnp.mean(hn * hn, axis=-1, keepdims=True) + EPS)
            xr = hn * r
            rows = i * tm + lax.broadcasted_iota(jnp.int32, (tm, 1), 0)
            mask = jnp.logical_and(rows >= n_meta, rows < t_real)
            diff = jnp.where(mask, xr * gfv - tgt_ref[...], 0.0)
            loss_ref[...] += jnp.zeros_like(loss_ref) + 0.5 * jnp.sum(diff * diff) / d
            dy = diff / d
            gy = dy * gfv
            out_ref[...] = r * (gy - xr * jnp.mean(gy * xr, axis=-1, keepdims=True))
            dgf_ref[...] += _rowsum(dy * xr)

    row = pl.BlockSpec((tm, d), lambda i: (i, 0))
    vec = pl.BlockSpec((1, d), lambda i: (0, 0))
    in_specs = [row, vec, _any(), _any()]
    out_shape = [jax.ShapeDtypeStruct((tp, d), F32), jax.ShapeDtypeStruct((2, tp, f), BF16),
                 jax.ShapeDtypeStruct((tp, d), BF16)]
    out_specs = [row, pl.BlockSpec((2, tm, f), lambda i: (0, i, 0)), row]
    args = [h, g, wgu, wd]
    if with_loss:
        in_specs += [row, vec]
        out_shape += [jax.ShapeDtypeStruct((1, d), F32), jax.ShapeDtypeStruct((1, d), F32)]
        out_specs += [vec, vec]
        args += [tgt, gf]
    return _call(body, "ffn_fwd_loss" if with_loss else "ffn_fwd", (nt,), in_specs, out_specs, out_shape,
                 [pltpu.VMEM((2 * f, d), BF16), pltpu.VMEM((f, d), BF16), pltpu.SemaphoreType.DMA((2,))],
                 args, comm)


def _ffn_bwd(dh, h, gu, g, wgu, wd, tm, comm=None):
    tp, d = h.shape
    f = wd.shape[0]
    fc = f // FFN_CHUNKS
    nt = tp // tm

    def body(dh_ref, h_ref, gu_ref, g_ref, wgu_hbm, wd_hbm,
             dhin_ref, dgu_ref, act_ref, df_ref, dg_ref, wgu_v, wd_v, dn_v, sems):
        i, j = pl.program_id(0), pl.program_id(1)

        @pl.when(jnp.logical_and(i == 0, j == 0))
        def _():
            _load_weights([(wgu_hbm, wgu_v), (wd_hbm, wd_v)], sems)
            dg_ref[...] = jnp.zeros_like(dg_ref)

        dfb = (FFN_RES * dh_ref[...]).astype(BF16)

        @pl.when(j == 0)
        def _():
            df_ref[...] = dfb
            dn_v[...] = jnp.zeros_like(dn_v)

        lo = pl.multiple_of(j * fc, 16)
        dact = _nt(dfb, wd_v[pl.ds(lo, fc), :])
        gate = gu_ref[0].astype(F32)
        up = gu_ref[1].astype(F32)
        sg = _sigmoid(gate)
        silu = gate * sg
        act_ref[...] = (silu * up).astype(BF16)
        dgate = (dact * up * (sg * (1.0 + gate * (1.0 - sg)))).astype(BF16)
        dup = (dact * silu).astype(BF16)
        dgu_ref[0] = dgate
        dgu_ref[1] = dup
        dn_v[...] += _nn(dgate, wgu_v[pl.ds(lo, fc), :]) + _nn(dup, wgu_v[pl.ds(pl.multiple_of(f + j * fc, 16), fc), :])

        @pl.when(j == FFN_CHUNKS - 1)
        def _():
            x = h_ref[...]
            r = lax.rsqrt(jnp.mean(x * x, axis=-1, keepdims=True) + EPS)
            dx, dgp = _rms_bwd(dn_v[...], x, r, g_ref[...])
            dhin_ref[...] = dh_ref[...] + dx
            dg_ref[...] += dgp

    row = pl.BlockSpec((tm, d), lambda i, j: (i, 0))
    vec = pl.BlockSpec((1, d), lambda i, j: (0, 0))
    hid2 = pl.BlockSpec((2, tm, fc), lambda i, j: (0, i, j))
    return _call(
        body, "ffn_bwd", (nt, FFN_CHUNKS),
        [row, row, hid2, vec, _any(), _any()],
        [row, hid2, pl.BlockSpec((tm, fc), lambda i, j: (i, j)), row, vec],
        [jax.ShapeDtypeStruct((tp, d), F32), jax.ShapeDtypeStruct((2, tp, f), BF16),
         jax.ShapeDtypeStruct((tp, f), BF16), jax.ShapeDtypeStruct((tp, d), BF16),
         jax.ShapeDtypeStruct((1, d), F32)],
        [pltpu.VMEM((2 * f, d), BF16), pltpu.VMEM((f, d), BF16), pltpu.VMEM((tm, d), F32),
         pltpu.SemaphoreType.DMA((2,))],
        [dh, h, gu, g, wgu, wd], comm)


def _piece_segments(q, d, nb_cols):
    segs = []
    for j in range(N_DEV):
        lo, hi = max(q * d, j * nb_cols), min((q + 1) * d, (j + 1) * nb_cols)
        if lo < hi:
            segs.append((j, lo - q * d, hi - q * d, lo - j * nb_cols, hi - j * nb_cols))
    return segs


def _w3_copies(w3_hbm, rows, w3_v):
    return [(w3_hbm.at[k, pl.ds(q * rows, rows)], w3_v.at[q, pl.ds(k * rows, rows)])
            for q in range(3) for k in range(N_DEV)]


def _gates(xrb, wg_ref, ba, bx, lam, hd):
    pre_r, pre_i = [], []
    for hh in range(N_HEADS):
        xh = xrb[:, hh * hd:(hh + 1) * hd]
        pre_r.append(_nn(xh, wg_ref[0, hh]))
        pre_i.append(_nn(xh, wg_ref[1, hh]))
    r = _sigmoid(jnp.concatenate(pre_r, axis=1) + ba)
    ig = _sigmoid(jnp.concatenate(pre_i, axis=1) + bx)
    sp = _softplus(-lam)
    log_a = -RG_LRU_C * r * sp
    a = jnp.exp(log_a)
    s = jnp.sqrt(_one_minus_exp(2.0 * log_a))
    return r, ig, sp, a, s


def _scan_fwd(a, u, h_prev):
    tm = a.shape[0]
    rows = lax.broadcasted_iota(jnp.int32, a.shape, 0)
    d = 1
    while d < tm:
        if d < SUBLANES:
            keep = rows >= d
            u = jnp.where(keep, a * pltpu.roll(u, d, 0) + u, u)
            a = jnp.where(keep, a * pltpu.roll(a, d, 0), a)
        else:
            u = jnp.concatenate([u[:d], a[d:] * u[:tm - d] + u[d:]], axis=0)
            a = jnp.concatenate([a[:d], a[d:] * a[:tm - d]], axis=0)
        d *= 2
    return u + a * h_prev


def _scan_bwd(b, v, g_next):
    tm = b.shape[0]
    rows = lax.broadcasted_iota(jnp.int32, b.shape, 0)
    d = 1
    while d < tm:
        if d < SUBLANES:
            keep = rows < tm - d
            v = jnp.where(keep, v + b * pltpu.roll(v, tm - d, 0), v)
            b = jnp.where(keep, b * pltpu.roll(b, tm - d, 0), b)
        else:
            v = jnp.concatenate([v[:tm - d] + b[:tm - d] * v[d:], v[tm - d:]], axis=0)
            b = jnp.concatenate([b[:tm - d] * b[d:], b[tm - d:]], axis=0)
        d *= 2
    return v + b * g_next


def _shifted_copies(ext_ref, es_ref, n_rows):
    for s in range(1, SUBLANES):
        es_ref[s, pl.ds(0, n_rows), :] = ext_ref[pl.ds(s, n_rows), :]


def _tap(ext_ref, es_ref, off, tm):
    q, s = divmod(off, SUBLANES)
    if s == 0:
        return ext_ref[pl.ds(SUBLANES * q, tm), :]
    return es_ref[s, pl.ds(SUBLANES * q, tm), :]


def _mixer_fwd(h, g, b_in, win_all, cw4, cb4, wg, ba, bx, lam, cw31, cb31, lng, lnb, bcp, w3_all, tm, comm=None):
    tp, d = h.shape
    nb_cols = win_all.shape[-1]
    n_in = N_DEV * nb_cols
    hd = wg.shape[-1]
    k4, k31 = cw4.shape[0], cw31.shape[0]
    w3_rows = d // N_DEV

    def body(h_ref, g_ref, b_ref, win_hbm, cw4_ref, cb4_ref, wg_ref, ba_ref, bx_ref, lam_ref, cw31_ref, cb31_ref,
             lng_ref, lnb_ref, bcp_ref, w3_hbm,
             h2_ref, p_ref, n_ref, xr_ref, hs_ref, v1_ref, ya_ref, yb_ref,
             win_v, w3_v, ext4, ext31, es31, hcar, sems):
        @pl.when(pl.program_id(0) == 0)
        def _():
            _load_weights([(win_hbm, win_v)] + _w3_copies(w3_hbm, w3_rows, w3_v), sems)
            ext4[pl.ds(0, CONV4_HALO), :] = jnp.zeros((CONV4_HALO, d), F32)
            ext31[pl.ds(0, CONV31_HALO), :] = jnp.zeros((CONV31_HALO, d), F32)
            hcar[...] = jnp.zeros_like(hcar)

        n, _ = _rms_fwd(h_ref[...], g_ref[...])
        nb = n.astype(BF16)
        n_ref[...] = nb

        def piece(q):
            parts = [_nn(nb, win_v[j, :, bl:bh]) for j, _, _, bl, bh in _piece_segments(q, d, nb_cols)]
            pq = (jnp.concatenate(parts, axis=1) + b_ref[:, q * d:(q + 1) * d]).astype(BF16)
            p_ref[:, q * d:(q + 1) * d] = pq
            return pq.astype(F32)

        x_rnn, y_rnn, glu_v, glu_g, gate_a, gate_b = [piece(q) for q in range(6)]

        ext4[pl.ds(CONV4_HALO, tm), :] = x_rnn
        xr = cb4_ref[...] + jnp.zeros((tm, d), F32)
        for k in range(k4):
            xr = xr + cw4_ref[k:k + 1, :] * ext4[pl.ds(CONV4_HALO - (k4 - 1) + k, tm), :]
        ext4[pl.ds(0, CONV4_HALO), :] = ext4[pl.ds(tm, CONV4_HALO), :]
        xrb = xr.astype(BF16)
        xr_ref[...] = xrb
        xr = xrb.astype(F32)
        _, ig, _, a, s = _gates(xrb, wg_ref, ba_ref[...], bx_ref[...], lam_ref[...], hd)
        hseq = _scan_fwd(a, s * (ig * xr), hcar[0:1, :])
        hcar[0:1, :] = hseq[tm - 1:tm, :]
        hs_ref[...] = hseq.astype(BF16)
        gl, _ = _gelu(y_rnn)
        ya = _nn((hseq * gl).astype(BF16), w3_v[0])
        ya_ref[...] = ya.astype(BF16)

        ext31[pl.ds(CONV31_HALO, tm), :] = glu_v * _sigmoid(glu_g)
        _shifted_copies(ext31, es31, tm + CONV31_HALO - SUBLANES)
        v1 = cb31_ref[...] + jnp.zeros((tm, d), F32)
        for k in range(k31):
            v1 = v1 + cw31_ref[k:k + 1, :] * _tap(ext31, es31, CONV31_HALO - (k31 - 1) + k, tm)
        ext31[pl.ds(0, CONV31_HALO), :] = ext31[pl.ds(tm, CONV31_HALO), :]
        v1b = v1.astype(BF16)
        v1_ref[...] = v1b
        v1 = v1b.astype(F32)
        xc = v1 - jnp.mean(v1, axis=-1, keepdims=True)
        rstd = lax.rsqrt(jnp.mean(xc * xc, axis=-1, keepdims=True) + EPS)
        v2 = xc * rstd * lng_ref[...] + lnb_ref[...]
        yb = _nn((v2 * _sigmoid(v2)).astype(BF16), w3_v[1]) + bcp_ref[...]
        yb_ref[...] = yb.astype(BF16)

        merged = _sigmoid(gate_a) * ya + _sigmoid(gate_b) * yb
        h2_ref[...] = h_ref[...] + _nn(merged.astype(BF16), w3_v[2])

    row = pl.BlockSpec((tm, d), lambda i: (i, 0))
    wide = pl.BlockSpec((tm, n_in), lambda i: (i, 0))
    full = lambda a: pl.BlockSpec(a.shape, lambda i, nd=a.ndim: (0,) * nd)
    smalls = [cw4, cb4, wg, ba, bx, lam, cw31, cb31, lng, lnb, bcp]
    return _call(
        body, "mixer_fwd", (tp // tm,),
        [row, full(g), full(b_in), _any()] + [full(a) for a in smalls] + [_any()],
        [row, wide] + [row] * 6,
        [jax.ShapeDtypeStruct((tp, d), F32), jax.ShapeDtypeStruct((tp, n_in), BF16)]
        + [jax.ShapeDtypeStruct((tp, d), BF16)] * 6,
        [pltpu.VMEM(win_all.shape, BF16),
         pltpu.VMEM((3, d, d), BF16),
         pltpu.VMEM((tm + CONV4_HALO, d), F32),
         pltpu.VMEM((tm + CONV31_HALO, d), F32),
         pltpu.VMEM((SUBLANES, tm + CONV31_HALO, d), F32),
         pltpu.VMEM((SUBLANES, d), F32),
         pltpu.SemaphoreType.DMA((1 + 3 * N_DEV,))],
        [h, g, b_in, win_all, *smalls, w3_all], comm)


SG_BIN, SG_CW4, SG_CB4, SG_BA, SG_BX, SG_LAM, SG_CB31, SG_LNG, SG_LNB, SG_BCP, SG_MIX, SG_CW31 = 0, 6, 10, 11, 12, 13, 14, 15, 16, 17, 18, 19


def _mixer_bwd(dh2, h, g, proj, xr_s, hs_s, v1_s, ya_s, yb_s, win_all, cw4, wg, ba, bx, lam, cw31, lng, lnb, w3_all, tm,
               comm=None):
    tp, d = dh2.shape
    nb_cols = win_all.shape[-1]
    n_in = proj.shape[1]
    hd = wg.shape[-1]
    k4, k31 = cw4.shape[0], cw31.shape[0]
    nt = tp // tm
    w3_rows = d // N_DEV
    sg_rows = -(-(SG_CW31 + k31) // SUBLANES) * SUBLANES
    halo_rows = 16
    per = tm // halo_rows

    def body(dh_ref, h_ref, g_ref, p_ref, xr_ref, hs_ref, hh_ref, v1_ref, ya_ref, yb_ref, win_hbm,
             cw4_ref, wg_ref, ba_ref, bx_ref, lam_ref, cw31_ref, lng_ref, lnb_ref, w3_hbm,
             dh1_ref, dp_ref, x3_ref, y3_ref, yg_ref, sg_ref,
             win_v, w3_v, extd4, extd31, es31, gcar, sems):
        i = pl.program_id(0)
        tile = nt - 1 - i

        @pl.when(i == 0)
        def _():
            _load_weights([(win_hbm, win_v)] + _w3_copies(w3_hbm, w3_rows, w3_v), sems)
            extd4[pl.ds(tm, CONV4_HALO), :] = jnp.zeros((CONV4_HALO, d), F32)
            extd31[pl.ds(tm, CONV31_HALO), :] = jnp.zeros((CONV31_HALO, d), F32)
            gcar[...] = jnp.zeros_like(gcar)
            sg_ref[...] = jnp.zeros_like(sg_ref)

        def acc(row, val):
            sg_ref[row:row + 1, :] += _rowsum(val)

        rows = lax.broadcasted_iota(jnp.int32, (tm, d), 0)
        x_rnn = p_ref[:, 0:d].astype(F32)
        y_rnn = p_ref[:, d:2 * d].astype(F32)
        glu_v = p_ref[:, 2 * d:3 * d].astype(F32)
        glu_g = p_ref[:, 3 * d:4 * d].astype(F32)
        sga = _sigmoid(p_ref[:, 4 * d:5 * d].astype(F32))
        sgb = _sigmoid(p_ref[:, 5 * d:6 * d].astype(F32))
        ya = ya_ref[...].astype(F32)
        yb = yb_ref[...].astype(F32)

        dmob = dh_ref[...].astype(BF16)
        dmerged = _nt(dmob, w3_v[2])
        x3_ref[:, 0:d] = (sga * ya + sgb * yb).astype(BF16)
        y3_ref[:, 0:d] = dmob
        dya = sga * dmerged
        dyb = sgb * dmerged
        dn_parts = []

        def emit(q, val):
            vb = val.astype(BF16)
            dp_ref[:, q * d:(q + 1) * d] = vb
            acc(SG_BIN + q, val)
            for j, lo, hi, bl, bh in _piece_segments(q, d, nb_cols):
                term = _nt(vb[:, lo:hi], win_v[j, :, bl:bh])
                dn_parts[:] = [term if not dn_parts else dn_parts[0] + term]

        emit(4, dmerged * ya * sga * (1.0 - sga))
        emit(5, dmerged * yb * sgb * (1.0 - sgb))

        dyab = dya.astype(BF16)
        y3_ref[:, d:2 * d] = dyab
        dza = _nt(dyab, w3_v[0])
        hsv = hs_ref[...].astype(F32)
        gl, th = _gelu(y_rnn)
        x3_ref[:, d:2 * d] = (hsv * gl).astype(BF16)
        emit(1, dza * hsv * _gelu_grad(y_rnn, th))
        dhs = dza * gl
        xrb = xr_ref[...]
        xr = xrb.astype(F32)
        lam_v = lam_ref[...]
        r, ig, sp, a, s = _gates(xrb, wg_ref, ba_ref[...], bx_ref[...], lam_v, hd)
        b = jnp.where(rows == tm - 1, gcar[1:2, :], pltpu.roll(a, tm - 1, 0))
        big_g = _scan_bwd(b, dhs, gcar[0:1, :])
        gcar[0:1, :] = big_g[0:1, :]
        gcar[1:2, :] = a[0:1, :]
        h_before = jnp.where(tile > 0, hh_ref[halo_rows - 1:halo_rows, :].astype(F32), 0.0)
        h_prev = jnp.where(rows == 0, h_before, pltpu.roll(hsv, 1, 0))
        ds = big_g * ig * xr
        dla = big_g * h_prev * a - ds * (a * a) / jnp.maximum(s, 1e-20)
        acc(SG_LAM, dla * r * (RG_LRU_C * _sigmoid(-lam_v)))
        dpr = dla * (-RG_LRU_C * sp) * r * (1.0 - r)
        dpi = big_g * s * xr * ig * (1.0 - ig)
        acc(SG_BA, dpr)
        acc(SG_BX, dpi)
        dprb = dpr.astype(BF16)
        dpib = dpi.astype(BF16)
        yg_ref[:, 0:d] = dprb
        yg_ref[:, d:2 * d] = dpib
        back = []
        for hh in range(N_HEADS):
            sl = slice(hh * hd, (hh + 1) * hd)
            back.append(_nt(dprb[:, sl], wg_ref[0, hh]) + _nt(dpib[:, sl], wg_ref[1, hh]))
        dxr = big_g * s * ig + jnp.concatenate(back, axis=1)
        acc(SG_CB4, dxr)
        extd4[pl.ds(0, tm), :] = dxr
        dx_rnn = jnp.zeros((tm, d), F32)
        for k in range(k4):
            term = extd4[pl.ds(k4 - 1 - k, tm), :]
            dx_rnn = dx_rnn + cw4_ref[k:k + 1, :] * term
            acc(SG_CW4 + k, x_rnn * term)
        extd4[pl.ds(tm, CONV4_HALO), :] = extd4[pl.ds(0, CONV4_HALO), :]
        emit(0, dx_rnn)

        dybb = dyb.astype(BF16)
        y3_ref[:, 2 * d:3 * d] = dybb
        acc(SG_BCP, dyb)
        dv3 = _nt(dybb, w3_v[1])
        v1 = v1_ref[...].astype(F32)
        xc = v1 - jnp.mean(v1, axis=-1, keepdims=True)
        rstd = lax.rsqrt(jnp.mean(xc * xc, axis=-1, keepdims=True) + EPS)
        xhat = xc * rstd
        lng_v = lng_ref[...]
        v2 = xhat * lng_v + lnb_ref[...]
        s2 = _sigmoid(v2)
        x3_ref[:, 2 * d:3 * d] = (v2 * s2).astype(BF16)
        dv2 = dv3 * (s2 * (1.0 + v2 * (1.0 - s2)))
        acc(SG_LNG, dv2 * xhat)
        acc(SG_LNB, dv2)
        dxh = dv2 * lng_v
        dv1 = rstd * (dxh - jnp.mean(dxh, axis=-1, keepdims=True)
                      - xhat * jnp.mean(dxh * xhat, axis=-1, keepdims=True))
        acc(SG_CB31, dv1)
        extd31[pl.ds(0, tm), :] = dv1
        _shifted_copies(extd31, es31, tm + CONV31_HALO - SUBLANES)
        sgg = _sigmoid(glu_g)
        v0 = glu_v * sgg
        dv0 = jnp.zeros((tm, d), F32)
        for k in range(k31):
            term = _tap(extd31, es31, k31 - 1 - k, tm)
            dv0 = dv0 + cw31_ref[k:k + 1, :] * term
            acc(SG_CW31 + k, v0 * term)
        extd31[pl.ds(tm, CONV31_HALO), :] = extd31[pl.ds(0, CONV31_HALO), :]
        emit(2, dv0 * sgg)
        emit(3, dv0 * glu_v * sgg * (1.0 - sgg))

        dn = dn_parts[0]
        x = h_ref[...]
        rr = lax.rsqrt(jnp.mean(x * x, axis=-1, keepdims=True) + EPS)
        dx, dgp = _rms_bwd(dn, x, rr, g_ref[...])
        dh1_ref[...] = dh_ref[...] + dx
        sg_ref[SG_MIX:SG_MIX + 1, :] += dgp

    rev = lambda i: (nt - 1 - i, 0)
    row = pl.BlockSpec((tm, d), rev)
    wide = pl.BlockSpec((tm, n_in), rev)
    full = lambda a: pl.BlockSpec(a.shape, lambda i, nd=a.ndim: (0,) * nd)
    halo = pl.BlockSpec((halo_rows, d), lambda i: (jnp.maximum((nt - 1 - i) * per - 1, 0), 0))
    smalls = [cw4, wg, ba, bx, lam, cw31, lng, lnb]
    return _call(
        body, "mixer_bwd", (nt,),
        [row, row, full(g), wide, row, row, halo, row, row, row, _any()]
        + [full(a) for a in smalls] + [_any()],
        [row, wide, pl.BlockSpec((tm, 3 * d), rev), pl.BlockSpec((tm, 3 * d), rev),
         pl.BlockSpec((tm, 2 * d), rev), pl.BlockSpec((sg_rows, d), lambda i: (0, 0))],
        [jax.ShapeDtypeStruct((tp, d), F32), jax.ShapeDtypeStruct((tp, n_in), BF16),
         jax.ShapeDtypeStruct((tp, 3 * d), BF16), jax.ShapeDtypeStruct((tp, 3 * d), BF16),
         jax.ShapeDtypeStruct((tp, 2 * d), BF16), jax.ShapeDtypeStruct((sg_rows, d), F32)],
        [pltpu.VMEM(win_all.shape, BF16),
         pltpu.VMEM((3, d, d), BF16),
         pltpu.VMEM((tm + CONV4_HALO, d), F32),
         pltpu.VMEM((tm + CONV31_HALO, d), F32),
         pltpu.VMEM((SUBLANES, tm + CONV31_HALO, d), F32),
         pltpu.VMEM((SUBLANES, d), F32),
         pltpu.SemaphoreType.DMA((1 + 3 * N_DEV,))],
        [dh2, h, g, proj, xr_s, hs_s, hs_s, v1_s, ya_s, yb_s, win_all, *smalls, w3_all], comm)


def _tn_matmul(name, x, y, x_spec, y_spec, n_blocks, kb, nb, tm, tp, out_shape, out_spec, out_view, comm=None):
    nt = tp // tm

    def body(x_ref, y_ref, o_ref, acc):
        i = pl.program_id(1)

        @pl.when(i == 0)
        def _():
            acc[...] = jnp.zeros_like(acc)

        acc[...] += _tn(x_ref[...], y_ref[...])

        @pl.when(i == nt - 1)
        def _():
            o_ref[...] = acc[...].astype(BF16).reshape(out_view)

    outs, extra = _call(body, name, (n_blocks, nt), [x_spec, y_spec], [out_spec],
                        [jax.ShapeDtypeStruct(out_shape, BF16)], [pltpu.VMEM((kb, nb), F32)], [x, y], comm)
    return outs[0], extra


def kernel(x, meta_tokens, ffn1_norm, ffn1_w_gu, ffn1_w_down, mix_norm, w_in, b_in, rnn_conv_w, rnn_conv_b, rg_w_a, rg_b_a, rg_w_x, rg_b_x, rg_lambda, rnn_w_proj, conv_dw_w, conv_dw_b, conv_ln_g, conv_ln_b, conv_w_proj, conv_b_proj, w_out, ffn2_norm, ffn2_w_gu, ffn2_w_down, final_norm, loss_target, m_meta_tokens, m_ffn1_norm, m_ffn1_w_gu, m_ffn1_w_down, m_mix_norm, m_w_in, m_b_in, m_rnn_conv_w, m_rnn_conv_b, m_rg_w_a, m_rg_b_a, m_rg_w_x, m_rg_b_x, m_rg_lambda, m_rnn_w_proj, m_conv_dw_w, m_conv_dw_b, m_conv_ln_g, m_conv_ln_b, m_conv_w_proj, m_conv_b_proj, m_w_out, m_ffn2_norm, m_ffn2_w_gu, m_ffn2_w_down, m_final_norm, v_meta_tokens, v_ffn1_norm, v_ffn1_w_gu, v_ffn1_w_down, v_mix_norm, v_w_in, v_b_in, v_rnn_conv_w, v_rnn_conv_b, v_rg_w_a, v_rg_b_a, v_rg_w_x, v_rg_b_x, v_rg_lambda, v_rnn_w_proj, v_conv_dw_w, v_conv_dw_b, v_conv_ln_g, v_conv_ln_b, v_conv_w_proj, v_conv_b_proj, v_w_out, v_ffn2_norm, v_ffn2_w_gu, v_ffn2_w_down, v_final_norm):
    w = dict(locals())
    seq, d = x.shape[1], x.shape[2]
    n_meta = meta_tokens.shape[0]
    t_real = n_meta + seq
    tp, tm, tmx, tmt = _tiles(t_real)
    fb = ffn1_w_gu.shape[-1]
    wr = ffn1_w_down.shape[1]
    f = N_DEV * wr
    fc = f // FFN_CHUNKS
    nbc = w_in.shape[-1]
    n_in = N_DEV * nbc
    pr = rnn_w_proj.shape[1]
    hd = rg_w_a.shape[-1]
    gr = rg_w_a.shape[2]
    cw = meta_tokens.shape[1]
    k4, k31 = rnn_conv_w.shape[1], conv_dw_w.shape[1]
    assert n_in == 6 * d and 2 * wr == fb and N_HEADS * hd == d and pr * N_DEV == d

    xi, yi, ci = lax.axis_index("x"), lax.axis_index("y"), lax.axis_index("c")
    core = ci.astype(jnp.int32).reshape(1)
    chip = (2 * xi + yi).astype(jnp.int32).reshape(1)
    me_index = (4 * xi + 2 * yi + ci).astype(jnp.int32).reshape(1)

    for nm in ("ffn1_w_gu", "ffn2_w_gu"):
        for pre in ("", "m_", "v_"):
            w[pre + nm] = jnp.swapaxes(w[pre + nm], 1, 2)

    wgut1 = w["ffn1_w_gu"][0].astype(BF16)
    wgut2 = w["ffn2_w_gu"][0].astype(BF16)
    wd1 = ffn1_w_down[0].astype(BF16)
    wd2 = ffn2_w_down[0].astype(BF16)
    win_loc = w_in[0].astype(BF16)
    w3_loc = jnp.concatenate([rnn_w_proj[0], conv_w_proj[0], w_out[0]], axis=0).astype(BF16)
    wg_loc = jnp.stack([rg_w_a[0], rg_w_x[0]]).astype(BF16)
    n_small = n_meta + k4 + k31
    small_rows = -(-n_small // SUBLANES) * SUBLANES
    small_loc = jnp.concatenate([meta_tokens, rnn_conv_w[0], conv_dw_w[0],
                                 jnp.zeros((small_rows - n_small, cw), F32)], axis=0)
    (wgut1_all, wd1_all, wg_all, small_all), h0, tgt = _first_gather(
        [wgut1, wd1, wg_loc, small_loc], 3, x[0], loss_target[0], n_meta, tp)
    wg = wg_all.transpose(1, 2, 0, 3, 4).reshape(2, N_HEADS, hd, hd)
    small_full = small_all.transpose(1, 0, 2).reshape(small_rows, d)
    cw4 = small_full[n_meta:n_meta + k4]
    cw31 = small_full[n_meta + k4:n_meta + k4 + k31]

    wgu1, wdn1 = wgut1_all.reshape(2 * f, d), wd1_all.reshape(f, d)
    (h1, gu1, n1), (win_all, w3_all) = _ffn_fwd(h0, ffn1_norm, wgu1, wdn1, tm, comm=_Gather([win_loc, w3_loc]))
    (h2, proj, n2, xr_s, hs_s, v1_s, ya_s, yb_s), (wgut2_all, wd2_all) = _mixer_fwd(
        h1, mix_norm, b_in, win_all, cw4, rnn_conv_b, wg, rg_b_a, rg_b_x, rg_lambda, cw31, conv_dw_b, conv_ln_g,
        conv_ln_b, conv_b_proj, w3_all, tmx, comm=_Gather([wgut2, wd2]))
    wgu2, wdn2 = wgut2_all.reshape(2 * f, d), wd2_all.reshape(f, d)
    (dh3, gu2, n3, loss_part, dgf), _ = _ffn_fwd(h2, ffn2_norm, wgu2, wdn2, tm,
                                                 loss=(tgt, final_norm.reshape(1, d), n_meta, t_real))

    def d_w_gu(tag, dgu, n_s, comm=None):
        g, extra = _tn_matmul(
            "d_w_gu" + tag, dgu, n_s,
            pl.BlockSpec((None, tmt, fc), lambda b, i: (b // FFN_CHUNKS, i, b % FFN_CHUNKS)),
            pl.BlockSpec((tmt, d), lambda b, i: (i, 0)),
            2 * FFN_CHUNKS, fc, d, tmt, tp, (2 * FFN_CHUNKS, fc, d),
            pl.BlockSpec((None, fc, d), lambda b, i: (b, 0, 0)), (fc, d), comm)
        return g.reshape(N_DEV, fb, d), extra

    def d_w_down(tag, act, df, comm=None):
        g, extra = _tn_matmul(
            "d_w_down" + tag, act, df,
            pl.BlockSpec((tmt, fc), lambda b, i: (i, b)), pl.BlockSpec((tmt, d), lambda b, i: (i, 0)),
            FFN_CHUNKS, fc, d, tmt, tp, (FFN_CHUNKS, fc, d),
            pl.BlockSpec((None, fc, d), lambda b, i: (b, 0, 0)), (fc, d), comm)
        return g.reshape(N_DEV, wr, d), extra

    (dh2, dgu2, act2, df2, dg_ffn2), _ = _ffn_bwd(dh3, h2, gu2, ffn2_norm, wgu2, wdn2, tm)
    g_wgu2, _ = d_w_gu("2", dgu2, n3)
    g_wd2, _ = d_w_down("2", act2, df2)
    (dh1, dproj, x3, y3, yg, sg), (r_wd2, r_wgu2) = _mixer_bwd(
        dh2, h1, mix_norm, proj, xr_s, hs_s, v1_s, ya_s, yb_s, win_all, cw4, wg, rg_b_a, rg_b_x, rg_lambda, cw31,
        conv_ln_g, conv_ln_b, w3_all, tmx, comm=_Scatter([g_wd2, g_wgu2]))
    g_w3, _ = _tn_matmul(
        "d_w_proj3", x3, y3,
        pl.BlockSpec((tmt, d), lambda b, i: (i, b)), pl.BlockSpec((tmt, d), lambda b, i: (i, b)),
        3, d, d, tmt, tp, (N_DEV, 3, pr, d), pl.BlockSpec((N_DEV, None, pr, d), lambda b, i: (0, b, 0, 0)),
        (N_DEV, pr, d))
    g_wg, _ = _tn_matmul(
        "d_w_gates", xr_s, yg,
        pl.BlockSpec((tmt, hd), lambda b, i: (i, b % N_HEADS)), pl.BlockSpec((tmt, hd), lambda b, i: (i, b)),
        2 * N_HEADS, hd, hd, tmt, tp, (N_DEV, 2 * N_HEADS, gr, hd),
        pl.BlockSpec((N_DEV, None, gr, hd), lambda b, i: (0, b, 0, 0)), (N_DEV, gr, hd))
    g_win, (r_w3, r_wg) = _tn_matmul(
        "d_w_in", n2, dproj,
        pl.BlockSpec((tmt, d), lambda b, i: (i, 0)), pl.BlockSpec((tmt, nbc), lambda b, i: (i, b)),
        N_DEV, d, nbc, tmt, tp, (N_DEV, d, nbc), pl.BlockSpec((None, d, nbc), lambda b, i: (b, 0, 0)), (d, nbc),
        comm=_Scatter([g_w3, g_wg]))
    dg_mix = sg[SG_MIX:SG_MIX + 1]
    (dh0, dgu1, act1, df1, dg_ffn1), (r_win,) = _ffn_bwd(dh1, h0, gu1, ffn1_norm, wgu1, wdn1, tm,
                                                         comm=_Scatter([g_win]))
    grad_x = dh0[n_meta:t_real][None]

    rep_rows = [("ffn1_norm", dg_ffn1), ("mix_norm", dg_mix), ("b_in", sg[SG_BIN:SG_BIN + 6]),
                ("rnn_conv_b", sg[SG_CB4:SG_CB4 + 1]), ("rg_b_a", sg[SG_BA:SG_BA + 1]),
                ("rg_b_x", sg[SG_BX:SG_BX + 1]), ("rg_lambda", sg[SG_LAM:SG_LAM + 1]),
                ("conv_dw_b", sg[SG_CB31:SG_CB31 + 1]), ("conv_ln_g", sg[SG_LNG:SG_LNG + 1]),
                ("conv_ln_b", sg[SG_LNB:SG_LNB + 1]), ("conv_b_proj", sg[SG_BCP:SG_BCP + 1]),
                ("ffn2_norm", dg_ffn2), ("final_norm", dgf)]
    col_rows = [("meta_tokens", dh0[:n_meta]), ("rnn_conv_w", sg[SG_CW4:SG_CW4 + k4]),
                ("conv_dw_w", sg[SG_CW31:SG_CW31 + k31])]
    layout, pieces, r0 = [], [], 0
    for nm, part in rep_rows:
        nr = part.shape[0]
        kind = "wide" if nm == "b_in" else "rep"
        as2d = lambda a: a.reshape(1, -1) if a.ndim == 1 else a
        layout.append((kind, r0, nr, as2d(w[nm]), as2d(w["m_" + nm]), as2d(w["v_" + nm])))
        pieces.append(part)
        r0 += nr
    for nm, part in col_rows:
        nr = part.shape[0]
        sq = lambda a: a.reshape(a.shape[-2], a.shape[-1])
        layout.append(("col", r0, nr, sq(w[nm]), sq(w["m_" + nm]), sq(w["v_" + nm])))
        pieces.append(part)
        r0 += nr
    total_rows = -(-(r0 + 1) // SUBLANES) * SUBLANES
    pieces.append(jnp.zeros((total_rows - 1 - r0, d), F32))
    pieces.append(loss_part)
    small_partial = jnp.concatenate(pieces, axis=0)

    g_wd1, (small_partials,) = d_w_down("1", act1, df1, comm=_Bcast(small_partial))
    g_wgu1, (r_wd1,) = d_w_gu("1", dgu1, n1, comm=_Scatter([g_wd1]))

    g_last = g_wgu1.reshape((4, 2) + g_wgu1.shape[1:])
    (from_sibling,) = _pair_exchange([g_last])
    comb_wgu1 = _pair_add(g_last, from_sibling, core)
    (r_wgu1,) = _chip_exchange([comb_wgu1])

    groups = [(g_wd1, r_wd1, me_index, ["ffn1_w_down"]), (comb_wgu1, r_wgu1, chip, ["ffn1_w_gu"]),
              (g_wd2, r_wd2, me_index, ["ffn2_w_down"]), (g_wgu2, r_wgu2, me_index, ["ffn2_w_gu"]),
              (g_win, r_win, me_index, ["w_in"]), (g_w3, r_w3, me_index, ["w_out", "rnn_w_proj", "conv_w_proj"]),
              (g_wg, r_wg, me_index, ["rg_w_a", "rg_w_x"])]
    res = {}
    for own, recv, idx, group in groups:
        outs = _final_adamw(own, recv, idx, [(w[nm], w["m_" + nm], w["v_" + nm]) for nm in group])
        for nm, o in zip(group, outs):
            res[nm] = o
    for nm in ("ffn1_w_gu", "ffn2_w_gu"):
        res[nm] = tuple(jnp.swapaxes(a, 1, 2) for a in res[nm])

    total, small_out = _small_adamw(small_partials, layout, me_index)
    for (nm, _), o in zip(rep_rows + col_rows, small_out):
        res[nm] = tuple(a.reshape(w[nm].shape) for a in o)

    order = ["meta_tokens", "ffn1_norm", "ffn1_w_gu", "ffn1_w_down", "mix_norm", "w_in", "b_in", "rnn_conv_w",
             "rnn_conv_b", "rg_w_a", "rg_b_a", "rg_w_x", "rg_b_x", "rg_lambda", "rnn_w_proj", "conv_dw_w",
             "conv_dw_b", "conv_ln_g", "conv_ln_b", "conv_w_proj", "conv_b_proj", "w_out", "ffn2_norm",
             "ffn2_w_gu", "ffn2_w_down", "final_norm"]
    return (total[total_rows - 1, 0], grad_x, *[res[nm][0] for nm in order], *[res[nm][1] for nm in order],
            *[res[nm][2] for nm in order], *[res[nm][3] for nm in order])
```

```python
import functools
import math

import jax
import jax.numpy as jnp
from jax import lax
from jax.experimental import pallas as pl
from jax.experimental.pallas import tpu as pltpu

F32 = jnp.float32
BF16 = jnp.bfloat16
MESH = pl.DeviceIdType.MESH
N_DEV = 8
N_HEADS = 4
RG_LRU_C = 8.0
EPS = 1e-6
FFN_RES = 0.5
ADAM_LR, ADAM_B1, ADAM_B2, ADAM_EPS, ADAM_WD, ADAM_STEP = 0.001, 0.9, 0.999, 1e-08, 0.01, 10
V7X_VMEM_LIMIT = 56 * 1024 * 1024
CONV4_HALO = 8
CONV31_HALO = 32
SUBLANES = 8
FFN_CHUNKS = 2
GELU_C = math.sqrt(2.0 / math.pi)
GELU_K = 0.044715


def _any():
    return pl.BlockSpec(memory_space=pl.ANY)


def _params(n_grid):
    return pltpu.CompilerParams(dimension_semantics=("arbitrary",) * n_grid, vmem_limit_bytes=V7X_VMEM_LIMIT)


def _nn(a, b):
    return jnp.dot(a, b, preferred_element_type=F32)


def _nt(a, b):
    return lax.dot_general(a, b, (((1,), (1,)), ((), ())), preferred_element_type=F32)


def _tn(a, b):
    return lax.dot_general(a, b, (((0,), (0,)), ((), ())), preferred_element_type=F32)


def _sigmoid(x):
    return 0.5 * jnp.tanh(0.5 * x) + 0.5


def _rowsum(x):
    return jnp.sum(x, axis=0, keepdims=True)


def _rms_fwd(x, g):
    r = lax.rsqrt(jnp.mean(x * x, axis=-1, keepdims=True) + EPS)
    return x * r * g, r


def _rms_bwd(dn, x, r, g):
    xr = x * r
    gy = dn * g
    dx = r * (gy - xr * jnp.mean(gy * xr, axis=-1, keepdims=True))
    return dx, _rowsum(dn * xr)


def _gelu(y):
    t = jnp.tanh(GELU_C * (y + GELU_K * y * y * y))
    return 0.5 * y * (1.0 + t), t


def _gelu_grad(y, t):
    return 0.5 * (1.0 + t) + 0.5 * y * (1.0 - t * t) * GELU_C * (1.0 + 3.0 * GELU_K * y * y)


def _softplus(x):
    return jnp.maximum(x, 0.0) + jnp.log(1.0 + jnp.exp(-jnp.abs(x)))


def _one_minus_exp(z):
    series = -z * (1.0 + 0.5 * z * (1.0 + z * (1.0 / 3.0) * (1.0 + 0.25 * z)))
    return jnp.where(z > -0.05, series, 1.0 - jnp.exp(z))


def _tiles(t_real):
    if t_real > 2048:
        tm = 384
        tp = -(-t_real // tm) * tm
        return tp, tm, tm // 3, tp // 2
    tm = 128
    tp = -(-t_real // tm) * tm
    return tp, tm, tm // 2, tm


def _load_weights(copies, sems):
    cps = [pltpu.make_async_copy(s, d, sems.at[k]) for k, (s, d) in enumerate(copies)]
    for cp in cps:
        cp.start()
    for cp in cps:
        cp.wait()


def _position():
    x, y, c = lax.axis_index("x"), lax.axis_index("y"), lax.axis_index("c")
    chips = [(1 - x, y), (x, 1 - y), (1 - x, 1 - y)]
    return x, y, c, chips


def _slot(p):
    return 4 * p[0] + 2 * p[1] + p[2]


class _Gather:
    def __init__(self, shards):
        self.shards = list(shards)
        self.n = len(self.shards)

    def inputs(self):
        return self.shards

    def out_shape(self):
        return [jax.ShapeDtypeStruct((N_DEV,) + s.shape, s.dtype) for s in self.shards]

    def scratch(self):
        return [pltpu.SemaphoreType.DMA((7 * self.n,)), pltpu.SemaphoreType.DMA((7 * self.n,)),
                pltpu.SemaphoreType.DMA((self.n,))]

    def _plan(self, ins, outs, sems):
        send_sems, recv_sems, local_sems = sems
        x, y, c, chips = _position()
        me, sibling = (x, y, c), (x, y, 1 - c)

        def copy(a, k, block, to, src=None):
            dst = outs[a].at[_slot(block)]
            return pltpu.make_async_remote_copy(
                src_ref=dst if src is None else src, dst_ref=dst,
                send_sem=send_sems.at[7 * a + k], recv_sem=recv_sems.at[7 * a + k],
                device_id=to, device_id_type=MESH)

        mine = [pltpu.make_async_copy(ins[a], outs[a].at[_slot(me)], local_sems.at[a]) for a in range(self.n)]
        first = []
        for a in range(self.n):
            first.append(copy(a, 0, me, sibling, src=ins[a]))
            first += [copy(a, 1 + j, me, (*chip, c), src=ins[a]) for j, chip in enumerate(chips)]
        return copy, mine, first, me, sibling, c, chips

    def start(self, ins, outs, sems):
        _, mine, first, *_ = self._plan(ins, outs, sems)
        for cp in mine + first:
            cp.start()

    def finish(self, ins, outs, sems):
        copy, mine, first, me, sibling, c, chips = self._plan(ins, outs, sems)
        passed = []
        for j, chip in enumerate(chips):
            for a in range(self.n):
                copy(a, 1 + j, (*chip, c), me).wait_recv()
                fwd = copy(a, 4 + j, (*chip, c), sibling)
                fwd.start()
                passed.append(fwd)
        for a in range(self.n):
            copy(a, 0, sibling, me).wait_recv()
            for j, chip in enumerate(chips):
                copy(a, 4 + j, (*chip, 1 - c), me).wait_recv()
        for cp in first + passed:
            cp.wait_send()
        for cp in mine:
            cp.wait()


class _Scatter:
    def __init__(self, grads):
        self.grads = list(grads)
        self.n = len(self.grads)

    def inputs(self):
        return self.grads

    def out_shape(self):
        return [jax.ShapeDtypeStruct((N_DEV - 1,) + g.shape[1:], g.dtype) for g in self.grads]

    def scratch(self):
        return [pltpu.SemaphoreType.DMA((7 * self.n,)), pltpu.SemaphoreType.DMA((7 * self.n,))]

    def _plan(self, ins, outs, sems):
        send_sems, recv_sems = sems
        x, y, c, _ = _position()
        cps = []
        for a in range(self.n):
            for k in range(1, N_DEV):
                peer = (x ^ (k >> 2), y ^ ((k >> 1) & 1), c ^ (k & 1))
                cps.append(pltpu.make_async_remote_copy(
                    src_ref=ins[a].at[_slot(peer)], dst_ref=outs[a].at[k - 1],
                    send_sem=send_sems.at[7 * a + k - 1], recv_sem=recv_sems.at[7 * a + k - 1],
                    device_id=peer, device_id_type=MESH))
        return cps

    def start(self, ins, outs, sems):
        for cp in self._plan(ins, outs, sems):
            cp.start()

    def finish(self, ins, outs, sems):
        for cp in self._plan(ins, outs, sems):
            cp.wait()


def _hosted(inner, n_in, n_out, comm, grid):
    if comm is None:
        return inner
    nc_in, nc_out, ns = len(comm.inputs()), len(comm.out_shape()), len(comm.scratch())

    def body(*refs):
        o0 = n_in + nc_in
        s0 = o0 + n_out + nc_out
        main = refs[:n_in] + refs[o0:o0 + n_out] + refs[s0:len(refs) - ns]
        c_in, c_out, c_sems = refs[n_in:o0], refs[o0 + n_out:s0], refs[len(refs) - ns:]
        ids = [pl.program_id(ax) for ax in range(len(grid))]
        first = functools.reduce(jnp.logical_and, [i == 0 for i in ids])
        last = functools.reduce(jnp.logical_and, [i == g - 1 for i, g in zip(ids, grid)])

        @pl.when(first)
        def _():
            comm.start(c_in, c_out, c_sems)

        inner(*main)

        @pl.when(last)
        def _():
            comm.finish(c_in, c_out, c_sems)

    return body


def _call(inner, name, grid, in_specs, out_specs, out_shape, scratch, args, comm=None):
    n_in, n_out = len(args), len(out_shape)
    body = _hosted(inner, n_in, n_out, comm, grid)
    if comm is not None:
        in_specs = list(in_specs) + [_any()] * len(comm.inputs())
        args = list(args) + comm.inputs()
        out_specs = list(out_specs) + [_any()] * len(comm.out_shape())
        out_shape = list(out_shape) + comm.out_shape()
        scratch = list(scratch) + comm.scratch()
    outs = pl.pallas_call(
        body, name=name, grid=grid, in_specs=list(in_specs), out_specs=list(out_specs), out_shape=list(out_shape),
        scratch_shapes=list(scratch), compiler_params=_params(len(grid)))(*args)
    return list(outs[:n_out]), list(outs[n_out:])


class _Bcast:
    def __init__(self, block):
        self.block = block

    def inputs(self):
        return [self.block]

    def out_shape(self):
        return [jax.ShapeDtypeStruct((N_DEV,) + self.block.shape, self.block.dtype)]

    def scratch(self):
        return [pltpu.SemaphoreType.DMA((N_DEV - 1,)), pltpu.SemaphoreType.DMA((N_DEV - 1,)),
                pltpu.SemaphoreType.DMA((1,))]

    def _plan(self, ins, outs, sems):
        send_sems, recv_sems, local_sem = sems
        x, y, c, _ = _position()
        mine = outs[0].at[_slot((x, y, c))]
        cps = []
        for k in range(1, N_DEV):
            peer = (x ^ (k >> 2), y ^ ((k >> 1) & 1), c ^ (k & 1))
            cps.append(pltpu.make_async_remote_copy(
                src_ref=ins[0], dst_ref=mine, send_sem=send_sems.at[k - 1], recv_sem=recv_sems.at[k - 1],
                device_id=peer, device_id_type=MESH))
        return pltpu.make_async_copy(ins[0], mine, local_sem.at[0]), cps

    def start(self, ins, outs, sems):
        own, cps = self._plan(ins, outs, sems)
        own.start()
        for cp in cps:
            cp.start()

    def finish(self, ins, outs, sems):
        own, cps = self._plan(ins, outs, sems)
        for cp in cps:
            cp.wait()
        own.wait()


def _all_gather(shards):
    comm = _Gather(shards)
    n = comm.n

    def body(*refs):
        comm.start(refs[:n], refs[n:2 * n], refs[2 * n:])
        comm.finish(refs[:n], refs[n:2 * n], refs[2 * n:])

    return pl.pallas_call(
        body, name="weights_all_gather", out_shape=comm.out_shape(),
        in_specs=[_any()] * n, out_specs=[_any()] * n, scratch_shapes=comm.scratch(),
    )(*shards)


def _pair_exchange(grads):
    n = len(grads)

    def body(*refs):
        ins, outs = refs[:n], refs[n:2 * n]
        send_sems, recv_sems = refs[2 * n:]
        x, y, c, _ = _position()
        cps = [pltpu.make_async_remote_copy(
            src_ref=ins[a].at[:, 1 - c], dst_ref=outs[a],
            send_sem=send_sems.at[a], recv_sem=recv_sems.at[a],
            device_id=(x, y, 1 - c), device_id_type=MESH) for a in range(n)]
        for cp in cps:
            cp.start()
        for cp in cps:
            cp.wait()

    return pl.pallas_call(
        body, name="grads_pair_exchange",
        out_shape=[jax.ShapeDtypeStruct((4,) + g.shape[2:], g.dtype) for g in grads],
        in_specs=[_any()] * n, out_specs=[_any()] * n,
        scratch_shapes=[pltpu.SemaphoreType.DMA((n,)), pltpu.SemaphoreType.DMA((n,))],
    )(*grads)


def _chip_exchange(combs):
    n = len(combs)

    def body(*refs):
        ins, outs = refs[:n], refs[n:2 * n]
        send_sems, recv_sems = refs[2 * n:]
        x, y, c, chips = _position()
        cps = []
        for a in range(n):
            for j, (cx, cy) in enumerate(chips):
                cps.append(pltpu.make_async_remote_copy(
                    src_ref=ins[a].at[2 * cx + cy], dst_ref=outs[a].at[j],
                    send_sem=send_sems.at[3 * a + j], recv_sem=recv_sems.at[3 * a + j],
                    device_id=(cx, cy, c), device_id_type=MESH))
        for cp in cps:
            cp.start()
        for cp in cps:
            cp.wait()

    return pl.pallas_call(
        body, name="grads_chip_exchange",
        out_shape=[jax.ShapeDtypeStruct((3,) + g.shape[1:], g.dtype) for g in combs],
        in_specs=[_any()] * n, out_specs=[_any()] * n,
        scratch_shapes=[pltpu.SemaphoreType.DMA((3 * n,)), pltpu.SemaphoreType.DMA((3 * n,))],
    )(*combs)


def _pair_add(grad, recv, core):
    blk = grad.shape[2:]
    zeros = (0,) * len(blk)

    def body(core_ref, g_ref, r_ref, o_ref):
        del core_ref
        o_ref[...] = (g_ref[...].astype(F32) + r_ref[...].astype(F32)).astype(BF16)

    return pl.pallas_call(
        body, name="grads_pair_add",
        out_shape=jax.ShapeDtypeStruct((4,) + blk, BF16),
        grid_spec=pltpu.PrefetchScalarGridSpec(
            num_scalar_prefetch=1, grid=(4,),
            in_specs=[pl.BlockSpec((None, None) + blk, lambda i, cr: (i, cr[0]) + zeros),
                      pl.BlockSpec((None,) + blk, lambda i, cr: (i,) + zeros)],
            out_specs=pl.BlockSpec((None,) + blk, lambda i, cr: (i,) + zeros)),
        compiler_params=_params(1),
    )(core, grad, recv)


def _adamw(w, g, m, v):
    m2 = ADAM_B1 * m + (1.0 - ADAM_B1) * g
    v2 = ADAM_B2 * v + (1.0 - ADAM_B2) * (g * g)
    m_hat = m2 / (1.0 - ADAM_B1 ** ADAM_STEP)
    v_hat = v2 / (1.0 - ADAM_B2 ** ADAM_STEP)
    delta = -ADAM_LR * (m_hat / (jnp.sqrt(v_hat) + ADAM_EPS) + ADAM_WD * w)
    return delta, m2, v2


def _final_adamw(own, recv, idx, parts):
    blk = own.shape[1:]
    n_recv = recv.shape[0]
    n_parts = len(parts)
    per = blk[0] // n_parts if n_parts > 1 else None
    rows = blk[-2]
    n_chunks = 1 if n_parts > 1 else (4 if rows % 64 == 0 and rows >= 512 else (2 if rows % 32 == 0 else 1))
    cblk = blk[:-2] + (rows // n_chunks, blk[-1])
    lead = (0,) * (len(blk) - 2)

    def body(idx_ref, c_ref, r_ref, *refs):
        del idx_ref
        ins, outs = refs[:3 * n_parts], refs[3 * n_parts:]
        g = c_ref[...].astype(F32)
        for k in range(n_recv):
            g = g + r_ref[k].astype(F32)
        for p in range(n_parts):
            w_ref, m_ref, v_ref = ins[3 * p:3 * p + 3]
            if n_parts == 1:
                gp = g
            elif per == 1:
                gp = g[p]
            else:
                gp = g[p * per:(p + 1) * per]
            delta, m2, v2 = _adamw(w_ref[0], gp, m_ref[0], v_ref[0])
            o = outs[4 * p:4 * p + 4]
            o[0][0] = gp
            o[1][0] = delta
            o[2][0] = m2
            o[3][0] = v2

    flat = [a for wmv in parts for a in wmv]

    def part_spec(a):
        shape = a.shape[:-2] + (a.shape[-2] // n_chunks, a.shape[-1])
        return pl.BlockSpec(shape, lambda i, cr, nd=a.ndim: (0,) * (nd - 2) + (i, 0))

    outs = pl.pallas_call(
        body, name="grads_sum_adamw",
        out_shape=[jax.ShapeDtypeStruct(wmv[0].shape, F32) for wmv in parts for _ in range(4)],
        grid_spec=pltpu.PrefetchScalarGridSpec(
            num_scalar_prefetch=1, grid=(n_chunks,),
            in_specs=[pl.BlockSpec((None,) + cblk, lambda i, cr: (cr[0],) + lead + (i, 0)),
                      pl.BlockSpec((n_recv,) + cblk, lambda i, cr: (0,) + lead + (i, 0))]
                     + [part_spec(a) for a in flat],
            out_specs=[part_spec(wmv[0]) for wmv in parts for _ in range(4)]),
        compiler_params=_params(1),
    )(idx, own, recv, *flat)
    return [tuple(outs[4 * p:4 * p + 4]) for p in range(n_parts)]


def _small_adamw(partials, layout, me_index):
    _, rows, d = partials.shape
    n = len(layout)
    cw = d // N_DEV

    def body(me_ref, p_ref, *refs):
        ins, t_ref, outs = refs[:3 * n], refs[3 * n], refs[3 * n + 1:]
        me = me_ref[0]
        total = p_ref[0]
        for j in range(1, N_DEV):
            total = total + p_ref[j]
        t_ref[...] = total
        for e, (kind, r0, nr, _, _, _) in enumerate(layout):
            w_ref, m_ref, v_ref = ins[3 * e:3 * e + 3]
            o = outs[4 * e:4 * e + 4]
            if kind == "rep":
                g = t_ref[r0:r0 + nr, :]
                delta, m2, v2 = _adamw(w_ref[...], g, m_ref[...], v_ref[...])
                for ref, val in zip(o, (g, delta, m2, v2)):
                    ref[...] = val
            elif kind == "wide":
                for q in range(nr):
                    sl = slice(q * d, (q + 1) * d)
                    g = t_ref[r0 + q:r0 + q + 1, :]
                    delta, m2, v2 = _adamw(w_ref[:, sl], g, m_ref[:, sl], v_ref[:, sl])
                    for ref, val in zip(o, (g, delta, m2, v2)):
                        ref[:, sl] = val
            else:
                for j in range(N_DEV):
                    @pl.when(me == j)
                    def _(j=j, o=o, w_ref=w_ref, m_ref=m_ref, v_ref=v_ref, r0=r0, nr=nr):
                        g = t_ref[r0:r0 + nr, j * cw:(j + 1) * cw]
                        delta, m2, v2 = _adamw(w_ref[...], g, m_ref[...], v_ref[...])
                        for ref, val in zip(o, (g, delta, m2, v2)):
                            ref[...] = val

    flat = [a for ent in layout for a in ent[3:]]
    vm = pl.BlockSpec(memory_space=pltpu.VMEM)
    outs = pl.pallas_call(
        body, name="small_adamw",
        out_shape=[jax.ShapeDtypeStruct((rows, d), F32)]
                  + [jax.ShapeDtypeStruct(ent[3].shape, F32) for ent in layout for _ in range(4)],
        in_specs=[pl.BlockSpec(memory_space=pltpu.SMEM), vm] + [vm] * len(flat),
        out_specs=[vm] * (1 + 4 * n),
        compiler_params=pltpu.CompilerParams(vmem_limit_bytes=V7X_VMEM_LIMIT),
    )(me_index, partials, *flat)
    return outs[0], [tuple(outs[1 + 4 * e:5 + 4 * e]) for e in range(n)]


def _ffn_fwd(h, g, wgu, wd, tm, loss=None, comm=None):
    tp, d = h.shape
    f = wd.shape[0]
    fc = f // FFN_CHUNKS
    nt = tp // tm
    with_loss = loss is not None
    if with_loss:
        tgt, gf, n_meta, t_real = loss

    def body(*refs):
        if with_loss:
            (h_ref, g_ref, wgu_hbm, wd_hbm, tgt_ref, gf_ref, out_ref, gu_ref, n_ref, loss_ref, dgf_ref,
             wgu_v, wd_v, sems) = refs
        else:
            h_ref, g_ref, wgu_hbm, wd_hbm, out_ref, gu_ref, n_ref, wgu_v, wd_v, sems = refs
        i = pl.program_id(0)

        @pl.when(i == 0)
        def _():
            _load_weights([(wgu_hbm, wgu_v), (wd_hbm, wd_v)], sems)
            if with_loss:
                loss_ref[...] = jnp.zeros_like(loss_ref)
                dgf_ref[...] = jnp.zeros_like(dgf_ref)

        x = h_ref[...]
        n, _ = _rms_fwd(x, g_ref[...])
        nb = n.astype(BF16)
        n_ref[...] = nb
        acc = jnp.zeros((tm, d), F32)
        for j in range(FFN_CHUNKS):
            cols = slice(j * fc, (j + 1) * fc)
            gate = _nt(nb, wgu_v[pl.ds(j * fc, fc), :])
            up = _nt(nb, wgu_v[pl.ds(f + j * fc, fc), :])
            gu_ref[0, :, cols] = gate.astype(BF16)
            gu_ref[1, :, cols] = up.astype(BF16)
            act = (gate * _sigmoid(gate) * up).astype(BF16)
            acc = acc + _nn(act, wd_v[pl.ds(j * fc, fc), :])
        hn = x + FFN_RES * acc
        if not with_loss:
            out_ref[...] = hn
        else:
            gfv = gf_ref[...]
            r = lax.rsqrt(jnp.mean(hn * hn, axis=-1, keepdims=True) + EPS)
            xr = hn * r
            rows = i * tm + lax.broadcasted_iota(jnp.int32, (tm, 1), 0)
            mask = jnp.logical_and(rows >= n_meta, rows < t_real)
            diff = jnp.where(mask, xr * gfv - tgt_ref[...], 0.0)
            loss_ref[...] += jnp.zeros_like(loss_ref) + 0.5 * jnp.sum(diff * diff) / d
            dy = diff / d
            gy = dy * gfv
            out_ref[...] = r * (gy - xr * jnp.mean(gy * xr, axis=-1, keepdims=True))
            dgf_ref[...] += _rowsum(dy * xr)

    row = pl.BlockSpec((tm, d), lambda i: (i, 0))
    vec = pl.BlockSpec((1, d), lambda i: (0, 0))
    in_specs = [row, vec, _any(), _any()]
    out_shape = [jax.ShapeDtypeStruct((tp, d), F32), jax.ShapeDtypeStruct((2, tp, f), BF16),
                 jax.ShapeDtypeStruct((tp, d), BF16)]
    out_specs = [row, pl.BlockSpec((2, tm, f), lambda i: (0, i, 0)), row]
    args = [h, g, wgu, wd]
    if with_loss:
        in_specs += [row, vec]
        out_shape += [jax.ShapeDtypeStruct((1, d), F32), jax.ShapeDtypeStruct((1, d), F32)]
        out_specs += [vec, vec]
        args += [tgt, gf]
    return _call(body, "ffn_fwd_loss" if with_loss else "ffn_fwd", (nt,), in_specs, out_specs, out_shape,
                 [pltpu.VMEM((2 * f, d), BF16), pltpu.VMEM((f, d), BF16), pltpu.SemaphoreType.DMA((2,))],
                 args, comm)


def _ffn_bwd(dh, h, gu, g, wgu, wd, tm, comm=None):
    tp, d = h.shape
    f = wd.shape[0]
    fc = f // FFN_CHUNKS
    nt = tp // tm

    def body(dh_ref, h_ref, gu_ref, g_ref, wgu_hbm, wd_hbm,
             dhin_ref, dgu_ref, act_ref, df_ref, dg_ref, wgu_v, wd_v, dn_v, sems):
        i, j = pl.program_id(0), pl.program_id(1)

        @pl.when(jnp.logical_and(i == 0, j == 0))
        def _():
            _load_weights([(wgu_hbm, wgu_v), (wd_hbm, wd_v)], sems)
            dg_ref[...] = jnp.zeros_like(dg_ref)

        dfb = (FFN_RES * dh_ref[...]).astype(BF16)

        @pl.when(j == 0)
        def _():
            df_ref[...] = dfb
            dn_v[...] = jnp.zeros_like(dn_v)

        lo = pl.multiple_of(j * fc, 16)
        dact = _nt(dfb, wd_v[pl.ds(lo, fc), :])
        gate = gu_ref[0].astype(F32)
        up = gu_ref[1].astype(F32)
        sg = _sigmoid(gate)
        silu = gate * sg
        act_ref[...] = (silu * up).astype(BF16)
        dgate = (dact * up * (sg * (1.0 + gate * (1.0 - sg)))).astype(BF16)
        dup = (dact * silu).astype(BF16)
        dgu_ref[0] = dgate
        dgu_ref[1] = dup
        dn_v[...] += _nn(dgate, wgu_v[pl.ds(lo, fc), :]) + _nn(dup, wgu_v[pl.ds(pl.multiple_of(f + j * fc, 16), fc), :])

        @pl.when(j == FFN_CHUNKS - 1)
        def _():
            x = h_ref[...]
            r = lax.rsqrt(jnp.mean(x * x, axis=-1, keepdims=True) + EPS)
            dx, dgp = _rms_bwd(dn_v[...], x, r, g_ref[...])
            dhin_ref[...] = dh_ref[...] + dx
            dg_ref[...] += dgp

    row = pl.BlockSpec((tm, d), lambda i, j: (i, 0))
    vec = pl.BlockSpec((1, d), lambda i, j: (0, 0))
    hid2 = pl.BlockSpec((2, tm, fc), lambda i, j: (0, i, j))
    return _call(
        body, "ffn_bwd", (nt, FFN_CHUNKS),
        [row, row, hid2, vec, _any(), _any()],
        [row, hid2, pl.BlockSpec((tm, fc), lambda i, j: (i, j)), row, vec],
        [jax.ShapeDtypeStruct((tp, d), F32), jax.ShapeDtypeStruct((2, tp, f), BF16),
         jax.ShapeDtypeStruct((tp, f), BF16), jax.ShapeDtypeStruct((tp, d), BF16),
         jax.ShapeDtypeStruct((1, d), F32)],
        [pltpu.VMEM((2 * f, d), BF16), pltpu.VMEM((f, d), BF16), pltpu.VMEM((tm, d), F32),
         pltpu.SemaphoreType.DMA((2,))],
        [dh, h, gu, g, wgu, wd], comm)


def _piece_segments(q, d, nb_cols):
    segs = []
    for j in range(N_DEV):
        lo, hi = max(q * d, j * nb_cols), min((q + 1) * d, (j + 1) * nb_cols)
        if lo < hi:
            segs.append((j, lo - q * d, hi - q * d, lo - j * nb_cols, hi - j * nb_cols))
    return segs


def _w3_copies(w3_hbm, rows, w3_v):
    return [(w3_hbm.at[k, pl.ds(q * rows, rows)], w3_v.at[q, pl.ds(k * rows, rows)])
            for q in range(3) for k in range(N_DEV)]


def _gates(xrb, wg_ref, ba, bx, lam, hd):
    pre_r, pre_i = [], []
    for hh in range(N_HEADS):
        xh = xrb[:, hh * hd:(hh + 1) * hd]
        pre_r.append(_nn(xh, wg_ref[0, hh]))
        pre_i.append(_nn(xh, wg_ref[1, hh]))
    r = _sigmoid(jnp.concatenate(pre_r, axis=1) + ba)
    ig = _sigmoid(jnp.concatenate(pre_i, axis=1) + bx)
    sp = _softplus(-lam)
    log_a = -RG_LRU_C * r * sp
    a = jnp.exp(log_a)
    s = jnp.sqrt(_one_minus_exp(2.0 * log_a))
    return r, ig, sp, a, s


def _scan_fwd(a, u, h_prev):
    tm = a.shape[0]
    rows = lax.broadcasted_iota(jnp.int32, a.shape, 0)
    d = 1
    while d < tm:
        if d < SUBLANES:
            keep = rows >= d
            u = jnp.where(keep, a * pltpu.roll(u, d, 0) + u, u)
            a = jnp.where(keep, a * pltpu.roll(a, d, 0), a)
        else:
            u = jnp.concatenate([u[:d], a[d:] * u[:tm - d] + u[d:]], axis=0)
            a = jnp.concatenate([a[:d], a[d:] * a[:tm - d]], axis=0)
        d *= 2
    return u + a * h_prev


def _scan_bwd(b, v, g_next):
    tm = b.shape[0]
    rows = lax.broadcasted_iota(jnp.int32, b.shape, 0)
    d = 1
    while d < tm:
        if d < SUBLANES:
            keep = rows < tm - d
            v = jnp.where(keep, v + b * pltpu.roll(v, tm - d, 0), v)
            b = jnp.where(keep, b * pltpu.roll(b, tm - d, 0), b)
        else:
            v = jnp.concatenate([v[:tm - d] + b[:tm - d] * v[d:], v[tm - d:]], axis=0)
            b = jnp.concatenate([b[:tm - d] * b[d:], b[tm - d:]], axis=0)
        d *= 2
    return v + b * g_next


def _shifted_copies(ext_ref, es_ref, n_rows):
    for s in range(1, SUBLANES):
        es_ref[s, pl.ds(0, n_rows), :] = ext_ref[pl.ds(s, n_rows), :]


def _tap(ext_ref, es_ref, off, tm):
    q, s = divmod(off, SUBLANES)
    if s == 0:
        return ext_ref[pl.ds(SUBLANES * q, tm), :]
    return es_ref[s, pl.ds(SUBLANES * q, tm), :]


def _mixer_fwd(h, g, b_in, win_all, cw4, cb4, wg, ba, bx, lam, cw31, cb31, lng, lnb, bcp, w3_all, tm, comm=None):
    tp, d = h.shape
    nb_cols = win_all.shape[-1]
    n_in = N_DEV * nb_cols
    hd = wg.shape[-1]
    k4, k31 = cw4.shape[0], cw31.shape[0]
    w3_rows = d // N_DEV

    def body(h_ref, g_ref, b_ref, win_hbm, cw4_ref, cb4_ref, wg_ref, ba_ref, bx_ref, lam_ref, cw31_ref, cb31_ref,
             lng_ref, lnb_ref, bcp_ref, w3_hbm,
             h2_ref, p_ref, n_ref, xr_ref, hs_ref, v1_ref, ya_ref, yb_ref,
             win_v, w3_v, ext4, ext31, es31, hcar, sems):
        @pl.when(pl.program_id(0) == 0)
        def _():
            _load_weights([(win_hbm, win_v)] + _w3_copies(w3_hbm, w3_rows, w3_v), sems)
            ext4[pl.ds(0, CONV4_HALO), :] = jnp.zeros((CONV4_HALO, d), F32)
            ext31[pl.ds(0, CONV31_HALO), :] = jnp.zeros((CONV31_HALO, d), F32)
            hcar[...] = jnp.zeros_like(hcar)

        n, _ = _rms_fwd(h_ref[...], g_ref[...])
        nb = n.astype(BF16)
        n_ref[...] = nb

        def piece(q):
            parts = [_nn(nb, win_v[j, :, bl:bh]) for j, _, _, bl, bh in _piece_segments(q, d, nb_cols)]
            pq = (jnp.concatenate(parts, axis=1) + b_ref[:, q * d:(q + 1) * d]).astype(BF16)
            p_ref[:, q * d:(q + 1) * d] = pq
            return pq.astype(F32)

        x_rnn, y_rnn, glu_v, glu_g, gate_a, gate_b = [piece(q) for q in range(6)]

        ext4[pl.ds(CONV4_HALO, tm), :] = x_rnn
        xr = cb4_ref[...] + jnp.zeros((tm, d), F32)
        for k in range(k4):
            xr = xr + cw4_ref[k:k + 1, :] * ext4[pl.ds(CONV4_HALO - (k4 - 1) + k, tm), :]
        ext4[pl.ds(0, CONV4_HALO), :] = ext4[pl.ds(tm, CONV4_HALO), :]
        xrb = xr.astype(BF16)
        xr_ref[...] = xrb
        xr = xrb.astype(F32)
        _, ig, _, a, s = _gates(xrb, wg_ref, ba_ref[...], bx_ref[...], lam_ref[...], hd)
        hseq = _scan_fwd(a, s * (ig * xr), hcar[0:1, :])
        hcar[0:1, :] = hseq[tm - 1:tm, :]
        hs_ref[...] = hseq.astype(BF16)
        gl, _ = _gelu(y_rnn)
        ya = _nn((hseq * gl).astype(BF16), w3_v[0])
        ya_ref[...] = ya.astype(BF16)

        ext31[pl.ds(CONV31_HALO, tm), :] = glu_v * _sigmoid(glu_g)
        _shifted_copies(ext31, es31, tm + CONV31_HALO - SUBLANES)
        v1 = cb31_ref[...] + jnp.zeros((tm, d), F32)
        for k in range(k31):
            v1 = v1 + cw31_ref[k:k + 1, :] * _tap(ext31, es31, CONV31_HALO - (k31 - 1) + k, tm)
        ext31[pl.ds(0, CONV31_HALO), :] = ext31[pl.ds(tm, CONV31_HALO), :]
        v1b = v1.astype(BF16)
        v1_ref[...] = v1b
        v1 = v1b.astype(F32)
        xc = v1 - jnp.mean(v1, axis=-1, keepdims=True)
        rstd = lax.rsqrt(jnp.mean(xc * xc, axis=-1, keepdims=True) + EPS)
        v2 = xc * rstd * lng_ref[...] + lnb_ref[...]
        yb = _nn((v2 * _sigmoid(v2)).astype(BF16), w3_v[1]) + bcp_ref[...]
        yb_ref[...] = yb.astype(BF16)

        merged = _sigmoid(gate_a) * ya + _sigmoid(gate_b) * yb
        h2_ref[...] = h_ref[...] + _nn(merged.astype(BF16), w3_v[2])

    row = pl.BlockSpec((tm, d), lambda i: (i, 0))
    wide = pl.BlockSpec((tm, n_in), lambda i: (i, 0))
    full = lambda a: pl.BlockSpec(a.shape, lambda i, nd=a.ndim: (0,) * nd)
    smalls = [cw4, cb4, wg, ba, bx, lam, cw31, cb31, lng, lnb, bcp]
    return _call(
        body, "mixer_fwd", (tp // tm,),
        [row, full(g), full(b_in), _any()] + [full(a) for a in smalls] + [_any()],
        [row, wide] + [row] * 6,
        [jax.ShapeDtypeStruct((tp, d), F32), jax.ShapeDtypeStruct((tp, n_in), BF16)]
        + [jax.ShapeDtypeStruct((tp, d), BF16)] * 6,
        [pltpu.VMEM(win_all.shape, BF16),
         pltpu.VMEM((3, d, d), BF16),
         pltpu.VMEM((tm + CONV4_HALO, d), F32),
         pltpu.VMEM((tm + CONV31_HALO, d), F32),
         pltpu.VMEM((SUBLANES, tm + CONV31_HALO, d), F32),
         pltpu.VMEM((SUBLANES, d), F32),
         pltpu.SemaphoreType.DMA((1 + 3 * N_DEV,))],
        [h, g, b_in, win_all, *smalls, w3_all], comm)


SG_BIN, SG_CW4, SG_CB4, SG_BA, SG_BX, SG_LAM, SG_CB31, SG_LNG, SG_LNB, SG_BCP, SG_MIX, SG_CW31 = 0, 6, 10, 11, 12, 13, 14, 15, 16, 17, 18, 19


def _mixer_bwd(dh2, h, g, proj, xr_s, hs_s, v1_s, ya_s, yb_s, win_all, cw4, wg, ba, bx, lam, cw31, lng, lnb, w3_all, tm,
               comm=None):
    tp, d = dh2.shape
    nb_cols = win_all.shape[-1]
    n_in = proj.shape[1]
    hd = wg.shape[-1]
    k4, k31 = cw4.shape[0], cw31.shape[0]
    nt = tp // tm
    w3_rows = d // N_DEV
    sg_rows = -(-(SG_CW31 + k31) // SUBLANES) * SUBLANES
    halo_rows = 16
    per = tm // halo_rows

    def body(dh_ref, h_ref, g_ref, p_ref, xr_ref, hs_ref, hh_ref, v1_ref, ya_ref, yb_ref, win_hbm,
             cw4_ref, wg_ref, ba_ref, bx_ref, lam_ref, cw31_ref, lng_ref, lnb_ref, w3_hbm,
             dh1_ref, dp_ref, x3_ref, y3_ref, yg_ref, sg_ref,
             win_v, w3_v, extd4, extd31, es31, gcar, sems):
        i = pl.program_id(0)
        tile = nt - 1 - i

        @pl.when(i == 0)
        def _():
            _load_weights([(win_hbm, win_v)] + _w3_copies(w3_hbm, w3_rows, w3_v), sems)
            extd4[pl.ds(tm, CONV4_HALO), :] = jnp.zeros((CONV4_HALO, d), F32)
            extd31[pl.ds(tm, CONV31_HALO), :] = jnp.zeros((CONV31_HALO, d), F32)
            gcar[...] = jnp.zeros_like(gcar)
            sg_ref[...] = jnp.zeros_like(sg_ref)

        def acc(row, val):
            sg_ref[row:row + 1, :] += _rowsum(val)

        rows = lax.broadcasted_iota(jnp.int32, (tm, d), 0)
        x_rnn = p_ref[:, 0:d].astype(F32)
        y_rnn = p_ref[:, d:2 * d].astype(F32)
        glu_v = p_ref[:, 2 * d:3 * d].astype(F32)
        glu_g = p_ref[:, 3 * d:4 * d].astype(F32)
        sga = _sigmoid(p_ref[:, 4 * d:5 * d].astype(F32))
        sgb = _sigmoid(p_ref[:, 5 * d:6 * d].astype(F32))
        ya = ya_ref[...].astype(F32)
        yb = yb_ref[...].astype(F32)

        dmob = dh_ref[...].astype(BF16)
        dmerged = _nt(dmob, w3_v[2])
        x3_ref[:, 0:d] = (sga * ya + sgb * yb).astype(BF16)
        y3_ref[:, 0:d] = dmob
        dya = sga * dmerged
        dyb = sgb * dmerged
        dn_parts = []

        def emit(q, val):
            vb = val.astype(BF16)
            dp_ref[:, q * d:(q + 1) * d] = vb
            acc(SG_BIN + q, val)
            for j, lo, hi, bl, bh in _piece_segments(q, d, nb_cols):
                term = _nt(vb[:, lo:hi], win_v[j, :, bl:bh])
                dn_parts[:] = [term if not dn_parts else dn_parts[0] + term]

        emit(4, dmerged * ya * sga * (1.0 - sga))
        emit(5, dmerged * yb * sgb * (1.0 - sgb))

        dyab = dya.astype(BF16)
        y3_ref[:, d:2 * d] = dyab
        dza = _nt(dyab, w3_v[0])
        hsv = hs_ref[...].astype(F32)
        gl, th = _gelu(y_rnn)
        x3_ref[:, d:2 * d] = (hsv * gl).astype(BF16)
        emit(1, dza * hsv * _gelu_grad(y_rnn, th))
        dhs = dza * gl
        xrb = xr_ref[...]
        xr = xrb.astype(F32)
        lam_v = lam_ref[...]
        r, ig, sp, a, s = _gates(xrb, wg_ref, ba_ref[...], bx_ref[...], lam_v, hd)
        b = jnp.where(rows == tm - 1, gcar[1:2, :], pltpu.roll(a, tm - 1, 0))
        big_g = _scan_bwd(b, dhs, gcar[0:1, :])
        gcar[0:1, :] = big_g[0:1, :]
        gcar[1:2, :] = a[0:1, :]
        h_before = jnp.where(tile > 0, hh_ref[halo_rows - 1:halo_rows, :].astype(F32), 0.0)
        h_prev = jnp.where(rows == 0, h_before, pltpu.roll(hsv, 1, 0))
        ds = big_g * ig * xr
        dla = big_g * h_prev * a - ds * (a * a) / jnp.maximum(s, 1e-20)
        acc(SG_LAM, dla * r * (RG_LRU_C * _sigmoid(-lam_v)))
        dpr = dla * (-RG_LRU_C * sp) * r * (1.0 - r)
        dpi = big_g * s * xr * ig * (1.0 - ig)
        acc(SG_BA, dpr)
        acc(SG_BX, dpi)
        dprb = dpr.astype(BF16)
        dpib = dpi.astype(BF16)
        yg_ref[:, 0:d] = dprb
        yg_ref[:, d:2 * d] = dpib
        back = []
        for hh in range(N_HEADS):
            sl = slice(hh * hd, (hh + 1) * hd)
            back.append(_nt(dprb[:, sl], wg_ref[0, hh]) + _nt(dpib[:, sl], wg_ref[1, hh]))
        dxr = big_g * s * ig + jnp.concatenate(back, axis=1)
        acc(SG_CB4, dxr)
        extd4[pl.ds(0, tm), :] = dxr
        dx_rnn = jnp.zeros((tm, d), F32)
        for k in range(k4):
            term = extd4[pl.ds(k4 - 1 - k, tm), :]
            dx_rnn = dx_rnn + cw4_ref[k:k + 1, :] * term
            acc(SG_CW4 + k, x_rnn * term)
        extd4[pl.ds(tm, CONV4_HALO), :] = extd4[pl.ds(0, CONV4_HALO), :]
        emit(0, dx_rnn)

        dybb = dyb.astype(BF16)
        y3_ref[:, 2 * d:3 * d] = dybb
        acc(SG_BCP, dyb)
        dv3 = _nt(dybb, w3_v[1])
        v1 = v1_ref[...].astype(F32)
        xc = v1 - jnp.mean(v1, axis=-1, keepdims=True)
        rstd = lax.rsqrt(jnp.mean(xc * xc, axis=-1, keepdims=True) + EPS)
        xhat = xc * rstd
        lng_v = lng_ref[...]
        v2 = xhat * lng_v + lnb_ref[...]
        s2 = _sigmoid(v2)
        x3_ref[:, 2 * d:3 * d] = (v2 * s2).astype(BF16)
        dv2 = dv3 * (s2 * (1.0 + v2 * (1.0 - s2)))
        acc(SG_LNG, dv2 * xhat)
        acc(SG_LNB, dv2)
        dxh = dv2 * lng_v
        dv1 = rstd * (dxh - jnp.mean(dxh, axis=-1, keepdims=True)
                      - xhat * jnp.mean(dxh * xhat, axis=-1, keepdims=True))
        acc(SG_CB31, dv1)
        extd31[pl.ds(0, tm), :] = dv1
        _shifted_copies(extd31, es31, tm + CONV31_HALO - SUBLANES)
        sgg = _sigmoid(glu_g)
        v0 = glu_v * sgg
        dv0 = jnp.zeros((tm, d), F32)
        for k in range(k31):
            term = _tap(extd31, es31, k31 - 1 - k, tm)
            dv0 = dv0 + cw31_ref[k:k + 1, :] * term
            acc(SG_CW31 + k, v0 * term)
        extd31[pl.ds(tm, CONV31_HALO), :] = extd31[pl.ds(0, CONV31_HALO), :]
        emit(2, dv0 * sgg)
        emit(3, dv0 * glu_v * sgg * (1.0 - sgg))

        dn = dn_parts[0]
        x = h_ref[...]
        rr = lax.rsqrt(jnp.mean(x * x, axis=-1, keepdims=True) + EPS)
        dx, dgp = _rms_bwd(dn, x, rr, g_ref[...])
        dh1_ref[...] = dh_ref[...] + dx
        sg_ref[SG_MIX:SG_MIX + 1, :] += dgp

    rev = lambda i: (nt - 1 - i, 0)
    row = pl.BlockSpec((tm, d), rev)
    wide = pl.BlockSpec((tm, n_in), rev)
    full = lambda a: pl.BlockSpec(a.shape, lambda i, nd=a.ndim: (0,) * nd)
    halo = pl.BlockSpec((halo_rows, d), lambda i: (jnp.maximum((nt - 1 - i) * per - 1, 0), 0))
    smalls = [cw4, wg, ba, bx, lam, cw31, lng, lnb]
    return _call(
        body, "mixer_bwd", (nt,),
        [row, row, full(g), wide, row, row, halo, row, row, row, _any()]
        + [full(a) for a in smalls] + [_any()],
        [row, wide, pl.BlockSpec((tm, 3 * d), rev), pl.BlockSpec((tm, 3 * d), rev),
         pl.BlockSpec((tm, 2 * d), rev), pl.BlockSpec((sg_rows, d), lambda i: (0, 0))],
        [jax.ShapeDtypeStruct((tp, d), F32), jax.ShapeDtypeStruct((tp, n_in), BF16),
         jax.ShapeDtypeStruct((tp, 3 * d), BF16), jax.ShapeDtypeStruct((tp, 3 * d), BF16),
         jax.ShapeDtypeStruct((tp, 2 * d), BF16), jax.ShapeDtypeStruct((sg_rows, d), F32)],
        [pltpu.VMEM(win_all.shape, BF16),
         pltpu.VMEM((3, d, d), BF16),
         pltpu.VMEM((tm + CONV4_HALO, d), F32),
         pltpu.VMEM((tm + CONV31_HALO, d), F32),
         pltpu.VMEM((SUBLANES, tm + CONV31_HALO, d), F32),
         pltpu.VMEM((SUBLANES, d), F32),
         pltpu.SemaphoreType.DMA((1 + 3 * N_DEV,))],
        [dh2, h, g, proj, xr_s, hs_s, hs_s, v1_s, ya_s, yb_s, win_all, *smalls, w3_all], comm)


def _tn_matmul(name, x, y, x_spec, y_spec, n_blocks, kb, nb, tm, tp, out_shape, out_spec, out_view, comm=None):
    nt = tp // tm

    def body(x_ref, y_ref, o_ref, acc):
        i = pl.program_id(1)

        @pl.when(i == 0)
        def _():
            acc[...] = jnp.zeros_like(acc)

        acc[...] += _tn(x_ref[...], y_ref[...])

        @pl.when(i == nt - 1)
        def _():
            o_ref[...] = acc[...].astype(BF16).reshape(out_view)

    outs, extra = _call(body, name, (n_blocks, nt), [x_spec, y_spec], [out_spec],
                        [jax.ShapeDtypeStruct(out_shape, BF16)], [pltpu.VMEM((kb, nb), F32)], [x, y], comm)
    return outs[0], extra


def kernel(x, meta_tokens, ffn1_norm, ffn1_w_gu, ffn1_w_down, mix_norm, w_in, b_in, rnn_conv_w, rnn_conv_b, rg_w_a, rg_b_a, rg_w_x, rg_b_x, rg_lambda, rnn_w_proj, conv_dw_w, conv_dw_b, conv_ln_g, conv_ln_b, conv_w_proj, conv_b_proj, w_out, ffn2_norm, ffn2_w_gu, ffn2_w_down, final_norm, loss_target, m_meta_tokens, m_ffn1_norm, m_ffn1_w_gu, m_ffn1_w_down, m_mix_norm, m_w_in, m_b_in, m_rnn_conv_w, m_rnn_conv_b, m_rg_w_a, m_rg_b_a, m_rg_w_x, m_rg_b_x, m_rg_lambda, m_rnn_w_proj, m_conv_dw_w, m_conv_dw_b, m_conv_ln_g, m_conv_ln_b, m_conv_w_proj, m_conv_b_proj, m_w_out, m_ffn2_norm, m_ffn2_w_gu, m_ffn2_w_down, m_final_norm, v_meta_tokens, v_ffn1_norm, v_ffn1_w_gu, v_ffn1_w_down, v_mix_norm, v_w_in, v_b_in, v_rnn_conv_w, v_rnn_conv_b, v_rg_w_a, v_rg_b_a, v_rg_w_x, v_rg_b_x, v_rg_lambda, v_rnn_w_proj, v_conv_dw_w, v_conv_dw_b, v_conv_ln_g, v_conv_ln_b, v_conv_w_proj, v_conv_b_proj, v_w_out, v_ffn2_norm, v_ffn2_w_gu, v_ffn2_w_down, v_final_norm):
    w = dict(locals())
    seq, d = x.shape[1], x.shape[2]
    n_meta = meta_tokens.shape[0]
    t_real = n_meta + seq
    tp, tm, tmx, tmt = _tiles(t_real)
    fb = ffn1_w_gu.shape[-1]
    wr = ffn1_w_down.shape[1]
    f = N_DEV * wr
    fc = f // FFN_CHUNKS
    nbc = w_in.shape[-1]
    n_in = N_DEV * nbc
    pr = rnn_w_proj.shape[1]
    hd = rg_w_a.shape[-1]
    gr = rg_w_a.shape[2]
    cw = meta_tokens.shape[1]
    k4, k31 = rnn_conv_w.shape[1], conv_dw_w.shape[1]
    assert n_in == 6 * d and 2 * wr == fb and N_HEADS * hd == d and pr * N_DEV == d

    xi, yi, ci = lax.axis_index("x"), lax.axis_index("y"), lax.axis_index("c")
    core = ci.astype(jnp.int32).reshape(1)
    chip = (2 * xi + yi).astype(jnp.int32).reshape(1)
    me_index = (4 * xi + 2 * yi + ci).astype(jnp.int32).reshape(1)

    for nm in ("ffn1_w_gu", "ffn2_w_gu"):
        for pre in ("", "m_", "v_"):
            w[pre + nm] = jnp.swapaxes(w[pre + nm], 1, 2)

    wgut1 = w["ffn1_w_gu"][0].astype(BF16)
    wgut2 = w["ffn2_w_gu"][0].astype(BF16)
    wd1 = ffn1_w_down[0].astype(BF16)
    wd2 = ffn2_w_down[0].astype(BF16)
    win_loc = w_in[0].astype(BF16)
    w3_loc = jnp.concatenate([rnn_w_proj[0], conv_w_proj[0], w_out[0]], axis=0).astype(BF16)
    wg_loc = jnp.stack([rg_w_a[0], rg_w_x[0]]).astype(BF16)
    n_small = n_meta + k4 + k31
    small_rows = -(-n_small // SUBLANES) * SUBLANES
    small_loc = jnp.concatenate([meta_tokens, rnn_conv_w[0], conv_dw_w[0],
                                 jnp.zeros((small_rows - n_small, cw), F32)], axis=0)
    wgut1_all, wd1_all, wg_all, small_all = _all_gather([wgut1, wd1, wg_loc, small_loc])
    wg = wg_all.transpose(1, 2, 0, 3, 4).reshape(2, N_HEADS, hd, hd)
    small_full = small_all.transpose(1, 0, 2).reshape(small_rows, d)
    meta_full = small_full[:n_meta]
    cw4 = small_full[n_meta:n_meta + k4]
    cw31 = small_full[n_meta + k4:n_meta + k4 + k31]

    pad = jnp.zeros((tp - t_real, d), F32)
    h0 = jnp.concatenate([meta_full, x[0], pad], axis=0)
    tgt = jnp.concatenate([jnp.zeros((n_meta, d), F32), loss_target[0], pad], axis=0)
    wgu1, wdn1 = wgut1_all.reshape(2 * f, d), wd1_all.reshape(f, d)
    (h1, gu1, n1), (win_all, w3_all) = _ffn_fwd(h0, ffn1_norm, wgu1, wdn1, tm, comm=_Gather([win_loc, w3_loc]))
    (h2, proj, n2, xr_s, hs_s, v1_s, ya_s, yb_s), (wgut2_all, wd2_all) = _mixer_fwd(
        h1, mix_norm, b_in, win_all, cw4, rnn_conv_b, wg, rg_b_a, rg_b_x, rg_lambda, cw31, conv_dw_b, conv_ln_g,
        conv_ln_b, conv_b_proj, w3_all, tmx, comm=_Gather([wgut2, wd2]))
    wgu2, wdn2 = wgut2_all.reshape(2 * f, d), wd2_all.reshape(f, d)
    (dh3, gu2, n3, loss_part, dgf), _ = _ffn_fwd(h2, ffn2_norm, wgu2, wdn2, tm,
                                                 loss=(tgt, final_norm.reshape(1, d), n_meta, t_real))

    def d_w_gu(tag, dgu, n_s, comm=None):
        g, extra = _tn_matmul(
            "d_w_gu" + tag, dgu, n_s,
            pl.BlockSpec((None, tmt, fc), lambda b, i: (b // FFN_CHUNKS, i, b % FFN_CHUNKS)),
            pl.BlockSpec((tmt, d), lambda b, i: (i, 0)),
            2 * FFN_CHUNKS, fc, d, tmt, tp, (2 * FFN_CHUNKS, fc, d),
            pl.BlockSpec((None, fc, d), lambda b, i: (b, 0, 0)), (fc, d), comm)
        return g.reshape(N_DEV, fb, d), extra

    def d_w_down(tag, act, df, comm=None):
        g, extra = _tn_matmul(
            "d_w_down" + tag, act, df,
            pl.BlockSpec((tmt, fc), lambda b, i: (i, b)), pl.BlockSpec((tmt, d), lambda b, i: (i, 0)),
            FFN_CHUNKS, fc, d, tmt, tp, (FFN_CHUNKS, fc, d),
            pl.BlockSpec((None, fc, d), lambda b, i: (b, 0, 0)), (fc, d), comm)
        return g.reshape(N_DEV, wr, d), extra

    (dh2, dgu2, act2, df2, dg_ffn2), _ = _ffn_bwd(dh3, h2, gu2, ffn2_norm, wgu2, wdn2, tm)
    g_wgu2, _ = d_w_gu("2", dgu2, n3)
    g_wd2, _ = d_w_down("2", act2, df2)
    (dh1, dproj, x3, y3, yg, sg), (r_wd2, r_wgu2) = _mixer_bwd(
        dh2, h1, mix_norm, proj, xr_s, hs_s, v1_s, ya_s, yb_s, win_all, cw4, wg, rg_b_a, rg_b_x, rg_lambda, cw31,
        conv_ln_g, conv_ln_b, w3_all, tmx, comm=_Scatter([g_wd2, g_wgu2]))
    g_w3, _ = _tn_matmul(
        "d_w_proj3", x3, y3,
        pl.BlockSpec((tmt, d), lambda b, i: (i, b)), pl.BlockSpec((tmt, d), lambda b, i: (i, b)),
        3, d, d, tmt, tp, (N_DEV, 3, pr, d), pl.BlockSpec((N_DEV, None, pr, d), lambda b, i: (0, b, 0, 0)),
        (N_DEV, pr, d))
    g_wg, _ = _tn_matmul(
        "d_w_gates", xr_s, yg,
        pl.BlockSpec((tmt, hd), lambda b, i: (i, b % N_HEADS)), pl.BlockSpec((tmt, hd), lambda b, i: (i, b)),
        2 * N_HEADS, hd, hd, tmt, tp, (N_DEV, 2 * N_HEADS, gr, hd),
        pl.BlockSpec((N_DEV, None, gr, hd), lambda b, i: (0, b, 0, 0)), (N_DEV, gr, hd))
    g_win, (r_w3, r_wg) = _tn_matmul(
        "d_w_in", n2, dproj,
        pl.BlockSpec((tmt, d), lambda b, i: (i, 0)), pl.BlockSpec((tmt, nbc), lambda b, i: (i, b)),
        N_DEV, d, nbc, tmt, tp, (N_DEV, d, nbc), pl.BlockSpec((None, d, nbc), lambda b, i: (b, 0, 0)), (d, nbc),
        comm=_Scatter([g_w3, g_wg]))
    dg_mix = sg[SG_MIX:SG_MIX + 1]
    (dh0, dgu1, act1, df1, dg_ffn1), (r_win,) = _ffn_bwd(dh1, h0, gu1, ffn1_norm, wgu1, wdn1, tm,
                                                         comm=_Scatter([g_win]))
    grad_x = dh0[n_meta:t_real][None]

    rep_rows = [("ffn1_norm", dg_ffn1), ("mix_norm", dg_mix), ("b_in", sg[SG_BIN:SG_BIN + 6]),
                ("rnn_conv_b", sg[SG_CB4:SG_CB4 + 1]), ("rg_b_a", sg[SG_BA:SG_BA + 1]),
                ("rg_b_x", sg[SG_BX:SG_BX + 1]), ("rg_lambda", sg[SG_LAM:SG_LAM + 1]),
                ("conv_dw_b", sg[SG_CB31:SG_CB31 + 1]), ("conv_ln_g", sg[SG_LNG:SG_LNG + 1]),
                ("conv_ln_b", sg[SG_LNB:SG_LNB + 1]), ("conv_b_proj", sg[SG_BCP:SG_BCP + 1]),
                ("ffn2_norm", dg_ffn2), ("final_norm", dgf)]
    col_rows = [("meta_tokens", dh0[:n_meta]), ("rnn_conv_w", sg[SG_CW4:SG_CW4 + k4]),
                ("conv_dw_w", sg[SG_CW31:SG_CW31 + k31])]
    layout, pieces, r0 = [], [], 0
    for nm, part in rep_rows:
        nr = part.shape[0]
        kind = "wide" if nm == "b_in" else "rep"
        as2d = lambda a: a.reshape(1, -1) if a.ndim == 1 else a
        layout.append((kind, r0, nr, as2d(w[nm]), as2d(w["m_" + nm]), as2d(w["v_" + nm])))
        pieces.append(part)
        r0 += nr
    for nm, part in col_rows:
        nr = part.shape[0]
        sq = lambda a: a.reshape(a.shape[-2], a.shape[-1])
        layout.append(("col", r0, nr, sq(w[nm]), sq(w["m_" + nm]), sq(w["v_" + nm])))
        pieces.append(part)
        r0 += nr
    total_rows = -(-(r0 + 1) // SUBLANES) * SUBLANES
    pieces.append(jnp.zeros((total_rows - 1 - r0, d), F32))
    pieces.append(loss_part)
    small_partial = jnp.concatenate(pieces, axis=0)

    g_wd1, (small_partials,) = d_w_down("1", act1, df1, comm=_Bcast(small_partial))
    g_wgu1, (r_wd1,) = d_w_gu("1", dgu1, n1, comm=_Scatter([g_wd1]))

    g_last = g_wgu1.reshape((4, 2) + g_wgu1.shape[1:])
    (from_sibling,) = _pair_exchange([g_last])
    comb_wgu1 = _pair_add(g_last, from_sibling, core)
    (r_wgu1,) = _chip_exchange([comb_wgu1])

    groups = [(g_wd1, r_wd1, me_index, ["ffn1_w_down"]), (comb_wgu1, r_wgu1, chip, ["ffn1_w_gu"]),
              (g_wd2, r_wd2, me_index, ["ffn2_w_down"]), (g_wgu2, r_wgu2, me_index, ["ffn2_w_gu"]),
              (g_win, r_win, me_index, ["w_in"]), (g_w3, r_w3, me_index, ["w_out", "rnn_w_proj", "conv_w_proj"]),
              (g_wg, r_wg, me_index, ["rg_w_a", "rg_w_x"])]
    res = {}
    for own, recv, idx, group in groups:
        outs = _final_adamw(own, recv, idx, [(w[nm], w["m_" + nm], w["v_" + nm]) for nm in group])
        for nm, o in zip(group, outs):
            res[nm] = o
    for nm in ("ffn1_w_gu", "ffn2_w_gu"):
        res[nm] = tuple(jnp.swapaxes(a, 1, 2) for a in res[nm])

    total, small_out = _small_adamw(small_partials, layout, me_index)
    for (nm, _), o in zip(rep_rows + col_rows, small_out):
        res[nm] = tuple(a.reshape(w[nm].shape) for a in o)

    order = ["meta_tokens", "ffn1_norm", "ffn1_w_gu", "ffn1_w_down", "mix_norm", "w_in", "b_in", "rnn_conv_w",
             "rnn_conv_b", "rg_w_a", "rg_b_a", "rg_w_x", "rg_b_x", "rg_lambda", "rnn_w_proj", "conv_dw_w",
             "conv_dw_b", "conv_ln_g", "conv_ln_b", "conv_w_proj", "conv_b_proj", "w_out", "ffn2_norm",
             "ffn2_w_gu", "ffn2_w_down", "final_norm"]
    return (total[total_rows - 1, 0], grad_x, *[res[nm][0] for nm in order], *[res[nm][1] for nm in order],
            *[res[nm][2] for nm in order], *[res[nm][3] for nm in order])
```

```python
import functools
import math

import jax
import jax.numpy as jnp
from jax import lax
from jax.experimental import pallas as pl
from jax.experimental.pallas import tpu as pltpu

F32 = jnp.float32
BF16 = jnp.bfloat16
MESH = pl.DeviceIdType.MESH
N_DEV = 8
N_HEADS = 4
RG_LRU_C = 8.0
EPS = 1e-6
FFN_RES = 0.5
ADAM_LR, ADAM_B1, ADAM_B2, ADAM_EPS, ADAM_WD, ADAM_STEP = 0.001, 0.9, 0.999, 1e-08, 0.01, 10
V7X_VMEM_LIMIT = 56 * 1024 * 1024
CONV4_HALO = 8
CONV31_HALO = 32
SUBLANES = 8
FFN_CHUNKS = 2
GELU_C = math.sqrt(2.0 / math.pi)
GELU_K = 0.044715


def _any():
    return pl.BlockSpec(memory_space=pl.ANY)


def _params(n_grid):
    return pltpu.CompilerParams(dimension_semantics=("arbitrary",) * n_grid, vmem_limit_bytes=V7X_VMEM_LIMIT)


def _nn(a, b):
    return jnp.dot(a, b, preferred_element_type=F32)


def _nt(a, b):
    return lax.dot_general(a, b, (((1,), (1,)), ((), ())), preferred_element_type=F32)


def _tn(a, b):
    return lax.dot_general(a, b, (((0,), (0,)), ((), ())), preferred_element_type=F32)


def _sigmoid(x):
    return 0.5 * jnp.tanh(0.5 * x) + 0.5


def _rowsum(x):
    return jnp.sum(x, axis=0, keepdims=True)


def _rms_fwd(x, g):
    r = lax.rsqrt(jnp.mean(x * x, axis=-1, keepdims=True) + EPS)
    return x * r * g, r


def _rms_bwd(dn, x, r, g):
    xr = x * r
    gy = dn * g
    dx = r * (gy - xr * jnp.mean(gy * xr, axis=-1, keepdims=True))
    return dx, _rowsum(dn * xr)


def _gelu(y):
    t = jnp.tanh(GELU_C * (y + GELU_K * y * y * y))
    return 0.5 * y * (1.0 + t), t


def _gelu_grad(y, t):
    return 0.5 * (1.0 + t) + 0.5 * y * (1.0 - t * t) * GELU_C * (1.0 + 3.0 * GELU_K * y * y)


def _softplus(x):
    return jnp.maximum(x, 0.0) + jnp.log(1.0 + jnp.exp(-jnp.abs(x)))


def _one_minus_exp(z):
    series = -z * (1.0 + 0.5 * z * (1.0 + z * (1.0 / 3.0) * (1.0 + 0.25 * z)))
    return jnp.where(z > -0.05, series, 1.0 - jnp.exp(z))


def _tiles(t_real):
    if t_real > 2048:
        tm = 384
        tp = -(-t_real // tm) * tm
        return tp, tm, tm // 2, tm // 2, tp // 2
    tm = 128
    tp = -(-t_real // tm) * tm
    return tp, tm, tm // 2, tm // 2, tm


def _load_weights(copies, sems):
    cps = [pltpu.make_async_copy(s, d, sems.at[k]) for k, (s, d) in enumerate(copies)]
    for cp in cps:
        cp.start()
    for cp in cps:
        cp.wait()


def _position():
    x, y, c = lax.axis_index("x"), lax.axis_index("y"), lax.axis_index("c")
    chips = [(1 - x, y), (x, 1 - y), (1 - x, 1 - y)]
    return x, y, c, chips


def _slot(p):
    return 4 * p[0] + 2 * p[1] + p[2]


class _Gather:
    def __init__(self, shards):
        self.shards = list(shards)
        self.n = len(self.shards)

    def inputs(self):
        return self.shards

    def out_shape(self):
        return [jax.ShapeDtypeStruct((N_DEV,) + s.shape, s.dtype) for s in self.shards]

    def scratch(self):
        return [pltpu.SemaphoreType.DMA((7 * self.n,)), pltpu.SemaphoreType.DMA((7 * self.n,)),
                pltpu.SemaphoreType.DMA((self.n,))]

    def _plan(self, ins, outs, sems):
        send_sems, recv_sems, local_sems = sems
        x, y, c, chips = _position()
        me, sibling = (x, y, c), (x, y, 1 - c)

        def copy(a, k, block, to, src=None):
            dst = outs[a].at[_slot(block)]
            return pltpu.make_async_remote_copy(
                src_ref=dst if src is None else src, dst_ref=dst,
                send_sem=send_sems.at[7 * a + k], recv_sem=recv_sems.at[7 * a + k],
                device_id=to, device_id_type=MESH)

        mine = [pltpu.make_async_copy(ins[a], outs[a].at[_slot(me)], local_sems.at[a]) for a in range(self.n)]
        first = []
        for a in range(self.n):
            first.append(copy(a, 0, me, sibling, src=ins[a]))
            first += [copy(a, 1 + j, me, (*chip, c), src=ins[a]) for j, chip in enumerate(chips)]
        return copy, mine, first, me, sibling, c, chips

    def start(self, ins, outs, sems):
        _, mine, first, *_ = self._plan(ins, outs, sems)
        for cp in mine + first:
            cp.start()

    def finish(self, ins, outs, sems):
        copy, mine, first, me, sibling, c, chips = self._plan(ins, outs, sems)
        passed = []
        for j, chip in enumerate(chips):
            for a in range(self.n):
                copy(a, 1 + j, (*chip, c), me).wait_recv()
                fwd = copy(a, 4 + j, (*chip, c), sibling)
                fwd.start()
                passed.append(fwd)
        for a in range(self.n):
            copy(a, 0, sibling, me).wait_recv()
            for j, chip in enumerate(chips):
                copy(a, 4 + j, (*chip, 1 - c), me).wait_recv()
        for cp in first + passed:
            cp.wait_send()
        for cp in mine:
            cp.wait()


class _Scatter:
    def __init__(self, grads):
        self.grads = list(grads)
        self.n = len(self.grads)

    def inputs(self):
        return self.grads

    def out_shape(self):
        return [jax.ShapeDtypeStruct((N_DEV - 1,) + g.shape[1:], g.dtype) for g in self.grads]

    def scratch(self):
        return [pltpu.SemaphoreType.DMA((7 * self.n,)), pltpu.SemaphoreType.DMA((7 * self.n,))]

    def _plan(self, ins, outs, sems):
        send_sems, recv_sems = sems
        x, y, c, _ = _position()
        cps = []
        for a in range(self.n):
            for k in range(1, N_DEV):
                peer = (x ^ (k >> 2), y ^ ((k >> 1) & 1), c ^ (k & 1))
                cps.append(pltpu.make_async_remote_copy(
                    src_ref=ins[a].at[_slot(peer)], dst_ref=outs[a].at[k - 1],
                    send_sem=send_sems.at[7 * a + k - 1], recv_sem=recv_sems.at[7 * a + k - 1],
                    device_id=peer, device_id_type=MESH))
        return cps

    def start(self, ins, outs, sems):
        for cp in self._plan(ins, outs, sems):
            cp.start()

    def finish(self, ins, outs, sems):
        for cp in self._plan(ins, outs, sems):
            cp.wait()


def _hosted(inner, n_in, n_out, comm, grid):
    if comm is None:
        return inner
    nc_in, nc_out, ns = len(comm.inputs()), len(comm.out_shape()), len(comm.scratch())

    def body(*refs):
        o0 = n_in + nc_in
        s0 = o0 + n_out + nc_out
        main = refs[:n_in] + refs[o0:o0 + n_out] + refs[s0:len(refs) - ns]
        c_in, c_out, c_sems = refs[n_in:o0], refs[o0 + n_out:s0], refs[len(refs) - ns:]
        ids = [pl.program_id(ax) for ax in range(len(grid))]
        first = functools.reduce(jnp.logical_and, [i == 0 for i in ids])
        last = functools.reduce(jnp.logical_and, [i == g - 1 for i, g in zip(ids, grid)])

        @pl.when(first)
        def _():
            comm.start(c_in, c_out, c_sems)

        inner(*main)

        @pl.when(last)
        def _():
            comm.finish(c_in, c_out, c_sems)

    return body


def _call(inner, name, grid, in_specs, out_specs, out_shape, scratch, args, comm=None):
    n_in, n_out = len(args), len(out_shape)
    body = _hosted(inner, n_in, n_out, comm, grid)
    if comm is not None:
        in_specs = list(in_specs) + [_any()] * len(comm.inputs())
        args = list(args) + comm.inputs()
        out_specs = list(out_specs) + [_any()] * len(comm.out_shape())
        out_shape = list(out_shape) + comm.out_shape()
        scratch = list(scratch) + comm.scratch()
    outs = pl.pallas_call(
        body, name=name, grid=grid, in_specs=list(in_specs), out_specs=list(out_specs), out_shape=list(out_shape),
        scratch_shapes=list(scratch), compiler_params=_params(len(grid)))(*args)
    return list(outs[:n_out]), list(outs[n_out:])


class _Bcast:
    def __init__(self, block):
        self.block = block

    def inputs(self):
        return [self.block]

    def out_shape(self):
        return [jax.ShapeDtypeStruct((N_DEV,) + self.block.shape, self.block.dtype)]

    def scratch(self):
        return [pltpu.SemaphoreType.DMA((N_DEV - 1,)), pltpu.SemaphoreType.DMA((N_DEV - 1,)),
                pltpu.SemaphoreType.DMA((1,))]

    def _plan(self, ins, outs, sems):
        send_sems, recv_sems, local_sem = sems
        x, y, c, _ = _position()
        mine = outs[0].at[_slot((x, y, c))]
        cps = []
        for k in range(1, N_DEV):
            peer = (x ^ (k >> 2), y ^ ((k >> 1) & 1), c ^ (k & 1))
            cps.append(pltpu.make_async_remote_copy(
                src_ref=ins[0], dst_ref=mine, send_sem=send_sems.at[k - 1], recv_sem=recv_sems.at[k - 1],
                device_id=peer, device_id_type=MESH))
        return pltpu.make_async_copy(ins[0], mine, local_sem.at[0]), cps

    def start(self, ins, outs, sems):
        own, cps = self._plan(ins, outs, sems)
        own.start()
        for cp in cps:
            cp.start()

    def finish(self, ins, outs, sems):
        own, cps = self._plan(ins, outs, sems)
        for cp in cps:
            cp.wait()
        own.wait()


def _all_gather(shards):
    comm = _Gather(shards)
    n = comm.n

    def body(*refs):
        comm.start(refs[:n], refs[n:2 * n], refs[2 * n:])
        comm.finish(refs[:n], refs[n:2 * n], refs[2 * n:])

    return pl.pallas_call(
        body, name="weights_all_gather", out_shape=comm.out_shape(),
        in_specs=[_any()] * n, out_specs=[_any()] * n, scratch_shapes=comm.scratch(),
    )(*shards)


def _pair_exchange(grads):
    n = len(grads)

    def body(*refs):
        ins, outs = refs[:n], refs[n:2 * n]
        send_sems, recv_sems = refs[2 * n:]
        x, y, c, _ = _position()
        cps = [pltpu.make_async_remote_copy(
            src_ref=ins[a].at[:, 1 - c], dst_ref=outs[a],
            send_sem=send_sems.at[a], recv_sem=recv_sems.at[a],
            device_id=(x, y, 1 - c), device_id_type=MESH) for a in range(n)]
        for cp in cps:
            cp.start()
        for cp in cps:
            cp.wait()

    return pl.pallas_call(
        body, name="grads_pair_exchange",
        out_shape=[jax.ShapeDtypeStruct((4,) + g.shape[2:], g.dtype) for g in grads],
        in_specs=[_any()] * n, out_specs=[_any()] * n,
        scratch_shapes=[pltpu.SemaphoreType.DMA((n,)), pltpu.SemaphoreType.DMA((n,))],
    )(*grads)


def _chip_exchange(combs):
    n = len(combs)

    def body(*refs):
        ins, outs = refs[:n], refs[n:2 * n]
        send_sems, recv_sems = refs[2 * n:]
        x, y, c, chips = _position()
        cps = []
        for a in range(n):
            for j, (cx, cy) in enumerate(chips):
                cps.append(pltpu.make_async_remote_copy(
                    src_ref=ins[a].at[2 * cx + cy], dst_ref=outs[a].at[j],
                    send_sem=send_sems.at[3 * a + j], recv_sem=recv_sems.at[3 * a + j],
                    device_id=(cx, cy, c), device_id_type=MESH))
        for cp in cps:
            cp.start()
        for cp in cps:
            cp.wait()

    return pl.pallas_call(
        body, name="grads_chip_exchange",
        out_shape=[jax.ShapeDtypeStruct((3,) + g.shape[1:], g.dtype) for g in combs],
        in_specs=[_any()] * n, out_specs=[_any()] * n,
        scratch_shapes=[pltpu.SemaphoreType.DMA((3 * n,)), pltpu.SemaphoreType.DMA((3 * n,))],
    )(*combs)


def _pair_add(grad, recv, core):
    blk = grad.shape[2:]
    zeros = (0,) * len(blk)

    def body(core_ref, g_ref, r_ref, o_ref):
        del core_ref
        o_ref[...] = (g_ref[...].astype(F32) + r_ref[...].astype(F32)).astype(BF16)

    return pl.pallas_call(
        body, name="grads_pair_add",
        out_shape=jax.ShapeDtypeStruct((4,) + blk, BF16),
        grid_spec=pltpu.PrefetchScalarGridSpec(
            num_scalar_prefetch=1, grid=(4,),
            in_specs=[pl.BlockSpec((None, None) + blk, lambda i, cr: (i, cr[0]) + zeros),
                      pl.BlockSpec((None,) + blk, lambda i, cr: (i,) + zeros)],
            out_specs=pl.BlockSpec((None,) + blk, lambda i, cr: (i,) + zeros)),
        compiler_params=_params(1),
    )(core, grad, recv)


def _adamw(w, g, m, v):
    m2 = ADAM_B1 * m + (1.0 - ADAM_B1) * g
    v2 = ADAM_B2 * v + (1.0 - ADAM_B2) * (g * g)
    m_hat = m2 / (1.0 - ADAM_B1 ** ADAM_STEP)
    v_hat = v2 / (1.0 - ADAM_B2 ** ADAM_STEP)
    delta = -ADAM_LR * (m_hat / (jnp.sqrt(v_hat) + ADAM_EPS) + ADAM_WD * w)
    return delta, m2, v2


def _final_adamw(own, recv, idx, parts):
    blk = own.shape[1:]
    n_recv = recv.shape[0]
    n_parts = len(parts)
    per = blk[0] // n_parts if n_parts > 1 else None
    rows = blk[-2]
    n_chunks = 1 if n_parts > 1 else (4 if rows % 64 == 0 and rows >= 512 else (2 if rows % 32 == 0 else 1))
    cblk = blk[:-2] + (rows // n_chunks, blk[-1])
    lead = (0,) * (len(blk) - 2)

    def body(idx_ref, c_ref, r_ref, *refs):
        del idx_ref
        ins, outs = refs[:3 * n_parts], refs[3 * n_parts:]
        g = c_ref[...].astype(F32)
        for k in range(n_recv):
            g = g + r_ref[k].astype(F32)
        for p in range(n_parts):
            w_ref, m_ref, v_ref = ins[3 * p:3 * p + 3]
            if n_parts == 1:
                gp = g
            elif per == 1:
                gp = g[p]
            else:
                gp = g[p * per:(p + 1) * per]
            delta, m2, v2 = _adamw(w_ref[0], gp, m_ref[0], v_ref[0])
            o = outs[4 * p:4 * p + 4]
            o[0][0] = gp
            o[1][0] = delta
            o[2][0] = m2
            o[3][0] = v2

    flat = [a for wmv in parts for a in wmv]

    def part_spec(a):
        shape = a.shape[:-2] + (a.shape[-2] // n_chunks, a.shape[-1])
        return pl.BlockSpec(shape, lambda i, cr, nd=a.ndim: (0,) * (nd - 2) + (i, 0))

    outs = pl.pallas_call(
        body, name="grads_sum_adamw",
        out_shape=[jax.ShapeDtypeStruct(wmv[0].shape, F32) for wmv in parts for _ in range(4)],
        grid_spec=pltpu.PrefetchScalarGridSpec(
            num_scalar_prefetch=1, grid=(n_chunks,),
            in_specs=[pl.BlockSpec((None,) + cblk, lambda i, cr: (cr[0],) + lead + (i, 0)),
                      pl.BlockSpec((n_recv,) + cblk, lambda i, cr: (0,) + lead + (i, 0))]
                     + [part_spec(a) for a in flat],
            out_specs=[part_spec(wmv[0]) for wmv in parts for _ in range(4)]),
        compiler_params=_params(1),
    )(idx, own, recv, *flat)
    return [tuple(outs[4 * p:4 * p + 4]) for p in range(n_parts)]


def _small_adamw(partials, layout, me_index):
    _, rows, d = partials.shape
    n = len(layout)
    cw = d // N_DEV

    def body(me_ref, p_ref, *refs):
        ins, t_ref, outs = refs[:3 * n], refs[3 * n], refs[3 * n + 1:]
        me = me_ref[0]
        total = p_ref[0]
        for j in range(1, N_DEV):
            total = total + p_ref[j]
        t_ref[...] = total
        for e, (kind, r0, nr, _, _, _) in enumerate(layout):
            w_ref, m_ref, v_ref = ins[3 * e:3 * e + 3]
            o = outs[4 * e:4 * e + 4]
            if kind == "rep":
                g = t_ref[r0:r0 + nr, :]
                delta, m2, v2 = _adamw(w_ref[...], g, m_ref[...], v_ref[...])
                for ref, val in zip(o, (g, delta, m2, v2)):
                    ref[...] = val
            elif kind == "wide":
                for q in range(nr):
                    sl = slice(q * d, (q + 1) * d)
                    g = t_ref[r0 + q:r0 + q + 1, :]
                    delta, m2, v2 = _adamw(w_ref[:, sl], g, m_ref[:, sl], v_ref[:, sl])
                    for ref, val in zip(o, (g, delta, m2, v2)):
                        ref[:, sl] = val
            else:
                for j in range(N_DEV):
                    @pl.when(me == j)
                    def _(j=j, o=o, w_ref=w_ref, m_ref=m_ref, v_ref=v_ref, r0=r0, nr=nr):
                        g = t_ref[r0:r0 + nr, j * cw:(j + 1) * cw]
                        delta, m2, v2 = _adamw(w_ref[...], g, m_ref[...], v_ref[...])
                        for ref, val in zip(o, (g, delta, m2, v2)):
                            ref[...] = val

    flat = [a for ent in layout for a in ent[3:]]
    vm = pl.BlockSpec(memory_space=pltpu.VMEM)
    outs = pl.pallas_call(
        body, name="small_adamw",
        out_shape=[jax.ShapeDtypeStruct((rows, d), F32)]
                  + [jax.ShapeDtypeStruct(ent[3].shape, F32) for ent in layout for _ in range(4)],
        in_specs=[pl.BlockSpec(memory_space=pltpu.SMEM), vm] + [vm] * len(flat),
        out_specs=[vm] * (1 + 4 * n),
        compiler_params=pltpu.CompilerParams(vmem_limit_bytes=V7X_VMEM_LIMIT),
    )(me_index, partials, *flat)
    return outs[0], [tuple(outs[1 + 4 * e:5 + 4 * e]) for e in range(n)]


def _ffn_fwd(h, g, wgu, wd, tm, loss=None, comm=None):
    tp, d = h.shape
    f = wd.shape[0]
    fc = f // FFN_CHUNKS
    nt = tp // tm
    with_loss = loss is not None
    if with_loss:
        tgt, gf, n_meta, t_real = loss

    def body(*refs):
        if with_loss:
            (h_ref, g_ref, wgu_hbm, wd_hbm, tgt_ref, gf_ref, out_ref, gu_ref, n_ref, loss_ref, dgf_ref,
             wgu_v, wd_v, sems) = refs
        else:
            h_ref, g_ref, wgu_hbm, wd_hbm, out_ref, gu_ref, n_ref, wgu_v, wd_v, sems = refs
        i = pl.program_id(0)

        @pl.when(i == 0)
        def _():
            _load_weights([(wgu_hbm, wgu_v), (wd_hbm, wd_v)], sems)
            if with_loss:
                loss_ref[...] = jnp.zeros_like(loss_ref)
                dgf_ref[...] = jnp.zeros_like(dgf_ref)

        x = h_ref[...]
        n, _ = _rms_fwd(x, g_ref[...])
        nb = n.astype(BF16)
        n_ref[...] = nb
        acc = jnp.zeros((tm, d), F32)
        for j in range(FFN_CHUNKS):
            cols = slice(j * fc, (j + 1) * fc)
            gate = _nt(nb, wgu_v[pl.ds(j * fc, fc), :])
            up = _nt(nb, wgu_v[pl.ds(f + j * fc, fc), :])
            gu_ref[0, :, cols] = gate.astype(BF16)
            gu_ref[1, :, cols] = up.astype(BF16)
            act = (gate * _sigmoid(gate) * up).astype(BF16)
            acc = acc + _nn(act, wd_v[pl.ds(j * fc, fc), :])
        hn = x + FFN_RES * acc
        if not with_loss:
            out_ref[...] = hn
        else:
            gfv = gf_ref[...]
            r = lax.rsqrt(jnp.mean(hn * hn, axis=-1, keepdims=True) + EPS)
            xr = hn * r
            rows = i * tm + lax.broadcasted_iota(jnp.int32, (tm, 1), 0)
            mask = jnp.logical_and(rows >= n_meta, rows < t_real)
            diff = jnp.where(mask, xr * gfv - tgt_ref[...], 0.0)
            loss_ref[...] += jnp.zeros_like(loss_ref) + 0.5 * jnp.sum(diff * diff) / d
            dy = diff / d
            gy = dy * gfv
            out_ref[...] = r * (gy - xr * jnp.mean(gy * xr, axis=-1, keepdims=True))
            dgf_ref[...] += _rowsum(dy * xr)

    row = pl.BlockSpec((tm, d), lambda i: (i, 0))
    vec = pl.BlockSpec((1, d), lambda i: (0, 0))
    in_specs = [row, vec, _any(), _any()]
    out_shape = [jax.ShapeDtypeStruct((tp, d), F32), jax.ShapeDtypeStruct((2, tp, f), BF16),
                 jax.ShapeDtypeStruct((tp, d), BF16)]
    out_specs = [row, pl.BlockSpec((2, tm, f), lambda i: (0, i, 0)), row]
    args = [h, g, wgu, wd]
    if with_loss:
        in_specs += [row, vec]
        out_shape += [jax.ShapeDtypeStruct((1, d), F32), jax.ShapeDtypeStruct((1, d), F32)]
        out_specs += [vec, vec]
        args += [tgt, gf]
    return _call(body, "ffn_fwd_loss" if with_loss else "ffn_fwd", (nt,), in_specs, out_specs, out_shape,
                 [pltpu.VMEM((2 * f, d), BF16), pltpu.VMEM((f, d), BF16), pltpu.SemaphoreType.DMA((2,))],
                 args, comm)


def _ffn_bwd(dh, h, gu, g, wgu, wd, tm, comm=None):
    tp, d = h.shape
    f = wd.shape[0]
    fc = f // FFN_CHUNKS
    nt = tp // tm

    def body(dh_ref, h_ref, gu_ref, g_ref, wgu_hbm, wd_hbm,
             dhin_ref, dgu_ref, act_ref, df_ref, dg_ref, wgu_v, wd_v, dn_v, sems):
        i, j = pl.program_id(0), pl.program_id(1)

        @pl.when(jnp.logical_and(i == 0, j == 0))
        def _():
            _load_weights([(wgu_hbm, wgu_v), (wd_hbm, wd_v)], sems)
            dg_ref[...] = jnp.zeros_like(dg_ref)

        dfb = (FFN_RES * dh_ref[...]).astype(BF16)

        @pl.when(j == 0)
        def _():
            df_ref[...] = dfb
            dn_v[...] = jnp.zeros_like(dn_v)

        lo = pl.multiple_of(j * fc, 16)
        dact = _nt(dfb, wd_v[pl.ds(lo, fc), :])
        gate = gu_ref[0].astype(F32)
        up = gu_ref[1].astype(F32)
        sg = _sigmoid(gate)
        silu = gate * sg
        act_ref[...] = (silu * up).astype(BF16)
        dgate = (dact * up * (sg * (1.0 + gate * (1.0 - sg)))).astype(BF16)
        dup = (dact * silu).astype(BF16)
        dgu_ref[0] = dgate
        dgu_ref[1] = dup
        dn_v[...] += _nn(dgate, wgu_v[pl.ds(lo, fc), :]) + _nn(dup, wgu_v[pl.ds(pl.multiple_of(f + j * fc, 16), fc), :])

        @pl.when(j == FFN_CHUNKS - 1)
        def _():
            x = h_ref[...]
            r = lax.rsqrt(jnp.mean(x * x, axis=-1, keepdims=True) + EPS)
            dx, dgp = _rms_bwd(dn_v[...], x, r, g_ref[...])
            dhin_ref[...] = dh_ref[...] + dx
            dg_ref[...] += dgp

    row = pl.BlockSpec((tm, d), lambda i, j: (i, 0))
    vec = pl.BlockSpec((1, d), lambda i, j: (0, 0))
    hid2 = pl.BlockSpec((2, tm, fc), lambda i, j: (0, i, j))
    return _call(
        body, "ffn_bwd", (nt, FFN_CHUNKS),
        [row, row, hid2, vec, _any(), _any()],
        [row, hid2, pl.BlockSpec((tm, fc), lambda i, j: (i, j)), row, vec],
        [jax.ShapeDtypeStruct((tp, d), F32), jax.ShapeDtypeStruct((2, tp, f), BF16),
         jax.ShapeDtypeStruct((tp, f), BF16), jax.ShapeDtypeStruct((tp, d), BF16),
         jax.ShapeDtypeStruct((1, d), F32)],
        [pltpu.VMEM((2 * f, d), BF16), pltpu.VMEM((f, d), BF16), pltpu.VMEM((tm, d), F32),
         pltpu.SemaphoreType.DMA((2,))],
        [dh, h, gu, g, wgu, wd], comm)


def _piece_segments(q, d, nb_cols):
    segs = []
    for j in range(N_DEV):
        lo, hi = max(q * d, j * nb_cols), min((q + 1) * d, (j + 1) * nb_cols)
        if lo < hi:
            segs.append((j, lo - q * d, hi - q * d, lo - j * nb_cols, hi - j * nb_cols))
    return segs


def _w3_copies(w3_hbm, rows, w3_v):
    return [(w3_hbm.at[k, pl.ds(q * rows, rows)], w3_v.at[q, pl.ds(k * rows, rows)])
            for q in range(3) for k in range(N_DEV)]


def _gates(xrb, wg_ref, ba, bx, lam, hd):
    pre_r, pre_i = [], []
    for hh in range(N_HEADS):
        xh = xrb[:, hh * hd:(hh + 1) * hd]
        pre_r.append(_nn(xh, wg_ref[0, hh]))
        pre_i.append(_nn(xh, wg_ref[1, hh]))
    r = _sigmoid(jnp.concatenate(pre_r, axis=1) + ba)
    ig = _sigmoid(jnp.concatenate(pre_i, axis=1) + bx)
    sp = _softplus(-lam)
    log_a = -RG_LRU_C * r * sp
    a = jnp.exp(log_a)
    s = jnp.sqrt(_one_minus_exp(2.0 * log_a))
    return r, ig, sp, a, s


def _scan_fwd(a, u, h_prev):
    tm = a.shape[0]
    rows = lax.broadcasted_iota(jnp.int32, a.shape, 0)
    d = 1
    while d < tm:
        if d < SUBLANES:
            keep = rows >= d
            u = jnp.where(keep, a * pltpu.roll(u, d, 0) + u, u)
            a = jnp.where(keep, a * pltpu.roll(a, d, 0), a)
        else:
            u = jnp.concatenate([u[:d], a[d:] * u[:tm - d] + u[d:]], axis=0)
            a = jnp.concatenate([a[:d], a[d:] * a[:tm - d]], axis=0)
        d *= 2
    return u + a * h_prev


def _scan_bwd(b, v, g_next):
    tm = b.shape[0]
    rows = lax.broadcasted_iota(jnp.int32, b.shape, 0)
    d = 1
    while d < tm:
        if d < SUBLANES:
            keep = rows < tm - d
            v = jnp.where(keep, v + b * pltpu.roll(v, tm - d, 0), v)
            b = jnp.where(keep, b * pltpu.roll(b, tm - d, 0), b)
        else:
            v = jnp.concatenate([v[:tm - d] + b[:tm - d] * v[d:], v[tm - d:]], axis=0)
            b = jnp.concatenate([b[:tm - d] * b[d:], b[tm - d:]], axis=0)
        d *= 2
    return v + b * g_next


def _shifted_copies(ext_ref, es_ref, n_rows):
    for s in range(1, SUBLANES):
        es_ref[s, pl.ds(0, n_rows), :] = ext_ref[pl.ds(s, n_rows), :]


def _tap(ext_ref, es_ref, off, tm):
    q, s = divmod(off, SUBLANES)
    if s == 0:
        return ext_ref[pl.ds(SUBLANES * q, tm), :]
    return es_ref[s, pl.ds(SUBLANES * q, tm), :]


def _mixer_fwd(h, g, b_in, win_all, cw4, cb4, wg, ba, bx, lam, cw31, cb31, lng, lnb, bcp, w3_all, tm, comm=None):
    tp, d = h.shape
    nb_cols = win_all.shape[-1]
    n_in = N_DEV * nb_cols
    hd = wg.shape[-1]
    k4, k31 = cw4.shape[0], cw31.shape[0]
    w3_rows = d // N_DEV

    def body(h_ref, g_ref, b_ref, win_hbm, cw4_ref, cb4_ref, wg_ref, ba_ref, bx_ref, lam_ref, cw31_ref, cb31_ref,
             lng_ref, lnb_ref, bcp_ref, w3_hbm,
             h2_ref, p_ref, n_ref, xr_ref, hs_ref, v1_ref, ya_ref, yb_ref,
             win_v, w3_v, ext4, ext31, es31, hcar, sems):
        @pl.when(pl.program_id(0) == 0)
        def _():
            _load_weights([(win_hbm, win_v)] + _w3_copies(w3_hbm, w3_rows, w3_v), sems)
            ext4[pl.ds(0, CONV4_HALO), :] = jnp.zeros((CONV4_HALO, d), F32)
            ext31[pl.ds(0, CONV31_HALO), :] = jnp.zeros((CONV31_HALO, d), F32)
            hcar[...] = jnp.zeros_like(hcar)

        n, _ = _rms_fwd(h_ref[...], g_ref[...])
        nb = n.astype(BF16)
        n_ref[...] = nb

        def piece(q):
            parts = [_nn(nb, win_v[j, :, bl:bh]) for j, _, _, bl, bh in _piece_segments(q, d, nb_cols)]
            pq = (jnp.concatenate(parts, axis=1) + b_ref[:, q * d:(q + 1) * d]).astype(BF16)
            p_ref[:, q * d:(q + 1) * d] = pq
            return pq.astype(F32)

        x_rnn, y_rnn, glu_v, glu_g, gate_a, gate_b = [piece(q) for q in range(6)]

        ext4[pl.ds(CONV4_HALO, tm), :] = x_rnn
        xr = cb4_ref[...] + jnp.zeros((tm, d), F32)
        for k in range(k4):
            xr = xr + cw4_ref[k:k + 1, :] * ext4[pl.ds(CONV4_HALO - (k4 - 1) + k, tm), :]
        ext4[pl.ds(0, CONV4_HALO), :] = ext4[pl.ds(tm, CONV4_HALO), :]
        xrb = xr.astype(BF16)
        xr_ref[...] = xrb
        xr = xrb.astype(F32)
        _, ig, _, a, s = _gates(xrb, wg_ref, ba_ref[...], bx_ref[...], lam_ref[...], hd)
        hseq = _scan_fwd(a, s * (ig * xr), hcar[0:1, :])
        hcar[0:1, :] = hseq[tm - 1:tm, :]
        hs_ref[...] = hseq.astype(BF16)
        gl, _ = _gelu(y_rnn)
        ya = _nn((hseq * gl).astype(BF16), w3_v[0])
        ya_ref[...] = ya.astype(BF16)

        ext31[pl.ds(CONV31_HALO, tm), :] = glu_v * _sigmoid(glu_g)
        _shifted_copies(ext31, es31, tm + CONV31_HALO - SUBLANES)
        v1 = cb31_ref[...] + jnp.zeros((tm, d), F32)
        for k in range(k31):
            v1 = v1 + cw31_ref[k:k + 1, :] * _tap(ext31, es31, CONV31_HALO - (k31 - 1) + k, tm)
        ext31[pl.ds(0, CONV31_HALO), :] = ext31[pl.ds(tm, CONV31_HALO), :]
        v1b = v1.astype(BF16)
        v1_ref[...] = v1b
        v1 = v1b.astype(F32)
        xc = v1 - jnp.mean(v1, axis=-1, keepdims=True)
        rstd = lax.rsqrt(jnp.mean(xc * xc, axis=-1, keepdims=True) + EPS)
        v2 = xc * rstd * lng_ref[...] + lnb_ref[...]
        yb = _nn((v2 * _sigmoid(v2)).astype(BF16), w3_v[1]) + bcp_ref[...]
        yb_ref[...] = yb.astype(BF16)

        merged = _sigmoid(gate_a) * ya + _sigmoid(gate_b) * yb
        h2_ref[...] = h_ref[...] + _nn(merged.astype(BF16), w3_v[2])

    row = pl.BlockSpec((tm, d), lambda i: (i, 0))
    wide = pl.BlockSpec((tm, n_in), lambda i: (i, 0))
    full = lambda a: pl.BlockSpec(a.shape, lambda i, nd=a.ndim: (0,) * nd)
    smalls = [cw4, cb4, wg, ba, bx, lam, cw31, cb31, lng, lnb, bcp]
    return _call(
        body, "mixer_fwd", (tp // tm,),
        [row, full(g), full(b_in), _any()] + [full(a) for a in smalls] + [_any()],
        [row, wide] + [row] * 6,
        [jax.ShapeDtypeStruct((tp, d), F32), jax.ShapeDtypeStruct((tp, n_in), BF16)]
        + [jax.ShapeDtypeStruct((tp, d), BF16)] * 6,
        [pltpu.VMEM(win_all.shape, BF16),
         pltpu.VMEM((3, d, d), BF16),
         pltpu.VMEM((tm + CONV4_HALO, d), F32),
         pltpu.VMEM((tm + CONV31_HALO, d), F32),
         pltpu.VMEM((SUBLANES, tm + CONV31_HALO, d), F32),
         pltpu.VMEM((SUBLANES, d), F32),
         pltpu.SemaphoreType.DMA((1 + 3 * N_DEV,))],
        [h, g, b_in, win_all, *smalls, w3_all], comm)


SG_BIN, SG_CW4, SG_CB4, SG_BA, SG_BX, SG_LAM, SG_CB31, SG_LNG, SG_LNB, SG_BCP, SG_MIX, SG_CW31 = 0, 6, 10, 11, 12, 13, 14, 15, 16, 17, 18, 19


def _mixer_bwd(dh2, h, g, proj, xr_s, hs_s, v1_s, ya_s, yb_s, win_all, cw4, wg, ba, bx, lam, cw31, lng, lnb, w3_all, tm,
               comm=None):
    tp, d = dh2.shape
    nb_cols = win_all.shape[-1]
    n_in = proj.shape[1]
    hd = wg.shape[-1]
    k4, k31 = cw4.shape[0], cw31.shape[0]
    nt = tp // tm
    w3_rows = d // N_DEV
    sg_rows = -(-(SG_CW31 + k31) // SUBLANES) * SUBLANES
    halo_rows = 16
    per = tm // halo_rows

    def body(dh_ref, h_ref, g_ref, p_ref, xr_ref, hs_ref, hh_ref, v1_ref, ya_ref, yb_ref, win_hbm,
             cw4_ref, wg_ref, ba_ref, bx_ref, lam_ref, cw31_ref, lng_ref, lnb_ref, w3_hbm,
             dh1_ref, dp_ref, x3_ref, y3_ref, yg_ref, sg_ref,
             win_v, w3_v, extd4, extd31, es31, gcar, sems):
        i = pl.program_id(0)
        tile = nt - 1 - i

        @pl.when(i == 0)
        def _():
            _load_weights([(win_hbm, win_v)] + _w3_copies(w3_hbm, w3_rows, w3_v), sems)
            extd4[pl.ds(tm, CONV4_HALO), :] = jnp.zeros((CONV4_HALO, d), F32)
            extd31[pl.ds(tm, CONV31_HALO), :] = jnp.zeros((CONV31_HALO, d), F32)
            gcar[...] = jnp.zeros_like(gcar)
            sg_ref[...] = jnp.zeros_like(sg_ref)

        def acc(row, val):
            sg_ref[row:row + 1, :] += _rowsum(val)

        rows = lax.broadcasted_iota(jnp.int32, (tm, d), 0)
        x_rnn = p_ref[:, 0:d].astype(F32)
        y_rnn = p_ref[:, d:2 * d].astype(F32)
        glu_v = p_ref[:, 2 * d:3 * d].astype(F32)
        glu_g = p_ref[:, 3 * d:4 * d].astype(F32)
        sga = _sigmoid(p_ref[:, 4 * d:5 * d].astype(F32))
        sgb = _sigmoid(p_ref[:, 5 * d:6 * d].astype(F32))
        ya = ya_ref[...].astype(F32)
        yb = yb_ref[...].astype(F32)

        dmob = dh_ref[...].astype(BF16)
        dmerged = _nt(dmob, w3_v[2])
        x3_ref[:, 0:d] = (sga * ya + sgb * yb).astype(BF16)
        y3_ref[:, 0:d] = dmob
        dya = sga * dmerged
        dyb = sgb * dmerged
        dn_parts = []

        def emit(q, val):
            vb = val.astype(BF16)
            dp_ref[:, q * d:(q + 1) * d] = vb
            acc(SG_BIN + q, val)
            for j, lo, hi, bl, bh in _piece_segments(q, d, nb_cols):
                term = _nt(vb[:, lo:hi], win_v[j, :, bl:bh])
                dn_parts[:] = [term if not dn_parts else dn_parts[0] + term]

        emit(4, dmerged * ya * sga * (1.0 - sga))
        emit(5, dmerged * yb * sgb * (1.0 - sgb))

        dyab = dya.astype(BF16)
        y3_ref[:, d:2 * d] = dyab
        dza = _nt(dyab, w3_v[0])
        hsv = hs_ref[...].astype(F32)
        gl, th = _gelu(y_rnn)
        x3_ref[:, d:2 * d] = (hsv * gl).astype(BF16)
        emit(1, dza * hsv * _gelu_grad(y_rnn, th))
        dhs = dza * gl
        xrb = xr_ref[...]
        xr = xrb.astype(F32)
        lam_v = lam_ref[...]
        r, ig, sp, a, s = _gates(xrb, wg_ref, ba_ref[...], bx_ref[...], lam_v, hd)
        b = jnp.where(rows == tm - 1, gcar[1:2, :], pltpu.roll(a, tm - 1, 0))
        big_g = _scan_bwd(b, dhs, gcar[0:1, :])
        gcar[0:1, :] = big_g[0:1, :]
        gcar[1:2, :] = a[0:1, :]
        h_before = jnp.where(tile > 0, hh_ref[halo_rows - 1:halo_rows, :].astype(F32), 0.0)
        h_prev = jnp.where(rows == 0, h_before, pltpu.roll(hsv, 1, 0))
        ds = big_g * ig * xr
        dla = big_g * h_prev * a - ds * (a * a) / jnp.maximum(s, 1e-20)
        acc(SG_LAM, dla * r * (RG_LRU_C * _sigmoid(-lam_v)))
        dpr = dla * (-RG_LRU_C * sp) * r * (1.0 - r)
        dpi = big_g * s * xr * ig * (1.0 - ig)
        acc(SG_BA, dpr)
        acc(SG_BX, dpi)
        dprb = dpr.astype(BF16)
        dpib = dpi.astype(BF16)
        yg_ref[:, 0:d] = dprb
        yg_ref[:, d:2 * d] = dpib
        back = []
        for hh in range(N_HEADS):
            sl = slice(hh * hd, (hh + 1) * hd)
            back.append(_nt(dprb[:, sl], wg_ref[0, hh]) + _nt(dpib[:, sl], wg_ref[1, hh]))
        dxr = big_g * s * ig + jnp.concatenate(back, axis=1)
        acc(SG_CB4, dxr)
        extd4[pl.ds(0, tm), :] = dxr
        dx_rnn = jnp.zeros((tm, d), F32)
        for k in range(k4):
            term = extd4[pl.ds(k4 - 1 - k, tm), :]
            dx_rnn = dx_rnn + cw4_ref[k:k + 1, :] * term
            acc(SG_CW4 + k, x_rnn * term)
        extd4[pl.ds(tm, CONV4_HALO), :] = extd4[pl.ds(0, CONV4_HALO), :]
        emit(0, dx_rnn)

        dybb = dyb.astype(BF16)
        y3_ref[:, 2 * d:3 * d] = dybb
        acc(SG_BCP, dyb)
        dv3 = _nt(dybb, w3_v[1])
        v1 = v1_ref[...].astype(F32)
        xc = v1 - jnp.mean(v1, axis=-1, keepdims=True)
        rstd = lax.rsqrt(jnp.mean(xc * xc, axis=-1, keepdims=True) + EPS)
        xhat = xc * rstd
        lng_v = lng_ref[...]
        v2 = xhat * lng_v + lnb_ref[...]
        s2 = _sigmoid(v2)
        x3_ref[:, 2 * d:3 * d] = (v2 * s2).astype(BF16)
        dv2 = dv3 * (s2 * (1.0 + v2 * (1.0 - s2)))
        acc(SG_LNG, dv2 * xhat)
        acc(SG_LNB, dv2)
        dxh = dv2 * lng_v
        dv1 = rstd * (dxh - jnp.mean(dxh, axis=-1, keepdims=True)
                      - xhat * jnp.mean(dxh * xhat, axis=-1, keepdims=True))
        acc(SG_CB31, dv1)
        extd31[pl.ds(0, tm), :] = dv1
        _shifted_copies(extd31, es31, tm + CONV31_HALO - SUBLANES)
        sgg = _sigmoid(glu_g)
        v0 = glu_v * sgg
        dv0 = jnp.zeros((tm, d), F32)
        for k in range(k31):
            term = _tap(extd31, es31, k31 - 1 - k, tm)
            dv0 = dv0 + cw31_ref[k:k + 1, :] * term
            acc(SG_CW31 + k, v0 * term)
        extd31[pl.ds(tm, CONV31_HALO), :] = extd31[pl.ds(0, CONV31_HALO), :]
        emit(2, dv0 * sgg)
        emit(3, dv0 * glu_v * sgg * (1.0 - sgg))

        dn = dn_parts[0]
        x = h_ref[...]
        rr = lax.rsqrt(jnp.mean(x * x, axis=-1, keepdims=True) + EPS)
        dx, dgp = _rms_bwd(dn, x, rr, g_ref[...])
        dh1_ref[...] = dh_ref[...] + dx
        sg_ref[SG_MIX:SG_MIX + 1, :] += dgp

    rev = lambda i: (nt - 1 - i, 0)
    row = pl.BlockSpec((tm, d), rev)
    wide = pl.BlockSpec((tm, n_in), rev)
    full = lambda a: pl.BlockSpec(a.shape, lambda i, nd=a.ndim: (0,) * nd)
    halo = pl.BlockSpec((halo_rows, d), lambda i: (jnp.maximum((nt - 1 - i) * per - 1, 0), 0))
    smalls = [cw4, wg, ba, bx, lam, cw31, lng, lnb]
    return _call(
        body, "mixer_bwd", (nt,),
        [row, row, full(g), wide, row, row, halo, row, row, row, _any()]
        + [full(a) for a in smalls] + [_any()],
        [row, wide, pl.BlockSpec((tm, 3 * d), rev), pl.BlockSpec((tm, 3 * d), rev),
         pl.BlockSpec((tm, 2 * d), rev), pl.BlockSpec((sg_rows, d), lambda i: (0, 0))],
        [jax.ShapeDtypeStruct((tp, d), F32), jax.ShapeDtypeStruct((tp, n_in), BF16),
         jax.ShapeDtypeStruct((tp, 3 * d), BF16), jax.ShapeDtypeStruct((tp, 3 * d), BF16),
         jax.ShapeDtypeStruct((tp, 2 * d), BF16), jax.ShapeDtypeStruct((sg_rows, d), F32)],
        [pltpu.VMEM(win_all.shape, BF16),
         pltpu.VMEM((3, d, d), BF16),
         pltpu.VMEM((tm + CONV4_HALO, d), F32),
         pltpu.VMEM((tm + CONV31_HALO, d), F32),
         pltpu.VMEM((SUBLANES, tm + CONV31_HALO, d), F32),
         pltpu.VMEM((SUBLANES, d), F32),
         pltpu.SemaphoreType.DMA((1 + 3 * N_DEV,))],
        [dh2, h, g, proj, xr_s, hs_s, hs_s, v1_s, ya_s, yb_s, win_all, *smalls, w3_all], comm)


def _tn_matmul(name, x, y, x_spec, y_spec, n_blocks, kb, nb, tm, tp, out_shape, out_spec, out_view, comm=None):
    nt = tp // tm

    def body(x_ref, y_ref, o_ref, acc):
        i = pl.program_id(1)

        @pl.when(i == 0)
        def _():
            acc[...] = jnp.zeros_like(acc)

        acc[...] += _tn(x_ref[...], y_ref[...])

        @pl.when(i == nt - 1)
        def _():
            o_ref[...] = acc[...].astype(BF16).reshape(out_view)

    outs, extra = _call(body, name, (n_blocks, nt), [x_spec, y_spec], [out_spec],
                        [jax.ShapeDtypeStruct(out_shape, BF16)], [pltpu.VMEM((kb, nb), F32)], [x, y], comm)
    return outs[0], extra


def kernel(x, meta_tokens, ffn1_norm, ffn1_w_gu, ffn1_w_down, mix_norm, w_in, b_in, rnn_conv_w, rnn_conv_b, rg_w_a, rg_b_a, rg_w_x, rg_b_x, rg_lambda, rnn_w_proj, conv_dw_w, conv_dw_b, conv_ln_g, conv_ln_b, conv_w_proj, conv_b_proj, w_out, ffn2_norm, ffn2_w_gu, ffn2_w_down, final_norm, loss_target, m_meta_tokens, m_ffn1_norm, m_ffn1_w_gu, m_ffn1_w_down, m_mix_norm, m_w_in, m_b_in, m_rnn_conv_w, m_rnn_conv_b, m_rg_w_a, m_rg_b_a, m_rg_w_x, m_rg_b_x, m_rg_lambda, m_rnn_w_proj, m_conv_dw_w, m_conv_dw_b, m_conv_ln_g, m_conv_ln_b, m_conv_w_proj, m_conv_b_proj, m_w_out, m_ffn2_norm, m_ffn2_w_gu, m_ffn2_w_down, m_final_norm, v_meta_tokens, v_ffn1_norm, v_ffn1_w_gu, v_ffn1_w_down, v_mix_norm, v_w_in, v_b_in, v_rnn_conv_w, v_rnn_conv_b, v_rg_w_a, v_rg_b_a, v_rg_w_x, v_rg_b_x, v_rg_lambda, v_rnn_w_proj, v_conv_dw_w, v_conv_dw_b, v_conv_ln_g, v_conv_ln_b, v_conv_w_proj, v_conv_b_proj, v_w_out, v_ffn2_norm, v_ffn2_w_gu, v_ffn2_w_down, v_final_norm):
    w = dict(locals())
    seq, d = x.shape[1], x.shape[2]
    n_meta = meta_tokens.shape[0]
    t_real = n_meta + seq
    tp, tm, tmx_fwd, tmx, tmt = _tiles(t_real)
    fb = ffn1_w_gu.shape[-1]
    wr = ffn1_w_down.shape[1]
    f = N_DEV * wr
    fc = f // FFN_CHUNKS
    nbc = w_in.shape[-1]
    n_in = N_DEV * nbc
    pr = rnn_w_proj.shape[1]
    hd = rg_w_a.shape[-1]
    gr = rg_w_a.shape[2]
    cw = meta_tokens.shape[1]
    k4, k31 = rnn_conv_w.shape[1], conv_dw_w.shape[1]
    assert n_in == 6 * d and 2 * wr == fb and N_HEADS * hd == d and pr * N_DEV == d

    xi, yi, ci = lax.axis_index("x"), lax.axis_index("y"), lax.axis_index("c")
    core = ci.astype(jnp.int32).reshape(1)
    chip = (2 * xi + yi).astype(jnp.int32).reshape(1)
    me_index = (4 * xi + 2 * yi + ci).astype(jnp.int32).reshape(1)

    for nm in ("ffn1_w_gu", "ffn2_w_gu"):
        for pre in ("", "m_", "v_"):
            w[pre + nm] = jnp.swapaxes(w[pre + nm], 1, 2)

    wgut1 = w["ffn1_w_gu"][0].astype(BF16)
    wgut2 = w["ffn2_w_gu"][0].astype(BF16)
    wd1 = ffn1_w_down[0].astype(BF16)
    wd2 = ffn2_w_down[0].astype(BF16)
    win_loc = w_in[0].astype(BF16)
    w3_loc = jnp.concatenate([rnn_w_proj[0], conv_w_proj[0], w_out[0]], axis=0).astype(BF16)
    wg_loc = jnp.stack([rg_w_a[0], rg_w_x[0]]).astype(BF16)
    n_small = n_meta + k4 + k31
    small_rows = -(-n_small // SUBLANES) * SUBLANES
    small_loc = jnp.concatenate([meta_tokens, rnn_conv_w[0], conv_dw_w[0],
                                 jnp.zeros((small_rows - n_small, cw), F32)], axis=0)
    wgut1_all, wd1_all, wg_all, small_all = _all_gather([wgut1, wd1, wg_loc, small_loc])
    wg = wg_all.transpose(1, 2, 0, 3, 4).reshape(2, N_HEADS, hd, hd)
    small_full = small_all.transpose(1, 0, 2).reshape(small_rows, d)
    meta_full = small_full[:n_meta]
    cw4 = small_full[n_meta:n_meta + k4]
    cw31 = small_full[n_meta + k4:n_meta + k4 + k31]

    pad = jnp.zeros((tp - t_real, d), F32)
    h0 = jnp.concatenate([meta_full, x[0], pad], axis=0)
    tgt = jnp.concatenate([jnp.zeros((n_meta, d), F32), loss_target[0], pad], axis=0)
    wgu1, wdn1 = wgut1_all.reshape(2 * f, d), wd1_all.reshape(f, d)
    (h1, gu1, n1), (win_all, w3_all) = _ffn_fwd(h0, ffn1_norm, wgu1, wdn1, tm, comm=_Gather([win_loc, w3_loc]))
    (h2, proj, n2, xr_s, hs_s, v1_s, ya_s, yb_s), (wgut2_all, wd2_all) = _mixer_fwd(
        h1, mix_norm, b_in, win_all, cw4, rnn_conv_b, wg, rg_b_a, rg_b_x, rg_lambda, cw31, conv_dw_b, conv_ln_g,
        conv_ln_b, conv_b_proj, w3_all, tmx_fwd, comm=_Gather([wgut2, wd2]))
    wgu2, wdn2 = wgut2_all.reshape(2 * f, d), wd2_all.reshape(f, d)
    (dh3, gu2, n3, loss_part, dgf), _ = _ffn_fwd(h2, ffn2_norm, wgu2, wdn2, tm,
                                                 loss=(tgt, final_norm.reshape(1, d), n_meta, t_real))

    def d_w_gu(tag, dgu, n_s, comm=None):
        g, extra = _tn_matmul(
            "d_w_gu" + tag, dgu, n_s,
            pl.BlockSpec((None, tmt, fc), lambda b, i: (b // FFN_CHUNKS, i, b % FFN_CHUNKS)),
            pl.BlockSpec((tmt, d), lambda b, i: (i, 0)),
            2 * FFN_CHUNKS, fc, d, tmt, tp, (2 * FFN_CHUNKS, fc, d),
            pl.BlockSpec((None, fc, d), lambda b, i: (b, 0, 0)), (fc, d), comm)
        return g.reshape(N_DEV, fb, d), extra

    def d_w_down(tag, act, df, comm=None):
        g, extra = _tn_matmul(
            "d_w_down" + tag, act, df,
            pl.BlockSpec((tmt, fc), lambda b, i: (i, b)), pl.BlockSpec((tmt, d), lambda b, i: (i, 0)),
            FFN_CHUNKS, fc, d, tmt, tp, (FFN_CHUNKS, fc, d),
            pl.BlockSpec((None, fc, d), lambda b, i: (b, 0, 0)), (fc, d), comm)
        return g.reshape(N_DEV, wr, d), extra

    (dh2, dgu2, act2, df2, dg_ffn2), _ = _ffn_bwd(dh3, h2, gu2, ffn2_norm, wgu2, wdn2, tm)
    g_wgu2, _ = d_w_gu("2", dgu2, n3)
    g_wd2, _ = d_w_down("2", act2, df2)
    (dh1, dproj, x3, y3, yg, sg), (r_wd2, r_wgu2) = _mixer_bwd(
        dh2, h1, mix_norm, proj, xr_s, hs_s, v1_s, ya_s, yb_s, win_all, cw4, wg, rg_b_a, rg_b_x, rg_lambda, cw31,
        conv_ln_g, conv_ln_b, w3_all, tmx, comm=_Scatter([g_wd2, g_wgu2]))
    g_w3, _ = _tn_matmul(
        "d_w_proj3", x3, y3,
        pl.BlockSpec((tmt, d), lambda b, i: (i, b)), pl.BlockSpec((tmt, d), lambda b, i: (i, b)),
        3, d, d, tmt, tp, (N_DEV, 3, pr, d), pl.BlockSpec((N_DEV, None, pr, d), lambda b, i: (0, b, 0, 0)),
        (N_DEV, pr, d))
    g_wg, _ = _tn_matmul(
        "d_w_gates", xr_s, yg,
        pl.BlockSpec((tmt, hd), lambda b, i: (i, b % N_HEADS)), pl.BlockSpec((tmt, hd), lambda b, i: (i, b)),
        2 * N_HEADS, hd, hd, tmt, tp, (N_DEV, 2 * N_HEADS, gr, hd),
        pl.BlockSpec((N_DEV, None, gr, hd), lambda b, i: (0, b, 0, 0)), (N_DEV, gr, hd))
    g_win, (r_w3, r_wg) = _tn_matmul(
        "d_w_in", n2, dproj,
        pl.BlockSpec((tmt, d), lambda b, i: (i, 0)), pl.BlockSpec((tmt, nbc), lambda b, i: (i, b)),
        N_DEV, d, nbc, tmt, tp, (N_DEV, d, nbc), pl.BlockSpec((None, d, nbc), lambda b, i: (b, 0, 0)), (d, nbc),
        comm=_Scatter([g_w3, g_wg]))
    dg_mix = sg[SG_MIX:SG_MIX + 1]
    (dh0, dgu1, act1, df1, dg_ffn1), (r_win,) = _ffn_bwd(dh1, h0, gu1, ffn1_norm, wgu1, wdn1, tm,
                                                         comm=_Scatter([g_win]))
    grad_x = dh0[n_meta:t_real][None]

    rep_rows = [("ffn1_norm", dg_ffn1), ("mix_norm", dg_mix), ("b_in", sg[SG_BIN:SG_BIN + 6]),
                ("rnn_conv_b", sg[SG_CB4:SG_CB4 + 1]), ("rg_b_a", sg[SG_BA:SG_BA + 1]),
                ("rg_b_x", sg[SG_BX:SG_BX + 1]), ("rg_lambda", sg[SG_LAM:SG_LAM + 1]),
                ("conv_dw_b", sg[SG_CB31:SG_CB31 + 1]), ("conv_ln_g", sg[SG_LNG:SG_LNG + 1]),
                ("conv_ln_b", sg[SG_LNB:SG_LNB + 1]), ("conv_b_proj", sg[SG_BCP:SG_BCP + 1]),
                ("ffn2_norm", dg_ffn2), ("final_norm", dgf)]
    col_rows = [("meta_tokens", dh0[:n_meta]), ("rnn_conv_w", sg[SG_CW4:SG_CW4 + k4]),
                ("conv_dw_w", sg[SG_CW31:SG_CW31 + k31])]
    layout, pieces, r0 = [], [], 0
    for nm, part in rep_rows:
        nr = part.shape[0]
        kind = "wide" if nm == "b_in" else "rep"
        as2d = lambda a: a.reshape(1, -1) if a.ndim == 1 else a
        layout.append((kind, r0, nr, as2d(w[nm]), as2d(w["m_" + nm]), as2d(w["v_" + nm])))
        pieces.append(part)
        r0 += nr
    for nm, part in col_rows:
        nr = part.shape[0]
        sq = lambda a: a.reshape(a.shape[-2], a.shape[-1])
        layout.append(("col", r0, nr, sq(w[nm]), sq(w["m_" + nm]), sq(w["v_" + nm])))
        pieces.append(part)
        r0 += nr
    total_rows = -(-(r0 + 1) // SUBLANES) * SUBLANES
    pieces.append(jnp.zeros((total_rows - 1 - r0, d), F32))
    pieces.append(loss_part)
    small_partial = jnp.concatenate(pieces, axis=0)

    g_wd1, (small_partials,) = d_w_down("1", act1, df1, comm=_Bcast(small_partial))
    g_wgu1, (r_wd1,) = d_w_gu("1", dgu1, n1, comm=_Scatter([g_wd1]))

    g_last = g_wgu1.reshape((4, 2) + g_wgu1.shape[1:])
    (from_sibling,) = _pair_exchange([g_last])
    comb_wgu1 = _pair_add(g_last, from_sibling, core)
    (r_wgu1,) = _chip_exchange([comb_wgu1])

    groups = [(g_wd1, r_wd1, me_index, ["ffn1_w_down"]), (comb_wgu1, r_wgu1, chip, ["ffn1_w_gu"]),
              (g_wd2, r_wd2, me_index, ["ffn2_w_down"]), (g_wgu2, r_wgu2, me_index, ["ffn2_w_gu"]),
              (g_win, r_win, me_index, ["w_in"]), (g_w3, r_w3, me_index, ["w_out", "rnn_w_proj", "conv_w_proj"]),
              (g_wg, r_wg, me_index, ["rg_w_a", "rg_w_x"])]
    res = {}
    for own, recv, idx, group in groups:
        outs = _final_adamw(own, recv, idx, [(w[nm], w["m_" + nm], w["v_" + nm]) for nm in group])
        for nm, o in zip(group, outs):
            res[nm] = o
    for nm in ("ffn1_w_gu", "ffn2_w_gu"):
        res[nm] = tuple(jnp.swapaxes(a, 1, 2) for a in res[nm])

    total, small_out = _small_adamw(small_partials, layout, me_index)
    for (nm, _), o in zip(rep_rows + col_rows, small_out):
        res[nm] = tuple(a.reshape(w[nm].shape) for a in o)

    order = ["meta_tokens", "ffn1_norm", "ffn1_w_gu", "ffn1_w_down", "mix_norm", "w_in", "b_in", "rnn_conv_w",
             "rnn_conv_b", "rg_w_a", "rg_b_a", "rg_w_x", "rg_b_x", "rg_lambda", "rnn_w_proj", "conv_dw_w",
             "conv_dw_b", "conv_ln_g", "conv_ln_b", "conv_w_proj", "conv_b_proj", "w_out", "ffn2_norm",
             "ffn2_w_gu", "ffn2_w_down", "final_norm"]
    return (total[total_rows - 1, 0], grad_x, *[res[nm][0] for nm in order], *[res[nm][1] for nm in order],
            *[res[nm][2] for nm in order], *[res[nm][3] for nm in order])
```

```python
import functools
import math

import jax
import jax.numpy as jnp
from jax import lax
from jax.experimental import pallas as pl
from jax.experimental.pallas import tpu as pltpu

F32 = jnp.float32
BF16 = jnp.bfloat16
MESH = pl.DeviceIdType.MESH
N_DEV = 8
N_HEADS = 4
RG_LRU_C = 8.0
EPS = 1e-6
FFN_RES = 0.5
ADAM_LR, ADAM_B1, ADAM_B2, ADAM_EPS, ADAM_WD, ADAM_STEP = 0.001, 0.9, 0.999, 1e-08, 0.01, 10
V7X_VMEM_LIMIT = 56 * 1024 * 1024
CONV4_HALO = 8
CONV31_HALO = 32
SUBLANES = 8
FFN_CHUNKS = 2
FFN_FWD_CHUNKS = 1
GELU_C = math.sqrt(2.0 / math.pi)
GELU_K = 0.044715


def _any():
    return pl.BlockSpec(memory_space=pl.ANY)


def _params(n_grid):
    return pltpu.CompilerParams(dimension_semantics=("arbitrary",) * n_grid, vmem_limit_bytes=V7X_VMEM_LIMIT)


def _nn(a, b):
    return jnp.dot(a, b, preferred_element_type=F32)


def _nt(a, b):
    return lax.dot_general(a, b, (((1,), (1,)), ((), ())), preferred_element_type=F32)


def _tn(a, b):
    return lax.dot_general(a, b, (((0,), (0,)), ((), ())), preferred_element_type=F32)


def _sigmoid(x):
    return 0.5 * jnp.tanh(0.5 * x) + 0.5


def _rowsum(x):
    return jnp.sum(x, axis=0, keepdims=True)


def _rms_fwd(x, g):
    r = lax.rsqrt(jnp.mean(x * x, axis=-1, keepdims=True) + EPS)
    return x * r * g, r


def _rms_bwd(dn, x, r, g):
    xr = x * r
    gy = dn * g
    dx = r * (gy - xr * jnp.mean(gy * xr, axis=-1, keepdims=True))
    return dx, _rowsum(dn * xr)


def _gelu(y):
    t = jnp.tanh(GELU_C * (y + GELU_K * y * y * y))
    return 0.5 * y * (1.0 + t), t


def _gelu_grad(y, t):
    return 0.5 * (1.0 + t) + 0.5 * y * (1.0 - t * t) * GELU_C * (1.0 + 3.0 * GELU_K * y * y)


def _softplus(x):
    return jnp.maximum(x, 0.0) + jnp.log(1.0 + jnp.exp(-jnp.abs(x)))


def _one_minus_exp(z):
    series = -z * (1.0 + 0.5 * z * (1.0 + z * (1.0 / 3.0) * (1.0 + 0.25 * z)))
    return jnp.where(z > -0.05, series, 1.0 - jnp.exp(z))


def _tiles(t_real):
    if t_real > 2048:
        tm = 384
        tp = -(-t_real // tm) * tm
        return tp, tm, tm // 2, tm // 2, tp // 2
    tm = 128
    tp = -(-t_real // tm) * tm
    return tp, tm, tm // 2, tm // 2, tm


def _load_weights(copies, sems):
    cps = [pltpu.make_async_copy(s, d, sems.at[k]) for k, (s, d) in enumerate(copies)]
    for cp in cps:
        cp.start()
    for cp in cps:
        cp.wait()


def _position():
    x, y, c = lax.axis_index("x"), lax.axis_index("y"), lax.axis_index("c")
    chips = [(1 - x, y), (x, 1 - y), (1 - x, 1 - y)]
    return x, y, c, chips


def _slot(p):
    return 4 * p[0] + 2 * p[1] + p[2]


class _Gather:
    def __init__(self, shards):
        self.shards = list(shards)
        self.n = len(self.shards)

    def inputs(self):
        return self.shards

    def out_shape(self):
        return [jax.ShapeDtypeStruct((N_DEV,) + s.shape, s.dtype) for s in self.shards]

    def scratch(self):
        return [pltpu.SemaphoreType.DMA((7 * self.n,)), pltpu.SemaphoreType.DMA((7 * self.n,)),
                pltpu.SemaphoreType.DMA((self.n,))]

    def _plan(self, ins, outs, sems):
        send_sems, recv_sems, local_sems = sems
        x, y, c, chips = _position()
        me, sibling = (x, y, c), (x, y, 1 - c)

        def copy(a, k, block, to, src=None):
            dst = outs[a].at[_slot(block)]
            return pltpu.make_async_remote_copy(
                src_ref=dst if src is None else src, dst_ref=dst,
                send_sem=send_sems.at[7 * a + k], recv_sem=recv_sems.at[7 * a + k],
                device_id=to, device_id_type=MESH)

        mine = [pltpu.make_async_copy(ins[a], outs[a].at[_slot(me)], local_sems.at[a]) for a in range(self.n)]
        first = []
        for a in range(self.n):
            first.append(copy(a, 0, me, sibling, src=ins[a]))
            first += [copy(a, 1 + j, me, (*chip, c), src=ins[a]) for j, chip in enumerate(chips)]
        return copy, mine, first, me, sibling, c, chips

    def start(self, ins, outs, sems):
        _, mine, first, *_ = self._plan(ins, outs, sems)
        for cp in mine + first:
            cp.start()

    def finish(self, ins, outs, sems):
        copy, mine, first, me, sibling, c, chips = self._plan(ins, outs, sems)
        passed = []
        for j, chip in enumerate(chips):
            for a in range(self.n):
                copy(a, 1 + j, (*chip, c), me).wait_recv()
                fwd = copy(a, 4 + j, (*chip, c), sibling)
                fwd.start()
                passed.append(fwd)
        for a in range(self.n):
            copy(a, 0, sibling, me).wait_recv()
            for j, chip in enumerate(chips):
                copy(a, 4 + j, (*chip, 1 - c), me).wait_recv()
        for cp in first + passed:
            cp.wait_send()
        for cp in mine:
            cp.wait()


class _Scatter:
    def __init__(self, grads):
        self.grads = list(grads)
        self.n = len(self.grads)

    def inputs(self):
        return self.grads

    def out_shape(self):
        return [jax.ShapeDtypeStruct((N_DEV - 1,) + g.shape[1:], g.dtype) for g in self.grads]

    def scratch(self):
        return [pltpu.SemaphoreType.DMA((7 * self.n,)), pltpu.SemaphoreType.DMA((7 * self.n,))]

    def _plan(self, ins, outs, sems):
        send_sems, recv_sems = sems
        x, y, c, _ = _position()
        cps = []
        for a in range(self.n):
            for k in range(1, N_DEV):
                peer = (x ^ (k >> 2), y ^ ((k >> 1) & 1), c ^ (k & 1))
                cps.append(pltpu.make_async_remote_copy(
                    src_ref=ins[a].at[_slot(peer)], dst_ref=outs[a].at[k - 1],
                    send_sem=send_sems.at[7 * a + k - 1], recv_sem=recv_sems.at[7 * a + k - 1],
                    device_id=peer, device_id_type=MESH))
        return cps

    def start(self, ins, outs, sems):
        for cp in self._plan(ins, outs, sems):
            cp.start()

    def finish(self, ins, outs, sems):
        for cp in self._plan(ins, outs, sems):
            cp.wait()


def _hosted(inner, n_in, n_out, comm, grid):
    if comm is None:
        return inner
    nc_in, nc_out, ns = len(comm.inputs()), len(comm.out_shape()), len(comm.scratch())

    def body(*refs):
        o0 = n_in + nc_in
        s0 = o0 + n_out + nc_out
        main = refs[:n_in] + refs[o0:o0 + n_out] + refs[s0:len(refs) - ns]
        c_in, c_out, c_sems = refs[n_in:o0], refs[o0 + n_out:s0], refs[len(refs) - ns:]
        ids = [pl.program_id(ax) for ax in range(len(grid))]
        first = functools.reduce(jnp.logical_and, [i == 0 for i in ids])
        last = functools.reduce(jnp.logical_and, [i == g - 1 for i, g in zip(ids, grid)])

        @pl.when(first)
        def _():
            comm.start(c_in, c_out, c_sems)

        inner(*main)

        @pl.when(last)
        def _():
            comm.finish(c_in, c_out, c_sems)

    return body


def _call(inner, name, grid, in_specs, out_specs, out_shape, scratch, args, comm=None):
    n_in, n_out = len(args), len(out_shape)
    body = _hosted(inner, n_in, n_out, comm, grid)
    if comm is not None:
        in_specs = list(in_specs) + [_any()] * len(comm.inputs())
        args = list(args) + comm.inputs()
        out_specs = list(out_specs) + [_any()] * len(comm.out_shape())
        out_shape = list(out_shape) + comm.out_shape()
        scratch = list(scratch) + comm.scratch()
    outs = pl.pallas_call(
        body, name=name, grid=grid, in_specs=list(in_specs), out_specs=list(out_specs), out_shape=list(out_shape),
        scratch_shapes=list(scratch), compiler_params=_params(len(grid)))(*args)
    return list(outs[:n_out]), list(outs[n_out:])


class _Bcast:
    def __init__(self, block):
        self.block = block

    def inputs(self):
        return [self.block]

    def out_shape(self):
        return [jax.ShapeDtypeStruct((N_DEV,) + self.block.shape, self.block.dtype)]

    def scratch(self):
        return [pltpu.SemaphoreType.DMA((N_DEV - 1,)), pltpu.SemaphoreType.DMA((N_DEV - 1,)),
                pltpu.SemaphoreType.DMA((1,))]

    def _plan(self, ins, outs, sems):
        send_sems, recv_sems, local_sem = sems
        x, y, c, _ = _position()
        mine = outs[0].at[_slot((x, y, c))]
        cps = []
        for k in range(1, N_DEV):
            peer = (x ^ (k >> 2), y ^ ((k >> 1) & 1), c ^ (k & 1))
            cps.append(pltpu.make_async_remote_copy(
                src_ref=ins[0], dst_ref=mine, send_sem=send_sems.at[k - 1], recv_sem=recv_sems.at[k - 1],
                device_id=peer, device_id_type=MESH))
        return pltpu.make_async_copy(ins[0], mine, local_sem.at[0]), cps

    def start(self, ins, outs, sems):
        own, cps = self._plan(ins, outs, sems)
        own.start()
        for cp in cps:
            cp.start()

    def finish(self, ins, outs, sems):
        own, cps = self._plan(ins, outs, sems)
        for cp in cps:
            cp.wait()
        own.wait()


def _all_gather(shards):
    comm = _Gather(shards)
    n = comm.n

    def body(*refs):
        comm.start(refs[:n], refs[n:2 * n], refs[2 * n:])
        comm.finish(refs[:n], refs[n:2 * n], refs[2 * n:])

    return pl.pallas_call(
        body, name="weights_all_gather", out_shape=comm.out_shape(),
        in_specs=[_any()] * n, out_specs=[_any()] * n, scratch_shapes=comm.scratch(),
    )(*shards)


def _pair_exchange(grads):
    n = len(grads)

    def body(*refs):
        ins, outs = refs[:n], refs[n:2 * n]
        send_sems, recv_sems = refs[2 * n:]
        x, y, c, _ = _position()
        cps = [pltpu.make_async_remote_copy(
            src_ref=ins[a].at[:, 1 - c], dst_ref=outs[a],
            send_sem=send_sems.at[a], recv_sem=recv_sems.at[a],
            device_id=(x, y, 1 - c), device_id_type=MESH) for a in range(n)]
        for cp in cps:
            cp.start()
        for cp in cps:
            cp.wait()

    return pl.pallas_call(
        body, name="grads_pair_exchange",
        out_shape=[jax.ShapeDtypeStruct((4,) + g.shape[2:], g.dtype) for g in grads],
        in_specs=[_any()] * n, out_specs=[_any()] * n,
        scratch_shapes=[pltpu.SemaphoreType.DMA((n,)), pltpu.SemaphoreType.DMA((n,))],
    )(*grads)


def _chip_exchange(combs):
    n = len(combs)

    def body(*refs):
        ins, outs = refs[:n], refs[n:2 * n]
        send_sems, recv_sems = refs[2 * n:]
        x, y, c, chips = _position()
        cps = []
        for a in range(n):
            for j, (cx, cy) in enumerate(chips):
                cps.append(pltpu.make_async_remote_copy(
                    src_ref=ins[a].at[2 * cx + cy], dst_ref=outs[a].at[j],
                    send_sem=send_sems.at[3 * a + j], recv_sem=recv_sems.at[3 * a + j],
                    device_id=(cx, cy, c), device_id_type=MESH))
        for cp in cps:
            cp.start()
        for cp in cps:
            cp.wait()

    return pl.pallas_call(
        body, name="grads_chip_exchange",
        out_shape=[jax.ShapeDtypeStruct((3,) + g.shape[1:], g.dtype) for g in combs],
        in_specs=[_any()] * n, out_specs=[_any()] * n,
        scratch_shapes=[pltpu.SemaphoreType.DMA((3 * n,)), pltpu.SemaphoreType.DMA((3 * n,))],
    )(*combs)


def _pair_add(grad, recv, core):
    blk = grad.shape[2:]
    zeros = (0,) * len(blk)

    def body(core_ref, g_ref, r_ref, o_ref):
        del core_ref
        o_ref[...] = (g_ref[...].astype(F32) + r_ref[...].astype(F32)).astype(BF16)

    return pl.pallas_call(
        body, name="grads_pair_add",
        out_shape=jax.ShapeDtypeStruct((4,) + blk, BF16),
        grid_spec=pltpu.PrefetchScalarGridSpec(
            num_scalar_prefetch=1, grid=(4,),
            in_specs=[pl.BlockSpec((None, None) + blk, lambda i, cr: (i, cr[0]) + zeros),
                      pl.BlockSpec((None,) + blk, lambda i, cr: (i,) + zeros)],
            out_specs=pl.BlockSpec((None,) + blk, lambda i, cr: (i,) + zeros)),
        compiler_params=_params(1),
    )(core, grad, recv)


def _adamw(w, g, m, v):
    m2 = ADAM_B1 * m + (1.0 - ADAM_B1) * g
    v2 = ADAM_B2 * v + (1.0 - ADAM_B2) * (g * g)
    m_hat = m2 / (1.0 - ADAM_B1 ** ADAM_STEP)
    v_hat = v2 / (1.0 - ADAM_B2 ** ADAM_STEP)
    delta = -ADAM_LR * (m_hat / (jnp.sqrt(v_hat) + ADAM_EPS) + ADAM_WD * w)
    return delta, m2, v2


def _final_adamw(own, recv, idx, parts):
    blk = own.shape[1:]
    n_recv = recv.shape[0]
    n_parts = len(parts)
    per = blk[0] // n_parts if n_parts > 1 else None
    rows = blk[-2]
    n_chunks = 1 if n_parts > 1 else (4 if rows % 64 == 0 and rows >= 512 else (2 if rows % 32 == 0 else 1))
    cblk = blk[:-2] + (rows // n_chunks, blk[-1])
    lead = (0,) * (len(blk) - 2)

    def body(idx_ref, c_ref, r_ref, *refs):
        del idx_ref
        ins, outs = refs[:3 * n_parts], refs[3 * n_parts:]
        g = c_ref[...].astype(F32)
        for k in range(n_recv):
            g = g + r_ref[k].astype(F32)
        for p in range(n_parts):
            w_ref, m_ref, v_ref = ins[3 * p:3 * p + 3]
            if n_parts == 1:
                gp = g
            elif per == 1:
                gp = g[p]
            else:
                gp = g[p * per:(p + 1) * per]
            delta, m2, v2 = _adamw(w_ref[0], gp, m_ref[0], v_ref[0])
            o = outs[4 * p:4 * p + 4]
            o[0][0] = gp
            o[1][0] = delta
            o[2][0] = m2
            o[3][0] = v2

    flat = [a for wmv in parts for a in wmv]

    def part_spec(a):
        shape = a.shape[:-2] + (a.shape[-2] // n_chunks, a.shape[-1])
        return pl.BlockSpec(shape, lambda i, cr, nd=a.ndim: (0,) * (nd - 2) + (i, 0))

    outs = pl.pallas_call(
        body, name="grads_sum_adamw",
        out_shape=[jax.ShapeDtypeStruct(wmv[0].shape, F32) for wmv in parts for _ in range(4)],
        grid_spec=pltpu.PrefetchScalarGridSpec(
            num_scalar_prefetch=1, grid=(n_chunks,),
            in_specs=[pl.BlockSpec((None,) + cblk, lambda i, cr: (cr[0],) + lead + (i, 0)),
                      pl.BlockSpec((n_recv,) + cblk, lambda i, cr: (0,) + lead + (i, 0))]
                     + [part_spec(a) for a in flat],
            out_specs=[part_spec(wmv[0]) for wmv in parts for _ in range(4)]),
        compiler_params=_params(1),
    )(idx, own, recv, *flat)
    return [tuple(outs[4 * p:4 * p + 4]) for p in range(n_parts)]


def _small_adamw(partials, layout, me_index):
    _, rows, d = partials.shape
    n = len(layout)
    cw = d // N_DEV

    def body(me_ref, p_ref, *refs):
        ins, t_ref, outs = refs[:3 * n], refs[3 * n], refs[3 * n + 1:]
        me = me_ref[0]
        total = p_ref[0]
        for j in range(1, N_DEV):
            total = total + p_ref[j]
        t_ref[...] = total
        for e, (kind, r0, nr, _, _, _) in enumerate(layout):
            w_ref, m_ref, v_ref = ins[3 * e:3 * e + 3]
            o = outs[4 * e:4 * e + 4]
            if kind == "rep":
                g = t_ref[r0:r0 + nr, :]
                delta, m2, v2 = _adamw(w_ref[...], g, m_ref[...], v_ref[...])
                for ref, val in zip(o, (g, delta, m2, v2)):
                    ref[...] = val
            elif kind == "wide":
                for q in range(nr):
                    sl = slice(q * d, (q + 1) * d)
                    g = t_ref[r0 + q:r0 + q + 1, :]
                    delta, m2, v2 = _adamw(w_ref[:, sl], g, m_ref[:, sl], v_ref[:, sl])
                    for ref, val in zip(o, (g, delta, m2, v2)):
                        ref[:, sl] = val
            else:
                for j in range(N_DEV):
                    @pl.when(me == j)
                    def _(j=j, o=o, w_ref=w_ref, m_ref=m_ref, v_ref=v_ref, r0=r0, nr=nr):
                        g = t_ref[r0:r0 + nr, j * cw:(j + 1) * cw]
                        delta, m2, v2 = _adamw(w_ref[...], g, m_ref[...], v_ref[...])
                        for ref, val in zip(o, (g, delta, m2, v2)):
                            ref[...] = val

    flat = [a for ent in layout for a in ent[3:]]
    vm = pl.BlockSpec(memory_space=pltpu.VMEM)
    outs = pl.pallas_call(
        body, name="small_adamw",
        out_shape=[jax.ShapeDtypeStruct((rows, d), F32)]
                  + [jax.ShapeDtypeStruct(ent[3].shape, F32) for ent in layout for _ in range(4)],
        in_specs=[pl.BlockSpec(memory_space=pltpu.SMEM), vm] + [vm] * len(flat),
        out_specs=[vm] * (1 + 4 * n),
        compiler_params=pltpu.CompilerParams(vmem_limit_bytes=V7X_VMEM_LIMIT),
    )(me_index, partials, *flat)
    return outs[0], [tuple(outs[1 + 4 * e:5 + 4 * e]) for e in range(n)]


def _ffn_fwd(h, g, wgu, wd, tm, loss=None, comm=None):
    tp, d = h.shape
    f = wd.shape[0]
    fc = f // FFN_FWD_CHUNKS
    nt = tp // tm
    with_loss = loss is not None
    if with_loss:
        tgt, gf, n_meta, t_real = loss

    def body(*refs):
        if with_loss:
            (h_ref, g_ref, wgu_hbm, wd_hbm, tgt_ref, gf_ref, out_ref, gu_ref, n_ref, loss_ref, dgf_ref,
             wgu_v, wd_v, sems) = refs
        else:
            h_ref, g_ref, wgu_hbm, wd_hbm, out_ref, gu_ref, n_ref, wgu_v, wd_v, sems = refs
        i = pl.program_id(0)

        @pl.when(i == 0)
        def _():
            _load_weights([(wgu_hbm, wgu_v), (wd_hbm, wd_v)], sems)
            if with_loss:
                loss_ref[...] = jnp.zeros_like(loss_ref)
                dgf_ref[...] = jnp.zeros_like(dgf_ref)

        x = h_ref[...]
        n, _ = _rms_fwd(x, g_ref[...])
        nb = n.astype(BF16)
        n_ref[...] = nb
        acc = jnp.zeros((tm, d), F32)
        for j in range(FFN_FWD_CHUNKS):
            cols = slice(j * fc, (j + 1) * fc)
            gate = _nt(nb, wgu_v[pl.ds(j * fc, fc), :])
            up = _nt(nb, wgu_v[pl.ds(f + j * fc, fc), :])
            gu_ref[0, :, cols] = gate.astype(BF16)
            gu_ref[1, :, cols] = up.astype(BF16)
            act = (gate * _sigmoid(gate) * up).astype(BF16)
            acc = acc + _nn(act, wd_v[pl.ds(j * fc, fc), :])
        hn = x + FFN_RES * acc
        if not with_loss:
            out_ref[...] = hn
        else:
            gfv = gf_ref[...]
            r = lax.rsqrt(jnp.mean(hn * hn, axis=-1, keepdims=True) + EPS)
            xr = hn * r
            rows = i * tm + lax.broadcasted_iota(jnp.int32, (tm, 1), 0)
            mask = jnp.logical_and(rows >= n_meta, rows < t_real)
            diff = jnp.where(mask, xr * gfv - tgt_ref[...], 0.0)
            loss_ref[...] += jnp.zeros_like(loss_ref) + 0.5 * jnp.sum(diff * diff) / d
            dy = diff / d
            gy = dy * gfv
            out_ref[...] = r * (gy - xr * jnp.mean(gy * xr, axis=-1, keepdims=True))
            dgf_ref[...] += _rowsum(dy * xr)

    row = pl.BlockSpec((tm, d), lambda i: (i, 0))
    vec = pl.BlockSpec((1, d), lambda i: (0, 0))
    in_specs = [row, vec, _any(), _any()]
    out_shape = [jax.ShapeDtypeStruct((tp, d), F32), jax.ShapeDtypeStruct((2, tp, f), BF16),
                 jax.ShapeDtypeStruct((tp, d), BF16)]
    out_specs = [row, pl.BlockSpec((2, tm, f), lambda i: (0, i, 0)), row]
    args = [h, g, wgu, wd]
    if with_loss:
        in_specs += [row, vec]
        out_shape += [jax.ShapeDtypeStruct((1, d), F32), jax.ShapeDtypeStruct((1, d), F32)]
        out_specs += [vec, vec]
        args += [tgt, gf]
    return _call(body, "ffn_fwd_loss" if with_loss else "ffn_fwd", (nt,), in_specs, out_specs, out_shape,
                 [pltpu.VMEM((2 * f, d), BF16), pltpu.VMEM((f, d), BF16), pltpu.SemaphoreType.DMA((2,))],
                 args, comm)


def _ffn_bwd(dh, h, gu, g, wgu, wd, tm, comm=None):
    tp, d = h.shape
    f = wd.shape[0]
    fc = f // FFN_CHUNKS
    nt = tp // tm

    def body(dh_ref, h_ref, gu_ref, g_ref, wgu_hbm, wd_hbm,
             dhin_ref, dgu_ref, act_ref, df_ref, dg_ref, wgu_v, wd_v, dn_v, sems):
        i, j = pl.program_id(0), pl.program_id(1)

        @pl.when(jnp.logical_and(i == 0, j == 0))
        def _():
            _load_weights([(wgu_hbm, wgu_v), (wd_hbm, wd_v)], sems)
            dg_ref[...] = jnp.zeros_like(dg_ref)

        dfb = (FFN_RES * dh_ref[...]).astype(BF16)

        @pl.when(j == 0)
        def _():
            df_ref[...] = dfb
            dn_v[...] = jnp.zeros_like(dn_v)

        lo = pl.multiple_of(j * fc, 16)
        dact = _nt(dfb, wd_v[pl.ds(lo, fc), :])
        gate = gu_ref[0].astype(F32)
        up = gu_ref[1].astype(F32)
        sg = _sigmoid(gate)
        silu = gate * sg
        act_ref[...] = (silu * up).astype(BF16)
        dgate = (dact * up * (sg * (1.0 + gate * (1.0 - sg)))).astype(BF16)
        dup = (dact * silu).astype(BF16)
        dgu_ref[0] = dgate
        dgu_ref[1] = dup
        dn_v[...] += _nn(dgate, wgu_v[pl.ds(lo, fc), :]) + _nn(dup, wgu_v[pl.ds(pl.multiple_of(f + j * fc, 16), fc), :])

        @pl.when(j == FFN_CHUNKS - 1)
        def _():
            x = h_ref[...]
            r = lax.rsqrt(jnp.mean(x * x, axis=-1, keepdims=True) + EPS)
            dx, dgp = _rms_bwd(dn_v[...], x, r, g_ref[...])
            dhin_ref[...] = dh_ref[...] + dx
            dg_ref[...] += dgp

    row = pl.BlockSpec((tm, d), lambda i, j: (i, 0))
    vec = pl.BlockSpec((1, d), lambda i, j: (0, 0))
    hid2 = pl.BlockSpec((2, tm, fc), lambda i, j: (0, i, j))
    return _call(
        body, "ffn_bwd", (nt, FFN_CHUNKS),
        [row, row, hid2, vec, _any(), _any()],
        [row, hid2, pl.BlockSpec((tm, fc), lambda i, j: (i, j)), row, vec],
        [jax.ShapeDtypeStruct((tp, d), F32), jax.ShapeDtypeStruct((2, tp, f), BF16),
         jax.ShapeDtypeStruct((tp, f), BF16), jax.ShapeDtypeStruct((tp, d), BF16),
         jax.ShapeDtypeStruct((1, d), F32)],
        [pltpu.VMEM((2 * f, d), BF16), pltpu.VMEM((f, d), BF16), pltpu.VMEM((tm, d), F32),
         pltpu.SemaphoreType.DMA((2,))],
        [dh, h, gu, g, wgu, wd], comm)


def _piece_segments(q, d, nb_cols):
    segs = []
    for j in range(N_DEV):
        lo, hi = max(q * d, j * nb_cols), min((q + 1) * d, (j + 1) * nb_cols)
        if lo < hi:
            segs.append((j, lo - q * d, hi - q * d, lo - j * nb_cols, hi - j * nb_cols))
    return segs


def _w3_copies(w3_hbm, rows, w3_v):
    return [(w3_hbm.at[k, pl.ds(q * rows, rows)], w3_v.at[q, pl.ds(k * rows, rows)])
            for q in range(3) for k in range(N_DEV)]


def _gates(xrb, wg_ref, ba, bx, lam, hd):
    pre_r, pre_i = [], []
    for hh in range(N_HEADS):
        xh = xrb[:, hh * hd:(hh + 1) * hd]
        pre_r.append(_nn(xh, wg_ref[0, hh]))
        pre_i.append(_nn(xh, wg_ref[1, hh]))
    r = _sigmoid(jnp.concatenate(pre_r, axis=1) + ba)
    ig = _sigmoid(jnp.concatenate(pre_i, axis=1) + bx)
    sp = _softplus(-lam)
    log_a = -RG_LRU_C * r * sp
    a = jnp.exp(log_a)
    s = jnp.sqrt(_one_minus_exp(2.0 * log_a))
    return r, ig, sp, a, s


def _scan_fwd(a, u, h_prev):
    tm = a.shape[0]
    rows = lax.broadcasted_iota(jnp.int32, a.shape, 0)
    d = 1
    while d < tm:
        if d < SUBLANES:
            keep = rows >= d
            u = jnp.where(keep, a * pltpu.roll(u, d, 0) + u, u)
            a = jnp.where(keep, a * pltpu.roll(a, d, 0), a)
        else:
            u = jnp.concatenate([u[:d], a[d:] * u[:tm - d] + u[d:]], axis=0)
            a = jnp.concatenate([a[:d], a[d:] * a[:tm - d]], axis=0)
        d *= 2
    return u + a * h_prev


def _scan_bwd(b, v, g_next):
    tm = b.shape[0]
    rows = lax.broadcasted_iota(jnp.int32, b.shape, 0)
    d = 1
    while d < tm:
        if d < SUBLANES:
            keep = rows < tm - d
            v = jnp.where(keep, v + b * pltpu.roll(v, tm - d, 0), v)
            b = jnp.where(keep, b * pltpu.roll(b, tm - d, 0), b)
        else:
            v = jnp.concatenate([v[:tm - d] + b[:tm - d] * v[d:], v[tm - d:]], axis=0)
            b = jnp.concatenate([b[:tm - d] * b[d:], b[tm - d:]], axis=0)
        d *= 2
    return v + b * g_next


def _shifted_copies(ext_ref, es_ref, n_rows):
    for s in range(1, SUBLANES):
        es_ref[s, pl.ds(0, n_rows), :] = ext_ref[pl.ds(s, n_rows), :]


def _tap(ext_ref, es_ref, off, tm):
    q, s = divmod(off, SUBLANES)
    if s == 0:
        return ext_ref[pl.ds(SUBLANES * q, tm), :]
    return es_ref[s, pl.ds(SUBLANES * q, tm), :]


def _mixer_fwd(h, g, b_in, win_all, cw4, cb4, wg, ba, bx, lam, cw31, cb31, lng, lnb, bcp, w3_all, tm, comm=None):
    tp, d = h.shape
    nb_cols = win_all.shape[-1]
    n_in = N_DEV * nb_cols
    hd = wg.shape[-1]
    k4, k31 = cw4.shape[0], cw31.shape[0]
    w3_rows = d // N_DEV

    def body(h_ref, g_ref, b_ref, win_hbm, cw4_ref, cb4_ref, wg_ref, ba_ref, bx_ref, lam_ref, cw31_ref, cb31_ref,
             lng_ref, lnb_ref, bcp_ref, w3_hbm,
             h2_ref, p_ref, n_ref, xr_ref, hs_ref, v1_ref, ya_ref, yb_ref,
             win_v, w3_v, ext4, ext31, es31, hcar, sems):
        @pl.when(pl.program_id(0) == 0)
        def _():
            _load_weights([(win_hbm, win_v)] + _w3_copies(w3_hbm, w3_rows, w3_v), sems)
            ext4[pl.ds(0, CONV4_HALO), :] = jnp.zeros((CONV4_HALO, d), F32)
            ext31[pl.ds(0, CONV31_HALO), :] = jnp.zeros((CONV31_HALO, d), F32)
            hcar[...] = jnp.zeros_like(hcar)

        n, _ = _rms_fwd(h_ref[...], g_ref[...])
        nb = n.astype(BF16)
        n_ref[...] = nb

        def piece(q):
            parts = [_nn(nb, win_v[j, :, bl:bh]) for j, _, _, bl, bh in _piece_segments(q, d, nb_cols)]
            pq = (jnp.concatenate(parts, axis=1) + b_ref[:, q * d:(q + 1) * d]).astype(BF16)
            p_ref[:, q * d:(q + 1) * d] = pq
            return pq.astype(F32)

        x_rnn, y_rnn, glu_v, glu_g, gate_a, gate_b = [piece(q) for q in range(6)]

        ext4[pl.ds(CONV4_HALO, tm), :] = x_rnn
        xr = cb4_ref[...] + jnp.zeros((tm, d), F32)
        for k in range(k4):
            xr = xr + cw4_ref[k:k + 1, :] * ext4[pl.ds(CONV4_HALO - (k4 - 1) + k, tm), :]
        ext4[pl.ds(0, CONV4_HALO), :] = ext4[pl.ds(tm, CONV4_HALO), :]
        xrb = xr.astype(BF16)
        xr_ref[...] = xrb
        xr = xrb.astype(F32)
        _, ig, _, a, s = _gates(xrb, wg_ref, ba_ref[...], bx_ref[...], lam_ref[...], hd)
        hseq = _scan_fwd(a, s * (ig * xr), hcar[0:1, :])
        hcar[0:1, :] = hseq[tm - 1:tm, :]
        hs_ref[...] = hseq.astype(BF16)
        gl, _ = _gelu(y_rnn)
        ya = _nn((hseq * gl).astype(BF16), w3_v[0])
        ya_ref[...] = ya.astype(BF16)

        ext31[pl.ds(CONV31_HALO, tm), :] = glu_v * _sigmoid(glu_g)
        _shifted_copies(ext31, es31, tm + CONV31_HALO - SUBLANES)
        v1 = cb31_ref[...] + jnp.zeros((tm, d), F32)
        for k in range(k31):
            v1 = v1 + cw31_ref[k:k + 1, :] * _tap(ext31, es31, CONV31_HALO - (k31 - 1) + k, tm)
        ext31[pl.ds(0, CONV31_HALO), :] = ext31[pl.ds(tm, CONV31_HALO), :]
        v1b = v1.astype(BF16)
        v1_ref[...] = v1b
        v1 = v1b.astype(F32)
        xc = v1 - jnp.mean(v1, axis=-1, keepdims=True)
        rstd = lax.rsqrt(jnp.mean(xc * xc, axis=-1, keepdims=True) + EPS)
        v2 = xc * rstd * lng_ref[...] + lnb_ref[...]
        yb = _nn((v2 * _sigmoid(v2)).astype(BF16), w3_v[1]) + bcp_ref[...]
        yb_ref[...] = yb.astype(BF16)

        merged = _sigmoid(gate_a) * ya + _sigmoid(gate_b) * yb
        h2_ref[...] = h_ref[...] + _nn(merged.astype(BF16), w3_v[2])

    row = pl.BlockSpec((tm, d), lambda i: (i, 0))
    wide = pl.BlockSpec((tm, n_in), lambda i: (i, 0))
    full = lambda a: pl.BlockSpec(a.shape, lambda i, nd=a.ndim: (0,) * nd)
    smalls = [cw4, cb4, wg, ba, bx, lam, cw31, cb31, lng, lnb, bcp]
    return _call(
        body, "mixer_fwd", (tp // tm,),
        [row, full(g), full(b_in), _any()] + [full(a) for a in smalls] + [_any()],
        [row, wide] + [row] * 6,
        [jax.ShapeDtypeStruct((tp, d), F32), jax.ShapeDtypeStruct((tp, n_in), BF16)]
        + [jax.ShapeDtypeStruct((tp, d), BF16)] * 6,
        [pltpu.VMEM(win_all.shape, BF16),
         pltpu.VMEM((3, d, d), BF16),
         pltpu.VMEM((tm + CONV4_HALO, d), F32),
         pltpu.VMEM((tm + CONV31_HALO, d), F32),
         pltpu.VMEM((SUBLANES, tm + CONV31_HALO, d), F32),
         pltpu.VMEM((SUBLANES, d), F32),
         pltpu.SemaphoreType.DMA((1 + 3 * N_DEV,))],
        [h, g, b_in, win_all, *smalls, w3_all], comm)


SG_BIN, SG_CW4, SG_CB4, SG_BA, SG_BX, SG_LAM, SG_CB31, SG_LNG, SG_LNB, SG_BCP, SG_MIX, SG_CW31 = 0, 6, 10, 11, 12, 13, 14, 15, 16, 17, 18, 19


def _mixer_bwd(dh2, h, g, proj, xr_s, hs_s, v1_s, ya_s, yb_s, win_t, cw4, wg, ba, bx, lam, cw31, lng, lnb, w3_all, tm,
               comm=None):
    tp, d = dh2.shape
    n_in = proj.shape[1]
    hd = wg.shape[-1]
    k4, k31 = cw4.shape[0], cw31.shape[0]
    nt = tp // tm
    w3_rows = d // N_DEV
    sg_rows = -(-(SG_CW31 + k31) // SUBLANES) * SUBLANES
    halo_rows = 16
    per = tm // halo_rows

    def body(dh_ref, h_ref, g_ref, p_ref, xr_ref, hs_ref, hh_ref, v1_ref, ya_ref, yb_ref, win_hbm,
             cw4_ref, wg_ref, wgt_ref, ba_ref, bx_ref, lam_ref, cw31_ref, lng_ref, lnb_ref, w3_hbm,
             dh1_ref, dp_ref, x3_ref, y3_ref, yg_ref, sg_ref,
             win_v, w3_v, extd4, extd31, es31, gcar, sems):
        i = pl.program_id(0)
        tile = nt - 1 - i

        @pl.when(i == 0)
        def _():
            _load_weights([(win_hbm, win_v)] + _w3_copies(w3_hbm, w3_rows, w3_v), sems)
            for q in range(3):
                w3_v[q] = w3_v[q].T
            extd4[pl.ds(tm, CONV4_HALO), :] = jnp.zeros((CONV4_HALO, d), F32)
            extd31[pl.ds(tm, CONV31_HALO), :] = jnp.zeros((CONV31_HALO, d), F32)
            gcar[...] = jnp.zeros_like(gcar)
            sg_ref[...] = jnp.zeros_like(sg_ref)

        def acc(row, val):
            sg_ref[row:row + 1, :] += _rowsum(val)

        rows = lax.broadcasted_iota(jnp.int32, (tm, d), 0)
        x_rnn = p_ref[:, 0:d].astype(F32)
        y_rnn = p_ref[:, d:2 * d].astype(F32)
        glu_v = p_ref[:, 2 * d:3 * d].astype(F32)
        glu_g = p_ref[:, 3 * d:4 * d].astype(F32)
        sga = _sigmoid(p_ref[:, 4 * d:5 * d].astype(F32))
        sgb = _sigmoid(p_ref[:, 5 * d:6 * d].astype(F32))
        ya = ya_ref[...].astype(F32)
        yb = yb_ref[...].astype(F32)

        dmob = dh_ref[...].astype(BF16)
        dmerged = _nn(dmob, w3_v[2])
        x3_ref[:, 0:d] = (sga * ya + sgb * yb).astype(BF16)
        y3_ref[:, 0:d] = dmob
        dya = sga * dmerged
        dyb = sgb * dmerged
        dn_parts = []

        def emit(q, val):
            vb = val.astype(BF16)
            dp_ref[:, q * d:(q + 1) * d] = vb
            acc(SG_BIN + q, val)
            term = _nn(vb, win_v[pl.ds(q * d, d), :])
            dn_parts[:] = [term if not dn_parts else dn_parts[0] + term]

        emit(4, dmerged * ya * sga * (1.0 - sga))
        emit(5, dmerged * yb * sgb * (1.0 - sgb))

        dyab = dya.astype(BF16)
        y3_ref[:, d:2 * d] = dyab
        dza = _nn(dyab, w3_v[0])
        hsv = hs_ref[...].astype(F32)
        gl, th = _gelu(y_rnn)
        x3_ref[:, d:2 * d] = (hsv * gl).astype(BF16)
        emit(1, dza * hsv * _gelu_grad(y_rnn, th))
        dhs = dza * gl
        xrb = xr_ref[...]
        xr = xrb.astype(F32)
        lam_v = lam_ref[...]
        r, ig, sp, a, s = _gates(xrb, wg_ref, ba_ref[...], bx_ref[...], lam_v, hd)
        b = jnp.where(rows == tm - 1, gcar[1:2, :], pltpu.roll(a, tm - 1, 0))
        big_g = _scan_bwd(b, dhs, gcar[0:1, :])
        gcar[0:1, :] = big_g[0:1, :]
        gcar[1:2, :] = a[0:1, :]
        h_before = jnp.where(tile > 0, hh_ref[halo_rows - 1:halo_rows, :].astype(F32), 0.0)
        h_prev = jnp.where(rows == 0, h_before, pltpu.roll(hsv, 1, 0))
        ds = big_g * ig * xr
        dla = big_g * h_prev * a - ds * (a * a) / jnp.maximum(s, 1e-20)
        acc(SG_LAM, dla * r * (RG_LRU_C * _sigmoid(-lam_v)))
        dpr = dla * (-RG_LRU_C * sp) * r * (1.0 - r)
        dpi = big_g * s * xr * ig * (1.0 - ig)
        acc(SG_BA, dpr)
        acc(SG_BX, dpi)
        dprb = dpr.astype(BF16)
        dpib = dpi.astype(BF16)
        yg_ref[:, 0:d] = dprb
        yg_ref[:, d:2 * d] = dpib
        back = []
        for hh in range(N_HEADS):
            sl = slice(hh * hd, (hh + 1) * hd)
            back.append(_nn(dprb[:, sl], wgt_ref[0, hh]) + _nn(dpib[:, sl], wgt_ref[1, hh]))
        dxr = big_g * s * ig + jnp.concatenate(back, axis=1)
        acc(SG_CB4, dxr)
        extd4[pl.ds(0, tm), :] = dxr
        dx_rnn = jnp.zeros((tm, d), F32)
        for k in range(k4):
            term = extd4[pl.ds(k4 - 1 - k, tm), :]
            dx_rnn = dx_rnn + cw4_ref[k:k + 1, :] * term
            acc(SG_CW4 + k, x_rnn * term)
        extd4[pl.ds(tm, CONV4_HALO), :] = extd4[pl.ds(0, CONV4_HALO), :]
        emit(0, dx_rnn)

        dybb = dyb.astype(BF16)
        y3_ref[:, 2 * d:3 * d] = dybb
        acc(SG_BCP, dyb)
        dv3 = _nn(dybb, w3_v[1])
        v1 = v1_ref[...].astype(F32)
        xc = v1 - jnp.mean(v1, axis=-1, keepdims=True)
        rstd = lax.rsqrt(jnp.mean(xc * xc, axis=-1, keepdims=True) + EPS)
        xhat = xc * rstd
        lng_v = lng_ref[...]
        v2 = xhat * lng_v + lnb_ref[...]
        s2 = _sigmoid(v2)
        x3_ref[:, 2 * d:3 * d] = (v2 * s2).astype(BF16)
        dv2 = dv3 * (s2 * (1.0 + v2 * (1.0 - s2)))
        acc(SG_LNG, dv2 * xhat)
        acc(SG_LNB, dv2)
        dxh = dv2 * lng_v
        dv1 = rstd * (dxh - jnp.mean(dxh, axis=-1, keepdims=True)
                      - xhat * jnp.mean(dxh * xhat, axis=-1, keepdims=True))
        acc(SG_CB31, dv1)
        extd31[pl.ds(0, tm), :] = dv1
        _shifted_copies(extd31, es31, tm + CONV31_HALO - SUBLANES)
        sgg = _sigmoid(glu_g)
        v0 = glu_v * sgg
        dv0 = jnp.zeros((tm, d), F32)
        for k in range(k31):
            term = _tap(extd31, es31, k31 - 1 - k, tm)
            dv0 = dv0 + cw31_ref[k:k + 1, :] * term
            acc(SG_CW31 + k, v0 * term)
        extd31[pl.ds(tm, CONV31_HALO), :] = extd31[pl.ds(0, CONV31_HALO), :]
        emit(2, dv0 * sgg)
        emit(3, dv0 * glu_v * sgg * (1.0 - sgg))

        dn = dn_parts[0]
        x = h_ref[...]
        rr = lax.rsqrt(jnp.mean(x * x, axis=-1, keepdims=True) + EPS)
        dx, dgp = _rms_bwd(dn, x, rr, g_ref[...])
        dh1_ref[...] = dh_ref[...] + dx
        sg_ref[SG_MIX:SG_MIX + 1, :] += dgp

    rev = lambda i: (nt - 1 - i, 0)
    row = pl.BlockSpec((tm, d), rev)
    wide = pl.BlockSpec((tm, n_in), rev)
    full = lambda a: pl.BlockSpec(a.shape, lambda i, nd=a.ndim: (0,) * nd)
    halo = pl.BlockSpec((halo_rows, d), lambda i: (jnp.maximum((nt - 1 - i) * per - 1, 0), 0))
    smalls = [cw4, wg, jnp.swapaxes(wg, 2, 3), ba, bx, lam, cw31, lng, lnb]
    return _call(
        body, "mixer_bwd", (nt,),
        [row, row, full(g), wide, row, row, halo, row, row, row, _any()]
        + [full(a) for a in smalls] + [_any()],
        [row, wide, pl.BlockSpec((tm, 3 * d), rev), pl.BlockSpec((tm, 3 * d), rev),
         pl.BlockSpec((tm, 2 * d), rev), pl.BlockSpec((sg_rows, d), lambda i: (0, 0))],
        [jax.ShapeDtypeStruct((tp, d), F32), jax.ShapeDtypeStruct((tp, n_in), BF16),
         jax.ShapeDtypeStruct((tp, 3 * d), BF16), jax.ShapeDtypeStruct((tp, 3 * d), BF16),
         jax.ShapeDtypeStruct((tp, 2 * d), BF16), jax.ShapeDtypeStruct((sg_rows, d), F32)],
        [pltpu.VMEM(win_t.shape, BF16),
         pltpu.VMEM((3, d, d), BF16),
         pltpu.VMEM((tm + CONV4_HALO, d), F32),
         pltpu.VMEM((tm + CONV31_HALO, d), F32),
         pltpu.VMEM((SUBLANES, tm + CONV31_HALO, d), F32),
         pltpu.VMEM((SUBLANES, d), F32),
         pltpu.SemaphoreType.DMA((1 + 3 * N_DEV,))],
        [dh2, h, g, proj, xr_s, hs_s, hs_s, v1_s, ya_s, yb_s, win_t, *smalls, w3_all], comm)


def _tn_matmul(name, x, y, x_spec, y_spec, n_blocks, kb, nb, tm, tp, out_shape, out_spec, out_view, comm=None):
    nt = tp // tm

    def body(x_ref, y_ref, o_ref, acc):
        i = pl.program_id(1)

        @pl.when(i == 0)
        def _():
            acc[...] = jnp.zeros_like(acc)

        acc[...] += _tn(x_ref[...], y_ref[...])

        @pl.when(i == nt - 1)
        def _():
            o_ref[...] = acc[...].astype(BF16).reshape(out_view)

    outs, extra = _call(body, name, (n_blocks, nt), [x_spec, y_spec], [out_spec],
                        [jax.ShapeDtypeStruct(out_shape, BF16)], [pltpu.VMEM((kb, nb), F32)], [x, y], comm)
    return outs[0], extra


def kernel(x, meta_tokens, ffn1_norm, ffn1_w_gu, ffn1_w_down, mix_norm, w_in, b_in, rnn_conv_w, rnn_conv_b, rg_w_a, rg_b_a, rg_w_x, rg_b_x, rg_lambda, rnn_w_proj, conv_dw_w, conv_dw_b, conv_ln_g, conv_ln_b, conv_w_proj, conv_b_proj, w_out, ffn2_norm, ffn2_w_gu, ffn2_w_down, final_norm, loss_target, m_meta_tokens, m_ffn1_norm, m_ffn1_w_gu, m_ffn1_w_down, m_mix_norm, m_w_in, m_b_in, m_rnn_conv_w, m_rnn_conv_b, m_rg_w_a, m_rg_b_a, m_rg_w_x, m_rg_b_x, m_rg_lambda, m_rnn_w_proj, m_conv_dw_w, m_conv_dw_b, m_conv_ln_g, m_conv_ln_b, m_conv_w_proj, m_conv_b_proj, m_w_out, m_ffn2_norm, m_ffn2_w_gu, m_ffn2_w_down, m_final_norm, v_meta_tokens, v_ffn1_norm, v_ffn1_w_gu, v_ffn1_w_down, v_mix_norm, v_w_in, v_b_in, v_rnn_conv_w, v_rnn_conv_b, v_rg_w_a, v_rg_b_a, v_rg_w_x, v_rg_b_x, v_rg_lambda, v_rnn_w_proj, v_conv_dw_w, v_conv_dw_b, v_conv_ln_g, v_conv_ln_b, v_conv_w_proj, v_conv_b_proj, v_w_out, v_ffn2_norm, v_ffn2_w_gu, v_ffn2_w_down, v_final_norm):
    w = dict(locals())
    seq, d = x.shape[1], x.shape[2]
    n_meta = meta_tokens.shape[0]
    t_real = n_meta + seq
    tp, tm, tmx_fwd, tmx, tmt = _tiles(t_real)
    fb = ffn1_w_gu.shape[-1]
    wr = ffn1_w_down.shape[1]
    f = N_DEV * wr
    fc = f // FFN_CHUNKS
    nbc = w_in.shape[-1]
    n_in = N_DEV * nbc
    pr = rnn_w_proj.shape[1]
    hd = rg_w_a.shape[-1]
    gr = rg_w_a.shape[2]
    cw = meta_tokens.shape[1]
    k4, k31 = rnn_conv_w.shape[1], conv_dw_w.shape[1]
    assert n_in == 6 * d and 2 * wr == fb and N_HEADS * hd == d and pr * N_DEV == d

    xi, yi, ci = lax.axis_index("x"), lax.axis_index("y"), lax.axis_index("c")
    core = ci.astype(jnp.int32).reshape(1)
    chip = (2 * xi + yi).astype(jnp.int32).reshape(1)
    me_index = (4 * xi + 2 * yi + ci).astype(jnp.int32).reshape(1)

    for nm in ("ffn1_w_gu", "ffn2_w_gu"):
        for pre in ("", "m_", "v_"):
            w[pre + nm] = jnp.swapaxes(w[pre + nm], 1, 2)

    wgut1 = w["ffn1_w_gu"][0].astype(BF16)
    wgut2 = w["ffn2_w_gu"][0].astype(BF16)
    wd1 = ffn1_w_down[0].astype(BF16)
    wd2 = ffn2_w_down[0].astype(BF16)
    win_loc = w_in[0].astype(BF16)
    win_t_loc = jnp.swapaxes(w_in[0], 0, 1).astype(BF16)
    w3_loc = jnp.concatenate([rnn_w_proj[0], conv_w_proj[0], w_out[0]], axis=0).astype(BF16)
    wg_loc = jnp.stack([rg_w_a[0], rg_w_x[0]]).astype(BF16)
    n_small = n_meta + k4 + k31
    small_rows = -(-n_small // SUBLANES) * SUBLANES
    small_loc = jnp.concatenate([meta_tokens, rnn_conv_w[0], conv_dw_w[0],
                                 jnp.zeros((small_rows - n_small, cw), F32)], axis=0)
    wgut1_all, wd1_all, wg_all, small_all = _all_gather([wgut1, wd1, wg_loc, small_loc])
    wg = wg_all.transpose(1, 2, 0, 3, 4).reshape(2, N_HEADS, hd, hd)
    small_full = small_all.transpose(1, 0, 2).reshape(small_rows, d)
    meta_full = small_full[:n_meta]
    cw4 = small_full[n_meta:n_meta + k4]
    cw31 = small_full[n_meta + k4:n_meta + k4 + k31]

    pad = jnp.zeros((tp - t_real, d), F32)
    h0 = jnp.concatenate([meta_full, x[0], pad], axis=0)
    tgt = jnp.concatenate([jnp.zeros((n_meta, d), F32), loss_target[0], pad], axis=0)
    wgu1, wdn1 = wgut1_all.reshape(2 * f, d), wd1_all.reshape(f, d)
    (h1, gu1, n1), (win_all, w3_all) = _ffn_fwd(h0, ffn1_norm, wgu1, wdn1, tm, comm=_Gather([win_loc, w3_loc]))
    (h2, proj, n2, xr_s, hs_s, v1_s, ya_s, yb_s), (wgut2_all, wd2_all) = _mixer_fwd(
        h1, mix_norm, b_in, win_all, cw4, rnn_conv_b, wg, rg_b_a, rg_b_x, rg_lambda, cw31, conv_dw_b, conv_ln_g,
        conv_ln_b, conv_b_proj, w3_all, tmx_fwd, comm=_Gather([wgut2, wd2]))
    wgu2, wdn2 = wgut2_all.reshape(2 * f, d), wd2_all.reshape(f, d)
    (dh3, gu2, n3, loss_part, dgf), (win_t_all,) = _ffn_fwd(
        h2, ffn2_norm, wgu2, wdn2, tm, loss=(tgt, final_norm.reshape(1, d), n_meta, t_real),
        comm=_Gather([win_t_loc]))
    win_t = win_t_all.reshape(n_in, d)

    def d_w_gu(tag, dgu, n_s, comm=None):
        g, extra = _tn_matmul(
            "d_w_gu" + tag, dgu, n_s,
            pl.BlockSpec((None, tmt, fc), lambda b, i: (b // FFN_CHUNKS, i, b % FFN_CHUNKS)),
            pl.BlockSpec((tmt, d), lambda b, i: (i, 0)),
            2 * FFN_CHUNKS, fc, d, tmt, tp, (2 * FFN_CHUNKS, fc, d),
            pl.BlockSpec((None, fc, d), lambda b, i: (b, 0, 0)), (fc, d), comm)
        return g.reshape(N_DEV, fb, d), extra

    def d_w_down(tag, act, df, comm=None):
        g, extra = _tn_matmul(
            "d_w_down" + tag, act, df,
            pl.BlockSpec((tmt, fc), lambda b, i: (i, b)), pl.BlockSpec((tmt, d), lambda b, i: (i, 0)),
            FFN_CHUNKS, fc, d, tmt, tp, (FFN_CHUNKS, fc, d),
            pl.BlockSpec((None, fc, d), lambda b, i: (b, 0, 0)), (fc, d), comm)
        return g.reshape(N_DEV, wr, d), extra

    (dh2, dgu2, act2, df2, dg_ffn2), _ = _ffn_bwd(dh3, h2, gu2, ffn2_norm, wgu2, wdn2, tm)
    g_wgu2, _ = d_w_gu("2", dgu2, n3)
    g_wd2, _ = d_w_down("2", act2, df2)
    (dh1, dproj, x3, y3, yg, sg), (r_wd2, r_wgu2) = _mixer_bwd(
        dh2, h1, mix_norm, proj, xr_s, hs_s, v1_s, ya_s, yb_s, win_t, cw4, wg, rg_b_a, rg_b_x, rg_lambda, cw31,
        conv_ln_g, conv_ln_b, w3_all, tmx, comm=_Scatter([g_wd2, g_wgu2]))
    g_w3, _ = _tn_matmul(
        "d_w_proj3", x3, y3,
        pl.BlockSpec((tmt, d), lambda b, i: (i, b)), pl.BlockSpec((tmt, d), lambda b, i: (i, b)),
        3, d, d, tmt, tp, (N_DEV, 3, pr, d), pl.BlockSpec((N_DEV, None, pr, d), lambda b, i: (0, b, 0, 0)),
        (N_DEV, pr, d))
    g_wg, _ = _tn_matmul(
        "d_w_gates", xr_s, yg,
        pl.BlockSpec((tmt, hd), lambda b, i: (i, b % N_HEADS)), pl.BlockSpec((tmt, hd), lambda b, i: (i, b)),
        2 * N_HEADS, hd, hd, tmt, tp, (N_DEV, 2 * N_HEADS, gr, hd),
        pl.BlockSpec((N_DEV, None, gr, hd), lambda b, i: (0, b, 0, 0)), (N_DEV, gr, hd))
    g_win, (r_w3, r_wg) = _tn_matmul(
        "d_w_in", n2, dproj,
        pl.BlockSpec((tmt, d), lambda b, i: (i, 0)), pl.BlockSpec((tmt, nbc), lambda b, i: (i, b)),
        N_DEV, d, nbc, tmt, tp, (N_DEV, d, nbc), pl.BlockSpec((None, d, nbc), lambda b, i: (b, 0, 0)), (d, nbc),
        comm=_Scatter([g_w3, g_wg]))
    dg_mix = sg[SG_MIX:SG_MIX + 1]
    (dh0, dgu1, act1, df1, dg_ffn1), (r_win,) = _ffn_bwd(dh1, h0, gu1, ffn1_norm, wgu1, wdn1, tm,
                                                         comm=_Scatter([g_win]))
    grad_x = dh0[n_meta:t_real][None]

    rep_rows = [("ffn1_norm", dg_ffn1), ("mix_norm", dg_mix), ("b_in", sg[SG_BIN:SG_BIN + 6]),
                ("rnn_conv_b", sg[SG_CB4:SG_CB4 + 1]), ("rg_b_a", sg[SG_BA:SG_BA + 1]),
                ("rg_b_x", sg[SG_BX:SG_BX + 1]), ("rg_lambda", sg[SG_LAM:SG_LAM + 1]),
                ("conv_dw_b", sg[SG_CB31:SG_CB31 + 1]), ("conv_ln_g", sg[SG_LNG:SG_LNG + 1]),
                ("conv_ln_b", sg[SG_LNB:SG_LNB + 1]), ("conv_b_proj", sg[SG_BCP:SG_BCP + 1]),
                ("ffn2_norm", dg_ffn2), ("final_norm", dgf)]
    col_rows = [("meta_tokens", dh0[:n_meta]), ("rnn_conv_w", sg[SG_CW4:SG_CW4 + k4]),
                ("conv_dw_w", sg[SG_CW31:SG_CW31 + k31])]
    layout, pieces, r0 = [], [], 0
    for nm, part in rep_rows:
        nr = part.shape[0]
        kind = "wide" if nm == "b_in" else "rep"
        as2d = lambda a: a.reshape(1, -1) if a.ndim == 1 else a
        layout.append((kind, r0, nr, as2d(w[nm]), as2d(w["m_" + nm]), as2d(w["v_" + nm])))
        pieces.append(part)
        r0 += nr
    for nm, part in col_rows:
        nr = part.shape[0]
        sq = lambda a: a.reshape(a.shape[-2], a.shape[-1])
        layout.append(("col", r0, nr, sq(w[nm]), sq(w["m_" + nm]), sq(w["v_" + nm])))
        pieces.append(part)
        r0 += nr
    total_rows = -(-(r0 + 1) // SUBLANES) * SUBLANES
    pieces.append(jnp.zeros((total_rows - 1 - r0, d), F32))
    pieces.append(loss_part)
    small_partial = jnp.concatenate(pieces, axis=0)

    g_wd1, (small_partials,) = d_w_down("1", act1, df1, comm=_Bcast(small_partial))
    g_wgu1, (r_wd1,) = d_w_gu("1", dgu1, n1, comm=_Scatter([g_wd1]))

    g_last = g_wgu1.reshape((4, 2) + g_wgu1.shape[1:])
    (from_sibling,) = _pair_exchange([g_last])
    comb_wgu1 = _pair_add(g_last, from_sibling, core)
    (r_wgu1,) = _chip_exchange([comb_wgu1])

    groups = [(g_wd1, r_wd1, me_index, ["ffn1_w_down"]), (comb_wgu1, r_wgu1, chip, ["ffn1_w_gu"]),
              (g_wd2, r_wd2, me_index, ["ffn2_w_down"]), (g_wgu2, r_wgu2, me_index, ["ffn2_w_gu"]),
              (g_win, r_win, me_index, ["w_in"]), (g_w3, r_w3, me_index, ["w_out", "rnn_w_proj", "conv_w_proj"]),
              (g_wg, r_wg, me_index, ["rg_w_a", "rg_w_x"])]
    res = {}
    for own, recv, idx, group in groups:
        outs = _final_adamw(own, recv, idx, [(w[nm], w["m_" + nm], w["v_" + nm]) for nm in group])
        for nm, o in zip(group, outs):
            res[nm] = o
    for nm in ("ffn1_w_gu", "ffn2_w_gu"):
        res[nm] = tuple(jnp.swapaxes(a, 1, 2) for a in res[nm])

    total, small_out = _small_adamw(small_partials, layout, me_index)
    for (nm, _), o in zip(rep_rows + col_rows, small_out):
        res[nm] = tuple(a.reshape(w[nm].shape) for a in o)

    order = ["meta_tokens", "ffn1_norm", "ffn1_w_gu", "ffn1_w_down", "mix_norm", "w_in", "b_in", "rnn_conv_w",
             "rnn_conv_b", "rg_w_a", "rg_b_a", "rg_w_x", "rg_b_x", "rg_lambda", "rnn_w_proj", "conv_dw_w",
             "conv_dw_b", "conv_ln_g", "conv_ln_b", "conv_w_proj", "conv_b_proj", "w_out", "ffn2_norm",
             "ffn2_w_gu", "ffn2_w_down", "final_norm"]
    return (total[total_rows - 1, 0], grad_x, *[res[nm][0] for nm in order], *[res[nm][1] for nm in order],
            *[res[nm][2] for nm in order], *[res[nm][3] for nm in order])
```

```python
import functools
import math

import jax
import jax.numpy as jnp
from jax import lax
from jax.experimental import pallas as pl
from jax.experimental.pallas import tpu as pltpu

F32 = jnp.float32
BF16 = jnp.bfloat16
MESH = pl.DeviceIdType.MESH
N_DEV = 8
N_HEADS = 4
RG_LRU_C = 8.0
EPS = 1e-6
FFN_RES = 0.5
ADAM_LR, ADAM_B1, ADAM_B2, ADAM_EPS, ADAM_WD, ADAM_STEP = 0.001, 0.9, 0.999, 1e-08, 0.01, 10
V7X_VMEM_LIMIT = 56 * 1024 * 1024
CONV4_HALO = 8
CONV31_HALO = 32
SUBLANES = 8
STAGE_ROWS = 512
FFN_CHUNKS = 2
FFN_FWD_CHUNKS = 1
GELU_C = math.sqrt(2.0 / math.pi)
GELU_K = 0.044715


def _any():
    return pl.BlockSpec(memory_space=pl.ANY)


def _params(n_grid):
    return pltpu.CompilerParams(dimension_semantics=("arbitrary",) * n_grid, vmem_limit_bytes=V7X_VMEM_LIMIT)


def _nn(a, b):
    return jnp.dot(a, b, preferred_element_type=F32)


def _nt(a, b):
    return lax.dot_general(a, b, (((1,), (1,)), ((), ())), preferred_element_type=F32)


def _tn(a, b):
    return lax.dot_general(a, b, (((0,), (0,)), ((), ())), preferred_element_type=F32)


def _sigmoid(x):
    return 0.5 * jnp.tanh(0.5 * x) + 0.5


def _rowsum(x):
    return jnp.sum(x, axis=0, keepdims=True)


def _rms_fwd(x, g):
    r = lax.rsqrt(jnp.mean(x * x, axis=-1, keepdims=True) + EPS)
    return x * r * g, r


def _rms_bwd(dn, x, r, g):
    xr = x * r
    gy = dn * g
    dx = r * (gy - xr * jnp.mean(gy * xr, axis=-1, keepdims=True))
    return dx, _rowsum(dn * xr)


def _gelu(y):
    t = jnp.tanh(GELU_C * (y + GELU_K * y * y * y))
    return 0.5 * y * (1.0 + t), t


def _gelu_grad(y, t):
    return 0.5 * (1.0 + t) + 0.5 * y * (1.0 - t * t) * GELU_C * (1.0 + 3.0 * GELU_K * y * y)


def _softplus(x):
    return jnp.maximum(x, 0.0) + jnp.log(1.0 + jnp.exp(-jnp.abs(x)))


def _one_minus_exp(z):
    series = -z * (1.0 + 0.5 * z * (1.0 + z * (1.0 / 3.0) * (1.0 + 0.25 * z)))
    return jnp.where(z > -0.05, series, 1.0 - jnp.exp(z))


def _tiles(t_real):
    if t_real > 2048:
        tm = 384
        tp = -(-t_real // tm) * tm
        return tp, tm, tm // 2, tm // 2, tp // 2
    tm = 128
    tp = -(-t_real // tm) * tm
    return tp, tm, tm // 2, tm // 2, tm


def _load_weights(copies, sems):
    cps = [pltpu.make_async_copy(s, d, sems.at[k]) for k, (s, d) in enumerate(copies)]
    for cp in cps:
        cp.start()
    for cp in cps:
        cp.wait()


def _position():
    x, y, c = lax.axis_index("x"), lax.axis_index("y"), lax.axis_index("c")
    chips = [(1 - x, y), (x, 1 - y), (1 - x, 1 - y)]
    return x, y, c, chips


def _slot(p):
    return 4 * p[0] + 2 * p[1] + p[2]


class _Gather:
    def __init__(self, shards, pass_on_at=None):
        self.shards = list(shards)
        self.n = len(self.shards)
        self.pass_on_at = pass_on_at

    def inputs(self):
        return self.shards

    def out_shape(self):
        return [jax.ShapeDtypeStruct((N_DEV,) + s.shape, s.dtype) for s in self.shards]

    def scratch(self):
        return [pltpu.SemaphoreType.DMA((7 * self.n,)), pltpu.SemaphoreType.DMA((7 * self.n,)),
                pltpu.SemaphoreType.DMA((self.n,))]

    def _plan(self, ins, outs, sems):
        send_sems, recv_sems, local_sems = sems
        x, y, c, chips = _position()
        me, sibling = (x, y, c), (x, y, 1 - c)

        def copy(a, k, block, to, src=None):
            dst = outs[a].at[_slot(block)]
            return pltpu.make_async_remote_copy(
                src_ref=dst if src is None else src, dst_ref=dst,
                send_sem=send_sems.at[7 * a + k], recv_sem=recv_sems.at[7 * a + k],
                device_id=to, device_id_type=MESH)

        mine = [pltpu.make_async_copy(ins[a], outs[a].at[_slot(me)], local_sems.at[a]) for a in range(self.n)]
        first = []
        for a in range(self.n):
            first.append(copy(a, 0, me, sibling, src=ins[a]))
            first += [copy(a, 1 + j, me, (*chip, c), src=ins[a]) for j, chip in enumerate(chips)]
        return copy, mine, first, me, sibling, c, chips

    def start(self, ins, outs, sems):
        _, mine, first, *_ = self._plan(ins, outs, sems)
        for cp in mine + first:
            cp.start()

    def pass_on(self, ins, outs, sems):
        copy, _, _, me, sibling, c, chips = self._plan(ins, outs, sems)
        for j, chip in enumerate(chips):
            for a in range(self.n):
                copy(a, 1 + j, (*chip, c), me).wait_recv()
                copy(a, 4 + j, (*chip, c), sibling).start()

    def finish(self, ins, outs, sems):
        if self.pass_on_at is None:
            self.pass_on(ins, outs, sems)
        copy, mine, first, me, sibling, c, chips = self._plan(ins, outs, sems)
        passed = [copy(a, 4 + j, (*chip, c), sibling) for j, chip in enumerate(chips) for a in range(self.n)]
        for a in range(self.n):
            copy(a, 0, sibling, me).wait_recv()
            for j, chip in enumerate(chips):
                copy(a, 4 + j, (*chip, 1 - c), me).wait_recv()
        for cp in first + passed:
            cp.wait_send()
        for cp in mine:
            cp.wait()


class _Scatter:
    def __init__(self, grads):
        self.grads = list(grads)
        self.n = len(self.grads)

    def inputs(self):
        return self.grads

    def out_shape(self):
        return [jax.ShapeDtypeStruct((N_DEV - 1,) + g.shape[1:], g.dtype) for g in self.grads]

    def scratch(self):
        return [pltpu.SemaphoreType.DMA((7 * self.n,)), pltpu.SemaphoreType.DMA((7 * self.n,))]

    def _plan(self, ins, outs, sems):
        send_sems, recv_sems = sems
        x, y, c, _ = _position()
        cps = []
        for a in range(self.n):
            for k in range(1, N_DEV):
                peer = (x ^ (k >> 2), y ^ ((k >> 1) & 1), c ^ (k & 1))
                cps.append(pltpu.make_async_remote_copy(
                    src_ref=ins[a].at[_slot(peer)], dst_ref=outs[a].at[k - 1],
                    send_sem=send_sems.at[7 * a + k - 1], recv_sem=recv_sems.at[7 * a + k - 1],
                    device_id=peer, device_id_type=MESH))
        return cps

    def start(self, ins, outs, sems):
        for cp in self._plan(ins, outs, sems):
            cp.start()

    def finish(self, ins, outs, sems):
        for cp in self._plan(ins, outs, sems):
            cp.wait()


def _hosted(inner, n_in, n_out, comm, grid):
    if comm is None:
        return inner
    nc_in, nc_out, ns = len(comm.inputs()), len(comm.out_shape()), len(comm.scratch())

    def body(*refs):
        o0 = n_in + nc_in
        s0 = o0 + n_out + nc_out
        main = refs[:n_in] + refs[o0:o0 + n_out] + refs[s0:len(refs) - ns]
        c_in, c_out, c_sems = refs[n_in:o0], refs[o0 + n_out:s0], refs[len(refs) - ns:]
        ids = [pl.program_id(ax) for ax in range(len(grid))]
        first = functools.reduce(jnp.logical_and, [i == 0 for i in ids])
        last = functools.reduce(jnp.logical_and, [i == g - 1 for i, g in zip(ids, grid)])

        @pl.when(first)
        def _():
            comm.start(c_in, c_out, c_sems)

        inner(*main)

        if getattr(comm, "pass_on_at", None) is not None:
            assert len(grid) == 1
            @pl.when(ids[0] == min(grid[0] - 1, int(comm.pass_on_at * grid[0])))
            def _():
                comm.pass_on(c_in, c_out, c_sems)

        @pl.when(last)
        def _():
            comm.finish(c_in, c_out, c_sems)

    return body


def _call(inner, name, grid, in_specs, out_specs, out_shape, scratch, args, comm=None):
    n_in, n_out = len(args), len(out_shape)
    body = _hosted(inner, n_in, n_out, comm, grid)
    if comm is not None:
        in_specs = list(in_specs) + [_any()] * len(comm.inputs())
        args = list(args) + comm.inputs()
        out_specs = list(out_specs) + [_any()] * len(comm.out_shape())
        out_shape = list(out_shape) + comm.out_shape()
        scratch = list(scratch) + comm.scratch()
    outs = pl.pallas_call(
        body, name=name, grid=grid, in_specs=list(in_specs), out_specs=list(out_specs), out_shape=list(out_shape),
        scratch_shapes=list(scratch), compiler_params=_params(len(grid)))(*args)
    return list(outs[:n_out]), list(outs[n_out:])


class _Bcast:
    def __init__(self, block):
        self.block = block

    def inputs(self):
        return [self.block]

    def out_shape(self):
        return [jax.ShapeDtypeStruct((N_DEV,) + self.block.shape, self.block.dtype)]

    def scratch(self):
        return [pltpu.SemaphoreType.DMA((N_DEV - 1,)), pltpu.SemaphoreType.DMA((N_DEV - 1,)),
                pltpu.SemaphoreType.DMA((1,))]

    def _plan(self, ins, outs, sems):
        send_sems, recv_sems, local_sem = sems
        x, y, c, _ = _position()
        mine = outs[0].at[_slot((x, y, c))]
        cps = []
        for k in range(1, N_DEV):
            peer = (x ^ (k >> 2), y ^ ((k >> 1) & 1), c ^ (k & 1))
            cps.append(pltpu.make_async_remote_copy(
                src_ref=ins[0], dst_ref=mine, send_sem=send_sems.at[k - 1], recv_sem=recv_sems.at[k - 1],
                device_id=peer, device_id_type=MESH))
        return pltpu.make_async_copy(ins[0], mine, local_sem.at[0]), cps

    def start(self, ins, outs, sems):
        own, cps = self._plan(ins, outs, sems)
        own.start()
        for cp in cps:
            cp.start()

    def finish(self, ins, outs, sems):
        own, cps = self._plan(ins, outs, sems)
        for cp in cps:
            cp.wait()
        own.wait()


def _first_gather(shards, small_idx, x2, t2, n_meta, tp):
    comm = _Gather(shards)
    n = comm.n
    seq, d = x2.shape
    t_real = n_meta + seq
    n_pad = tp - t_real
    cw = d // N_DEV
    rows = STAGE_ROWS if seq % STAGE_ROWS == 0 else seq
    n_chunks = seq // rows

    def body(*refs):
        ins, (x_ref, t_ref) = refs[:n], refs[n:n + 2]
        outs, (h0_ref, tg_ref) = refs[n + 2:2 * n + 2], refs[2 * n + 2:2 * n + 4]
        sems = refs[2 * n + 4:2 * n + 7]
        buf, zeros, in_sems, out_sems, misc_sems = refs[2 * n + 7:]
        comm.start(ins, outs, sems)
        zeros[...] = jnp.zeros_like(zeros)
        fills = [pltpu.make_async_copy(zeros.at[pl.ds(0, n_pad)], h0_ref.at[pl.ds(t_real, n_pad)], misc_sems.at[0]),
                 pltpu.make_async_copy(zeros.at[pl.ds(0, n_pad)], tg_ref.at[pl.ds(t_real, n_pad)], misc_sems.at[1]),
                 pltpu.make_async_copy(zeros.at[pl.ds(0, n_meta)], tg_ref.at[pl.ds(0, n_meta)], misc_sems.at[2])]
        for cp in fills:
            cp.start()
        jobs = [(src, dst, c) for src, dst in ((x_ref, h0_ref), (t_ref, tg_ref)) for c in range(n_chunks)]

        def load(k):
            src, _, c = jobs[k]
            return pltpu.make_async_copy(src.at[pl.ds(c * rows, rows)], buf.at[k % 2], in_sems.at[k % 2])

        def store(k):
            _, dst, c = jobs[k]
            return pltpu.make_async_copy(buf.at[k % 2], dst.at[pl.ds(n_meta + c * rows, rows)], out_sems.at[k % 2])

        load(0).start()
        for k in range(len(jobs)):
            load(k).wait()
            if k + 1 < len(jobs):
                if k >= 1:
                    store(k - 1).wait()
                load(k + 1).start()
            store(k).start()
        for k in range(max(0, len(jobs) - 2), len(jobs)):
            store(k).wait()
        comm.finish(ins, outs, sems)
        meta = [pltpu.make_async_copy(outs[small_idx].at[k, pl.ds(0, n_meta)],
                                      h0_ref.at[pl.ds(0, n_meta), pl.ds(k * cw, cw)], misc_sems.at[3 + k])
                for k in range(N_DEV)]
        for cp in meta:
            cp.start()
        for cp in fills + meta:
            cp.wait()

    staged = [jax.ShapeDtypeStruct((tp, d), F32)] * 2
    outs = pl.pallas_call(
        body, name="weights_all_gather", out_shape=comm.out_shape() + staged,
        in_specs=[_any()] * (n + 2), out_specs=[_any()] * (n + 2),
        scratch_shapes=comm.scratch() + [
            pltpu.VMEM((2, rows, d), F32), pltpu.VMEM((max(n_pad, n_meta), d), F32),
            pltpu.SemaphoreType.DMA((2,)), pltpu.SemaphoreType.DMA((2,)), pltpu.SemaphoreType.DMA((3 + N_DEV,))],
        compiler_params=pltpu.CompilerParams(vmem_limit_bytes=V7X_VMEM_LIMIT),
    )(*shards, x2, t2)
    return outs[:n], outs[n], outs[n + 1]


def _pair_exchange(grads):
    n = len(grads)

    def body(*refs):
        ins, outs = refs[:n], refs[n:2 * n]
        send_sems, recv_sems = refs[2 * n:]
        x, y, c, _ = _position()
        cps = [pltpu.make_async_remote_copy(
            src_ref=ins[a].at[:, 1 - c], dst_ref=outs[a],
            send_sem=send_sems.at[a], recv_sem=recv_sems.at[a],
            device_id=(x, y, 1 - c), device_id_type=MESH) for a in range(n)]
        for cp in cps:
            cp.start()
        for cp in cps:
            cp.wait()

    return pl.pallas_call(
        body, name="grads_pair_exchange",
        out_shape=[jax.ShapeDtypeStruct((4,) + g.shape[2:], g.dtype) for g in grads],
        in_specs=[_any()] * n, out_specs=[_any()] * n,
        scratch_shapes=[pltpu.SemaphoreType.DMA((n,)), pltpu.SemaphoreType.DMA((n,))],
    )(*grads)


def _chip_exchange(combs):
    n = len(combs)

    def body(*refs):
        ins, outs = refs[:n], refs[n:2 * n]
        send_sems, recv_sems = refs[2 * n:]
        x, y, c, chips = _position()
        cps = []
        for a in range(n):
            for j, (cx, cy) in enumerate(chips):
                cps.append(pltpu.make_async_remote_copy(
                    src_ref=ins[a].at[2 * cx + cy], dst_ref=outs[a].at[j],
                    send_sem=send_sems.at[3 * a + j], recv_sem=recv_sems.at[3 * a + j],
                    device_id=(cx, cy, c), device_id_type=MESH))
        for cp in cps:
            cp.start()
        for cp in cps:
            cp.wait()

    return pl.pallas_call(
        body, name="grads_chip_exchange",
        out_shape=[jax.ShapeDtypeStruct((3,) + g.shape[1:], g.dtype) for g in combs],
        in_specs=[_any()] * n, out_specs=[_any()] * n,
        scratch_shapes=[pltpu.SemaphoreType.DMA((3 * n,)), pltpu.SemaphoreType.DMA((3 * n,))],
    )(*combs)


def _pair_add(grad, recv, core):
    blk = grad.shape[2:]
    zeros = (0,) * len(blk)

    def body(core_ref, g_ref, r_ref, o_ref):
        del core_ref
        o_ref[...] = (g_ref[...].astype(F32) + r_ref[...].astype(F32)).astype(BF16)

    return pl.pallas_call(
        body, name="grads_pair_add",
        out_shape=jax.ShapeDtypeStruct((4,) + blk, BF16),
        grid_spec=pltpu.PrefetchScalarGridSpec(
            num_scalar_prefetch=1, grid=(4,),
            in_specs=[pl.BlockSpec((None, None) + blk, lambda i, cr: (i, cr[0]) + zeros),
                      pl.BlockSpec((None,) + blk, lambda i, cr: (i,) + zeros)],
            out_specs=pl.BlockSpec((None,) + blk, lambda i, cr: (i,) + zeros)),
        compiler_params=_params(1),
    )(core, grad, recv)


def _adamw(w, g, m, v):
    m2 = ADAM_B1 * m + (1.0 - ADAM_B1) * g
    v2 = ADAM_B2 * v + (1.0 - ADAM_B2) * (g * g)
    m_hat = m2 / (1.0 - ADAM_B1 ** ADAM_STEP)
    v_hat = v2 / (1.0 - ADAM_B2 ** ADAM_STEP)
    delta = -ADAM_LR * (m_hat / (jnp.sqrt(v_hat) + ADAM_EPS) + ADAM_WD * w)
    return delta, m2, v2


def _final_adamw(own, recv, idx, parts):
    blk = own.shape[1:]
    n_recv = recv.shape[0]
    n_parts = len(parts)
    per = blk[0] // n_parts if n_parts > 1 else None
    rows = blk[-2]
    n_chunks = 1 if n_parts > 1 else (4 if rows % 64 == 0 and rows >= 512 else (2 if rows % 32 == 0 else 1))
    cblk = blk[:-2] + (rows // n_chunks, blk[-1])
    lead = (0,) * (len(blk) - 2)

    def body(idx_ref, c_ref, r_ref, *refs):
        del idx_ref
        ins, outs = refs[:3 * n_parts], refs[3 * n_parts:]
        g = c_ref[...].astype(F32)
        for k in range(n_recv):
            g = g + r_ref[k].astype(F32)
        for p in range(n_parts):
            w_ref, m_ref, v_ref = ins[3 * p:3 * p + 3]
            if n_parts == 1:
                gp = g
            elif per == 1:
                gp = g[p]
            else:
                gp = g[p * per:(p + 1) * per]
            delta, m2, v2 = _adamw(w_ref[0], gp, m_ref[0], v_ref[0])
            o = outs[4 * p:4 * p + 4]
            o[0][0] = gp
            o[1][0] = delta
            o[2][0] = m2
            o[3][0] = v2

    flat = [a for wmv in parts for a in wmv]

    def part_spec(a):
        shape = a.shape[:-2] + (a.shape[-2] // n_chunks, a.shape[-1])
        return pl.BlockSpec(shape, lambda i, cr, nd=a.ndim: (0,) * (nd - 2) + (i, 0))

    outs = pl.pallas_call(
        body, name="grads_sum_adamw",
        out_shape=[jax.ShapeDtypeStruct(wmv[0].shape, F32) for wmv in parts for _ in range(4)],
        grid_spec=pltpu.PrefetchScalarGridSpec(
            num_scalar_prefetch=1, grid=(n_chunks,),
            in_specs=[pl.BlockSpec((None,) + cblk, lambda i, cr: (cr[0],) + lead + (i, 0)),
                      pl.BlockSpec((n_recv,) + cblk, lambda i, cr: (0,) + lead + (i, 0))]
                     + [part_spec(a) for a in flat],
            out_specs=[part_spec(wmv[0]) for wmv in parts for _ in range(4)]),
        compiler_params=_params(1),
    )(idx, own, recv, *flat)
    return [tuple(outs[4 * p:4 * p + 4]) for p in range(n_parts)]


def _small_adamw(partials, layout, me_index):
    _, rows, d = partials.shape
    n = len(layout)
    cw = d // N_DEV

    def body(me_ref, p_ref, *refs):
        ins, t_ref, outs = refs[:3 * n], refs[3 * n], refs[3 * n + 1:]
        me = me_ref[0]
        total = p_ref[0]
        for j in range(1, N_DEV):
            total = total + p_ref[j]
        t_ref[...] = total
        for e, (kind, r0, nr, _, _, _) in enumerate(layout):
            w_ref, m_ref, v_ref = ins[3 * e:3 * e + 3]
            o = outs[4 * e:4 * e + 4]
            if kind == "rep":
                g = t_ref[r0:r0 + nr, :]
                delta, m2, v2 = _adamw(w_ref[...], g, m_ref[...], v_ref[...])
                for ref, val in zip(o, (g, delta, m2, v2)):
                    ref[...] = val
            elif kind == "wide":
                for q in range(nr):
                    sl = slice(q * d, (q + 1) * d)
                    g = t_ref[r0 + q:r0 + q + 1, :]
                    delta, m2, v2 = _adamw(w_ref[:, sl], g, m_ref[:, sl], v_ref[:, sl])
                    for ref, val in zip(o, (g, delta, m2, v2)):
                        ref[:, sl] = val
            else:
                for j in range(N_DEV):
                    @pl.when(me == j)
                    def _(j=j, o=o, w_ref=w_ref, m_ref=m_ref, v_ref=v_ref, r0=r0, nr=nr):
                        g = t_ref[r0:r0 + nr, j * cw:(j + 1) * cw]
                        delta, m2, v2 = _adamw(w_ref[...], g, m_ref[...], v_ref[...])
                        for ref, val in zip(o, (g, delta, m2, v2)):
                            ref[...] = val

    flat = [a for ent in layout for a in ent[3:]]
    vm = pl.BlockSpec(memory_space=pltpu.VMEM)
    outs = pl.pallas_call(
        body, name="small_adamw",
        out_shape=[jax.ShapeDtypeStruct((rows, d), F32)]
                  + [jax.ShapeDtypeStruct(ent[3].shape, F32) for ent in layout for _ in range(4)],
        in_specs=[pl.BlockSpec(memory_space=pltpu.SMEM), vm] + [vm] * len(flat),
        out_specs=[vm] * (1 + 4 * n),
        compiler_params=pltpu.CompilerParams(vmem_limit_bytes=V7X_VMEM_LIMIT),
    )(me_index, partials, *flat)
    return outs[0], [tuple(outs[1 + 4 * e:5 + 4 * e]) for e in range(n)]


def _ffn_fwd(h, g, wgu, wd, tm, loss=None, comm=None):
    tp, d = h.shape
    f = wd.shape[0]
    fc = f // FFN_FWD_CHUNKS
    nt = tp // tm
    with_loss = loss is not None
    if with_loss:
        tgt, gf, n_meta, t_real = loss

    def body(*refs):
        if with_loss:
            (h_ref, g_ref, wgu_hbm, wd_hbm, tgt_ref, gf_ref, out_ref, gu_ref, n_ref, loss_ref, dgf_ref,
             wgu_v, wd_v, sems) = refs
        else:
            h_ref, g_ref, wgu_hbm, wd_hbm, out_ref, gu_ref, n_ref, wgu_v, wd_v, sems = refs
        i = pl.program_id(0)

        @pl.when(i == 0)
        def _():
            _load_weights([(wgu_hbm, wgu_v), (wd_hbm, wd_v)], sems)
            if with_loss:
                loss_ref[...] = jnp.zeros_like(loss_ref)
                dgf_ref[...] = jnp.zeros_like(dgf_ref)

        x = h_ref[...]
        n, _ = _rms_fwd(x, g_ref[...])
        nb = n.astype(BF16)
        n_ref[...] = nb
        acc = jnp.zeros((tm, d), F32)
        for j in range(FFN_FWD_CHUNKS):
            cols = slice(j * fc, (j + 1) * fc)
            gate = _nt(nb, wgu_v[pl.ds(j * fc, fc), :])
            up = _nt(nb, wgu_v[pl.ds(f + j * fc, fc), :])
            gu_ref[0, :, cols] = gate.astype(BF16)
            gu_ref[1, :, cols] = up.astype(BF16)
            act = (gate * _sigmoid(gate) * up).astype(BF16)
            acc = acc + _nn(act, wd_v[pl.ds(j * fc, fc), :])
        hn = x + FFN_RES * acc
        if not with_loss:
            out_ref[...] = hn
        else:
            gfv = gf_ref[...]
            r = lax.rsqrt(jnp.mean(hn * hn, axis=-1, keepdims=True) + EPS)
            xr = hn * r
            rows = i * tm + lax.broadcasted_iota(jnp.int32, (tm, 1), 0)
            mask = jnp.logical_and(rows >= n_meta, rows < t_real)
            diff = jnp.where(mask, xr * gfv - tgt_ref[...], 0.0)
            loss_ref[...] += jnp.zeros_like(loss_ref) + 0.5 * jnp.sum(diff * diff) / d
            dy = diff / d
            gy = dy * gfv
            out_ref[...] = r * (gy - xr * jnp.mean(gy * xr, axis=-1, keepdims=True))
            dgf_ref[...] += _rowsum(dy * xr)

    row = pl.BlockSpec((tm, d), lambda i: (i, 0))
    vec = pl.BlockSpec((1, d), lambda i: (0, 0))
    in_specs = [row, vec, _any(), _any()]
    out_shape = [jax.ShapeDtypeStruct((tp, d), F32), jax.ShapeDtypeStruct((2, tp, f), BF16),
                 jax.ShapeDtypeStruct((tp, d), BF16)]
    out_specs = [row, pl.BlockSpec((2, tm, f), lambda i: (0, i, 0)), row]
    args = [h, g, wgu, wd]
    if with_loss:
        in_specs += [row, vec]
        out_shape += [jax.ShapeDtypeStruct((1, d), F32), jax.ShapeDtypeStruct((1, d), F32)]
        out_specs += [vec, vec]
        args += [tgt, gf]
    return _call(body, "ffn_fwd_loss" if with_loss else "ffn_fwd", (nt,), in_specs, out_specs, out_shape,
                 [pltpu.VMEM((2 * f, d), BF16), pltpu.VMEM((f, d), BF16), pltpu.SemaphoreType.DMA((2,))],
                 args, comm)


def _ffn_bwd(dh, h, gu, g, wgu, wd, tm, comm=None):
    tp, d = h.shape
    f = wd.shape[0]
    fc = f // FFN_CHUNKS
    nt = tp // tm

    def body(dh_ref, h_ref, gu_ref, g_ref, wgu_hbm, wd_hbm,
             dhin_ref, dgu_ref, act_ref, df_ref, dg_ref, wgu_v, wd_v, dn_v, sems):
        i, j = pl.program_id(0), pl.program_id(1)

        @pl.when(jnp.logical_and(i == 0, j == 0))
        def _():
            _load_weights([(wgu_hbm, wgu_v), (wd_hbm, wd_v)], sems)
            dg_ref[...] = jnp.zeros_like(dg_ref)

        dfb = (FFN_RES * dh_ref[...]).astype(BF16)

        @pl.when(j == 0)
        def _():
            df_ref[...] = dfb
            dn_v[...] = jnp.zeros_like(dn_v)

        lo = pl.multiple_of(j * fc, 16)
        dact = _nt(dfb, wd_v[pl.ds(lo, fc), :])
        gate = gu_ref[0].astype(F32)
        up = gu_ref[1].astype(F32)
        sg = _sigmoid(gate)
        silu = gate * sg
        act_ref[...] = (silu * up).astype(BF16)
        dgate = (dact * up * (sg * (1.0 + gate * (1.0 - sg)))).astype(BF16)
        dup = (dact * silu).astype(BF16)
        dgu_ref[0] = dgate
        dgu_ref[1] = dup
        dn_v[...] += _nn(dgate, wgu_v[pl.ds(lo, fc), :]) + _nn(dup, wgu_v[pl.ds(pl.multiple_of(f + j * fc, 16), fc), :])

        @pl.when(j == FFN_CHUNKS - 1)
        def _():
            x = h_ref[...]
            r = lax.rsqrt(jnp.mean(x * x, axis=-1, keepdims=True) + EPS)
            dx, dgp = _rms_bwd(dn_v[...], x, r, g_ref[...])
            dhin_ref[...] = dh_ref[...] + dx
            dg_ref[...] += dgp

    row = pl.BlockSpec((tm, d), lambda i, j: (i, 0))
    vec = pl.BlockSpec((1, d), lambda i, j: (0, 0))
    hid2 = pl.BlockSpec((2, tm, fc), lambda i, j: (0, i, j))
    return _call(
        body, "ffn_bwd", (nt, FFN_CHUNKS),
        [row, row, hid2, vec, _any(), _any()],
        [row, hid2, pl.BlockSpec((tm, fc), lambda i, j: (i, j)), row, vec],
        [jax.ShapeDtypeStruct((tp, d), F32), jax.ShapeDtypeStruct((2, tp, f), BF16),
         jax.ShapeDtypeStruct((tp, f), BF16), jax.ShapeDtypeStruct((tp, d), BF16),
         jax.ShapeDtypeStruct((1, d), F32)],
        [pltpu.VMEM((2 * f, d), BF16), pltpu.VMEM((f, d), BF16), pltpu.VMEM((tm, d), F32),
         pltpu.SemaphoreType.DMA((2,))],
        [dh, h, gu, g, wgu, wd], comm)


def _piece_segments(q, d, nb_cols):
    segs = []
    for j in range(N_DEV):
        lo, hi = max(q * d, j * nb_cols), min((q + 1) * d, (j + 1) * nb_cols)
        if lo < hi:
            segs.append((j, lo - q * d, hi - q * d, lo - j * nb_cols, hi - j * nb_cols))
    return segs


def _w3_copies(w3_hbm, rows, w3_v):
    return [(w3_hbm.at[k, pl.ds(q * rows, rows)], w3_v.at[q, pl.ds(k * rows, rows)])
            for q in range(3) for k in range(N_DEV)]


def _gates(xrb, wg_ref, ba, bx, lam, hd):
    pre_r, pre_i = [], []
    for hh in range(N_HEADS):
        xh = xrb[:, hh * hd:(hh + 1) * hd]
        pre_r.append(_nn(xh, wg_ref[0, hh]))
        pre_i.append(_nn(xh, wg_ref[1, hh]))
    r = _sigmoid(jnp.concatenate(pre_r, axis=1) + ba)
    ig = _sigmoid(jnp.concatenate(pre_i, axis=1) + bx)
    sp = _softplus(-lam)
    log_a = -RG_LRU_C * r * sp
    a = jnp.exp(log_a)
    s = jnp.sqrt(_one_minus_exp(2.0 * log_a))
    return r, ig, sp, a, s


def _scan_fwd(a, u, h_prev):
    tm = a.shape[0]
    rows = lax.broadcasted_iota(jnp.int32, a.shape, 0)
    d = 1
    while d < tm:
        if d < SUBLANES:
            keep = rows >= d
            u = jnp.where(keep, a * pltpu.roll(u, d, 0) + u, u)
            a = jnp.where(keep, a * pltpu.roll(a, d, 0), a)
        else:
            u = jnp.concatenate([u[:d], a[d:] * u[:tm - d] + u[d:]], axis=0)
            a = jnp.concatenate([a[:d], a[d:] * a[:tm - d]], axis=0)
        d *= 2
    return u + a * h_prev


def _scan_bwd(b, v, g_next):
    tm = b.shape[0]
    rows = lax.broadcasted_iota(jnp.int32, b.shape, 0)
    d = 1
    while d < tm:
        if d < SUBLANES:
            keep = rows < tm - d
            v = jnp.where(keep, v + b * pltpu.roll(v, tm - d, 0), v)
            b = jnp.where(keep, b * pltpu.roll(b, tm - d, 0), b)
        else:
            v = jnp.concatenate([v[:tm - d] + b[:tm - d] * v[d:], v[tm - d:]], axis=0)
            b = jnp.concatenate([b[:tm - d] * b[d:], b[tm - d:]], axis=0)
        d *= 2
    return v + b * g_next


def _shifted_copies(ext_ref, es_ref, n_rows):
    for s in range(1, SUBLANES):
        es_ref[s, pl.ds(0, n_rows), :] = ext_ref[pl.ds(s, n_rows), :]


def _tap(ext_ref, es_ref, off, tm):
    q, s = divmod(off, SUBLANES)
    if s == 0:
        return ext_ref[pl.ds(SUBLANES * q, tm), :]
    return es_ref[s, pl.ds(SUBLANES * q, tm), :]


def _mixer_fwd(h, g, b_in, win_all, cw4, cb4, wg, ba, bx, lam, cw31, cb31, lng, lnb, bcp, w3_all, tm, comm=None):
    tp, d = h.shape
    nb_cols = win_all.shape[-1]
    n_in = N_DEV * nb_cols
    hd = wg.shape[-1]
    k4, k31 = cw4.shape[0], cw31.shape[0]
    w3_rows = d // N_DEV

    def body(h_ref, g_ref, b_ref, win_hbm, cw4_ref, cb4_ref, wg_ref, ba_ref, bx_ref, lam_ref, cw31_ref, cb31_ref,
             lng_ref, lnb_ref, bcp_ref, w3_hbm,
             h2_ref, p_ref, n_ref, xr_ref, hs_ref, v1_ref, ya_ref, yb_ref,
             win_v, w3_v, ext4, ext31, es31, hcar, sems):
        @pl.when(pl.program_id(0) == 0)
        def _():
            _load_weights([(win_hbm, win_v)] + _w3_copies(w3_hbm, w3_rows, w3_v), sems)
            ext4[pl.ds(0, CONV4_HALO), :] = jnp.zeros((CONV4_HALO, d), F32)
            ext31[pl.ds(0, CONV31_HALO), :] = jnp.zeros((CONV31_HALO, d), F32)
            hcar[...] = jnp.zeros_like(hcar)

        n, _ = _rms_fwd(h_ref[...], g_ref[...])
        nb = n.astype(BF16)
        n_ref[...] = nb

        def piece(q):
            parts = [_nn(nb, win_v[j, :, bl:bh]) for j, _, _, bl, bh in _piece_segments(q, d, nb_cols)]
            pq = (jnp.concatenate(parts, axis=1) + b_ref[:, q * d:(q + 1) * d]).astype(BF16)
            p_ref[:, q * d:(q + 1) * d] = pq
            return pq.astype(F32)

        x_rnn, y_rnn, glu_v, glu_g, gate_a, gate_b = [piece(q) for q in range(6)]

        ext4[pl.ds(CONV4_HALO, tm), :] = x_rnn
        xr = cb4_ref[...] + jnp.zeros((tm, d), F32)
        for k in range(k4):
            xr = xr + cw4_ref[k:k + 1, :] * ext4[pl.ds(CONV4_HALO - (k4 - 1) + k, tm), :]
        ext4[pl.ds(0, CONV4_HALO), :] = ext4[pl.ds(tm, CONV4_HALO), :]
        xrb = xr.astype(BF16)
        xr_ref[...] = xrb
        xr = xrb.astype(F32)
        _, ig, _, a, s = _gates(xrb, wg_ref, ba_ref[...], bx_ref[...], lam_ref[...], hd)
        hseq = _scan_fwd(a, s * (ig * xr), hcar[0:1, :])
        hcar[0:1, :] = hseq[tm - 1:tm, :]
        hs_ref[...] = hseq.astype(BF16)
        gl, _ = _gelu(y_rnn)
        ya = _nn((hseq * gl).astype(BF16), w3_v[0])
        ya_ref[...] = ya.astype(BF16)

        ext31[pl.ds(CONV31_HALO, tm), :] = glu_v * _sigmoid(glu_g)
        _shifted_copies(ext31, es31, tm + CONV31_HALO - SUBLANES)
        v1 = cb31_ref[...] + jnp.zeros((tm, d), F32)
        for k in range(k31):
            v1 = v1 + cw31_ref[k:k + 1, :] * _tap(ext31, es31, CONV31_HALO - (k31 - 1) + k, tm)
        ext31[pl.ds(0, CONV31_HALO), :] = ext31[pl.ds(tm, CONV31_HALO), :]
        v1b = v1.astype(BF16)
        v1_ref[...] = v1b
        v1 = v1b.astype(F32)
        xc = v1 - jnp.mean(v1, axis=-1, keepdims=True)
        rstd = lax.rsqrt(jnp.mean(xc * xc, axis=-1, keepdims=True) + EPS)
        v2 = xc * rstd * lng_ref[...] + lnb_ref[...]
        yb = _nn((v2 * _sigmoid(v2)).astype(BF16), w3_v[1]) + bcp_ref[...]
        yb_ref[...] = yb.astype(BF16)

        merged = _sigmoid(gate_a) * ya + _sigmoid(gate_b) * yb
        h2_ref[...] = h_ref[...] + _nn(merged.astype(BF16), w3_v[2])

    row = pl.BlockSpec((tm, d), lambda i: (i, 0))
    wide = pl.BlockSpec((tm, n_in), lambda i: (i, 0))
    full = lambda a: pl.BlockSpec(a.shape, lambda i, nd=a.ndim: (0,) * nd)
    smalls = [cw4, cb4, wg, ba, bx, lam, cw31, cb31, lng, lnb, bcp]
    return _call(
        body, "mixer_fwd", (tp // tm,),
        [row, full(g), full(b_in), _any()] + [full(a) for a in smalls] + [_any()],
        [row, wide] + [row] * 6,
        [jax.ShapeDtypeStruct((tp, d), F32), jax.ShapeDtypeStruct((tp, n_in), BF16)]
        + [jax.ShapeDtypeStruct((tp, d), BF16)] * 6,
        [pltpu.VMEM(win_all.shape, BF16),
         pltpu.VMEM((3, d, d), BF16),
         pltpu.VMEM((tm + CONV4_HALO, d), F32),
         pltpu.VMEM((tm + CONV31_HALO, d), F32),
         pltpu.VMEM((SUBLANES, tm + CONV31_HALO, d), F32),
         pltpu.VMEM((SUBLANES, d), F32),
         pltpu.SemaphoreType.DMA((1 + 3 * N_DEV,))],
        [h, g, b_in, win_all, *smalls, w3_all], comm)


SG_BIN, SG_CW4, SG_CB4, SG_BA, SG_BX, SG_LAM, SG_CB31, SG_LNG, SG_LNB, SG_BCP, SG_MIX, SG_CW31 = 0, 6, 10, 11, 12, 13, 14, 15, 16, 17, 18, 19


def _mixer_bwd(dh2, h, g, proj, xr_s, hs_s, v1_s, ya_s, yb_s, win_t, cw4, wg, ba, bx, lam, cw31, lng, lnb, w3_all, tm,
               comm=None):
    tp, d = dh2.shape
    n_in = proj.shape[1]
    hd = wg.shape[-1]
    k4, k31 = cw4.shape[0], cw31.shape[0]
    nt = tp // tm
    w3_rows = d // N_DEV
    sg_rows = -(-(SG_CW31 + k31) // SUBLANES) * SUBLANES
    halo_rows = 16
    per = tm // halo_rows

    def body(dh_ref, h_ref, g_ref, p_ref, xr_ref, hs_ref, hh_ref, v1_ref, ya_ref, yb_ref, win_hbm,
             cw4_ref, wg_ref, wgt_ref, ba_ref, bx_ref, lam_ref, cw31_ref, lng_ref, lnb_ref, w3_hbm,
             dh1_ref, dp_ref, x3_ref, y3_ref, yg_ref, sg_ref,
             win_v, w3_v, extd4, extd31, es31, gcar, sems):
        i = pl.program_id(0)
        tile = nt - 1 - i

        @pl.when(i == 0)
        def _():
            _load_weights([(win_hbm, win_v)] + _w3_copies(w3_hbm, w3_rows, w3_v), sems)
            for q in range(3):
                w3_v[q] = w3_v[q].T
            extd4[pl.ds(tm, CONV4_HALO), :] = jnp.zeros((CONV4_HALO, d), F32)
            extd31[pl.ds(tm, CONV31_HALO), :] = jnp.zeros((CONV31_HALO, d), F32)
            gcar[...] = jnp.zeros_like(gcar)
            sg_ref[...] = jnp.zeros_like(sg_ref)

        def acc(row, val):
            sg_ref[row:row + 1, :] += _rowsum(val)

        rows = lax.broadcasted_iota(jnp.int32, (tm, d), 0)
        x_rnn = p_ref[:, 0:d].astype(F32)
        y_rnn = p_ref[:, d:2 * d].astype(F32)
        glu_v = p_ref[:, 2 * d:3 * d].astype(F32)
        glu_g = p_ref[:, 3 * d:4 * d].astype(F32)
        sga = _sigmoid(p_ref[:, 4 * d:5 * d].astype(F32))
        sgb = _sigmoid(p_ref[:, 5 * d:6 * d].astype(F32))
        ya = ya_ref[...].astype(F32)
        yb = yb_ref[...].astype(F32)

        dmob = dh_ref[...].astype(BF16)
        dmerged = _nn(dmob, w3_v[2])
        x3_ref[:, 0:d] = (sga * ya + sgb * yb).astype(BF16)
        y3_ref[:, 0:d] = dmob
        dya = sga * dmerged
        dyb = sgb * dmerged
        dn_parts = []

        def emit(q, val):
            vb = val.astype(BF16)
            dp_ref[:, q * d:(q + 1) * d] = vb
            acc(SG_BIN + q, val)
            term = _nn(vb, win_v[pl.ds(q * d, d), :])
            dn_parts[:] = [term if not dn_parts else dn_parts[0] + term]

        emit(4, dmerged * ya * sga * (1.0 - sga))
        emit(5, dmerged * yb * sgb * (1.0 - sgb))

        dyab = dya.astype(BF16)
        y3_ref[:, d:2 * d] = dyab
        dza = _nn(dyab, w3_v[0])
        hsv = hs_ref[...].astype(F32)
        gl, th = _gelu(y_rnn)
        x3_ref[:, d:2 * d] = (hsv * gl).astype(BF16)
        emit(1, dza * hsv * _gelu_grad(y_rnn, th))
        dhs = dza * gl
        xrb = xr_ref[...]
        xr = xrb.astype(F32)
        lam_v = lam_ref[...]
        r, ig, sp, a, s = _gates(xrb, wg_ref, ba_ref[...], bx_ref[...], lam_v, hd)
        b = jnp.where(rows == tm - 1, gcar[1:2, :], pltpu.roll(a, tm - 1, 0))
        big_g = _scan_bwd(b, dhs, gcar[0:1, :])
        gcar[0:1, :] = big_g[0:1, :]
        gcar[1:2, :] = a[0:1, :]
        h_before = jnp.where(tile > 0, hh_ref[halo_rows - 1:halo_rows, :].astype(F32), 0.0)
        h_prev = jnp.where(rows == 0, h_before, pltpu.roll(hsv, 1, 0))
        ds = big_g * ig * xr
        dla = big_g * h_prev * a - ds * (a * a) / jnp.maximum(s, 1e-20)
        acc(SG_LAM, dla * r * (RG_LRU_C * _sigmoid(-lam_v)))
        dpr = dla * (-RG_LRU_C * sp) * r * (1.0 - r)
        dpi = big_g * s * xr * ig * (1.0 - ig)
        acc(SG_BA, dpr)
        acc(SG_BX, dpi)
        dprb = dpr.astype(BF16)
        dpib = dpi.astype(BF16)
        yg_ref[:, 0:d] = dprb
        yg_ref[:, d:2 * d] = dpib
        back = []
        for hh in range(N_HEADS):
            sl = slice(hh * hd, (hh + 1) * hd)
            back.append(_nn(dprb[:, sl], wgt_ref[0, hh]) + _nn(dpib[:, sl], wgt_ref[1, hh]))
        dxr = big_g * s * ig + jnp.concatenate(back, axis=1)
        acc(SG_CB4, dxr)
        extd4[pl.ds(0, tm), :] = dxr
        dx_rnn = jnp.zeros((tm, d), F32)
        for k in range(k4):
            term = extd4[pl.ds(k4 - 1 - k, tm), :]
            dx_rnn = dx_rnn + cw4_ref[k:k + 1, :] * term
            acc(SG_CW4 + k, x_rnn * term)
        extd4[pl.ds(tm, CONV4_HALO), :] = extd4[pl.ds(0, CONV4_HALO), :]
        emit(0, dx_rnn)

        dybb = dyb.astype(BF16)
        y3_ref[:, 2 * d:3 * d] = dybb
        acc(SG_BCP, dyb)
        dv3 = _nn(dybb, w3_v[1])
        v1 = v1_ref[...].astype(F32)
        xc = v1 - jnp.mean(v1, axis=-1, keepdims=True)
        rstd = lax.rsqrt(jnp.mean(xc * xc, axis=-1, keepdims=True) + EPS)
        xhat = xc * rstd
        lng_v = lng_ref[...]
        v2 = xhat * lng_v + lnb_ref[...]
        s2 = _sigmoid(v2)
        x3_ref[:, 2 * d:3 * d] = (v2 * s2).astype(BF16)
        dv2 = dv3 * (s2 * (1.0 + v2 * (1.0 - s2)))
        acc(SG_LNG, dv2 * xhat)
        acc(SG_LNB, dv2)
        dxh = dv2 * lng_v
        dv1 = rstd * (dxh - jnp.mean(dxh, axis=-1, keepdims=True)
                      - xhat * jnp.mean(dxh * xhat, axis=-1, keepdims=True))
        acc(SG_CB31, dv1)
        extd31[pl.ds(0, tm), :] = dv1
        _shifted_copies(extd31, es31, tm + CONV31_HALO - SUBLANES)
        sgg = _sigmoid(glu_g)
        v0 = glu_v * sgg
        dv0 = jnp.zeros((tm, d), F32)
        for k in range(k31):
            term = _tap(extd31, es31, k31 - 1 - k, tm)
            dv0 = dv0 + cw31_ref[k:k + 1, :] * term
            acc(SG_CW31 + k, v0 * term)
        extd31[pl.ds(tm, CONV31_HALO), :] = extd31[pl.ds(0, CONV31_HALO), :]
        emit(2, dv0 * sgg)
        emit(3, dv0 * glu_v * sgg * (1.0 - sgg))

        dn = dn_parts[0]
        x = h_ref[...]
        rr = lax.rsqrt(jnp.mean(x * x, axis=-1, keepdims=True) + EPS)
        dx, dgp = _rms_bwd(dn, x, rr, g_ref[...])
        dh1_ref[...] = dh_ref[...] + dx
        sg_ref[SG_MIX:SG_MIX + 1, :] += dgp

    rev = lambda i: (nt - 1 - i, 0)
    row = pl.BlockSpec((tm, d), rev)
    wide = pl.BlockSpec((tm, n_in), rev)
    full = lambda a: pl.BlockSpec(a.shape, lambda i, nd=a.ndim: (0,) * nd)
    halo = pl.BlockSpec((halo_rows, d), lambda i: (jnp.maximum((nt - 1 - i) * per - 1, 0), 0))
    smalls = [cw4, wg, jnp.swapaxes(wg, 2, 3), ba, bx, lam, cw31, lng, lnb]
    return _call(
        body, "mixer_bwd", (nt,),
        [row, row, full(g), wide, row, row, halo, row, row, row, _any()]
        + [full(a) for a in smalls] + [_any()],
        [row, wide, pl.BlockSpec((tm, 3 * d), rev), pl.BlockSpec((tm, 3 * d), rev),
         pl.BlockSpec((tm, 2 * d), rev), pl.BlockSpec((sg_rows, d), lambda i: (0, 0))],
        [jax.ShapeDtypeStruct((tp, d), F32), jax.ShapeDtypeStruct((tp, n_in), BF16),
         jax.ShapeDtypeStruct((tp, 3 * d), BF16), jax.ShapeDtypeStruct((tp, 3 * d), BF16),
         jax.ShapeDtypeStruct((tp, 2 * d), BF16), jax.ShapeDtypeStruct((sg_rows, d), F32)],
        [pltpu.VMEM(win_t.shape, BF16),
         pltpu.VMEM((3, d, d), BF16),
         pltpu.VMEM((tm + CONV4_HALO, d), F32),
         pltpu.VMEM((tm + CONV31_HALO, d), F32),
         pltpu.VMEM((SUBLANES, tm + CONV31_HALO, d), F32),
         pltpu.VMEM((SUBLANES, d), F32),
         pltpu.SemaphoreType.DMA((1 + 3 * N_DEV,))],
        [dh2, h, g, proj, xr_s, hs_s, hs_s, v1_s, ya_s, yb_s, win_t, *smalls, w3_all], comm)


def _tn_matmul(name, x, y, x_spec, y_spec, n_blocks, kb, nb, tm, tp, out_shape, out_spec, out_view, comm=None):
    nt = tp // tm

    def body(x_ref, y_ref, o_ref, acc):
        i = pl.program_id(1)

        @pl.when(i == 0)
        def _():
            acc[...] = jnp.zeros_like(acc)

        acc[...] += _tn(x_ref[...], y_ref[...])

        @pl.when(i == nt - 1)
        def _():
            o_ref[...] = acc[...].astype(BF16).reshape(out_view)

    outs, extra = _call(body, name, (n_blocks, nt), [x_spec, y_spec], [out_spec],
                        [jax.ShapeDtypeStruct(out_shape, BF16)], [pltpu.VMEM((kb, nb), F32)], [x, y], comm)
    return outs[0], extra


def kernel(x, meta_tokens, ffn1_norm, ffn1_w_gu, ffn1_w_down, mix_norm, w_in, b_in, rnn_conv_w, rnn_conv_b, rg_w_a, rg_b_a, rg_w_x, rg_b_x, rg_lambda, rnn_w_proj, conv_dw_w, conv_dw_b, conv_ln_g, conv_ln_b, conv_w_proj, conv_b_proj, w_out, ffn2_norm, ffn2_w_gu, ffn2_w_down, final_norm, loss_target, m_meta_tokens, m_ffn1_norm, m_ffn1_w_gu, m_ffn1_w_down, m_mix_norm, m_w_in, m_b_in, m_rnn_conv_w, m_rnn_conv_b, m_rg_w_a, m_rg_b_a, m_rg_w_x, m_rg_b_x, m_rg_lambda, m_rnn_w_proj, m_conv_dw_w, m_conv_dw_b, m_conv_ln_g, m_conv_ln_b, m_conv_w_proj, m_conv_b_proj, m_w_out, m_ffn2_norm, m_ffn2_w_gu, m_ffn2_w_down, m_final_norm, v_meta_tokens, v_ffn1_norm, v_ffn1_w_gu, v_ffn1_w_down, v_mix_norm, v_w_in, v_b_in, v_rnn_conv_w, v_rnn_conv_b, v_rg_w_a, v_rg_b_a, v_rg_w_x, v_rg_b_x, v_rg_lambda, v_rnn_w_proj, v_conv_dw_w, v_conv_dw_b, v_conv_ln_g, v_conv_ln_b, v_conv_w_proj, v_conv_b_proj, v_w_out, v_ffn2_norm, v_ffn2_w_gu, v_ffn2_w_down, v_final_norm):
    w = dict(locals())
    seq, d = x.shape[1], x.shape[2]
    n_meta = meta_tokens.shape[0]
    t_real = n_meta + seq
    tp, tm, tmx_fwd, tmx, tmt = _tiles(t_real)
    fb = ffn1_w_gu.shape[-1]
    wr = ffn1_w_down.shape[1]
    f = N_DEV * wr
    fc = f // FFN_CHUNKS
    nbc = w_in.shape[-1]
    n_in = N_DEV * nbc
    pr = rnn_w_proj.shape[1]
    hd = rg_w_a.shape[-1]
    gr = rg_w_a.shape[2]
    cw = meta_tokens.shape[1]
    k4, k31 = rnn_conv_w.shape[1], conv_dw_w.shape[1]
    assert n_in == 6 * d and 2 * wr == fb and N_HEADS * hd == d and pr * N_DEV == d

    xi, yi, ci = lax.axis_index("x"), lax.axis_index("y"), lax.axis_index("c")
    core = ci.astype(jnp.int32).reshape(1)
    chip = (2 * xi + yi).astype(jnp.int32).reshape(1)
    me_index = (4 * xi + 2 * yi + ci).astype(jnp.int32).reshape(1)

    for nm in ("ffn1_w_gu", "ffn2_w_gu"):
        for pre in ("", "m_", "v_"):
            w[pre + nm] = jnp.swapaxes(w[pre + nm], 1, 2)

    wgut1 = w["ffn1_w_gu"][0].astype(BF16)
    wgut2 = w["ffn2_w_gu"][0].astype(BF16)
    wd1 = ffn1_w_down[0].astype(BF16)
    wd2 = ffn2_w_down[0].astype(BF16)
    win_loc = w_in[0].astype(BF16)
    win_t_loc = jnp.swapaxes(w_in[0], 0, 1).astype(BF16)
    w3_loc = jnp.concatenate([rnn_w_proj[0], conv_w_proj[0], w_out[0]], axis=0).astype(BF16)
    wg_loc = jnp.stack([rg_w_a[0], rg_w_x[0]]).astype(BF16)
    n_small = n_meta + k4 + k31
    small_rows = -(-n_small // SUBLANES) * SUBLANES
    small_loc = jnp.concatenate([meta_tokens, rnn_conv_w[0], conv_dw_w[0],
                                 jnp.zeros((small_rows - n_small, cw), F32)], axis=0)
    (wgut1_all, wd1_all, wg_all, small_all), h0, tgt = _first_gather(
        [wgut1, wd1, wg_loc, small_loc], 3, x[0], loss_target[0], n_meta, tp)
    wg = wg_all.transpose(1, 2, 0, 3, 4).reshape(2, N_HEADS, hd, hd)
    small_full = small_all.transpose(1, 0, 2).reshape(small_rows, d)
    cw4 = small_full[n_meta:n_meta + k4]
    cw31 = small_full[n_meta + k4:n_meta + k4 + k31]

    wgu1, wdn1 = wgut1_all.reshape(2 * f, d), wd1_all.reshape(f, d)
    (h1, gu1, n1), (win_all, w3_all) = _ffn_fwd(h0, ffn1_norm, wgu1, wdn1, tm, comm=_Gather([win_loc, w3_loc]))
    (h2, proj, n2, xr_s, hs_s, v1_s, ya_s, yb_s), (wgut2_all, wd2_all) = _mixer_fwd(
        h1, mix_norm, b_in, win_all, cw4, rnn_conv_b, wg, rg_b_a, rg_b_x, rg_lambda, cw31, conv_dw_b, conv_ln_g,
        conv_ln_b, conv_b_proj, w3_all, tmx_fwd, comm=_Gather([wgut2, wd2], pass_on_at=0.6))
    wgu2, wdn2 = wgut2_all.reshape(2 * f, d), wd2_all.reshape(f, d)
    (dh3, gu2, n3, loss_part, dgf), (win_t_all,) = _ffn_fwd(
        h2, ffn2_norm, wgu2, wdn2, tm, loss=(tgt, final_norm.reshape(1, d), n_meta, t_real),
        comm=_Gather([win_t_loc], pass_on_at=0.8))
    win_t = win_t_all.reshape(n_in, d)

    def d_w_gu(tag, dgu, n_s, comm=None):
        g, extra = _tn_matmul(
            "d_w_gu" + tag, dgu, n_s,
            pl.BlockSpec((None, tmt, fc), lambda b, i: (b // FFN_CHUNKS, i, b % FFN_CHUNKS)),
            pl.BlockSpec((tmt, d), lambda b, i: (i, 0)),
            2 * FFN_CHUNKS, fc, d, tmt, tp, (2 * FFN_CHUNKS, fc, d),
            pl.BlockSpec((None, fc, d), lambda b, i: (b, 0, 0)), (fc, d), comm)
        return g.reshape(N_DEV, fb, d), extra

    def d_w_down(tag, act, df, comm=None):
        g, extra = _tn_matmul(
            "d_w_down" + tag, act, df,
            pl.BlockSpec((tmt, fc), lambda b, i: (i, b)), pl.BlockSpec((tmt, d), lambda b, i: (i, 0)),
            FFN_CHUNKS, fc, d, tmt, tp, (FFN_CHUNKS, fc, d),
            pl.BlockSpec((None, fc, d), lambda b, i: (b, 0, 0)), (fc, d), comm)
        return g.reshape(N_DEV, wr, d), extra

    (dh2, dgu2, act2, df2, dg_ffn2), _ = _ffn_bwd(dh3, h2, gu2, ffn2_norm, wgu2, wdn2, tm)
    g_wgu2, _ = d_w_gu("2", dgu2, n3)
    g_wd2, _ = d_w_down("2", act2, df2)
    (dh1, dproj, x3, y3, yg, sg), (r_wd2, r_wgu2) = _mixer_bwd(
        dh2, h1, mix_norm, proj, xr_s, hs_s, v1_s, ya_s, yb_s, win_t, cw4, wg, rg_b_a, rg_b_x, rg_lambda, cw31,
        conv_ln_g, conv_ln_b, w3_all, tmx, comm=_Scatter([g_wd2, g_wgu2]))
    g_w3, _ = _tn_matmul(
        "d_w_proj3", x3, y3,
        pl.BlockSpec((tmt, d), lambda b, i: (i, b)), pl.BlockSpec((tmt, d), lambda b, i: (i, b)),
        3, d, d, tmt, tp, (N_DEV, 3, pr, d), pl.BlockSpec((N_DEV, None, pr, d), lambda b, i: (0, b, 0, 0)),
        (N_DEV, pr, d))
    g_wg, _ = _tn_matmul(
        "d_w_gates", xr_s, yg,
        pl.BlockSpec((tmt, hd), lambda b, i: (i, b % N_HEADS)), pl.BlockSpec((tmt, hd), lambda b, i: (i, b)),
        2 * N_HEADS, hd, hd, tmt, tp, (N_DEV, 2 * N_HEADS, gr, hd),
        pl.BlockSpec((N_DEV, None, gr, hd), lambda b, i: (0, b, 0, 0)), (N_DEV, gr, hd))
    g_win, (r_w3, r_wg) = _tn_matmul(
        "d_w_in", n2, dproj,
        pl.BlockSpec((tmt, d), lambda b, i: (i, 0)), pl.BlockSpec((tmt, nbc), lambda b, i: (i, b)),
        N_DEV, d, nbc, tmt, tp, (N_DEV, d, nbc), pl.BlockSpec((None, d, nbc), lambda b, i: (b, 0, 0)), (d, nbc),
        comm=_Scatter([g_w3, g_wg]))
    dg_mix = sg[SG_MIX:SG_MIX + 1]
    (dh0, dgu1, act1, df1, dg_ffn1), (r_win,) = _ffn_bwd(dh1, h0, gu1, ffn1_norm, wgu1, wdn1, tm,
                                                         comm=_Scatter([g_win]))
    grad_x = dh0[n_meta:t_real][None]

    rep_rows = [("ffn1_norm", dg_ffn1), ("mix_norm", dg_mix), ("b_in", sg[SG_BIN:SG_BIN + 6]),
                ("rnn_conv_b", sg[SG_CB4:SG_CB4 + 1]), ("rg_b_a", sg[SG_BA:SG_BA + 1]),
                ("rg_b_x", sg[SG_BX:SG_BX + 1]), ("rg_lambda", sg[SG_LAM:SG_LAM + 1]),
                ("conv_dw_b", sg[SG_CB31:SG_CB31 + 1]), ("conv_ln_g", sg[SG_LNG:SG_LNG + 1]),
                ("conv_ln_b", sg[SG_LNB:SG_LNB + 1]), ("conv_b_proj", sg[SG_BCP:SG_BCP + 1]),
                ("ffn2_norm", dg_ffn2), ("final_norm", dgf)]
    col_rows = [("meta_tokens", dh0[:n_meta]), ("rnn_conv_w", sg[SG_CW4:SG_CW4 + k4]),
                ("conv_dw_w", sg[SG_CW31:SG_CW31 + k31])]
    layout, pieces, r0 = [], [], 0
    for nm, part in rep_rows:
        nr = part.shape[0]
        kind = "wide" if nm == "b_in" else "rep"
        as2d = lambda a: a.reshape(1, -1) if a.ndim == 1 else a
        layout.append((kind, r0, nr, as2d(w[nm]), as2d(w["m_" + nm]), as2d(w["v_" + nm])))
        pieces.append(part)
        r0 += nr
    for nm, part in col_rows:
        nr = part.shape[0]
        sq = lambda a: a.reshape(a.shape[-2], a.shape[-1])
        layout.append(("col", r0, nr, sq(w[nm]), sq(w["m_" + nm]), sq(w["v_" + nm])))
        pieces.append(part)
        r0 += nr
    total_rows = -(-(r0 + 1) // SUBLANES) * SUBLANES
    pieces.append(jnp.zeros((total_rows - 1 - r0, d), F32))
    pieces.append(loss_part)
    small_partial = jnp.concatenate(pieces, axis=0)

    g_wd1, (small_partials,) = d_w_down("1", act1, df1, comm=_Bcast(small_partial))
    g_wgu1, (r_wd1,) = d_w_gu("1", dgu1, n1, comm=_Scatter([g_wd1]))

    g_last = g_wgu1.reshape((4, 2) + g_wgu1.shape[1:])
    (from_sibling,) = _pair_exchange([g_last])
    comb_wgu1 = _pair_add(g_last, from_sibling, core)
    (r_wgu1,) = _chip_exchange([comb_wgu1])

    groups = [(g_wd1, r_wd1, me_index, ["ffn1_w_down"]), (comb_wgu1, r_wgu1, chip, ["ffn1_w_gu"]),
              (g_wd2, r_wd2, me_index, ["ffn2_w_down"]), (g_wgu2, r_wgu2, me_index, ["ffn2_w_gu"]),
              (g_win, r_win, me_index, ["w_in"]), (g_w3, r_w3, me_index, ["w_out", "rnn_w_proj", "conv_w_proj"]),
              (g_wg, r_wg, me_index, ["rg_w_a", "rg_w_x"])]
    res = {}
    for own, recv, idx, group in groups:
        outs = _final_adamw(own, recv, idx, [(w[nm], w["m_" + nm], w["v_" + nm]) for nm in group])
        for nm, o in zip(group, outs):
            res[nm] = o
    for nm in ("ffn1_w_gu", "ffn2_w_gu"):
        res[nm] = tuple(jnp.swapaxes(a, 1, 2) for a in res[nm])

    total, small_out = _small_adamw(small_partials, layout, me_index)
    for (nm, _), o in zip(rep_rows + col_rows, small_out):
        res[nm] = tuple(a.reshape(w[nm].shape) for a in o)

    order = ["meta_tokens", "ffn1_norm", "ffn1_w_gu", "ffn1_w_down", "mix_norm", "w_in", "b_in", "rnn_conv_w",
             "rnn_conv_b", "rg_w_a", "rg_b_a", "rg_w_x", "rg_b_x", "rg_lambda", "rnn_w_proj", "conv_dw_w",
             "conv_dw_b", "conv_ln_g", "conv_ln_b", "conv_w_proj", "conv_b_proj", "w_out", "ffn2_norm",
             "ffn2_w_gu", "ffn2_w_down", "final_norm"]
    return (total[total_rows - 1, 0], grad_x, *[res[nm][0] for nm in order], *[res[nm][1] for nm in order],
            *[res[nm][2] for nm in order], *[res[nm][3] for nm in order])
```

```python
import functools
import math

import jax
import jax.numpy as jnp
from jax import lax
from jax.experimental import pallas as pl
from jax.experimental.pallas import tpu as pltpu

F32 = jnp.float32
BF16 = jnp.bfloat16
MESH = pl.DeviceIdType.MESH
N_DEV = 8
N_HEADS = 4
RG_LRU_C = 8.0
EPS = 1e-6
FFN_RES = 0.5
ADAM_LR, ADAM_B1, ADAM_B2, ADAM_EPS, ADAM_WD, ADAM_STEP = 0.001, 0.9, 0.999, 1e-08, 0.01, 10
V7X_VMEM_LIMIT = 56 * 1024 * 1024
CONV4_HALO = 8
CONV31_HALO = 32
SUBLANES = 8
STAGE_ROWS = 512
FFN_CHUNKS = 2
FFN_FWD_CHUNKS = 1
GELU_C = math.sqrt(2.0 / math.pi)
GELU_K = 0.044715


def _any():
    return pl.BlockSpec(memory_space=pl.ANY)


def _params(n_grid):
    return pltpu.CompilerParams(dimension_semantics=("arbitrary",) * n_grid, vmem_limit_bytes=V7X_VMEM_LIMIT)


def _nn(a, b):
    return jnp.dot(a, b, preferred_element_type=F32)


def _nt(a, b):
    return lax.dot_general(a, b, (((1,), (1,)), ((), ())), preferred_element_type=F32)


def _tn(a, b):
    return lax.dot_general(a, b, (((0,), (0,)), ((), ())), preferred_element_type=F32)


def _sigmoid(x):
    return 0.5 * jnp.tanh(0.5 * x) + 0.5


def _rowsum(x):
    return jnp.sum(x, axis=0, keepdims=True)


def _rms_fwd(x, g):
    r = lax.rsqrt(jnp.mean(x * x, axis=-1, keepdims=True) + EPS)
    return x * r * g, r


def _rms_bwd(dn, x, r, g):
    xr = x * r
    gy = dn * g
    dx = r * (gy - xr * jnp.mean(gy * xr, axis=-1, keepdims=True))
    return dx, _rowsum(dn * xr)


def _gelu(y):
    t = jnp.tanh(GELU_C * (y + GELU_K * y * y * y))
    return 0.5 * y * (1.0 + t), t


def _gelu_grad(y, t):
    return 0.5 * (1.0 + t) + 0.5 * y * (1.0 - t * t) * GELU_C * (1.0 + 3.0 * GELU_K * y * y)


def _softplus(x):
    return jnp.maximum(x, 0.0) + jnp.log(1.0 + jnp.exp(-jnp.abs(x)))


def _one_minus_exp(z):
    series = -z * (1.0 + 0.5 * z * (1.0 + z * (1.0 / 3.0) * (1.0 + 0.25 * z)))
    return jnp.where(z > -0.05, series, 1.0 - jnp.exp(z))


def _tiles(t_real):
    if t_real > 2048:
        tm = 384
        tp = -(-t_real // tm) * tm
        return tp, tm, tm // 2, tm // 2, tp // 2
    tm = 128
    tp = -(-t_real // tm) * tm
    return tp, tm, tm // 2, tm // 2, tm


def _load_weights(copies, sems):
    cps = [pltpu.make_async_copy(s, d, sems.at[k]) for k, (s, d) in enumerate(copies)]
    for cp in cps:
        cp.start()
    for cp in cps:
        cp.wait()


def _position():
    x, y, c = lax.axis_index("x"), lax.axis_index("y"), lax.axis_index("c")
    chips = [(1 - x, y), (x, 1 - y), (1 - x, 1 - y)]
    return x, y, c, chips


def _slot(p):
    return 4 * p[0] + 2 * p[1] + p[2]


class _Gather:
    def __init__(self, shards, pass_on_at=None):
        self.shards = list(shards)
        self.n = len(self.shards)
        self.pass_on_at = pass_on_at

    def inputs(self):
        return self.shards

    def out_shape(self):
        return [jax.ShapeDtypeStruct((N_DEV,) + s.shape, s.dtype) for s in self.shards]

    N_SEMS = 9

    def scratch(self):
        return [pltpu.SemaphoreType.DMA((self.N_SEMS * self.n,)), pltpu.SemaphoreType.DMA((self.N_SEMS * self.n,)),
                pltpu.SemaphoreType.DMA((self.n,))]

    def _plan(self, ins, outs, sems):
        send_sems, recv_sems, local_sems = sems
        x, y, c, _ = _position()
        me, sib, xn, yn, dg = (x, y, c), (x, y, 1 - c), (1 - x, y, c), (x, 1 - y, c), (1 - x, 1 - y, c)
        other = lambda p: (p[0], p[1], 1 - c)

        def blk(a, p, half=None):
            ref = outs[a].at[_slot(p)]
            if half is None:
                return ref
            rows = self.shards[a].shape[0] // 2
            return ref.at[pl.ds(half * rows, rows)]

        def copy(a, k, dst, to, src=None):
            return pltpu.make_async_remote_copy(
                src_ref=dst if src is None else src, dst_ref=dst,
                send_sem=send_sems.at[self.N_SEMS * a + k], recv_sem=recv_sems.at[self.N_SEMS * a + k],
                device_id=to, device_id_type=MESH)

        cp = {"mine": [pltpu.make_async_copy(ins[a], blk(a, me), local_sems.at[a]) for a in range(self.n)]}
        for a in range(self.n):
            cp[a] = dict(
                own=[copy(a, 0, blk(a, me), sib, src=ins[a]), copy(a, 1, blk(a, me), xn, src=ins[a]),
                     copy(a, 2, blk(a, me), yn, src=ins[a])],
                from_x=copy(a, 1, blk(a, xn), me), from_y=copy(a, 2, blk(a, yn), me),
                relay_x=copy(a, 3, blk(a, xn, 0), yn), relay_y=copy(a, 4, blk(a, yn, 1), xn),
                diag0=copy(a, 3, blk(a, dg, 0), me), diag1=copy(a, 4, blk(a, dg, 1), me),
                pass_x=copy(a, 5, blk(a, xn), sib), pass_y=copy(a, 6, blk(a, yn), sib),
                pass_d0=copy(a, 7, blk(a, dg, 0), sib), pass_d1=copy(a, 8, blk(a, dg, 1), sib),
                from_sib=[copy(a, 0, blk(a, sib), me), copy(a, 5, blk(a, other(xn)), me),
                          copy(a, 6, blk(a, other(yn)), me), copy(a, 7, blk(a, other(dg), 0), me),
                          copy(a, 8, blk(a, other(dg), 1), me)])
        return cp

    def start(self, ins, outs, sems):
        cp = self._plan(ins, outs, sems)
        for c in cp["mine"]:
            c.start()
        for a in range(self.n):
            for c in cp[a]["own"]:
                c.start()

    def pass_on(self, ins, outs, sems):
        cp = self._plan(ins, outs, sems)
        for a in range(self.n):
            cp[a]["from_x"].wait_recv()
            cp[a]["relay_x"].start()
            cp[a]["pass_x"].start()
        for a in range(self.n):
            cp[a]["from_y"].wait_recv()
            cp[a]["relay_y"].start()
            cp[a]["pass_y"].start()

    def finish(self, ins, outs, sems):
        if self.pass_on_at is None:
            self.pass_on(ins, outs, sems)
        cp = self._plan(ins, outs, sems)
        for a in range(self.n):
            cp[a]["diag0"].wait_recv()
            cp[a]["pass_d0"].start()
            cp[a]["diag1"].wait_recv()
            cp[a]["pass_d1"].start()
        for a in range(self.n):
            for c in cp[a]["from_sib"]:
                c.wait_recv()
            for c in cp[a]["own"] + [cp[a][k] for k in ("relay_x", "relay_y", "pass_x", "pass_y", "pass_d0", "pass_d1")]:
                c.wait_send()
        for c in cp["mine"]:
            c.wait()


class _Scatter:
    def __init__(self, grads):
        self.grads = list(grads)
        self.n = len(self.grads)

    def inputs(self):
        return self.grads

    def out_shape(self):
        return [jax.ShapeDtypeStruct((N_DEV - 1,) + g.shape[1:], g.dtype) for g in self.grads]

    def scratch(self):
        return [pltpu.SemaphoreType.DMA((7 * self.n,)), pltpu.SemaphoreType.DMA((7 * self.n,))]

    def _plan(self, ins, outs, sems):
        send_sems, recv_sems = sems
        x, y, c, _ = _position()
        cps = []
        for a in range(self.n):
            for k in range(1, N_DEV):
                peer = (x ^ (k >> 2), y ^ ((k >> 1) & 1), c ^ (k & 1))
                cps.append(pltpu.make_async_remote_copy(
                    src_ref=ins[a].at[_slot(peer)], dst_ref=outs[a].at[k - 1],
                    send_sem=send_sems.at[7 * a + k - 1], recv_sem=recv_sems.at[7 * a + k - 1],
                    device_id=peer, device_id_type=MESH))
        return cps

    def start(self, ins, outs, sems):
        for cp in self._plan(ins, outs, sems):
            cp.start()

    def finish(self, ins, outs, sems):
        for cp in self._plan(ins, outs, sems):
            cp.wait()


def _hosted(inner, n_in, n_out, comm, grid):
    if comm is None:
        return inner
    nc_in, nc_out, ns = len(comm.inputs()), len(comm.out_shape()), len(comm.scratch())

    def body(*refs):
        o0 = n_in + nc_in
        s0 = o0 + n_out + nc_out
        main = refs[:n_in] + refs[o0:o0 + n_out] + refs[s0:len(refs) - ns]
        c_in, c_out, c_sems = refs[n_in:o0], refs[o0 + n_out:s0], refs[len(refs) - ns:]
        ids = [pl.program_id(ax) for ax in range(len(grid))]
        first = functools.reduce(jnp.logical_and, [i == 0 for i in ids])
        last = functools.reduce(jnp.logical_and, [i == g - 1 for i, g in zip(ids, grid)])

        @pl.when(first)
        def _():
            comm.start(c_in, c_out, c_sems)

        inner(*main)

        if getattr(comm, "pass_on_at", None) is not None:
            assert len(grid) == 1
            @pl.when(ids[0] == min(grid[0] - 1, int(comm.pass_on_at * grid[0])))
            def _():
                comm.pass_on(c_in, c_out, c_sems)

        @pl.when(last)
        def _():
            comm.finish(c_in, c_out, c_sems)

    return body


def _call(inner, name, grid, in_specs, out_specs, out_shape, scratch, args, comm=None):
    n_in, n_out = len(args), len(out_shape)
    body = _hosted(inner, n_in, n_out, comm, grid)
    if comm is not None:
        in_specs = list(in_specs) + [_any()] * len(comm.inputs())
        args = list(args) + comm.inputs()
        out_specs = list(out_specs) + [_any()] * len(comm.out_shape())
        out_shape = list(out_shape) + comm.out_shape()
        scratch = list(scratch) + comm.scratch()
    outs = pl.pallas_call(
        body, name=name, grid=grid, in_specs=list(in_specs), out_specs=list(out_specs), out_shape=list(out_shape),
        scratch_shapes=list(scratch), compiler_params=_params(len(grid)))(*args)
    return list(outs[:n_out]), list(outs[n_out:])


class _Bcast:
    def __init__(self, block):
        self.block = block

    def inputs(self):
        return [self.block]

    def out_shape(self):
        return [jax.ShapeDtypeStruct((N_DEV,) + self.block.shape, self.block.dtype)]

    def scratch(self):
        return [pltpu.SemaphoreType.DMA((N_DEV - 1,)), pltpu.SemaphoreType.DMA((N_DEV - 1,)),
                pltpu.SemaphoreType.DMA((1,))]

    def _plan(self, ins, outs, sems):
        send_sems, recv_sems, local_sem = sems
        x, y, c, _ = _position()
        mine = outs[0].at[_slot((x, y, c))]
        cps = []
        for k in range(1, N_DEV):
            peer = (x ^ (k >> 2), y ^ ((k >> 1) & 1), c ^ (k & 1))
            cps.append(pltpu.make_async_remote_copy(
                src_ref=ins[0], dst_ref=mine, send_sem=send_sems.at[k - 1], recv_sem=recv_sems.at[k - 1],
                device_id=peer, device_id_type=MESH))
        return pltpu.make_async_copy(ins[0], mine, local_sem.at[0]), cps

    def start(self, ins, outs, sems):
        own, cps = self._plan(ins, outs, sems)
        own.start()
        for cp in cps:
            cp.start()

    def finish(self, ins, outs, sems):
        own, cps = self._plan(ins, outs, sems)
        for cp in cps:
            cp.wait()
        own.wait()


def _first_gather(shards, small_idx, x2, t2, n_meta, tp):
    comm = _Gather(shards)
    n = comm.n
    seq, d = x2.shape
    t_real = n_meta + seq
    n_pad = tp - t_real
    cw = d // N_DEV
    rows = STAGE_ROWS if seq % STAGE_ROWS == 0 else seq
    n_chunks = seq // rows

    def body(*refs):
        ins, (x_ref, t_ref) = refs[:n], refs[n:n + 2]
        outs, (h0_ref, tg_ref) = refs[n + 2:2 * n + 2], refs[2 * n + 2:2 * n + 4]
        sems = refs[2 * n + 4:2 * n + 7]
        buf, zeros, in_sems, out_sems, misc_sems = refs[2 * n + 7:]
        comm.start(ins, outs, sems)
        zeros[...] = jnp.zeros_like(zeros)
        fills = [pltpu.make_async_copy(zeros.at[pl.ds(0, n_pad)], h0_ref.at[pl.ds(t_real, n_pad)], misc_sems.at[0]),
                 pltpu.make_async_copy(zeros.at[pl.ds(0, n_pad)], tg_ref.at[pl.ds(t_real, n_pad)], misc_sems.at[1]),
                 pltpu.make_async_copy(zeros.at[pl.ds(0, n_meta)], tg_ref.at[pl.ds(0, n_meta)], misc_sems.at[2])]
        for cp in fills:
            cp.start()
        jobs = [(src, dst, c) for src, dst in ((x_ref, h0_ref), (t_ref, tg_ref)) for c in range(n_chunks)]

        def load(k):
            src, _, c = jobs[k]
            return pltpu.make_async_copy(src.at[pl.ds(c * rows, rows)], buf.at[k % 2], in_sems.at[k % 2])

        def store(k):
            _, dst, c = jobs[k]
            return pltpu.make_async_copy(buf.at[k % 2], dst.at[pl.ds(n_meta + c * rows, rows)], out_sems.at[k % 2])

        load(0).start()
        for k in range(len(jobs)):
            load(k).wait()
            if k + 1 < len(jobs):
                if k >= 1:
                    store(k - 1).wait()
                load(k + 1).start()
            store(k).start()
        for k in range(max(0, len(jobs) - 2), len(jobs)):
            store(k).wait()
        comm.finish(ins, outs, sems)
        meta = [pltpu.make_async_copy(outs[small_idx].at[k, pl.ds(0, n_meta)],
                                      h0_ref.at[pl.ds(0, n_meta), pl.ds(k * cw, cw)], misc_sems.at[3 + k])
                for k in range(N_DEV)]
        for cp in meta:
            cp.start()
        for cp in fills + meta:
            cp.wait()

    staged = [jax.ShapeDtypeStruct((tp, d), F32)] * 2
    outs = pl.pallas_call(
        body, name="weights_all_gather", out_shape=comm.out_shape() + staged,
        in_specs=[_any()] * (n + 2), out_specs=[_any()] * (n + 2),
        scratch_shapes=comm.scratch() + [
            pltpu.VMEM((2, rows, d), F32), pltpu.VMEM((max(n_pad, n_meta), d), F32),
            pltpu.SemaphoreType.DMA((2,)), pltpu.SemaphoreType.DMA((2,)), pltpu.SemaphoreType.DMA((3 + N_DEV,))],
        compiler_params=pltpu.CompilerParams(vmem_limit_bytes=V7X_VMEM_LIMIT),
    )(*shards, x2, t2)
    return outs[:n], outs[n], outs[n + 1]


def _pair_exchange(grads):
    n = len(grads)

    def body(*refs):
        ins, outs = refs[:n], refs[n:2 * n]
        send_sems, recv_sems = refs[2 * n:]
        x, y, c, _ = _position()
        cps = [pltpu.make_async_remote_copy(
            src_ref=ins[a].at[:, 1 - c], dst_ref=outs[a],
            send_sem=send_sems.at[a], recv_sem=recv_sems.at[a],
            device_id=(x, y, 1 - c), device_id_type=MESH) for a in range(n)]
        for cp in cps:
            cp.start()
        for cp in cps:
            cp.wait()

    return pl.pallas_call(
        body, name="grads_pair_exchange",
        out_shape=[jax.ShapeDtypeStruct((4,) + g.shape[2:], g.dtype) for g in grads],
        in_specs=[_any()] * n, out_specs=[_any()] * n,
        scratch_shapes=[pltpu.SemaphoreType.DMA((n,)), pltpu.SemaphoreType.DMA((n,))],
    )(*grads)


def _chip_exchange(combs):
    n = len(combs)

    def body(*refs):
        ins, outs = refs[:n], refs[n:2 * n]
        send_sems, recv_sems = refs[2 * n:]
        x, y, c, chips = _position()
        cps = []
        for a in range(n):
            for j, (cx, cy) in enumerate(chips):
                cps.append(pltpu.make_async_remote_copy(
                    src_ref=ins[a].at[2 * cx + cy], dst_ref=outs[a].at[j],
                    send_sem=send_sems.at[3 * a + j], recv_sem=recv_sems.at[3 * a + j],
                    device_id=(cx, cy, c), device_id_type=MESH))
        for cp in cps:
            cp.start()
        for cp in cps:
            cp.wait()

    return pl.pallas_call(
        body, name="grads_chip_exchange",
        out_shape=[jax.ShapeDtypeStruct((3,) + g.shape[1:], g.dtype) for g in combs],
        in_specs=[_any()] * n, out_specs=[_any()] * n,
        scratch_shapes=[pltpu.SemaphoreType.DMA((3 * n,)), pltpu.SemaphoreType.DMA((3 * n,))],
    )(*combs)


def _pair_add(grad, recv, core):
    blk = grad.shape[2:]
    zeros = (0,) * len(blk)

    def body(core_ref, g_ref, r_ref, o_ref):
        del core_ref
        o_ref[...] = (g_ref[...].astype(F32) + r_ref[...].astype(F32)).astype(BF16)

    return pl.pallas_call(
        body, name="grads_pair_add",
        out_shape=jax.ShapeDtypeStruct((4,) + blk, BF16),
        grid_spec=pltpu.PrefetchScalarGridSpec(
            num_scalar_prefetch=1, grid=(4,),
            in_specs=[pl.BlockSpec((None, None) + blk, lambda i, cr: (i, cr[0]) + zeros),
                      pl.BlockSpec((None,) + blk, lambda i, cr: (i,) + zeros)],
            out_specs=pl.BlockSpec((None,) + blk, lambda i, cr: (i,) + zeros)),
        compiler_params=_params(1),
    )(core, grad, recv)


def _adamw(w, g, m, v):
    m2 = ADAM_B1 * m + (1.0 - ADAM_B1) * g
    v2 = ADAM_B2 * v + (1.0 - ADAM_B2) * (g * g)
    m_hat = m2 / (1.0 - ADAM_B1 ** ADAM_STEP)
    v_hat = v2 / (1.0 - ADAM_B2 ** ADAM_STEP)
    delta = -ADAM_LR * (m_hat / (jnp.sqrt(v_hat) + ADAM_EPS) + ADAM_WD * w)
    return delta, m2, v2


def _final_adamw(own, recv, idx, parts):
    blk = own.shape[1:]
    n_recv = recv.shape[0]
    n_parts = len(parts)
    per = blk[0] // n_parts if n_parts > 1 else None
    rows = blk[-2]
    n_chunks = 1 if n_parts > 1 else (4 if rows % 64 == 0 and rows >= 512 else (2 if rows % 32 == 0 else 1))
    cblk = blk[:-2] + (rows // n_chunks, blk[-1])
    lead = (0,) * (len(blk) - 2)

    def body(idx_ref, c_ref, r_ref, *refs):
        del idx_ref
        ins, outs = refs[:3 * n_parts], refs[3 * n_parts:]
        g = c_ref[...].astype(F32)
        for k in range(n_recv):
            g = g + r_ref[k].astype(F32)
        for p in range(n_parts):
            w_ref, m_ref, v_ref = ins[3 * p:3 * p + 3]
            if n_parts == 1:
                gp = g
            elif per == 1:
                gp = g[p]
            else:
                gp = g[p * per:(p + 1) * per]
            delta, m2, v2 = _adamw(w_ref[0], gp, m_ref[0], v_ref[0])
            o = outs[4 * p:4 * p + 4]
            o[0][0] = gp
            o[1][0] = delta
            o[2][0] = m2
            o[3][0] = v2

    flat = [a for wmv in parts for a in wmv]

    def part_spec(a):
        shape = a.shape[:-2] + (a.shape[-2] // n_chunks, a.shape[-1])
        return pl.BlockSpec(shape, lambda i, cr, nd=a.ndim: (0,) * (nd - 2) + (i, 0))

    outs = pl.pallas_call(
        body, name="grads_sum_adamw",
        out_shape=[jax.ShapeDtypeStruct(wmv[0].shape, F32) for wmv in parts for _ in range(4)],
        grid_spec=pltpu.PrefetchScalarGridSpec(
            num_scalar_prefetch=1, grid=(n_chunks,),
            in_specs=[pl.BlockSpec((None,) + cblk, lambda i, cr: (cr[0],) + lead + (i, 0)),
                      pl.BlockSpec((n_recv,) + cblk, lambda i, cr: (0,) + lead + (i, 0))]
                     + [part_spec(a) for a in flat],
            out_specs=[part_spec(wmv[0]) for wmv in parts for _ in range(4)]),
        compiler_params=_params(1),
    )(idx, own, recv, *flat)
    return [tuple(outs[4 * p:4 * p + 4]) for p in range(n_parts)]


def _small_adamw(partials, layout, me_index):
    _, rows, d = partials.shape
    n = len(layout)
    cw = d // N_DEV

    def body(me_ref, p_ref, *refs):
        ins, t_ref, outs = refs[:3 * n], refs[3 * n], refs[3 * n + 1:]
        me = me_ref[0]
        total = p_ref[0]
        for j in range(1, N_DEV):
            total = total + p_ref[j]
        t_ref[...] = total
        for e, (kind, r0, nr, _, _, _) in enumerate(layout):
            w_ref, m_ref, v_ref = ins[3 * e:3 * e + 3]
            o = outs[4 * e:4 * e + 4]
            if kind == "rep":
                g = t_ref[r0:r0 + nr, :]
                delta, m2, v2 = _adamw(w_ref[...], g, m_ref[...], v_ref[...])
                for ref, val in zip(o, (g, delta, m2, v2)):
                    ref[...] = val
            elif kind == "wide":
                for q in range(nr):
                    sl = slice(q * d, (q + 1) * d)
                    g = t_ref[r0 + q:r0 + q + 1, :]
                    delta, m2, v2 = _adamw(w_ref[:, sl], g, m_ref[:, sl], v_ref[:, sl])
                    for ref, val in zip(o, (g, delta, m2, v2)):
                        ref[:, sl] = val
            else:
                for j in range(N_DEV):
                    @pl.when(me == j)
                    def _(j=j, o=o, w_ref=w_ref, m_ref=m_ref, v_ref=v_ref, r0=r0, nr=nr):
                        g = t_ref[r0:r0 + nr, j * cw:(j + 1) * cw]
                        delta, m2, v2 = _adamw(w_ref[...], g, m_ref[...], v_ref[...])
                        for ref, val in zip(o, (g, delta, m2, v2)):
                            ref[...] = val

    flat = [a for ent in layout for a in ent[3:]]
    vm = pl.BlockSpec(memory_space=pltpu.VMEM)
    outs = pl.pallas_call(
        body, name="small_adamw",
        out_shape=[jax.ShapeDtypeStruct((rows, d), F32)]
                  + [jax.ShapeDtypeStruct(ent[3].shape, F32) for ent in layout for _ in range(4)],
        in_specs=[pl.BlockSpec(memory_space=pltpu.SMEM), vm] + [vm] * len(flat),
        out_specs=[vm] * (1 + 4 * n),
        compiler_params=pltpu.CompilerParams(vmem_limit_bytes=V7X_VMEM_LIMIT),
    )(me_index, partials, *flat)
    return outs[0], [tuple(outs[1 + 4 * e:5 + 4 * e]) for e in range(n)]


def _ffn_fwd(h, g, wgu, wd, tm, loss=None, comm=None):
    tp, d = h.shape
    f = wd.shape[0]
    fc = f // FFN_FWD_CHUNKS
    nt = tp // tm
    with_loss = loss is not None
    if with_loss:
        tgt, gf, n_meta, t_real = loss

    def body(*refs):
        if with_loss:
            (h_ref, g_ref, wgu_hbm, wd_hbm, tgt_ref, gf_ref, out_ref, gu_ref, n_ref, loss_ref, dgf_ref,
             wgu_v, wd_v, sems) = refs
        else:
            h_ref, g_ref, wgu_hbm, wd_hbm, out_ref, gu_ref, n_ref, wgu_v, wd_v, sems = refs
        i = pl.program_id(0)

        @pl.when(i == 0)
        def _():
            _load_weights([(wgu_hbm, wgu_v), (wd_hbm, wd_v)], sems)
            if with_loss:
                loss_ref[...] = jnp.zeros_like(loss_ref)
                dgf_ref[...] = jnp.zeros_like(dgf_ref)

        x = h_ref[...]
        n, _ = _rms_fwd(x, g_ref[...])
        nb = n.astype(BF16)
        n_ref[...] = nb
        acc = jnp.zeros((tm, d), F32)
        for j in range(FFN_FWD_CHUNKS):
            cols = slice(j * fc, (j + 1) * fc)
            gate = _nt(nb, wgu_v[pl.ds(j * fc, fc), :])
            up = _nt(nb, wgu_v[pl.ds(f + j * fc, fc), :])
            gu_ref[0, :, cols] = gate.astype(BF16)
            gu_ref[1, :, cols] = up.astype(BF16)
            act = (gate * _sigmoid(gate) * up).astype(BF16)
            acc = acc + _nn(act, wd_v[pl.ds(j * fc, fc), :])
        hn = x + FFN_RES * acc
        if not with_loss:
            out_ref[...] = hn
        else:
            gfv = gf_ref[...]
            r = lax.rsqrt(jnp.mean(hn * hn, axis=-1, keepdims=True) + EPS)
            xr = hn * r
            rows = i * tm + lax.broadcasted_iota(jnp.int32, (tm, 1), 0)
            mask = jnp.logical_and(rows >= n_meta, rows < t_real)
            diff = jnp.where(mask, xr * gfv - tgt_ref[...], 0.0)
            loss_ref[...] += jnp.zeros_like(loss_ref) + 0.5 * jnp.sum(diff * diff) / d
            dy = diff / d
            gy = dy * gfv
            out_ref[...] = r * (gy - xr * jnp.mean(gy * xr, axis=-1, keepdims=True))
            dgf_ref[...] += _rowsum(dy * xr)

    row = pl.BlockSpec((tm, d), lambda i: (i, 0))
    vec = pl.BlockSpec((1, d), lambda i: (0, 0))
    in_specs = [row, vec, _any(), _any()]
    out_shape = [jax.ShapeDtypeStruct((tp, d), F32), jax.ShapeDtypeStruct((2, tp, f), BF16),
                 jax.ShapeDtypeStruct((tp, d), BF16)]
    out_specs = [row, pl.BlockSpec((2, tm, f), lambda i: (0, i, 0)), row]
    args = [h, g, wgu, wd]
    if with_loss:
        in_specs += [row, vec]
        out_shape += [jax.ShapeDtypeStruct((1, d), F32), jax.ShapeDtypeStruct((1, d), F32)]
        out_specs += [vec, vec]
        args += [tgt, gf]
    return _call(body, "ffn_fwd_loss" if with_loss else "ffn_fwd", (nt,), in_specs, out_specs, out_shape,
                 [pltpu.VMEM((2 * f, d), BF16), pltpu.VMEM((f, d), BF16), pltpu.SemaphoreType.DMA((2,))],
                 args, comm)


def _ffn_bwd(dh, h, gu, g, wgu, wd, tm, comm=None):
    tp, d = h.shape
    f = wd.shape[0]
    fc = f // FFN_CHUNKS
    nt = tp // tm

    def body(dh_ref, h_ref, gu_ref, g_ref, wgu_hbm, wd_hbm,
             dhin_ref, dgu_ref, act_ref, df_ref, dg_ref, wgu_v, wd_v, dn_v, sems):
        i, j = pl.program_id(0), pl.program_id(1)

        @pl.when(jnp.logical_and(i == 0, j == 0))
        def _():
            _load_weights([(wgu_hbm, wgu_v), (wd_hbm, wd_v)], sems)
            dg_ref[...] = jnp.zeros_like(dg_ref)

        dfb = (FFN_RES * dh_ref[...]).astype(BF16)

        @pl.when(j == 0)
        def _():
            df_ref[...] = dfb
            dn_v[...] = jnp.zeros_like(dn_v)

        lo = pl.multiple_of(j * fc, 16)
        dact = _nt(dfb, wd_v[pl.ds(lo, fc), :])
        gate = gu_ref[0].astype(F32)
        up = gu_ref[1].astype(F32)
        sg = _sigmoid(gate)
        silu = gate * sg
        act_ref[...] = (silu * up).astype(BF16)
        dgate = (dact * up * (sg * (1.0 + gate * (1.0 - sg)))).astype(BF16)
        dup = (dact * silu).astype(BF16)
        dgu_ref[0] = dgate
        dgu_ref[1] = dup
        dn_v[...] += _nn(dgate, wgu_v[pl.ds(lo, fc), :]) + _nn(dup, wgu_v[pl.ds(pl.multiple_of(f + j * fc, 16), fc), :])

        @pl.when(j == FFN_CHUNKS - 1)
        def _():
            x = h_ref[...]
            r = lax.rsqrt(jnp.mean(x * x, axis=-1, keepdims=True) + EPS)
            dx, dgp = _rms_bwd(dn_v[...], x, r, g_ref[...])
            dhin_ref[...] = dh_ref[...] + dx
            dg_ref[...] += dgp

    row = pl.BlockSpec((tm, d), lambda i, j: (i, 0))
    vec = pl.BlockSpec((1, d), lambda i, j: (0, 0))
    hid2 = pl.BlockSpec((2, tm, fc), lambda i, j: (0, i, j))
    return _call(
        body, "ffn_bwd", (nt, FFN_CHUNKS),
        [row, row, hid2, vec, _any(), _any()],
        [row, hid2, pl.BlockSpec((tm, fc), lambda i, j: (i, j)), row, vec],
        [jax.ShapeDtypeStruct((tp, d), F32), jax.ShapeDtypeStruct((2, tp, f), BF16),
         jax.ShapeDtypeStruct((tp, f), BF16), jax.ShapeDtypeStruct((tp, d), BF16),
         jax.ShapeDtypeStruct((1, d), F32)],
        [pltpu.VMEM((2 * f, d), BF16), pltpu.VMEM((f, d), BF16), pltpu.VMEM((tm, d), F32),
         pltpu.SemaphoreType.DMA((2,))],
        [dh, h, gu, g, wgu, wd], comm)


def _piece_segments(q, d, nb_cols):
    segs = []
    for j in range(N_DEV):
        lo, hi = max(q * d, j * nb_cols), min((q + 1) * d, (j + 1) * nb_cols)
        if lo < hi:
            segs.append((j, lo - q * d, hi - q * d, lo - j * nb_cols, hi - j * nb_cols))
    return segs


def _w3_copies(w3_hbm, rows, w3_v):
    return [(w3_hbm.at[k, pl.ds(q * rows, rows)], w3_v.at[q, pl.ds(k * rows, rows)])
            for q in range(3) for k in range(N_DEV)]


def _gates(xrb, wg_ref, ba, bx, lam, hd):
    pre_r, pre_i = [], []
    for hh in range(N_HEADS):
        xh = xrb[:, hh * hd:(hh + 1) * hd]
        pre_r.append(_nn(xh, wg_ref[0, hh]))
        pre_i.append(_nn(xh, wg_ref[1, hh]))
    r = _sigmoid(jnp.concatenate(pre_r, axis=1) + ba)
    ig = _sigmoid(jnp.concatenate(pre_i, axis=1) + bx)
    sp = _softplus(-lam)
    log_a = -RG_LRU_C * r * sp
    a = jnp.exp(log_a)
    s = jnp.sqrt(_one_minus_exp(2.0 * log_a))
    return r, ig, sp, a, s


def _scan_fwd(a, u, h_prev):
    tm = a.shape[0]
    rows = lax.broadcasted_iota(jnp.int32, a.shape, 0)
    d = 1
    while d < tm:
        if d < SUBLANES:
            keep = rows >= d
            u = jnp.where(keep, a * pltpu.roll(u, d, 0) + u, u)
            a = jnp.where(keep, a * pltpu.roll(a, d, 0), a)
        else:
            u = jnp.concatenate([u[:d], a[d:] * u[:tm - d] + u[d:]], axis=0)
            a = jnp.concatenate([a[:d], a[d:] * a[:tm - d]], axis=0)
        d *= 2
    return u + a * h_prev


def _scan_bwd(b, v, g_next):
    tm = b.shape[0]
    rows = lax.broadcasted_iota(jnp.int32, b.shape, 0)
    d = 1
    while d < tm:
        if d < SUBLANES:
            keep = rows < tm - d
            v = jnp.where(keep, v + b * pltpu.roll(v, tm - d, 0), v)
            b = jnp.where(keep, b * pltpu.roll(b, tm - d, 0), b)
        else:
            v = jnp.concatenate([v[:tm - d] + b[:tm - d] * v[d:], v[tm - d:]], axis=0)
            b = jnp.concatenate([b[:tm - d] * b[d:], b[tm - d:]], axis=0)
        d *= 2
    return v + b * g_next


def _shifted_copies(ext_ref, es_ref, n_rows):
    for s in range(1, SUBLANES):
        es_ref[s, pl.ds(0, n_rows), :] = ext_ref[pl.ds(s, n_rows), :]


def _tap(ext_ref, es_ref, off, tm):
    q, s = divmod(off, SUBLANES)
    if s == 0:
        return ext_ref[pl.ds(SUBLANES * q, tm), :]
    return es_ref[s, pl.ds(SUBLANES * q, tm), :]


def _mixer_fwd(h, g, b_in, win_all, cw4, cb4, wg, ba, bx, lam, cw31, cb31, lng, lnb, bcp, w3_all, tm, comm=None):
    tp, d = h.shape
    nb_cols = win_all.shape[-1]
    n_in = N_DEV * nb_cols
    hd = wg.shape[-1]
    k4, k31 = cw4.shape[0], cw31.shape[0]
    w3_rows = d // N_DEV

    def body(h_ref, g_ref, b_ref, win_hbm, cw4_ref, cb4_ref, wg_ref, ba_ref, bx_ref, lam_ref, cw31_ref, cb31_ref,
             lng_ref, lnb_ref, bcp_ref, w3_hbm,
             h2_ref, p_ref, n_ref, xr_ref, hs_ref, v1_ref, ya_ref, yb_ref,
             win_v, w3_v, ext4, ext31, es31, hcar, sems):
        @pl.when(pl.program_id(0) == 0)
        def _():
            _load_weights([(win_hbm, win_v)] + _w3_copies(w3_hbm, w3_rows, w3_v), sems)
            ext4[pl.ds(0, CONV4_HALO), :] = jnp.zeros((CONV4_HALO, d), F32)
            ext31[pl.ds(0, CONV31_HALO), :] = jnp.zeros((CONV31_HALO, d), F32)
            hcar[...] = jnp.zeros_like(hcar)

        n, _ = _rms_fwd(h_ref[...], g_ref[...])
        nb = n.astype(BF16)
        n_ref[...] = nb

        def piece(q):
            parts = [_nn(nb, win_v[j, :, bl:bh]) for j, _, _, bl, bh in _piece_segments(q, d, nb_cols)]
            pq = (jnp.concatenate(parts, axis=1) + b_ref[:, q * d:(q + 1) * d]).astype(BF16)
            p_ref[:, q * d:(q + 1) * d] = pq
            return pq.astype(F32)

        x_rnn, y_rnn, glu_v, glu_g, gate_a, gate_b = [piece(q) for q in range(6)]

        ext4[pl.ds(CONV4_HALO, tm), :] = x_rnn
        xr = cb4_ref[...] + jnp.zeros((tm, d), F32)
        for k in range(k4):
            xr = xr + cw4_ref[k:k + 1, :] * ext4[pl.ds(CONV4_HALO - (k4 - 1) + k, tm), :]
        ext4[pl.ds(0, CONV4_HALO), :] = ext4[pl.ds(tm, CONV4_HALO), :]
        xrb = xr.astype(BF16)
        xr_ref[...] = xrb
        xr = xrb.astype(F32)
        _, ig, _, a, s = _gates(xrb, wg_ref, ba_ref[...], bx_ref[...], lam_ref[...], hd)
        hseq = _scan_fwd(a, s * (ig * xr), hcar[0:1, :])
        hcar[0:1, :] = hseq[tm - 1:tm, :]
        hs_ref[...] = hseq.astype(BF16)
        gl, _ = _gelu(y_rnn)
        ya = _nn((hseq * gl).astype(BF16), w3_v[0])
        ya_ref[...] = ya.astype(BF16)

        ext31[pl.ds(CONV31_HALO, tm), :] = glu_v * _sigmoid(glu_g)
        _shifted_copies(ext31, es31, tm + CONV31_HALO - SUBLANES)
        v1 = cb31_ref[...] + jnp.zeros((tm, d), F32)
        for k in range(k31):
            v1 = v1 + cw31_ref[k:k + 1, :] * _tap(ext31, es31, CONV31_HALO - (k31 - 1) + k, tm)
        ext31[pl.ds(0, CONV31_HALO), :] = ext31[pl.ds(tm, CONV31_HALO), :]
        v1b = v1.astype(BF16)
        v1_ref[...] = v1b
        v1 = v1b.astype(F32)
        xc = v1 - jnp.mean(v1, axis=-1, keepdims=True)
        rstd = lax.rsqrt(jnp.mean(xc * xc, axis=-1, keepdims=True) + EPS)
        v2 = xc * rstd * lng_ref[...] + lnb_ref[...]
        yb = _nn((v2 * _sigmoid(v2)).astype(BF16), w3_v[1]) + bcp_ref[...]
        yb_ref[...] = yb.astype(BF16)

        merged = _sigmoid(gate_a) * ya + _sigmoid(gate_b) * yb
        h2_ref[...] = h_ref[...] + _nn(merged.astype(BF16), w3_v[2])

    row = pl.BlockSpec((tm, d), lambda i: (i, 0))
    wide = pl.BlockSpec((tm, n_in), lambda i: (i, 0))
    full = lambda a: pl.BlockSpec(a.shape, lambda i, nd=a.ndim: (0,) * nd)
    smalls = [cw4, cb4, wg, ba, bx, lam, cw31, cb31, lng, lnb, bcp]
    return _call(
        body, "mixer_fwd", (tp // tm,),
        [row, full(g), full(b_in), _any()] + [full(a) for a in smalls] + [_any()],
        [row, wide] + [row] * 6,
        [jax.ShapeDtypeStruct((tp, d), F32), jax.ShapeDtypeStruct((tp, n_in), BF16)]
        + [jax.ShapeDtypeStruct((tp, d), BF16)] * 6,
        [pltpu.VMEM(win_all.shape, BF16),
         pltpu.VMEM((3, d, d), BF16),
         pltpu.VMEM((tm + CONV4_HALO, d), F32),
         pltpu.VMEM((tm + CONV31_HALO, d), F32),
         pltpu.VMEM((SUBLANES, tm + CONV31_HALO, d), F32),
         pltpu.VMEM((SUBLANES, d), F32),
         pltpu.SemaphoreType.DMA((1 + 3 * N_DEV,))],
        [h, g, b_in, win_all, *smalls, w3_all], comm)


SG_BIN, SG_CW4, SG_CB4, SG_BA, SG_BX, SG_LAM, SG_CB31, SG_LNG, SG_LNB, SG_BCP, SG_MIX, SG_CW31 = 0, 6, 10, 11, 12, 13, 14, 15, 16, 17, 18, 19


def _mixer_bwd(dh2, h, g, proj, xr_s, hs_s, v1_s, ya_s, yb_s, win_t, cw4, wg, ba, bx, lam, cw31, lng, lnb, w3_all, tm,
               comm=None):
    tp, d = dh2.shape
    n_in = proj.shape[1]
    hd = wg.shape[-1]
    k4, k31 = cw4.shape[0], cw31.shape[0]
    nt = tp // tm
    w3_rows = d // N_DEV
    sg_rows = -(-(SG_CW31 + k31) // SUBLANES) * SUBLANES
    halo_rows = 16
    per = tm // halo_rows

    def body(dh_ref, h_ref, g_ref, p_ref, xr_ref, hs_ref, hh_ref, v1_ref, ya_ref, yb_ref, win_hbm,
             cw4_ref, wg_ref, wgt_ref, ba_ref, bx_ref, lam_ref, cw31_ref, lng_ref, lnb_ref, w3_hbm,
             dh1_ref, dp_ref, x3_ref, y3_ref, yg_ref, sg_ref,
             win_v, w3_v, extd4, extd31, es31, gcar, sems):
        i = pl.program_id(0)
        tile = nt - 1 - i

        @pl.when(i == 0)
        def _():
            _load_weights([(win_hbm, win_v)] + _w3_copies(w3_hbm, w3_rows, w3_v), sems)
            for q in range(3):
                w3_v[q] = w3_v[q].T
            extd4[pl.ds(tm, CONV4_HALO), :] = jnp.zeros((CONV4_HALO, d), F32)
            extd31[pl.ds(tm, CONV31_HALO), :] = jnp.zeros((CONV31_HALO, d), F32)
            gcar[...] = jnp.zeros_like(gcar)
            sg_ref[...] = jnp.zeros_like(sg_ref)

        def acc(row, val):
            sg_ref[row:row + 1, :] += _rowsum(val)

        rows = lax.broadcasted_iota(jnp.int32, (tm, d), 0)
        x_rnn = p_ref[:, 0:d].astype(F32)
        y_rnn = p_ref[:, d:2 * d].astype(F32)
        glu_v = p_ref[:, 2 * d:3 * d].astype(F32)
        glu_g = p_ref[:, 3 * d:4 * d].astype(F32)
        sga = _sigmoid(p_ref[:, 4 * d:5 * d].astype(F32))
        sgb = _sigmoid(p_ref[:, 5 * d:6 * d].astype(F32))
        ya = ya_ref[...].astype(F32)
        yb = yb_ref[...].astype(F32)

        dmob = dh_ref[...].astype(BF16)
        dmerged = _nn(dmob, w3_v[2])
        x3_ref[:, 0:d] = (sga * ya + sgb * yb).astype(BF16)
        y3_ref[:, 0:d] = dmob
        dya = sga * dmerged
        dyb = sgb * dmerged
        dn_parts = []

        def emit(q, val):
            vb = val.astype(BF16)
            dp_ref[:, q * d:(q + 1) * d] = vb
            acc(SG_BIN + q, val)
            term = _nn(vb, win_v[pl.ds(q * d, d), :])
            dn_parts[:] = [term if not dn_parts else dn_parts[0] + term]

        emit(4, dmerged * ya * sga * (1.0 - sga))
        emit(5, dmerged * yb * sgb * (1.0 - sgb))

        dyab = dya.astype(BF16)
        y3_ref[:, d:2 * d] = dyab
        dza = _nn(dyab, w3_v[0])
        hsv = hs_ref[...].astype(F32)
        gl, th = _gelu(y_rnn)
        x3_ref[:, d:2 * d] = (hsv * gl).astype(BF16)
        emit(1, dza * hsv * _gelu_grad(y_rnn, th))
        dhs = dza * gl
        xrb = xr_ref[...]
        xr = xrb.astype(F32)
        lam_v = lam_ref[...]
        r, ig, sp, a, s = _gates(xrb, wg_ref, ba_ref[...], bx_ref[...], lam_v, hd)
        b = jnp.where(rows == tm - 1, gcar[1:2, :], pltpu.roll(a, tm - 1, 0))
        big_g = _scan_bwd(b, dhs, gcar[0:1, :])
        gcar[0:1, :] = big_g[0:1, :]
        gcar[1:2, :] = a[0:1, :]
        h_before = jnp.where(tile > 0, hh_ref[halo_rows - 1:halo_rows, :].astype(F32), 0.0)
        h_prev = jnp.where(rows == 0, h_before, pltpu.roll(hsv, 1, 0))
        ds = big_g * ig * xr
        dla = big_g * h_prev * a - ds * (a * a) / jnp.maximum(s, 1e-20)
        acc(SG_LAM, dla * r * (RG_LRU_C * _sigmoid(-lam_v)))
        dpr = dla * (-RG_LRU_C * sp) * r * (1.0 - r)
        dpi = big_g * s * xr * ig * (1.0 - ig)
        acc(SG_BA, dpr)
        acc(SG_BX, dpi)
        dprb = dpr.astype(BF16)
        dpib = dpi.astype(BF16)
        yg_ref[:, 0:d] = dprb
        yg_ref[:, d:2 * d] = dpib
        back = []
        for hh in range(N_HEADS):
            sl = slice(hh * hd, (hh + 1) * hd)
            back.append(_nn(dprb[:, sl], wgt_ref[0, hh]) + _nn(dpib[:, sl], wgt_ref[1, hh]))
        dxr = big_g * s * ig + jnp.concatenate(back, axis=1)
        acc(SG_CB4, dxr)
        extd4[pl.ds(0, tm), :] = dxr
        dx_rnn = jnp.zeros((tm, d), F32)
        for k in range(k4):
            term = extd4[pl.ds(k4 - 1 - k, tm), :]
            dx_rnn = dx_rnn + cw4_ref[k:k + 1, :] * term
            acc(SG_CW4 + k, x_rnn * term)
        extd4[pl.ds(tm, CONV4_HALO), :] = extd4[pl.ds(0, CONV4_HALO), :]
        emit(0, dx_rnn)

        dybb = dyb.astype(BF16)
        y3_ref[:, 2 * d:3 * d] = dybb
        acc(SG_BCP, dyb)
        dv3 = _nn(dybb, w3_v[1])
        v1 = v1_ref[...].astype(F32)
        xc = v1 - jnp.mean(v1, axis=-1, keepdims=True)
        rstd = lax.rsqrt(jnp.mean(xc * xc, axis=-1, keepdims=True) + EPS)
        xhat = xc * rstd
        lng_v = lng_ref[...]
        v2 = xhat * lng_v + lnb_ref[...]
        s2 = _sigmoid(v2)
        x3_ref[:, 2 * d:3 * d] = (v2 * s2).astype(BF16)
        dv2 = dv3 * (s2 * (1.0 + v2 * (1.0 - s2)))
        acc(SG_LNG, dv2 * xhat)
        acc(SG_LNB, dv2)
        dxh = dv2 * lng_v
        dv1 = rstd * (dxh - jnp.mean(dxh, axis=-1, keepdims=True)
                      - xhat * jnp.mean(dxh * xhat, axis=-1, keepdims=True))
        acc(SG_CB31, dv1)
        extd31[pl.ds(0, tm), :] = dv1
        _shifted_copies(extd31, es31, tm + CONV31_HALO - SUBLANES)
        sgg = _sigmoid(glu_g)
        v0 = glu_v * sgg
        dv0 = jnp.zeros((tm, d), F32)
        for k in range(k31):
            term = _tap(extd31, es31, k31 - 1 - k, tm)
            dv0 = dv0 + cw31_ref[k:k + 1, :] * term
            acc(SG_CW31 + k, v0 * term)
        extd31[pl.ds(tm, CONV31_HALO), :] = extd31[pl.ds(0, CONV31_HALO), :]
        emit(2, dv0 * sgg)
        emit(3, dv0 * glu_v * sgg * (1.0 - sgg))

        dn = dn_parts[0]
        x = h_ref[...]
        rr = lax.rsqrt(jnp.mean(x * x, axis=-1, keepdims=True) + EPS)
        dx, dgp = _rms_bwd(dn, x, rr, g_ref[...])
        dh1_ref[...] = dh_ref[...] + dx
        sg_ref[SG_MIX:SG_MIX + 1, :] += dgp

    rev = lambda i: (nt - 1 - i, 0)
    row = pl.BlockSpec((tm, d), rev)
    wide = pl.BlockSpec((tm, n_in), rev)
    full = lambda a: pl.BlockSpec(a.shape, lambda i, nd=a.ndim: (0,) * nd)
    halo = pl.BlockSpec((halo_rows, d), lambda i: (jnp.maximum((nt - 1 - i) * per - 1, 0), 0))
    smalls = [cw4, wg, jnp.swapaxes(wg, 2, 3), ba, bx, lam, cw31, lng, lnb]
    return _call(
        body, "mixer_bwd", (nt,),
        [row, row, full(g), wide, row, row, halo, row, row, row, _any()]
        + [full(a) for a in smalls] + [_any()],
        [row, wide, pl.BlockSpec((tm, 3 * d), rev), pl.BlockSpec((tm, 3 * d), rev),
         pl.BlockSpec((tm, 2 * d), rev), pl.BlockSpec((sg_rows, d), lambda i: (0, 0))],
        [jax.ShapeDtypeStruct((tp, d), F32), jax.ShapeDtypeStruct((tp, n_in), BF16),
         jax.ShapeDtypeStruct((tp, 3 * d), BF16), jax.ShapeDtypeStruct((tp, 3 * d), BF16),
         jax.ShapeDtypeStruct((tp, 2 * d), BF16), jax.ShapeDtypeStruct((sg_rows, d), F32)],
        [pltpu.VMEM(win_t.shape, BF16),
         pltpu.VMEM((3, d, d), BF16),
         pltpu.VMEM((tm + CONV4_HALO, d), F32),
         pltpu.VMEM((tm + CONV31_HALO, d), F32),
         pltpu.VMEM((SUBLANES, tm + CONV31_HALO, d), F32),
         pltpu.VMEM((SUBLANES, d), F32),
         pltpu.SemaphoreType.DMA((1 + 3 * N_DEV,))],
        [dh2, h, g, proj, xr_s, hs_s, hs_s, v1_s, ya_s, yb_s, win_t, *smalls, w3_all], comm)


def _tn_matmul(name, x, y, x_spec, y_spec, n_blocks, kb, nb, tm, tp, out_shape, out_spec, out_view, comm=None):
    nt = tp // tm

    def body(x_ref, y_ref, o_ref, acc):
        i = pl.program_id(1)

        @pl.when(i == 0)
        def _():
            acc[...] = jnp.zeros_like(acc)

        acc[...] += _tn(x_ref[...], y_ref[...])

        @pl.when(i == nt - 1)
        def _():
            o_ref[...] = acc[...].astype(BF16).reshape(out_view)

    outs, extra = _call(body, name, (n_blocks, nt), [x_spec, y_spec], [out_spec],
                        [jax.ShapeDtypeStruct(out_shape, BF16)], [pltpu.VMEM((kb, nb), F32)], [x, y], comm)
    return outs[0], extra


def kernel(x, meta_tokens, ffn1_norm, ffn1_w_gu, ffn1_w_down, mix_norm, w_in, b_in, rnn_conv_w, rnn_conv_b, rg_w_a, rg_b_a, rg_w_x, rg_b_x, rg_lambda, rnn_w_proj, conv_dw_w, conv_dw_b, conv_ln_g, conv_ln_b, conv_w_proj, conv_b_proj, w_out, ffn2_norm, ffn2_w_gu, ffn2_w_down, final_norm, loss_target, m_meta_tokens, m_ffn1_norm, m_ffn1_w_gu, m_ffn1_w_down, m_mix_norm, m_w_in, m_b_in, m_rnn_conv_w, m_rnn_conv_b, m_rg_w_a, m_rg_b_a, m_rg_w_x, m_rg_b_x, m_rg_lambda, m_rnn_w_proj, m_conv_dw_w, m_conv_dw_b, m_conv_ln_g, m_conv_ln_b, m_conv_w_proj, m_conv_b_proj, m_w_out, m_ffn2_norm, m_ffn2_w_gu, m_ffn2_w_down, m_final_norm, v_meta_tokens, v_ffn1_norm, v_ffn1_w_gu, v_ffn1_w_down, v_mix_norm, v_w_in, v_b_in, v_rnn_conv_w, v_rnn_conv_b, v_rg_w_a, v_rg_b_a, v_rg_w_x, v_rg_b_x, v_rg_lambda, v_rnn_w_proj, v_conv_dw_w, v_conv_dw_b, v_conv_ln_g, v_conv_ln_b, v_conv_w_proj, v_conv_b_proj, v_w_out, v_ffn2_norm, v_ffn2_w_gu, v_ffn2_w_down, v_final_norm):
    w = dict(locals())
    seq, d = x.shape[1], x.shape[2]
    n_meta = meta_tokens.shape[0]
    t_real = n_meta + seq
    tp, tm, tmx_fwd, tmx, tmt = _tiles(t_real)
    fb = ffn1_w_gu.shape[-1]
    wr = ffn1_w_down.shape[1]
    f = N_DEV * wr
    fc = f // FFN_CHUNKS
    nbc = w_in.shape[-1]
    n_in = N_DEV * nbc
    pr = rnn_w_proj.shape[1]
    hd = rg_w_a.shape[-1]
    gr = rg_w_a.shape[2]
    cw = meta_tokens.shape[1]
    k4, k31 = rnn_conv_w.shape[1], conv_dw_w.shape[1]
    assert n_in == 6 * d and 2 * wr == fb and N_HEADS * hd == d and pr * N_DEV == d

    xi, yi, ci = lax.axis_index("x"), lax.axis_index("y"), lax.axis_index("c")
    core = ci.astype(jnp.int32).reshape(1)
    chip = (2 * xi + yi).astype(jnp.int32).reshape(1)
    me_index = (4 * xi + 2 * yi + ci).astype(jnp.int32).reshape(1)

    for nm in ("ffn1_w_gu", "ffn2_w_gu"):
        for pre in ("", "m_", "v_"):
            w[pre + nm] = jnp.swapaxes(w[pre + nm], 1, 2)

    wgut1 = w["ffn1_w_gu"][0].astype(BF16)
    wgut2 = w["ffn2_w_gu"][0].astype(BF16)
    wd1 = ffn1_w_down[0].astype(BF16)
    wd2 = ffn2_w_down[0].astype(BF16)
    win_loc = w_in[0].astype(BF16)
    win_t_loc = jnp.swapaxes(w_in[0], 0, 1).astype(BF16)
    w3_loc = jnp.concatenate([rnn_w_proj[0], conv_w_proj[0], w_out[0]], axis=0).astype(BF16)
    wg_loc = jnp.stack([rg_w_a[0], rg_w_x[0]]).astype(BF16)
    n_small = n_meta + k4 + k31
    small_rows = -(-n_small // SUBLANES) * SUBLANES
    small_loc = jnp.concatenate([meta_tokens, rnn_conv_w[0], conv_dw_w[0],
                                 jnp.zeros((small_rows - n_small, cw), F32)], axis=0)
    (wgut1_all, wd1_all, wg_all, small_all), h0, tgt = _first_gather(
        [wgut1, wd1, wg_loc, small_loc], 3, x[0], loss_target[0], n_meta, tp)
    wg = wg_all.transpose(1, 2, 0, 3, 4).reshape(2, N_HEADS, hd, hd)
    small_full = small_all.transpose(1, 0, 2).reshape(small_rows, d)
    cw4 = small_full[n_meta:n_meta + k4]
    cw31 = small_full[n_meta + k4:n_meta + k4 + k31]

    wgu1, wdn1 = wgut1_all.reshape(2 * f, d), wd1_all.reshape(f, d)
    (h1, gu1, n1), (win_all, w3_all) = _ffn_fwd(h0, ffn1_norm, wgu1, wdn1, tm,
                                                comm=_Gather([win_loc, w3_loc], pass_on_at=0.55))
    (h2, proj, n2, xr_s, hs_s, v1_s, ya_s, yb_s), (wgut2_all, wd2_all) = _mixer_fwd(
        h1, mix_norm, b_in, win_all, cw4, rnn_conv_b, wg, rg_b_a, rg_b_x, rg_lambda, cw31, conv_dw_b, conv_ln_g,
        conv_ln_b, conv_b_proj, w3_all, tmx_fwd, comm=_Gather([wgut2, wd2], pass_on_at=0.6))
    wgu2, wdn2 = wgut2_all.reshape(2 * f, d), wd2_all.reshape(f, d)
    (dh3, gu2, n3, loss_part, dgf), (win_t_all,) = _ffn_fwd(
        h2, ffn2_norm, wgu2, wdn2, tm, loss=(tgt, final_norm.reshape(1, d), n_meta, t_real),
        comm=_Gather([win_t_loc], pass_on_at=0.6))
    win_t = win_t_all.reshape(n_in, d)

    def d_w_gu(tag, dgu, n_s, comm=None):
        g, extra = _tn_matmul(
            "d_w_gu" + tag, dgu, n_s,
            pl.BlockSpec((None, tmt, fc), lambda b, i: (b // FFN_CHUNKS, i, b % FFN_CHUNKS)),
            pl.BlockSpec((tmt, d), lambda b, i: (i, 0)),
            2 * FFN_CHUNKS, fc, d, tmt, tp, (2 * FFN_CHUNKS, fc, d),
            pl.BlockSpec((None, fc, d), lambda b, i: (b, 0, 0)), (fc, d), comm)
        return g.reshape(N_DEV, fb, d), extra

    def d_w_down(tag, act, df, comm=None):
        g, extra = _tn_matmul(
            "d_w_down" + tag, act, df,
            pl.BlockSpec((tmt, fc), lambda b, i: (i, b)), pl.BlockSpec((tmt, d), lambda b, i: (i, 0)),
            FFN_CHUNKS, fc, d, tmt, tp, (FFN_CHUNKS, fc, d),
            pl.BlockSpec((None, fc, d), lambda b, i: (b, 0, 0)), (fc, d), comm)
        return g.reshape(N_DEV, wr, d), extra

    (dh2, dgu2, act2, df2, dg_ffn2), _ = _ffn_bwd(dh3, h2, gu2, ffn2_norm, wgu2, wdn2, tm)
    g_wgu2, _ = d_w_gu("2", dgu2, n3)
    g_wd2, _ = d_w_down("2", act2, df2)
    (dh1, dproj, x3, y3, yg, sg), (r_wd2, r_wgu2) = _mixer_bwd(
        dh2, h1, mix_norm, proj, xr_s, hs_s, v1_s, ya_s, yb_s, win_t, cw4, wg, rg_b_a, rg_b_x, rg_lambda, cw31,
        conv_ln_g, conv_ln_b, w3_all, tmx, comm=_Scatter([g_wd2, g_wgu2]))
    g_w3, _ = _tn_matmul(
        "d_w_proj3", x3, y3,
        pl.BlockSpec((tmt, d), lambda b, i: (i, b)), pl.BlockSpec((tmt, d), lambda b, i: (i, b)),
        3, d, d, tmt, tp, (N_DEV, 3, pr, d), pl.BlockSpec((N_DEV, None, pr, d), lambda b, i: (0, b, 0, 0)),
        (N_DEV, pr, d))
    g_wg, _ = _tn_matmul(
        "d_w_gates", xr_s, yg,
        pl.BlockSpec((tmt, hd), lambda b, i: (i, b % N_HEADS)), pl.BlockSpec((tmt, hd), lambda b, i: (i, b)),
        2 * N_HEADS, hd, hd, tmt, tp, (N_DEV, 2 * N_HEADS, gr, hd),
        pl.BlockSpec((N_DEV, None, gr, hd), lambda b, i: (0, b, 0, 0)), (N_DEV, gr, hd))
    g_win, (r_w3, r_wg) = _tn_matmul(
        "d_w_in", n2, dproj,
        pl.BlockSpec((tmt, d), lambda b, i: (i, 0)), pl.BlockSpec((tmt, nbc), lambda b, i: (i, b)),
        N_DEV, d, nbc, tmt, tp, (N_DEV, d, nbc), pl.BlockSpec((None, d, nbc), lambda b, i: (b, 0, 0)), (d, nbc),
        comm=_Scatter([g_w3, g_wg]))
    dg_mix = sg[SG_MIX:SG_MIX + 1]
    (dh0, dgu1, act1, df1, dg_ffn1), (r_win,) = _ffn_bwd(dh1, h0, gu1, ffn1_norm, wgu1, wdn1, tm,
                                                         comm=_Scatter([g_win]))
    grad_x = dh0[n_meta:t_real][None]

    rep_rows = [("ffn1_norm", dg_ffn1), ("mix_norm", dg_mix), ("b_in", sg[SG_BIN:SG_BIN + 6]),
                ("rnn_conv_b", sg[SG_CB4:SG_CB4 + 1]), ("rg_b_a", sg[SG_BA:SG_BA + 1]),
                ("rg_b_x", sg[SG_BX:SG_BX + 1]), ("rg_lambda", sg[SG_LAM:SG_LAM + 1]),
                ("conv_dw_b", sg[SG_CB31:SG_CB31 + 1]), ("conv_ln_g", sg[SG_LNG:SG_LNG + 1]),
                ("conv_ln_b", sg[SG_LNB:SG_LNB + 1]), ("conv_b_proj", sg[SG_BCP:SG_BCP + 1]),
                ("ffn2_norm", dg_ffn2), ("final_norm", dgf)]
    col_rows = [("meta_tokens", dh0[:n_meta]), ("rnn_conv_w", sg[SG_CW4:SG_CW4 + k4]),
                ("conv_dw_w", sg[SG_CW31:SG_CW31 + k31])]
    layout, pieces, r0 = [], [], 0
    for nm, part in rep_rows:
        nr = part.shape[0]
        kind = "wide" if nm == "b_in" else "rep"
        as2d = lambda a: a.reshape(1, -1) if a.ndim == 1 else a
        layout.append((kind, r0, nr, as2d(w[nm]), as2d(w["m_" + nm]), as2d(w["v_" + nm])))
        pieces.append(part)
        r0 += nr
    for nm, part in col_rows:
        nr = part.shape[0]
        sq = lambda a: a.reshape(a.shape[-2], a.shape[-1])
        layout.append(("col", r0, nr, sq(w[nm]), sq(w["m_" + nm]), sq(w["v_" + nm])))
        pieces.append(part)
        r0 += nr
    total_rows = -(-(r0 + 1) // SUBLANES) * SUBLANES
    pieces.append(jnp.zeros((total_rows - 1 - r0, d), F32))
    pieces.append(loss_part)
    small_partial = jnp.concatenate(pieces, axis=0)

    g_wd1, (small_partials,) = d_w_down("1", act1, df1, comm=_Bcast(small_partial))
    g_wgu1, (r_wd1,) = d_w_gu("1", dgu1, n1, comm=_Scatter([g_wd1]))

    g_last = g_wgu1.reshape((4, 2) + g_wgu1.shape[1:])
    (from_sibling,) = _pair_exchange([g_last])
    comb_wgu1 = _pair_add(g_last, from_sibling, core)
    (r_wgu1,) = _chip_exchange([comb_wgu1])

    groups = [(g_wd1, r_wd1, me_index, ["ffn1_w_down"]), (comb_wgu1, r_wgu1, chip, ["ffn1_w_gu"]),
              (g_wd2, r_wd2, me_index, ["ffn2_w_down"]), (g_wgu2, r_wgu2, me_index, ["ffn2_w_gu"]),
              (g_win, r_win, me_index, ["w_in"]), (g_w3, r_w3, me_index, ["w_out", "rnn_w_proj", "conv_w_proj"]),
              (g_wg, r_wg, me_index, ["rg_w_a", "rg_w_x"])]
    res = {}
    for own, recv, idx, group in groups:
        outs = _final_adamw(own, recv, idx, [(w[nm], w["m_" + nm], w["v_" + nm]) for nm in group])
        for nm, o in zip(group, outs):
            res[nm] = o
    for nm in ("ffn1_w_gu", "ffn2_w_gu"):
        res[nm] = tuple(jnp.swapaxes(a, 1, 2) for a in res[nm])

    total, small_out = _small_adamw(small_partials, layout, me_index)
    for (nm, _), o in zip(rep_rows + col_rows, small_out):
        res[nm] = tuple(a.reshape(w[nm].shape) for a in o)

    order = ["meta_tokens", "ffn1_norm", "ffn1_w_gu", "ffn1_w_down", "mix_norm", "w_in", "b_in", "rnn_conv_w",
             "rnn_conv_b", "rg_w_a", "rg_b_a", "rg_w_x", "rg_b_x", "rg_lambda", "rnn_w_proj", "conv_dw_w",
             "conv_dw_b", "conv_ln_g", "conv_ln_b", "conv_w_proj", "conv_b_proj", "w_out", "ffn2_norm",
             "ffn2_w_gu", "ffn2_w_down", "final_norm"]
    return (total[total_rows - 1, 0], grad_x, *[res[nm][0] for nm in order], *[res[nm][1] for nm in order],
            *[res[nm][2] for nm in order], *[res[nm][3] for nm in order])
```

```python
import functools
import math

import jax
import jax.numpy as jnp
from jax import lax
from jax.experimental import pallas as pl
from jax.experimental.pallas import tpu as pltpu

F32 = jnp.float32
BF16 = jnp.bfloat16
MESH = pl.DeviceIdType.MESH
N_DEV = 8
N_HEADS = 4
RG_LRU_C = 8.0
EPS = 1e-6
FFN_RES = 0.5
ADAM_LR, ADAM_B1, ADAM_B2, ADAM_EPS, ADAM_WD, ADAM_STEP = 0.001, 0.9, 0.999, 1e-08, 0.01, 10
V7X_VMEM_LIMIT = 56 * 1024 * 1024
CONV4_HALO = 8
CONV31_HALO = 32
SUBLANES = 8
STAGE_ROWS = 512
FFN_CHUNKS = 2
FFN_FWD_CHUNKS = 1
GELU_C = math.sqrt(2.0 / math.pi)
GELU_K = 0.044715


def _any():
    return pl.BlockSpec(memory_space=pl.ANY)


def _params(n_grid):
    return pltpu.CompilerParams(dimension_semantics=("arbitrary",) * n_grid, vmem_limit_bytes=V7X_VMEM_LIMIT)


def _nn(a, b):
    return jnp.dot(a, b, preferred_element_type=F32)


def _nt(a, b):
    return lax.dot_general(a, b, (((1,), (1,)), ((), ())), preferred_element_type=F32)


def _tn(a, b):
    return lax.dot_general(a, b, (((0,), (0,)), ((), ())), preferred_element_type=F32)


def _sigmoid(x):
    return 0.5 * jnp.tanh(0.5 * x) + 0.5


def _rowsum(x):
    return jnp.sum(x, axis=0, keepdims=True)


def _rms_fwd(x, g):
    r = lax.rsqrt(jnp.mean(x * x, axis=-1, keepdims=True) + EPS)
    return x * r * g, r


def _rms_bwd(dn, x, r, g):
    xr = x * r
    gy = dn * g
    dx = r * (gy - xr * jnp.mean(gy * xr, axis=-1, keepdims=True))
    return dx, _rowsum(dn * xr)


def _gelu(y):
    t = jnp.tanh(GELU_C * (y + GELU_K * y * y * y))
    return 0.5 * y * (1.0 + t), t


def _gelu_grad(y, t):
    return 0.5 * (1.0 + t) + 0.5 * y * (1.0 - t * t) * GELU_C * (1.0 + 3.0 * GELU_K * y * y)


def _softplus(x):
    return jnp.maximum(x, 0.0) + jnp.log(1.0 + jnp.exp(-jnp.abs(x)))


def _one_minus_exp(z):
    series = -z * (1.0 + 0.5 * z * (1.0 + z * (1.0 / 3.0) * (1.0 + 0.25 * z)))
    return jnp.where(z > -0.05, series, 1.0 - jnp.exp(z))


def _tiles(t_real):
    if t_real > 2048:
        tm = 384
        tp = -(-t_real // tm) * tm
        return tp, tm, tm // 2, tm // 2, tp // 2, tp
    tm = 128
    tp = -(-t_real // tm) * tm
    return tp, tm, tm // 2, tm // 2, tm, tm


def _load_weights(copies, sems):
    cps = [pltpu.make_async_copy(s, d, sems.at[k]) for k, (s, d) in enumerate(copies)]
    for cp in cps:
        cp.start()
    for cp in cps:
        cp.wait()


def _position():
    x, y, c = lax.axis_index("x"), lax.axis_index("y"), lax.axis_index("c")
    chips = [(1 - x, y), (x, 1 - y), (1 - x, 1 - y)]
    return x, y, c, chips


def _slot(p):
    return 4 * p[0] + 2 * p[1] + p[2]


class _Lazy(dict):
    def __getitem__(self, key):
        val = dict.__getitem__(self, key)
        return val() if callable(val) else val


class _Gather:
    def __init__(self, shards, pass_on_at=None):
        self.shards = list(shards)
        self.n = len(self.shards)
        self.pass_on_at = pass_on_at

    def inputs(self):
        return self.shards

    def out_shape(self):
        return [jax.ShapeDtypeStruct((N_DEV,) + s.shape, s.dtype) for s in self.shards]

    N_SEMS = 9

    def scratch(self):
        return [pltpu.SemaphoreType.DMA((self.N_SEMS * self.n,)), pltpu.SemaphoreType.DMA((self.N_SEMS * self.n,)),
                pltpu.SemaphoreType.DMA((self.n,))]

    def _plan(self, ins, outs, sems):
        send_sems, recv_sems, local_sems = sems
        x, y, c, _ = _position()
        me, sib, xn, yn, dg = (x, y, c), (x, y, 1 - c), (1 - x, y, c), (x, 1 - y, c), (1 - x, 1 - y, c)
        other = lambda p: (p[0], p[1], 1 - c)

        def blk(a, p, half=None):
            ref = outs[a].at[_slot(p)]
            if half is None:
                return ref
            rows = self.shards[a].shape[0] // 2
            return ref.at[pl.ds(half * rows, rows)]

        def copy(a, k, dst, to, src=None):
            return pltpu.make_async_remote_copy(
                src_ref=dst if src is None else src, dst_ref=dst,
                send_sem=send_sems.at[self.N_SEMS * a + k], recv_sem=recv_sems.at[self.N_SEMS * a + k],
                device_id=to, device_id_type=MESH)

        cp = _Lazy(mine=lambda: [pltpu.make_async_copy(ins[a], blk(a, me), local_sems.at[a]) for a in range(self.n)])
        for a in range(self.n):
            cp[a] = _Lazy(
                own=lambda a=a: [copy(a, 0, blk(a, me), sib, src=ins[a]), copy(a, 1, blk(a, me), xn, src=ins[a]),
                                 copy(a, 2, blk(a, me), yn, src=ins[a])],
                from_x=lambda a=a: copy(a, 1, blk(a, xn), me), from_y=lambda a=a: copy(a, 2, blk(a, yn), me),
                relay_x=lambda a=a: copy(a, 3, blk(a, xn, 0), yn), relay_y=lambda a=a: copy(a, 4, blk(a, yn, 1), xn),
                diag0=lambda a=a: copy(a, 3, blk(a, dg, 0), me), diag1=lambda a=a: copy(a, 4, blk(a, dg, 1), me),
                pass_x=lambda a=a: copy(a, 5, blk(a, xn), sib), pass_y=lambda a=a: copy(a, 6, blk(a, yn), sib),
                pass_d0=lambda a=a: copy(a, 7, blk(a, dg, 0), sib), pass_d1=lambda a=a: copy(a, 8, blk(a, dg, 1), sib),
                from_sib=lambda a=a: [copy(a, 0, blk(a, sib), me), copy(a, 5, blk(a, other(xn)), me),
                                      copy(a, 6, blk(a, other(yn)), me), copy(a, 7, blk(a, other(dg), 0), me),
                                      copy(a, 8, blk(a, other(dg), 1), me)])
        return cp

    def start(self, ins, outs, sems):
        cp = self._plan(ins, outs, sems)
        for c in cp["mine"]:
            c.start()
        for a in range(self.n):
            for c in cp[a]["own"]:
                c.start()

    def pass_on(self, ins, outs, sems):
        cp = self._plan(ins, outs, sems)
        for a in range(self.n):
            cp[a]["from_x"].wait_recv()
            cp[a]["relay_x"].start()
            cp[a]["pass_x"].start()
        for a in range(self.n):
            cp[a]["from_y"].wait_recv()
            cp[a]["relay_y"].start()
            cp[a]["pass_y"].start()

    def finish(self, ins, outs, sems):
        if self.pass_on_at is None:
            self.pass_on(ins, outs, sems)
        cp = self._plan(ins, outs, sems)
        for a in range(self.n):
            cp[a]["diag0"].wait_recv()
            cp[a]["pass_d0"].start()
            cp[a]["diag1"].wait_recv()
            cp[a]["pass_d1"].start()
        for a in range(self.n):
            for c in cp[a]["from_sib"]:
                c.wait_recv()
            for c in cp[a]["own"] + [cp[a][k] for k in ("relay_x", "relay_y", "pass_x", "pass_y", "pass_d0", "pass_d1")]:
                c.wait_send()
        for c in cp["mine"]:
            c.wait()


class _Scatter:
    def __init__(self, grads):
        self.grads = list(grads)
        self.n = len(self.grads)

    def inputs(self):
        return self.grads

    def out_shape(self):
        return [jax.ShapeDtypeStruct((N_DEV - 1,) + g.shape[1:], g.dtype) for g in self.grads]

    def scratch(self):
        return [pltpu.SemaphoreType.DMA((7 * self.n,)), pltpu.SemaphoreType.DMA((7 * self.n,))]

    def _plan(self, ins, outs, sems):
        send_sems, recv_sems = sems
        x, y, c, _ = _position()
        cps = []
        for a in range(self.n):
            for k in range(1, N_DEV):
                peer = (x ^ (k >> 2), y ^ ((k >> 1) & 1), c ^ (k & 1))
                cps.append(pltpu.make_async_remote_copy(
                    src_ref=ins[a].at[_slot(peer)], dst_ref=outs[a].at[k - 1],
                    send_sem=send_sems.at[7 * a + k - 1], recv_sem=recv_sems.at[7 * a + k - 1],
                    device_id=peer, device_id_type=MESH))
        return cps

    def start(self, ins, outs, sems):
        for cp in self._plan(ins, outs, sems):
            cp.start()

    def finish(self, ins, outs, sems):
        for cp in self._plan(ins, outs, sems):
            cp.wait()


def _hosted(inner, n_in, n_out, comm, grid):
    if comm is None:
        return inner
    nc_in, nc_out, ns = len(comm.inputs()), len(comm.out_shape()), len(comm.scratch())

    def body(*refs):
        o0 = n_in + nc_in
        s0 = o0 + n_out + nc_out
        main = refs[:n_in] + refs[o0:o0 + n_out] + refs[s0:len(refs) - ns]
        c_in, c_out, c_sems = refs[n_in:o0], refs[o0 + n_out:s0], refs[len(refs) - ns:]
        ids = [pl.program_id(ax) for ax in range(len(grid))]
        first = functools.reduce(jnp.logical_and, [i == 0 for i in ids])
        last = functools.reduce(jnp.logical_and, [i == g - 1 for i, g in zip(ids, grid)])

        @pl.when(first)
        def _():
            comm.start(c_in, c_out, c_sems)

        inner(*main)

        if getattr(comm, "pass_on_at", None) is not None:
            assert len(grid) == 1
            @pl.when(ids[0] == min(grid[0] - 1, int(comm.pass_on_at * grid[0])))
            def _():
                comm.pass_on(c_in, c_out, c_sems)

        @pl.when(last)
        def _():
            comm.finish(c_in, c_out, c_sems)

    return body


def _call(inner, name, grid, in_specs, out_specs, out_shape, scratch, args, comm=None):
    n_in, n_out = len(args), len(out_shape)
    body = _hosted(inner, n_in, n_out, comm, grid)
    if comm is not None:
        in_specs = list(in_specs) + [_any()] * len(comm.inputs())
        args = list(args) + comm.inputs()
        out_specs = list(out_specs) + [_any()] * len(comm.out_shape())
        out_shape = list(out_shape) + comm.out_shape()
        scratch = list(scratch) + comm.scratch()
    outs = pl.pallas_call(
        body, name=name, grid=grid, in_specs=list(in_specs), out_specs=list(out_specs), out_shape=list(out_shape),
        scratch_shapes=list(scratch), compiler_params=_params(len(grid)))(*args)
    return list(outs[:n_out]), list(outs[n_out:])


class _Bcast:
    def __init__(self, block):
        self.block = block

    def inputs(self):
        return [self.block]

    def out_shape(self):
        return [jax.ShapeDtypeStruct((N_DEV,) + self.block.shape, self.block.dtype)]

    def scratch(self):
        return [pltpu.SemaphoreType.DMA((N_DEV - 1,)), pltpu.SemaphoreType.DMA((N_DEV - 1,)),
                pltpu.SemaphoreType.DMA((1,))]

    def _plan(self, ins, outs, sems):
        send_sems, recv_sems, local_sem = sems
        x, y, c, _ = _position()
        mine = outs[0].at[_slot((x, y, c))]
        cps = []
        for k in range(1, N_DEV):
            peer = (x ^ (k >> 2), y ^ ((k >> 1) & 1), c ^ (k & 1))
            cps.append(pltpu.make_async_remote_copy(
                src_ref=ins[0], dst_ref=mine, send_sem=send_sems.at[k - 1], recv_sem=recv_sems.at[k - 1],
                device_id=peer, device_id_type=MESH))
        return pltpu.make_async_copy(ins[0], mine, local_sem.at[0]), cps

    def start(self, ins, outs, sems):
        own, cps = self._plan(ins, outs, sems)
        own.start()
        for cp in cps:
            cp.start()

    def finish(self, ins, outs, sems):
        own, cps = self._plan(ins, outs, sems)
        for cp in cps:
            cp.wait()
        own.wait()


def _first_gather(shards, small_idx, x2, t2, n_meta, tp):
    comm = _Gather(shards)
    n = comm.n
    seq, d = x2.shape
    t_real = n_meta + seq
    n_pad = tp - t_real
    cw = d // N_DEV
    rows = STAGE_ROWS if seq % STAGE_ROWS == 0 else seq
    n_chunks = seq // rows

    def body(*refs):
        ins, (x_ref, t_ref) = refs[:n], refs[n:n + 2]
        outs, (h0_ref, tg_ref) = refs[n + 2:2 * n + 2], refs[2 * n + 2:2 * n + 4]
        sems = refs[2 * n + 4:2 * n + 7]
        buf, zeros, in_sems, out_sems, misc_sems = refs[2 * n + 7:]
        comm.start(ins, outs, sems)
        zeros[...] = jnp.zeros_like(zeros)
        fills = [pltpu.make_async_copy(zeros.at[pl.ds(0, n_pad)], h0_ref.at[pl.ds(t_real, n_pad)], misc_sems.at[0]),
                 pltpu.make_async_copy(zeros.at[pl.ds(0, n_pad)], tg_ref.at[pl.ds(t_real, n_pad)], misc_sems.at[1]),
                 pltpu.make_async_copy(zeros.at[pl.ds(0, n_meta)], tg_ref.at[pl.ds(0, n_meta)], misc_sems.at[2])]
        for cp in fills:
            cp.start()
        jobs = [(src, dst, c) for src, dst in ((x_ref, h0_ref), (t_ref, tg_ref)) for c in range(n_chunks)]

        def load(k):
            src, _, c = jobs[k]
            return pltpu.make_async_copy(src.at[pl.ds(c * rows, rows)], buf.at[k % 2], in_sems.at[k % 2])

        def store(k):
            _, dst, c = jobs[k]
            return pltpu.make_async_copy(buf.at[k % 2], dst.at[pl.ds(n_meta + c * rows, rows)], out_sems.at[k % 2])

        load(0).start()
        for k in range(len(jobs)):
            load(k).wait()
            if k + 1 < len(jobs):
                if k >= 1:
                    store(k - 1).wait()
                load(k + 1).start()
            store(k).start()
        for k in range(max(0, len(jobs) - 2), len(jobs)):
            store(k).wait()
        comm.finish(ins, outs, sems)
        meta = [pltpu.make_async_copy(outs[small_idx].at[k, pl.ds(0, n_meta)],
                                      h0_ref.at[pl.ds(0, n_meta), pl.ds(k * cw, cw)], misc_sems.at[3 + k])
                for k in range(N_DEV)]
        for cp in meta:
            cp.start()
        for cp in fills + meta:
            cp.wait()

    staged = [jax.ShapeDtypeStruct((tp, d), F32)] * 2
    outs = pl.pallas_call(
        body, name="weights_all_gather", out_shape=comm.out_shape() + staged,
        in_specs=[_any()] * (n + 2), out_specs=[_any()] * (n + 2),
        scratch_shapes=comm.scratch() + [
            pltpu.VMEM((2, rows, d), F32), pltpu.VMEM((max(n_pad, n_meta), d), F32),
            pltpu.SemaphoreType.DMA((2,)), pltpu.SemaphoreType.DMA((2,)), pltpu.SemaphoreType.DMA((3 + N_DEV,))],
        compiler_params=pltpu.CompilerParams(vmem_limit_bytes=V7X_VMEM_LIMIT),
    )(*shards, x2, t2)
    return outs[:n], outs[n], outs[n + 1]


def _pair_exchange(grads):
    n = len(grads)

    def body(*refs):
        ins, outs = refs[:n], refs[n:2 * n]
        send_sems, recv_sems = refs[2 * n:]
        x, y, c, _ = _position()
        cps = [pltpu.make_async_remote_copy(
            src_ref=ins[a].at[:, 1 - c], dst_ref=outs[a],
            send_sem=send_sems.at[a], recv_sem=recv_sems.at[a],
            device_id=(x, y, 1 - c), device_id_type=MESH) for a in range(n)]
        for cp in cps:
            cp.start()
        for cp in cps:
            cp.wait()

    return pl.pallas_call(
        body, name="grads_pair_exchange",
        out_shape=[jax.ShapeDtypeStruct((4,) + g.shape[2:], g.dtype) for g in grads],
        in_specs=[_any()] * n, out_specs=[_any()] * n,
        scratch_shapes=[pltpu.SemaphoreType.DMA((n,)), pltpu.SemaphoreType.DMA((n,))],
    )(*grads)


def _chip_exchange(combs):
    n = len(combs)

    def body(*refs):
        ins, outs = refs[:n], refs[n:2 * n]
        send_sems, recv_sems = refs[2 * n:]
        x, y, c, chips = _position()
        cps = []
        for a in range(n):
            for j, (cx, cy) in enumerate(chips):
                cps.append(pltpu.make_async_remote_copy(
                    src_ref=ins[a].at[2 * cx + cy], dst_ref=outs[a].at[j],
                    send_sem=send_sems.at[3 * a + j], recv_sem=recv_sems.at[3 * a + j],
                    device_id=(cx, cy, c), device_id_type=MESH))
        for cp in cps:
            cp.start()
        for cp in cps:
            cp.wait()

    return pl.pallas_call(
        body, name="grads_chip_exchange",
        out_shape=[jax.ShapeDtypeStruct((3,) + g.shape[1:], g.dtype) for g in combs],
        in_specs=[_any()] * n, out_specs=[_any()] * n,
        scratch_shapes=[pltpu.SemaphoreType.DMA((3 * n,)), pltpu.SemaphoreType.DMA((3 * n,))],
    )(*combs)


def _pair_add(grad, recv, core):
    blk = grad.shape[2:]
    zeros = (0,) * len(blk)

    def body(core_ref, g_ref, r_ref, o_ref):
        del core_ref
        o_ref[...] = (g_ref[...].astype(F32) + r_ref[...].astype(F32)).astype(BF16)

    return pl.pallas_call(
        body, name="grads_pair_add",
        out_shape=jax.ShapeDtypeStruct((4,) + blk, BF16),
        grid_spec=pltpu.PrefetchScalarGridSpec(
            num_scalar_prefetch=1, grid=(4,),
            in_specs=[pl.BlockSpec((None, None) + blk, lambda i, cr: (i, cr[0]) + zeros),
                      pl.BlockSpec((None,) + blk, lambda i, cr: (i,) + zeros)],
            out_specs=pl.BlockSpec((None,) + blk, lambda i, cr: (i,) + zeros)),
        compiler_params=_params(1),
    )(core, grad, recv)


def _adamw(w, g, m, v):
    m2 = ADAM_B1 * m + (1.0 - ADAM_B1) * g
    v2 = ADAM_B2 * v + (1.0 - ADAM_B2) * (g * g)
    m_hat = m2 / (1.0 - ADAM_B1 ** ADAM_STEP)
    v_hat = v2 / (1.0 - ADAM_B2 ** ADAM_STEP)
    delta = -ADAM_LR * (m_hat / (jnp.sqrt(v_hat) + ADAM_EPS) + ADAM_WD * w)
    return delta, m2, v2


def _final_adamw(own, recv, idx, parts):
    blk = own.shape[1:]
    n_recv = recv.shape[0]
    n_parts = len(parts)
    per = blk[0] // n_parts if n_parts > 1 else None
    rows = blk[-2]
    n_chunks = 1 if n_parts > 1 else (4 if rows % 64 == 0 and rows >= 512 else (2 if rows % 32 == 0 else 1))
    cblk = blk[:-2] + (rows // n_chunks, blk[-1])
    lead = (0,) * (len(blk) - 2)

    def body(idx_ref, c_ref, r_ref, *refs):
        del idx_ref
        ins, outs = refs[:3 * n_parts], refs[3 * n_parts:]
        g = c_ref[...].astype(F32)
        for k in range(n_recv):
            g = g + r_ref[k].astype(F32)
        for p in range(n_parts):
            w_ref, m_ref, v_ref = ins[3 * p:3 * p + 3]
            if n_parts == 1:
                gp = g
            elif per == 1:
                gp = g[p]
            else:
                gp = g[p * per:(p + 1) * per]
            delta, m2, v2 = _adamw(w_ref[0], gp, m_ref[0], v_ref[0])
            o = outs[4 * p:4 * p + 4]
            o[0][0] = gp
            o[1][0] = delta
            o[2][0] = m2
            o[3][0] = v2

    flat = [a for wmv in parts for a in wmv]

    def part_spec(a):
        shape = a.shape[:-2] + (a.shape[-2] // n_chunks, a.shape[-1])
        return pl.BlockSpec(shape, lambda i, cr, nd=a.ndim: (0,) * (nd - 2) + (i, 0))

    outs = pl.pallas_call(
        body, name="grads_sum_adamw",
        out_shape=[jax.ShapeDtypeStruct(wmv[0].shape, F32) for wmv in parts for _ in range(4)],
        grid_spec=pltpu.PrefetchScalarGridSpec(
            num_scalar_prefetch=1, grid=(n_chunks,),
            in_specs=[pl.BlockSpec((None,) + cblk, lambda i, cr: (cr[0],) + lead + (i, 0)),
                      pl.BlockSpec((n_recv,) + cblk, lambda i, cr: (0,) + lead + (i, 0))]
                     + [part_spec(a) for a in flat],
            out_specs=[part_spec(wmv[0]) for wmv in parts for _ in range(4)]),
        compiler_params=_params(1),
    )(idx, own, recv, *flat)
    return [tuple(outs[4 * p:4 * p + 4]) for p in range(n_parts)]


def _small_adamw(partials, layout, me_index):
    _, rows, d = partials.shape
    n = len(layout)
    cw = d // N_DEV

    def body(me_ref, p_ref, *refs):
        ins, t_ref, outs = refs[:3 * n], refs[3 * n], refs[3 * n + 1:]
        me = me_ref[0]
        total = p_ref[0]
        for j in range(1, N_DEV):
            total = total + p_ref[j]
        t_ref[...] = total
        for e, (kind, r0, nr, _, _, _) in enumerate(layout):
            w_ref, m_ref, v_ref = ins[3 * e:3 * e + 3]
            o = outs[4 * e:4 * e + 4]
            if kind == "rep":
                g = t_ref[r0:r0 + nr, :]
                delta, m2, v2 = _adamw(w_ref[...], g, m_ref[...], v_ref[...])
                for ref, val in zip(o, (g, delta, m2, v2)):
                    ref[...] = val
            elif kind == "wide":
                for q in range(nr):
                    sl = slice(q * d, (q + 1) * d)
                    g = t_ref[r0 + q:r0 + q + 1, :]
                    delta, m2, v2 = _adamw(w_ref[:, sl], g, m_ref[:, sl], v_ref[:, sl])
                    for ref, val in zip(o, (g, delta, m2, v2)):
                        ref[:, sl] = val
            else:
                for j in range(N_DEV):
                    @pl.when(me == j)
                    def _(j=j, o=o, w_ref=w_ref, m_ref=m_ref, v_ref=v_ref, r0=r0, nr=nr):
                        g = t_ref[r0:r0 + nr, j * cw:(j + 1) * cw]
                        delta, m2, v2 = _adamw(w_ref[...], g, m_ref[...], v_ref[...])
                        for ref, val in zip(o, (g, delta, m2, v2)):
                            ref[...] = val

    flat = [a for ent in layout for a in ent[3:]]
    vm = pl.BlockSpec(memory_space=pltpu.VMEM)
    outs = pl.pallas_call(
        body, name="small_adamw",
        out_shape=[jax.ShapeDtypeStruct((rows, d), F32)]
                  + [jax.ShapeDtypeStruct(ent[3].shape, F32) for ent in layout for _ in range(4)],
        in_specs=[pl.BlockSpec(memory_space=pltpu.SMEM), vm] + [vm] * len(flat),
        out_specs=[vm] * (1 + 4 * n),
        compiler_params=pltpu.CompilerParams(vmem_limit_bytes=V7X_VMEM_LIMIT),
    )(me_index, partials, *flat)
    return outs[0], [tuple(outs[1 + 4 * e:5 + 4 * e]) for e in range(n)]


def _ffn_fwd(h, g, wgu, wd, tm, loss=None, comm=None):
    tp, d = h.shape
    f = wd.shape[0]
    fc = f // FFN_FWD_CHUNKS
    nt = tp // tm
    with_loss = loss is not None
    if with_loss:
        tgt, gf, n_meta, t_real = loss

    def body(*refs):
        if with_loss:
            (h_ref, g_ref, wgu_hbm, wd_hbm, tgt_ref, gf_ref, out_ref, gu_ref, n_ref, loss_ref, dgf_ref,
             wgu_v, wd_v, sems) = refs
        else:
            h_ref, g_ref, wgu_hbm, wd_hbm, out_ref, gu_ref, n_ref, wgu_v, wd_v, sems = refs
        i = pl.program_id(0)

        @pl.when(i == 0)
        def _():
            _load_weights([(wgu_hbm, wgu_v), (wd_hbm, wd_v)], sems)
            if with_loss:
                loss_ref[...] = jnp.zeros_like(loss_ref)
                dgf_ref[...] = jnp.zeros_like(dgf_ref)

        x = h_ref[...]
        n, _ = _rms_fwd(x, g_ref[...])
        nb = n.astype(BF16)
        n_ref[...] = nb
        acc = jnp.zeros((tm, d), F32)
        for j in range(FFN_FWD_CHUNKS):
            cols = slice(j * fc, (j + 1) * fc)
            gate = _nt(nb, wgu_v[pl.ds(j * fc, fc), :])
            up = _nt(nb, wgu_v[pl.ds(f + j * fc, fc), :])
            gu_ref[0, :, cols] = gate.astype(BF16)
            gu_ref[1, :, cols] = up.astype(BF16)
            act = (gate * _sigmoid(gate) * up).astype(BF16)
            acc = acc + _nn(act, wd_v[pl.ds(j * fc, fc), :])
        hn = x + FFN_RES * acc
        if not with_loss:
            out_ref[...] = hn
        else:
            gfv = gf_ref[...]
            r = lax.rsqrt(jnp.mean(hn * hn, axis=-1, keepdims=True) + EPS)
            xr = hn * r
            rows = i * tm + lax.broadcasted_iota(jnp.int32, (tm, 1), 0)
            mask = jnp.logical_and(rows >= n_meta, rows < t_real)
            diff = jnp.where(mask, xr * gfv - tgt_ref[...], 0.0)
            loss_ref[...] += jnp.zeros_like(loss_ref) + 0.5 * jnp.sum(diff * diff) / d
            dy = diff / d
            gy = dy * gfv
            out_ref[...] = r * (gy - xr * jnp.mean(gy * xr, axis=-1, keepdims=True))
            dgf_ref[...] += _rowsum(dy * xr)

    row = pl.BlockSpec((tm, d), lambda i: (i, 0))
    vec = pl.BlockSpec((1, d), lambda i: (0, 0))
    in_specs = [row, vec, _any(), _any()]
    out_shape = [jax.ShapeDtypeStruct((tp, d), F32), jax.ShapeDtypeStruct((2, tp, f), BF16),
                 jax.ShapeDtypeStruct((tp, d), BF16)]
    out_specs = [row, pl.BlockSpec((2, tm, f), lambda i: (0, i, 0)), row]
    args = [h, g, wgu, wd]
    if with_loss:
        in_specs += [row, vec]
        out_shape += [jax.ShapeDtypeStruct((1, d), F32), jax.ShapeDtypeStruct((1, d), F32)]
        out_specs += [vec, vec]
        args += [tgt, gf]
    return _call(body, "ffn_fwd_loss" if with_loss else "ffn_fwd", (nt,), in_specs, out_specs, out_shape,
                 [pltpu.VMEM((2 * f, d), BF16), pltpu.VMEM((f, d), BF16), pltpu.SemaphoreType.DMA((2,))],
                 args, comm)


def _ffn_bwd(dh, h, gu, g, wgu, wd, tm, comm=None):
    tp, d = h.shape
    f = wd.shape[0]
    fc = f // FFN_CHUNKS
    nt = tp // tm

    def body(dh_ref, h_ref, gu_ref, g_ref, wgu_hbm, wd_hbm,
             dhin_ref, dgu_ref, act_ref, df_ref, dg_ref, wgu_v, wd_v, dn_v, sems):
        i, j = pl.program_id(0), pl.program_id(1)

        @pl.when(jnp.logical_and(i == 0, j == 0))
        def _():
            _load_weights([(wgu_hbm, wgu_v), (wd_hbm, wd_v)], sems)
            dg_ref[...] = jnp.zeros_like(dg_ref)

        dfb = (FFN_RES * dh_ref[...]).astype(BF16)

        @pl.when(j == 0)
        def _():
            df_ref[...] = dfb
            dn_v[...] = jnp.zeros_like(dn_v)

        lo = pl.multiple_of(j * fc, 16)
        dact = _nt(dfb, wd_v[pl.ds(lo, fc), :])
        gate = gu_ref[0].astype(F32)
        up = gu_ref[1].astype(F32)
        sg = _sigmoid(gate)
        silu = gate * sg
        act_ref[...] = (silu * up).astype(BF16)
        dgate = (dact * up * (sg * (1.0 + gate * (1.0 - sg)))).astype(BF16)
        dup = (dact * silu).astype(BF16)
        dgu_ref[0] = dgate
        dgu_ref[1] = dup
        dn_v[...] += _nn(dgate, wgu_v[pl.ds(lo, fc), :]) + _nn(dup, wgu_v[pl.ds(pl.multiple_of(f + j * fc, 16), fc), :])

        @pl.when(j == FFN_CHUNKS - 1)
        def _():
            x = h_ref[...]
            r = lax.rsqrt(jnp.mean(x * x, axis=-1, keepdims=True) + EPS)
            dx, dgp = _rms_bwd(dn_v[...], x, r, g_ref[...])
            dhin_ref[...] = dh_ref[...] + dx
            dg_ref[...] += dgp

    row = pl.BlockSpec((tm, d), lambda i, j: (i, 0))
    vec = pl.BlockSpec((1, d), lambda i, j: (0, 0))
    hid2 = pl.BlockSpec((2, tm, fc), lambda i, j: (0, i, j))
    return _call(
        body, "ffn_bwd", (nt, FFN_CHUNKS),
        [row, row, hid2, vec, _any(), _any()],
        [row, hid2, pl.BlockSpec((tm, fc), lambda i, j: (i, j)), row, vec],
        [jax.ShapeDtypeStruct((tp, d), F32), jax.ShapeDtypeStruct((2, tp, f), BF16),
         jax.ShapeDtypeStruct((tp, f), BF16), jax.ShapeDtypeStruct((tp, d), BF16),
         jax.ShapeDtypeStruct((1, d), F32)],
        [pltpu.VMEM((2 * f, d), BF16), pltpu.VMEM((f, d), BF16), pltpu.VMEM((tm, d), F32),
         pltpu.SemaphoreType.DMA((2,))],
        [dh, h, gu, g, wgu, wd], comm)


def _piece_segments(q, d, nb_cols):
    segs = []
    for j in range(N_DEV):
        lo, hi = max(q * d, j * nb_cols), min((q + 1) * d, (j + 1) * nb_cols)
        if lo < hi:
            segs.append((j, lo - q * d, hi - q * d, lo - j * nb_cols, hi - j * nb_cols))
    return segs


def _w3_copies(w3_hbm, rows, w3_v):
    return [(w3_hbm.at[k, pl.ds(q * rows, rows)], w3_v.at[q, pl.ds(k * rows, rows)])
            for q in range(3) for k in range(N_DEV)]


def _gates(xrb, wg_ref, ba, bx, lam, hd):
    pre_r, pre_i = [], []
    for hh in range(N_HEADS):
        xh = xrb[:, hh * hd:(hh + 1) * hd]
        pre_r.append(_nn(xh, wg_ref[0, hh]))
        pre_i.append(_nn(xh, wg_ref[1, hh]))
    r = _sigmoid(jnp.concatenate(pre_r, axis=1) + ba)
    ig = _sigmoid(jnp.concatenate(pre_i, axis=1) + bx)
    sp = _softplus(-lam)
    log_a = -RG_LRU_C * r * sp
    a = jnp.exp(log_a)
    s = jnp.sqrt(_one_minus_exp(2.0 * log_a))
    return r, ig, sp, a, s


def _scan_fwd(a, u, h_prev):
    tm = a.shape[0]
    rows = lax.broadcasted_iota(jnp.int32, a.shape, 0)
    d = 1
    while d < tm:
        if d < SUBLANES:
            keep = rows >= d
            u = jnp.where(keep, a * pltpu.roll(u, d, 0) + u, u)
            a = jnp.where(keep, a * pltpu.roll(a, d, 0), a)
        else:
            u = jnp.concatenate([u[:d], a[d:] * u[:tm - d] + u[d:]], axis=0)
            a = jnp.concatenate([a[:d], a[d:] * a[:tm - d]], axis=0)
        d *= 2
    return u + a * h_prev


def _scan_bwd(b, v, g_next):
    tm = b.shape[0]
    rows = lax.broadcasted_iota(jnp.int32, b.shape, 0)
    d = 1
    while d < tm:
        if d < SUBLANES:
            keep = rows < tm - d
            v = jnp.where(keep, v + b * pltpu.roll(v, tm - d, 0), v)
            b = jnp.where(keep, b * pltpu.roll(b, tm - d, 0), b)
        else:
            v = jnp.concatenate([v[:tm - d] + b[:tm - d] * v[d:], v[tm - d:]], axis=0)
            b = jnp.concatenate([b[:tm - d] * b[d:], b[tm - d:]], axis=0)
        d *= 2
    return v + b * g_next


def _shifted_copies(ext_ref, es_ref, n_rows):
    for s in range(1, SUBLANES):
        es_ref[s, pl.ds(0, n_rows), :] = ext_ref[pl.ds(s, n_rows), :]


def _tap(ext_ref, es_ref, off, tm):
    q, s = divmod(off, SUBLANES)
    if s == 0:
        return ext_ref[pl.ds(SUBLANES * q, tm), :]
    return es_ref[s, pl.ds(SUBLANES * q, tm), :]


def _mixer_fwd(h, g, b_in, win_all, cw4, cb4, wg, ba, bx, lam, cw31, cb31, lng, lnb, bcp, w3_all, tm, comm=None):
    tp, d = h.shape
    nb_cols = win_all.shape[-1]
    n_in = N_DEV * nb_cols
    hd = wg.shape[-1]
    k4, k31 = cw4.shape[0], cw31.shape[0]
    w3_rows = d // N_DEV

    def body(h_ref, g_ref, b_ref, win_hbm, cw4_ref, cb4_ref, wg_ref, ba_ref, bx_ref, lam_ref, cw31_ref, cb31_ref,
             lng_ref, lnb_ref, bcp_ref, w3_hbm,
             h2_ref, p_ref, n_ref, xr_ref, hs_ref, v1_ref, ya_ref, yb_ref,
             win_v, w3_v, ext4, ext31, es31, hcar, sems):
        @pl.when(pl.program_id(0) == 0)
        def _():
            _load_weights([(win_hbm, win_v)] + _w3_copies(w3_hbm, w3_rows, w3_v), sems)
            ext4[pl.ds(0, CONV4_HALO), :] = jnp.zeros((CONV4_HALO, d), F32)
            ext31[pl.ds(0, CONV31_HALO), :] = jnp.zeros((CONV31_HALO, d), F32)
            hcar[...] = jnp.zeros_like(hcar)

        n, _ = _rms_fwd(h_ref[...], g_ref[...])
        nb = n.astype(BF16)
        n_ref[...] = nb

        def piece(q):
            parts = [_nn(nb, win_v[j, :, bl:bh]) for j, _, _, bl, bh in _piece_segments(q, d, nb_cols)]
            pq = (jnp.concatenate(parts, axis=1) + b_ref[:, q * d:(q + 1) * d]).astype(BF16)
            p_ref[:, q * d:(q + 1) * d] = pq
            return pq.astype(F32)

        x_rnn, y_rnn, glu_v, glu_g, gate_a, gate_b = [piece(q) for q in range(6)]

        ext4[pl.ds(CONV4_HALO, tm), :] = x_rnn
        xr = cb4_ref[...] + jnp.zeros((tm, d), F32)
        for k in range(k4):
            xr = xr + cw4_ref[k:k + 1, :] * ext4[pl.ds(CONV4_HALO - (k4 - 1) + k, tm), :]
        ext4[pl.ds(0, CONV4_HALO), :] = ext4[pl.ds(tm, CONV4_HALO), :]
        xrb = xr.astype(BF16)
        xr_ref[...] = xrb
        xr = xrb.astype(F32)
        _, ig, _, a, s = _gates(xrb, wg_ref, ba_ref[...], bx_ref[...], lam_ref[...], hd)
        hseq = _scan_fwd(a, s * (ig * xr), hcar[0:1, :])
        hcar[0:1, :] = hseq[tm - 1:tm, :]
        hs_ref[...] = hseq.astype(BF16)
        gl, _ = _gelu(y_rnn)
        ya = _nn((hseq * gl).astype(BF16), w3_v[0])
        ya_ref[...] = ya.astype(BF16)

        ext31[pl.ds(CONV31_HALO, tm), :] = glu_v * _sigmoid(glu_g)
        _shifted_copies(ext31, es31, tm + CONV31_HALO - SUBLANES)
        v1 = cb31_ref[...] + jnp.zeros((tm, d), F32)
        for k in range(k31):
            v1 = v1 + cw31_ref[k:k + 1, :] * _tap(ext31, es31, CONV31_HALO - (k31 - 1) + k, tm)
        ext31[pl.ds(0, CONV31_HALO), :] = ext31[pl.ds(tm, CONV31_HALO), :]
        v1b = v1.astype(BF16)
        v1_ref[...] = v1b
        v1 = v1b.astype(F32)
        xc = v1 - jnp.mean(v1, axis=-1, keepdims=True)
        rstd = lax.rsqrt(jnp.mean(xc * xc, axis=-1, keepdims=True) + EPS)
        v2 = xc * rstd * lng_ref[...] + lnb_ref[...]
        yb = _nn((v2 * _sigmoid(v2)).astype(BF16), w3_v[1]) + bcp_ref[...]
        yb_ref[...] = yb.astype(BF16)

        merged = _sigmoid(gate_a) * ya + _sigmoid(gate_b) * yb
        h2_ref[...] = h_ref[...] + _nn(merged.astype(BF16), w3_v[2])

    row = pl.BlockSpec((tm, d), lambda i: (i, 0))
    wide = pl.BlockSpec((tm, n_in), lambda i: (i, 0))
    full = lambda a: pl.BlockSpec(a.shape, lambda i, nd=a.ndim: (0,) * nd)
    smalls = [cw4, cb4, wg, ba, bx, lam, cw31, cb31, lng, lnb, bcp]
    return _call(
        body, "mixer_fwd", (tp // tm,),
        [row, full(g), full(b_in), _any()] + [full(a) for a in smalls] + [_any()],
        [row, wide] + [row] * 6,
        [jax.ShapeDtypeStruct((tp, d), F32), jax.ShapeDtypeStruct((tp, n_in), BF16)]
        + [jax.ShapeDtypeStruct((tp, d), BF16)] * 6,
        [pltpu.VMEM(win_all.shape, BF16),
         pltpu.VMEM((3, d, d), BF16),
         pltpu.VMEM((tm + CONV4_HALO, d), F32),
         pltpu.VMEM((tm + CONV31_HALO, d), F32),
         pltpu.VMEM((SUBLANES, tm + CONV31_HALO, d), F32),
         pltpu.VMEM((SUBLANES, d), F32),
         pltpu.SemaphoreType.DMA((1 + 3 * N_DEV,))],
        [h, g, b_in, win_all, *smalls, w3_all], comm)


SG_BIN, SG_CW4, SG_CB4, SG_BA, SG_BX, SG_LAM, SG_CB31, SG_LNG, SG_LNB, SG_BCP, SG_MIX, SG_CW31 = 0, 6, 10, 11, 12, 13, 14, 15, 16, 17, 18, 19


def _mixer_bwd(dh2, h, g, proj, xr_s, hs_s, v1_s, ya_s, yb_s, win_t, cw4, wg, ba, bx, lam, cw31, lng, lnb, w3_all, tm,
               comm=None):
    tp, d = dh2.shape
    n_in = proj.shape[1]
    hd = wg.shape[-1]
    k4, k31 = cw4.shape[0], cw31.shape[0]
    nt = tp // tm
    w3_rows = d // N_DEV
    sg_rows = -(-(SG_CW31 + k31) // SUBLANES) * SUBLANES
    halo_rows = 16
    per = tm // halo_rows

    def body(dh_ref, h_ref, g_ref, p_ref, xr_ref, hs_ref, hh_ref, v1_ref, ya_ref, yb_ref, win_hbm,
             cw4_ref, wg_ref, wgt_ref, ba_ref, bx_ref, lam_ref, cw31_ref, lng_ref, lnb_ref, w3_hbm,
             dh1_ref, dp_ref, x3_ref, y3_ref, yg_ref, sg_ref,
             win_v, w3_v, extd4, extd31, es31, gcar, sems):
        i = pl.program_id(0)
        tile = nt - 1 - i

        @pl.when(i == 0)
        def _():
            _load_weights([(win_hbm, win_v)] + _w3_copies(w3_hbm, w3_rows, w3_v), sems)
            for q in range(3):
                w3_v[q] = w3_v[q].T
            extd4[pl.ds(tm, CONV4_HALO), :] = jnp.zeros((CONV4_HALO, d), F32)
            extd31[pl.ds(tm, CONV31_HALO), :] = jnp.zeros((CONV31_HALO, d), F32)
            gcar[...] = jnp.zeros_like(gcar)
            sg_ref[...] = jnp.zeros_like(sg_ref)

        def acc(row, val):
            sg_ref[row:row + 1, :] += _rowsum(val)

        rows = lax.broadcasted_iota(jnp.int32, (tm, d), 0)
        x_rnn = p_ref[:, 0:d].astype(F32)
        y_rnn = p_ref[:, d:2 * d].astype(F32)
        glu_v = p_ref[:, 2 * d:3 * d].astype(F32)
        glu_g = p_ref[:, 3 * d:4 * d].astype(F32)
        sga = _sigmoid(p_ref[:, 4 * d:5 * d].astype(F32))
        sgb = _sigmoid(p_ref[:, 5 * d:6 * d].astype(F32))
        ya = ya_ref[...].astype(F32)
        yb = yb_ref[...].astype(F32)

        dmob = dh_ref[...].astype(BF16)
        dmerged = _nn(dmob, w3_v[2])
        x3_ref[:, 0:d] = (sga * ya + sgb * yb).astype(BF16)
        y3_ref[:, 0:d] = dmob
        dya = sga * dmerged
        dyb = sgb * dmerged
        dn_parts = []

        def emit(q, val):
            vb = val.astype(BF16)
            dp_ref[:, q * d:(q + 1) * d] = vb
            acc(SG_BIN + q, val)
            term = _nn(vb, win_v[pl.ds(q * d, d), :])
            dn_parts[:] = [term if not dn_parts else dn_parts[0] + term]

        emit(4, dmerged * ya * sga * (1.0 - sga))
        emit(5, dmerged * yb * sgb * (1.0 - sgb))

        dyab = dya.astype(BF16)
        y3_ref[:, d:2 * d] = dyab
        dza = _nn(dyab, w3_v[0])
        hsv = hs_ref[...].astype(F32)
        gl, th = _gelu(y_rnn)
        x3_ref[:, d:2 * d] = (hsv * gl).astype(BF16)
        emit(1, dza * hsv * _gelu_grad(y_rnn, th))
        dhs = dza * gl
        xrb = xr_ref[...]
        xr = xrb.astype(F32)
        lam_v = lam_ref[...]
        r, ig, sp, a, s = _gates(xrb, wg_ref, ba_ref[...], bx_ref[...], lam_v, hd)
        b = jnp.where(rows == tm - 1, gcar[1:2, :], pltpu.roll(a, tm - 1, 0))
        big_g = _scan_bwd(b, dhs, gcar[0:1, :])
        gcar[0:1, :] = big_g[0:1, :]
        gcar[1:2, :] = a[0:1, :]
        h_before = jnp.where(tile > 0, hh_ref[halo_rows - 1:halo_rows, :].astype(F32), 0.0)
        h_prev = jnp.where(rows == 0, h_before, pltpu.roll(hsv, 1, 0))
        ds = big_g * ig * xr
        dla = big_g * h_prev * a - ds * (a * a) / jnp.maximum(s, 1e-20)
        acc(SG_LAM, dla * r * (RG_LRU_C * _sigmoid(-lam_v)))
        dpr = dla * (-RG_LRU_C * sp) * r * (1.0 - r)
        dpi = big_g * s * xr * ig * (1.0 - ig)
        acc(SG_BA, dpr)
        acc(SG_BX, dpi)
        dprb = dpr.astype(BF16)
        dpib = dpi.astype(BF16)
        yg_ref[:, 0:d] = dprb
        yg_ref[:, d:2 * d] = dpib
        back = []
        for hh in range(N_HEADS):
            sl = slice(hh * hd, (hh + 1) * hd)
            back.append(_nn(dprb[:, sl], wgt_ref[0, hh]) + _nn(dpib[:, sl], wgt_ref[1, hh]))
        dxr = big_g * s * ig + jnp.concatenate(back, axis=1)
        acc(SG_CB4, dxr)
        extd4[pl.ds(0, tm), :] = dxr
        dx_rnn = jnp.zeros((tm, d), F32)
        for k in range(k4):
            term = extd4[pl.ds(k4 - 1 - k, tm), :]
            dx_rnn = dx_rnn + cw4_ref[k:k + 1, :] * term
            acc(SG_CW4 + k, x_rnn * term)
        extd4[pl.ds(tm, CONV4_HALO), :] = extd4[pl.ds(0, CONV4_HALO), :]
        emit(0, dx_rnn)

        dybb = dyb.astype(BF16)
        y3_ref[:, 2 * d:3 * d] = dybb
        acc(SG_BCP, dyb)
        dv3 = _nn(dybb, w3_v[1])
        v1 = v1_ref[...].astype(F32)
        xc = v1 - jnp.mean(v1, axis=-1, keepdims=True)
        rstd = lax.rsqrt(jnp.mean(xc * xc, axis=-1, keepdims=True) + EPS)
        xhat = xc * rstd
        lng_v = lng_ref[...]
        v2 = xhat * lng_v + lnb_ref[...]
        s2 = _sigmoid(v2)
        x3_ref[:, 2 * d:3 * d] = (v2 * s2).astype(BF16)
        dv2 = dv3 * (s2 * (1.0 + v2 * (1.0 - s2)))
        acc(SG_LNG, dv2 * xhat)
        acc(SG_LNB, dv2)
        dxh = dv2 * lng_v
        dv1 = rstd * (dxh - jnp.mean(dxh, axis=-1, keepdims=True)
                      - xhat * jnp.mean(dxh * xhat, axis=-1, keepdims=True))
        acc(SG_CB31, dv1)
        extd31[pl.ds(0, tm), :] = dv1
        _shifted_copies(extd31, es31, tm + CONV31_HALO - SUBLANES)
        sgg = _sigmoid(glu_g)
        v0 = glu_v * sgg
        dv0 = jnp.zeros((tm, d), F32)
        for k in range(k31):
            term = _tap(extd31, es31, k31 - 1 - k, tm)
            dv0 = dv0 + cw31_ref[k:k + 1, :] * term
            acc(SG_CW31 + k, v0 * term)
        extd31[pl.ds(tm, CONV31_HALO), :] = extd31[pl.ds(0, CONV31_HALO), :]
        emit(2, dv0 * sgg)
        emit(3, dv0 * glu_v * sgg * (1.0 - sgg))

        dn = dn_parts[0]
        x = h_ref[...]
        rr = lax.rsqrt(jnp.mean(x * x, axis=-1, keepdims=True) + EPS)
        dx, dgp = _rms_bwd(dn, x, rr, g_ref[...])
        dh1_ref[...] = dh_ref[...] + dx
        sg_ref[SG_MIX:SG_MIX + 1, :] += dgp

    rev = lambda i: (nt - 1 - i, 0)
    row = pl.BlockSpec((tm, d), rev)
    wide = pl.BlockSpec((tm, n_in), rev)
    full = lambda a: pl.BlockSpec(a.shape, lambda i, nd=a.ndim: (0,) * nd)
    halo = pl.BlockSpec((halo_rows, d), lambda i: (jnp.maximum((nt - 1 - i) * per - 1, 0), 0))
    smalls = [cw4, wg, jnp.swapaxes(wg, 2, 3), ba, bx, lam, cw31, lng, lnb]
    return _call(
        body, "mixer_bwd", (nt,),
        [row, row, full(g), wide, row, row, halo, row, row, row, _any()]
        + [full(a) for a in smalls] + [_any()],
        [row, wide, pl.BlockSpec((tm, 3 * d), rev), pl.BlockSpec((tm, 3 * d), rev),
         pl.BlockSpec((tm, 2 * d), rev), pl.BlockSpec((sg_rows, d), lambda i: (0, 0))],
        [jax.ShapeDtypeStruct((tp, d), F32), jax.ShapeDtypeStruct((tp, n_in), BF16),
         jax.ShapeDtypeStruct((tp, 3 * d), BF16), jax.ShapeDtypeStruct((tp, 3 * d), BF16),
         jax.ShapeDtypeStruct((tp, 2 * d), BF16), jax.ShapeDtypeStruct((sg_rows, d), F32)],
        [pltpu.VMEM(win_t.shape, BF16),
         pltpu.VMEM((3, d, d), BF16),
         pltpu.VMEM((tm + CONV4_HALO, d), F32),
         pltpu.VMEM((tm + CONV31_HALO, d), F32),
         pltpu.VMEM((SUBLANES, tm + CONV31_HALO, d), F32),
         pltpu.VMEM((SUBLANES, d), F32),
         pltpu.SemaphoreType.DMA((1 + 3 * N_DEV,))],
        [dh2, h, g, proj, xr_s, hs_s, hs_s, v1_s, ya_s, yb_s, win_t, *smalls, w3_all], comm)


def _tn_matmul(name, x, y, x_spec, y_spec, n_blocks, kb, nb, tm, tp, out_shape, out_spec, out_view, comm=None,
               rider=None):
    nt = tp // tm
    n_steps = n_blocks * nt
    if rider is not None:
        own, recv, wmv, n_chunks = rider
        assert n_chunks <= n_steps and own.shape[0] % n_chunks == 0
        n_recv, (rows, cols) = recv.shape[0], own.shape
        crows = rows // n_chunks

    def body(*refs):
        if rider is None:
            x_ref, y_ref, o_ref, acc = refs
        else:
            x_ref, y_ref, own_ref, recv_ref, w_ref, m_ref, v_ref, o_ref, g_out, d_out, m_out, v_out, acc = refs
        b, i = pl.program_id(0), pl.program_id(1)

        @pl.when(i == 0)
        def _():
            acc[...] = jnp.zeros_like(acc)

        acc[...] += _tn(x_ref[...], y_ref[...])

        @pl.when(i == nt - 1)
        def _():
            o_ref[...] = acc[...].astype(BF16).reshape(out_view)

        if rider is not None:
            @pl.when(b * nt + i < n_chunks)
            def _():
                g = own_ref[...].astype(F32)
                for k in range(n_recv):
                    g = g + recv_ref[k].astype(F32)
                delta, m2, v2 = _adamw(w_ref[0], g, m_ref[0], v_ref[0])
                g_out[0] = g
                d_out[0] = delta
                m_out[0] = m2
                v_out[0] = v2

    in_specs, out_specs, args = [x_spec, y_spec], [out_spec], [x, y]
    out_shapes = [jax.ShapeDtypeStruct(out_shape, BF16)]
    if rider is not None:
        chunk = lambda b, i: jnp.minimum(b * nt + i, n_chunks - 1)
        part = pl.BlockSpec((1, crows, cols), lambda b, i: (0, chunk(b, i), 0))
        in_specs += [pl.BlockSpec((crows, cols), lambda b, i: (chunk(b, i), 0)),
                     pl.BlockSpec((n_recv, crows, cols), lambda b, i: (0, chunk(b, i), 0)), part, part, part]
        out_specs += [part] * 4
        out_shapes += [jax.ShapeDtypeStruct(wmv[0].shape, F32)] * 4
        args += [own, recv, *wmv]
    outs, extra = _call(body, name, (n_blocks, nt), in_specs, out_specs, out_shapes,
                        [pltpu.VMEM((kb, nb), F32)], args, comm)
    return outs[0], extra, tuple(outs[1:])


def kernel(x, meta_tokens, ffn1_norm, ffn1_w_gu, ffn1_w_down, mix_norm, w_in, b_in, rnn_conv_w, rnn_conv_b, rg_w_a, rg_b_a, rg_w_x, rg_b_x, rg_lambda, rnn_w_proj, conv_dw_w, conv_dw_b, conv_ln_g, conv_ln_b, conv_w_proj, conv_b_proj, w_out, ffn2_norm, ffn2_w_gu, ffn2_w_down, final_norm, loss_target, m_meta_tokens, m_ffn1_norm, m_ffn1_w_gu, m_ffn1_w_down, m_mix_norm, m_w_in, m_b_in, m_rnn_conv_w, m_rnn_conv_b, m_rg_w_a, m_rg_b_a, m_rg_w_x, m_rg_b_x, m_rg_lambda, m_rnn_w_proj, m_conv_dw_w, m_conv_dw_b, m_conv_ln_g, m_conv_ln_b, m_conv_w_proj, m_conv_b_proj, m_w_out, m_ffn2_norm, m_ffn2_w_gu, m_ffn2_w_down, m_final_norm, v_meta_tokens, v_ffn1_norm, v_ffn1_w_gu, v_ffn1_w_down, v_mix_norm, v_w_in, v_b_in, v_rnn_conv_w, v_rnn_conv_b, v_rg_w_a, v_rg_b_a, v_rg_w_x, v_rg_b_x, v_rg_lambda, v_rnn_w_proj, v_conv_dw_w, v_conv_dw_b, v_conv_ln_g, v_conv_ln_b, v_conv_w_proj, v_conv_b_proj, v_w_out, v_ffn2_norm, v_ffn2_w_gu, v_ffn2_w_down, v_final_norm):
    w = dict(locals())
    seq, d = x.shape[1], x.shape[2]
    n_meta = meta_tokens.shape[0]
    t_real = n_meta + seq
    tp, tm, tmx_fwd, tmx, tmt, tmw = _tiles(t_real)
    fb = ffn1_w_gu.shape[-1]
    wr = ffn1_w_down.shape[1]
    f = N_DEV * wr
    fc = f // FFN_CHUNKS
    nbc = w_in.shape[-1]
    n_in = N_DEV * nbc
    pr = rnn_w_proj.shape[1]
    hd = rg_w_a.shape[-1]
    gr = rg_w_a.shape[2]
    cw = meta_tokens.shape[1]
    k4, k31 = rnn_conv_w.shape[1], conv_dw_w.shape[1]
    assert n_in == 6 * d and 2 * wr == fb and N_HEADS * hd == d and pr * N_DEV == d

    xi, yi, ci = lax.axis_index("x"), lax.axis_index("y"), lax.axis_index("c")
    core = ci.astype(jnp.int32).reshape(1)
    chip = (2 * xi + yi).astype(jnp.int32).reshape(1)
    me_index = (4 * xi + 2 * yi + ci).astype(jnp.int32).reshape(1)

    for nm in ("ffn1_w_gu", "ffn2_w_gu"):
        for pre in ("", "m_", "v_"):
            w[pre + nm] = jnp.swapaxes(w[pre + nm], 1, 2)

    wgut1 = w["ffn1_w_gu"][0].astype(BF16)
    wgut2 = w["ffn2_w_gu"][0].astype(BF16)
    wd1 = ffn1_w_down[0].astype(BF16)
    wd2 = ffn2_w_down[0].astype(BF16)
    win_loc = w_in[0].astype(BF16)
    win_t_loc = jnp.swapaxes(w_in[0], 0, 1).astype(BF16)
    w3_loc = jnp.concatenate([rnn_w_proj[0], conv_w_proj[0], w_out[0]], axis=0).astype(BF16)
    wg_loc = jnp.stack([rg_w_a[0], rg_w_x[0]]).astype(BF16)
    n_small = n_meta + k4 + k31
    small_rows = -(-n_small // SUBLANES) * SUBLANES
    small_loc = jnp.concatenate([meta_tokens, rnn_conv_w[0], conv_dw_w[0],
                                 jnp.zeros((small_rows - n_small, cw), F32)], axis=0)
    (wgut1_all, wd1_all, wg_all, small_all), h0, tgt = _first_gather(
        [wgut1, wd1, wg_loc, small_loc], 3, x[0], loss_target[0], n_meta, tp)
    wg = wg_all.transpose(1, 2, 0, 3, 4).reshape(2, N_HEADS, hd, hd)
    small_full = small_all.transpose(1, 0, 2).reshape(small_rows, d)
    cw4 = small_full[n_meta:n_meta + k4]
    cw31 = small_full[n_meta + k4:n_meta + k4 + k31]

    wgu1, wdn1 = wgut1_all.reshape(2 * f, d), wd1_all.reshape(f, d)
    (h1, gu1, n1), (win_all, w3_all) = _ffn_fwd(h0, ffn1_norm, wgu1, wdn1, tm,
                                                comm=_Gather([win_loc, w3_loc], pass_on_at=0.55))
    (h2, proj, n2, xr_s, hs_s, v1_s, ya_s, yb_s), (wgut2_all, wd2_all) = _mixer_fwd(
        h1, mix_norm, b_in, win_all, cw4, rnn_conv_b, wg, rg_b_a, rg_b_x, rg_lambda, cw31, conv_dw_b, conv_ln_g,
        conv_ln_b, conv_b_proj, w3_all, tmx_fwd, comm=_Gather([wgut2, wd2], pass_on_at=0.6))
    wgu2, wdn2 = wgut2_all.reshape(2 * f, d), wd2_all.reshape(f, d)
    (dh3, gu2, n3, loss_part, dgf), (win_t_all,) = _ffn_fwd(
        h2, ffn2_norm, wgu2, wdn2, tm, loss=(tgt, final_norm.reshape(1, d), n_meta, t_real),
        comm=_Gather([win_t_loc], pass_on_at=0.7))
    win_t = win_t_all.reshape(n_in, d)

    def d_w_gu(tag, dgu, n_s, comm=None, rider=None):
        g, extra, rode = _tn_matmul(
            "d_w_gu" + tag, dgu, n_s,
            pl.BlockSpec((None, tmt, fc), lambda b, i: (b // FFN_CHUNKS, i, b % FFN_CHUNKS)),
            pl.BlockSpec((tmt, d), lambda b, i: (i, 0)),
            2 * FFN_CHUNKS, fc, d, tmt, tp, (2 * FFN_CHUNKS, fc, d),
            pl.BlockSpec((None, fc, d), lambda b, i: (b, 0, 0)), (fc, d), comm, rider)
        return g.reshape(N_DEV, fb, d), extra, rode

    def d_w_down(tag, act, df, comm=None):
        g, extra, _ = _tn_matmul(
            "d_w_down" + tag, act, df,
            pl.BlockSpec((tmt, fc), lambda b, i: (i, b)), pl.BlockSpec((tmt, d), lambda b, i: (i, 0)),
            FFN_CHUNKS, fc, d, tmt, tp, (FFN_CHUNKS, fc, d),
            pl.BlockSpec((None, fc, d), lambda b, i: (b, 0, 0)), (fc, d), comm)
        return g.reshape(N_DEV, wr, d), extra

    (dh2, dgu2, act2, df2, dg_ffn2), _ = _ffn_bwd(dh3, h2, gu2, ffn2_norm, wgu2, wdn2, tm)
    g_wgu2, _, _ = d_w_gu("2", dgu2, n3)
    g_wd2, _ = d_w_down("2", act2, df2)
    (dh1, dproj, x3, y3, yg, sg), (r_wd2, r_wgu2) = _mixer_bwd(
        dh2, h1, mix_norm, proj, xr_s, hs_s, v1_s, ya_s, yb_s, win_t, cw4, wg, rg_b_a, rg_b_x, rg_lambda, cw31,
        conv_ln_g, conv_ln_b, w3_all, tmx, comm=_Scatter([g_wd2, g_wgu2]))
    res = {}

    def rider_for(nm, g_all, r_all, n_chunks):
        own = lax.dynamic_index_in_dim(g_all, me_index[0], 0, keepdims=False)
        return own, r_all, (w[nm], w["m_" + nm], w["v_" + nm]), n_chunks

    g_w3, _, _ = _tn_matmul(
        "d_w_proj3", x3, y3,
        pl.BlockSpec((tmw, d), lambda b, i: (i, b)), pl.BlockSpec((tmw, d), lambda b, i: (i, b)),
        3, d, d, tmw, tp, (N_DEV, 3, pr, d), pl.BlockSpec((N_DEV, None, pr, d), lambda b, i: (0, b, 0, 0)),
        (N_DEV, pr, d))
    g_wg, _, res["ffn2_w_down"] = _tn_matmul(
        "d_w_gates", xr_s, yg,
        pl.BlockSpec((tmw, hd), lambda b, i: (i, b % N_HEADS)), pl.BlockSpec((tmw, hd), lambda b, i: (i, b)),
        2 * N_HEADS, hd, hd, tmw, tp, (N_DEV, 2 * N_HEADS, gr, hd),
        pl.BlockSpec((N_DEV, None, gr, hd), lambda b, i: (0, b, 0, 0)), (N_DEV, gr, hd),
        rider=rider_for("ffn2_w_down", g_wd2, r_wd2, 2))
    g_win, (r_w3, r_wg), res["ffn2_w_gu"] = _tn_matmul(
        "d_w_in", n2, dproj,
        pl.BlockSpec((tmw, d), lambda b, i: (i, 0)), pl.BlockSpec((tmw, nbc), lambda b, i: (i, b)),
        N_DEV, d, nbc, tmw, tp, (N_DEV, d, nbc), pl.BlockSpec((None, d, nbc), lambda b, i: (b, 0, 0)), (d, nbc),
        comm=_Scatter([g_w3, g_wg]), rider=rider_for("ffn2_w_gu", g_wgu2, r_wgu2, 4))
    dg_mix = sg[SG_MIX:SG_MIX + 1]
    (dh0, dgu1, act1, df1, dg_ffn1), (r_win,) = _ffn_bwd(dh1, h0, gu1, ffn1_norm, wgu1, wdn1, tm,
                                                         comm=_Scatter([g_win]))
    grad_x = dh0[n_meta:t_real][None]

    rep_rows = [("ffn1_norm", dg_ffn1), ("mix_norm", dg_mix), ("b_in", sg[SG_BIN:SG_BIN + 6]),
                ("rnn_conv_b", sg[SG_CB4:SG_CB4 + 1]), ("rg_b_a", sg[SG_BA:SG_BA + 1]),
                ("rg_b_x", sg[SG_BX:SG_BX + 1]), ("rg_lambda", sg[SG_LAM:SG_LAM + 1]),
                ("conv_dw_b", sg[SG_CB31:SG_CB31 + 1]), ("conv_ln_g", sg[SG_LNG:SG_LNG + 1]),
                ("conv_ln_b", sg[SG_LNB:SG_LNB + 1]), ("conv_b_proj", sg[SG_BCP:SG_BCP + 1]),
                ("ffn2_norm", dg_ffn2), ("final_norm", dgf)]
    col_rows = [("meta_tokens", dh0[:n_meta]), ("rnn_conv_w", sg[SG_CW4:SG_CW4 + k4]),
                ("conv_dw_w", sg[SG_CW31:SG_CW31 + k31])]
    layout, pieces, r0 = [], [], 0
    for nm, part in rep_rows:
        nr = part.shape[0]
        kind = "wide" if nm == "b_in" else "rep"
        as2d = lambda a: a.reshape(1, -1) if a.ndim == 1 else a
        layout.append((kind, r0, nr, as2d(w[nm]), as2d(w["m_" + nm]), as2d(w["v_" + nm])))
        pieces.append(part)
        r0 += nr
    for nm, part in col_rows:
        nr = part.shape[0]
        sq = lambda a: a.reshape(a.shape[-2], a.shape[-1])
        layout.append(("col", r0, nr, sq(w[nm]), sq(w["m_" + nm]), sq(w["v_" + nm])))
        pieces.append(part)
        r0 += nr
    total_rows = -(-(r0 + 1) // SUBLANES) * SUBLANES
    pieces.append(jnp.zeros((total_rows - 1 - r0, d), F32))
    pieces.append(loss_part)
    small_partial = jnp.concatenate(pieces, axis=0)

    g_wd1, (small_partials,) = d_w_down("1", act1, df1, comm=_Bcast(small_partial))
    g_wgu1, (r_wd1,), res["w_in"] = d_w_gu("1", dgu1, n1, comm=_Scatter([g_wd1]),
                                            rider=rider_for("w_in", g_win, r_win, 8))

    g_last = g_wgu1.reshape((4, 2) + g_wgu1.shape[1:])
    (from_sibling,) = _pair_exchange([g_last])
    comb_wgu1 = _pair_add(g_last, from_sibling, core)
    (r_wgu1,) = _chip_exchange([comb_wgu1])

    groups = [(g_wd1, r_wd1, me_index, ["ffn1_w_down"]), (comb_wgu1, r_wgu1, chip, ["ffn1_w_gu"]),
              (g_w3, r_w3, me_index, ["w_out", "rnn_w_proj", "conv_w_proj"]),
              (g_wg, r_wg, me_index, ["rg_w_a", "rg_w_x"])]
    for own, recv, idx, group in groups:
        outs = _final_adamw(own, recv, idx, [(w[nm], w["m_" + nm], w["v_" + nm]) for nm in group])
        for nm, o in zip(group, outs):
            res[nm] = o
    for nm in ("ffn1_w_gu", "ffn2_w_gu"):
        res[nm] = tuple(jnp.swapaxes(a, 1, 2) for a in res[nm])

    total, small_out = _small_adamw(small_partials, layout, me_index)
    for (nm, _), o in zip(rep_rows + col_rows, small_out):
        res[nm] = tuple(a.reshape(w[nm].shape) for a in o)

    order = ["meta_tokens", "ffn1_norm", "ffn1_w_gu", "ffn1_w_down", "mix_norm", "w_in", "b_in", "rnn_conv_w",
             "rnn_conv_b", "rg_w_a", "rg_b_a", "rg_w_x", "rg_b_x", "rg_lambda", "rnn_w_proj", "conv_dw_w",
             "conv_dw_b", "conv_ln_g", "conv_ln_b", "conv_w_proj", "conv_b_proj", "w_out", "ffn2_norm",
             "ffn2_w_gu", "ffn2_w_down", "final_norm"]
    return (total[total_rows - 1, 0], grad_x, *[res[nm][0] for nm in order], *[res[nm][1] for nm in order],
            *[res[nm][2] for nm in order], *[res[nm][3] for nm in order])
```

```python
import functools
import math

import jax
import jax.numpy as jnp
from jax import lax
from jax.experimental import pallas as pl
from jax.experimental.pallas import tpu as pltpu

F32 = jnp.float32
BF16 = jnp.bfloat16
MESH = pl.DeviceIdType.MESH
N_DEV = 8
N_HEADS = 4
RG_LRU_C = 8.0
EPS = 1e-6
FFN_RES = 0.5
ADAM_LR, ADAM_B1, ADAM_B2, ADAM_EPS, ADAM_WD, ADAM_STEP = 0.001, 0.9, 0.999, 1e-08, 0.01, 10
V7X_VMEM_LIMIT = 56 * 1024 * 1024
CONV4_HALO = 8
CONV31_HALO = 32
SUBLANES = 8
STAGE_ROWS = 512
FFN_CHUNKS = 2
FFN_FWD_CHUNKS = 1
GELU_C = math.sqrt(2.0 / math.pi)
GELU_K = 0.044715


def _any():
    return pl.BlockSpec(memory_space=pl.ANY)


def _params(n_grid):
    return pltpu.CompilerParams(dimension_semantics=("arbitrary",) * n_grid, vmem_limit_bytes=V7X_VMEM_LIMIT)


def _nn(a, b):
    return jnp.dot(a, b, preferred_element_type=F32)


def _nt(a, b):
    return lax.dot_general(a, b, (((1,), (1,)), ((), ())), preferred_element_type=F32)


def _tn(a, b):
    return lax.dot_general(a, b, (((0,), (0,)), ((), ())), preferred_element_type=F32)


def _sigmoid(x):
    return 0.5 * jnp.tanh(0.5 * x) + 0.5


def _rowsum(x):
    return jnp.sum(x, axis=0, keepdims=True)


def _rms_fwd(x, g):
    r = lax.rsqrt(jnp.mean(x * x, axis=-1, keepdims=True) + EPS)
    return x * r * g, r


def _rms_bwd(dn, x, r, g):
    xr = x * r
    gy = dn * g
    dx = r * (gy - xr * jnp.mean(gy * xr, axis=-1, keepdims=True))
    return dx, _rowsum(dn * xr)


def _gelu(y):
    t = jnp.tanh(GELU_C * (y + GELU_K * y * y * y))
    return 0.5 * y * (1.0 + t), t


def _gelu_grad(y, t):
    return 0.5 * (1.0 + t) + 0.5 * y * (1.0 - t * t) * GELU_C * (1.0 + 3.0 * GELU_K * y * y)


def _softplus(x):
    return jnp.maximum(x, 0.0) + jnp.log(1.0 + jnp.exp(-jnp.abs(x)))


def _one_minus_exp(z):
    series = -z * (1.0 + 0.5 * z * (1.0 + z * (1.0 / 3.0) * (1.0 + 0.25 * z)))
    return jnp.where(z > -0.05, series, 1.0 - jnp.exp(z))


def _tiles(t_real):
    if t_real > 2048:
        tm = 384
        tp = -(-t_real // tm) * tm
        return tp, tm, tm // 2, tm // 2, tp // 2, tp
    tm = 128
    tp = -(-t_real // tm) * tm
    return tp, tm, tm // 2, tm // 2, tm, tm


def _load_weights(copies, sems):
    cps = [pltpu.make_async_copy(s, d, sems.at[k]) for k, (s, d) in enumerate(copies)]
    for cp in cps:
        cp.start()
    for cp in cps:
        cp.wait()


def _position():
    x, y, c = lax.axis_index("x"), lax.axis_index("y"), lax.axis_index("c")
    chips = [(1 - x, y), (x, 1 - y), (1 - x, 1 - y)]
    return x, y, c, chips


def _slot(p):
    return 4 * p[0] + 2 * p[1] + p[2]


class _Lazy(dict):
    def __getitem__(self, key):
        val = dict.__getitem__(self, key)
        return val() if callable(val) else val


class _Gather:
    def __init__(self, shards, pass_on_at=None):
        self.shards = list(shards)
        self.n = len(self.shards)
        self.pass_on_at = pass_on_at

    def inputs(self):
        return self.shards

    def out_shape(self):
        return [jax.ShapeDtypeStruct((N_DEV,) + s.shape, s.dtype) for s in self.shards]

    N_SEMS = 9

    def scratch(self):
        return [pltpu.SemaphoreType.DMA((self.N_SEMS * self.n,)), pltpu.SemaphoreType.DMA((self.N_SEMS * self.n,)),
                pltpu.SemaphoreType.DMA((self.n,))]

    def _plan(self, ins, outs, sems):
        send_sems, recv_sems, local_sems = sems
        x, y, c, _ = _position()
        me, sib, xn, yn, dg = (x, y, c), (x, y, 1 - c), (1 - x, y, c), (x, 1 - y, c), (1 - x, 1 - y, c)
        other = lambda p: (p[0], p[1], 1 - c)

        def blk(a, p, half=None):
            ref = outs[a].at[_slot(p)]
            if half is None:
                return ref
            rows = self.shards[a].shape[0] // 2
            return ref.at[pl.ds(half * rows, rows)]

        def copy(a, k, dst, to, src=None):
            return pltpu.make_async_remote_copy(
                src_ref=dst if src is None else src, dst_ref=dst,
                send_sem=send_sems.at[self.N_SEMS * a + k], recv_sem=recv_sems.at[self.N_SEMS * a + k],
                device_id=to, device_id_type=MESH)

        cp = _Lazy(mine=lambda: [pltpu.make_async_copy(ins[a], blk(a, me), local_sems.at[a]) for a in range(self.n)])
        for a in range(self.n):
            cp[a] = _Lazy(
                own=lambda a=a: [copy(a, 0, blk(a, me), sib, src=ins[a]), copy(a, 1, blk(a, me), xn, src=ins[a]),
                                 copy(a, 2, blk(a, me), yn, src=ins[a])],
                from_x=lambda a=a: copy(a, 1, blk(a, xn), me), from_y=lambda a=a: copy(a, 2, blk(a, yn), me),
                relay_x=lambda a=a: copy(a, 3, blk(a, xn, 0), yn), relay_y=lambda a=a: copy(a, 4, blk(a, yn, 1), xn),
                diag0=lambda a=a: copy(a, 3, blk(a, dg, 0), me), diag1=lambda a=a: copy(a, 4, blk(a, dg, 1), me),
                pass_x=lambda a=a: copy(a, 5, blk(a, xn), sib), pass_y=lambda a=a: copy(a, 6, blk(a, yn), sib),
                pass_d0=lambda a=a: copy(a, 7, blk(a, dg, 0), sib), pass_d1=lambda a=a: copy(a, 8, blk(a, dg, 1), sib),
                from_sib=lambda a=a: [copy(a, 0, blk(a, sib), me), copy(a, 5, blk(a, other(xn)), me),
                                      copy(a, 6, blk(a, other(yn)), me), copy(a, 7, blk(a, other(dg), 0), me),
                                      copy(a, 8, blk(a, other(dg), 1), me)])
        return cp

    def start(self, ins, outs, sems):
        cp = self._plan(ins, outs, sems)
        for c in cp["mine"]:
            c.start()
        for a in range(self.n):
            for c in cp[a]["own"]:
                c.start()

    def pass_on(self, ins, outs, sems):
        cp = self._plan(ins, outs, sems)
        for a in range(self.n):
            cp[a]["from_x"].wait_recv()
            cp[a]["relay_x"].start()
            cp[a]["pass_x"].start()
        for a in range(self.n):
            cp[a]["from_y"].wait_recv()
            cp[a]["relay_y"].start()
            cp[a]["pass_y"].start()

    def finish(self, ins, outs, sems):
        if self.pass_on_at is None:
            self.pass_on(ins, outs, sems)
        cp = self._plan(ins, outs, sems)
        for a in range(self.n):
            cp[a]["diag0"].wait_recv()
            cp[a]["pass_d0"].start()
            cp[a]["diag1"].wait_recv()
            cp[a]["pass_d1"].start()
        for a in range(self.n):
            for c in cp[a]["from_sib"]:
                c.wait_recv()
            for c in cp[a]["own"] + [cp[a][k] for k in ("relay_x", "relay_y", "pass_x", "pass_y", "pass_d0", "pass_d1")]:
                c.wait_send()
        for c in cp["mine"]:
            c.wait()


class _Scatter:
    def __init__(self, grads):
        self.grads = list(grads)
        self.n = len(self.grads)

    def inputs(self):
        return self.grads

    def out_shape(self):
        return [jax.ShapeDtypeStruct((N_DEV - 1,) + g.shape[1:], g.dtype) for g in self.grads]

    def scratch(self):
        return [pltpu.SemaphoreType.DMA((7 * self.n,)), pltpu.SemaphoreType.DMA((7 * self.n,))]

    def _plan(self, ins, outs, sems):
        send_sems, recv_sems = sems
        x, y, c, _ = _position()
        cps = []
        for a in range(self.n):
            for k in range(1, N_DEV):
                peer = (x ^ (k >> 2), y ^ ((k >> 1) & 1), c ^ (k & 1))
                cps.append(pltpu.make_async_remote_copy(
                    src_ref=ins[a].at[_slot(peer)], dst_ref=outs[a].at[k - 1],
                    send_sem=send_sems.at[7 * a + k - 1], recv_sem=recv_sems.at[7 * a + k - 1],
                    device_id=peer, device_id_type=MESH))
        return cps

    def start(self, ins, outs, sems):
        for cp in self._plan(ins, outs, sems):
            cp.start()

    def finish(self, ins, outs, sems):
        for cp in self._plan(ins, outs, sems):
            cp.wait()


def _hosted(inner, n_in, n_out, comm, grid):
    if comm is None:
        return inner
    nc_in, nc_out, ns = len(comm.inputs()), len(comm.out_shape()), len(comm.scratch())

    def body(*refs):
        o0 = n_in + nc_in
        s0 = o0 + n_out + nc_out
        main = refs[:n_in] + refs[o0:o0 + n_out] + refs[s0:len(refs) - ns]
        c_in, c_out, c_sems = refs[n_in:o0], refs[o0 + n_out:s0], refs[len(refs) - ns:]
        ids = [pl.program_id(ax) for ax in range(len(grid))]
        first = functools.reduce(jnp.logical_and, [i == 0 for i in ids])
        last = functools.reduce(jnp.logical_and, [i == g - 1 for i, g in zip(ids, grid)])

        @pl.when(first)
        def _():
            comm.start(c_in, c_out, c_sems)

        inner(*main)

        if getattr(comm, "pass_on_at", None) is not None:
            assert len(grid) == 1
            @pl.when(ids[0] == min(grid[0] - 1, int(comm.pass_on_at * grid[0])))
            def _():
                comm.pass_on(c_in, c_out, c_sems)

        @pl.when(last)
        def _():
            comm.finish(c_in, c_out, c_sems)

    return body


def _call(inner, name, grid, in_specs, out_specs, out_shape, scratch, args, comm=None):
    n_in, n_out = len(args), len(out_shape)
    body = _hosted(inner, n_in, n_out, comm, grid)
    if comm is not None:
        in_specs = list(in_specs) + [_any()] * len(comm.inputs())
        args = list(args) + comm.inputs()
        out_specs = list(out_specs) + [_any()] * len(comm.out_shape())
        out_shape = list(out_shape) + comm.out_shape()
        scratch = list(scratch) + comm.scratch()
    outs = pl.pallas_call(
        body, name=name, grid=grid, in_specs=list(in_specs), out_specs=list(out_specs), out_shape=list(out_shape),
        scratch_shapes=list(scratch), compiler_params=_params(len(grid)))(*args)
    return list(outs[:n_out]), list(outs[n_out:])


class _Bcast:
    def __init__(self, block):
        self.block = block

    def inputs(self):
        return [self.block]

    def out_shape(self):
        return [jax.ShapeDtypeStruct((N_DEV,) + self.block.shape, self.block.dtype)]

    def scratch(self):
        return [pltpu.SemaphoreType.DMA((N_DEV - 1,)), pltpu.SemaphoreType.DMA((N_DEV - 1,)),
                pltpu.SemaphoreType.DMA((1,))]

    def _plan(self, ins, outs, sems):
        send_sems, recv_sems, local_sem = sems
        x, y, c, _ = _position()
        mine = outs[0].at[_slot((x, y, c))]
        cps = []
        for k in range(1, N_DEV):
            peer = (x ^ (k >> 2), y ^ ((k >> 1) & 1), c ^ (k & 1))
            cps.append(pltpu.make_async_remote_copy(
                src_ref=ins[0], dst_ref=mine, send_sem=send_sems.at[k - 1], recv_sem=recv_sems.at[k - 1],
                device_id=peer, device_id_type=MESH))
        return pltpu.make_async_copy(ins[0], mine, local_sem.at[0]), cps

    def start(self, ins, outs, sems):
        own, cps = self._plan(ins, outs, sems)
        own.start()
        for cp in cps:
            cp.start()

    def finish(self, ins, outs, sems):
        own, cps = self._plan(ins, outs, sems)
        for cp in cps:
            cp.wait()
        own.wait()


def _first_gather(shards, small_idx, x2, t2, n_meta, tp):
    comm = _Gather(shards)
    n = comm.n
    seq, d = x2.shape
    t_real = n_meta + seq
    n_pad = tp - t_real
    cw = d // N_DEV
    rows = STAGE_ROWS if seq % STAGE_ROWS == 0 else seq
    n_chunks = seq // rows

    def body(*refs):
        ins, (x_ref, t_ref) = refs[:n], refs[n:n + 2]
        outs, (h0_ref, tg_ref) = refs[n + 2:2 * n + 2], refs[2 * n + 2:2 * n + 4]
        sems = refs[2 * n + 4:2 * n + 7]
        buf, zeros, in_sems, out_sems, misc_sems = refs[2 * n + 7:]
        comm.start(ins, outs, sems)
        zeros[...] = jnp.zeros_like(zeros)
        fills = [pltpu.make_async_copy(zeros.at[pl.ds(0, n_pad)], h0_ref.at[pl.ds(t_real, n_pad)], misc_sems.at[0]),
                 pltpu.make_async_copy(zeros.at[pl.ds(0, n_pad)], tg_ref.at[pl.ds(t_real, n_pad)], misc_sems.at[1]),
                 pltpu.make_async_copy(zeros.at[pl.ds(0, n_meta)], tg_ref.at[pl.ds(0, n_meta)], misc_sems.at[2])]
        for cp in fills:
            cp.start()
        jobs = [(src, dst, c) for src, dst in ((x_ref, h0_ref), (t_ref, tg_ref)) for c in range(n_chunks)]

        def load(k):
            src, _, c = jobs[k]
            return pltpu.make_async_copy(src.at[pl.ds(c * rows, rows)], buf.at[k % 2], in_sems.at[k % 2])

        def store(k):
            _, dst, c = jobs[k]
            return pltpu.make_async_copy(buf.at[k % 2], dst.at[pl.ds(n_meta + c * rows, rows)], out_sems.at[k % 2])

        load(0).start()
        for k in range(len(jobs)):
            load(k).wait()
            if k + 1 < len(jobs):
                if k >= 1:
                    store(k - 1).wait()
                load(k + 1).start()
            store(k).start()
        for k in range(max(0, len(jobs) - 2), len(jobs)):
            store(k).wait()
        comm.finish(ins, outs, sems)
        meta = [pltpu.make_async_copy(outs[small_idx].at[k, pl.ds(0, n_meta)],
                                      h0_ref.at[pl.ds(0, n_meta), pl.ds(k * cw, cw)], misc_sems.at[3 + k])
                for k in range(N_DEV)]
        for cp in meta:
            cp.start()
        for cp in fills + meta:
            cp.wait()

    staged = [jax.ShapeDtypeStruct((tp, d), F32)] * 2
    outs = pl.pallas_call(
        body, name="weights_all_gather", out_shape=comm.out_shape() + staged,
        in_specs=[_any()] * (n + 2), out_specs=[_any()] * (n + 2),
        scratch_shapes=comm.scratch() + [
            pltpu.VMEM((2, rows, d), F32), pltpu.VMEM((max(n_pad, n_meta), d), F32),
            pltpu.SemaphoreType.DMA((2,)), pltpu.SemaphoreType.DMA((2,)), pltpu.SemaphoreType.DMA((3 + N_DEV,))],
        compiler_params=pltpu.CompilerParams(vmem_limit_bytes=V7X_VMEM_LIMIT),
    )(*shards, x2, t2)
    return outs[:n], outs[n], outs[n + 1]


def _pair_exchange(grads):
    n = len(grads)

    def body(*refs):
        ins, outs = refs[:n], refs[n:2 * n]
        send_sems, recv_sems = refs[2 * n:]
        x, y, c, _ = _position()
        cps = [pltpu.make_async_remote_copy(
            src_ref=ins[a].at[:, 1 - c], dst_ref=outs[a],
            send_sem=send_sems.at[a], recv_sem=recv_sems.at[a],
            device_id=(x, y, 1 - c), device_id_type=MESH) for a in range(n)]
        for cp in cps:
            cp.start()
        for cp in cps:
            cp.wait()

    return pl.pallas_call(
        body, name="grads_pair_exchange",
        out_shape=[jax.ShapeDtypeStruct((4,) + g.shape[2:], g.dtype) for g in grads],
        in_specs=[_any()] * n, out_specs=[_any()] * n,
        scratch_shapes=[pltpu.SemaphoreType.DMA((n,)), pltpu.SemaphoreType.DMA((n,))],
    )(*grads)


def _chip_copies(c_ref, land_ref, sems):
    _, _, c, chips = _position()
    return [pltpu.make_async_remote_copy(
        src_ref=c_ref.at[2 * cx + cy], dst_ref=land_ref.at[j], send_sem=sems[j], recv_sem=sems[3 + j],
        device_id=(cx, cy, c), device_id_type=MESH) for j, (cx, cy) in enumerate(chips)]


def _chip_exchange_start(comb):
    hbm = pl.BlockSpec(memory_space=pltpu.HBM)
    sem = pl.BlockSpec(memory_space=pltpu.SEMAPHORE)

    def body(c_ref, land_ref, *refs):
        for cp in _chip_copies(c_ref, land_ref, refs[:6]):
            cp.start()
        token = refs[8]
        token[...] = jnp.zeros_like(token)

    land = lax.empty((3,) + comb.shape[1:], comb.dtype)
    outs = pl.pallas_call(
        body, name="grads_chip_exchange_start",
        out_shape=(pltpu.SemaphoreType.DMA(()),) * 6
        + (pltpu.HBM(comb.shape, comb.dtype), pltpu.HBM(land.shape, land.dtype),
           jax.ShapeDtypeStruct((SUBLANES, 128), F32)),
        in_specs=(hbm, hbm), out_specs=(sem,) * 6 + (hbm, hbm, pl.BlockSpec(memory_space=pltpu.VMEM)),
        input_output_aliases={0: 6, 1: 7},
        compiler_params=pltpu.CompilerParams(has_side_effects=pltpu.SideEffectType.DATAFLOW_SIDE_EFFECTING),
    )(pltpu.with_memory_space_constraint(comb, pltpu.HBM), pltpu.with_memory_space_constraint(land, pltpu.HBM))
    return outs[:6], outs[6], outs[7], outs[8]


def _chip_exchange_wait(sems, comb_thru, land_thru, after):
    hbm = pl.BlockSpec(memory_space=pltpu.HBM)
    sem = pl.BlockSpec(memory_space=pltpu.SEMAPHORE)

    def body(c_ref, land_ref, *refs):
        for cp in _chip_copies(c_ref, land_ref, refs[:6]):
            cp.wait_send()
            cp.wait_recv()

    return pl.pallas_call(
        body, name="grads_chip_exchange_wait",
        out_shape=(pltpu.HBM(comb_thru.shape, comb_thru.dtype), pltpu.HBM(land_thru.shape, land_thru.dtype)),
        in_specs=(hbm, hbm) + (sem,) * 6 + (pl.BlockSpec(memory_space=pl.ANY),), out_specs=(hbm, hbm),
        input_output_aliases={0: 0, 1: 1},
        compiler_params=pltpu.CompilerParams(has_side_effects=pltpu.SideEffectType.DATAFLOW_SIDE_EFFECTING),
    )(comb_thru, land_thru, *sems, after)


def _pair_add(grad, recv, core):
    blk = grad.shape[2:]
    zeros = (0,) * len(blk)

    def body(core_ref, g_ref, r_ref, o_ref):
        del core_ref
        o_ref[...] = (g_ref[...].astype(F32) + r_ref[...].astype(F32)).astype(BF16)

    return pl.pallas_call(
        body, name="grads_pair_add",
        out_shape=jax.ShapeDtypeStruct((4,) + blk, BF16),
        grid_spec=pltpu.PrefetchScalarGridSpec(
            num_scalar_prefetch=1, grid=(4,),
            in_specs=[pl.BlockSpec((None, None) + blk, lambda i, cr: (i, cr[0]) + zeros),
                      pl.BlockSpec((None,) + blk, lambda i, cr: (i,) + zeros)],
            out_specs=pl.BlockSpec((None,) + blk, lambda i, cr: (i,) + zeros)),
        compiler_params=_params(1),
    )(core, grad, recv)


def _adamw(w, g, m, v):
    m2 = ADAM_B1 * m + (1.0 - ADAM_B1) * g
    v2 = ADAM_B2 * v + (1.0 - ADAM_B2) * (g * g)
    m_hat = m2 / (1.0 - ADAM_B1 ** ADAM_STEP)
    v_hat = v2 / (1.0 - ADAM_B2 ** ADAM_STEP)
    delta = -ADAM_LR * (m_hat / (jnp.sqrt(v_hat) + ADAM_EPS) + ADAM_WD * w)
    return delta, m2, v2


def _final_adamw(own, recv, idx, parts, after):
    blk = own.shape[1:]
    n_recv = recv.shape[0]
    n_parts = len(parts)
    per = blk[0] // n_parts if n_parts > 1 else None
    rows = blk[-2]
    n_chunks = 1 if n_parts > 1 else (4 if rows % 64 == 0 and rows >= 512 else (2 if rows % 32 == 0 else 1))
    cblk = blk[:-2] + (rows // n_chunks, blk[-1])
    lead = (0,) * (len(blk) - 2)

    def body(idx_ref, c_ref, r_ref, after_ref, *refs):
        del idx_ref, after_ref
        ins, outs = refs[:3 * n_parts], refs[3 * n_parts:]
        g = c_ref[...].astype(F32)
        for k in range(n_recv):
            g = g + r_ref[k].astype(F32)
        for p in range(n_parts):
            w_ref, m_ref, v_ref = ins[3 * p:3 * p + 3]
            if n_parts == 1:
                gp = g
            elif per == 1:
                gp = g[p]
            else:
                gp = g[p * per:(p + 1) * per]
            delta, m2, v2 = _adamw(w_ref[0], gp, m_ref[0], v_ref[0])
            o = outs[4 * p:4 * p + 4]
            o[0][0] = gp
            o[1][0] = delta
            o[2][0] = m2
            o[3][0] = v2

    flat = [a for wmv in parts for a in wmv]

    def part_spec(a):
        shape = a.shape[:-2] + (a.shape[-2] // n_chunks, a.shape[-1])
        return pl.BlockSpec(shape, lambda i, cr, nd=a.ndim: (0,) * (nd - 2) + (i, 0))

    outs = pl.pallas_call(
        body, name="grads_sum_adamw",
        out_shape=[jax.ShapeDtypeStruct(wmv[0].shape, F32) for wmv in parts for _ in range(4)],
        grid_spec=pltpu.PrefetchScalarGridSpec(
            num_scalar_prefetch=1, grid=(n_chunks,),
            in_specs=[pl.BlockSpec((None,) + cblk, lambda i, cr: (cr[0],) + lead + (i, 0)),
                      pl.BlockSpec((n_recv,) + cblk, lambda i, cr: (0,) + lead + (i, 0))]
                     + [_any()] + [part_spec(a) for a in flat],
            out_specs=[part_spec(wmv[0]) for wmv in parts for _ in range(4)]),
        compiler_params=_params(1),
    )(idx, own, recv, after, *flat)
    return [tuple(outs[4 * p:4 * p + 4]) for p in range(n_parts)]


def _small_adamw(partials, layout, me_index):
    _, rows, d = partials.shape
    n = len(layout)
    cw = d // N_DEV

    def body(me_ref, p_ref, *refs):
        ins, t_ref, outs = refs[:3 * n], refs[3 * n], refs[3 * n + 1:]
        me = me_ref[0]
        total = p_ref[0]
        for j in range(1, N_DEV):
            total = total + p_ref[j]
        t_ref[...] = total
        for e, (kind, r0, nr, _, _, _) in enumerate(layout):
            w_ref, m_ref, v_ref = ins[3 * e:3 * e + 3]
            o = outs[4 * e:4 * e + 4]
            if kind == "rep":
                g = t_ref[r0:r0 + nr, :]
                delta, m2, v2 = _adamw(w_ref[...], g, m_ref[...], v_ref[...])
                for ref, val in zip(o, (g, delta, m2, v2)):
                    ref[...] = val
            elif kind == "wide":
                for q in range(nr):
                    sl = slice(q * d, (q + 1) * d)
                    g = t_ref[r0 + q:r0 + q + 1, :]
                    delta, m2, v2 = _adamw(w_ref[:, sl], g, m_ref[:, sl], v_ref[:, sl])
                    for ref, val in zip(o, (g, delta, m2, v2)):
                        ref[:, sl] = val
            else:
                for j in range(N_DEV):
                    @pl.when(me == j)
                    def _(j=j, o=o, w_ref=w_ref, m_ref=m_ref, v_ref=v_ref, r0=r0, nr=nr):
                        g = t_ref[r0:r0 + nr, j * cw:(j + 1) * cw]
                        delta, m2, v2 = _adamw(w_ref[...], g, m_ref[...], v_ref[...])
                        for ref, val in zip(o, (g, delta, m2, v2)):
                            ref[...] = val

    flat = [a for ent in layout for a in ent[3:]]
    vm = pl.BlockSpec(memory_space=pltpu.VMEM)
    outs = pl.pallas_call(
        body, name="small_adamw",
        out_shape=[jax.ShapeDtypeStruct((rows, d), F32)]
                  + [jax.ShapeDtypeStruct(ent[3].shape, F32) for ent in layout for _ in range(4)],
        in_specs=[pl.BlockSpec(memory_space=pltpu.SMEM), vm] + [vm] * len(flat),
        out_specs=[vm] * (1 + 4 * n),
        compiler_params=pltpu.CompilerParams(vmem_limit_bytes=V7X_VMEM_LIMIT),
    )(me_index, partials, *flat)
    return outs[0], [tuple(outs[1 + 4 * e:5 + 4 * e]) for e in range(n)]


def _ffn_fwd(h, g, wgu, wd, tm, loss=None, comm=None):
    tp, d = h.shape
    f = wd.shape[0]
    fc = f // FFN_FWD_CHUNKS
    nt = tp // tm
    with_loss = loss is not None
    if with_loss:
        tgt, gf, n_meta, t_real = loss

    def body(*refs):
        if with_loss:
            (h_ref, g_ref, wgu_hbm, wd_hbm, tgt_ref, gf_ref, out_ref, gu_ref, n_ref, loss_ref, dgf_ref,
             wgu_v, wd_v, sems) = refs
        else:
            h_ref, g_ref, wgu_hbm, wd_hbm, out_ref, gu_ref, n_ref, wgu_v, wd_v, sems = refs
        i = pl.program_id(0)

        @pl.when(i == 0)
        def _():
            _load_weights([(wgu_hbm, wgu_v), (wd_hbm, wd_v)], sems)
            if with_loss:
                loss_ref[...] = jnp.zeros_like(loss_ref)
                dgf_ref[...] = jnp.zeros_like(dgf_ref)

        x = h_ref[...]
        n, _ = _rms_fwd(x, g_ref[...])
        nb = n.astype(BF16)
        n_ref[...] = nb
        acc = jnp.zeros((tm, d), F32)
        for j in range(FFN_FWD_CHUNKS):
            cols = slice(j * fc, (j + 1) * fc)
            gate = _nt(nb, wgu_v[pl.ds(j * fc, fc), :])
            up = _nt(nb, wgu_v[pl.ds(f + j * fc, fc), :])
            gu_ref[0, :, cols] = gate.astype(BF16)
            gu_ref[1, :, cols] = up.astype(BF16)
            act = (gate * _sigmoid(gate) * up).astype(BF16)
            acc = acc + _nn(act, wd_v[pl.ds(j * fc, fc), :])
        hn = x + FFN_RES * acc
        if not with_loss:
            out_ref[...] = hn
        else:
            gfv = gf_ref[...]
            r = lax.rsqrt(jnp.mean(hn * hn, axis=-1, keepdims=True) + EPS)
            xr = hn * r
            rows = i * tm + lax.broadcasted_iota(jnp.int32, (tm, 1), 0)
            mask = jnp.logical_and(rows >= n_meta, rows < t_real)
            diff = jnp.where(mask, xr * gfv - tgt_ref[...], 0.0)
            loss_ref[0:1, :] += jnp.zeros((1, d), F32) + 0.5 * jnp.sum(diff * diff) / d
            dy = diff / d
            gy = dy * gfv
            out_ref[...] = r * (gy - xr * jnp.mean(gy * xr, axis=-1, keepdims=True))
            dgf_ref[0:1, :] += _rowsum(dy * xr)

    row = pl.BlockSpec((tm, d), lambda i: (i, 0))
    vec = pl.BlockSpec((1, d), lambda i: (0, 0))
    in_specs = [row, vec, _any(), _any()]
    out_shape = [jax.ShapeDtypeStruct((tp, d), F32), jax.ShapeDtypeStruct((2, tp, f), BF16),
                 jax.ShapeDtypeStruct((tp, d), BF16)]
    out_specs = [row, pl.BlockSpec((2, tm, f), lambda i: (0, i, 0)), row]
    args = [h, g, wgu, wd]
    if with_loss:
        in_specs += [row, vec]
        out_shape += [jax.ShapeDtypeStruct((SUBLANES, d), F32)] * 2
        out_specs += [pl.BlockSpec((SUBLANES, d), lambda i: (0, 0))] * 2
        args += [tgt, gf]
    return _call(body, "ffn_fwd_loss" if with_loss else "ffn_fwd", (nt,), in_specs, out_specs, out_shape,
                 [pltpu.VMEM((2 * f, d), BF16), pltpu.VMEM((f, d), BF16), pltpu.SemaphoreType.DMA((2,))],
                 args, comm)


def _ffn_bwd(dh, h, gu, g, wgu, wd, tm, comm=None):
    tp, d = h.shape
    f = wd.shape[0]
    fc = f // FFN_CHUNKS
    nt = tp // tm

    def body(dh_ref, h_ref, gu_ref, g_ref, wgu_hbm, wd_hbm,
             dhin_ref, dgu_ref, act_ref, df_ref, dg_ref, wgu_v, wd_v, dn_v, sems):
        i, j = pl.program_id(0), pl.program_id(1)

        @pl.when(jnp.logical_and(i == 0, j == 0))
        def _():
            _load_weights([(wgu_hbm, wgu_v), (wd_hbm, wd_v)], sems)
            dg_ref[...] = jnp.zeros_like(dg_ref)

        dfb = (FFN_RES * dh_ref[...]).astype(BF16)

        @pl.when(j == 0)
        def _():
            df_ref[...] = dfb
            dn_v[...] = jnp.zeros_like(dn_v)

        lo = pl.multiple_of(j * fc, 16)
        dact = _nt(dfb, wd_v[pl.ds(lo, fc), :])
        gate = gu_ref[0].astype(F32)
        up = gu_ref[1].astype(F32)
        sg = _sigmoid(gate)
        silu = gate * sg
        act_ref[...] = (silu * up).astype(BF16)
        dgate = (dact * up * (sg * (1.0 + gate * (1.0 - sg)))).astype(BF16)
        dup = (dact * silu).astype(BF16)
        dgu_ref[0] = dgate
        dgu_ref[1] = dup
        dn_v[...] += _nn(dgate, wgu_v[pl.ds(lo, fc), :]) + _nn(dup, wgu_v[pl.ds(pl.multiple_of(f + j * fc, 16), fc), :])

        @pl.when(j == FFN_CHUNKS - 1)
        def _():
            x = h_ref[...]
            r = lax.rsqrt(jnp.mean(x * x, axis=-1, keepdims=True) + EPS)
            dx, dgp = _rms_bwd(dn_v[...], x, r, g_ref[...])
            dhin_ref[...] = dh_ref[...] + dx
            dg_ref[0:1, :] += dgp

    row = pl.BlockSpec((tm, d), lambda i, j: (i, 0))
    vec = pl.BlockSpec((1, d), lambda i, j: (0, 0))
    hid2 = pl.BlockSpec((2, tm, fc), lambda i, j: (0, i, j))
    return _call(
        body, "ffn_bwd", (nt, FFN_CHUNKS),
        [row, row, hid2, vec, _any(), _any()],
        [row, hid2, pl.BlockSpec((tm, fc), lambda i, j: (i, j)), row,
         pl.BlockSpec((SUBLANES, d), lambda i, j: (0, 0))],
        [jax.ShapeDtypeStruct((tp, d), F32), jax.ShapeDtypeStruct((2, tp, f), BF16),
         jax.ShapeDtypeStruct((tp, f), BF16), jax.ShapeDtypeStruct((tp, d), BF16),
         jax.ShapeDtypeStruct((SUBLANES, d), F32)],
        [pltpu.VMEM((2 * f, d), BF16), pltpu.VMEM((f, d), BF16), pltpu.VMEM((tm, d), F32),
         pltpu.SemaphoreType.DMA((2,))],
        [dh, h, gu, g, wgu, wd], comm)


def _piece_segments(q, d, nb_cols):
    segs = []
    for j in range(N_DEV):
        lo, hi = max(q * d, j * nb_cols), min((q + 1) * d, (j + 1) * nb_cols)
        if lo < hi:
            segs.append((j, lo - q * d, hi - q * d, lo - j * nb_cols, hi - j * nb_cols))
    return segs


def _w3_copies(w3_hbm, rows, w3_v):
    return [(w3_hbm.at[k, pl.ds(q * rows, rows)], w3_v.at[q, pl.ds(k * rows, rows)])
            for q in range(3) for k in range(N_DEV)]


def _gates(xrb, wg_ref, ba, bx, lam, hd):
    pre_r, pre_i = [], []
    for hh in range(N_HEADS):
        xh = xrb[:, hh * hd:(hh + 1) * hd]
        pre_r.append(_nn(xh, wg_ref[0, hh]))
        pre_i.append(_nn(xh, wg_ref[1, hh]))
    r = _sigmoid(jnp.concatenate(pre_r, axis=1) + ba)
    ig = _sigmoid(jnp.concatenate(pre_i, axis=1) + bx)
    sp = _softplus(-lam)
    log_a = -RG_LRU_C * r * sp
    a = jnp.exp(log_a)
    s = jnp.sqrt(_one_minus_exp(2.0 * log_a))
    return r, ig, sp, a, s


def _scan_fwd(a, u, h_prev):
    tm = a.shape[0]
    rows = lax.broadcasted_iota(jnp.int32, a.shape, 0)
    d = 1
    while d < tm:
        if d < SUBLANES:
            keep = rows >= d
            u = jnp.where(keep, a * pltpu.roll(u, d, 0) + u, u)
            a = jnp.where(keep, a * pltpu.roll(a, d, 0), a)
        else:
            u = jnp.concatenate([u[:d], a[d:] * u[:tm - d] + u[d:]], axis=0)
            a = jnp.concatenate([a[:d], a[d:] * a[:tm - d]], axis=0)
        d *= 2
    return u + a * h_prev


def _scan_bwd(b, v, g_next):
    tm = b.shape[0]
    rows = lax.broadcasted_iota(jnp.int32, b.shape, 0)
    d = 1
    while d < tm:
        if d < SUBLANES:
            keep = rows < tm - d
            v = jnp.where(keep, v + b * pltpu.roll(v, tm - d, 0), v)
            b = jnp.where(keep, b * pltpu.roll(b, tm - d, 0), b)
        else:
            v = jnp.concatenate([v[:tm - d] + b[:tm - d] * v[d:], v[tm - d:]], axis=0)
            b = jnp.concatenate([b[:tm - d] * b[d:], b[tm - d:]], axis=0)
        d *= 2
    return v + b * g_next


def _shifted_copies(ext_ref, es_ref, n_rows):
    for s in range(1, SUBLANES):
        es_ref[s, pl.ds(0, n_rows), :] = ext_ref[pl.ds(s, n_rows), :]


def _tap(ext_ref, es_ref, off, tm):
    q, s = divmod(off, SUBLANES)
    if s == 0:
        return ext_ref[pl.ds(SUBLANES * q, tm), :]
    return es_ref[s, pl.ds(SUBLANES * q, tm), :]


def _mixer_fwd(h, g, b_in, win_all, cw4, cb4, wg, ba, bx, lam, cw31, cb31, lng, lnb, bcp, w3_all, tm, comm=None):
    tp, d = h.shape
    nb_cols = win_all.shape[-1]
    n_in = N_DEV * nb_cols
    hd = wg.shape[-1]
    k4, k31 = cw4.shape[0], cw31.shape[0]
    w3_rows = d // N_DEV

    def body(h_ref, g_ref, b_ref, win_hbm, cw4_ref, cb4_ref, wg_ref, ba_ref, bx_ref, lam_ref, cw31_ref, cb31_ref,
             lng_ref, lnb_ref, bcp_ref, w3_hbm,
             h2_ref, p_ref, n_ref, xr_ref, hs_ref, v1_ref, ya_ref, yb_ref,
             win_v, w3_v, ext4, ext31, es31, hcar, sems):
        @pl.when(pl.program_id(0) == 0)
        def _():
            _load_weights([(win_hbm, win_v)] + _w3_copies(w3_hbm, w3_rows, w3_v), sems)
            ext4[pl.ds(0, CONV4_HALO), :] = jnp.zeros((CONV4_HALO, d), F32)
            ext31[pl.ds(0, CONV31_HALO), :] = jnp.zeros((CONV31_HALO, d), F32)
            hcar[...] = jnp.zeros_like(hcar)

        n, _ = _rms_fwd(h_ref[...], g_ref[...])
        nb = n.astype(BF16)
        n_ref[...] = nb

        def piece(q):
            parts = [_nn(nb, win_v[j, :, bl:bh]) for j, _, _, bl, bh in _piece_segments(q, d, nb_cols)]
            pq = (jnp.concatenate(parts, axis=1) + b_ref[:, q * d:(q + 1) * d]).astype(BF16)
            p_ref[:, q * d:(q + 1) * d] = pq
            return pq.astype(F32)

        x_rnn, y_rnn, glu_v, glu_g, gate_a, gate_b = [piece(q) for q in range(6)]

        ext4[pl.ds(CONV4_HALO, tm), :] = x_rnn
        xr = cb4_ref[...] + jnp.zeros((tm, d), F32)
        for k in range(k4):
            xr = xr + cw4_ref[k:k + 1, :] * ext4[pl.ds(CONV4_HALO - (k4 - 1) + k, tm), :]
        ext4[pl.ds(0, CONV4_HALO), :] = ext4[pl.ds(tm, CONV4_HALO), :]
        xrb = xr.astype(BF16)
        xr_ref[...] = xrb
        xr = xrb.astype(F32)
        _, ig, _, a, s = _gates(xrb, wg_ref, ba_ref[...], bx_ref[...], lam_ref[...], hd)
        hseq = _scan_fwd(a, s * (ig * xr), hcar[0:1, :])
        hcar[0:1, :] = hseq[tm - 1:tm, :]
        hs_ref[...] = hseq.astype(BF16)
        gl, _ = _gelu(y_rnn)
        ya = _nn((hseq * gl).astype(BF16), w3_v[0])
        ya_ref[...] = ya.astype(BF16)

        ext31[pl.ds(CONV31_HALO, tm), :] = glu_v * _sigmoid(glu_g)
        _shifted_copies(ext31, es31, tm + CONV31_HALO - SUBLANES)
        v1 = cb31_ref[...] + jnp.zeros((tm, d), F32)
        for k in range(k31):
            v1 = v1 + cw31_ref[k:k + 1, :] * _tap(ext31, es31, CONV31_HALO - (k31 - 1) + k, tm)
        ext31[pl.ds(0, CONV31_HALO), :] = ext31[pl.ds(tm, CONV31_HALO), :]
        v1b = v1.astype(BF16)
        v1_ref[...] = v1b
        v1 = v1b.astype(F32)
        xc = v1 - jnp.mean(v1, axis=-1, keepdims=True)
        rstd = lax.rsqrt(jnp.mean(xc * xc, axis=-1, keepdims=True) + EPS)
        v2 = xc * rstd * lng_ref[...] + lnb_ref[...]
        yb = _nn((v2 * _sigmoid(v2)).astype(BF16), w3_v[1]) + bcp_ref[...]
        yb_ref[...] = yb.astype(BF16)

        merged = _sigmoid(gate_a) * ya + _sigmoid(gate_b) * yb
        h2_ref[...] = h_ref[...] + _nn(merged.astype(BF16), w3_v[2])

    row = pl.BlockSpec((tm, d), lambda i: (i, 0))
    wide = pl.BlockSpec((tm, n_in), lambda i: (i, 0))
    full = lambda a: pl.BlockSpec(a.shape, lambda i, nd=a.ndim: (0,) * nd)
    smalls = [cw4, cb4, wg, ba, bx, lam, cw31, cb31, lng, lnb, bcp]
    return _call(
        body, "mixer_fwd", (tp // tm,),
        [row, full(g), full(b_in), _any()] + [full(a) for a in smalls] + [_any()],
        [row, wide] + [row] * 6,
        [jax.ShapeDtypeStruct((tp, d), F32), jax.ShapeDtypeStruct((tp, n_in), BF16)]
        + [jax.ShapeDtypeStruct((tp, d), BF16)] * 6,
        [pltpu.VMEM(win_all.shape, BF16),
         pltpu.VMEM((3, d, d), BF16),
         pltpu.VMEM((tm + CONV4_HALO, d), F32),
         pltpu.VMEM((tm + CONV31_HALO, d), F32),
         pltpu.VMEM((SUBLANES, tm + CONV31_HALO, d), F32),
         pltpu.VMEM((SUBLANES, d), F32),
         pltpu.SemaphoreType.DMA((1 + 3 * N_DEV,))],
        [h, g, b_in, win_all, *smalls, w3_all], comm)


SG_BIN, SG_CW4, SG_CB4, SG_BA, SG_BX, SG_LAM, SG_CB31, SG_LNG, SG_LNB, SG_BCP, SG_MIX, SG_CW31 = 0, 6, 10, 11, 12, 13, 14, 15, 16, 17, 18, 19


def _mixer_bwd(dh2, h, g, proj, xr_s, hs_s, v1_s, ya_s, yb_s, win_t, cw4, wg, ba, bx, lam, cw31, lng, lnb, w3_all, tm,
               comm=None):
    tp, d = dh2.shape
    n_in = proj.shape[1]
    hd = wg.shape[-1]
    k4, k31 = cw4.shape[0], cw31.shape[0]
    nt = tp // tm
    w3_rows = d // N_DEV
    sg_rows = -(-(SG_CW31 + k31) // SUBLANES) * SUBLANES
    halo_rows = 16
    per = tm // halo_rows

    def body(dh_ref, h_ref, g_ref, p_ref, xr_ref, hs_ref, hh_ref, v1_ref, ya_ref, yb_ref, win_hbm,
             cw4_ref, wg_ref, wgt_ref, ba_ref, bx_ref, lam_ref, cw31_ref, lng_ref, lnb_ref, w3_hbm,
             dh1_ref, dp_ref, x3_ref, y3_ref, yg_ref, sg_ref,
             win_v, w3_v, extd4, extd31, es31, gcar, sems):
        i = pl.program_id(0)
        tile = nt - 1 - i

        @pl.when(i == 0)
        def _():
            _load_weights([(win_hbm, win_v)] + _w3_copies(w3_hbm, w3_rows, w3_v), sems)
            for q in range(3):
                w3_v[q] = w3_v[q].T
            extd4[pl.ds(tm, CONV4_HALO), :] = jnp.zeros((CONV4_HALO, d), F32)
            extd31[pl.ds(tm, CONV31_HALO), :] = jnp.zeros((CONV31_HALO, d), F32)
            gcar[...] = jnp.zeros_like(gcar)
            sg_ref[...] = jnp.zeros_like(sg_ref)

        def acc(row, val):
            sg_ref[row:row + 1, :] += _rowsum(val)

        rows = lax.broadcasted_iota(jnp.int32, (tm, d), 0)
        x_rnn = p_ref[:, 0:d].astype(F32)
        y_rnn = p_ref[:, d:2 * d].astype(F32)
        glu_v = p_ref[:, 2 * d:3 * d].astype(F32)
        glu_g = p_ref[:, 3 * d:4 * d].astype(F32)
        sga = _sigmoid(p_ref[:, 4 * d:5 * d].astype(F32))
        sgb = _sigmoid(p_ref[:, 5 * d:6 * d].astype(F32))
        ya = ya_ref[...].astype(F32)
        yb = yb_ref[...].astype(F32)

        dmob = dh_ref[...].astype(BF16)
        dmerged = _nn(dmob, w3_v[2])
        x3_ref[:, 0:d] = (sga * ya + sgb * yb).astype(BF16)
        y3_ref[:, 0:d] = dmob
        dya = sga * dmerged
        dyb = sgb * dmerged
        dn_parts = []

        def emit(q, val):
            vb = val.astype(BF16)
            dp_ref[:, q * d:(q + 1) * d] = vb
            acc(SG_BIN + q, val)
            term = _nn(vb, win_v[pl.ds(q * d, d), :])
            dn_parts[:] = [term if not dn_parts else dn_parts[0] + term]

        emit(4, dmerged * ya * sga * (1.0 - sga))
        emit(5, dmerged * yb * sgb * (1.0 - sgb))

        dyab = dya.astype(BF16)
        y3_ref[:, d:2 * d] = dyab
        dza = _nn(dyab, w3_v[0])
        hsv = hs_ref[...].astype(F32)
        gl, th = _gelu(y_rnn)
        x3_ref[:, d:2 * d] = (hsv * gl).astype(BF16)
        emit(1, dza * hsv * _gelu_grad(y_rnn, th))
        dhs = dza * gl
        xrb = xr_ref[...]
        xr = xrb.astype(F32)
        lam_v = lam_ref[...]
        r, ig, sp, a, s = _gates(xrb, wg_ref, ba_ref[...], bx_ref[...], lam_v, hd)
        b = jnp.where(rows == tm - 1, gcar[1:2, :], pltpu.roll(a, tm - 1, 0))
        big_g = _scan_bwd(b, dhs, gcar[0:1, :])
        gcar[0:1, :] = big_g[0:1, :]
        gcar[1:2, :] = a[0:1, :]
        h_before = jnp.where(tile > 0, hh_ref[halo_rows - 1:halo_rows, :].astype(F32), 0.0)
        h_prev = jnp.where(rows == 0, h_before, pltpu.roll(hsv, 1, 0))
        ds = big_g * ig * xr
        dla = big_g * h_prev * a - ds * (a * a) / jnp.maximum(s, 1e-20)
        acc(SG_LAM, dla * r * (RG_LRU_C * _sigmoid(-lam_v)))
        dpr = dla * (-RG_LRU_C * sp) * r * (1.0 - r)
        dpi = big_g * s * xr * ig * (1.0 - ig)
        acc(SG_BA, dpr)
        acc(SG_BX, dpi)
        dprb = dpr.astype(BF16)
        dpib = dpi.astype(BF16)
        yg_ref[:, 0:d] = dprb
        yg_ref[:, d:2 * d] = dpib
        back = []
        for hh in range(N_HEADS):
            sl = slice(hh * hd, (hh + 1) * hd)
            back.append(_nn(dprb[:, sl], wgt_ref[0, hh]) + _nn(dpib[:, sl], wgt_ref[1, hh]))
        dxr = big_g * s * ig + jnp.concatenate(back, axis=1)
        acc(SG_CB4, dxr)
        extd4[pl.ds(0, tm), :] = dxr
        dx_rnn = jnp.zeros((tm, d), F32)
        for k in range(k4):
            term = extd4[pl.ds(k4 - 1 - k, tm), :]
            dx_rnn = dx_rnn + cw4_ref[k:k + 1, :] * term
            acc(SG_CW4 + k, x_rnn * term)
        extd4[pl.ds(tm, CONV4_HALO), :] = extd4[pl.ds(0, CONV4_HALO), :]
        emit(0, dx_rnn)

        dybb = dyb.astype(BF16)
        y3_ref[:, 2 * d:3 * d] = dybb
        acc(SG_BCP, dyb)
        dv3 = _nn(dybb, w3_v[1])
        v1 = v1_ref[...].astype(F32)
        xc = v1 - jnp.mean(v1, axis=-1, keepdims=True)
        rstd = lax.rsqrt(jnp.mean(xc * xc, axis=-1, keepdims=True) + EPS)
        xhat = xc * rstd
        lng_v = lng_ref[...]
        v2 = xhat * lng_v + lnb_ref[...]
        s2 = _sigmoid(v2)
        x3_ref[:, 2 * d:3 * d] = (v2 * s2).astype(BF16)
        dv2 = dv3 * (s2 * (1.0 + v2 * (1.0 - s2)))
        acc(SG_LNG, dv2 * xhat)
        acc(SG_LNB, dv2)
        dxh = dv2 * lng_v
        dv1 = rstd * (dxh - jnp.mean(dxh, axis=-1, keepdims=True)
                      - xhat * jnp.mean(dxh * xhat, axis=-1, keepdims=True))
        acc(SG_CB31, dv1)
        extd31[pl.ds(0, tm), :] = dv1
        _shifted_copies(extd31, es31, tm + CONV31_HALO - SUBLANES)
        sgg = _sigmoid(glu_g)
        v0 = glu_v * sgg
        dv0 = jnp.zeros((tm, d), F32)
        for k in range(k31):
            term = _tap(extd31, es31, k31 - 1 - k, tm)
            dv0 = dv0 + cw31_ref[k:k + 1, :] * term
            acc(SG_CW31 + k, v0 * term)
        extd31[pl.ds(tm, CONV31_HALO), :] = extd31[pl.ds(0, CONV31_HALO), :]
        emit(2, dv0 * sgg)
        emit(3, dv0 * glu_v * sgg * (1.0 - sgg))

        dn = dn_parts[0]
        x = h_ref[...]
        rr = lax.rsqrt(jnp.mean(x * x, axis=-1, keepdims=True) + EPS)
        dx, dgp = _rms_bwd(dn, x, rr, g_ref[...])
        dh1_ref[...] = dh_ref[...] + dx
        sg_ref[SG_MIX:SG_MIX + 1, :] += dgp

    rev = lambda i: (nt - 1 - i, 0)
    row = pl.BlockSpec((tm, d), rev)
    wide = pl.BlockSpec((tm, n_in), rev)
    full = lambda a: pl.BlockSpec(a.shape, lambda i, nd=a.ndim: (0,) * nd)
    halo = pl.BlockSpec((halo_rows, d), lambda i: (jnp.maximum((nt - 1 - i) * per - 1, 0), 0))
    smalls = [cw4, wg, jnp.swapaxes(wg, 2, 3), ba, bx, lam, cw31, lng, lnb]
    return _call(
        body, "mixer_bwd", (nt,),
        [row, row, full(g), wide, row, row, halo, row, row, row, _any()]
        + [full(a) for a in smalls] + [_any()],
        [row, wide, pl.BlockSpec((tm, 3 * d), rev), pl.BlockSpec((tm, 3 * d), rev),
         pl.BlockSpec((tm, 2 * d), rev), pl.BlockSpec((sg_rows, d), lambda i: (0, 0))],
        [jax.ShapeDtypeStruct((tp, d), F32), jax.ShapeDtypeStruct((tp, n_in), BF16),
         jax.ShapeDtypeStruct((tp, 3 * d), BF16), jax.ShapeDtypeStruct((tp, 3 * d), BF16),
         jax.ShapeDtypeStruct((tp, 2 * d), BF16), jax.ShapeDtypeStruct((sg_rows, d), F32)],
        [pltpu.VMEM(win_t.shape, BF16),
         pltpu.VMEM((3, d, d), BF16),
         pltpu.VMEM((tm + CONV4_HALO, d), F32),
         pltpu.VMEM((tm + CONV31_HALO, d), F32),
         pltpu.VMEM((SUBLANES, tm + CONV31_HALO, d), F32),
         pltpu.VMEM((SUBLANES, d), F32),
         pltpu.SemaphoreType.DMA((1 + 3 * N_DEV,))],
        [dh2, h, g, proj, xr_s, hs_s, hs_s, v1_s, ya_s, yb_s, win_t, *smalls, w3_all], comm)


def _tn_matmul(name, x, y, x_spec, y_spec, n_blocks, kb, nb, tm, tp, out_shape, out_spec, out_view, comm=None):
    nt = tp // tm

    def body(x_ref, y_ref, o_ref, acc):
        i = pl.program_id(1)

        @pl.when(i == 0)
        def _():
            acc[...] = jnp.zeros_like(acc)

        acc[...] += _tn(x_ref[...], y_ref[...])

        @pl.when(i == nt - 1)
        def _():
            o_ref[...] = acc[...].astype(BF16).reshape(out_view)

    outs, extra = _call(body, name, (n_blocks, nt), [x_spec, y_spec], [out_spec],
                        [jax.ShapeDtypeStruct(out_shape, BF16)], [pltpu.VMEM((kb, nb), F32)], [x, y], comm)
    return outs[0], extra


def kernel(x, meta_tokens, ffn1_norm, ffn1_w_gu, ffn1_w_down, mix_norm, w_in, b_in, rnn_conv_w, rnn_conv_b, rg_w_a, rg_b_a, rg_w_x, rg_b_x, rg_lambda, rnn_w_proj, conv_dw_w, conv_dw_b, conv_ln_g, conv_ln_b, conv_w_proj, conv_b_proj, w_out, ffn2_norm, ffn2_w_gu, ffn2_w_down, final_norm, loss_target, m_meta_tokens, m_ffn1_norm, m_ffn1_w_gu, m_ffn1_w_down, m_mix_norm, m_w_in, m_b_in, m_rnn_conv_w, m_rnn_conv_b, m_rg_w_a, m_rg_b_a, m_rg_w_x, m_rg_b_x, m_rg_lambda, m_rnn_w_proj, m_conv_dw_w, m_conv_dw_b, m_conv_ln_g, m_conv_ln_b, m_conv_w_proj, m_conv_b_proj, m_w_out, m_ffn2_norm, m_ffn2_w_gu, m_ffn2_w_down, m_final_norm, v_meta_tokens, v_ffn1_norm, v_ffn1_w_gu, v_ffn1_w_down, v_mix_norm, v_w_in, v_b_in, v_rnn_conv_w, v_rnn_conv_b, v_rg_w_a, v_rg_b_a, v_rg_w_x, v_rg_b_x, v_rg_lambda, v_rnn_w_proj, v_conv_dw_w, v_conv_dw_b, v_conv_ln_g, v_conv_ln_b, v_conv_w_proj, v_conv_b_proj, v_w_out, v_ffn2_norm, v_ffn2_w_gu, v_ffn2_w_down, v_final_norm):
    w = dict(locals())
    seq, d = x.shape[1], x.shape[2]
    n_meta = meta_tokens.shape[0]
    t_real = n_meta + seq
    tp, tm, tmx_fwd, tmx, tmt, tmw = _tiles(t_real)
    fb = ffn1_w_gu.shape[-1]
    wr = ffn1_w_down.shape[1]
    f = N_DEV * wr
    fc = f // FFN_CHUNKS
    nbc = w_in.shape[-1]
    n_in = N_DEV * nbc
    pr = rnn_w_proj.shape[1]
    hd = rg_w_a.shape[-1]
    gr = rg_w_a.shape[2]
    cw = meta_tokens.shape[1]
    k4, k31 = rnn_conv_w.shape[1], conv_dw_w.shape[1]
    assert n_in == 6 * d and 2 * wr == fb and N_HEADS * hd == d and pr * N_DEV == d

    xi, yi, ci = lax.axis_index("x"), lax.axis_index("y"), lax.axis_index("c")
    core = ci.astype(jnp.int32).reshape(1)
    chip = (2 * xi + yi).astype(jnp.int32).reshape(1)
    me_index = (4 * xi + 2 * yi + ci).astype(jnp.int32).reshape(1)

    for nm in ("ffn1_w_gu", "ffn2_w_gu"):
        for pre in ("", "m_", "v_"):
            w[pre + nm] = jnp.swapaxes(w[pre + nm], 1, 2)

    wgut1 = w["ffn1_w_gu"][0].astype(BF16)
    wgut2 = w["ffn2_w_gu"][0].astype(BF16)
    wd1 = ffn1_w_down[0].astype(BF16)
    wd2 = ffn2_w_down[0].astype(BF16)
    win_loc = w_in[0].astype(BF16)
    win_t_loc = jnp.swapaxes(w_in[0], 0, 1).astype(BF16)
    w3_loc = jnp.concatenate([rnn_w_proj[0], conv_w_proj[0], w_out[0]], axis=0).astype(BF16)
    wg_loc = jnp.stack([rg_w_a[0], rg_w_x[0]]).astype(BF16)
    n_small = n_meta + k4 + k31
    small_rows = -(-n_small // SUBLANES) * SUBLANES
    small_loc = jnp.concatenate([meta_tokens, rnn_conv_w[0], conv_dw_w[0],
                                 jnp.zeros((small_rows - n_small, cw), F32)], axis=0)
    (wgut1_all, wd1_all, wg_all, small_all), h0, tgt = _first_gather(
        [wgut1, wd1, wg_loc, small_loc], 3, x[0], loss_target[0], n_meta, tp)
    wg = wg_all.transpose(1, 2, 0, 3, 4).reshape(2, N_HEADS, hd, hd)
    small_full = small_all.transpose(1, 0, 2).reshape(small_rows, d)
    cw4 = small_full[n_meta:n_meta + k4]
    cw31 = small_full[n_meta + k4:n_meta + k4 + k31]

    wgu1, wdn1 = wgut1_all.reshape(2 * f, d), wd1_all.reshape(f, d)
    (h1, gu1, n1), (win_all, w3_all) = _ffn_fwd(h0, ffn1_norm, wgu1, wdn1, tm,
                                                comm=_Gather([win_loc, w3_loc], pass_on_at=0.55))
    (h2, proj, n2, xr_s, hs_s, v1_s, ya_s, yb_s), (wgut2_all, wd2_all) = _mixer_fwd(
        h1, mix_norm, b_in, win_all, cw4, rnn_conv_b, wg, rg_b_a, rg_b_x, rg_lambda, cw31, conv_dw_b, conv_ln_g,
        conv_ln_b, conv_b_proj, w3_all, tmx_fwd, comm=_Gather([wgut2, wd2], pass_on_at=0.6))
    wgu2, wdn2 = wgut2_all.reshape(2 * f, d), wd2_all.reshape(f, d)
    (dh3, gu2, n3, loss_part, dgf), (win_t_all,) = _ffn_fwd(
        h2, ffn2_norm, wgu2, wdn2, tm, loss=(tgt, final_norm.reshape(1, d), n_meta, t_real),
        comm=_Gather([win_t_loc], pass_on_at=0.7))
    win_t = win_t_all.reshape(n_in, d)

    def d_w_gu(tag, dgu, n_s, comm=None):
        g, extra = _tn_matmul(
            "d_w_gu" + tag, dgu, n_s,
            pl.BlockSpec((None, tmt, fc), lambda b, i: (b // FFN_CHUNKS, i, b % FFN_CHUNKS)),
            pl.BlockSpec((tmt, d), lambda b, i: (i, 0)),
            2 * FFN_CHUNKS, fc, d, tmt, tp, (2 * FFN_CHUNKS, fc, d),
            pl.BlockSpec((None, fc, d), lambda b, i: (b, 0, 0)), (fc, d), comm)
        return g.reshape(N_DEV, fb, d), extra

    def d_w_down(tag, act, df, comm=None):
        g, extra = _tn_matmul(
            "d_w_down" + tag, act, df,
            pl.BlockSpec((tmt, fc), lambda b, i: (i, b)), pl.BlockSpec((tmt, d), lambda b, i: (i, 0)),
            FFN_CHUNKS, fc, d, tmt, tp, (FFN_CHUNKS, fc, d),
            pl.BlockSpec((None, fc, d), lambda b, i: (b, 0, 0)), (fc, d), comm)
        return g.reshape(N_DEV, wr, d), extra

    (dh2, dgu2, act2, df2, dg_ffn2), _ = _ffn_bwd(dh3, h2, gu2, ffn2_norm, wgu2, wdn2, tm)
    g_wgu2, _ = d_w_gu("2", dgu2, n3)
    g_wd2, _ = d_w_down("2", act2, df2)
    (dh1, dproj, x3, y3, yg, sg), (r_wd2, r_wgu2) = _mixer_bwd(
        dh2, h1, mix_norm, proj, xr_s, hs_s, v1_s, ya_s, yb_s, win_t, cw4, wg, rg_b_a, rg_b_x, rg_lambda, cw31,
        conv_ln_g, conv_ln_b, w3_all, tmx, comm=_Scatter([g_wd2, g_wgu2]))
    g_w3, _ = _tn_matmul(
        "d_w_proj3", x3, y3,
        pl.BlockSpec((tmw, d), lambda b, i: (i, b)), pl.BlockSpec((tmw, d), lambda b, i: (i, b)),
        3, d, d, tmw, tp, (N_DEV, 3, pr, d), pl.BlockSpec((N_DEV, None, pr, d), lambda b, i: (0, b, 0, 0)),
        (N_DEV, pr, d))
    g_wg, _ = _tn_matmul(
        "d_w_gates", xr_s, yg,
        pl.BlockSpec((tmw, hd), lambda b, i: (i, b % N_HEADS)), pl.BlockSpec((tmw, hd), lambda b, i: (i, b)),
        2 * N_HEADS, hd, hd, tmw, tp, (N_DEV, 2 * N_HEADS, gr, hd),
        pl.BlockSpec((N_DEV, None, gr, hd), lambda b, i: (0, b, 0, 0)), (N_DEV, gr, hd))
    g_win, (r_w3, r_wg) = _tn_matmul(
        "d_w_in", n2, dproj,
        pl.BlockSpec((tmw, d), lambda b, i: (i, 0)), pl.BlockSpec((tmw, nbc), lambda b, i: (i, b)),
        N_DEV, d, nbc, tmw, tp, (N_DEV, d, nbc), pl.BlockSpec((None, d, nbc), lambda b, i: (b, 0, 0)), (d, nbc),
        comm=_Scatter([g_w3, g_wg]))
    (dh0, dgu1, act1, df1, dg_ffn1), (r_win,) = _ffn_bwd(dh1, h0, gu1, ffn1_norm, wgu1, wdn1, tm,
                                                         comm=_Scatter([g_win]))
    grad_x = dh0[n_meta:t_real][None]

    pieces = [sg, dh0[:n_meta], dg_ffn1, dg_ffn2, dgf, loss_part]
    assert all(p.shape[0] % SUBLANES == 0 for p in pieces)
    at, r0 = [], 0
    for p in pieces:
        at.append(r0)
        r0 += p.shape[0]
    loss_row = at[5]
    rep_rows = [("ffn1_norm", at[2], 1), ("mix_norm", SG_MIX, 1), ("b_in", SG_BIN, 6), ("rnn_conv_b", SG_CB4, 1),
                ("rg_b_a", SG_BA, 1), ("rg_b_x", SG_BX, 1), ("rg_lambda", SG_LAM, 1), ("conv_dw_b", SG_CB31, 1),
                ("conv_ln_g", SG_LNG, 1), ("conv_ln_b", SG_LNB, 1), ("conv_b_proj", SG_BCP, 1),
                ("ffn2_norm", at[3], 1), ("final_norm", at[4], 1)]
    col_rows = [("meta_tokens", at[1], n_meta), ("rnn_conv_w", SG_CW4, k4), ("conv_dw_w", SG_CW31, k31)]
    layout = []
    for nm, row0, nr in rep_rows:
        kind = "wide" if nm == "b_in" else "rep"
        as2d = lambda a: a.reshape(1, -1) if a.ndim == 1 else a
        layout.append((kind, row0, nr, as2d(w[nm]), as2d(w["m_" + nm]), as2d(w["v_" + nm])))
    for nm, row0, nr in col_rows:
        sq = lambda a: a.reshape(a.shape[-2], a.shape[-1])
        layout.append(("col", row0, nr, sq(w[nm]), sq(w["m_" + nm]), sq(w["v_" + nm])))
    small_partial = jnp.concatenate(pieces, axis=0)

    g_wd1, (small_partials,) = d_w_down("1", act1, df1, comm=_Bcast(small_partial))
    g_wgu1, (r_wd1,) = d_w_gu("1", dgu1, n1, comm=_Scatter([g_wd1]))

    g_last = g_wgu1.reshape((4, 2) + g_wgu1.shape[1:])
    (from_sibling,) = _pair_exchange([g_last])
    comb_wgu1 = _pair_add(g_last, from_sibling, core)
    sems, comb_thru, land_thru, after = _chip_exchange_start(comb_wgu1)

    groups = [(g_wd1, r_wd1, me_index, ["ffn1_w_down"]),
              (g_wd2, r_wd2, me_index, ["ffn2_w_down"]), (g_wgu2, r_wgu2, me_index, ["ffn2_w_gu"]),
              (g_win, r_win, me_index, ["w_in"]), (g_w3, r_w3, me_index, ["w_out", "rnn_w_proj", "conv_w_proj"]),
              (g_wg, r_wg, me_index, ["rg_w_a", "rg_w_x"]), (None, None, chip, ["ffn1_w_gu"])]
    res = {}
    for own, recv, idx, group in groups:
        if own is None:
            own, recv = _chip_exchange_wait(sems, comb_thru, land_thru, after)
        outs = _final_adamw(own, recv, idx, [(w[nm], w["m_" + nm], w["v_" + nm]) for nm in group], after)
        after = outs[-1][0]
        for nm, o in zip(group, outs):
            res[nm] = o
    for nm in ("ffn1_w_gu", "ffn2_w_gu"):
        res[nm] = tuple(jnp.swapaxes(a, 1, 2) for a in res[nm])

    total, small_out = _small_adamw(small_partials, layout, me_index)
    for (nm, _, _), o in zip(rep_rows + col_rows, small_out):
        res[nm] = tuple(a.reshape(w[nm].shape) for a in o)

    order = ["meta_tokens", "ffn1_norm", "ffn1_w_gu", "ffn1_w_down", "mix_norm", "w_in", "b_in", "rnn_conv_w",
             "rnn_conv_b", "rg_w_a", "rg_b_a", "rg_w_x", "rg_b_x", "rg_lambda", "rnn_w_proj", "conv_dw_w",
             "conv_dw_b", "conv_ln_g", "conv_ln_b", "conv_w_proj", "conv_b_proj", "w_out", "ffn2_norm",
             "ffn2_w_gu", "ffn2_w_down", "final_norm"]
    return (total[loss_row, 0], grad_x, *[res[nm][0] for nm in order], *[res[nm][1] for nm in order],
            *[res[nm][2] for nm in order], *[res[nm][3] for nm in order])
```

```python
import functools
import math

import jax
import jax.numpy as jnp
from jax import lax
from jax.experimental import pallas as pl
from jax.experimental.pallas import tpu as pltpu

F32 = jnp.float32
BF16 = jnp.bfloat16
MESH = pl.DeviceIdType.MESH
N_DEV = 8
N_HEADS = 4
RG_LRU_C = 8.0
EPS = 1e-6
FFN_RES = 0.5
ADAM_LR, ADAM_B1, ADAM_B2, ADAM_EPS, ADAM_WD, ADAM_STEP = 0.001, 0.9, 0.999, 1e-08, 0.01, 10
V7X_VMEM_LIMIT = 56 * 1024 * 1024
CONV4_HALO = 8
CONV31_HALO = 32
SUBLANES = 8
STAGE_ROWS = 512
TAIL_FFN1, TAIL_FINAL, TAIL_LOSS, TAIL_FFN2 = 0, 1, 2, 3
FFN_CHUNKS = 2
FFN_FWD_CHUNKS = 1
GELU_C = math.sqrt(2.0 / math.pi)
GELU_K = 0.044715


def _any():
    return pl.BlockSpec(memory_space=pl.ANY)


def _params(n_grid):
    return pltpu.CompilerParams(dimension_semantics=("arbitrary",) * n_grid, vmem_limit_bytes=V7X_VMEM_LIMIT)


def _nn(a, b):
    return jnp.dot(a, b, preferred_element_type=F32)


def _nt(a, b):
    return lax.dot_general(a, b, (((1,), (1,)), ((), ())), preferred_element_type=F32)


def _tn(a, b):
    return lax.dot_general(a, b, (((0,), (0,)), ((), ())), preferred_element_type=F32)


def _sigmoid(x):
    return 0.5 * jnp.tanh(0.5 * x) + 0.5


def _rowsum(x):
    return jnp.sum(x, axis=0, keepdims=True)


def _rms_fwd(x, g):
    r = lax.rsqrt(jnp.mean(x * x, axis=-1, keepdims=True) + EPS)
    return x * r * g, r


def _rms_bwd(dn, x, r, g):
    xr = x * r
    gy = dn * g
    dx = r * (gy - xr * jnp.mean(gy * xr, axis=-1, keepdims=True))
    return dx, _rowsum(dn * xr)


def _gelu(y):
    t = jnp.tanh(GELU_C * (y + GELU_K * y * y * y))
    return 0.5 * y * (1.0 + t), t


def _gelu_grad(y, t):
    return 0.5 * (1.0 + t) + 0.5 * y * (1.0 - t * t) * GELU_C * (1.0 + 3.0 * GELU_K * y * y)


def _softplus(x):
    return jnp.maximum(x, 0.0) + jnp.log(1.0 + jnp.exp(-jnp.abs(x)))


def _one_minus_exp(z):
    series = -z * (1.0 + 0.5 * z * (1.0 + z * (1.0 / 3.0) * (1.0 + 0.25 * z)))
    return jnp.where(z > -0.05, series, 1.0 - jnp.exp(z))


def _tiles(t_real):
    if t_real > 2048:
        tm = 384
        tp = -(-t_real // tm) * tm
        return tp, tm, tm // 2, tm // 2, tp // 2, tp
    tm = 128
    tp = -(-t_real // tm) * tm
    return tp, tm, tm // 2, tm // 2, tm, tm


def _load_weights(copies, sems):
    cps = [pltpu.make_async_copy(s, d, sems.at[k]) for k, (s, d) in enumerate(copies)]
    for cp in cps:
        cp.start()
    for cp in cps:
        cp.wait()


def _position():
    x, y, c = lax.axis_index("x"), lax.axis_index("y"), lax.axis_index("c")
    chips = [(1 - x, y), (x, 1 - y), (1 - x, 1 - y)]
    return x, y, c, chips


def _slot(p):
    return 4 * p[0] + 2 * p[1] + p[2]


class _Lazy(dict):
    def __getitem__(self, key):
        val = dict.__getitem__(self, key)
        return val() if callable(val) else val


class _Gather:
    def __init__(self, shards, pass_on_at=None):
        self.shards = list(shards)
        self.n = len(self.shards)
        self.pass_on_at = pass_on_at

    def inputs(self):
        return self.shards

    def out_shape(self):
        return [jax.ShapeDtypeStruct((N_DEV,) + s.shape, s.dtype) for s in self.shards]

    N_SEMS = 9

    def scratch(self):
        return [pltpu.SemaphoreType.DMA((self.N_SEMS * self.n,)), pltpu.SemaphoreType.DMA((self.N_SEMS * self.n,)),
                pltpu.SemaphoreType.DMA((self.n,))]

    def _plan(self, ins, outs, sems):
        send_sems, recv_sems, local_sems = sems
        x, y, c, _ = _position()
        me, sib, xn, yn, dg = (x, y, c), (x, y, 1 - c), (1 - x, y, c), (x, 1 - y, c), (1 - x, 1 - y, c)
        other = lambda p: (p[0], p[1], 1 - c)

        def blk(a, p, half=None):
            ref = outs[a].at[_slot(p)]
            if half is None:
                return ref
            rows = self.shards[a].shape[0] // 2
            return ref.at[pl.ds(half * rows, rows)]

        def copy(a, k, dst, to, src=None):
            return pltpu.make_async_remote_copy(
                src_ref=dst if src is None else src, dst_ref=dst,
                send_sem=send_sems.at[self.N_SEMS * a + k], recv_sem=recv_sems.at[self.N_SEMS * a + k],
                device_id=to, device_id_type=MESH)

        cp = _Lazy(mine=lambda: [pltpu.make_async_copy(ins[a], blk(a, me), local_sems.at[a]) for a in range(self.n)])
        for a in range(self.n):
            cp[a] = _Lazy(
                own=lambda a=a: [copy(a, 0, blk(a, me), sib, src=ins[a]), copy(a, 1, blk(a, me), xn, src=ins[a]),
                                 copy(a, 2, blk(a, me), yn, src=ins[a])],
                from_x=lambda a=a: copy(a, 1, blk(a, xn), me), from_y=lambda a=a: copy(a, 2, blk(a, yn), me),
                relay_x=lambda a=a: copy(a, 3, blk(a, xn, 0), yn), relay_y=lambda a=a: copy(a, 4, blk(a, yn, 1), xn),
                diag0=lambda a=a: copy(a, 3, blk(a, dg, 0), me), diag1=lambda a=a: copy(a, 4, blk(a, dg, 1), me),
                pass_x=lambda a=a: copy(a, 5, blk(a, xn), sib), pass_y=lambda a=a: copy(a, 6, blk(a, yn), sib),
                pass_d0=lambda a=a: copy(a, 7, blk(a, dg, 0), sib), pass_d1=lambda a=a: copy(a, 8, blk(a, dg, 1), sib),
                from_sib=lambda a=a: [copy(a, 0, blk(a, sib), me), copy(a, 5, blk(a, other(xn)), me),
                                      copy(a, 6, blk(a, other(yn)), me), copy(a, 7, blk(a, other(dg), 0), me),
                                      copy(a, 8, blk(a, other(dg), 1), me)])
        return cp

    def start(self, ins, outs, sems):
        cp = self._plan(ins, outs, sems)
        for c in cp["mine"]:
            c.start()
        for a in range(self.n):
            for c in cp[a]["own"]:
                c.start()

    def pass_on(self, ins, outs, sems):
        cp = self._plan(ins, outs, sems)
        for a in range(self.n):
            cp[a]["from_x"].wait_recv()
            cp[a]["relay_x"].start()
            cp[a]["pass_x"].start()
        for a in range(self.n):
            cp[a]["from_y"].wait_recv()
            cp[a]["relay_y"].start()
            cp[a]["pass_y"].start()

    def finish(self, ins, outs, sems):
        if self.pass_on_at is None:
            self.pass_on(ins, outs, sems)
        cp = self._plan(ins, outs, sems)
        for a in range(self.n):
            cp[a]["diag0"].wait_recv()
            cp[a]["pass_d0"].start()
            cp[a]["diag1"].wait_recv()
            cp[a]["pass_d1"].start()
        for a in range(self.n):
            for c in cp[a]["from_sib"]:
                c.wait_recv()
            for c in cp[a]["own"] + [cp[a][k] for k in ("relay_x", "relay_y", "pass_x", "pass_y", "pass_d0", "pass_d1")]:
                c.wait_send()
        for c in cp["mine"]:
            c.wait()


class _Scatter:
    def __init__(self, grads):
        self.grads = list(grads)
        self.n = len(self.grads)

    def inputs(self):
        return self.grads

    def out_shape(self):
        return [jax.ShapeDtypeStruct((N_DEV - 1,) + g.shape[1:], g.dtype) for g in self.grads]

    def scratch(self):
        return [pltpu.SemaphoreType.DMA((7 * self.n,)), pltpu.SemaphoreType.DMA((7 * self.n,))]

    def _plan(self, ins, outs, sems):
        send_sems, recv_sems = sems
        x, y, c, _ = _position()
        cps = []
        for a in range(self.n):
            for k in range(1, N_DEV):
                peer = (x ^ (k >> 2), y ^ ((k >> 1) & 1), c ^ (k & 1))
                cps.append(pltpu.make_async_remote_copy(
                    src_ref=ins[a].at[_slot(peer)], dst_ref=outs[a].at[k - 1],
                    send_sem=send_sems.at[7 * a + k - 1], recv_sem=recv_sems.at[7 * a + k - 1],
                    device_id=peer, device_id_type=MESH))
        return cps

    def start(self, ins, outs, sems):
        for cp in self._plan(ins, outs, sems):
            cp.start()

    def finish(self, ins, outs, sems):
        for cp in self._plan(ins, outs, sems):
            cp.wait()


def _hosted(inner, n_in, n_out, comm, grid):
    if comm is None:
        return inner
    nc_in, nc_out, ns = len(comm.inputs()), len(comm.out_shape()), len(comm.scratch())

    def body(*refs):
        o0 = n_in + nc_in
        s0 = o0 + n_out + nc_out
        main = refs[:n_in] + refs[o0:o0 + n_out] + refs[s0:len(refs) - ns]
        c_in, c_out, c_sems = refs[n_in:o0], refs[o0 + n_out:s0], refs[len(refs) - ns:]
        ids = [pl.program_id(ax) for ax in range(len(grid))]
        first = functools.reduce(jnp.logical_and, [i == 0 for i in ids])
        last = functools.reduce(jnp.logical_and, [i == g - 1 for i, g in zip(ids, grid)])

        @pl.when(first)
        def _():
            comm.start(c_in, c_out, c_sems)

        inner(*main)

        if getattr(comm, "pass_on_at", None) is not None:
            assert len(grid) == 1
            @pl.when(ids[0] == min(grid[0] - 1, int(comm.pass_on_at * grid[0])))
            def _():
                comm.pass_on(c_in, c_out, c_sems)

        @pl.when(last)
        def _():
            comm.finish(c_in, c_out, c_sems)

    return body


def _call(inner, name, grid, in_specs, out_specs, out_shape, scratch, args, comm=None):
    n_in, n_out = len(args), len(out_shape)
    body = _hosted(inner, n_in, n_out, comm, grid)
    if comm is not None:
        in_specs = list(in_specs) + [_any()] * len(comm.inputs())
        args = list(args) + comm.inputs()
        out_specs = list(out_specs) + [_any()] * len(comm.out_shape())
        out_shape = list(out_shape) + comm.out_shape()
        scratch = list(scratch) + comm.scratch()
    outs = pl.pallas_call(
        body, name=name, grid=grid, in_specs=list(in_specs), out_specs=list(out_specs), out_shape=list(out_shape),
        scratch_shapes=list(scratch), compiler_params=_params(len(grid)))(*args)
    return list(outs[:n_out]), list(outs[n_out:])


class _Bcast:
    def __init__(self, block):
        self.block = block

    def inputs(self):
        return [self.block]

    def out_shape(self):
        return [jax.ShapeDtypeStruct((N_DEV,) + self.block.shape, self.block.dtype)]

    def scratch(self):
        return [pltpu.SemaphoreType.DMA((N_DEV - 1,)), pltpu.SemaphoreType.DMA((N_DEV - 1,)),
                pltpu.SemaphoreType.DMA((1,))]

    def _plan(self, ins, outs, sems):
        send_sems, recv_sems, local_sem = sems
        x, y, c, _ = _position()
        mine = outs[0].at[_slot((x, y, c))]
        cps = []
        for k in range(1, N_DEV):
            peer = (x ^ (k >> 2), y ^ ((k >> 1) & 1), c ^ (k & 1))
            cps.append(pltpu.make_async_remote_copy(
                src_ref=ins[0], dst_ref=mine, send_sem=send_sems.at[k - 1], recv_sem=recv_sems.at[k - 1],
                device_id=peer, device_id_type=MESH))
        return pltpu.make_async_copy(ins[0], mine, local_sem.at[0]), cps

    def start(self, ins, outs, sems):
        own, cps = self._plan(ins, outs, sems)
        own.start()
        for cp in cps:
            cp.start()

    def finish(self, ins, outs, sems):
        own, cps = self._plan(ins, outs, sems)
        for cp in cps:
            cp.wait()
        own.wait()


def _first_gather(shards, small_idx, x2, t2, n_meta, tp):
    comm = _Gather(shards)
    n = comm.n
    seq, d = x2.shape
    t_real = n_meta + seq
    n_pad = tp - t_real
    cw = d // N_DEV
    rows = STAGE_ROWS if seq % STAGE_ROWS == 0 else seq
    n_chunks = seq // rows

    def body(*refs):
        ins, (x_ref, t_ref) = refs[:n], refs[n:n + 2]
        outs, (h0_ref, tg_ref) = refs[n + 2:2 * n + 2], refs[2 * n + 2:2 * n + 4]
        sems = refs[2 * n + 4:2 * n + 7]
        buf, zeros, in_sems, out_sems, misc_sems = refs[2 * n + 7:]
        comm.start(ins, outs, sems)
        zeros[...] = jnp.zeros_like(zeros)
        fills = [pltpu.make_async_copy(zeros.at[pl.ds(0, n_pad)], h0_ref.at[pl.ds(t_real, n_pad)], misc_sems.at[0]),
                 pltpu.make_async_copy(zeros.at[pl.ds(0, n_pad)], tg_ref.at[pl.ds(t_real, n_pad)], misc_sems.at[1]),
                 pltpu.make_async_copy(zeros.at[pl.ds(0, n_meta)], tg_ref.at[pl.ds(0, n_meta)], misc_sems.at[2])]
        for cp in fills:
            cp.start()
        jobs = [(src, dst, c) for src, dst in ((x_ref, h0_ref), (t_ref, tg_ref)) for c in range(n_chunks)]

        def load(k):
            src, _, c = jobs[k]
            return pltpu.make_async_copy(src.at[pl.ds(c * rows, rows)], buf.at[k % 2], in_sems.at[k % 2])

        def store(k):
            _, dst, c = jobs[k]
            return pltpu.make_async_copy(buf.at[k % 2], dst.at[pl.ds(n_meta + c * rows, rows)], out_sems.at[k % 2])

        load(0).start()
        for k in range(len(jobs)):
            load(k).wait()
            if k + 1 < len(jobs):
                if k >= 1:
                    store(k - 1).wait()
                load(k + 1).start()
            store(k).start()
        for k in range(max(0, len(jobs) - 2), len(jobs)):
            store(k).wait()
        comm.finish(ins, outs, sems)
        meta = [pltpu.make_async_copy(outs[small_idx].at[k, pl.ds(0, n_meta)],
                                      h0_ref.at[pl.ds(0, n_meta), pl.ds(k * cw, cw)], misc_sems.at[3 + k])
                for k in range(N_DEV)]
        for cp in meta:
            cp.start()
        for cp in fills + meta:
            cp.wait()

    staged = [jax.ShapeDtypeStruct((tp, d), F32)] * 2
    outs = pl.pallas_call(
        body, name="weights_all_gather", out_shape=comm.out_shape() + staged,
        in_specs=[_any()] * (n + 2), out_specs=[_any()] * (n + 2),
        scratch_shapes=comm.scratch() + [
            pltpu.VMEM((2, rows, d), F32), pltpu.VMEM((max(n_pad, n_meta), d), F32),
            pltpu.SemaphoreType.DMA((2,)), pltpu.SemaphoreType.DMA((2,)), pltpu.SemaphoreType.DMA((3 + N_DEV,))],
        compiler_params=pltpu.CompilerParams(vmem_limit_bytes=V7X_VMEM_LIMIT),
    )(*shards, x2, t2)
    return outs[:n], outs[n], outs[n + 1]


def _pair_exchange(grads):
    n = len(grads)

    def body(*refs):
        ins, outs = refs[:n], refs[n:2 * n]
        send_sems, recv_sems = refs[2 * n:]
        x, y, c, _ = _position()
        cps = [pltpu.make_async_remote_copy(
            src_ref=ins[a].at[:, 1 - c], dst_ref=outs[a],
            send_sem=send_sems.at[a], recv_sem=recv_sems.at[a],
            device_id=(x, y, 1 - c), device_id_type=MESH) for a in range(n)]
        for cp in cps:
            cp.start()
        for cp in cps:
            cp.wait()

    return pl.pallas_call(
        body, name="grads_pair_exchange",
        out_shape=[jax.ShapeDtypeStruct((4,) + g.shape[2:], g.dtype) for g in grads],
        in_specs=[_any()] * n, out_specs=[_any()] * n,
        scratch_shapes=[pltpu.SemaphoreType.DMA((n,)), pltpu.SemaphoreType.DMA((n,))],
    )(*grads)


def _chip_copies(c_ref, land_ref, sems):
    _, _, c, chips = _position()
    return [pltpu.make_async_remote_copy(
        src_ref=c_ref.at[2 * cx + cy], dst_ref=land_ref.at[j], send_sem=sems[j], recv_sem=sems[3 + j],
        device_id=(cx, cy, c), device_id_type=MESH) for j, (cx, cy) in enumerate(chips)]


def _scatter_copies(g_ref, land_ref, sems):
    x, y, c, _ = _position()
    cps = []
    for k in range(1, N_DEV):
        peer = (x ^ (k >> 2), y ^ ((k >> 1) & 1), c ^ (k & 1))
        cps.append(pltpu.make_async_remote_copy(
            src_ref=g_ref.at[_slot(peer)], dst_ref=land_ref.at[k - 1], send_sem=sems[k - 1],
            recv_sem=sems[N_DEV - 1 + k - 1], device_id=peer, device_id_type=MESH))
    return cps


def _exchange_start(name, copies, n_copies, src):
    hbm = pl.BlockSpec(memory_space=pltpu.HBM)
    sem = pl.BlockSpec(memory_space=pltpu.SEMAPHORE)
    n_sems = 2 * n_copies

    def body(s_ref, land_ref, *refs):
        for cp in copies(s_ref, land_ref, refs[:n_sems]):
            cp.start()
        token = refs[n_sems + 2]
        token[...] = jnp.zeros_like(token)

    land = lax.empty((n_copies,) + src.shape[1:], src.dtype)
    outs = pl.pallas_call(
        body, name=name + "_start",
        out_shape=(pltpu.SemaphoreType.DMA(()),) * n_sems
        + (pltpu.HBM(src.shape, src.dtype), pltpu.HBM(land.shape, land.dtype),
           jax.ShapeDtypeStruct((SUBLANES, 128), F32)),
        in_specs=(hbm, hbm), out_specs=(sem,) * n_sems + (hbm, hbm, pl.BlockSpec(memory_space=pltpu.VMEM)),
        input_output_aliases={0: n_sems, 1: n_sems + 1},
        compiler_params=pltpu.CompilerParams(has_side_effects=pltpu.SideEffectType.DATAFLOW_SIDE_EFFECTING),
    )(pltpu.with_memory_space_constraint(src, pltpu.HBM), pltpu.with_memory_space_constraint(land, pltpu.HBM))
    return outs[:n_sems], outs[n_sems], outs[n_sems + 1], outs[n_sems + 2]


def _exchange_wait(name, copies, sems, src_thru, land_thru, after):
    hbm = pl.BlockSpec(memory_space=pltpu.HBM)
    sem = pl.BlockSpec(memory_space=pltpu.SEMAPHORE)
    n_sems = len(sems)

    def body(s_ref, land_ref, *refs):
        for cp in copies(s_ref, land_ref, refs[:n_sems]):
            cp.wait_send()
            cp.wait_recv()

    return pl.pallas_call(
        body, name=name + "_wait",
        out_shape=(pltpu.HBM(src_thru.shape, src_thru.dtype), pltpu.HBM(land_thru.shape, land_thru.dtype)),
        in_specs=(hbm, hbm) + (sem,) * n_sems + (pl.BlockSpec(memory_space=pl.ANY),), out_specs=(hbm, hbm),
        input_output_aliases={0: 0, 1: 1},
        compiler_params=pltpu.CompilerParams(has_side_effects=pltpu.SideEffectType.DATAFLOW_SIDE_EFFECTING),
    )(src_thru, land_thru, *sems, after)


def _pair_add(grad, recv, core):
    blk = grad.shape[2:]
    zeros = (0,) * len(blk)

    def body(core_ref, g_ref, r_ref, o_ref):
        del core_ref
        o_ref[...] = (g_ref[...].astype(F32) + r_ref[...].astype(F32)).astype(BF16)

    return pl.pallas_call(
        body, name="grads_pair_add",
        out_shape=jax.ShapeDtypeStruct((4,) + blk, BF16),
        grid_spec=pltpu.PrefetchScalarGridSpec(
            num_scalar_prefetch=1, grid=(4,),
            in_specs=[pl.BlockSpec((None, None) + blk, lambda i, cr: (i, cr[0]) + zeros),
                      pl.BlockSpec((None,) + blk, lambda i, cr: (i,) + zeros)],
            out_specs=pl.BlockSpec((None,) + blk, lambda i, cr: (i,) + zeros)),
        compiler_params=_params(1),
    )(core, grad, recv)


def _adamw(w, g, m, v):
    m2 = ADAM_B1 * m + (1.0 - ADAM_B1) * g
    v2 = ADAM_B2 * v + (1.0 - ADAM_B2) * (g * g)
    m_hat = m2 / (1.0 - ADAM_B1 ** ADAM_STEP)
    v_hat = v2 / (1.0 - ADAM_B2 ** ADAM_STEP)
    delta = -ADAM_LR * (m_hat / (jnp.sqrt(v_hat) + ADAM_EPS) + ADAM_WD * w)
    return delta, m2, v2


def _final_adamw(own, recv, idx, parts, after):
    blk = own.shape[1:]
    n_recv = recv.shape[0]
    n_parts = len(parts)
    per = blk[0] // n_parts if n_parts > 1 else None
    rows = blk[-2]
    n_chunks = 1 if n_parts > 1 else (4 if rows % 64 == 0 and rows >= 512 else (2 if rows % 32 == 0 else 1))
    cblk = blk[:-2] + (rows // n_chunks, blk[-1])
    lead = (0,) * (len(blk) - 2)

    def body(idx_ref, c_ref, r_ref, after_ref, *refs):
        del idx_ref, after_ref
        ins, outs = refs[:3 * n_parts], refs[3 * n_parts:]
        g = c_ref[...].astype(F32)
        for k in range(n_recv):
            g = g + r_ref[k].astype(F32)
        for p in range(n_parts):
            w_ref, m_ref, v_ref = ins[3 * p:3 * p + 3]
            if n_parts == 1:
                gp = g
            elif per == 1:
                gp = g[p]
            else:
                gp = g[p * per:(p + 1) * per]
            delta, m2, v2 = _adamw(w_ref[0], gp, m_ref[0], v_ref[0])
            o = outs[4 * p:4 * p + 4]
            o[0][0] = gp
            o[1][0] = delta
            o[2][0] = m2
            o[3][0] = v2

    flat = [a for wmv in parts for a in wmv]

    def part_spec(a):
        shape = a.shape[:-2] + (a.shape[-2] // n_chunks, a.shape[-1])
        return pl.BlockSpec(shape, lambda i, cr, nd=a.ndim: (0,) * (nd - 2) + (i, 0))

    outs = pl.pallas_call(
        body, name="grads_sum_adamw",
        out_shape=[jax.ShapeDtypeStruct(wmv[0].shape, F32) for wmv in parts for _ in range(4)],
        grid_spec=pltpu.PrefetchScalarGridSpec(
            num_scalar_prefetch=1, grid=(n_chunks,),
            in_specs=[pl.BlockSpec((None,) + cblk, lambda i, cr: (cr[0],) + lead + (i, 0)),
                      pl.BlockSpec((n_recv,) + cblk, lambda i, cr: (0,) + lead + (i, 0))]
                     + [_any()] + [part_spec(a) for a in flat],
            out_specs=[part_spec(wmv[0]) for wmv in parts for _ in range(4)]),
        compiler_params=_params(1),
    )(idx, own, recv, after, *flat)
    return [tuple(outs[4 * p:4 * p + 4]) for p in range(n_parts)]


def _small_adamw(partials, layout, me_index):
    _, rows, d = partials.shape
    n = len(layout)
    cw = d // N_DEV

    def body(me_ref, p_ref, *refs):
        ins, t_ref, outs = refs[:3 * n], refs[3 * n], refs[3 * n + 1:]
        me = me_ref[0]
        total = p_ref[0]
        for j in range(1, N_DEV):
            total = total + p_ref[j]
        t_ref[...] = total
        for e, (kind, r0, nr, _, _, _) in enumerate(layout):
            w_ref, m_ref, v_ref = ins[3 * e:3 * e + 3]
            o = outs[4 * e:4 * e + 4]
            if kind == "rep":
                g = t_ref[r0:r0 + nr, :]
                delta, m2, v2 = _adamw(w_ref[...], g, m_ref[...], v_ref[...])
                for ref, val in zip(o, (g, delta, m2, v2)):
                    ref[...] = val
            elif kind == "wide":
                for q in range(nr):
                    sl = slice(q * d, (q + 1) * d)
                    g = t_ref[r0 + q:r0 + q + 1, :]
                    delta, m2, v2 = _adamw(w_ref[:, sl], g, m_ref[:, sl], v_ref[:, sl])
                    for ref, val in zip(o, (g, delta, m2, v2)):
                        ref[:, sl] = val
            else:
                for j in range(N_DEV):
                    @pl.when(me == j)
                    def _(j=j, o=o, w_ref=w_ref, m_ref=m_ref, v_ref=v_ref, r0=r0, nr=nr):
                        g = t_ref[r0:r0 + nr, j * cw:(j + 1) * cw]
                        delta, m2, v2 = _adamw(w_ref[...], g, m_ref[...], v_ref[...])
                        for ref, val in zip(o, (g, delta, m2, v2)):
                            ref[...] = val

    flat = [a for ent in layout for a in ent[3:]]
    vm = pl.BlockSpec(memory_space=pltpu.VMEM)
    outs = pl.pallas_call(
        body, name="small_adamw",
        out_shape=[jax.ShapeDtypeStruct((rows, d), F32)]
                  + [jax.ShapeDtypeStruct(ent[3].shape, F32) for ent in layout for _ in range(4)],
        in_specs=[pl.BlockSpec(memory_space=pltpu.SMEM), vm] + [vm] * len(flat),
        out_specs=[vm] * (1 + 4 * n),
        compiler_params=pltpu.CompilerParams(vmem_limit_bytes=V7X_VMEM_LIMIT),
    )(me_index, partials, *flat)
    return outs[0], [tuple(outs[1 + 4 * e:5 + 4 * e]) for e in range(n)]


def _ffn_fwd(h, g, wgu, wd, tm, loss=None, comm=None):
    tp, d = h.shape
    f = wd.shape[0]
    fc = f // FFN_FWD_CHUNKS
    nt = tp // tm
    with_loss = loss is not None
    if with_loss:
        tgt, gf, n_meta, t_real = loss

    def body(*refs):
        if with_loss:
            (h_ref, g_ref, wgu_hbm, wd_hbm, tgt_ref, gf_ref, out_ref, gu_ref, n_ref, tail_ref,
             wgu_v, wd_v, sems) = refs
        else:
            h_ref, g_ref, wgu_hbm, wd_hbm, out_ref, gu_ref, n_ref, wgu_v, wd_v, sems = refs
        i = pl.program_id(0)

        @pl.when(i == 0)
        def _():
            _load_weights([(wgu_hbm, wgu_v), (wd_hbm, wd_v)], sems)
            if with_loss:
                tail_ref[...] = jnp.zeros_like(tail_ref)

        x = h_ref[...]
        n, _ = _rms_fwd(x, g_ref[...])
        nb = n.astype(BF16)
        n_ref[...] = nb
        acc = jnp.zeros((tm, d), F32)
        for j in range(FFN_FWD_CHUNKS):
            cols = slice(j * fc, (j + 1) * fc)
            gate = _nt(nb, wgu_v[pl.ds(j * fc, fc), :])
            up = _nt(nb, wgu_v[pl.ds(f + j * fc, fc), :])
            gu_ref[0, :, cols] = gate.astype(BF16)
            gu_ref[1, :, cols] = up.astype(BF16)
            act = (gate * _sigmoid(gate) * up).astype(BF16)
            acc = acc + _nn(act, wd_v[pl.ds(j * fc, fc), :])
        hn = x + FFN_RES * acc
        if not with_loss:
            out_ref[...] = hn
        else:
            gfv = gf_ref[...]
            r = lax.rsqrt(jnp.mean(hn * hn, axis=-1, keepdims=True) + EPS)
            xr = hn * r
            rows = i * tm + lax.broadcasted_iota(jnp.int32, (tm, 1), 0)
            mask = jnp.logical_and(rows >= n_meta, rows < t_real)
            diff = jnp.where(mask, xr * gfv - tgt_ref[...], 0.0)
            tail_ref[TAIL_LOSS:TAIL_LOSS + 1, :] += jnp.zeros((1, d), F32) + 0.5 * jnp.sum(diff * diff) / d
            dy = diff / d
            gy = dy * gfv
            out_ref[...] = r * (gy - xr * jnp.mean(gy * xr, axis=-1, keepdims=True))
            tail_ref[TAIL_FINAL:TAIL_FINAL + 1, :] += _rowsum(dy * xr)

    row = pl.BlockSpec((tm, d), lambda i: (i, 0))
    vec = pl.BlockSpec((1, d), lambda i: (0, 0))
    in_specs = [row, vec, _any(), _any()]
    out_shape = [jax.ShapeDtypeStruct((tp, d), F32), jax.ShapeDtypeStruct((2, tp, f), BF16),
                 jax.ShapeDtypeStruct((tp, d), BF16)]
    out_specs = [row, pl.BlockSpec((2, tm, f), lambda i: (0, i, 0)), row]
    args = [h, g, wgu, wd]
    if with_loss:
        in_specs += [row, vec]
        out_shape += [jax.ShapeDtypeStruct((SUBLANES, d), F32)]
        out_specs += [pl.BlockSpec((SUBLANES, d), lambda i: (0, 0))]
        args += [tgt, gf]
    return _call(body, "ffn_fwd_loss" if with_loss else "ffn_fwd", (nt,), in_specs, out_specs, out_shape,
                 [pltpu.VMEM((2 * f, d), BF16), pltpu.VMEM((f, d), BF16), pltpu.SemaphoreType.DMA((2,))],
                 args, comm)


def _ffn_bwd(dh, h, gu, g, wgu, wd, tm, tail, tail_row, after):
    tp, d = h.shape
    f = wd.shape[0]
    fc = f // FFN_CHUNKS
    nt = tp // tm

    def body(dh_ref, h_ref, gu_ref, g_ref, tail_ref, wgu_hbm, wd_hbm, after_ref,
             dhin_ref, dgu_ref, act_ref, df_ref, dg_ref, wgu_v, wd_v, dn_v, sems):
        del after_ref
        i, j = pl.program_id(0), pl.program_id(1)

        @pl.when(jnp.logical_and(i == 0, j == 0))
        def _():
            _load_weights([(wgu_hbm, wgu_v), (wd_hbm, wd_v)], sems)
            dg_ref[...] = tail_ref[...]

        dfb = (FFN_RES * dh_ref[...]).astype(BF16)

        @pl.when(j == 0)
        def _():
            df_ref[...] = dfb
            dn_v[...] = jnp.zeros_like(dn_v)

        lo = pl.multiple_of(j * fc, 16)
        dact = _nt(dfb, wd_v[pl.ds(lo, fc), :])
        gate = gu_ref[0].astype(F32)
        up = gu_ref[1].astype(F32)
        sg = _sigmoid(gate)
        silu = gate * sg
        act_ref[...] = (silu * up).astype(BF16)
        dgate = (dact * up * (sg * (1.0 + gate * (1.0 - sg)))).astype(BF16)
        dup = (dact * silu).astype(BF16)
        dgu_ref[0] = dgate
        dgu_ref[1] = dup
        dn_v[...] += _nn(dgate, wgu_v[pl.ds(lo, fc), :]) + _nn(dup, wgu_v[pl.ds(pl.multiple_of(f + j * fc, 16), fc), :])

        @pl.when(j == FFN_CHUNKS - 1)
        def _():
            x = h_ref[...]
            r = lax.rsqrt(jnp.mean(x * x, axis=-1, keepdims=True) + EPS)
            dx, dgp = _rms_bwd(dn_v[...], x, r, g_ref[...])
            dhin_ref[...] = dh_ref[...] + dx
            dg_ref[tail_row:tail_row + 1, :] += dgp

    row = pl.BlockSpec((tm, d), lambda i, j: (i, 0))
    vec = pl.BlockSpec((1, d), lambda i, j: (0, 0))
    tile = pl.BlockSpec((SUBLANES, d), lambda i, j: (0, 0))
    hid2 = pl.BlockSpec((2, tm, fc), lambda i, j: (0, i, j))
    return _call(
        body, "ffn_bwd", (nt, FFN_CHUNKS),
        [row, row, hid2, vec, tile, _any(), _any(), _any()],
        [row, hid2, pl.BlockSpec((tm, fc), lambda i, j: (i, j)), row, tile],
        [jax.ShapeDtypeStruct((tp, d), F32), jax.ShapeDtypeStruct((2, tp, f), BF16),
         jax.ShapeDtypeStruct((tp, f), BF16), jax.ShapeDtypeStruct((tp, d), BF16),
         jax.ShapeDtypeStruct((SUBLANES, d), F32)],
        [pltpu.VMEM((2 * f, d), BF16), pltpu.VMEM((f, d), BF16), pltpu.VMEM((tm, d), F32),
         pltpu.SemaphoreType.DMA((2,))],
        [dh, h, gu, g, tail, wgu, wd, after])


def _piece_segments(q, d, nb_cols):
    segs = []
    for j in range(N_DEV):
        lo, hi = max(q * d, j * nb_cols), min((q + 1) * d, (j + 1) * nb_cols)
        if lo < hi:
            segs.append((j, lo - q * d, hi - q * d, lo - j * nb_cols, hi - j * nb_cols))
    return segs


def _w3_copies(w3_hbm, rows, w3_v):
    return [(w3_hbm.at[k, pl.ds(q * rows, rows)], w3_v.at[q, pl.ds(k * rows, rows)])
            for q in range(3) for k in range(N_DEV)]


def _gates(xrb, wg_ref, ba, bx, lam, hd):
    pre_r, pre_i = [], []
    for hh in range(N_HEADS):
        xh = xrb[:, hh * hd:(hh + 1) * hd]
        pre_r.append(_nn(xh, wg_ref[0, hh]))
        pre_i.append(_nn(xh, wg_ref[1, hh]))
    r = _sigmoid(jnp.concatenate(pre_r, axis=1) + ba)
    ig = _sigmoid(jnp.concatenate(pre_i, axis=1) + bx)
    sp = _softplus(-lam)
    log_a = -RG_LRU_C * r * sp
    a = jnp.exp(log_a)
    s = jnp.sqrt(_one_minus_exp(2.0 * log_a))
    return r, ig, sp, a, s


def _scan_fwd(a, u, h_prev):
    tm = a.shape[0]
    rows = lax.broadcasted_iota(jnp.int32, a.shape, 0)
    d = 1
    while d < tm:
        if d < SUBLANES:
            keep = rows >= d
            u = jnp.where(keep, a * pltpu.roll(u, d, 0) + u, u)
            a = jnp.where(keep, a * pltpu.roll(a, d, 0), a)
        else:
            u = jnp.concatenate([u[:d], a[d:] * u[:tm - d] + u[d:]], axis=0)
            a = jnp.concatenate([a[:d], a[d:] * a[:tm - d]], axis=0)
        d *= 2
    return u + a * h_prev


def _scan_bwd(b, v, g_next):
    tm = b.shape[0]
    rows = lax.broadcasted_iota(jnp.int32, b.shape, 0)
    d = 1
    while d < tm:
        if d < SUBLANES:
            keep = rows < tm - d
            v = jnp.where(keep, v + b * pltpu.roll(v, tm - d, 0), v)
            b = jnp.where(keep, b * pltpu.roll(b, tm - d, 0), b)
        else:
            v = jnp.concatenate([v[:tm - d] + b[:tm - d] * v[d:], v[tm - d:]], axis=0)
            b = jnp.concatenate([b[:tm - d] * b[d:], b[tm - d:]], axis=0)
        d *= 2
    return v + b * g_next


def _shifted_copies(ext_ref, es_ref, n_rows):
    for s in range(1, SUBLANES):
        es_ref[s, pl.ds(0, n_rows), :] = ext_ref[pl.ds(s, n_rows), :]


def _tap(ext_ref, es_ref, off, tm):
    q, s = divmod(off, SUBLANES)
    if s == 0:
        return ext_ref[pl.ds(SUBLANES * q, tm), :]
    return es_ref[s, pl.ds(SUBLANES * q, tm), :]


def _mixer_fwd(h, g, b_in, win_all, cw4, cb4, wg, ba, bx, lam, cw31, cb31, lng, lnb, bcp, w3_all, tm, comm=None):
    tp, d = h.shape
    nb_cols = win_all.shape[-1]
    n_in = N_DEV * nb_cols
    hd = wg.shape[-1]
    k4, k31 = cw4.shape[0], cw31.shape[0]
    w3_rows = d // N_DEV

    def body(h_ref, g_ref, b_ref, win_hbm, cw4_ref, cb4_ref, wg_ref, ba_ref, bx_ref, lam_ref, cw31_ref, cb31_ref,
             lng_ref, lnb_ref, bcp_ref, w3_hbm,
             h2_ref, p_ref, n_ref, xr_ref, hs_ref, v1_ref, ya_ref, yb_ref,
             win_v, w3_v, ext4, ext31, es31, hcar, sems):
        @pl.when(pl.program_id(0) == 0)
        def _():
            _load_weights([(win_hbm, win_v)] + _w3_copies(w3_hbm, w3_rows, w3_v), sems)
            ext4[pl.ds(0, CONV4_HALO), :] = jnp.zeros((CONV4_HALO, d), F32)
            ext31[pl.ds(0, CONV31_HALO), :] = jnp.zeros((CONV31_HALO, d), F32)
            hcar[...] = jnp.zeros_like(hcar)

        n, _ = _rms_fwd(h_ref[...], g_ref[...])
        nb = n.astype(BF16)
        n_ref[...] = nb

        def piece(q):
            parts = [_nn(nb, win_v[j, :, bl:bh]) for j, _, _, bl, bh in _piece_segments(q, d, nb_cols)]
            pq = (jnp.concatenate(parts, axis=1) + b_ref[:, q * d:(q + 1) * d]).astype(BF16)
            p_ref[:, q * d:(q + 1) * d] = pq
            return pq.astype(F32)

        x_rnn, y_rnn, glu_v, glu_g, gate_a, gate_b = [piece(q) for q in range(6)]

        ext4[pl.ds(CONV4_HALO, tm), :] = x_rnn
        xr = cb4_ref[...] + jnp.zeros((tm, d), F32)
        for k in range(k4):
            xr = xr + cw4_ref[k:k + 1, :] * ext4[pl.ds(CONV4_HALO - (k4 - 1) + k, tm), :]
        ext4[pl.ds(0, CONV4_HALO), :] = ext4[pl.ds(tm, CONV4_HALO), :]
        xrb = xr.astype(BF16)
        xr_ref[...] = xrb
        xr = xrb.astype(F32)
        _, ig, _, a, s = _gates(xrb, wg_ref, ba_ref[...], bx_ref[...], lam_ref[...], hd)
        hseq = _scan_fwd(a, s * (ig * xr), hcar[0:1, :])
        hcar[0:1, :] = hseq[tm - 1:tm, :]
        hs_ref[...] = hseq.astype(BF16)
        gl, _ = _gelu(y_rnn)
        ya = _nn((hseq * gl).astype(BF16), w3_v[0])
        ya_ref[...] = ya.astype(BF16)

        ext31[pl.ds(CONV31_HALO, tm), :] = glu_v * _sigmoid(glu_g)
        _shifted_copies(ext31, es31, tm + CONV31_HALO - SUBLANES)
        v1 = cb31_ref[...] + jnp.zeros((tm, d), F32)
        for k in range(k31):
            v1 = v1 + cw31_ref[k:k + 1, :] * _tap(ext31, es31, CONV31_HALO - (k31 - 1) + k, tm)
        ext31[pl.ds(0, CONV31_HALO), :] = ext31[pl.ds(tm, CONV31_HALO), :]
        v1b = v1.astype(BF16)
        v1_ref[...] = v1b
        v1 = v1b.astype(F32)
        xc = v1 - jnp.mean(v1, axis=-1, keepdims=True)
        rstd = lax.rsqrt(jnp.mean(xc * xc, axis=-1, keepdims=True) + EPS)
        v2 = xc * rstd * lng_ref[...] + lnb_ref[...]
        yb = _nn((v2 * _sigmoid(v2)).astype(BF16), w3_v[1]) + bcp_ref[...]
        yb_ref[...] = yb.astype(BF16)

        merged = _sigmoid(gate_a) * ya + _sigmoid(gate_b) * yb
        h2_ref[...] = h_ref[...] + _nn(merged.astype(BF16), w3_v[2])

    row = pl.BlockSpec((tm, d), lambda i: (i, 0))
    wide = pl.BlockSpec((tm, n_in), lambda i: (i, 0))
    full = lambda a: pl.BlockSpec(a.shape, lambda i, nd=a.ndim: (0,) * nd)
    smalls = [cw4, cb4, wg, ba, bx, lam, cw31, cb31, lng, lnb, bcp]
    return _call(
        body, "mixer_fwd", (tp // tm,),
        [row, full(g), full(b_in), _any()] + [full(a) for a in smalls] + [_any()],
        [row, wide] + [row] * 6,
        [jax.ShapeDtypeStruct((tp, d), F32), jax.ShapeDtypeStruct((tp, n_in), BF16)]
        + [jax.ShapeDtypeStruct((tp, d), BF16)] * 6,
        [pltpu.VMEM(win_all.shape, BF16),
         pltpu.VMEM((3, d, d), BF16),
         pltpu.VMEM((tm + CONV4_HALO, d), F32),
         pltpu.VMEM((tm + CONV31_HALO, d), F32),
         pltpu.VMEM((SUBLANES, tm + CONV31_HALO, d), F32),
         pltpu.VMEM((SUBLANES, d), F32),
         pltpu.SemaphoreType.DMA((1 + 3 * N_DEV,))],
        [h, g, b_in, win_all, *smalls, w3_all], comm)


SG_BIN, SG_CW4, SG_CB4, SG_BA, SG_BX, SG_LAM, SG_CB31, SG_LNG, SG_LNB, SG_BCP, SG_MIX, SG_CW31 = 0, 6, 10, 11, 12, 13, 14, 15, 16, 17, 18, 19


def _mixer_bwd(dh2, h, g, proj, xr_s, hs_s, v1_s, ya_s, yb_s, win_t, cw4, wg, ba, bx, lam, cw31, lng, lnb, w3_all, tm,
               comm=None):
    tp, d = dh2.shape
    n_in = proj.shape[1]
    hd = wg.shape[-1]
    k4, k31 = cw4.shape[0], cw31.shape[0]
    nt = tp // tm
    w3_rows = d // N_DEV
    sg_rows = -(-(SG_CW31 + k31) // SUBLANES) * SUBLANES
    halo_rows = 16
    per = tm // halo_rows

    def body(dh_ref, h_ref, g_ref, p_ref, xr_ref, hs_ref, hh_ref, v1_ref, ya_ref, yb_ref, win_hbm,
             cw4_ref, wg_ref, wgt_ref, ba_ref, bx_ref, lam_ref, cw31_ref, lng_ref, lnb_ref, w3_hbm,
             dh1_ref, dp_ref, x3_ref, y3_ref, yg_ref, sg_ref,
             win_v, w3_v, extd4, extd31, es31, gcar, sems):
        i = pl.program_id(0)
        tile = nt - 1 - i

        @pl.when(i == 0)
        def _():
            _load_weights([(win_hbm, win_v)] + _w3_copies(w3_hbm, w3_rows, w3_v), sems)
            for q in range(3):
                w3_v[q] = w3_v[q].T
            extd4[pl.ds(tm, CONV4_HALO), :] = jnp.zeros((CONV4_HALO, d), F32)
            extd31[pl.ds(tm, CONV31_HALO), :] = jnp.zeros((CONV31_HALO, d), F32)
            gcar[...] = jnp.zeros_like(gcar)
            sg_ref[...] = jnp.zeros_like(sg_ref)

        def acc(row, val):
            sg_ref[row:row + 1, :] += _rowsum(val)

        rows = lax.broadcasted_iota(jnp.int32, (tm, d), 0)
        x_rnn = p_ref[:, 0:d].astype(F32)
        y_rnn = p_ref[:, d:2 * d].astype(F32)
        glu_v = p_ref[:, 2 * d:3 * d].astype(F32)
        glu_g = p_ref[:, 3 * d:4 * d].astype(F32)
        sga = _sigmoid(p_ref[:, 4 * d:5 * d].astype(F32))
        sgb = _sigmoid(p_ref[:, 5 * d:6 * d].astype(F32))
        ya = ya_ref[...].astype(F32)
        yb = yb_ref[...].astype(F32)

        dmob = dh_ref[...].astype(BF16)
        dmerged = _nn(dmob, w3_v[2])
        x3_ref[:, 0:d] = (sga * ya + sgb * yb).astype(BF16)
        y3_ref[:, 0:d] = dmob
        dya = sga * dmerged
        dyb = sgb * dmerged
        dn_parts = []

        def emit(q, val):
            vb = val.astype(BF16)
            dp_ref[:, q * d:(q + 1) * d] = vb
            acc(SG_BIN + q, val)
            term = _nn(vb, win_v[pl.ds(q * d, d), :])
            dn_parts[:] = [term if not dn_parts else dn_parts[0] + term]

        emit(4, dmerged * ya * sga * (1.0 - sga))
        emit(5, dmerged * yb * sgb * (1.0 - sgb))

        dyab = dya.astype(BF16)
        y3_ref[:, d:2 * d] = dyab
        dza = _nn(dyab, w3_v[0])
        hsv = hs_ref[...].astype(F32)
        gl, th = _gelu(y_rnn)
        x3_ref[:, d:2 * d] = (hsv * gl).astype(BF16)
        emit(1, dza * hsv * _gelu_grad(y_rnn, th))
        dhs = dza * gl
        xrb = xr_ref[...]
        xr = xrb.astype(F32)
        lam_v = lam_ref[...]
        r, ig, sp, a, s = _gates(xrb, wg_ref, ba_ref[...], bx_ref[...], lam_v, hd)
        b = jnp.where(rows == tm - 1, gcar[1:2, :], pltpu.roll(a, tm - 1, 0))
        big_g = _scan_bwd(b, dhs, gcar[0:1, :])
        gcar[0:1, :] = big_g[0:1, :]
        gcar[1:2, :] = a[0:1, :]
        h_before = jnp.where(tile > 0, hh_ref[halo_rows - 1:halo_rows, :].astype(F32), 0.0)
        h_prev = jnp.where(rows == 0, h_before, pltpu.roll(hsv, 1, 0))
        ds = big_g * ig * xr
        dla = big_g * h_prev * a - ds * (a * a) / jnp.maximum(s, 1e-20)
        acc(SG_LAM, dla * r * (RG_LRU_C * _sigmoid(-lam_v)))
        dpr = dla * (-RG_LRU_C * sp) * r * (1.0 - r)
        dpi = big_g * s * xr * ig * (1.0 - ig)
        acc(SG_BA, dpr)
        acc(SG_BX, dpi)
        dprb = dpr.astype(BF16)
        dpib = dpi.astype(BF16)
        yg_ref[:, 0:d] = dprb
        yg_ref[:, d:2 * d] = dpib
        back = []
        for hh in range(N_HEADS):
            sl = slice(hh * hd, (hh + 1) * hd)
            back.append(_nn(dprb[:, sl], wgt_ref[0, hh]) + _nn(dpib[:, sl], wgt_ref[1, hh]))
        dxr = big_g * s * ig + jnp.concatenate(back, axis=1)
        acc(SG_CB4, dxr)
        extd4[pl.ds(0, tm), :] = dxr
        dx_rnn = jnp.zeros((tm, d), F32)
        for k in range(k4):
            term = extd4[pl.ds(k4 - 1 - k, tm), :]
            dx_rnn = dx_rnn + cw4_ref[k:k + 1, :] * term
            acc(SG_CW4 + k, x_rnn * term)
        extd4[pl.ds(tm, CONV4_HALO), :] = extd4[pl.ds(0, CONV4_HALO), :]
        emit(0, dx_rnn)

        dybb = dyb.astype(BF16)
        y3_ref[:, 2 * d:3 * d] = dybb
        acc(SG_BCP, dyb)
        dv3 = _nn(dybb, w3_v[1])
        v1 = v1_ref[...].astype(F32)
        xc = v1 - jnp.mean(v1, axis=-1, keepdims=True)
        rstd = lax.rsqrt(jnp.mean(xc * xc, axis=-1, keepdims=True) + EPS)
        xhat = xc * rstd
        lng_v = lng_ref[...]
        v2 = xhat * lng_v + lnb_ref[...]
        s2 = _sigmoid(v2)
        x3_ref[:, 2 * d:3 * d] = (v2 * s2).astype(BF16)
        dv2 = dv3 * (s2 * (1.0 + v2 * (1.0 - s2)))
        acc(SG_LNG, dv2 * xhat)
        acc(SG_LNB, dv2)
        dxh = dv2 * lng_v
        dv1 = rstd * (dxh - jnp.mean(dxh, axis=-1, keepdims=True)
                      - xhat * jnp.mean(dxh * xhat, axis=-1, keepdims=True))
        acc(SG_CB31, dv1)
        extd31[pl.ds(0, tm), :] = dv1
        _shifted_copies(extd31, es31, tm + CONV31_HALO - SUBLANES)
        sgg = _sigmoid(glu_g)
        v0 = glu_v * sgg
        dv0 = jnp.zeros((tm, d), F32)
        for k in range(k31):
            term = _tap(extd31, es31, k31 - 1 - k, tm)
            dv0 = dv0 + cw31_ref[k:k + 1, :] * term
            acc(SG_CW31 + k, v0 * term)
        extd31[pl.ds(tm, CONV31_HALO), :] = extd31[pl.ds(0, CONV31_HALO), :]
        emit(2, dv0 * sgg)
        emit(3, dv0 * glu_v * sgg * (1.0 - sgg))

        dn = dn_parts[0]
        x = h_ref[...]
        rr = lax.rsqrt(jnp.mean(x * x, axis=-1, keepdims=True) + EPS)
        dx, dgp = _rms_bwd(dn, x, rr, g_ref[...])
        dh1_ref[...] = dh_ref[...] + dx
        sg_ref[SG_MIX:SG_MIX + 1, :] += dgp

    rev = lambda i: (nt - 1 - i, 0)
    row = pl.BlockSpec((tm, d), rev)
    wide = pl.BlockSpec((tm, n_in), rev)
    full = lambda a: pl.BlockSpec(a.shape, lambda i, nd=a.ndim: (0,) * nd)
    halo = pl.BlockSpec((halo_rows, d), lambda i: (jnp.maximum((nt - 1 - i) * per - 1, 0), 0))
    smalls = [cw4, wg, jnp.swapaxes(wg, 2, 3), ba, bx, lam, cw31, lng, lnb]
    return _call(
        body, "mixer_bwd", (nt,),
        [row, row, full(g), wide, row, row, halo, row, row, row, _any()]
        + [full(a) for a in smalls] + [_any()],
        [row, wide, pl.BlockSpec((tm, 3 * d), rev), pl.BlockSpec((tm, 3 * d), rev),
         pl.BlockSpec((tm, 2 * d), rev), pl.BlockSpec((sg_rows, d), lambda i: (0, 0))],
        [jax.ShapeDtypeStruct((tp, d), F32), jax.ShapeDtypeStruct((tp, n_in), BF16),
         jax.ShapeDtypeStruct((tp, 3 * d), BF16), jax.ShapeDtypeStruct((tp, 3 * d), BF16),
         jax.ShapeDtypeStruct((tp, 2 * d), BF16), jax.ShapeDtypeStruct((sg_rows, d), F32)],
        [pltpu.VMEM(win_t.shape, BF16),
         pltpu.VMEM((3, d, d), BF16),
         pltpu.VMEM((tm + CONV4_HALO, d), F32),
         pltpu.VMEM((tm + CONV31_HALO, d), F32),
         pltpu.VMEM((SUBLANES, tm + CONV31_HALO, d), F32),
         pltpu.VMEM((SUBLANES, d), F32),
         pltpu.SemaphoreType.DMA((1 + 3 * N_DEV,))],
        [dh2, h, g, proj, xr_s, hs_s, hs_s, v1_s, ya_s, yb_s, win_t, *smalls, w3_all], comm)


def _tn_matmul(name, x, y, x_spec, y_spec, n_blocks, kb, nb, tm, tp, out_shape, out_spec, out_view, comm=None):
    nt = tp // tm

    def body(x_ref, y_ref, o_ref, acc):
        i = pl.program_id(1)

        @pl.when(i == 0)
        def _():
            acc[...] = jnp.zeros_like(acc)

        acc[...] += _tn(x_ref[...], y_ref[...])

        @pl.when(i == nt - 1)
        def _():
            o_ref[...] = acc[...].astype(BF16).reshape(out_view)

    outs, extra = _call(body, name, (n_blocks, nt), [x_spec, y_spec], [out_spec],
                        [jax.ShapeDtypeStruct(out_shape, BF16)], [pltpu.VMEM((kb, nb), F32)], [x, y], comm)
    return outs[0], extra


def kernel(x, meta_tokens, ffn1_norm, ffn1_w_gu, ffn1_w_down, mix_norm, w_in, b_in, rnn_conv_w, rnn_conv_b, rg_w_a, rg_b_a, rg_w_x, rg_b_x, rg_lambda, rnn_w_proj, conv_dw_w, conv_dw_b, conv_ln_g, conv_ln_b, conv_w_proj, conv_b_proj, w_out, ffn2_norm, ffn2_w_gu, ffn2_w_down, final_norm, loss_target, m_meta_tokens, m_ffn1_norm, m_ffn1_w_gu, m_ffn1_w_down, m_mix_norm, m_w_in, m_b_in, m_rnn_conv_w, m_rnn_conv_b, m_rg_w_a, m_rg_b_a, m_rg_w_x, m_rg_b_x, m_rg_lambda, m_rnn_w_proj, m_conv_dw_w, m_conv_dw_b, m_conv_ln_g, m_conv_ln_b, m_conv_w_proj, m_conv_b_proj, m_w_out, m_ffn2_norm, m_ffn2_w_gu, m_ffn2_w_down, m_final_norm, v_meta_tokens, v_ffn1_norm, v_ffn1_w_gu, v_ffn1_w_down, v_mix_norm, v_w_in, v_b_in, v_rnn_conv_w, v_rnn_conv_b, v_rg_w_a, v_rg_b_a, v_rg_w_x, v_rg_b_x, v_rg_lambda, v_rnn_w_proj, v_conv_dw_w, v_conv_dw_b, v_conv_ln_g, v_conv_ln_b, v_conv_w_proj, v_conv_b_proj, v_w_out, v_ffn2_norm, v_ffn2_w_gu, v_ffn2_w_down, v_final_norm):
    w = dict(locals())
    seq, d = x.shape[1], x.shape[2]
    n_meta = meta_tokens.shape[0]
    t_real = n_meta + seq
    tp, tm, tmx_fwd, tmx, tmt, tmw = _tiles(t_real)
    fb = ffn1_w_gu.shape[-1]
    wr = ffn1_w_down.shape[1]
    f = N_DEV * wr
    fc = f // FFN_CHUNKS
    nbc = w_in.shape[-1]
    n_in = N_DEV * nbc
    pr = rnn_w_proj.shape[1]
    hd = rg_w_a.shape[-1]
    gr = rg_w_a.shape[2]
    cw = meta_tokens.shape[1]
    k4, k31 = rnn_conv_w.shape[1], conv_dw_w.shape[1]
    assert n_in == 6 * d and 2 * wr == fb and N_HEADS * hd == d and pr * N_DEV == d

    xi, yi, ci = lax.axis_index("x"), lax.axis_index("y"), lax.axis_index("c")
    core = ci.astype(jnp.int32).reshape(1)
    chip = (2 * xi + yi).astype(jnp.int32).reshape(1)
    me_index = (4 * xi + 2 * yi + ci).astype(jnp.int32).reshape(1)

    for nm in ("ffn1_w_gu", "ffn2_w_gu"):
        for pre in ("", "m_", "v_"):
            w[pre + nm] = jnp.swapaxes(w[pre + nm], 1, 2)

    wgut1 = w["ffn1_w_gu"][0].astype(BF16)
    wgut2 = w["ffn2_w_gu"][0].astype(BF16)
    wd1 = ffn1_w_down[0].astype(BF16)
    wd2 = ffn2_w_down[0].astype(BF16)
    win_loc = w_in[0].astype(BF16)
    win_t_loc = jnp.swapaxes(w_in[0], 0, 1).astype(BF16)
    w3_loc = jnp.concatenate([rnn_w_proj[0], conv_w_proj[0], w_out[0]], axis=0).astype(BF16)
    wg_loc = jnp.stack([rg_w_a[0], rg_w_x[0]]).astype(BF16)
    n_small = n_meta + k4 + k31
    small_rows = -(-n_small // SUBLANES) * SUBLANES
    small_loc = jnp.concatenate([meta_tokens, rnn_conv_w[0], conv_dw_w[0],
                                 jnp.zeros((small_rows - n_small, cw), F32)], axis=0)
    (wgut1_all, wd1_all, wg_all, small_all), h0, tgt = _first_gather(
        [wgut1, wd1, wg_loc, small_loc], 3, x[0], loss_target[0], n_meta, tp)
    wg = wg_all.transpose(1, 2, 0, 3, 4).reshape(2, N_HEADS, hd, hd)
    small_full = small_all.transpose(1, 0, 2).reshape(small_rows, d)
    cw4 = small_full[n_meta:n_meta + k4]
    cw31 = small_full[n_meta + k4:n_meta + k4 + k31]

    wgu1, wdn1 = wgut1_all.reshape(2 * f, d), wd1_all.reshape(f, d)
    (h1, gu1, n1), (win_all, w3_all) = _ffn_fwd(h0, ffn1_norm, wgu1, wdn1, tm,
                                                comm=_Gather([win_loc, w3_loc], pass_on_at=0.55))
    (h2, proj, n2, xr_s, hs_s, v1_s, ya_s, yb_s), (wgut2_all, wd2_all) = _mixer_fwd(
        h1, mix_norm, b_in, win_all, cw4, rnn_conv_b, wg, rg_b_a, rg_b_x, rg_lambda, cw31, conv_dw_b, conv_ln_g,
        conv_ln_b, conv_b_proj, w3_all, tmx_fwd, comm=_Gather([wgut2, wd2], pass_on_at=0.6))
    wgu2, wdn2 = wgut2_all.reshape(2 * f, d), wd2_all.reshape(f, d)
    (dh3, gu2, n3, tail), (win_t_all,) = _ffn_fwd(
        h2, ffn2_norm, wgu2, wdn2, tm, loss=(tgt, final_norm.reshape(1, d), n_meta, t_real),
        comm=_Gather([win_t_loc], pass_on_at=0.7))
    win_t = win_t_all.reshape(n_in, d)

    def d_w_gu(tag, dgu, n_s, comm=None):
        g, extra = _tn_matmul(
            "d_w_gu" + tag, dgu, n_s,
            pl.BlockSpec((None, tmt, fc), lambda b, i: (b // FFN_CHUNKS, i, b % FFN_CHUNKS)),
            pl.BlockSpec((tmt, d), lambda b, i: (i, 0)),
            2 * FFN_CHUNKS, fc, d, tmt, tp, (2 * FFN_CHUNKS, fc, d),
            pl.BlockSpec((None, fc, d), lambda b, i: (b, 0, 0)), (fc, d), comm)
        return g.reshape(N_DEV, fb, d), extra

    def d_w_down(tag, act, df, comm=None):
        g, extra = _tn_matmul(
            "d_w_down" + tag, act, df,
            pl.BlockSpec((tmt, fc), lambda b, i: (i, b)), pl.BlockSpec((tmt, d), lambda b, i: (i, 0)),
            FFN_CHUNKS, fc, d, tmt, tp, (FFN_CHUNKS, fc, d),
            pl.BlockSpec((None, fc, d), lambda b, i: (b, 0, 0)), (fc, d), comm)
        return g.reshape(N_DEV, wr, d), extra

    (dh2, dgu2, act2, df2, tail), _ = _ffn_bwd(dh3, h2, gu2, ffn2_norm, wgu2, wdn2, tm, tail, TAIL_FFN2, n3)
    g_wgu2, _ = d_w_gu("2", dgu2, n3)
    g_wd2, _ = d_w_down("2", act2, df2)
    (dh1, dproj, x3, y3, yg, sg), (r_wd2, r_wgu2) = _mixer_bwd(
        dh2, h1, mix_norm, proj, xr_s, hs_s, v1_s, ya_s, yb_s, win_t, cw4, wg, rg_b_a, rg_b_x, rg_lambda, cw31,
        conv_ln_g, conv_ln_b, w3_all, tmx, comm=_Scatter([g_wd2, g_wgu2]))
    g_w3, _ = _tn_matmul(
        "d_w_proj3", x3, y3,
        pl.BlockSpec((tmw, d), lambda b, i: (i, b)), pl.BlockSpec((tmw, d), lambda b, i: (i, b)),
        3, d, d, tmw, tp, (N_DEV, 3, pr, d), pl.BlockSpec((N_DEV, None, pr, d), lambda b, i: (0, b, 0, 0)),
        (N_DEV, pr, d))
    g_wg, _ = _tn_matmul(
        "d_w_gates", xr_s, yg,
        pl.BlockSpec((tmw, hd), lambda b, i: (i, b % N_HEADS)), pl.BlockSpec((tmw, hd), lambda b, i: (i, b)),
        2 * N_HEADS, hd, hd, tmw, tp, (N_DEV, 2 * N_HEADS, gr, hd),
        pl.BlockSpec((N_DEV, None, gr, hd), lambda b, i: (0, b, 0, 0)), (N_DEV, gr, hd))
    g_win, (r_w3, r_wg) = _tn_matmul(
        "d_w_in", n2, dproj,
        pl.BlockSpec((tmw, d), lambda b, i: (i, 0)), pl.BlockSpec((tmw, nbc), lambda b, i: (i, b)),
        N_DEV, d, nbc, tmw, tp, (N_DEV, d, nbc), pl.BlockSpec((None, d, nbc), lambda b, i: (b, 0, 0)), (d, nbc),
        comm=_Scatter([g_w3, g_wg]))
    win_sems, g_win_thru, win_land, win_token = _exchange_start("grads_w_in_exchange", _scatter_copies, N_DEV - 1, g_win)
    (dh0, dgu1, act1, df1, tail), _ = _ffn_bwd(dh1, h0, gu1, ffn1_norm, wgu1, wdn1, tm, tail, TAIL_FFN1, win_token)
    grad_x = dh0[n_meta:t_real][None]

    pieces = [sg, dh0[:n_meta], tail]
    assert all(p.shape[0] % SUBLANES == 0 for p in pieces)
    at = [0, sg.shape[0], sg.shape[0] + n_meta]
    loss_row = at[2] + TAIL_LOSS
    rep_rows = [("ffn1_norm", at[2] + TAIL_FFN1, 1), ("mix_norm", SG_MIX, 1), ("b_in", SG_BIN, 6),
                ("rnn_conv_b", SG_CB4, 1),
                ("rg_b_a", SG_BA, 1), ("rg_b_x", SG_BX, 1), ("rg_lambda", SG_LAM, 1), ("conv_dw_b", SG_CB31, 1),
                ("conv_ln_g", SG_LNG, 1), ("conv_ln_b", SG_LNB, 1), ("conv_b_proj", SG_BCP, 1),
                ("ffn2_norm", at[2] + TAIL_FFN2, 1), ("final_norm", at[2] + TAIL_FINAL, 1)]
    col_rows = [("meta_tokens", at[1], n_meta), ("rnn_conv_w", SG_CW4, k4), ("conv_dw_w", SG_CW31, k31)]
    layout = []
    for nm, row0, nr in rep_rows:
        kind = "wide" if nm == "b_in" else "rep"
        as2d = lambda a: a.reshape(1, -1) if a.ndim == 1 else a
        layout.append((kind, row0, nr, as2d(w[nm]), as2d(w["m_" + nm]), as2d(w["v_" + nm])))
    for nm, row0, nr in col_rows:
        sq = lambda a: a.reshape(a.shape[-2], a.shape[-1])
        layout.append(("col", row0, nr, sq(w[nm]), sq(w["m_" + nm]), sq(w["v_" + nm])))
    small_partial = jnp.concatenate(pieces, axis=0)

    g_wd1, (small_partials,) = d_w_down("1", act1, df1, comm=_Bcast(small_partial))
    g_wgu1, (r_wd1,) = d_w_gu("1", dgu1, n1, comm=_Scatter([g_wd1]))

    g_last = g_wgu1.reshape((4, 2) + g_wgu1.shape[1:])
    (from_sibling,) = _pair_exchange([g_last])
    comb_wgu1 = _pair_add(g_last, from_sibling, core)
    sems, comb_thru, land_thru, after = _exchange_start("grads_chip_exchange", _chip_copies, 3, comb_wgu1)
    g_win, r_win = _exchange_wait("grads_w_in_exchange", _scatter_copies, win_sems, g_win_thru, win_land, after)

    groups = [(g_wd1, r_wd1, me_index, ["ffn1_w_down"]),
              (g_wd2, r_wd2, me_index, ["ffn2_w_down"]), (g_wgu2, r_wgu2, me_index, ["ffn2_w_gu"]),
              (g_win, r_win, me_index, ["w_in"]), (g_w3, r_w3, me_index, ["w_out", "rnn_w_proj", "conv_w_proj"]),
              (g_wg, r_wg, me_index, ["rg_w_a", "rg_w_x"]), (None, None, chip, ["ffn1_w_gu"])]
    res = {}
    for own, recv, idx, group in groups:
        if own is None:
            own, recv = _exchange_wait("grads_chip_exchange", _chip_copies, sems, comb_thru, land_thru, after)
        outs = _final_adamw(own, recv, idx, [(w[nm], w["m_" + nm], w["v_" + nm]) for nm in group], after)
        after = outs[-1][0]
        for nm, o in zip(group, outs):
            res[nm] = o
    for nm in ("ffn1_w_gu", "ffn2_w_gu"):
        res[nm] = tuple(jnp.swapaxes(a, 1, 2) for a in res[nm])

    total, small_out = _small_adamw(small_partials, layout, me_index)
    for (nm, _, _), o in zip(rep_rows + col_rows, small_out):
        res[nm] = tuple(a.reshape(w[nm].shape) for a in o)

    order = ["meta_tokens", "ffn1_norm", "ffn1_w_gu", "ffn1_w_down", "mix_norm", "w_in", "b_in", "rnn_conv_w",
             "rnn_conv_b", "rg_w_a", "rg_b_a", "rg_w_x", "rg_b_x", "rg_lambda", "rnn_w_proj", "conv_dw_w",
             "conv_dw_b", "conv_ln_g", "conv_ln_b", "conv_w_proj", "conv_b_proj", "w_out", "ffn2_norm",
             "ffn2_w_gu", "ffn2_w_down", "final_norm"]
    return (total[loss_row, 0], grad_x, *[res[nm][0] for nm in order], *[res[nm][1] for nm in order],
            *[res[nm][2] for nm in order], *[res[nm][3] for nm in order])
```

```python
import functools
import math

import jax
import jax.numpy as jnp
from jax import lax
from jax.experimental import pallas as pl
from jax.experimental.pallas import tpu as pltpu

F32 = jnp.float32
BF16 = jnp.bfloat16
MESH = pl.DeviceIdType.MESH
N_DEV = 8
N_HEADS = 4
RG_LRU_C = 8.0
EPS = 1e-6
FFN_RES = 0.5
ADAM_LR, ADAM_B1, ADAM_B2, ADAM_EPS, ADAM_WD, ADAM_STEP = 0.001, 0.9, 0.999, 1e-08, 0.01, 10
V7X_VMEM_LIMIT = 56 * 1024 * 1024
CONV4_HALO = 8
CONV31_HALO = 32
SUBLANES = 8
STAGE_ROWS = 512
TAIL_FFN1, TAIL_FINAL, TAIL_LOSS, TAIL_FFN2 = 0, 1, 2, 3
FFN_CHUNKS = 2
FFN_FWD_CHUNKS = 1
GELU_C = math.sqrt(2.0 / math.pi)
GELU_K = 0.044715


def _any():
    return pl.BlockSpec(memory_space=pl.ANY)


def _params(n_grid):
    return pltpu.CompilerParams(dimension_semantics=("arbitrary",) * n_grid, vmem_limit_bytes=V7X_VMEM_LIMIT)


def _nn(a, b):
    return jnp.dot(a, b, preferred_element_type=F32)


def _nt(a, b):
    return lax.dot_general(a, b, (((1,), (1,)), ((), ())), preferred_element_type=F32)


def _tn(a, b):
    return lax.dot_general(a, b, (((0,), (0,)), ((), ())), preferred_element_type=F32)


def _sigmoid(x):
    return 0.5 * jnp.tanh(0.5 * x) + 0.5


def _rowsum(x):
    return jnp.sum(x, axis=0, keepdims=True)


def _rms_fwd(x, g):
    r = lax.rsqrt(jnp.mean(x * x, axis=-1, keepdims=True) + EPS)
    return x * r * g, r


def _rms_bwd(dn, x, r, g):
    xr = x * r
    gy = dn * g
    dx = r * (gy - xr * jnp.mean(gy * xr, axis=-1, keepdims=True))
    return dx, _rowsum(dn * xr)


def _gelu(y):
    t = jnp.tanh(GELU_C * (y + GELU_K * y * y * y))
    return 0.5 * y * (1.0 + t), t


def _gelu_grad(y, t):
    return 0.5 * (1.0 + t) + 0.5 * y * (1.0 - t * t) * GELU_C * (1.0 + 3.0 * GELU_K * y * y)


def _softplus(x):
    return jnp.maximum(x, 0.0) + jnp.log(1.0 + jnp.exp(-jnp.abs(x)))


def _one_minus_exp(z):
    series = -z * (1.0 + 0.5 * z * (1.0 + z * (1.0 / 3.0) * (1.0 + 0.25 * z)))
    return jnp.where(z > -0.05, series, 1.0 - jnp.exp(z))


def _tiles(t_real):
    if t_real > 2048:
        tm = 384
        tp = -(-t_real // tm) * tm
        return tp, tm, tm // 2, tm // 2, tp // 2, tp
    tm = 128
    tp = -(-t_real // tm) * tm
    return tp, tm, tm // 2, tm // 2, tm, tm


def _load_weights(copies, sems):
    cps = [pltpu.make_async_copy(s, d, sems.at[k]) for k, (s, d) in enumerate(copies)]
    for cp in cps:
        cp.start()
    for cp in cps:
        cp.wait()


def _position():
    x, y, c = lax.axis_index("x"), lax.axis_index("y"), lax.axis_index("c")
    chips = [(1 - x, y), (x, 1 - y), (1 - x, 1 - y)]
    return x, y, c, chips


def _slot(p):
    return 4 * p[0] + 2 * p[1] + p[2]


class _Lazy(dict):
    def __getitem__(self, key):
        val = dict.__getitem__(self, key)
        return val() if callable(val) else val


class _Gather:
    def __init__(self, shards, pass_on_at=None):
        self.shards = list(shards)
        self.n = len(self.shards)
        self.pass_on_at = pass_on_at

    def inputs(self):
        return self.shards

    def out_shape(self):
        return [jax.ShapeDtypeStruct((N_DEV,) + s.shape, s.dtype) for s in self.shards]

    N_SEMS = 9

    def scratch(self):
        return [pltpu.SemaphoreType.DMA((self.N_SEMS * self.n,)), pltpu.SemaphoreType.DMA((self.N_SEMS * self.n,)),
                pltpu.SemaphoreType.DMA((self.n,))]

    def _plan(self, ins, outs, sems):
        send_sems, recv_sems, local_sems = sems
        x, y, c, _ = _position()
        me, sib, xn, yn, dg = (x, y, c), (x, y, 1 - c), (1 - x, y, c), (x, 1 - y, c), (1 - x, 1 - y, c)
        other = lambda p: (p[0], p[1], 1 - c)

        def blk(a, p, half=None):
            ref = outs[a].at[_slot(p)]
            if half is None:
                return ref
            rows = self.shards[a].shape[0] // 2
            return ref.at[pl.ds(half * rows, rows)]

        def copy(a, k, dst, to, src=None):
            return pltpu.make_async_remote_copy(
                src_ref=dst if src is None else src, dst_ref=dst,
                send_sem=send_sems.at[self.N_SEMS * a + k], recv_sem=recv_sems.at[self.N_SEMS * a + k],
                device_id=to, device_id_type=MESH)

        cp = _Lazy(mine=lambda: [pltpu.make_async_copy(ins[a], blk(a, me), local_sems.at[a]) for a in range(self.n)])
        for a in range(self.n):
            cp[a] = _Lazy(
                own=lambda a=a: [copy(a, 0, blk(a, me), sib, src=ins[a]), copy(a, 1, blk(a, me), xn, src=ins[a]),
                                 copy(a, 2, blk(a, me), yn, src=ins[a])],
                from_x=lambda a=a: copy(a, 1, blk(a, xn), me), from_y=lambda a=a: copy(a, 2, blk(a, yn), me),
                relay_x=lambda a=a: copy(a, 3, blk(a, xn, 0), yn), relay_y=lambda a=a: copy(a, 4, blk(a, yn, 1), xn),
                diag0=lambda a=a: copy(a, 3, blk(a, dg, 0), me), diag1=lambda a=a: copy(a, 4, blk(a, dg, 1), me),
                pass_x=lambda a=a: copy(a, 5, blk(a, xn), sib), pass_y=lambda a=a: copy(a, 6, blk(a, yn), sib),
                pass_d0=lambda a=a: copy(a, 7, blk(a, dg, 0), sib), pass_d1=lambda a=a: copy(a, 8, blk(a, dg, 1), sib),
                from_sib=lambda a=a: [copy(a, 0, blk(a, sib), me), copy(a, 5, blk(a, other(xn)), me),
                                      copy(a, 6, blk(a, other(yn)), me), copy(a, 7, blk(a, other(dg), 0), me),
                                      copy(a, 8, blk(a, other(dg), 1), me)])
        return cp

    def start(self, ins, outs, sems):
        cp = self._plan(ins, outs, sems)
        for c in cp["mine"]:
            c.start()
        for a in range(self.n):
            for c in cp[a]["own"]:
                c.start()

    def pass_on(self, ins, outs, sems):
        cp = self._plan(ins, outs, sems)
        for a in range(self.n):
            cp[a]["from_x"].wait_recv()
            cp[a]["relay_x"].start()
            cp[a]["pass_x"].start()
        for a in range(self.n):
            cp[a]["from_y"].wait_recv()
            cp[a]["relay_y"].start()
            cp[a]["pass_y"].start()

    def pass_on_relayed(self, ins, outs, sems):
        cp = self._plan(ins, outs, sems)
        for a in range(self.n):
            cp[a]["diag0"].wait_recv()
            cp[a]["pass_d0"].start()
            cp[a]["diag1"].wait_recv()
            cp[a]["pass_d1"].start()

    def finish(self, ins, outs, sems):
        if self.pass_on_at is None:
            self.pass_on(ins, outs, sems)
            self.pass_on_relayed(ins, outs, sems)
        cp = self._plan(ins, outs, sems)
        for a in range(self.n):
            for c in cp[a]["from_sib"]:
                c.wait_recv()
            for c in cp[a]["own"] + [cp[a][k] for k in ("relay_x", "relay_y", "pass_x", "pass_y", "pass_d0", "pass_d1")]:
                c.wait_send()
        for c in cp["mine"]:
            c.wait()


class _Scatter:
    def __init__(self, grads):
        self.grads = list(grads)
        self.n = len(self.grads)

    def inputs(self):
        return self.grads

    def out_shape(self):
        return [jax.ShapeDtypeStruct((N_DEV - 1,) + g.shape[1:], g.dtype) for g in self.grads]

    def scratch(self):
        return [pltpu.SemaphoreType.DMA((7 * self.n,)), pltpu.SemaphoreType.DMA((7 * self.n,))]

    def _plan(self, ins, outs, sems):
        send_sems, recv_sems = sems
        x, y, c, _ = _position()
        cps = []
        for a in range(self.n):
            for k in range(1, N_DEV):
                peer = (x ^ (k >> 2), y ^ ((k >> 1) & 1), c ^ (k & 1))
                cps.append(pltpu.make_async_remote_copy(
                    src_ref=ins[a].at[_slot(peer)], dst_ref=outs[a].at[k - 1],
                    send_sem=send_sems.at[7 * a + k - 1], recv_sem=recv_sems.at[7 * a + k - 1],
                    device_id=peer, device_id_type=MESH))
        return cps

    def start(self, ins, outs, sems):
        for cp in self._plan(ins, outs, sems):
            cp.start()

    def finish(self, ins, outs, sems):
        for cp in self._plan(ins, outs, sems):
            cp.wait()


def _hosted(inner, n_in, n_out, comm, grid):
    if comm is None:
        return inner
    nc_in, nc_out, ns = len(comm.inputs()), len(comm.out_shape()), len(comm.scratch())

    def body(*refs):
        o0 = n_in + nc_in
        s0 = o0 + n_out + nc_out
        main = refs[:n_in] + refs[o0:o0 + n_out] + refs[s0:len(refs) - ns]
        c_in, c_out, c_sems = refs[n_in:o0], refs[o0 + n_out:s0], refs[len(refs) - ns:]
        ids = [pl.program_id(ax) for ax in range(len(grid))]
        first = functools.reduce(jnp.logical_and, [i == 0 for i in ids])
        last = functools.reduce(jnp.logical_and, [i == g - 1 for i, g in zip(ids, grid)])

        @pl.when(first)
        def _():
            comm.start(c_in, c_out, c_sems)

        inner(*main)

        if getattr(comm, "pass_on_at", None) is not None:
            assert len(grid) == 1
            first_at, second_at = (min(grid[0] - 1, int(frac * grid[0])) for frac in comm.pass_on_at)
            assert first_at < second_at

            @pl.when(ids[0] == first_at)
            def _():
                comm.pass_on(c_in, c_out, c_sems)

            @pl.when(ids[0] == second_at)
            def _():
                comm.pass_on_relayed(c_in, c_out, c_sems)

        @pl.when(last)
        def _():
            comm.finish(c_in, c_out, c_sems)

    return body


def _call(inner, name, grid, in_specs, out_specs, out_shape, scratch, args, comm=None):
    n_in, n_out = len(args), len(out_shape)
    body = _hosted(inner, n_in, n_out, comm, grid)
    if comm is not None:
        in_specs = list(in_specs) + [_any()] * len(comm.inputs())
        args = list(args) + comm.inputs()
        out_specs = list(out_specs) + [_any()] * len(comm.out_shape())
        out_shape = list(out_shape) + comm.out_shape()
        scratch = list(scratch) + comm.scratch()
    outs = pl.pallas_call(
        body, name=name, grid=grid, in_specs=list(in_specs), out_specs=list(out_specs), out_shape=list(out_shape),
        scratch_shapes=list(scratch), compiler_params=_params(len(grid)))(*args)
    return list(outs[:n_out]), list(outs[n_out:])


class _Bcast:
    def __init__(self, block):
        self.block = block

    def inputs(self):
        return [self.block]

    def out_shape(self):
        return [jax.ShapeDtypeStruct((N_DEV,) + self.block.shape, self.block.dtype)]

    def scratch(self):
        return [pltpu.SemaphoreType.DMA((N_DEV - 1,)), pltpu.SemaphoreType.DMA((N_DEV - 1,)),
                pltpu.SemaphoreType.DMA((1,))]

    def _plan(self, ins, outs, sems):
        send_sems, recv_sems, local_sem = sems
        x, y, c, _ = _position()
        mine = outs[0].at[_slot((x, y, c))]
        cps = []
        for k in range(1, N_DEV):
            peer = (x ^ (k >> 2), y ^ ((k >> 1) & 1), c ^ (k & 1))
            cps.append(pltpu.make_async_remote_copy(
                src_ref=ins[0], dst_ref=mine, send_sem=send_sems.at[k - 1], recv_sem=recv_sems.at[k - 1],
                device_id=peer, device_id_type=MESH))
        return pltpu.make_async_copy(ins[0], mine, local_sem.at[0]), cps

    def start(self, ins, outs, sems):
        own, cps = self._plan(ins, outs, sems)
        own.start()
        for cp in cps:
            cp.start()

    def finish(self, ins, outs, sems):
        own, cps = self._plan(ins, outs, sems)
        for cp in cps:
            cp.wait()
        own.wait()


def _first_gather(shards, small_idx, x2, t2, n_meta, tp):
    comm = _Gather(shards)
    n = comm.n
    seq, d = x2.shape
    t_real = n_meta + seq
    n_pad = tp - t_real
    cw = d // N_DEV
    rows = STAGE_ROWS if seq % STAGE_ROWS == 0 else seq
    n_chunks = seq // rows

    def body(*refs):
        ins, (x_ref, t_ref) = refs[:n], refs[n:n + 2]
        outs, (h0_ref, tg_ref) = refs[n + 2:2 * n + 2], refs[2 * n + 2:2 * n + 4]
        sems = refs[2 * n + 4:2 * n + 7]
        buf, zeros, in_sems, out_sems, misc_sems = refs[2 * n + 7:]
        comm.start(ins, outs, sems)
        zeros[...] = jnp.zeros_like(zeros)
        fills = [pltpu.make_async_copy(zeros.at[pl.ds(0, n_pad)], h0_ref.at[pl.ds(t_real, n_pad)], misc_sems.at[0]),
                 pltpu.make_async_copy(zeros.at[pl.ds(0, n_pad)], tg_ref.at[pl.ds(t_real, n_pad)], misc_sems.at[1]),
                 pltpu.make_async_copy(zeros.at[pl.ds(0, n_meta)], tg_ref.at[pl.ds(0, n_meta)], misc_sems.at[2])]
        for cp in fills:
            cp.start()
        jobs = [(src, dst, c) for src, dst in ((x_ref, h0_ref), (t_ref, tg_ref)) for c in range(n_chunks)]

        def load(k):
            src, _, c = jobs[k]
            return pltpu.make_async_copy(src.at[pl.ds(c * rows, rows)], buf.at[k % 2], in_sems.at[k % 2])

        def store(k):
            _, dst, c = jobs[k]
            return pltpu.make_async_copy(buf.at[k % 2], dst.at[pl.ds(n_meta + c * rows, rows)], out_sems.at[k % 2])

        load(0).start()
        for k in range(len(jobs)):
            load(k).wait()
            if k + 1 < len(jobs):
                if k >= 1:
                    store(k - 1).wait()
                load(k + 1).start()
            store(k).start()
        for k in range(max(0, len(jobs) - 2), len(jobs)):
            store(k).wait()
        comm.finish(ins, outs, sems)
        meta = [pltpu.make_async_copy(outs[small_idx].at[k, pl.ds(0, n_meta)],
                                      h0_ref.at[pl.ds(0, n_meta), pl.ds(k * cw, cw)], misc_sems.at[3 + k])
                for k in range(N_DEV)]
        for cp in meta:
            cp.start()
        for cp in fills + meta:
            cp.wait()

    staged = [jax.ShapeDtypeStruct((tp, d), F32)] * 2
    outs = pl.pallas_call(
        body, name="weights_all_gather", out_shape=comm.out_shape() + staged,
        in_specs=[_any()] * (n + 2), out_specs=[_any()] * (n + 2),
        scratch_shapes=comm.scratch() + [
            pltpu.VMEM((2, rows, d), F32), pltpu.VMEM((max(n_pad, n_meta), d), F32),
            pltpu.SemaphoreType.DMA((2,)), pltpu.SemaphoreType.DMA((2,)), pltpu.SemaphoreType.DMA((3 + N_DEV,))],
        compiler_params=pltpu.CompilerParams(vmem_limit_bytes=V7X_VMEM_LIMIT),
    )(*shards, x2, t2)
    return outs[:n], outs[n], outs[n + 1]


def _pair_exchange(grads):
    n = len(grads)

    def body(*refs):
        ins, outs = refs[:n], refs[n:2 * n]
        send_sems, recv_sems = refs[2 * n:]
        x, y, c, _ = _position()
        cps = [pltpu.make_async_remote_copy(
            src_ref=ins[a].at[:, 1 - c], dst_ref=outs[a],
            send_sem=send_sems.at[a], recv_sem=recv_sems.at[a],
            device_id=(x, y, 1 - c), device_id_type=MESH) for a in range(n)]
        for cp in cps:
            cp.start()
        for cp in cps:
            cp.wait()

    return pl.pallas_call(
        body, name="grads_pair_exchange",
        out_shape=[jax.ShapeDtypeStruct((4,) + g.shape[2:], g.dtype) for g in grads],
        in_specs=[_any()] * n, out_specs=[_any()] * n,
        scratch_shapes=[pltpu.SemaphoreType.DMA((n,)), pltpu.SemaphoreType.DMA((n,))],
    )(*grads)


def _chip_copies(c_ref, land_ref, sems):
    _, _, c, chips = _position()
    return [pltpu.make_async_remote_copy(
        src_ref=c_ref.at[2 * cx + cy], dst_ref=land_ref.at[j], send_sem=sems[j], recv_sem=sems[3 + j],
        device_id=(cx, cy, c), device_id_type=MESH) for j, (cx, cy) in enumerate(chips)]


def _scatter_copies(g_ref, land_ref, sems):
    x, y, c, _ = _position()
    cps = []
    for k in range(1, N_DEV):
        peer = (x ^ (k >> 2), y ^ ((k >> 1) & 1), c ^ (k & 1))
        cps.append(pltpu.make_async_remote_copy(
            src_ref=g_ref.at[_slot(peer)], dst_ref=land_ref.at[k - 1], send_sem=sems[k - 1],
            recv_sem=sems[N_DEV - 1 + k - 1], device_id=peer, device_id_type=MESH))
    return cps


def _exchange_start(name, copies, n_copies, src):
    hbm = pl.BlockSpec(memory_space=pltpu.HBM)
    sem = pl.BlockSpec(memory_space=pltpu.SEMAPHORE)
    n_sems = 2 * n_copies

    def body(s_ref, land_ref, *refs):
        for cp in copies(s_ref, land_ref, refs[:n_sems]):
            cp.start()
        token = refs[n_sems + 2]
        token[...] = jnp.zeros_like(token)

    land = lax.empty((n_copies,) + src.shape[1:], src.dtype)
    outs = pl.pallas_call(
        body, name=name + "_start",
        out_shape=(pltpu.SemaphoreType.DMA(()),) * n_sems
        + (pltpu.HBM(src.shape, src.dtype), pltpu.HBM(land.shape, land.dtype),
           jax.ShapeDtypeStruct((SUBLANES, 128), F32)),
        in_specs=(hbm, hbm), out_specs=(sem,) * n_sems + (hbm, hbm, pl.BlockSpec(memory_space=pltpu.VMEM)),
        input_output_aliases={0: n_sems, 1: n_sems + 1},
        compiler_params=pltpu.CompilerParams(has_side_effects=pltpu.SideEffectType.DATAFLOW_SIDE_EFFECTING),
    )(pltpu.with_memory_space_constraint(src, pltpu.HBM), pltpu.with_memory_space_constraint(land, pltpu.HBM))
    return outs[:n_sems], outs[n_sems], outs[n_sems + 1], outs[n_sems + 2]


def _exchange_wait(name, copies, sems, src_thru, land_thru, after):
    hbm = pl.BlockSpec(memory_space=pltpu.HBM)
    sem = pl.BlockSpec(memory_space=pltpu.SEMAPHORE)
    n_sems = len(sems)

    def body(s_ref, land_ref, *refs):
        for cp in copies(s_ref, land_ref, refs[:n_sems]):
            cp.wait_send()
            cp.wait_recv()

    return pl.pallas_call(
        body, name=name + "_wait",
        out_shape=(pltpu.HBM(src_thru.shape, src_thru.dtype), pltpu.HBM(land_thru.shape, land_thru.dtype)),
        in_specs=(hbm, hbm) + (sem,) * n_sems + (pl.BlockSpec(memory_space=pl.ANY),), out_specs=(hbm, hbm),
        input_output_aliases={0: 0, 1: 1},
        compiler_params=pltpu.CompilerParams(has_side_effects=pltpu.SideEffectType.DATAFLOW_SIDE_EFFECTING),
    )(src_thru, land_thru, *sems, after)


def _pair_add(grad, recv, core):
    blk = grad.shape[2:]
    zeros = (0,) * len(blk)

    def body(core_ref, g_ref, r_ref, o_ref):
        del core_ref
        o_ref[...] = (g_ref[...].astype(F32) + r_ref[...].astype(F32)).astype(BF16)

    return pl.pallas_call(
        body, name="grads_pair_add",
        out_shape=jax.ShapeDtypeStruct((4,) + blk, BF16),
        grid_spec=pltpu.PrefetchScalarGridSpec(
            num_scalar_prefetch=1, grid=(4,),
            in_specs=[pl.BlockSpec((None, None) + blk, lambda i, cr: (i, cr[0]) + zeros),
                      pl.BlockSpec((None,) + blk, lambda i, cr: (i,) + zeros)],
            out_specs=pl.BlockSpec((None,) + blk, lambda i, cr: (i,) + zeros)),
        compiler_params=_params(1),
    )(core, grad, recv)


def _adamw(w, g, m, v):
    m2 = ADAM_B1 * m + (1.0 - ADAM_B1) * g
    v2 = ADAM_B2 * v + (1.0 - ADAM_B2) * (g * g)
    m_hat = m2 / (1.0 - ADAM_B1 ** ADAM_STEP)
    v_hat = v2 / (1.0 - ADAM_B2 ** ADAM_STEP)
    delta = -ADAM_LR * (m_hat / (jnp.sqrt(v_hat) + ADAM_EPS) + ADAM_WD * w)
    return delta, m2, v2


def _final_adamw(own, recv, idx, parts, after):
    blk = own.shape[1:]
    n_recv = recv.shape[0]
    n_parts = len(parts)
    per = blk[0] // n_parts if n_parts > 1 else None
    rows = blk[-2]
    n_chunks = 1 if n_parts > 1 else (4 if rows % 64 == 0 and rows >= 512 else (2 if rows % 32 == 0 else 1))
    cblk = blk[:-2] + (rows // n_chunks, blk[-1])
    lead = (0,) * (len(blk) - 2)

    def body(idx_ref, c_ref, r_ref, after_ref, *refs):
        del idx_ref, after_ref
        ins, outs = refs[:3 * n_parts], refs[3 * n_parts:]
        g = c_ref[...].astype(F32)
        for k in range(n_recv):
            g = g + r_ref[k].astype(F32)
        for p in range(n_parts):
            w_ref, m_ref, v_ref = ins[3 * p:3 * p + 3]
            if n_parts == 1:
                gp = g
            elif per == 1:
                gp = g[p]
            else:
                gp = g[p * per:(p + 1) * per]
            delta, m2, v2 = _adamw(w_ref[0], gp, m_ref[0], v_ref[0])
            o = outs[4 * p:4 * p + 4]
            o[0][0] = gp
            o[1][0] = delta
            o[2][0] = m2
            o[3][0] = v2

    flat = [a for wmv in parts for a in wmv]

    def part_spec(a):
        shape = a.shape[:-2] + (a.shape[-2] // n_chunks, a.shape[-1])
        return pl.BlockSpec(shape, lambda i, cr, nd=a.ndim: (0,) * (nd - 2) + (i, 0))

    outs = pl.pallas_call(
        body, name="grads_sum_adamw",
        out_shape=[jax.ShapeDtypeStruct(wmv[0].shape, F32) for wmv in parts for _ in range(4)],
        grid_spec=pltpu.PrefetchScalarGridSpec(
            num_scalar_prefetch=1, grid=(n_chunks,),
            in_specs=[pl.BlockSpec((None,) + cblk, lambda i, cr: (cr[0],) + lead + (i, 0)),
                      pl.BlockSpec((n_recv,) + cblk, lambda i, cr: (0,) + lead + (i, 0))]
                     + [_any()] + [part_spec(a) for a in flat],
            out_specs=[part_spec(wmv[0]) for wmv in parts for _ in range(4)]),
        compiler_params=_params(1),
    )(idx, own, recv, after, *flat)
    return [tuple(outs[4 * p:4 * p + 4]) for p in range(n_parts)]


def _small_adamw(partials, layout, me_index):
    _, rows, d = partials.shape
    n = len(layout)
    cw = d // N_DEV

    def body(me_ref, p_ref, *refs):
        ins, t_ref, outs = refs[:3 * n], refs[3 * n], refs[3 * n + 1:]
        me = me_ref[0]
        total = p_ref[0]
        for j in range(1, N_DEV):
            total = total + p_ref[j]
        t_ref[...] = total
        for e, (kind, r0, nr, _, _, _) in enumerate(layout):
            w_ref, m_ref, v_ref = ins[3 * e:3 * e + 3]
            o = outs[4 * e:4 * e + 4]
            if kind == "rep":
                g = t_ref[r0:r0 + nr, :]
                delta, m2, v2 = _adamw(w_ref[...], g, m_ref[...], v_ref[...])
                for ref, val in zip(o, (g, delta, m2, v2)):
                    ref[...] = val
            elif kind == "wide":
                for q in range(nr):
                    sl = slice(q * d, (q + 1) * d)
                    g = t_ref[r0 + q:r0 + q + 1, :]
                    delta, m2, v2 = _adamw(w_ref[:, sl], g, m_ref[:, sl], v_ref[:, sl])
                    for ref, val in zip(o, (g, delta, m2, v2)):
                        ref[:, sl] = val
            else:
                for j in range(N_DEV):
                    @pl.when(me == j)
                    def _(j=j, o=o, w_ref=w_ref, m_ref=m_ref, v_ref=v_ref, r0=r0, nr=nr):
                        g = t_ref[r0:r0 + nr, j * cw:(j + 1) * cw]
                        delta, m2, v2 = _adamw(w_ref[...], g, m_ref[...], v_ref[...])
                        for ref, val in zip(o, (g, delta, m2, v2)):
                            ref[...] = val

    flat = [a for ent in layout for a in ent[3:]]
    vm = pl.BlockSpec(memory_space=pltpu.VMEM)
    outs = pl.pallas_call(
        body, name="small_adamw",
        out_shape=[jax.ShapeDtypeStruct((rows, d), F32)]
                  + [jax.ShapeDtypeStruct(ent[3].shape, F32) for ent in layout for _ in range(4)],
        in_specs=[pl.BlockSpec(memory_space=pltpu.SMEM), vm] + [vm] * len(flat),
        out_specs=[vm] * (1 + 4 * n),
        compiler_params=pltpu.CompilerParams(vmem_limit_bytes=V7X_VMEM_LIMIT),
    )(me_index, partials, *flat)
    return outs[0], [tuple(outs[1 + 4 * e:5 + 4 * e]) for e in range(n)]


def _ffn_fwd(h, g, wgu, wd, tm, loss=None, comm=None):
    tp, d = h.shape
    f = wd.shape[0]
    fc = f // FFN_FWD_CHUNKS
    nt = tp // tm
    with_loss = loss is not None
    if with_loss:
        tgt, gf, n_meta, t_real = loss

    def body(*refs):
        if with_loss:
            (h_ref, g_ref, wgu_hbm, wd_hbm, tgt_ref, gf_ref, out_ref, gu_ref, n_ref, tail_ref,
             wgu_v, wd_v, sems) = refs
        else:
            h_ref, g_ref, wgu_hbm, wd_hbm, out_ref, gu_ref, n_ref, wgu_v, wd_v, sems = refs
        i = pl.program_id(0)

        @pl.when(i == 0)
        def _():
            _load_weights([(wgu_hbm, wgu_v), (wd_hbm, wd_v)], sems)
            if with_loss:
                tail_ref[...] = jnp.zeros_like(tail_ref)

        x = h_ref[...]
        n, _ = _rms_fwd(x, g_ref[...])
        nb = n.astype(BF16)
        n_ref[...] = nb
        acc = jnp.zeros((tm, d), F32)
        for j in range(FFN_FWD_CHUNKS):
            cols = slice(j * fc, (j + 1) * fc)
            gate = _nt(nb, wgu_v[pl.ds(j * fc, fc), :])
            up = _nt(nb, wgu_v[pl.ds(f + j * fc, fc), :])
            gu_ref[0, :, cols] = gate.astype(BF16)
            gu_ref[1, :, cols] = up.astype(BF16)
            act = (gate * _sigmoid(gate) * up).astype(BF16)
            acc = acc + _nn(act, wd_v[pl.ds(j * fc, fc), :])
        hn = x + FFN_RES * acc
        if not with_loss:
            out_ref[...] = hn
        else:
            gfv = gf_ref[...]
            r = lax.rsqrt(jnp.mean(hn * hn, axis=-1, keepdims=True) + EPS)
            xr = hn * r
            rows = i * tm + lax.broadcasted_iota(jnp.int32, (tm, 1), 0)
            mask = jnp.logical_and(rows >= n_meta, rows < t_real)
            diff = jnp.where(mask, xr * gfv - tgt_ref[...], 0.0)
            tail_ref[TAIL_LOSS:TAIL_LOSS + 1, :] += jnp.zeros((1, d), F32) + 0.5 * jnp.sum(diff * diff) / d
            dy = diff / d
            gy = dy * gfv
            out_ref[...] = r * (gy - xr * jnp.mean(gy * xr, axis=-1, keepdims=True))
            tail_ref[TAIL_FINAL:TAIL_FINAL + 1, :] += _rowsum(dy * xr)

    row = pl.BlockSpec((tm, d), lambda i: (i, 0))
    vec = pl.BlockSpec((1, d), lambda i: (0, 0))
    in_specs = [row, vec, _any(), _any()]
    out_shape = [jax.ShapeDtypeStruct((tp, d), F32), jax.ShapeDtypeStruct((2, tp, f), BF16),
                 jax.ShapeDtypeStruct((tp, d), BF16)]
    out_specs = [row, pl.BlockSpec((2, tm, f), lambda i: (0, i, 0)), row]
    args = [h, g, wgu, wd]
    if with_loss:
        in_specs += [row, vec]
        out_shape += [jax.ShapeDtypeStruct((SUBLANES, d), F32)]
        out_specs += [pl.BlockSpec((SUBLANES, d), lambda i: (0, 0))]
        args += [tgt, gf]
    return _call(body, "ffn_fwd_loss" if with_loss else "ffn_fwd", (nt,), in_specs, out_specs, out_shape,
                 [pltpu.VMEM((2 * f, d), BF16), pltpu.VMEM((f, d), BF16), pltpu.SemaphoreType.DMA((2,))],
                 args, comm)


def _ffn_bwd(dh, h, gu, g, wgu, wd, tm, tail, tail_row, after):
    tp, d = h.shape
    f = wd.shape[0]
    fc = f // FFN_CHUNKS
    nt = tp // tm

    def body(dh_ref, h_ref, gu_ref, g_ref, tail_ref, wgu_hbm, wd_hbm, after_ref,
             dhin_ref, dgu_ref, act_ref, df_ref, dg_ref, wgu_v, wd_v, dn_v, sems):
        del after_ref
        i, j = pl.program_id(0), pl.program_id(1)

        @pl.when(jnp.logical_and(i == 0, j == 0))
        def _():
            _load_weights([(wgu_hbm, wgu_v), (wd_hbm, wd_v)], sems)
            dg_ref[...] = tail_ref[...]

        dfb = (FFN_RES * dh_ref[...]).astype(BF16)

        @pl.when(j == 0)
        def _():
            df_ref[...] = dfb
            dn_v[...] = jnp.zeros_like(dn_v)

        lo = pl.multiple_of(j * fc, 16)
        dact = _nt(dfb, wd_v[pl.ds(lo, fc), :])
        gate = gu_ref[0].astype(F32)
        up = gu_ref[1].astype(F32)
        sg = _sigmoid(gate)
        silu = gate * sg
        act_ref[...] = (silu * up).astype(BF16)
        dgate = (dact * up * (sg * (1.0 + gate * (1.0 - sg)))).astype(BF16)
        dup = (dact * silu).astype(BF16)
        dgu_ref[0] = dgate
        dgu_ref[1] = dup
        dn_v[...] += _nn(dgate, wgu_v[pl.ds(lo, fc), :]) + _nn(dup, wgu_v[pl.ds(pl.multiple_of(f + j * fc, 16), fc), :])

        @pl.when(j == FFN_CHUNKS - 1)
        def _():
            x = h_ref[...]
            r = lax.rsqrt(jnp.mean(x * x, axis=-1, keepdims=True) + EPS)
            dx, dgp = _rms_bwd(dn_v[...], x, r, g_ref[...])
            dhin_ref[...] = dh_ref[...] + dx
            dg_ref[tail_row:tail_row + 1, :] += dgp

    row = pl.BlockSpec((tm, d), lambda i, j: (i, 0))
    vec = pl.BlockSpec((1, d), lambda i, j: (0, 0))
    tile = pl.BlockSpec((SUBLANES, d), lambda i, j: (0, 0))
    hid2 = pl.BlockSpec((2, tm, fc), lambda i, j: (0, i, j))
    return _call(
        body, "ffn_bwd", (nt, FFN_CHUNKS),
        [row, row, hid2, vec, tile, _any(), _any(), _any()],
        [row, hid2, pl.BlockSpec((tm, fc), lambda i, j: (i, j)), row, tile],
        [jax.ShapeDtypeStruct((tp, d), F32), jax.ShapeDtypeStruct((2, tp, f), BF16),
         jax.ShapeDtypeStruct((tp, f), BF16), jax.ShapeDtypeStruct((tp, d), BF16),
         jax.ShapeDtypeStruct((SUBLANES, d), F32)],
        [pltpu.VMEM((2 * f, d), BF16), pltpu.VMEM((f, d), BF16), pltpu.VMEM((tm, d), F32),
         pltpu.SemaphoreType.DMA((2,))],
        [dh, h, gu, g, tail, wgu, wd, after])


def _piece_segments(q, d, nb_cols):
    segs = []
    for j in range(N_DEV):
        lo, hi = max(q * d, j * nb_cols), min((q + 1) * d, (j + 1) * nb_cols)
        if lo < hi:
            segs.append((j, lo - q * d, hi - q * d, lo - j * nb_cols, hi - j * nb_cols))
    return segs


def _w3_copies(w3_hbm, rows, w3_v):
    return [(w3_hbm.at[k, pl.ds(q * rows, rows)], w3_v.at[q, pl.ds(k * rows, rows)])
            for q in range(3) for k in range(N_DEV)]


def _gates(xrb, wg_ref, ba, bx, lam, hd):
    pre_r, pre_i = [], []
    for hh in range(N_HEADS):
        xh = xrb[:, hh * hd:(hh + 1) * hd]
        pre_r.append(_nn(xh, wg_ref[0, hh]))
        pre_i.append(_nn(xh, wg_ref[1, hh]))
    r = _sigmoid(jnp.concatenate(pre_r, axis=1) + ba)
    ig = _sigmoid(jnp.concatenate(pre_i, axis=1) + bx)
    sp = _softplus(-lam)
    log_a = -RG_LRU_C * r * sp
    a = jnp.exp(log_a)
    s = jnp.sqrt(_one_minus_exp(2.0 * log_a))
    return r, ig, sp, a, s


def _scan_fwd(a, u, h_prev):
    tm = a.shape[0]
    rows = lax.broadcasted_iota(jnp.int32, a.shape, 0)
    d = 1
    while d < tm:
        if d < SUBLANES:
            keep = rows >= d
            u = jnp.where(keep, a * pltpu.roll(u, d, 0) + u, u)
            a = jnp.where(keep, a * pltpu.roll(a, d, 0), a)
        else:
            u = jnp.concatenate([u[:d], a[d:] * u[:tm - d] + u[d:]], axis=0)
            a = jnp.concatenate([a[:d], a[d:] * a[:tm - d]], axis=0)
        d *= 2
    return u + a * h_prev


def _scan_bwd(b, v, g_next):
    tm = b.shape[0]
    rows = lax.broadcasted_iota(jnp.int32, b.shape, 0)
    d = 1
    while d < tm:
        if d < SUBLANES:
            keep = rows < tm - d
            v = jnp.where(keep, v + b * pltpu.roll(v, tm - d, 0), v)
            b = jnp.where(keep, b * pltpu.roll(b, tm - d, 0), b)
        else:
            v = jnp.concatenate([v[:tm - d] + b[:tm - d] * v[d:], v[tm - d:]], axis=0)
            b = jnp.concatenate([b[:tm - d] * b[d:], b[tm - d:]], axis=0)
        d *= 2
    return v + b * g_next


def _shifted_copies(ext_ref, es_ref, n_rows):
    for s in range(1, SUBLANES):
        es_ref[s, pl.ds(0, n_rows), :] = ext_ref[pl.ds(s, n_rows), :]


def _tap(ext_ref, es_ref, off, tm):
    q, s = divmod(off, SUBLANES)
    if s == 0:
        return ext_ref[pl.ds(SUBLANES * q, tm), :]
    return es_ref[s, pl.ds(SUBLANES * q, tm), :]


def _mixer_fwd(h, g, b_in, win_all, cw4, cb4, wg, ba, bx, lam, cw31, cb31, lng, lnb, bcp, w3_all, tm, comm=None):
    tp, d = h.shape
    nb_cols = win_all.shape[-1]
    n_in = N_DEV * nb_cols
    hd = wg.shape[-1]
    k4, k31 = cw4.shape[0], cw31.shape[0]
    w3_rows = d // N_DEV

    def body(h_ref, g_ref, b_ref, win_hbm, cw4_ref, cb4_ref, wg_ref, ba_ref, bx_ref, lam_ref, cw31_ref, cb31_ref,
             lng_ref, lnb_ref, bcp_ref, w3_hbm,
             h2_ref, p_ref, n_ref, xr_ref, hs_ref, v1_ref, ya_ref, yb_ref,
             win_v, w3_v, ext4, ext31, es31, hcar, sems):
        @pl.when(pl.program_id(0) == 0)
        def _():
            _load_weights([(win_hbm, win_v)] + _w3_copies(w3_hbm, w3_rows, w3_v), sems)
            ext4[pl.ds(0, CONV4_HALO), :] = jnp.zeros((CONV4_HALO, d), F32)
            ext31[pl.ds(0, CONV31_HALO), :] = jnp.zeros((CONV31_HALO, d), F32)
            hcar[...] = jnp.zeros_like(hcar)

        n, _ = _rms_fwd(h_ref[...], g_ref[...])
        nb = n.astype(BF16)
        n_ref[...] = nb

        def piece(q):
            parts = [_nn(nb, win_v[j, :, bl:bh]) for j, _, _, bl, bh in _piece_segments(q, d, nb_cols)]
            pq = (jnp.concatenate(parts, axis=1) + b_ref[:, q * d:(q + 1) * d]).astype(BF16)
            p_ref[:, q * d:(q + 1) * d] = pq
            return pq.astype(F32)

        x_rnn, y_rnn, glu_v, glu_g, gate_a, gate_b = [piece(q) for q in range(6)]

        ext4[pl.ds(CONV4_HALO, tm), :] = x_rnn
        xr = cb4_ref[...] + jnp.zeros((tm, d), F32)
        for k in range(k4):
            xr = xr + cw4_ref[k:k + 1, :] * ext4[pl.ds(CONV4_HALO - (k4 - 1) + k, tm), :]
        ext4[pl.ds(0, CONV4_HALO), :] = ext4[pl.ds(tm, CONV4_HALO), :]
        xrb = xr.astype(BF16)
        xr_ref[...] = xrb
        xr = xrb.astype(F32)
        _, ig, _, a, s = _gates(xrb, wg_ref, ba_ref[...], bx_ref[...], lam_ref[...], hd)
        hseq = _scan_fwd(a, s * (ig * xr), hcar[0:1, :])
        hcar[0:1, :] = hseq[tm - 1:tm, :]
        hs_ref[...] = hseq.astype(BF16)
        gl, _ = _gelu(y_rnn)
        ya = _nn((hseq * gl).astype(BF16), w3_v[0])
        ya_ref[...] = ya.astype(BF16)

        ext31[pl.ds(CONV31_HALO, tm), :] = glu_v * _sigmoid(glu_g)
        _shifted_copies(ext31, es31, tm + CONV31_HALO - SUBLANES)
        v1 = cb31_ref[...] + jnp.zeros((tm, d), F32)
        for k in range(k31):
            v1 = v1 + cw31_ref[k:k + 1, :] * _tap(ext31, es31, CONV31_HALO - (k31 - 1) + k, tm)
        ext31[pl.ds(0, CONV31_HALO), :] = ext31[pl.ds(tm, CONV31_HALO), :]
        v1b = v1.astype(BF16)
        v1_ref[...] = v1b
        v1 = v1b.astype(F32)
        xc = v1 - jnp.mean(v1, axis=-1, keepdims=True)
        rstd = lax.rsqrt(jnp.mean(xc * xc, axis=-1, keepdims=True) + EPS)
        v2 = xc * rstd * lng_ref[...] + lnb_ref[...]
        yb = _nn((v2 * _sigmoid(v2)).astype(BF16), w3_v[1]) + bcp_ref[...]
        yb_ref[...] = yb.astype(BF16)

        merged = _sigmoid(gate_a) * ya + _sigmoid(gate_b) * yb
        h2_ref[...] = h_ref[...] + _nn(merged.astype(BF16), w3_v[2])

    row = pl.BlockSpec((tm, d), lambda i: (i, 0))
    wide = pl.BlockSpec((tm, n_in), lambda i: (i, 0))
    full = lambda a: pl.BlockSpec(a.shape, lambda i, nd=a.ndim: (0,) * nd)
    smalls = [cw4, cb4, wg, ba, bx, lam, cw31, cb31, lng, lnb, bcp]
    return _call(
        body, "mixer_fwd", (tp // tm,),
        [row, full(g), full(b_in), _any()] + [full(a) for a in smalls] + [_any()],
        [row, wide] + [row] * 6,
        [jax.ShapeDtypeStruct((tp, d), F32), jax.ShapeDtypeStruct((tp, n_in), BF16)]
        + [jax.ShapeDtypeStruct((tp, d), BF16)] * 6,
        [pltpu.VMEM(win_all.shape, BF16),
         pltpu.VMEM((3, d, d), BF16),
         pltpu.VMEM((tm + CONV4_HALO, d), F32),
         pltpu.VMEM((tm + CONV31_HALO, d), F32),
         pltpu.VMEM((SUBLANES, tm + CONV31_HALO, d), F32),
         pltpu.VMEM((SUBLANES, d), F32),
         pltpu.SemaphoreType.DMA((1 + 3 * N_DEV,))],
        [h, g, b_in, win_all, *smalls, w3_all], comm)


SG_BIN, SG_CW4, SG_CB4, SG_BA, SG_BX, SG_LAM, SG_CB31, SG_LNG, SG_LNB, SG_BCP, SG_MIX, SG_CW31 = 0, 6, 10, 11, 12, 13, 14, 15, 16, 17, 18, 19


def _mixer_bwd(dh2, h, g, proj, xr_s, hs_s, v1_s, ya_s, yb_s, win_t, cw4, wg, ba, bx, lam, cw31, lng, lnb, w3_all, tm,
               comm=None):
    tp, d = dh2.shape
    n_in = proj.shape[1]
    hd = wg.shape[-1]
    k4, k31 = cw4.shape[0], cw31.shape[0]
    nt = tp // tm
    w3_rows = d // N_DEV
    sg_rows = -(-(SG_CW31 + k31) // SUBLANES) * SUBLANES
    halo_rows = 16
    per = tm // halo_rows

    def body(dh_ref, h_ref, g_ref, p_ref, xr_ref, hs_ref, hh_ref, v1_ref, ya_ref, yb_ref, win_hbm,
             cw4_ref, wg_ref, wgt_ref, ba_ref, bx_ref, lam_ref, cw31_ref, lng_ref, lnb_ref, w3_hbm,
             dh1_ref, dp_ref, x3_ref, y3_ref, yg_ref, sg_ref,
             win_v, w3_v, extd4, extd31, es31, gcar, sems):
        i = pl.program_id(0)
        tile = nt - 1 - i

        @pl.when(i == 0)
        def _():
            _load_weights([(win_hbm, win_v)] + _w3_copies(w3_hbm, w3_rows, w3_v), sems)
            for q in range(3):
                w3_v[q] = w3_v[q].T
            extd4[pl.ds(tm, CONV4_HALO), :] = jnp.zeros((CONV4_HALO, d), F32)
            extd31[pl.ds(tm, CONV31_HALO), :] = jnp.zeros((CONV31_HALO, d), F32)
            gcar[...] = jnp.zeros_like(gcar)
            sg_ref[...] = jnp.zeros_like(sg_ref)

        def acc(row, val):
            sg_ref[row:row + 1, :] += _rowsum(val)

        rows = lax.broadcasted_iota(jnp.int32, (tm, d), 0)
        x_rnn = p_ref[:, 0:d].astype(F32)
        y_rnn = p_ref[:, d:2 * d].astype(F32)
        glu_v = p_ref[:, 2 * d:3 * d].astype(F32)
        glu_g = p_ref[:, 3 * d:4 * d].astype(F32)
        sga = _sigmoid(p_ref[:, 4 * d:5 * d].astype(F32))
        sgb = _sigmoid(p_ref[:, 5 * d:6 * d].astype(F32))
        ya = ya_ref[...].astype(F32)
        yb = yb_ref[...].astype(F32)

        dmob = dh_ref[...].astype(BF16)
        dmerged = _nn(dmob, w3_v[2])
        x3_ref[:, 0:d] = (sga * ya + sgb * yb).astype(BF16)
        y3_ref[:, 0:d] = dmob
        dya = sga * dmerged
        dyb = sgb * dmerged
        dn_parts = []

        def emit(q, val):
            vb = val.astype(BF16)
            dp_ref[:, q * d:(q + 1) * d] = vb
            acc(SG_BIN + q, val)
            term = _nn(vb, win_v[pl.ds(q * d, d), :])
            dn_parts[:] = [term if not dn_parts else dn_parts[0] + term]

        emit(4, dmerged * ya * sga * (1.0 - sga))
        emit(5, dmerged * yb * sgb * (1.0 - sgb))

        dyab = dya.astype(BF16)
        y3_ref[:, d:2 * d] = dyab
        dza = _nn(dyab, w3_v[0])
        hsv = hs_ref[...].astype(F32)
        gl, th = _gelu(y_rnn)
        x3_ref[:, d:2 * d] = (hsv * gl).astype(BF16)
        emit(1, dza * hsv * _gelu_grad(y_rnn, th))
        dhs = dza * gl
        xrb = xr_ref[...]
        xr = xrb.astype(F32)
        lam_v = lam_ref[...]
        r, ig, sp, a, s = _gates(xrb, wg_ref, ba_ref[...], bx_ref[...], lam_v, hd)
        b = jnp.where(rows == tm - 1, gcar[1:2, :], pltpu.roll(a, tm - 1, 0))
        big_g = _scan_bwd(b, dhs, gcar[0:1, :])
        gcar[0:1, :] = big_g[0:1, :]
        gcar[1:2, :] = a[0:1, :]
        h_before = jnp.where(tile > 0, hh_ref[halo_rows - 1:halo_rows, :].astype(F32), 0.0)
        h_prev = jnp.where(rows == 0, h_before, pltpu.roll(hsv, 1, 0))
        ds = big_g * ig * xr
        dla = big_g * h_prev * a - ds * (a * a) / jnp.maximum(s, 1e-20)
        acc(SG_LAM, dla * r * (RG_LRU_C * _sigmoid(-lam_v)))
        dpr = dla * (-RG_LRU_C * sp) * r * (1.0 - r)
        dpi = big_g * s * xr * ig * (1.0 - ig)
        acc(SG_BA, dpr)
        acc(SG_BX, dpi)
        dprb = dpr.astype(BF16)
        dpib = dpi.astype(BF16)
        yg_ref[:, 0:d] = dprb
        yg_ref[:, d:2 * d] = dpib
        back = []
        for hh in range(N_HEADS):
            sl = slice(hh * hd, (hh + 1) * hd)
            back.append(_nn(dprb[:, sl], wgt_ref[0, hh]) + _nn(dpib[:, sl], wgt_ref[1, hh]))
        dxr = big_g * s * ig + jnp.concatenate(back, axis=1)
        acc(SG_CB4, dxr)
        extd4[pl.ds(0, tm), :] = dxr
        dx_rnn = jnp.zeros((tm, d), F32)
        for k in range(k4):
            term = extd4[pl.ds(k4 - 1 - k, tm), :]
            dx_rnn = dx_rnn + cw4_ref[k:k + 1, :] * term
            acc(SG_CW4 + k, x_rnn * term)
        extd4[pl.ds(tm, CONV4_HALO), :] = extd4[pl.ds(0, CONV4_HALO), :]
        emit(0, dx_rnn)

        dybb = dyb.astype(BF16)
        y3_ref[:, 2 * d:3 * d] = dybb
        acc(SG_BCP, dyb)
        dv3 = _nn(dybb, w3_v[1])
        v1 = v1_ref[...].astype(F32)
        xc = v1 - jnp.mean(v1, axis=-1, keepdims=True)
        rstd = lax.rsqrt(jnp.mean(xc * xc, axis=-1, keepdims=True) + EPS)
        xhat = xc * rstd
        lng_v = lng_ref[...]
        v2 = xhat * lng_v + lnb_ref[...]
        s2 = _sigmoid(v2)
        x3_ref[:, 2 * d:3 * d] = (v2 * s2).astype(BF16)
        dv2 = dv3 * (s2 * (1.0 + v2 * (1.0 - s2)))
        acc(SG_LNG, dv2 * xhat)
        acc(SG_LNB, dv2)
        dxh = dv2 * lng_v
        dv1 = rstd * (dxh - jnp.mean(dxh, axis=-1, keepdims=True)
                      - xhat * jnp.mean(dxh * xhat, axis=-1, keepdims=True))
        acc(SG_CB31, dv1)
        extd31[pl.ds(0, tm), :] = dv1
        _shifted_copies(extd31, es31, tm + CONV31_HALO - SUBLANES)
        sgg = _sigmoid(glu_g)
        v0 = glu_v * sgg
        dv0 = jnp.zeros((tm, d), F32)
        for k in range(k31):
            term = _tap(extd31, es31, k31 - 1 - k, tm)
            dv0 = dv0 + cw31_ref[k:k + 1, :] * term
            acc(SG_CW31 + k, v0 * term)
        extd31[pl.ds(tm, CONV31_HALO), :] = extd31[pl.ds(0, CONV31_HALO), :]
        emit(2, dv0 * sgg)
        emit(3, dv0 * glu_v * sgg * (1.0 - sgg))

        dn = dn_parts[0]
        x = h_ref[...]
        rr = lax.rsqrt(jnp.mean(x * x, axis=-1, keepdims=True) + EPS)
        dx, dgp = _rms_bwd(dn, x, rr, g_ref[...])
        dh1_ref[...] = dh_ref[...] + dx
        sg_ref[SG_MIX:SG_MIX + 1, :] += dgp

    rev = lambda i: (nt - 1 - i, 0)
    row = pl.BlockSpec((tm, d), rev)
    wide = pl.BlockSpec((tm, n_in), rev)
    full = lambda a: pl.BlockSpec(a.shape, lambda i, nd=a.ndim: (0,) * nd)
    halo = pl.BlockSpec((halo_rows, d), lambda i: (jnp.maximum((nt - 1 - i) * per - 1, 0), 0))
    smalls = [cw4, wg, jnp.swapaxes(wg, 2, 3), ba, bx, lam, cw31, lng, lnb]
    return _call(
        body, "mixer_bwd", (nt,),
        [row, row, full(g), wide, row, row, halo, row, row, row, _any()]
        + [full(a) for a in smalls] + [_any()],
        [row, wide, pl.BlockSpec((tm, 3 * d), rev), pl.BlockSpec((tm, 3 * d), rev),
         pl.BlockSpec((tm, 2 * d), rev), pl.BlockSpec((sg_rows, d), lambda i: (0, 0))],
        [jax.ShapeDtypeStruct((tp, d), F32), jax.ShapeDtypeStruct((tp, n_in), BF16),
         jax.ShapeDtypeStruct((tp, 3 * d), BF16), jax.ShapeDtypeStruct((tp, 3 * d), BF16),
         jax.ShapeDtypeStruct((tp, 2 * d), BF16), jax.ShapeDtypeStruct((sg_rows, d), F32)],
        [pltpu.VMEM(win_t.shape, BF16),
         pltpu.VMEM((3, d, d), BF16),
         pltpu.VMEM((tm + CONV4_HALO, d), F32),
         pltpu.VMEM((tm + CONV31_HALO, d), F32),
         pltpu.VMEM((SUBLANES, tm + CONV31_HALO, d), F32),
         pltpu.VMEM((SUBLANES, d), F32),
         pltpu.SemaphoreType.DMA((1 + 3 * N_DEV,))],
        [dh2, h, g, proj, xr_s, hs_s, hs_s, v1_s, ya_s, yb_s, win_t, *smalls, w3_all], comm)


def _tn_matmul(name, x, y, x_spec, y_spec, n_blocks, kb, nb, tm, tp, out_shape, out_spec, out_view, comm=None):
    nt = tp // tm

    def body(x_ref, y_ref, o_ref, acc):
        i = pl.program_id(1)

        @pl.when(i == 0)
        def _():
            acc[...] = jnp.zeros_like(acc)

        acc[...] += _tn(x_ref[...], y_ref[...])

        @pl.when(i == nt - 1)
        def _():
            o_ref[...] = acc[...].astype(BF16).reshape(out_view)

    outs, extra = _call(body, name, (n_blocks, nt), [x_spec, y_spec], [out_spec],
                        [jax.ShapeDtypeStruct(out_shape, BF16)], [pltpu.VMEM((kb, nb), F32)], [x, y], comm)
    return outs[0], extra


def kernel(x, meta_tokens, ffn1_norm, ffn1_w_gu, ffn1_w_down, mix_norm, w_in, b_in, rnn_conv_w, rnn_conv_b, rg_w_a, rg_b_a, rg_w_x, rg_b_x, rg_lambda, rnn_w_proj, conv_dw_w, conv_dw_b, conv_ln_g, conv_ln_b, conv_w_proj, conv_b_proj, w_out, ffn2_norm, ffn2_w_gu, ffn2_w_down, final_norm, loss_target, m_meta_tokens, m_ffn1_norm, m_ffn1_w_gu, m_ffn1_w_down, m_mix_norm, m_w_in, m_b_in, m_rnn_conv_w, m_rnn_conv_b, m_rg_w_a, m_rg_b_a, m_rg_w_x, m_rg_b_x, m_rg_lambda, m_rnn_w_proj, m_conv_dw_w, m_conv_dw_b, m_conv_ln_g, m_conv_ln_b, m_conv_w_proj, m_conv_b_proj, m_w_out, m_ffn2_norm, m_ffn2_w_gu, m_ffn2_w_down, m_final_norm, v_meta_tokens, v_ffn1_norm, v_ffn1_w_gu, v_ffn1_w_down, v_mix_norm, v_w_in, v_b_in, v_rnn_conv_w, v_rnn_conv_b, v_rg_w_a, v_rg_b_a, v_rg_w_x, v_rg_b_x, v_rg_lambda, v_rnn_w_proj, v_conv_dw_w, v_conv_dw_b, v_conv_ln_g, v_conv_ln_b, v_conv_w_proj, v_conv_b_proj, v_w_out, v_ffn2_norm, v_ffn2_w_gu, v_ffn2_w_down, v_final_norm):
    w = dict(locals())
    seq, d = x.shape[1], x.shape[2]
    n_meta = meta_tokens.shape[0]
    t_real = n_meta + seq
    tp, tm, tmx_fwd, tmx, tmt, tmw = _tiles(t_real)
    fb = ffn1_w_gu.shape[-1]
    wr = ffn1_w_down.shape[1]
    f = N_DEV * wr
    fc = f // FFN_CHUNKS
    nbc = w_in.shape[-1]
    n_in = N_DEV * nbc
    pr = rnn_w_proj.shape[1]
    hd = rg_w_a.shape[-1]
    gr = rg_w_a.shape[2]
    cw = meta_tokens.shape[1]
    k4, k31 = rnn_conv_w.shape[1], conv_dw_w.shape[1]
    assert n_in == 6 * d and 2 * wr == fb and N_HEADS * hd == d and pr * N_DEV == d

    xi, yi, ci = lax.axis_index("x"), lax.axis_index("y"), lax.axis_index("c")
    core = ci.astype(jnp.int32).reshape(1)
    chip = (2 * xi + yi).astype(jnp.int32).reshape(1)
    me_index = (4 * xi + 2 * yi + ci).astype(jnp.int32).reshape(1)

    for nm in ("ffn1_w_gu", "ffn2_w_gu"):
        for pre in ("", "m_", "v_"):
            w[pre + nm] = jnp.swapaxes(w[pre + nm], 1, 2)

    wgut1 = w["ffn1_w_gu"][0].astype(BF16)
    wgut2 = w["ffn2_w_gu"][0].astype(BF16)
    wd1 = ffn1_w_down[0].astype(BF16)
    wd2 = ffn2_w_down[0].astype(BF16)
    win_loc = w_in[0].astype(BF16)
    win_t_loc = jnp.swapaxes(w_in[0], 0, 1).astype(BF16)
    w3_loc = jnp.concatenate([rnn_w_proj[0], conv_w_proj[0], w_out[0]], axis=0).astype(BF16)
    wg_loc = jnp.stack([rg_w_a[0], rg_w_x[0]]).astype(BF16)
    n_small = n_meta + k4 + k31
    small_rows = -(-n_small // SUBLANES) * SUBLANES
    small_loc = jnp.concatenate([meta_tokens, rnn_conv_w[0], conv_dw_w[0],
                                 jnp.zeros((small_rows - n_small, cw), F32)], axis=0)
    (wgut1_all, wd1_all, wg_all, small_all), h0, tgt = _first_gather(
        [wgut1, wd1, wg_loc, small_loc], 3, x[0], loss_target[0], n_meta, tp)
    wg = wg_all.transpose(1, 2, 0, 3, 4).reshape(2, N_HEADS, hd, hd)
    small_full = small_all.transpose(1, 0, 2).reshape(small_rows, d)
    cw4 = small_full[n_meta:n_meta + k4]
    cw31 = small_full[n_meta + k4:n_meta + k4 + k31]

    wgu1, wdn1 = wgut1_all.reshape(2 * f, d), wd1_all.reshape(f, d)
    (h1, gu1, n1), (win_all, w3_all) = _ffn_fwd(h0, ffn1_norm, wgu1, wdn1, tm,
                                                comm=_Gather([win_loc, w3_loc], pass_on_at=(0.55, 0.9)))
    (h2, proj, n2, xr_s, hs_s, v1_s, ya_s, yb_s), (wgut2_all, wd2_all) = _mixer_fwd(
        h1, mix_norm, b_in, win_all, cw4, rnn_conv_b, wg, rg_b_a, rg_b_x, rg_lambda, cw31, conv_dw_b, conv_ln_g,
        conv_ln_b, conv_b_proj, w3_all, tmx_fwd, comm=_Gather([wgut2, wd2], pass_on_at=(0.3, 0.5)))
    wgu2, wdn2 = wgut2_all.reshape(2 * f, d), wd2_all.reshape(f, d)
    (dh3, gu2, n3, tail), (win_t_all,) = _ffn_fwd(
        h2, ffn2_norm, wgu2, wdn2, tm, loss=(tgt, final_norm.reshape(1, d), n_meta, t_real),
        comm=_Gather([win_t_loc], pass_on_at=(0.45, 0.75)))
    win_t = win_t_all.reshape(n_in, d)

    def d_w_gu(tag, dgu, n_s, comm=None):
        g, extra = _tn_matmul(
            "d_w_gu" + tag, dgu, n_s,
            pl.BlockSpec((None, tmt, fc), lambda b, i: (b // FFN_CHUNKS, i, b % FFN_CHUNKS)),
            pl.BlockSpec((tmt, d), lambda b, i: (i, 0)),
            2 * FFN_CHUNKS, fc, d, tmt, tp, (2 * FFN_CHUNKS, fc, d),
            pl.BlockSpec((None, fc, d), lambda b, i: (b, 0, 0)), (fc, d), comm)
        return g.reshape(N_DEV, fb, d), extra

    def d_w_down(tag, act, df, comm=None):
        g, extra = _tn_matmul(
            "d_w_down" + tag, act, df,
            pl.BlockSpec((tmt, fc), lambda b, i: (i, b)), pl.BlockSpec((tmt, d), lambda b, i: (i, 0)),
            FFN_CHUNKS, fc, d, tmt, tp, (FFN_CHUNKS, fc, d),
            pl.BlockSpec((None, fc, d), lambda b, i: (b, 0, 0)), (fc, d), comm)
        return g.reshape(N_DEV, wr, d), extra

    (dh2, dgu2, act2, df2, tail), _ = _ffn_bwd(dh3, h2, gu2, ffn2_norm, wgu2, wdn2, tm, tail, TAIL_FFN2, n3)
    g_wgu2, _ = d_w_gu("2", dgu2, n3)
    g_wd2, _ = d_w_down("2", act2, df2)
    (dh1, dproj, x3, y3, yg, sg), (r_wd2, r_wgu2) = _mixer_bwd(
        dh2, h1, mix_norm, proj, xr_s, hs_s, v1_s, ya_s, yb_s, win_t, cw4, wg, rg_b_a, rg_b_x, rg_lambda, cw31,
        conv_ln_g, conv_ln_b, w3_all, tmx, comm=_Scatter([g_wd2, g_wgu2]))
    g_w3, _ = _tn_matmul(
        "d_w_proj3", x3, y3,
        pl.BlockSpec((tmw, d), lambda b, i: (i, b)), pl.BlockSpec((tmw, d), lambda b, i: (i, b)),
        3, d, d, tmw, tp, (N_DEV, 3, pr, d), pl.BlockSpec((N_DEV, None, pr, d), lambda b, i: (0, b, 0, 0)),
        (N_DEV, pr, d))
    g_wg, _ = _tn_matmul(
        "d_w_gates", xr_s, yg,
        pl.BlockSpec((tmw, hd), lambda b, i: (i, b % N_HEADS)), pl.BlockSpec((tmw, hd), lambda b, i: (i, b)),
        2 * N_HEADS, hd, hd, tmw, tp, (N_DEV, 2 * N_HEADS, gr, hd),
        pl.BlockSpec((N_DEV, None, gr, hd), lambda b, i: (0, b, 0, 0)), (N_DEV, gr, hd))
    w3_sems, g_w3_thru, w3_land, _ = _exchange_start("grads_proj3_exchange", _scatter_copies, N_DEV - 1, g_w3)
    g_win, (r_wg,) = _tn_matmul(
        "d_w_in", n2, dproj,
        pl.BlockSpec((tmw, d), lambda b, i: (i, 0)), pl.BlockSpec((tmw, nbc), lambda b, i: (i, b)),
        N_DEV, d, nbc, tmw, tp, (N_DEV, d, nbc), pl.BlockSpec((None, d, nbc), lambda b, i: (b, 0, 0)), (d, nbc),
        comm=_Scatter([g_wg]))
    win_sems, g_win_thru, win_land, win_token = _exchange_start("grads_w_in_exchange", _scatter_copies, N_DEV - 1, g_win)
    (dh0, dgu1, act1, df1, tail), _ = _ffn_bwd(dh1, h0, gu1, ffn1_norm, wgu1, wdn1, tm, tail, TAIL_FFN1, win_token)
    grad_x = dh0[n_meta:t_real][None]

    pieces = [sg, dh0[:n_meta], tail]
    assert all(p.shape[0] % SUBLANES == 0 for p in pieces)
    at = [0, sg.shape[0], sg.shape[0] + n_meta]
    loss_row = at[2] + TAIL_LOSS
    rep_rows = [("ffn1_norm", at[2] + TAIL_FFN1, 1), ("mix_norm", SG_MIX, 1), ("b_in", SG_BIN, 6),
                ("rnn_conv_b", SG_CB4, 1),
                ("rg_b_a", SG_BA, 1), ("rg_b_x", SG_BX, 1), ("rg_lambda", SG_LAM, 1), ("conv_dw_b", SG_CB31, 1),
                ("conv_ln_g", SG_LNG, 1), ("conv_ln_b", SG_LNB, 1), ("conv_b_proj", SG_BCP, 1),
                ("ffn2_norm", at[2] + TAIL_FFN2, 1), ("final_norm", at[2] + TAIL_FINAL, 1)]
    col_rows = [("meta_tokens", at[1], n_meta), ("rnn_conv_w", SG_CW4, k4), ("conv_dw_w", SG_CW31, k31)]
    layout = []
    for nm, row0, nr in rep_rows:
        kind = "wide" if nm == "b_in" else "rep"
        as2d = lambda a: a.reshape(1, -1) if a.ndim == 1 else a
        layout.append((kind, row0, nr, as2d(w[nm]), as2d(w["m_" + nm]), as2d(w["v_" + nm])))
    for nm, row0, nr in col_rows:
        sq = lambda a: a.reshape(a.shape[-2], a.shape[-1])
        layout.append(("col", row0, nr, sq(w[nm]), sq(w["m_" + nm]), sq(w["v_" + nm])))
    small_partial = jnp.concatenate(pieces, axis=0)

    g_wd1, (small_partials,) = d_w_down("1", act1, df1, comm=_Bcast(small_partial))
    g_wgu1, (r_wd1,) = d_w_gu("1", dgu1, n1, comm=_Scatter([g_wd1]))

    g_last = g_wgu1.reshape((4, 2) + g_wgu1.shape[1:])
    (from_sibling,) = _pair_exchange([g_last])
    comb_wgu1 = _pair_add(g_last, from_sibling, core)
    sems, comb_thru, land_thru, after = _exchange_start("grads_chip_exchange", _chip_copies, 3, comb_wgu1)
    g_win, r_win = _exchange_wait("grads_w_in_exchange", _scatter_copies, win_sems, g_win_thru, win_land, after)
    g_w3, r_w3 = _exchange_wait("grads_proj3_exchange", _scatter_copies, w3_sems, g_w3_thru, w3_land, after)

    groups = [(g_wd1, r_wd1, me_index, ["ffn1_w_down"]),
              (g_wd2, r_wd2, me_index, ["ffn2_w_down"]), (g_wgu2, r_wgu2, me_index, ["ffn2_w_gu"]),
              (g_win, r_win, me_index, ["w_in"]), (g_w3, r_w3, me_index, ["w_out", "rnn_w_proj", "conv_w_proj"]),
              (g_wg, r_wg, me_index, ["rg_w_a", "rg_w_x"]), (None, None, chip, ["ffn1_w_gu"])]
    res = {}
    for own, recv, idx, group in groups:
        if own is None:
            own, recv = _exchange_wait("grads_chip_exchange", _chip_copies, sems, comb_thru, land_thru, after)
        outs = _final_adamw(own, recv, idx, [(w[nm], w["m_" + nm], w["v_" + nm]) for nm in group], after)
        after = outs[-1][0]
        for nm, o in zip(group, outs):
            res[nm] = o
    for nm in ("ffn1_w_gu", "ffn2_w_gu"):
        res[nm] = tuple(jnp.swapaxes(a, 1, 2) for a in res[nm])

    total, small_out = _small_adamw(small_partials, layout, me_index)
    for (nm, _, _), o in zip(rep_rows + col_rows, small_out):
        res[nm] = tuple(a.reshape(w[nm].shape) for a in o)

    order = ["meta_tokens", "ffn1_norm", "ffn1_w_gu", "ffn1_w_down", "mix_norm", "w_in", "b_in", "rnn_conv_w",
             "rnn_conv_b", "rg_w_a", "rg_b_a", "rg_w_x", "rg_b_x", "rg_lambda", "rnn_w_proj", "conv_dw_w",
             "conv_dw_b", "conv_ln_g", "conv_ln_b", "conv_w_proj", "conv_b_proj", "w_out", "ffn2_norm",
             "ffn2_w_gu", "ffn2_w_down", "final_norm"]
    return (total[loss_row, 0], grad_x, *[res[nm][0] for nm in order], *[res[nm][1] for nm in order],
            *[res[nm][2] for nm in order], *[res[nm][3] for nm in order])
```

```python
import functools
import math

import jax
import jax.numpy as jnp
from jax import lax
from jax.experimental import pallas as pl
from jax.experimental.pallas import tpu as pltpu

F32 = jnp.float32
BF16 = jnp.bfloat16
MESH = pl.DeviceIdType.MESH
N_DEV = 8
N_HEADS = 4
RG_LRU_C = 8.0
EPS = 1e-6
FFN_RES = 0.5
ADAM_LR, ADAM_B1, ADAM_B2, ADAM_EPS, ADAM_WD, ADAM_STEP = 0.001, 0.9, 0.999, 1e-08, 0.01, 10
V7X_VMEM_LIMIT = 56 * 1024 * 1024
CONV4_HALO = 8
CONV31_HALO = 32
SUBLANES = 8
STAGE_ROWS = 512
TAIL_FFN1, TAIL_FINAL, TAIL_LOSS, TAIL_FFN2 = 0, 1, 2, 3
FFN_CHUNKS = 2
FFN_FWD_CHUNKS = 1
GELU_C = math.sqrt(2.0 / math.pi)
GELU_K = 0.044715


def _any():
    return pl.BlockSpec(memory_space=pl.ANY)


def _params(n_grid):
    return pltpu.CompilerParams(dimension_semantics=("arbitrary",) * n_grid, vmem_limit_bytes=V7X_VMEM_LIMIT)


def _nn(a, b):
    return jnp.dot(a, b, preferred_element_type=F32)


def _nt(a, b):
    return lax.dot_general(a, b, (((1,), (1,)), ((), ())), preferred_element_type=F32)


def _tn(a, b):
    return lax.dot_general(a, b, (((0,), (0,)), ((), ())), preferred_element_type=F32)


def _sigmoid(x):
    return 0.5 * jnp.tanh(0.5 * x) + 0.5


def _rowsum(x):
    return jnp.sum(x, axis=0, keepdims=True)


def _rms_fwd(x, g):
    r = lax.rsqrt(jnp.mean(x * x, axis=-1, keepdims=True) + EPS)
    return x * r * g, r


def _rms_bwd(dn, x, r, g):
    xr = x * r
    gy = dn * g
    dx = r * (gy - xr * jnp.mean(gy * xr, axis=-1, keepdims=True))
    return dx, _rowsum(dn * xr)


def _gelu(y):
    t = jnp.tanh(GELU_C * (y + GELU_K * y * y * y))
    return 0.5 * y * (1.0 + t), t


def _gelu_grad(y, t):
    return 0.5 * (1.0 + t) + 0.5 * y * (1.0 - t * t) * GELU_C * (1.0 + 3.0 * GELU_K * y * y)


def _softplus(x):
    return jnp.maximum(x, 0.0) + jnp.log(1.0 + jnp.exp(-jnp.abs(x)))


def _one_minus_exp(z):
    series = -z * (1.0 + 0.5 * z * (1.0 + z * (1.0 / 3.0) * (1.0 + 0.25 * z)))
    return jnp.where(z > -0.05, series, 1.0 - jnp.exp(z))


def _tiles(t_real):
    if t_real > 2048:
        tm = 384
        tp = -(-t_real // tm) * tm
        return tp, tm, tm // 2, tm // 2, tp // 2, tp
    tm = 128
    tp = -(-t_real // tm) * tm
    return tp, tm, tm // 2, tm // 2, tm, tm


def _load_weights(copies, sems):
    cps = [pltpu.make_async_copy(s, d, sems.at[k]) for k, (s, d) in enumerate(copies)]
    for cp in cps:
        cp.start()
    for cp in cps:
        cp.wait()


def _position():
    x, y, c = lax.axis_index("x"), lax.axis_index("y"), lax.axis_index("c")
    chips = [(1 - x, y), (x, 1 - y), (1 - x, 1 - y)]
    return x, y, c, chips


def _slot(p):
    return 4 * p[0] + 2 * p[1] + p[2]


class _Lazy(dict):
    def __getitem__(self, key):
        val = dict.__getitem__(self, key)
        return val() if callable(val) else val


class _Gather:
    def __init__(self, shards, pass_on_at=None):
        self.shards = list(shards)
        self.n = len(self.shards)
        self.pass_on_at = pass_on_at

    def inputs(self):
        return self.shards

    def out_shape(self):
        return [jax.ShapeDtypeStruct((N_DEV,) + s.shape, s.dtype) for s in self.shards]

    N_SEMS = 9

    def scratch(self):
        return [pltpu.SemaphoreType.DMA((self.N_SEMS * self.n,)), pltpu.SemaphoreType.DMA((self.N_SEMS * self.n,)),
                pltpu.SemaphoreType.DMA((self.n,))]

    def _plan(self, ins, outs, sems):
        send_sems, recv_sems, local_sems = sems
        x, y, c, _ = _position()
        me, sib, xn, yn, dg = (x, y, c), (x, y, 1 - c), (1 - x, y, c), (x, 1 - y, c), (1 - x, 1 - y, c)
        other = lambda p: (p[0], p[1], 1 - c)

        def blk(a, p, half=None):
            ref = outs[a].at[_slot(p)]
            if half is None:
                return ref
            rows = self.shards[a].shape[0] // 2
            return ref.at[pl.ds(half * rows, rows)]

        def copy(a, k, dst, to, src=None):
            return pltpu.make_async_remote_copy(
                src_ref=dst if src is None else src, dst_ref=dst,
                send_sem=send_sems.at[self.N_SEMS * a + k], recv_sem=recv_sems.at[self.N_SEMS * a + k],
                device_id=to, device_id_type=MESH)

        cp = _Lazy(mine=lambda: [pltpu.make_async_copy(ins[a], blk(a, me), local_sems.at[a]) for a in range(self.n)])
        for a in range(self.n):
            cp[a] = _Lazy(
                own=lambda a=a: [copy(a, 0, blk(a, me), sib, src=ins[a]), copy(a, 1, blk(a, me), xn, src=ins[a]),
                                 copy(a, 2, blk(a, me), yn, src=ins[a])],
                from_x=lambda a=a: copy(a, 1, blk(a, xn), me), from_y=lambda a=a: copy(a, 2, blk(a, yn), me),
                relay_x=lambda a=a: copy(a, 3, blk(a, xn, 0), yn), relay_y=lambda a=a: copy(a, 4, blk(a, yn, 1), xn),
                diag0=lambda a=a: copy(a, 3, blk(a, dg, 0), me), diag1=lambda a=a: copy(a, 4, blk(a, dg, 1), me),
                pass_x=lambda a=a: copy(a, 5, blk(a, xn), sib), pass_y=lambda a=a: copy(a, 6, blk(a, yn), sib),
                pass_d0=lambda a=a: copy(a, 7, blk(a, dg, 0), sib), pass_d1=lambda a=a: copy(a, 8, blk(a, dg, 1), sib),
                from_sib=lambda a=a: [copy(a, 0, blk(a, sib), me), copy(a, 5, blk(a, other(xn)), me),
                                      copy(a, 6, blk(a, other(yn)), me), copy(a, 7, blk(a, other(dg), 0), me),
                                      copy(a, 8, blk(a, other(dg), 1), me)])
        return cp

    def start(self, ins, outs, sems):
        cp = self._plan(ins, outs, sems)
        for c in cp["mine"]:
            c.start()
        for a in range(self.n):
            for c in cp[a]["own"]:
                c.start()

    def pass_on(self, ins, outs, sems):
        cp = self._plan(ins, outs, sems)
        for a in range(self.n):
            cp[a]["from_x"].wait_recv()
            cp[a]["relay_x"].start()
            cp[a]["pass_x"].start()
        for a in range(self.n):
            cp[a]["from_y"].wait_recv()
            cp[a]["relay_y"].start()
            cp[a]["pass_y"].start()

    def pass_on_relayed(self, ins, outs, sems):
        cp = self._plan(ins, outs, sems)
        for a in range(self.n):
            cp[a]["diag0"].wait_recv()
            cp[a]["pass_d0"].start()
            cp[a]["diag1"].wait_recv()
            cp[a]["pass_d1"].start()

    def finish(self, ins, outs, sems):
        if self.pass_on_at is None:
            self.pass_on(ins, outs, sems)
            self.pass_on_relayed(ins, outs, sems)
        cp = self._plan(ins, outs, sems)
        for a in range(self.n):
            for c in cp[a]["from_sib"]:
                c.wait_recv()
            for c in cp[a]["own"] + [cp[a][k] for k in ("relay_x", "relay_y", "pass_x", "pass_y", "pass_d0", "pass_d1")]:
                c.wait_send()
        for c in cp["mine"]:
            c.wait()


class _Scatter:
    def __init__(self, grads):
        self.grads = list(grads)
        self.n = len(self.grads)

    def inputs(self):
        return self.grads

    def out_shape(self):
        return [jax.ShapeDtypeStruct((N_DEV - 1,) + g.shape[1:], g.dtype) for g in self.grads]

    def scratch(self):
        return [pltpu.SemaphoreType.DMA((7 * self.n,)), pltpu.SemaphoreType.DMA((7 * self.n,))]

    def _plan(self, ins, outs, sems):
        send_sems, recv_sems = sems
        x, y, c, _ = _position()
        cps = []
        for a in range(self.n):
            for k in range(1, N_DEV):
                peer = (x ^ (k >> 2), y ^ ((k >> 1) & 1), c ^ (k & 1))
                cps.append(pltpu.make_async_remote_copy(
                    src_ref=ins[a].at[_slot(peer)], dst_ref=outs[a].at[k - 1],
                    send_sem=send_sems.at[7 * a + k - 1], recv_sem=recv_sems.at[7 * a + k - 1],
                    device_id=peer, device_id_type=MESH))
        return cps

    def start(self, ins, outs, sems):
        for cp in self._plan(ins, outs, sems):
            cp.start()

    def finish(self, ins, outs, sems):
        for cp in self._plan(ins, outs, sems):
            cp.wait()


def _hosted(inner, n_in, n_out, comm, grid):
    if comm is None:
        return inner
    nc_in, nc_out, ns = len(comm.inputs()), len(comm.out_shape()), len(comm.scratch())

    def body(*refs):
        o0 = n_in + nc_in
        s0 = o0 + n_out + nc_out
        main = refs[:n_in] + refs[o0:o0 + n_out] + refs[s0:len(refs) - ns]
        c_in, c_out, c_sems = refs[n_in:o0], refs[o0 + n_out:s0], refs[len(refs) - ns:]
        ids = [pl.program_id(ax) for ax in range(len(grid))]
        first = functools.reduce(jnp.logical_and, [i == 0 for i in ids])
        last = functools.reduce(jnp.logical_and, [i == g - 1 for i, g in zip(ids, grid)])

        @pl.when(first)
        def _():
            comm.start(c_in, c_out, c_sems)

        inner(*main)

        if getattr(comm, "pass_on_at", None) is not None:
            assert len(grid) == 1
            first_at, second_at = (min(grid[0] - 1, int(frac * grid[0])) for frac in comm.pass_on_at)
            assert first_at < second_at

            @pl.when(ids[0] == first_at)
            def _():
                comm.pass_on(c_in, c_out, c_sems)

            @pl.when(ids[0] == second_at)
            def _():
                comm.pass_on_relayed(c_in, c_out, c_sems)

        @pl.when(last)
        def _():
            comm.finish(c_in, c_out, c_sems)

    return body


def _call(inner, name, grid, in_specs, out_specs, out_shape, scratch, args, comm=None):
    n_in, n_out = len(args), len(out_shape)
    body = _hosted(inner, n_in, n_out, comm, grid)
    if comm is not None:
        in_specs = list(in_specs) + [_any()] * len(comm.inputs())
        args = list(args) + comm.inputs()
        out_specs = list(out_specs) + [_any()] * len(comm.out_shape())
        out_shape = list(out_shape) + comm.out_shape()
        scratch = list(scratch) + comm.scratch()
    outs = pl.pallas_call(
        body, name=name, grid=grid, in_specs=list(in_specs), out_specs=list(out_specs), out_shape=list(out_shape),
        scratch_shapes=list(scratch), compiler_params=_params(len(grid)))(*args)
    return list(outs[:n_out]), list(outs[n_out:])


class _Bcast:
    def __init__(self, block):
        self.block = block

    def inputs(self):
        return [self.block]

    def out_shape(self):
        return [jax.ShapeDtypeStruct((N_DEV,) + self.block.shape, self.block.dtype)]

    def scratch(self):
        return [pltpu.SemaphoreType.DMA((N_DEV - 1,)), pltpu.SemaphoreType.DMA((N_DEV - 1,)),
                pltpu.SemaphoreType.DMA((1,))]

    def _plan(self, ins, outs, sems):
        send_sems, recv_sems, local_sem = sems
        x, y, c, _ = _position()
        mine = outs[0].at[_slot((x, y, c))]
        cps = []
        for k in range(1, N_DEV):
            peer = (x ^ (k >> 2), y ^ ((k >> 1) & 1), c ^ (k & 1))
            cps.append(pltpu.make_async_remote_copy(
                src_ref=ins[0], dst_ref=mine, send_sem=send_sems.at[k - 1], recv_sem=recv_sems.at[k - 1],
                device_id=peer, device_id_type=MESH))
        return pltpu.make_async_copy(ins[0], mine, local_sem.at[0]), cps

    def start(self, ins, outs, sems):
        own, cps = self._plan(ins, outs, sems)
        own.start()
        for cp in cps:
            cp.start()

    def finish(self, ins, outs, sems):
        own, cps = self._plan(ins, outs, sems)
        for cp in cps:
            cp.wait()
        own.wait()


def _first_gather(shards, small_idx, x2, t2, n_meta, tp):
    comm = _Gather(shards)
    n = comm.n
    seq, d = x2.shape
    t_real = n_meta + seq
    n_pad = tp - t_real
    cw = d // N_DEV
    rows = STAGE_ROWS if seq % STAGE_ROWS == 0 else seq
    n_chunks = seq // rows

    def body(*refs):
        ins, (x_ref, t_ref) = refs[:n], refs[n:n + 2]
        outs, (h0_ref, tg_ref) = refs[n + 2:2 * n + 2], refs[2 * n + 2:2 * n + 4]
        sems = refs[2 * n + 4:2 * n + 7]
        buf, zeros, in_sems, out_sems, misc_sems = refs[2 * n + 7:]
        comm.start(ins, outs, sems)
        zeros[...] = jnp.zeros_like(zeros)
        fills = [pltpu.make_async_copy(zeros.at[pl.ds(0, n_pad)], h0_ref.at[pl.ds(t_real, n_pad)], misc_sems.at[0]),
                 pltpu.make_async_copy(zeros.at[pl.ds(0, n_pad)], tg_ref.at[pl.ds(t_real, n_pad)], misc_sems.at[1]),
                 pltpu.make_async_copy(zeros.at[pl.ds(0, n_meta)], tg_ref.at[pl.ds(0, n_meta)], misc_sems.at[2])]
        for cp in fills:
            cp.start()
        jobs = [(src, dst, c) for src, dst in ((x_ref, h0_ref), (t_ref, tg_ref)) for c in range(n_chunks)]

        def load(k):
            src, _, c = jobs[k]
            return pltpu.make_async_copy(src.at[pl.ds(c * rows, rows)], buf.at[k % 2], in_sems.at[k % 2])

        def store(k):
            _, dst, c = jobs[k]
            return pltpu.make_async_copy(buf.at[k % 2], dst.at[pl.ds(n_meta + c * rows, rows)], out_sems.at[k % 2])

        load(0).start()
        for k in range(len(jobs)):
            load(k).wait()
            if k + 1 < len(jobs):
                if k >= 1:
                    store(k - 1).wait()
                load(k + 1).start()
            store(k).start()
        for k in range(max(0, len(jobs) - 2), len(jobs)):
            store(k).wait()
        comm.finish(ins, outs, sems)
        meta = [pltpu.make_async_copy(outs[small_idx].at[k, pl.ds(0, n_meta)],
                                      h0_ref.at[pl.ds(0, n_meta), pl.ds(k * cw, cw)], misc_sems.at[3 + k])
                for k in range(N_DEV)]
        for cp in meta:
            cp.start()
        for cp in fills + meta:
            cp.wait()

    staged = [jax.ShapeDtypeStruct((tp, d), F32)] * 2
    outs = pl.pallas_call(
        body, name="weights_all_gather", out_shape=comm.out_shape() + staged,
        in_specs=[_any()] * (n + 2), out_specs=[_any()] * (n + 2),
        scratch_shapes=comm.scratch() + [
            pltpu.VMEM((2, rows, d), F32), pltpu.VMEM((max(n_pad, n_meta), d), F32),
            pltpu.SemaphoreType.DMA((2,)), pltpu.SemaphoreType.DMA((2,)), pltpu.SemaphoreType.DMA((3 + N_DEV,))],
        compiler_params=pltpu.CompilerParams(vmem_limit_bytes=V7X_VMEM_LIMIT),
    )(*shards, x2, t2)
    return outs[:n], outs[n], outs[n + 1]


def _pair_exchange(grads):
    n = len(grads)

    def body(*refs):
        ins, outs = refs[:n], refs[n:2 * n]
        send_sems, recv_sems = refs[2 * n:]
        x, y, c, _ = _position()
        cps = [pltpu.make_async_remote_copy(
            src_ref=ins[a].at[:, 1 - c], dst_ref=outs[a],
            send_sem=send_sems.at[a], recv_sem=recv_sems.at[a],
            device_id=(x, y, 1 - c), device_id_type=MESH) for a in range(n)]
        for cp in cps:
            cp.start()
        for cp in cps:
            cp.wait()

    return pl.pallas_call(
        body, name="grads_pair_exchange",
        out_shape=[jax.ShapeDtypeStruct((4,) + g.shape[2:], g.dtype) for g in grads],
        in_specs=[_any()] * n, out_specs=[_any()] * n,
        scratch_shapes=[pltpu.SemaphoreType.DMA((n,)), pltpu.SemaphoreType.DMA((n,))],
    )(*grads)


def _chip_copies(c_ref, land_ref, sems):
    _, _, c, chips = _position()
    return [pltpu.make_async_remote_copy(
        src_ref=c_ref.at[2 * cx + cy], dst_ref=land_ref.at[j], send_sem=sems[j], recv_sem=sems[3 + j],
        device_id=(cx, cy, c), device_id_type=MESH) for j, (cx, cy) in enumerate(chips)]


def _scatter_copies(g_ref, land_ref, sems):
    x, y, c, _ = _position()
    cps = []
    for k in range(1, N_DEV):
        peer = (x ^ (k >> 2), y ^ ((k >> 1) & 1), c ^ (k & 1))
        cps.append(pltpu.make_async_remote_copy(
            src_ref=g_ref.at[_slot(peer)], dst_ref=land_ref.at[k - 1], send_sem=sems[k - 1],
            recv_sem=sems[N_DEV - 1 + k - 1], device_id=peer, device_id_type=MESH))
    return cps


def _exchange_start(name, copies, n_copies, src):
    hbm = pl.BlockSpec(memory_space=pltpu.HBM)
    sem = pl.BlockSpec(memory_space=pltpu.SEMAPHORE)
    n_sems = 2 * n_copies

    def body(s_ref, land_ref, *refs):
        for cp in copies(s_ref, land_ref, refs[:n_sems]):
            cp.start()
        token = refs[n_sems + 2]
        token[...] = jnp.zeros_like(token)

    land = lax.empty((n_copies,) + src.shape[1:], src.dtype)
    outs = pl.pallas_call(
        body, name=name + "_start",
        out_shape=(pltpu.SemaphoreType.DMA(()),) * n_sems
        + (pltpu.HBM(src.shape, src.dtype), pltpu.HBM(land.shape, land.dtype),
           jax.ShapeDtypeStruct((SUBLANES, 128), F32)),
        in_specs=(hbm, hbm), out_specs=(sem,) * n_sems + (hbm, hbm, pl.BlockSpec(memory_space=pltpu.VMEM)),
        input_output_aliases={0: n_sems, 1: n_sems + 1},
        compiler_params=pltpu.CompilerParams(has_side_effects=pltpu.SideEffectType.DATAFLOW_SIDE_EFFECTING),
    )(pltpu.with_memory_space_constraint(src, pltpu.HBM), pltpu.with_memory_space_constraint(land, pltpu.HBM))
    return outs[:n_sems], outs[n_sems], outs[n_sems + 1], outs[n_sems + 2]


def _exchange_wait(name, copies, sems, src_thru, land_thru, after):
    hbm = pl.BlockSpec(memory_space=pltpu.HBM)
    sem = pl.BlockSpec(memory_space=pltpu.SEMAPHORE)
    n_sems = len(sems)

    def body(s_ref, land_ref, *refs):
        for cp in copies(s_ref, land_ref, refs[:n_sems]):
            cp.wait_send()
            cp.wait_recv()

    return pl.pallas_call(
        body, name=name + "_wait",
        out_shape=(pltpu.HBM(src_thru.shape, src_thru.dtype), pltpu.HBM(land_thru.shape, land_thru.dtype)),
        in_specs=(hbm, hbm) + (sem,) * n_sems + (pl.BlockSpec(memory_space=pl.ANY),), out_specs=(hbm, hbm),
        input_output_aliases={0: 0, 1: 1},
        compiler_params=pltpu.CompilerParams(has_side_effects=pltpu.SideEffectType.DATAFLOW_SIDE_EFFECTING),
    )(src_thru, land_thru, *sems, after)


def _pair_add(grad, recv, core):
    blk = grad.shape[2:]
    zeros = (0,) * len(blk)

    def body(core_ref, g_ref, r_ref, o_ref):
        del core_ref
        o_ref[...] = (g_ref[...].astype(F32) + r_ref[...].astype(F32)).astype(BF16)

    return pl.pallas_call(
        body, name="grads_pair_add",
        out_shape=jax.ShapeDtypeStruct((4,) + blk, BF16),
        grid_spec=pltpu.PrefetchScalarGridSpec(
            num_scalar_prefetch=1, grid=(4,),
            in_specs=[pl.BlockSpec((None, None) + blk, lambda i, cr: (i, cr[0]) + zeros),
                      pl.BlockSpec((None,) + blk, lambda i, cr: (i,) + zeros)],
            out_specs=pl.BlockSpec((None,) + blk, lambda i, cr: (i,) + zeros)),
        compiler_params=_params(1),
    )(core, grad, recv)


def _adamw(w, g, m, v):
    m2 = ADAM_B1 * m + (1.0 - ADAM_B1) * g
    v2 = ADAM_B2 * v + (1.0 - ADAM_B2) * (g * g)
    m_hat = m2 / (1.0 - ADAM_B1 ** ADAM_STEP)
    v_hat = v2 / (1.0 - ADAM_B2 ** ADAM_STEP)
    delta = -ADAM_LR * (m_hat / (jnp.sqrt(v_hat) + ADAM_EPS) + ADAM_WD * w)
    return delta, m2, v2


def _final_adamw(own, recv, idx, parts, after):
    blk = own.shape[1:]
    n_recv = recv.shape[0]
    n_parts = len(parts)
    per = blk[0] // n_parts if n_parts > 1 else None
    rows = blk[-2]
    n_chunks = 1 if n_parts > 1 else (4 if rows % 64 == 0 and rows >= 512 else (2 if rows % 32 == 0 else 1))
    cblk = blk[:-2] + (rows // n_chunks, blk[-1])
    lead = (0,) * (len(blk) - 2)

    def body(idx_ref, c_ref, r_ref, after_ref, *refs):
        del idx_ref, after_ref
        ins, outs = refs[:3 * n_parts], refs[3 * n_parts:]
        g = c_ref[...].astype(F32)
        for k in range(n_recv):
            g = g + r_ref[k].astype(F32)
        for p in range(n_parts):
            w_ref, m_ref, v_ref = ins[3 * p:3 * p + 3]
            if n_parts == 1:
                gp = g
            elif per == 1:
                gp = g[p]
            else:
                gp = g[p * per:(p + 1) * per]
            delta, m2, v2 = _adamw(w_ref[0], gp, m_ref[0], v_ref[0])
            o = outs[4 * p:4 * p + 4]
            o[0][0] = gp
            o[1][0] = delta
            o[2][0] = m2
            o[3][0] = v2

    flat = [a for wmv in parts for a in wmv]

    def part_spec(a):
        shape = a.shape[:-2] + (a.shape[-2] // n_chunks, a.shape[-1])
        return pl.BlockSpec(shape, lambda i, cr, nd=a.ndim: (0,) * (nd - 2) + (i, 0))

    outs = pl.pallas_call(
        body, name="grads_sum_adamw",
        out_shape=[jax.ShapeDtypeStruct(wmv[0].shape, F32) for wmv in parts for _ in range(4)],
        grid_spec=pltpu.PrefetchScalarGridSpec(
            num_scalar_prefetch=1, grid=(n_chunks,),
            in_specs=[pl.BlockSpec((None,) + cblk, lambda i, cr: (cr[0],) + lead + (i, 0)),
                      pl.BlockSpec((n_recv,) + cblk, lambda i, cr: (0,) + lead + (i, 0))]
                     + [_any()] + [part_spec(a) for a in flat],
            out_specs=[part_spec(wmv[0]) for wmv in parts for _ in range(4)]),
        compiler_params=_params(1),
    )(idx, own, recv, after, *flat)
    return [tuple(outs[4 * p:4 * p + 4]) for p in range(n_parts)]


def _small_adamw(partials, layout, me_index):
    _, rows, d = partials.shape
    n = len(layout)
    cw = d // N_DEV

    def body(me_ref, p_ref, *refs):
        ins, t_ref, outs = refs[:3 * n], refs[3 * n], refs[3 * n + 1:]
        me = me_ref[0]
        total = p_ref[0]
        for j in range(1, N_DEV):
            total = total + p_ref[j]
        t_ref[...] = total
        for e, (kind, r0, nr, _, _, _) in enumerate(layout):
            w_ref, m_ref, v_ref = ins[3 * e:3 * e + 3]
            o = outs[4 * e:4 * e + 4]
            if kind == "rep":
                g = t_ref[r0:r0 + nr, :]
                delta, m2, v2 = _adamw(w_ref[...], g, m_ref[...], v_ref[...])
                for ref, val in zip(o, (g, delta, m2, v2)):
                    ref[...] = val
            elif kind == "wide":
                for q in range(nr):
                    sl = slice(q * d, (q + 1) * d)
                    g = t_ref[r0 + q:r0 + q + 1, :]
                    delta, m2, v2 = _adamw(w_ref[:, sl], g, m_ref[:, sl], v_ref[:, sl])
                    for ref, val in zip(o, (g, delta, m2, v2)):
                        ref[:, sl] = val
            else:
                for j in range(N_DEV):
                    @pl.when(me == j)
                    def _(j=j, o=o, w_ref=w_ref, m_ref=m_ref, v_ref=v_ref, r0=r0, nr=nr):
                        g = t_ref[r0:r0 + nr, j * cw:(j + 1) * cw]
                        delta, m2, v2 = _adamw(w_ref[...], g, m_ref[...], v_ref[...])
                        for ref, val in zip(o, (g, delta, m2, v2)):
                            ref[...] = val

    flat = [a for ent in layout for a in ent[3:]]
    vm = pl.BlockSpec(memory_space=pltpu.VMEM)
    outs = pl.pallas_call(
        body, name="small_adamw",
        out_shape=[jax.ShapeDtypeStruct((rows, d), F32)]
                  + [jax.ShapeDtypeStruct(ent[3].shape, F32) for ent in layout for _ in range(4)],
        in_specs=[pl.BlockSpec(memory_space=pltpu.SMEM), vm] + [vm] * len(flat),
        out_specs=[vm] * (1 + 4 * n),
        compiler_params=pltpu.CompilerParams(vmem_limit_bytes=V7X_VMEM_LIMIT),
    )(me_index, partials, *flat)
    return outs[0], [tuple(outs[1 + 4 * e:5 + 4 * e]) for e in range(n)]


def _ffn_fwd(h, g, wgu, wd, tm, loss=None, comm=None):
    tp, d = h.shape
    f = wd.shape[0]
    fc = f // FFN_FWD_CHUNKS
    nt = tp // tm
    with_loss = loss is not None
    if with_loss:
        tgt, gf, n_meta, t_real = loss

    def body(*refs):
        if with_loss:
            (h_ref, g_ref, wgu_hbm, wd_hbm, tgt_ref, gf_ref, out_ref, gu_ref, n_ref, tail_ref,
             wgu_v, wd_v, sems) = refs
        else:
            h_ref, g_ref, wgu_hbm, wd_hbm, out_ref, gu_ref, n_ref, wgu_v, wd_v, sems = refs
        i = pl.program_id(0)

        @pl.when(i == 0)
        def _():
            _load_weights([(wgu_hbm, wgu_v), (wd_hbm, wd_v)], sems)
            if with_loss:
                tail_ref[...] = jnp.zeros_like(tail_ref)

        x = h_ref[...]
        n, _ = _rms_fwd(x, g_ref[...])
        nb = n.astype(BF16)
        n_ref[...] = nb
        acc = jnp.zeros((tm, d), F32)
        for j in range(FFN_FWD_CHUNKS):
            cols = slice(j * fc, (j + 1) * fc)
            gate = _nt(nb, wgu_v[pl.ds(j * fc, fc), :])
            up = _nt(nb, wgu_v[pl.ds(f + j * fc, fc), :])
            gu_ref[0, :, cols] = gate.astype(BF16)
            gu_ref[1, :, cols] = up.astype(BF16)
            act = (gate * _sigmoid(gate) * up).astype(BF16)
            acc = acc + _nn(act, wd_v[pl.ds(j * fc, fc), :])
        hn = x + FFN_RES * acc
        if not with_loss:
            out_ref[...] = hn
        else:
            gfv = gf_ref[...]
            r = lax.rsqrt(jnp.mean(hn * hn, axis=-1, keepdims=True) + EPS)
            xr = hn * r
            rows = i * tm + lax.broadcasted_iota(jnp.int32, (tm, 1), 0)
            mask = jnp.logical_and(rows >= n_meta, rows < t_real)
            diff = jnp.where(mask, xr * gfv - tgt_ref[...], 0.0)
            tail_ref[TAIL_LOSS:TAIL_LOSS + 1, :] += jnp.zeros((1, d), F32) + 0.5 * jnp.sum(diff * diff) / d
            dy = diff / d
            gy = dy * gfv
            out_ref[...] = r * (gy - xr * jnp.mean(gy * xr, axis=-1, keepdims=True))
            tail_ref[TAIL_FINAL:TAIL_FINAL + 1, :] += _rowsum(dy * xr)

    row = pl.BlockSpec((tm, d), lambda i: (i, 0))
    vec = pl.BlockSpec((1, d), lambda i: (0, 0))
    in_specs = [row, vec, _any(), _any()]
    out_shape = [jax.ShapeDtypeStruct((tp, d), F32), jax.ShapeDtypeStruct((2, tp, f), BF16),
                 jax.ShapeDtypeStruct((tp, d), BF16)]
    out_specs = [row, pl.BlockSpec((2, tm, f), lambda i: (0, i, 0)), row]
    args = [h, g, wgu, wd]
    if with_loss:
        in_specs += [row, vec]
        out_shape += [jax.ShapeDtypeStruct((SUBLANES, d), F32)]
        out_specs += [pl.BlockSpec((SUBLANES, d), lambda i: (0, 0))]
        args += [tgt, gf]
    return _call(body, "ffn_fwd_loss" if with_loss else "ffn_fwd", (nt,), in_specs, out_specs, out_shape,
                 [pltpu.VMEM((2 * f, d), BF16), pltpu.VMEM((f, d), BF16), pltpu.SemaphoreType.DMA((2,))],
                 args, comm)


def _ffn_bwd(dh, h, gu, g, wgu, wd, tm, tail, tail_row, after):
    tp, d = h.shape
    f = wd.shape[0]
    fc = f // FFN_CHUNKS
    nt = tp // tm

    def body(dh_ref, h_ref, gu_ref, g_ref, tail_ref, wgu_hbm, wd_hbm, after_ref,
             dhin_ref, dgu_ref, act_ref, df_ref, dg_ref, wgu_v, wd_v, dn_v, sems):
        del after_ref
        i, j = pl.program_id(0), pl.program_id(1)

        @pl.when(jnp.logical_and(i == 0, j == 0))
        def _():
            _load_weights([(wgu_hbm, wgu_v), (wd_hbm, wd_v)], sems)
            dg_ref[...] = tail_ref[...]

        dfb = (FFN_RES * dh_ref[...]).astype(BF16)

        @pl.when(j == 0)
        def _():
            df_ref[...] = dfb
            dn_v[...] = jnp.zeros_like(dn_v)

        lo = pl.multiple_of(j * fc, 16)
        dact = _nt(dfb, wd_v[pl.ds(lo, fc), :])
        gate = gu_ref[0].astype(F32)
        up = gu_ref[1].astype(F32)
        sg = _sigmoid(gate)
        silu = gate * sg
        act_ref[...] = (silu * up).astype(BF16)
        dgate = (dact * up * (sg * (1.0 + gate * (1.0 - sg)))).astype(BF16)
        dup = (dact * silu).astype(BF16)
        dgu_ref[0] = dgate
        dgu_ref[1] = dup
        dn_v[...] += _nn(dgate, wgu_v[pl.ds(lo, fc), :]) + _nn(dup, wgu_v[pl.ds(pl.multiple_of(f + j * fc, 16), fc), :])

        @pl.when(j == FFN_CHUNKS - 1)
        def _():
            x = h_ref[...]
            r = lax.rsqrt(jnp.mean(x * x, axis=-1, keepdims=True) + EPS)
            dx, dgp = _rms_bwd(dn_v[...], x, r, g_ref[...])
            dhin_ref[...] = dh_ref[...] + dx
            dg_ref[tail_row:tail_row + 1, :] += dgp

    row = pl.BlockSpec((tm, d), lambda i, j: (i, 0))
    vec = pl.BlockSpec((1, d), lambda i, j: (0, 0))
    tile = pl.BlockSpec((SUBLANES, d), lambda i, j: (0, 0))
    hid2 = pl.BlockSpec((2, tm, fc), lambda i, j: (0, i, j))
    return _call(
        body, "ffn_bwd", (nt, FFN_CHUNKS),
        [row, row, hid2, vec, tile, _any(), _any(), _any()],
        [row, hid2, pl.BlockSpec((tm, fc), lambda i, j: (i, j)), row, tile],
        [jax.ShapeDtypeStruct((tp, d), F32), jax.ShapeDtypeStruct((2, tp, f), BF16),
         jax.ShapeDtypeStruct((tp, f), BF16), jax.ShapeDtypeStruct((tp, d), BF16),
         jax.ShapeDtypeStruct((SUBLANES, d), F32)],
        [pltpu.VMEM((2 * f, d), BF16), pltpu.VMEM((f, d), BF16), pltpu.VMEM((tm, d), F32),
         pltpu.SemaphoreType.DMA((2,))],
        [dh, h, gu, g, tail, wgu, wd, after])


def _piece_segments(q, d, nb_cols):
    segs = []
    for j in range(N_DEV):
        lo, hi = max(q * d, j * nb_cols), min((q + 1) * d, (j + 1) * nb_cols)
        if lo < hi:
            segs.append((j, lo - q * d, hi - q * d, lo - j * nb_cols, hi - j * nb_cols))
    return segs


def _w3_copies(w3_hbm, rows, w3_v):
    return [(w3_hbm.at[k, pl.ds(q * rows, rows)], w3_v.at[q, pl.ds(k * rows, rows)])
            for q in range(3) for k in range(N_DEV)]


def _gates(xrb, wg_ref, ba, bx, lam, hd):
    pre_r, pre_i = [], []
    for hh in range(N_HEADS):
        xh = xrb[:, hh * hd:(hh + 1) * hd]
        pre_r.append(_nn(xh, wg_ref[0, hh]))
        pre_i.append(_nn(xh, wg_ref[1, hh]))
    r = _sigmoid(jnp.concatenate(pre_r, axis=1) + ba)
    ig = _sigmoid(jnp.concatenate(pre_i, axis=1) + bx)
    sp = _softplus(-lam)
    log_a = -RG_LRU_C * r * sp
    a = jnp.exp(log_a)
    s = jnp.sqrt(_one_minus_exp(2.0 * log_a))
    return r, ig, sp, a, s


def _scan_fwd(a, u, h_prev):
    tm = a.shape[0]
    rows = lax.broadcasted_iota(jnp.int32, a.shape, 0)
    d = 1
    while d < tm:
        if d < SUBLANES:
            keep = rows >= d
            u = jnp.where(keep, a * pltpu.roll(u, d, 0) + u, u)
            a = jnp.where(keep, a * pltpu.roll(a, d, 0), a)
        else:
            u = jnp.concatenate([u[:d], a[d:] * u[:tm - d] + u[d:]], axis=0)
            a = jnp.concatenate([a[:d], a[d:] * a[:tm - d]], axis=0)
        d *= 2
    return u + a * h_prev


def _scan_bwd(b, v, g_next):
    tm = b.shape[0]
    rows = lax.broadcasted_iota(jnp.int32, b.shape, 0)
    d = 1
    while d < tm:
        if d < SUBLANES:
            keep = rows < tm - d
            v = jnp.where(keep, v + b * pltpu.roll(v, tm - d, 0), v)
            b = jnp.where(keep, b * pltpu.roll(b, tm - d, 0), b)
        else:
            v = jnp.concatenate([v[:tm - d] + b[:tm - d] * v[d:], v[tm - d:]], axis=0)
            b = jnp.concatenate([b[:tm - d] * b[d:], b[tm - d:]], axis=0)
        d *= 2
    return v + b * g_next


def _shifted_copies(ext_ref, es_ref, n_rows):
    for s in range(1, SUBLANES):
        es_ref[s, pl.ds(0, n_rows), :] = ext_ref[pl.ds(s, n_rows), :]


def _tap(ext_ref, es_ref, off, tm):
    q, s = divmod(off, SUBLANES)
    if s == 0:
        return ext_ref[pl.ds(SUBLANES * q, tm), :]
    return es_ref[s, pl.ds(SUBLANES * q, tm), :]


def _mixer_fwd(h, g, b_in, win_all, cw4, cb4, wg, ba, bx, lam, cw31, cb31, lng, lnb, bcp, w3_all, tm, comm=None):
    tp, d = h.shape
    nb_cols = win_all.shape[-1]
    n_in = N_DEV * nb_cols
    hd = wg.shape[-1]
    k4, k31 = cw4.shape[0], cw31.shape[0]
    w3_rows = d // N_DEV

    def body(h_ref, g_ref, b_ref, win_hbm, cw4_ref, cb4_ref, wg_ref, ba_ref, bx_ref, lam_ref, cw31_ref, cb31_ref,
             lng_ref, lnb_ref, bcp_ref, w3_hbm,
             h2_ref, p_ref, n_ref, xr_ref, hs_ref, v1_ref, ya_ref, yb_ref,
             win_v, w3_v, ext4, ext31, es31, hcar, sems):
        @pl.when(pl.program_id(0) == 0)
        def _():
            _load_weights([(win_hbm, win_v)] + _w3_copies(w3_hbm, w3_rows, w3_v), sems)
            ext4[pl.ds(0, CONV4_HALO), :] = jnp.zeros((CONV4_HALO, d), F32)
            ext31[pl.ds(0, CONV31_HALO), :] = jnp.zeros((CONV31_HALO, d), F32)
            hcar[...] = jnp.zeros_like(hcar)

        n, _ = _rms_fwd(h_ref[...], g_ref[...])
        nb = n.astype(BF16)
        n_ref[...] = nb

        def piece(q):
            parts = [_nn(nb, win_v[j, :, bl:bh]) for j, _, _, bl, bh in _piece_segments(q, d, nb_cols)]
            pq = (jnp.concatenate(parts, axis=1) + b_ref[:, q * d:(q + 1) * d]).astype(BF16)
            p_ref[:, q * d:(q + 1) * d] = pq
            return pq.astype(F32)

        x_rnn, y_rnn, glu_v, glu_g, gate_a, gate_b = [piece(q) for q in range(6)]

        ext4[pl.ds(CONV4_HALO, tm), :] = x_rnn
        xr = cb4_ref[...] + jnp.zeros((tm, d), F32)
        for k in range(k4):
            xr = xr + cw4_ref[k:k + 1, :] * ext4[pl.ds(CONV4_HALO - (k4 - 1) + k, tm), :]
        ext4[pl.ds(0, CONV4_HALO), :] = ext4[pl.ds(tm, CONV4_HALO), :]
        xrb = xr.astype(BF16)
        xr_ref[...] = xrb
        xr = xrb.astype(F32)
        _, ig, _, a, s = _gates(xrb, wg_ref, ba_ref[...], bx_ref[...], lam_ref[...], hd)
        hseq = _scan_fwd(a, s * (ig * xr), hcar[0:1, :])
        hcar[0:1, :] = hseq[tm - 1:tm, :]
        hs_ref[...] = hseq.astype(BF16)
        gl, _ = _gelu(y_rnn)
        ya = _nn((hseq * gl).astype(BF16), w3_v[0])
        ya_ref[...] = ya.astype(BF16)

        ext31[pl.ds(CONV31_HALO, tm), :] = glu_v * _sigmoid(glu_g)
        _shifted_copies(ext31, es31, tm + CONV31_HALO - SUBLANES)
        v1 = cb31_ref[...] + jnp.zeros((tm, d), F32)
        for k in range(k31):
            v1 = v1 + cw31_ref[k:k + 1, :] * _tap(ext31, es31, CONV31_HALO - (k31 - 1) + k, tm)
        ext31[pl.ds(0, CONV31_HALO), :] = ext31[pl.ds(tm, CONV31_HALO), :]
        v1b = v1.astype(BF16)
        v1_ref[...] = v1b
        v1 = v1b.astype(F32)
        xc = v1 - jnp.mean(v1, axis=-1, keepdims=True)
        rstd = lax.rsqrt(jnp.mean(xc * xc, axis=-1, keepdims=True) + EPS)
        v2 = xc * rstd * lng_ref[...] + lnb_ref[...]
        yb = _nn((v2 * _sigmoid(v2)).astype(BF16), w3_v[1]) + bcp_ref[...]
        yb_ref[...] = yb.astype(BF16)

        merged = _sigmoid(gate_a) * ya + _sigmoid(gate_b) * yb
        h2_ref[...] = h_ref[...] + _nn(merged.astype(BF16), w3_v[2])

    row = pl.BlockSpec((tm, d), lambda i: (i, 0))
    wide = pl.BlockSpec((tm, n_in), lambda i: (i, 0))
    full = lambda a: pl.BlockSpec(a.shape, lambda i, nd=a.ndim: (0,) * nd)
    smalls = [cw4, cb4, wg, ba, bx, lam, cw31, cb31, lng, lnb, bcp]
    return _call(
        body, "mixer_fwd", (tp // tm,),
        [row, full(g), full(b_in), _any()] + [full(a) for a in smalls] + [_any()],
        [row, wide] + [row] * 6,
        [jax.ShapeDtypeStruct((tp, d), F32), jax.ShapeDtypeStruct((tp, n_in), BF16)]
        + [jax.ShapeDtypeStruct((tp, d), BF16)] * 6,
        [pltpu.VMEM(win_all.shape, BF16),
         pltpu.VMEM((3, d, d), BF16),
         pltpu.VMEM((tm + CONV4_HALO, d), F32),
         pltpu.VMEM((tm + CONV31_HALO, d), F32),
         pltpu.VMEM((SUBLANES, tm + CONV31_HALO, d), F32),
         pltpu.VMEM((SUBLANES, d), F32),
         pltpu.SemaphoreType.DMA((1 + 3 * N_DEV,))],
        [h, g, b_in, win_all, *smalls, w3_all], comm)


SG_BIN, SG_CW4, SG_CB4, SG_BA, SG_BX, SG_LAM, SG_CB31, SG_LNG, SG_LNB, SG_BCP, SG_MIX, SG_CW31 = 0, 6, 10, 11, 12, 13, 14, 15, 16, 17, 18, 19


def _mixer_bwd(dh2, h, g, proj, xr_s, hs_s, v1_s, ya_s, yb_s, win_t, cw4, wg, ba, bx, lam, cw31, lng, lnb, w3_all, tm,
               comm=None):
    tp, d = dh2.shape
    n_in = proj.shape[1]
    hd = wg.shape[-1]
    k4, k31 = cw4.shape[0], cw31.shape[0]
    nt = tp // tm
    w3_rows = d // N_DEV
    sg_rows = -(-(SG_CW31 + k31) // SUBLANES) * SUBLANES
    halo_rows = 16
    per = tm // halo_rows

    def body(dh_ref, h_ref, g_ref, p_ref, xr_ref, hs_ref, hh_ref, v1_ref, ya_ref, yb_ref, win_hbm,
             cw4_ref, wg_ref, wgt_ref, ba_ref, bx_ref, lam_ref, cw31_ref, lng_ref, lnb_ref, w3_hbm,
             dh1_ref, dp_ref, x3_ref, y3_ref, yg_ref, sg_ref,
             win_v, w3_v, extd4, extd31, es31, gcar, sems):
        i = pl.program_id(0)
        tile = nt - 1 - i

        @pl.when(i == 0)
        def _():
            _load_weights([(win_hbm, win_v)] + _w3_copies(w3_hbm, w3_rows, w3_v), sems)
            for q in range(3):
                w3_v[q] = w3_v[q].T
            extd4[pl.ds(tm, CONV4_HALO), :] = jnp.zeros((CONV4_HALO, d), F32)
            extd31[pl.ds(tm, CONV31_HALO), :] = jnp.zeros((CONV31_HALO, d), F32)
            gcar[...] = jnp.zeros_like(gcar)
            sg_ref[...] = jnp.zeros_like(sg_ref)

        def acc(row, val):
            sg_ref[row:row + 1, :] += _rowsum(val)

        rows = lax.broadcasted_iota(jnp.int32, (tm, d), 0)
        x_rnn = p_ref[:, 0:d].astype(F32)
        y_rnn = p_ref[:, d:2 * d].astype(F32)
        glu_v = p_ref[:, 2 * d:3 * d].astype(F32)
        glu_g = p_ref[:, 3 * d:4 * d].astype(F32)
        sga = _sigmoid(p_ref[:, 4 * d:5 * d].astype(F32))
        sgb = _sigmoid(p_ref[:, 5 * d:6 * d].astype(F32))
        ya = ya_ref[...].astype(F32)
        yb = yb_ref[...].astype(F32)

        dmob = dh_ref[...].astype(BF16)
        dmerged = _nn(dmob, w3_v[2])
        x3_ref[:, 0:d] = (sga * ya + sgb * yb).astype(BF16)
        y3_ref[:, 0:d] = dmob
        dya = sga * dmerged
        dyb = sgb * dmerged
        dn_parts = []

        def emit(q, val):
            vb = val.astype(BF16)
            dp_ref[:, q * d:(q + 1) * d] = vb
            acc(SG_BIN + q, val)
            term = _nn(vb, win_v[pl.ds(q * d, d), :])
            dn_parts[:] = [term if not dn_parts else dn_parts[0] + term]

        emit(4, dmerged * ya * sga * (1.0 - sga))
        emit(5, dmerged * yb * sgb * (1.0 - sgb))

        dyab = dya.astype(BF16)
        y3_ref[:, d:2 * d] = dyab
        dza = _nn(dyab, w3_v[0])
        hsv = hs_ref[...].astype(F32)
        gl, th = _gelu(y_rnn)
        x3_ref[:, d:2 * d] = (hsv * gl).astype(BF16)
        emit(1, dza * hsv * _gelu_grad(y_rnn, th))
        dhs = dza * gl
        xrb = xr_ref[...]
        xr = xrb.astype(F32)
        lam_v = lam_ref[...]
        r, ig, sp, a, s = _gates(xrb, wg_ref, ba_ref[...], bx_ref[...], lam_v, hd)
        b = jnp.where(rows == tm - 1, gcar[1:2, :], pltpu.roll(a, tm - 1, 0))
        big_g = _scan_bwd(b, dhs, gcar[0:1, :])
        gcar[0:1, :] = big_g[0:1, :]
        gcar[1:2, :] = a[0:1, :]
        h_before = jnp.where(tile > 0, hh_ref[halo_rows - 1:halo_rows, :].astype(F32), 0.0)
        h_prev = jnp.where(rows == 0, h_before, pltpu.roll(hsv, 1, 0))
        ds = big_g * ig * xr
        dla = big_g * h_prev * a - ds * (a * a) / jnp.maximum(s, 1e-20)
        acc(SG_LAM, dla * r * (RG_LRU_C * _sigmoid(-lam_v)))
        dpr = dla * (-RG_LRU_C * sp) * r * (1.0 - r)
        dpi = big_g * s * xr * ig * (1.0 - ig)
        acc(SG_BA, dpr)
        acc(SG_BX, dpi)
        dprb = dpr.astype(BF16)
        dpib = dpi.astype(BF16)
        yg_ref[:, 0:d] = dprb
        yg_ref[:, d:2 * d] = dpib
        back = []
        for hh in range(N_HEADS):
            sl = slice(hh * hd, (hh + 1) * hd)
            back.append(_nn(dprb[:, sl], wgt_ref[0, hh]) + _nn(dpib[:, sl], wgt_ref[1, hh]))
        dxr = big_g * s * ig + jnp.concatenate(back, axis=1)
        acc(SG_CB4, dxr)
        extd4[pl.ds(0, tm), :] = dxr
        dx_rnn = jnp.zeros((tm, d), F32)
        for k in range(k4):
            term = extd4[pl.ds(k4 - 1 - k, tm), :]
            dx_rnn = dx_rnn + cw4_ref[k:k + 1, :] * term
            acc(SG_CW4 + k, x_rnn * term)
        extd4[pl.ds(tm, CONV4_HALO), :] = extd4[pl.ds(0, CONV4_HALO), :]
        emit(0, dx_rnn)

        dybb = dyb.astype(BF16)
        y3_ref[:, 2 * d:3 * d] = dybb
        acc(SG_BCP, dyb)
        dv3 = _nn(dybb, w3_v[1])
        v1 = v1_ref[...].astype(F32)
        xc = v1 - jnp.mean(v1, axis=-1, keepdims=True)
        rstd = lax.rsqrt(jnp.mean(xc * xc, axis=-1, keepdims=True) + EPS)
        xhat = xc * rstd
        lng_v = lng_ref[...]
        v2 = xhat * lng_v + lnb_ref[...]
        s2 = _sigmoid(v2)
        x3_ref[:, 2 * d:3 * d] = (v2 * s2).astype(BF16)
        dv2 = dv3 * (s2 * (1.0 + v2 * (1.0 - s2)))
        acc(SG_LNG, dv2 * xhat)
        acc(SG_LNB, dv2)
        dxh = dv2 * lng_v
        dv1 = rstd * (dxh - jnp.mean(dxh, axis=-1, keepdims=True)
                      - xhat * jnp.mean(dxh * xhat, axis=-1, keepdims=True))
        acc(SG_CB31, dv1)
        extd31[pl.ds(0, tm), :] = dv1
        _shifted_copies(extd31, es31, tm + CONV31_HALO - SUBLANES)
        sgg = _sigmoid(glu_g)
        v0 = glu_v * sgg
        dv0 = jnp.zeros((tm, d), F32)
        for k in range(k31):
            term = _tap(extd31, es31, k31 - 1 - k, tm)
            dv0 = dv0 + cw31_ref[k:k + 1, :] * term
            acc(SG_CW31 + k, v0 * term)
        extd31[pl.ds(tm, CONV31_HALO), :] = extd31[pl.ds(0, CONV31_HALO), :]
        emit(2, dv0 * sgg)
        emit(3, dv0 * glu_v * sgg * (1.0 - sgg))

        dn = dn_parts[0]
        x = h_ref[...]
        rr = lax.rsqrt(jnp.mean(x * x, axis=-1, keepdims=True) + EPS)
        dx, dgp = _rms_bwd(dn, x, rr, g_ref[...])
        dh1_ref[...] = dh_ref[...] + dx
        sg_ref[SG_MIX:SG_MIX + 1, :] += dgp

    rev = lambda i: (nt - 1 - i, 0)
    row = pl.BlockSpec((tm, d), rev)
    wide = pl.BlockSpec((tm, n_in), rev)
    full = lambda a: pl.BlockSpec(a.shape, lambda i, nd=a.ndim: (0,) * nd)
    halo = pl.BlockSpec((halo_rows, d), lambda i: (jnp.maximum((nt - 1 - i) * per - 1, 0), 0))
    smalls = [cw4, wg, jnp.swapaxes(wg, 2, 3), ba, bx, lam, cw31, lng, lnb]
    return _call(
        body, "mixer_bwd", (nt,),
        [row, row, full(g), wide, row, row, halo, row, row, row, _any()]
        + [full(a) for a in smalls] + [_any()],
        [row, wide, pl.BlockSpec((tm, 3 * d), rev), pl.BlockSpec((tm, 3 * d), rev),
         pl.BlockSpec((tm, 2 * d), rev), pl.BlockSpec((sg_rows, d), lambda i: (0, 0))],
        [jax.ShapeDtypeStruct((tp, d), F32), jax.ShapeDtypeStruct((tp, n_in), BF16),
         jax.ShapeDtypeStruct((tp, 3 * d), BF16), jax.ShapeDtypeStruct((tp, 3 * d), BF16),
         jax.ShapeDtypeStruct((tp, 2 * d), BF16), jax.ShapeDtypeStruct((sg_rows, d), F32)],
        [pltpu.VMEM(win_t.shape, BF16),
         pltpu.VMEM((3, d, d), BF16),
         pltpu.VMEM((tm + CONV4_HALO, d), F32),
         pltpu.VMEM((tm + CONV31_HALO, d), F32),
         pltpu.VMEM((SUBLANES, tm + CONV31_HALO, d), F32),
         pltpu.VMEM((SUBLANES, d), F32),
         pltpu.SemaphoreType.DMA((1 + 3 * N_DEV,))],
        [dh2, h, g, proj, xr_s, hs_s, hs_s, v1_s, ya_s, yb_s, win_t, *smalls, w3_all], comm)


def _tn_matmul(name, x, y, x_spec, y_spec, n_blocks, kb, nb, tm, tp, out_shape, out_spec, out_view, comm=None,
               after=None):
    nt = tp // tm

    def body(x_ref, y_ref, *refs):
        o_ref, acc = refs[-2:]
        i = pl.program_id(1)

        @pl.when(i == 0)
        def _():
            acc[...] = jnp.zeros_like(acc)

        acc[...] += _tn(x_ref[...], y_ref[...])

        @pl.when(i == nt - 1)
        def _():
            o_ref[...] = acc[...].astype(BF16).reshape(out_view)

    follows = [] if after is None else [after]
    outs, extra = _call(body, name, (n_blocks, nt), [x_spec, y_spec] + [_any()] * len(follows), [out_spec],
                        [jax.ShapeDtypeStruct(out_shape, BF16)], [pltpu.VMEM((kb, nb), F32)], [x, y] + follows,
                        comm)
    return outs[0], extra


def kernel(x, meta_tokens, ffn1_norm, ffn1_w_gu, ffn1_w_down, mix_norm, w_in, b_in, rnn_conv_w, rnn_conv_b, rg_w_a, rg_b_a, rg_w_x, rg_b_x, rg_lambda, rnn_w_proj, conv_dw_w, conv_dw_b, conv_ln_g, conv_ln_b, conv_w_proj, conv_b_proj, w_out, ffn2_norm, ffn2_w_gu, ffn2_w_down, final_norm, loss_target, m_meta_tokens, m_ffn1_norm, m_ffn1_w_gu, m_ffn1_w_down, m_mix_norm, m_w_in, m_b_in, m_rnn_conv_w, m_rnn_conv_b, m_rg_w_a, m_rg_b_a, m_rg_w_x, m_rg_b_x, m_rg_lambda, m_rnn_w_proj, m_conv_dw_w, m_conv_dw_b, m_conv_ln_g, m_conv_ln_b, m_conv_w_proj, m_conv_b_proj, m_w_out, m_ffn2_norm, m_ffn2_w_gu, m_ffn2_w_down, m_final_norm, v_meta_tokens, v_ffn1_norm, v_ffn1_w_gu, v_ffn1_w_down, v_mix_norm, v_w_in, v_b_in, v_rnn_conv_w, v_rnn_conv_b, v_rg_w_a, v_rg_b_a, v_rg_w_x, v_rg_b_x, v_rg_lambda, v_rnn_w_proj, v_conv_dw_w, v_conv_dw_b, v_conv_ln_g, v_conv_ln_b, v_conv_w_proj, v_conv_b_proj, v_w_out, v_ffn2_norm, v_ffn2_w_gu, v_ffn2_w_down, v_final_norm):
    w = dict(locals())
    seq, d = x.shape[1], x.shape[2]
    n_meta = meta_tokens.shape[0]
    t_real = n_meta + seq
    tp, tm, tmx_fwd, tmx, tmt, tmw = _tiles(t_real)
    fb = ffn1_w_gu.shape[-1]
    wr = ffn1_w_down.shape[1]
    f = N_DEV * wr
    fc = f // FFN_CHUNKS
    nbc = w_in.shape[-1]
    n_in = N_DEV * nbc
    pr = rnn_w_proj.shape[1]
    hd = rg_w_a.shape[-1]
    gr = rg_w_a.shape[2]
    cw = meta_tokens.shape[1]
    k4, k31 = rnn_conv_w.shape[1], conv_dw_w.shape[1]
    assert n_in == 6 * d and 2 * wr == fb and N_HEADS * hd == d and pr * N_DEV == d

    xi, yi, ci = lax.axis_index("x"), lax.axis_index("y"), lax.axis_index("c")
    core = ci.astype(jnp.int32).reshape(1)
    chip = (2 * xi + yi).astype(jnp.int32).reshape(1)
    me_index = (4 * xi + 2 * yi + ci).astype(jnp.int32).reshape(1)

    for nm in ("ffn1_w_gu", "ffn2_w_gu"):
        for pre in ("", "m_", "v_"):
            w[pre + nm] = jnp.swapaxes(w[pre + nm], 1, 2)

    wgut1 = w["ffn1_w_gu"][0].astype(BF16)
    wgut2 = w["ffn2_w_gu"][0].astype(BF16)
    wd1 = ffn1_w_down[0].astype(BF16)
    wd2 = ffn2_w_down[0].astype(BF16)
    win_loc = w_in[0].astype(BF16)
    win_t_loc = jnp.swapaxes(w_in[0], 0, 1).astype(BF16)
    w3_loc = jnp.concatenate([rnn_w_proj[0], conv_w_proj[0], w_out[0]], axis=0).astype(BF16)
    wg_loc = jnp.stack([rg_w_a[0], rg_w_x[0]]).astype(BF16)
    n_small = n_meta + k4 + k31
    small_rows = -(-n_small // SUBLANES) * SUBLANES
    small_loc = jnp.concatenate([meta_tokens, rnn_conv_w[0], conv_dw_w[0],
                                 jnp.zeros((small_rows - n_small, cw), F32)], axis=0)
    (wgut1_all, wd1_all, wg_all, small_all), h0, tgt = _first_gather(
        [wgut1, wd1, wg_loc, small_loc], 3, x[0], loss_target[0], n_meta, tp)
    wg = wg_all.transpose(1, 2, 0, 3, 4).reshape(2, N_HEADS, hd, hd)
    small_full = small_all.transpose(1, 0, 2).reshape(small_rows, d)
    cw4 = small_full[n_meta:n_meta + k4]
    cw31 = small_full[n_meta + k4:n_meta + k4 + k31]

    wgu1, wdn1 = wgut1_all.reshape(2 * f, d), wd1_all.reshape(f, d)
    (h1, gu1, n1), (win_all, w3_all) = _ffn_fwd(h0, ffn1_norm, wgu1, wdn1, tm,
                                                comm=_Gather([win_loc, w3_loc], pass_on_at=(0.55, 0.9)))
    (h2, proj, n2, xr_s, hs_s, v1_s, ya_s, yb_s), (wgut2_all, wd2_all) = _mixer_fwd(
        h1, mix_norm, b_in, win_all, cw4, rnn_conv_b, wg, rg_b_a, rg_b_x, rg_lambda, cw31, conv_dw_b, conv_ln_g,
        conv_ln_b, conv_b_proj, w3_all, tmx_fwd, comm=_Gather([wgut2, wd2], pass_on_at=(0.3, 0.5)))
    wgu2, wdn2 = wgut2_all.reshape(2 * f, d), wd2_all.reshape(f, d)
    (dh3, gu2, n3, tail), (win_t_all,) = _ffn_fwd(
        h2, ffn2_norm, wgu2, wdn2, tm, loss=(tgt, final_norm.reshape(1, d), n_meta, t_real),
        comm=_Gather([win_t_loc], pass_on_at=(0.45, 0.75)))
    win_t = win_t_all.reshape(n_in, d)

    def d_w_gu(tag, dgu, n_s, comm=None):
        g, extra = _tn_matmul(
            "d_w_gu" + tag, dgu, n_s,
            pl.BlockSpec((None, tmt, fc), lambda b, i: (b // FFN_CHUNKS, i, b % FFN_CHUNKS)),
            pl.BlockSpec((tmt, d), lambda b, i: (i, 0)),
            2 * FFN_CHUNKS, fc, d, tmt, tp, (2 * FFN_CHUNKS, fc, d),
            pl.BlockSpec((None, fc, d), lambda b, i: (b, 0, 0)), (fc, d), comm)
        return g.reshape(N_DEV, fb, d), extra

    def d_w_down(tag, act, df, comm=None):
        g, extra = _tn_matmul(
            "d_w_down" + tag, act, df,
            pl.BlockSpec((tmt, fc), lambda b, i: (i, b)), pl.BlockSpec((tmt, d), lambda b, i: (i, 0)),
            FFN_CHUNKS, fc, d, tmt, tp, (FFN_CHUNKS, fc, d),
            pl.BlockSpec((None, fc, d), lambda b, i: (b, 0, 0)), (fc, d), comm)
        return g.reshape(N_DEV, wr, d), extra

    (dh2, dgu2, act2, df2, tail), _ = _ffn_bwd(dh3, h2, gu2, ffn2_norm, wgu2, wdn2, tm, tail, TAIL_FFN2, n3)
    g_wgu2, _ = d_w_gu("2", dgu2, n3)
    g_wd2, _ = d_w_down("2", act2, df2)
    (dh1, dproj, x3, y3, yg, sg), (r_wd2, r_wgu2) = _mixer_bwd(
        dh2, h1, mix_norm, proj, xr_s, hs_s, v1_s, ya_s, yb_s, win_t, cw4, wg, rg_b_a, rg_b_x, rg_lambda, cw31,
        conv_ln_g, conv_ln_b, w3_all, tmx, comm=_Scatter([g_wd2, g_wgu2]))
    g_w3, _ = _tn_matmul(
        "d_w_proj3", x3, y3,
        pl.BlockSpec((tmw, d), lambda b, i: (i, b)), pl.BlockSpec((tmw, d), lambda b, i: (i, b)),
        3, d, d, tmw, tp, (N_DEV, 3, pr, d), pl.BlockSpec((N_DEV, None, pr, d), lambda b, i: (0, b, 0, 0)),
        (N_DEV, pr, d))
    g_wg, _ = _tn_matmul(
        "d_w_gates", xr_s, yg,
        pl.BlockSpec((tmw, hd), lambda b, i: (i, b % N_HEADS)), pl.BlockSpec((tmw, hd), lambda b, i: (i, b)),
        2 * N_HEADS, hd, hd, tmw, tp, (N_DEV, 2 * N_HEADS, gr, hd),
        pl.BlockSpec((N_DEV, None, gr, hd), lambda b, i: (0, b, 0, 0)), (N_DEV, gr, hd))
    w3_sems, g_w3_thru, w3_land, w3_token = _exchange_start("grads_proj3_exchange", _scatter_copies, N_DEV - 1, g_w3)
    g_win, (r_wg,) = _tn_matmul(
        "d_w_in", n2, dproj,
        pl.BlockSpec((tmw, d), lambda b, i: (i, 0)), pl.BlockSpec((tmw, nbc), lambda b, i: (i, b)),
        N_DEV, d, nbc, tmw, tp, (N_DEV, d, nbc), pl.BlockSpec((None, d, nbc), lambda b, i: (b, 0, 0)), (d, nbc),
        comm=_Scatter([g_wg]), after=w3_token)
    win_sems, g_win_thru, win_land, win_token = _exchange_start("grads_w_in_exchange", _scatter_copies, N_DEV - 1, g_win)
    (dh0, dgu1, act1, df1, tail), _ = _ffn_bwd(dh1, h0, gu1, ffn1_norm, wgu1, wdn1, tm, tail, TAIL_FFN1, win_token)
    grad_x = dh0[n_meta:t_real][None]

    pieces = [sg, dh0[:n_meta], tail]
    assert all(p.shape[0] % SUBLANES == 0 for p in pieces)
    at = [0, sg.shape[0], sg.shape[0] + n_meta]
    loss_row = at[2] + TAIL_LOSS
    rep_rows = [("ffn1_norm", at[2] + TAIL_FFN1, 1), ("mix_norm", SG_MIX, 1), ("b_in", SG_BIN, 6),
                ("rnn_conv_b", SG_CB4, 1),
                ("rg_b_a", SG_BA, 1), ("rg_b_x", SG_BX, 1), ("rg_lambda", SG_LAM, 1), ("conv_dw_b", SG_CB31, 1),
                ("conv_ln_g", SG_LNG, 1), ("conv_ln_b", SG_LNB, 1), ("conv_b_proj", SG_BCP, 1),
                ("ffn2_norm", at[2] + TAIL_FFN2, 1), ("final_norm", at[2] + TAIL_FINAL, 1)]
    col_rows = [("meta_tokens", at[1], n_meta), ("rnn_conv_w", SG_CW4, k4), ("conv_dw_w", SG_CW31, k31)]
    layout = []
    for nm, row0, nr in rep_rows:
        kind = "wide" if nm == "b_in" else "rep"
        as2d = lambda a: a.reshape(1, -1) if a.ndim == 1 else a
        layout.append((kind, row0, nr, as2d(w[nm]), as2d(w["m_" + nm]), as2d(w["v_" + nm])))
    for nm, row0, nr in col_rows:
        sq = lambda a: a.reshape(a.shape[-2], a.shape[-1])
        layout.append(("col", row0, nr, sq(w[nm]), sq(w["m_" + nm]), sq(w["v_" + nm])))
    small_partial = jnp.concatenate(pieces, axis=0)

    g_wd1, (small_partials,) = d_w_down("1", act1, df1, comm=_Bcast(small_partial))
    g_wgu1, (r_wd1,) = d_w_gu("1", dgu1, n1, comm=_Scatter([g_wd1]))

    g_last = g_wgu1.reshape((4, 2) + g_wgu1.shape[1:])
    (from_sibling,) = _pair_exchange([g_last])
    comb_wgu1 = _pair_add(g_last, from_sibling, core)
    sems, comb_thru, land_thru, after = _exchange_start("grads_chip_exchange", _chip_copies, 3, comb_wgu1)
    g_win, r_win = _exchange_wait("grads_w_in_exchange", _scatter_copies, win_sems, g_win_thru, win_land, after)
    g_w3, r_w3 = _exchange_wait("grads_proj3_exchange", _scatter_copies, w3_sems, g_w3_thru, w3_land, after)

    groups = [(g_wd1, r_wd1, me_index, ["ffn1_w_down"]),
              (g_wd2, r_wd2, me_index, ["ffn2_w_down"]), (g_wgu2, r_wgu2, me_index, ["ffn2_w_gu"]),
              (g_win, r_win, me_index, ["w_in"]), (g_w3, r_w3, me_index, ["w_out", "rnn_w_proj", "conv_w_proj"]),
              (g_wg, r_wg, me_index, ["rg_w_a", "rg_w_x"]), (None, None, chip, ["ffn1_w_gu"])]
    res = {}
    for own, recv, idx, group in groups:
        if own is None:
            own, recv = _exchange_wait("grads_chip_exchange", _chip_copies, sems, comb_thru, land_thru, after)
        outs = _final_adamw(own, recv, idx, [(w[nm], w["m_" + nm], w["v_" + nm]) for nm in group], after)
        after = outs[-1][0]
        for nm, o in zip(group, outs):
            res[nm] = o
    for nm in ("ffn1_w_gu", "ffn2_w_gu"):
        res[nm] = tuple(jnp.swapaxes(a, 1, 2) for a in res[nm])

    total, small_out = _small_adamw(small_partials, layout, me_index)
    for (nm, _, _), o in zip(rep_rows + col_rows, small_out):
        res[nm] = tuple(a.reshape(w[nm].shape) for a in o)

    order = ["meta_tokens", "ffn1_norm", "ffn1_w_gu", "ffn1_w_down", "mix_norm", "w_in", "b_in", "rnn_conv_w",
             "rnn_conv_b", "rg_w_a", "rg_b_a", "rg_w_x", "rg_b_x", "rg_lambda", "rnn_w_proj", "conv_dw_w",
             "conv_dw_b", "conv_ln_g", "conv_ln_b", "conv_w_proj", "conv_b_proj", "w_out", "ffn2_norm",
             "ffn2_w_gu", "ffn2_w_down", "final_norm"]
    return (total[loss_row, 0], grad_x, *[res[nm][0] for nm in order], *[res[nm][1] for nm in order],
            *[res[nm][2] for nm in order], *[res[nm][3] for nm in order])
```

```python
import functools
import math

import jax
import jax.numpy as jnp
from jax import lax
from jax.experimental import pallas as pl
from jax.experimental.pallas import tpu as pltpu

F32 = jnp.float32
BF16 = jnp.bfloat16
MESH = pl.DeviceIdType.MESH
N_DEV = 8
N_HEADS = 4
RG_LRU_C = 8.0
EPS = 1e-6
FFN_RES = 0.5
ADAM_LR, ADAM_B1, ADAM_B2, ADAM_EPS, ADAM_WD, ADAM_STEP = 0.001, 0.9, 0.999, 1e-08, 0.01, 10
V7X_VMEM_LIMIT = 56 * 1024 * 1024
CONV4_HALO = 8
CONV31_HALO = 32
SUBLANES = 8
STAGE_ROWS = 512
TAIL_FFN1, TAIL_FINAL, TAIL_LOSS, TAIL_FFN2 = 0, 1, 2, 3
FFN_CHUNKS = 2
FFN_FWD_CHUNKS = 1
GELU_C = math.sqrt(2.0 / math.pi)
GELU_K = 0.044715


def _any():
    return pl.BlockSpec(memory_space=pl.ANY)


def _params(n_grid):
    return pltpu.CompilerParams(dimension_semantics=("arbitrary",) * n_grid, vmem_limit_bytes=V7X_VMEM_LIMIT)


def _nn(a, b):
    return jnp.dot(a, b, preferred_element_type=F32)


def _nt(a, b):
    return lax.dot_general(a, b, (((1,), (1,)), ((), ())), preferred_element_type=F32)


def _tn(a, b):
    return lax.dot_general(a, b, (((0,), (0,)), ((), ())), preferred_element_type=F32)


def _sigmoid(x):
    return 0.5 * jnp.tanh(0.5 * x) + 0.5


def _rowsum(x):
    return jnp.sum(x, axis=0, keepdims=True)


def _rms_fwd(x, g):
    r = lax.rsqrt(jnp.mean(x * x, axis=-1, keepdims=True) + EPS)
    return x * r * g, r


def _rms_bwd(dn, x, r, g):
    xr = x * r
    gy = dn * g
    dx = r * (gy - xr * jnp.mean(gy * xr, axis=-1, keepdims=True))
    return dx, _rowsum(dn * xr)


def _gelu(y):
    t = jnp.tanh(GELU_C * (y + GELU_K * y * y * y))
    return 0.5 * y * (1.0 + t), t


def _gelu_grad(y, t):
    return 0.5 * (1.0 + t) + 0.5 * y * (1.0 - t * t) * GELU_C * (1.0 + 3.0 * GELU_K * y * y)


def _softplus(x):
    return jnp.maximum(x, 0.0) + jnp.log(1.0 + jnp.exp(-jnp.abs(x)))


def _one_minus_exp(z):
    series = -z * (1.0 + 0.5 * z * (1.0 + z * (1.0 / 3.0) * (1.0 + 0.25 * z)))
    return jnp.where(z > -0.05, series, 1.0 - jnp.exp(z))


def _tiles(t_real):
    if t_real > 2048:
        tm = 384
        tp = -(-t_real // tm) * tm
        return tp, tm, tm // 2, tm // 2, tp // 2, tp
    tm = 128
    tp = -(-t_real // tm) * tm
    return tp, tm, tm // 2, tm // 2, tm, tm


def _load_weights(copies, sems):
    cps = [pltpu.make_async_copy(s, d, sems.at[k]) for k, (s, d) in enumerate(copies)]
    for cp in cps:
        cp.start()
    for cp in cps:
        cp.wait()


def _position():
    x, y, c = lax.axis_index("x"), lax.axis_index("y"), lax.axis_index("c")
    chips = [(1 - x, y), (x, 1 - y), (1 - x, 1 - y)]
    return x, y, c, chips


def _slot(p):
    return 4 * p[0] + 2 * p[1] + p[2]


class _Lazy(dict):
    def __getitem__(self, key):
        val = dict.__getitem__(self, key)
        return val() if callable(val) else val


class _Gather:
    def __init__(self, shards, pass_on_at=None):
        self.shards = list(shards)
        self.n = len(self.shards)
        self.pass_on_at = pass_on_at

    def inputs(self):
        return self.shards

    def out_shape(self):
        return [jax.ShapeDtypeStruct((N_DEV,) + s.shape, s.dtype) for s in self.shards]

    N_SEMS = 9

    def scratch(self):
        return [pltpu.SemaphoreType.DMA((self.N_SEMS * self.n,)), pltpu.SemaphoreType.DMA((self.N_SEMS * self.n,)),
                pltpu.SemaphoreType.DMA((self.n,))]

    def _plan(self, ins, outs, sems):
        send_sems, recv_sems, local_sems = sems
        x, y, c, _ = _position()
        me, sib, xn, yn, dg = (x, y, c), (x, y, 1 - c), (1 - x, y, c), (x, 1 - y, c), (1 - x, 1 - y, c)
        other = lambda p: (p[0], p[1], 1 - c)

        def blk(a, p, half=None):
            ref = outs[a].at[_slot(p)]
            if half is None:
                return ref
            rows = self.shards[a].shape[0] // 2
            return ref.at[pl.ds(half * rows, rows)]

        def copy(a, k, dst, to, src=None):
            return pltpu.make_async_remote_copy(
                src_ref=dst if src is None else src, dst_ref=dst,
                send_sem=send_sems.at[self.N_SEMS * a + k], recv_sem=recv_sems.at[self.N_SEMS * a + k],
                device_id=to, device_id_type=MESH)

        cp = _Lazy(mine=lambda: [pltpu.make_async_copy(ins[a], blk(a, me), local_sems.at[a]) for a in range(self.n)])
        for a in range(self.n):
            cp[a] = _Lazy(
                own=lambda a=a: [copy(a, 0, blk(a, me), sib, src=ins[a]), copy(a, 1, blk(a, me), xn, src=ins[a]),
                                 copy(a, 2, blk(a, me), yn, src=ins[a])],
                from_x=lambda a=a: copy(a, 1, blk(a, xn), me), from_y=lambda a=a: copy(a, 2, blk(a, yn), me),
                relay_x=lambda a=a: copy(a, 3, blk(a, xn, 0), yn), relay_y=lambda a=a: copy(a, 4, blk(a, yn, 1), xn),
                diag0=lambda a=a: copy(a, 3, blk(a, dg, 0), me), diag1=lambda a=a: copy(a, 4, blk(a, dg, 1), me),
                pass_x=lambda a=a: copy(a, 5, blk(a, xn), sib), pass_y=lambda a=a: copy(a, 6, blk(a, yn), sib),
                pass_d0=lambda a=a: copy(a, 7, blk(a, dg, 0), sib), pass_d1=lambda a=a: copy(a, 8, blk(a, dg, 1), sib),
                from_sib=lambda a=a: [copy(a, 0, blk(a, sib), me), copy(a, 5, blk(a, other(xn)), me),
                                      copy(a, 6, blk(a, other(yn)), me), copy(a, 7, blk(a, other(dg), 0), me),
                                      copy(a, 8, blk(a, other(dg), 1), me)])
        return cp

    def start(self, ins, outs, sems):
        cp = self._plan(ins, outs, sems)
        for c in cp["mine"]:
            c.start()
        for a in range(self.n):
            for c in cp[a]["own"]:
                c.start()

    def pass_on(self, ins, outs, sems):
        cp = self._plan(ins, outs, sems)
        for a in range(self.n):
            cp[a]["from_x"].wait_recv()
            cp[a]["relay_x"].start()
            cp[a]["pass_x"].start()
        for a in range(self.n):
            cp[a]["from_y"].wait_recv()
            cp[a]["relay_y"].start()
            cp[a]["pass_y"].start()

    def pass_on_relayed(self, ins, outs, sems):
        cp = self._plan(ins, outs, sems)
        for a in range(self.n):
            cp[a]["diag0"].wait_recv()
            cp[a]["pass_d0"].start()
            cp[a]["diag1"].wait_recv()
            cp[a]["pass_d1"].start()

    def finish(self, ins, outs, sems):
        if self.pass_on_at is None:
            self.pass_on(ins, outs, sems)
            self.pass_on_relayed(ins, outs, sems)
        cp = self._plan(ins, outs, sems)
        for a in range(self.n):
            for c in cp[a]["from_sib"]:
                c.wait_recv()
            for c in cp[a]["own"] + [cp[a][k] for k in ("relay_x", "relay_y", "pass_x", "pass_y", "pass_d0", "pass_d1")]:
                c.wait_send()
        for c in cp["mine"]:
            c.wait()


class _Scatter:
    def __init__(self, grads):
        self.grads = list(grads)
        self.n = len(self.grads)

    def inputs(self):
        return self.grads

    def out_shape(self):
        return [jax.ShapeDtypeStruct((N_DEV - 1,) + g.shape[1:], g.dtype) for g in self.grads]

    def scratch(self):
        return [pltpu.SemaphoreType.DMA((7 * self.n,)), pltpu.SemaphoreType.DMA((7 * self.n,))]

    def _plan(self, ins, outs, sems):
        send_sems, recv_sems = sems
        x, y, c, _ = _position()
        cps = []
        for a in range(self.n):
            for k in range(1, N_DEV):
                peer = (x ^ (k >> 2), y ^ ((k >> 1) & 1), c ^ (k & 1))
                cps.append(pltpu.make_async_remote_copy(
                    src_ref=ins[a].at[_slot(peer)], dst_ref=outs[a].at[k - 1],
                    send_sem=send_sems.at[7 * a + k - 1], recv_sem=recv_sems.at[7 * a + k - 1],
                    device_id=peer, device_id_type=MESH))
        return cps

    def start(self, ins, outs, sems):
        for cp in self._plan(ins, outs, sems):
            cp.start()

    def finish(self, ins, outs, sems):
        for cp in self._plan(ins, outs, sems):
            cp.wait()


def _hosted(inner, n_in, n_out, comm, grid):
    if comm is None:
        return inner
    nc_in, nc_out, ns = len(comm.inputs()), len(comm.out_shape()), len(comm.scratch())

    def body(*refs):
        o0 = n_in + nc_in
        s0 = o0 + n_out + nc_out
        main = refs[:n_in] + refs[o0:o0 + n_out] + refs[s0:len(refs) - ns]
        c_in, c_out, c_sems = refs[n_in:o0], refs[o0 + n_out:s0], refs[len(refs) - ns:]
        ids = [pl.program_id(ax) for ax in range(len(grid))]
        first = functools.reduce(jnp.logical_and, [i == 0 for i in ids])
        last = functools.reduce(jnp.logical_and, [i == g - 1 for i, g in zip(ids, grid)])

        @pl.when(first)
        def _():
            comm.start(c_in, c_out, c_sems)

        inner(*main)

        if getattr(comm, "pass_on_at", None) is not None:
            assert len(grid) == 1
            first_at, second_at = (min(grid[0] - 1, int(frac * grid[0])) for frac in comm.pass_on_at)
            assert first_at < second_at

            @pl.when(ids[0] == first_at)
            def _():
                comm.pass_on(c_in, c_out, c_sems)

            @pl.when(ids[0] == second_at)
            def _():
                comm.pass_on_relayed(c_in, c_out, c_sems)

        @pl.when(last)
        def _():
            comm.finish(c_in, c_out, c_sems)

    return body


def _call(inner, name, grid, in_specs, out_specs, out_shape, scratch, args, comm=None):
    n_in, n_out = len(args), len(out_shape)
    body = _hosted(inner, n_in, n_out, comm, grid)
    if comm is not None:
        in_specs = list(in_specs) + [_any()] * len(comm.inputs())
        args = list(args) + comm.inputs()
        out_specs = list(out_specs) + [_any()] * len(comm.out_shape())
        out_shape = list(out_shape) + comm.out_shape()
        scratch = list(scratch) + comm.scratch()
    outs = pl.pallas_call(
        body, name=name, grid=grid, in_specs=list(in_specs), out_specs=list(out_specs), out_shape=list(out_shape),
        scratch_shapes=list(scratch), compiler_params=_params(len(grid)))(*args)
    return list(outs[:n_out]), list(outs[n_out:])


class _Bcast:
    def __init__(self, block):
        self.block = block

    def inputs(self):
        return [self.block]

    def out_shape(self):
        return [jax.ShapeDtypeStruct((N_DEV,) + self.block.shape, self.block.dtype)]

    def scratch(self):
        return [pltpu.SemaphoreType.DMA((N_DEV - 1,)), pltpu.SemaphoreType.DMA((N_DEV - 1,)),
                pltpu.SemaphoreType.DMA((1,))]

    def _plan(self, ins, outs, sems):
        send_sems, recv_sems, local_sem = sems
        x, y, c, _ = _position()
        mine = outs[0].at[_slot((x, y, c))]
        cps = []
        for k in range(1, N_DEV):
            peer = (x ^ (k >> 2), y ^ ((k >> 1) & 1), c ^ (k & 1))
            cps.append(pltpu.make_async_remote_copy(
                src_ref=ins[0], dst_ref=mine, send_sem=send_sems.at[k - 1], recv_sem=recv_sems.at[k - 1],
                device_id=peer, device_id_type=MESH))
        return pltpu.make_async_copy(ins[0], mine, local_sem.at[0]), cps

    def start(self, ins, outs, sems):
        own, cps = self._plan(ins, outs, sems)
        own.start()
        for cp in cps:
            cp.start()

    def finish(self, ins, outs, sems):
        own, cps = self._plan(ins, outs, sems)
        for cp in cps:
            cp.wait()
        own.wait()


def _first_gather(shards, small_idx, x2, t2, n_meta, tp):
    comm = _Gather(shards)
    n = comm.n
    seq, d = x2.shape
    t_real = n_meta + seq
    n_pad = tp - t_real
    cw = d // N_DEV
    rows = STAGE_ROWS if seq % STAGE_ROWS == 0 else seq
    n_chunks = seq // rows

    def body(*refs):
        ins, (x_ref, t_ref) = refs[:n], refs[n:n + 2]
        outs, (h0_ref, tg_ref) = refs[n + 2:2 * n + 2], refs[2 * n + 2:2 * n + 4]
        sems = refs[2 * n + 4:2 * n + 7]
        buf, zeros, in_sems, out_sems, misc_sems = refs[2 * n + 7:]
        comm.start(ins, outs, sems)
        zeros[...] = jnp.zeros_like(zeros)
        fills = [pltpu.make_async_copy(zeros.at[pl.ds(0, n_pad)], h0_ref.at[pl.ds(t_real, n_pad)], misc_sems.at[0]),
                 pltpu.make_async_copy(zeros.at[pl.ds(0, n_pad)], tg_ref.at[pl.ds(t_real, n_pad)], misc_sems.at[1]),
                 pltpu.make_async_copy(zeros.at[pl.ds(0, n_meta)], tg_ref.at[pl.ds(0, n_meta)], misc_sems.at[2])]
        for cp in fills:
            cp.start()
        jobs = [(src, dst, c) for src, dst in ((x_ref, h0_ref), (t_ref, tg_ref)) for c in range(n_chunks)]

        def load(k):
            src, _, c = jobs[k]
            return pltpu.make_async_copy(src.at[pl.ds(c * rows, rows)], buf.at[k % 2], in_sems.at[k % 2])

        def store(k):
            _, dst, c = jobs[k]
            return pltpu.make_async_copy(buf.at[k % 2], dst.at[pl.ds(n_meta + c * rows, rows)], out_sems.at[k % 2])

        load(0).start()
        for k in range(len(jobs)):
            load(k).wait()
            if k + 1 < len(jobs):
                if k >= 1:
                    store(k - 1).wait()
                load(k + 1).start()
            store(k).start()
        for k in range(max(0, len(jobs) - 2), len(jobs)):
            store(k).wait()
        comm.finish(ins, outs, sems)
        meta = [pltpu.make_async_copy(outs[small_idx].at[k, pl.ds(0, n_meta)],
                                      h0_ref.at[pl.ds(0, n_meta), pl.ds(k * cw, cw)], misc_sems.at[3 + k])
                for k in range(N_DEV)]
        for cp in meta:
            cp.start()
        for cp in fills + meta:
            cp.wait()

    staged = [jax.ShapeDtypeStruct((tp, d), F32)] * 2
    outs = pl.pallas_call(
        body, name="weights_all_gather", out_shape=comm.out_shape() + staged,
        in_specs=[_any()] * (n + 2), out_specs=[_any()] * (n + 2),
        scratch_shapes=comm.scratch() + [
            pltpu.VMEM((2, rows, d), F32), pltpu.VMEM((max(n_pad, n_meta), d), F32),
            pltpu.SemaphoreType.DMA((2,)), pltpu.SemaphoreType.DMA((2,)), pltpu.SemaphoreType.DMA((3 + N_DEV,))],
        compiler_params=pltpu.CompilerParams(vmem_limit_bytes=V7X_VMEM_LIMIT),
    )(*shards, x2, t2)
    return outs[:n], outs[n], outs[n + 1]


def _chip_copies(c_ref, land_ref, sems):
    _, _, c, chips = _position()
    return [pltpu.make_async_remote_copy(
        src_ref=c_ref.at[2 * cx + cy], dst_ref=land_ref.at[j], send_sem=sems[j], recv_sem=sems[3 + j],
        device_id=(cx, cy, c), device_id_type=MESH) for j, (cx, cy) in enumerate(chips)]


def _scatter_copies(g_ref, land_ref, sems):
    x, y, c, _ = _position()
    cps = []
    for k in range(1, N_DEV):
        peer = (x ^ (k >> 2), y ^ ((k >> 1) & 1), c ^ (k & 1))
        cps.append(pltpu.make_async_remote_copy(
            src_ref=g_ref.at[_slot(peer)], dst_ref=land_ref.at[k - 1], send_sem=sems[k - 1],
            recv_sem=sems[N_DEV - 1 + k - 1], device_id=peer, device_id_type=MESH))
    return cps


def _exchange_start(name, copies, n_copies, src):
    hbm = pl.BlockSpec(memory_space=pltpu.HBM)
    sem = pl.BlockSpec(memory_space=pltpu.SEMAPHORE)
    n_sems = 2 * n_copies

    def body(s_ref, land_ref, *refs):
        for cp in copies(s_ref, land_ref, refs[:n_sems]):
            cp.start()
        token = refs[n_sems + 2]
        token[...] = jnp.zeros_like(token)

    land = lax.empty((n_copies,) + src.shape[1:], src.dtype)
    outs = pl.pallas_call(
        body, name=name + "_start",
        out_shape=(pltpu.SemaphoreType.DMA(()),) * n_sems
        + (pltpu.HBM(src.shape, src.dtype), pltpu.HBM(land.shape, land.dtype),
           jax.ShapeDtypeStruct((SUBLANES, 128), F32)),
        in_specs=(hbm, hbm), out_specs=(sem,) * n_sems + (hbm, hbm, pl.BlockSpec(memory_space=pltpu.VMEM)),
        input_output_aliases={0: n_sems, 1: n_sems + 1},
        compiler_params=pltpu.CompilerParams(has_side_effects=pltpu.SideEffectType.DATAFLOW_SIDE_EFFECTING),
    )(pltpu.with_memory_space_constraint(src, pltpu.HBM), pltpu.with_memory_space_constraint(land, pltpu.HBM))
    return outs[:n_sems], outs[n_sems], outs[n_sems + 1], outs[n_sems + 2]


def _exchange_wait(name, copies, sems, src_thru, land_thru, after):
    hbm = pl.BlockSpec(memory_space=pltpu.HBM)
    sem = pl.BlockSpec(memory_space=pltpu.SEMAPHORE)
    n_sems = len(sems)

    def body(s_ref, land_ref, *refs):
        for cp in copies(s_ref, land_ref, refs[:n_sems]):
            cp.wait_send()
            cp.wait_recv()

    return pl.pallas_call(
        body, name=name + "_wait",
        out_shape=(pltpu.HBM(src_thru.shape, src_thru.dtype), pltpu.HBM(land_thru.shape, land_thru.dtype)),
        in_specs=(hbm, hbm) + (sem,) * n_sems + (pl.BlockSpec(memory_space=pl.ANY),), out_specs=(hbm, hbm),
        input_output_aliases={0: 0, 1: 1},
        compiler_params=pltpu.CompilerParams(has_side_effects=pltpu.SideEffectType.DATAFLOW_SIDE_EFFECTING),
    )(src_thru, land_thru, *sems, after)


def _pair_reduce(grad, core):
    blk = grad.shape[2:]
    zeros = (0,) * len(blk)

    def body(core_ref, g_hbm, own_ref, o_ref, landed, send_sems, recv_sems):
        del core_ref
        i = pl.program_id(0)
        x, y, c, _ = _position()

        def copy(k):
            return pltpu.make_async_remote_copy(
                src_ref=g_hbm.at[k, 1 - c], dst_ref=landed.at[k], send_sem=send_sems.at[k],
                recv_sem=recv_sems.at[k], device_id=(x, y, 1 - c), device_id_type=MESH)

        @pl.when(i == 0)
        def _():
            for k in range(4):
                copy(k).start()

        for k in range(4):
            @pl.when(i == k)
            def _(k=k):
                copy(k).wait_recv()

        o_ref[...] = (own_ref[...].astype(F32) + landed[i].astype(F32)).astype(BF16)

        @pl.when(i == 3)
        def _():
            for k in range(4):
                copy(k).wait_send()

    return pl.pallas_call(
        body, name="grads_pair_reduce",
        out_shape=jax.ShapeDtypeStruct((4,) + blk, BF16),
        grid_spec=pltpu.PrefetchScalarGridSpec(
            num_scalar_prefetch=1, grid=(4,),
            in_specs=[_any(), pl.BlockSpec((None, None) + blk, lambda i, cr: (i, cr[0]) + zeros)],
            out_specs=pl.BlockSpec((None,) + blk, lambda i, cr: (i,) + zeros),
            scratch_shapes=[pltpu.VMEM((4,) + blk, BF16), pltpu.SemaphoreType.DMA((4,)),
                            pltpu.SemaphoreType.DMA((4,))]),
        compiler_params=_params(1),
    )(core, grad, grad)


def _adamw(w, g, m, v):
    m2 = ADAM_B1 * m + (1.0 - ADAM_B1) * g
    v2 = ADAM_B2 * v + (1.0 - ADAM_B2) * (g * g)
    m_hat = m2 / (1.0 - ADAM_B1 ** ADAM_STEP)
    v_hat = v2 / (1.0 - ADAM_B2 ** ADAM_STEP)
    delta = -ADAM_LR * (m_hat / (jnp.sqrt(v_hat) + ADAM_EPS) + ADAM_WD * w)
    return delta, m2, v2


def _final_adamw(own, recv, idx, parts, after):
    blk = own.shape[1:]
    n_recv = recv.shape[0]
    n_parts = len(parts)
    per = blk[0] // n_parts if n_parts > 1 else None
    rows = blk[-2]
    n_chunks = 1 if n_parts > 1 else (4 if rows % 64 == 0 and rows >= 512 else (2 if rows % 32 == 0 else 1))
    cblk = blk[:-2] + (rows // n_chunks, blk[-1])
    lead = (0,) * (len(blk) - 2)

    def body(idx_ref, c_ref, r_ref, after_ref, *refs):
        del idx_ref, after_ref
        ins, outs = refs[:3 * n_parts], refs[3 * n_parts:]
        g = c_ref[...].astype(F32)
        for k in range(n_recv):
            g = g + r_ref[k].astype(F32)
        for p in range(n_parts):
            w_ref, m_ref, v_ref = ins[3 * p:3 * p + 3]
            if n_parts == 1:
                gp = g
            elif per == 1:
                gp = g[p]
            else:
                gp = g[p * per:(p + 1) * per]
            delta, m2, v2 = _adamw(w_ref[0], gp, m_ref[0], v_ref[0])
            o = outs[4 * p:4 * p + 4]
            o[0][0] = gp
            o[1][0] = delta
            o[2][0] = m2
            o[3][0] = v2

    flat = [a for wmv in parts for a in wmv]

    def part_spec(a):
        shape = a.shape[:-2] + (a.shape[-2] // n_chunks, a.shape[-1])
        return pl.BlockSpec(shape, lambda i, cr, nd=a.ndim: (0,) * (nd - 2) + (i, 0))

    outs = pl.pallas_call(
        body, name="grads_sum_adamw",
        out_shape=[jax.ShapeDtypeStruct(wmv[0].shape, F32) for wmv in parts for _ in range(4)],
        grid_spec=pltpu.PrefetchScalarGridSpec(
            num_scalar_prefetch=1, grid=(n_chunks,),
            in_specs=[pl.BlockSpec((None,) + cblk, lambda i, cr: (cr[0],) + lead + (i, 0)),
                      pl.BlockSpec((n_recv,) + cblk, lambda i, cr: (0,) + lead + (i, 0))]
                     + [_any()] + [part_spec(a) for a in flat],
            out_specs=[part_spec(wmv[0]) for wmv in parts for _ in range(4)]),
        compiler_params=_params(1),
    )(idx, own, recv, after, *flat)
    return [tuple(outs[4 * p:4 * p + 4]) for p in range(n_parts)]


def _small_adamw(partials, layout, me_index):
    _, rows, d = partials.shape
    n = len(layout)
    cw = d // N_DEV

    def body(me_ref, p_ref, *refs):
        ins, t_ref, outs = refs[:3 * n], refs[3 * n], refs[3 * n + 1:]
        me = me_ref[0]
        total = p_ref[0]
        for j in range(1, N_DEV):
            total = total + p_ref[j]
        t_ref[...] = total
        for e, (kind, r0, nr, _, _, _) in enumerate(layout):
            w_ref, m_ref, v_ref = ins[3 * e:3 * e + 3]
            o = outs[4 * e:4 * e + 4]
            if kind == "rep":
                g = t_ref[r0:r0 + nr, :]
                delta, m2, v2 = _adamw(w_ref[...], g, m_ref[...], v_ref[...])
                for ref, val in zip(o, (g, delta, m2, v2)):
                    ref[...] = val
            elif kind == "wide":
                for q in range(nr):
                    sl = slice(q * d, (q + 1) * d)
                    g = t_ref[r0 + q:r0 + q + 1, :]
                    delta, m2, v2 = _adamw(w_ref[:, sl], g, m_ref[:, sl], v_ref[:, sl])
                    for ref, val in zip(o, (g, delta, m2, v2)):
                        ref[:, sl] = val
            else:
                for j in range(N_DEV):
                    @pl.when(me == j)
                    def _(j=j, o=o, w_ref=w_ref, m_ref=m_ref, v_ref=v_ref, r0=r0, nr=nr):
                        g = t_ref[r0:r0 + nr, j * cw:(j + 1) * cw]
                        delta, m2, v2 = _adamw(w_ref[...], g, m_ref[...], v_ref[...])
                        for ref, val in zip(o, (g, delta, m2, v2)):
                            ref[...] = val

    flat = [a for ent in layout for a in ent[3:]]
    vm = pl.BlockSpec(memory_space=pltpu.VMEM)
    outs = pl.pallas_call(
        body, name="small_adamw",
        out_shape=[jax.ShapeDtypeStruct((rows, d), F32)]
                  + [jax.ShapeDtypeStruct(ent[3].shape, F32) for ent in layout for _ in range(4)],
        in_specs=[pl.BlockSpec(memory_space=pltpu.SMEM), vm] + [vm] * len(flat),
        out_specs=[vm] * (1 + 4 * n),
        compiler_params=pltpu.CompilerParams(vmem_limit_bytes=V7X_VMEM_LIMIT),
    )(me_index, partials, *flat)
    return outs[0], [tuple(outs[1 + 4 * e:5 + 4 * e]) for e in range(n)]


def _ffn_fwd(h, g, wgu, wd, tm, loss=None, comm=None):
    tp, d = h.shape
    f = wd.shape[0]
    fc = f // FFN_FWD_CHUNKS
    nt = tp // tm
    with_loss = loss is not None
    if with_loss:
        tgt, gf, n_meta, t_real = loss

    def body(*refs):
        if with_loss:
            (h_ref, g_ref, wgu_hbm, wd_hbm, tgt_ref, gf_ref, out_ref, gu_ref, n_ref, tail_ref,
             wgu_v, wd_v, sems) = refs
        else:
            h_ref, g_ref, wgu_hbm, wd_hbm, out_ref, gu_ref, n_ref, wgu_v, wd_v, sems = refs
        i = pl.program_id(0)

        @pl.when(i == 0)
        def _():
            _load_weights([(wgu_hbm, wgu_v), (wd_hbm, wd_v)], sems)
            if with_loss:
                tail_ref[...] = jnp.zeros_like(tail_ref)

        x = h_ref[...]
        n, _ = _rms_fwd(x, g_ref[...])
        nb = n.astype(BF16)
        n_ref[...] = nb
        acc = jnp.zeros((tm, d), F32)
        for j in range(FFN_FWD_CHUNKS):
            cols = slice(j * fc, (j + 1) * fc)
            gate = _nt(nb, wgu_v[pl.ds(j * fc, fc), :])
            up = _nt(nb, wgu_v[pl.ds(f + j * fc, fc), :])
            gu_ref[0, :, cols] = gate.astype(BF16)
            gu_ref[1, :, cols] = up.astype(BF16)
            act = (gate * _sigmoid(gate) * up).astype(BF16)
            acc = acc + _nn(act, wd_v[pl.ds(j * fc, fc), :])
        hn = x + FFN_RES * acc
        if not with_loss:
            out_ref[...] = hn
        else:
            gfv = gf_ref[...]
            r = lax.rsqrt(jnp.mean(hn * hn, axis=-1, keepdims=True) + EPS)
            xr = hn * r
            rows = i * tm + lax.broadcasted_iota(jnp.int32, (tm, 1), 0)
            mask = jnp.logical_and(rows >= n_meta, rows < t_real)
            diff = jnp.where(mask, xr * gfv - tgt_ref[...], 0.0)
            tail_ref[TAIL_LOSS:TAIL_LOSS + 1, :] += jnp.zeros((1, d), F32) + 0.5 * jnp.sum(diff * diff) / d
            dy = diff / d
            gy = dy * gfv
            out_ref[...] = r * (gy - xr * jnp.mean(gy * xr, axis=-1, keepdims=True))
            tail_ref[TAIL_FINAL:TAIL_FINAL + 1, :] += _rowsum(dy * xr)

    row = pl.BlockSpec((tm, d), lambda i: (i, 0))
    vec = pl.BlockSpec((1, d), lambda i: (0, 0))
    in_specs = [row, vec, _any(), _any()]
    out_shape = [jax.ShapeDtypeStruct((tp, d), F32), jax.ShapeDtypeStruct((2, tp, f), BF16),
                 jax.ShapeDtypeStruct((tp, d), BF16)]
    out_specs = [row, pl.BlockSpec((2, tm, f), lambda i: (0, i, 0)), row]
    args = [h, g, wgu, wd]
    if with_loss:
        in_specs += [row, vec]
        out_shape += [jax.ShapeDtypeStruct((SUBLANES, d), F32)]
        out_specs += [pl.BlockSpec((SUBLANES, d), lambda i: (0, 0))]
        args += [tgt, gf]
    return _call(body, "ffn_fwd_loss" if with_loss else "ffn_fwd", (nt,), in_specs, out_specs, out_shape,
                 [pltpu.VMEM((2 * f, d), BF16), pltpu.VMEM((f, d), BF16), pltpu.SemaphoreType.DMA((2,))],
                 args, comm)


def _ffn_bwd(dh, h, gu, g, wgu, wd, tm, tail, tail_row, after):
    tp, d = h.shape
    f = wd.shape[0]
    fc = f // FFN_CHUNKS
    nt = tp // tm

    def body(dh_ref, h_ref, gu_ref, g_ref, tail_ref, wgu_hbm, wd_hbm, after_ref,
             dhin_ref, dgu_ref, act_ref, df_ref, dg_ref, wgu_v, wd_v, dn_v, sems):
        del after_ref
        i, j = pl.program_id(0), pl.program_id(1)

        @pl.when(jnp.logical_and(i == 0, j == 0))
        def _():
            _load_weights([(wgu_hbm, wgu_v), (wd_hbm, wd_v)], sems)
            dg_ref[...] = tail_ref[...]

        dfb = (FFN_RES * dh_ref[...]).astype(BF16)

        @pl.when(j == 0)
        def _():
            df_ref[...] = dfb
            dn_v[...] = jnp.zeros_like(dn_v)

        lo = pl.multiple_of(j * fc, 16)
        dact = _nt(dfb, wd_v[pl.ds(lo, fc), :])
        gate = gu_ref[0].astype(F32)
        up = gu_ref[1].astype(F32)
        sg = _sigmoid(gate)
        silu = gate * sg
        act_ref[...] = (silu * up).astype(BF16)
        dgate = (dact * up * (sg * (1.0 + gate * (1.0 - sg)))).astype(BF16)
        dup = (dact * silu).astype(BF16)
        dgu_ref[0] = dgate
        dgu_ref[1] = dup
        dn_v[...] += _nn(dgate, wgu_v[pl.ds(lo, fc), :]) + _nn(dup, wgu_v[pl.ds(pl.multiple_of(f + j * fc, 16), fc), :])

        @pl.when(j == FFN_CHUNKS - 1)
        def _():
            x = h_ref[...]
            r = lax.rsqrt(jnp.mean(x * x, axis=-1, keepdims=True) + EPS)
            dx, dgp = _rms_bwd(dn_v[...], x, r, g_ref[...])
            dhin_ref[...] = dh_ref[...] + dx
            dg_ref[tail_row:tail_row + 1, :] += dgp

    row = pl.BlockSpec((tm, d), lambda i, j: (i, 0))
    vec = pl.BlockSpec((1, d), lambda i, j: (0, 0))
    tile = pl.BlockSpec((SUBLANES, d), lambda i, j: (0, 0))
    hid2 = pl.BlockSpec((2, tm, fc), lambda i, j: (0, i, j))
    return _call(
        body, "ffn_bwd", (nt, FFN_CHUNKS),
        [row, row, hid2, vec, tile, _any(), _any(), _any()],
        [row, hid2, pl.BlockSpec((tm, fc), lambda i, j: (i, j)), row, tile],
        [jax.ShapeDtypeStruct((tp, d), F32), jax.ShapeDtypeStruct((2, tp, f), BF16),
         jax.ShapeDtypeStruct((tp, f), BF16), jax.ShapeDtypeStruct((tp, d), BF16),
         jax.ShapeDtypeStruct((SUBLANES, d), F32)],
        [pltpu.VMEM((2 * f, d), BF16), pltpu.VMEM((f, d), BF16), pltpu.VMEM((tm, d), F32),
         pltpu.SemaphoreType.DMA((2,))],
        [dh, h, gu, g, tail, wgu, wd, after])


def _piece_segments(q, d, nb_cols):
    segs = []
    for j in range(N_DEV):
        lo, hi = max(q * d, j * nb_cols), min((q + 1) * d, (j + 1) * nb_cols)
        if lo < hi:
            segs.append((j, lo - q * d, hi - q * d, lo - j * nb_cols, hi - j * nb_cols))
    return segs


def _w3_copies(w3_hbm, rows, w3_v):
    return [(w3_hbm.at[k, pl.ds(q * rows, rows)], w3_v.at[q, pl.ds(k * rows, rows)])
            for q in range(3) for k in range(N_DEV)]


def _gates(xrb, wg_ref, ba, bx, lam, hd):
    pre_r, pre_i = [], []
    for hh in range(N_HEADS):
        xh = xrb[:, hh * hd:(hh + 1) * hd]
        pre_r.append(_nn(xh, wg_ref[0, hh]))
        pre_i.append(_nn(xh, wg_ref[1, hh]))
    r = _sigmoid(jnp.concatenate(pre_r, axis=1) + ba)
    ig = _sigmoid(jnp.concatenate(pre_i, axis=1) + bx)
    sp = _softplus(-lam)
    log_a = -RG_LRU_C * r * sp
    a = jnp.exp(log_a)
    s = jnp.sqrt(_one_minus_exp(2.0 * log_a))
    return r, ig, sp, a, s


def _scan_fwd(a, u, h_prev):
    tm = a.shape[0]
    rows = lax.broadcasted_iota(jnp.int32, a.shape, 0)
    d = 1
    while d < tm:
        if d < SUBLANES:
            keep = rows >= d
            u = jnp.where(keep, a * pltpu.roll(u, d, 0) + u, u)
            a = jnp.where(keep, a * pltpu.roll(a, d, 0), a)
        else:
            u = jnp.concatenate([u[:d], a[d:] * u[:tm - d] + u[d:]], axis=0)
            a = jnp.concatenate([a[:d], a[d:] * a[:tm - d]], axis=0)
        d *= 2
    return u + a * h_prev


def _scan_bwd(b, v, g_next):
    tm = b.shape[0]
    rows = lax.broadcasted_iota(jnp.int32, b.shape, 0)
    d = 1
    while d < tm:
        if d < SUBLANES:
            keep = rows < tm - d
            v = jnp.where(keep, v + b * pltpu.roll(v, tm - d, 0), v)
            b = jnp.where(keep, b * pltpu.roll(b, tm - d, 0), b)
        else:
            v = jnp.concatenate([v[:tm - d] + b[:tm - d] * v[d:], v[tm - d:]], axis=0)
            b = jnp.concatenate([b[:tm - d] * b[d:], b[tm - d:]], axis=0)
        d *= 2
    return v + b * g_next


def _shifted_copies(ext_ref, es_ref, n_rows):
    for s in range(1, SUBLANES):
        es_ref[s, pl.ds(0, n_rows), :] = ext_ref[pl.ds(s, n_rows), :]


def _tap(ext_ref, es_ref, off, tm):
    q, s = divmod(off, SUBLANES)
    if s == 0:
        return ext_ref[pl.ds(SUBLANES * q, tm), :]
    return es_ref[s, pl.ds(SUBLANES * q, tm), :]


def _mixer_fwd(h, g, b_in, win_all, cw4, cb4, wg, ba, bx, lam, cw31, cb31, lng, lnb, bcp, w3_all, tm, comm=None):
    tp, d = h.shape
    nb_cols = win_all.shape[-1]
    n_in = N_DEV * nb_cols
    hd = wg.shape[-1]
    k4, k31 = cw4.shape[0], cw31.shape[0]
    w3_rows = d // N_DEV

    def body(h_ref, g_ref, b_ref, win_hbm, cw4_ref, cb4_ref, wg_ref, ba_ref, bx_ref, lam_ref, cw31_ref, cb31_ref,
             lng_ref, lnb_ref, bcp_ref, w3_hbm,
             h2_ref, p_ref, n_ref, xr_ref, hs_ref, v1_ref, ya_ref, yb_ref,
             win_v, w3_v, ext4, ext31, es31, hcar, sems):
        @pl.when(pl.program_id(0) == 0)
        def _():
            _load_weights([(win_hbm, win_v)] + _w3_copies(w3_hbm, w3_rows, w3_v), sems)
            ext4[pl.ds(0, CONV4_HALO), :] = jnp.zeros((CONV4_HALO, d), F32)
            ext31[pl.ds(0, CONV31_HALO), :] = jnp.zeros((CONV31_HALO, d), F32)
            hcar[...] = jnp.zeros_like(hcar)

        n, _ = _rms_fwd(h_ref[...], g_ref[...])
        nb = n.astype(BF16)
        n_ref[...] = nb

        def piece(q):
            parts = [_nn(nb, win_v[j, :, bl:bh]) for j, _, _, bl, bh in _piece_segments(q, d, nb_cols)]
            pq = (jnp.concatenate(parts, axis=1) + b_ref[:, q * d:(q + 1) * d]).astype(BF16)
            p_ref[:, q * d:(q + 1) * d] = pq
            return pq.astype(F32)

        x_rnn, y_rnn, glu_v, glu_g, gate_a, gate_b = [piece(q) for q in range(6)]

        ext4[pl.ds(CONV4_HALO, tm), :] = x_rnn
        xr = cb4_ref[...] + jnp.zeros((tm, d), F32)
        for k in range(k4):
            xr = xr + cw4_ref[k:k + 1, :] * ext4[pl.ds(CONV4_HALO - (k4 - 1) + k, tm), :]
        ext4[pl.ds(0, CONV4_HALO), :] = ext4[pl.ds(tm, CONV4_HALO), :]
        xrb = xr.astype(BF16)
        xr_ref[...] = xrb
        xr = xrb.astype(F32)
        _, ig, _, a, s = _gates(xrb, wg_ref, ba_ref[...], bx_ref[...], lam_ref[...], hd)
        hseq = _scan_fwd(a, s * (ig * xr), hcar[0:1, :])
        hcar[0:1, :] = hseq[tm - 1:tm, :]
        hs_ref[...] = hseq.astype(BF16)
        gl, _ = _gelu(y_rnn)
        ya = _nn((hseq * gl).astype(BF16), w3_v[0])
        ya_ref[...] = ya.astype(BF16)

        ext31[pl.ds(CONV31_HALO, tm), :] = glu_v * _sigmoid(glu_g)
        _shifted_copies(ext31, es31, tm + CONV31_HALO - SUBLANES)
        v1 = cb31_ref[...] + jnp.zeros((tm, d), F32)
        for k in range(k31):
            v1 = v1 + cw31_ref[k:k + 1, :] * _tap(ext31, es31, CONV31_HALO - (k31 - 1) + k, tm)
        ext31[pl.ds(0, CONV31_HALO), :] = ext31[pl.ds(tm, CONV31_HALO), :]
        v1b = v1.astype(BF16)
        v1_ref[...] = v1b
        v1 = v1b.astype(F32)
        xc = v1 - jnp.mean(v1, axis=-1, keepdims=True)
        rstd = lax.rsqrt(jnp.mean(xc * xc, axis=-1, keepdims=True) + EPS)
        v2 = xc * rstd * lng_ref[...] + lnb_ref[...]
        yb = _nn((v2 * _sigmoid(v2)).astype(BF16), w3_v[1]) + bcp_ref[...]
        yb_ref[...] = yb.astype(BF16)

        merged = _sigmoid(gate_a) * ya + _sigmoid(gate_b) * yb
        h2_ref[...] = h_ref[...] + _nn(merged.astype(BF16), w3_v[2])

    row = pl.BlockSpec((tm, d), lambda i: (i, 0))
    wide = pl.BlockSpec((tm, n_in), lambda i: (i, 0))
    full = lambda a: pl.BlockSpec(a.shape, lambda i, nd=a.ndim: (0,) * nd)
    smalls = [cw4, cb4, wg, ba, bx, lam, cw31, cb31, lng, lnb, bcp]
    return _call(
        body, "mixer_fwd", (tp // tm,),
        [row, full(g), full(b_in), _any()] + [full(a) for a in smalls] + [_any()],
        [row, wide] + [row] * 6,
        [jax.ShapeDtypeStruct((tp, d), F32), jax.ShapeDtypeStruct((tp, n_in), BF16)]
        + [jax.ShapeDtypeStruct((tp, d), BF16)] * 6,
        [pltpu.VMEM(win_all.shape, BF16),
         pltpu.VMEM((3, d, d), BF16),
         pltpu.VMEM((tm + CONV4_HALO, d), F32),
         pltpu.VMEM((tm + CONV31_HALO, d), F32),
         pltpu.VMEM((SUBLANES, tm + CONV31_HALO, d), F32),
         pltpu.VMEM((SUBLANES, d), F32),
         pltpu.SemaphoreType.DMA((1 + 3 * N_DEV,))],
        [h, g, b_in, win_all, *smalls, w3_all], comm)


SG_BIN, SG_CW4, SG_CB4, SG_BA, SG_BX, SG_LAM, SG_CB31, SG_LNG, SG_LNB, SG_BCP, SG_MIX, SG_CW31 = 0, 6, 10, 11, 12, 13, 14, 15, 16, 17, 18, 19


def _mixer_bwd(dh2, h, g, proj, xr_s, hs_s, v1_s, ya_s, yb_s, win_t, cw4, wg, ba, bx, lam, cw31, lng, lnb, w3_all, tm,
               comm=None):
    tp, d = dh2.shape
    n_in = proj.shape[1]
    hd = wg.shape[-1]
    k4, k31 = cw4.shape[0], cw31.shape[0]
    nt = tp // tm
    w3_rows = d // N_DEV
    sg_rows = -(-(SG_CW31 + k31) // SUBLANES) * SUBLANES
    halo_rows = 16
    per = tm // halo_rows

    def body(dh_ref, h_ref, g_ref, p_ref, xr_ref, hs_ref, hh_ref, v1_ref, ya_ref, yb_ref, win_hbm,
             cw4_ref, wg_ref, wgt_ref, ba_ref, bx_ref, lam_ref, cw31_ref, lng_ref, lnb_ref, w3_hbm,
             dh1_ref, dp_ref, x3_ref, y3_ref, yg_ref, sg_ref,
             win_v, w3_v, extd4, extd31, es31, gcar, sems):
        i = pl.program_id(0)
        tile = nt - 1 - i

        @pl.when(i == 0)
        def _():
            _load_weights([(win_hbm, win_v)] + _w3_copies(w3_hbm, w3_rows, w3_v), sems)
            for q in range(3):
                w3_v[q] = w3_v[q].T
            extd4[pl.ds(tm, CONV4_HALO), :] = jnp.zeros((CONV4_HALO, d), F32)
            extd31[pl.ds(tm, CONV31_HALO), :] = jnp.zeros((CONV31_HALO, d), F32)
            gcar[...] = jnp.zeros_like(gcar)
            sg_ref[...] = jnp.zeros_like(sg_ref)

        def acc(row, val):
            sg_ref[row:row + 1, :] += _rowsum(val)

        rows = lax.broadcasted_iota(jnp.int32, (tm, d), 0)
        x_rnn = p_ref[:, 0:d].astype(F32)
        y_rnn = p_ref[:, d:2 * d].astype(F32)
        glu_v = p_ref[:, 2 * d:3 * d].astype(F32)
        glu_g = p_ref[:, 3 * d:4 * d].astype(F32)
        sga = _sigmoid(p_ref[:, 4 * d:5 * d].astype(F32))
        sgb = _sigmoid(p_ref[:, 5 * d:6 * d].astype(F32))
        ya = ya_ref[...].astype(F32)
        yb = yb_ref[...].astype(F32)

        dmob = dh_ref[...].astype(BF16)
        dmerged = _nn(dmob, w3_v[2])
        x3_ref[:, 0:d] = (sga * ya + sgb * yb).astype(BF16)
        y3_ref[:, 0:d] = dmob
        dya = sga * dmerged
        dyb = sgb * dmerged
        dn_parts = []

        def emit(q, val):
            vb = val.astype(BF16)
            dp_ref[:, q * d:(q + 1) * d] = vb
            acc(SG_BIN + q, val)
            term = _nn(vb, win_v[pl.ds(q * d, d), :])
            dn_parts[:] = [term if not dn_parts else dn_parts[0] + term]

        emit(4, dmerged * ya * sga * (1.0 - sga))
        emit(5, dmerged * yb * sgb * (1.0 - sgb))

        dyab = dya.astype(BF16)
        y3_ref[:, d:2 * d] = dyab
        dza = _nn(dyab, w3_v[0])
        hsv = hs_ref[...].astype(F32)
        gl, th = _gelu(y_rnn)
        x3_ref[:, d:2 * d] = (hsv * gl).astype(BF16)
        emit(1, dza * hsv * _gelu_grad(y_rnn, th))
        dhs = dza * gl
        xrb = xr_ref[...]
        xr = xrb.astype(F32)
        lam_v = lam_ref[...]
        r, ig, sp, a, s = _gates(xrb, wg_ref, ba_ref[...], bx_ref[...], lam_v, hd)
        b = jnp.where(rows == tm - 1, gcar[1:2, :], pltpu.roll(a, tm - 1, 0))
        big_g = _scan_bwd(b, dhs, gcar[0:1, :])
        gcar[0:1, :] = big_g[0:1, :]
        gcar[1:2, :] = a[0:1, :]
        h_before = jnp.where(tile > 0, hh_ref[halo_rows - 1:halo_rows, :].astype(F32), 0.0)
        h_prev = jnp.where(rows == 0, h_before, pltpu.roll(hsv, 1, 0))
        ds = big_g * ig * xr
        dla = big_g * h_prev * a - ds * (a * a) / jnp.maximum(s, 1e-20)
        acc(SG_LAM, dla * r * (RG_LRU_C * _sigmoid(-lam_v)))
        dpr = dla * (-RG_LRU_C * sp) * r * (1.0 - r)
        dpi = big_g * s * xr * ig * (1.0 - ig)
        acc(SG_BA, dpr)
        acc(SG_BX, dpi)
        dprb = dpr.astype(BF16)
        dpib = dpi.astype(BF16)
        yg_ref[:, 0:d] = dprb
        yg_ref[:, d:2 * d] = dpib
        back = []
        for hh in range(N_HEADS):
            sl = slice(hh * hd, (hh + 1) * hd)
            back.append(_nn(dprb[:, sl], wgt_ref[0, hh]) + _nn(dpib[:, sl], wgt_ref[1, hh]))
        dxr = big_g * s * ig + jnp.concatenate(back, axis=1)
        acc(SG_CB4, dxr)
        extd4[pl.ds(0, tm), :] = dxr
        dx_rnn = jnp.zeros((tm, d), F32)
        for k in range(k4):
            term = extd4[pl.ds(k4 - 1 - k, tm), :]
            dx_rnn = dx_rnn + cw4_ref[k:k + 1, :] * term
            acc(SG_CW4 + k, x_rnn * term)
        extd4[pl.ds(tm, CONV4_HALO), :] = extd4[pl.ds(0, CONV4_HALO), :]
        emit(0, dx_rnn)

        dybb = dyb.astype(BF16)
        y3_ref[:, 2 * d:3 * d] = dybb
        acc(SG_BCP, dyb)
        dv3 = _nn(dybb, w3_v[1])
        v1 = v1_ref[...].astype(F32)
        xc = v1 - jnp.mean(v1, axis=-1, keepdims=True)
        rstd = lax.rsqrt(jnp.mean(xc * xc, axis=-1, keepdims=True) + EPS)
        xhat = xc * rstd
        lng_v = lng_ref[...]
        v2 = xhat * lng_v + lnb_ref[...]
        s2 = _sigmoid(v2)
        x3_ref[:, 2 * d:3 * d] = (v2 * s2).astype(BF16)
        dv2 = dv3 * (s2 * (1.0 + v2 * (1.0 - s2)))
        acc(SG_LNG, dv2 * xhat)
        acc(SG_LNB, dv2)
        dxh = dv2 * lng_v
        dv1 = rstd * (dxh - jnp.mean(dxh, axis=-1, keepdims=True)
                      - xhat * jnp.mean(dxh * xhat, axis=-1, keepdims=True))
        acc(SG_CB31, dv1)
        extd31[pl.ds(0, tm), :] = dv1
        _shifted_copies(extd31, es31, tm + CONV31_HALO - SUBLANES)
        sgg = _sigmoid(glu_g)
        v0 = glu_v * sgg
        dv0 = jnp.zeros((tm, d), F32)
        for k in range(k31):
            term = _tap(extd31, es31, k31 - 1 - k, tm)
            dv0 = dv0 + cw31_ref[k:k + 1, :] * term
            acc(SG_CW31 + k, v0 * term)
        extd31[pl.ds(tm, CONV31_HALO), :] = extd31[pl.ds(0, CONV31_HALO), :]
        emit(2, dv0 * sgg)
        emit(3, dv0 * glu_v * sgg * (1.0 - sgg))

        dn = dn_parts[0]
        x = h_ref[...]
        rr = lax.rsqrt(jnp.mean(x * x, axis=-1, keepdims=True) + EPS)
        dx, dgp = _rms_bwd(dn, x, rr, g_ref[...])
        dh1_ref[...] = dh_ref[...] + dx
        sg_ref[SG_MIX:SG_MIX + 1, :] += dgp

    rev = lambda i: (nt - 1 - i, 0)
    row = pl.BlockSpec((tm, d), rev)
    wide = pl.BlockSpec((tm, n_in), rev)
    full = lambda a: pl.BlockSpec(a.shape, lambda i, nd=a.ndim: (0,) * nd)
    halo = pl.BlockSpec((halo_rows, d), lambda i: (jnp.maximum((nt - 1 - i) * per - 1, 0), 0))
    smalls = [cw4, wg, jnp.swapaxes(wg, 2, 3), ba, bx, lam, cw31, lng, lnb]
    return _call(
        body, "mixer_bwd", (nt,),
        [row, row, full(g), wide, row, row, halo, row, row, row, _any()]
        + [full(a) for a in smalls] + [_any()],
        [row, wide, pl.BlockSpec((tm, 3 * d), rev), pl.BlockSpec((tm, 3 * d), rev),
         pl.BlockSpec((tm, 2 * d), rev), pl.BlockSpec((sg_rows, d), lambda i: (0, 0))],
        [jax.ShapeDtypeStruct((tp, d), F32), jax.ShapeDtypeStruct((tp, n_in), BF16),
         jax.ShapeDtypeStruct((tp, 3 * d), BF16), jax.ShapeDtypeStruct((tp, 3 * d), BF16),
         jax.ShapeDtypeStruct((tp, 2 * d), BF16), jax.ShapeDtypeStruct((sg_rows, d), F32)],
        [pltpu.VMEM(win_t.shape, BF16),
         pltpu.VMEM((3, d, d), BF16),
         pltpu.VMEM((tm + CONV4_HALO, d), F32),
         pltpu.VMEM((tm + CONV31_HALO, d), F32),
         pltpu.VMEM((SUBLANES, tm + CONV31_HALO, d), F32),
         pltpu.VMEM((SUBLANES, d), F32),
         pltpu.SemaphoreType.DMA((1 + 3 * N_DEV,))],
        [dh2, h, g, proj, xr_s, hs_s, hs_s, v1_s, ya_s, yb_s, win_t, *smalls, w3_all], comm)


def _tn_matmul(name, x, y, x_spec, y_spec, n_blocks, kb, nb, tm, tp, out_shape, out_spec, out_view, comm=None,
               after=None):
    nt = tp // tm

    def body(x_ref, y_ref, *refs):
        o_ref, acc = refs[-2:]
        i = pl.program_id(1)

        @pl.when(i == 0)
        def _():
            acc[...] = jnp.zeros_like(acc)

        acc[...] += _tn(x_ref[...], y_ref[...])

        @pl.when(i == nt - 1)
        def _():
            o_ref[...] = acc[...].astype(BF16).reshape(out_view)

    follows = [] if after is None else [after]
    outs, extra = _call(body, name, (n_blocks, nt), [x_spec, y_spec] + [_any()] * len(follows), [out_spec],
                        [jax.ShapeDtypeStruct(out_shape, BF16)], [pltpu.VMEM((kb, nb), F32)], [x, y] + follows,
                        comm)
    return outs[0], extra


def kernel(x, meta_tokens, ffn1_norm, ffn1_w_gu, ffn1_w_down, mix_norm, w_in, b_in, rnn_conv_w, rnn_conv_b, rg_w_a, rg_b_a, rg_w_x, rg_b_x, rg_lambda, rnn_w_proj, conv_dw_w, conv_dw_b, conv_ln_g, conv_ln_b, conv_w_proj, conv_b_proj, w_out, ffn2_norm, ffn2_w_gu, ffn2_w_down, final_norm, loss_target, m_meta_tokens, m_ffn1_norm, m_ffn1_w_gu, m_ffn1_w_down, m_mix_norm, m_w_in, m_b_in, m_rnn_conv_w, m_rnn_conv_b, m_rg_w_a, m_rg_b_a, m_rg_w_x, m_rg_b_x, m_rg_lambda, m_rnn_w_proj, m_conv_dw_w, m_conv_dw_b, m_conv_ln_g, m_conv_ln_b, m_conv_w_proj, m_conv_b_proj, m_w_out, m_ffn2_norm, m_ffn2_w_gu, m_ffn2_w_down, m_final_norm, v_meta_tokens, v_ffn1_norm, v_ffn1_w_gu, v_ffn1_w_down, v_mix_norm, v_w_in, v_b_in, v_rnn_conv_w, v_rnn_conv_b, v_rg_w_a, v_rg_b_a, v_rg_w_x, v_rg_b_x, v_rg_lambda, v_rnn_w_proj, v_conv_dw_w, v_conv_dw_b, v_conv_ln_g, v_conv_ln_b, v_conv_w_proj, v_conv_b_proj, v_w_out, v_ffn2_norm, v_ffn2_w_gu, v_ffn2_w_down, v_final_norm):
    w = dict(locals())
    seq, d = x.shape[1], x.shape[2]
    n_meta = meta_tokens.shape[0]
    t_real = n_meta + seq
    tp, tm, tmx_fwd, tmx, tmt, tmw = _tiles(t_real)
    fb = ffn1_w_gu.shape[-1]
    wr = ffn1_w_down.shape[1]
    f = N_DEV * wr
    fc = f // FFN_CHUNKS
    nbc = w_in.shape[-1]
    n_in = N_DEV * nbc
    pr = rnn_w_proj.shape[1]
    hd = rg_w_a.shape[-1]
    gr = rg_w_a.shape[2]
    cw = meta_tokens.shape[1]
    k4, k31 = rnn_conv_w.shape[1], conv_dw_w.shape[1]
    assert n_in == 6 * d and 2 * wr == fb and N_HEADS * hd == d and pr * N_DEV == d

    xi, yi, ci = lax.axis_index("x"), lax.axis_index("y"), lax.axis_index("c")
    core = ci.astype(jnp.int32).reshape(1)
    chip = (2 * xi + yi).astype(jnp.int32).reshape(1)
    me_index = (4 * xi + 2 * yi + ci).astype(jnp.int32).reshape(1)

    for nm in ("ffn1_w_gu", "ffn2_w_gu"):
        for pre in ("", "m_", "v_"):
            w[pre + nm] = jnp.swapaxes(w[pre + nm], 1, 2)

    wgut1 = w["ffn1_w_gu"][0].astype(BF16)
    wgut2 = w["ffn2_w_gu"][0].astype(BF16)
    wd1 = ffn1_w_down[0].astype(BF16)
    wd2 = ffn2_w_down[0].astype(BF16)
    win_loc = w_in[0].astype(BF16)
    win_t_loc = jnp.swapaxes(w_in[0], 0, 1).astype(BF16)
    w3_loc = jnp.concatenate([rnn_w_proj[0], conv_w_proj[0], w_out[0]], axis=0).astype(BF16)
    wg_loc = jnp.stack([rg_w_a[0], rg_w_x[0]]).astype(BF16)
    n_small = n_meta + k4 + k31
    small_rows = -(-n_small // SUBLANES) * SUBLANES
    small_loc = jnp.concatenate([meta_tokens, rnn_conv_w[0], conv_dw_w[0],
                                 jnp.zeros((small_rows - n_small, cw), F32)], axis=0)
    (wgut1_all, wd1_all, small_all), h0, tgt = _first_gather(
        [wgut1, wd1, small_loc], 2, x[0], loss_target[0], n_meta, tp)
    small_full = small_all.transpose(1, 0, 2).reshape(small_rows, d)
    cw4 = small_full[n_meta:n_meta + k4]
    cw31 = small_full[n_meta + k4:n_meta + k4 + k31]

    wgu1, wdn1 = wgut1_all.reshape(2 * f, d), wd1_all.reshape(f, d)
    (h1, gu1, n1), (win_all, w3_all, wg_all) = _ffn_fwd(
        h0, ffn1_norm, wgu1, wdn1, tm, comm=_Gather([win_loc, w3_loc, wg_loc], pass_on_at=(0.55, 0.9)))
    wg = wg_all.transpose(1, 2, 0, 3, 4).reshape(2, N_HEADS, hd, hd)
    (h2, proj, n2, xr_s, hs_s, v1_s, ya_s, yb_s), (wgut2_all, wd2_all) = _mixer_fwd(
        h1, mix_norm, b_in, win_all, cw4, rnn_conv_b, wg, rg_b_a, rg_b_x, rg_lambda, cw31, conv_dw_b, conv_ln_g,
        conv_ln_b, conv_b_proj, w3_all, tmx_fwd, comm=_Gather([wgut2, wd2], pass_on_at=(0.3, 0.5)))
    wgu2, wdn2 = wgut2_all.reshape(2 * f, d), wd2_all.reshape(f, d)
    (dh3, gu2, n3, tail), (win_t_all,) = _ffn_fwd(
        h2, ffn2_norm, wgu2, wdn2, tm, loss=(tgt, final_norm.reshape(1, d), n_meta, t_real),
        comm=_Gather([win_t_loc], pass_on_at=(0.45, 0.75)))
    win_t = win_t_all.reshape(n_in, d)

    def d_w_gu(tag, dgu, n_s, comm=None):
        g, extra = _tn_matmul(
            "d_w_gu" + tag, dgu, n_s,
            pl.BlockSpec((None, tmt, fc), lambda b, i: (b // FFN_CHUNKS, i, b % FFN_CHUNKS)),
            pl.BlockSpec((tmt, d), lambda b, i: (i, 0)),
            2 * FFN_CHUNKS, fc, d, tmt, tp, (2 * FFN_CHUNKS, fc, d),
            pl.BlockSpec((None, fc, d), lambda b, i: (b, 0, 0)), (fc, d), comm)
        return g.reshape(N_DEV, fb, d), extra

    def d_w_down(tag, act, df, comm=None):
        g, extra = _tn_matmul(
            "d_w_down" + tag, act, df,
            pl.BlockSpec((tmt, fc), lambda b, i: (i, b)), pl.BlockSpec((tmt, d), lambda b, i: (i, 0)),
            FFN_CHUNKS, fc, d, tmt, tp, (FFN_CHUNKS, fc, d),
            pl.BlockSpec((None, fc, d), lambda b, i: (b, 0, 0)), (fc, d), comm)
        return g.reshape(N_DEV, wr, d), extra

    (dh2, dgu2, act2, df2, tail), _ = _ffn_bwd(dh3, h2, gu2, ffn2_norm, wgu2, wdn2, tm, tail, TAIL_FFN2, n3)
    g_wgu2, _ = d_w_gu("2", dgu2, n3)
    g_wd2, _ = d_w_down("2", act2, df2)
    (dh1, dproj, x3, y3, yg, sg), (r_wd2, r_wgu2) = _mixer_bwd(
        dh2, h1, mix_norm, proj, xr_s, hs_s, v1_s, ya_s, yb_s, win_t, cw4, wg, rg_b_a, rg_b_x, rg_lambda, cw31,
        conv_ln_g, conv_ln_b, w3_all, tmx, comm=_Scatter([g_wd2, g_wgu2]))
    g_w3, _ = _tn_matmul(
        "d_w_proj3", x3, y3,
        pl.BlockSpec((tmw, d), lambda b, i: (i, b)), pl.BlockSpec((tmw, d), lambda b, i: (i, b)),
        3, d, d, tmw, tp, (N_DEV, 3, pr, d), pl.BlockSpec((N_DEV, None, pr, d), lambda b, i: (0, b, 0, 0)),
        (N_DEV, pr, d))
    g_wg, _ = _tn_matmul(
        "d_w_gates", xr_s, yg,
        pl.BlockSpec((tmw, hd), lambda b, i: (i, b % N_HEADS)), pl.BlockSpec((tmw, hd), lambda b, i: (i, b)),
        2 * N_HEADS, hd, hd, tmw, tp, (N_DEV, 2 * N_HEADS, gr, hd),
        pl.BlockSpec((N_DEV, None, gr, hd), lambda b, i: (0, b, 0, 0)), (N_DEV, gr, hd))
    w3_sems, g_w3_thru, w3_land, w3_token = _exchange_start("grads_proj3_exchange", _scatter_copies, N_DEV - 1, g_w3)
    g_win, (r_wg,) = _tn_matmul(
        "d_w_in", n2, dproj,
        pl.BlockSpec((tmw, d), lambda b, i: (i, 0)), pl.BlockSpec((tmw, nbc), lambda b, i: (i, b)),
        N_DEV, d, nbc, tmw, tp, (N_DEV, d, nbc), pl.BlockSpec((None, d, nbc), lambda b, i: (b, 0, 0)), (d, nbc),
        comm=_Scatter([g_wg]), after=w3_token)
    win_sems, g_win_thru, win_land, win_token = _exchange_start("grads_w_in_exchange", _scatter_copies, N_DEV - 1, g_win)
    (dh0, dgu1, act1, df1, tail), _ = _ffn_bwd(dh1, h0, gu1, ffn1_norm, wgu1, wdn1, tm, tail, TAIL_FFN1, win_token)
    grad_x = dh0[n_meta:t_real][None]

    pieces = [sg, dh0[:n_meta], tail]
    assert all(p.shape[0] % SUBLANES == 0 for p in pieces)
    at = [0, sg.shape[0], sg.shape[0] + n_meta]
    loss_row = at[2] + TAIL_LOSS
    rep_rows = [("ffn1_norm", at[2] + TAIL_FFN1, 1), ("mix_norm", SG_MIX, 1), ("b_in", SG_BIN, 6),
                ("rnn_conv_b", SG_CB4, 1),
                ("rg_b_a", SG_BA, 1), ("rg_b_x", SG_BX, 1), ("rg_lambda", SG_LAM, 1), ("conv_dw_b", SG_CB31, 1),
                ("conv_ln_g", SG_LNG, 1), ("conv_ln_b", SG_LNB, 1), ("conv_b_proj", SG_BCP, 1),
                ("ffn2_norm", at[2] + TAIL_FFN2, 1), ("final_norm", at[2] + TAIL_FINAL, 1)]
    col_rows = [("meta_tokens", at[1], n_meta), ("rnn_conv_w", SG_CW4, k4), ("conv_dw_w", SG_CW31, k31)]
    layout = []
    for nm, row0, nr in rep_rows:
        kind = "wide" if nm == "b_in" else "rep"
        as2d = lambda a: a.reshape(1, -1) if a.ndim == 1 else a
        layout.append((kind, row0, nr, as2d(w[nm]), as2d(w["m_" + nm]), as2d(w["v_" + nm])))
    for nm, row0, nr in col_rows:
        sq = lambda a: a.reshape(a.shape[-2], a.shape[-1])
        layout.append(("col", row0, nr, sq(w[nm]), sq(w["m_" + nm]), sq(w["v_" + nm])))
    small_partial = jnp.concatenate(pieces, axis=0)

    g_wd1, (small_partials,) = d_w_down("1", act1, df1, comm=_Bcast(small_partial))
    g_wgu1, (r_wd1,) = d_w_gu("1", dgu1, n1, comm=_Scatter([g_wd1]))

    g_last = g_wgu1.reshape((4, 2) + g_wgu1.shape[1:])
    comb_wgu1 = _pair_reduce(g_last, core)
    sems, comb_thru, land_thru, after = _exchange_start("grads_chip_exchange", _chip_copies, 3, comb_wgu1)
    g_win, r_win = _exchange_wait("grads_w_in_exchange", _scatter_copies, win_sems, g_win_thru, win_land, after)
    g_w3, r_w3 = _exchange_wait("grads_proj3_exchange", _scatter_copies, w3_sems, g_w3_thru, w3_land, after)

    groups = [(g_wd1, r_wd1, me_index, ["ffn1_w_down"]),
              (g_wd2, r_wd2, me_index, ["ffn2_w_down"]), (g_wgu2, r_wgu2, me_index, ["ffn2_w_gu"]),
              (g_win, r_win, me_index, ["w_in"]), (g_w3, r_w3, me_index, ["w_out", "rnn_w_proj", "conv_w_proj"]),
              (g_wg, r_wg, me_index, ["rg_w_a", "rg_w_x"]), (None, None, chip, ["ffn1_w_gu"])]
    res = {}
    for own, recv, idx, group in groups:
        if own is None:
            own, recv = _exchange_wait("grads_chip_exchange", _chip_copies, sems, comb_thru, land_thru, after)
        outs = _final_adamw(own, recv, idx, [(w[nm], w["m_" + nm], w["v_" + nm]) for nm in group], after)
        after = outs[-1][0]
        for nm, o in zip(group, outs):
            res[nm] = o
    for nm in ("ffn1_w_gu", "ffn2_w_gu"):
        res[nm] = tuple(jnp.swapaxes(a, 1, 2) for a in res[nm])

    total, small_out = _small_adamw(small_partials, layout, me_index)
    for (nm, _, _), o in zip(rep_rows + col_rows, small_out):
        res[nm] = tuple(a.reshape(w[nm].shape) for a in o)

    order = ["meta_tokens", "ffn1_norm", "ffn1_w_gu", "ffn1_w_down", "mix_norm", "w_in", "b_in", "rnn_conv_w",
             "rnn_conv_b", "rg_w_a", "rg_b_a", "rg_w_x", "rg_b_x", "rg_lambda", "rnn_w_proj", "conv_dw_w",
             "conv_dw_b", "conv_ln_g", "conv_ln_b", "conv_w_proj", "conv_b_proj", "w_out", "ffn2_norm",
             "ffn2_w_gu", "ffn2_w_down", "final_norm"]
    return (total[loss_row, 0], grad_x, *[res[nm][0] for nm in order], *[res[nm][1] for nm in order],
            *[res[nm][2] for nm in order], *[res[nm][3] for nm in order])
```

```python
import functools
import math

import jax
import jax.numpy as jnp
from jax import lax
from jax.experimental import pallas as pl
from jax.experimental.pallas import tpu as pltpu

F32 = jnp.float32
BF16 = jnp.bfloat16
MESH = pl.DeviceIdType.MESH
N_DEV = 8
N_HEADS = 4
RG_LRU_C = 8.0
EPS = 1e-6
FFN_RES = 0.5
ADAM_LR, ADAM_B1, ADAM_B2, ADAM_EPS, ADAM_WD, ADAM_STEP = 0.001, 0.9, 0.999, 1e-08, 0.01, 10
V7X_VMEM_LIMIT = 56 * 1024 * 1024
CONV4_HALO = 8
CONV31_HALO = 32
SUBLANES = 8
STAGE_ROWS = 512
TAIL_FFN1, TAIL_FINAL, TAIL_LOSS, TAIL_FFN2 = 0, 1, 2, 3
FFN_CHUNKS = 2
FFN_FWD_CHUNKS = 1
GELU_C = math.sqrt(2.0 / math.pi)
GELU_K = 0.044715


def _any():
    return pl.BlockSpec(memory_space=pl.ANY)


def _params(n_grid):
    return pltpu.CompilerParams(dimension_semantics=("arbitrary",) * n_grid, vmem_limit_bytes=V7X_VMEM_LIMIT)


def _nn(a, b):
    return jnp.dot(a, b, preferred_element_type=F32)


def _nt(a, b):
    return lax.dot_general(a, b, (((1,), (1,)), ((), ())), preferred_element_type=F32)


def _tn(a, b):
    return lax.dot_general(a, b, (((0,), (0,)), ((), ())), preferred_element_type=F32)


def _sigmoid(x):
    return 0.5 * jnp.tanh(0.5 * x) + 0.5


def _rowsum(x):
    return jnp.sum(x, axis=0, keepdims=True)


def _rms_fwd(x, g):
    r = lax.rsqrt(jnp.mean(x * x, axis=-1, keepdims=True) + EPS)
    return x * r * g, r


def _rms_bwd(dn, x, r, g):
    xr = x * r
    gy = dn * g
    dx = r * (gy - xr * jnp.mean(gy * xr, axis=-1, keepdims=True))
    return dx, _rowsum(dn * xr)


def _gelu(y):
    t = jnp.tanh(GELU_C * (y + GELU_K * y * y * y))
    return 0.5 * y * (1.0 + t), t


def _gelu_grad(y, t):
    return 0.5 * (1.0 + t) + 0.5 * y * (1.0 - t * t) * GELU_C * (1.0 + 3.0 * GELU_K * y * y)


def _softplus(x):
    return jnp.maximum(x, 0.0) + jnp.log(1.0 + jnp.exp(-jnp.abs(x)))


def _one_minus_exp(z):
    series = -z * (1.0 + 0.5 * z * (1.0 + z * (1.0 / 3.0) * (1.0 + 0.25 * z)))
    return jnp.where(z > -0.05, series, 1.0 - jnp.exp(z))


def _tiles(t_real):
    if t_real > 2048:
        tm = 384
        tp = -(-t_real // tm) * tm
        return tp, tm, tm // 2, tm // 2, tp // 2, tp
    tm = 128
    tp = -(-t_real // tm) * tm
    return tp, tm, tm // 2, tm // 2, tm, tm


def _load_weights(copies, sems):
    cps = [pltpu.make_async_copy(s, d, sems.at[k]) for k, (s, d) in enumerate(copies)]
    for cp in cps:
        cp.start()
    for cp in cps:
        cp.wait()


def _position():
    x, y, c = lax.axis_index("x"), lax.axis_index("y"), lax.axis_index("c")
    chips = [(1 - x, y), (x, 1 - y), (1 - x, 1 - y)]
    return x, y, c, chips


def _slot(p):
    return 4 * p[0] + 2 * p[1] + p[2]


class _Lazy(dict):
    def __getitem__(self, key):
        val = dict.__getitem__(self, key)
        return val() if callable(val) else val


class _Gather:
    def __init__(self, shards, pass_on_at=None):
        self.shards = list(shards)
        self.n = len(self.shards)
        self.pass_on_at = pass_on_at

    def inputs(self):
        return self.shards

    def out_shape(self):
        return [jax.ShapeDtypeStruct((N_DEV,) + s.shape, s.dtype) for s in self.shards]

    N_SEMS = 9

    def scratch(self):
        return [pltpu.SemaphoreType.DMA((self.N_SEMS * self.n,)), pltpu.SemaphoreType.DMA((self.N_SEMS * self.n,)),
                pltpu.SemaphoreType.DMA((self.n,))]

    def _plan(self, ins, outs, sems):
        send_sems, recv_sems, local_sems = sems
        x, y, c, _ = _position()
        me, sib, xn, yn, dg = (x, y, c), (x, y, 1 - c), (1 - x, y, c), (x, 1 - y, c), (1 - x, 1 - y, c)
        other = lambda p: (p[0], p[1], 1 - c)

        def blk(a, p, half=None):
            ref = outs[a].at[_slot(p)]
            if half is None:
                return ref
            rows = self.shards[a].shape[0] // 2
            return ref.at[pl.ds(half * rows, rows)]

        def copy(a, k, dst, to, src=None):
            return pltpu.make_async_remote_copy(
                src_ref=dst if src is None else src, dst_ref=dst,
                send_sem=send_sems.at[self.N_SEMS * a + k], recv_sem=recv_sems.at[self.N_SEMS * a + k],
                device_id=to, device_id_type=MESH)

        cp = _Lazy(mine=lambda: [pltpu.make_async_copy(ins[a], blk(a, me), local_sems.at[a]) for a in range(self.n)])
        for a in range(self.n):
            cp[a] = _Lazy(
                own=lambda a=a: [copy(a, 0, blk(a, me), sib, src=ins[a]), copy(a, 1, blk(a, me), xn, src=ins[a]),
                                 copy(a, 2, blk(a, me), yn, src=ins[a])],
                from_x=lambda a=a: copy(a, 1, blk(a, xn), me), from_y=lambda a=a: copy(a, 2, blk(a, yn), me),
                relay_x=lambda a=a: copy(a, 3, blk(a, xn, 0), yn), relay_y=lambda a=a: copy(a, 4, blk(a, yn, 1), xn),
                diag0=lambda a=a: copy(a, 3, blk(a, dg, 0), me), diag1=lambda a=a: copy(a, 4, blk(a, dg, 1), me),
                pass_x=lambda a=a: copy(a, 5, blk(a, xn), sib), pass_y=lambda a=a: copy(a, 6, blk(a, yn), sib),
                pass_d0=lambda a=a: copy(a, 7, blk(a, dg, 0), sib), pass_d1=lambda a=a: copy(a, 8, blk(a, dg, 1), sib),
                from_sib=lambda a=a: [copy(a, 0, blk(a, sib), me), copy(a, 5, blk(a, other(xn)), me),
                                      copy(a, 6, blk(a, other(yn)), me), copy(a, 7, blk(a, other(dg), 0), me),
                                      copy(a, 8, blk(a, other(dg), 1), me)])
        return cp

    def start(self, ins, outs, sems):
        cp = self._plan(ins, outs, sems)
        for c in cp["mine"]:
            c.start()
        for a in range(self.n):
            for c in cp[a]["own"]:
                c.start()

    def pass_on(self, ins, outs, sems):
        cp = self._plan(ins, outs, sems)
        for a in range(self.n):
            cp[a]["from_x"].wait_recv()
            cp[a]["relay_x"].start()
            cp[a]["pass_x"].start()
        for a in range(self.n):
            cp[a]["from_y"].wait_recv()
            cp[a]["relay_y"].start()
            cp[a]["pass_y"].start()

    def pass_on_relayed(self, ins, outs, sems):
        cp = self._plan(ins, outs, sems)
        for a in range(self.n):
            cp[a]["diag0"].wait_recv()
            cp[a]["pass_d0"].start()
            cp[a]["diag1"].wait_recv()
            cp[a]["pass_d1"].start()

    def finish(self, ins, outs, sems):
        if self.pass_on_at is None:
            self.pass_on(ins, outs, sems)
            self.pass_on_relayed(ins, outs, sems)
        cp = self._plan(ins, outs, sems)
        for a in range(self.n):
            for c in cp[a]["from_sib"]:
                c.wait_recv()
            for c in cp[a]["own"] + [cp[a][k] for k in ("relay_x", "relay_y", "pass_x", "pass_y", "pass_d0", "pass_d1")]:
                c.wait_send()
        for c in cp["mine"]:
            c.wait()


class _Scatter:
    def __init__(self, grads):
        self.grads = list(grads)
        self.n = len(self.grads)

    def inputs(self):
        return self.grads

    def out_shape(self):
        return [jax.ShapeDtypeStruct((N_DEV - 1,) + g.shape[1:], g.dtype) for g in self.grads]

    def scratch(self):
        return [pltpu.SemaphoreType.DMA((7 * self.n,)), pltpu.SemaphoreType.DMA((7 * self.n,))]

    def _plan(self, ins, outs, sems):
        send_sems, recv_sems = sems
        x, y, c, _ = _position()
        cps = []
        for a in range(self.n):
            for k in range(1, N_DEV):
                peer = (x ^ (k >> 2), y ^ ((k >> 1) & 1), c ^ (k & 1))
                cps.append(pltpu.make_async_remote_copy(
                    src_ref=ins[a].at[_slot(peer)], dst_ref=outs[a].at[k - 1],
                    send_sem=send_sems.at[7 * a + k - 1], recv_sem=recv_sems.at[7 * a + k - 1],
                    device_id=peer, device_id_type=MESH))
        return cps

    def start(self, ins, outs, sems):
        for cp in self._plan(ins, outs, sems):
            cp.start()

    def finish(self, ins, outs, sems):
        for cp in self._plan(ins, outs, sems):
            cp.wait()


def _hosted(inner, n_in, n_out, comm, grid):
    if comm is None:
        return inner
    nc_in, nc_out, ns = len(comm.inputs()), len(comm.out_shape()), len(comm.scratch())

    def body(*refs):
        o0 = n_in + nc_in
        s0 = o0 + n_out + nc_out
        main = refs[:n_in] + refs[o0:o0 + n_out] + refs[s0:len(refs) - ns]
        c_in, c_out, c_sems = refs[n_in:o0], refs[o0 + n_out:s0], refs[len(refs) - ns:]
        ids = [pl.program_id(ax) for ax in range(len(grid))]
        first = functools.reduce(jnp.logical_and, [i == 0 for i in ids])
        last = functools.reduce(jnp.logical_and, [i == g - 1 for i, g in zip(ids, grid)])

        @pl.when(first)
        def _():
            comm.start(c_in, c_out, c_sems)

        inner(*main)

        if getattr(comm, "pass_on_at", None) is not None:
            assert len(grid) == 1
            first_at, second_at = (min(grid[0] - 1, int(frac * grid[0])) for frac in comm.pass_on_at)
            assert first_at < second_at

            @pl.when(ids[0] == first_at)
            def _():
                comm.pass_on(c_in, c_out, c_sems)

            @pl.when(ids[0] == second_at)
            def _():
                comm.pass_on_relayed(c_in, c_out, c_sems)

        @pl.when(last)
        def _():
            comm.finish(c_in, c_out, c_sems)

    return body


def _call(inner, name, grid, in_specs, out_specs, out_shape, scratch, args, comm=None):
    n_in, n_out = len(args), len(out_shape)
    body = _hosted(inner, n_in, n_out, comm, grid)
    if comm is not None:
        in_specs = list(in_specs) + [_any()] * len(comm.inputs())
        args = list(args) + comm.inputs()
        out_specs = list(out_specs) + [_any()] * len(comm.out_shape())
        out_shape = list(out_shape) + comm.out_shape()
        scratch = list(scratch) + comm.scratch()
    outs = pl.pallas_call(
        body, name=name, grid=grid, in_specs=list(in_specs), out_specs=list(out_specs), out_shape=list(out_shape),
        scratch_shapes=list(scratch), compiler_params=_params(len(grid)))(*args)
    return list(outs[:n_out]), list(outs[n_out:])


class _Bcast:
    def __init__(self, block):
        self.block = block

    def inputs(self):
        return [self.block]

    def out_shape(self):
        return [jax.ShapeDtypeStruct((N_DEV,) + self.block.shape, self.block.dtype)]

    def scratch(self):
        return [pltpu.SemaphoreType.DMA((N_DEV - 1,)), pltpu.SemaphoreType.DMA((N_DEV - 1,)),
                pltpu.SemaphoreType.DMA((1,))]

    def _plan(self, ins, outs, sems):
        send_sems, recv_sems, local_sem = sems
        x, y, c, _ = _position()
        mine = outs[0].at[_slot((x, y, c))]
        cps = []
        for k in range(1, N_DEV):
            peer = (x ^ (k >> 2), y ^ ((k >> 1) & 1), c ^ (k & 1))
            cps.append(pltpu.make_async_remote_copy(
                src_ref=ins[0], dst_ref=mine, send_sem=send_sems.at[k - 1], recv_sem=recv_sems.at[k - 1],
                device_id=peer, device_id_type=MESH))
        return pltpu.make_async_copy(ins[0], mine, local_sem.at[0]), cps

    def start(self, ins, outs, sems):
        own, cps = self._plan(ins, outs, sems)
        own.start()
        for cp in cps:
            cp.start()

    def finish(self, ins, outs, sems):
        own, cps = self._plan(ins, outs, sems)
        for cp in cps:
            cp.wait()
        own.wait()


def _first_gather(shards, small_idx, x2, t2, n_meta, tp):
    comm = _Gather(shards)
    n = comm.n
    seq, d = x2.shape
    t_real = n_meta + seq
    n_pad = tp - t_real
    cw = d // N_DEV
    rows = STAGE_ROWS if seq % STAGE_ROWS == 0 else seq
    n_chunks = seq // rows

    def body(*refs):
        ins, (x_ref, t_ref) = refs[:n], refs[n:n + 2]
        outs, (h0_ref, tg_ref) = refs[n + 2:2 * n + 2], refs[2 * n + 2:2 * n + 4]
        sems = refs[2 * n + 4:2 * n + 7]
        buf, zeros, in_sems, out_sems, misc_sems = refs[2 * n + 7:]
        comm.start(ins, outs, sems)
        zeros[...] = jnp.zeros_like(zeros)
        fills = [pltpu.make_async_copy(zeros.at[pl.ds(0, n_pad)], h0_ref.at[pl.ds(t_real, n_pad)], misc_sems.at[0]),
                 pltpu.make_async_copy(zeros.at[pl.ds(0, n_pad)], tg_ref.at[pl.ds(t_real, n_pad)], misc_sems.at[1]),
                 pltpu.make_async_copy(zeros.at[pl.ds(0, n_meta)], tg_ref.at[pl.ds(0, n_meta)], misc_sems.at[2])]
        for cp in fills:
            cp.start()
        jobs = [(src, dst, c) for src, dst in ((x_ref, h0_ref), (t_ref, tg_ref)) for c in range(n_chunks)]

        def load(k):
            src, _, c = jobs[k]
            return pltpu.make_async_copy(src.at[pl.ds(c * rows, rows)], buf.at[k % 2], in_sems.at[k % 2])

        def store(k):
            _, dst, c = jobs[k]
            return pltpu.make_async_copy(buf.at[k % 2], dst.at[pl.ds(n_meta + c * rows, rows)], out_sems.at[k % 2])

        load(0).start()
        for k in range(len(jobs)):
            load(k).wait()
            if k + 1 < len(jobs):
                if k >= 1:
                    store(k - 1).wait()
                load(k + 1).start()
            store(k).start()
        for k in range(max(0, len(jobs) - 2), len(jobs)):
            store(k).wait()
        comm.finish(ins, outs, sems)
        meta = [pltpu.make_async_copy(outs[small_idx].at[k, pl.ds(0, n_meta)],
                                      h0_ref.at[pl.ds(0, n_meta), pl.ds(k * cw, cw)], misc_sems.at[3 + k])
                for k in range(N_DEV)]
        for cp in meta:
            cp.start()
        for cp in fills + meta:
            cp.wait()

    staged = [jax.ShapeDtypeStruct((tp, d), F32)] * 2
    outs = pl.pallas_call(
        body, name="weights_all_gather", out_shape=comm.out_shape() + staged,
        in_specs=[_any()] * (n + 2), out_specs=[_any()] * (n + 2),
        scratch_shapes=comm.scratch() + [
            pltpu.VMEM((2, rows, d), F32), pltpu.VMEM((max(n_pad, n_meta), d), F32),
            pltpu.SemaphoreType.DMA((2,)), pltpu.SemaphoreType.DMA((2,)), pltpu.SemaphoreType.DMA((3 + N_DEV,))],
        compiler_params=pltpu.CompilerParams(vmem_limit_bytes=V7X_VMEM_LIMIT),
    )(*shards, x2, t2)
    return outs[:n], outs[n], outs[n + 1]


def _chip_copies(c_ref, land_ref, sems):
    _, _, c, chips = _position()
    return [pltpu.make_async_remote_copy(
        src_ref=c_ref.at[2 * cx + cy], dst_ref=land_ref.at[j], send_sem=sems[j], recv_sem=sems[3 + j],
        device_id=(cx, cy, c), device_id_type=MESH) for j, (cx, cy) in enumerate(chips)]


def _scatter_copies(g_ref, land_ref, sems):
    x, y, c, _ = _position()
    cps = []
    for k in range(1, N_DEV):
        peer = (x ^ (k >> 2), y ^ ((k >> 1) & 1), c ^ (k & 1))
        cps.append(pltpu.make_async_remote_copy(
            src_ref=g_ref.at[_slot(peer)], dst_ref=land_ref.at[k - 1], send_sem=sems[k - 1],
            recv_sem=sems[N_DEV - 1 + k - 1], device_id=peer, device_id_type=MESH))
    return cps


def _exchange_start(name, copies, n_copies, src):
    hbm = pl.BlockSpec(memory_space=pltpu.HBM)
    sem = pl.BlockSpec(memory_space=pltpu.SEMAPHORE)
    n_sems = 2 * n_copies

    def body(s_ref, land_ref, *refs):
        for cp in copies(s_ref, land_ref, refs[:n_sems]):
            cp.start()
        token = refs[n_sems + 2]
        token[...] = jnp.zeros_like(token)

    land = lax.empty((n_copies,) + src.shape[1:], src.dtype)
    outs = pl.pallas_call(
        body, name=name + "_start",
        out_shape=(pltpu.SemaphoreType.DMA(()),) * n_sems
        + (pltpu.HBM(src.shape, src.dtype), pltpu.HBM(land.shape, land.dtype),
           jax.ShapeDtypeStruct((SUBLANES, 128), F32)),
        in_specs=(hbm, hbm), out_specs=(sem,) * n_sems + (hbm, hbm, pl.BlockSpec(memory_space=pltpu.VMEM)),
        input_output_aliases={0: n_sems, 1: n_sems + 1},
        compiler_params=pltpu.CompilerParams(has_side_effects=pltpu.SideEffectType.DATAFLOW_SIDE_EFFECTING),
    )(pltpu.with_memory_space_constraint(src, pltpu.HBM), pltpu.with_memory_space_constraint(land, pltpu.HBM))
    return outs[:n_sems], outs[n_sems], outs[n_sems + 1], outs[n_sems + 2]


def _exchange_wait(name, copies, sems, src_thru, land_thru, after):
    hbm = pl.BlockSpec(memory_space=pltpu.HBM)
    sem = pl.BlockSpec(memory_space=pltpu.SEMAPHORE)
    n_sems = len(sems)

    def body(s_ref, land_ref, *refs):
        for cp in copies(s_ref, land_ref, refs[:n_sems]):
            cp.wait_send()
            cp.wait_recv()

    return pl.pallas_call(
        body, name=name + "_wait",
        out_shape=(pltpu.HBM(src_thru.shape, src_thru.dtype), pltpu.HBM(land_thru.shape, land_thru.dtype)),
        in_specs=(hbm, hbm) + (sem,) * n_sems + (pl.BlockSpec(memory_space=pl.ANY),), out_specs=(hbm, hbm),
        input_output_aliases={0: 0, 1: 1},
        compiler_params=pltpu.CompilerParams(has_side_effects=pltpu.SideEffectType.DATAFLOW_SIDE_EFFECTING),
    )(src_thru, land_thru, *sems, after)


def _pair_reduce(grad, core):
    blk = grad.shape[2:]
    zeros = (0,) * len(blk)

    def body(core_ref, g_hbm, own_ref, o_ref, landed, send_sems, recv_sems):
        del core_ref
        i = pl.program_id(0)
        x, y, c, _ = _position()

        def copy(k):
            return pltpu.make_async_remote_copy(
                src_ref=g_hbm.at[k, 1 - c], dst_ref=landed.at[k], send_sem=send_sems.at[k],
                recv_sem=recv_sems.at[k], device_id=(x, y, 1 - c), device_id_type=MESH)

        @pl.when(i == 0)
        def _():
            for k in range(4):
                copy(k).start()

        for k in range(4):
            @pl.when(i == k)
            def _(k=k):
                copy(k).wait_recv()

        o_ref[...] = (own_ref[...].astype(F32) + landed[i].astype(F32)).astype(BF16)

        @pl.when(i == 3)
        def _():
            for k in range(4):
                copy(k).wait_send()

    return pl.pallas_call(
        body, name="grads_pair_reduce",
        out_shape=jax.ShapeDtypeStruct((4,) + blk, BF16),
        grid_spec=pltpu.PrefetchScalarGridSpec(
            num_scalar_prefetch=1, grid=(4,),
            in_specs=[_any(), pl.BlockSpec((None, None) + blk, lambda i, cr: (i, cr[0]) + zeros)],
            out_specs=pl.BlockSpec((None,) + blk, lambda i, cr: (i,) + zeros),
            scratch_shapes=[pltpu.VMEM((4,) + blk, BF16), pltpu.SemaphoreType.DMA((4,)),
                            pltpu.SemaphoreType.DMA((4,))]),
        compiler_params=_params(1),
    )(core, grad, grad)


def _adamw(w, g, m, v):
    m2 = ADAM_B1 * m + (1.0 - ADAM_B1) * g
    v2 = ADAM_B2 * v + (1.0 - ADAM_B2) * (g * g)
    m_hat = m2 / (1.0 - ADAM_B1 ** ADAM_STEP)
    v_hat = v2 / (1.0 - ADAM_B2 ** ADAM_STEP)
    delta = -ADAM_LR * (m_hat / (jnp.sqrt(v_hat) + ADAM_EPS) + ADAM_WD * w)
    return delta, m2, v2


def _final_adamw(own, recv, idx, parts, after):
    blk = own.shape[1:]
    n_recv = recv.shape[0]
    n_parts = len(parts)
    per = blk[0] // n_parts if n_parts > 1 else None
    rows = blk[-2]
    n_chunks = 1 if n_parts > 1 else (4 if rows % 64 == 0 and rows >= 512 else (2 if rows % 32 == 0 else 1))
    cblk = blk[:-2] + (rows // n_chunks, blk[-1])
    lead = (0,) * (len(blk) - 2)

    def body(idx_ref, c_ref, r_ref, after_ref, *refs):
        del idx_ref, after_ref
        ins, outs = refs[:3 * n_parts], refs[3 * n_parts:]
        g = c_ref[...].astype(F32)
        for k in range(n_recv):
            g = g + r_ref[k].astype(F32)
        for p in range(n_parts):
            w_ref, m_ref, v_ref = ins[3 * p:3 * p + 3]
            if n_parts == 1:
                gp = g
            elif per == 1:
                gp = g[p]
            else:
                gp = g[p * per:(p + 1) * per]
            delta, m2, v2 = _adamw(w_ref[0], gp, m_ref[0], v_ref[0])
            o = outs[4 * p:4 * p + 4]
            o[0][0] = gp
            o[1][0] = delta
            o[2][0] = m2
            o[3][0] = v2

    flat = [a for wmv in parts for a in wmv]

    def part_spec(a):
        shape = a.shape[:-2] + (a.shape[-2] // n_chunks, a.shape[-1])
        return pl.BlockSpec(shape, lambda i, cr, nd=a.ndim: (0,) * (nd - 2) + (i, 0))

    outs = pl.pallas_call(
        body, name="grads_sum_adamw",
        out_shape=[jax.ShapeDtypeStruct(wmv[0].shape, F32) for wmv in parts for _ in range(4)],
        grid_spec=pltpu.PrefetchScalarGridSpec(
            num_scalar_prefetch=1, grid=(n_chunks,),
            in_specs=[pl.BlockSpec((None,) + cblk, lambda i, cr: (cr[0],) + lead + (i, 0)),
                      pl.BlockSpec((n_recv,) + cblk, lambda i, cr: (0,) + lead + (i, 0))]
                     + [_any()] + [part_spec(a) for a in flat],
            out_specs=[part_spec(wmv[0]) for wmv in parts for _ in range(4)]),
        compiler_params=_params(1),
    )(idx, own, recv, after, *flat)
    return [tuple(outs[4 * p:4 * p + 4]) for p in range(n_parts)]


def _small_adamw(partials, layout, me_index, after):
    _, rows, d = partials.shape
    n = len(layout)
    cw = d // N_DEV

    def body(me_ref, p_ref, after_ref, *refs):
        ins, t_ref, outs = refs[:3 * n], refs[3 * n], refs[3 * n + 1:]
        me = me_ref[0]
        total = p_ref[0]
        for j in range(1, N_DEV):
            total = total + p_ref[j]
        t_ref[...] = total
        for e, (kind, r0, nr, _, _, _) in enumerate(layout):
            w_ref, m_ref, v_ref = ins[3 * e:3 * e + 3]
            o = outs[4 * e:4 * e + 4]
            if kind == "rep":
                g = t_ref[r0:r0 + nr, :]
                delta, m2, v2 = _adamw(w_ref[...], g, m_ref[...], v_ref[...])
                for ref, val in zip(o, (g, delta, m2, v2)):
                    ref[...] = val
            elif kind == "wide":
                for q in range(nr):
                    sl = slice(q * d, (q + 1) * d)
                    g = t_ref[r0 + q:r0 + q + 1, :]
                    delta, m2, v2 = _adamw(w_ref[:, sl], g, m_ref[:, sl], v_ref[:, sl])
                    for ref, val in zip(o, (g, delta, m2, v2)):
                        ref[:, sl] = val
            else:
                for j in range(N_DEV):
                    @pl.when(me == j)
                    def _(j=j, o=o, w_ref=w_ref, m_ref=m_ref, v_ref=v_ref, r0=r0, nr=nr):
                        g = t_ref[r0:r0 + nr, j * cw:(j + 1) * cw]
                        delta, m2, v2 = _adamw(w_ref[...], g, m_ref[...], v_ref[...])
                        for ref, val in zip(o, (g, delta, m2, v2)):
                            ref[...] = val

    flat = [a for ent in layout for a in ent[3:]]
    vm = pl.BlockSpec(memory_space=pltpu.VMEM)
    outs = pl.pallas_call(
        body, name="small_adamw",
        out_shape=[jax.ShapeDtypeStruct((rows, d), F32)]
                  + [jax.ShapeDtypeStruct(ent[3].shape, F32) for ent in layout for _ in range(4)],
        in_specs=[pl.BlockSpec(memory_space=pltpu.SMEM), vm, _any()] + [vm] * len(flat),
        out_specs=[vm] * (1 + 4 * n),
        compiler_params=pltpu.CompilerParams(vmem_limit_bytes=V7X_VMEM_LIMIT),
    )(me_index, partials, after, *flat)
    return outs[0], [tuple(outs[1 + 4 * e:5 + 4 * e]) for e in range(n)]


def _ffn_fwd(h, g, wgu, wd, tm, loss=None, comm=None):
    tp, d = h.shape
    f = wd.shape[0]
    fc = f // FFN_FWD_CHUNKS
    nt = tp // tm
    with_loss = loss is not None
    if with_loss:
        tgt, gf, n_meta, t_real = loss

    def body(*refs):
        if with_loss:
            (h_ref, g_ref, wgu_hbm, wd_hbm, tgt_ref, gf_ref, out_ref, gu_ref, n_ref, tail_ref,
             wgu_v, wd_v, sems) = refs
        else:
            h_ref, g_ref, wgu_hbm, wd_hbm, out_ref, gu_ref, n_ref, wgu_v, wd_v, sems = refs
        i = pl.program_id(0)

        @pl.when(i == 0)
        def _():
            _load_weights([(wgu_hbm, wgu_v), (wd_hbm, wd_v)], sems)
            if with_loss:
                tail_ref[...] = jnp.zeros_like(tail_ref)

        x = h_ref[...]
        n, _ = _rms_fwd(x, g_ref[...])
        nb = n.astype(BF16)
        n_ref[...] = nb
        acc = jnp.zeros((tm, d), F32)
        for j in range(FFN_FWD_CHUNKS):
            cols = slice(j * fc, (j + 1) * fc)
            gate = _nt(nb, wgu_v[pl.ds(j * fc, fc), :])
            up = _nt(nb, wgu_v[pl.ds(f + j * fc, fc), :])
            gu_ref[0, :, cols] = gate.astype(BF16)
            gu_ref[1, :, cols] = up.astype(BF16)
            act = (gate * _sigmoid(gate) * up).astype(BF16)
            acc = acc + _nn(act, wd_v[pl.ds(j * fc, fc), :])
        hn = x + FFN_RES * acc
        if not with_loss:
            out_ref[...] = hn
        else:
            gfv = gf_ref[...]
            r = lax.rsqrt(jnp.mean(hn * hn, axis=-1, keepdims=True) + EPS)
            xr = hn * r
            rows = i * tm + lax.broadcasted_iota(jnp.int32, (tm, 1), 0)
            mask = jnp.logical_and(rows >= n_meta, rows < t_real)
            diff = jnp.where(mask, xr * gfv - tgt_ref[...], 0.0)
            tail_ref[TAIL_LOSS:TAIL_LOSS + 1, :] += jnp.zeros((1, d), F32) + 0.5 * jnp.sum(diff * diff) / d
            dy = diff / d
            gy = dy * gfv
            out_ref[...] = r * (gy - xr * jnp.mean(gy * xr, axis=-1, keepdims=True))
            tail_ref[TAIL_FINAL:TAIL_FINAL + 1, :] += _rowsum(dy * xr)

    row = pl.BlockSpec((tm, d), lambda i: (i, 0))
    vec = pl.BlockSpec((1, d), lambda i: (0, 0))
    in_specs = [row, vec, _any(), _any()]
    out_shape = [jax.ShapeDtypeStruct((tp, d), F32), jax.ShapeDtypeStruct((2, tp, f), BF16),
                 jax.ShapeDtypeStruct((tp, d), BF16)]
    out_specs = [row, pl.BlockSpec((2, tm, f), lambda i: (0, i, 0)), row]
    args = [h, g, wgu, wd]
    if with_loss:
        in_specs += [row, vec]
        out_shape += [jax.ShapeDtypeStruct((SUBLANES, d), F32)]
        out_specs += [pl.BlockSpec((SUBLANES, d), lambda i: (0, 0))]
        args += [tgt, gf]
    return _call(body, "ffn_fwd_loss" if with_loss else "ffn_fwd", (nt,), in_specs, out_specs, out_shape,
                 [pltpu.VMEM((2 * f, d), BF16), pltpu.VMEM((f, d), BF16), pltpu.SemaphoreType.DMA((2,))],
                 args, comm)


def _ffn_bwd(dh, h, gu, g, wgu, wd, tm, tail, tail_row, after):
    tp, d = h.shape
    f = wd.shape[0]
    fc = f // FFN_CHUNKS
    nt = tp // tm

    def body(dh_ref, h_ref, gu_ref, g_ref, tail_ref, wgu_hbm, wd_hbm, after_ref,
             dhin_ref, dgu_ref, act_ref, df_ref, dg_ref, wgu_v, wd_v, dn_v, sems):
        del after_ref
        i, j = pl.program_id(0), pl.program_id(1)

        @pl.when(jnp.logical_and(i == 0, j == 0))
        def _():
            _load_weights([(wgu_hbm, wgu_v), (wd_hbm, wd_v)], sems)
            dg_ref[...] = tail_ref[...]

        dfb = (FFN_RES * dh_ref[...]).astype(BF16)

        @pl.when(j == 0)
        def _():
            df_ref[...] = dfb
            dn_v[...] = jnp.zeros_like(dn_v)

        lo = pl.multiple_of(j * fc, 16)
        dact = _nt(dfb, wd_v[pl.ds(lo, fc), :])
        gate = gu_ref[0].astype(F32)
        up = gu_ref[1].astype(F32)
        sg = _sigmoid(gate)
        silu = gate * sg
        act_ref[...] = (silu * up).astype(BF16)
        dgate = (dact * up * (sg * (1.0 + gate * (1.0 - sg)))).astype(BF16)
        dup = (dact * silu).astype(BF16)
        dgu_ref[0] = dgate
        dgu_ref[1] = dup
        dn_v[...] += _nn(dgate, wgu_v[pl.ds(lo, fc), :]) + _nn(dup, wgu_v[pl.ds(pl.multiple_of(f + j * fc, 16), fc), :])

        @pl.when(j == FFN_CHUNKS - 1)
        def _():
            x = h_ref[...]
            r = lax.rsqrt(jnp.mean(x * x, axis=-1, keepdims=True) + EPS)
            dx, dgp = _rms_bwd(dn_v[...], x, r, g_ref[...])
            dhin_ref[...] = dh_ref[...] + dx
            dg_ref[tail_row:tail_row + 1, :] += dgp

    row = pl.BlockSpec((tm, d), lambda i, j: (i, 0))
    vec = pl.BlockSpec((1, d), lambda i, j: (0, 0))
    tile = pl.BlockSpec((SUBLANES, d), lambda i, j: (0, 0))
    hid2 = pl.BlockSpec((2, tm, fc), lambda i, j: (0, i, j))
    return _call(
        body, "ffn_bwd", (nt, FFN_CHUNKS),
        [row, row, hid2, vec, tile, _any(), _any(), _any()],
        [row, hid2, pl.BlockSpec((tm, fc), lambda i, j: (i, j)), row, tile],
        [jax.ShapeDtypeStruct((tp, d), F32), jax.ShapeDtypeStruct((2, tp, f), BF16),
         jax.ShapeDtypeStruct((tp, f), BF16), jax.ShapeDtypeStruct((tp, d), BF16),
         jax.ShapeDtypeStruct((SUBLANES, d), F32)],
        [pltpu.VMEM((2 * f, d), BF16), pltpu.VMEM((f, d), BF16), pltpu.VMEM((tm, d), F32),
         pltpu.SemaphoreType.DMA((2,))],
        [dh, h, gu, g, tail, wgu, wd, after])


def _piece_segments(q, d, nb_cols):
    segs = []
    for j in range(N_DEV):
        lo, hi = max(q * d, j * nb_cols), min((q + 1) * d, (j + 1) * nb_cols)
        if lo < hi:
            segs.append((j, lo - q * d, hi - q * d, lo - j * nb_cols, hi - j * nb_cols))
    return segs


def _w3_copies(w3_hbm, rows, w3_v):
    return [(w3_hbm.at[k, pl.ds(q * rows, rows)], w3_v.at[q, pl.ds(k * rows, rows)])
            for q in range(3) for k in range(N_DEV)]


def _gates(xrb, wg_ref, ba, bx, lam, hd):
    pre_r, pre_i = [], []
    for hh in range(N_HEADS):
        xh = xrb[:, hh * hd:(hh + 1) * hd]
        pre_r.append(_nn(xh, wg_ref[0, hh]))
        pre_i.append(_nn(xh, wg_ref[1, hh]))
    r = _sigmoid(jnp.concatenate(pre_r, axis=1) + ba)
    ig = _sigmoid(jnp.concatenate(pre_i, axis=1) + bx)
    sp = _softplus(-lam)
    log_a = -RG_LRU_C * r * sp
    a = jnp.exp(log_a)
    s = jnp.sqrt(_one_minus_exp(2.0 * log_a))
    return r, ig, sp, a, s


def _scan_fwd(a, u, h_prev):
    tm = a.shape[0]
    rows = lax.broadcasted_iota(jnp.int32, a.shape, 0)
    d = 1
    while d < tm:
        if d < SUBLANES:
            keep = rows >= d
            u = jnp.where(keep, a * pltpu.roll(u, d, 0) + u, u)
            a = jnp.where(keep, a * pltpu.roll(a, d, 0), a)
        else:
            u = jnp.concatenate([u[:d], a[d:] * u[:tm - d] + u[d:]], axis=0)
            a = jnp.concatenate([a[:d], a[d:] * a[:tm - d]], axis=0)
        d *= 2
    return u + a * h_prev


def _scan_bwd(b, v, g_next):
    tm = b.shape[0]
    rows = lax.broadcasted_iota(jnp.int32, b.shape, 0)
    d = 1
    while d < tm:
        if d < SUBLANES:
            keep = rows < tm - d
            v = jnp.where(keep, v + b * pltpu.roll(v, tm - d, 0), v)
            b = jnp.where(keep, b * pltpu.roll(b, tm - d, 0), b)
        else:
            v = jnp.concatenate([v[:tm - d] + b[:tm - d] * v[d:], v[tm - d:]], axis=0)
            b = jnp.concatenate([b[:tm - d] * b[d:], b[tm - d:]], axis=0)
        d *= 2
    return v + b * g_next


def _shifted_copies(ext_ref, es_ref, n_rows):
    for s in range(1, SUBLANES):
        es_ref[s, pl.ds(0, n_rows), :] = ext_ref[pl.ds(s, n_rows), :]


def _tap(ext_ref, es_ref, off, tm):
    q, s = divmod(off, SUBLANES)
    if s == 0:
        return ext_ref[pl.ds(SUBLANES * q, tm), :]
    return es_ref[s, pl.ds(SUBLANES * q, tm), :]


def _mixer_fwd(h, g, b_in, win_all, cw4, cb4, wg, ba, bx, lam, cw31, cb31, lng, lnb, bcp, w3_all, tm, comm=None):
    tp, d = h.shape
    nb_cols = win_all.shape[-1]
    n_in = N_DEV * nb_cols
    hd = wg.shape[-1]
    k4, k31 = cw4.shape[0], cw31.shape[0]
    w3_rows = d // N_DEV

    def body(h_ref, g_ref, b_ref, win_hbm, cw4_ref, cb4_ref, wg_ref, ba_ref, bx_ref, lam_ref, cw31_ref, cb31_ref,
             lng_ref, lnb_ref, bcp_ref, w3_hbm,
             h2_ref, p_ref, n_ref, xr_ref, hs_ref, v1_ref, ya_ref, yb_ref,
             win_v, w3_v, ext4, ext31, es31, hcar, sems):
        @pl.when(pl.program_id(0) == 0)
        def _():
            _load_weights([(win_hbm, win_v)] + _w3_copies(w3_hbm, w3_rows, w3_v), sems)
            ext4[pl.ds(0, CONV4_HALO), :] = jnp.zeros((CONV4_HALO, d), F32)
            ext31[pl.ds(0, CONV31_HALO), :] = jnp.zeros((CONV31_HALO, d), F32)
            hcar[...] = jnp.zeros_like(hcar)

        n, _ = _rms_fwd(h_ref[...], g_ref[...])
        nb = n.astype(BF16)
        n_ref[...] = nb

        def piece(q):
            parts = [_nn(nb, win_v[j, :, bl:bh]) for j, _, _, bl, bh in _piece_segments(q, d, nb_cols)]
            pq = (jnp.concatenate(parts, axis=1) + b_ref[:, q * d:(q + 1) * d]).astype(BF16)
            p_ref[:, q * d:(q + 1) * d] = pq
            return pq.astype(F32)

        x_rnn, y_rnn, glu_v, glu_g, gate_a, gate_b = [piece(q) for q in range(6)]

        ext4[pl.ds(CONV4_HALO, tm), :] = x_rnn
        xr = cb4_ref[...] + jnp.zeros((tm, d), F32)
        for k in range(k4):
            xr = xr + cw4_ref[k:k + 1, :] * ext4[pl.ds(CONV4_HALO - (k4 - 1) + k, tm), :]
        ext4[pl.ds(0, CONV4_HALO), :] = ext4[pl.ds(tm, CONV4_HALO), :]
        xrb = xr.astype(BF16)
        xr_ref[...] = xrb
        xr = xrb.astype(F32)
        _, ig, _, a, s = _gates(xrb, wg_ref, ba_ref[...], bx_ref[...], lam_ref[...], hd)
        hseq = _scan_fwd(a, s * (ig * xr), hcar[0:1, :])
        hcar[0:1, :] = hseq[tm - 1:tm, :]
        hs_ref[...] = hseq.astype(BF16)
        gl, _ = _gelu(y_rnn)
        ya = _nn((hseq * gl).astype(BF16), w3_v[0])
        ya_ref[...] = ya.astype(BF16)

        ext31[pl.ds(CONV31_HALO, tm), :] = glu_v * _sigmoid(glu_g)
        _shifted_copies(ext31, es31, tm + CONV31_HALO - SUBLANES)
        v1 = cb31_ref[...] + jnp.zeros((tm, d), F32)
        for k in range(k31):
            v1 = v1 + cw31_ref[k:k + 1, :] * _tap(ext31, es31, CONV31_HALO - (k31 - 1) + k, tm)
        ext31[pl.ds(0, CONV31_HALO), :] = ext31[pl.ds(tm, CONV31_HALO), :]
        v1b = v1.astype(BF16)
        v1_ref[...] = v1b
        v1 = v1b.astype(F32)
        xc = v1 - jnp.mean(v1, axis=-1, keepdims=True)
        rstd = lax.rsqrt(jnp.mean(xc * xc, axis=-1, keepdims=True) + EPS)
        v2 = xc * rstd * lng_ref[...] + lnb_ref[...]
        yb = _nn((v2 * _sigmoid(v2)).astype(BF16), w3_v[1]) + bcp_ref[...]
        yb_ref[...] = yb.astype(BF16)

        merged = _sigmoid(gate_a) * ya + _sigmoid(gate_b) * yb
        h2_ref[...] = h_ref[...] + _nn(merged.astype(BF16), w3_v[2])

    row = pl.BlockSpec((tm, d), lambda i: (i, 0))
    wide = pl.BlockSpec((tm, n_in), lambda i: (i, 0))
    full = lambda a: pl.BlockSpec(a.shape, lambda i, nd=a.ndim: (0,) * nd)
    smalls = [cw4, cb4, wg, ba, bx, lam, cw31, cb31, lng, lnb, bcp]
    return _call(
        body, "mixer_fwd", (tp // tm,),
        [row, full(g), full(b_in), _any()] + [full(a) for a in smalls] + [_any()],
        [row, wide] + [row] * 6,
        [jax.ShapeDtypeStruct((tp, d), F32), jax.ShapeDtypeStruct((tp, n_in), BF16)]
        + [jax.ShapeDtypeStruct((tp, d), BF16)] * 6,
        [pltpu.VMEM(win_all.shape, BF16),
         pltpu.VMEM((3, d, d), BF16),
         pltpu.VMEM((tm + CONV4_HALO, d), F32),
         pltpu.VMEM((tm + CONV31_HALO, d), F32),
         pltpu.VMEM((SUBLANES, tm + CONV31_HALO, d), F32),
         pltpu.VMEM((SUBLANES, d), F32),
         pltpu.SemaphoreType.DMA((1 + 3 * N_DEV,))],
        [h, g, b_in, win_all, *smalls, w3_all], comm)


SG_BIN, SG_CW4, SG_CB4, SG_BA, SG_BX, SG_LAM, SG_CB31, SG_LNG, SG_LNB, SG_BCP, SG_MIX, SG_CW31 = 0, 6, 10, 11, 12, 13, 14, 15, 16, 17, 18, 19


def _mixer_bwd(dh2, h, g, proj, xr_s, hs_s, v1_s, ya_s, yb_s, win_t, cw4, wg, ba, bx, lam, cw31, lng, lnb, w3_all, tm,
               comm=None):
    tp, d = dh2.shape
    n_in = proj.shape[1]
    hd = wg.shape[-1]
    k4, k31 = cw4.shape[0], cw31.shape[0]
    nt = tp // tm
    w3_rows = d // N_DEV
    sg_rows = -(-(SG_CW31 + k31) // SUBLANES) * SUBLANES
    halo_rows = 16
    per = tm // halo_rows

    def body(dh_ref, h_ref, g_ref, p_ref, xr_ref, hs_ref, hh_ref, v1_ref, ya_ref, yb_ref, win_hbm,
             cw4_ref, wg_ref, wgt_ref, ba_ref, bx_ref, lam_ref, cw31_ref, lng_ref, lnb_ref, w3_hbm,
             dh1_ref, dp_ref, x3_ref, y3_ref, yg_ref, sg_ref,
             win_v, w3_v, extd4, extd31, es31, gcar, sems):
        i = pl.program_id(0)
        tile = nt - 1 - i

        @pl.when(i == 0)
        def _():
            _load_weights([(win_hbm, win_v)] + _w3_copies(w3_hbm, w3_rows, w3_v), sems)
            for q in range(3):
                w3_v[q] = w3_v[q].T
            extd4[pl.ds(tm, CONV4_HALO), :] = jnp.zeros((CONV4_HALO, d), F32)
            extd31[pl.ds(tm, CONV31_HALO), :] = jnp.zeros((CONV31_HALO, d), F32)
            gcar[...] = jnp.zeros_like(gcar)
            sg_ref[...] = jnp.zeros_like(sg_ref)

        def acc(row, val):
            sg_ref[row:row + 1, :] += _rowsum(val)

        rows = lax.broadcasted_iota(jnp.int32, (tm, d), 0)
        x_rnn = p_ref[:, 0:d].astype(F32)
        y_rnn = p_ref[:, d:2 * d].astype(F32)
        glu_v = p_ref[:, 2 * d:3 * d].astype(F32)
        glu_g = p_ref[:, 3 * d:4 * d].astype(F32)
        sga = _sigmoid(p_ref[:, 4 * d:5 * d].astype(F32))
        sgb = _sigmoid(p_ref[:, 5 * d:6 * d].astype(F32))
        ya = ya_ref[...].astype(F32)
        yb = yb_ref[...].astype(F32)

        dmob = dh_ref[...].astype(BF16)
        dmerged = _nn(dmob, w3_v[2])
        x3_ref[:, 0:d] = (sga * ya + sgb * yb).astype(BF16)
        y3_ref[:, 0:d] = dmob
        dya = sga * dmerged
        dyb = sgb * dmerged
        dn_parts = []

        def emit(q, val):
            vb = val.astype(BF16)
            dp_ref[:, q * d:(q + 1) * d] = vb
            acc(SG_BIN + q, val)
            term = _nn(vb, win_v[pl.ds(q * d, d), :])
            dn_parts[:] = [term if not dn_parts else dn_parts[0] + term]

        emit(4, dmerged * ya * sga * (1.0 - sga))
        emit(5, dmerged * yb * sgb * (1.0 - sgb))

        dyab = dya.astype(BF16)
        y3_ref[:, d:2 * d] = dyab
        dza = _nn(dyab, w3_v[0])
        hsv = hs_ref[...].astype(F32)
        gl, th = _gelu(y_rnn)
        x3_ref[:, d:2 * d] = (hsv * gl).astype(BF16)
        emit(1, dza * hsv * _gelu_grad(y_rnn, th))
        dhs = dza * gl
        xrb = xr_ref[...]
        xr = xrb.astype(F32)
        lam_v = lam_ref[...]
        r, ig, sp, a, s = _gates(xrb, wg_ref, ba_ref[...], bx_ref[...], lam_v, hd)
        b = jnp.where(rows == tm - 1, gcar[1:2, :], pltpu.roll(a, tm - 1, 0))
        big_g = _scan_bwd(b, dhs, gcar[0:1, :])
        gcar[0:1, :] = big_g[0:1, :]
        gcar[1:2, :] = a[0:1, :]
        h_before = jnp.where(tile > 0, hh_ref[halo_rows - 1:halo_rows, :].astype(F32), 0.0)
        h_prev = jnp.where(rows == 0, h_before, pltpu.roll(hsv, 1, 0))
        ds = big_g * ig * xr
        dla = big_g * h_prev * a - ds * (a * a) / jnp.maximum(s, 1e-20)
        acc(SG_LAM, dla * r * (RG_LRU_C * _sigmoid(-lam_v)))
        dpr = dla * (-RG_LRU_C * sp) * r * (1.0 - r)
        dpi = big_g * s * xr * ig * (1.0 - ig)
        acc(SG_BA, dpr)
        acc(SG_BX, dpi)
        dprb = dpr.astype(BF16)
        dpib = dpi.astype(BF16)
        yg_ref[:, 0:d] = dprb
        yg_ref[:, d:2 * d] = dpib
        back = []
        for hh in range(N_HEADS):
            sl = slice(hh * hd, (hh + 1) * hd)
            back.append(_nn(dprb[:, sl], wgt_ref[0, hh]) + _nn(dpib[:, sl], wgt_ref[1, hh]))
        dxr = big_g * s * ig + jnp.concatenate(back, axis=1)
        acc(SG_CB4, dxr)
        extd4[pl.ds(0, tm), :] = dxr
        dx_rnn = jnp.zeros((tm, d), F32)
        for k in range(k4):
            term = extd4[pl.ds(k4 - 1 - k, tm), :]
            dx_rnn = dx_rnn + cw4_ref[k:k + 1, :] * term
            acc(SG_CW4 + k, x_rnn * term)
        extd4[pl.ds(tm, CONV4_HALO), :] = extd4[pl.ds(0, CONV4_HALO), :]
        emit(0, dx_rnn)

        dybb = dyb.astype(BF16)
        y3_ref[:, 2 * d:3 * d] = dybb
        acc(SG_BCP, dyb)
        dv3 = _nn(dybb, w3_v[1])
        v1 = v1_ref[...].astype(F32)
        xc = v1 - jnp.mean(v1, axis=-1, keepdims=True)
        rstd = lax.rsqrt(jnp.mean(xc * xc, axis=-1, keepdims=True) + EPS)
        xhat = xc * rstd
        lng_v = lng_ref[...]
        v2 = xhat * lng_v + lnb_ref[...]
        s2 = _sigmoid(v2)
        x3_ref[:, 2 * d:3 * d] = (v2 * s2).astype(BF16)
        dv2 = dv3 * (s2 * (1.0 + v2 * (1.0 - s2)))
        acc(SG_LNG, dv2 * xhat)
        acc(SG_LNB, dv2)
        dxh = dv2 * lng_v
        dv1 = rstd * (dxh - jnp.mean(dxh, axis=-1, keepdims=True)
                      - xhat * jnp.mean(dxh * xhat, axis=-1, keepdims=True))
        acc(SG_CB31, dv1)
        extd31[pl.ds(0, tm), :] = dv1
        _shifted_copies(extd31, es31, tm + CONV31_HALO - SUBLANES)
        sgg = _sigmoid(glu_g)
        v0 = glu_v * sgg
        dv0 = jnp.zeros((tm, d), F32)
        for k in range(k31):
            term = _tap(extd31, es31, k31 - 1 - k, tm)
            dv0 = dv0 + cw31_ref[k:k + 1, :] * term
            acc(SG_CW31 + k, v0 * term)
        extd31[pl.ds(tm, CONV31_HALO), :] = extd31[pl.ds(0, CONV31_HALO), :]
        emit(2, dv0 * sgg)
        emit(3, dv0 * glu_v * sgg * (1.0 - sgg))

        dn = dn_parts[0]
        x = h_ref[...]
        rr = lax.rsqrt(jnp.mean(x * x, axis=-1, keepdims=True) + EPS)
        dx, dgp = _rms_bwd(dn, x, rr, g_ref[...])
        dh1_ref[...] = dh_ref[...] + dx
        sg_ref[SG_MIX:SG_MIX + 1, :] += dgp

    rev = lambda i: (nt - 1 - i, 0)
    row = pl.BlockSpec((tm, d), rev)
    wide = pl.BlockSpec((tm, n_in), rev)
    full = lambda a: pl.BlockSpec(a.shape, lambda i, nd=a.ndim: (0,) * nd)
    halo = pl.BlockSpec((halo_rows, d), lambda i: (jnp.maximum((nt - 1 - i) * per - 1, 0), 0))
    smalls = [cw4, wg, jnp.swapaxes(wg, 2, 3), ba, bx, lam, cw31, lng, lnb]
    return _call(
        body, "mixer_bwd", (nt,),
        [row, row, full(g), wide, row, row, halo, row, row, row, _any()]
        + [full(a) for a in smalls] + [_any()],
        [row, wide, pl.BlockSpec((tm, 3 * d), rev), pl.BlockSpec((tm, 3 * d), rev),
         pl.BlockSpec((tm, 2 * d), rev), pl.BlockSpec((sg_rows, d), lambda i: (0, 0))],
        [jax.ShapeDtypeStruct((tp, d), F32), jax.ShapeDtypeStruct((tp, n_in), BF16),
         jax.ShapeDtypeStruct((tp, 3 * d), BF16), jax.ShapeDtypeStruct((tp, 3 * d), BF16),
         jax.ShapeDtypeStruct((tp, 2 * d), BF16), jax.ShapeDtypeStruct((sg_rows, d), F32)],
        [pltpu.VMEM(win_t.shape, BF16),
         pltpu.VMEM((3, d, d), BF16),
         pltpu.VMEM((tm + CONV4_HALO, d), F32),
         pltpu.VMEM((tm + CONV31_HALO, d), F32),
         pltpu.VMEM((SUBLANES, tm + CONV31_HALO, d), F32),
         pltpu.VMEM((SUBLANES, d), F32),
         pltpu.SemaphoreType.DMA((1 + 3 * N_DEV,))],
        [dh2, h, g, proj, xr_s, hs_s, hs_s, v1_s, ya_s, yb_s, win_t, *smalls, w3_all], comm)


def _tn_matmul(name, x, y, x_spec, y_spec, n_blocks, kb, nb, tm, tp, out_shape, out_spec, out_view, comm=None,
               after=None):
    nt = tp // tm

    def body(x_ref, y_ref, *refs):
        o_ref, acc = refs[-2:]
        i = pl.program_id(1)

        @pl.when(i == 0)
        def _():
            acc[...] = jnp.zeros_like(acc)

        acc[...] += _tn(x_ref[...], y_ref[...])

        @pl.when(i == nt - 1)
        def _():
            o_ref[...] = acc[...].astype(BF16).reshape(out_view)

    follows = [] if after is None else [after]
    outs, extra = _call(body, name, (n_blocks, nt), [x_spec, y_spec] + [_any()] * len(follows), [out_spec],
                        [jax.ShapeDtypeStruct(out_shape, BF16)], [pltpu.VMEM((kb, nb), F32)], [x, y] + follows,
                        comm)
    return outs[0], extra


def kernel(x, meta_tokens, ffn1_norm, ffn1_w_gu, ffn1_w_down, mix_norm, w_in, b_in, rnn_conv_w, rnn_conv_b, rg_w_a, rg_b_a, rg_w_x, rg_b_x, rg_lambda, rnn_w_proj, conv_dw_w, conv_dw_b, conv_ln_g, conv_ln_b, conv_w_proj, conv_b_proj, w_out, ffn2_norm, ffn2_w_gu, ffn2_w_down, final_norm, loss_target, m_meta_tokens, m_ffn1_norm, m_ffn1_w_gu, m_ffn1_w_down, m_mix_norm, m_w_in, m_b_in, m_rnn_conv_w, m_rnn_conv_b, m_rg_w_a, m_rg_b_a, m_rg_w_x, m_rg_b_x, m_rg_lambda, m_rnn_w_proj, m_conv_dw_w, m_conv_dw_b, m_conv_ln_g, m_conv_ln_b, m_conv_w_proj, m_conv_b_proj, m_w_out, m_ffn2_norm, m_ffn2_w_gu, m_ffn2_w_down, m_final_norm, v_meta_tokens, v_ffn1_norm, v_ffn1_w_gu, v_ffn1_w_down, v_mix_norm, v_w_in, v_b_in, v_rnn_conv_w, v_rnn_conv_b, v_rg_w_a, v_rg_b_a, v_rg_w_x, v_rg_b_x, v_rg_lambda, v_rnn_w_proj, v_conv_dw_w, v_conv_dw_b, v_conv_ln_g, v_conv_ln_b, v_conv_w_proj, v_conv_b_proj, v_w_out, v_ffn2_norm, v_ffn2_w_gu, v_ffn2_w_down, v_final_norm):
    w = dict(locals())
    seq, d = x.shape[1], x.shape[2]
    n_meta = meta_tokens.shape[0]
    t_real = n_meta + seq
    tp, tm, tmx_fwd, tmx, tmt, tmw = _tiles(t_real)
    fb = ffn1_w_gu.shape[-1]
    wr = ffn1_w_down.shape[1]
    f = N_DEV * wr
    fc = f // FFN_CHUNKS
    nbc = w_in.shape[-1]
    n_in = N_DEV * nbc
    pr = rnn_w_proj.shape[1]
    hd = rg_w_a.shape[-1]
    gr = rg_w_a.shape[2]
    cw = meta_tokens.shape[1]
    k4, k31 = rnn_conv_w.shape[1], conv_dw_w.shape[1]
    assert n_in == 6 * d and 2 * wr == fb and N_HEADS * hd == d and pr * N_DEV == d

    xi, yi, ci = lax.axis_index("x"), lax.axis_index("y"), lax.axis_index("c")
    core = ci.astype(jnp.int32).reshape(1)
    chip = (2 * xi + yi).astype(jnp.int32).reshape(1)
    me_index = (4 * xi + 2 * yi + ci).astype(jnp.int32).reshape(1)

    for nm in ("ffn1_w_gu", "ffn2_w_gu"):
        for pre in ("", "m_", "v_"):
            w[pre + nm] = jnp.swapaxes(w[pre + nm], 1, 2)

    wgut1 = w["ffn1_w_gu"][0].astype(BF16)
    wgut2 = w["ffn2_w_gu"][0].astype(BF16)
    wd1 = ffn1_w_down[0].astype(BF16)
    wd2 = ffn2_w_down[0].astype(BF16)
    win_loc = w_in[0].astype(BF16)
    win_t_loc = jnp.swapaxes(w_in[0], 0, 1).astype(BF16)
    w3_loc = jnp.concatenate([rnn_w_proj[0], conv_w_proj[0], w_out[0]], axis=0).astype(BF16)
    wg_loc = jnp.stack([rg_w_a[0], rg_w_x[0]]).astype(BF16)
    n_small = n_meta + k4 + k31
    small_rows = -(-n_small // SUBLANES) * SUBLANES
    small_loc = jnp.concatenate([meta_tokens, rnn_conv_w[0], conv_dw_w[0],
                                 jnp.zeros((small_rows - n_small, cw), F32)], axis=0)
    (wgut1_all, wd1_all, small_all), h0, tgt = _first_gather(
        [wgut1, wd1, small_loc], 2, x[0], loss_target[0], n_meta, tp)
    small_full = small_all.transpose(1, 0, 2).reshape(small_rows, d)
    cw4 = small_full[n_meta:n_meta + k4]
    cw31 = small_full[n_meta + k4:n_meta + k4 + k31]

    wgu1, wdn1 = wgut1_all.reshape(2 * f, d), wd1_all.reshape(f, d)
    (h1, gu1, n1), (win_all, w3_all, wg_all) = _ffn_fwd(
        h0, ffn1_norm, wgu1, wdn1, tm, comm=_Gather([win_loc, w3_loc, wg_loc], pass_on_at=(0.65, 0.95)))
    wg = wg_all.transpose(1, 2, 0, 3, 4).reshape(2, N_HEADS, hd, hd)
    (h2, proj, n2, xr_s, hs_s, v1_s, ya_s, yb_s), (wgut2_all, wd2_all) = _mixer_fwd(
        h1, mix_norm, b_in, win_all, cw4, rnn_conv_b, wg, rg_b_a, rg_b_x, rg_lambda, cw31, conv_dw_b, conv_ln_g,
        conv_ln_b, conv_b_proj, w3_all, tmx_fwd, comm=_Gather([wgut2, wd2], pass_on_at=(0.3, 0.5)))
    wgu2, wdn2 = wgut2_all.reshape(2 * f, d), wd2_all.reshape(f, d)
    (dh3, gu2, n3, tail), (win_t_all,) = _ffn_fwd(
        h2, ffn2_norm, wgu2, wdn2, tm, loss=(tgt, final_norm.reshape(1, d), n_meta, t_real),
        comm=_Gather([win_t_loc], pass_on_at=(0.45, 0.75)))
    win_t = win_t_all.reshape(n_in, d)

    def d_w_gu(tag, dgu, n_s, comm=None):
        g, extra = _tn_matmul(
            "d_w_gu" + tag, dgu, n_s,
            pl.BlockSpec((None, tmt, fc), lambda b, i: (b // FFN_CHUNKS, i, b % FFN_CHUNKS)),
            pl.BlockSpec((tmt, d), lambda b, i: (i, 0)),
            2 * FFN_CHUNKS, fc, d, tmt, tp, (2 * FFN_CHUNKS, fc, d),
            pl.BlockSpec((None, fc, d), lambda b, i: (b, 0, 0)), (fc, d), comm)
        return g.reshape(N_DEV, fb, d), extra

    def d_w_down(tag, act, df, comm=None):
        g, extra = _tn_matmul(
            "d_w_down" + tag, act, df,
            pl.BlockSpec((tmt, fc), lambda b, i: (i, b)), pl.BlockSpec((tmt, d), lambda b, i: (i, 0)),
            FFN_CHUNKS, fc, d, tmt, tp, (FFN_CHUNKS, fc, d),
            pl.BlockSpec((None, fc, d), lambda b, i: (b, 0, 0)), (fc, d), comm)
        return g.reshape(N_DEV, wr, d), extra

    (dh2, dgu2, act2, df2, tail), _ = _ffn_bwd(dh3, h2, gu2, ffn2_norm, wgu2, wdn2, tm, tail, TAIL_FFN2, n3)
    g_wgu2, _ = d_w_gu("2", dgu2, n3)
    g_wd2, _ = d_w_down("2", act2, df2)
    (dh1, dproj, x3, y3, yg, sg), (r_wd2, r_wgu2) = _mixer_bwd(
        dh2, h1, mix_norm, proj, xr_s, hs_s, v1_s, ya_s, yb_s, win_t, cw4, wg, rg_b_a, rg_b_x, rg_lambda, cw31,
        conv_ln_g, conv_ln_b, w3_all, tmx, comm=_Scatter([g_wd2, g_wgu2]))
    g_w3, _ = _tn_matmul(
        "d_w_proj3", x3, y3,
        pl.BlockSpec((tmw, d), lambda b, i: (i, b)), pl.BlockSpec((tmw, d), lambda b, i: (i, b)),
        3, d, d, tmw, tp, (N_DEV, 3, pr, d), pl.BlockSpec((N_DEV, None, pr, d), lambda b, i: (0, b, 0, 0)),
        (N_DEV, pr, d))
    g_wg, _ = _tn_matmul(
        "d_w_gates", xr_s, yg,
        pl.BlockSpec((tmw, hd), lambda b, i: (i, b % N_HEADS)), pl.BlockSpec((tmw, hd), lambda b, i: (i, b)),
        2 * N_HEADS, hd, hd, tmw, tp, (N_DEV, 2 * N_HEADS, gr, hd),
        pl.BlockSpec((N_DEV, None, gr, hd), lambda b, i: (0, b, 0, 0)), (N_DEV, gr, hd))
    w3_sems, g_w3_thru, w3_land, w3_token = _exchange_start("grads_proj3_exchange", _scatter_copies, N_DEV - 1, g_w3)
    g_win, (r_wg,) = _tn_matmul(
        "d_w_in", n2, dproj,
        pl.BlockSpec((tmw, d), lambda b, i: (i, 0)), pl.BlockSpec((tmw, nbc), lambda b, i: (i, b)),
        N_DEV, d, nbc, tmw, tp, (N_DEV, d, nbc), pl.BlockSpec((None, d, nbc), lambda b, i: (b, 0, 0)), (d, nbc),
        comm=_Scatter([g_wg]), after=w3_token)
    win_sems, g_win_thru, win_land, win_token = _exchange_start("grads_w_in_exchange", _scatter_copies, N_DEV - 1, g_win)
    (dh0, dgu1, act1, df1, tail), _ = _ffn_bwd(dh1, h0, gu1, ffn1_norm, wgu1, wdn1, tm, tail, TAIL_FFN1, win_token)

    pieces = [sg, dh0[:n_meta], tail]
    assert all(p.shape[0] % SUBLANES == 0 for p in pieces)
    at = [0, sg.shape[0], sg.shape[0] + n_meta]
    loss_row = at[2] + TAIL_LOSS
    rep_rows = [("ffn1_norm", at[2] + TAIL_FFN1, 1), ("mix_norm", SG_MIX, 1), ("b_in", SG_BIN, 6),
                ("rnn_conv_b", SG_CB4, 1),
                ("rg_b_a", SG_BA, 1), ("rg_b_x", SG_BX, 1), ("rg_lambda", SG_LAM, 1), ("conv_dw_b", SG_CB31, 1),
                ("conv_ln_g", SG_LNG, 1), ("conv_ln_b", SG_LNB, 1), ("conv_b_proj", SG_BCP, 1),
                ("ffn2_norm", at[2] + TAIL_FFN2, 1), ("final_norm", at[2] + TAIL_FINAL, 1)]
    col_rows = [("meta_tokens", at[1], n_meta), ("rnn_conv_w", SG_CW4, k4), ("conv_dw_w", SG_CW31, k31)]
    layout = []
    for nm, row0, nr in rep_rows:
        kind = "wide" if nm == "b_in" else "rep"
        as2d = lambda a: a.reshape(1, -1) if a.ndim == 1 else a
        layout.append((kind, row0, nr, as2d(w[nm]), as2d(w["m_" + nm]), as2d(w["v_" + nm])))
    for nm, row0, nr in col_rows:
        sq = lambda a: a.reshape(a.shape[-2], a.shape[-1])
        layout.append(("col", row0, nr, sq(w[nm]), sq(w["m_" + nm]), sq(w["v_" + nm])))
    small_partial = jnp.concatenate(pieces, axis=0)

    g_wd1, (small_partials,) = d_w_down("1", act1, df1, comm=_Bcast(small_partial))
    g_wgu1, (r_wd1,) = d_w_gu("1", dgu1, n1, comm=_Scatter([g_wd1]))

    g_last = g_wgu1.reshape((4, 2) + g_wgu1.shape[1:])
    comb_wgu1 = _pair_reduce(g_last, core)
    sems, comb_thru, land_thru, after = _exchange_start("grads_chip_exchange", _chip_copies, 3, comb_wgu1)
    g_win, r_win = _exchange_wait("grads_w_in_exchange", _scatter_copies, win_sems, g_win_thru, win_land, after)
    g_w3, r_w3 = _exchange_wait("grads_proj3_exchange", _scatter_copies, w3_sems, g_w3_thru, w3_land, after)

    grad_x = (dh0[n_meta:t_real] + after[0, 0])[None]
    total, small_out = _small_adamw(small_partials, layout, me_index, grad_x)
    after = total

    groups = [(g_wd1, r_wd1, me_index, ["ffn1_w_down"]),
              (g_wd2, r_wd2, me_index, ["ffn2_w_down"]), (g_wgu2, r_wgu2, me_index, ["ffn2_w_gu"]),
              (g_win, r_win, me_index, ["w_in"]), (g_w3, r_w3, me_index, ["w_out", "rnn_w_proj", "conv_w_proj"]),
              (g_wg, r_wg, me_index, ["rg_w_a", "rg_w_x"]), (None, None, chip, ["ffn1_w_gu"])]
    res = {}
    for own, recv, idx, group in groups:
        if own is None:
            own, recv = _exchange_wait("grads_chip_exchange", _chip_copies, sems, comb_thru, land_thru, after)
        outs = _final_adamw(own, recv, idx, [(w[nm], w["m_" + nm], w["v_" + nm]) for nm in group], after)
        after = outs[-1][0]
        for nm, o in zip(group, outs):
            res[nm] = o
    for nm in ("ffn1_w_gu", "ffn2_w_gu"):
        res[nm] = tuple(jnp.swapaxes(a, 1, 2) for a in res[nm])
    for (nm, _, _), o in zip(rep_rows + col_rows, small_out):
        res[nm] = tuple(a.reshape(w[nm].shape) for a in o)


    order = ["meta_tokens", "ffn1_norm", "ffn1_w_gu", "ffn1_w_down", "mix_norm", "w_in", "b_in", "rnn_conv_w",
             "rnn_conv_b", "rg_w_a", "rg_b_a", "rg_w_x", "rg_b_x", "rg_lambda", "rnn_w_proj", "conv_dw_w",
             "conv_dw_b", "conv_ln_g", "conv_ln_b", "conv_w_proj", "conv_b_proj", "w_out", "ffn2_norm",
             "ffn2_w_gu", "ffn2_w_down", "final_norm"]
    return (total[loss_row, 0], grad_x, *[res[nm][0] for nm in order], *[res[nm][1] for nm in order],
            *[res[nm][2] for nm in order], *[res[nm][3] for nm in order])
```

```python
import functools
import math

import jax
import jax.numpy as jnp
from jax import lax
from jax.experimental import pallas as pl
from jax.experimental.pallas import tpu as pltpu

F32 = jnp.float32
BF16 = jnp.bfloat16
MESH = pl.DeviceIdType.MESH
N_DEV = 8
N_HEADS = 4
RG_LRU_C = 8.0
EPS = 1e-6
FFN_RES = 0.5
ADAM_LR, ADAM_B1, ADAM_B2, ADAM_EPS, ADAM_WD, ADAM_STEP = 0.001, 0.9, 0.999, 1e-08, 0.01, 10
V7X_VMEM_LIMIT = 56 * 1024 * 1024
CONV4_HALO = 8
CONV31_HALO = 32
SUBLANES = 8
STAGE_ROWS = 512
TAIL_FFN1, TAIL_FINAL, TAIL_LOSS, TAIL_FFN2 = 0, 1, 2, 3
FFN_CHUNKS = 2
FFN_FWD_CHUNKS = 1
GELU_C = math.sqrt(2.0 / math.pi)
GELU_K = 0.044715


def _any():
    return pl.BlockSpec(memory_space=pl.ANY)


def _params(n_grid):
    return pltpu.CompilerParams(dimension_semantics=("arbitrary",) * n_grid, vmem_limit_bytes=V7X_VMEM_LIMIT)


def _nn(a, b):
    return jnp.dot(a, b, preferred_element_type=F32)


def _nt(a, b):
    return lax.dot_general(a, b, (((1,), (1,)), ((), ())), preferred_element_type=F32)


def _tn(a, b):
    return lax.dot_general(a, b, (((0,), (0,)), ((), ())), preferred_element_type=F32)


def _sigmoid(x):
    return 0.5 * jnp.tanh(0.5 * x) + 0.5


def _rowsum(x):
    return jnp.sum(x, axis=0, keepdims=True)


def _rms_fwd(x, g):
    r = lax.rsqrt(jnp.mean(x * x, axis=-1, keepdims=True) + EPS)
    return x * r * g, r


def _rms_bwd(dn, x, r, g):
    xr = x * r
    gy = dn * g
    dx = r * (gy - xr * jnp.mean(gy * xr, axis=-1, keepdims=True))
    return dx, _rowsum(dn * xr)


def _gelu(y):
    t = jnp.tanh(GELU_C * (y + GELU_K * y * y * y))
    return 0.5 * y * (1.0 + t), t


def _gelu_grad(y, t):
    return 0.5 * (1.0 + t) + 0.5 * y * (1.0 - t * t) * GELU_C * (1.0 + 3.0 * GELU_K * y * y)


def _softplus(x):
    return jnp.maximum(x, 0.0) + jnp.log(1.0 + jnp.exp(-jnp.abs(x)))


def _one_minus_exp(z):
    series = -z * (1.0 + 0.5 * z * (1.0 + z * (1.0 / 3.0) * (1.0 + 0.25 * z)))
    return jnp.where(z > -0.05, series, 1.0 - jnp.exp(z))


def _tiles(t_real):
    if t_real > 2048:
        tm = 384
        tp = -(-t_real // tm) * tm
        return tp, tm, tm // 2, tm // 2, tp, tp
    tm = 128
    tp = -(-t_real // tm) * tm
    return tp, tm, tm // 2, tm // 2, tm, tm


def _load_weights(copies, sems):
    cps = [pltpu.make_async_copy(s, d, sems.at[k]) for k, (s, d) in enumerate(copies)]
    for cp in cps:
        cp.start()
    for cp in cps:
        cp.wait()


def _position():
    x, y, c = lax.axis_index("x"), lax.axis_index("y"), lax.axis_index("c")
    chips = [(1 - x, y), (x, 1 - y), (1 - x, 1 - y)]
    return x, y, c, chips


def _slot(p):
    return 4 * p[0] + 2 * p[1] + p[2]


class _Lazy(dict):
    def __getitem__(self, key):
        val = dict.__getitem__(self, key)
        return val() if callable(val) else val


class _Gather:
    def __init__(self, shards, pass_on_at=None):
        self.shards = list(shards)
        self.n = len(self.shards)
        self.pass_on_at = pass_on_at

    def inputs(self):
        return self.shards

    def out_shape(self):
        return [jax.ShapeDtypeStruct((N_DEV,) + s.shape, s.dtype) for s in self.shards]

    N_SEMS = 9

    def scratch(self):
        return [pltpu.SemaphoreType.DMA((self.N_SEMS * self.n,)), pltpu.SemaphoreType.DMA((self.N_SEMS * self.n,)),
                pltpu.SemaphoreType.DMA((self.n,))]

    def _plan(self, ins, outs, sems):
        send_sems, recv_sems, local_sems = sems
        x, y, c, _ = _position()
        me, sib, xn, yn, dg = (x, y, c), (x, y, 1 - c), (1 - x, y, c), (x, 1 - y, c), (1 - x, 1 - y, c)
        other = lambda p: (p[0], p[1], 1 - c)

        def blk(a, p, half=None):
            ref = outs[a].at[_slot(p)]
            if half is None:
                return ref
            rows = self.shards[a].shape[0] // 2
            return ref.at[pl.ds(half * rows, rows)]

        def copy(a, k, dst, to, src=None):
            return pltpu.make_async_remote_copy(
                src_ref=dst if src is None else src, dst_ref=dst,
                send_sem=send_sems.at[self.N_SEMS * a + k], recv_sem=recv_sems.at[self.N_SEMS * a + k],
                device_id=to, device_id_type=MESH)

        cp = _Lazy(mine=lambda: [pltpu.make_async_copy(ins[a], blk(a, me), local_sems.at[a]) for a in range(self.n)])
        for a in range(self.n):
            cp[a] = _Lazy(
                own=lambda a=a: [copy(a, 0, blk(a, me), sib, src=ins[a]), copy(a, 1, blk(a, me), xn, src=ins[a]),
                                 copy(a, 2, blk(a, me), yn, src=ins[a])],
                from_x=lambda a=a: copy(a, 1, blk(a, xn), me), from_y=lambda a=a: copy(a, 2, blk(a, yn), me),
                relay_x=lambda a=a: copy(a, 3, blk(a, xn, 0), yn), relay_y=lambda a=a: copy(a, 4, blk(a, yn, 1), xn),
                diag0=lambda a=a: copy(a, 3, blk(a, dg, 0), me), diag1=lambda a=a: copy(a, 4, blk(a, dg, 1), me),
                pass_x=lambda a=a: copy(a, 5, blk(a, xn), sib), pass_y=lambda a=a: copy(a, 6, blk(a, yn), sib),
                pass_d0=lambda a=a: copy(a, 7, blk(a, dg, 0), sib), pass_d1=lambda a=a: copy(a, 8, blk(a, dg, 1), sib),
                from_sib=lambda a=a: [copy(a, 0, blk(a, sib), me), copy(a, 5, blk(a, other(xn)), me),
                                      copy(a, 6, blk(a, other(yn)), me), copy(a, 7, blk(a, other(dg), 0), me),
                                      copy(a, 8, blk(a, other(dg), 1), me)])
        return cp

    def start(self, ins, outs, sems):
        cp = self._plan(ins, outs, sems)
        for c in cp["mine"]:
            c.start()
        for a in range(self.n):
            for c in cp[a]["own"]:
                c.start()

    def pass_on(self, ins, outs, sems):
        cp = self._plan(ins, outs, sems)
        for a in range(self.n):
            cp[a]["from_x"].wait_recv()
            cp[a]["relay_x"].start()
            cp[a]["pass_x"].start()
        for a in range(self.n):
            cp[a]["from_y"].wait_recv()
            cp[a]["relay_y"].start()
            cp[a]["pass_y"].start()

    def pass_on_relayed(self, ins, outs, sems):
        cp = self._plan(ins, outs, sems)
        for a in range(self.n):
            cp[a]["diag0"].wait_recv()
            cp[a]["pass_d0"].start()
            cp[a]["diag1"].wait_recv()
            cp[a]["pass_d1"].start()

    def finish(self, ins, outs, sems):
        if self.pass_on_at is None:
            self.pass_on(ins, outs, sems)
            self.pass_on_relayed(ins, outs, sems)
        cp = self._plan(ins, outs, sems)
        for a in range(self.n):
            for c in cp[a]["from_sib"]:
                c.wait_recv()
            for c in cp[a]["own"] + [cp[a][k] for k in ("relay_x", "relay_y", "pass_x", "pass_y", "pass_d0", "pass_d1")]:
                c.wait_send()
        for c in cp["mine"]:
            c.wait()


class _Scatter:
    def __init__(self, grads):
        self.grads = list(grads)
        self.n = len(self.grads)

    def inputs(self):
        return self.grads

    def out_shape(self):
        return [jax.ShapeDtypeStruct((N_DEV - 1,) + g.shape[1:], g.dtype) for g in self.grads]

    def scratch(self):
        return [pltpu.SemaphoreType.DMA((7 * self.n,)), pltpu.SemaphoreType.DMA((7 * self.n,))]

    def _plan(self, ins, outs, sems):
        send_sems, recv_sems = sems
        x, y, c, _ = _position()
        cps = []
        for a in range(self.n):
            for k in range(1, N_DEV):
                peer = (x ^ (k >> 2), y ^ ((k >> 1) & 1), c ^ (k & 1))
                cps.append(pltpu.make_async_remote_copy(
                    src_ref=ins[a].at[_slot(peer)], dst_ref=outs[a].at[k - 1],
                    send_sem=send_sems.at[7 * a + k - 1], recv_sem=recv_sems.at[7 * a + k - 1],
                    device_id=peer, device_id_type=MESH))
        return cps

    def start(self, ins, outs, sems):
        for cp in self._plan(ins, outs, sems):
            cp.start()

    def finish(self, ins, outs, sems):
        for cp in self._plan(ins, outs, sems):
            cp.wait()


def _hosted(inner, n_in, n_out, comm, grid):
    if comm is None:
        return inner
    nc_in, nc_out, ns = len(comm.inputs()), len(comm.out_shape()), len(comm.scratch())

    def body(*refs):
        o0 = n_in + nc_in
        s0 = o0 + n_out + nc_out
        main = refs[:n_in] + refs[o0:o0 + n_out] + refs[s0:len(refs) - ns]
        c_in, c_out, c_sems = refs[n_in:o0], refs[o0 + n_out:s0], refs[len(refs) - ns:]
        ids = [pl.program_id(ax) for ax in range(len(grid))]
        first = functools.reduce(jnp.logical_and, [i == 0 for i in ids])
        last = functools.reduce(jnp.logical_and, [i == g - 1 for i, g in zip(ids, grid)])

        @pl.when(first)
        def _():
            comm.start(c_in, c_out, c_sems)

        inner(*main)

        if getattr(comm, "pass_on_at", None) is not None:
            assert len(grid) == 1
            first_at, second_at = (min(grid[0] - 1, int(frac * grid[0])) for frac in comm.pass_on_at)
            assert first_at < second_at

            @pl.when(ids[0] == first_at)
            def _():
                comm.pass_on(c_in, c_out, c_sems)

            @pl.when(ids[0] == second_at)
            def _():
                comm.pass_on_relayed(c_in, c_out, c_sems)

        @pl.when(last)
        def _():
            comm.finish(c_in, c_out, c_sems)

    return body


def _call(inner, name, grid, in_specs, out_specs, out_shape, scratch, args, comm=None):
    n_in, n_out = len(args), len(out_shape)
    body = _hosted(inner, n_in, n_out, comm, grid)
    if comm is not None:
        in_specs = list(in_specs) + [_any()] * len(comm.inputs())
        args = list(args) + comm.inputs()
        out_specs = list(out_specs) + [_any()] * len(comm.out_shape())
        out_shape = list(out_shape) + comm.out_shape()
        scratch = list(scratch) + comm.scratch()
    outs = pl.pallas_call(
        body, name=name, grid=grid, in_specs=list(in_specs), out_specs=list(out_specs), out_shape=list(out_shape),
        scratch_shapes=list(scratch), compiler_params=_params(len(grid)))(*args)
    return list(outs[:n_out]), list(outs[n_out:])


class _Bcast:
    def __init__(self, block):
        self.block = block

    def inputs(self):
        return [self.block]

    def out_shape(self):
        return [jax.ShapeDtypeStruct((N_DEV,) + self.block.shape, self.block.dtype)]

    def scratch(self):
        return [pltpu.SemaphoreType.DMA((N_DEV - 1,)), pltpu.SemaphoreType.DMA((N_DEV - 1,)),
                pltpu.SemaphoreType.DMA((1,))]

    def _plan(self, ins, outs, sems):
        send_sems, recv_sems, local_sem = sems
        x, y, c, _ = _position()
        mine = outs[0].at[_slot((x, y, c))]
        cps = []
        for k in range(1, N_DEV):
            peer = (x ^ (k >> 2), y ^ ((k >> 1) & 1), c ^ (k & 1))
            cps.append(pltpu.make_async_remote_copy(
                src_ref=ins[0], dst_ref=mine, send_sem=send_sems.at[k - 1], recv_sem=recv_sems.at[k - 1],
                device_id=peer, device_id_type=MESH))
        return pltpu.make_async_copy(ins[0], mine, local_sem.at[0]), cps

    def start(self, ins, outs, sems):
        own, cps = self._plan(ins, outs, sems)
        own.start()
        for cp in cps:
            cp.start()

    def finish(self, ins, outs, sems):
        own, cps = self._plan(ins, outs, sems)
        for cp in cps:
            cp.wait()
        own.wait()


def _first_gather(shards, small_idx, x2, t2, n_meta, tp):
    comm = _Gather(shards)
    n = comm.n
    seq, d = x2.shape
    t_real = n_meta + seq
    n_pad = tp - t_real
    cw = d // N_DEV
    rows = STAGE_ROWS if seq % STAGE_ROWS == 0 else seq
    n_chunks = seq // rows

    def body(*refs):
        ins, (x_ref, t_ref) = refs[:n], refs[n:n + 2]
        outs, (h0_ref, tg_ref) = refs[n + 2:2 * n + 2], refs[2 * n + 2:2 * n + 4]
        sems = refs[2 * n + 4:2 * n + 7]
        buf, zeros, in_sems, out_sems, misc_sems = refs[2 * n + 7:]
        comm.start(ins, outs, sems)
        zeros[...] = jnp.zeros_like(zeros)
        fills = [pltpu.make_async_copy(zeros.at[pl.ds(0, n_pad)], h0_ref.at[pl.ds(t_real, n_pad)], misc_sems.at[0]),
                 pltpu.make_async_copy(zeros.at[pl.ds(0, n_pad)], tg_ref.at[pl.ds(t_real, n_pad)], misc_sems.at[1]),
                 pltpu.make_async_copy(zeros.at[pl.ds(0, n_meta)], tg_ref.at[pl.ds(0, n_meta)], misc_sems.at[2])]
        for cp in fills:
            cp.start()
        jobs = [(src, dst, c) for src, dst in ((x_ref, h0_ref), (t_ref, tg_ref)) for c in range(n_chunks)]

        def load(k):
            src, _, c = jobs[k]
            return pltpu.make_async_copy(src.at[pl.ds(c * rows, rows)], buf.at[k % 2], in_sems.at[k % 2])

        def store(k):
            _, dst, c = jobs[k]
            return pltpu.make_async_copy(buf.at[k % 2], dst.at[pl.ds(n_meta + c * rows, rows)], out_sems.at[k % 2])

        load(0).start()
        for k in range(len(jobs)):
            load(k).wait()
            if k + 1 < len(jobs):
                if k >= 1:
                    store(k - 1).wait()
                load(k + 1).start()
            store(k).start()
        for k in range(max(0, len(jobs) - 2), len(jobs)):
            store(k).wait()
        comm.finish(ins, outs, sems)
        meta = [pltpu.make_async_copy(outs[small_idx].at[k, pl.ds(0, n_meta)],
                                      h0_ref.at[pl.ds(0, n_meta), pl.ds(k * cw, cw)], misc_sems.at[3 + k])
                for k in range(N_DEV)]
        for cp in meta:
            cp.start()
        for cp in fills + meta:
            cp.wait()

    staged = [jax.ShapeDtypeStruct((tp, d), F32)] * 2
    outs = pl.pallas_call(
        body, name="weights_all_gather", out_shape=comm.out_shape() + staged,
        in_specs=[_any()] * (n + 2), out_specs=[_any()] * (n + 2),
        scratch_shapes=comm.scratch() + [
            pltpu.VMEM((2, rows, d), F32), pltpu.VMEM((max(n_pad, n_meta), d), F32),
            pltpu.SemaphoreType.DMA((2,)), pltpu.SemaphoreType.DMA((2,)), pltpu.SemaphoreType.DMA((3 + N_DEV,))],
        compiler_params=pltpu.CompilerParams(vmem_limit_bytes=V7X_VMEM_LIMIT),
    )(*shards, x2, t2)
    return outs[:n], outs[n], outs[n + 1]


def _chip_copies(c_ref, land_ref, sems):
    _, _, c, chips = _position()
    return [pltpu.make_async_remote_copy(
        src_ref=c_ref.at[2 * cx + cy], dst_ref=land_ref.at[j], send_sem=sems[j], recv_sem=sems[3 + j],
        device_id=(cx, cy, c), device_id_type=MESH) for j, (cx, cy) in enumerate(chips)]


def _scatter_copies(g_ref, land_ref, sems):
    x, y, c, _ = _position()
    cps = []
    for k in range(1, N_DEV):
        peer = (x ^ (k >> 2), y ^ ((k >> 1) & 1), c ^ (k & 1))
        cps.append(pltpu.make_async_remote_copy(
            src_ref=g_ref.at[_slot(peer)], dst_ref=land_ref.at[k - 1], send_sem=sems[k - 1],
            recv_sem=sems[N_DEV - 1 + k - 1], device_id=peer, device_id_type=MESH))
    return cps


def _exchange_start(name, copies, n_copies, src):
    hbm = pl.BlockSpec(memory_space=pltpu.HBM)
    sem = pl.BlockSpec(memory_space=pltpu.SEMAPHORE)
    n_sems = 2 * n_copies

    def body(s_ref, land_ref, *refs):
        for cp in copies(s_ref, land_ref, refs[:n_sems]):
            cp.start()
        token = refs[n_sems + 2]
        token[...] = jnp.zeros_like(token)

    land = lax.empty((n_copies,) + src.shape[1:], src.dtype)
    outs = pl.pallas_call(
        body, name=name + "_start",
        out_shape=(pltpu.SemaphoreType.DMA(()),) * n_sems
        + (pltpu.HBM(src.shape, src.dtype), pltpu.HBM(land.shape, land.dtype),
           jax.ShapeDtypeStruct((SUBLANES, 128), F32)),
        in_specs=(hbm, hbm), out_specs=(sem,) * n_sems + (hbm, hbm, pl.BlockSpec(memory_space=pltpu.VMEM)),
        input_output_aliases={0: n_sems, 1: n_sems + 1},
        compiler_params=pltpu.CompilerParams(has_side_effects=pltpu.SideEffectType.DATAFLOW_SIDE_EFFECTING),
    )(pltpu.with_memory_space_constraint(src, pltpu.HBM), pltpu.with_memory_space_constraint(land, pltpu.HBM))
    return outs[:n_sems], outs[n_sems], outs[n_sems + 1], outs[n_sems + 2]


def _exchange_wait(name, copies, sems, src_thru, land_thru, after):
    hbm = pl.BlockSpec(memory_space=pltpu.HBM)
    sem = pl.BlockSpec(memory_space=pltpu.SEMAPHORE)
    n_sems = len(sems)

    def body(s_ref, land_ref, *refs):
        for cp in copies(s_ref, land_ref, refs[:n_sems]):
            cp.wait_send()
            cp.wait_recv()

    return pl.pallas_call(
        body, name=name + "_wait",
        out_shape=(pltpu.HBM(src_thru.shape, src_thru.dtype), pltpu.HBM(land_thru.shape, land_thru.dtype)),
        in_specs=(hbm, hbm) + (sem,) * n_sems + (pl.BlockSpec(memory_space=pl.ANY),), out_specs=(hbm, hbm),
        input_output_aliases={0: 0, 1: 1},
        compiler_params=pltpu.CompilerParams(has_side_effects=pltpu.SideEffectType.DATAFLOW_SIDE_EFFECTING),
    )(src_thru, land_thru, *sems, after)


def _pair_reduce(grad, core):
    blk = grad.shape[2:]
    zeros = (0,) * len(blk)

    def body(core_ref, g_hbm, own_ref, o_ref, landed, send_sems, recv_sems):
        del core_ref
        i = pl.program_id(0)
        x, y, c, _ = _position()

        def copy(k):
            return pltpu.make_async_remote_copy(
                src_ref=g_hbm.at[k, 1 - c], dst_ref=landed.at[k], send_sem=send_sems.at[k],
                recv_sem=recv_sems.at[k], device_id=(x, y, 1 - c), device_id_type=MESH)

        @pl.when(i == 0)
        def _():
            for k in range(4):
                copy(k).start()

        for k in range(4):
            @pl.when(i == k)
            def _(k=k):
                copy(k).wait_recv()

        o_ref[...] = (own_ref[...].astype(F32) + landed[i].astype(F32)).astype(BF16)

        @pl.when(i == 3)
        def _():
            for k in range(4):
                copy(k).wait_send()

    return pl.pallas_call(
        body, name="grads_pair_reduce",
        out_shape=jax.ShapeDtypeStruct((4,) + blk, BF16),
        grid_spec=pltpu.PrefetchScalarGridSpec(
            num_scalar_prefetch=1, grid=(4,),
            in_specs=[_any(), pl.BlockSpec((None, None) + blk, lambda i, cr: (i, cr[0]) + zeros)],
            out_specs=pl.BlockSpec((None,) + blk, lambda i, cr: (i,) + zeros),
            scratch_shapes=[pltpu.VMEM((4,) + blk, BF16), pltpu.SemaphoreType.DMA((4,)),
                            pltpu.SemaphoreType.DMA((4,))]),
        compiler_params=_params(1),
    )(core, grad, grad)


def _adamw(w, g, m, v):
    m2 = ADAM_B1 * m + (1.0 - ADAM_B1) * g
    v2 = ADAM_B2 * v + (1.0 - ADAM_B2) * (g * g)
    m_hat = m2 / (1.0 - ADAM_B1 ** ADAM_STEP)
    v_hat = v2 / (1.0 - ADAM_B2 ** ADAM_STEP)
    delta = -ADAM_LR * (m_hat / (jnp.sqrt(v_hat) + ADAM_EPS) + ADAM_WD * w)
    return delta, m2, v2


def _final_adamw(own, recv, idx, parts, after):
    blk = own.shape[1:]
    n_recv = recv.shape[0]
    n_parts = len(parts)
    per = blk[0] // n_parts if n_parts > 1 else None
    rows = blk[-2]
    n_chunks = 1 if n_parts > 1 else (4 if rows % 64 == 0 and rows >= 512 else (2 if rows % 32 == 0 else 1))
    cblk = blk[:-2] + (rows // n_chunks, blk[-1])
    lead = (0,) * (len(blk) - 2)

    def body(idx_ref, c_ref, r_ref, after_ref, *refs):
        del idx_ref, after_ref
        ins, outs = refs[:3 * n_parts], refs[3 * n_parts:]
        g = c_ref[...].astype(F32)
        for k in range(n_recv):
            g = g + r_ref[k].astype(F32)
        for p in range(n_parts):
            w_ref, m_ref, v_ref = ins[3 * p:3 * p + 3]
            if n_parts == 1:
                gp = g
            elif per == 1:
                gp = g[p]
            else:
                gp = g[p * per:(p + 1) * per]
            delta, m2, v2 = _adamw(w_ref[0], gp, m_ref[0], v_ref[0])
            o = outs[4 * p:4 * p + 4]
            o[0][0] = gp
            o[1][0] = delta
            o[2][0] = m2
            o[3][0] = v2

    flat = [a for wmv in parts for a in wmv]

    def part_spec(a):
        shape = a.shape[:-2] + (a.shape[-2] // n_chunks, a.shape[-1])
        return pl.BlockSpec(shape, lambda i, cr, nd=a.ndim: (0,) * (nd - 2) + (i, 0))

    outs = pl.pallas_call(
        body, name="grads_sum_adamw",
        out_shape=[jax.ShapeDtypeStruct(wmv[0].shape, F32) for wmv in parts for _ in range(4)],
        grid_spec=pltpu.PrefetchScalarGridSpec(
            num_scalar_prefetch=1, grid=(n_chunks,),
            in_specs=[pl.BlockSpec((None,) + cblk, lambda i, cr: (cr[0],) + lead + (i, 0)),
                      pl.BlockSpec((n_recv,) + cblk, lambda i, cr: (0,) + lead + (i, 0))]
                     + [_any()] + [part_spec(a) for a in flat],
            out_specs=[part_spec(wmv[0]) for wmv in parts for _ in range(4)]),
        compiler_params=_params(1),
    )(idx, own, recv, after, *flat)
    return [tuple(outs[4 * p:4 * p + 4]) for p in range(n_parts)]


def _small_adamw(partials, layout, me_index, after):
    _, rows, d = partials.shape
    n = len(layout)
    cw = d // N_DEV

    def body(me_ref, p_ref, after_ref, *refs):
        ins, t_ref, outs = refs[:3 * n], refs[3 * n], refs[3 * n + 1:]
        me = me_ref[0]
        total = p_ref[0]
        for j in range(1, N_DEV):
            total = total + p_ref[j]
        t_ref[...] = total
        for e, (kind, r0, nr, _, _, _) in enumerate(layout):
            w_ref, m_ref, v_ref = ins[3 * e:3 * e + 3]
            o = outs[4 * e:4 * e + 4]
            if kind == "rep":
                g = t_ref[r0:r0 + nr, :]
                delta, m2, v2 = _adamw(w_ref[...], g, m_ref[...], v_ref[...])
                for ref, val in zip(o, (g, delta, m2, v2)):
                    ref[...] = val
            elif kind == "wide":
                for q in range(nr):
                    sl = slice(q * d, (q + 1) * d)
                    g = t_ref[r0 + q:r0 + q + 1, :]
                    delta, m2, v2 = _adamw(w_ref[:, sl], g, m_ref[:, sl], v_ref[:, sl])
                    for ref, val in zip(o, (g, delta, m2, v2)):
                        ref[:, sl] = val
            else:
                for j in range(N_DEV):
                    @pl.when(me == j)
                    def _(j=j, o=o, w_ref=w_ref, m_ref=m_ref, v_ref=v_ref, r0=r0, nr=nr):
                        g = t_ref[r0:r0 + nr, j * cw:(j + 1) * cw]
                        delta, m2, v2 = _adamw(w_ref[...], g, m_ref[...], v_ref[...])
                        for ref, val in zip(o, (g, delta, m2, v2)):
                            ref[...] = val

    flat = [a for ent in layout for a in ent[3:]]
    vm = pl.BlockSpec(memory_space=pltpu.VMEM)
    outs = pl.pallas_call(
        body, name="small_adamw",
        out_shape=[jax.ShapeDtypeStruct((rows, d), F32)]
                  + [jax.ShapeDtypeStruct(ent[3].shape, F32) for ent in layout for _ in range(4)],
        in_specs=[pl.BlockSpec(memory_space=pltpu.SMEM), vm, _any()] + [vm] * len(flat),
        out_specs=[vm] * (1 + 4 * n),
        compiler_params=pltpu.CompilerParams(vmem_limit_bytes=V7X_VMEM_LIMIT),
    )(me_index, partials, after, *flat)
    return outs[0], [tuple(outs[1 + 4 * e:5 + 4 * e]) for e in range(n)]


def _ffn_fwd(h, g, wgu, wd, tm, loss=None, comm=None):
    tp, d = h.shape
    f = wd.shape[0]
    fc = f // FFN_FWD_CHUNKS
    nt = tp // tm
    with_loss = loss is not None
    if with_loss:
        tgt, gf, n_meta, t_real = loss

    def body(*refs):
        if with_loss:
            (h_ref, g_ref, wgu_hbm, wd_hbm, tgt_ref, gf_ref, out_ref, gu_ref, n_ref, tail_ref,
             wgu_v, wd_v, sems) = refs
        else:
            h_ref, g_ref, wgu_hbm, wd_hbm, out_ref, gu_ref, n_ref, wgu_v, wd_v, sems = refs
        i = pl.program_id(0)

        @pl.when(i == 0)
        def _():
            _load_weights([(wgu_hbm, wgu_v), (wd_hbm, wd_v)], sems)
            if with_loss:
                tail_ref[...] = jnp.zeros_like(tail_ref)

        x = h_ref[...]
        n, _ = _rms_fwd(x, g_ref[...])
        nb = n.astype(BF16)
        n_ref[...] = nb
        acc = jnp.zeros((tm, d), F32)
        for j in range(FFN_FWD_CHUNKS):
            cols = slice(j * fc, (j + 1) * fc)
            gate = _nt(nb, wgu_v[pl.ds(j * fc, fc), :])
            up = _nt(nb, wgu_v[pl.ds(f + j * fc, fc), :])
            gu_ref[0, :, cols] = gate.astype(BF16)
            gu_ref[1, :, cols] = up.astype(BF16)
            act = (gate * _sigmoid(gate) * up).astype(BF16)
            acc = acc + _nn(act, wd_v[pl.ds(j * fc, fc), :])
        hn = x + FFN_RES * acc
        if not with_loss:
            out_ref[...] = hn
        else:
            gfv = gf_ref[...]
            r = lax.rsqrt(jnp.mean(hn * hn, axis=-1, keepdims=True) + EPS)
            xr = hn * r
            rows = i * tm + lax.broadcasted_iota(jnp.int32, (tm, 1), 0)
            mask = jnp.logical_and(rows >= n_meta, rows < t_real)
            diff = jnp.where(mask, xr * gfv - tgt_ref[...], 0.0)
            tail_ref[TAIL_LOSS:TAIL_LOSS + 1, :] += jnp.zeros((1, d), F32) + 0.5 * jnp.sum(diff * diff) / d
            dy = diff / d
            gy = dy * gfv
            out_ref[...] = r * (gy - xr * jnp.mean(gy * xr, axis=-1, keepdims=True))
            tail_ref[TAIL_FINAL:TAIL_FINAL + 1, :] += _rowsum(dy * xr)

    row = pl.BlockSpec((tm, d), lambda i: (i, 0))
    vec = pl.BlockSpec((1, d), lambda i: (0, 0))
    in_specs = [row, vec, _any(), _any()]
    out_shape = [jax.ShapeDtypeStruct((tp, d), F32), jax.ShapeDtypeStruct((2, tp, f), BF16),
                 jax.ShapeDtypeStruct((tp, d), BF16)]
    out_specs = [row, pl.BlockSpec((2, tm, f), lambda i: (0, i, 0)), row]
    args = [h, g, wgu, wd]
    if with_loss:
        in_specs += [row, vec]
        out_shape += [jax.ShapeDtypeStruct((SUBLANES, d), F32)]
        out_specs += [pl.BlockSpec((SUBLANES, d), lambda i: (0, 0))]
        args += [tgt, gf]
    return _call(body, "ffn_fwd_loss" if with_loss else "ffn_fwd", (nt,), in_specs, out_specs, out_shape,
                 [pltpu.VMEM((2 * f, d), BF16), pltpu.VMEM((f, d), BF16), pltpu.SemaphoreType.DMA((2,))],
                 args, comm)


def _ffn_bwd(dh, h, gu, g, wgu, wd, tm, tail, tail_row, after):
    tp, d = h.shape
    f = wd.shape[0]
    fc = f // FFN_CHUNKS
    nt = tp // tm

    def body(dh_ref, h_ref, gu_ref, g_ref, tail_ref, wgu_hbm, wd_hbm, after_ref,
             dhin_ref, dgu_ref, act_ref, df_ref, dg_ref, wgu_v, wd_v, dn_v, sems):
        del after_ref
        i, j = pl.program_id(0), pl.program_id(1)

        @pl.when(jnp.logical_and(i == 0, j == 0))
        def _():
            _load_weights([(wgu_hbm, wgu_v), (wd_hbm, wd_v)], sems)
            dg_ref[...] = tail_ref[...]

        dfb = (FFN_RES * dh_ref[...]).astype(BF16)

        @pl.when(j == 0)
        def _():
            df_ref[...] = dfb
            dn_v[...] = jnp.zeros_like(dn_v)

        lo = pl.multiple_of(j * fc, 16)
        dact = _nt(dfb, wd_v[pl.ds(lo, fc), :])
        gate = gu_ref[0].astype(F32)
        up = gu_ref[1].astype(F32)
        sg = _sigmoid(gate)
        silu = gate * sg
        act_ref[...] = (silu * up).astype(BF16)
        dgate = (dact * up * (sg * (1.0 + gate * (1.0 - sg)))).astype(BF16)
        dup = (dact * silu).astype(BF16)
        dgu_ref[0] = dgate
        dgu_ref[1] = dup
        dn_v[...] += _nn(dgate, wgu_v[pl.ds(lo, fc), :]) + _nn(dup, wgu_v[pl.ds(pl.multiple_of(f + j * fc, 16), fc), :])

        @pl.when(j == FFN_CHUNKS - 1)
        def _():
            x = h_ref[...]
            r = lax.rsqrt(jnp.mean(x * x, axis=-1, keepdims=True) + EPS)
            dx, dgp = _rms_bwd(dn_v[...], x, r, g_ref[...])
            dhin_ref[...] = dh_ref[...] + dx
            dg_ref[tail_row:tail_row + 1, :] += dgp

    row = pl.BlockSpec((tm, d), lambda i, j: (i, 0))
    vec = pl.BlockSpec((1, d), lambda i, j: (0, 0))
    tile = pl.BlockSpec((SUBLANES, d), lambda i, j: (0, 0))
    hid2 = pl.BlockSpec((2, tm, fc), lambda i, j: (0, i, j))
    return _call(
        body, "ffn_bwd", (nt, FFN_CHUNKS),
        [row, row, hid2, vec, tile, _any(), _any(), _any()],
        [row, hid2, pl.BlockSpec((tm, fc), lambda i, j: (i, j)), row, tile],
        [jax.ShapeDtypeStruct((tp, d), F32), jax.ShapeDtypeStruct((2, tp, f), BF16),
         jax.ShapeDtypeStruct((tp, f), BF16), jax.ShapeDtypeStruct((tp, d), BF16),
         jax.ShapeDtypeStruct((SUBLANES, d), F32)],
        [pltpu.VMEM((2 * f, d), BF16), pltpu.VMEM((f, d), BF16), pltpu.VMEM((tm, d), F32),
         pltpu.SemaphoreType.DMA((2,))],
        [dh, h, gu, g, tail, wgu, wd, after])


def _piece_segments(q, d, nb_cols):
    segs = []
    for j in range(N_DEV):
        lo, hi = max(q * d, j * nb_cols), min((q + 1) * d, (j + 1) * nb_cols)
        if lo < hi:
            segs.append((j, lo - q * d, hi - q * d, lo - j * nb_cols, hi - j * nb_cols))
    return segs


def _w3_copies(w3_hbm, rows, w3_v):
    return [(w3_hbm.at[k, pl.ds(q * rows, rows)], w3_v.at[q, pl.ds(k * rows, rows)])
            for q in range(3) for k in range(N_DEV)]


def _gates(xrb, wg_ref, ba, bx, lam, hd):
    pre_r, pre_i = [], []
    for hh in range(N_HEADS):
        xh = xrb[:, hh * hd:(hh + 1) * hd]
        pre_r.append(_nn(xh, wg_ref[0, hh]))
        pre_i.append(_nn(xh, wg_ref[1, hh]))
    r = _sigmoid(jnp.concatenate(pre_r, axis=1) + ba)
    ig = _sigmoid(jnp.concatenate(pre_i, axis=1) + bx)
    sp = _softplus(-lam)
    log_a = -RG_LRU_C * r * sp
    a = jnp.exp(log_a)
    s = jnp.sqrt(_one_minus_exp(2.0 * log_a))
    return r, ig, sp, a, s


def _scan_fwd(a, u, h_prev):
    tm = a.shape[0]
    rows = lax.broadcasted_iota(jnp.int32, a.shape, 0)
    d = 1
    while d < tm:
        if d < SUBLANES:
            keep = rows >= d
            u = jnp.where(keep, a * pltpu.roll(u, d, 0) + u, u)
            a = jnp.where(keep, a * pltpu.roll(a, d, 0), a)
        else:
            u = jnp.concatenate([u[:d], a[d:] * u[:tm - d] + u[d:]], axis=0)
            a = jnp.concatenate([a[:d], a[d:] * a[:tm - d]], axis=0)
        d *= 2
    return u + a * h_prev


def _scan_bwd(b, v, g_next):
    tm = b.shape[0]
    rows = lax.broadcasted_iota(jnp.int32, b.shape, 0)
    d = 1
    while d < tm:
        if d < SUBLANES:
            keep = rows < tm - d
            v = jnp.where(keep, v + b * pltpu.roll(v, tm - d, 0), v)
            b = jnp.where(keep, b * pltpu.roll(b, tm - d, 0), b)
        else:
            v = jnp.concatenate([v[:tm - d] + b[:tm - d] * v[d:], v[tm - d:]], axis=0)
            b = jnp.concatenate([b[:tm - d] * b[d:], b[tm - d:]], axis=0)
        d *= 2
    return v + b * g_next


def _shifted_copies(ext_ref, es_ref, n_rows):
    for s in range(1, SUBLANES):
        es_ref[s, pl.ds(0, n_rows), :] = ext_ref[pl.ds(s, n_rows), :]


def _tap(ext_ref, es_ref, off, tm):
    q, s = divmod(off, SUBLANES)
    if s == 0:
        return ext_ref[pl.ds(SUBLANES * q, tm), :]
    return es_ref[s, pl.ds(SUBLANES * q, tm), :]


def _mixer_fwd(h, g, b_in, win_all, cw4, cb4, wg, ba, bx, lam, cw31, cb31, lng, lnb, bcp, w3_all, tm, comm=None):
    tp, d = h.shape
    nb_cols = win_all.shape[-1]
    n_in = N_DEV * nb_cols
    hd = wg.shape[-1]
    k4, k31 = cw4.shape[0], cw31.shape[0]
    w3_rows = d // N_DEV

    def body(h_ref, g_ref, b_ref, win_hbm, cw4_ref, cb4_ref, wg_ref, ba_ref, bx_ref, lam_ref, cw31_ref, cb31_ref,
             lng_ref, lnb_ref, bcp_ref, w3_hbm,
             h2_ref, p_ref, n_ref, xr_ref, hs_ref, v1_ref, ya_ref, yb_ref,
             win_v, w3_v, ext4, ext31, es31, hcar, sems):
        @pl.when(pl.program_id(0) == 0)
        def _():
            _load_weights([(win_hbm, win_v)] + _w3_copies(w3_hbm, w3_rows, w3_v), sems)
            ext4[pl.ds(0, CONV4_HALO), :] = jnp.zeros((CONV4_HALO, d), F32)
            ext31[pl.ds(0, CONV31_HALO), :] = jnp.zeros((CONV31_HALO, d), F32)
            hcar[...] = jnp.zeros_like(hcar)

        n, _ = _rms_fwd(h_ref[...], g_ref[...])
        nb = n.astype(BF16)
        n_ref[...] = nb

        def piece(q):
            parts = [_nn(nb, win_v[j, :, bl:bh]) for j, _, _, bl, bh in _piece_segments(q, d, nb_cols)]
            pq = (jnp.concatenate(parts, axis=1) + b_ref[:, q * d:(q + 1) * d]).astype(BF16)
            p_ref[:, q * d:(q + 1) * d] = pq
            return pq.astype(F32)

        x_rnn, y_rnn, glu_v, glu_g, gate_a, gate_b = [piece(q) for q in range(6)]

        ext4[pl.ds(CONV4_HALO, tm), :] = x_rnn
        xr = cb4_ref[...] + jnp.zeros((tm, d), F32)
        for k in range(k4):
            xr = xr + cw4_ref[k:k + 1, :] * ext4[pl.ds(CONV4_HALO - (k4 - 1) + k, tm), :]
        ext4[pl.ds(0, CONV4_HALO), :] = ext4[pl.ds(tm, CONV4_HALO), :]
        xrb = xr.astype(BF16)
        xr_ref[...] = xrb
        xr = xrb.astype(F32)
        _, ig, _, a, s = _gates(xrb, wg_ref, ba_ref[...], bx_ref[...], lam_ref[...], hd)
        hseq = _scan_fwd(a, s * (ig * xr), hcar[0:1, :])
        hcar[0:1, :] = hseq[tm - 1:tm, :]
        hs_ref[...] = hseq.astype(BF16)
        gl, _ = _gelu(y_rnn)
        ya = _nn((hseq * gl).astype(BF16), w3_v[0])
        ya_ref[...] = ya.astype(BF16)

        ext31[pl.ds(CONV31_HALO, tm), :] = glu_v * _sigmoid(glu_g)
        _shifted_copies(ext31, es31, tm + CONV31_HALO - SUBLANES)
        v1 = cb31_ref[...] + jnp.zeros((tm, d), F32)
        for k in range(k31):
            v1 = v1 + cw31_ref[k:k + 1, :] * _tap(ext31, es31, CONV31_HALO - (k31 - 1) + k, tm)
        ext31[pl.ds(0, CONV31_HALO), :] = ext31[pl.ds(tm, CONV31_HALO), :]
        v1b = v1.astype(BF16)
        v1_ref[...] = v1b
        v1 = v1b.astype(F32)
        xc = v1 - jnp.mean(v1, axis=-1, keepdims=True)
        rstd = lax.rsqrt(jnp.mean(xc * xc, axis=-1, keepdims=True) + EPS)
        v2 = xc * rstd * lng_ref[...] + lnb_ref[...]
        yb = _nn((v2 * _sigmoid(v2)).astype(BF16), w3_v[1]) + bcp_ref[...]
        yb_ref[...] = yb.astype(BF16)

        merged = _sigmoid(gate_a) * ya + _sigmoid(gate_b) * yb
        h2_ref[...] = h_ref[...] + _nn(merged.astype(BF16), w3_v[2])

    row = pl.BlockSpec((tm, d), lambda i: (i, 0))
    wide = pl.BlockSpec((tm, n_in), lambda i: (i, 0))
    full = lambda a: pl.BlockSpec(a.shape, lambda i, nd=a.ndim: (0,) * nd)
    smalls = [cw4, cb4, wg, ba, bx, lam, cw31, cb31, lng, lnb, bcp]
    return _call(
        body, "mixer_fwd", (tp // tm,),
        [row, full(g), full(b_in), _any()] + [full(a) for a in smalls] + [_any()],
        [row, wide] + [row] * 6,
        [jax.ShapeDtypeStruct((tp, d), F32), jax.ShapeDtypeStruct((tp, n_in), BF16)]
        + [jax.ShapeDtypeStruct((tp, d), BF16)] * 6,
        [pltpu.VMEM(win_all.shape, BF16),
         pltpu.VMEM((3, d, d), BF16),
         pltpu.VMEM((tm + CONV4_HALO, d), F32),
         pltpu.VMEM((tm + CONV31_HALO, d), F32),
         pltpu.VMEM((SUBLANES, tm + CONV31_HALO, d), F32),
         pltpu.VMEM((SUBLANES, d), F32),
         pltpu.SemaphoreType.DMA((1 + 3 * N_DEV,))],
        [h, g, b_in, win_all, *smalls, w3_all], comm)


SG_BIN, SG_CW4, SG_CB4, SG_BA, SG_BX, SG_LAM, SG_CB31, SG_LNG, SG_LNB, SG_BCP, SG_MIX, SG_CW31 = 0, 6, 10, 11, 12, 13, 14, 15, 16, 17, 18, 19


def _mixer_bwd(dh2, h, g, proj, xr_s, hs_s, v1_s, ya_s, yb_s, win_t, cw4, wg, ba, bx, lam, cw31, lng, lnb, w3_all, tm,
               comm=None):
    tp, d = dh2.shape
    n_in = proj.shape[1]
    hd = wg.shape[-1]
    k4, k31 = cw4.shape[0], cw31.shape[0]
    nt = tp // tm
    w3_rows = d // N_DEV
    sg_rows = -(-(SG_CW31 + k31) // SUBLANES) * SUBLANES
    halo_rows = 16
    per = tm // halo_rows

    def body(dh_ref, h_ref, g_ref, p_ref, xr_ref, hs_ref, hh_ref, v1_ref, ya_ref, yb_ref, win_hbm,
             cw4_ref, wg_ref, wgt_ref, ba_ref, bx_ref, lam_ref, cw31_ref, lng_ref, lnb_ref, w3_hbm,
             dh1_ref, dp_ref, x3_ref, y3_ref, yg_ref, sg_ref,
             win_v, w3_v, extd4, extd31, es31, gcar, sems):
        i = pl.program_id(0)
        tile = nt - 1 - i

        @pl.when(i == 0)
        def _():
            _load_weights([(win_hbm, win_v)] + _w3_copies(w3_hbm, w3_rows, w3_v), sems)
            for q in range(3):
                w3_v[q] = w3_v[q].T
            extd4[pl.ds(tm, CONV4_HALO), :] = jnp.zeros((CONV4_HALO, d), F32)
            extd31[pl.ds(tm, CONV31_HALO), :] = jnp.zeros((CONV31_HALO, d), F32)
            gcar[...] = jnp.zeros_like(gcar)
            sg_ref[...] = jnp.zeros_like(sg_ref)

        def acc(row, val):
            sg_ref[row:row + 1, :] += _rowsum(val)

        rows = lax.broadcasted_iota(jnp.int32, (tm, d), 0)
        x_rnn = p_ref[:, 0:d].astype(F32)
        y_rnn = p_ref[:, d:2 * d].astype(F32)
        glu_v = p_ref[:, 2 * d:3 * d].astype(F32)
        glu_g = p_ref[:, 3 * d:4 * d].astype(F32)
        sga = _sigmoid(p_ref[:, 4 * d:5 * d].astype(F32))
        sgb = _sigmoid(p_ref[:, 5 * d:6 * d].astype(F32))
        ya = ya_ref[...].astype(F32)
        yb = yb_ref[...].astype(F32)

        dmob = dh_ref[...].astype(BF16)
        dmerged = _nn(dmob, w3_v[2])
        x3_ref[:, 0:d] = (sga * ya + sgb * yb).astype(BF16)
        y3_ref[:, 0:d] = dmob
        dya = sga * dmerged
        dyb = sgb * dmerged
        dn_parts = []

        def emit(q, val):
            vb = val.astype(BF16)
            dp_ref[:, q * d:(q + 1) * d] = vb
            acc(SG_BIN + q, val)
            term = _nn(vb, win_v[pl.ds(q * d, d), :])
            dn_parts[:] = [term if not dn_parts else dn_parts[0] + term]

        emit(4, dmerged * ya * sga * (1.0 - sga))
        emit(5, dmerged * yb * sgb * (1.0 - sgb))

        dyab = dya.astype(BF16)
        y3_ref[:, d:2 * d] = dyab
        dza = _nn(dyab, w3_v[0])
        hsv = hs_ref[...].astype(F32)
        gl, th = _gelu(y_rnn)
        x3_ref[:, d:2 * d] = (hsv * gl).astype(BF16)
        emit(1, dza * hsv * _gelu_grad(y_rnn, th))
        dhs = dza * gl
        xrb = xr_ref[...]
        xr = xrb.astype(F32)
        lam_v = lam_ref[...]
        r, ig, sp, a, s = _gates(xrb, wg_ref, ba_ref[...], bx_ref[...], lam_v, hd)
        b = jnp.where(rows == tm - 1, gcar[1:2, :], pltpu.roll(a, tm - 1, 0))
        big_g = _scan_bwd(b, dhs, gcar[0:1, :])
        gcar[0:1, :] = big_g[0:1, :]
        gcar[1:2, :] = a[0:1, :]
        h_before = jnp.where(tile > 0, hh_ref[halo_rows - 1:halo_rows, :].astype(F32), 0.0)
        h_prev = jnp.where(rows == 0, h_before, pltpu.roll(hsv, 1, 0))
        ds = big_g * ig * xr
        dla = big_g * h_prev * a - ds * (a * a) / jnp.maximum(s, 1e-20)
        acc(SG_LAM, dla * r * (RG_LRU_C * _sigmoid(-lam_v)))
        dpr = dla * (-RG_LRU_C * sp) * r * (1.0 - r)
        dpi = big_g * s * xr * ig * (1.0 - ig)
        acc(SG_BA, dpr)
        acc(SG_BX, dpi)
        dprb = dpr.astype(BF16)
        dpib = dpi.astype(BF16)
        yg_ref[:, 0:d] = dprb
        yg_ref[:, d:2 * d] = dpib
        back = []
        for hh in range(N_HEADS):
            sl = slice(hh * hd, (hh + 1) * hd)
            back.append(_nn(dprb[:, sl], wgt_ref[0, hh]) + _nn(dpib[:, sl], wgt_ref[1, hh]))
        dxr = big_g * s * ig + jnp.concatenate(back, axis=1)
        acc(SG_CB4, dxr)
        extd4[pl.ds(0, tm), :] = dxr
        dx_rnn = jnp.zeros((tm, d), F32)
        for k in range(k4):
            term = extd4[pl.ds(k4 - 1 - k, tm), :]
            dx_rnn = dx_rnn + cw4_ref[k:k + 1, :] * term
            acc(SG_CW4 + k, x_rnn * term)
        extd4[pl.ds(tm, CONV4_HALO), :] = extd4[pl.ds(0, CONV4_HALO), :]
        emit(0, dx_rnn)

        dybb = dyb.astype(BF16)
        y3_ref[:, 2 * d:3 * d] = dybb
        acc(SG_BCP, dyb)
        dv3 = _nn(dybb, w3_v[1])
        v1 = v1_ref[...].astype(F32)
        xc = v1 - jnp.mean(v1, axis=-1, keepdims=True)
        rstd = lax.rsqrt(jnp.mean(xc * xc, axis=-1, keepdims=True) + EPS)
        xhat = xc * rstd
        lng_v = lng_ref[...]
        v2 = xhat * lng_v + lnb_ref[...]
        s2 = _sigmoid(v2)
        x3_ref[:, 2 * d:3 * d] = (v2 * s2).astype(BF16)
        dv2 = dv3 * (s2 * (1.0 + v2 * (1.0 - s2)))
        acc(SG_LNG, dv2 * xhat)
        acc(SG_LNB, dv2)
        dxh = dv2 * lng_v
        dv1 = rstd * (dxh - jnp.mean(dxh, axis=-1, keepdims=True)
                      - xhat * jnp.mean(dxh * xhat, axis=-1, keepdims=True))
        acc(SG_CB31, dv1)
        extd31[pl.ds(0, tm), :] = dv1
        _shifted_copies(extd31, es31, tm + CONV31_HALO - SUBLANES)
        sgg = _sigmoid(glu_g)
        v0 = glu_v * sgg
        dv0 = jnp.zeros((tm, d), F32)
        for k in range(k31):
            term = _tap(extd31, es31, k31 - 1 - k, tm)
            dv0 = dv0 + cw31_ref[k:k + 1, :] * term
            acc(SG_CW31 + k, v0 * term)
        extd31[pl.ds(tm, CONV31_HALO), :] = extd31[pl.ds(0, CONV31_HALO), :]
        emit(2, dv0 * sgg)
        emit(3, dv0 * glu_v * sgg * (1.0 - sgg))

        dn = dn_parts[0]
        x = h_ref[...]
        rr = lax.rsqrt(jnp.mean(x * x, axis=-1, keepdims=True) + EPS)
        dx, dgp = _rms_bwd(dn, x, rr, g_ref[...])
        dh1_ref[...] = dh_ref[...] + dx
        sg_ref[SG_MIX:SG_MIX + 1, :] += dgp

    rev = lambda i: (nt - 1 - i, 0)
    row = pl.BlockSpec((tm, d), rev)
    wide = pl.BlockSpec((tm, n_in), rev)
    full = lambda a: pl.BlockSpec(a.shape, lambda i, nd=a.ndim: (0,) * nd)
    halo = pl.BlockSpec((halo_rows, d), lambda i: (jnp.maximum((nt - 1 - i) * per - 1, 0), 0))
    smalls = [cw4, wg, jnp.swapaxes(wg, 2, 3), ba, bx, lam, cw31, lng, lnb]
    return _call(
        body, "mixer_bwd", (nt,),
        [row, row, full(g), wide, row, row, halo, row, row, row, _any()]
        + [full(a) for a in smalls] + [_any()],
        [row, wide, pl.BlockSpec((tm, 3 * d), rev), pl.BlockSpec((tm, 3 * d), rev),
         pl.BlockSpec((tm, 2 * d), rev), pl.BlockSpec((sg_rows, d), lambda i: (0, 0))],
        [jax.ShapeDtypeStruct((tp, d), F32), jax.ShapeDtypeStruct((tp, n_in), BF16),
         jax.ShapeDtypeStruct((tp, 3 * d), BF16), jax.ShapeDtypeStruct((tp, 3 * d), BF16),
         jax.ShapeDtypeStruct((tp, 2 * d), BF16), jax.ShapeDtypeStruct((sg_rows, d), F32)],
        [pltpu.VMEM(win_t.shape, BF16),
         pltpu.VMEM((3, d, d), BF16),
         pltpu.VMEM((tm + CONV4_HALO, d), F32),
         pltpu.VMEM((tm + CONV31_HALO, d), F32),
         pltpu.VMEM((SUBLANES, tm + CONV31_HALO, d), F32),
         pltpu.VMEM((SUBLANES, d), F32),
         pltpu.SemaphoreType.DMA((1 + 3 * N_DEV,))],
        [dh2, h, g, proj, xr_s, hs_s, hs_s, v1_s, ya_s, yb_s, win_t, *smalls, w3_all], comm)


def _tn_matmul(name, x, y, x_spec, y_spec, n_blocks, kb, nb, tm, tp, out_shape, out_spec, out_view, comm=None,
               after=None):
    nt = tp // tm

    def body(x_ref, y_ref, *refs):
        o_ref, acc = refs[-2:]
        i = pl.program_id(1)

        @pl.when(i == 0)
        def _():
            acc[...] = jnp.zeros_like(acc)

        acc[...] += _tn(x_ref[...], y_ref[...])

        @pl.when(i == nt - 1)
        def _():
            o_ref[...] = acc[...].astype(BF16).reshape(out_view)

    follows = [] if after is None else [after]
    outs, extra = _call(body, name, (n_blocks, nt), [x_spec, y_spec] + [_any()] * len(follows), [out_spec],
                        [jax.ShapeDtypeStruct(out_shape, BF16)], [pltpu.VMEM((kb, nb), F32)], [x, y] + follows,
                        comm)
    return outs[0], extra


def kernel(x, meta_tokens, ffn1_norm, ffn1_w_gu, ffn1_w_down, mix_norm, w_in, b_in, rnn_conv_w, rnn_conv_b, rg_w_a, rg_b_a, rg_w_x, rg_b_x, rg_lambda, rnn_w_proj, conv_dw_w, conv_dw_b, conv_ln_g, conv_ln_b, conv_w_proj, conv_b_proj, w_out, ffn2_norm, ffn2_w_gu, ffn2_w_down, final_norm, loss_target, m_meta_tokens, m_ffn1_norm, m_ffn1_w_gu, m_ffn1_w_down, m_mix_norm, m_w_in, m_b_in, m_rnn_conv_w, m_rnn_conv_b, m_rg_w_a, m_rg_b_a, m_rg_w_x, m_rg_b_x, m_rg_lambda, m_rnn_w_proj, m_conv_dw_w, m_conv_dw_b, m_conv_ln_g, m_conv_ln_b, m_conv_w_proj, m_conv_b_proj, m_w_out, m_ffn2_norm, m_ffn2_w_gu, m_ffn2_w_down, m_final_norm, v_meta_tokens, v_ffn1_norm, v_ffn1_w_gu, v_ffn1_w_down, v_mix_norm, v_w_in, v_b_in, v_rnn_conv_w, v_rnn_conv_b, v_rg_w_a, v_rg_b_a, v_rg_w_x, v_rg_b_x, v_rg_lambda, v_rnn_w_proj, v_conv_dw_w, v_conv_dw_b, v_conv_ln_g, v_conv_ln_b, v_conv_w_proj, v_conv_b_proj, v_w_out, v_ffn2_norm, v_ffn2_w_gu, v_ffn2_w_down, v_final_norm):
    w = dict(locals())
    seq, d = x.shape[1], x.shape[2]
    n_meta = meta_tokens.shape[0]
    t_real = n_meta + seq
    tp, tm, tmx_fwd, tmx, tmt, tmw = _tiles(t_real)
    fb = ffn1_w_gu.shape[-1]
    wr = ffn1_w_down.shape[1]
    f = N_DEV * wr
    fc = f // FFN_CHUNKS
    nbc = w_in.shape[-1]
    n_in = N_DEV * nbc
    pr = rnn_w_proj.shape[1]
    hd = rg_w_a.shape[-1]
    gr = rg_w_a.shape[2]
    cw = meta_tokens.shape[1]
    k4, k31 = rnn_conv_w.shape[1], conv_dw_w.shape[1]
    assert n_in == 6 * d and 2 * wr == fb and N_HEADS * hd == d and pr * N_DEV == d

    xi, yi, ci = lax.axis_index("x"), lax.axis_index("y"), lax.axis_index("c")
    core = ci.astype(jnp.int32).reshape(1)
    chip = (2 * xi + yi).astype(jnp.int32).reshape(1)
    me_index = (4 * xi + 2 * yi + ci).astype(jnp.int32).reshape(1)

    for nm in ("ffn1_w_gu", "ffn2_w_gu"):
        for pre in ("", "m_", "v_"):
            w[pre + nm] = jnp.swapaxes(w[pre + nm], 1, 2)

    wgut1 = w["ffn1_w_gu"][0].astype(BF16)
    wgut2 = w["ffn2_w_gu"][0].astype(BF16)
    wd1 = ffn1_w_down[0].astype(BF16)
    wd2 = ffn2_w_down[0].astype(BF16)
    win_loc = w_in[0].astype(BF16)
    win_t_loc = jnp.swapaxes(w_in[0], 0, 1).astype(BF16)
    w3_loc = jnp.concatenate([rnn_w_proj[0], conv_w_proj[0], w_out[0]], axis=0).astype(BF16)
    wg_loc = jnp.stack([rg_w_a[0], rg_w_x[0]]).astype(BF16)
    n_small = n_meta + k4 + k31
    small_rows = -(-n_small // SUBLANES) * SUBLANES
    small_loc = jnp.concatenate([meta_tokens, rnn_conv_w[0], conv_dw_w[0],
                                 jnp.zeros((small_rows - n_small, cw), F32)], axis=0)
    (wgut1_all, wd1_all, small_all), h0, tgt = _first_gather(
        [wgut1, wd1, small_loc], 2, x[0], loss_target[0], n_meta, tp)
    small_full = small_all.transpose(1, 0, 2).reshape(small_rows, d)
    cw4 = small_full[n_meta:n_meta + k4]
    cw31 = small_full[n_meta + k4:n_meta + k4 + k31]

    wgu1, wdn1 = wgut1_all.reshape(2 * f, d), wd1_all.reshape(f, d)
    (h1, gu1, n1), (win_all, w3_all, wg_all) = _ffn_fwd(
        h0, ffn1_norm, wgu1, wdn1, tm, comm=_Gather([win_loc, w3_loc, wg_loc], pass_on_at=(0.65, 0.95)))
    wg = wg_all.transpose(1, 2, 0, 3, 4).reshape(2, N_HEADS, hd, hd)
    (h2, proj, n2, xr_s, hs_s, v1_s, ya_s, yb_s), (wgut2_all, wd2_all) = _mixer_fwd(
        h1, mix_norm, b_in, win_all, cw4, rnn_conv_b, wg, rg_b_a, rg_b_x, rg_lambda, cw31, conv_dw_b, conv_ln_g,
        conv_ln_b, conv_b_proj, w3_all, tmx_fwd, comm=_Gather([wgut2, wd2], pass_on_at=(0.3, 0.5)))
    wgu2, wdn2 = wgut2_all.reshape(2 * f, d), wd2_all.reshape(f, d)
    (dh3, gu2, n3, tail), (win_t_all,) = _ffn_fwd(
        h2, ffn2_norm, wgu2, wdn2, tm, loss=(tgt, final_norm.reshape(1, d), n_meta, t_real),
        comm=_Gather([win_t_loc], pass_on_at=(0.45, 0.75)))
    win_t = win_t_all.reshape(n_in, d)

    def d_w_gu(tag, dgu, n_s, comm=None):
        g, extra = _tn_matmul(
            "d_w_gu" + tag, dgu, n_s,
            pl.BlockSpec((None, tmt, fc), lambda b, i: (b // FFN_CHUNKS, i, b % FFN_CHUNKS)),
            pl.BlockSpec((tmt, d), lambda b, i: (i, 0)),
            2 * FFN_CHUNKS, fc, d, tmt, tp, (2 * FFN_CHUNKS, fc, d),
            pl.BlockSpec((None, fc, d), lambda b, i: (b, 0, 0)), (fc, d), comm)
        return g.reshape(N_DEV, fb, d), extra

    def d_w_down(tag, act, df, comm=None):
        g, extra = _tn_matmul(
            "d_w_down" + tag, act, df,
            pl.BlockSpec((tmt, fc), lambda b, i: (i, b)), pl.BlockSpec((tmt, d), lambda b, i: (i, 0)),
            FFN_CHUNKS, fc, d, tmt, tp, (FFN_CHUNKS, fc, d),
            pl.BlockSpec((None, fc, d), lambda b, i: (b, 0, 0)), (fc, d), comm)
        return g.reshape(N_DEV, wr, d), extra

    (dh2, dgu2, act2, df2, tail), _ = _ffn_bwd(dh3, h2, gu2, ffn2_norm, wgu2, wdn2, tm, tail, TAIL_FFN2, n3)
    g_wgu2, _ = d_w_gu("2", dgu2, n3)
    g_wd2, _ = d_w_down("2", act2, df2)
    (dh1, dproj, x3, y3, yg, sg), (r_wd2, r_wgu2) = _mixer_bwd(
        dh2, h1, mix_norm, proj, xr_s, hs_s, v1_s, ya_s, yb_s, win_t, cw4, wg, rg_b_a, rg_b_x, rg_lambda, cw31,
        conv_ln_g, conv_ln_b, w3_all, tmx, comm=_Scatter([g_wd2, g_wgu2]))
    g_w3, _ = _tn_matmul(
        "d_w_proj3", x3, y3,
        pl.BlockSpec((tmw, d), lambda b, i: (i, b)), pl.BlockSpec((tmw, d), lambda b, i: (i, b)),
        3, d, d, tmw, tp, (N_DEV, 3, pr, d), pl.BlockSpec((N_DEV, None, pr, d), lambda b, i: (0, b, 0, 0)),
        (N_DEV, pr, d))
    g_wg, _ = _tn_matmul(
        "d_w_gates", xr_s, yg,
        pl.BlockSpec((tmw, hd), lambda b, i: (i, b % N_HEADS)), pl.BlockSpec((tmw, hd), lambda b, i: (i, b)),
        2 * N_HEADS, hd, hd, tmw, tp, (N_DEV, 2 * N_HEADS, gr, hd),
        pl.BlockSpec((N_DEV, None, gr, hd), lambda b, i: (0, b, 0, 0)), (N_DEV, gr, hd))
    w3_sems, g_w3_thru, w3_land, w3_token = _exchange_start("grads_proj3_exchange", _scatter_copies, N_DEV - 1, g_w3)
    g_win, (r_wg,) = _tn_matmul(
        "d_w_in", n2, dproj,
        pl.BlockSpec((tmw, d), lambda b, i: (i, 0)), pl.BlockSpec((tmw, nbc), lambda b, i: (i, b)),
        N_DEV, d, nbc, tmw, tp, (N_DEV, d, nbc), pl.BlockSpec((None, d, nbc), lambda b, i: (b, 0, 0)), (d, nbc),
        comm=_Scatter([g_wg]), after=w3_token)
    win_sems, g_win_thru, win_land, win_token = _exchange_start("grads_w_in_exchange", _scatter_copies, N_DEV - 1, g_win)
    (dh0, dgu1, act1, df1, tail), _ = _ffn_bwd(dh1, h0, gu1, ffn1_norm, wgu1, wdn1, tm, tail, TAIL_FFN1, win_token)

    pieces = [sg, dh0[:n_meta], tail]
    assert all(p.shape[0] % SUBLANES == 0 for p in pieces)
    at = [0, sg.shape[0], sg.shape[0] + n_meta]
    loss_row = at[2] + TAIL_LOSS
    rep_rows = [("ffn1_norm", at[2] + TAIL_FFN1, 1), ("mix_norm", SG_MIX, 1), ("b_in", SG_BIN, 6),
                ("rnn_conv_b", SG_CB4, 1),
                ("rg_b_a", SG_BA, 1), ("rg_b_x", SG_BX, 1), ("rg_lambda", SG_LAM, 1), ("conv_dw_b", SG_CB31, 1),
                ("conv_ln_g", SG_LNG, 1), ("conv_ln_b", SG_LNB, 1), ("conv_b_proj", SG_BCP, 1),
                ("ffn2_norm", at[2] + TAIL_FFN2, 1), ("final_norm", at[2] + TAIL_FINAL, 1)]
    col_rows = [("meta_tokens", at[1], n_meta), ("rnn_conv_w", SG_CW4, k4), ("conv_dw_w", SG_CW31, k31)]
    layout = []
    for nm, row0, nr in rep_rows:
        kind = "wide" if nm == "b_in" else "rep"
        as2d = lambda a: a.reshape(1, -1) if a.ndim == 1 else a
        layout.append((kind, row0, nr, as2d(w[nm]), as2d(w["m_" + nm]), as2d(w["v_" + nm])))
    for nm, row0, nr in col_rows:
        sq = lambda a: a.reshape(a.shape[-2], a.shape[-1])
        layout.append(("col", row0, nr, sq(w[nm]), sq(w["m_" + nm]), sq(w["v_" + nm])))
    small_partial = jnp.concatenate(pieces, axis=0)

    g_wd1, (small_partials,) = d_w_down("1", act1, df1, comm=_Bcast(small_partial))
    g_wgu1, (r_wd1,) = d_w_gu("1", dgu1, n1, comm=_Scatter([g_wd1]))

    g_last = g_wgu1.reshape((4, 2) + g_wgu1.shape[1:])
    comb_wgu1 = _pair_reduce(g_last, core)
    sems, comb_thru, land_thru, after = _exchange_start("grads_chip_exchange", _chip_copies, 3, comb_wgu1)
    g_win, r_win = _exchange_wait("grads_w_in_exchange", _scatter_copies, win_sems, g_win_thru, win_land, after)
    g_w3, r_w3 = _exchange_wait("grads_proj3_exchange", _scatter_copies, w3_sems, g_w3_thru, w3_land, after)

    grad_x = (dh0[n_meta:t_real] + after[0, 0])[None]
    total, small_out = _small_adamw(small_partials, layout, me_index, grad_x)
    after = total

    groups = [(g_wd1, r_wd1, me_index, ["ffn1_w_down"]),
              (g_wd2, r_wd2, me_index, ["ffn2_w_down"]), (g_wgu2, r_wgu2, me_index, ["ffn2_w_gu"]),
              (g_win, r_win, me_index, ["w_in"]), (g_w3, r_w3, me_index, ["w_out", "rnn_w_proj", "conv_w_proj"]),
              (g_wg, r_wg, me_index, ["rg_w_a", "rg_w_x"]), (None, None, chip, ["ffn1_w_gu"])]
    res = {}
    for own, recv, idx, group in groups:
        if own is None:
            own, recv = _exchange_wait("grads_chip_exchange", _chip_copies, sems, comb_thru, land_thru, after)
        outs = _final_adamw(own, recv, idx, [(w[nm], w["m_" + nm], w["v_" + nm]) for nm in group], after)
        after = outs[-1][0]
        for nm, o in zip(group, outs):
            res[nm] = o
    for nm in ("ffn1_w_gu", "ffn2_w_gu"):
        res[nm] = tuple(jnp.swapaxes(a, 1, 2) for a in res[nm])
    for (nm, _, _), o in zip(rep_rows + col_rows, small_out):
        res[nm] = tuple(a.reshape(w[nm].shape) for a in o)


    order = ["meta_tokens", "ffn1_norm", "ffn1_w_gu", "ffn1_w_down", "mix_norm", "w_in", "b_in", "rnn_conv_w",
             "rnn_conv_b", "rg_w_a", "rg_b_a", "rg_w_x", "rg_b_x", "rg_lambda", "rnn_w_proj", "conv_dw_w",
             "conv_dw_b", "conv_ln_g", "conv_ln_b", "conv_w_proj", "conv_b_proj", "w_out", "ffn2_norm",
             "ffn2_w_gu", "ffn2_w_down", "final_norm"]
    return (total[loss_row, 0], grad_x, *[res[nm][0] for nm in order], *[res[nm][1] for nm in order],
            *[res[nm][2] for nm in order], *[res[nm][3] for nm in order])
```

```python
import functools
import math

import jax
import jax.numpy as jnp
from jax import lax
from jax.experimental import pallas as pl
from jax.experimental.pallas import tpu as pltpu

F32 = jnp.float32
BF16 = jnp.bfloat16
MESH = pl.DeviceIdType.MESH
N_DEV = 8
N_HEADS = 4
RG_LRU_C = 8.0
EPS = 1e-6
FFN_RES = 0.5
ADAM_LR, ADAM_B1, ADAM_B2, ADAM_EPS, ADAM_WD, ADAM_STEP = 0.001, 0.9, 0.999, 1e-08, 0.01, 10
V7X_VMEM_LIMIT = 56 * 1024 * 1024
CONV4_HALO = 8
CONV31_HALO = 32
SUBLANES = 8
STAGE_ROWS = 512
TAIL_FFN1, TAIL_FINAL, TAIL_LOSS, TAIL_FFN2 = 0, 1, 2, 3
FFN_CHUNKS = 2
FFN_FWD_CHUNKS = 1
GELU_C = math.sqrt(2.0 / math.pi)
GELU_K = 0.044715


def _any():
    return pl.BlockSpec(memory_space=pl.ANY)


def _params(n_grid):
    return pltpu.CompilerParams(dimension_semantics=("arbitrary",) * n_grid, vmem_limit_bytes=V7X_VMEM_LIMIT)


def _nn(a, b):
    return jnp.dot(a, b, preferred_element_type=F32)


def _nt(a, b):
    return lax.dot_general(a, b, (((1,), (1,)), ((), ())), preferred_element_type=F32)


def _tn(a, b):
    return lax.dot_general(a, b, (((0,), (0,)), ((), ())), preferred_element_type=F32)


def _sigmoid(x):
    return 0.5 * jnp.tanh(0.5 * x) + 0.5


def _rowsum(x):
    return jnp.sum(x, axis=0, keepdims=True)


def _rms_fwd(x, g):
    r = lax.rsqrt(jnp.mean(x * x, axis=-1, keepdims=True) + EPS)
    return x * r * g, r


def _rms_bwd(dn, x, r, g):
    xr = x * r
    gy = dn * g
    dx = r * (gy - xr * jnp.mean(gy * xr, axis=-1, keepdims=True))
    return dx, _rowsum(dn * xr)


def _gelu(y):
    t = jnp.tanh(GELU_C * (y + GELU_K * y * y * y))
    return 0.5 * y * (1.0 + t), t


def _gelu_grad(y, t):
    return 0.5 * (1.0 + t) + 0.5 * y * (1.0 - t * t) * GELU_C * (1.0 + 3.0 * GELU_K * y * y)


def _softplus(x):
    return jnp.maximum(x, 0.0) + jnp.log(1.0 + jnp.exp(-jnp.abs(x)))


def _one_minus_exp(z):
    series = -z * (1.0 + 0.5 * z * (1.0 + z * (1.0 / 3.0) * (1.0 + 0.25 * z)))
    return jnp.where(z > -0.05, series, 1.0 - jnp.exp(z))


def _tiles(t_real):
    if t_real > 2048:
        tm = 416
        tp = -(-t_real // tm) * tm
        return tp, tm, tm // 2, tm // 2, tp, tp
    tm = 128
    tp = -(-t_real // tm) * tm
    return tp, tm, tm // 2, tm // 2, tm, tm


def _load_weights(copies, sems):
    cps = [pltpu.make_async_copy(s, d, sems.at[k]) for k, (s, d) in enumerate(copies)]
    for cp in cps:
        cp.start()
    for cp in cps:
        cp.wait()


def _position():
    x, y, c = lax.axis_index("x"), lax.axis_index("y"), lax.axis_index("c")
    chips = [(1 - x, y), (x, 1 - y), (1 - x, 1 - y)]
    return x, y, c, chips


def _slot(p):
    return 4 * p[0] + 2 * p[1] + p[2]


class _Lazy(dict):
    def __getitem__(self, key):
        val = dict.__getitem__(self, key)
        return val() if callable(val) else val


class _Gather:
    def __init__(self, shards, pass_on_at=None):
        self.shards = list(shards)
        self.n = len(self.shards)
        self.pass_on_at = pass_on_at

    def inputs(self):
        return self.shards

    def out_shape(self):
        return [jax.ShapeDtypeStruct((N_DEV,) + s.shape, s.dtype) for s in self.shards]

    N_SEMS = 9

    def scratch(self):
        return [pltpu.SemaphoreType.DMA((self.N_SEMS * self.n,)), pltpu.SemaphoreType.DMA((self.N_SEMS * self.n,)),
                pltpu.SemaphoreType.DMA((self.n,))]

    def _plan(self, ins, outs, sems):
        send_sems, recv_sems, local_sems = sems
        x, y, c, _ = _position()
        me, sib, xn, yn, dg = (x, y, c), (x, y, 1 - c), (1 - x, y, c), (x, 1 - y, c), (1 - x, 1 - y, c)
        other = lambda p: (p[0], p[1], 1 - c)

        def blk(a, p, half=None):
            ref = outs[a].at[_slot(p)]
            if half is None:
                return ref
            rows = self.shards[a].shape[0] // 2
            return ref.at[pl.ds(half * rows, rows)]

        def copy(a, k, dst, to, src=None):
            return pltpu.make_async_remote_copy(
                src_ref=dst if src is None else src, dst_ref=dst,
                send_sem=send_sems.at[self.N_SEMS * a + k], recv_sem=recv_sems.at[self.N_SEMS * a + k],
                device_id=to, device_id_type=MESH)

        cp = _Lazy(mine=lambda: [pltpu.make_async_copy(ins[a], blk(a, me), local_sems.at[a]) for a in range(self.n)])
        for a in range(self.n):
            cp[a] = _Lazy(
                own=lambda a=a: [copy(a, 0, blk(a, me), sib, src=ins[a]), copy(a, 1, blk(a, me), xn, src=ins[a]),
                                 copy(a, 2, blk(a, me), yn, src=ins[a])],
                from_x=lambda a=a: copy(a, 1, blk(a, xn), me), from_y=lambda a=a: copy(a, 2, blk(a, yn), me),
                relay_x=lambda a=a: copy(a, 3, blk(a, xn, 0), yn), relay_y=lambda a=a: copy(a, 4, blk(a, yn, 1), xn),
                diag0=lambda a=a: copy(a, 3, blk(a, dg, 0), me), diag1=lambda a=a: copy(a, 4, blk(a, dg, 1), me),
                pass_x=lambda a=a: copy(a, 5, blk(a, xn), sib), pass_y=lambda a=a: copy(a, 6, blk(a, yn), sib),
                pass_d0=lambda a=a: copy(a, 7, blk(a, dg, 0), sib), pass_d1=lambda a=a: copy(a, 8, blk(a, dg, 1), sib),
                from_sib=lambda a=a: [copy(a, 0, blk(a, sib), me), copy(a, 5, blk(a, other(xn)), me),
                                      copy(a, 6, blk(a, other(yn)), me), copy(a, 7, blk(a, other(dg), 0), me),
                                      copy(a, 8, blk(a, other(dg), 1), me)])
        return cp

    def start(self, ins, outs, sems):
        cp = self._plan(ins, outs, sems)
        for c in cp["mine"]:
            c.start()
        for a in range(self.n):
            for c in cp[a]["own"]:
                c.start()

    def pass_on(self, ins, outs, sems):
        cp = self._plan(ins, outs, sems)
        for a in range(self.n):
            cp[a]["from_x"].wait_recv()
            cp[a]["relay_x"].start()
            cp[a]["pass_x"].start()
        for a in range(self.n):
            cp[a]["from_y"].wait_recv()
            cp[a]["relay_y"].start()
            cp[a]["pass_y"].start()

    def pass_on_relayed(self, ins, outs, sems):
        cp = self._plan(ins, outs, sems)
        for a in range(self.n):
            cp[a]["diag0"].wait_recv()
            cp[a]["pass_d0"].start()
            cp[a]["diag1"].wait_recv()
            cp[a]["pass_d1"].start()

    def finish(self, ins, outs, sems):
        if self.pass_on_at is None:
            self.pass_on(ins, outs, sems)
            self.pass_on_relayed(ins, outs, sems)
        cp = self._plan(ins, outs, sems)
        for a in range(self.n):
            for c in cp[a]["from_sib"]:
                c.wait_recv()
            for c in cp[a]["own"] + [cp[a][k] for k in ("relay_x", "relay_y", "pass_x", "pass_y", "pass_d0", "pass_d1")]:
                c.wait_send()
        for c in cp["mine"]:
            c.wait()


class _Scatter:
    def __init__(self, grads):
        self.grads = list(grads)
        self.n = len(self.grads)

    def inputs(self):
        return self.grads

    def out_shape(self):
        return [jax.ShapeDtypeStruct((N_DEV - 1,) + g.shape[1:], g.dtype) for g in self.grads]

    def scratch(self):
        return [pltpu.SemaphoreType.DMA((7 * self.n,)), pltpu.SemaphoreType.DMA((7 * self.n,))]

    def _plan(self, ins, outs, sems):
        send_sems, recv_sems = sems
        x, y, c, _ = _position()
        cps = []
        for a in range(self.n):
            for k in range(1, N_DEV):
                peer = (x ^ (k >> 2), y ^ ((k >> 1) & 1), c ^ (k & 1))
                cps.append(pltpu.make_async_remote_copy(
                    src_ref=ins[a].at[_slot(peer)], dst_ref=outs[a].at[k - 1],
                    send_sem=send_sems.at[7 * a + k - 1], recv_sem=recv_sems.at[7 * a + k - 1],
                    device_id=peer, device_id_type=MESH))
        return cps

    def start(self, ins, outs, sems):
        for cp in self._plan(ins, outs, sems):
            cp.start()

    def finish(self, ins, outs, sems):
        for cp in self._plan(ins, outs, sems):
            cp.wait()


def _hosted(inner, n_in, n_out, comm, grid):
    if comm is None:
        return inner
    nc_in, nc_out, ns = len(comm.inputs()), len(comm.out_shape()), len(comm.scratch())

    def body(*refs):
        o0 = n_in + nc_in
        s0 = o0 + n_out + nc_out
        main = refs[:n_in] + refs[o0:o0 + n_out] + refs[s0:len(refs) - ns]
        c_in, c_out, c_sems = refs[n_in:o0], refs[o0 + n_out:s0], refs[len(refs) - ns:]
        ids = [pl.program_id(ax) for ax in range(len(grid))]
        first = functools.reduce(jnp.logical_and, [i == 0 for i in ids])
        last = functools.reduce(jnp.logical_and, [i == g - 1 for i, g in zip(ids, grid)])

        @pl.when(first)
        def _():
            comm.start(c_in, c_out, c_sems)

        inner(*main)

        if getattr(comm, "pass_on_at", None) is not None:
            assert len(grid) == 1
            first_at, second_at = (min(grid[0] - 1, int(frac * grid[0])) for frac in comm.pass_on_at)
            assert first_at < second_at

            @pl.when(ids[0] == first_at)
            def _():
                comm.pass_on(c_in, c_out, c_sems)

            @pl.when(ids[0] == second_at)
            def _():
                comm.pass_on_relayed(c_in, c_out, c_sems)

        @pl.when(last)
        def _():
            comm.finish(c_in, c_out, c_sems)

    return body


def _call(inner, name, grid, in_specs, out_specs, out_shape, scratch, args, comm=None):
    n_in, n_out = len(args), len(out_shape)
    body = _hosted(inner, n_in, n_out, comm, grid)
    if comm is not None:
        in_specs = list(in_specs) + [_any()] * len(comm.inputs())
        args = list(args) + comm.inputs()
        out_specs = list(out_specs) + [_any()] * len(comm.out_shape())
        out_shape = list(out_shape) + comm.out_shape()
        scratch = list(scratch) + comm.scratch()
    outs = pl.pallas_call(
        body, name=name, grid=grid, in_specs=list(in_specs), out_specs=list(out_specs), out_shape=list(out_shape),
        scratch_shapes=list(scratch), compiler_params=_params(len(grid)))(*args)
    return list(outs[:n_out]), list(outs[n_out:])


class _Bcast:
    def __init__(self, block):
        self.block = block

    def inputs(self):
        return [self.block]

    def out_shape(self):
        return [jax.ShapeDtypeStruct((N_DEV,) + self.block.shape, self.block.dtype)]

    def scratch(self):
        return [pltpu.SemaphoreType.DMA((N_DEV - 1,)), pltpu.SemaphoreType.DMA((N_DEV - 1,)),
                pltpu.SemaphoreType.DMA((1,))]

    def _plan(self, ins, outs, sems):
        send_sems, recv_sems, local_sem = sems
        x, y, c, _ = _position()
        mine = outs[0].at[_slot((x, y, c))]
        cps = []
        for k in range(1, N_DEV):
            peer = (x ^ (k >> 2), y ^ ((k >> 1) & 1), c ^ (k & 1))
            cps.append(pltpu.make_async_remote_copy(
                src_ref=ins[0], dst_ref=mine, send_sem=send_sems.at[k - 1], recv_sem=recv_sems.at[k - 1],
                device_id=peer, device_id_type=MESH))
        return pltpu.make_async_copy(ins[0], mine, local_sem.at[0]), cps

    def start(self, ins, outs, sems):
        own, cps = self._plan(ins, outs, sems)
        own.start()
        for cp in cps:
            cp.start()

    def finish(self, ins, outs, sems):
        own, cps = self._plan(ins, outs, sems)
        for cp in cps:
            cp.wait()
        own.wait()


def _first_gather(shards, small_idx, x2, t2, n_meta, tp):
    comm = _Gather(shards)
    n = comm.n
    seq, d = x2.shape
    t_real = n_meta + seq
    n_pad = tp - t_real
    cw = d // N_DEV
    rows = STAGE_ROWS if seq % STAGE_ROWS == 0 else seq
    n_chunks = seq // rows

    def body(*refs):
        ins, (x_ref, t_ref) = refs[:n], refs[n:n + 2]
        outs, (h0_ref, tg_ref) = refs[n + 2:2 * n + 2], refs[2 * n + 2:2 * n + 4]
        sems = refs[2 * n + 4:2 * n + 7]
        buf, zeros, in_sems, out_sems, misc_sems = refs[2 * n + 7:]
        comm.start(ins, outs, sems)
        zeros[...] = jnp.zeros_like(zeros)
        fills = [pltpu.make_async_copy(zeros.at[pl.ds(0, n_pad)], h0_ref.at[pl.ds(t_real, n_pad)], misc_sems.at[0]),
                 pltpu.make_async_copy(zeros.at[pl.ds(0, n_pad)], tg_ref.at[pl.ds(t_real, n_pad)], misc_sems.at[1]),
                 pltpu.make_async_copy(zeros.at[pl.ds(0, n_meta)], tg_ref.at[pl.ds(0, n_meta)], misc_sems.at[2])]
        for cp in fills:
            cp.start()
        jobs = [(src, dst, c) for src, dst in ((x_ref, h0_ref), (t_ref, tg_ref)) for c in range(n_chunks)]

        def load(k):
            src, _, c = jobs[k]
            return pltpu.make_async_copy(src.at[pl.ds(c * rows, rows)], buf.at[k % 2], in_sems.at[k % 2])

        def store(k):
            _, dst, c = jobs[k]
            return pltpu.make_async_copy(buf.at[k % 2], dst.at[pl.ds(n_meta + c * rows, rows)], out_sems.at[k % 2])

        load(0).start()
        for k in range(len(jobs)):
            load(k).wait()
            if k + 1 < len(jobs):
                if k >= 1:
                    store(k - 1).wait()
                load(k + 1).start()
            store(k).start()
        for k in range(max(0, len(jobs) - 2), len(jobs)):
            store(k).wait()
        comm.finish(ins, outs, sems)
        meta = [pltpu.make_async_copy(outs[small_idx].at[k, pl.ds(0, n_meta)],
                                      h0_ref.at[pl.ds(0, n_meta), pl.ds(k * cw, cw)], misc_sems.at[3 + k])
                for k in range(N_DEV)]
        for cp in meta:
            cp.start()
        for cp in fills + meta:
            cp.wait()

    staged = [jax.ShapeDtypeStruct((tp, d), F32)] * 2
    outs = pl.pallas_call(
        body, name="weights_all_gather", out_shape=comm.out_shape() + staged,
        in_specs=[_any()] * (n + 2), out_specs=[_any()] * (n + 2),
        scratch_shapes=comm.scratch() + [
            pltpu.VMEM((2, rows, d), F32), pltpu.VMEM((max(n_pad, n_meta), d), F32),
            pltpu.SemaphoreType.DMA((2,)), pltpu.SemaphoreType.DMA((2,)), pltpu.SemaphoreType.DMA((3 + N_DEV,))],
        compiler_params=pltpu.CompilerParams(vmem_limit_bytes=V7X_VMEM_LIMIT),
    )(*shards, x2, t2)
    return outs[:n], outs[n], outs[n + 1]


def _chip_copies(c_ref, land_ref, sems):
    _, _, c, chips = _position()
    return [pltpu.make_async_remote_copy(
        src_ref=c_ref.at[2 * cx + cy], dst_ref=land_ref.at[j], send_sem=sems[j], recv_sem=sems[3 + j],
        device_id=(cx, cy, c), device_id_type=MESH) for j, (cx, cy) in enumerate(chips)]


def _scatter_copies(g_ref, land_ref, sems):
    x, y, c, _ = _position()
    cps = []
    for k in range(1, N_DEV):
        peer = (x ^ (k >> 2), y ^ ((k >> 1) & 1), c ^ (k & 1))
        cps.append(pltpu.make_async_remote_copy(
            src_ref=g_ref.at[_slot(peer)], dst_ref=land_ref.at[k - 1], send_sem=sems[k - 1],
            recv_sem=sems[N_DEV - 1 + k - 1], device_id=peer, device_id_type=MESH))
    return cps


def _exchange_start(name, copies, n_copies, src):
    hbm = pl.BlockSpec(memory_space=pltpu.HBM)
    sem = pl.BlockSpec(memory_space=pltpu.SEMAPHORE)
    n_sems = 2 * n_copies

    def body(s_ref, land_ref, *refs):
        for cp in copies(s_ref, land_ref, refs[:n_sems]):
            cp.start()
        token = refs[n_sems + 2]
        token[...] = jnp.zeros_like(token)

    land = lax.empty((n_copies,) + src.shape[1:], src.dtype)
    outs = pl.pallas_call(
        body, name=name + "_start",
        out_shape=(pltpu.SemaphoreType.DMA(()),) * n_sems
        + (pltpu.HBM(src.shape, src.dtype), pltpu.HBM(land.shape, land.dtype),
           jax.ShapeDtypeStruct((SUBLANES, 128), F32)),
        in_specs=(hbm, hbm), out_specs=(sem,) * n_sems + (hbm, hbm, pl.BlockSpec(memory_space=pltpu.VMEM)),
        input_output_aliases={0: n_sems, 1: n_sems + 1},
        compiler_params=pltpu.CompilerParams(has_side_effects=pltpu.SideEffectType.DATAFLOW_SIDE_EFFECTING),
    )(pltpu.with_memory_space_constraint(src, pltpu.HBM), pltpu.with_memory_space_constraint(land, pltpu.HBM))
    return outs[:n_sems], outs[n_sems], outs[n_sems + 1], outs[n_sems + 2]


def _exchange_wait(name, copies, sems, src_thru, land_thru, after):
    hbm = pl.BlockSpec(memory_space=pltpu.HBM)
    sem = pl.BlockSpec(memory_space=pltpu.SEMAPHORE)
    n_sems = len(sems)

    def body(s_ref, land_ref, *refs):
        for cp in copies(s_ref, land_ref, refs[:n_sems]):
            cp.wait_send()
            cp.wait_recv()

    return pl.pallas_call(
        body, name=name + "_wait",
        out_shape=(pltpu.HBM(src_thru.shape, src_thru.dtype), pltpu.HBM(land_thru.shape, land_thru.dtype)),
        in_specs=(hbm, hbm) + (sem,) * n_sems + (pl.BlockSpec(memory_space=pl.ANY),), out_specs=(hbm, hbm),
        input_output_aliases={0: 0, 1: 1},
        compiler_params=pltpu.CompilerParams(has_side_effects=pltpu.SideEffectType.DATAFLOW_SIDE_EFFECTING),
    )(src_thru, land_thru, *sems, after)


def _pair_reduce(grad, core):
    blk = grad.shape[2:]
    zeros = (0,) * len(blk)

    def body(core_ref, g_hbm, own_ref, o_ref, landed, send_sems, recv_sems):
        del core_ref
        i = pl.program_id(0)
        x, y, c, _ = _position()

        def copy(k):
            return pltpu.make_async_remote_copy(
                src_ref=g_hbm.at[k, 1 - c], dst_ref=landed.at[k], send_sem=send_sems.at[k],
                recv_sem=recv_sems.at[k], device_id=(x, y, 1 - c), device_id_type=MESH)

        @pl.when(i == 0)
        def _():
            for k in range(4):
                copy(k).start()

        for k in range(4):
            @pl.when(i == k)
            def _(k=k):
                copy(k).wait_recv()

        o_ref[...] = (own_ref[...].astype(F32) + landed[i].astype(F32)).astype(BF16)

        @pl.when(i == 3)
        def _():
            for k in range(4):
                copy(k).wait_send()

    return pl.pallas_call(
        body, name="grads_pair_reduce",
        out_shape=jax.ShapeDtypeStruct((4,) + blk, BF16),
        grid_spec=pltpu.PrefetchScalarGridSpec(
            num_scalar_prefetch=1, grid=(4,),
            in_specs=[_any(), pl.BlockSpec((None, None) + blk, lambda i, cr: (i, cr[0]) + zeros)],
            out_specs=pl.BlockSpec((None,) + blk, lambda i, cr: (i,) + zeros),
            scratch_shapes=[pltpu.VMEM((4,) + blk, BF16), pltpu.SemaphoreType.DMA((4,)),
                            pltpu.SemaphoreType.DMA((4,))]),
        compiler_params=_params(1),
    )(core, grad, grad)


def _adamw(w, g, m, v):
    m2 = ADAM_B1 * m + (1.0 - ADAM_B1) * g
    v2 = ADAM_B2 * v + (1.0 - ADAM_B2) * (g * g)
    m_hat = m2 / (1.0 - ADAM_B1 ** ADAM_STEP)
    v_hat = v2 / (1.0 - ADAM_B2 ** ADAM_STEP)
    delta = -ADAM_LR * (m_hat / (jnp.sqrt(v_hat) + ADAM_EPS) + ADAM_WD * w)
    return delta, m2, v2


def _final_adamw(own, recv, idx, parts, after):
    blk = own.shape[1:]
    n_recv = recv.shape[0]
    n_parts = len(parts)
    per = blk[0] // n_parts if n_parts > 1 else None
    rows = blk[-2]
    n_chunks = 1 if n_parts > 1 else (4 if rows % 64 == 0 and rows >= 512 else (2 if rows % 32 == 0 else 1))
    cblk = blk[:-2] + (rows // n_chunks, blk[-1])
    lead = (0,) * (len(blk) - 2)

    def body(idx_ref, c_ref, r_ref, after_ref, *refs):
        del idx_ref, after_ref
        ins, outs = refs[:3 * n_parts], refs[3 * n_parts:]
        g = c_ref[...].astype(F32)
        for k in range(n_recv):
            g = g + r_ref[k].astype(F32)
        for p in range(n_parts):
            w_ref, m_ref, v_ref = ins[3 * p:3 * p + 3]
            if n_parts == 1:
                gp = g
            elif per == 1:
                gp = g[p]
            else:
                gp = g[p * per:(p + 1) * per]
            delta, m2, v2 = _adamw(w_ref[0], gp, m_ref[0], v_ref[0])
            o = outs[4 * p:4 * p + 4]
            o[0][0] = gp
            o[1][0] = delta
            o[2][0] = m2
            o[3][0] = v2

    flat = [a for wmv in parts for a in wmv]

    def part_spec(a):
        shape = a.shape[:-2] + (a.shape[-2] // n_chunks, a.shape[-1])
        return pl.BlockSpec(shape, lambda i, cr, nd=a.ndim: (0,) * (nd - 2) + (i, 0))

    outs = pl.pallas_call(
        body, name="grads_sum_adamw",
        out_shape=[jax.ShapeDtypeStruct(wmv[0].shape, F32) for wmv in parts for _ in range(4)],
        grid_spec=pltpu.PrefetchScalarGridSpec(
            num_scalar_prefetch=1, grid=(n_chunks,),
            in_specs=[pl.BlockSpec((None,) + cblk, lambda i, cr: (cr[0],) + lead + (i, 0)),
                      pl.BlockSpec((n_recv,) + cblk, lambda i, cr: (0,) + lead + (i, 0))]
                     + [_any()] + [part_spec(a) for a in flat],
            out_specs=[part_spec(wmv[0]) for wmv in parts for _ in range(4)]),
        compiler_params=_params(1),
    )(idx, own, recv, after, *flat)
    return [tuple(outs[4 * p:4 * p + 4]) for p in range(n_parts)]


def _small_adamw(partials, layout, me_index, after):
    _, rows, d = partials.shape
    n = len(layout)
    cw = d // N_DEV

    def body(me_ref, p_ref, after_ref, *refs):
        ins, t_ref, outs = refs[:3 * n], refs[3 * n], refs[3 * n + 1:]
        me = me_ref[0]
        total = p_ref[0]
        for j in range(1, N_DEV):
            total = total + p_ref[j]
        t_ref[...] = total
        for e, (kind, r0, nr, _, _, _) in enumerate(layout):
            w_ref, m_ref, v_ref = ins[3 * e:3 * e + 3]
            o = outs[4 * e:4 * e + 4]
            if kind == "rep":
                g = t_ref[r0:r0 + nr, :]
                delta, m2, v2 = _adamw(w_ref[...], g, m_ref[...], v_ref[...])
                for ref, val in zip(o, (g, delta, m2, v2)):
                    ref[...] = val
            elif kind == "wide":
                for q in range(nr):
                    sl = slice(q * d, (q + 1) * d)
                    g = t_ref[r0 + q:r0 + q + 1, :]
                    delta, m2, v2 = _adamw(w_ref[:, sl], g, m_ref[:, sl], v_ref[:, sl])
                    for ref, val in zip(o, (g, delta, m2, v2)):
                        ref[:, sl] = val
            else:
                for j in range(N_DEV):
                    @pl.when(me == j)
                    def _(j=j, o=o, w_ref=w_ref, m_ref=m_ref, v_ref=v_ref, r0=r0, nr=nr):
                        g = t_ref[r0:r0 + nr, j * cw:(j + 1) * cw]
                        delta, m2, v2 = _adamw(w_ref[...], g, m_ref[...], v_ref[...])
                        for ref, val in zip(o, (g, delta, m2, v2)):
                            ref[...] = val

    flat = [a for ent in layout for a in ent[3:]]
    vm = pl.BlockSpec(memory_space=pltpu.VMEM)
    outs = pl.pallas_call(
        body, name="small_adamw",
        out_shape=[jax.ShapeDtypeStruct((rows, d), F32)]
                  + [jax.ShapeDtypeStruct(ent[3].shape, F32) for ent in layout for _ in range(4)],
        in_specs=[pl.BlockSpec(memory_space=pltpu.SMEM), vm, _any()] + [vm] * len(flat),
        out_specs=[vm] * (1 + 4 * n),
        compiler_params=pltpu.CompilerParams(vmem_limit_bytes=V7X_VMEM_LIMIT),
    )(me_index, partials, after, *flat)
    return outs[0], [tuple(outs[1 + 4 * e:5 + 4 * e]) for e in range(n)]


def _ffn_fwd(h, g, wgu, wd, tm, loss=None, comm=None):
    tp, d = h.shape
    f = wd.shape[0]
    fc = f // FFN_FWD_CHUNKS
    nt = tp // tm
    with_loss = loss is not None
    if with_loss:
        tgt, gf, n_meta, t_real = loss

    def body(*refs):
        if with_loss:
            (h_ref, g_ref, wgu_hbm, wd_hbm, tgt_ref, gf_ref, out_ref, gu_ref, n_ref, tail_ref,
             wgu_v, wd_v, sems) = refs
        else:
            h_ref, g_ref, wgu_hbm, wd_hbm, out_ref, gu_ref, n_ref, wgu_v, wd_v, sems = refs
        i = pl.program_id(0)

        @pl.when(i == 0)
        def _():
            _load_weights([(wgu_hbm, wgu_v), (wd_hbm, wd_v)], sems)
            if with_loss:
                tail_ref[...] = jnp.zeros_like(tail_ref)

        x = h_ref[...]
        n, _ = _rms_fwd(x, g_ref[...])
        nb = n.astype(BF16)
        n_ref[...] = nb
        acc = jnp.zeros((tm, d), F32)
        for j in range(FFN_FWD_CHUNKS):
            cols = slice(j * fc, (j + 1) * fc)
            gate = _nt(nb, wgu_v[pl.ds(j * fc, fc), :])
            up = _nt(nb, wgu_v[pl.ds(f + j * fc, fc), :])
            gu_ref[0, :, cols] = gate.astype(BF16)
            gu_ref[1, :, cols] = up.astype(BF16)
            act = (gate * _sigmoid(gate) * up).astype(BF16)
            acc = acc + _nn(act, wd_v[pl.ds(j * fc, fc), :])
        hn = x + FFN_RES * acc
        if not with_loss:
            out_ref[...] = hn
        else:
            gfv = gf_ref[...]
            r = lax.rsqrt(jnp.mean(hn * hn, axis=-1, keepdims=True) + EPS)
            xr = hn * r
            rows = i * tm + lax.broadcasted_iota(jnp.int32, (tm, 1), 0)
            mask = jnp.logical_and(rows >= n_meta, rows < t_real)
            diff = jnp.where(mask, xr * gfv - tgt_ref[...], 0.0)
            tail_ref[TAIL_LOSS:TAIL_LOSS + 1, :] += jnp.zeros((1, d), F32) + 0.5 * jnp.sum(diff * diff) / d
            dy = diff / d
            gy = dy * gfv
            out_ref[...] = r * (gy - xr * jnp.mean(gy * xr, axis=-1, keepdims=True))
            tail_ref[TAIL_FINAL:TAIL_FINAL + 1, :] += _rowsum(dy * xr)

    row = pl.BlockSpec((tm, d), lambda i: (i, 0))
    vec = pl.BlockSpec((1, d), lambda i: (0, 0))
    in_specs = [row, vec, _any(), _any()]
    out_shape = [jax.ShapeDtypeStruct((tp, d), F32), jax.ShapeDtypeStruct((2, tp, f), BF16),
                 jax.ShapeDtypeStruct((tp, d), BF16)]
    out_specs = [row, pl.BlockSpec((2, tm, f), lambda i: (0, i, 0)), row]
    args = [h, g, wgu, wd]
    if with_loss:
        in_specs += [row, vec]
        out_shape += [jax.ShapeDtypeStruct((SUBLANES, d), F32)]
        out_specs += [pl.BlockSpec((SUBLANES, d), lambda i: (0, 0))]
        args += [tgt, gf]
    return _call(body, "ffn_fwd_loss" if with_loss else "ffn_fwd", (nt,), in_specs, out_specs, out_shape,
                 [pltpu.VMEM((2 * f, d), BF16), pltpu.VMEM((f, d), BF16), pltpu.SemaphoreType.DMA((2,))],
                 args, comm)


def _ffn_bwd(dh, h, gu, g, wgu, wd, tm, tail, tail_row, after):
    tp, d = h.shape
    f = wd.shape[0]
    fc = f // FFN_CHUNKS
    nt = tp // tm

    def body(dh_ref, h_ref, gu_ref, g_ref, tail_ref, wgu_hbm, wd_hbm, after_ref,
             dhin_ref, dgu_ref, act_ref, df_ref, dg_ref, wgu_v, wd_v, dn_v, sems):
        del after_ref
        i, j = pl.program_id(0), pl.program_id(1)

        @pl.when(jnp.logical_and(i == 0, j == 0))
        def _():
            _load_weights([(wgu_hbm, wgu_v), (wd_hbm, wd_v)], sems)
            dg_ref[...] = tail_ref[...]

        dfb = (FFN_RES * dh_ref[...]).astype(BF16)

        @pl.when(j == 0)
        def _():
            df_ref[...] = dfb
            dn_v[...] = jnp.zeros_like(dn_v)

        lo = pl.multiple_of(j * fc, 16)
        dact = _nt(dfb, wd_v[pl.ds(lo, fc), :])
        gate = gu_ref[0].astype(F32)
        up = gu_ref[1].astype(F32)
        sg = _sigmoid(gate)
        silu = gate * sg
        act_ref[...] = (silu * up).astype(BF16)
        dgate = (dact * up * (sg * (1.0 + gate * (1.0 - sg)))).astype(BF16)
        dup = (dact * silu).astype(BF16)
        dgu_ref[0] = dgate
        dgu_ref[1] = dup
        dn_v[...] += _nn(dgate, wgu_v[pl.ds(lo, fc), :]) + _nn(dup, wgu_v[pl.ds(pl.multiple_of(f + j * fc, 16), fc), :])

        @pl.when(j == FFN_CHUNKS - 1)
        def _():
            x = h_ref[...]
            r = lax.rsqrt(jnp.mean(x * x, axis=-1, keepdims=True) + EPS)
            dx, dgp = _rms_bwd(dn_v[...], x, r, g_ref[...])
            dhin_ref[...] = dh_ref[...] + dx
            dg_ref[tail_row:tail_row + 1, :] += dgp

    row = pl.BlockSpec((tm, d), lambda i, j: (i, 0))
    vec = pl.BlockSpec((1, d), lambda i, j: (0, 0))
    tile = pl.BlockSpec((SUBLANES, d), lambda i, j: (0, 0))
    hid2 = pl.BlockSpec((2, tm, fc), lambda i, j: (0, i, j))
    return _call(
        body, "ffn_bwd", (nt, FFN_CHUNKS),
        [row, row, hid2, vec, tile, _any(), _any(), _any()],
        [row, hid2, pl.BlockSpec((tm, fc), lambda i, j: (i, j)), row, tile],
        [jax.ShapeDtypeStruct((tp, d), F32), jax.ShapeDtypeStruct((2, tp, f), BF16),
         jax.ShapeDtypeStruct((tp, f), BF16), jax.ShapeDtypeStruct((tp, d), BF16),
         jax.ShapeDtypeStruct((SUBLANES, d), F32)],
        [pltpu.VMEM((2 * f, d), BF16), pltpu.VMEM((f, d), BF16), pltpu.VMEM((tm, d), F32),
         pltpu.SemaphoreType.DMA((2,))],
        [dh, h, gu, g, tail, wgu, wd, after])


def _piece_segments(q, d, nb_cols):
    segs = []
    for j in range(N_DEV):
        lo, hi = max(q * d, j * nb_cols), min((q + 1) * d, (j + 1) * nb_cols)
        if lo < hi:
            segs.append((j, lo - q * d, hi - q * d, lo - j * nb_cols, hi - j * nb_cols))
    return segs


def _w3_copies(w3_hbm, rows, w3_v):
    return [(w3_hbm.at[k, pl.ds(q * rows, rows)], w3_v.at[q, pl.ds(k * rows, rows)])
            for q in range(3) for k in range(N_DEV)]


def _gates(xrb, wg_ref, ba, bx, lam, hd):
    pre_r, pre_i = [], []
    for hh in range(N_HEADS):
        xh = xrb[:, hh * hd:(hh + 1) * hd]
        pre_r.append(_nn(xh, wg_ref[0, hh]))
        pre_i.append(_nn(xh, wg_ref[1, hh]))
    r = _sigmoid(jnp.concatenate(pre_r, axis=1) + ba)
    ig = _sigmoid(jnp.concatenate(pre_i, axis=1) + bx)
    sp = _softplus(-lam)
    log_a = -RG_LRU_C * r * sp
    a = jnp.exp(log_a)
    s = jnp.sqrt(_one_minus_exp(2.0 * log_a))
    return r, ig, sp, a, s


def _scan_fwd(a, u, h_prev):
    tm = a.shape[0]
    rows = lax.broadcasted_iota(jnp.int32, a.shape, 0)
    d = 1
    while d < tm:
        if d < SUBLANES:
            keep = rows >= d
            u = jnp.where(keep, a * pltpu.roll(u, d, 0) + u, u)
            a = jnp.where(keep, a * pltpu.roll(a, d, 0), a)
        else:
            u = jnp.concatenate([u[:d], a[d:] * u[:tm - d] + u[d:]], axis=0)
            a = jnp.concatenate([a[:d], a[d:] * a[:tm - d]], axis=0)
        d *= 2
    return u + a * h_prev


def _scan_bwd(b, v, g_next):
    tm = b.shape[0]
    rows = lax.broadcasted_iota(jnp.int32, b.shape, 0)
    d = 1
    while d < tm:
        if d < SUBLANES:
            keep = rows < tm - d
            v = jnp.where(keep, v + b * pltpu.roll(v, tm - d, 0), v)
            b = jnp.where(keep, b * pltpu.roll(b, tm - d, 0), b)
        else:
            v = jnp.concatenate([v[:tm - d] + b[:tm - d] * v[d:], v[tm - d:]], axis=0)
            b = jnp.concatenate([b[:tm - d] * b[d:], b[tm - d:]], axis=0)
        d *= 2
    return v + b * g_next


def _shifted_copies(ext_ref, es_ref, n_rows):
    for s in range(1, SUBLANES):
        es_ref[s, pl.ds(0, n_rows), :] = ext_ref[pl.ds(s, n_rows), :]


def _tap(ext_ref, es_ref, off, tm):
    q, s = divmod(off, SUBLANES)
    if s == 0:
        return ext_ref[pl.ds(SUBLANES * q, tm), :]
    return es_ref[s, pl.ds(SUBLANES * q, tm), :]


def _mixer_fwd(h, g, b_in, win_all, cw4, cb4, wg, ba, bx, lam, cw31, cb31, lng, lnb, bcp, w3_all, tm, comm=None):
    tp, d = h.shape
    nb_cols = win_all.shape[-1]
    n_in = N_DEV * nb_cols
    hd = wg.shape[-1]
    k4, k31 = cw4.shape[0], cw31.shape[0]
    w3_rows = d // N_DEV

    def body(h_ref, g_ref, b_ref, win_hbm, cw4_ref, cb4_ref, wg_ref, ba_ref, bx_ref, lam_ref, cw31_ref, cb31_ref,
             lng_ref, lnb_ref, bcp_ref, w3_hbm,
             h2_ref, p_ref, n_ref, xr_ref, hs_ref, v1_ref, ya_ref, yb_ref,
             win_v, w3_v, ext4, ext31, es31, hcar, sems):
        @pl.when(pl.program_id(0) == 0)
        def _():
            _load_weights([(win_hbm, win_v)] + _w3_copies(w3_hbm, w3_rows, w3_v), sems)
            ext4[pl.ds(0, CONV4_HALO), :] = jnp.zeros((CONV4_HALO, d), F32)
            ext31[pl.ds(0, CONV31_HALO), :] = jnp.zeros((CONV31_HALO, d), F32)
            hcar[...] = jnp.zeros_like(hcar)

        n, _ = _rms_fwd(h_ref[...], g_ref[...])
        nb = n.astype(BF16)
        n_ref[...] = nb

        def piece(q):
            parts = [_nn(nb, win_v[j, :, bl:bh]) for j, _, _, bl, bh in _piece_segments(q, d, nb_cols)]
            pq = (jnp.concatenate(parts, axis=1) + b_ref[:, q * d:(q + 1) * d]).astype(BF16)
            p_ref[:, q * d:(q + 1) * d] = pq
            return pq.astype(F32)

        x_rnn, y_rnn, glu_v, glu_g, gate_a, gate_b = [piece(q) for q in range(6)]

        ext4[pl.ds(CONV4_HALO, tm), :] = x_rnn
        xr = cb4_ref[...] + jnp.zeros((tm, d), F32)
        for k in range(k4):
            xr = xr + cw4_ref[k:k + 1, :] * ext4[pl.ds(CONV4_HALO - (k4 - 1) + k, tm), :]
        ext4[pl.ds(0, CONV4_HALO), :] = ext4[pl.ds(tm, CONV4_HALO), :]
        xrb = xr.astype(BF16)
        xr_ref[...] = xrb
        xr = xrb.astype(F32)
        _, ig, _, a, s = _gates(xrb, wg_ref, ba_ref[...], bx_ref[...], lam_ref[...], hd)
        hseq = _scan_fwd(a, s * (ig * xr), hcar[0:1, :])
        hcar[0:1, :] = hseq[tm - 1:tm, :]
        hs_ref[...] = hseq.astype(BF16)
        gl, _ = _gelu(y_rnn)
        ya = _nn((hseq * gl).astype(BF16), w3_v[0])
        ya_ref[...] = ya.astype(BF16)

        ext31[pl.ds(CONV31_HALO, tm), :] = glu_v * _sigmoid(glu_g)
        _shifted_copies(ext31, es31, tm + CONV31_HALO - SUBLANES)
        v1 = cb31_ref[...] + jnp.zeros((tm, d), F32)
        for k in range(k31):
            v1 = v1 + cw31_ref[k:k + 1, :] * _tap(ext31, es31, CONV31_HALO - (k31 - 1) + k, tm)
        ext31[pl.ds(0, CONV31_HALO), :] = ext31[pl.ds(tm, CONV31_HALO), :]
        v1b = v1.astype(BF16)
        v1_ref[...] = v1b
        v1 = v1b.astype(F32)
        xc = v1 - jnp.mean(v1, axis=-1, keepdims=True)
        rstd = lax.rsqrt(jnp.mean(xc * xc, axis=-1, keepdims=True) + EPS)
        v2 = xc * rstd * lng_ref[...] + lnb_ref[...]
        yb = _nn((v2 * _sigmoid(v2)).astype(BF16), w3_v[1]) + bcp_ref[...]
        yb_ref[...] = yb.astype(BF16)

        merged = _sigmoid(gate_a) * ya + _sigmoid(gate_b) * yb
        h2_ref[...] = h_ref[...] + _nn(merged.astype(BF16), w3_v[2])

    row = pl.BlockSpec((tm, d), lambda i: (i, 0))
    wide = pl.BlockSpec((tm, n_in), lambda i: (i, 0))
    full = lambda a: pl.BlockSpec(a.shape, lambda i, nd=a.ndim: (0,) * nd)
    smalls = [cw4, cb4, wg, ba, bx, lam, cw31, cb31, lng, lnb, bcp]
    return _call(
        body, "mixer_fwd", (tp // tm,),
        [row, full(g), full(b_in), _any()] + [full(a) for a in smalls] + [_any()],
        [row, wide] + [row] * 6,
        [jax.ShapeDtypeStruct((tp, d), F32), jax.ShapeDtypeStruct((tp, n_in), BF16)]
        + [jax.ShapeDtypeStruct((tp, d), BF16)] * 6,
        [pltpu.VMEM(win_all.shape, BF16),
         pltpu.VMEM((3, d, d), BF16),
         pltpu.VMEM((tm + CONV4_HALO, d), F32),
         pltpu.VMEM((tm + CONV31_HALO, d), F32),
         pltpu.VMEM((SUBLANES, tm + CONV31_HALO, d), F32),
         pltpu.VMEM((SUBLANES, d), F32),
         pltpu.SemaphoreType.DMA((1 + 3 * N_DEV,))],
        [h, g, b_in, win_all, *smalls, w3_all], comm)


SG_BIN, SG_CW4, SG_CB4, SG_BA, SG_BX, SG_LAM, SG_CB31, SG_LNG, SG_LNB, SG_BCP, SG_MIX, SG_CW31 = 0, 6, 10, 11, 12, 13, 14, 15, 16, 17, 18, 19


def _mixer_bwd(dh2, h, g, proj, xr_s, hs_s, v1_s, ya_s, yb_s, win_t, cw4, wg, ba, bx, lam, cw31, lng, lnb, w3_all, tm,
               comm=None):
    tp, d = dh2.shape
    n_in = proj.shape[1]
    hd = wg.shape[-1]
    k4, k31 = cw4.shape[0], cw31.shape[0]
    nt = tp // tm
    w3_rows = d // N_DEV
    sg_rows = -(-(SG_CW31 + k31) // SUBLANES) * SUBLANES
    halo_rows = 16
    per = tm // halo_rows

    def body(dh_ref, h_ref, g_ref, p_ref, xr_ref, hs_ref, hh_ref, v1_ref, ya_ref, yb_ref, win_hbm,
             cw4_ref, wg_ref, wgt_ref, ba_ref, bx_ref, lam_ref, cw31_ref, lng_ref, lnb_ref, w3_hbm,
             dh1_ref, dp_ref, x3_ref, y3_ref, yg_ref, sg_ref,
             win_v, w3_v, extd4, extd31, es31, gcar, sems):
        i = pl.program_id(0)
        tile = nt - 1 - i

        @pl.when(i == 0)
        def _():
            _load_weights([(win_hbm, win_v)] + _w3_copies(w3_hbm, w3_rows, w3_v), sems)
            for q in range(3):
                w3_v[q] = w3_v[q].T
            extd4[pl.ds(tm, CONV4_HALO), :] = jnp.zeros((CONV4_HALO, d), F32)
            extd31[pl.ds(tm, CONV31_HALO), :] = jnp.zeros((CONV31_HALO, d), F32)
            gcar[...] = jnp.zeros_like(gcar)
            sg_ref[...] = jnp.zeros_like(sg_ref)

        def acc(row, val):
            sg_ref[row:row + 1, :] += _rowsum(val)

        rows = lax.broadcasted_iota(jnp.int32, (tm, d), 0)
        x_rnn = p_ref[:, 0:d].astype(F32)
        y_rnn = p_ref[:, d:2 * d].astype(F32)
        glu_v = p_ref[:, 2 * d:3 * d].astype(F32)
        glu_g = p_ref[:, 3 * d:4 * d].astype(F32)
        sga = _sigmoid(p_ref[:, 4 * d:5 * d].astype(F32))
        sgb = _sigmoid(p_ref[:, 5 * d:6 * d].astype(F32))
        ya = ya_ref[...].astype(F32)
        yb = yb_ref[...].astype(F32)

        dmob = dh_ref[...].astype(BF16)
        dmerged = _nn(dmob, w3_v[2])
        x3_ref[:, 0:d] = (sga * ya + sgb * yb).astype(BF16)
        y3_ref[:, 0:d] = dmob
        dya = sga * dmerged
        dyb = sgb * dmerged
        dn_parts = []

        def emit(q, val):
            vb = val.astype(BF16)
            dp_ref[:, q * d:(q + 1) * d] = vb
            acc(SG_BIN + q, val)
            term = _nn(vb, win_v[pl.ds(q * d, d), :])
            dn_parts[:] = [term if not dn_parts else dn_parts[0] + term]

        emit(4, dmerged * ya * sga * (1.0 - sga))
        emit(5, dmerged * yb * sgb * (1.0 - sgb))

        dyab = dya.astype(BF16)
        y3_ref[:, d:2 * d] = dyab
        dza = _nn(dyab, w3_v[0])
        hsv = hs_ref[...].astype(F32)
        gl, th = _gelu(y_rnn)
        x3_ref[:, d:2 * d] = (hsv * gl).astype(BF16)
        emit(1, dza * hsv * _gelu_grad(y_rnn, th))
        dhs = dza * gl
        xrb = xr_ref[...]
        xr = xrb.astype(F32)
        lam_v = lam_ref[...]
        r, ig, sp, a, s = _gates(xrb, wg_ref, ba_ref[...], bx_ref[...], lam_v, hd)
        b = jnp.where(rows == tm - 1, gcar[1:2, :], pltpu.roll(a, tm - 1, 0))
        big_g = _scan_bwd(b, dhs, gcar[0:1, :])
        gcar[0:1, :] = big_g[0:1, :]
        gcar[1:2, :] = a[0:1, :]
        h_before = jnp.where(tile > 0, hh_ref[halo_rows - 1:halo_rows, :].astype(F32), 0.0)
        h_prev = jnp.where(rows == 0, h_before, pltpu.roll(hsv, 1, 0))
        ds = big_g * ig * xr
        dla = big_g * h_prev * a - ds * (a * a) / jnp.maximum(s, 1e-20)
        acc(SG_LAM, dla * r * (RG_LRU_C * _sigmoid(-lam_v)))
        dpr = dla * (-RG_LRU_C * sp) * r * (1.0 - r)
        dpi = big_g * s * xr * ig * (1.0 - ig)
        acc(SG_BA, dpr)
        acc(SG_BX, dpi)
        dprb = dpr.astype(BF16)
        dpib = dpi.astype(BF16)
        yg_ref[:, 0:d] = dprb
        yg_ref[:, d:2 * d] = dpib
        back = []
        for hh in range(N_HEADS):
            sl = slice(hh * hd, (hh + 1) * hd)
            back.append(_nn(dprb[:, sl], wgt_ref[0, hh]) + _nn(dpib[:, sl], wgt_ref[1, hh]))
        dxr = big_g * s * ig + jnp.concatenate(back, axis=1)
        acc(SG_CB4, dxr)
        extd4[pl.ds(0, tm), :] = dxr
        dx_rnn = jnp.zeros((tm, d), F32)
        for k in range(k4):
            term = extd4[pl.ds(k4 - 1 - k, tm), :]
            dx_rnn = dx_rnn + cw4_ref[k:k + 1, :] * term
            acc(SG_CW4 + k, x_rnn * term)
        extd4[pl.ds(tm, CONV4_HALO), :] = extd4[pl.ds(0, CONV4_HALO), :]
        emit(0, dx_rnn)

        dybb = dyb.astype(BF16)
        y3_ref[:, 2 * d:3 * d] = dybb
        acc(SG_BCP, dyb)
        dv3 = _nn(dybb, w3_v[1])
        v1 = v1_ref[...].astype(F32)
        xc = v1 - jnp.mean(v1, axis=-1, keepdims=True)
        rstd = lax.rsqrt(jnp.mean(xc * xc, axis=-1, keepdims=True) + EPS)
        xhat = xc * rstd
        lng_v = lng_ref[...]
        v2 = xhat * lng_v + lnb_ref[...]
        s2 = _sigmoid(v2)
        x3_ref[:, 2 * d:3 * d] = (v2 * s2).astype(BF16)
        dv2 = dv3 * (s2 * (1.0 + v2 * (1.0 - s2)))
        acc(SG_LNG, dv2 * xhat)
        acc(SG_LNB, dv2)
        dxh = dv2 * lng_v
        dv1 = rstd * (dxh - jnp.mean(dxh, axis=-1, keepdims=True)
                      - xhat * jnp.mean(dxh * xhat, axis=-1, keepdims=True))
        acc(SG_CB31, dv1)
        extd31[pl.ds(0, tm), :] = dv1
        _shifted_copies(extd31, es31, tm + CONV31_HALO - SUBLANES)
        sgg = _sigmoid(glu_g)
        v0 = glu_v * sgg
        dv0 = jnp.zeros((tm, d), F32)
        for k in range(k31):
            term = _tap(extd31, es31, k31 - 1 - k, tm)
            dv0 = dv0 + cw31_ref[k:k + 1, :] * term
            acc(SG_CW31 + k, v0 * term)
        extd31[pl.ds(tm, CONV31_HALO), :] = extd31[pl.ds(0, CONV31_HALO), :]
        emit(2, dv0 * sgg)
        emit(3, dv0 * glu_v * sgg * (1.0 - sgg))

        dn = dn_parts[0]
        x = h_ref[...]
        rr = lax.rsqrt(jnp.mean(x * x, axis=-1, keepdims=True) + EPS)
        dx, dgp = _rms_bwd(dn, x, rr, g_ref[...])
        dh1_ref[...] = dh_ref[...] + dx
        sg_ref[SG_MIX:SG_MIX + 1, :] += dgp

    rev = lambda i: (nt - 1 - i, 0)
    row = pl.BlockSpec((tm, d), rev)
    wide = pl.BlockSpec((tm, n_in), rev)
    full = lambda a: pl.BlockSpec(a.shape, lambda i, nd=a.ndim: (0,) * nd)
    halo = pl.BlockSpec((halo_rows, d), lambda i: (jnp.maximum((nt - 1 - i) * per - 1, 0), 0))
    smalls = [cw4, wg, jnp.swapaxes(wg, 2, 3), ba, bx, lam, cw31, lng, lnb]
    return _call(
        body, "mixer_bwd", (nt,),
        [row, row, full(g), wide, row, row, halo, row, row, row, _any()]
        + [full(a) for a in smalls] + [_any()],
        [row, wide, pl.BlockSpec((tm, 3 * d), rev), pl.BlockSpec((tm, 3 * d), rev),
         pl.BlockSpec((tm, 2 * d), rev), pl.BlockSpec((sg_rows, d), lambda i: (0, 0))],
        [jax.ShapeDtypeStruct((tp, d), F32), jax.ShapeDtypeStruct((tp, n_in), BF16),
         jax.ShapeDtypeStruct((tp, 3 * d), BF16), jax.ShapeDtypeStruct((tp, 3 * d), BF16),
         jax.ShapeDtypeStruct((tp, 2 * d), BF16), jax.ShapeDtypeStruct((sg_rows, d), F32)],
        [pltpu.VMEM(win_t.shape, BF16),
         pltpu.VMEM((3, d, d), BF16),
         pltpu.VMEM((tm + CONV4_HALO, d), F32),
         pltpu.VMEM((tm + CONV31_HALO, d), F32),
         pltpu.VMEM((SUBLANES, tm + CONV31_HALO, d), F32),
         pltpu.VMEM((SUBLANES, d), F32),
         pltpu.SemaphoreType.DMA((1 + 3 * N_DEV,))],
        [dh2, h, g, proj, xr_s, hs_s, hs_s, v1_s, ya_s, yb_s, win_t, *smalls, w3_all], comm)


def _tn_matmul(name, x, y, x_spec, y_spec, n_blocks, kb, nb, tm, tp, out_shape, out_spec, out_view, comm=None,
               after=None):
    nt = tp // tm

    def body(x_ref, y_ref, *refs):
        o_ref, acc = refs[-2:]
        i = pl.program_id(1)

        @pl.when(i == 0)
        def _():
            acc[...] = jnp.zeros_like(acc)

        acc[...] += _tn(x_ref[...], y_ref[...])

        @pl.when(i == nt - 1)
        def _():
            o_ref[...] = acc[...].astype(BF16).reshape(out_view)

    follows = [] if after is None else [after]
    outs, extra = _call(body, name, (n_blocks, nt), [x_spec, y_spec] + [_any()] * len(follows), [out_spec],
                        [jax.ShapeDtypeStruct(out_shape, BF16)], [pltpu.VMEM((kb, nb), F32)], [x, y] + follows,
                        comm)
    return outs[0], extra


def kernel(x, meta_tokens, ffn1_norm, ffn1_w_gu, ffn1_w_down, mix_norm, w_in, b_in, rnn_conv_w, rnn_conv_b, rg_w_a, rg_b_a, rg_w_x, rg_b_x, rg_lambda, rnn_w_proj, conv_dw_w, conv_dw_b, conv_ln_g, conv_ln_b, conv_w_proj, conv_b_proj, w_out, ffn2_norm, ffn2_w_gu, ffn2_w_down, final_norm, loss_target, m_meta_tokens, m_ffn1_norm, m_ffn1_w_gu, m_ffn1_w_down, m_mix_norm, m_w_in, m_b_in, m_rnn_conv_w, m_rnn_conv_b, m_rg_w_a, m_rg_b_a, m_rg_w_x, m_rg_b_x, m_rg_lambda, m_rnn_w_proj, m_conv_dw_w, m_conv_dw_b, m_conv_ln_g, m_conv_ln_b, m_conv_w_proj, m_conv_b_proj, m_w_out, m_ffn2_norm, m_ffn2_w_gu, m_ffn2_w_down, m_final_norm, v_meta_tokens, v_ffn1_norm, v_ffn1_w_gu, v_ffn1_w_down, v_mix_norm, v_w_in, v_b_in, v_rnn_conv_w, v_rnn_conv_b, v_rg_w_a, v_rg_b_a, v_rg_w_x, v_rg_b_x, v_rg_lambda, v_rnn_w_proj, v_conv_dw_w, v_conv_dw_b, v_conv_ln_g, v_conv_ln_b, v_conv_w_proj, v_conv_b_proj, v_w_out, v_ffn2_norm, v_ffn2_w_gu, v_ffn2_w_down, v_final_norm):
    w = dict(locals())
    seq, d = x.shape[1], x.shape[2]
    n_meta = meta_tokens.shape[0]
    t_real = n_meta + seq
    tp, tm, tmx_fwd, tmx, tmt, tmw = _tiles(t_real)
    fb = ffn1_w_gu.shape[-1]
    wr = ffn1_w_down.shape[1]
    f = N_DEV * wr
    fc = f // FFN_CHUNKS
    nbc = w_in.shape[-1]
    n_in = N_DEV * nbc
    pr = rnn_w_proj.shape[1]
    hd = rg_w_a.shape[-1]
    gr = rg_w_a.shape[2]
    cw = meta_tokens.shape[1]
    k4, k31 = rnn_conv_w.shape[1], conv_dw_w.shape[1]
    assert n_in == 6 * d and 2 * wr == fb and N_HEADS * hd == d and pr * N_DEV == d

    xi, yi, ci = lax.axis_index("x"), lax.axis_index("y"), lax.axis_index("c")
    core = ci.astype(jnp.int32).reshape(1)
    chip = (2 * xi + yi).astype(jnp.int32).reshape(1)
    me_index = (4 * xi + 2 * yi + ci).astype(jnp.int32).reshape(1)

    for nm in ("ffn1_w_gu", "ffn2_w_gu"):
        for pre in ("", "m_", "v_"):
            w[pre + nm] = jnp.swapaxes(w[pre + nm], 1, 2)

    wgut1 = w["ffn1_w_gu"][0].astype(BF16)
    wgut2 = w["ffn2_w_gu"][0].astype(BF16)
    wd1 = ffn1_w_down[0].astype(BF16)
    wd2 = ffn2_w_down[0].astype(BF16)
    win_loc = w_in[0].astype(BF16)
    win_t_loc = jnp.swapaxes(w_in[0], 0, 1).astype(BF16)
    w3_loc = jnp.concatenate([rnn_w_proj[0], conv_w_proj[0], w_out[0]], axis=0).astype(BF16)
    wg_loc = jnp.stack([rg_w_a[0], rg_w_x[0]]).astype(BF16)
    n_small = n_meta + k4 + k31
    small_rows = -(-n_small // SUBLANES) * SUBLANES
    small_loc = jnp.concatenate([meta_tokens, rnn_conv_w[0], conv_dw_w[0],
                                 jnp.zeros((small_rows - n_small, cw), F32)], axis=0)
    (wgut1_all, wd1_all, small_all), h0, tgt = _first_gather(
        [wgut1, wd1, small_loc], 2, x[0], loss_target[0], n_meta, tp)
    small_full = small_all.transpose(1, 0, 2).reshape(small_rows, d)
    cw4 = small_full[n_meta:n_meta + k4]
    cw31 = small_full[n_meta + k4:n_meta + k4 + k31]

    wgu1, wdn1 = wgut1_all.reshape(2 * f, d), wd1_all.reshape(f, d)
    (h1, gu1, n1), (win_all, w3_all, wg_all) = _ffn_fwd(
        h0, ffn1_norm, wgu1, wdn1, tm, comm=_Gather([win_loc, w3_loc, wg_loc], pass_on_at=(0.65, 0.95)))
    wg = wg_all.transpose(1, 2, 0, 3, 4).reshape(2, N_HEADS, hd, hd)
    (h2, proj, n2, xr_s, hs_s, v1_s, ya_s, yb_s), (wgut2_all, wd2_all) = _mixer_fwd(
        h1, mix_norm, b_in, win_all, cw4, rnn_conv_b, wg, rg_b_a, rg_b_x, rg_lambda, cw31, conv_dw_b, conv_ln_g,
        conv_ln_b, conv_b_proj, w3_all, tmx_fwd, comm=_Gather([wgut2, wd2], pass_on_at=(0.3, 0.5)))
    wgu2, wdn2 = wgut2_all.reshape(2 * f, d), wd2_all.reshape(f, d)
    (dh3, gu2, n3, tail), (win_t_all,) = _ffn_fwd(
        h2, ffn2_norm, wgu2, wdn2, tm, loss=(tgt, final_norm.reshape(1, d), n_meta, t_real),
        comm=_Gather([win_t_loc], pass_on_at=(0.45, 0.75)))
    win_t = win_t_all.reshape(n_in, d)

    def d_w_gu(tag, dgu, n_s, comm=None):
        g, extra = _tn_matmul(
            "d_w_gu" + tag, dgu, n_s,
            pl.BlockSpec((None, tmt, fc), lambda b, i: (b // FFN_CHUNKS, i, b % FFN_CHUNKS)),
            pl.BlockSpec((tmt, d), lambda b, i: (i, 0)),
            2 * FFN_CHUNKS, fc, d, tmt, tp, (2 * FFN_CHUNKS, fc, d),
            pl.BlockSpec((None, fc, d), lambda b, i: (b, 0, 0)), (fc, d), comm)
        return g.reshape(N_DEV, fb, d), extra

    def d_w_down(tag, act, df, comm=None):
        g, extra = _tn_matmul(
            "d_w_down" + tag, act, df,
            pl.BlockSpec((tmt, fc), lambda b, i: (i, b)), pl.BlockSpec((tmt, d), lambda b, i: (i, 0)),
            FFN_CHUNKS, fc, d, tmt, tp, (FFN_CHUNKS, fc, d),
            pl.BlockSpec((None, fc, d), lambda b, i: (b, 0, 0)), (fc, d), comm)
        return g.reshape(N_DEV, wr, d), extra

    (dh2, dgu2, act2, df2, tail), _ = _ffn_bwd(dh3, h2, gu2, ffn2_norm, wgu2, wdn2, tm, tail, TAIL_FFN2, n3)
    g_wgu2, _ = d_w_gu("2", dgu2, n3)
    g_wd2, _ = d_w_down("2", act2, df2)
    (dh1, dproj, x3, y3, yg, sg), (r_wd2, r_wgu2) = _mixer_bwd(
        dh2, h1, mix_norm, proj, xr_s, hs_s, v1_s, ya_s, yb_s, win_t, cw4, wg, rg_b_a, rg_b_x, rg_lambda, cw31,
        conv_ln_g, conv_ln_b, w3_all, tmx, comm=_Scatter([g_wd2, g_wgu2]))
    g_w3, _ = _tn_matmul(
        "d_w_proj3", x3, y3,
        pl.BlockSpec((tmw, d), lambda b, i: (i, b)), pl.BlockSpec((tmw, d), lambda b, i: (i, b)),
        3, d, d, tmw, tp, (N_DEV, 3, pr, d), pl.BlockSpec((N_DEV, None, pr, d), lambda b, i: (0, b, 0, 0)),
        (N_DEV, pr, d))
    g_wg, _ = _tn_matmul(
        "d_w_gates", xr_s, yg,
        pl.BlockSpec((tmw, hd), lambda b, i: (i, b % N_HEADS)), pl.BlockSpec((tmw, hd), lambda b, i: (i, b)),
        2 * N_HEADS, hd, hd, tmw, tp, (N_DEV, 2 * N_HEADS, gr, hd),
        pl.BlockSpec((N_DEV, None, gr, hd), lambda b, i: (0, b, 0, 0)), (N_DEV, gr, hd))
    w3_sems, g_w3_thru, w3_land, w3_token = _exchange_start("grads_proj3_exchange", _scatter_copies, N_DEV - 1, g_w3)
    g_win, (r_wg,) = _tn_matmul(
        "d_w_in", n2, dproj,
        pl.BlockSpec((tmw, d), lambda b, i: (i, 0)), pl.BlockSpec((tmw, nbc), lambda b, i: (i, b)),
        N_DEV, d, nbc, tmw, tp, (N_DEV, d, nbc), pl.BlockSpec((None, d, nbc), lambda b, i: (b, 0, 0)), (d, nbc),
        comm=_Scatter([g_wg]), after=w3_token)
    win_sems, g_win_thru, win_land, win_token = _exchange_start("grads_w_in_exchange", _scatter_copies, N_DEV - 1, g_win)
    (dh0, dgu1, act1, df1, tail), _ = _ffn_bwd(dh1, h0, gu1, ffn1_norm, wgu1, wdn1, tm, tail, TAIL_FFN1, win_token)

    pieces = [sg, dh0[:n_meta], tail]
    assert all(p.shape[0] % SUBLANES == 0 for p in pieces)
    at = [0, sg.shape[0], sg.shape[0] + n_meta]
    loss_row = at[2] + TAIL_LOSS
    rep_rows = [("ffn1_norm", at[2] + TAIL_FFN1, 1), ("mix_norm", SG_MIX, 1), ("b_in", SG_BIN, 6),
                ("rnn_conv_b", SG_CB4, 1),
                ("rg_b_a", SG_BA, 1), ("rg_b_x", SG_BX, 1), ("rg_lambda", SG_LAM, 1), ("conv_dw_b", SG_CB31, 1),
                ("conv_ln_g", SG_LNG, 1), ("conv_ln_b", SG_LNB, 1), ("conv_b_proj", SG_BCP, 1),
                ("ffn2_norm", at[2] + TAIL_FFN2, 1), ("final_norm", at[2] + TAIL_FINAL, 1)]
    col_rows = [("meta_tokens", at[1], n_meta), ("rnn_conv_w", SG_CW4, k4), ("conv_dw_w", SG_CW31, k31)]
    layout = []
    for nm, row0, nr in rep_rows:
        kind = "wide" if nm == "b_in" else "rep"
        as2d = lambda a: a.reshape(1, -1) if a.ndim == 1 else a
        layout.append((kind, row0, nr, as2d(w[nm]), as2d(w["m_" + nm]), as2d(w["v_" + nm])))
    for nm, row0, nr in col_rows:
        sq = lambda a: a.reshape(a.shape[-2], a.shape[-1])
        layout.append(("col", row0, nr, sq(w[nm]), sq(w["m_" + nm]), sq(w["v_" + nm])))
    small_partial = jnp.concatenate(pieces, axis=0)

    g_wd1, (small_partials,) = d_w_down("1", act1, df1, comm=_Bcast(small_partial))
    g_wgu1, (r_wd1,) = d_w_gu("1", dgu1, n1, comm=_Scatter([g_wd1]))

    g_last = g_wgu1.reshape((4, 2) + g_wgu1.shape[1:])
    comb_wgu1 = _pair_reduce(g_last, core)
    sems, comb_thru, land_thru, after = _exchange_start("grads_chip_exchange", _chip_copies, 3, comb_wgu1)
    g_win, r_win = _exchange_wait("grads_w_in_exchange", _scatter_copies, win_sems, g_win_thru, win_land, after)
    g_w3, r_w3 = _exchange_wait("grads_proj3_exchange", _scatter_copies, w3_sems, g_w3_thru, w3_land, after)

    grad_x = (dh0[n_meta:t_real] + after[0, 0])[None]
    total, small_out = _small_adamw(small_partials, layout, me_index, grad_x)
    after = total

    groups = [(g_wd1, r_wd1, me_index, ["ffn1_w_down"]),
              (g_wd2, r_wd2, me_index, ["ffn2_w_down"]), (g_wgu2, r_wgu2, me_index, ["ffn2_w_gu"]),
              (g_win, r_win, me_index, ["w_in"]), (g_w3, r_w3, me_index, ["w_out", "rnn_w_proj", "conv_w_proj"]),
              (g_wg, r_wg, me_index, ["rg_w_a", "rg_w_x"]), (None, None, chip, ["ffn1_w_gu"])]
    res = {}
    for own, recv, idx, group in groups:
        if own is None:
            own, recv = _exchange_wait("grads_chip_exchange", _chip_copies, sems, comb_thru, land_thru, after)
        outs = _final_adamw(own, recv, idx, [(w[nm], w["m_" + nm], w["v_" + nm]) for nm in group], after)
        after = outs[-1][0]
        for nm, o in zip(group, outs):
            res[nm] = o
    for nm in ("ffn1_w_gu", "ffn2_w_gu"):
        res[nm] = tuple(jnp.swapaxes(a, 1, 2) for a in res[nm])
    for (nm, _, _), o in zip(rep_rows + col_rows, small_out):
        res[nm] = tuple(a.reshape(w[nm].shape) for a in o)


    order = ["meta_tokens", "ffn1_norm", "ffn1_w_gu", "ffn1_w_down", "mix_norm", "w_in", "b_in", "rnn_conv_w",
             "rnn_conv_b", "rg_w_a", "rg_b_a", "rg_w_x", "rg_b_x", "rg_lambda", "rnn_w_proj", "conv_dw_w",
             "conv_dw_b", "conv_ln_g", "conv_ln_b", "conv_w_proj", "conv_b_proj", "w_out", "ffn2_norm",
             "ffn2_w_gu", "ffn2_w_down", "final_norm"]
    return (total[loss_row, 0], grad_x, *[res[nm][0] for nm in order], *[res[nm][1] for nm in order],
            *[res[nm][2] for nm in order], *[res[nm][3] for nm in order])
```

```python
import functools
import math

import jax
import jax.numpy as jnp
from jax import lax
from jax.experimental import pallas as pl
from jax.experimental.pallas import tpu as pltpu

F32 = jnp.float32
BF16 = jnp.bfloat16
MESH = pl.DeviceIdType.MESH
N_DEV = 8
N_HEADS = 4
RG_LRU_C = 8.0
EPS = 1e-6
FFN_RES = 0.5
ADAM_LR, ADAM_B1, ADAM_B2, ADAM_EPS, ADAM_WD, ADAM_STEP = 0.001, 0.9, 0.999, 1e-08, 0.01, 10
V7X_VMEM_LIMIT = 56 * 1024 * 1024
CONV4_HALO = 8
CONV31_HALO = 32
SUBLANES = 8
STAGE_ROWS = 512
TAIL_FFN1, TAIL_FINAL, TAIL_LOSS, TAIL_FFN2 = 0, 1, 2, 3
FFN_CHUNKS = 2
FFN_FWD_CHUNKS = 1
FFN_BWD_CHUNKS = 1
GELU_C = math.sqrt(2.0 / math.pi)
GELU_K = 0.044715


def _any():
    return pl.BlockSpec(memory_space=pl.ANY)


def _params(n_grid):
    return pltpu.CompilerParams(dimension_semantics=("arbitrary",) * n_grid, vmem_limit_bytes=V7X_VMEM_LIMIT)


def _nn(a, b):
    return jnp.dot(a, b, preferred_element_type=F32)


def _nt(a, b):
    return lax.dot_general(a, b, (((1,), (1,)), ((), ())), preferred_element_type=F32)


def _tn(a, b):
    return lax.dot_general(a, b, (((0,), (0,)), ((), ())), preferred_element_type=F32)


def _sigmoid(x):
    return 0.5 * jnp.tanh(0.5 * x) + 0.5


def _rowsum(x):
    return jnp.sum(x, axis=0, keepdims=True)


def _rms_fwd(x, g):
    r = lax.rsqrt(jnp.mean(x * x, axis=-1, keepdims=True) + EPS)
    return x * r * g, r


def _rms_bwd(dn, x, r, g):
    xr = x * r
    gy = dn * g
    dx = r * (gy - xr * jnp.mean(gy * xr, axis=-1, keepdims=True))
    return dx, _rowsum(dn * xr)


def _gelu(y):
    t = jnp.tanh(GELU_C * (y + GELU_K * y * y * y))
    return 0.5 * y * (1.0 + t), t


def _gelu_grad(y, t):
    return 0.5 * (1.0 + t) + 0.5 * y * (1.0 - t * t) * GELU_C * (1.0 + 3.0 * GELU_K * y * y)


def _softplus(x):
    return jnp.maximum(x, 0.0) + jnp.log(1.0 + jnp.exp(-jnp.abs(x)))


def _one_minus_exp(z):
    series = -z * (1.0 + 0.5 * z * (1.0 + z * (1.0 / 3.0) * (1.0 + 0.25 * z)))
    return jnp.where(z > -0.05, series, 1.0 - jnp.exp(z))


def _tiles(t_real):
    if t_real > 2048:
        tm = 416
        tp = -(-t_real // tm) * tm
        return tp, tm, tm // 2, tm // 2, tp, tp
    tm = 128
    tp = -(-t_real // tm) * tm
    return tp, tm, tm // 2, tm // 2, tm, tm


def _load_weights(copies, sems):
    cps = [pltpu.make_async_copy(s, d, sems.at[k]) for k, (s, d) in enumerate(copies)]
    for cp in cps:
        cp.start()
    for cp in cps:
        cp.wait()


def _position():
    x, y, c = lax.axis_index("x"), lax.axis_index("y"), lax.axis_index("c")
    chips = [(1 - x, y), (x, 1 - y), (1 - x, 1 - y)]
    return x, y, c, chips


def _slot(p):
    return 4 * p[0] + 2 * p[1] + p[2]


class _Lazy(dict):
    def __getitem__(self, key):
        val = dict.__getitem__(self, key)
        return val() if callable(val) else val


class _Gather:
    def __init__(self, shards, pass_on_at=None):
        self.shards = list(shards)
        self.n = len(self.shards)
        self.pass_on_at = pass_on_at

    def inputs(self):
        return self.shards

    def out_shape(self):
        return [jax.ShapeDtypeStruct((N_DEV,) + s.shape, s.dtype) for s in self.shards]

    N_SEMS = 9

    def scratch(self):
        return [pltpu.SemaphoreType.DMA((self.N_SEMS * self.n,)), pltpu.SemaphoreType.DMA((self.N_SEMS * self.n,)),
                pltpu.SemaphoreType.DMA((self.n,))]

    def _plan(self, ins, outs, sems):
        send_sems, recv_sems, local_sems = sems
        x, y, c, _ = _position()
        me, sib, xn, yn, dg = (x, y, c), (x, y, 1 - c), (1 - x, y, c), (x, 1 - y, c), (1 - x, 1 - y, c)
        other = lambda p: (p[0], p[1], 1 - c)

        def blk(a, p, half=None):
            ref = outs[a].at[_slot(p)]
            if half is None:
                return ref
            rows = self.shards[a].shape[0] // 2
            return ref.at[pl.ds(half * rows, rows)]

        def copy(a, k, dst, to, src=None):
            return pltpu.make_async_remote_copy(
                src_ref=dst if src is None else src, dst_ref=dst,
                send_sem=send_sems.at[self.N_SEMS * a + k], recv_sem=recv_sems.at[self.N_SEMS * a + k],
                device_id=to, device_id_type=MESH)

        cp = _Lazy(mine=lambda: [pltpu.make_async_copy(ins[a], blk(a, me), local_sems.at[a]) for a in range(self.n)])
        for a in range(self.n):
            cp[a] = _Lazy(
                own=lambda a=a: [copy(a, 0, blk(a, me), sib, src=ins[a]), copy(a, 1, blk(a, me), xn, src=ins[a]),
                                 copy(a, 2, blk(a, me), yn, src=ins[a])],
                from_x=lambda a=a: copy(a, 1, blk(a, xn), me), from_y=lambda a=a: copy(a, 2, blk(a, yn), me),
                relay_x=lambda a=a: copy(a, 3, blk(a, xn, 0), yn), relay_y=lambda a=a: copy(a, 4, blk(a, yn, 1), xn),
                diag0=lambda a=a: copy(a, 3, blk(a, dg, 0), me), diag1=lambda a=a: copy(a, 4, blk(a, dg, 1), me),
                pass_x=lambda a=a: copy(a, 5, blk(a, xn), sib), pass_y=lambda a=a: copy(a, 6, blk(a, yn), sib),
                pass_d0=lambda a=a: copy(a, 7, blk(a, dg, 0), sib), pass_d1=lambda a=a: copy(a, 8, blk(a, dg, 1), sib),
                from_sib=lambda a=a: [copy(a, 0, blk(a, sib), me), copy(a, 5, blk(a, other(xn)), me),
                                      copy(a, 6, blk(a, other(yn)), me), copy(a, 7, blk(a, other(dg), 0), me),
                                      copy(a, 8, blk(a, other(dg), 1), me)])
        return cp

    def start(self, ins, outs, sems):
        cp = self._plan(ins, outs, sems)
        for c in cp["mine"]:
            c.start()
        for a in range(self.n):
            for c in cp[a]["own"]:
                c.start()

    def pass_on(self, ins, outs, sems):
        cp = self._plan(ins, outs, sems)
        for a in range(self.n):
            cp[a]["from_x"].wait_recv()
            cp[a]["relay_x"].start()
            cp[a]["pass_x"].start()
        for a in range(self.n):
            cp[a]["from_y"].wait_recv()
            cp[a]["relay_y"].start()
            cp[a]["pass_y"].start()

    def pass_on_relayed(self, ins, outs, sems):
        cp = self._plan(ins, outs, sems)
        for a in range(self.n):
            cp[a]["diag0"].wait_recv()
            cp[a]["pass_d0"].start()
            cp[a]["diag1"].wait_recv()
            cp[a]["pass_d1"].start()

    def finish(self, ins, outs, sems):
        if self.pass_on_at is None:
            self.pass_on(ins, outs, sems)
            self.pass_on_relayed(ins, outs, sems)
        cp = self._plan(ins, outs, sems)
        for a in range(self.n):
            for c in cp[a]["from_sib"]:
                c.wait_recv()
            for c in cp[a]["own"] + [cp[a][k] for k in ("relay_x", "relay_y", "pass_x", "pass_y", "pass_d0", "pass_d1")]:
                c.wait_send()
        for c in cp["mine"]:
            c.wait()


class _Scatter:
    def __init__(self, grads):
        self.grads = list(grads)
        self.n = len(self.grads)

    def inputs(self):
        return self.grads

    def out_shape(self):
        return [jax.ShapeDtypeStruct((N_DEV - 1,) + g.shape[1:], g.dtype) for g in self.grads]

    def scratch(self):
        return [pltpu.SemaphoreType.DMA((7 * self.n,)), pltpu.SemaphoreType.DMA((7 * self.n,))]

    def _plan(self, ins, outs, sems):
        send_sems, recv_sems = sems
        x, y, c, _ = _position()
        cps = []
        for a in range(self.n):
            for k in range(1, N_DEV):
                peer = (x ^ (k >> 2), y ^ ((k >> 1) & 1), c ^ (k & 1))
                cps.append(pltpu.make_async_remote_copy(
                    src_ref=ins[a].at[_slot(peer)], dst_ref=outs[a].at[k - 1],
                    send_sem=send_sems.at[7 * a + k - 1], recv_sem=recv_sems.at[7 * a + k - 1],
                    device_id=peer, device_id_type=MESH))
        return cps

    def start(self, ins, outs, sems):
        for cp in self._plan(ins, outs, sems):
            cp.start()

    def finish(self, ins, outs, sems):
        for cp in self._plan(ins, outs, sems):
            cp.wait()


def _hosted(inner, n_in, n_out, comm, grid):
    if comm is None:
        return inner
    nc_in, nc_out, ns = len(comm.inputs()), len(comm.out_shape()), len(comm.scratch())

    def body(*refs):
        o0 = n_in + nc_in
        s0 = o0 + n_out + nc_out
        main = refs[:n_in] + refs[o0:o0 + n_out] + refs[s0:len(refs) - ns]
        c_in, c_out, c_sems = refs[n_in:o0], refs[o0 + n_out:s0], refs[len(refs) - ns:]
        ids = [pl.program_id(ax) for ax in range(len(grid))]
        first = functools.reduce(jnp.logical_and, [i == 0 for i in ids])
        last = functools.reduce(jnp.logical_and, [i == g - 1 for i, g in zip(ids, grid)])

        @pl.when(first)
        def _():
            comm.start(c_in, c_out, c_sems)

        inner(*main)

        if getattr(comm, "pass_on_at", None) is not None:
            assert len(grid) == 1
            first_at, second_at = (min(grid[0] - 1, int(frac * grid[0])) for frac in comm.pass_on_at)
            assert first_at < second_at

            @pl.when(ids[0] == first_at)
            def _():
                comm.pass_on(c_in, c_out, c_sems)

            @pl.when(ids[0] == second_at)
            def _():
                comm.pass_on_relayed(c_in, c_out, c_sems)

        @pl.when(last)
        def _():
            comm.finish(c_in, c_out, c_sems)

    return body


def _call(inner, name, grid, in_specs, out_specs, out_shape, scratch, args, comm=None):
    n_in, n_out = len(args), len(out_shape)
    body = _hosted(inner, n_in, n_out, comm, grid)
    if comm is not None:
        in_specs = list(in_specs) + [_any()] * len(comm.inputs())
        args = list(args) + comm.inputs()
        out_specs = list(out_specs) + [_any()] * len(comm.out_shape())
        out_shape = list(out_shape) + comm.out_shape()
        scratch = list(scratch) + comm.scratch()
    outs = pl.pallas_call(
        body, name=name, grid=grid, in_specs=list(in_specs), out_specs=list(out_specs), out_shape=list(out_shape),
        scratch_shapes=list(scratch), compiler_params=_params(len(grid)))(*args)
    return list(outs[:n_out]), list(outs[n_out:])


class _Bcast:
    def __init__(self, block):
        self.block = block

    def inputs(self):
        return [self.block]

    def out_shape(self):
        return [jax.ShapeDtypeStruct((N_DEV,) + self.block.shape, self.block.dtype)]

    def scratch(self):
        return [pltpu.SemaphoreType.DMA((N_DEV - 1,)), pltpu.SemaphoreType.DMA((N_DEV - 1,)),
                pltpu.SemaphoreType.DMA((1,))]

    def _plan(self, ins, outs, sems):
        send_sems, recv_sems, local_sem = sems
        x, y, c, _ = _position()
        mine = outs[0].at[_slot((x, y, c))]
        cps = []
        for k in range(1, N_DEV):
            peer = (x ^ (k >> 2), y ^ ((k >> 1) & 1), c ^ (k & 1))
            cps.append(pltpu.make_async_remote_copy(
                src_ref=ins[0], dst_ref=mine, send_sem=send_sems.at[k - 1], recv_sem=recv_sems.at[k - 1],
                device_id=peer, device_id_type=MESH))
        return pltpu.make_async_copy(ins[0], mine, local_sem.at[0]), cps

    def start(self, ins, outs, sems):
        own, cps = self._plan(ins, outs, sems)
        own.start()
        for cp in cps:
            cp.start()

    def finish(self, ins, outs, sems):
        own, cps = self._plan(ins, outs, sems)
        for cp in cps:
            cp.wait()
        own.wait()


def _first_gather(shards, small_idx, x2, t2, n_meta, tp):
    comm = _Gather(shards)
    n = comm.n
    seq, d = x2.shape
    t_real = n_meta + seq
    n_pad = tp - t_real
    cw = d // N_DEV
    rows = STAGE_ROWS if seq % STAGE_ROWS == 0 else seq
    n_chunks = seq // rows

    def body(*refs):
        ins, (x_ref, t_ref) = refs[:n], refs[n:n + 2]
        outs, (h0_ref, tg_ref) = refs[n + 2:2 * n + 2], refs[2 * n + 2:2 * n + 4]
        sems = refs[2 * n + 4:2 * n + 7]
        buf, zeros, in_sems, out_sems, misc_sems = refs[2 * n + 7:]
        comm.start(ins, outs, sems)
        zeros[...] = jnp.zeros_like(zeros)
        fills = [pltpu.make_async_copy(zeros.at[pl.ds(0, n_pad)], h0_ref.at[pl.ds(t_real, n_pad)], misc_sems.at[0]),
                 pltpu.make_async_copy(zeros.at[pl.ds(0, n_pad)], tg_ref.at[pl.ds(t_real, n_pad)], misc_sems.at[1]),
                 pltpu.make_async_copy(zeros.at[pl.ds(0, n_meta)], tg_ref.at[pl.ds(0, n_meta)], misc_sems.at[2])]
        for cp in fills:
            cp.start()
        jobs = [(src, dst, c) for src, dst in ((x_ref, h0_ref), (t_ref, tg_ref)) for c in range(n_chunks)]

        def load(k):
            src, _, c = jobs[k]
            return pltpu.make_async_copy(src.at[pl.ds(c * rows, rows)], buf.at[k % 2], in_sems.at[k % 2])

        def store(k):
            _, dst, c = jobs[k]
            return pltpu.make_async_copy(buf.at[k % 2], dst.at[pl.ds(n_meta + c * rows, rows)], out_sems.at[k % 2])

        load(0).start()
        for k in range(len(jobs)):
            load(k).wait()
            if k + 1 < len(jobs):
                if k >= 1:
                    store(k - 1).wait()
                load(k + 1).start()
            store(k).start()
        for k in range(max(0, len(jobs) - 2), len(jobs)):
            store(k).wait()
        comm.finish(ins, outs, sems)
        meta = [pltpu.make_async_copy(outs[small_idx].at[k, pl.ds(0, n_meta)],
                                      h0_ref.at[pl.ds(0, n_meta), pl.ds(k * cw, cw)], misc_sems.at[3 + k])
                for k in range(N_DEV)]
        for cp in meta:
            cp.start()
        for cp in fills + meta:
            cp.wait()

    staged = [jax.ShapeDtypeStruct((tp, d), F32)] * 2
    outs = pl.pallas_call(
        body, name="weights_all_gather", out_shape=comm.out_shape() + staged,
        in_specs=[_any()] * (n + 2), out_specs=[_any()] * (n + 2),
        scratch_shapes=comm.scratch() + [
            pltpu.VMEM((2, rows, d), F32), pltpu.VMEM((max(n_pad, n_meta), d), F32),
            pltpu.SemaphoreType.DMA((2,)), pltpu.SemaphoreType.DMA((2,)), pltpu.SemaphoreType.DMA((3 + N_DEV,))],
        compiler_params=pltpu.CompilerParams(vmem_limit_bytes=V7X_VMEM_LIMIT),
    )(*shards, x2, t2)
    return outs[:n], outs[n], outs[n + 1]


def _chip_copies(c_ref, land_ref, sems):
    _, _, c, chips = _position()
    return [pltpu.make_async_remote_copy(
        src_ref=c_ref.at[2 * cx + cy], dst_ref=land_ref.at[j], send_sem=sems[j], recv_sem=sems[3 + j],
        device_id=(cx, cy, c), device_id_type=MESH) for j, (cx, cy) in enumerate(chips)]


def _scatter_copies(g_ref, land_ref, sems):
    x, y, c, _ = _position()
    cps = []
    for k in range(1, N_DEV):
        peer = (x ^ (k >> 2), y ^ ((k >> 1) & 1), c ^ (k & 1))
        cps.append(pltpu.make_async_remote_copy(
            src_ref=g_ref.at[_slot(peer)], dst_ref=land_ref.at[k - 1], send_sem=sems[k - 1],
            recv_sem=sems[N_DEV - 1 + k - 1], device_id=peer, device_id_type=MESH))
    return cps


def _exchange_start(name, copies, n_copies, src):
    hbm = pl.BlockSpec(memory_space=pltpu.HBM)
    sem = pl.BlockSpec(memory_space=pltpu.SEMAPHORE)
    n_sems = 2 * n_copies

    def body(s_ref, land_ref, *refs):
        for cp in copies(s_ref, land_ref, refs[:n_sems]):
            cp.start()
        token = refs[n_sems + 2]
        token[...] = jnp.zeros_like(token)

    land = lax.empty((n_copies,) + src.shape[1:], src.dtype)
    outs = pl.pallas_call(
        body, name=name + "_start",
        out_shape=(pltpu.SemaphoreType.DMA(()),) * n_sems
        + (pltpu.HBM(src.shape, src.dtype), pltpu.HBM(land.shape, land.dtype),
           jax.ShapeDtypeStruct((SUBLANES, 128), F32)),
        in_specs=(hbm, hbm), out_specs=(sem,) * n_sems + (hbm, hbm, pl.BlockSpec(memory_space=pltpu.VMEM)),
        input_output_aliases={0: n_sems, 1: n_sems + 1},
        compiler_params=pltpu.CompilerParams(has_side_effects=pltpu.SideEffectType.DATAFLOW_SIDE_EFFECTING),
    )(pltpu.with_memory_space_constraint(src, pltpu.HBM), pltpu.with_memory_space_constraint(land, pltpu.HBM))
    return outs[:n_sems], outs[n_sems], outs[n_sems + 1], outs[n_sems + 2]


def _exchange_wait(name, copies, sems, src_thru, land_thru, after):
    hbm = pl.BlockSpec(memory_space=pltpu.HBM)
    sem = pl.BlockSpec(memory_space=pltpu.SEMAPHORE)
    n_sems = len(sems)

    def body(s_ref, land_ref, *refs):
        for cp in copies(s_ref, land_ref, refs[:n_sems]):
            cp.wait_send()
            cp.wait_recv()

    return pl.pallas_call(
        body, name=name + "_wait",
        out_shape=(pltpu.HBM(src_thru.shape, src_thru.dtype), pltpu.HBM(land_thru.shape, land_thru.dtype)),
        in_specs=(hbm, hbm) + (sem,) * n_sems + (pl.BlockSpec(memory_space=pl.ANY),), out_specs=(hbm, hbm),
        input_output_aliases={0: 0, 1: 1},
        compiler_params=pltpu.CompilerParams(has_side_effects=pltpu.SideEffectType.DATAFLOW_SIDE_EFFECTING),
    )(src_thru, land_thru, *sems, after)


def _pair_reduce(grad, core):
    blk = grad.shape[2:]
    zeros = (0,) * len(blk)

    def body(core_ref, g_hbm, own_ref, o_ref, landed, send_sems, recv_sems):
        del core_ref
        i = pl.program_id(0)
        x, y, c, _ = _position()

        def copy(k):
            return pltpu.make_async_remote_copy(
                src_ref=g_hbm.at[k, 1 - c], dst_ref=landed.at[k], send_sem=send_sems.at[k],
                recv_sem=recv_sems.at[k], device_id=(x, y, 1 - c), device_id_type=MESH)

        @pl.when(i == 0)
        def _():
            for k in range(4):
                copy(k).start()

        for k in range(4):
            @pl.when(i == k)
            def _(k=k):
                copy(k).wait_recv()

        o_ref[...] = (own_ref[...].astype(F32) + landed[i].astype(F32)).astype(BF16)

        @pl.when(i == 3)
        def _():
            for k in range(4):
                copy(k).wait_send()

    return pl.pallas_call(
        body, name="grads_pair_reduce",
        out_shape=jax.ShapeDtypeStruct((4,) + blk, BF16),
        grid_spec=pltpu.PrefetchScalarGridSpec(
            num_scalar_prefetch=1, grid=(4,),
            in_specs=[_any(), pl.BlockSpec((None, None) + blk, lambda i, cr: (i, cr[0]) + zeros)],
            out_specs=pl.BlockSpec((None,) + blk, lambda i, cr: (i,) + zeros),
            scratch_shapes=[pltpu.VMEM((4,) + blk, BF16), pltpu.SemaphoreType.DMA((4,)),
                            pltpu.SemaphoreType.DMA((4,))]),
        compiler_params=_params(1),
    )(core, grad, grad)


def _adamw(w, g, m, v):
    m2 = ADAM_B1 * m + (1.0 - ADAM_B1) * g
    v2 = ADAM_B2 * v + (1.0 - ADAM_B2) * (g * g)
    m_hat = m2 / (1.0 - ADAM_B1 ** ADAM_STEP)
    v_hat = v2 / (1.0 - ADAM_B2 ** ADAM_STEP)
    delta = -ADAM_LR * (m_hat / (jnp.sqrt(v_hat) + ADAM_EPS) + ADAM_WD * w)
    return delta, m2, v2


def _final_adamw(own, recv, idx, parts, after):
    blk = own.shape[1:]
    n_recv = recv.shape[0]
    n_parts = len(parts)
    per = blk[0] // n_parts if n_parts > 1 else None
    rows = blk[-2]
    n_chunks = 1 if n_parts > 1 else (4 if rows % 64 == 0 and rows >= 512 else (2 if rows % 32 == 0 else 1))
    cblk = blk[:-2] + (rows // n_chunks, blk[-1])
    lead = (0,) * (len(blk) - 2)

    def body(idx_ref, c_ref, r_ref, after_ref, *refs):
        del idx_ref, after_ref
        ins, outs = refs[:3 * n_parts], refs[3 * n_parts:]
        g = c_ref[...].astype(F32)
        for k in range(n_recv):
            g = g + r_ref[k].astype(F32)
        for p in range(n_parts):
            w_ref, m_ref, v_ref = ins[3 * p:3 * p + 3]
            if n_parts == 1:
                gp = g
            elif per == 1:
                gp = g[p]
            else:
                gp = g[p * per:(p + 1) * per]
            delta, m2, v2 = _adamw(w_ref[0], gp, m_ref[0], v_ref[0])
            o = outs[4 * p:4 * p + 4]
            o[0][0] = gp
            o[1][0] = delta
            o[2][0] = m2
            o[3][0] = v2

    flat = [a for wmv in parts for a in wmv]

    def part_spec(a):
        shape = a.shape[:-2] + (a.shape[-2] // n_chunks, a.shape[-1])
        return pl.BlockSpec(shape, lambda i, cr, nd=a.ndim: (0,) * (nd - 2) + (i, 0))

    outs = pl.pallas_call(
        body, name="grads_sum_adamw",
        out_shape=[jax.ShapeDtypeStruct(wmv[0].shape, F32) for wmv in parts for _ in range(4)],
        grid_spec=pltpu.PrefetchScalarGridSpec(
            num_scalar_prefetch=1, grid=(n_chunks,),
            in_specs=[pl.BlockSpec((None,) + cblk, lambda i, cr: (cr[0],) + lead + (i, 0)),
                      pl.BlockSpec((n_recv,) + cblk, lambda i, cr: (0,) + lead + (i, 0))]
                     + [_any()] + [part_spec(a) for a in flat],
            out_specs=[part_spec(wmv[0]) for wmv in parts for _ in range(4)]),
        compiler_params=_params(1),
    )(idx, own, recv, after, *flat)
    return [tuple(outs[4 * p:4 * p + 4]) for p in range(n_parts)]


def _small_adamw(partials, layout, me_index, after):
    _, rows, d = partials.shape
    n = len(layout)
    cw = d // N_DEV

    def body(me_ref, p_ref, after_ref, *refs):
        ins, t_ref, outs = refs[:3 * n], refs[3 * n], refs[3 * n + 1:]
        me = me_ref[0]
        total = p_ref[0]
        for j in range(1, N_DEV):
            total = total + p_ref[j]
        t_ref[...] = total
        for e, (kind, r0, nr, _, _, _) in enumerate(layout):
            w_ref, m_ref, v_ref = ins[3 * e:3 * e + 3]
            o = outs[4 * e:4 * e + 4]
            if kind == "rep":
                g = t_ref[r0:r0 + nr, :]
                delta, m2, v2 = _adamw(w_ref[...], g, m_ref[...], v_ref[...])
                for ref, val in zip(o, (g, delta, m2, v2)):
                    ref[...] = val
            elif kind == "wide":
                for q in range(nr):
                    sl = slice(q * d, (q + 1) * d)
                    g = t_ref[r0 + q:r0 + q + 1, :]
                    delta, m2, v2 = _adamw(w_ref[:, sl], g, m_ref[:, sl], v_ref[:, sl])
                    for ref, val in zip(o, (g, delta, m2, v2)):
                        ref[:, sl] = val
            else:
                for j in range(N_DEV):
                    @pl.when(me == j)
                    def _(j=j, o=o, w_ref=w_ref, m_ref=m_ref, v_ref=v_ref, r0=r0, nr=nr):
                        g = t_ref[r0:r0 + nr, j * cw:(j + 1) * cw]
                        delta, m2, v2 = _adamw(w_ref[...], g, m_ref[...], v_ref[...])
                        for ref, val in zip(o, (g, delta, m2, v2)):
                            ref[...] = val

    flat = [a for ent in layout for a in ent[3:]]
    vm = pl.BlockSpec(memory_space=pltpu.VMEM)
    outs = pl.pallas_call(
        body, name="small_adamw",
        out_shape=[jax.ShapeDtypeStruct((rows, d), F32)]
                  + [jax.ShapeDtypeStruct(ent[3].shape, F32) for ent in layout for _ in range(4)],
        in_specs=[pl.BlockSpec(memory_space=pltpu.SMEM), vm, _any()] + [vm] * len(flat),
        out_specs=[vm] * (1 + 4 * n),
        compiler_params=pltpu.CompilerParams(vmem_limit_bytes=V7X_VMEM_LIMIT),
    )(me_index, partials, after, *flat)
    return outs[0], [tuple(outs[1 + 4 * e:5 + 4 * e]) for e in range(n)]


def _ffn_fwd(h, g, wgu, wd, tm, loss=None, comm=None):
    tp, d = h.shape
    f = wd.shape[0]
    fc = f // FFN_FWD_CHUNKS
    nt = tp // tm
    with_loss = loss is not None
    if with_loss:
        tgt, gf, n_meta, t_real = loss

    def body(*refs):
        if with_loss:
            (h_ref, g_ref, wgu_hbm, wd_hbm, tgt_ref, gf_ref, out_ref, gu_ref, n_ref, tail_ref,
             wgu_v, wd_v, sems) = refs
        else:
            h_ref, g_ref, wgu_hbm, wd_hbm, out_ref, gu_ref, n_ref, wgu_v, wd_v, sems = refs
        i = pl.program_id(0)

        @pl.when(i == 0)
        def _():
            _load_weights([(wgu_hbm, wgu_v), (wd_hbm, wd_v)], sems)
            if with_loss:
                tail_ref[...] = jnp.zeros_like(tail_ref)

        x = h_ref[...]
        n, _ = _rms_fwd(x, g_ref[...])
        nb = n.astype(BF16)
        n_ref[...] = nb
        acc = jnp.zeros((tm, d), F32)
        for j in range(FFN_FWD_CHUNKS):
            cols = slice(j * fc, (j + 1) * fc)
            gate = _nt(nb, wgu_v[pl.ds(j * fc, fc), :])
            up = _nt(nb, wgu_v[pl.ds(f + j * fc, fc), :])
            gu_ref[0, :, cols] = gate.astype(BF16)
            gu_ref[1, :, cols] = up.astype(BF16)
            act = (gate * _sigmoid(gate) * up).astype(BF16)
            acc = acc + _nn(act, wd_v[pl.ds(j * fc, fc), :])
        hn = x + FFN_RES * acc
        if not with_loss:
            out_ref[...] = hn
        else:
            gfv = gf_ref[...]
            r = lax.rsqrt(jnp.mean(hn * hn, axis=-1, keepdims=True) + EPS)
            xr = hn * r
            rows = i * tm + lax.broadcasted_iota(jnp.int32, (tm, 1), 0)
            mask = jnp.logical_and(rows >= n_meta, rows < t_real)
            diff = jnp.where(mask, xr * gfv - tgt_ref[...], 0.0)
            tail_ref[TAIL_LOSS:TAIL_LOSS + 1, :] += jnp.zeros((1, d), F32) + 0.5 * jnp.sum(diff * diff) / d
            dy = diff / d
            gy = dy * gfv
            out_ref[...] = r * (gy - xr * jnp.mean(gy * xr, axis=-1, keepdims=True))
            tail_ref[TAIL_FINAL:TAIL_FINAL + 1, :] += _rowsum(dy * xr)

    row = pl.BlockSpec((tm, d), lambda i: (i, 0))
    vec = pl.BlockSpec((1, d), lambda i: (0, 0))
    in_specs = [row, vec, _any(), _any()]
    out_shape = [jax.ShapeDtypeStruct((tp, d), F32), jax.ShapeDtypeStruct((2, tp, f), BF16),
                 jax.ShapeDtypeStruct((tp, d), BF16)]
    out_specs = [row, pl.BlockSpec((2, tm, f), lambda i: (0, i, 0)), row]
    args = [h, g, wgu, wd]
    if with_loss:
        in_specs += [row, vec]
        out_shape += [jax.ShapeDtypeStruct((SUBLANES, d), F32)]
        out_specs += [pl.BlockSpec((SUBLANES, d), lambda i: (0, 0))]
        args += [tgt, gf]
    return _call(body, "ffn_fwd_loss" if with_loss else "ffn_fwd", (nt,), in_specs, out_specs, out_shape,
                 [pltpu.VMEM((2 * f, d), BF16), pltpu.VMEM((f, d), BF16), pltpu.SemaphoreType.DMA((2,))],
                 args, comm)


def _ffn_bwd(dh, h, gu, g, wgu, wd, tm, tail, tail_row, after):
    tp, d = h.shape
    f = wd.shape[0]
    fc = f // FFN_BWD_CHUNKS
    nt = tp // tm

    def body(dh_ref, h_ref, gu_ref, g_ref, tail_ref, wgu_hbm, wd_hbm, after_ref,
             dhin_ref, dgu_ref, act_ref, df_ref, dg_ref, wgu_v, wd_v, dn_v, sems):
        del after_ref
        i, j = pl.program_id(0), pl.program_id(1)

        @pl.when(jnp.logical_and(i == 0, j == 0))
        def _():
            _load_weights([(wgu_hbm, wgu_v), (wd_hbm, wd_v)], sems)
            dg_ref[...] = tail_ref[...]

        dfb = (FFN_RES * dh_ref[...]).astype(BF16)

        @pl.when(j == 0)
        def _():
            df_ref[...] = dfb
            dn_v[...] = jnp.zeros_like(dn_v)

        lo = pl.multiple_of(j * fc, 16)
        dact = _nt(dfb, wd_v[pl.ds(lo, fc), :])
        gate = gu_ref[0].astype(F32)
        up = gu_ref[1].astype(F32)
        sg = _sigmoid(gate)
        silu = gate * sg
        act_ref[...] = (silu * up).astype(BF16)
        dgate = (dact * up * (sg * (1.0 + gate * (1.0 - sg)))).astype(BF16)
        dup = (dact * silu).astype(BF16)
        dgu_ref[0] = dgate
        dgu_ref[1] = dup
        dn_v[...] += _nn(dgate, wgu_v[pl.ds(lo, fc), :]) + _nn(dup, wgu_v[pl.ds(pl.multiple_of(f + j * fc, 16), fc), :])

        @pl.when(j == FFN_BWD_CHUNKS - 1)
        def _():
            x = h_ref[...]
            r = lax.rsqrt(jnp.mean(x * x, axis=-1, keepdims=True) + EPS)
            dx, dgp = _rms_bwd(dn_v[...], x, r, g_ref[...])
            dhin_ref[...] = dh_ref[...] + dx
            dg_ref[tail_row:tail_row + 1, :] += dgp

    row = pl.BlockSpec((tm, d), lambda i, j: (i, 0))
    vec = pl.BlockSpec((1, d), lambda i, j: (0, 0))
    tile = pl.BlockSpec((SUBLANES, d), lambda i, j: (0, 0))
    hid2 = pl.BlockSpec((2, tm, fc), lambda i, j: (0, i, j))
    return _call(
        body, "ffn_bwd", (nt, FFN_BWD_CHUNKS),
        [row, row, hid2, vec, tile, _any(), _any(), _any()],
        [row, hid2, pl.BlockSpec((tm, fc), lambda i, j: (i, j)), row, tile],
        [jax.ShapeDtypeStruct((tp, d), F32), jax.ShapeDtypeStruct((2, tp, f), BF16),
         jax.ShapeDtypeStruct((tp, f), BF16), jax.ShapeDtypeStruct((tp, d), BF16),
         jax.ShapeDtypeStruct((SUBLANES, d), F32)],
        [pltpu.VMEM((2 * f, d), BF16), pltpu.VMEM((f, d), BF16), pltpu.VMEM((tm, d), F32),
         pltpu.SemaphoreType.DMA((2,))],
        [dh, h, gu, g, tail, wgu, wd, after])


def _piece_segments(q, d, nb_cols):
    segs = []
    for j in range(N_DEV):
        lo, hi = max(q * d, j * nb_cols), min((q + 1) * d, (j + 1) * nb_cols)
        if lo < hi:
            segs.append((j, lo - q * d, hi - q * d, lo - j * nb_cols, hi - j * nb_cols))
    return segs


def _w3_copies(w3_hbm, rows, w3_v):
    return [(w3_hbm.at[k, pl.ds(q * rows, rows)], w3_v.at[q, pl.ds(k * rows, rows)])
            for q in range(3) for k in range(N_DEV)]


def _gates(xrb, wg_ref, ba, bx, lam, hd):
    pre_r, pre_i = [], []
    for hh in range(N_HEADS):
        xh = xrb[:, hh * hd:(hh + 1) * hd]
        pre_r.append(_nn(xh, wg_ref[0, hh]))
        pre_i.append(_nn(xh, wg_ref[1, hh]))
    r = _sigmoid(jnp.concatenate(pre_r, axis=1) + ba)
    ig = _sigmoid(jnp.concatenate(pre_i, axis=1) + bx)
    sp = _softplus(-lam)
    log_a = -RG_LRU_C * r * sp
    a = jnp.exp(log_a)
    s = jnp.sqrt(_one_minus_exp(2.0 * log_a))
    return r, ig, sp, a, s


def _scan_fwd(a, u, h_prev):
    tm = a.shape[0]
    rows = lax.broadcasted_iota(jnp.int32, a.shape, 0)
    d = 1
    while d < tm:
        if d < SUBLANES:
            keep = rows >= d
            u = jnp.where(keep, a * pltpu.roll(u, d, 0) + u, u)
            a = jnp.where(keep, a * pltpu.roll(a, d, 0), a)
        else:
            u = jnp.concatenate([u[:d], a[d:] * u[:tm - d] + u[d:]], axis=0)
            a = jnp.concatenate([a[:d], a[d:] * a[:tm - d]], axis=0)
        d *= 2
    return u + a * h_prev


def _scan_bwd(b, v, g_next):
    tm = b.shape[0]
    rows = lax.broadcasted_iota(jnp.int32, b.shape, 0)
    d = 1
    while d < tm:
        if d < SUBLANES:
            keep = rows < tm - d
            v = jnp.where(keep, v + b * pltpu.roll(v, tm - d, 0), v)
            b = jnp.where(keep, b * pltpu.roll(b, tm - d, 0), b)
        else:
            v = jnp.concatenate([v[:tm - d] + b[:tm - d] * v[d:], v[tm - d:]], axis=0)
            b = jnp.concatenate([b[:tm - d] * b[d:], b[tm - d:]], axis=0)
        d *= 2
    return v + b * g_next


def _shifted_copies(ext_ref, es_ref, n_rows):
    for s in range(1, SUBLANES):
        es_ref[s, pl.ds(0, n_rows), :] = ext_ref[pl.ds(s, n_rows), :]


def _tap(ext_ref, es_ref, off, tm):
    q, s = divmod(off, SUBLANES)
    if s == 0:
        return ext_ref[pl.ds(SUBLANES * q, tm), :]
    return es_ref[s, pl.ds(SUBLANES * q, tm), :]


def _mixer_fwd(h, g, b_in, win_all, cw4, cb4, wg, ba, bx, lam, cw31, cb31, lng, lnb, bcp, w3_all, tm, comm=None):
    tp, d = h.shape
    nb_cols = win_all.shape[-1]
    n_in = N_DEV * nb_cols
    hd = wg.shape[-1]
    k4, k31 = cw4.shape[0], cw31.shape[0]
    w3_rows = d // N_DEV

    def body(h_ref, g_ref, b_ref, win_hbm, cw4_ref, cb4_ref, wg_ref, ba_ref, bx_ref, lam_ref, cw31_ref, cb31_ref,
             lng_ref, lnb_ref, bcp_ref, w3_hbm,
             h2_ref, p_ref, n_ref, xr_ref, hs_ref, v1_ref, ya_ref, yb_ref,
             win_v, w3_v, ext4, ext31, es31, hcar, sems):
        @pl.when(pl.program_id(0) == 0)
        def _():
            _load_weights([(win_hbm, win_v)] + _w3_copies(w3_hbm, w3_rows, w3_v), sems)
            ext4[pl.ds(0, CONV4_HALO), :] = jnp.zeros((CONV4_HALO, d), F32)
            ext31[pl.ds(0, CONV31_HALO), :] = jnp.zeros((CONV31_HALO, d), F32)
            hcar[...] = jnp.zeros_like(hcar)

        n, _ = _rms_fwd(h_ref[...], g_ref[...])
        nb = n.astype(BF16)
        n_ref[...] = nb

        def piece(q):
            parts = [_nn(nb, win_v[j, :, bl:bh]) for j, _, _, bl, bh in _piece_segments(q, d, nb_cols)]
            pq = (jnp.concatenate(parts, axis=1) + b_ref[:, q * d:(q + 1) * d]).astype(BF16)
            p_ref[:, q * d:(q + 1) * d] = pq
            return pq.astype(F32)

        x_rnn, y_rnn, glu_v, glu_g, gate_a, gate_b = [piece(q) for q in range(6)]

        ext4[pl.ds(CONV4_HALO, tm), :] = x_rnn
        xr = cb4_ref[...] + jnp.zeros((tm, d), F32)
        for k in range(k4):
            xr = xr + cw4_ref[k:k + 1, :] * ext4[pl.ds(CONV4_HALO - (k4 - 1) + k, tm), :]
        ext4[pl.ds(0, CONV4_HALO), :] = ext4[pl.ds(tm, CONV4_HALO), :]
        xrb = xr.astype(BF16)
        xr_ref[...] = xrb
        xr = xrb.astype(F32)
        _, ig, _, a, s = _gates(xrb, wg_ref, ba_ref[...], bx_ref[...], lam_ref[...], hd)
        hseq = _scan_fwd(a, s * (ig * xr), hcar[0:1, :])
        hcar[0:1, :] = hseq[tm - 1:tm, :]
        hs_ref[...] = hseq.astype(BF16)
        gl, _ = _gelu(y_rnn)
        ya = _nn((hseq * gl).astype(BF16), w3_v[0])
        ya_ref[...] = ya.astype(BF16)

        ext31[pl.ds(CONV31_HALO, tm), :] = glu_v * _sigmoid(glu_g)
        _shifted_copies(ext31, es31, tm + CONV31_HALO - SUBLANES)
        v1 = cb31_ref[...] + jnp.zeros((tm, d), F32)
        for k in range(k31):
            v1 = v1 + cw31_ref[k:k + 1, :] * _tap(ext31, es31, CONV31_HALO - (k31 - 1) + k, tm)
        ext31[pl.ds(0, CONV31_HALO), :] = ext31[pl.ds(tm, CONV31_HALO), :]
        v1b = v1.astype(BF16)
        v1_ref[...] = v1b
        v1 = v1b.astype(F32)
        xc = v1 - jnp.mean(v1, axis=-1, keepdims=True)
        rstd = lax.rsqrt(jnp.mean(xc * xc, axis=-1, keepdims=True) + EPS)
        v2 = xc * rstd * lng_ref[...] + lnb_ref[...]
        yb = _nn((v2 * _sigmoid(v2)).astype(BF16), w3_v[1]) + bcp_ref[...]
        yb_ref[...] = yb.astype(BF16)

        merged = _sigmoid(gate_a) * ya + _sigmoid(gate_b) * yb
        h2_ref[...] = h_ref[...] + _nn(merged.astype(BF16), w3_v[2])

    row = pl.BlockSpec((tm, d), lambda i: (i, 0))
    wide = pl.BlockSpec((tm, n_in), lambda i: (i, 0))
    full = lambda a: pl.BlockSpec(a.shape, lambda i, nd=a.ndim: (0,) * nd)
    smalls = [cw4, cb4, wg, ba, bx, lam, cw31, cb31, lng, lnb, bcp]
    return _call(
        body, "mixer_fwd", (tp // tm,),
        [row, full(g), full(b_in), _any()] + [full(a) for a in smalls] + [_any()],
        [row, wide] + [row] * 6,
        [jax.ShapeDtypeStruct((tp, d), F32), jax.ShapeDtypeStruct((tp, n_in), BF16)]
        + [jax.ShapeDtypeStruct((tp, d), BF16)] * 6,
        [pltpu.VMEM(win_all.shape, BF16),
         pltpu.VMEM((3, d, d), BF16),
         pltpu.VMEM((tm + CONV4_HALO, d), F32),
         pltpu.VMEM((tm + CONV31_HALO, d), F32),
         pltpu.VMEM((SUBLANES, tm + CONV31_HALO, d), F32),
         pltpu.VMEM((SUBLANES, d), F32),
         pltpu.SemaphoreType.DMA((1 + 3 * N_DEV,))],
        [h, g, b_in, win_all, *smalls, w3_all], comm)


SG_BIN, SG_CW4, SG_CB4, SG_BA, SG_BX, SG_LAM, SG_CB31, SG_LNG, SG_LNB, SG_BCP, SG_MIX, SG_CW31 = 0, 6, 10, 11, 12, 13, 14, 15, 16, 17, 18, 19


def _mixer_bwd(dh2, h, g, proj, xr_s, hs_s, v1_s, ya_s, yb_s, win_t, cw4, wg, ba, bx, lam, cw31, lng, lnb, w3_all, tm,
               comm=None):
    tp, d = dh2.shape
    n_in = proj.shape[1]
    hd = wg.shape[-1]
    k4, k31 = cw4.shape[0], cw31.shape[0]
    nt = tp // tm
    w3_rows = d // N_DEV
    sg_rows = -(-(SG_CW31 + k31) // SUBLANES) * SUBLANES
    halo_rows = 16
    per = tm // halo_rows

    def body(dh_ref, h_ref, g_ref, p_ref, xr_ref, hs_ref, hh_ref, v1_ref, ya_ref, yb_ref, win_hbm,
             cw4_ref, wg_ref, wgt_ref, ba_ref, bx_ref, lam_ref, cw31_ref, lng_ref, lnb_ref, w3_hbm,
             dh1_ref, dp_ref, x3_ref, y3_ref, yg_ref, sg_ref,
             win_v, w3_v, extd4, extd31, es31, gcar, sems):
        i = pl.program_id(0)
        tile = nt - 1 - i

        @pl.when(i == 0)
        def _():
            _load_weights([(win_hbm, win_v)] + _w3_copies(w3_hbm, w3_rows, w3_v), sems)
            for q in range(3):
                w3_v[q] = w3_v[q].T
            extd4[pl.ds(tm, CONV4_HALO), :] = jnp.zeros((CONV4_HALO, d), F32)
            extd31[pl.ds(tm, CONV31_HALO), :] = jnp.zeros((CONV31_HALO, d), F32)
            gcar[...] = jnp.zeros_like(gcar)
            sg_ref[...] = jnp.zeros_like(sg_ref)

        def acc(row, val):
            sg_ref[row:row + 1, :] += _rowsum(val)

        rows = lax.broadcasted_iota(jnp.int32, (tm, d), 0)
        x_rnn = p_ref[:, 0:d].astype(F32)
        y_rnn = p_ref[:, d:2 * d].astype(F32)
        glu_v = p_ref[:, 2 * d:3 * d].astype(F32)
        glu_g = p_ref[:, 3 * d:4 * d].astype(F32)
        sga = _sigmoid(p_ref[:, 4 * d:5 * d].astype(F32))
        sgb = _sigmoid(p_ref[:, 5 * d:6 * d].astype(F32))
        ya = ya_ref[...].astype(F32)
        yb = yb_ref[...].astype(F32)

        dmob = dh_ref[...].astype(BF16)
        dmerged = _nn(dmob, w3_v[2])
        x3_ref[:, 0:d] = (sga * ya + sgb * yb).astype(BF16)
        y3_ref[:, 0:d] = dmob
        dya = sga * dmerged
        dyb = sgb * dmerged
        dn_parts = []

        def emit(q, val):
            vb = val.astype(BF16)
            dp_ref[:, q * d:(q + 1) * d] = vb
            acc(SG_BIN + q, val)
            term = _nn(vb, win_v[pl.ds(q * d, d), :])
            dn_parts[:] = [term if not dn_parts else dn_parts[0] + term]

        emit(4, dmerged * ya * sga * (1.0 - sga))
        emit(5, dmerged * yb * sgb * (1.0 - sgb))

        dyab = dya.astype(BF16)
        y3_ref[:, d:2 * d] = dyab
        dza = _nn(dyab, w3_v[0])
        hsv = hs_ref[...].astype(F32)
        gl, th = _gelu(y_rnn)
        x3_ref[:, d:2 * d] = (hsv * gl).astype(BF16)
        emit(1, dza * hsv * _gelu_grad(y_rnn, th))
        dhs = dza * gl
        xrb = xr_ref[...]
        xr = xrb.astype(F32)
        lam_v = lam_ref[...]
        r, ig, sp, a, s = _gates(xrb, wg_ref, ba_ref[...], bx_ref[...], lam_v, hd)
        b = jnp.where(rows == tm - 1, gcar[1:2, :], pltpu.roll(a, tm - 1, 0))
        big_g = _scan_bwd(b, dhs, gcar[0:1, :])
        gcar[0:1, :] = big_g[0:1, :]
        gcar[1:2, :] = a[0:1, :]
        h_before = jnp.where(tile > 0, hh_ref[halo_rows - 1:halo_rows, :].astype(F32), 0.0)
        h_prev = jnp.where(rows == 0, h_before, pltpu.roll(hsv, 1, 0))
        ds = big_g * ig * xr
        dla = big_g * h_prev * a - ds * (a * a) / jnp.maximum(s, 1e-20)
        acc(SG_LAM, dla * r * (RG_LRU_C * _sigmoid(-lam_v)))
        dpr = dla * (-RG_LRU_C * sp) * r * (1.0 - r)
        dpi = big_g * s * xr * ig * (1.0 - ig)
        acc(SG_BA, dpr)
        acc(SG_BX, dpi)
        dprb = dpr.astype(BF16)
        dpib = dpi.astype(BF16)
        yg_ref[:, 0:d] = dprb
        yg_ref[:, d:2 * d] = dpib
        back = []
        for hh in range(N_HEADS):
            sl = slice(hh * hd, (hh + 1) * hd)
            back.append(_nn(dprb[:, sl], wgt_ref[0, hh]) + _nn(dpib[:, sl], wgt_ref[1, hh]))
        dxr = big_g * s * ig + jnp.concatenate(back, axis=1)
        acc(SG_CB4, dxr)
        extd4[pl.ds(0, tm), :] = dxr
        dx_rnn = jnp.zeros((tm, d), F32)
        for k in range(k4):
            term = extd4[pl.ds(k4 - 1 - k, tm), :]
            dx_rnn = dx_rnn + cw4_ref[k:k + 1, :] * term
            acc(SG_CW4 + k, x_rnn * term)
        extd4[pl.ds(tm, CONV4_HALO), :] = extd4[pl.ds(0, CONV4_HALO), :]
        emit(0, dx_rnn)

        dybb = dyb.astype(BF16)
        y3_ref[:, 2 * d:3 * d] = dybb
        acc(SG_BCP, dyb)
        dv3 = _nn(dybb, w3_v[1])
        v1 = v1_ref[...].astype(F32)
        xc = v1 - jnp.mean(v1, axis=-1, keepdims=True)
        rstd = lax.rsqrt(jnp.mean(xc * xc, axis=-1, keepdims=True) + EPS)
        xhat = xc * rstd
        lng_v = lng_ref[...]
        v2 = xhat * lng_v + lnb_ref[...]
        s2 = _sigmoid(v2)
        x3_ref[:, 2 * d:3 * d] = (v2 * s2).astype(BF16)
        dv2 = dv3 * (s2 * (1.0 + v2 * (1.0 - s2)))
        acc(SG_LNG, dv2 * xhat)
        acc(SG_LNB, dv2)
        dxh = dv2 * lng_v
        dv1 = rstd * (dxh - jnp.mean(dxh, axis=-1, keepdims=True)
                      - xhat * jnp.mean(dxh * xhat, axis=-1, keepdims=True))
        acc(SG_CB31, dv1)
        extd31[pl.ds(0, tm), :] = dv1
        _shifted_copies(extd31, es31, tm + CONV31_HALO - SUBLANES)
        sgg = _sigmoid(glu_g)
        v0 = glu_v * sgg
        dv0 = jnp.zeros((tm, d), F32)
        for k in range(k31):
            term = _tap(extd31, es31, k31 - 1 - k, tm)
            dv0 = dv0 + cw31_ref[k:k + 1, :] * term
            acc(SG_CW31 + k, v0 * term)
        extd31[pl.ds(tm, CONV31_HALO), :] = extd31[pl.ds(0, CONV31_HALO), :]
        emit(2, dv0 * sgg)
        emit(3, dv0 * glu_v * sgg * (1.0 - sgg))

        dn = dn_parts[0]
        x = h_ref[...]
        rr = lax.rsqrt(jnp.mean(x * x, axis=-1, keepdims=True) + EPS)
        dx, dgp = _rms_bwd(dn, x, rr, g_ref[...])
        dh1_ref[...] = dh_ref[...] + dx
        sg_ref[SG_MIX:SG_MIX + 1, :] += dgp

    rev = lambda i: (nt - 1 - i, 0)
    row = pl.BlockSpec((tm, d), rev)
    wide = pl.BlockSpec((tm, n_in), rev)
    full = lambda a: pl.BlockSpec(a.shape, lambda i, nd=a.ndim: (0,) * nd)
    halo = pl.BlockSpec((halo_rows, d), lambda i: (jnp.maximum((nt - 1 - i) * per - 1, 0), 0))
    smalls = [cw4, wg, jnp.swapaxes(wg, 2, 3), ba, bx, lam, cw31, lng, lnb]
    return _call(
        body, "mixer_bwd", (nt,),
        [row, row, full(g), wide, row, row, halo, row, row, row, _any()]
        + [full(a) for a in smalls] + [_any()],
        [row, wide, pl.BlockSpec((tm, 3 * d), rev), pl.BlockSpec((tm, 3 * d), rev),
         pl.BlockSpec((tm, 2 * d), rev), pl.BlockSpec((sg_rows, d), lambda i: (0, 0))],
        [jax.ShapeDtypeStruct((tp, d), F32), jax.ShapeDtypeStruct((tp, n_in), BF16),
         jax.ShapeDtypeStruct((tp, 3 * d), BF16), jax.ShapeDtypeStruct((tp, 3 * d), BF16),
         jax.ShapeDtypeStruct((tp, 2 * d), BF16), jax.ShapeDtypeStruct((sg_rows, d), F32)],
        [pltpu.VMEM(win_t.shape, BF16),
         pltpu.VMEM((3, d, d), BF16),
         pltpu.VMEM((tm + CONV4_HALO, d), F32),
         pltpu.VMEM((tm + CONV31_HALO, d), F32),
         pltpu.VMEM((SUBLANES, tm + CONV31_HALO, d), F32),
         pltpu.VMEM((SUBLANES, d), F32),
         pltpu.SemaphoreType.DMA((1 + 3 * N_DEV,))],
        [dh2, h, g, proj, xr_s, hs_s, hs_s, v1_s, ya_s, yb_s, win_t, *smalls, w3_all], comm)


def _tn_matmul(name, x, y, x_spec, y_spec, n_blocks, kb, nb, tm, tp, out_shape, out_spec, out_view, comm=None,
               after=None):
    nt = tp // tm

    def body(x_ref, y_ref, *refs):
        o_ref, acc = refs[-2:]
        i = pl.program_id(1)

        @pl.when(i == 0)
        def _():
            acc[...] = jnp.zeros_like(acc)

        acc[...] += _tn(x_ref[...], y_ref[...])

        @pl.when(i == nt - 1)
        def _():
            o_ref[...] = acc[...].astype(BF16).reshape(out_view)

    follows = [] if after is None else [after]
    outs, extra = _call(body, name, (n_blocks, nt), [x_spec, y_spec] + [_any()] * len(follows), [out_spec],
                        [jax.ShapeDtypeStruct(out_shape, BF16)], [pltpu.VMEM((kb, nb), F32)], [x, y] + follows,
                        comm)
    return outs[0], extra


def kernel(x, meta_tokens, ffn1_norm, ffn1_w_gu, ffn1_w_down, mix_norm, w_in, b_in, rnn_conv_w, rnn_conv_b, rg_w_a, rg_b_a, rg_w_x, rg_b_x, rg_lambda, rnn_w_proj, conv_dw_w, conv_dw_b, conv_ln_g, conv_ln_b, conv_w_proj, conv_b_proj, w_out, ffn2_norm, ffn2_w_gu, ffn2_w_down, final_norm, loss_target, m_meta_tokens, m_ffn1_norm, m_ffn1_w_gu, m_ffn1_w_down, m_mix_norm, m_w_in, m_b_in, m_rnn_conv_w, m_rnn_conv_b, m_rg_w_a, m_rg_b_a, m_rg_w_x, m_rg_b_x, m_rg_lambda, m_rnn_w_proj, m_conv_dw_w, m_conv_dw_b, m_conv_ln_g, m_conv_ln_b, m_conv_w_proj, m_conv_b_proj, m_w_out, m_ffn2_norm, m_ffn2_w_gu, m_ffn2_w_down, m_final_norm, v_meta_tokens, v_ffn1_norm, v_ffn1_w_gu, v_ffn1_w_down, v_mix_norm, v_w_in, v_b_in, v_rnn_conv_w, v_rnn_conv_b, v_rg_w_a, v_rg_b_a, v_rg_w_x, v_rg_b_x, v_rg_lambda, v_rnn_w_proj, v_conv_dw_w, v_conv_dw_b, v_conv_ln_g, v_conv_ln_b, v_conv_w_proj, v_conv_b_proj, v_w_out, v_ffn2_norm, v_ffn2_w_gu, v_ffn2_w_down, v_final_norm):
    w = dict(locals())
    seq, d = x.shape[1], x.shape[2]
    n_meta = meta_tokens.shape[0]
    t_real = n_meta + seq
    tp, tm, tmx_fwd, tmx, tmt, tmw = _tiles(t_real)
    fb = ffn1_w_gu.shape[-1]
    wr = ffn1_w_down.shape[1]
    f = N_DEV * wr
    fc = f // FFN_CHUNKS
    nbc = w_in.shape[-1]
    n_in = N_DEV * nbc
    pr = rnn_w_proj.shape[1]
    hd = rg_w_a.shape[-1]
    gr = rg_w_a.shape[2]
    cw = meta_tokens.shape[1]
    k4, k31 = rnn_conv_w.shape[1], conv_dw_w.shape[1]
    assert n_in == 6 * d and 2 * wr == fb and N_HEADS * hd == d and pr * N_DEV == d

    xi, yi, ci = lax.axis_index("x"), lax.axis_index("y"), lax.axis_index("c")
    core = ci.astype(jnp.int32).reshape(1)
    chip = (2 * xi + yi).astype(jnp.int32).reshape(1)
    me_index = (4 * xi + 2 * yi + ci).astype(jnp.int32).reshape(1)

    for nm in ("ffn1_w_gu", "ffn2_w_gu"):
        for pre in ("", "m_", "v_"):
            w[pre + nm] = jnp.swapaxes(w[pre + nm], 1, 2)

    wgut1 = w["ffn1_w_gu"][0].astype(BF16)
    wgut2 = w["ffn2_w_gu"][0].astype(BF16)
    wd1 = ffn1_w_down[0].astype(BF16)
    wd2 = ffn2_w_down[0].astype(BF16)
    win_loc = w_in[0].astype(BF16)
    win_t_loc = jnp.swapaxes(w_in[0], 0, 1).astype(BF16)
    w3_loc = jnp.concatenate([rnn_w_proj[0], conv_w_proj[0], w_out[0]], axis=0).astype(BF16)
    wg_loc = jnp.stack([rg_w_a[0], rg_w_x[0]]).astype(BF16)
    n_small = n_meta + k4 + k31
    small_rows = -(-n_small // SUBLANES) * SUBLANES
    small_loc = jnp.concatenate([meta_tokens, rnn_conv_w[0], conv_dw_w[0],
                                 jnp.zeros((small_rows - n_small, cw), F32)], axis=0)
    (wgut1_all, wd1_all, small_all), h0, tgt = _first_gather(
        [wgut1, wd1, small_loc], 2, x[0], loss_target[0], n_meta, tp)
    small_full = small_all.transpose(1, 0, 2).reshape(small_rows, d)
    cw4 = small_full[n_meta:n_meta + k4]
    cw31 = small_full[n_meta + k4:n_meta + k4 + k31]

    wgu1, wdn1 = wgut1_all.reshape(2 * f, d), wd1_all.reshape(f, d)
    (h1, gu1, n1), (win_all, w3_all, wg_all) = _ffn_fwd(
        h0, ffn1_norm, wgu1, wdn1, tm, comm=_Gather([win_loc, w3_loc, wg_loc], pass_on_at=(0.65, 0.95)))
    wg = wg_all.transpose(1, 2, 0, 3, 4).reshape(2, N_HEADS, hd, hd)
    (h2, proj, n2, xr_s, hs_s, v1_s, ya_s, yb_s), (wgut2_all, wd2_all) = _mixer_fwd(
        h1, mix_norm, b_in, win_all, cw4, rnn_conv_b, wg, rg_b_a, rg_b_x, rg_lambda, cw31, conv_dw_b, conv_ln_g,
        conv_ln_b, conv_b_proj, w3_all, tmx_fwd, comm=_Gather([wgut2, wd2], pass_on_at=(0.3, 0.5)))
    wgu2, wdn2 = wgut2_all.reshape(2 * f, d), wd2_all.reshape(f, d)
    (dh3, gu2, n3, tail), (win_t_all,) = _ffn_fwd(
        h2, ffn2_norm, wgu2, wdn2, tm, loss=(tgt, final_norm.reshape(1, d), n_meta, t_real),
        comm=_Gather([win_t_loc], pass_on_at=(0.45, 0.75)))
    win_t = win_t_all.reshape(n_in, d)

    def d_w_gu(tag, dgu, n_s, comm=None):
        g, extra = _tn_matmul(
            "d_w_gu" + tag, dgu, n_s,
            pl.BlockSpec((None, tmt, fc), lambda b, i: (b // FFN_CHUNKS, i, b % FFN_CHUNKS)),
            pl.BlockSpec((tmt, d), lambda b, i: (i, 0)),
            2 * FFN_CHUNKS, fc, d, tmt, tp, (2 * FFN_CHUNKS, fc, d),
            pl.BlockSpec((None, fc, d), lambda b, i: (b, 0, 0)), (fc, d), comm)
        return g.reshape(N_DEV, fb, d), extra

    def d_w_down(tag, act, df, comm=None):
        g, extra = _tn_matmul(
            "d_w_down" + tag, act, df,
            pl.BlockSpec((tmt, fc), lambda b, i: (i, b)), pl.BlockSpec((tmt, d), lambda b, i: (i, 0)),
            FFN_CHUNKS, fc, d, tmt, tp, (FFN_CHUNKS, fc, d),
            pl.BlockSpec((None, fc, d), lambda b, i: (b, 0, 0)), (fc, d), comm)
        return g.reshape(N_DEV, wr, d), extra

    (dh2, dgu2, act2, df2, tail), _ = _ffn_bwd(dh3, h2, gu2, ffn2_norm, wgu2, wdn2, tmx, tail, TAIL_FFN2, n3)
    g_wgu2, _ = d_w_gu("2", dgu2, n3)
    g_wd2, _ = d_w_down("2", act2, df2)
    (dh1, dproj, x3, y3, yg, sg), (r_wd2, r_wgu2) = _mixer_bwd(
        dh2, h1, mix_norm, proj, xr_s, hs_s, v1_s, ya_s, yb_s, win_t, cw4, wg, rg_b_a, rg_b_x, rg_lambda, cw31,
        conv_ln_g, conv_ln_b, w3_all, tmx, comm=_Scatter([g_wd2, g_wgu2]))
    g_w3, _ = _tn_matmul(
        "d_w_proj3", x3, y3,
        pl.BlockSpec((tmw, d), lambda b, i: (i, b)), pl.BlockSpec((tmw, d), lambda b, i: (i, b)),
        3, d, d, tmw, tp, (N_DEV, 3, pr, d), pl.BlockSpec((N_DEV, None, pr, d), lambda b, i: (0, b, 0, 0)),
        (N_DEV, pr, d))
    g_wg, _ = _tn_matmul(
        "d_w_gates", xr_s, yg,
        pl.BlockSpec((tmw, hd), lambda b, i: (i, b % N_HEADS)), pl.BlockSpec((tmw, hd), lambda b, i: (i, b)),
        2 * N_HEADS, hd, hd, tmw, tp, (N_DEV, 2 * N_HEADS, gr, hd),
        pl.BlockSpec((N_DEV, None, gr, hd), lambda b, i: (0, b, 0, 0)), (N_DEV, gr, hd))
    w3_sems, g_w3_thru, w3_land, w3_token = _exchange_start("grads_proj3_exchange", _scatter_copies, N_DEV - 1, g_w3)
    g_win, (r_wg,) = _tn_matmul(
        "d_w_in", n2, dproj,
        pl.BlockSpec((tmw, d), lambda b, i: (i, 0)), pl.BlockSpec((tmw, nbc), lambda b, i: (i, b)),
        N_DEV, d, nbc, tmw, tp, (N_DEV, d, nbc), pl.BlockSpec((None, d, nbc), lambda b, i: (b, 0, 0)), (d, nbc),
        comm=_Scatter([g_wg]), after=w3_token)
    win_sems, g_win_thru, win_land, win_token = _exchange_start("grads_w_in_exchange", _scatter_copies, N_DEV - 1, g_win)
    (dh0, dgu1, act1, df1, tail), _ = _ffn_bwd(dh1, h0, gu1, ffn1_norm, wgu1, wdn1, tmx, tail, TAIL_FFN1, win_token)

    pieces = [sg, dh0[:n_meta], tail]
    assert all(p.shape[0] % SUBLANES == 0 for p in pieces)
    at = [0, sg.shape[0], sg.shape[0] + n_meta]
    loss_row = at[2] + TAIL_LOSS
    rep_rows = [("ffn1_norm", at[2] + TAIL_FFN1, 1), ("mix_norm", SG_MIX, 1), ("b_in", SG_BIN, 6),
                ("rnn_conv_b", SG_CB4, 1),
                ("rg_b_a", SG_BA, 1), ("rg_b_x", SG_BX, 1), ("rg_lambda", SG_LAM, 1), ("conv_dw_b", SG_CB31, 1),
                ("conv_ln_g", SG_LNG, 1), ("conv_ln_b", SG_LNB, 1), ("conv_b_proj", SG_BCP, 1),
                ("ffn2_norm", at[2] + TAIL_FFN2, 1), ("final_norm", at[2] + TAIL_FINAL, 1)]
    col_rows = [("meta_tokens", at[1], n_meta), ("rnn_conv_w", SG_CW4, k4), ("conv_dw_w", SG_CW31, k31)]
    layout = []
    for nm, row0, nr in rep_rows:
        kind = "wide" if nm == "b_in" else "rep"
        as2d = lambda a: a.reshape(1, -1) if a.ndim == 1 else a
        layout.append((kind, row0, nr, as2d(w[nm]), as2d(w["m_" + nm]), as2d(w["v_" + nm])))
    for nm, row0, nr in col_rows:
        sq = lambda a: a.reshape(a.shape[-2], a.shape[-1])
        layout.append(("col", row0, nr, sq(w[nm]), sq(w["m_" + nm]), sq(w["v_" + nm])))
    small_partial = jnp.concatenate(pieces, axis=0)

    g_wd1, (small_partials,) = d_w_down("1", act1, df1, comm=_Bcast(small_partial))
    g_wgu1, (r_wd1,) = d_w_gu("1", dgu1, n1, comm=_Scatter([g_wd1]))

    g_last = g_wgu1.reshape((4, 2) + g_wgu1.shape[1:])
    comb_wgu1 = _pair_reduce(g_last, core)
    sems, comb_thru, land_thru, after = _exchange_start("grads_chip_exchange", _chip_copies, 3, comb_wgu1)
    g_win, r_win = _exchange_wait("grads_w_in_exchange", _scatter_copies, win_sems, g_win_thru, win_land, after)
    g_w3, r_w3 = _exchange_wait("grads_proj3_exchange", _scatter_copies, w3_sems, g_w3_thru, w3_land, after)

    grad_x = (dh0[n_meta:t_real] + after[0, 0])[None]
    total, small_out = _small_adamw(small_partials, layout, me_index, grad_x)
    after = total

    groups = [(g_wd1, r_wd1, me_index, ["ffn1_w_down"]),
              (g_wd2, r_wd2, me_index, ["ffn2_w_down"]), (g_wgu2, r_wgu2, me_index, ["ffn2_w_gu"]),
              (g_win, r_win, me_index, ["w_in"]), (g_w3, r_w3, me_index, ["w_out", "rnn_w_proj", "conv_w_proj"]),
              (g_wg, r_wg, me_index, ["rg_w_a", "rg_w_x"]), (None, None, chip, ["ffn1_w_gu"])]
    res = {}
    for own, recv, idx, group in groups:
        if own is None:
            own, recv = _exchange_wait("grads_chip_exchange", _chip_copies, sems, comb_thru, land_thru, after)
        outs = _final_adamw(own, recv, idx, [(w[nm], w["m_" + nm], w["v_" + nm]) for nm in group], after)
        after = outs[-1][0]
        for nm, o in zip(group, outs):
            res[nm] = o
    for nm in ("ffn1_w_gu", "ffn2_w_gu"):
        res[nm] = tuple(jnp.swapaxes(a, 1, 2) for a in res[nm])
    for (nm, _, _), o in zip(rep_rows + col_rows, small_out):
        res[nm] = tuple(a.reshape(w[nm].shape) for a in o)


    order = ["meta_tokens", "ffn1_norm", "ffn1_w_gu", "ffn1_w_down", "mix_norm", "w_in", "b_in", "rnn_conv_w",
             "rnn_conv_b", "rg_w_a", "rg_b_a", "rg_w_x", "rg_b_x", "rg_lambda", "rnn_w_proj", "conv_dw_w",
             "conv_dw_b", "conv_ln_g", "conv_ln_b", "conv_w_proj", "conv_b_proj", "w_out", "ffn2_norm",
             "ffn2_w_gu", "ffn2_w_down", "final_norm"]
    return (total[loss_row, 0], grad_x, *[res[nm][0] for nm in order], *[res[nm][1] for nm in order],
            *[res[nm][2] for nm in order], *[res[nm][3] for nm in order])
```

```python
import functools
import math

import jax
import jax.numpy as jnp
from jax import lax
from jax.experimental import pallas as pl
from jax.experimental.pallas import tpu as pltpu

F32 = jnp.float32
BF16 = jnp.bfloat16
MESH = pl.DeviceIdType.MESH
N_DEV = 8
N_HEADS = 4
RG_LRU_C = 8.0
EPS = 1e-6
FFN_RES = 0.5
ADAM_LR, ADAM_B1, ADAM_B2, ADAM_EPS, ADAM_WD, ADAM_STEP = 0.001, 0.9, 0.999, 1e-08, 0.01, 10
V7X_VMEM_LIMIT = 56 * 1024 * 1024
CONV4_HALO = 8
CONV31_HALO = 32
SUBLANES = 8
STAGE_ROWS = 512
TAIL_FFN1, TAIL_FINAL, TAIL_LOSS, TAIL_FFN2 = 0, 1, 2, 3
FFN_CHUNKS = 2
FFN_FWD_CHUNKS = 1
GELU_C = math.sqrt(2.0 / math.pi)
GELU_K = 0.044715


def _any():
    return pl.BlockSpec(memory_space=pl.ANY)


def _params(n_grid):
    return pltpu.CompilerParams(dimension_semantics=("arbitrary",) * n_grid, vmem_limit_bytes=V7X_VMEM_LIMIT)


def _nn(a, b):
    return jnp.dot(a, b, preferred_element_type=F32)


def _nt(a, b):
    return lax.dot_general(a, b, (((1,), (1,)), ((), ())), preferred_element_type=F32)


def _tn(a, b):
    return lax.dot_general(a, b, (((0,), (0,)), ((), ())), preferred_element_type=F32)


def _sigmoid(x):
    return 0.5 * jnp.tanh(0.5 * x) + 0.5


def _rowsum(x):
    return jnp.sum(x, axis=0, keepdims=True)


def _rms_fwd(x, g):
    r = lax.rsqrt(jnp.mean(x * x, axis=-1, keepdims=True) + EPS)
    return x * r * g, r


def _rms_bwd(dn, x, r, g):
    xr = x * r
    gy = dn * g
    dx = r * (gy - xr * jnp.mean(gy * xr, axis=-1, keepdims=True))
    return dx, _rowsum(dn * xr)


def _gelu(y):
    t = jnp.tanh(GELU_C * (y + GELU_K * y * y * y))
    return 0.5 * y * (1.0 + t), t


def _gelu_grad(y, t):
    return 0.5 * (1.0 + t) + 0.5 * y * (1.0 - t * t) * GELU_C * (1.0 + 3.0 * GELU_K * y * y)


def _softplus(x):
    return jnp.maximum(x, 0.0) + jnp.log(1.0 + jnp.exp(-jnp.abs(x)))


def _one_minus_exp(z):
    series = -z * (1.0 + 0.5 * z * (1.0 + z * (1.0 / 3.0) * (1.0 + 0.25 * z)))
    return jnp.where(z > -0.05, series, 1.0 - jnp.exp(z))


def _tiles(t_real):
    if t_real > 2048:
        tm = 416
        tp = -(-t_real // tm) * tm
        return tp, tm, tm // 2, tm // 2, tp, tp
    tm = 128
    tp = -(-t_real // tm) * tm
    return tp, tm, tm // 2, tm // 2, tm, tm


def _load_weights(copies, sems):
    cps = [pltpu.make_async_copy(s, d, sems.at[k]) for k, (s, d) in enumerate(copies)]
    for cp in cps:
        cp.start()
    for cp in cps:
        cp.wait()


def _position():
    x, y, c = lax.axis_index("x"), lax.axis_index("y"), lax.axis_index("c")
    chips = [(1 - x, y), (x, 1 - y), (1 - x, 1 - y)]
    return x, y, c, chips


def _slot(p):
    return 4 * p[0] + 2 * p[1] + p[2]


class _Lazy(dict):
    def __getitem__(self, key):
        val = dict.__getitem__(self, key)
        return val() if callable(val) else val


class _Gather:
    def __init__(self, shards, pass_on_at=None):
        self.shards = list(shards)
        self.n = len(self.shards)
        self.pass_on_at = pass_on_at

    def inputs(self):
        return self.shards

    def out_shape(self):
        return [jax.ShapeDtypeStruct((N_DEV,) + s.shape, s.dtype) for s in self.shards]

    N_SEMS = 9

    def scratch(self):
        return [pltpu.SemaphoreType.DMA((self.N_SEMS * self.n,)), pltpu.SemaphoreType.DMA((self.N_SEMS * self.n,)),
                pltpu.SemaphoreType.DMA((self.n,))]

    def _plan(self, ins, outs, sems):
        send_sems, recv_sems, local_sems = sems
        x, y, c, _ = _position()
        me, sib, xn, yn, dg = (x, y, c), (x, y, 1 - c), (1 - x, y, c), (x, 1 - y, c), (1 - x, 1 - y, c)
        other = lambda p: (p[0], p[1], 1 - c)

        def blk(a, p, half=None):
            ref = outs[a].at[_slot(p)]
            if half is None:
                return ref
            rows = self.shards[a].shape[0] // 2
            return ref.at[pl.ds(half * rows, rows)]

        def copy(a, k, dst, to, src=None):
            return pltpu.make_async_remote_copy(
                src_ref=dst if src is None else src, dst_ref=dst,
                send_sem=send_sems.at[self.N_SEMS * a + k], recv_sem=recv_sems.at[self.N_SEMS * a + k],
                device_id=to, device_id_type=MESH)

        cp = _Lazy(mine=lambda: [pltpu.make_async_copy(ins[a], blk(a, me), local_sems.at[a]) for a in range(self.n)])
        for a in range(self.n):
            cp[a] = _Lazy(
                own=lambda a=a: [copy(a, 0, blk(a, me), sib, src=ins[a]), copy(a, 1, blk(a, me), xn, src=ins[a]),
                                 copy(a, 2, blk(a, me), yn, src=ins[a])],
                from_x=lambda a=a: copy(a, 1, blk(a, xn), me), from_y=lambda a=a: copy(a, 2, blk(a, yn), me),
                relay_x=lambda a=a: copy(a, 3, blk(a, xn, 0), yn), relay_y=lambda a=a: copy(a, 4, blk(a, yn, 1), xn),
                diag0=lambda a=a: copy(a, 3, blk(a, dg, 0), me), diag1=lambda a=a: copy(a, 4, blk(a, dg, 1), me),
                pass_x=lambda a=a: copy(a, 5, blk(a, xn), sib), pass_y=lambda a=a: copy(a, 6, blk(a, yn), sib),
                pass_d0=lambda a=a: copy(a, 7, blk(a, dg, 0), sib), pass_d1=lambda a=a: copy(a, 8, blk(a, dg, 1), sib),
                from_sib=lambda a=a: [copy(a, 0, blk(a, sib), me), copy(a, 5, blk(a, other(xn)), me),
                                      copy(a, 6, blk(a, other(yn)), me), copy(a, 7, blk(a, other(dg), 0), me),
                                      copy(a, 8, blk(a, other(dg), 1), me)])
        return cp

    def start(self, ins, outs, sems):
        cp = self._plan(ins, outs, sems)
        for c in cp["mine"]:
            c.start()
        for a in range(self.n):
            for c in cp[a]["own"]:
                c.start()

    def pass_on(self, ins, outs, sems):
        cp = self._plan(ins, outs, sems)
        for a in range(self.n):
            cp[a]["from_x"].wait_recv()
            cp[a]["relay_x"].start()
            cp[a]["pass_x"].start()
        for a in range(self.n):
            cp[a]["from_y"].wait_recv()
            cp[a]["relay_y"].start()
            cp[a]["pass_y"].start()

    def pass_on_relayed(self, ins, outs, sems):
        cp = self._plan(ins, outs, sems)
        for a in range(self.n):
            cp[a]["diag0"].wait_recv()
            cp[a]["pass_d0"].start()
            cp[a]["diag1"].wait_recv()
            cp[a]["pass_d1"].start()

    def finish(self, ins, outs, sems):
        if self.pass_on_at is None:
            self.pass_on(ins, outs, sems)
            self.pass_on_relayed(ins, outs, sems)
        cp = self._plan(ins, outs, sems)
        for a in range(self.n):
            for c in cp[a]["from_sib"]:
                c.wait_recv()
            for c in cp[a]["own"] + [cp[a][k] for k in ("relay_x", "relay_y", "pass_x", "pass_y", "pass_d0", "pass_d1")]:
                c.wait_send()
        for c in cp["mine"]:
            c.wait()


class _Scatter:
    def __init__(self, grads):
        self.grads = list(grads)
        self.n = len(self.grads)

    def inputs(self):
        return self.grads

    def out_shape(self):
        return [jax.ShapeDtypeStruct((N_DEV - 1,) + g.shape[1:], g.dtype) for g in self.grads]

    def scratch(self):
        return [pltpu.SemaphoreType.DMA((7 * self.n,)), pltpu.SemaphoreType.DMA((7 * self.n,))]

    def _plan(self, ins, outs, sems):
        send_sems, recv_sems = sems
        x, y, c, _ = _position()
        cps = []
        for a in range(self.n):
            for k in range(1, N_DEV):
                peer = (x ^ (k >> 2), y ^ ((k >> 1) & 1), c ^ (k & 1))
                cps.append(pltpu.make_async_remote_copy(
                    src_ref=ins[a].at[_slot(peer)], dst_ref=outs[a].at[k - 1],
                    send_sem=send_sems.at[7 * a + k - 1], recv_sem=recv_sems.at[7 * a + k - 1],
                    device_id=peer, device_id_type=MESH))
        return cps

    def start(self, ins, outs, sems):
        for cp in self._plan(ins, outs, sems):
            cp.start()

    def finish(self, ins, outs, sems):
        for cp in self._plan(ins, outs, sems):
            cp.wait()


def _hosted(inner, n_in, n_out, comm, grid):
    if comm is None:
        return inner
    nc_in, nc_out, ns = len(comm.inputs()), len(comm.out_shape()), len(comm.scratch())

    def body(*refs):
        o0 = n_in + nc_in
        s0 = o0 + n_out + nc_out
        main = refs[:n_in] + refs[o0:o0 + n_out] + refs[s0:len(refs) - ns]
        c_in, c_out, c_sems = refs[n_in:o0], refs[o0 + n_out:s0], refs[len(refs) - ns:]
        ids = [pl.program_id(ax) for ax in range(len(grid))]
        first = functools.reduce(jnp.logical_and, [i == 0 for i in ids])
        last = functools.reduce(jnp.logical_and, [i == g - 1 for i, g in zip(ids, grid)])

        @pl.when(first)
        def _():
            comm.start(c_in, c_out, c_sems)

        inner(*main)

        if getattr(comm, "pass_on_at", None) is not None:
            assert len(grid) == 1
            first_at, second_at = (min(grid[0] - 1, int(frac * grid[0])) for frac in comm.pass_on_at)
            assert first_at < second_at

            @pl.when(ids[0] == first_at)
            def _():
                comm.pass_on(c_in, c_out, c_sems)

            @pl.when(ids[0] == second_at)
            def _():
                comm.pass_on_relayed(c_in, c_out, c_sems)

        @pl.when(last)
        def _():
            comm.finish(c_in, c_out, c_sems)

    return body


def _call(inner, name, grid, in_specs, out_specs, out_shape, scratch, args, comm=None):
    n_in, n_out = len(args), len(out_shape)
    body = _hosted(inner, n_in, n_out, comm, grid)
    if comm is not None:
        in_specs = list(in_specs) + [_any()] * len(comm.inputs())
        args = list(args) + comm.inputs()
        out_specs = list(out_specs) + [_any()] * len(comm.out_shape())
        out_shape = list(out_shape) + comm.out_shape()
        scratch = list(scratch) + comm.scratch()
    outs = pl.pallas_call(
        body, name=name, grid=grid, in_specs=list(in_specs), out_specs=list(out_specs), out_shape=list(out_shape),
        scratch_shapes=list(scratch), compiler_params=_params(len(grid)))(*args)
    return list(outs[:n_out]), list(outs[n_out:])


class _Bcast:
    def __init__(self, block):
        self.block = block

    def inputs(self):
        return [self.block]

    def out_shape(self):
        return [jax.ShapeDtypeStruct((N_DEV,) + self.block.shape, self.block.dtype)]

    def scratch(self):
        return [pltpu.SemaphoreType.DMA((N_DEV - 1,)), pltpu.SemaphoreType.DMA((N_DEV - 1,)),
                pltpu.SemaphoreType.DMA((1,))]

    def _plan(self, ins, outs, sems):
        send_sems, recv_sems, local_sem = sems
        x, y, c, _ = _position()
        mine = outs[0].at[_slot((x, y, c))]
        cps = []
        for k in range(1, N_DEV):
            peer = (x ^ (k >> 2), y ^ ((k >> 1) & 1), c ^ (k & 1))
            cps.append(pltpu.make_async_remote_copy(
                src_ref=ins[0], dst_ref=mine, send_sem=send_sems.at[k - 1], recv_sem=recv_sems.at[k - 1],
                device_id=peer, device_id_type=MESH))
        return pltpu.make_async_copy(ins[0], mine, local_sem.at[0]), cps

    def start(self, ins, outs, sems):
        own, cps = self._plan(ins, outs, sems)
        own.start()
        for cp in cps:
            cp.start()

    def finish(self, ins, outs, sems):
        own, cps = self._plan(ins, outs, sems)
        for cp in cps:
            cp.wait()
        own.wait()


def _first_gather(shards, small_idx, x2, t2, n_meta, tp):
    comm = _Gather(shards)
    n = comm.n
    seq, d = x2.shape
    t_real = n_meta + seq
    n_pad = tp - t_real
    cw = d // N_DEV
    rows = STAGE_ROWS if seq % STAGE_ROWS == 0 else seq
    n_chunks = seq // rows

    def body(*refs):
        ins, (x_ref, t_ref) = refs[:n], refs[n:n + 2]
        outs, (h0_ref, tg_ref) = refs[n + 2:2 * n + 2], refs[2 * n + 2:2 * n + 4]
        sems = refs[2 * n + 4:2 * n + 7]
        buf, zeros, in_sems, out_sems, misc_sems = refs[2 * n + 7:]
        comm.start(ins, outs, sems)
        zeros[...] = jnp.zeros_like(zeros)
        fills = [pltpu.make_async_copy(zeros.at[pl.ds(0, n_pad)], h0_ref.at[pl.ds(t_real, n_pad)], misc_sems.at[0]),
                 pltpu.make_async_copy(zeros.at[pl.ds(0, n_pad)], tg_ref.at[pl.ds(t_real, n_pad)], misc_sems.at[1]),
                 pltpu.make_async_copy(zeros.at[pl.ds(0, n_meta)], tg_ref.at[pl.ds(0, n_meta)], misc_sems.at[2])]
        for cp in fills:
            cp.start()
        jobs = [(src, dst, c) for src, dst in ((x_ref, h0_ref), (t_ref, tg_ref)) for c in range(n_chunks)]

        def load(k):
            src, _, c = jobs[k]
            return pltpu.make_async_copy(src.at[pl.ds(c * rows, rows)], buf.at[k % 2], in_sems.at[k % 2])

        def store(k):
            _, dst, c = jobs[k]
            return pltpu.make_async_copy(buf.at[k % 2], dst.at[pl.ds(n_meta + c * rows, rows)], out_sems.at[k % 2])

        load(0).start()
        for k in range(len(jobs)):
            load(k).wait()
            if k + 1 < len(jobs):
                if k >= 1:
                    store(k - 1).wait()
                load(k + 1).start()
            store(k).start()
        for k in range(max(0, len(jobs) - 2), len(jobs)):
            store(k).wait()
        comm.finish(ins, outs, sems)
        meta = [pltpu.make_async_copy(outs[small_idx].at[k, pl.ds(0, n_meta)],
                                      h0_ref.at[pl.ds(0, n_meta), pl.ds(k * cw, cw)], misc_sems.at[3 + k])
                for k in range(N_DEV)]
        for cp in meta:
            cp.start()
        for cp in fills + meta:
            cp.wait()

    staged = [jax.ShapeDtypeStruct((tp, d), F32)] * 2
    outs = pl.pallas_call(
        body, name="weights_all_gather", out_shape=comm.out_shape() + staged,
        in_specs=[_any()] * (n + 2), out_specs=[_any()] * (n + 2),
        scratch_shapes=comm.scratch() + [
            pltpu.VMEM((2, rows, d), F32), pltpu.VMEM((max(n_pad, n_meta), d), F32),
            pltpu.SemaphoreType.DMA((2,)), pltpu.SemaphoreType.DMA((2,)), pltpu.SemaphoreType.DMA((3 + N_DEV,))],
        compiler_params=pltpu.CompilerParams(vmem_limit_bytes=V7X_VMEM_LIMIT),
    )(*shards, x2, t2)
    return outs[:n], outs[n], outs[n + 1]


def _chip_copies(c_ref, land_ref, sems):
    _, _, c, chips = _position()
    return [pltpu.make_async_remote_copy(
        src_ref=c_ref.at[2 * cx + cy], dst_ref=land_ref.at[j], send_sem=sems[j], recv_sem=sems[3 + j],
        device_id=(cx, cy, c), device_id_type=MESH) for j, (cx, cy) in enumerate(chips)]


def _scatter_copies(g_ref, land_ref, sems):
    x, y, c, _ = _position()
    cps = []
    for k in range(1, N_DEV):
        peer = (x ^ (k >> 2), y ^ ((k >> 1) & 1), c ^ (k & 1))
        cps.append(pltpu.make_async_remote_copy(
            src_ref=g_ref.at[_slot(peer)], dst_ref=land_ref.at[k - 1], send_sem=sems[k - 1],
            recv_sem=sems[N_DEV - 1 + k - 1], device_id=peer, device_id_type=MESH))
    return cps


def _exchange_start(name, copies, n_copies, src):
    hbm = pl.BlockSpec(memory_space=pltpu.HBM)
    sem = pl.BlockSpec(memory_space=pltpu.SEMAPHORE)
    n_sems = 2 * n_copies

    def body(s_ref, land_ref, *refs):
        for cp in copies(s_ref, land_ref, refs[:n_sems]):
            cp.start()
        token = refs[n_sems + 2]
        token[...] = jnp.zeros_like(token)

    land = lax.empty((n_copies,) + src.shape[1:], src.dtype)
    outs = pl.pallas_call(
        body, name=name + "_start",
        out_shape=(pltpu.SemaphoreType.DMA(()),) * n_sems
        + (pltpu.HBM(src.shape, src.dtype), pltpu.HBM(land.shape, land.dtype),
           jax.ShapeDtypeStruct((SUBLANES, 128), F32)),
        in_specs=(hbm, hbm), out_specs=(sem,) * n_sems + (hbm, hbm, pl.BlockSpec(memory_space=pltpu.VMEM)),
        input_output_aliases={0: n_sems, 1: n_sems + 1},
        compiler_params=pltpu.CompilerParams(has_side_effects=pltpu.SideEffectType.DATAFLOW_SIDE_EFFECTING),
    )(pltpu.with_memory_space_constraint(src, pltpu.HBM), pltpu.with_memory_space_constraint(land, pltpu.HBM))
    return outs[:n_sems], outs[n_sems], outs[n_sems + 1], outs[n_sems + 2]


def _exchange_wait(name, copies, sems, src_thru, land_thru, after):
    hbm = pl.BlockSpec(memory_space=pltpu.HBM)
    sem = pl.BlockSpec(memory_space=pltpu.SEMAPHORE)
    n_sems = len(sems)

    def body(s_ref, land_ref, *refs):
        for cp in copies(s_ref, land_ref, refs[:n_sems]):
            cp.wait_send()
            cp.wait_recv()

    return pl.pallas_call(
        body, name=name + "_wait",
        out_shape=(pltpu.HBM(src_thru.shape, src_thru.dtype), pltpu.HBM(land_thru.shape, land_thru.dtype)),
        in_specs=(hbm, hbm) + (sem,) * n_sems + (pl.BlockSpec(memory_space=pl.ANY),), out_specs=(hbm, hbm),
        input_output_aliases={0: 0, 1: 1},
        compiler_params=pltpu.CompilerParams(has_side_effects=pltpu.SideEffectType.DATAFLOW_SIDE_EFFECTING),
    )(src_thru, land_thru, *sems, after)


def _pair_reduce(grad, core):
    blk = grad.shape[2:]
    zeros = (0,) * len(blk)

    def body(core_ref, g_hbm, own_ref, o_ref, landed, send_sems, recv_sems):
        del core_ref
        i = pl.program_id(0)
        x, y, c, _ = _position()

        def copy(k):
            return pltpu.make_async_remote_copy(
                src_ref=g_hbm.at[k, 1 - c], dst_ref=landed.at[k], send_sem=send_sems.at[k],
                recv_sem=recv_sems.at[k], device_id=(x, y, 1 - c), device_id_type=MESH)

        @pl.when(i == 0)
        def _():
            for k in range(4):
                copy(k).start()

        for k in range(4):
            @pl.when(i == k)
            def _(k=k):
                copy(k).wait_recv()

        o_ref[...] = (own_ref[...].astype(F32) + landed[i].astype(F32)).astype(BF16)

        @pl.when(i == 3)
        def _():
            for k in range(4):
                copy(k).wait_send()

    return pl.pallas_call(
        body, name="grads_pair_reduce",
        out_shape=jax.ShapeDtypeStruct((4,) + blk, BF16),
        grid_spec=pltpu.PrefetchScalarGridSpec(
            num_scalar_prefetch=1, grid=(4,),
            in_specs=[_any(), pl.BlockSpec((None, None) + blk, lambda i, cr: (i, cr[0]) + zeros)],
            out_specs=pl.BlockSpec((None,) + blk, lambda i, cr: (i,) + zeros),
            scratch_shapes=[pltpu.VMEM((4,) + blk, BF16), pltpu.SemaphoreType.DMA((4,)),
                            pltpu.SemaphoreType.DMA((4,))]),
        compiler_params=_params(1),
    )(core, grad, grad)


def _adamw(w, g, m, v):
    m2 = ADAM_B1 * m + (1.0 - ADAM_B1) * g
    v2 = ADAM_B2 * v + (1.0 - ADAM_B2) * (g * g)
    m_hat = m2 / (1.0 - ADAM_B1 ** ADAM_STEP)
    v_hat = v2 / (1.0 - ADAM_B2 ** ADAM_STEP)
    delta = -ADAM_LR * (m_hat / (jnp.sqrt(v_hat) + ADAM_EPS) + ADAM_WD * w)
    return delta, m2, v2


def _final_adamw(own, recv, idx, parts, after):
    blk = own.shape[1:]
    n_recv = recv.shape[0]
    n_parts = len(parts)
    per = blk[0] // n_parts if n_parts > 1 else None
    rows = blk[-2]
    n_chunks = 1 if n_parts > 1 else (4 if rows % 64 == 0 and rows >= 512 else (2 if rows % 32 == 0 else 1))
    cblk = blk[:-2] + (rows // n_chunks, blk[-1])
    lead = (0,) * (len(blk) - 2)

    def body(idx_ref, c_ref, r_ref, after_ref, *refs):
        del idx_ref, after_ref
        ins, outs = refs[:3 * n_parts], refs[3 * n_parts:]
        g = c_ref[...].astype(F32)
        for k in range(n_recv):
            g = g + r_ref[k].astype(F32)
        for p in range(n_parts):
            w_ref, m_ref, v_ref = ins[3 * p:3 * p + 3]
            if n_parts == 1:
                gp = g
            elif per == 1:
                gp = g[p]
            else:
                gp = g[p * per:(p + 1) * per]
            delta, m2, v2 = _adamw(w_ref[0], gp, m_ref[0], v_ref[0])
            o = outs[4 * p:4 * p + 4]
            o[0][0] = gp
            o[1][0] = delta
            o[2][0] = m2
            o[3][0] = v2

    flat = [a for wmv in parts for a in wmv]

    def part_spec(a):
        shape = a.shape[:-2] + (a.shape[-2] // n_chunks, a.shape[-1])
        return pl.BlockSpec(shape, lambda i, cr, nd=a.ndim: (0,) * (nd - 2) + (i, 0))

    outs = pl.pallas_call(
        body, name="grads_sum_adamw",
        out_shape=[jax.ShapeDtypeStruct(wmv[0].shape, F32) for wmv in parts for _ in range(4)],
        grid_spec=pltpu.PrefetchScalarGridSpec(
            num_scalar_prefetch=1, grid=(n_chunks,),
            in_specs=[pl.BlockSpec((None,) + cblk, lambda i, cr: (cr[0],) + lead + (i, 0)),
                      pl.BlockSpec((n_recv,) + cblk, lambda i, cr: (0,) + lead + (i, 0))]
                     + [_any()] + [part_spec(a) for a in flat],
            out_specs=[part_spec(wmv[0]) for wmv in parts for _ in range(4)]),
        compiler_params=_params(1),
    )(idx, own, recv, after, *flat)
    return [tuple(outs[4 * p:4 * p + 4]) for p in range(n_parts)]


def _small_adamw(partials, layout, me_index, after):
    _, rows, d = partials.shape
    n = len(layout)
    cw = d // N_DEV

    def body(me_ref, p_ref, after_ref, *refs):
        ins, t_ref, outs = refs[:3 * n], refs[3 * n], refs[3 * n + 1:]
        me = me_ref[0]
        total = p_ref[0]
        for j in range(1, N_DEV):
            total = total + p_ref[j]
        t_ref[...] = total
        for e, (kind, r0, nr, _, _, _) in enumerate(layout):
            w_ref, m_ref, v_ref = ins[3 * e:3 * e + 3]
            o = outs[4 * e:4 * e + 4]
            if kind == "rep":
                g = t_ref[r0:r0 + nr, :]
                delta, m2, v2 = _adamw(w_ref[...], g, m_ref[...], v_ref[...])
                for ref, val in zip(o, (g, delta, m2, v2)):
                    ref[...] = val
            elif kind == "wide":
                for q in range(nr):
                    sl = slice(q * d, (q + 1) * d)
                    g = t_ref[r0 + q:r0 + q + 1, :]
                    delta, m2, v2 = _adamw(w_ref[:, sl], g, m_ref[:, sl], v_ref[:, sl])
                    for ref, val in zip(o, (g, delta, m2, v2)):
                        ref[:, sl] = val
            else:
                for j in range(N_DEV):
                    @pl.when(me == j)
                    def _(j=j, o=o, w_ref=w_ref, m_ref=m_ref, v_ref=v_ref, r0=r0, nr=nr):
                        g = t_ref[r0:r0 + nr, j * cw:(j + 1) * cw]
                        delta, m2, v2 = _adamw(w_ref[...], g, m_ref[...], v_ref[...])
                        for ref, val in zip(o, (g, delta, m2, v2)):
                            ref[...] = val

    flat = [a for ent in layout for a in ent[3:]]
    vm = pl.BlockSpec(memory_space=pltpu.VMEM)
    outs = pl.pallas_call(
        body, name="small_adamw",
        out_shape=[jax.ShapeDtypeStruct((rows, d), F32)]
                  + [jax.ShapeDtypeStruct(ent[3].shape, F32) for ent in layout for _ in range(4)],
        in_specs=[pl.BlockSpec(memory_space=pltpu.SMEM), vm, _any()] + [vm] * len(flat),
        out_specs=[vm] * (1 + 4 * n),
        compiler_params=pltpu.CompilerParams(vmem_limit_bytes=V7X_VMEM_LIMIT),
    )(me_index, partials, after, *flat)
    return outs[0], [tuple(outs[1 + 4 * e:5 + 4 * e]) for e in range(n)]


def _ffn_fwd(h, g, wgu, wd, tm, loss=None, comm=None):
    tp, d = h.shape
    f = wd.shape[0]
    fc = f // FFN_FWD_CHUNKS
    nt = tp // tm
    with_loss = loss is not None
    if with_loss:
        tgt, gf, n_meta, t_real = loss

    def body(*refs):
        if with_loss:
            (h_ref, g_ref, wgu_hbm, wd_hbm, tgt_ref, gf_ref, out_ref, gu_ref, n_ref, tail_ref,
             wgu_v, wd_v, sems) = refs
        else:
            h_ref, g_ref, wgu_hbm, wd_hbm, out_ref, gu_ref, n_ref, wgu_v, wd_v, sems = refs
        i = pl.program_id(0)

        @pl.when(i == 0)
        def _():
            _load_weights([(wgu_hbm, wgu_v), (wd_hbm, wd_v)], sems)
            if with_loss:
                tail_ref[...] = jnp.zeros_like(tail_ref)

        x = h_ref[...]
        n, _ = _rms_fwd(x, g_ref[...])
        nb = n.astype(BF16)
        n_ref[...] = nb
        acc = jnp.zeros((tm, d), F32)
        for j in range(FFN_FWD_CHUNKS):
            cols = slice(j * fc, (j + 1) * fc)
            gate = _nt(nb, wgu_v[pl.ds(j * fc, fc), :])
            up = _nt(nb, wgu_v[pl.ds(f + j * fc, fc), :])
            gu_ref[0, :, cols] = gate.astype(BF16)
            gu_ref[1, :, cols] = up.astype(BF16)
            act = (gate * _sigmoid(gate) * up).astype(BF16)
            acc = acc + _nn(act, wd_v[pl.ds(j * fc, fc), :])
        hn = x + FFN_RES * acc
        if not with_loss:
            out_ref[...] = hn
        else:
            gfv = gf_ref[...]
            r = lax.rsqrt(jnp.mean(hn * hn, axis=-1, keepdims=True) + EPS)
            xr = hn * r
            rows = i * tm + lax.broadcasted_iota(jnp.int32, (tm, 1), 0)
            mask = jnp.logical_and(rows >= n_meta, rows < t_real)
            diff = jnp.where(mask, xr * gfv - tgt_ref[...], 0.0)
            tail_ref[TAIL_LOSS:TAIL_LOSS + 1, :] += jnp.zeros((1, d), F32) + 0.5 * jnp.sum(diff * diff) / d
            dy = diff / d
            gy = dy * gfv
            out_ref[...] = r * (gy - xr * jnp.mean(gy * xr, axis=-1, keepdims=True))
            tail_ref[TAIL_FINAL:TAIL_FINAL + 1, :] += _rowsum(dy * xr)

    row = pl.BlockSpec((tm, d), lambda i: (i, 0))
    vec = pl.BlockSpec((1, d), lambda i: (0, 0))
    in_specs = [row, vec, _any(), _any()]
    out_shape = [jax.ShapeDtypeStruct((tp, d), F32), jax.ShapeDtypeStruct((2, tp, f), BF16),
                 jax.ShapeDtypeStruct((tp, d), BF16)]
    out_specs = [row, pl.BlockSpec((2, tm, f), lambda i: (0, i, 0)), row]
    args = [h, g, wgu, wd]
    if with_loss:
        in_specs += [row, vec]
        out_shape += [jax.ShapeDtypeStruct((SUBLANES, d), F32)]
        out_specs += [pl.BlockSpec((SUBLANES, d), lambda i: (0, 0))]
        args += [tgt, gf]
    return _call(body, "ffn_fwd_loss" if with_loss else "ffn_fwd", (nt,), in_specs, out_specs, out_shape,
                 [pltpu.VMEM((2 * f, d), BF16), pltpu.VMEM((f, d), BF16), pltpu.SemaphoreType.DMA((2,))],
                 args, comm)


def _ffn_bwd(dh, h, gu, g, wgu, wd, tm, tail, tail_row, after):
    tp, d = h.shape
    f = wd.shape[0]
    fc = f // FFN_CHUNKS
    nt = tp // tm

    def body(dh_ref, h_ref, gu_ref, g_ref, tail_ref, wgu_hbm, wd_hbm, after_ref,
             dhin_ref, dgu_ref, act_ref, df_ref, dg_ref, wgu_v, wd_v, dn_v, sems):
        del after_ref
        i, j = pl.program_id(0), pl.program_id(1)

        @pl.when(jnp.logical_and(i == 0, j == 0))
        def _():
            _load_weights([(wgu_hbm, wgu_v), (wd_hbm, wd_v)], sems)
            dg_ref[...] = tail_ref[...]

        dfb = (FFN_RES * dh_ref[...]).astype(BF16)

        @pl.when(j == 0)
        def _():
            df_ref[...] = dfb
            dn_v[...] = jnp.zeros_like(dn_v)

        lo = pl.multiple_of(j * fc, 16)
        dact = _nt(dfb, wd_v[pl.ds(lo, fc), :])
        gate = gu_ref[0].astype(F32)
        up = gu_ref[1].astype(F32)
        sg = _sigmoid(gate)
        silu = gate * sg
        act_ref[...] = (silu * up).astype(BF16)
        dgate = (dact * up * (sg * (1.0 + gate * (1.0 - sg)))).astype(BF16)
        dup = (dact * silu).astype(BF16)
        dgu_ref[0] = dgate
        dgu_ref[1] = dup
        dn_v[...] += _nn(dgate, wgu_v[pl.ds(lo, fc), :]) + _nn(dup, wgu_v[pl.ds(pl.multiple_of(f + j * fc, 16), fc), :])

        @pl.when(j == FFN_CHUNKS - 1)
        def _():
            x = h_ref[...]
            r = lax.rsqrt(jnp.mean(x * x, axis=-1, keepdims=True) + EPS)
            dx, dgp = _rms_bwd(dn_v[...], x, r, g_ref[...])
            dhin_ref[...] = dh_ref[...] + dx
            dg_ref[tail_row:tail_row + 1, :] += dgp

    row = pl.BlockSpec((tm, d), lambda i, j: (i, 0))
    vec = pl.BlockSpec((1, d), lambda i, j: (0, 0))
    tile = pl.BlockSpec((SUBLANES, d), lambda i, j: (0, 0))
    hid2 = pl.BlockSpec((2, tm, fc), lambda i, j: (0, i, j))
    return _call(
        body, "ffn_bwd", (nt, FFN_CHUNKS),
        [row, row, hid2, vec, tile, _any(), _any(), _any()],
        [row, hid2, pl.BlockSpec((tm, fc), lambda i, j: (i, j)), row, tile],
        [jax.ShapeDtypeStruct((tp, d), F32), jax.ShapeDtypeStruct((2, tp, f), BF16),
         jax.ShapeDtypeStruct((tp, f), BF16), jax.ShapeDtypeStruct((tp, d), BF16),
         jax.ShapeDtypeStruct((SUBLANES, d), F32)],
        [pltpu.VMEM((2 * f, d), BF16), pltpu.VMEM((f, d), BF16), pltpu.VMEM((tm, d), F32),
         pltpu.SemaphoreType.DMA((2,))],
        [dh, h, gu, g, tail, wgu, wd, after])


def _piece_segments(q, d, nb_cols):
    segs = []
    for j in range(N_DEV):
        lo, hi = max(q * d, j * nb_cols), min((q + 1) * d, (j + 1) * nb_cols)
        if lo < hi:
            segs.append((j, lo - q * d, hi - q * d, lo - j * nb_cols, hi - j * nb_cols))
    return segs


def _w3_copies(w3_hbm, rows, w3_v):
    return [(w3_hbm.at[k, pl.ds(q * rows, rows)], w3_v.at[q, pl.ds(k * rows, rows)])
            for q in range(3) for k in range(N_DEV)]


def _gates(xrb, wg_ref, ba, bx, lam, hd):
    pre_r, pre_i = [], []
    for hh in range(N_HEADS):
        xh = xrb[:, hh * hd:(hh + 1) * hd]
        pre_r.append(_nn(xh, wg_ref[0, hh]))
        pre_i.append(_nn(xh, wg_ref[1, hh]))
    r = _sigmoid(jnp.concatenate(pre_r, axis=1) + ba)
    ig = _sigmoid(jnp.concatenate(pre_i, axis=1) + bx)
    sp = _softplus(-lam)
    log_a = -RG_LRU_C * r * sp
    a = jnp.exp(log_a)
    s = jnp.sqrt(_one_minus_exp(2.0 * log_a))
    return r, ig, sp, a, s


def _scan_fwd(a, u, h_prev):
    tm = a.shape[0]
    rows = lax.broadcasted_iota(jnp.int32, a.shape, 0)
    d = 1
    while d < tm:
        if d < SUBLANES:
            keep = rows >= d
            u = jnp.where(keep, a * pltpu.roll(u, d, 0) + u, u)
            a = jnp.where(keep, a * pltpu.roll(a, d, 0), a)
        else:
            u = jnp.concatenate([u[:d], a[d:] * u[:tm - d] + u[d:]], axis=0)
            a = jnp.concatenate([a[:d], a[d:] * a[:tm - d]], axis=0)
        d *= 2
    return u + a * h_prev


def _scan_bwd(b, v, g_next):
    tm = b.shape[0]
    rows = lax.broadcasted_iota(jnp.int32, b.shape, 0)
    d = 1
    while d < tm:
        if d < SUBLANES:
            keep = rows < tm - d
            v = jnp.where(keep, v + b * pltpu.roll(v, tm - d, 0), v)
            b = jnp.where(keep, b * pltpu.roll(b, tm - d, 0), b)
        else:
            v = jnp.concatenate([v[:tm - d] + b[:tm - d] * v[d:], v[tm - d:]], axis=0)
            b = jnp.concatenate([b[:tm - d] * b[d:], b[tm - d:]], axis=0)
        d *= 2
    return v + b * g_next


def _shifted_copies(ext_ref, es_ref, n_rows):
    for s in range(1, SUBLANES):
        es_ref[s, pl.ds(0, n_rows), :] = ext_ref[pl.ds(s, n_rows), :]


def _tap(ext_ref, es_ref, off, tm):
    q, s = divmod(off, SUBLANES)
    if s == 0:
        return ext_ref[pl.ds(SUBLANES * q, tm), :]
    return es_ref[s, pl.ds(SUBLANES * q, tm), :]


def _mixer_fwd(h, g, b_in, win_all, cw4, cb4, wg, ba, bx, lam, cw31, cb31, lng, lnb, bcp, w3_all, tm, comm=None):
    tp, d = h.shape
    nb_cols = win_all.shape[-1]
    n_in = N_DEV * nb_cols
    hd = wg.shape[-1]
    k4, k31 = cw4.shape[0], cw31.shape[0]
    w3_rows = d // N_DEV

    def body(h_ref, g_ref, b_ref, win_hbm, cw4_ref, cb4_ref, wg_ref, ba_ref, bx_ref, lam_ref, cw31_ref, cb31_ref,
             lng_ref, lnb_ref, bcp_ref, w3_hbm,
             h2_ref, p_ref, n_ref, xr_ref, hs_ref, v1_ref, ya_ref, yb_ref,
             win_v, w3_v, ext4, ext31, es31, hcar, sems):
        @pl.when(pl.program_id(0) == 0)
        def _():
            _load_weights([(win_hbm, win_v)] + _w3_copies(w3_hbm, w3_rows, w3_v), sems)
            ext4[pl.ds(0, CONV4_HALO), :] = jnp.zeros((CONV4_HALO, d), F32)
            ext31[pl.ds(0, CONV31_HALO), :] = jnp.zeros((CONV31_HALO, d), F32)
            hcar[...] = jnp.zeros_like(hcar)

        n, _ = _rms_fwd(h_ref[...], g_ref[...])
        nb = n.astype(BF16)
        n_ref[...] = nb

        def piece(q):
            parts = [_nn(nb, win_v[j, :, bl:bh]) for j, _, _, bl, bh in _piece_segments(q, d, nb_cols)]
            pq = (jnp.concatenate(parts, axis=1) + b_ref[:, q * d:(q + 1) * d]).astype(BF16)
            p_ref[:, q * d:(q + 1) * d] = pq
            return pq.astype(F32)

        x_rnn, y_rnn, glu_v, glu_g, gate_a, gate_b = [piece(q) for q in range(6)]

        ext4[pl.ds(CONV4_HALO, tm), :] = x_rnn
        xr = cb4_ref[...] + jnp.zeros((tm, d), F32)
        for k in range(k4):
            xr = xr + cw4_ref[k:k + 1, :] * ext4[pl.ds(CONV4_HALO - (k4 - 1) + k, tm), :]
        ext4[pl.ds(0, CONV4_HALO), :] = ext4[pl.ds(tm, CONV4_HALO), :]
        xrb = xr.astype(BF16)
        xr_ref[...] = xrb
        xr = xrb.astype(F32)
        _, ig, _, a, s = _gates(xrb, wg_ref, ba_ref[...], bx_ref[...], lam_ref[...], hd)
        hseq = _scan_fwd(a, s * (ig * xr), hcar[0:1, :])
        hcar[0:1, :] = hseq[tm - 1:tm, :]
        hs_ref[...] = hseq.astype(BF16)
        gl, _ = _gelu(y_rnn)
        ya = _nn((hseq * gl).astype(BF16), w3_v[0])
        ya_ref[...] = ya.astype(BF16)

        ext31[pl.ds(CONV31_HALO, tm), :] = glu_v * _sigmoid(glu_g)
        _shifted_copies(ext31, es31, tm + CONV31_HALO - SUBLANES)
        v1 = cb31_ref[...] + jnp.zeros((tm, d), F32)
        for k in range(k31):
            v1 = v1 + cw31_ref[k:k + 1, :] * _tap(ext31, es31, CONV31_HALO - (k31 - 1) + k, tm)
        ext31[pl.ds(0, CONV31_HALO), :] = ext31[pl.ds(tm, CONV31_HALO), :]
        v1b = v1.astype(BF16)
        v1_ref[...] = v1b
        v1 = v1b.astype(F32)
        xc = v1 - jnp.mean(v1, axis=-1, keepdims=True)
        rstd = lax.rsqrt(jnp.mean(xc * xc, axis=-1, keepdims=True) + EPS)
        v2 = xc * rstd * lng_ref[...] + lnb_ref[...]
        yb = _nn((v2 * _sigmoid(v2)).astype(BF16), w3_v[1]) + bcp_ref[...]
        yb_ref[...] = yb.astype(BF16)

        merged = _sigmoid(gate_a) * ya + _sigmoid(gate_b) * yb
        h2_ref[...] = h_ref[...] + _nn(merged.astype(BF16), w3_v[2])

    row = pl.BlockSpec((tm, d), lambda i: (i, 0))
    wide = pl.BlockSpec((tm, n_in), lambda i: (i, 0))
    full = lambda a: pl.BlockSpec(a.shape, lambda i, nd=a.ndim: (0,) * nd)
    smalls = [cw4, cb4, wg, ba, bx, lam, cw31, cb31, lng, lnb, bcp]
    return _call(
        body, "mixer_fwd", (tp // tm,),
        [row, full(g), full(b_in), _any()] + [full(a) for a in smalls] + [_any()],
        [row, wide] + [row] * 6,
        [jax.ShapeDtypeStruct((tp, d), F32), jax.ShapeDtypeStruct((tp, n_in), BF16)]
        + [jax.ShapeDtypeStruct((tp, d), BF16)] * 6,
        [pltpu.VMEM(win_all.shape, BF16),
         pltpu.VMEM((3, d, d), BF16),
         pltpu.VMEM((tm + CONV4_HALO, d), F32),
         pltpu.VMEM((tm + CONV31_HALO, d), F32),
         pltpu.VMEM((SUBLANES, tm + CONV31_HALO, d), F32),
         pltpu.VMEM((SUBLANES, d), F32),
         pltpu.SemaphoreType.DMA((1 + 3 * N_DEV,))],
        [h, g, b_in, win_all, *smalls, w3_all], comm)


SG_BIN, SG_CW4, SG_CB4, SG_BA, SG_BX, SG_LAM, SG_CB31, SG_LNG, SG_LNB, SG_BCP, SG_MIX, SG_CW31 = 0, 6, 10, 11, 12, 13, 14, 15, 16, 17, 18, 19


def _mixer_bwd(dh2, h, g, proj, xr_s, hs_s, v1_s, ya_s, yb_s, win_t, cw4, wg, ba, bx, lam, cw31, lng, lnb, w3_all, tm,
               comm=None):
    tp, d = dh2.shape
    n_in = proj.shape[1]
    hd = wg.shape[-1]
    k4, k31 = cw4.shape[0], cw31.shape[0]
    nt = tp // tm
    w3_rows = d // N_DEV
    sg_rows = -(-(SG_CW31 + k31) // SUBLANES) * SUBLANES
    halo_rows = 16
    per = tm // halo_rows

    def body(dh_ref, h_ref, g_ref, p_ref, xr_ref, hs_ref, hh_ref, v1_ref, ya_ref, yb_ref, win_hbm,
             cw4_ref, wg_ref, wgt_ref, ba_ref, bx_ref, lam_ref, cw31_ref, lng_ref, lnb_ref, w3_hbm,
             dh1_ref, dp_ref, x3_ref, y3_ref, yg_ref, sg_ref,
             win_v, w3_v, extd4, extd31, es31, gcar, sems):
        i = pl.program_id(0)
        tile = nt - 1 - i

        @pl.when(i == 0)
        def _():
            _load_weights([(win_hbm, win_v)] + _w3_copies(w3_hbm, w3_rows, w3_v), sems)
            for q in range(3):
                w3_v[q] = w3_v[q].T
            extd4[pl.ds(tm, CONV4_HALO), :] = jnp.zeros((CONV4_HALO, d), F32)
            extd31[pl.ds(tm, CONV31_HALO), :] = jnp.zeros((CONV31_HALO, d), F32)
            gcar[...] = jnp.zeros_like(gcar)
            sg_ref[...] = jnp.zeros_like(sg_ref)

        def acc(row, val):
            sg_ref[row:row + 1, :] += _rowsum(val)

        rows = lax.broadcasted_iota(jnp.int32, (tm, d), 0)
        x_rnn = p_ref[:, 0:d].astype(F32)
        y_rnn = p_ref[:, d:2 * d].astype(F32)
        glu_v = p_ref[:, 2 * d:3 * d].astype(F32)
        glu_g = p_ref[:, 3 * d:4 * d].astype(F32)
        sga = _sigmoid(p_ref[:, 4 * d:5 * d].astype(F32))
        sgb = _sigmoid(p_ref[:, 5 * d:6 * d].astype(F32))
        ya = ya_ref[...].astype(F32)
        yb = yb_ref[...].astype(F32)

        dmob = dh_ref[...].astype(BF16)
        dmerged = _nn(dmob, w3_v[2])
        x3_ref[:, 0:d] = (sga * ya + sgb * yb).astype(BF16)
        y3_ref[:, 0:d] = dmob
        dya = sga * dmerged
        dyb = sgb * dmerged
        dn_parts = []

        def emit(q, val):
            vb = val.astype(BF16)
            dp_ref[:, q * d:(q + 1) * d] = vb
            acc(SG_BIN + q, val)
            term = _nn(vb, win_v[pl.ds(q * d, d), :])
            dn_parts[:] = [term if not dn_parts else dn_parts[0] + term]

        emit(4, dmerged * ya * sga * (1.0 - sga))
        emit(5, dmerged * yb * sgb * (1.0 - sgb))

        dyab = dya.astype(BF16)
        y3_ref[:, d:2 * d] = dyab
        dza = _nn(dyab, w3_v[0])
        hsv = hs_ref[...].astype(F32)
        gl, th = _gelu(y_rnn)
        x3_ref[:, d:2 * d] = (hsv * gl).astype(BF16)
        emit(1, dza * hsv * _gelu_grad(y_rnn, th))
        dhs = dza * gl
        xrb = xr_ref[...]
        xr = xrb.astype(F32)
        lam_v = lam_ref[...]
        r, ig, sp, a, s = _gates(xrb, wg_ref, ba_ref[...], bx_ref[...], lam_v, hd)
        b = jnp.where(rows == tm - 1, gcar[1:2, :], pltpu.roll(a, tm - 1, 0))
        big_g = _scan_bwd(b, dhs, gcar[0:1, :])
        gcar[0:1, :] = big_g[0:1, :]
        gcar[1:2, :] = a[0:1, :]
        h_before = jnp.where(tile > 0, hh_ref[halo_rows - 1:halo_rows, :].astype(F32), 0.0)
        h_prev = jnp.where(rows == 0, h_before, pltpu.roll(hsv, 1, 0))
        ds = big_g * ig * xr
        dla = big_g * h_prev * a - ds * (a * a) / jnp.maximum(s, 1e-20)
        acc(SG_LAM, dla * r * (RG_LRU_C * _sigmoid(-lam_v)))
        dpr = dla * (-RG_LRU_C * sp) * r * (1.0 - r)
        dpi = big_g * s * xr * ig * (1.0 - ig)
        acc(SG_BA, dpr)
        acc(SG_BX, dpi)
        dprb = dpr.astype(BF16)
        dpib = dpi.astype(BF16)
        yg_ref[:, 0:d] = dprb
        yg_ref[:, d:2 * d] = dpib
        back = []
        for hh in range(N_HEADS):
            sl = slice(hh * hd, (hh + 1) * hd)
            back.append(_nn(dprb[:, sl], wgt_ref[0, hh]) + _nn(dpib[:, sl], wgt_ref[1, hh]))
        dxr = big_g * s * ig + jnp.concatenate(back, axis=1)
        acc(SG_CB4, dxr)
        extd4[pl.ds(0, tm), :] = dxr
        dx_rnn = jnp.zeros((tm, d), F32)
        for k in range(k4):
            term = extd4[pl.ds(k4 - 1 - k, tm), :]
            dx_rnn = dx_rnn + cw4_ref[k:k + 1, :] * term
            acc(SG_CW4 + k, x_rnn * term)
        extd4[pl.ds(tm, CONV4_HALO), :] = extd4[pl.ds(0, CONV4_HALO), :]
        emit(0, dx_rnn)

        dybb = dyb.astype(BF16)
        y3_ref[:, 2 * d:3 * d] = dybb
        acc(SG_BCP, dyb)
        dv3 = _nn(dybb, w3_v[1])
        v1 = v1_ref[...].astype(F32)
        xc = v1 - jnp.mean(v1, axis=-1, keepdims=True)
        rstd = lax.rsqrt(jnp.mean(xc * xc, axis=-1, keepdims=True) + EPS)
        xhat = xc * rstd
        lng_v = lng_ref[...]
        v2 = xhat * lng_v + lnb_ref[...]
        s2 = _sigmoid(v2)
        x3_ref[:, 2 * d:3 * d] = (v2 * s2).astype(BF16)
        dv2 = dv3 * (s2 * (1.0 + v2 * (1.0 - s2)))
        acc(SG_LNG, dv2 * xhat)
        acc(SG_LNB, dv2)
        dxh = dv2 * lng_v
        dv1 = rstd * (dxh - jnp.mean(dxh, axis=-1, keepdims=True)
                      - xhat * jnp.mean(dxh * xhat, axis=-1, keepdims=True))
        acc(SG_CB31, dv1)
        extd31[pl.ds(0, tm), :] = dv1
        _shifted_copies(extd31, es31, tm + CONV31_HALO - SUBLANES)
        sgg = _sigmoid(glu_g)
        v0 = glu_v * sgg
        dv0 = jnp.zeros((tm, d), F32)
        for k in range(k31):
            term = _tap(extd31, es31, k31 - 1 - k, tm)
            dv0 = dv0 + cw31_ref[k:k + 1, :] * term
            acc(SG_CW31 + k, v0 * term)
        extd31[pl.ds(tm, CONV31_HALO), :] = extd31[pl.ds(0, CONV31_HALO), :]
        emit(2, dv0 * sgg)
        emit(3, dv0 * glu_v * sgg * (1.0 - sgg))

        dn = dn_parts[0]
        x = h_ref[...]
        rr = lax.rsqrt(jnp.mean(x * x, axis=-1, keepdims=True) + EPS)
        dx, dgp = _rms_bwd(dn, x, rr, g_ref[...])
        dh1_ref[...] = dh_ref[...] + dx
        sg_ref[SG_MIX:SG_MIX + 1, :] += dgp

    rev = lambda i: (nt - 1 - i, 0)
    row = pl.BlockSpec((tm, d), rev)
    wide = pl.BlockSpec((tm, n_in), rev)
    full = lambda a: pl.BlockSpec(a.shape, lambda i, nd=a.ndim: (0,) * nd)
    halo = pl.BlockSpec((halo_rows, d), lambda i: (jnp.maximum((nt - 1 - i) * per - 1, 0), 0))
    smalls = [cw4, wg, jnp.swapaxes(wg, 2, 3), ba, bx, lam, cw31, lng, lnb]
    return _call(
        body, "mixer_bwd", (nt,),
        [row, row, full(g), wide, row, row, halo, row, row, row, _any()]
        + [full(a) for a in smalls] + [_any()],
        [row, wide, pl.BlockSpec((tm, 3 * d), rev), pl.BlockSpec((tm, 3 * d), rev),
         pl.BlockSpec((tm, 2 * d), rev), pl.BlockSpec((sg_rows, d), lambda i: (0, 0))],
        [jax.ShapeDtypeStruct((tp, d), F32), jax.ShapeDtypeStruct((tp, n_in), BF16),
         jax.ShapeDtypeStruct((tp, 3 * d), BF16), jax.ShapeDtypeStruct((tp, 3 * d), BF16),
         jax.ShapeDtypeStruct((tp, 2 * d), BF16), jax.ShapeDtypeStruct((sg_rows, d), F32)],
        [pltpu.VMEM(win_t.shape, BF16),
         pltpu.VMEM((3, d, d), BF16),
         pltpu.VMEM((tm + CONV4_HALO, d), F32),
         pltpu.VMEM((tm + CONV31_HALO, d), F32),
         pltpu.VMEM((SUBLANES, tm + CONV31_HALO, d), F32),
         pltpu.VMEM((SUBLANES, d), F32),
         pltpu.SemaphoreType.DMA((1 + 3 * N_DEV,))],
        [dh2, h, g, proj, xr_s, hs_s, hs_s, v1_s, ya_s, yb_s, win_t, *smalls, w3_all], comm)


def _tn_matmul(name, x, y, x_spec, y_spec, n_blocks, kb, nb, tm, tp, out_shape, out_spec, out_view, comm=None,
               after=None):
    nt = tp // tm

    def body(x_ref, y_ref, *refs):
        o_ref, acc = refs[-2:]
        i = pl.program_id(1)

        @pl.when(i == 0)
        def _():
            acc[...] = jnp.zeros_like(acc)

        acc[...] += _tn(x_ref[...], y_ref[...])

        @pl.when(i == nt - 1)
        def _():
            o_ref[...] = acc[...].astype(BF16).reshape(out_view)

    follows = [] if after is None else [after]
    outs, extra = _call(body, name, (n_blocks, nt), [x_spec, y_spec] + [_any()] * len(follows), [out_spec],
                        [jax.ShapeDtypeStruct(out_shape, BF16)], [pltpu.VMEM((kb, nb), F32)], [x, y] + follows,
                        comm)
    return outs[0], extra


def kernel(x, meta_tokens, ffn1_norm, ffn1_w_gu, ffn1_w_down, mix_norm, w_in, b_in, rnn_conv_w, rnn_conv_b, rg_w_a, rg_b_a, rg_w_x, rg_b_x, rg_lambda, rnn_w_proj, conv_dw_w, conv_dw_b, conv_ln_g, conv_ln_b, conv_w_proj, conv_b_proj, w_out, ffn2_norm, ffn2_w_gu, ffn2_w_down, final_norm, loss_target, m_meta_tokens, m_ffn1_norm, m_ffn1_w_gu, m_ffn1_w_down, m_mix_norm, m_w_in, m_b_in, m_rnn_conv_w, m_rnn_conv_b, m_rg_w_a, m_rg_b_a, m_rg_w_x, m_rg_b_x, m_rg_lambda, m_rnn_w_proj, m_conv_dw_w, m_conv_dw_b, m_conv_ln_g, m_conv_ln_b, m_conv_w_proj, m_conv_b_proj, m_w_out, m_ffn2_norm, m_ffn2_w_gu, m_ffn2_w_down, m_final_norm, v_meta_tokens, v_ffn1_norm, v_ffn1_w_gu, v_ffn1_w_down, v_mix_norm, v_w_in, v_b_in, v_rnn_conv_w, v_rnn_conv_b, v_rg_w_a, v_rg_b_a, v_rg_w_x, v_rg_b_x, v_rg_lambda, v_rnn_w_proj, v_conv_dw_w, v_conv_dw_b, v_conv_ln_g, v_conv_ln_b, v_conv_w_proj, v_conv_b_proj, v_w_out, v_ffn2_norm, v_ffn2_w_gu, v_ffn2_w_down, v_final_norm):
    w = dict(locals())
    seq, d = x.shape[1], x.shape[2]
    n_meta = meta_tokens.shape[0]
    t_real = n_meta + seq
    tp, tm, tmx_fwd, tmx, tmt, tmw = _tiles(t_real)
    fb = ffn1_w_gu.shape[-1]
    wr = ffn1_w_down.shape[1]
    f = N_DEV * wr
    fc = f // FFN_CHUNKS
    nbc = w_in.shape[-1]
    n_in = N_DEV * nbc
    pr = rnn_w_proj.shape[1]
    hd = rg_w_a.shape[-1]
    gr = rg_w_a.shape[2]
    cw = meta_tokens.shape[1]
    k4, k31 = rnn_conv_w.shape[1], conv_dw_w.shape[1]
    assert n_in == 6 * d and 2 * wr == fb and N_HEADS * hd == d and pr * N_DEV == d

    xi, yi, ci = lax.axis_index("x"), lax.axis_index("y"), lax.axis_index("c")
    core = ci.astype(jnp.int32).reshape(1)
    chip = (2 * xi + yi).astype(jnp.int32).reshape(1)
    me_index = (4 * xi + 2 * yi + ci).astype(jnp.int32).reshape(1)

    for nm in ("ffn1_w_gu", "ffn2_w_gu"):
        for pre in ("", "m_", "v_"):
            w[pre + nm] = jnp.swapaxes(w[pre + nm], 1, 2)

    wgut1 = w["ffn1_w_gu"][0].astype(BF16)
    wgut2 = w["ffn2_w_gu"][0].astype(BF16)
    wd1 = ffn1_w_down[0].astype(BF16)
    wd2 = ffn2_w_down[0].astype(BF16)
    win_loc = w_in[0].astype(BF16)
    win_t_loc = jnp.swapaxes(w_in[0], 0, 1).astype(BF16)
    w3_loc = jnp.concatenate([rnn_w_proj[0], conv_w_proj[0], w_out[0]], axis=0).astype(BF16)
    wg_loc = jnp.stack([rg_w_a[0], rg_w_x[0]]).astype(BF16)
    n_small = n_meta + k4 + k31
    small_rows = -(-n_small // SUBLANES) * SUBLANES
    small_loc = jnp.concatenate([meta_tokens, rnn_conv_w[0], conv_dw_w[0],
                                 jnp.zeros((small_rows - n_small, cw), F32)], axis=0)
    (wgut1_all, wd1_all, small_all), h0, tgt = _first_gather(
        [wgut1, wd1, small_loc], 2, x[0], loss_target[0], n_meta, tp)
    small_full = small_all.transpose(1, 0, 2).reshape(small_rows, d)
    cw4 = small_full[n_meta:n_meta + k4]
    cw31 = small_full[n_meta + k4:n_meta + k4 + k31]

    wgu1, wdn1 = wgut1_all.reshape(2 * f, d), wd1_all.reshape(f, d)
    (h1, gu1, n1), (win_all, w3_all, wg_all) = _ffn_fwd(
        h0, ffn1_norm, wgu1, wdn1, tm, comm=_Gather([win_loc, w3_loc, wg_loc], pass_on_at=(0.65, 0.95)))
    wg = wg_all.transpose(1, 2, 0, 3, 4).reshape(2, N_HEADS, hd, hd)
    (h2, proj, n2, xr_s, hs_s, v1_s, ya_s, yb_s), (wgut2_all, wd2_all) = _mixer_fwd(
        h1, mix_norm, b_in, win_all, cw4, rnn_conv_b, wg, rg_b_a, rg_b_x, rg_lambda, cw31, conv_dw_b, conv_ln_g,
        conv_ln_b, conv_b_proj, w3_all, tmx_fwd, comm=_Gather([wgut2, wd2], pass_on_at=(0.3, 0.5)))
    wgu2, wdn2 = wgut2_all.reshape(2 * f, d), wd2_all.reshape(f, d)
    (dh3, gu2, n3, tail), (win_t_all,) = _ffn_fwd(
        h2, ffn2_norm, wgu2, wdn2, tm, loss=(tgt, final_norm.reshape(1, d), n_meta, t_real),
        comm=_Gather([win_t_loc], pass_on_at=(0.45, 0.75)))
    win_t = win_t_all.reshape(n_in, d)

    def d_w_gu(tag, dgu, n_s, after=None):
        g, _ = _tn_matmul(
            "d_w_gu" + tag, dgu, n_s,
            pl.BlockSpec((None, tmt, fc), lambda b, i: (b // FFN_CHUNKS, i, b % FFN_CHUNKS)),
            pl.BlockSpec((tmt, d), lambda b, i: (i, 0)),
            2 * FFN_CHUNKS, fc, d, tmt, tp, (2 * FFN_CHUNKS, fc, d),
            pl.BlockSpec((None, fc, d), lambda b, i: (b, 0, 0)), (fc, d), after=after)
        return g.reshape(N_DEV, fb, d)

    def d_w_down(tag, act, df, comm=None):
        g, extra = _tn_matmul(
            "d_w_down" + tag, act, df,
            pl.BlockSpec((tmt, fc), lambda b, i: (i, b)), pl.BlockSpec((tmt, d), lambda b, i: (i, 0)),
            FFN_CHUNKS, fc, d, tmt, tp, (FFN_CHUNKS, fc, d),
            pl.BlockSpec((None, fc, d), lambda b, i: (b, 0, 0)), (fc, d), comm)
        return g.reshape(N_DEV, wr, d), extra

    (dh2, dgu2, act2, df2, tail), _ = _ffn_bwd(dh3, h2, gu2, ffn2_norm, wgu2, wdn2, tm, tail, TAIL_FFN2, n3)
    g_wgu2 = d_w_gu("2", dgu2, n3)
    g_wd2, _ = d_w_down("2", act2, df2)
    (dh1, dproj, x3, y3, yg, sg), (r_wd2, r_wgu2) = _mixer_bwd(
        dh2, h1, mix_norm, proj, xr_s, hs_s, v1_s, ya_s, yb_s, win_t, cw4, wg, rg_b_a, rg_b_x, rg_lambda, cw31,
        conv_ln_g, conv_ln_b, w3_all, tmx, comm=_Scatter([g_wd2, g_wgu2]))
    g_w3, _ = _tn_matmul(
        "d_w_proj3", x3, y3,
        pl.BlockSpec((tmw, d), lambda b, i: (i, b)), pl.BlockSpec((tmw, d), lambda b, i: (i, b)),
        3, d, d, tmw, tp, (N_DEV, 3, pr, d), pl.BlockSpec((N_DEV, None, pr, d), lambda b, i: (0, b, 0, 0)),
        (N_DEV, pr, d))
    g_wg, _ = _tn_matmul(
        "d_w_gates", xr_s, yg,
        pl.BlockSpec((tmw, hd), lambda b, i: (i, b % N_HEADS)), pl.BlockSpec((tmw, hd), lambda b, i: (i, b)),
        2 * N_HEADS, hd, hd, tmw, tp, (N_DEV, 2 * N_HEADS, gr, hd),
        pl.BlockSpec((N_DEV, None, gr, hd), lambda b, i: (0, b, 0, 0)), (N_DEV, gr, hd))
    w3_sems, g_w3_thru, w3_land, w3_token = _exchange_start("grads_proj3_exchange", _scatter_copies, N_DEV - 1, g_w3)
    g_win, (r_wg,) = _tn_matmul(
        "d_w_in", n2, dproj,
        pl.BlockSpec((tmw, d), lambda b, i: (i, 0)), pl.BlockSpec((tmw, nbc), lambda b, i: (i, b)),
        N_DEV, d, nbc, tmw, tp, (N_DEV, d, nbc), pl.BlockSpec((None, d, nbc), lambda b, i: (b, 0, 0)), (d, nbc),
        comm=_Scatter([g_wg]), after=w3_token)
    win_sems, g_win_thru, win_land, win_token = _exchange_start("grads_w_in_exchange", _scatter_copies, N_DEV - 1, g_win)
    (dh0, dgu1, act1, df1, tail), _ = _ffn_bwd(dh1, h0, gu1, ffn1_norm, wgu1, wdn1, tm, tail, TAIL_FFN1, win_token)

    pieces = [sg, dh0[:n_meta], tail]
    assert all(p.shape[0] % SUBLANES == 0 for p in pieces)
    at = [0, sg.shape[0], sg.shape[0] + n_meta]
    loss_row = at[2] + TAIL_LOSS
    rep_rows = [("ffn1_norm", at[2] + TAIL_FFN1, 1), ("mix_norm", SG_MIX, 1), ("b_in", SG_BIN, 6),
                ("rnn_conv_b", SG_CB4, 1),
                ("rg_b_a", SG_BA, 1), ("rg_b_x", SG_BX, 1), ("rg_lambda", SG_LAM, 1), ("conv_dw_b", SG_CB31, 1),
                ("conv_ln_g", SG_LNG, 1), ("conv_ln_b", SG_LNB, 1), ("conv_b_proj", SG_BCP, 1),
                ("ffn2_norm", at[2] + TAIL_FFN2, 1), ("final_norm", at[2] + TAIL_FINAL, 1)]
    col_rows = [("meta_tokens", at[1], n_meta), ("rnn_conv_w", SG_CW4, k4), ("conv_dw_w", SG_CW31, k31)]
    layout = []
    for nm, row0, nr in rep_rows:
        kind = "wide" if nm == "b_in" else "rep"
        as2d = lambda a: a.reshape(1, -1) if a.ndim == 1 else a
        layout.append((kind, row0, nr, as2d(w[nm]), as2d(w["m_" + nm]), as2d(w["v_" + nm])))
    for nm, row0, nr in col_rows:
        sq = lambda a: a.reshape(a.shape[-2], a.shape[-1])
        layout.append(("col", row0, nr, sq(w[nm]), sq(w["m_" + nm]), sq(w["v_" + nm])))
    small_partial = jnp.concatenate(pieces, axis=0)

    g_wd1, (small_partials,) = d_w_down("1", act1, df1, comm=_Bcast(small_partial))
    wd1_sems, g_wd1_thru, wd1_land, wd1_token = _exchange_start(
        "grads_w_down1_exchange", _scatter_copies, N_DEV - 1, g_wd1)
    g_wgu1 = d_w_gu("1", dgu1, n1, after=wd1_token)

    g_last = g_wgu1.reshape((4, 2) + g_wgu1.shape[1:])
    comb_wgu1 = _pair_reduce(g_last, core)
    sems, comb_thru, land_thru, after = _exchange_start("grads_chip_exchange", _chip_copies, 3, comb_wgu1)
    g_win, r_win = _exchange_wait("grads_w_in_exchange", _scatter_copies, win_sems, g_win_thru, win_land, after)
    g_w3, r_w3 = _exchange_wait("grads_proj3_exchange", _scatter_copies, w3_sems, g_w3_thru, w3_land, after)
    g_wd1, r_wd1 = _exchange_wait("grads_w_down1_exchange", _scatter_copies, wd1_sems, g_wd1_thru, wd1_land, after)

    grad_x = (dh0[n_meta:t_real] + after[0, 0])[None]
    total, small_out = _small_adamw(small_partials, layout, me_index, grad_x)
    after = total

    groups = [(g_wd1, r_wd1, me_index, ["ffn1_w_down"]),
              (g_wd2, r_wd2, me_index, ["ffn2_w_down"]), (g_wgu2, r_wgu2, me_index, ["ffn2_w_gu"]),
              (g_win, r_win, me_index, ["w_in"]), (g_w3, r_w3, me_index, ["w_out", "rnn_w_proj", "conv_w_proj"]),
              (g_wg, r_wg, me_index, ["rg_w_a", "rg_w_x"]), (None, None, chip, ["ffn1_w_gu"])]
    res = {}
    for own, recv, idx, group in groups:
        if own is None:
            own, recv = _exchange_wait("grads_chip_exchange", _chip_copies, sems, comb_thru, land_thru, after)
        outs = _final_adamw(own, recv, idx, [(w[nm], w["m_" + nm], w["v_" + nm]) for nm in group], after)
        after = outs[-1][0]
        for nm, o in zip(group, outs):
            res[nm] = o
    for nm in ("ffn1_w_gu", "ffn2_w_gu"):
        res[nm] = tuple(jnp.swapaxes(a, 1, 2) for a in res[nm])
    for (nm, _, _), o in zip(rep_rows + col_rows, small_out):
        res[nm] = tuple(a.reshape(w[nm].shape) for a in o)


    order = ["meta_tokens", "ffn1_norm", "ffn1_w_gu", "ffn1_w_down", "mix_norm", "w_in", "b_in", "rnn_conv_w",
             "rnn_conv_b", "rg_w_a", "rg_b_a", "rg_w_x", "rg_b_x", "rg_lambda", "rnn_w_proj", "conv_dw_w",
             "conv_dw_b", "conv_ln_g", "conv_ln_b", "conv_w_proj", "conv_b_proj", "w_out", "ffn2_norm",
             "ffn2_w_gu", "ffn2_w_down", "final_norm"]
    return (total[loss_row, 0], grad_x, *[res[nm][0] for nm in order], *[res[nm][1] for nm in order],
            *[res[nm][2] for nm in order], *[res[nm][3] for nm in order])
```

```python
import functools
import math

import jax
import jax.numpy as jnp
from jax import lax
from jax.experimental import pallas as pl
from jax.experimental.pallas import tpu as pltpu

F32 = jnp.float32
BF16 = jnp.bfloat16
MESH = pl.DeviceIdType.MESH
N_DEV = 8
N_HEADS = 4
RG_LRU_C = 8.0
EPS = 1e-6
FFN_RES = 0.5
ADAM_LR, ADAM_B1, ADAM_B2, ADAM_EPS, ADAM_WD, ADAM_STEP = 0.001, 0.9, 0.999, 1e-08, 0.01, 10
V7X_VMEM_LIMIT = 56 * 1024 * 1024
CONV4_HALO = 8
CONV31_HALO = 32
SUBLANES = 8
STAGE_ROWS = 512
TAIL_FFN1, TAIL_FINAL, TAIL_LOSS, TAIL_FFN2 = 0, 1, 2, 3
FFN_CHUNKS = 2
FFN_FWD_CHUNKS = 1
GELU_C = math.sqrt(2.0 / math.pi)
GELU_K = 0.044715


def _any():
    return pl.BlockSpec(memory_space=pl.ANY)


def _params(n_grid):
    return pltpu.CompilerParams(dimension_semantics=("arbitrary",) * n_grid, vmem_limit_bytes=V7X_VMEM_LIMIT)


def _nn(a, b):
    return jnp.dot(a, b, preferred_element_type=F32)


def _nt(a, b):
    return lax.dot_general(a, b, (((1,), (1,)), ((), ())), preferred_element_type=F32)


def _tn(a, b):
    return lax.dot_general(a, b, (((0,), (0,)), ((), ())), preferred_element_type=F32)


def _sigmoid(x):
    return 0.5 * jnp.tanh(0.5 * x) + 0.5


def _rowsum(x):
    return jnp.sum(x, axis=0, keepdims=True)


def _rms_fwd(x, g):
    r = lax.rsqrt(jnp.mean(x * x, axis=-1, keepdims=True) + EPS)
    return x * r * g, r


def _rms_bwd(dn, x, r, g):
    xr = x * r
    gy = dn * g
    dx = r * (gy - xr * jnp.mean(gy * xr, axis=-1, keepdims=True))
    return dx, _rowsum(dn * xr)


def _gelu(y):
    t = jnp.tanh(GELU_C * (y + GELU_K * y * y * y))
    return 0.5 * y * (1.0 + t), t


def _gelu_grad(y, t):
    return 0.5 * (1.0 + t) + 0.5 * y * (1.0 - t * t) * GELU_C * (1.0 + 3.0 * GELU_K * y * y)


def _softplus(x):
    return jnp.maximum(x, 0.0) + jnp.log(1.0 + jnp.exp(-jnp.abs(x)))


def _one_minus_exp(z):
    series = -z * (1.0 + 0.5 * z * (1.0 + z * (1.0 / 3.0) * (1.0 + 0.25 * z)))
    return jnp.where(z > -0.05, series, 1.0 - jnp.exp(z))


def _tiles(t_real):
    if t_real > 2048:
        tm = 416
        tp = -(-t_real // tm) * tm
        return tp, tm, tm // 2, tm // 2, tp, tp
    tm = 128
    tp = -(-t_real // tm) * tm
    return tp, tm, tm // 2, tm // 2, tm, tm


def _load_weights(copies, sems):
    cps = [pltpu.make_async_copy(s, d, sems.at[k]) for k, (s, d) in enumerate(copies)]
    for cp in cps:
        cp.start()
    for cp in cps:
        cp.wait()


def _position():
    x, y, c = lax.axis_index("x"), lax.axis_index("y"), lax.axis_index("c")
    chips = [(1 - x, y), (x, 1 - y), (1 - x, 1 - y)]
    return x, y, c, chips


def _slot(p):
    return 4 * p[0] + 2 * p[1] + p[2]


class _Lazy(dict):
    def __getitem__(self, key):
        val = dict.__getitem__(self, key)
        return val() if callable(val) else val


class _Gather:
    def __init__(self, shards, pass_on_at=None):
        self.shards = list(shards)
        self.n = len(self.shards)
        self.pass_on_at = pass_on_at

    def inputs(self):
        return self.shards

    def out_shape(self):
        return [jax.ShapeDtypeStruct((N_DEV,) + s.shape, s.dtype) for s in self.shards]

    N_SEMS = 9

    def scratch(self):
        return [pltpu.SemaphoreType.DMA((self.N_SEMS * self.n,)), pltpu.SemaphoreType.DMA((self.N_SEMS * self.n,)),
                pltpu.SemaphoreType.DMA((self.n,))]

    def _plan(self, ins, outs, sems):
        send_sems, recv_sems, local_sems = sems
        x, y, c, _ = _position()
        me, sib, xn, yn, dg = (x, y, c), (x, y, 1 - c), (1 - x, y, c), (x, 1 - y, c), (1 - x, 1 - y, c)
        other = lambda p: (p[0], p[1], 1 - c)

        def blk(a, p, half=None):
            ref = outs[a].at[_slot(p)]
            if half is None:
                return ref
            rows = self.shards[a].shape[0] // 2
            return ref.at[pl.ds(half * rows, rows)]

        def copy(a, k, dst, to, src=None):
            return pltpu.make_async_remote_copy(
                src_ref=dst if src is None else src, dst_ref=dst,
                send_sem=send_sems.at[self.N_SEMS * a + k], recv_sem=recv_sems.at[self.N_SEMS * a + k],
                device_id=to, device_id_type=MESH)

        cp = _Lazy(mine=lambda: [pltpu.make_async_copy(ins[a], blk(a, me), local_sems.at[a]) for a in range(self.n)])
        for a in range(self.n):
            cp[a] = _Lazy(
                own=lambda a=a: [copy(a, 0, blk(a, me), sib, src=ins[a]), copy(a, 1, blk(a, me), xn, src=ins[a]),
                                 copy(a, 2, blk(a, me), yn, src=ins[a])],
                from_x=lambda a=a: copy(a, 1, blk(a, xn), me), from_y=lambda a=a: copy(a, 2, blk(a, yn), me),
                relay_x=lambda a=a: copy(a, 3, blk(a, xn, 0), yn), relay_y=lambda a=a: copy(a, 4, blk(a, yn, 1), xn),
                diag0=lambda a=a: copy(a, 3, blk(a, dg, 0), me), diag1=lambda a=a: copy(a, 4, blk(a, dg, 1), me),
                pass_x=lambda a=a: copy(a, 5, blk(a, xn), sib), pass_y=lambda a=a: copy(a, 6, blk(a, yn), sib),
                pass_d0=lambda a=a: copy(a, 7, blk(a, dg, 0), sib), pass_d1=lambda a=a: copy(a, 8, blk(a, dg, 1), sib),
                from_sib=lambda a=a: [copy(a, 0, blk(a, sib), me), copy(a, 5, blk(a, other(xn)), me),
                                      copy(a, 6, blk(a, other(yn)), me), copy(a, 7, blk(a, other(dg), 0), me),
                                      copy(a, 8, blk(a, other(dg), 1), me)])
        return cp

    def start(self, ins, outs, sems):
        cp = self._plan(ins, outs, sems)
        for c in cp["mine"]:
            c.start()
        for a in range(self.n):
            for c in cp[a]["own"]:
                c.start()

    def pass_on(self, ins, outs, sems):
        cp = self._plan(ins, outs, sems)
        for a in range(self.n):
            cp[a]["from_x"].wait_recv()
            cp[a]["relay_x"].start()
            cp[a]["pass_x"].start()
        for a in range(self.n):
            cp[a]["from_y"].wait_recv()
            cp[a]["relay_y"].start()
            cp[a]["pass_y"].start()

    def pass_on_relayed(self, ins, outs, sems):
        cp = self._plan(ins, outs, sems)
        for a in range(self.n):
            cp[a]["diag0"].wait_recv()
            cp[a]["pass_d0"].start()
            cp[a]["diag1"].wait_recv()
            cp[a]["pass_d1"].start()

    def finish(self, ins, outs, sems):
        if self.pass_on_at is None:
            self.pass_on(ins, outs, sems)
            self.pass_on_relayed(ins, outs, sems)
        cp = self._plan(ins, outs, sems)
        for a in range(self.n):
            for c in cp[a]["from_sib"]:
                c.wait_recv()
            for c in cp[a]["own"] + [cp[a][k] for k in ("relay_x", "relay_y", "pass_x", "pass_y", "pass_d0", "pass_d1")]:
                c.wait_send()
        for c in cp["mine"]:
            c.wait()


class _Scatter:
    def __init__(self, grads):
        self.grads = list(grads)
        self.n = len(self.grads)

    def inputs(self):
        return self.grads

    def out_shape(self):
        return [jax.ShapeDtypeStruct((N_DEV - 1,) + g.shape[1:], g.dtype) for g in self.grads]

    def scratch(self):
        return [pltpu.SemaphoreType.DMA((7 * self.n,)), pltpu.SemaphoreType.DMA((7 * self.n,))]

    def _plan(self, ins, outs, sems):
        send_sems, recv_sems = sems
        x, y, c, _ = _position()
        cps = []
        for a in range(self.n):
            for k in range(1, N_DEV):
                peer = (x ^ (k >> 2), y ^ ((k >> 1) & 1), c ^ (k & 1))
                cps.append(pltpu.make_async_remote_copy(
                    src_ref=ins[a].at[_slot(peer)], dst_ref=outs[a].at[k - 1],
                    send_sem=send_sems.at[7 * a + k - 1], recv_sem=recv_sems.at[7 * a + k - 1],
                    device_id=peer, device_id_type=MESH))
        return cps

    def start(self, ins, outs, sems):
        for cp in self._plan(ins, outs, sems):
            cp.start()

    def finish(self, ins, outs, sems):
        for cp in self._plan(ins, outs, sems):
            cp.wait()


def _hosted(inner, n_in, n_out, comm, grid):
    if comm is None:
        return inner
    nc_in, nc_out, ns = len(comm.inputs()), len(comm.out_shape()), len(comm.scratch())

    def body(*refs):
        o0 = n_in + nc_in
        s0 = o0 + n_out + nc_out
        main = refs[:n_in] + refs[o0:o0 + n_out] + refs[s0:len(refs) - ns]
        c_in, c_out, c_sems = refs[n_in:o0], refs[o0 + n_out:s0], refs[len(refs) - ns:]
        ids = [pl.program_id(ax) for ax in range(len(grid))]
        first = functools.reduce(jnp.logical_and, [i == 0 for i in ids])
        last = functools.reduce(jnp.logical_and, [i == g - 1 for i, g in zip(ids, grid)])

        @pl.when(first)
        def _():
            comm.start(c_in, c_out, c_sems)

        inner(*main)

        if getattr(comm, "pass_on_at", None) is not None:
            assert len(grid) == 1
            first_at, second_at = (min(grid[0] - 1, int(frac * grid[0])) for frac in comm.pass_on_at)
            assert first_at < second_at

            @pl.when(ids[0] == first_at)
            def _():
                comm.pass_on(c_in, c_out, c_sems)

            @pl.when(ids[0] == second_at)
            def _():
                comm.pass_on_relayed(c_in, c_out, c_sems)

        @pl.when(last)
        def _():
            comm.finish(c_in, c_out, c_sems)

    return body


def _call(inner, name, grid, in_specs, out_specs, out_shape, scratch, args, comm=None):
    n_in, n_out = len(args), len(out_shape)
    body = _hosted(inner, n_in, n_out, comm, grid)
    if comm is not None:
        in_specs = list(in_specs) + [_any()] * len(comm.inputs())
        args = list(args) + comm.inputs()
        out_specs = list(out_specs) + [_any()] * len(comm.out_shape())
        out_shape = list(out_shape) + comm.out_shape()
        scratch = list(scratch) + comm.scratch()
    outs = pl.pallas_call(
        body, name=name, grid=grid, in_specs=list(in_specs), out_specs=list(out_specs), out_shape=list(out_shape),
        scratch_shapes=list(scratch), compiler_params=_params(len(grid)))(*args)
    return list(outs[:n_out]), list(outs[n_out:])


class _Bcast:
    def __init__(self, block):
        self.block = block

    def inputs(self):
        return [self.block]

    def out_shape(self):
        return [jax.ShapeDtypeStruct((N_DEV,) + self.block.shape, self.block.dtype)]

    def scratch(self):
        return [pltpu.SemaphoreType.DMA((N_DEV - 1,)), pltpu.SemaphoreType.DMA((N_DEV - 1,)),
                pltpu.SemaphoreType.DMA((1,))]

    def _plan(self, ins, outs, sems):
        send_sems, recv_sems, local_sem = sems
        x, y, c, _ = _position()
        mine = outs[0].at[_slot((x, y, c))]
        cps = []
        for k in range(1, N_DEV):
            peer = (x ^ (k >> 2), y ^ ((k >> 1) & 1), c ^ (k & 1))
            cps.append(pltpu.make_async_remote_copy(
                src_ref=ins[0], dst_ref=mine, send_sem=send_sems.at[k - 1], recv_sem=recv_sems.at[k - 1],
                device_id=peer, device_id_type=MESH))
        return pltpu.make_async_copy(ins[0], mine, local_sem.at[0]), cps

    def start(self, ins, outs, sems):
        own, cps = self._plan(ins, outs, sems)
        own.start()
        for cp in cps:
            cp.start()

    def finish(self, ins, outs, sems):
        own, cps = self._plan(ins, outs, sems)
        for cp in cps:
            cp.wait()
        own.wait()


def _first_gather(shards, small_idx, x2, t2, n_meta, tp):
    comm = _Gather(shards)
    n = comm.n
    seq, d = x2.shape
    t_real = n_meta + seq
    n_pad = tp - t_real
    cw = d // N_DEV
    rows = STAGE_ROWS if seq % STAGE_ROWS == 0 else seq
    n_chunks = seq // rows

    def body(*refs):
        ins, (x_ref, t_ref) = refs[:n], refs[n:n + 2]
        outs, (h0_ref, tg_ref) = refs[n + 2:2 * n + 2], refs[2 * n + 2:2 * n + 4]
        sems = refs[2 * n + 4:2 * n + 7]
        buf, zeros, in_sems, out_sems, misc_sems = refs[2 * n + 7:]
        comm.start(ins, outs, sems)
        zeros[...] = jnp.zeros_like(zeros)
        fills = [pltpu.make_async_copy(zeros.at[pl.ds(0, n_pad)], h0_ref.at[pl.ds(t_real, n_pad)], misc_sems.at[0]),
                 pltpu.make_async_copy(zeros.at[pl.ds(0, n_pad)], tg_ref.at[pl.ds(t_real, n_pad)], misc_sems.at[1]),
                 pltpu.make_async_copy(zeros.at[pl.ds(0, n_meta)], tg_ref.at[pl.ds(0, n_meta)], misc_sems.at[2])]
        for cp in fills:
            cp.start()
        jobs = [(src, dst, c) for src, dst in ((x_ref, h0_ref), (t_ref, tg_ref)) for c in range(n_chunks)]

        def load(k):
            src, _, c = jobs[k]
            return pltpu.make_async_copy(src.at[pl.ds(c * rows, rows)], buf.at[k % 2], in_sems.at[k % 2])

        def store(k):
            _, dst, c = jobs[k]
            return pltpu.make_async_copy(buf.at[k % 2], dst.at[pl.ds(n_meta + c * rows, rows)], out_sems.at[k % 2])

        load(0).start()
        for k in range(len(jobs)):
            load(k).wait()
            if k + 1 < len(jobs):
                if k >= 1:
                    store(k - 1).wait()
                load(k + 1).start()
            store(k).start()
        for k in range(max(0, len(jobs) - 2), len(jobs)):
            store(k).wait()
        comm.finish(ins, outs, sems)
        meta = [pltpu.make_async_copy(outs[small_idx].at[k, pl.ds(0, n_meta)],
                                      h0_ref.at[pl.ds(0, n_meta), pl.ds(k * cw, cw)], misc_sems.at[3 + k])
                for k in range(N_DEV)]
        for cp in meta:
            cp.start()
        for cp in fills + meta:
            cp.wait()

    staged = [jax.ShapeDtypeStruct((tp, d), F32)] * 2
    outs = pl.pallas_call(
        body, name="weights_all_gather", out_shape=comm.out_shape() + staged,
        in_specs=[_any()] * (n + 2), out_specs=[_any()] * (n + 2),
        scratch_shapes=comm.scratch() + [
            pltpu.VMEM((2, rows, d), F32), pltpu.VMEM((max(n_pad, n_meta), d), F32),
            pltpu.SemaphoreType.DMA((2,)), pltpu.SemaphoreType.DMA((2,)), pltpu.SemaphoreType.DMA((3 + N_DEV,))],
        compiler_params=pltpu.CompilerParams(vmem_limit_bytes=V7X_VMEM_LIMIT),
    )(*shards, x2, t2)
    return outs[:n], outs[n], outs[n + 1]


def _chip_copies(c_ref, land_ref, sems):
    _, _, c, chips = _position()
    return [pltpu.make_async_remote_copy(
        src_ref=c_ref.at[2 * cx + cy], dst_ref=land_ref.at[j], send_sem=sems[j], recv_sem=sems[3 + j],
        device_id=(cx, cy, c), device_id_type=MESH) for j, (cx, cy) in enumerate(chips)]


def _scatter_copies(g_ref, land_ref, sems):
    x, y, c, _ = _position()
    cps = []
    for k in range(1, N_DEV):
        peer = (x ^ (k >> 2), y ^ ((k >> 1) & 1), c ^ (k & 1))
        cps.append(pltpu.make_async_remote_copy(
            src_ref=g_ref.at[_slot(peer)], dst_ref=land_ref.at[k - 1], send_sem=sems[k - 1],
            recv_sem=sems[N_DEV - 1 + k - 1], device_id=peer, device_id_type=MESH))
    return cps


def _exchange_start(name, copies, n_copies, src):
    hbm = pl.BlockSpec(memory_space=pltpu.HBM)
    sem = pl.BlockSpec(memory_space=pltpu.SEMAPHORE)
    n_sems = 2 * n_copies

    def body(s_ref, land_ref, *refs):
        for cp in copies(s_ref, land_ref, refs[:n_sems]):
            cp.start()
        token = refs[n_sems + 2]
        token[...] = jnp.zeros_like(token)

    land = lax.empty((n_copies,) + src.shape[1:], src.dtype)
    outs = pl.pallas_call(
        body, name=name + "_start",
        out_shape=(pltpu.SemaphoreType.DMA(()),) * n_sems
        + (pltpu.HBM(src.shape, src.dtype), pltpu.HBM(land.shape, land.dtype),
           jax.ShapeDtypeStruct((SUBLANES, 128), F32)),
        in_specs=(hbm, hbm), out_specs=(sem,) * n_sems + (hbm, hbm, pl.BlockSpec(memory_space=pltpu.VMEM)),
        input_output_aliases={0: n_sems, 1: n_sems + 1},
        compiler_params=pltpu.CompilerParams(has_side_effects=pltpu.SideEffectType.DATAFLOW_SIDE_EFFECTING),
    )(pltpu.with_memory_space_constraint(src, pltpu.HBM), pltpu.with_memory_space_constraint(land, pltpu.HBM))
    return outs[:n_sems], outs[n_sems], outs[n_sems + 1], outs[n_sems + 2]


def _exchange_wait(name, copies, sems, src_thru, land_thru, after):
    hbm = pl.BlockSpec(memory_space=pltpu.HBM)
    sem = pl.BlockSpec(memory_space=pltpu.SEMAPHORE)
    n_sems = len(sems)

    def body(s_ref, land_ref, *refs):
        for cp in copies(s_ref, land_ref, refs[:n_sems]):
            cp.wait_send()
            cp.wait_recv()

    return pl.pallas_call(
        body, name=name + "_wait",
        out_shape=(pltpu.HBM(src_thru.shape, src_thru.dtype), pltpu.HBM(land_thru.shape, land_thru.dtype)),
        in_specs=(hbm, hbm) + (sem,) * n_sems + (pl.BlockSpec(memory_space=pl.ANY),), out_specs=(hbm, hbm),
        input_output_aliases={0: 0, 1: 1},
        compiler_params=pltpu.CompilerParams(has_side_effects=pltpu.SideEffectType.DATAFLOW_SIDE_EFFECTING),
    )(src_thru, land_thru, *sems, after)


def _pair_reduce(grad, core):
    blk = grad.shape[2:]
    zeros = (0,) * len(blk)

    def body(core_ref, g_hbm, own_ref, o_ref, landed, send_sems, recv_sems):
        del core_ref
        i = pl.program_id(0)
        x, y, c, _ = _position()

        def copy(k):
            return pltpu.make_async_remote_copy(
                src_ref=g_hbm.at[k, 1 - c], dst_ref=landed.at[k], send_sem=send_sems.at[k],
                recv_sem=recv_sems.at[k], device_id=(x, y, 1 - c), device_id_type=MESH)

        @pl.when(i == 0)
        def _():
            for k in range(4):
                copy(k).start()

        for k in range(4):
            @pl.when(i == k)
            def _(k=k):
                copy(k).wait_recv()

        o_ref[...] = (own_ref[...].astype(F32) + landed[i].astype(F32)).astype(BF16)

        @pl.when(i == 3)
        def _():
            for k in range(4):
                copy(k).wait_send()

    return pl.pallas_call(
        body, name="grads_pair_reduce",
        out_shape=jax.ShapeDtypeStruct((4,) + blk, BF16),
        grid_spec=pltpu.PrefetchScalarGridSpec(
            num_scalar_prefetch=1, grid=(4,),
            in_specs=[_any(), pl.BlockSpec((None, None) + blk, lambda i, cr: (i, cr[0]) + zeros)],
            out_specs=pl.BlockSpec((None,) + blk, lambda i, cr: (i,) + zeros),
            scratch_shapes=[pltpu.VMEM((4,) + blk, BF16), pltpu.SemaphoreType.DMA((4,)),
                            pltpu.SemaphoreType.DMA((4,))]),
        compiler_params=_params(1),
    )(core, grad, grad)


def _adamw(w, g, m, v):
    m2 = ADAM_B1 * m + (1.0 - ADAM_B1) * g
    v2 = ADAM_B2 * v + (1.0 - ADAM_B2) * (g * g)
    m_hat = m2 / (1.0 - ADAM_B1 ** ADAM_STEP)
    v_hat = v2 / (1.0 - ADAM_B2 ** ADAM_STEP)
    delta = -ADAM_LR * (m_hat / (jnp.sqrt(v_hat) + ADAM_EPS) + ADAM_WD * w)
    return delta, m2, v2


def _final_adamw(own, recv, idx, parts, after):
    blk = own.shape[1:]
    n_recv = recv.shape[0]
    n_parts = len(parts)
    per = blk[0] // n_parts if n_parts > 1 else None
    rows = blk[-2]
    n_chunks = 1 if n_parts > 1 else (4 if rows % 64 == 0 and rows >= 512 else (2 if rows % 32 == 0 else 1))
    cblk = blk[:-2] + (rows // n_chunks, blk[-1])
    lead = (0,) * (len(blk) - 2)

    def body(idx_ref, c_ref, r_ref, after_ref, *refs):
        del idx_ref, after_ref
        ins, outs = refs[:3 * n_parts], refs[3 * n_parts:]
        g = c_ref[...].astype(F32)
        for k in range(n_recv):
            g = g + r_ref[k].astype(F32)
        for p in range(n_parts):
            w_ref, m_ref, v_ref = ins[3 * p:3 * p + 3]
            if n_parts == 1:
                gp = g
            elif per == 1:
                gp = g[p]
            else:
                gp = g[p * per:(p + 1) * per]
            delta, m2, v2 = _adamw(w_ref[0], gp, m_ref[0], v_ref[0])
            o = outs[4 * p:4 * p + 4]
            o[0][0] = gp
            o[1][0] = delta
            o[2][0] = m2
            o[3][0] = v2

    flat = [a for wmv in parts for a in wmv]

    def part_spec(a):
        shape = a.shape[:-2] + (a.shape[-2] // n_chunks, a.shape[-1])
        return pl.BlockSpec(shape, lambda i, cr, nd=a.ndim: (0,) * (nd - 2) + (i, 0))

    outs = pl.pallas_call(
        body, name="grads_sum_adamw",
        out_shape=[jax.ShapeDtypeStruct(wmv[0].shape, F32) for wmv in parts for _ in range(4)],
        grid_spec=pltpu.PrefetchScalarGridSpec(
            num_scalar_prefetch=1, grid=(n_chunks,),
            in_specs=[pl.BlockSpec((None,) + cblk, lambda i, cr: (cr[0],) + lead + (i, 0)),
                      pl.BlockSpec((n_recv,) + cblk, lambda i, cr: (0,) + lead + (i, 0))]
                     + [_any()] + [part_spec(a) for a in flat],
            out_specs=[part_spec(wmv[0]) for wmv in parts for _ in range(4)]),
        compiler_params=_params(1),
    )(idx, own, recv, after, *flat)
    return [tuple(outs[4 * p:4 * p + 4]) for p in range(n_parts)]


def _small_adamw(partials, layout, me_index, after):
    _, rows, d = partials.shape
    n = len(layout)
    cw = d // N_DEV

    def body(me_ref, p_ref, after_ref, *refs):
        ins, t_ref, outs = refs[:3 * n], refs[3 * n], refs[3 * n + 1:]
        me = me_ref[0]
        total = p_ref[0]
        for j in range(1, N_DEV):
            total = total + p_ref[j]
        t_ref[...] = total
        for e, (kind, r0, nr, _, _, _) in enumerate(layout):
            w_ref, m_ref, v_ref = ins[3 * e:3 * e + 3]
            o = outs[4 * e:4 * e + 4]
            if kind == "rep":
                g = t_ref[r0:r0 + nr, :]
                delta, m2, v2 = _adamw(w_ref[...], g, m_ref[...], v_ref[...])
                for ref, val in zip(o, (g, delta, m2, v2)):
                    ref[...] = val
            elif kind == "wide":
                for q in range(nr):
                    sl = slice(q * d, (q + 1) * d)
                    g = t_ref[r0 + q:r0 + q + 1, :]
                    delta, m2, v2 = _adamw(w_ref[:, sl], g, m_ref[:, sl], v_ref[:, sl])
                    for ref, val in zip(o, (g, delta, m2, v2)):
                        ref[:, sl] = val
            else:
                for j in range(N_DEV):
                    @pl.when(me == j)
                    def _(j=j, o=o, w_ref=w_ref, m_ref=m_ref, v_ref=v_ref, r0=r0, nr=nr):
                        g = t_ref[r0:r0 + nr, j * cw:(j + 1) * cw]
                        delta, m2, v2 = _adamw(w_ref[...], g, m_ref[...], v_ref[...])
                        for ref, val in zip(o, (g, delta, m2, v2)):
                            ref[...] = val

    flat = [a for ent in layout for a in ent[3:]]
    vm = pl.BlockSpec(memory_space=pltpu.VMEM)
    outs = pl.pallas_call(
        body, name="small_adamw",
        out_shape=[jax.ShapeDtypeStruct((rows, d), F32)]
                  + [jax.ShapeDtypeStruct(ent[3].shape, F32) for ent in layout for _ in range(4)],
        in_specs=[pl.BlockSpec(memory_space=pltpu.SMEM), vm, _any()] + [vm] * len(flat),
        out_specs=[vm] * (1 + 4 * n),
        compiler_params=pltpu.CompilerParams(vmem_limit_bytes=V7X_VMEM_LIMIT),
    )(me_index, partials, after, *flat)
    return outs[0], [tuple(outs[1 + 4 * e:5 + 4 * e]) for e in range(n)]


def _ffn_fwd(h, g, wgu, wd, tm, loss=None, comm=None):
    tp, d = h.shape
    f = wd.shape[0]
    fc = f // FFN_FWD_CHUNKS
    nt = tp // tm
    with_loss = loss is not None
    if with_loss:
        tgt, gf, n_meta, t_real = loss

    def body(*refs):
        if with_loss:
            (h_ref, g_ref, wgu_hbm, wd_hbm, tgt_ref, gf_ref, out_ref, gu_ref, n_ref, tail_ref,
             wgu_v, wd_v, sems) = refs
        else:
            h_ref, g_ref, wgu_hbm, wd_hbm, out_ref, gu_ref, n_ref, wgu_v, wd_v, sems = refs
        i = pl.program_id(0)

        @pl.when(i == 0)
        def _():
            _load_weights([(wgu_hbm, wgu_v), (wd_hbm, wd_v)], sems)
            if with_loss:
                tail_ref[...] = jnp.zeros_like(tail_ref)

        x = h_ref[...]
        n, _ = _rms_fwd(x, g_ref[...])
        nb = n.astype(BF16)
        n_ref[...] = nb
        acc = jnp.zeros((tm, d), F32)
        for j in range(FFN_FWD_CHUNKS):
            cols = slice(j * fc, (j + 1) * fc)
            gate = _nt(nb, wgu_v[pl.ds(j * fc, fc), :])
            up = _nt(nb, wgu_v[pl.ds(f + j * fc, fc), :])
            gu_ref[0, :, cols] = gate.astype(BF16)
            gu_ref[1, :, cols] = up.astype(BF16)
            act = (gate * _sigmoid(gate) * up).astype(BF16)
            acc = acc + _nn(act, wd_v[pl.ds(j * fc, fc), :])
        hn = x + FFN_RES * acc
        if not with_loss:
            out_ref[...] = hn
        else:
            gfv = gf_ref[...]
            r = lax.rsqrt(jnp.mean(hn * hn, axis=-1, keepdims=True) + EPS)
            xr = hn * r
            rows = i * tm + lax.broadcasted_iota(jnp.int32, (tm, 1), 0)
            mask = jnp.logical_and(rows >= n_meta, rows < t_real)
            diff = jnp.where(mask, xr * gfv - tgt_ref[...], 0.0)
            tail_ref[TAIL_LOSS:TAIL_LOSS + 1, :] += jnp.zeros((1, d), F32) + 0.5 * jnp.sum(diff * diff) / d
            dy = diff / d
            gy = dy * gfv
            out_ref[...] = r * (gy - xr * jnp.mean(gy * xr, axis=-1, keepdims=True))
            tail_ref[TAIL_FINAL:TAIL_FINAL + 1, :] += _rowsum(dy * xr)

    row = pl.BlockSpec((tm, d), lambda i: (i, 0))
    vec = pl.BlockSpec((1, d), lambda i: (0, 0))
    in_specs = [row, vec, _any(), _any()]
    out_shape = [jax.ShapeDtypeStruct((tp, d), F32), jax.ShapeDtypeStruct((2, tp, f), BF16),
                 jax.ShapeDtypeStruct((tp, d), BF16)]
    out_specs = [row, pl.BlockSpec((2, tm, f), lambda i: (0, i, 0)), row]
    args = [h, g, wgu, wd]
    if with_loss:
        in_specs += [row, vec]
        out_shape += [jax.ShapeDtypeStruct((SUBLANES, d), F32)]
        out_specs += [pl.BlockSpec((SUBLANES, d), lambda i: (0, 0))]
        args += [tgt, gf]
    return _call(body, "ffn_fwd_loss" if with_loss else "ffn_fwd", (nt,), in_specs, out_specs, out_shape,
                 [pltpu.VMEM((2 * f, d), BF16), pltpu.VMEM((f, d), BF16), pltpu.SemaphoreType.DMA((2,))],
                 args, comm)


def _ffn_bwd(dh, h, gu, g, wgu, wd, tm, tail, tail_row, after):
    tp, d = h.shape
    f = wd.shape[0]
    fc = f // FFN_CHUNKS
    nt = tp // tm

    def body(dh_ref, h_ref, gu_ref, g_ref, tail_ref, wgu_hbm, wd_hbm, after_ref,
             dhin_ref, dgu_ref, act_ref, df_ref, dg_ref, wgu_v, wd_v, dn_v, sems):
        del after_ref
        i, j = pl.program_id(0), pl.program_id(1)

        @pl.when(jnp.logical_and(i == 0, j == 0))
        def _():
            _load_weights([(wgu_hbm, wgu_v), (wd_hbm, wd_v)], sems)
            dg_ref[...] = tail_ref[...]

        dfb = (FFN_RES * dh_ref[...]).astype(BF16)

        @pl.when(j == 0)
        def _():
            df_ref[...] = dfb
            dn_v[...] = jnp.zeros_like(dn_v)

        lo = pl.multiple_of(j * fc, 16)
        dact = _nt(dfb, wd_v[pl.ds(lo, fc), :])
        gate = gu_ref[0].astype(F32)
        up = gu_ref[1].astype(F32)
        sg = _sigmoid(gate)
        silu = gate * sg
        act_ref[...] = (silu * up).astype(BF16)
        dgate = (dact * up * (sg * (1.0 + gate * (1.0 - sg)))).astype(BF16)
        dup = (dact * silu).astype(BF16)
        dgu_ref[0] = dgate
        dgu_ref[1] = dup
        dn_v[...] += _nn(dgate, wgu_v[pl.ds(lo, fc), :]) + _nn(dup, wgu_v[pl.ds(pl.multiple_of(f + j * fc, 16), fc), :])

        @pl.when(j == FFN_CHUNKS - 1)
        def _():
            x = h_ref[...]
            r = lax.rsqrt(jnp.mean(x * x, axis=-1, keepdims=True) + EPS)
            dx, dgp = _rms_bwd(dn_v[...], x, r, g_ref[...])
            dhin_ref[...] = dh_ref[...] + dx
            dg_ref[tail_row:tail_row + 1, :] += dgp

    row = pl.BlockSpec((tm, d), lambda i, j: (i, 0))
    vec = pl.BlockSpec((1, d), lambda i, j: (0, 0))
    tile = pl.BlockSpec((SUBLANES, d), lambda i, j: (0, 0))
    hid2 = pl.BlockSpec((2, tm, fc), lambda i, j: (0, i, j))
    return _call(
        body, "ffn_bwd", (nt, FFN_CHUNKS),
        [row, row, hid2, vec, tile, _any(), _any(), _any()],
        [row, hid2, pl.BlockSpec((tm, fc), lambda i, j: (i, j)), row, tile],
        [jax.ShapeDtypeStruct((tp, d), F32), jax.ShapeDtypeStruct((2, tp, f), BF16),
         jax.ShapeDtypeStruct((tp, f), BF16), jax.ShapeDtypeStruct((tp, d), BF16),
         jax.ShapeDtypeStruct((SUBLANES, d), F32)],
        [pltpu.VMEM((2 * f, d), BF16), pltpu.VMEM((f, d), BF16), pltpu.VMEM((tm, d), F32),
         pltpu.SemaphoreType.DMA((2,))],
        [dh, h, gu, g, tail, wgu, wd, after])


def _piece_segments(q, d, nb_cols):
    segs = []
    for j in range(N_DEV):
        lo, hi = max(q * d, j * nb_cols), min((q + 1) * d, (j + 1) * nb_cols)
        if lo < hi:
            segs.append((j, lo - q * d, hi - q * d, lo - j * nb_cols, hi - j * nb_cols))
    return segs


def _w3_copies(w3_hbm, rows, w3_v):
    return [(w3_hbm.at[k, pl.ds(q * rows, rows)], w3_v.at[q, pl.ds(k * rows, rows)])
            for q in range(3) for k in range(N_DEV)]


def _gates(xrb, wg_ref, ba, bx, lam, hd):
    pre_r, pre_i = [], []
    for hh in range(N_HEADS):
        xh = xrb[:, hh * hd:(hh + 1) * hd]
        pre_r.append(_nn(xh, wg_ref[0, hh]))
        pre_i.append(_nn(xh, wg_ref[1, hh]))
    r = _sigmoid(jnp.concatenate(pre_r, axis=1) + ba)
    ig = _sigmoid(jnp.concatenate(pre_i, axis=1) + bx)
    sp = _softplus(-lam)
    log_a = -RG_LRU_C * r * sp
    a = jnp.exp(log_a)
    s = jnp.sqrt(_one_minus_exp(2.0 * log_a))
    return r, ig, sp, a, s


def _scan_fwd(a, u, h_prev):
    tm = a.shape[0]
    rows = lax.broadcasted_iota(jnp.int32, a.shape, 0)
    d = 1
    while d < tm:
        if d < SUBLANES:
            keep = rows >= d
            u = jnp.where(keep, a * pltpu.roll(u, d, 0) + u, u)
            a = jnp.where(keep, a * pltpu.roll(a, d, 0), a)
        else:
            u = jnp.concatenate([u[:d], a[d:] * u[:tm - d] + u[d:]], axis=0)
            a = jnp.concatenate([a[:d], a[d:] * a[:tm - d]], axis=0)
        d *= 2
    return u + a * h_prev


def _scan_bwd(b, v, g_next):
    tm = b.shape[0]
    rows = lax.broadcasted_iota(jnp.int32, b.shape, 0)
    d = 1
    while d < tm:
        if d < SUBLANES:
            keep = rows < tm - d
            v = jnp.where(keep, v + b * pltpu.roll(v, tm - d, 0), v)
            b = jnp.where(keep, b * pltpu.roll(b, tm - d, 0), b)
        else:
            v = jnp.concatenate([v[:tm - d] + b[:tm - d] * v[d:], v[tm - d:]], axis=0)
            b = jnp.concatenate([b[:tm - d] * b[d:], b[tm - d:]], axis=0)
        d *= 2
    return v + b * g_next


def _shifted_copies(ext_ref, es_ref, n_rows):
    for s in range(1, SUBLANES):
        es_ref[s, pl.ds(0, n_rows), :] = ext_ref[pl.ds(s, n_rows), :]


def _tap(ext_ref, es_ref, off, tm):
    q, s = divmod(off, SUBLANES)
    if s == 0:
        return ext_ref[pl.ds(SUBLANES * q, tm), :]
    return es_ref[s, pl.ds(SUBLANES * q, tm), :]


def _mixer_fwd(h, g, b_in, win_all, cw4, cb4, wg, ba, bx, lam, cw31, cb31, lng, lnb, bcp, w3_all, tm, comm=None):
    tp, d = h.shape
    nb_cols = win_all.shape[-1]
    n_in = N_DEV * nb_cols
    hd = wg.shape[-1]
    k4, k31 = cw4.shape[0], cw31.shape[0]
    w3_rows = d // N_DEV

    def body(h_ref, g_ref, b_ref, win_hbm, cw4_ref, cb4_ref, wg_ref, ba_ref, bx_ref, lam_ref, cw31_ref, cb31_ref,
             lng_ref, lnb_ref, bcp_ref, w3_hbm,
             h2_ref, p_ref, n_ref, xr_ref, hs_ref, v1_ref, ya_ref, yb_ref,
             win_v, w3_v, ext4, ext31, es31, hcar, sems):
        @pl.when(pl.program_id(0) == 0)
        def _():
            _load_weights([(win_hbm, win_v)] + _w3_copies(w3_hbm, w3_rows, w3_v), sems)
            ext4[pl.ds(0, CONV4_HALO), :] = jnp.zeros((CONV4_HALO, d), F32)
            ext31[pl.ds(0, CONV31_HALO), :] = jnp.zeros((CONV31_HALO, d), F32)
            hcar[...] = jnp.zeros_like(hcar)

        n, _ = _rms_fwd(h_ref[...], g_ref[...])
        nb = n.astype(BF16)
        n_ref[...] = nb

        def piece(q):
            parts = [_nn(nb, win_v[j, :, bl:bh]) for j, _, _, bl, bh in _piece_segments(q, d, nb_cols)]
            pq = (jnp.concatenate(parts, axis=1) + b_ref[:, q * d:(q + 1) * d]).astype(BF16)
            p_ref[:, q * d:(q + 1) * d] = pq
            return pq.astype(F32)

        x_rnn, y_rnn, glu_v, glu_g, gate_a, gate_b = [piece(q) for q in range(6)]

        ext4[pl.ds(CONV4_HALO, tm), :] = x_rnn
        xr = cb4_ref[...] + jnp.zeros((tm, d), F32)
        for k in range(k4):
            xr = xr + cw4_ref[k:k + 1, :] * ext4[pl.ds(CONV4_HALO - (k4 - 1) + k, tm), :]
        ext4[pl.ds(0, CONV4_HALO), :] = ext4[pl.ds(tm, CONV4_HALO), :]
        xrb = xr.astype(BF16)
        xr_ref[...] = xrb
        xr = xrb.astype(F32)
        _, ig, _, a, s = _gates(xrb, wg_ref, ba_ref[...], bx_ref[...], lam_ref[...], hd)
        hseq = _scan_fwd(a, s * (ig * xr), hcar[0:1, :])
        hcar[0:1, :] = hseq[tm - 1:tm, :]
        hs_ref[...] = hseq.astype(BF16)
        gl, _ = _gelu(y_rnn)
        ya = _nn((hseq * gl).astype(BF16), w3_v[0])
        ya_ref[...] = ya.astype(BF16)

        ext31[pl.ds(CONV31_HALO, tm), :] = glu_v * _sigmoid(glu_g)
        _shifted_copies(ext31, es31, tm + CONV31_HALO - SUBLANES)
        v1 = cb31_ref[...] + jnp.zeros((tm, d), F32)
        for k in range(k31):
            v1 = v1 + cw31_ref[k:k + 1, :] * _tap(ext31, es31, CONV31_HALO - (k31 - 1) + k, tm)
        ext31[pl.ds(0, CONV31_HALO), :] = ext31[pl.ds(tm, CONV31_HALO), :]
        v1b = v1.astype(BF16)
        v1_ref[...] = v1b
        v1 = v1b.astype(F32)
        xc = v1 - jnp.mean(v1, axis=-1, keepdims=True)
        rstd = lax.rsqrt(jnp.mean(xc * xc, axis=-1, keepdims=True) + EPS)
        v2 = xc * rstd * lng_ref[...] + lnb_ref[...]
        yb = _nn((v2 * _sigmoid(v2)).astype(BF16), w3_v[1]) + bcp_ref[...]
        yb_ref[...] = yb.astype(BF16)

        merged = _sigmoid(gate_a) * ya + _sigmoid(gate_b) * yb
        h2_ref[...] = h_ref[...] + _nn(merged.astype(BF16), w3_v[2])

    row = pl.BlockSpec((tm, d), lambda i: (i, 0))
    wide = pl.BlockSpec((tm, n_in), lambda i: (i, 0))
    full = lambda a: pl.BlockSpec(a.shape, lambda i, nd=a.ndim: (0,) * nd)
    smalls = [cw4, cb4, wg, ba, bx, lam, cw31, cb31, lng, lnb, bcp]
    return _call(
        body, "mixer_fwd", (tp // tm,),
        [row, full(g), full(b_in), _any()] + [full(a) for a in smalls] + [_any()],
        [row, wide] + [row] * 6,
        [jax.ShapeDtypeStruct((tp, d), F32), jax.ShapeDtypeStruct((tp, n_in), BF16)]
        + [jax.ShapeDtypeStruct((tp, d), BF16)] * 6,
        [pltpu.VMEM(win_all.shape, BF16),
         pltpu.VMEM((3, d, d), BF16),
         pltpu.VMEM((tm + CONV4_HALO, d), F32),
         pltpu.VMEM((tm + CONV31_HALO, d), F32),
         pltpu.VMEM((SUBLANES, tm + CONV31_HALO, d), F32),
         pltpu.VMEM((SUBLANES, d), F32),
         pltpu.SemaphoreType.DMA((1 + 3 * N_DEV,))],
        [h, g, b_in, win_all, *smalls, w3_all], comm)


SG_BIN, SG_CW4, SG_CB4, SG_BA, SG_BX, SG_LAM, SG_CB31, SG_LNG, SG_LNB, SG_BCP, SG_MIX, SG_CW31 = 0, 6, 10, 11, 12, 13, 14, 15, 16, 17, 18, 19


def _mixer_bwd(dh2, h, g, proj, xr_s, hs_s, v1_s, ya_s, yb_s, win_t, cw4, wg, ba, bx, lam, cw31, lng, lnb, w3_all, tm,
               comm=None):
    tp, d = dh2.shape
    n_in = proj.shape[1]
    hd = wg.shape[-1]
    k4, k31 = cw4.shape[0], cw31.shape[0]
    nt = tp // tm
    w3_rows = d // N_DEV
    sg_rows = -(-(SG_CW31 + k31) // SUBLANES) * SUBLANES
    halo_rows = 16
    per = tm // halo_rows

    def body(dh_ref, h_ref, g_ref, p_ref, xr_ref, hs_ref, hh_ref, v1_ref, ya_ref, yb_ref, win_hbm,
             cw4_ref, wg_ref, wgt_ref, ba_ref, bx_ref, lam_ref, cw31_ref, lng_ref, lnb_ref, w3_hbm,
             dh1_ref, dp_ref, x3_ref, y3_ref, yg_ref, sg_ref,
             win_v, w3_v, extd4, extd31, es31, gcar, sems):
        i = pl.program_id(0)
        tile = nt - 1 - i

        @pl.when(i == 0)
        def _():
            _load_weights([(win_hbm, win_v)] + _w3_copies(w3_hbm, w3_rows, w3_v), sems)
            for q in range(3):
                w3_v[q] = w3_v[q].T
            extd4[pl.ds(tm, CONV4_HALO), :] = jnp.zeros((CONV4_HALO, d), F32)
            extd31[pl.ds(tm, CONV31_HALO), :] = jnp.zeros((CONV31_HALO, d), F32)
            gcar[...] = jnp.zeros_like(gcar)
            sg_ref[...] = jnp.zeros_like(sg_ref)

        def acc(row, val):
            sg_ref[row:row + 1, :] += _rowsum(val)

        rows = lax.broadcasted_iota(jnp.int32, (tm, d), 0)
        x_rnn = p_ref[:, 0:d].astype(F32)
        y_rnn = p_ref[:, d:2 * d].astype(F32)
        glu_v = p_ref[:, 2 * d:3 * d].astype(F32)
        glu_g = p_ref[:, 3 * d:4 * d].astype(F32)
        sga = _sigmoid(p_ref[:, 4 * d:5 * d].astype(F32))
        sgb = _sigmoid(p_ref[:, 5 * d:6 * d].astype(F32))
        ya = ya_ref[...].astype(F32)
        yb = yb_ref[...].astype(F32)

        dmob = dh_ref[...].astype(BF16)
        dmerged = _nn(dmob, w3_v[2])
        x3_ref[:, 0:d] = (sga * ya + sgb * yb).astype(BF16)
        y3_ref[:, 0:d] = dmob
        dya = sga * dmerged
        dyb = sgb * dmerged
        dn_parts = []

        def emit(q, val):
            vb = val.astype(BF16)
            dp_ref[:, q * d:(q + 1) * d] = vb
            acc(SG_BIN + q, val)
            term = _nn(vb, win_v[pl.ds(q * d, d), :])
            dn_parts[:] = [term if not dn_parts else dn_parts[0] + term]

        emit(4, dmerged * ya * sga * (1.0 - sga))
        emit(5, dmerged * yb * sgb * (1.0 - sgb))

        dyab = dya.astype(BF16)
        y3_ref[:, d:2 * d] = dyab
        dza = _nn(dyab, w3_v[0])
        hsv = hs_ref[...].astype(F32)
        gl, th = _gelu(y_rnn)
        x3_ref[:, d:2 * d] = (hsv * gl).astype(BF16)
        emit(1, dza * hsv * _gelu_grad(y_rnn, th))
        dhs = dza * gl
        xrb = xr_ref[...]
        xr = xrb.astype(F32)
        lam_v = lam_ref[...]
        r, ig, sp, a, s = _gates(xrb, wg_ref, ba_ref[...], bx_ref[...], lam_v, hd)
        b = jnp.where(rows == tm - 1, gcar[1:2, :], pltpu.roll(a, tm - 1, 0))
        big_g = _scan_bwd(b, dhs, gcar[0:1, :])
        gcar[0:1, :] = big_g[0:1, :]
        gcar[1:2, :] = a[0:1, :]
        h_before = jnp.where(tile > 0, hh_ref[halo_rows - 1:halo_rows, :].astype(F32), 0.0)
        h_prev = jnp.where(rows == 0, h_before, pltpu.roll(hsv, 1, 0))
        ds = big_g * ig * xr
        dla = big_g * h_prev * a - ds * (a * a) / jnp.maximum(s, 1e-20)
        acc(SG_LAM, dla * r * (RG_LRU_C * _sigmoid(-lam_v)))
        dpr = dla * (-RG_LRU_C * sp) * r * (1.0 - r)
        dpi = big_g * s * xr * ig * (1.0 - ig)
        acc(SG_BA, dpr)
        acc(SG_BX, dpi)
        dprb = dpr.astype(BF16)
        dpib = dpi.astype(BF16)
        yg_ref[:, 0:d] = dprb
        yg_ref[:, d:2 * d] = dpib
        back = []
        for hh in range(N_HEADS):
            sl = slice(hh * hd, (hh + 1) * hd)
            back.append(_nn(dprb[:, sl], wgt_ref[0, hh]) + _nn(dpib[:, sl], wgt_ref[1, hh]))
        dxr = big_g * s * ig + jnp.concatenate(back, axis=1)
        acc(SG_CB4, dxr)
        extd4[pl.ds(0, tm), :] = dxr
        dx_rnn = jnp.zeros((tm, d), F32)
        for k in range(k4):
            term = extd4[pl.ds(k4 - 1 - k, tm), :]
            dx_rnn = dx_rnn + cw4_ref[k:k + 1, :] * term
            acc(SG_CW4 + k, x_rnn * term)
        extd4[pl.ds(tm, CONV4_HALO), :] = extd4[pl.ds(0, CONV4_HALO), :]
        emit(0, dx_rnn)

        dybb = dyb.astype(BF16)
        y3_ref[:, 2 * d:3 * d] = dybb
        acc(SG_BCP, dyb)
        dv3 = _nn(dybb, w3_v[1])
        v1 = v1_ref[...].astype(F32)
        xc = v1 - jnp.mean(v1, axis=-1, keepdims=True)
        rstd = lax.rsqrt(jnp.mean(xc * xc, axis=-1, keepdims=True) + EPS)
        xhat = xc * rstd
        lng_v = lng_ref[...]
        v2 = xhat * lng_v + lnb_ref[...]
        s2 = _sigmoid(v2)
        x3_ref[:, 2 * d:3 * d] = (v2 * s2).astype(BF16)
        dv2 = dv3 * (s2 * (1.0 + v2 * (1.0 - s2)))
        acc(SG_LNG, dv2 * xhat)
        acc(SG_LNB, dv2)
        dxh = dv2 * lng_v
        dv1 = rstd * (dxh - jnp.mean(dxh, axis=-1, keepdims=True)
                      - xhat * jnp.mean(dxh * xhat, axis=-1, keepdims=True))
        acc(SG_CB31, dv1)
        extd31[pl.ds(0, tm), :] = dv1
        _shifted_copies(extd31, es31, tm + CONV31_HALO - SUBLANES)
        sgg = _sigmoid(glu_g)
        v0 = glu_v * sgg
        dv0 = jnp.zeros((tm, d), F32)
        for k in range(k31):
            term = _tap(extd31, es31, k31 - 1 - k, tm)
            dv0 = dv0 + cw31_ref[k:k + 1, :] * term
            acc(SG_CW31 + k, v0 * term)
        extd31[pl.ds(tm, CONV31_HALO), :] = extd31[pl.ds(0, CONV31_HALO), :]
        emit(2, dv0 * sgg)
        emit(3, dv0 * glu_v * sgg * (1.0 - sgg))

        dn = dn_parts[0]
        x = h_ref[...]
        rr = lax.rsqrt(jnp.mean(x * x, axis=-1, keepdims=True) + EPS)
        dx, dgp = _rms_bwd(dn, x, rr, g_ref[...])
        dh1_ref[...] = dh_ref[...] + dx
        sg_ref[SG_MIX:SG_MIX + 1, :] += dgp

    rev = lambda i: (nt - 1 - i, 0)
    row = pl.BlockSpec((tm, d), rev)
    wide = pl.BlockSpec((tm, n_in), rev)
    full = lambda a: pl.BlockSpec(a.shape, lambda i, nd=a.ndim: (0,) * nd)
    halo = pl.BlockSpec((halo_rows, d), lambda i: (jnp.maximum((nt - 1 - i) * per - 1, 0), 0))
    smalls = [cw4, wg, jnp.swapaxes(wg, 2, 3), ba, bx, lam, cw31, lng, lnb]
    return _call(
        body, "mixer_bwd", (nt,),
        [row, row, full(g), wide, row, row, halo, row, row, row, _any()]
        + [full(a) for a in smalls] + [_any()],
        [row, wide, pl.BlockSpec((tm, 3 * d), rev), pl.BlockSpec((tm, 3 * d), rev),
         pl.BlockSpec((tm, 2 * d), rev), pl.BlockSpec((sg_rows, d), lambda i: (0, 0))],
        [jax.ShapeDtypeStruct((tp, d), F32), jax.ShapeDtypeStruct((tp, n_in), BF16),
         jax.ShapeDtypeStruct((tp, 3 * d), BF16), jax.ShapeDtypeStruct((tp, 3 * d), BF16),
         jax.ShapeDtypeStruct((tp, 2 * d), BF16), jax.ShapeDtypeStruct((sg_rows, d), F32)],
        [pltpu.VMEM(win_t.shape, BF16),
         pltpu.VMEM((3, d, d), BF16),
         pltpu.VMEM((tm + CONV4_HALO, d), F32),
         pltpu.VMEM((tm + CONV31_HALO, d), F32),
         pltpu.VMEM((SUBLANES, tm + CONV31_HALO, d), F32),
         pltpu.VMEM((SUBLANES, d), F32),
         pltpu.SemaphoreType.DMA((1 + 3 * N_DEV,))],
        [dh2, h, g, proj, xr_s, hs_s, hs_s, v1_s, ya_s, yb_s, win_t, *smalls, w3_all], comm)


def _tn_matmul(name, x, y, x_spec, y_spec, n_blocks, kb, nb, tm, tp, out_shape, out_spec, out_view, comm=None,
               after=None):
    nt = tp // tm

    def body(x_ref, y_ref, *refs):
        o_ref, acc = refs[-2:]
        i = pl.program_id(1)

        @pl.when(i == 0)
        def _():
            acc[...] = jnp.zeros_like(acc)

        acc[...] += _tn(x_ref[...], y_ref[...])

        @pl.when(i == nt - 1)
        def _():
            o_ref[...] = acc[...].astype(BF16).reshape(out_view)

    follows = [] if after is None else [after]
    outs, extra = _call(body, name, (n_blocks, nt), [x_spec, y_spec] + [_any()] * len(follows), [out_spec],
                        [jax.ShapeDtypeStruct(out_shape, BF16)], [pltpu.VMEM((kb, nb), F32)], [x, y] + follows,
                        comm)
    return outs[0], extra


def kernel(x, meta_tokens, ffn1_norm, ffn1_w_gu, ffn1_w_down, mix_norm, w_in, b_in, rnn_conv_w, rnn_conv_b, rg_w_a, rg_b_a, rg_w_x, rg_b_x, rg_lambda, rnn_w_proj, conv_dw_w, conv_dw_b, conv_ln_g, conv_ln_b, conv_w_proj, conv_b_proj, w_out, ffn2_norm, ffn2_w_gu, ffn2_w_down, final_norm, loss_target, m_meta_tokens, m_ffn1_norm, m_ffn1_w_gu, m_ffn1_w_down, m_mix_norm, m_w_in, m_b_in, m_rnn_conv_w, m_rnn_conv_b, m_rg_w_a, m_rg_b_a, m_rg_w_x, m_rg_b_x, m_rg_lambda, m_rnn_w_proj, m_conv_dw_w, m_conv_dw_b, m_conv_ln_g, m_conv_ln_b, m_conv_w_proj, m_conv_b_proj, m_w_out, m_ffn2_norm, m_ffn2_w_gu, m_ffn2_w_down, m_final_norm, v_meta_tokens, v_ffn1_norm, v_ffn1_w_gu, v_ffn1_w_down, v_mix_norm, v_w_in, v_b_in, v_rnn_conv_w, v_rnn_conv_b, v_rg_w_a, v_rg_b_a, v_rg_w_x, v_rg_b_x, v_rg_lambda, v_rnn_w_proj, v_conv_dw_w, v_conv_dw_b, v_conv_ln_g, v_conv_ln_b, v_conv_w_proj, v_conv_b_proj, v_w_out, v_ffn2_norm, v_ffn2_w_gu, v_ffn2_w_down, v_final_norm):
    w = dict(locals())
    seq, d = x.shape[1], x.shape[2]
    n_meta = meta_tokens.shape[0]
    t_real = n_meta + seq
    tp, tm, tmx_fwd, tmx, tmt, tmw = _tiles(t_real)
    fb = ffn1_w_gu.shape[-1]
    wr = ffn1_w_down.shape[1]
    f = N_DEV * wr
    fc = f // FFN_CHUNKS
    nbc = w_in.shape[-1]
    n_in = N_DEV * nbc
    pr = rnn_w_proj.shape[1]
    hd = rg_w_a.shape[-1]
    gr = rg_w_a.shape[2]
    cw = meta_tokens.shape[1]
    k4, k31 = rnn_conv_w.shape[1], conv_dw_w.shape[1]
    assert n_in == 6 * d and 2 * wr == fb and N_HEADS * hd == d and pr * N_DEV == d

    xi, yi, ci = lax.axis_index("x"), lax.axis_index("y"), lax.axis_index("c")
    core = ci.astype(jnp.int32).reshape(1)
    chip = (2 * xi + yi).astype(jnp.int32).reshape(1)
    me_index = (4 * xi + 2 * yi + ci).astype(jnp.int32).reshape(1)

    for nm in ("ffn1_w_gu", "ffn2_w_gu"):
        for pre in ("", "m_", "v_"):
            w[pre + nm] = jnp.swapaxes(w[pre + nm], 1, 2)

    wgut1 = w["ffn1_w_gu"][0].astype(BF16)
    wgut2 = w["ffn2_w_gu"][0].astype(BF16)
    wd1 = ffn1_w_down[0].astype(BF16)
    wd2 = ffn2_w_down[0].astype(BF16)
    win_loc = w_in[0].astype(BF16)
    win_t_loc = jnp.swapaxes(w_in[0], 0, 1).astype(BF16)
    w3_loc = jnp.concatenate([rnn_w_proj[0], conv_w_proj[0], w_out[0]], axis=0).astype(BF16)
    wg_loc = jnp.stack([rg_w_a[0], rg_w_x[0]]).astype(BF16)
    n_small = n_meta + k4 + k31
    small_rows = -(-n_small // SUBLANES) * SUBLANES
    small_loc = jnp.concatenate([meta_tokens, rnn_conv_w[0], conv_dw_w[0],
                                 jnp.zeros((small_rows - n_small, cw), F32)], axis=0)
    (wgut1_all, wd1_all, small_all), h0, tgt = _first_gather(
        [wgut1, wd1, small_loc], 2, x[0], loss_target[0], n_meta, tp)
    small_full = small_all.transpose(1, 0, 2).reshape(small_rows, d)
    cw4 = small_full[n_meta:n_meta + k4]
    cw31 = small_full[n_meta + k4:n_meta + k4 + k31]

    wgu1, wdn1 = wgut1_all.reshape(2 * f, d), wd1_all.reshape(f, d)
    (h1, gu1, n1), (win_all, w3_all, wg_all) = _ffn_fwd(
        h0, ffn1_norm, wgu1, wdn1, tm, comm=_Gather([win_loc, w3_loc, wg_loc], pass_on_at=(0.65, 0.95)))
    wg = wg_all.transpose(1, 2, 0, 3, 4).reshape(2, N_HEADS, hd, hd)
    (h2, proj, n2, xr_s, hs_s, v1_s, ya_s, yb_s), (wgut2_all, wd2_all, win_t_all) = _mixer_fwd(
        h1, mix_norm, b_in, win_all, cw4, rnn_conv_b, wg, rg_b_a, rg_b_x, rg_lambda, cw31, conv_dw_b, conv_ln_g,
        conv_ln_b, conv_b_proj, w3_all, tmx_fwd, comm=_Gather([wgut2, wd2, win_t_loc], pass_on_at=(0.45, 0.7)))
    wgu2, wdn2 = wgut2_all.reshape(2 * f, d), wd2_all.reshape(f, d)
    win_t = win_t_all.reshape(n_in, d)
    (dh3, gu2, n3, tail), _ = _ffn_fwd(
        h2, ffn2_norm, wgu2, wdn2, tm, loss=(tgt, final_norm.reshape(1, d), n_meta, t_real))

    def d_w_gu(tag, dgu, n_s, after=None):
        g, _ = _tn_matmul(
            "d_w_gu" + tag, dgu, n_s,
            pl.BlockSpec((None, tmt, fc), lambda b, i: (b // FFN_CHUNKS, i, b % FFN_CHUNKS)),
            pl.BlockSpec((tmt, d), lambda b, i: (i, 0)),
            2 * FFN_CHUNKS, fc, d, tmt, tp, (2 * FFN_CHUNKS, fc, d),
            pl.BlockSpec((None, fc, d), lambda b, i: (b, 0, 0)), (fc, d), after=after)
        return g.reshape(N_DEV, fb, d)

    def d_w_down(tag, act, df, comm=None):
        g, extra = _tn_matmul(
            "d_w_down" + tag, act, df,
            pl.BlockSpec((tmt, fc), lambda b, i: (i, b)), pl.BlockSpec((tmt, d), lambda b, i: (i, 0)),
            FFN_CHUNKS, fc, d, tmt, tp, (FFN_CHUNKS, fc, d),
            pl.BlockSpec((None, fc, d), lambda b, i: (b, 0, 0)), (fc, d), comm)
        return g.reshape(N_DEV, wr, d), extra

    (dh2, dgu2, act2, df2, tail), _ = _ffn_bwd(dh3, h2, gu2, ffn2_norm, wgu2, wdn2, tm, tail, TAIL_FFN2, n3)
    g_wgu2 = d_w_gu("2", dgu2, n3)
    g_wd2, _ = d_w_down("2", act2, df2)
    (dh1, dproj, x3, y3, yg, sg), (r_wd2, r_wgu2) = _mixer_bwd(
        dh2, h1, mix_norm, proj, xr_s, hs_s, v1_s, ya_s, yb_s, win_t, cw4, wg, rg_b_a, rg_b_x, rg_lambda, cw31,
        conv_ln_g, conv_ln_b, w3_all, tmx, comm=_Scatter([g_wd2, g_wgu2]))
    g_w3, _ = _tn_matmul(
        "d_w_proj3", x3, y3,
        pl.BlockSpec((tmw, d), lambda b, i: (i, b)), pl.BlockSpec((tmw, d), lambda b, i: (i, b)),
        3, d, d, tmw, tp, (N_DEV, 3, pr, d), pl.BlockSpec((N_DEV, None, pr, d), lambda b, i: (0, b, 0, 0)),
        (N_DEV, pr, d))
    g_wg, _ = _tn_matmul(
        "d_w_gates", xr_s, yg,
        pl.BlockSpec((tmw, hd), lambda b, i: (i, b % N_HEADS)), pl.BlockSpec((tmw, hd), lambda b, i: (i, b)),
        2 * N_HEADS, hd, hd, tmw, tp, (N_DEV, 2 * N_HEADS, gr, hd),
        pl.BlockSpec((N_DEV, None, gr, hd), lambda b, i: (0, b, 0, 0)), (N_DEV, gr, hd))
    w3_sems, g_w3_thru, w3_land, w3_token = _exchange_start("grads_proj3_exchange", _scatter_copies, N_DEV - 1, g_w3)
    g_win, (r_wg,) = _tn_matmul(
        "d_w_in", n2, dproj,
        pl.BlockSpec((tmw, d), lambda b, i: (i, 0)), pl.BlockSpec((tmw, nbc), lambda b, i: (i, b)),
        N_DEV, d, nbc, tmw, tp, (N_DEV, d, nbc), pl.BlockSpec((None, d, nbc), lambda b, i: (b, 0, 0)), (d, nbc),
        comm=_Scatter([g_wg]), after=w3_token)
    win_sems, g_win_thru, win_land, win_token = _exchange_start("grads_w_in_exchange", _scatter_copies, N_DEV - 1, g_win)
    (dh0, dgu1, act1, df1, tail), _ = _ffn_bwd(dh1, h0, gu1, ffn1_norm, wgu1, wdn1, tm, tail, TAIL_FFN1, win_token)

    pieces = [sg, dh0[:n_meta], tail]
    assert all(p.shape[0] % SUBLANES == 0 for p in pieces)
    at = [0, sg.shape[0], sg.shape[0] + n_meta]
    loss_row = at[2] + TAIL_LOSS
    rep_rows = [("ffn1_norm", at[2] + TAIL_FFN1, 1), ("mix_norm", SG_MIX, 1), ("b_in", SG_BIN, 6),
                ("rnn_conv_b", SG_CB4, 1),
                ("rg_b_a", SG_BA, 1), ("rg_b_x", SG_BX, 1), ("rg_lambda", SG_LAM, 1), ("conv_dw_b", SG_CB31, 1),
                ("conv_ln_g", SG_LNG, 1), ("conv_ln_b", SG_LNB, 1), ("conv_b_proj", SG_BCP, 1),
                ("ffn2_norm", at[2] + TAIL_FFN2, 1), ("final_norm", at[2] + TAIL_FINAL, 1)]
    col_rows = [("meta_tokens", at[1], n_meta), ("rnn_conv_w", SG_CW4, k4), ("conv_dw_w", SG_CW31, k31)]
    layout = []
    for nm, row0, nr in rep_rows:
        kind = "wide" if nm == "b_in" else "rep"
        as2d = lambda a: a.reshape(1, -1) if a.ndim == 1 else a
        layout.append((kind, row0, nr, as2d(w[nm]), as2d(w["m_" + nm]), as2d(w["v_" + nm])))
    for nm, row0, nr in col_rows:
        sq = lambda a: a.reshape(a.shape[-2], a.shape[-1])
        layout.append(("col", row0, nr, sq(w[nm]), sq(w["m_" + nm]), sq(w["v_" + nm])))
    small_partial = jnp.concatenate(pieces, axis=0)

    g_wd1, (small_partials,) = d_w_down("1", act1, df1, comm=_Bcast(small_partial))
    wd1_sems, g_wd1_thru, wd1_land, wd1_token = _exchange_start(
        "grads_w_down1_exchange", _scatter_copies, N_DEV - 1, g_wd1)
    g_wgu1 = d_w_gu("1", dgu1, n1, after=wd1_token)

    g_last = g_wgu1.reshape((4, 2) + g_wgu1.shape[1:])
    comb_wgu1 = _pair_reduce(g_last, core)
    sems, comb_thru, land_thru, after = _exchange_start("grads_chip_exchange", _chip_copies, 3, comb_wgu1)
    g_win, r_win = _exchange_wait("grads_w_in_exchange", _scatter_copies, win_sems, g_win_thru, win_land, after)
    g_w3, r_w3 = _exchange_wait("grads_proj3_exchange", _scatter_copies, w3_sems, g_w3_thru, w3_land, after)
    g_wd1, r_wd1 = _exchange_wait("grads_w_down1_exchange", _scatter_copies, wd1_sems, g_wd1_thru, wd1_land, after)

    grad_x = (dh0[n_meta:t_real] + after[0, 0])[None]
    total, small_out = _small_adamw(small_partials, layout, me_index, grad_x)
    after = total

    groups = [(g_wd1, r_wd1, me_index, ["ffn1_w_down"]),
              (g_wd2, r_wd2, me_index, ["ffn2_w_down"]), (g_wgu2, r_wgu2, me_index, ["ffn2_w_gu"]),
              (g_win, r_win, me_index, ["w_in"]), (g_w3, r_w3, me_index, ["w_out", "rnn_w_proj", "conv_w_proj"]),
              (g_wg, r_wg, me_index, ["rg_w_a", "rg_w_x"]), (None, None, chip, ["ffn1_w_gu"])]
    res = {}
    for own, recv, idx, group in groups:
        if own is None:
            own, recv = _exchange_wait("grads_chip_exchange", _chip_copies, sems, comb_thru, land_thru, after)
        outs = _final_adamw(own, recv, idx, [(w[nm], w["m_" + nm], w["v_" + nm]) for nm in group], after)
        after = outs[-1][0]
        for nm, o in zip(group, outs):
            res[nm] = o
    for nm in ("ffn1_w_gu", "ffn2_w_gu"):
        res[nm] = tuple(jnp.swapaxes(a, 1, 2) for a in res[nm])
    for (nm, _, _), o in zip(rep_rows + col_rows, small_out):
        res[nm] = tuple(a.reshape(w[nm].shape) for a in o)


    order = ["meta_tokens", "ffn1_norm", "ffn1_w_gu", "ffn1_w_down", "mix_norm", "w_in", "b_in", "rnn_conv_w",
             "rnn_conv_b", "rg_w_a", "rg_b_a", "rg_w_x", "rg_b_x", "rg_lambda", "rnn_w_proj", "conv_dw_w",
             "conv_dw_b", "conv_ln_g", "conv_ln_b", "conv_w_proj", "conv_b_proj", "w_out", "ffn2_norm",
             "ffn2_w_gu", "ffn2_w_down", "final_norm"]
    return (total[loss_row, 0], grad_x, *[res[nm][0] for nm in order], *[res[nm][1] for nm in order],
            *[res[nm][2] for nm in order], *[res[nm][3] for nm in order])
```

```python
import functools
import math

import jax
import jax.numpy as jnp
from jax import lax
from jax.experimental import pallas as pl
from jax.experimental.pallas import tpu as pltpu

F32 = jnp.float32
BF16 = jnp.bfloat16
MESH = pl.DeviceIdType.MESH
N_DEV = 8
N_HEADS = 4
RG_LRU_C = 8.0
EPS = 1e-6
FFN_RES = 0.5
ADAM_LR, ADAM_B1, ADAM_B2, ADAM_EPS, ADAM_WD, ADAM_STEP = 0.001, 0.9, 0.999, 1e-08, 0.01, 10
V7X_VMEM_LIMIT = 56 * 1024 * 1024
CONV4_HALO = 8
CONV31_HALO = 32
SUBLANES = 8
STAGE_ROWS = 512
TAIL_FFN1, TAIL_FINAL, TAIL_LOSS, TAIL_FFN2 = 0, 1, 2, 3
FFN_CHUNKS = 2
FFN_FWD_CHUNKS = 1
GELU_C = math.sqrt(2.0 / math.pi)
GELU_K = 0.044715


def _any():
    return pl.BlockSpec(memory_space=pl.ANY)


def _params(n_grid):
    return pltpu.CompilerParams(dimension_semantics=("arbitrary",) * n_grid, vmem_limit_bytes=V7X_VMEM_LIMIT)


def _nn(a, b):
    return jnp.dot(a, b, preferred_element_type=F32)


def _nt(a, b):
    return lax.dot_general(a, b, (((1,), (1,)), ((), ())), preferred_element_type=F32)


def _tn(a, b):
    return lax.dot_general(a, b, (((0,), (0,)), ((), ())), preferred_element_type=F32)


def _sigmoid(x):
    return 0.5 * jnp.tanh(0.5 * x) + 0.5


def _rowsum(x):
    return jnp.sum(x, axis=0, keepdims=True)


def _rms_fwd(x, g):
    r = lax.rsqrt(jnp.mean(x * x, axis=-1, keepdims=True) + EPS)
    return x * r * g, r


def _rms_bwd(dn, x, r, g):
    xr = x * r
    gy = dn * g
    dx = r * (gy - xr * jnp.mean(gy * xr, axis=-1, keepdims=True))
    return dx, _rowsum(dn * xr)


def _gelu(y):
    t = jnp.tanh(GELU_C * (y + GELU_K * y * y * y))
    return 0.5 * y * (1.0 + t), t


def _gelu_grad(y, t):
    return 0.5 * (1.0 + t) + 0.5 * y * (1.0 - t * t) * GELU_C * (1.0 + 3.0 * GELU_K * y * y)


def _softplus(x):
    return jnp.maximum(x, 0.0) + jnp.log(1.0 + jnp.exp(-jnp.abs(x)))


def _one_minus_exp(z):
    series = -z * (1.0 + 0.5 * z * (1.0 + z * (1.0 / 3.0) * (1.0 + 0.25 * z)))
    return jnp.where(z > -0.05, series, 1.0 - jnp.exp(z))


def _tiles(t_real):
    if t_real > 2048:
        tm = 416
        tp = -(-t_real // tm) * tm
        return tp, tm, tm // 2, tm // 2, tp, tp
    tm = 128
    tp = -(-t_real // tm) * tm
    return tp, tm, tm // 2, tm // 2, tm, tm


def _load_weights(copies, sems):
    cps = [pltpu.make_async_copy(s, d, sems.at[k]) for k, (s, d) in enumerate(copies)]
    for cp in cps:
        cp.start()
    for cp in cps:
        cp.wait()


def _position():
    x, y, c = lax.axis_index("x"), lax.axis_index("y"), lax.axis_index("c")
    chips = [(1 - x, y), (x, 1 - y), (1 - x, 1 - y)]
    return x, y, c, chips


def _slot(p):
    return 4 * p[0] + 2 * p[1] + p[2]


class _Lazy(dict):
    def __getitem__(self, key):
        val = dict.__getitem__(self, key)
        return val() if callable(val) else val


class _Gather:
    def __init__(self, shards, pass_on_at=None):
        self.shards = list(shards)
        self.n = len(self.shards)
        self.pass_on_at = pass_on_at

    def inputs(self):
        return self.shards

    def out_shape(self):
        return [jax.ShapeDtypeStruct((N_DEV,) + s.shape, s.dtype) for s in self.shards]

    N_SEMS = 9

    def scratch(self):
        return [pltpu.SemaphoreType.DMA((self.N_SEMS * self.n,)), pltpu.SemaphoreType.DMA((self.N_SEMS * self.n,)),
                pltpu.SemaphoreType.DMA((self.n,))]

    def _plan(self, ins, outs, sems):
        send_sems, recv_sems, local_sems = sems
        x, y, c, _ = _position()
        me, sib, xn, yn, dg = (x, y, c), (x, y, 1 - c), (1 - x, y, c), (x, 1 - y, c), (1 - x, 1 - y, c)
        other = lambda p: (p[0], p[1], 1 - c)

        def blk(a, p, half=None):
            ref = outs[a].at[_slot(p)]
            if half is None:
                return ref
            rows = self.shards[a].shape[0] // 2
            return ref.at[pl.ds(half * rows, rows)]

        def copy(a, k, dst, to, src=None):
            return pltpu.make_async_remote_copy(
                src_ref=dst if src is None else src, dst_ref=dst,
                send_sem=send_sems.at[self.N_SEMS * a + k], recv_sem=recv_sems.at[self.N_SEMS * a + k],
                device_id=to, device_id_type=MESH)

        cp = _Lazy(mine=lambda: [pltpu.make_async_copy(ins[a], blk(a, me), local_sems.at[a]) for a in range(self.n)])
        for a in range(self.n):
            cp[a] = _Lazy(
                own=lambda a=a: [copy(a, 0, blk(a, me), sib, src=ins[a]), copy(a, 1, blk(a, me), xn, src=ins[a]),
                                 copy(a, 2, blk(a, me), yn, src=ins[a])],
                from_x=lambda a=a: copy(a, 1, blk(a, xn), me), from_y=lambda a=a: copy(a, 2, blk(a, yn), me),
                relay_x=lambda a=a: copy(a, 3, blk(a, xn, 0), yn), relay_y=lambda a=a: copy(a, 4, blk(a, yn, 1), xn),
                diag0=lambda a=a: copy(a, 3, blk(a, dg, 0), me), diag1=lambda a=a: copy(a, 4, blk(a, dg, 1), me),
                pass_x=lambda a=a: copy(a, 5, blk(a, xn), sib), pass_y=lambda a=a: copy(a, 6, blk(a, yn), sib),
                pass_d0=lambda a=a: copy(a, 7, blk(a, dg, 0), sib), pass_d1=lambda a=a: copy(a, 8, blk(a, dg, 1), sib),
                from_sib=lambda a=a: [copy(a, 0, blk(a, sib), me), copy(a, 5, blk(a, other(xn)), me),
                                      copy(a, 6, blk(a, other(yn)), me), copy(a, 7, blk(a, other(dg), 0), me),
                                      copy(a, 8, blk(a, other(dg), 1), me)])
        return cp

    def start(self, ins, outs, sems):
        cp = self._plan(ins, outs, sems)
        for c in cp["mine"]:
            c.start()
        for a in range(self.n):
            for c in cp[a]["own"]:
                c.start()

    def pass_on(self, ins, outs, sems):
        cp = self._plan(ins, outs, sems)
        for a in range(self.n):
            cp[a]["from_x"].wait_recv()
            cp[a]["relay_x"].start()
            cp[a]["pass_x"].start()
        for a in range(self.n):
            cp[a]["from_y"].wait_recv()
            cp[a]["relay_y"].start()
            cp[a]["pass_y"].start()

    def pass_on_relayed(self, ins, outs, sems):
        cp = self._plan(ins, outs, sems)
        for a in range(self.n):
            cp[a]["diag0"].wait_recv()
            cp[a]["pass_d0"].start()
            cp[a]["diag1"].wait_recv()
            cp[a]["pass_d1"].start()

    def finish(self, ins, outs, sems):
        if self.pass_on_at is None:
            self.pass_on(ins, outs, sems)
            self.pass_on_relayed(ins, outs, sems)
        cp = self._plan(ins, outs, sems)
        for a in range(self.n):
            for c in cp[a]["from_sib"]:
                c.wait_recv()
            for c in cp[a]["own"] + [cp[a][k] for k in ("relay_x", "relay_y", "pass_x", "pass_y", "pass_d0", "pass_d1")]:
                c.wait_send()
        for c in cp["mine"]:
            c.wait()


class _Scatter:
    def __init__(self, grads):
        self.grads = list(grads)
        self.n = len(self.grads)

    def inputs(self):
        return self.grads

    def out_shape(self):
        return [jax.ShapeDtypeStruct((N_DEV - 1,) + g.shape[1:], g.dtype) for g in self.grads]

    def scratch(self):
        return [pltpu.SemaphoreType.DMA((7 * self.n,)), pltpu.SemaphoreType.DMA((7 * self.n,))]

    def _plan(self, ins, outs, sems):
        send_sems, recv_sems = sems
        x, y, c, _ = _position()
        cps = []
        for a in range(self.n):
            for k in range(1, N_DEV):
                peer = (x ^ (k >> 2), y ^ ((k >> 1) & 1), c ^ (k & 1))
                cps.append(pltpu.make_async_remote_copy(
                    src_ref=ins[a].at[_slot(peer)], dst_ref=outs[a].at[k - 1],
                    send_sem=send_sems.at[7 * a + k - 1], recv_sem=recv_sems.at[7 * a + k - 1],
                    device_id=peer, device_id_type=MESH))
        return cps

    def start(self, ins, outs, sems):
        for cp in self._plan(ins, outs, sems):
            cp.start()

    def finish(self, ins, outs, sems):
        for cp in self._plan(ins, outs, sems):
            cp.wait()


def _hosted(inner, n_in, n_out, comm, grid):
    if comm is None:
        return inner
    nc_in, nc_out, ns = len(comm.inputs()), len(comm.out_shape()), len(comm.scratch())

    def body(*refs):
        o0 = n_in + nc_in
        s0 = o0 + n_out + nc_out
        main = refs[:n_in] + refs[o0:o0 + n_out] + refs[s0:len(refs) - ns]
        c_in, c_out, c_sems = refs[n_in:o0], refs[o0 + n_out:s0], refs[len(refs) - ns:]
        ids = [pl.program_id(ax) for ax in range(len(grid))]
        first = functools.reduce(jnp.logical_and, [i == 0 for i in ids])
        last = functools.reduce(jnp.logical_and, [i == g - 1 for i, g in zip(ids, grid)])

        @pl.when(first)
        def _():
            comm.start(c_in, c_out, c_sems)

        inner(*main)

        if getattr(comm, "pass_on_at", None) is not None:
            assert len(grid) == 1
            first_at, second_at = (min(grid[0] - 1, int(frac * grid[0])) for frac in comm.pass_on_at)
            assert first_at < second_at

            @pl.when(ids[0] == first_at)
            def _():
                comm.pass_on(c_in, c_out, c_sems)

            @pl.when(ids[0] == second_at)
            def _():
                comm.pass_on_relayed(c_in, c_out, c_sems)

        @pl.when(last)
        def _():
            comm.finish(c_in, c_out, c_sems)

    return body


def _call(inner, name, grid, in_specs, out_specs, out_shape, scratch, args, comm=None):
    n_in, n_out = len(args), len(out_shape)
    body = _hosted(inner, n_in, n_out, comm, grid)
    if comm is not None:
        in_specs = list(in_specs) + [_any()] * len(comm.inputs())
        args = list(args) + comm.inputs()
        out_specs = list(out_specs) + [_any()] * len(comm.out_shape())
        out_shape = list(out_shape) + comm.out_shape()
        scratch = list(scratch) + comm.scratch()
    outs = pl.pallas_call(
        body, name=name, grid=grid, in_specs=list(in_specs), out_specs=list(out_specs), out_shape=list(out_shape),
        scratch_shapes=list(scratch), compiler_params=_params(len(grid)))(*args)
    return list(outs[:n_out]), list(outs[n_out:])


def _first_gather(shards, small_idx, x2, t2, n_meta, tp):
    comm = _Gather(shards)
    n = comm.n
    seq, d = x2.shape
    t_real = n_meta + seq
    n_pad = tp - t_real
    cw = d // N_DEV
    rows = STAGE_ROWS if seq % STAGE_ROWS == 0 else seq
    n_chunks = seq // rows

    def body(*refs):
        ins, (x_ref, t_ref) = refs[:n], refs[n:n + 2]
        outs, (h0_ref, tg_ref) = refs[n + 2:2 * n + 2], refs[2 * n + 2:2 * n + 4]
        sems = refs[2 * n + 4:2 * n + 7]
        buf, zeros, in_sems, out_sems, misc_sems = refs[2 * n + 7:]
        comm.start(ins, outs, sems)
        zeros[...] = jnp.zeros_like(zeros)
        fills = [pltpu.make_async_copy(zeros.at[pl.ds(0, n_pad)], h0_ref.at[pl.ds(t_real, n_pad)], misc_sems.at[0]),
                 pltpu.make_async_copy(zeros.at[pl.ds(0, n_pad)], tg_ref.at[pl.ds(t_real, n_pad)], misc_sems.at[1]),
                 pltpu.make_async_copy(zeros.at[pl.ds(0, n_meta)], tg_ref.at[pl.ds(0, n_meta)], misc_sems.at[2])]
        for cp in fills:
            cp.start()
        jobs = [(src, dst, c) for src, dst in ((x_ref, h0_ref), (t_ref, tg_ref)) for c in range(n_chunks)]

        def load(k):
            src, _, c = jobs[k]
            return pltpu.make_async_copy(src.at[pl.ds(c * rows, rows)], buf.at[k % 2], in_sems.at[k % 2])

        def store(k):
            _, dst, c = jobs[k]
            return pltpu.make_async_copy(buf.at[k % 2], dst.at[pl.ds(n_meta + c * rows, rows)], out_sems.at[k % 2])

        load(0).start()
        for k in range(len(jobs)):
            load(k).wait()
            if k + 1 < len(jobs):
                if k >= 1:
                    store(k - 1).wait()
                load(k + 1).start()
            store(k).start()
        for k in range(max(0, len(jobs) - 2), len(jobs)):
            store(k).wait()
        comm.finish(ins, outs, sems)
        meta = [pltpu.make_async_copy(outs[small_idx].at[k, pl.ds(0, n_meta)],
                                      h0_ref.at[pl.ds(0, n_meta), pl.ds(k * cw, cw)], misc_sems.at[3 + k])
                for k in range(N_DEV)]
        for cp in meta:
            cp.start()
        for cp in fills + meta:
            cp.wait()

    staged = [jax.ShapeDtypeStruct((tp, d), F32)] * 2
    outs = pl.pallas_call(
        body, name="weights_all_gather", out_shape=comm.out_shape() + staged,
        in_specs=[_any()] * (n + 2), out_specs=[_any()] * (n + 2),
        scratch_shapes=comm.scratch() + [
            pltpu.VMEM((2, rows, d), F32), pltpu.VMEM((max(n_pad, n_meta), d), F32),
            pltpu.SemaphoreType.DMA((2,)), pltpu.SemaphoreType.DMA((2,)), pltpu.SemaphoreType.DMA((3 + N_DEV,))],
        compiler_params=pltpu.CompilerParams(vmem_limit_bytes=V7X_VMEM_LIMIT),
    )(*shards, x2, t2)
    return outs[:n], outs[n], outs[n + 1]


def _chip_copies(c_ref, land_ref, sems):
    _, _, c, chips = _position()
    return [pltpu.make_async_remote_copy(
        src_ref=c_ref.at[2 * cx + cy], dst_ref=land_ref.at[j], send_sem=sems[j], recv_sem=sems[3 + j],
        device_id=(cx, cy, c), device_id_type=MESH) for j, (cx, cy) in enumerate(chips)]


def _scatter_copies(g_ref, land_ref, sems):
    x, y, c, _ = _position()
    cps = []
    for k in range(1, N_DEV):
        peer = (x ^ (k >> 2), y ^ ((k >> 1) & 1), c ^ (k & 1))
        cps.append(pltpu.make_async_remote_copy(
            src_ref=g_ref.at[_slot(peer)], dst_ref=land_ref.at[k - 1], send_sem=sems[k - 1],
            recv_sem=sems[N_DEV - 1 + k - 1], device_id=peer, device_id_type=MESH))
    return cps


def _bcast_copies(b_ref, land_ref, sems):
    x, y, c, _ = _position()
    cps = []
    for k in range(1, N_DEV):
        peer = (x ^ (k >> 2), y ^ ((k >> 1) & 1), c ^ (k & 1))
        cps.append(pltpu.make_async_remote_copy(
            src_ref=b_ref, dst_ref=land_ref.at[_slot((x, y, c))], send_sem=sems[k - 1],
            recv_sem=sems[N_DEV - 1 + k - 1], device_id=peer, device_id_type=MESH))
    return cps


def _exchange_start(name, copies, n_copies, src, keep_own=False):
    hbm = pl.BlockSpec(memory_space=pltpu.HBM)
    sem = pl.BlockSpec(memory_space=pltpu.SEMAPHORE)
    n_sems = 2 * n_copies

    def body(s_ref, land_ref, *refs):
        for cp in copies(s_ref, land_ref, refs[:n_sems]):
            cp.start()
        token = refs[n_sems + 2]
        token[...] = jnp.zeros_like(token)
        if keep_own:
            own = pltpu.make_async_copy(s_ref, land_ref.at[_slot(_position()[:3])], refs[n_sems + 3])
            own.start()
            own.wait()

    land = lax.empty((N_DEV,) + src.shape if keep_own else (n_copies,) + src.shape[1:], src.dtype)
    outs = pl.pallas_call(
        body, name=name + "_start",
        out_shape=(pltpu.SemaphoreType.DMA(()),) * n_sems
        + (pltpu.HBM(src.shape, src.dtype), pltpu.HBM(land.shape, land.dtype),
           jax.ShapeDtypeStruct((SUBLANES, 128), F32)),
        in_specs=(hbm, hbm), out_specs=(sem,) * n_sems + (hbm, hbm, pl.BlockSpec(memory_space=pltpu.VMEM)),
        scratch_shapes=[pltpu.SemaphoreType.DMA(())] if keep_own else [],
        input_output_aliases={0: n_sems, 1: n_sems + 1},
        compiler_params=pltpu.CompilerParams(has_side_effects=pltpu.SideEffectType.DATAFLOW_SIDE_EFFECTING),
    )(pltpu.with_memory_space_constraint(src, pltpu.HBM), pltpu.with_memory_space_constraint(land, pltpu.HBM))
    return outs[:n_sems], outs[n_sems], outs[n_sems + 1], outs[n_sems + 2]


def _exchange_wait(name, copies, sems, src_thru, land_thru, after):
    hbm = pl.BlockSpec(memory_space=pltpu.HBM)
    sem = pl.BlockSpec(memory_space=pltpu.SEMAPHORE)
    n_sems = len(sems)

    def body(s_ref, land_ref, *refs):
        for cp in copies(s_ref, land_ref, refs[:n_sems]):
            cp.wait_send()
            cp.wait_recv()

    return pl.pallas_call(
        body, name=name + "_wait",
        out_shape=(pltpu.HBM(src_thru.shape, src_thru.dtype), pltpu.HBM(land_thru.shape, land_thru.dtype)),
        in_specs=(hbm, hbm) + (sem,) * n_sems + (pl.BlockSpec(memory_space=pl.ANY),), out_specs=(hbm, hbm),
        input_output_aliases={0: 0, 1: 1},
        compiler_params=pltpu.CompilerParams(has_side_effects=pltpu.SideEffectType.DATAFLOW_SIDE_EFFECTING),
    )(src_thru, land_thru, *sems, after)


def _pair_reduce(grad, core):
    blk = grad.shape[2:]
    zeros = (0,) * len(blk)

    def body(core_ref, g_hbm, own_ref, o_ref, landed, send_sems, recv_sems):
        del core_ref
        i = pl.program_id(0)
        x, y, c, _ = _position()

        def copy(k):
            return pltpu.make_async_remote_copy(
                src_ref=g_hbm.at[k, 1 - c], dst_ref=landed.at[k], send_sem=send_sems.at[k],
                recv_sem=recv_sems.at[k], device_id=(x, y, 1 - c), device_id_type=MESH)

        @pl.when(i == 0)
        def _():
            for k in range(4):
                copy(k).start()

        for k in range(4):
            @pl.when(i == k)
            def _(k=k):
                copy(k).wait_recv()

        o_ref[...] = (own_ref[...].astype(F32) + landed[i].astype(F32)).astype(BF16)

        @pl.when(i == 3)
        def _():
            for k in range(4):
                copy(k).wait_send()

    return pl.pallas_call(
        body, name="grads_pair_reduce",
        out_shape=jax.ShapeDtypeStruct((4,) + blk, BF16),
        grid_spec=pltpu.PrefetchScalarGridSpec(
            num_scalar_prefetch=1, grid=(4,),
            in_specs=[_any(), pl.BlockSpec((None, None) + blk, lambda i, cr: (i, cr[0]) + zeros)],
            out_specs=pl.BlockSpec((None,) + blk, lambda i, cr: (i,) + zeros),
            scratch_shapes=[pltpu.VMEM((4,) + blk, BF16), pltpu.SemaphoreType.DMA((4,)),
                            pltpu.SemaphoreType.DMA((4,))]),
        compiler_params=_params(1),
    )(core, grad, grad)


def _adamw(w, g, m, v):
    m2 = ADAM_B1 * m + (1.0 - ADAM_B1) * g
    v2 = ADAM_B2 * v + (1.0 - ADAM_B2) * (g * g)
    m_hat = m2 / (1.0 - ADAM_B1 ** ADAM_STEP)
    v_hat = v2 / (1.0 - ADAM_B2 ** ADAM_STEP)
    delta = -ADAM_LR * (m_hat / (jnp.sqrt(v_hat) + ADAM_EPS) + ADAM_WD * w)
    return delta, m2, v2


def _final_adamw(own, recv, idx, parts, after):
    blk = own.shape[1:]
    n_recv = recv.shape[0]
    n_parts = len(parts)
    per = blk[0] // n_parts if n_parts > 1 else None
    rows = blk[-2]
    n_chunks = 1 if n_parts > 1 else (4 if rows % 64 == 0 and rows >= 512 else (2 if rows % 32 == 0 else 1))
    cblk = blk[:-2] + (rows // n_chunks, blk[-1])
    lead = (0,) * (len(blk) - 2)

    def body(idx_ref, c_ref, r_ref, after_ref, *refs):
        del idx_ref, after_ref
        ins, outs = refs[:3 * n_parts], refs[3 * n_parts:]
        g = c_ref[...].astype(F32)
        for k in range(n_recv):
            g = g + r_ref[k].astype(F32)
        for p in range(n_parts):
            w_ref, m_ref, v_ref = ins[3 * p:3 * p + 3]
            if n_parts == 1:
                gp = g
            elif per == 1:
                gp = g[p]
            else:
                gp = g[p * per:(p + 1) * per]
            delta, m2, v2 = _adamw(w_ref[0], gp, m_ref[0], v_ref[0])
            o = outs[4 * p:4 * p + 4]
            o[0][0] = gp
            o[1][0] = delta
            o[2][0] = m2
            o[3][0] = v2

    flat = [a for wmv in parts for a in wmv]

    def part_spec(a):
        shape = a.shape[:-2] + (a.shape[-2] // n_chunks, a.shape[-1])
        return pl.BlockSpec(shape, lambda i, cr, nd=a.ndim: (0,) * (nd - 2) + (i, 0))

    outs = pl.pallas_call(
        body, name="grads_sum_adamw",
        out_shape=[jax.ShapeDtypeStruct(wmv[0].shape, F32) for wmv in parts for _ in range(4)],
        grid_spec=pltpu.PrefetchScalarGridSpec(
            num_scalar_prefetch=1, grid=(n_chunks,),
            in_specs=[pl.BlockSpec((None,) + cblk, lambda i, cr: (cr[0],) + lead + (i, 0)),
                      pl.BlockSpec((n_recv,) + cblk, lambda i, cr: (0,) + lead + (i, 0))]
                     + [_any()] + [part_spec(a) for a in flat],
            out_specs=[part_spec(wmv[0]) for wmv in parts for _ in range(4)]),
        compiler_params=_params(1),
    )(idx, own, recv, after, *flat)
    return [tuple(outs[4 * p:4 * p + 4]) for p in range(n_parts)]


def _small_adamw(partials, layout, me_index, after):
    _, rows, d = partials.shape
    n = len(layout)
    cw = d // N_DEV

    def body(me_ref, p_ref, after_ref, *refs):
        ins, t_ref, outs = refs[:3 * n], refs[3 * n], refs[3 * n + 1:]
        me = me_ref[0]
        total = p_ref[0]
        for j in range(1, N_DEV):
            total = total + p_ref[j]
        t_ref[...] = total
        for e, (kind, r0, nr, _, _, _) in enumerate(layout):
            w_ref, m_ref, v_ref = ins[3 * e:3 * e + 3]
            o = outs[4 * e:4 * e + 4]
            if kind == "rep":
                g = t_ref[r0:r0 + nr, :]
                delta, m2, v2 = _adamw(w_ref[...], g, m_ref[...], v_ref[...])
                for ref, val in zip(o, (g, delta, m2, v2)):
                    ref[...] = val
            elif kind == "wide":
                for q in range(nr):
                    sl = slice(q * d, (q + 1) * d)
                    g = t_ref[r0 + q:r0 + q + 1, :]
                    delta, m2, v2 = _adamw(w_ref[:, sl], g, m_ref[:, sl], v_ref[:, sl])
                    for ref, val in zip(o, (g, delta, m2, v2)):
                        ref[:, sl] = val
            else:
                for j in range(N_DEV):
                    @pl.when(me == j)
                    def _(j=j, o=o, w_ref=w_ref, m_ref=m_ref, v_ref=v_ref, r0=r0, nr=nr):
                        g = t_ref[r0:r0 + nr, j * cw:(j + 1) * cw]
                        delta, m2, v2 = _adamw(w_ref[...], g, m_ref[...], v_ref[...])
                        for ref, val in zip(o, (g, delta, m2, v2)):
                            ref[...] = val

    flat = [a for ent in layout for a in ent[3:]]
    vm = pl.BlockSpec(memory_space=pltpu.VMEM)
    outs = pl.pallas_call(
        body, name="small_adamw",
        out_shape=[jax.ShapeDtypeStruct((rows, d), F32)]
                  + [jax.ShapeDtypeStruct(ent[3].shape, F32) for ent in layout for _ in range(4)],
        in_specs=[pl.BlockSpec(memory_space=pltpu.SMEM), vm, _any()] + [vm] * len(flat),
        out_specs=[vm] * (1 + 4 * n),
        compiler_params=pltpu.CompilerParams(vmem_limit_bytes=V7X_VMEM_LIMIT),
    )(me_index, partials, after, *flat)
    return outs[0], [tuple(outs[1 + 4 * e:5 + 4 * e]) for e in range(n)]


def _ffn_fwd(h, g, wgu, wd, tm, loss=None, comm=None):
    tp, d = h.shape
    f = wd.shape[0]
    fc = f // FFN_FWD_CHUNKS
    nt = tp // tm
    with_loss = loss is not None
    if with_loss:
        tgt, gf, n_meta, t_real = loss

    def body(*refs):
        if with_loss:
            (h_ref, g_ref, wgu_hbm, wd_hbm, tgt_ref, gf_ref, out_ref, gu_ref, n_ref, tail_ref,
             wgu_v, wd_v, sems) = refs
        else:
            h_ref, g_ref, wgu_hbm, wd_hbm, out_ref, gu_ref, n_ref, wgu_v, wd_v, sems = refs
        i = pl.program_id(0)

        @pl.when(i == 0)
        def _():
            _load_weights([(wgu_hbm, wgu_v), (wd_hbm, wd_v)], sems)
            if with_loss:
                tail_ref[...] = jnp.zeros_like(tail_ref)

        x = h_ref[...]
        n, _ = _rms_fwd(x, g_ref[...])
        nb = n.astype(BF16)
        n_ref[...] = nb
        acc = jnp.zeros((tm, d), F32)
        for j in range(FFN_FWD_CHUNKS):
            cols = slice(j * fc, (j + 1) * fc)
            gate = _nt(nb, wgu_v[pl.ds(j * fc, fc), :])
            up = _nt(nb, wgu_v[pl.ds(f + j * fc, fc), :])
            gu_ref[0, :, cols] = gate.astype(BF16)
            gu_ref[1, :, cols] = up.astype(BF16)
            act = (gate * _sigmoid(gate) * up).astype(BF16)
            acc = acc + _nn(act, wd_v[pl.ds(j * fc, fc), :])
        hn = x + FFN_RES * acc
        if not with_loss:
            out_ref[...] = hn
        else:
            gfv = gf_ref[...]
            r = lax.rsqrt(jnp.mean(hn * hn, axis=-1, keepdims=True) + EPS)
            xr = hn * r
            rows = i * tm + lax.broadcasted_iota(jnp.int32, (tm, 1), 0)
            mask = jnp.logical_and(rows >= n_meta, rows < t_real)
            diff = jnp.where(mask, xr * gfv - tgt_ref[...], 0.0)
            tail_ref[TAIL_LOSS:TAIL_LOSS + 1, :] += jnp.zeros((1, d), F32) + 0.5 * jnp.sum(diff * diff) / d
            dy = diff / d
            gy = dy * gfv
            out_ref[...] = r * (gy - xr * jnp.mean(gy * xr, axis=-1, keepdims=True))
            tail_ref[TAIL_FINAL:TAIL_FINAL + 1, :] += _rowsum(dy * xr)

    row = pl.BlockSpec((tm, d), lambda i: (i, 0))
    vec = pl.BlockSpec((1, d), lambda i: (0, 0))
    in_specs = [row, vec, _any(), _any()]
    out_shape = [jax.ShapeDtypeStruct((tp, d), F32), jax.ShapeDtypeStruct((2, tp, f), BF16),
                 jax.ShapeDtypeStruct((tp, d), BF16)]
    out_specs = [row, pl.BlockSpec((2, tm, f), lambda i: (0, i, 0)), row]
    args = [h, g, wgu, wd]
    if with_loss:
        in_specs += [row, vec]
        out_shape += [jax.ShapeDtypeStruct((SUBLANES, d), F32)]
        out_specs += [pl.BlockSpec((SUBLANES, d), lambda i: (0, 0))]
        args += [tgt, gf]
    return _call(body, "ffn_fwd_loss" if with_loss else "ffn_fwd", (nt,), in_specs, out_specs, out_shape,
                 [pltpu.VMEM((2 * f, d), BF16), pltpu.VMEM((f, d), BF16), pltpu.SemaphoreType.DMA((2,))],
                 args, comm)


def _ffn_bwd(dh, h, gu, g, wgu, wd, tm, tail, tail_row, after):
    tp, d = h.shape
    f = wd.shape[0]
    fc = f // FFN_CHUNKS
    nt = tp // tm

    def body(dh_ref, h_ref, gu_ref, g_ref, tail_ref, wgu_hbm, wd_hbm, after_ref,
             dhin_ref, dgu_ref, act_ref, df_ref, dg_ref, wgu_v, wd_v, dn_v, sems):
        del after_ref
        i, j = pl.program_id(0), pl.program_id(1)

        @pl.when(jnp.logical_and(i == 0, j == 0))
        def _():
            _load_weights([(wgu_hbm, wgu_v), (wd_hbm, wd_v)], sems)
            dg_ref[...] = tail_ref[...]

        dfb = (FFN_RES * dh_ref[...]).astype(BF16)

        @pl.when(j == 0)
        def _():
            df_ref[...] = dfb
            dn_v[...] = jnp.zeros_like(dn_v)

        lo = pl.multiple_of(j * fc, 16)
        dact = _nt(dfb, wd_v[pl.ds(lo, fc), :])
        gate = gu_ref[0].astype(F32)
        up = gu_ref[1].astype(F32)
        sg = _sigmoid(gate)
        silu = gate * sg
        act_ref[...] = (silu * up).astype(BF16)
        dgate = (dact * up * (sg * (1.0 + gate * (1.0 - sg)))).astype(BF16)
        dup = (dact * silu).astype(BF16)
        dgu_ref[0] = dgate
        dgu_ref[1] = dup
        dn_v[...] += _nn(dgate, wgu_v[pl.ds(lo, fc), :]) + _nn(dup, wgu_v[pl.ds(pl.multiple_of(f + j * fc, 16), fc), :])

        @pl.when(j == FFN_CHUNKS - 1)
        def _():
            x = h_ref[...]
            r = lax.rsqrt(jnp.mean(x * x, axis=-1, keepdims=True) + EPS)
            dx, dgp = _rms_bwd(dn_v[...], x, r, g_ref[...])
            dhin_ref[...] = dh_ref[...] + dx
            dg_ref[tail_row:tail_row + 1, :] += dgp

    row = pl.BlockSpec((tm, d), lambda i, j: (i, 0))
    vec = pl.BlockSpec((1, d), lambda i, j: (0, 0))
    tile = pl.BlockSpec((SUBLANES, d), lambda i, j: (0, 0))
    hid2 = pl.BlockSpec((2, tm, fc), lambda i, j: (0, i, j))
    return _call(
        body, "ffn_bwd", (nt, FFN_CHUNKS),
        [row, row, hid2, vec, tile, _any(), _any(), _any()],
        [row, hid2, pl.BlockSpec((tm, fc), lambda i, j: (i, j)), row, tile],
        [jax.ShapeDtypeStruct((tp, d), F32), jax.ShapeDtypeStruct((2, tp, f), BF16),
         jax.ShapeDtypeStruct((tp, f), BF16), jax.ShapeDtypeStruct((tp, d), BF16),
         jax.ShapeDtypeStruct((SUBLANES, d), F32)],
        [pltpu.VMEM((2 * f, d), BF16), pltpu.VMEM((f, d), BF16), pltpu.VMEM((tm, d), F32),
         pltpu.SemaphoreType.DMA((2,))],
        [dh, h, gu, g, tail, wgu, wd, after])


def _piece_segments(q, d, nb_cols):
    segs = []
    for j in range(N_DEV):
        lo, hi = max(q * d, j * nb_cols), min((q + 1) * d, (j + 1) * nb_cols)
        if lo < hi:
            segs.append((j, lo - q * d, hi - q * d, lo - j * nb_cols, hi - j * nb_cols))
    return segs


def _w3_copies(w3_hbm, rows, w3_v):
    return [(w3_hbm.at[k, pl.ds(q * rows, rows)], w3_v.at[q, pl.ds(k * rows, rows)])
            for q in range(3) for k in range(N_DEV)]


def _gates(xrb, wg_ref, ba, bx, lam, hd):
    pre_r, pre_i = [], []
    for hh in range(N_HEADS):
        xh = xrb[:, hh * hd:(hh + 1) * hd]
        pre_r.append(_nn(xh, wg_ref[0, hh]))
        pre_i.append(_nn(xh, wg_ref[1, hh]))
    r = _sigmoid(jnp.concatenate(pre_r, axis=1) + ba)
    ig = _sigmoid(jnp.concatenate(pre_i, axis=1) + bx)
    sp = _softplus(-lam)
    log_a = -RG_LRU_C * r * sp
    a = jnp.exp(log_a)
    s = jnp.sqrt(_one_minus_exp(2.0 * log_a))
    return r, ig, sp, a, s


def _scan_fwd(a, u, h_prev):
    tm = a.shape[0]
    rows = lax.broadcasted_iota(jnp.int32, a.shape, 0)
    d = 1
    while d < tm:
        if d < SUBLANES:
            keep = rows >= d
            u = jnp.where(keep, a * pltpu.roll(u, d, 0) + u, u)
            a = jnp.where(keep, a * pltpu.roll(a, d, 0), a)
        else:
            u = jnp.concatenate([u[:d], a[d:] * u[:tm - d] + u[d:]], axis=0)
            a = jnp.concatenate([a[:d], a[d:] * a[:tm - d]], axis=0)
        d *= 2
    return u + a * h_prev


def _scan_bwd(b, v, g_next):
    tm = b.shape[0]
    rows = lax.broadcasted_iota(jnp.int32, b.shape, 0)
    d = 1
    while d < tm:
        if d < SUBLANES:
            keep = rows < tm - d
            v = jnp.where(keep, v + b * pltpu.roll(v, tm - d, 0), v)
            b = jnp.where(keep, b * pltpu.roll(b, tm - d, 0), b)
        else:
            v = jnp.concatenate([v[:tm - d] + b[:tm - d] * v[d:], v[tm - d:]], axis=0)
            b = jnp.concatenate([b[:tm - d] * b[d:], b[tm - d:]], axis=0)
        d *= 2
    return v + b * g_next


def _shifted_copies(ext_ref, es_ref, n_rows):
    for s in range(1, SUBLANES):
        es_ref[s, pl.ds(0, n_rows), :] = ext_ref[pl.ds(s, n_rows), :]


def _tap(ext_ref, es_ref, off, tm):
    q, s = divmod(off, SUBLANES)
    if s == 0:
        return ext_ref[pl.ds(SUBLANES * q, tm), :]
    return es_ref[s, pl.ds(SUBLANES * q, tm), :]


def _mixer_fwd(h, g, b_in, win_all, cw4, cb4, wg, ba, bx, lam, cw31, cb31, lng, lnb, bcp, w3_all, tm, comm=None):
    tp, d = h.shape
    nb_cols = win_all.shape[-1]
    n_in = N_DEV * nb_cols
    hd = wg.shape[-1]
    k4, k31 = cw4.shape[0], cw31.shape[0]
    w3_rows = d // N_DEV

    def body(h_ref, g_ref, b_ref, win_hbm, cw4_ref, cb4_ref, wg_ref, ba_ref, bx_ref, lam_ref, cw31_ref, cb31_ref,
             lng_ref, lnb_ref, bcp_ref, w3_hbm,
             h2_ref, p_ref, n_ref, xr_ref, hs_ref, v1_ref, ya_ref, yb_ref,
             win_v, w3_v, ext4, ext31, es31, hcar, sems):
        @pl.when(pl.program_id(0) == 0)
        def _():
            _load_weights([(win_hbm, win_v)] + _w3_copies(w3_hbm, w3_rows, w3_v), sems)
            ext4[pl.ds(0, CONV4_HALO), :] = jnp.zeros((CONV4_HALO, d), F32)
            ext31[pl.ds(0, CONV31_HALO), :] = jnp.zeros((CONV31_HALO, d), F32)
            hcar[...] = jnp.zeros_like(hcar)

        n, _ = _rms_fwd(h_ref[...], g_ref[...])
        nb = n.astype(BF16)
        n_ref[...] = nb

        def piece(q):
            parts = [_nn(nb, win_v[j, :, bl:bh]) for j, _, _, bl, bh in _piece_segments(q, d, nb_cols)]
            pq = (jnp.concatenate(parts, axis=1) + b_ref[:, q * d:(q + 1) * d]).astype(BF16)
            p_ref[:, q * d:(q + 1) * d] = pq
            return pq.astype(F32)

        x_rnn, y_rnn, glu_v, glu_g, gate_a, gate_b = [piece(q) for q in range(6)]

        ext4[pl.ds(CONV4_HALO, tm), :] = x_rnn
        xr = cb4_ref[...] + jnp.zeros((tm, d), F32)
        for k in range(k4):
            xr = xr + cw4_ref[k:k + 1, :] * ext4[pl.ds(CONV4_HALO - (k4 - 1) + k, tm), :]
        ext4[pl.ds(0, CONV4_HALO), :] = ext4[pl.ds(tm, CONV4_HALO), :]
        xrb = xr.astype(BF16)
        xr_ref[...] = xrb
        xr = xrb.astype(F32)
        _, ig, _, a, s = _gates(xrb, wg_ref, ba_ref[...], bx_ref[...], lam_ref[...], hd)
        hseq = _scan_fwd(a, s * (ig * xr), hcar[0:1, :])
        hcar[0:1, :] = hseq[tm - 1:tm, :]
        hs_ref[...] = hseq.astype(BF16)
        gl, _ = _gelu(y_rnn)
        ya = _nn((hseq * gl).astype(BF16), w3_v[0])
        ya_ref[...] = ya.astype(BF16)

        ext31[pl.ds(CONV31_HALO, tm), :] = glu_v * _sigmoid(glu_g)
        _shifted_copies(ext31, es31, tm + CONV31_HALO - SUBLANES)
        v1 = cb31_ref[...] + jnp.zeros((tm, d), F32)
        for k in range(k31):
            v1 = v1 + cw31_ref[k:k + 1, :] * _tap(ext31, es31, CONV31_HALO - (k31 - 1) + k, tm)
        ext31[pl.ds(0, CONV31_HALO), :] = ext31[pl.ds(tm, CONV31_HALO), :]
        v1b = v1.astype(BF16)
        v1_ref[...] = v1b
        v1 = v1b.astype(F32)
        xc = v1 - jnp.mean(v1, axis=-1, keepdims=True)
        rstd = lax.rsqrt(jnp.mean(xc * xc, axis=-1, keepdims=True) + EPS)
        v2 = xc * rstd * lng_ref[...] + lnb_ref[...]
        yb = _nn((v2 * _sigmoid(v2)).astype(BF16), w3_v[1]) + bcp_ref[...]
        yb_ref[...] = yb.astype(BF16)

        merged = _sigmoid(gate_a) * ya + _sigmoid(gate_b) * yb
        h2_ref[...] = h_ref[...] + _nn(merged.astype(BF16), w3_v[2])

    row = pl.BlockSpec((tm, d), lambda i: (i, 0))
    wide = pl.BlockSpec((tm, n_in), lambda i: (i, 0))
    full = lambda a: pl.BlockSpec(a.shape, lambda i, nd=a.ndim: (0,) * nd)
    smalls = [cw4, cb4, wg, ba, bx, lam, cw31, cb31, lng, lnb, bcp]
    return _call(
        body, "mixer_fwd", (tp // tm,),
        [row, full(g), full(b_in), _any()] + [full(a) for a in smalls] + [_any()],
        [row, wide] + [row] * 6,
        [jax.ShapeDtypeStruct((tp, d), F32), jax.ShapeDtypeStruct((tp, n_in), BF16)]
        + [jax.ShapeDtypeStruct((tp, d), BF16)] * 6,
        [pltpu.VMEM(win_all.shape, BF16),
         pltpu.VMEM((3, d, d), BF16),
         pltpu.VMEM((tm + CONV4_HALO, d), F32),
         pltpu.VMEM((tm + CONV31_HALO, d), F32),
         pltpu.VMEM((SUBLANES, tm + CONV31_HALO, d), F32),
         pltpu.VMEM((SUBLANES, d), F32),
         pltpu.SemaphoreType.DMA((1 + 3 * N_DEV,))],
        [h, g, b_in, win_all, *smalls, w3_all], comm)


SG_BIN, SG_CW4, SG_CB4, SG_BA, SG_BX, SG_LAM, SG_CB31, SG_LNG, SG_LNB, SG_BCP, SG_MIX, SG_CW31 = 0, 6, 10, 11, 12, 13, 14, 15, 16, 17, 18, 19


def _mixer_bwd(dh2, h, g, proj, xr_s, hs_s, v1_s, ya_s, yb_s, win_t, cw4, wg, ba, bx, lam, cw31, lng, lnb, w3_all, tm,
               comm=None):
    tp, d = dh2.shape
    n_in = proj.shape[1]
    hd = wg.shape[-1]
    k4, k31 = cw4.shape[0], cw31.shape[0]
    nt = tp // tm
    w3_rows = d // N_DEV
    sg_rows = -(-(SG_CW31 + k31) // SUBLANES) * SUBLANES
    halo_rows = 16
    per = tm // halo_rows

    def body(dh_ref, h_ref, g_ref, p_ref, xr_ref, hs_ref, hh_ref, v1_ref, ya_ref, yb_ref, win_hbm,
             cw4_ref, wg_ref, wgt_ref, ba_ref, bx_ref, lam_ref, cw31_ref, lng_ref, lnb_ref, w3_hbm,
             dh1_ref, dp_ref, x3_ref, y3_ref, yg_ref, sg_ref,
             win_v, w3_v, extd4, extd31, es31, gcar, sems):
        i = pl.program_id(0)
        tile = nt - 1 - i

        @pl.when(i == 0)
        def _():
            _load_weights([(win_hbm, win_v)] + _w3_copies(w3_hbm, w3_rows, w3_v), sems)
            for q in range(3):
                w3_v[q] = w3_v[q].T
            extd4[pl.ds(tm, CONV4_HALO), :] = jnp.zeros((CONV4_HALO, d), F32)
            extd31[pl.ds(tm, CONV31_HALO), :] = jnp.zeros((CONV31_HALO, d), F32)
            gcar[...] = jnp.zeros_like(gcar)
            sg_ref[...] = jnp.zeros_like(sg_ref)

        def acc(row, val):
            sg_ref[row:row + 1, :] += _rowsum(val)

        rows = lax.broadcasted_iota(jnp.int32, (tm, d), 0)
        x_rnn = p_ref[:, 0:d].astype(F32)
        y_rnn = p_ref[:, d:2 * d].astype(F32)
        glu_v = p_ref[:, 2 * d:3 * d].astype(F32)
        glu_g = p_ref[:, 3 * d:4 * d].astype(F32)
        sga = _sigmoid(p_ref[:, 4 * d:5 * d].astype(F32))
        sgb = _sigmoid(p_ref[:, 5 * d:6 * d].astype(F32))
        ya = ya_ref[...].astype(F32)
        yb = yb_ref[...].astype(F32)

        dmob = dh_ref[...].astype(BF16)
        dmerged = _nn(dmob, w3_v[2])
        x3_ref[:, 0:d] = (sga * ya + sgb * yb).astype(BF16)
        y3_ref[:, 0:d] = dmob
        dya = sga * dmerged
        dyb = sgb * dmerged
        dn_parts = []

        def emit(q, val):
            vb = val.astype(BF16)
            dp_ref[:, q * d:(q + 1) * d] = vb
            acc(SG_BIN + q, val)
            term = _nn(vb, win_v[pl.ds(q * d, d), :])
            dn_parts[:] = [term if not dn_parts else dn_parts[0] + term]

        emit(4, dmerged * ya * sga * (1.0 - sga))
        emit(5, dmerged * yb * sgb * (1.0 - sgb))

        dyab = dya.astype(BF16)
        y3_ref[:, d:2 * d] = dyab
        dza = _nn(dyab, w3_v[0])
        hsv = hs_ref[...].astype(F32)
        gl, th = _gelu(y_rnn)
        x3_ref[:, d:2 * d] = (hsv * gl).astype(BF16)
        emit(1, dza * hsv * _gelu_grad(y_rnn, th))
        dhs = dza * gl
        xrb = xr_ref[...]
        xr = xrb.astype(F32)
        lam_v = lam_ref[...]
        r, ig, sp, a, s = _gates(xrb, wg_ref, ba_ref[...], bx_ref[...], lam_v, hd)
        b = jnp.where(rows == tm - 1, gcar[1:2, :], pltpu.roll(a, tm - 1, 0))
        big_g = _scan_bwd(b, dhs, gcar[0:1, :])
        gcar[0:1, :] = big_g[0:1, :]
        gcar[1:2, :] = a[0:1, :]
        h_before = jnp.where(tile > 0, hh_ref[halo_rows - 1:halo_rows, :].astype(F32), 0.0)
        h_prev = jnp.where(rows == 0, h_before, pltpu.roll(hsv, 1, 0))
        ds = big_g * ig * xr
        dla = big_g * h_prev * a - ds * (a * a) / jnp.maximum(s, 1e-20)
        acc(SG_LAM, dla * r * (RG_LRU_C * _sigmoid(-lam_v)))
        dpr = dla * (-RG_LRU_C * sp) * r * (1.0 - r)
        dpi = big_g * s * xr * ig * (1.0 - ig)
        acc(SG_BA, dpr)
        acc(SG_BX, dpi)
        dprb = dpr.astype(BF16)
        dpib = dpi.astype(BF16)
        yg_ref[:, 0:d] = dprb
        yg_ref[:, d:2 * d] = dpib
        back = []
        for hh in range(N_HEADS):
            sl = slice(hh * hd, (hh + 1) * hd)
            back.append(_nn(dprb[:, sl], wgt_ref[0, hh]) + _nn(dpib[:, sl], wgt_ref[1, hh]))
        dxr = big_g * s * ig + jnp.concatenate(back, axis=1)
        acc(SG_CB4, dxr)
        extd4[pl.ds(0, tm), :] = dxr
        dx_rnn = jnp.zeros((tm, d), F32)
        for k in range(k4):
            term = extd4[pl.ds(k4 - 1 - k, tm), :]
            dx_rnn = dx_rnn + cw4_ref[k:k + 1, :] * term
            acc(SG_CW4 + k, x_rnn * term)
        extd4[pl.ds(tm, CONV4_HALO), :] = extd4[pl.ds(0, CONV4_HALO), :]
        emit(0, dx_rnn)

        dybb = dyb.astype(BF16)
        y3_ref[:, 2 * d:3 * d] = dybb
        acc(SG_BCP, dyb)
        dv3 = _nn(dybb, w3_v[1])
        v1 = v1_ref[...].astype(F32)
        xc = v1 - jnp.mean(v1, axis=-1, keepdims=True)
        rstd = lax.rsqrt(jnp.mean(xc * xc, axis=-1, keepdims=True) + EPS)
        xhat = xc * rstd
        lng_v = lng_ref[...]
        v2 = xhat * lng_v + lnb_ref[...]
        s2 = _sigmoid(v2)
        x3_ref[:, 2 * d:3 * d] = (v2 * s2).astype(BF16)
        dv2 = dv3 * (s2 * (1.0 + v2 * (1.0 - s2)))
        acc(SG_LNG, dv2 * xhat)
        acc(SG_LNB, dv2)
        dxh = dv2 * lng_v
        dv1 = rstd * (dxh - jnp.mean(dxh, axis=-1, keepdims=True)
                      - xhat * jnp.mean(dxh * xhat, axis=-1, keepdims=True))
        acc(SG_CB31, dv1)
        extd31[pl.ds(0, tm), :] = dv1
        _shifted_copies(extd31, es31, tm + CONV31_HALO - SUBLANES)
        sgg = _sigmoid(glu_g)
        v0 = glu_v * sgg
        dv0 = jnp.zeros((tm, d), F32)
        for k in range(k31):
            term = _tap(extd31, es31, k31 - 1 - k, tm)
            dv0 = dv0 + cw31_ref[k:k + 1, :] * term
            acc(SG_CW31 + k, v0 * term)
        extd31[pl.ds(tm, CONV31_HALO), :] = extd31[pl.ds(0, CONV31_HALO), :]
        emit(2, dv0 * sgg)
        emit(3, dv0 * glu_v * sgg * (1.0 - sgg))

        dn = dn_parts[0]
        x = h_ref[...]
        rr = lax.rsqrt(jnp.mean(x * x, axis=-1, keepdims=True) + EPS)
        dx, dgp = _rms_bwd(dn, x, rr, g_ref[...])
        dh1_ref[...] = dh_ref[...] + dx
        sg_ref[SG_MIX:SG_MIX + 1, :] += dgp

    rev = lambda i: (nt - 1 - i, 0)
    row = pl.BlockSpec((tm, d), rev)
    wide = pl.BlockSpec((tm, n_in), rev)
    full = lambda a: pl.BlockSpec(a.shape, lambda i, nd=a.ndim: (0,) * nd)
    halo = pl.BlockSpec((halo_rows, d), lambda i: (jnp.maximum((nt - 1 - i) * per - 1, 0), 0))
    smalls = [cw4, wg, jnp.swapaxes(wg, 2, 3), ba, bx, lam, cw31, lng, lnb]
    return _call(
        body, "mixer_bwd", (nt,),
        [row, row, full(g), wide, row, row, halo, row, row, row, _any()]
        + [full(a) for a in smalls] + [_any()],
        [row, wide, pl.BlockSpec((tm, 3 * d), rev), pl.BlockSpec((tm, 3 * d), rev),
         pl.BlockSpec((tm, 2 * d), rev), pl.BlockSpec((sg_rows, d), lambda i: (0, 0))],
        [jax.ShapeDtypeStruct((tp, d), F32), jax.ShapeDtypeStruct((tp, n_in), BF16),
         jax.ShapeDtypeStruct((tp, 3 * d), BF16), jax.ShapeDtypeStruct((tp, 3 * d), BF16),
         jax.ShapeDtypeStruct((tp, 2 * d), BF16), jax.ShapeDtypeStruct((sg_rows, d), F32)],
        [pltpu.VMEM(win_t.shape, BF16),
         pltpu.VMEM((3, d, d), BF16),
         pltpu.VMEM((tm + CONV4_HALO, d), F32),
         pltpu.VMEM((tm + CONV31_HALO, d), F32),
         pltpu.VMEM((SUBLANES, tm + CONV31_HALO, d), F32),
         pltpu.VMEM((SUBLANES, d), F32),
         pltpu.SemaphoreType.DMA((1 + 3 * N_DEV,))],
        [dh2, h, g, proj, xr_s, hs_s, hs_s, v1_s, ya_s, yb_s, win_t, *smalls, w3_all], comm)


def _tn_matmul(name, x, y, x_spec, y_spec, n_blocks, kb, nb, tm, tp, out_shape, out_spec, out_view, comm=None,
               after=None):
    nt = tp // tm

    def body(x_ref, y_ref, *refs):
        o_ref, acc = refs[-2:]
        i = pl.program_id(1)

        @pl.when(i == 0)
        def _():
            acc[...] = jnp.zeros_like(acc)

        acc[...] += _tn(x_ref[...], y_ref[...])

        @pl.when(i == nt - 1)
        def _():
            o_ref[...] = acc[...].astype(BF16).reshape(out_view)

    follows = [] if after is None else [after]
    outs, extra = _call(body, name, (n_blocks, nt), [x_spec, y_spec] + [_any()] * len(follows), [out_spec],
                        [jax.ShapeDtypeStruct(out_shape, BF16)], [pltpu.VMEM((kb, nb), F32)], [x, y] + follows,
                        comm)
    return outs[0], extra


def kernel(x, meta_tokens, ffn1_norm, ffn1_w_gu, ffn1_w_down, mix_norm, w_in, b_in, rnn_conv_w, rnn_conv_b, rg_w_a, rg_b_a, rg_w_x, rg_b_x, rg_lambda, rnn_w_proj, conv_dw_w, conv_dw_b, conv_ln_g, conv_ln_b, conv_w_proj, conv_b_proj, w_out, ffn2_norm, ffn2_w_gu, ffn2_w_down, final_norm, loss_target, m_meta_tokens, m_ffn1_norm, m_ffn1_w_gu, m_ffn1_w_down, m_mix_norm, m_w_in, m_b_in, m_rnn_conv_w, m_rnn_conv_b, m_rg_w_a, m_rg_b_a, m_rg_w_x, m_rg_b_x, m_rg_lambda, m_rnn_w_proj, m_conv_dw_w, m_conv_dw_b, m_conv_ln_g, m_conv_ln_b, m_conv_w_proj, m_conv_b_proj, m_w_out, m_ffn2_norm, m_ffn2_w_gu, m_ffn2_w_down, m_final_norm, v_meta_tokens, v_ffn1_norm, v_ffn1_w_gu, v_ffn1_w_down, v_mix_norm, v_w_in, v_b_in, v_rnn_conv_w, v_rnn_conv_b, v_rg_w_a, v_rg_b_a, v_rg_w_x, v_rg_b_x, v_rg_lambda, v_rnn_w_proj, v_conv_dw_w, v_conv_dw_b, v_conv_ln_g, v_conv_ln_b, v_conv_w_proj, v_conv_b_proj, v_w_out, v_ffn2_norm, v_ffn2_w_gu, v_ffn2_w_down, v_final_norm):
    w = dict(locals())
    seq, d = x.shape[1], x.shape[2]
    n_meta = meta_tokens.shape[0]
    t_real = n_meta + seq
    tp, tm, tmx_fwd, tmx, tmt, tmw = _tiles(t_real)
    fb = ffn1_w_gu.shape[-1]
    wr = ffn1_w_down.shape[1]
    f = N_DEV * wr
    fc = f // FFN_CHUNKS
    nbc = w_in.shape[-1]
    n_in = N_DEV * nbc
    pr = rnn_w_proj.shape[1]
    hd = rg_w_a.shape[-1]
    gr = rg_w_a.shape[2]
    cw = meta_tokens.shape[1]
    k4, k31 = rnn_conv_w.shape[1], conv_dw_w.shape[1]
    assert n_in == 6 * d and 2 * wr == fb and N_HEADS * hd == d and pr * N_DEV == d

    xi, yi, ci = lax.axis_index("x"), lax.axis_index("y"), lax.axis_index("c")
    core = ci.astype(jnp.int32).reshape(1)
    chip = (2 * xi + yi).astype(jnp.int32).reshape(1)
    me_index = (4 * xi + 2 * yi + ci).astype(jnp.int32).reshape(1)

    for nm in ("ffn1_w_gu", "ffn2_w_gu"):
        for pre in ("", "m_", "v_"):
            w[pre + nm] = jnp.swapaxes(w[pre + nm], 1, 2)

    wgut1 = w["ffn1_w_gu"][0].astype(BF16)
    wgut2 = w["ffn2_w_gu"][0].astype(BF16)
    wd1 = ffn1_w_down[0].astype(BF16)
    wd2 = ffn2_w_down[0].astype(BF16)
    win_loc = w_in[0].astype(BF16)
    win_t_loc = jnp.swapaxes(w_in[0], 0, 1).astype(BF16)
    w3_loc = jnp.concatenate([rnn_w_proj[0], conv_w_proj[0], w_out[0]], axis=0).astype(BF16)
    wg_loc = jnp.stack([rg_w_a[0], rg_w_x[0]]).astype(BF16)
    n_small = n_meta + k4 + k31
    small_rows = -(-n_small // SUBLANES) * SUBLANES
    small_loc = jnp.concatenate([meta_tokens, rnn_conv_w[0], conv_dw_w[0],
                                 jnp.zeros((small_rows - n_small, cw), F32)], axis=0)
    (wgut1_all, wd1_all, small_all), h0, tgt = _first_gather(
        [wgut1, wd1, small_loc], 2, x[0], loss_target[0], n_meta, tp)
    small_full = small_all.transpose(1, 0, 2).reshape(small_rows, d)
    cw4 = small_full[n_meta:n_meta + k4]
    cw31 = small_full[n_meta + k4:n_meta + k4 + k31]

    wgu1, wdn1 = wgut1_all.reshape(2 * f, d), wd1_all.reshape(f, d)
    (h1, gu1, n1), (win_all, w3_all, wg_all) = _ffn_fwd(
        h0, ffn1_norm, wgu1, wdn1, tm, comm=_Gather([win_loc, w3_loc, wg_loc], pass_on_at=(0.65, 0.95)))
    wg = wg_all.transpose(1, 2, 0, 3, 4).reshape(2, N_HEADS, hd, hd)
    (h2, proj, n2, xr_s, hs_s, v1_s, ya_s, yb_s), (wgut2_all, wd2_all, win_t_all) = _mixer_fwd(
        h1, mix_norm, b_in, win_all, cw4, rnn_conv_b, wg, rg_b_a, rg_b_x, rg_lambda, cw31, conv_dw_b, conv_ln_g,
        conv_ln_b, conv_b_proj, w3_all, tmx_fwd, comm=_Gather([wgut2, wd2, win_t_loc], pass_on_at=(0.45, 0.7)))
    wgu2, wdn2 = wgut2_all.reshape(2 * f, d), wd2_all.reshape(f, d)
    win_t = win_t_all.reshape(n_in, d)
    (dh3, gu2, n3, tail), _ = _ffn_fwd(
        h2, ffn2_norm, wgu2, wdn2, tm, loss=(tgt, final_norm.reshape(1, d), n_meta, t_real))

    def d_w_gu(tag, dgu, n_s, after=None):
        g, _ = _tn_matmul(
            "d_w_gu" + tag, dgu, n_s,
            pl.BlockSpec((None, tmt, fc), lambda b, i: (b // FFN_CHUNKS, i, b % FFN_CHUNKS)),
            pl.BlockSpec((tmt, d), lambda b, i: (i, 0)),
            2 * FFN_CHUNKS, fc, d, tmt, tp, (2 * FFN_CHUNKS, fc, d),
            pl.BlockSpec((None, fc, d), lambda b, i: (b, 0, 0)), (fc, d), after=after)
        return g.reshape(N_DEV, fb, d)

    def d_w_down(tag, act, df, after=None):
        g, _ = _tn_matmul(
            "d_w_down" + tag, act, df,
            pl.BlockSpec((tmt, fc), lambda b, i: (i, b)), pl.BlockSpec((tmt, d), lambda b, i: (i, 0)),
            FFN_CHUNKS, fc, d, tmt, tp, (FFN_CHUNKS, fc, d),
            pl.BlockSpec((None, fc, d), lambda b, i: (b, 0, 0)), (fc, d), after=after)
        return g.reshape(N_DEV, wr, d)

    (dh2, dgu2, act2, df2, tail), _ = _ffn_bwd(dh3, h2, gu2, ffn2_norm, wgu2, wdn2, tm, tail, TAIL_FFN2, n3)
    g_wgu2 = d_w_gu("2", dgu2, n3)
    g_wd2 = d_w_down("2", act2, df2)
    (dh1, dproj, x3, y3, yg, sg), (r_wd2, r_wgu2) = _mixer_bwd(
        dh2, h1, mix_norm, proj, xr_s, hs_s, v1_s, ya_s, yb_s, win_t, cw4, wg, rg_b_a, rg_b_x, rg_lambda, cw31,
        conv_ln_g, conv_ln_b, w3_all, tmx, comm=_Scatter([g_wd2, g_wgu2]))
    g_w3, _ = _tn_matmul(
        "d_w_proj3", x3, y3,
        pl.BlockSpec((tmw, d), lambda b, i: (i, b)), pl.BlockSpec((tmw, d), lambda b, i: (i, b)),
        3, d, d, tmw, tp, (N_DEV, 3, pr, d), pl.BlockSpec((N_DEV, None, pr, d), lambda b, i: (0, b, 0, 0)),
        (N_DEV, pr, d))
    g_wg, _ = _tn_matmul(
        "d_w_gates", xr_s, yg,
        pl.BlockSpec((tmw, hd), lambda b, i: (i, b % N_HEADS)), pl.BlockSpec((tmw, hd), lambda b, i: (i, b)),
        2 * N_HEADS, hd, hd, tmw, tp, (N_DEV, 2 * N_HEADS, gr, hd),
        pl.BlockSpec((N_DEV, None, gr, hd), lambda b, i: (0, b, 0, 0)), (N_DEV, gr, hd))
    w3_sems, g_w3_thru, w3_land, w3_token = _exchange_start("grads_proj3_exchange", _scatter_copies, N_DEV - 1, g_w3)
    g_win, (r_wg,) = _tn_matmul(
        "d_w_in", n2, dproj,
        pl.BlockSpec((tmw, d), lambda b, i: (i, 0)), pl.BlockSpec((tmw, nbc), lambda b, i: (i, b)),
        N_DEV, d, nbc, tmw, tp, (N_DEV, d, nbc), pl.BlockSpec((None, d, nbc), lambda b, i: (b, 0, 0)), (d, nbc),
        comm=_Scatter([g_wg]), after=w3_token)
    win_sems, g_win_thru, win_land, win_token = _exchange_start("grads_w_in_exchange", _scatter_copies, N_DEV - 1, g_win)
    (dh0, dgu1, act1, df1, tail), _ = _ffn_bwd(dh1, h0, gu1, ffn1_norm, wgu1, wdn1, tm, tail, TAIL_FFN1, win_token)

    pieces = [sg, dh0[:n_meta], tail]
    assert all(p.shape[0] % SUBLANES == 0 for p in pieces)
    at = [0, sg.shape[0], sg.shape[0] + n_meta]
    loss_row = at[2] + TAIL_LOSS
    rep_rows = [("ffn1_norm", at[2] + TAIL_FFN1, 1), ("mix_norm", SG_MIX, 1), ("b_in", SG_BIN, 6),
                ("rnn_conv_b", SG_CB4, 1),
                ("rg_b_a", SG_BA, 1), ("rg_b_x", SG_BX, 1), ("rg_lambda", SG_LAM, 1), ("conv_dw_b", SG_CB31, 1),
                ("conv_ln_g", SG_LNG, 1), ("conv_ln_b", SG_LNB, 1), ("conv_b_proj", SG_BCP, 1),
                ("ffn2_norm", at[2] + TAIL_FFN2, 1), ("final_norm", at[2] + TAIL_FINAL, 1)]
    col_rows = [("meta_tokens", at[1], n_meta), ("rnn_conv_w", SG_CW4, k4), ("conv_dw_w", SG_CW31, k31)]
    layout = []
    for nm, row0, nr in rep_rows:
        kind = "wide" if nm == "b_in" else "rep"
        as2d = lambda a: a.reshape(1, -1) if a.ndim == 1 else a
        layout.append((kind, row0, nr, as2d(w[nm]), as2d(w["m_" + nm]), as2d(w["v_" + nm])))
    for nm, row0, nr in col_rows:
        sq = lambda a: a.reshape(a.shape[-2], a.shape[-1])
        layout.append(("col", row0, nr, sq(w[nm]), sq(w["m_" + nm]), sq(w["v_" + nm])))
    small_partial = jnp.concatenate(pieces, axis=0)

    small_sems, small_thru, small_land, small_token = _exchange_start(
        "grads_small_exchange", _bcast_copies, N_DEV - 1, small_partial, keep_own=True)
    g_wd1 = d_w_down("1", act1, df1, after=small_token)
    wd1_sems, g_wd1_thru, wd1_land, wd1_token = _exchange_start(
        "grads_w_down1_exchange", _scatter_copies, N_DEV - 1, g_wd1)
    g_wgu1 = d_w_gu("1", dgu1, n1, after=wd1_token)

    g_last = g_wgu1.reshape((4, 2) + g_wgu1.shape[1:])
    comb_wgu1 = _pair_reduce(g_last, core)
    sems, comb_thru, land_thru, after = _exchange_start("grads_chip_exchange", _chip_copies, 3, comb_wgu1)
    g_win, r_win = _exchange_wait("grads_w_in_exchange", _scatter_copies, win_sems, g_win_thru, win_land, after)
    g_w3, r_w3 = _exchange_wait("grads_proj3_exchange", _scatter_copies, w3_sems, g_w3_thru, w3_land, after)
    g_wd1, r_wd1 = _exchange_wait("grads_w_down1_exchange", _scatter_copies, wd1_sems, g_wd1_thru, wd1_land, after)
    _, small_partials = _exchange_wait("grads_small_exchange", _bcast_copies, small_sems, small_thru, small_land, after)

    grad_x = (dh0[n_meta:t_real] + after[0, 0])[None]
    total, small_out = _small_adamw(small_partials, layout, me_index, grad_x)
    after = total

    groups = [(g_wd1, r_wd1, me_index, ["ffn1_w_down"]),
              (g_wd2, r_wd2, me_index, ["ffn2_w_down"]), (g_wgu2, r_wgu2, me_index, ["ffn2_w_gu"]),
              (g_win, r_win, me_index, ["w_in"]), (g_w3, r_w3, me_index, ["w_out", "rnn_w_proj", "conv_w_proj"]),
              (g_wg, r_wg, me_index, ["rg_w_a", "rg_w_x"]), (None, None, chip, ["ffn1_w_gu"])]
    res = {}
    for own, recv, idx, group in groups:
        if own is None:
            own, recv = _exchange_wait("grads_chip_exchange", _chip_copies, sems, comb_thru, land_thru, after)
        outs = _final_adamw(own, recv, idx, [(w[nm], w["m_" + nm], w["v_" + nm]) for nm in group], after)
        after = outs[-1][0]
        for nm, o in zip(group, outs):
            res[nm] = o
    for nm in ("ffn1_w_gu", "ffn2_w_gu"):
        res[nm] = tuple(jnp.swapaxes(a, 1, 2) for a in res[nm])
    for (nm, _, _), o in zip(rep_rows + col_rows, small_out):
        res[nm] = tuple(a.reshape(w[nm].shape) for a in o)


    order = ["meta_tokens", "ffn1_norm", "ffn1_w_gu", "ffn1_w_down", "mix_norm", "w_in", "b_in", "rnn_conv_w",
             "rnn_conv_b", "rg_w_a", "rg_b_a", "rg_w_x", "rg_b_x", "rg_lambda", "rnn_w_proj", "conv_dw_w",
             "conv_dw_b", "conv_ln_g", "conv_ln_b", "conv_w_proj", "conv_b_proj", "w_out", "ffn2_norm",
             "ffn2_w_gu", "ffn2_w_down", "final_norm"]
    return (total[loss_row, 0], grad_x, *[res[nm][0] for nm in order], *[res[nm][1] for nm in order],
            *[res[nm][2] for nm in order], *[res[nm][3] for nm in order])
```

```python
import functools
import math

import jax
import jax.numpy as jnp
from jax import lax
from jax.experimental import pallas as pl
from jax.experimental.pallas import tpu as pltpu

F32 = jnp.float32
BF16 = jnp.bfloat16
MESH = pl.DeviceIdType.MESH
N_DEV = 8
N_HEADS = 4
RG_LRU_C = 8.0
EPS = 1e-6
FFN_RES = 0.5
ADAM_LR, ADAM_B1, ADAM_B2, ADAM_EPS, ADAM_WD, ADAM_STEP = 0.001, 0.9, 0.999, 1e-08, 0.01, 10
V7X_VMEM_LIMIT = 56 * 1024 * 1024
CONV4_HALO = 8
CONV31_HALO = 32
SUBLANES = 8
STAGE_ROWS = 512
TAIL_FFN1, TAIL_FINAL, TAIL_LOSS, TAIL_FFN2 = 0, 1, 2, 3
FFN_CHUNKS = 2
FFN_FWD_CHUNKS = 1
GELU_C = math.sqrt(2.0 / math.pi)
GELU_K = 0.044715


def _any():
    return pl.BlockSpec(memory_space=pl.ANY)


def _params(n_grid):
    return pltpu.CompilerParams(dimension_semantics=("arbitrary",) * n_grid, vmem_limit_bytes=V7X_VMEM_LIMIT)


def _nn(a, b):
    return jnp.dot(a, b, preferred_element_type=F32)


def _nt(a, b):
    return lax.dot_general(a, b, (((1,), (1,)), ((), ())), preferred_element_type=F32)


def _tn(a, b):
    return lax.dot_general(a, b, (((0,), (0,)), ((), ())), preferred_element_type=F32)


def _sigmoid(x):
    return 0.5 * jnp.tanh(0.5 * x) + 0.5


def _rowsum(x):
    return jnp.sum(x, axis=0, keepdims=True)


def _rms_fwd(x, g):
    r = lax.rsqrt(jnp.mean(x * x, axis=-1, keepdims=True) + EPS)
    return x * r * g, r


def _rms_bwd(dn, x, r, g):
    xr = x * r
    gy = dn * g
    dx = r * (gy - xr * jnp.mean(gy * xr, axis=-1, keepdims=True))
    return dx, _rowsum(dn * xr)


def _gelu(y):
    t = jnp.tanh(GELU_C * (y + GELU_K * y * y * y))
    return 0.5 * y * (1.0 + t), t


def _gelu_grad(y, t):
    return 0.5 * (1.0 + t) + 0.5 * y * (1.0 - t * t) * GELU_C * (1.0 + 3.0 * GELU_K * y * y)


def _softplus(x):
    return jnp.maximum(x, 0.0) + jnp.log(1.0 + jnp.exp(-jnp.abs(x)))


def _one_minus_exp(z):
    series = -z * (1.0 + 0.5 * z * (1.0 + z * (1.0 / 3.0) * (1.0 + 0.25 * z)))
    return jnp.where(z > -0.05, series, 1.0 - jnp.exp(z))


def _tiles(t_real):
    if t_real > 2048:
        tm = 416
        tp = -(-t_real // tm) * tm
        return tp, tm, tm // 2, tm // 2, tp, tp
    tm = 128
    tp = -(-t_real // tm) * tm
    return tp, tm, tm // 2, tm // 2, tm, tm


def _load_weights(copies, sems):
    cps = [pltpu.make_async_copy(s, d, sems.at[k]) for k, (s, d) in enumerate(copies)]
    for cp in cps:
        cp.start()
    for cp in cps:
        cp.wait()


def _position():
    x, y, c = lax.axis_index("x"), lax.axis_index("y"), lax.axis_index("c")
    chips = [(1 - x, y), (x, 1 - y), (1 - x, 1 - y)]
    return x, y, c, chips


def _slot(p):
    return 4 * p[0] + 2 * p[1] + p[2]


class _Lazy(dict):
    def __getitem__(self, key):
        val = dict.__getitem__(self, key)
        return val() if callable(val) else val


class _Gather:
    def __init__(self, shards, pass_on_at=None):
        self.shards = list(shards)
        self.n = len(self.shards)
        self.pass_on_at = pass_on_at

    def inputs(self):
        return self.shards

    def out_shape(self):
        return [jax.ShapeDtypeStruct((N_DEV,) + s.shape, s.dtype) for s in self.shards]

    N_SEMS = 9

    def scratch(self):
        return [pltpu.SemaphoreType.DMA((self.N_SEMS * self.n,)), pltpu.SemaphoreType.DMA((self.N_SEMS * self.n,)),
                pltpu.SemaphoreType.DMA((self.n,))]

    def _plan(self, ins, outs, sems):
        send_sems, recv_sems, local_sems = sems
        x, y, c, _ = _position()
        me, sib, xn, yn, dg = (x, y, c), (x, y, 1 - c), (1 - x, y, c), (x, 1 - y, c), (1 - x, 1 - y, c)
        other = lambda p: (p[0], p[1], 1 - c)

        def blk(a, p, half=None):
            ref = outs[a].at[_slot(p)]
            if half is None:
                return ref
            rows = self.shards[a].shape[0] // 2
            return ref.at[pl.ds(half * rows, rows)]

        def copy(a, k, dst, to, src=None):
            return pltpu.make_async_remote_copy(
                src_ref=dst if src is None else src, dst_ref=dst,
                send_sem=send_sems.at[self.N_SEMS * a + k], recv_sem=recv_sems.at[self.N_SEMS * a + k],
                device_id=to, device_id_type=MESH)

        cp = _Lazy(mine=lambda: [pltpu.make_async_copy(ins[a], blk(a, me), local_sems.at[a]) for a in range(self.n)])
        for a in range(self.n):
            cp[a] = _Lazy(
                own=lambda a=a: [copy(a, 0, blk(a, me), sib, src=ins[a]), copy(a, 1, blk(a, me), xn, src=ins[a]),
                                 copy(a, 2, blk(a, me), yn, src=ins[a])],
                from_x=lambda a=a: copy(a, 1, blk(a, xn), me), from_y=lambda a=a: copy(a, 2, blk(a, yn), me),
                relay_x=lambda a=a: copy(a, 3, blk(a, xn, 0), yn), relay_y=lambda a=a: copy(a, 4, blk(a, yn, 1), xn),
                diag0=lambda a=a: copy(a, 3, blk(a, dg, 0), me), diag1=lambda a=a: copy(a, 4, blk(a, dg, 1), me),
                pass_x=lambda a=a: copy(a, 5, blk(a, xn), sib), pass_y=lambda a=a: copy(a, 6, blk(a, yn), sib),
                pass_d0=lambda a=a: copy(a, 7, blk(a, dg, 0), sib), pass_d1=lambda a=a: copy(a, 8, blk(a, dg, 1), sib),
                from_sib=lambda a=a: [copy(a, 0, blk(a, sib), me), copy(a, 5, blk(a, other(xn)), me),
                                      copy(a, 6, blk(a, other(yn)), me), copy(a, 7, blk(a, other(dg), 0), me),
                                      copy(a, 8, blk(a, other(dg), 1), me)])
        return cp

    def start(self, ins, outs, sems):
        cp = self._plan(ins, outs, sems)
        for c in cp["mine"]:
            c.start()
        for a in range(self.n):
            for c in cp[a]["own"]:
                c.start()

    def pass_on(self, ins, outs, sems):
        cp = self._plan(ins, outs, sems)
        for a in range(self.n):
            cp[a]["from_x"].wait_recv()
            cp[a]["relay_x"].start()
            cp[a]["pass_x"].start()
        for a in range(self.n):
            cp[a]["from_y"].wait_recv()
            cp[a]["relay_y"].start()
            cp[a]["pass_y"].start()

    def pass_on_relayed(self, ins, outs, sems):
        cp = self._plan(ins, outs, sems)
        for a in range(self.n):
            cp[a]["diag0"].wait_recv()
            cp[a]["pass_d0"].start()
            cp[a]["diag1"].wait_recv()
            cp[a]["pass_d1"].start()

    def finish(self, ins, outs, sems):
        if self.pass_on_at is None:
            self.pass_on(ins, outs, sems)
            self.pass_on_relayed(ins, outs, sems)
        cp = self._plan(ins, outs, sems)
        for a in range(self.n):
            for c in cp[a]["from_sib"]:
                c.wait_recv()
            for c in cp[a]["own"] + [cp[a][k] for k in ("relay_x", "relay_y", "pass_x", "pass_y", "pass_d0", "pass_d1")]:
                c.wait_send()
        for c in cp["mine"]:
            c.wait()


class _Scatter:
    def __init__(self, grads):
        self.grads = list(grads)
        self.n = len(self.grads)

    def inputs(self):
        return self.grads

    def out_shape(self):
        return [jax.ShapeDtypeStruct((N_DEV - 1,) + g.shape[1:], g.dtype) for g in self.grads]

    def scratch(self):
        return [pltpu.SemaphoreType.DMA((7 * self.n,)), pltpu.SemaphoreType.DMA((7 * self.n,))]

    def _plan(self, ins, outs, sems):
        send_sems, recv_sems = sems
        x, y, c, _ = _position()
        cps = []
        for a in range(self.n):
            for k in range(1, N_DEV):
                peer = (x ^ (k >> 2), y ^ ((k >> 1) & 1), c ^ (k & 1))
                cps.append(pltpu.make_async_remote_copy(
                    src_ref=ins[a].at[_slot(peer)], dst_ref=outs[a].at[k - 1],
                    send_sem=send_sems.at[7 * a + k - 1], recv_sem=recv_sems.at[7 * a + k - 1],
                    device_id=peer, device_id_type=MESH))
        return cps

    def start(self, ins, outs, sems):
        for cp in self._plan(ins, outs, sems):
            cp.start()

    def finish(self, ins, outs, sems):
        for cp in self._plan(ins, outs, sems):
            cp.wait()


def _hosted(inner, n_in, n_out, comm, grid):
    if comm is None:
        return inner
    nc_in, nc_out, ns = len(comm.inputs()), len(comm.out_shape()), len(comm.scratch())

    def body(*refs):
        o0 = n_in + nc_in
        s0 = o0 + n_out + nc_out
        main = refs[:n_in] + refs[o0:o0 + n_out] + refs[s0:len(refs) - ns]
        c_in, c_out, c_sems = refs[n_in:o0], refs[o0 + n_out:s0], refs[len(refs) - ns:]
        ids = [pl.program_id(ax) for ax in range(len(grid))]
        first = functools.reduce(jnp.logical_and, [i == 0 for i in ids])
        last = functools.reduce(jnp.logical_and, [i == g - 1 for i, g in zip(ids, grid)])

        @pl.when(first)
        def _():
            comm.start(c_in, c_out, c_sems)

        inner(*main)

        if getattr(comm, "pass_on_at", None) is not None:
            assert len(grid) == 1
            first_at, second_at = (min(grid[0] - 1, int(frac * grid[0])) for frac in comm.pass_on_at)
            assert first_at < second_at

            @pl.when(ids[0] == first_at)
            def _():
                comm.pass_on(c_in, c_out, c_sems)

            @pl.when(ids[0] == second_at)
            def _():
                comm.pass_on_relayed(c_in, c_out, c_sems)

        @pl.when(last)
        def _():
            comm.finish(c_in, c_out, c_sems)

    return body


def _call(inner, name, grid, in_specs, out_specs, out_shape, scratch, args, comm=None):
    n_in, n_out = len(args), len(out_shape)
    body = _hosted(inner, n_in, n_out, comm, grid)
    if comm is not None:
        in_specs = list(in_specs) + [_any()] * len(comm.inputs())
        args = list(args) + comm.inputs()
        out_specs = list(out_specs) + [_any()] * len(comm.out_shape())
        out_shape = list(out_shape) + comm.out_shape()
        scratch = list(scratch) + comm.scratch()
    outs = pl.pallas_call(
        body, name=name, grid=grid, in_specs=list(in_specs), out_specs=list(out_specs), out_shape=list(out_shape),
        scratch_shapes=list(scratch), compiler_params=_params(len(grid)))(*args)
    return list(outs[:n_out]), list(outs[n_out:])


class _Bcast:
    def __init__(self, block):
        self.block = block

    def inputs(self):
        return [self.block]

    def out_shape(self):
        return [jax.ShapeDtypeStruct((N_DEV,) + self.block.shape, self.block.dtype)]

    def scratch(self):
        return [pltpu.SemaphoreType.DMA((N_DEV - 1,)), pltpu.SemaphoreType.DMA((N_DEV - 1,)),
                pltpu.SemaphoreType.DMA((1,))]

    def _plan(self, ins, outs, sems):
        send_sems, recv_sems, local_sem = sems
        x, y, c, _ = _position()
        mine = outs[0].at[_slot((x, y, c))]
        cps = []
        for k in range(1, N_DEV):
            peer = (x ^ (k >> 2), y ^ ((k >> 1) & 1), c ^ (k & 1))
            cps.append(pltpu.make_async_remote_copy(
                src_ref=ins[0], dst_ref=mine, send_sem=send_sems.at[k - 1], recv_sem=recv_sems.at[k - 1],
                device_id=peer, device_id_type=MESH))
        return pltpu.make_async_copy(ins[0], mine, local_sem.at[0]), cps

    def start(self, ins, outs, sems):
        own, cps = self._plan(ins, outs, sems)
        own.start()
        for cp in cps:
            cp.start()

    def finish(self, ins, outs, sems):
        own, cps = self._plan(ins, outs, sems)
        for cp in cps:
            cp.wait()
        own.wait()


def _first_gather(shards, small_idx, x2, t2, n_meta, tp):
    comm = _Gather(shards)
    n = comm.n
    seq, d = x2.shape
    t_real = n_meta + seq
    n_pad = tp - t_real
    cw = d // N_DEV
    rows = STAGE_ROWS if seq % STAGE_ROWS == 0 else seq
    n_chunks = seq // rows

    def body(*refs):
        ins, (x_ref, t_ref) = refs[:n], refs[n:n + 2]
        outs, (h0_ref, tg_ref) = refs[n + 2:2 * n + 2], refs[2 * n + 2:2 * n + 4]
        sems = refs[2 * n + 4:2 * n + 7]
        buf, zeros, in_sems, out_sems, misc_sems = refs[2 * n + 7:]
        comm.start(ins, outs, sems)
        zeros[...] = jnp.zeros_like(zeros)
        fills = [pltpu.make_async_copy(zeros.at[pl.ds(0, n_pad)], h0_ref.at[pl.ds(t_real, n_pad)], misc_sems.at[0]),
                 pltpu.make_async_copy(zeros.at[pl.ds(0, n_pad)], tg_ref.at[pl.ds(t_real, n_pad)], misc_sems.at[1]),
                 pltpu.make_async_copy(zeros.at[pl.ds(0, n_meta)], tg_ref.at[pl.ds(0, n_meta)], misc_sems.at[2])]
        for cp in fills:
            cp.start()
        jobs = [(src, dst, c) for src, dst in ((x_ref, h0_ref), (t_ref, tg_ref)) for c in range(n_chunks)]

        def load(k):
            src, _, c = jobs[k]
            return pltpu.make_async_copy(src.at[pl.ds(c * rows, rows)], buf.at[k % 2], in_sems.at[k % 2])

        def store(k):
            _, dst, c = jobs[k]
            return pltpu.make_async_copy(buf.at[k % 2], dst.at[pl.ds(n_meta + c * rows, rows)], out_sems.at[k % 2])

        load(0).start()
        for k in range(len(jobs)):
            load(k).wait()
            if k + 1 < len(jobs):
                if k >= 1:
                    store(k - 1).wait()
                load(k + 1).start()
            store(k).start()
        for k in range(max(0, len(jobs) - 2), len(jobs)):
            store(k).wait()
        comm.finish(ins, outs, sems)
        meta = [pltpu.make_async_copy(outs[small_idx].at[k, pl.ds(0, n_meta)],
                                      h0_ref.at[pl.ds(0, n_meta), pl.ds(k * cw, cw)], misc_sems.at[3 + k])
                for k in range(N_DEV)]
        for cp in meta:
            cp.start()
        for cp in fills + meta:
            cp.wait()

    staged = [jax.ShapeDtypeStruct((tp, d), F32)] * 2
    outs = pl.pallas_call(
        body, name="weights_all_gather", out_shape=comm.out_shape() + staged,
        in_specs=[_any()] * (n + 2), out_specs=[_any()] * (n + 2),
        scratch_shapes=comm.scratch() + [
            pltpu.VMEM((2, rows, d), F32), pltpu.VMEM((max(n_pad, n_meta), d), F32),
            pltpu.SemaphoreType.DMA((2,)), pltpu.SemaphoreType.DMA((2,)), pltpu.SemaphoreType.DMA((3 + N_DEV,))],
        compiler_params=pltpu.CompilerParams(vmem_limit_bytes=V7X_VMEM_LIMIT),
    )(*shards, x2, t2)
    return outs[:n], outs[n], outs[n + 1]


def _chip_copies(c_ref, land_ref, sems):
    _, _, c, chips = _position()
    return [pltpu.make_async_remote_copy(
        src_ref=c_ref.at[2 * cx + cy], dst_ref=land_ref.at[j], send_sem=sems[j], recv_sem=sems[3 + j],
        device_id=(cx, cy, c), device_id_type=MESH) for j, (cx, cy) in enumerate(chips)]


def _scatter_copies(g_ref, land_ref, sems):
    x, y, c, _ = _position()
    cps = []
    for k in range(1, N_DEV):
        peer = (x ^ (k >> 2), y ^ ((k >> 1) & 1), c ^ (k & 1))
        cps.append(pltpu.make_async_remote_copy(
            src_ref=g_ref.at[_slot(peer)], dst_ref=land_ref.at[k - 1], send_sem=sems[k - 1],
            recv_sem=sems[N_DEV - 1 + k - 1], device_id=peer, device_id_type=MESH))
    return cps


def _exchange_start(name, copies, n_copies, src):
    hbm = pl.BlockSpec(memory_space=pltpu.HBM)
    sem = pl.BlockSpec(memory_space=pltpu.SEMAPHORE)
    n_sems = 2 * n_copies

    def body(s_ref, land_ref, *refs):
        for cp in copies(s_ref, land_ref, refs[:n_sems]):
            cp.start()
        token = refs[n_sems + 2]
        token[...] = jnp.zeros_like(token)

    land = lax.empty((n_copies,) + src.shape[1:], src.dtype)
    outs = pl.pallas_call(
        body, name=name + "_start",
        out_shape=(pltpu.SemaphoreType.DMA(()),) * n_sems
        + (pltpu.HBM(src.shape, src.dtype), pltpu.HBM(land.shape, land.dtype),
           jax.ShapeDtypeStruct((SUBLANES, 128), F32)),
        in_specs=(hbm, hbm), out_specs=(sem,) * n_sems + (hbm, hbm, pl.BlockSpec(memory_space=pltpu.VMEM)),
        input_output_aliases={0: n_sems, 1: n_sems + 1},
        compiler_params=pltpu.CompilerParams(has_side_effects=pltpu.SideEffectType.DATAFLOW_SIDE_EFFECTING),
    )(pltpu.with_memory_space_constraint(src, pltpu.HBM), pltpu.with_memory_space_constraint(land, pltpu.HBM))
    return outs[:n_sems], outs[n_sems], outs[n_sems + 1], outs[n_sems + 2]


def _exchange_wait(name, copies, sems, src_thru, land_thru, after):
    hbm = pl.BlockSpec(memory_space=pltpu.HBM)
    sem = pl.BlockSpec(memory_space=pltpu.SEMAPHORE)
    n_sems = len(sems)

    def body(s_ref, land_ref, *refs):
        for cp in copies(s_ref, land_ref, refs[:n_sems]):
            cp.wait_send()
            cp.wait_recv()

    return pl.pallas_call(
        body, name=name + "_wait",
        out_shape=(pltpu.HBM(src_thru.shape, src_thru.dtype), pltpu.HBM(land_thru.shape, land_thru.dtype)),
        in_specs=(hbm, hbm) + (sem,) * n_sems + (pl.BlockSpec(memory_space=pl.ANY),), out_specs=(hbm, hbm),
        input_output_aliases={0: 0, 1: 1},
        compiler_params=pltpu.CompilerParams(has_side_effects=pltpu.SideEffectType.DATAFLOW_SIDE_EFFECTING),
    )(src_thru, land_thru, *sems, after)


def _pair_reduce(grad, core):
    blk = grad.shape[2:]
    zeros = (0,) * len(blk)

    def body(core_ref, g_hbm, own_ref, o_ref, landed, send_sems, recv_sems):
        del core_ref
        i = pl.program_id(0)
        x, y, c, _ = _position()

        def copy(k):
            return pltpu.make_async_remote_copy(
                src_ref=g_hbm.at[k, 1 - c], dst_ref=landed.at[k], send_sem=send_sems.at[k],
                recv_sem=recv_sems.at[k], device_id=(x, y, 1 - c), device_id_type=MESH)

        @pl.when(i == 0)
        def _():
            for k in range(4):
                copy(k).start()

        for k in range(4):
            @pl.when(i == k)
            def _(k=k):
                copy(k).wait_recv()

        o_ref[...] = (own_ref[...].astype(F32) + landed[i].astype(F32)).astype(BF16)

        @pl.when(i == 3)
        def _():
            for k in range(4):
                copy(k).wait_send()

    return pl.pallas_call(
        body, name="grads_pair_reduce",
        out_shape=jax.ShapeDtypeStruct((4,) + blk, BF16),
        grid_spec=pltpu.PrefetchScalarGridSpec(
            num_scalar_prefetch=1, grid=(4,),
            in_specs=[_any(), pl.BlockSpec((None, None) + blk, lambda i, cr: (i, cr[0]) + zeros)],
            out_specs=pl.BlockSpec((None,) + blk, lambda i, cr: (i,) + zeros),
            scratch_shapes=[pltpu.VMEM((4,) + blk, BF16), pltpu.SemaphoreType.DMA((4,)),
                            pltpu.SemaphoreType.DMA((4,))]),
        compiler_params=_params(1),
    )(core, grad, grad)


def _adamw(w, g, m, v):
    m2 = ADAM_B1 * m + (1.0 - ADAM_B1) * g
    v2 = ADAM_B2 * v + (1.0 - ADAM_B2) * (g * g)
    m_hat = m2 / (1.0 - ADAM_B1 ** ADAM_STEP)
    v_hat = v2 / (1.0 - ADAM_B2 ** ADAM_STEP)
    delta = -ADAM_LR * (m_hat / (jnp.sqrt(v_hat) + ADAM_EPS) + ADAM_WD * w)
    return delta, m2, v2


def _final_adamw(own, recv, idx, parts, after):
    blk = own.shape[1:]
    n_recv = recv.shape[0]
    n_parts = len(parts)
    per = blk[0] // n_parts if n_parts > 1 else None
    rows = blk[-2]
    n_chunks = 1 if n_parts > 1 else (4 if rows % 64 == 0 and rows >= 512 else (2 if rows % 32 == 0 else 1))
    cblk = blk[:-2] + (rows // n_chunks, blk[-1])
    lead = (0,) * (len(blk) - 2)

    def body(idx_ref, c_ref, r_ref, after_ref, *refs):
        del idx_ref, after_ref
        ins, outs = refs[:3 * n_parts], refs[3 * n_parts:]
        g = c_ref[...].astype(F32)
        for k in range(n_recv):
            g = g + r_ref[k].astype(F32)
        for p in range(n_parts):
            w_ref, m_ref, v_ref = ins[3 * p:3 * p + 3]
            if n_parts == 1:
                gp = g
            elif per == 1:
                gp = g[p]
            else:
                gp = g[p * per:(p + 1) * per]
            delta, m2, v2 = _adamw(w_ref[0], gp, m_ref[0], v_ref[0])
            o = outs[4 * p:4 * p + 4]
            o[0][0] = gp
            o[1][0] = delta
            o[2][0] = m2
            o[3][0] = v2

    flat = [a for wmv in parts for a in wmv]

    def part_spec(a):
        shape = a.shape[:-2] + (a.shape[-2] // n_chunks, a.shape[-1])
        return pl.BlockSpec(shape, lambda i, cr, nd=a.ndim: (0,) * (nd - 2) + (i, 0))

    outs = pl.pallas_call(
        body, name="grads_sum_adamw",
        out_shape=[jax.ShapeDtypeStruct(wmv[0].shape, F32) for wmv in parts for _ in range(4)],
        grid_spec=pltpu.PrefetchScalarGridSpec(
            num_scalar_prefetch=1, grid=(n_chunks,),
            in_specs=[pl.BlockSpec((None,) + cblk, lambda i, cr: (cr[0],) + lead + (i, 0)),
                      pl.BlockSpec((n_recv,) + cblk, lambda i, cr: (0,) + lead + (i, 0))]
                     + [_any()] + [part_spec(a) for a in flat],
            out_specs=[part_spec(wmv[0]) for wmv in parts for _ in range(4)]),
        compiler_params=_params(1),
    )(idx, own, recv, after, *flat)
    return [tuple(outs[4 * p:4 * p + 4]) for p in range(n_parts)]


def _small_adamw(partials, layout, me_index, after):
    _, rows, d = partials.shape
    n = len(layout)
    cw = d // N_DEV

    def body(me_ref, p_ref, after_ref, *refs):
        ins, t_ref, outs = refs[:3 * n], refs[3 * n], refs[3 * n + 1:]
        me = me_ref[0]
        total = p_ref[0]
        for j in range(1, N_DEV):
            total = total + p_ref[j]
        t_ref[...] = total
        for e, (kind, r0, nr, _, _, _) in enumerate(layout):
            w_ref, m_ref, v_ref = ins[3 * e:3 * e + 3]
            o = outs[4 * e:4 * e + 4]
            if kind == "rep":
                g = t_ref[r0:r0 + nr, :]
                delta, m2, v2 = _adamw(w_ref[...], g, m_ref[...], v_ref[...])
                for ref, val in zip(o, (g, delta, m2, v2)):
                    ref[...] = val
            elif kind == "wide":
                for q in range(nr):
                    sl = slice(q * d, (q + 1) * d)
                    g = t_ref[r0 + q:r0 + q + 1, :]
                    delta, m2, v2 = _adamw(w_ref[:, sl], g, m_ref[:, sl], v_ref[:, sl])
                    for ref, val in zip(o, (g, delta, m2, v2)):
                        ref[:, sl] = val
            else:
                for j in range(N_DEV):
                    @pl.when(me == j)
                    def _(j=j, o=o, w_ref=w_ref, m_ref=m_ref, v_ref=v_ref, r0=r0, nr=nr):
                        g = t_ref[r0:r0 + nr, j * cw:(j + 1) * cw]
                        delta, m2, v2 = _adamw(w_ref[...], g, m_ref[...], v_ref[...])
                        for ref, val in zip(o, (g, delta, m2, v2)):
                            ref[...] = val

    flat = [a for ent in layout for a in ent[3:]]
    vm = pl.BlockSpec(memory_space=pltpu.VMEM)
    outs = pl.pallas_call(
        body, name="small_adamw",
        out_shape=[jax.ShapeDtypeStruct((rows, d), F32)]
                  + [jax.ShapeDtypeStruct(ent[3].shape, F32) for ent in layout for _ in range(4)],
        in_specs=[pl.BlockSpec(memory_space=pltpu.SMEM), vm, _any()] + [vm] * len(flat),
        out_specs=[vm] * (1 + 4 * n),
        compiler_params=pltpu.CompilerParams(vmem_limit_bytes=V7X_VMEM_LIMIT),
    )(me_index, partials, after, *flat)
    return outs[0], [tuple(outs[1 + 4 * e:5 + 4 * e]) for e in range(n)]


def _ffn_fwd(h, g, wgu, wd, tm, loss=None, comm=None):
    tp, d = h.shape
    f = wd.shape[0]
    fc = f // FFN_FWD_CHUNKS
    nt = tp // tm
    with_loss = loss is not None
    if with_loss:
        tgt, gf, n_meta, t_real = loss

    def body(*refs):
        if with_loss:
            (h_ref, g_ref, wgu_hbm, wd_hbm, tgt_ref, gf_ref, out_ref, gu_ref, n_ref, tail_ref,
             wgu_v, wd_v, sems) = refs
        else:
            h_ref, g_ref, wgu_hbm, wd_hbm, out_ref, gu_ref, n_ref, wgu_v, wd_v, sems = refs
        i = pl.program_id(0)

        @pl.when(i == 0)
        def _():
            _load_weights([(wgu_hbm, wgu_v), (wd_hbm, wd_v)], sems)
            if with_loss:
                tail_ref[...] = jnp.zeros_like(tail_ref)

        x = h_ref[...]
        n, _ = _rms_fwd(x, g_ref[...])
        nb = n.astype(BF16)
        n_ref[...] = nb
        acc = jnp.zeros((tm, d), F32)
        for j in range(FFN_FWD_CHUNKS):
            cols = slice(j * fc, (j + 1) * fc)
            gate = _nt(nb, wgu_v[pl.ds(j * fc, fc), :])
            up = _nt(nb, wgu_v[pl.ds(f + j * fc, fc), :])
            gu_ref[0, :, cols] = gate.astype(BF16)
            gu_ref[1, :, cols] = up.astype(BF16)
            act = (gate * _sigmoid(gate) * up).astype(BF16)
            acc = acc + _nn(act, wd_v[pl.ds(j * fc, fc), :])
        hn = x + FFN_RES * acc
        if not with_loss:
            out_ref[...] = hn
        else:
            gfv = gf_ref[...]
            r = lax.rsqrt(jnp.mean(hn * hn, axis=-1, keepdims=True) + EPS)
            xr = hn * r
            rows = i * tm + lax.broadcasted_iota(jnp.int32, (tm, 1), 0)
            mask = jnp.logical_and(rows >= n_meta, rows < t_real)
            diff = jnp.where(mask, xr * gfv - tgt_ref[...], 0.0)
            tail_ref[TAIL_LOSS:TAIL_LOSS + 1, :] += jnp.zeros((1, d), F32) + 0.5 * jnp.sum(diff * diff) / d
            dy = diff / d
            gy = dy * gfv
            out_ref[...] = r * (gy - xr * jnp.mean(gy * xr, axis=-1, keepdims=True))
            tail_ref[TAIL_FINAL:TAIL_FINAL + 1, :] += _rowsum(dy * xr)

    row = pl.BlockSpec((tm, d), lambda i: (i, 0))
    vec = pl.BlockSpec((1, d), lambda i: (0, 0))
    in_specs = [row, vec, _any(), _any()]
    out_shape = [jax.ShapeDtypeStruct((tp, d), F32), jax.ShapeDtypeStruct((2, tp, f), BF16),
                 jax.ShapeDtypeStruct((tp, d), BF16)]
    out_specs = [row, pl.BlockSpec((2, tm, f), lambda i: (0, i, 0)), row]
    args = [h, g, wgu, wd]
    if with_loss:
        in_specs += [row, vec]
        out_shape += [jax.ShapeDtypeStruct((SUBLANES, d), F32)]
        out_specs += [pl.BlockSpec((SUBLANES, d), lambda i: (0, 0))]
        args += [tgt, gf]
    return _call(body, "ffn_fwd_loss" if with_loss else "ffn_fwd", (nt,), in_specs, out_specs, out_shape,
                 [pltpu.VMEM((2 * f, d), BF16), pltpu.VMEM((f, d), BF16), pltpu.SemaphoreType.DMA((2,))],
                 args, comm)


def _ffn_bwd(dh, h, gu, g, wgu, wd, tm, tail, tail_row, after):
    tp, d = h.shape
    f = wd.shape[0]
    fc = f // FFN_CHUNKS
    nt = tp // tm

    def body(dh_ref, h_ref, gu_ref, g_ref, tail_ref, wgu_hbm, wd_hbm, after_ref,
             dhin_ref, dgu_ref, act_ref, df_ref, dg_ref, wgu_v, wd_v, dn_v, sems):
        del after_ref
        i, j = pl.program_id(0), pl.program_id(1)

        @pl.when(jnp.logical_and(i == 0, j == 0))
        def _():
            _load_weights([(wgu_hbm, wgu_v), (wd_hbm, wd_v)], sems)
            dg_ref[...] = tail_ref[...]

        dfb = (FFN_RES * dh_ref[...]).astype(BF16)

        @pl.when(j == 0)
        def _():
            df_ref[...] = dfb
            dn_v[...] = jnp.zeros_like(dn_v)

        lo = pl.multiple_of(j * fc, 16)
        dact = _nt(dfb, wd_v[pl.ds(lo, fc), :])
        gate = gu_ref[0].astype(F32)
        up = gu_ref[1].astype(F32)
        sg = _sigmoid(gate)
        silu = gate * sg
        act_ref[...] = (silu * up).astype(BF16)
        dgate = (dact * up * (sg * (1.0 + gate * (1.0 - sg)))).astype(BF16)
        dup = (dact * silu).astype(BF16)
        dgu_ref[0] = dgate
        dgu_ref[1] = dup
        dn_v[...] += _nn(dgate, wgu_v[pl.ds(lo, fc), :]) + _nn(dup, wgu_v[pl.ds(pl.multiple_of(f + j * fc, 16), fc), :])

        @pl.when(j == FFN_CHUNKS - 1)
        def _():
            x = h_ref[...]
            r = lax.rsqrt(jnp.mean(x * x, axis=-1, keepdims=True) + EPS)
            dx, dgp = _rms_bwd(dn_v[...], x, r, g_ref[...])
            dhin_ref[...] = dh_ref[...] + dx
            dg_ref[tail_row:tail_row + 1, :] += dgp

    row = pl.BlockSpec((tm, d), lambda i, j: (i, 0))
    vec = pl.BlockSpec((1, d), lambda i, j: (0, 0))
    tile = pl.BlockSpec((SUBLANES, d), lambda i, j: (0, 0))
    hid2 = pl.BlockSpec((2, tm, fc), lambda i, j: (0, i, j))
    return _call(
        body, "ffn_bwd", (nt, FFN_CHUNKS),
        [row, row, hid2, vec, tile, _any(), _any(), _any()],
        [row, hid2, pl.BlockSpec((tm, fc), lambda i, j: (i, j)), row, tile],
        [jax.ShapeDtypeStruct((tp, d), F32), jax.ShapeDtypeStruct((2, tp, f), BF16),
         jax.ShapeDtypeStruct((tp, f), BF16), jax.ShapeDtypeStruct((tp, d), BF16),
         jax.ShapeDtypeStruct((SUBLANES, d), F32)],
        [pltpu.VMEM((2 * f, d), BF16), pltpu.VMEM((f, d), BF16), pltpu.VMEM((tm, d), F32),
         pltpu.SemaphoreType.DMA((2,))],
        [dh, h, gu, g, tail, wgu, wd, after])


def _piece_segments(q, d, nb_cols):
    segs = []
    for j in range(N_DEV):
        lo, hi = max(q * d, j * nb_cols), min((q + 1) * d, (j + 1) * nb_cols)
        if lo < hi:
            segs.append((j, lo - q * d, hi - q * d, lo - j * nb_cols, hi - j * nb_cols))
    return segs


def _w3_copies(w3_hbm, rows, w3_v):
    return [(w3_hbm.at[k, pl.ds(q * rows, rows)], w3_v.at[q, pl.ds(k * rows, rows)])
            for q in range(3) for k in range(N_DEV)]


def _gates(xrb, wg_ref, ba, bx, lam, hd):
    pre_r, pre_i = [], []
    for hh in range(N_HEADS):
        xh = xrb[:, hh * hd:(hh + 1) * hd]
        pre_r.append(_nn(xh, wg_ref[0, hh]))
        pre_i.append(_nn(xh, wg_ref[1, hh]))
    r = _sigmoid(jnp.concatenate(pre_r, axis=1) + ba)
    ig = _sigmoid(jnp.concatenate(pre_i, axis=1) + bx)
    sp = _softplus(-lam)
    log_a = -RG_LRU_C * r * sp
    a = jnp.exp(log_a)
    s = jnp.sqrt(_one_minus_exp(2.0 * log_a))
    return r, ig, sp, a, s


def _scan_fwd(a, u, h_prev):
    tm = a.shape[0]
    rows = lax.broadcasted_iota(jnp.int32, a.shape, 0)
    d = 1
    while d < tm:
        if d < SUBLANES:
            keep = rows >= d
            u = jnp.where(keep, a * pltpu.roll(u, d, 0) + u, u)
            a = jnp.where(keep, a * pltpu.roll(a, d, 0), a)
        else:
            u = jnp.concatenate([u[:d], a[d:] * u[:tm - d] + u[d:]], axis=0)
            a = jnp.concatenate([a[:d], a[d:] * a[:tm - d]], axis=0)
        d *= 2
    return u + a * h_prev


def _scan_bwd(b, v, g_next):
    tm = b.shape[0]
    rows = lax.broadcasted_iota(jnp.int32, b.shape, 0)
    d = 1
    while d < tm:
        if d < SUBLANES:
            keep = rows < tm - d
            v = jnp.where(keep, v + b * pltpu.roll(v, tm - d, 0), v)
            b = jnp.where(keep, b * pltpu.roll(b, tm - d, 0), b)
        else:
            v = jnp.concatenate([v[:tm - d] + b[:tm - d] * v[d:], v[tm - d:]], axis=0)
            b = jnp.concatenate([b[:tm - d] * b[d:], b[tm - d:]], axis=0)
        d *= 2
    return v + b * g_next


def _shifted_copies(ext_ref, es_ref, n_rows):
    for s in range(1, SUBLANES):
        es_ref[s, pl.ds(0, n_rows), :] = ext_ref[pl.ds(s, n_rows), :]


def _tap(ext_ref, es_ref, off, tm):
    q, s = divmod(off, SUBLANES)
    if s == 0:
        return ext_ref[pl.ds(SUBLANES * q, tm), :]
    return es_ref[s, pl.ds(SUBLANES * q, tm), :]


def _mixer_fwd(h, g, b_in, win_all, cw4, cb4, wg, ba, bx, lam, cw31, cb31, lng, lnb, bcp, w3_all, tm, comm=None):
    tp, d = h.shape
    nb_cols = win_all.shape[-1]
    n_in = N_DEV * nb_cols
    hd = wg.shape[-1]
    k4, k31 = cw4.shape[0], cw31.shape[0]
    w3_rows = d // N_DEV

    def body(h_ref, g_ref, b_ref, win_hbm, cw4_ref, cb4_ref, wg_ref, ba_ref, bx_ref, lam_ref, cw31_ref, cb31_ref,
             lng_ref, lnb_ref, bcp_ref, w3_hbm,
             h2_ref, p_ref, n_ref, xr_ref, hs_ref, v1_ref, ya_ref, yb_ref,
             win_v, w3_v, ext4, ext31, es31, hcar, sems):
        @pl.when(pl.program_id(0) == 0)
        def _():
            _load_weights([(win_hbm, win_v)] + _w3_copies(w3_hbm, w3_rows, w3_v), sems)
            ext4[pl.ds(0, CONV4_HALO), :] = jnp.zeros((CONV4_HALO, d), F32)
            ext31[pl.ds(0, CONV31_HALO), :] = jnp.zeros((CONV31_HALO, d), F32)
            hcar[...] = jnp.zeros_like(hcar)

        n, _ = _rms_fwd(h_ref[...], g_ref[...])
        nb = n.astype(BF16)
        n_ref[...] = nb

        def piece(q):
            parts = [_nn(nb, win_v[j, :, bl:bh]) for j, _, _, bl, bh in _piece_segments(q, d, nb_cols)]
            pq = (jnp.concatenate(parts, axis=1) + b_ref[:, q * d:(q + 1) * d]).astype(BF16)
            p_ref[:, q * d:(q + 1) * d] = pq
            return pq.astype(F32)

        x_rnn, y_rnn, glu_v, glu_g, gate_a, gate_b = [piece(q) for q in range(6)]

        ext4[pl.ds(CONV4_HALO, tm), :] = x_rnn
        xr = cb4_ref[...] + jnp.zeros((tm, d), F32)
        for k in range(k4):
            xr = xr + cw4_ref[k:k + 1, :] * ext4[pl.ds(CONV4_HALO - (k4 - 1) + k, tm), :]
        ext4[pl.ds(0, CONV4_HALO), :] = ext4[pl.ds(tm, CONV4_HALO), :]
        xrb = xr.astype(BF16)
        xr_ref[...] = xrb
        xr = xrb.astype(F32)
        _, ig, _, a, s = _gates(xrb, wg_ref, ba_ref[...], bx_ref[...], lam_ref[...], hd)
        hseq = _scan_fwd(a, s * (ig * xr), hcar[0:1, :])
        hcar[0:1, :] = hseq[tm - 1:tm, :]
        hs_ref[...] = hseq.astype(BF16)
        gl, _ = _gelu(y_rnn)
        ya = _nn((hseq * gl).astype(BF16), w3_v[0])
        ya_ref[...] = ya.astype(BF16)

        ext31[pl.ds(CONV31_HALO, tm), :] = glu_v * _sigmoid(glu_g)
        _shifted_copies(ext31, es31, tm + CONV31_HALO - SUBLANES)
        v1 = cb31_ref[...] + jnp.zeros((tm, d), F32)
        for k in range(k31):
            v1 = v1 + cw31_ref[k:k + 1, :] * _tap(ext31, es31, CONV31_HALO - (k31 - 1) + k, tm)
        ext31[pl.ds(0, CONV31_HALO), :] = ext31[pl.ds(tm, CONV31_HALO), :]
        v1b = v1.astype(BF16)
        v1_ref[...] = v1b
        v1 = v1b.astype(F32)
        xc = v1 - jnp.mean(v1, axis=-1, keepdims=True)
        rstd = lax.rsqrt(jnp.mean(xc * xc, axis=-1, keepdims=True) + EPS)
        v2 = xc * rstd * lng_ref[...] + lnb_ref[...]
        yb = _nn((v2 * _sigmoid(v2)).astype(BF16), w3_v[1]) + bcp_ref[...]
        yb_ref[...] = yb.astype(BF16)

        merged = _sigmoid(gate_a) * ya + _sigmoid(gate_b) * yb
        h2_ref[...] = h_ref[...] + _nn(merged.astype(BF16), w3_v[2])

    row = pl.BlockSpec((tm, d), lambda i: (i, 0))
    wide = pl.BlockSpec((tm, n_in), lambda i: (i, 0))
    full = lambda a: pl.BlockSpec(a.shape, lambda i, nd=a.ndim: (0,) * nd)
    smalls = [cw4, cb4, wg, ba, bx, lam, cw31, cb31, lng, lnb, bcp]
    return _call(
        body, "mixer_fwd", (tp // tm,),
        [row, full(g), full(b_in), _any()] + [full(a) for a in smalls] + [_any()],
        [row, wide] + [row] * 6,
        [jax.ShapeDtypeStruct((tp, d), F32), jax.ShapeDtypeStruct((tp, n_in), BF16)]
        + [jax.ShapeDtypeStruct((tp, d), BF16)] * 6,
        [pltpu.VMEM(win_all.shape, BF16),
         pltpu.VMEM((3, d, d), BF16),
         pltpu.VMEM((tm + CONV4_HALO, d), F32),
         pltpu.VMEM((tm + CONV31_HALO, d), F32),
         pltpu.VMEM((SUBLANES, tm + CONV31_HALO, d), F32),
         pltpu.VMEM((SUBLANES, d), F32),
         pltpu.SemaphoreType.DMA((1 + 3 * N_DEV,))],
        [h, g, b_in, win_all, *smalls, w3_all], comm)


SG_BIN, SG_CW4, SG_CB4, SG_BA, SG_BX, SG_LAM, SG_CB31, SG_LNG, SG_LNB, SG_BCP, SG_MIX, SG_CW31 = 0, 6, 10, 11, 12, 13, 14, 15, 16, 17, 18, 19


def _mixer_bwd(dh2, h, g, proj, xr_s, hs_s, v1_s, ya_s, yb_s, win_t, cw4, wg, ba, bx, lam, cw31, lng, lnb, w3_all, tm,
               comm=None):
    tp, d = dh2.shape
    n_in = proj.shape[1]
    hd = wg.shape[-1]
    k4, k31 = cw4.shape[0], cw31.shape[0]
    nt = tp // tm
    w3_rows = d // N_DEV
    sg_rows = -(-(SG_CW31 + k31) // SUBLANES) * SUBLANES
    halo_rows = 16
    per = tm // halo_rows

    def body(dh_ref, h_ref, g_ref, p_ref, xr_ref, hs_ref, hh_ref, v1_ref, ya_ref, yb_ref, win_hbm,
             cw4_ref, wg_ref, wgt_ref, ba_ref, bx_ref, lam_ref, cw31_ref, lng_ref, lnb_ref, w3_hbm,
             dh1_ref, dp_ref, x3_ref, y3_ref, yg_ref, sg_ref,
             win_v, w3_v, extd4, extd31, es31, gcar, sems):
        i = pl.program_id(0)
        tile = nt - 1 - i

        @pl.when(i == 0)
        def _():
            _load_weights([(win_hbm, win_v)] + _w3_copies(w3_hbm, w3_rows, w3_v), sems)
            for q in range(3):
                w3_v[q] = w3_v[q].T
            extd4[pl.ds(tm, CONV4_HALO), :] = jnp.zeros((CONV4_HALO, d), F32)
            extd31[pl.ds(tm, CONV31_HALO), :] = jnp.zeros((CONV31_HALO, d), F32)
            gcar[...] = jnp.zeros_like(gcar)
            sg_ref[...] = jnp.zeros_like(sg_ref)

        def acc(row, val):
            sg_ref[row:row + 1, :] += _rowsum(val)

        rows = lax.broadcasted_iota(jnp.int32, (tm, d), 0)
        x_rnn = p_ref[:, 0:d].astype(F32)
        y_rnn = p_ref[:, d:2 * d].astype(F32)
        glu_v = p_ref[:, 2 * d:3 * d].astype(F32)
        glu_g = p_ref[:, 3 * d:4 * d].astype(F32)
        sga = _sigmoid(p_ref[:, 4 * d:5 * d].astype(F32))
        sgb = _sigmoid(p_ref[:, 5 * d:6 * d].astype(F32))
        ya = ya_ref[...].astype(F32)
        yb = yb_ref[...].astype(F32)

        dmob = dh_ref[...].astype(BF16)
        dmerged = _nn(dmob, w3_v[2])
        x3_ref[:, 0:d] = (sga * ya + sgb * yb).astype(BF16)
        y3_ref[:, 0:d] = dmob
        dya = sga * dmerged
        dyb = sgb * dmerged
        dn_parts = []

        def emit(q, val):
            vb = val.astype(BF16)
            dp_ref[:, q * d:(q + 1) * d] = vb
            acc(SG_BIN + q, val)
            term = _nn(vb, win_v[pl.ds(q * d, d), :])
            dn_parts[:] = [term if not dn_parts else dn_parts[0] + term]

        emit(4, dmerged * ya * sga * (1.0 - sga))
        emit(5, dmerged * yb * sgb * (1.0 - sgb))

        dyab = dya.astype(BF16)
        y3_ref[:, d:2 * d] = dyab
        dza = _nn(dyab, w3_v[0])
        hsv = hs_ref[...].astype(F32)
        gl, th = _gelu(y_rnn)
        x3_ref[:, d:2 * d] = (hsv * gl).astype(BF16)
        emit(1, dza * hsv * _gelu_grad(y_rnn, th))
        dhs = dza * gl
        xrb = xr_ref[...]
        xr = xrb.astype(F32)
        lam_v = lam_ref[...]
        r, ig, sp, a, s = _gates(xrb, wg_ref, ba_ref[...], bx_ref[...], lam_v, hd)
        b = jnp.where(rows == tm - 1, gcar[1:2, :], pltpu.roll(a, tm - 1, 0))
        big_g = _scan_bwd(b, dhs, gcar[0:1, :])
        gcar[0:1, :] = big_g[0:1, :]
        gcar[1:2, :] = a[0:1, :]
        h_before = jnp.where(tile > 0, hh_ref[halo_rows - 1:halo_rows, :].astype(F32), 0.0)
        h_prev = jnp.where(rows == 0, h_before, pltpu.roll(hsv, 1, 0))
        ds = big_g * ig * xr
        dla = big_g * h_prev * a - ds * (a * a) / jnp.maximum(s, 1e-20)
        acc(SG_LAM, dla * r * (RG_LRU_C * _sigmoid(-lam_v)))
        dpr = dla * (-RG_LRU_C * sp) * r * (1.0 - r)
        dpi = big_g * s * xr * ig * (1.0 - ig)
        acc(SG_BA, dpr)
        acc(SG_BX, dpi)
        dprb = dpr.astype(BF16)
        dpib = dpi.astype(BF16)
        yg_ref[:, 0:d] = dprb
        yg_ref[:, d:2 * d] = dpib
        back = []
        for hh in range(N_HEADS):
            sl = slice(hh * hd, (hh + 1) * hd)
            back.append(_nn(dprb[:, sl], wgt_ref[0, hh]) + _nn(dpib[:, sl], wgt_ref[1, hh]))
        dxr = big_g * s * ig + jnp.concatenate(back, axis=1)
        acc(SG_CB4, dxr)
        extd4[pl.ds(0, tm), :] = dxr
        dx_rnn = jnp.zeros((tm, d), F32)
        for k in range(k4):
            term = extd4[pl.ds(k4 - 1 - k, tm), :]
            dx_rnn = dx_rnn + cw4_ref[k:k + 1, :] * term
            acc(SG_CW4 + k, x_rnn * term)
        extd4[pl.ds(tm, CONV4_HALO), :] = extd4[pl.ds(0, CONV4_HALO), :]
        emit(0, dx_rnn)

        dybb = dyb.astype(BF16)
        y3_ref[:, 2 * d:3 * d] = dybb
        acc(SG_BCP, dyb)
        dv3 = _nn(dybb, w3_v[1])
        v1 = v1_ref[...].astype(F32)
        xc = v1 - jnp.mean(v1, axis=-1, keepdims=True)
        rstd = lax.rsqrt(jnp.mean(xc * xc, axis=-1, keepdims=True) + EPS)
        xhat = xc * rstd
        lng_v = lng_ref[...]
        v2 = xhat * lng_v + lnb_ref[...]
        s2 = _sigmoid(v2)
        x3_ref[:, 2 * d:3 * d] = (v2 * s2).astype(BF16)
        dv2 = dv3 * (s2 * (1.0 + v2 * (1.0 - s2)))
        acc(SG_LNG, dv2 * xhat)
        acc(SG_LNB, dv2)
        dxh = dv2 * lng_v
        dv1 = rstd * (dxh - jnp.mean(dxh, axis=-1, keepdims=True)
                      - xhat * jnp.mean(dxh * xhat, axis=-1, keepdims=True))
        acc(SG_CB31, dv1)
        extd31[pl.ds(0, tm), :] = dv1
        _shifted_copies(extd31, es31, tm + CONV31_HALO - SUBLANES)
        sgg = _sigmoid(glu_g)
        v0 = glu_v * sgg
        dv0 = jnp.zeros((tm, d), F32)
        for k in range(k31):
            term = _tap(extd31, es31, k31 - 1 - k, tm)
            dv0 = dv0 + cw31_ref[k:k + 1, :] * term
            acc(SG_CW31 + k, v0 * term)
        extd31[pl.ds(tm, CONV31_HALO), :] = extd31[pl.ds(0, CONV31_HALO), :]
        emit(2, dv0 * sgg)
        emit(3, dv0 * glu_v * sgg * (1.0 - sgg))

        dn = dn_parts[0]
        x = h_ref[...]
        rr = lax.rsqrt(jnp.mean(x * x, axis=-1, keepdims=True) + EPS)
        dx, dgp = _rms_bwd(dn, x, rr, g_ref[...])
        dh1_ref[...] = dh_ref[...] + dx
        sg_ref[SG_MIX:SG_MIX + 1, :] += dgp

    rev = lambda i: (nt - 1 - i, 0)
    row = pl.BlockSpec((tm, d), rev)
    wide = pl.BlockSpec((tm, n_in), rev)
    full = lambda a: pl.BlockSpec(a.shape, lambda i, nd=a.ndim: (0,) * nd)
    halo = pl.BlockSpec((halo_rows, d), lambda i: (jnp.maximum((nt - 1 - i) * per - 1, 0), 0))
    smalls = [cw4, wg, jnp.swapaxes(wg, 2, 3), ba, bx, lam, cw31, lng, lnb]
    return _call(
        body, "mixer_bwd", (nt,),
        [row, row, full(g), wide, row, row, halo, row, row, row, _any()]
        + [full(a) for a in smalls] + [_any()],
        [row, wide, pl.BlockSpec((tm, 3 * d), rev), pl.BlockSpec((tm, 3 * d), rev),
         pl.BlockSpec((tm, 2 * d), rev), pl.BlockSpec((sg_rows, d), lambda i: (0, 0))],
        [jax.ShapeDtypeStruct((tp, d), F32), jax.ShapeDtypeStruct((tp, n_in), BF16),
         jax.ShapeDtypeStruct((tp, 3 * d), BF16), jax.ShapeDtypeStruct((tp, 3 * d), BF16),
         jax.ShapeDtypeStruct((tp, 2 * d), BF16), jax.ShapeDtypeStruct((sg_rows, d), F32)],
        [pltpu.VMEM(win_t.shape, BF16),
         pltpu.VMEM((3, d, d), BF16),
         pltpu.VMEM((tm + CONV4_HALO, d), F32),
         pltpu.VMEM((tm + CONV31_HALO, d), F32),
         pltpu.VMEM((SUBLANES, tm + CONV31_HALO, d), F32),
         pltpu.VMEM((SUBLANES, d), F32),
         pltpu.SemaphoreType.DMA((1 + 3 * N_DEV,))],
        [dh2, h, g, proj, xr_s, hs_s, hs_s, v1_s, ya_s, yb_s, win_t, *smalls, w3_all], comm)


def _tn_matmul(name, x, y, x_spec, y_spec, n_blocks, kb, nb, tm, tp, out_shape, out_spec, out_view, comm=None,
               after=None):
    nt = tp // tm

    def body(x_ref, y_ref, *refs):
        o_ref, acc = refs[-2:]
        i = pl.program_id(1)

        @pl.when(i == 0)
        def _():
            acc[...] = jnp.zeros_like(acc)

        acc[...] += _tn(x_ref[...], y_ref[...])

        @pl.when(i == nt - 1)
        def _():
            o_ref[...] = acc[...].astype(BF16).reshape(out_view)

    follows = [] if after is None else [after]
    outs, extra = _call(body, name, (n_blocks, nt), [x_spec, y_spec] + [_any()] * len(follows), [out_spec],
                        [jax.ShapeDtypeStruct(out_shape, BF16)], [pltpu.VMEM((kb, nb), F32)], [x, y] + follows,
                        comm)
    return outs[0], extra


def kernel(x, meta_tokens, ffn1_norm, ffn1_w_gu, ffn1_w_down, mix_norm, w_in, b_in, rnn_conv_w, rnn_conv_b, rg_w_a, rg_b_a, rg_w_x, rg_b_x, rg_lambda, rnn_w_proj, conv_dw_w, conv_dw_b, conv_ln_g, conv_ln_b, conv_w_proj, conv_b_proj, w_out, ffn2_norm, ffn2_w_gu, ffn2_w_down, final_norm, loss_target, m_meta_tokens, m_ffn1_norm, m_ffn1_w_gu, m_ffn1_w_down, m_mix_norm, m_w_in, m_b_in, m_rnn_conv_w, m_rnn_conv_b, m_rg_w_a, m_rg_b_a, m_rg_w_x, m_rg_b_x, m_rg_lambda, m_rnn_w_proj, m_conv_dw_w, m_conv_dw_b, m_conv_ln_g, m_conv_ln_b, m_conv_w_proj, m_conv_b_proj, m_w_out, m_ffn2_norm, m_ffn2_w_gu, m_ffn2_w_down, m_final_norm, v_meta_tokens, v_ffn1_norm, v_ffn1_w_gu, v_ffn1_w_down, v_mix_norm, v_w_in, v_b_in, v_rnn_conv_w, v_rnn_conv_b, v_rg_w_a, v_rg_b_a, v_rg_w_x, v_rg_b_x, v_rg_lambda, v_rnn_w_proj, v_conv_dw_w, v_conv_dw_b, v_conv_ln_g, v_conv_ln_b, v_conv_w_proj, v_conv_b_proj, v_w_out, v_ffn2_norm, v_ffn2_w_gu, v_ffn2_w_down, v_final_norm):
    w = dict(locals())
    seq, d = x.shape[1], x.shape[2]
    n_meta = meta_tokens.shape[0]
    t_real = n_meta + seq
    tp, tm, tmx_fwd, tmx, tmt, tmw = _tiles(t_real)
    fb = ffn1_w_gu.shape[-1]
    wr = ffn1_w_down.shape[1]
    f = N_DEV * wr
    fc = f // FFN_CHUNKS
    nbc = w_in.shape[-1]
    n_in = N_DEV * nbc
    pr = rnn_w_proj.shape[1]
    hd = rg_w_a.shape[-1]
    gr = rg_w_a.shape[2]
    cw = meta_tokens.shape[1]
    k4, k31 = rnn_conv_w.shape[1], conv_dw_w.shape[1]
    assert n_in == 6 * d and 2 * wr == fb and N_HEADS * hd == d and pr * N_DEV == d

    xi, yi, ci = lax.axis_index("x"), lax.axis_index("y"), lax.axis_index("c")
    core = ci.astype(jnp.int32).reshape(1)
    chip = (2 * xi + yi).astype(jnp.int32).reshape(1)
    me_index = (4 * xi + 2 * yi + ci).astype(jnp.int32).reshape(1)

    for nm in ("ffn1_w_gu", "ffn2_w_gu"):
        for pre in ("", "m_", "v_"):
            w[pre + nm] = jnp.swapaxes(w[pre + nm], 1, 2)

    wgut1 = w["ffn1_w_gu"][0].astype(BF16)
    wgut2 = w["ffn2_w_gu"][0].astype(BF16)
    wd1 = ffn1_w_down[0].astype(BF16)
    wd2 = ffn2_w_down[0].astype(BF16)
    win_loc = w_in[0].astype(BF16)
    win_t_loc = jnp.swapaxes(w_in[0], 0, 1).astype(BF16)
    w3_loc = jnp.concatenate([rnn_w_proj[0], conv_w_proj[0], w_out[0]], axis=0).astype(BF16)
    wg_loc = jnp.stack([rg_w_a[0], rg_w_x[0]]).astype(BF16)
    n_small = n_meta + k4 + k31
    small_rows = -(-n_small // SUBLANES) * SUBLANES
    small_loc = jnp.concatenate([meta_tokens, rnn_conv_w[0], conv_dw_w[0],
                                 jnp.zeros((small_rows - n_small, cw), F32)], axis=0)
    (wgut1_all, wd1_all, small_all), h0, tgt = _first_gather(
        [wgut1, wd1, small_loc], 2, x[0], loss_target[0], n_meta, tp)
    small_full = small_all.transpose(1, 0, 2).reshape(small_rows, d)
    cw4 = small_full[n_meta:n_meta + k4]
    cw31 = small_full[n_meta + k4:n_meta + k4 + k31]

    wgu1, wdn1 = wgut1_all.reshape(2 * f, d), wd1_all.reshape(f, d)
    (h1, gu1, n1), (win_all, w3_all, wg_all) = _ffn_fwd(
        h0, ffn1_norm, wgu1, wdn1, tm, comm=_Gather([win_loc, w3_loc, wg_loc], pass_on_at=(0.65, 0.95)))
    wg = wg_all.transpose(1, 2, 0, 3, 4).reshape(2, N_HEADS, hd, hd)
    (h2, proj, n2, xr_s, hs_s, v1_s, ya_s, yb_s), (wgut2_all, wd2_all, win_t_all) = _mixer_fwd(
        h1, mix_norm, b_in, win_all, cw4, rnn_conv_b, wg, rg_b_a, rg_b_x, rg_lambda, cw31, conv_dw_b, conv_ln_g,
        conv_ln_b, conv_b_proj, w3_all, tmx_fwd, comm=_Gather([wgut2, wd2, win_t_loc], pass_on_at=(0.45, 0.7)))
    wgu2, wdn2 = wgut2_all.reshape(2 * f, d), wd2_all.reshape(f, d)
    win_t = win_t_all.reshape(n_in, d)
    (dh3, gu2, n3, tail), _ = _ffn_fwd(
        h2, ffn2_norm, wgu2, wdn2, tm, loss=(tgt, final_norm.reshape(1, d), n_meta, t_real))

    def d_w_gu(tag, dgu, n_s, comm=None, after=None):
        g, extra = _tn_matmul(
            "d_w_gu" + tag, dgu, n_s,
            pl.BlockSpec((None, tmt, fc), lambda b, i: (b // FFN_CHUNKS, i, b % FFN_CHUNKS)),
            pl.BlockSpec((tmt, d), lambda b, i: (i, 0)),
            2 * FFN_CHUNKS, fc, d, tmt, tp, (2 * FFN_CHUNKS, fc, d),
            pl.BlockSpec((None, fc, d), lambda b, i: (b, 0, 0)), (fc, d), comm, after)
        return g.reshape(N_DEV, fb, d), extra

    def d_w_down(tag, act, df):
        g, _ = _tn_matmul(
            "d_w_down" + tag, act, df,
            pl.BlockSpec((tmt, fc), lambda b, i: (i, b)), pl.BlockSpec((tmt, d), lambda b, i: (i, 0)),
            FFN_CHUNKS, fc, d, tmt, tp, (FFN_CHUNKS, fc, d),
            pl.BlockSpec((None, fc, d), lambda b, i: (b, 0, 0)), (fc, d))
        return g.reshape(N_DEV, wr, d)

    (dh2, dgu2, act2, df2, tail), _ = _ffn_bwd(dh3, h2, gu2, ffn2_norm, wgu2, wdn2, tm, tail, TAIL_FFN2, n3)
    g_wgu2, _ = d_w_gu("2", dgu2, n3)
    g_wd2 = d_w_down("2", act2, df2)
    (dh1, dproj, x3, y3, yg, sg), (r_wd2, r_wgu2) = _mixer_bwd(
        dh2, h1, mix_norm, proj, xr_s, hs_s, v1_s, ya_s, yb_s, win_t, cw4, wg, rg_b_a, rg_b_x, rg_lambda, cw31,
        conv_ln_g, conv_ln_b, w3_all, tmx, comm=_Scatter([g_wd2, g_wgu2]))
    g_w3, _ = _tn_matmul(
        "d_w_proj3", x3, y3,
        pl.BlockSpec((tmw, d), lambda b, i: (i, b)), pl.BlockSpec((tmw, d), lambda b, i: (i, b)),
        3, d, d, tmw, tp, (N_DEV, 3, pr, d), pl.BlockSpec((N_DEV, None, pr, d), lambda b, i: (0, b, 0, 0)),
        (N_DEV, pr, d))
    g_wg, _ = _tn_matmul(
        "d_w_gates", xr_s, yg,
        pl.BlockSpec((tmw, hd), lambda b, i: (i, b % N_HEADS)), pl.BlockSpec((tmw, hd), lambda b, i: (i, b)),
        2 * N_HEADS, hd, hd, tmw, tp, (N_DEV, 2 * N_HEADS, gr, hd),
        pl.BlockSpec((N_DEV, None, gr, hd), lambda b, i: (0, b, 0, 0)), (N_DEV, gr, hd))
    w3_sems, g_w3_thru, w3_land, w3_token = _exchange_start("grads_proj3_exchange", _scatter_copies, N_DEV - 1, g_w3)
    g_win, (r_wg,) = _tn_matmul(
        "d_w_in", n2, dproj,
        pl.BlockSpec((tmw, d), lambda b, i: (i, 0)), pl.BlockSpec((tmw, nbc), lambda b, i: (i, b)),
        N_DEV, d, nbc, tmw, tp, (N_DEV, d, nbc), pl.BlockSpec((None, d, nbc), lambda b, i: (b, 0, 0)), (d, nbc),
        comm=_Scatter([g_wg]), after=w3_token)
    win_sems, g_win_thru, win_land, win_token = _exchange_start("grads_w_in_exchange", _scatter_copies, N_DEV - 1, g_win)
    (dh0, dgu1, act1, df1, tail), _ = _ffn_bwd(dh1, h0, gu1, ffn1_norm, wgu1, wdn1, tm, tail, TAIL_FFN1, win_token)

    pieces = [sg, dh0[:n_meta], tail]
    assert all(p.shape[0] % SUBLANES == 0 for p in pieces)
    at = [0, sg.shape[0], sg.shape[0] + n_meta]
    loss_row = at[2] + TAIL_LOSS
    rep_rows = [("ffn1_norm", at[2] + TAIL_FFN1, 1), ("mix_norm", SG_MIX, 1), ("b_in", SG_BIN, 6),
                ("rnn_conv_b", SG_CB4, 1),
                ("rg_b_a", SG_BA, 1), ("rg_b_x", SG_BX, 1), ("rg_lambda", SG_LAM, 1), ("conv_dw_b", SG_CB31, 1),
                ("conv_ln_g", SG_LNG, 1), ("conv_ln_b", SG_LNB, 1), ("conv_b_proj", SG_BCP, 1),
                ("ffn2_norm", at[2] + TAIL_FFN2, 1), ("final_norm", at[2] + TAIL_FINAL, 1)]
    col_rows = [("meta_tokens", at[1], n_meta), ("rnn_conv_w", SG_CW4, k4), ("conv_dw_w", SG_CW31, k31)]
    layout = []
    for nm, row0, nr in rep_rows:
        kind = "wide" if nm == "b_in" else "rep"
        as2d = lambda a: a.reshape(1, -1) if a.ndim == 1 else a
        layout.append((kind, row0, nr, as2d(w[nm]), as2d(w["m_" + nm]), as2d(w["v_" + nm])))
    for nm, row0, nr in col_rows:
        sq = lambda a: a.reshape(a.shape[-2], a.shape[-1])
        layout.append(("col", row0, nr, sq(w[nm]), sq(w["m_" + nm]), sq(w["v_" + nm])))
    small_partial = jnp.concatenate(pieces, axis=0)

    g_wd1 = d_w_down("1", act1, df1)
    wd1_sems, g_wd1_thru, wd1_land, wd1_token = _exchange_start(
        "grads_w_down1_exchange", _scatter_copies, N_DEV - 1, g_wd1)
    g_wgu1, (small_partials,) = d_w_gu("1", dgu1, n1, comm=_Bcast(small_partial), after=wd1_token)

    g_last = g_wgu1.reshape((4, 2) + g_wgu1.shape[1:])
    comb_wgu1 = _pair_reduce(g_last, core)
    sems, comb_thru, land_thru, after = _exchange_start("grads_chip_exchange", _chip_copies, 3, comb_wgu1)
    g_win, r_win = _exchange_wait("grads_w_in_exchange", _scatter_copies, win_sems, g_win_thru, win_land, after)
    g_w3, r_w3 = _exchange_wait("grads_proj3_exchange", _scatter_copies, w3_sems, g_w3_thru, w3_land, after)
    g_wd1, r_wd1 = _exchange_wait("grads_w_down1_exchange", _scatter_copies, wd1_sems, g_wd1_thru, wd1_land, after)

    grad_x = (dh0[n_meta:t_real] + after[0, 0])[None]
    total, small_out = _small_adamw(small_partials, layout, me_index, grad_x)
    after = total

    groups = [(g_wd1, r_wd1, me_index, ["ffn1_w_down"]),
              (g_wd2, r_wd2, me_index, ["ffn2_w_down"]), (g_wgu2, r_wgu2, me_index, ["ffn2_w_gu"]),
              (g_win, r_win, me_index, ["w_in"]), (g_w3, r_w3, me_index, ["w_out", "rnn_w_proj", "conv_w_proj"]),
              (g_wg, r_wg, me_index, ["rg_w_a", "rg_w_x"]), (None, None, chip, ["ffn1_w_gu"])]
    res = {}
    for own, recv, idx, group in groups:
        if own is None:
            own, recv = _exchange_wait("grads_chip_exchange", _chip_copies, sems, comb_thru, land_thru, after)
        outs = _final_adamw(own, recv, idx, [(w[nm], w["m_" + nm], w["v_" + nm]) for nm in group], after)
        after = outs[-1][0]
        for nm, o in zip(group, outs):
            res[nm] = o
    for nm in ("ffn1_w_gu", "ffn2_w_gu"):
        res[nm] = tuple(jnp.swapaxes(a, 1, 2) for a in res[nm])
    for (nm, _, _), o in zip(rep_rows + col_rows, small_out):
        res[nm] = tuple(a.reshape(w[nm].shape) for a in o)


    order = ["meta_tokens", "ffn1_norm", "ffn1_w_gu", "ffn1_w_down", "mix_norm", "w_in", "b_in", "rnn_conv_w",
             "rnn_conv_b", "rg_w_a", "rg_b_a", "rg_w_x", "rg_b_x", "rg_lambda", "rnn_w_proj", "conv_dw_w",
             "conv_dw_b", "conv_ln_g", "conv_ln_b", "conv_w_proj", "conv_b_proj", "w_out", "ffn2_norm",
             "ffn2_w_gu", "ffn2_w_down", "final_norm"]
    return (total[loss_row, 0], grad_x, *[res[nm][0] for nm in order], *[res[nm][1] for nm in order],
            *[res[nm][2] for nm in order], *[res[nm][3] for nm in order])
```

```python
import functools
import math

import jax
import jax.numpy as jnp
from jax import lax
from jax.experimental import pallas as pl
from jax.experimental.pallas import tpu as pltpu

F32 = jnp.float32
BF16 = jnp.bfloat16
MESH = pl.DeviceIdType.MESH
N_DEV = 8
N_HEADS = 4
RG_LRU_C = 8.0
EPS = 1e-6
FFN_RES = 0.5
ADAM_LR, ADAM_B1, ADAM_B2, ADAM_EPS, ADAM_WD, ADAM_STEP = 0.001, 0.9, 0.999, 1e-08, 0.01, 10
V7X_VMEM_LIMIT = 56 * 1024 * 1024
CONV4_HALO = 8
CONV31_HALO = 32
SUBLANES = 8
STAGE_ROWS = 512
TAIL_FFN1, TAIL_FINAL, TAIL_LOSS, TAIL_FFN2 = 0, 1, 2, 3
FFN_CHUNKS = 2
FFN_FWD_CHUNKS = 1
GELU_C = math.sqrt(2.0 / math.pi)
GELU_K = 0.044715


def _any():
    return pl.BlockSpec(memory_space=pl.ANY)


def _params(n_grid):
    return pltpu.CompilerParams(dimension_semantics=("arbitrary",) * n_grid, vmem_limit_bytes=V7X_VMEM_LIMIT)


def _nn(a, b):
    return jnp.dot(a, b, preferred_element_type=F32)


def _nt(a, b):
    return lax.dot_general(a, b, (((1,), (1,)), ((), ())), preferred_element_type=F32)


def _tn(a, b):
    return lax.dot_general(a, b, (((0,), (0,)), ((), ())), preferred_element_type=F32)


def _sigmoid(x):
    return 0.5 * jnp.tanh(0.5 * x) + 0.5


def _rowsum(x):
    return jnp.sum(x, axis=0, keepdims=True)


def _rms_fwd(x, g):
    r = lax.rsqrt(jnp.mean(x * x, axis=-1, keepdims=True) + EPS)
    return x * r * g, r


def _rms_bwd(dn, x, r, g):
    xr = x * r
    gy = dn * g
    dx = r * (gy - xr * jnp.mean(gy * xr, axis=-1, keepdims=True))
    return dx, _rowsum(dn * xr)


def _gelu(y):
    t = jnp.tanh(GELU_C * (y + GELU_K * y * y * y))
    return 0.5 * y * (1.0 + t), t


def _gelu_grad(y, t):
    return 0.5 * (1.0 + t) + 0.5 * y * (1.0 - t * t) * GELU_C * (1.0 + 3.0 * GELU_K * y * y)


def _softplus(x):
    return jnp.maximum(x, 0.0) + jnp.log(1.0 + jnp.exp(-jnp.abs(x)))


def _one_minus_exp(z):
    series = -z * (1.0 + 0.5 * z * (1.0 + z * (1.0 / 3.0) * (1.0 + 0.25 * z)))
    return jnp.where(z > -0.05, series, 1.0 - jnp.exp(z))


def _tiles(t_real):
    if t_real > 2048:
        tm = 416
        tp = -(-t_real // tm) * tm
        return tp, tm, tm // 2, tm // 2, tp, tp
    tm = 128
    tp = -(-t_real // tm) * tm
    return tp, tm, tm // 2, tm // 2, tm, tm


def _load_weights(copies, sems):
    cps = [pltpu.make_async_copy(s, d, sems.at[k]) for k, (s, d) in enumerate(copies)]
    for cp in cps:
        cp.start()
    for cp in cps:
        cp.wait()


def _position():
    x, y, c = lax.axis_index("x"), lax.axis_index("y"), lax.axis_index("c")
    chips = [(1 - x, y), (x, 1 - y), (1 - x, 1 - y)]
    return x, y, c, chips


def _slot(p):
    return 4 * p[0] + 2 * p[1] + p[2]


class _Lazy(dict):
    def __getitem__(self, key):
        val = dict.__getitem__(self, key)
        return val() if callable(val) else val


class _Gather:
    def __init__(self, shards, pass_on_at=None):
        self.shards = list(shards)
        self.n = len(self.shards)
        self.pass_on_at = pass_on_at

    def inputs(self):
        return self.shards

    def out_shape(self):
        return [jax.ShapeDtypeStruct((N_DEV,) + s.shape, s.dtype) for s in self.shards]

    N_SEMS = 9

    def scratch(self):
        return [pltpu.SemaphoreType.DMA((self.N_SEMS * self.n,)), pltpu.SemaphoreType.DMA((self.N_SEMS * self.n,)),
                pltpu.SemaphoreType.DMA((self.n,))]

    def _plan(self, ins, outs, sems):
        send_sems, recv_sems, local_sems = sems
        x, y, c, _ = _position()
        me, sib, xn, yn, dg = (x, y, c), (x, y, 1 - c), (1 - x, y, c), (x, 1 - y, c), (1 - x, 1 - y, c)
        other = lambda p: (p[0], p[1], 1 - c)

        def blk(a, p, half=None):
            ref = outs[a].at[_slot(p)]
            if half is None:
                return ref
            rows = self.shards[a].shape[0] // 2
            return ref.at[pl.ds(half * rows, rows)]

        def copy(a, k, dst, to, src=None):
            return pltpu.make_async_remote_copy(
                src_ref=dst if src is None else src, dst_ref=dst,
                send_sem=send_sems.at[self.N_SEMS * a + k], recv_sem=recv_sems.at[self.N_SEMS * a + k],
                device_id=to, device_id_type=MESH)

        cp = _Lazy(mine=lambda: [pltpu.make_async_copy(ins[a], blk(a, me), local_sems.at[a]) for a in range(self.n)])
        for a in range(self.n):
            cp[a] = _Lazy(
                own=lambda a=a: [copy(a, 0, blk(a, me), sib, src=ins[a]), copy(a, 1, blk(a, me), xn, src=ins[a]),
                                 copy(a, 2, blk(a, me), yn, src=ins[a])],
                from_x=lambda a=a: copy(a, 1, blk(a, xn), me), from_y=lambda a=a: copy(a, 2, blk(a, yn), me),
                relay_x=lambda a=a: copy(a, 3, blk(a, xn, 0), yn), relay_y=lambda a=a: copy(a, 4, blk(a, yn, 1), xn),
                diag0=lambda a=a: copy(a, 3, blk(a, dg, 0), me), diag1=lambda a=a: copy(a, 4, blk(a, dg, 1), me),
                pass_x=lambda a=a: copy(a, 5, blk(a, xn), sib), pass_y=lambda a=a: copy(a, 6, blk(a, yn), sib),
                pass_d0=lambda a=a: copy(a, 7, blk(a, dg, 0), sib), pass_d1=lambda a=a: copy(a, 8, blk(a, dg, 1), sib),
                from_sib=lambda a=a: [copy(a, 0, blk(a, sib), me), copy(a, 5, blk(a, other(xn)), me),
                                      copy(a, 6, blk(a, other(yn)), me), copy(a, 7, blk(a, other(dg), 0), me),
                                      copy(a, 8, blk(a, other(dg), 1), me)])
        return cp

    def start(self, ins, outs, sems):
        cp = self._plan(ins, outs, sems)
        for c in cp["mine"]:
            c.start()
        for a in range(self.n):
            for c in cp[a]["own"]:
                c.start()

    def pass_on(self, ins, outs, sems):
        cp = self._plan(ins, outs, sems)
        for a in range(self.n):
            cp[a]["from_x"].wait_recv()
            cp[a]["relay_x"].start()
            cp[a]["pass_x"].start()
        for a in range(self.n):
            cp[a]["from_y"].wait_recv()
            cp[a]["relay_y"].start()
            cp[a]["pass_y"].start()

    def pass_on_relayed(self, ins, outs, sems):
        cp = self._plan(ins, outs, sems)
        for a in range(self.n):
            cp[a]["diag0"].wait_recv()
            cp[a]["pass_d0"].start()
            cp[a]["diag1"].wait_recv()
            cp[a]["pass_d1"].start()

    def finish(self, ins, outs, sems):
        if self.pass_on_at is None:
            self.pass_on(ins, outs, sems)
            self.pass_on_relayed(ins, outs, sems)
        cp = self._plan(ins, outs, sems)
        for a in range(self.n):
            for c in cp[a]["from_sib"]:
                c.wait_recv()
            for c in cp[a]["own"] + [cp[a][k] for k in ("relay_x", "relay_y", "pass_x", "pass_y", "pass_d0", "pass_d1")]:
                c.wait_send()
        for c in cp["mine"]:
            c.wait()


class _Scatter:
    def __init__(self, grads):
        self.grads = list(grads)
        self.n = len(self.grads)

    def inputs(self):
        return self.grads

    def out_shape(self):
        return [jax.ShapeDtypeStruct((N_DEV - 1,) + g.shape[1:], g.dtype) for g in self.grads]

    def scratch(self):
        return [pltpu.SemaphoreType.DMA((7 * self.n,)), pltpu.SemaphoreType.DMA((7 * self.n,))]

    def _plan(self, ins, outs, sems):
        send_sems, recv_sems = sems
        x, y, c, _ = _position()
        cps = []
        for a in range(self.n):
            for k in range(1, N_DEV):
                peer = (x ^ (k >> 2), y ^ ((k >> 1) & 1), c ^ (k & 1))
                cps.append(pltpu.make_async_remote_copy(
                    src_ref=ins[a].at[_slot(peer)], dst_ref=outs[a].at[k - 1],
                    send_sem=send_sems.at[7 * a + k - 1], recv_sem=recv_sems.at[7 * a + k - 1],
                    device_id=peer, device_id_type=MESH))
        return cps

    def start(self, ins, outs, sems):
        for cp in self._plan(ins, outs, sems):
            cp.start()

    def finish(self, ins, outs, sems):
        for cp in self._plan(ins, outs, sems):
            cp.wait()


def _hosted(inner, n_in, n_out, comm, grid):
    if comm is None:
        return inner
    nc_in, nc_out, ns = len(comm.inputs()), len(comm.out_shape()), len(comm.scratch())

    def body(*refs):
        o0 = n_in + nc_in
        s0 = o0 + n_out + nc_out
        main = refs[:n_in] + refs[o0:o0 + n_out] + refs[s0:len(refs) - ns]
        c_in, c_out, c_sems = refs[n_in:o0], refs[o0 + n_out:s0], refs[len(refs) - ns:]
        ids = [pl.program_id(ax) for ax in range(len(grid))]
        first = functools.reduce(jnp.logical_and, [i == 0 for i in ids])
        last = functools.reduce(jnp.logical_and, [i == g - 1 for i, g in zip(ids, grid)])

        @pl.when(first)
        def _():
            comm.start(c_in, c_out, c_sems)

        inner(*main)

        if getattr(comm, "pass_on_at", None) is not None:
            assert len(grid) == 1
            first_at, second_at = (min(grid[0] - 1, int(frac * grid[0])) for frac in comm.pass_on_at)
            assert first_at < second_at

            @pl.when(ids[0] == first_at)
            def _():
                comm.pass_on(c_in, c_out, c_sems)

            @pl.when(ids[0] == second_at)
            def _():
                comm.pass_on_relayed(c_in, c_out, c_sems)

        @pl.when(last)
        def _():
            comm.finish(c_in, c_out, c_sems)

    return body


def _call(inner, name, grid, in_specs, out_specs, out_shape, scratch, args, comm=None):
    n_in, n_out = len(args), len(out_shape)
    body = _hosted(inner, n_in, n_out, comm, grid)
    if comm is not None:
        in_specs = list(in_specs) + [_any()] * len(comm.inputs())
        args = list(args) + comm.inputs()
        out_specs = list(out_specs) + [_any()] * len(comm.out_shape())
        out_shape = list(out_shape) + comm.out_shape()
        scratch = list(scratch) + comm.scratch()
    outs = pl.pallas_call(
        body, name=name, grid=grid, in_specs=list(in_specs), out_specs=list(out_specs), out_shape=list(out_shape),
        scratch_shapes=list(scratch), compiler_params=_params(len(grid)))(*args)
    return list(outs[:n_out]), list(outs[n_out:])


def _first_gather(shards, small_idx, x2, t2, n_meta, tp):
    comm = _Gather(shards)
    n = comm.n
    seq, d = x2.shape
    t_real = n_meta + seq
    n_pad = tp - t_real
    cw = d // N_DEV
    rows = STAGE_ROWS if seq % STAGE_ROWS == 0 else seq
    n_chunks = seq // rows

    def body(*refs):
        ins, (x_ref, t_ref) = refs[:n], refs[n:n + 2]
        outs, (h0_ref, tg_ref) = refs[n + 2:2 * n + 2], refs[2 * n + 2:2 * n + 4]
        sems = refs[2 * n + 4:2 * n + 7]
        buf, zeros, in_sems, out_sems, misc_sems = refs[2 * n + 7:]
        comm.start(ins, outs, sems)
        zeros[...] = jnp.zeros_like(zeros)
        fills = [pltpu.make_async_copy(zeros.at[pl.ds(0, n_pad)], h0_ref.at[pl.ds(t_real, n_pad)], misc_sems.at[0]),
                 pltpu.make_async_copy(zeros.at[pl.ds(0, n_pad)], tg_ref.at[pl.ds(t_real, n_pad)], misc_sems.at[1]),
                 pltpu.make_async_copy(zeros.at[pl.ds(0, n_meta)], tg_ref.at[pl.ds(0, n_meta)], misc_sems.at[2])]
        for cp in fills:
            cp.start()
        jobs = [(src, dst, c) for src, dst in ((x_ref, h0_ref), (t_ref, tg_ref)) for c in range(n_chunks)]

        def load(k):
            src, _, c = jobs[k]
            return pltpu.make_async_copy(src.at[pl.ds(c * rows, rows)], buf.at[k % 2], in_sems.at[k % 2])

        def store(k):
            _, dst, c = jobs[k]
            return pltpu.make_async_copy(buf.at[k % 2], dst.at[pl.ds(n_meta + c * rows, rows)], out_sems.at[k % 2])

        load(0).start()
        for k in range(len(jobs)):
            load(k).wait()
            if k + 1 < len(jobs):
                if k >= 1:
                    store(k - 1).wait()
                load(k + 1).start()
            store(k).start()
        for k in range(max(0, len(jobs) - 2), len(jobs)):
            store(k).wait()
        comm.finish(ins, outs, sems)
        meta = [pltpu.make_async_copy(outs[small_idx].at[k, pl.ds(0, n_meta)],
                                      h0_ref.at[pl.ds(0, n_meta), pl.ds(k * cw, cw)], misc_sems.at[3 + k])
                for k in range(N_DEV)]
        for cp in meta:
            cp.start()
        for cp in fills + meta:
            cp.wait()

    staged = [jax.ShapeDtypeStruct((tp, d), F32)] * 2
    outs = pl.pallas_call(
        body, name="weights_all_gather", out_shape=comm.out_shape() + staged,
        in_specs=[_any()] * (n + 2), out_specs=[_any()] * (n + 2),
        scratch_shapes=comm.scratch() + [
            pltpu.VMEM((2, rows, d), F32), pltpu.VMEM((max(n_pad, n_meta), d), F32),
            pltpu.SemaphoreType.DMA((2,)), pltpu.SemaphoreType.DMA((2,)), pltpu.SemaphoreType.DMA((3 + N_DEV,))],
        compiler_params=pltpu.CompilerParams(vmem_limit_bytes=V7X_VMEM_LIMIT),
    )(*shards, x2, t2)
    return outs[:n], outs[n], outs[n + 1]


def _chip_copies(c_ref, land_ref, sems):
    _, _, c, chips = _position()
    return [pltpu.make_async_remote_copy(
        src_ref=c_ref.at[2 * cx + cy], dst_ref=land_ref.at[j], send_sem=sems[j], recv_sem=sems[3 + j],
        device_id=(cx, cy, c), device_id_type=MESH) for j, (cx, cy) in enumerate(chips)]


def _scatter_copies(g_ref, land_ref, sems):
    x, y, c, _ = _position()
    cps = []
    for k in range(1, N_DEV):
        peer = (x ^ (k >> 2), y ^ ((k >> 1) & 1), c ^ (k & 1))
        cps.append(pltpu.make_async_remote_copy(
            src_ref=g_ref.at[_slot(peer)], dst_ref=land_ref.at[k - 1], send_sem=sems[k - 1],
            recv_sem=sems[N_DEV - 1 + k - 1], device_id=peer, device_id_type=MESH))
    return cps


def _bcast_copies(b_ref, land_ref, sems):
    x, y, c, _ = _position()
    mine = land_ref.at[_slot((x, y, c))]
    cps = []
    for k in range(1, N_DEV):
        peer = (x ^ (k >> 2), y ^ ((k >> 1) & 1), c ^ (k & 1))
        cps.append(pltpu.make_async_remote_copy(
            src_ref=b_ref, dst_ref=mine, send_sem=sems[k - 1], recv_sem=sems[N_DEV - 1 + k - 1],
            device_id=peer, device_id_type=MESH))
    return cps + [pltpu.make_async_copy(b_ref, mine, sems[2 * (N_DEV - 1)])]


def _exchanges_start(name, parts):
    hbm = pl.BlockSpec(memory_space=pltpu.HBM)
    sem = pl.BlockSpec(memory_space=pltpu.SEMAPHORE)
    n_parts = len(parts)
    total = sum(n for _, n, _, _ in parts)

    def body(*refs):
        ins, sems, token = refs[:2 * n_parts], refs[2 * n_parts:2 * n_parts + total], refs[4 * n_parts + total]
        at = 0
        for j, (copies, n, _, _) in enumerate(parts):
            for cp in copies(ins[2 * j], ins[2 * j + 1], sems[at:at + n]):
                cp.start()
            at += n
        token[...] = jnp.zeros_like(token)

    flat = []
    for _, _, src, land_shape in parts:
        flat += [pltpu.with_memory_space_constraint(src, pltpu.HBM),
                 pltpu.with_memory_space_constraint(lax.empty(land_shape, src.dtype), pltpu.HBM)]
    outs = pl.pallas_call(
        body, name=name + "_start",
        out_shape=(pltpu.SemaphoreType.DMA(()),) * total + tuple(pltpu.HBM(a.shape, a.dtype) for a in flat)
        + (jax.ShapeDtypeStruct((SUBLANES, 128), F32),),
        in_specs=(hbm,) * len(flat),
        out_specs=(sem,) * total + (hbm,) * len(flat) + (pl.BlockSpec(memory_space=pltpu.VMEM),),
        input_output_aliases={j: total + j for j in range(len(flat))},
        compiler_params=pltpu.CompilerParams(has_side_effects=pltpu.SideEffectType.DATAFLOW_SIDE_EFFECTING),
    )(*flat)
    res, at = [], 0
    for j, (_, n, _, _) in enumerate(parts):
        res.append((outs[at:at + n], outs[total + 2 * j], outs[total + 2 * j + 1]))
        at += n
    return res, outs[total + len(flat)]


def _exchange_start(name, copies, n_copies, src):
    (part,), token = _exchanges_start(name, [(copies, 2 * n_copies, src, (n_copies,) + src.shape[1:])])
    return (*part, token)


def _exchange_wait(name, copies, sems, src_thru, land_thru, after):
    hbm = pl.BlockSpec(memory_space=pltpu.HBM)
    sem = pl.BlockSpec(memory_space=pltpu.SEMAPHORE)
    n_sems = len(sems)

    def body(s_ref, land_ref, *refs):
        for cp in copies(s_ref, land_ref, refs[:n_sems]):
            cp.wait()

    return pl.pallas_call(
        body, name=name + "_wait",
        out_shape=(pltpu.HBM(src_thru.shape, src_thru.dtype), pltpu.HBM(land_thru.shape, land_thru.dtype)),
        in_specs=(hbm, hbm) + (sem,) * n_sems + (pl.BlockSpec(memory_space=pl.ANY),), out_specs=(hbm, hbm),
        input_output_aliases={0: 0, 1: 1},
        compiler_params=pltpu.CompilerParams(has_side_effects=pltpu.SideEffectType.DATAFLOW_SIDE_EFFECTING),
    )(src_thru, land_thru, *sems, after)


def _pair_reduce(grad, core):
    blk = grad.shape[2:]
    zeros = (0,) * len(blk)

    def body(core_ref, g_hbm, own_ref, o_ref, landed, send_sems, recv_sems):
        del core_ref
        i = pl.program_id(0)
        x, y, c, _ = _position()

        def copy(k):
            return pltpu.make_async_remote_copy(
                src_ref=g_hbm.at[k, 1 - c], dst_ref=landed.at[k], send_sem=send_sems.at[k],
                recv_sem=recv_sems.at[k], device_id=(x, y, 1 - c), device_id_type=MESH)

        @pl.when(i == 0)
        def _():
            for k in range(4):
                copy(k).start()

        for k in range(4):
            @pl.when(i == k)
            def _(k=k):
                copy(k).wait_recv()

        o_ref[...] = (own_ref[...].astype(F32) + landed[i].astype(F32)).astype(BF16)

        @pl.when(i == 3)
        def _():
            for k in range(4):
                copy(k).wait_send()

    return pl.pallas_call(
        body, name="grads_pair_reduce",
        out_shape=jax.ShapeDtypeStruct((4,) + blk, BF16),
        grid_spec=pltpu.PrefetchScalarGridSpec(
            num_scalar_prefetch=1, grid=(4,),
            in_specs=[_any(), pl.BlockSpec((None, None) + blk, lambda i, cr: (i, cr[0]) + zeros)],
            out_specs=pl.BlockSpec((None,) + blk, lambda i, cr: (i,) + zeros),
            scratch_shapes=[pltpu.VMEM((4,) + blk, BF16), pltpu.SemaphoreType.DMA((4,)),
                            pltpu.SemaphoreType.DMA((4,))]),
        compiler_params=_params(1),
    )(core, grad, grad)


def _adamw(w, g, m, v):
    m2 = ADAM_B1 * m + (1.0 - ADAM_B1) * g
    v2 = ADAM_B2 * v + (1.0 - ADAM_B2) * (g * g)
    m_hat = m2 / (1.0 - ADAM_B1 ** ADAM_STEP)
    v_hat = v2 / (1.0 - ADAM_B2 ** ADAM_STEP)
    delta = -ADAM_LR * (m_hat / (jnp.sqrt(v_hat) + ADAM_EPS) + ADAM_WD * w)
    return delta, m2, v2


def _final_adamw(own, recv, idx, parts, after):
    blk = own.shape[1:]
    n_recv = recv.shape[0]
    n_parts = len(parts)
    per = blk[0] // n_parts if n_parts > 1 else None
    rows = blk[-2]
    n_chunks = 1 if n_parts > 1 else (4 if rows % 64 == 0 and rows >= 512 else (2 if rows % 32 == 0 else 1))
    cblk = blk[:-2] + (rows // n_chunks, blk[-1])
    lead = (0,) * (len(blk) - 2)

    def body(idx_ref, c_ref, r_ref, after_ref, *refs):
        del idx_ref, after_ref
        ins, outs = refs[:3 * n_parts], refs[3 * n_parts:]
        g = c_ref[...].astype(F32)
        for k in range(n_recv):
            g = g + r_ref[k].astype(F32)
        for p in range(n_parts):
            w_ref, m_ref, v_ref = ins[3 * p:3 * p + 3]
            if n_parts == 1:
                gp = g
            elif per == 1:
                gp = g[p]
            else:
                gp = g[p * per:(p + 1) * per]
            delta, m2, v2 = _adamw(w_ref[0], gp, m_ref[0], v_ref[0])
            o = outs[4 * p:4 * p + 4]
            o[0][0] = gp
            o[1][0] = delta
            o[2][0] = m2
            o[3][0] = v2

    flat = [a for wmv in parts for a in wmv]

    def part_spec(a):
        shape = a.shape[:-2] + (a.shape[-2] // n_chunks, a.shape[-1])
        return pl.BlockSpec(shape, lambda i, cr, nd=a.ndim: (0,) * (nd - 2) + (i, 0))

    outs = pl.pallas_call(
        body, name="grads_sum_adamw",
        out_shape=[jax.ShapeDtypeStruct(wmv[0].shape, F32) for wmv in parts for _ in range(4)],
        grid_spec=pltpu.PrefetchScalarGridSpec(
            num_scalar_prefetch=1, grid=(n_chunks,),
            in_specs=[pl.BlockSpec((None,) + cblk, lambda i, cr: (cr[0],) + lead + (i, 0)),
                      pl.BlockSpec((n_recv,) + cblk, lambda i, cr: (0,) + lead + (i, 0))]
                     + [_any()] + [part_spec(a) for a in flat],
            out_specs=[part_spec(wmv[0]) for wmv in parts for _ in range(4)]),
        compiler_params=_params(1),
    )(idx, own, recv, after, *flat)
    return [tuple(outs[4 * p:4 * p + 4]) for p in range(n_parts)]


def _small_adamw(partials, layout, me_index, after):
    _, rows, d = partials.shape
    n = len(layout)
    cw = d // N_DEV

    def body(me_ref, p_ref, after_ref, *refs):
        ins, t_ref, outs = refs[:3 * n], refs[3 * n], refs[3 * n + 1:]
        me = me_ref[0]
        total = p_ref[0]
        for j in range(1, N_DEV):
            total = total + p_ref[j]
        t_ref[...] = total
        for e, (kind, r0, nr, _, _, _) in enumerate(layout):
            w_ref, m_ref, v_ref = ins[3 * e:3 * e + 3]
            o = outs[4 * e:4 * e + 4]
            if kind == "rep":
                g = t_ref[r0:r0 + nr, :]
                delta, m2, v2 = _adamw(w_ref[...], g, m_ref[...], v_ref[...])
                for ref, val in zip(o, (g, delta, m2, v2)):
                    ref[...] = val
            elif kind == "wide":
                for q in range(nr):
                    sl = slice(q * d, (q + 1) * d)
                    g = t_ref[r0 + q:r0 + q + 1, :]
                    delta, m2, v2 = _adamw(w_ref[:, sl], g, m_ref[:, sl], v_ref[:, sl])
                    for ref, val in zip(o, (g, delta, m2, v2)):
                        ref[:, sl] = val
            else:
                for j in range(N_DEV):
                    @pl.when(me == j)
                    def _(j=j, o=o, w_ref=w_ref, m_ref=m_ref, v_ref=v_ref, r0=r0, nr=nr):
                        g = t_ref[r0:r0 + nr, j * cw:(j + 1) * cw]
                        delta, m2, v2 = _adamw(w_ref[...], g, m_ref[...], v_ref[...])
                        for ref, val in zip(o, (g, delta, m2, v2)):
                            ref[...] = val

    flat = [a for ent in layout for a in ent[3:]]
    vm = pl.BlockSpec(memory_space=pltpu.VMEM)
    outs = pl.pallas_call(
        body, name="small_adamw",
        out_shape=[jax.ShapeDtypeStruct((rows, d), F32)]
                  + [jax.ShapeDtypeStruct(ent[3].shape, F32) for ent in layout for _ in range(4)],
        in_specs=[pl.BlockSpec(memory_space=pltpu.SMEM), vm, _any()] + [vm] * len(flat),
        out_specs=[vm] * (1 + 4 * n),
        compiler_params=pltpu.CompilerParams(vmem_limit_bytes=V7X_VMEM_LIMIT),
    )(me_index, partials, after, *flat)
    return outs[0], [tuple(outs[1 + 4 * e:5 + 4 * e]) for e in range(n)]


def _ffn_fwd(h, g, wgu, wd, tm, loss=None, comm=None):
    tp, d = h.shape
    f = wd.shape[0]
    fc = f // FFN_FWD_CHUNKS
    nt = tp // tm
    with_loss = loss is not None
    if with_loss:
        tgt, gf, n_meta, t_real = loss

    def body(*refs):
        if with_loss:
            (h_ref, g_ref, wgu_hbm, wd_hbm, tgt_ref, gf_ref, out_ref, gu_ref, n_ref, tail_ref,
             wgu_v, wd_v, sems) = refs
        else:
            h_ref, g_ref, wgu_hbm, wd_hbm, out_ref, gu_ref, n_ref, wgu_v, wd_v, sems = refs
        i = pl.program_id(0)

        @pl.when(i == 0)
        def _():
            _load_weights([(wgu_hbm, wgu_v), (wd_hbm, wd_v)], sems)
            if with_loss:
                tail_ref[...] = jnp.zeros_like(tail_ref)

        x = h_ref[...]
        n, _ = _rms_fwd(x, g_ref[...])
        nb = n.astype(BF16)
        n_ref[...] = nb
        acc = jnp.zeros((tm, d), F32)
        for j in range(FFN_FWD_CHUNKS):
            cols = slice(j * fc, (j + 1) * fc)
            gate = _nt(nb, wgu_v[pl.ds(j * fc, fc), :])
            up = _nt(nb, wgu_v[pl.ds(f + j * fc, fc), :])
            gu_ref[0, :, cols] = gate.astype(BF16)
            gu_ref[1, :, cols] = up.astype(BF16)
            act = (gate * _sigmoid(gate) * up).astype(BF16)
            acc = acc + _nn(act, wd_v[pl.ds(j * fc, fc), :])
        hn = x + FFN_RES * acc
        if not with_loss:
            out_ref[...] = hn
        else:
            gfv = gf_ref[...]
            r = lax.rsqrt(jnp.mean(hn * hn, axis=-1, keepdims=True) + EPS)
            xr = hn * r
            rows = i * tm + lax.broadcasted_iota(jnp.int32, (tm, 1), 0)
            mask = jnp.logical_and(rows >= n_meta, rows < t_real)
            diff = jnp.where(mask, xr * gfv - tgt_ref[...], 0.0)
            tail_ref[TAIL_LOSS:TAIL_LOSS + 1, :] += jnp.zeros((1, d), F32) + 0.5 * jnp.sum(diff * diff) / d
            dy = diff / d
            gy = dy * gfv
            out_ref[...] = r * (gy - xr * jnp.mean(gy * xr, axis=-1, keepdims=True))
            tail_ref[TAIL_FINAL:TAIL_FINAL + 1, :] += _rowsum(dy * xr)

    row = pl.BlockSpec((tm, d), lambda i: (i, 0))
    vec = pl.BlockSpec((1, d), lambda i: (0, 0))
    in_specs = [row, vec, _any(), _any()]
    out_shape = [jax.ShapeDtypeStruct((tp, d), F32), jax.ShapeDtypeStruct((2, tp, f), BF16),
                 jax.ShapeDtypeStruct((tp, d), BF16)]
    out_specs = [row, pl.BlockSpec((2, tm, f), lambda i: (0, i, 0)), row]
    args = [h, g, wgu, wd]
    if with_loss:
        in_specs += [row, vec]
        out_shape += [jax.ShapeDtypeStruct((SUBLANES, d), F32)]
        out_specs += [pl.BlockSpec((SUBLANES, d), lambda i: (0, 0))]
        args += [tgt, gf]
    return _call(body, "ffn_fwd_loss" if with_loss else "ffn_fwd", (nt,), in_specs, out_specs, out_shape,
                 [pltpu.VMEM((2 * f, d), BF16), pltpu.VMEM((f, d), BF16), pltpu.SemaphoreType.DMA((2,))],
                 args, comm)


def _ffn_bwd(dh, h, gu, g, wgu, wd, tm, tail, tail_row, after):
    tp, d = h.shape
    f = wd.shape[0]
    fc = f // FFN_CHUNKS
    nt = tp // tm

    def body(dh_ref, h_ref, gu_ref, g_ref, tail_ref, wgu_hbm, wd_hbm, after_ref,
             dhin_ref, dgu_ref, act_ref, df_ref, dg_ref, wgu_v, wd_v, dn_v, sems):
        del after_ref
        i, j = pl.program_id(0), pl.program_id(1)

        @pl.when(jnp.logical_and(i == 0, j == 0))
        def _():
            _load_weights([(wgu_hbm, wgu_v), (wd_hbm, wd_v)], sems)
            dg_ref[...] = tail_ref[...]

        dfb = (FFN_RES * dh_ref[...]).astype(BF16)

        @pl.when(j == 0)
        def _():
            df_ref[...] = dfb
            dn_v[...] = jnp.zeros_like(dn_v)

        lo = pl.multiple_of(j * fc, 16)
        dact = _nt(dfb, wd_v[pl.ds(lo, fc), :])
        gate = gu_ref[0].astype(F32)
        up = gu_ref[1].astype(F32)
        sg = _sigmoid(gate)
        silu = gate * sg
        act_ref[...] = (silu * up).astype(BF16)
        dgate = (dact * up * (sg * (1.0 + gate * (1.0 - sg)))).astype(BF16)
        dup = (dact * silu).astype(BF16)
        dgu_ref[0] = dgate
        dgu_ref[1] = dup
        dn_v[...] += _nn(dgate, wgu_v[pl.ds(lo, fc), :]) + _nn(dup, wgu_v[pl.ds(pl.multiple_of(f + j * fc, 16), fc), :])

        @pl.when(j == FFN_CHUNKS - 1)
        def _():
            x = h_ref[...]
            r = lax.rsqrt(jnp.mean(x * x, axis=-1, keepdims=True) + EPS)
            dx, dgp = _rms_bwd(dn_v[...], x, r, g_ref[...])
            dhin_ref[...] = dh_ref[...] + dx
            dg_ref[tail_row:tail_row + 1, :] += dgp

    row = pl.BlockSpec((tm, d), lambda i, j: (i, 0))
    vec = pl.BlockSpec((1, d), lambda i, j: (0, 0))
    tile = pl.BlockSpec((SUBLANES, d), lambda i, j: (0, 0))
    hid2 = pl.BlockSpec((2, tm, fc), lambda i, j: (0, i, j))
    return _call(
        body, "ffn_bwd", (nt, FFN_CHUNKS),
        [row, row, hid2, vec, tile, _any(), _any(), _any()],
        [row, hid2, pl.BlockSpec((tm, fc), lambda i, j: (i, j)), row, tile],
        [jax.ShapeDtypeStruct((tp, d), F32), jax.ShapeDtypeStruct((2, tp, f), BF16),
         jax.ShapeDtypeStruct((tp, f), BF16), jax.ShapeDtypeStruct((tp, d), BF16),
         jax.ShapeDtypeStruct((SUBLANES, d), F32)],
        [pltpu.VMEM((2 * f, d), BF16), pltpu.VMEM((f, d), BF16), pltpu.VMEM((tm, d), F32),
         pltpu.SemaphoreType.DMA((2,))],
        [dh, h, gu, g, tail, wgu, wd, after])


def _piece_segments(q, d, nb_cols):
    segs = []
    for j in range(N_DEV):
        lo, hi = max(q * d, j * nb_cols), min((q + 1) * d, (j + 1) * nb_cols)
        if lo < hi:
            segs.append((j, lo - q * d, hi - q * d, lo - j * nb_cols, hi - j * nb_cols))
    return segs


def _w3_copies(w3_hbm, rows, w3_v):
    return [(w3_hbm.at[k, pl.ds(q * rows, rows)], w3_v.at[q, pl.ds(k * rows, rows)])
            for q in range(3) for k in range(N_DEV)]


def _gates(xrb, wg_ref, ba, bx, lam, hd):
    pre_r, pre_i = [], []
    for hh in range(N_HEADS):
        xh = xrb[:, hh * hd:(hh + 1) * hd]
        pre_r.append(_nn(xh, wg_ref[0, hh]))
        pre_i.append(_nn(xh, wg_ref[1, hh]))
    r = _sigmoid(jnp.concatenate(pre_r, axis=1) + ba)
    ig = _sigmoid(jnp.concatenate(pre_i, axis=1) + bx)
    sp = _softplus(-lam)
    log_a = -RG_LRU_C * r * sp
    a = jnp.exp(log_a)
    s = jnp.sqrt(_one_minus_exp(2.0 * log_a))
    return r, ig, sp, a, s


def _scan_fwd(a, u, h_prev):
    tm = a.shape[0]
    rows = lax.broadcasted_iota(jnp.int32, a.shape, 0)
    d = 1
    while d < tm:
        if d < SUBLANES:
            keep = rows >= d
            u = jnp.where(keep, a * pltpu.roll(u, d, 0) + u, u)
            a = jnp.where(keep, a * pltpu.roll(a, d, 0), a)
        else:
            u = jnp.concatenate([u[:d], a[d:] * u[:tm - d] + u[d:]], axis=0)
            a = jnp.concatenate([a[:d], a[d:] * a[:tm - d]], axis=0)
        d *= 2
    return u + a * h_prev


def _scan_bwd(b, v, g_next):
    tm = b.shape[0]
    rows = lax.broadcasted_iota(jnp.int32, b.shape, 0)
    d = 1
    while d < tm:
        if d < SUBLANES:
            keep = rows < tm - d
            v = jnp.where(keep, v + b * pltpu.roll(v, tm - d, 0), v)
            b = jnp.where(keep, b * pltpu.roll(b, tm - d, 0), b)
        else:
            v = jnp.concatenate([v[:tm - d] + b[:tm - d] * v[d:], v[tm - d:]], axis=0)
            b = jnp.concatenate([b[:tm - d] * b[d:], b[tm - d:]], axis=0)
        d *= 2
    return v + b * g_next


def _shifted_copies(ext_ref, es_ref, n_rows):
    for s in range(1, SUBLANES):
        es_ref[s, pl.ds(0, n_rows), :] = ext_ref[pl.ds(s, n_rows), :]


def _tap(ext_ref, es_ref, off, tm):
    q, s = divmod(off, SUBLANES)
    if s == 0:
        return ext_ref[pl.ds(SUBLANES * q, tm), :]
    return es_ref[s, pl.ds(SUBLANES * q, tm), :]


def _mixer_fwd(h, g, b_in, win_all, cw4, cb4, wg, ba, bx, lam, cw31, cb31, lng, lnb, bcp, w3_all, tm, comm=None):
    tp, d = h.shape
    nb_cols = win_all.shape[-1]
    n_in = N_DEV * nb_cols
    hd = wg.shape[-1]
    k4, k31 = cw4.shape[0], cw31.shape[0]
    w3_rows = d // N_DEV

    def body(h_ref, g_ref, b_ref, win_hbm, cw4_ref, cb4_ref, wg_ref, ba_ref, bx_ref, lam_ref, cw31_ref, cb31_ref,
             lng_ref, lnb_ref, bcp_ref, w3_hbm,
             h2_ref, p_ref, n_ref, xr_ref, hs_ref, v1_ref, ya_ref, yb_ref,
             win_v, w3_v, ext4, ext31, es31, hcar, sems):
        @pl.when(pl.program_id(0) == 0)
        def _():
            _load_weights([(win_hbm, win_v)] + _w3_copies(w3_hbm, w3_rows, w3_v), sems)
            ext4[pl.ds(0, CONV4_HALO), :] = jnp.zeros((CONV4_HALO, d), F32)
            ext31[pl.ds(0, CONV31_HALO), :] = jnp.zeros((CONV31_HALO, d), F32)
            hcar[...] = jnp.zeros_like(hcar)

        n, _ = _rms_fwd(h_ref[...], g_ref[...])
        nb = n.astype(BF16)
        n_ref[...] = nb

        def piece(q):
            parts = [_nn(nb, win_v[j, :, bl:bh]) for j, _, _, bl, bh in _piece_segments(q, d, nb_cols)]
            pq = (jnp.concatenate(parts, axis=1) + b_ref[:, q * d:(q + 1) * d]).astype(BF16)
            p_ref[:, q * d:(q + 1) * d] = pq
            return pq.astype(F32)

        x_rnn, y_rnn, glu_v, glu_g, gate_a, gate_b = [piece(q) for q in range(6)]

        ext4[pl.ds(CONV4_HALO, tm), :] = x_rnn
        xr = cb4_ref[...] + jnp.zeros((tm, d), F32)
        for k in range(k4):
            xr = xr + cw4_ref[k:k + 1, :] * ext4[pl.ds(CONV4_HALO - (k4 - 1) + k, tm), :]
        ext4[pl.ds(0, CONV4_HALO), :] = ext4[pl.ds(tm, CONV4_HALO), :]
        xrb = xr.astype(BF16)
        xr_ref[...] = xrb
        xr = xrb.astype(F32)
        _, ig, _, a, s = _gates(xrb, wg_ref, ba_ref[...], bx_ref[...], lam_ref[...], hd)
        hseq = _scan_fwd(a, s * (ig * xr), hcar[0:1, :])
        hcar[0:1, :] = hseq[tm - 1:tm, :]
        hs_ref[...] = hseq.astype(BF16)
        gl, _ = _gelu(y_rnn)
        ya = _nn((hseq * gl).astype(BF16), w3_v[0])
        ya_ref[...] = ya.astype(BF16)

        ext31[pl.ds(CONV31_HALO, tm), :] = glu_v * _sigmoid(glu_g)
        _shifted_copies(ext31, es31, tm + CONV31_HALO - SUBLANES)
        v1 = cb31_ref[...] + jnp.zeros((tm, d), F32)
        for k in range(k31):
            v1 = v1 + cw31_ref[k:k + 1, :] * _tap(ext31, es31, CONV31_HALO - (k31 - 1) + k, tm)
        ext31[pl.ds(0, CONV31_HALO), :] = ext31[pl.ds(tm, CONV31_HALO), :]
        v1b = v1.astype(BF16)
        v1_ref[...] = v1b
        v1 = v1b.astype(F32)
        xc = v1 - jnp.mean(v1, axis=-1, keepdims=True)
        rstd = lax.rsqrt(jnp.mean(xc * xc, axis=-1, keepdims=True) + EPS)
        v2 = xc * rstd * lng_ref[...] + lnb_ref[...]
        yb = _nn((v2 * _sigmoid(v2)).astype(BF16), w3_v[1]) + bcp_ref[...]
        yb_ref[...] = yb.astype(BF16)

        merged = _sigmoid(gate_a) * ya + _sigmoid(gate_b) * yb
        h2_ref[...] = h_ref[...] + _nn(merged.astype(BF16), w3_v[2])

    row = pl.BlockSpec((tm, d), lambda i: (i, 0))
    wide = pl.BlockSpec((tm, n_in), lambda i: (i, 0))
    full = lambda a: pl.BlockSpec(a.shape, lambda i, nd=a.ndim: (0,) * nd)
    smalls = [cw4, cb4, wg, ba, bx, lam, cw31, cb31, lng, lnb, bcp]
    return _call(
        body, "mixer_fwd", (tp // tm,),
        [row, full(g), full(b_in), _any()] + [full(a) for a in smalls] + [_any()],
        [row, wide] + [row] * 6,
        [jax.ShapeDtypeStruct((tp, d), F32), jax.ShapeDtypeStruct((tp, n_in), BF16)]
        + [jax.ShapeDtypeStruct((tp, d), BF16)] * 6,
        [pltpu.VMEM(win_all.shape, BF16),
         pltpu.VMEM((3, d, d), BF16),
         pltpu.VMEM((tm + CONV4_HALO, d), F32),
         pltpu.VMEM((tm + CONV31_HALO, d), F32),
         pltpu.VMEM((SUBLANES, tm + CONV31_HALO, d), F32),
         pltpu.VMEM((SUBLANES, d), F32),
         pltpu.SemaphoreType.DMA((1 + 3 * N_DEV,))],
        [h, g, b_in, win_all, *smalls, w3_all], comm)


SG_BIN, SG_CW4, SG_CB4, SG_BA, SG_BX, SG_LAM, SG_CB31, SG_LNG, SG_LNB, SG_BCP, SG_MIX, SG_CW31 = 0, 6, 10, 11, 12, 13, 14, 15, 16, 17, 18, 19


def _mixer_bwd(dh2, h, g, proj, xr_s, hs_s, v1_s, ya_s, yb_s, win_t, cw4, wg, ba, bx, lam, cw31, lng, lnb, w3_all, tm,
               comm=None):
    tp, d = dh2.shape
    n_in = proj.shape[1]
    hd = wg.shape[-1]
    k4, k31 = cw4.shape[0], cw31.shape[0]
    nt = tp // tm
    w3_rows = d // N_DEV
    sg_rows = -(-(SG_CW31 + k31) // SUBLANES) * SUBLANES
    halo_rows = 16
    per = tm // halo_rows

    def body(dh_ref, h_ref, g_ref, p_ref, xr_ref, hs_ref, hh_ref, v1_ref, ya_ref, yb_ref, win_hbm,
             cw4_ref, wg_ref, wgt_ref, ba_ref, bx_ref, lam_ref, cw31_ref, lng_ref, lnb_ref, w3_hbm,
             dh1_ref, dp_ref, x3_ref, y3_ref, yg_ref, sg_ref,
             win_v, w3_v, extd4, extd31, es31, gcar, sems):
        i = pl.program_id(0)
        tile = nt - 1 - i

        @pl.when(i == 0)
        def _():
            _load_weights([(win_hbm, win_v)] + _w3_copies(w3_hbm, w3_rows, w3_v), sems)
            for q in range(3):
                w3_v[q] = w3_v[q].T
            extd4[pl.ds(tm, CONV4_HALO), :] = jnp.zeros((CONV4_HALO, d), F32)
            extd31[pl.ds(tm, CONV31_HALO), :] = jnp.zeros((CONV31_HALO, d), F32)
            gcar[...] = jnp.zeros_like(gcar)
            sg_ref[...] = jnp.zeros_like(sg_ref)

        def acc(row, val):
            sg_ref[row:row + 1, :] += _rowsum(val)

        rows = lax.broadcasted_iota(jnp.int32, (tm, d), 0)
        x_rnn = p_ref[:, 0:d].astype(F32)
        y_rnn = p_ref[:, d:2 * d].astype(F32)
        glu_v = p_ref[:, 2 * d:3 * d].astype(F32)
        glu_g = p_ref[:, 3 * d:4 * d].astype(F32)
        sga = _sigmoid(p_ref[:, 4 * d:5 * d].astype(F32))
        sgb = _sigmoid(p_ref[:, 5 * d:6 * d].astype(F32))
        ya = ya_ref[...].astype(F32)
        yb = yb_ref[...].astype(F32)

        dmob = dh_ref[...].astype(BF16)
        dmerged = _nn(dmob, w3_v[2])
        x3_ref[:, 0:d] = (sga * ya + sgb * yb).astype(BF16)
        y3_ref[:, 0:d] = dmob
        dya = sga * dmerged
        dyb = sgb * dmerged
        dn_parts = []

        def emit(q, val):
            vb = val.astype(BF16)
            dp_ref[:, q * d:(q + 1) * d] = vb
            acc(SG_BIN + q, val)
            term = _nn(vb, win_v[pl.ds(q * d, d), :])
            dn_parts[:] = [term if not dn_parts else dn_parts[0] + term]

        emit(4, dmerged * ya * sga * (1.0 - sga))
        emit(5, dmerged * yb * sgb * (1.0 - sgb))

        dyab = dya.astype(BF16)
        y3_ref[:, d:2 * d] = dyab
        dza = _nn(dyab, w3_v[0])
        hsv = hs_ref[...].astype(F32)
        gl, th = _gelu(y_rnn)
        x3_ref[:, d:2 * d] = (hsv * gl).astype(BF16)
        emit(1, dza * hsv * _gelu_grad(y_rnn, th))
        dhs = dza * gl
        xrb = xr_ref[...]
        xr = xrb.astype(F32)
        lam_v = lam_ref[...]
        r, ig, sp, a, s = _gates(xrb, wg_ref, ba_ref[...], bx_ref[...], lam_v, hd)
        b = jnp.where(rows == tm - 1, gcar[1:2, :], pltpu.roll(a, tm - 1, 0))
        big_g = _scan_bwd(b, dhs, gcar[0:1, :])
        gcar[0:1, :] = big_g[0:1, :]
        gcar[1:2, :] = a[0:1, :]
        h_before = jnp.where(tile > 0, hh_ref[halo_rows - 1:halo_rows, :].astype(F32), 0.0)
        h_prev = jnp.where(rows == 0, h_before, pltpu.roll(hsv, 1, 0))
        ds = big_g * ig * xr
        dla = big_g * h_prev * a - ds * (a * a) / jnp.maximum(s, 1e-20)
        acc(SG_LAM, dla * r * (RG_LRU_C * _sigmoid(-lam_v)))
        dpr = dla * (-RG_LRU_C * sp) * r * (1.0 - r)
        dpi = big_g * s * xr * ig * (1.0 - ig)
        acc(SG_BA, dpr)
        acc(SG_BX, dpi)
        dprb = dpr.astype(BF16)
        dpib = dpi.astype(BF16)
        yg_ref[:, 0:d] = dprb
        yg_ref[:, d:2 * d] = dpib
        back = []
        for hh in range(N_HEADS):
            sl = slice(hh * hd, (hh + 1) * hd)
            back.append(_nn(dprb[:, sl], wgt_ref[0, hh]) + _nn(dpib[:, sl], wgt_ref[1, hh]))
        dxr = big_g * s * ig + jnp.concatenate(back, axis=1)
        acc(SG_CB4, dxr)
        extd4[pl.ds(0, tm), :] = dxr
        dx_rnn = jnp.zeros((tm, d), F32)
        for k in range(k4):
            term = extd4[pl.ds(k4 - 1 - k, tm), :]
            dx_rnn = dx_rnn + cw4_ref[k:k + 1, :] * term
            acc(SG_CW4 + k, x_rnn * term)
        extd4[pl.ds(tm, CONV4_HALO), :] = extd4[pl.ds(0, CONV4_HALO), :]
        emit(0, dx_rnn)

        dybb = dyb.astype(BF16)
        y3_ref[:, 2 * d:3 * d] = dybb
        acc(SG_BCP, dyb)
        dv3 = _nn(dybb, w3_v[1])
        v1 = v1_ref[...].astype(F32)
        xc = v1 - jnp.mean(v1, axis=-1, keepdims=True)
        rstd = lax.rsqrt(jnp.mean(xc * xc, axis=-1, keepdims=True) + EPS)
        xhat = xc * rstd
        lng_v = lng_ref[...]
        v2 = xhat * lng_v + lnb_ref[...]
        s2 = _sigmoid(v2)
        x3_ref[:, 2 * d:3 * d] = (v2 * s2).astype(BF16)
        dv2 = dv3 * (s2 * (1.0 + v2 * (1.0 - s2)))
        acc(SG_LNG, dv2 * xhat)
        acc(SG_LNB, dv2)
        dxh = dv2 * lng_v
        dv1 = rstd * (dxh - jnp.mean(dxh, axis=-1, keepdims=True)
                      - xhat * jnp.mean(dxh * xhat, axis=-1, keepdims=True))
        acc(SG_CB31, dv1)
        extd31[pl.ds(0, tm), :] = dv1
        _shifted_copies(extd31, es31, tm + CONV31_HALO - SUBLANES)
        sgg = _sigmoid(glu_g)
        v0 = glu_v * sgg
        dv0 = jnp.zeros((tm, d), F32)
        for k in range(k31):
            term = _tap(extd31, es31, k31 - 1 - k, tm)
            dv0 = dv0 + cw31_ref[k:k + 1, :] * term
            acc(SG_CW31 + k, v0 * term)
        extd31[pl.ds(tm, CONV31_HALO), :] = extd31[pl.ds(0, CONV31_HALO), :]
        emit(2, dv0 * sgg)
        emit(3, dv0 * glu_v * sgg * (1.0 - sgg))

        dn = dn_parts[0]
        x = h_ref[...]
        rr = lax.rsqrt(jnp.mean(x * x, axis=-1, keepdims=True) + EPS)
        dx, dgp = _rms_bwd(dn, x, rr, g_ref[...])
        dh1_ref[...] = dh_ref[...] + dx
        sg_ref[SG_MIX:SG_MIX + 1, :] += dgp

    rev = lambda i: (nt - 1 - i, 0)
    row = pl.BlockSpec((tm, d), rev)
    wide = pl.BlockSpec((tm, n_in), rev)
    full = lambda a: pl.BlockSpec(a.shape, lambda i, nd=a.ndim: (0,) * nd)
    halo = pl.BlockSpec((halo_rows, d), lambda i: (jnp.maximum((nt - 1 - i) * per - 1, 0), 0))
    smalls = [cw4, wg, jnp.swapaxes(wg, 2, 3), ba, bx, lam, cw31, lng, lnb]
    return _call(
        body, "mixer_bwd", (nt,),
        [row, row, full(g), wide, row, row, halo, row, row, row, _any()]
        + [full(a) for a in smalls] + [_any()],
        [row, wide, pl.BlockSpec((tm, 3 * d), rev), pl.BlockSpec((tm, 3 * d), rev),
         pl.BlockSpec((tm, 2 * d), rev), pl.BlockSpec((sg_rows, d), lambda i: (0, 0))],
        [jax.ShapeDtypeStruct((tp, d), F32), jax.ShapeDtypeStruct((tp, n_in), BF16),
         jax.ShapeDtypeStruct((tp, 3 * d), BF16), jax.ShapeDtypeStruct((tp, 3 * d), BF16),
         jax.ShapeDtypeStruct((tp, 2 * d), BF16), jax.ShapeDtypeStruct((sg_rows, d), F32)],
        [pltpu.VMEM(win_t.shape, BF16),
         pltpu.VMEM((3, d, d), BF16),
         pltpu.VMEM((tm + CONV4_HALO, d), F32),
         pltpu.VMEM((tm + CONV31_HALO, d), F32),
         pltpu.VMEM((SUBLANES, tm + CONV31_HALO, d), F32),
         pltpu.VMEM((SUBLANES, d), F32),
         pltpu.SemaphoreType.DMA((1 + 3 * N_DEV,))],
        [dh2, h, g, proj, xr_s, hs_s, hs_s, v1_s, ya_s, yb_s, win_t, *smalls, w3_all], comm)


def _tn_matmul(name, x, y, x_spec, y_spec, n_blocks, kb, nb, tm, tp, out_shape, out_spec, out_view, comm=None,
               after=None):
    nt = tp // tm

    def body(x_ref, y_ref, *refs):
        o_ref, acc = refs[-2:]
        i = pl.program_id(1)

        @pl.when(i == 0)
        def _():
            acc[...] = jnp.zeros_like(acc)

        acc[...] += _tn(x_ref[...], y_ref[...])

        @pl.when(i == nt - 1)
        def _():
            o_ref[...] = acc[...].astype(BF16).reshape(out_view)

    follows = [] if after is None else [after]
    outs, extra = _call(body, name, (n_blocks, nt), [x_spec, y_spec] + [_any()] * len(follows), [out_spec],
                        [jax.ShapeDtypeStruct(out_shape, BF16)], [pltpu.VMEM((kb, nb), F32)], [x, y] + follows,
                        comm)
    return outs[0], extra


def kernel(x, meta_tokens, ffn1_norm, ffn1_w_gu, ffn1_w_down, mix_norm, w_in, b_in, rnn_conv_w, rnn_conv_b, rg_w_a, rg_b_a, rg_w_x, rg_b_x, rg_lambda, rnn_w_proj, conv_dw_w, conv_dw_b, conv_ln_g, conv_ln_b, conv_w_proj, conv_b_proj, w_out, ffn2_norm, ffn2_w_gu, ffn2_w_down, final_norm, loss_target, m_meta_tokens, m_ffn1_norm, m_ffn1_w_gu, m_ffn1_w_down, m_mix_norm, m_w_in, m_b_in, m_rnn_conv_w, m_rnn_conv_b, m_rg_w_a, m_rg_b_a, m_rg_w_x, m_rg_b_x, m_rg_lambda, m_rnn_w_proj, m_conv_dw_w, m_conv_dw_b, m_conv_ln_g, m_conv_ln_b, m_conv_w_proj, m_conv_b_proj, m_w_out, m_ffn2_norm, m_ffn2_w_gu, m_ffn2_w_down, m_final_norm, v_meta_tokens, v_ffn1_norm, v_ffn1_w_gu, v_ffn1_w_down, v_mix_norm, v_w_in, v_b_in, v_rnn_conv_w, v_rnn_conv_b, v_rg_w_a, v_rg_b_a, v_rg_w_x, v_rg_b_x, v_rg_lambda, v_rnn_w_proj, v_conv_dw_w, v_conv_dw_b, v_conv_ln_g, v_conv_ln_b, v_conv_w_proj, v_conv_b_proj, v_w_out, v_ffn2_norm, v_ffn2_w_gu, v_ffn2_w_down, v_final_norm):
    w = dict(locals())
    seq, d = x.shape[1], x.shape[2]
    n_meta = meta_tokens.shape[0]
    t_real = n_meta + seq
    tp, tm, tmx_fwd, tmx, tmt, tmw = _tiles(t_real)
    fb = ffn1_w_gu.shape[-1]
    wr = ffn1_w_down.shape[1]
    f = N_DEV * wr
    fc = f // FFN_CHUNKS
    nbc = w_in.shape[-1]
    n_in = N_DEV * nbc
    pr = rnn_w_proj.shape[1]
    hd = rg_w_a.shape[-1]
    gr = rg_w_a.shape[2]
    cw = meta_tokens.shape[1]
    k4, k31 = rnn_conv_w.shape[1], conv_dw_w.shape[1]
    assert n_in == 6 * d and 2 * wr == fb and N_HEADS * hd == d and pr * N_DEV == d

    xi, yi, ci = lax.axis_index("x"), lax.axis_index("y"), lax.axis_index("c")
    core = ci.astype(jnp.int32).reshape(1)
    chip = (2 * xi + yi).astype(jnp.int32).reshape(1)
    me_index = (4 * xi + 2 * yi + ci).astype(jnp.int32).reshape(1)

    for nm in ("ffn1_w_gu", "ffn2_w_gu"):
        for pre in ("", "m_", "v_"):
            w[pre + nm] = jnp.swapaxes(w[pre + nm], 1, 2)

    wgut1 = w["ffn1_w_gu"][0].astype(BF16)
    wgut2 = w["ffn2_w_gu"][0].astype(BF16)
    wd1 = ffn1_w_down[0].astype(BF16)
    wd2 = ffn2_w_down[0].astype(BF16)
    win_loc = w_in[0].astype(BF16)
    win_t_loc = jnp.swapaxes(w_in[0], 0, 1).astype(BF16)
    w3_loc = jnp.concatenate([rnn_w_proj[0], conv_w_proj[0], w_out[0]], axis=0).astype(BF16)
    wg_loc = jnp.stack([rg_w_a[0], rg_w_x[0]]).astype(BF16)
    n_small = n_meta + k4 + k31
    small_rows = -(-n_small // SUBLANES) * SUBLANES
    small_loc = jnp.concatenate([meta_tokens, rnn_conv_w[0], conv_dw_w[0],
                                 jnp.zeros((small_rows - n_small, cw), F32)], axis=0)
    (wgut1_all, wd1_all, small_all), h0, tgt = _first_gather(
        [wgut1, wd1, small_loc], 2, x[0], loss_target[0], n_meta, tp)
    small_full = small_all.transpose(1, 0, 2).reshape(small_rows, d)
    cw4 = small_full[n_meta:n_meta + k4]
    cw31 = small_full[n_meta + k4:n_meta + k4 + k31]

    wgu1, wdn1 = wgut1_all.reshape(2 * f, d), wd1_all.reshape(f, d)
    (h1, gu1, n1), (win_all, w3_all, wg_all) = _ffn_fwd(
        h0, ffn1_norm, wgu1, wdn1, tm, comm=_Gather([win_loc, w3_loc, wg_loc], pass_on_at=(0.65, 0.95)))
    wg = wg_all.transpose(1, 2, 0, 3, 4).reshape(2, N_HEADS, hd, hd)
    (h2, proj, n2, xr_s, hs_s, v1_s, ya_s, yb_s), (wgut2_all, wd2_all, win_t_all) = _mixer_fwd(
        h1, mix_norm, b_in, win_all, cw4, rnn_conv_b, wg, rg_b_a, rg_b_x, rg_lambda, cw31, conv_dw_b, conv_ln_g,
        conv_ln_b, conv_b_proj, w3_all, tmx_fwd, comm=_Gather([wgut2, wd2, win_t_loc], pass_on_at=(0.45, 0.7)))
    wgu2, wdn2 = wgut2_all.reshape(2 * f, d), wd2_all.reshape(f, d)
    win_t = win_t_all.reshape(n_in, d)
    (dh3, gu2, n3, tail), _ = _ffn_fwd(
        h2, ffn2_norm, wgu2, wdn2, tm, loss=(tgt, final_norm.reshape(1, d), n_meta, t_real))

    def d_w_gu(tag, dgu, n_s, after=None):
        g, _ = _tn_matmul(
            "d_w_gu" + tag, dgu, n_s,
            pl.BlockSpec((None, tmt, fc), lambda b, i: (b // FFN_CHUNKS, i, b % FFN_CHUNKS)),
            pl.BlockSpec((tmt, d), lambda b, i: (i, 0)),
            2 * FFN_CHUNKS, fc, d, tmt, tp, (2 * FFN_CHUNKS, fc, d),
            pl.BlockSpec((None, fc, d), lambda b, i: (b, 0, 0)), (fc, d), after=after)
        return g.reshape(N_DEV, fb, d)

    def d_w_down(tag, act, df):
        g, _ = _tn_matmul(
            "d_w_down" + tag, act, df,
            pl.BlockSpec((tmt, fc), lambda b, i: (i, b)), pl.BlockSpec((tmt, d), lambda b, i: (i, 0)),
            FFN_CHUNKS, fc, d, tmt, tp, (FFN_CHUNKS, fc, d),
            pl.BlockSpec((None, fc, d), lambda b, i: (b, 0, 0)), (fc, d))
        return g.reshape(N_DEV, wr, d)

    (dh2, dgu2, act2, df2, tail), _ = _ffn_bwd(dh3, h2, gu2, ffn2_norm, wgu2, wdn2, tm, tail, TAIL_FFN2, n3)
    g_wgu2 = d_w_gu("2", dgu2, n3)
    g_wd2 = d_w_down("2", act2, df2)
    (dh1, dproj, x3, y3, yg, sg), (r_wd2, r_wgu2) = _mixer_bwd(
        dh2, h1, mix_norm, proj, xr_s, hs_s, v1_s, ya_s, yb_s, win_t, cw4, wg, rg_b_a, rg_b_x, rg_lambda, cw31,
        conv_ln_g, conv_ln_b, w3_all, tmx, comm=_Scatter([g_wd2, g_wgu2]))
    g_w3, _ = _tn_matmul(
        "d_w_proj3", x3, y3,
        pl.BlockSpec((tmw, d), lambda b, i: (i, b)), pl.BlockSpec((tmw, d), lambda b, i: (i, b)),
        3, d, d, tmw, tp, (N_DEV, 3, pr, d), pl.BlockSpec((N_DEV, None, pr, d), lambda b, i: (0, b, 0, 0)),
        (N_DEV, pr, d))
    g_wg, _ = _tn_matmul(
        "d_w_gates", xr_s, yg,
        pl.BlockSpec((tmw, hd), lambda b, i: (i, b % N_HEADS)), pl.BlockSpec((tmw, hd), lambda b, i: (i, b)),
        2 * N_HEADS, hd, hd, tmw, tp, (N_DEV, 2 * N_HEADS, gr, hd),
        pl.BlockSpec((N_DEV, None, gr, hd), lambda b, i: (0, b, 0, 0)), (N_DEV, gr, hd))
    w3_sems, g_w3_thru, w3_land, w3_token = _exchange_start("grads_proj3_exchange", _scatter_copies, N_DEV - 1, g_w3)
    g_win, (r_wg,) = _tn_matmul(
        "d_w_in", n2, dproj,
        pl.BlockSpec((tmw, d), lambda b, i: (i, 0)), pl.BlockSpec((tmw, nbc), lambda b, i: (i, b)),
        N_DEV, d, nbc, tmw, tp, (N_DEV, d, nbc), pl.BlockSpec((None, d, nbc), lambda b, i: (b, 0, 0)), (d, nbc),
        comm=_Scatter([g_wg]), after=w3_token)
    win_sems, g_win_thru, win_land, win_token = _exchange_start("grads_w_in_exchange", _scatter_copies, N_DEV - 1, g_win)
    (dh0, dgu1, act1, df1, tail), _ = _ffn_bwd(dh1, h0, gu1, ffn1_norm, wgu1, wdn1, tm, tail, TAIL_FFN1, win_token)

    pieces = [sg, dh0[:n_meta], tail]
    assert all(p.shape[0] % SUBLANES == 0 for p in pieces)
    at = [0, sg.shape[0], sg.shape[0] + n_meta]
    loss_row = at[2] + TAIL_LOSS
    rep_rows = [("ffn1_norm", at[2] + TAIL_FFN1, 1), ("mix_norm", SG_MIX, 1), ("b_in", SG_BIN, 6),
                ("rnn_conv_b", SG_CB4, 1),
                ("rg_b_a", SG_BA, 1), ("rg_b_x", SG_BX, 1), ("rg_lambda", SG_LAM, 1), ("conv_dw_b", SG_CB31, 1),
                ("conv_ln_g", SG_LNG, 1), ("conv_ln_b", SG_LNB, 1), ("conv_b_proj", SG_BCP, 1),
                ("ffn2_norm", at[2] + TAIL_FFN2, 1), ("final_norm", at[2] + TAIL_FINAL, 1)]
    col_rows = [("meta_tokens", at[1], n_meta), ("rnn_conv_w", SG_CW4, k4), ("conv_dw_w", SG_CW31, k31)]
    layout = []
    for nm, row0, nr in rep_rows:
        kind = "wide" if nm == "b_in" else "rep"
        as2d = lambda a: a.reshape(1, -1) if a.ndim == 1 else a
        layout.append((kind, row0, nr, as2d(w[nm]), as2d(w["m_" + nm]), as2d(w["v_" + nm])))
    for nm, row0, nr in col_rows:
        sq = lambda a: a.reshape(a.shape[-2], a.shape[-1])
        layout.append(("col", row0, nr, sq(w[nm]), sq(w["m_" + nm]), sq(w["v_" + nm])))
    small_partial = jnp.concatenate(pieces, axis=0)

    g_wd1 = d_w_down("1", act1, df1)
    ((small_sems, small_thru, small_land), (wd1_sems, g_wd1_thru, wd1_land)), wd1_token = _exchanges_start(
        "grads_w_down1_exchange",
        [(_bcast_copies, 2 * (N_DEV - 1) + 1, small_partial, (N_DEV,) + small_partial.shape),
         (_scatter_copies, 2 * (N_DEV - 1), g_wd1, (N_DEV - 1,) + g_wd1.shape[1:])])
    g_wgu1 = d_w_gu("1", dgu1, n1, after=wd1_token)

    g_last = g_wgu1.reshape((4, 2) + g_wgu1.shape[1:])
    comb_wgu1 = _pair_reduce(g_last, core)
    sems, comb_thru, land_thru, after = _exchange_start("grads_chip_exchange", _chip_copies, 3, comb_wgu1)
    g_win, r_win = _exchange_wait("grads_w_in_exchange", _scatter_copies, win_sems, g_win_thru, win_land, after)
    g_w3, r_w3 = _exchange_wait("grads_proj3_exchange", _scatter_copies, w3_sems, g_w3_thru, w3_land, after)
    g_wd1, r_wd1 = _exchange_wait("grads_w_down1_exchange", _scatter_copies, wd1_sems, g_wd1_thru, wd1_land, after)
    _, small_partials = _exchange_wait("grads_small_exchange", _bcast_copies, small_sems, small_thru, small_land, after)

    grad_x = (dh0[n_meta:t_real] + after[0, 0])[None]
    total, small_out = _small_adamw(small_partials, layout, me_index, grad_x)
    after = total

    groups = [(g_wd1, r_wd1, me_index, ["ffn1_w_down"]),
              (g_wd2, r_wd2, me_index, ["ffn2_w_down"]), (g_wgu2, r_wgu2, me_index, ["ffn2_w_gu"]),
              (g_win, r_win, me_index, ["w_in"]), (g_w3, r_w3, me_index, ["w_out", "rnn_w_proj", "conv_w_proj"]),
              (g_wg, r_wg, me_index, ["rg_w_a", "rg_w_x"]), (None, None, chip, ["ffn1_w_gu"])]
    res = {}
    for own, recv, idx, group in groups:
        if own is None:
            own, recv = _exchange_wait("grads_chip_exchange", _chip_copies, sems, comb_thru, land_thru, after)
        outs = _final_adamw(own, recv, idx, [(w[nm], w["m_" + nm], w["v_" + nm]) for nm in group], after)
        after = outs[-1][0]
        for nm, o in zip(group, outs):
            res[nm] = o
    for nm in ("ffn1_w_gu", "ffn2_w_gu"):
        res[nm] = tuple(jnp.swapaxes(a, 1, 2) for a in res[nm])
    for (nm, _, _), o in zip(rep_rows + col_rows, small_out):
        res[nm] = tuple(a.reshape(w[nm].shape) for a in o)


    order = ["meta_tokens", "ffn1_norm", "ffn1_w_gu", "ffn1_w_down", "mix_norm", "w_in", "b_in", "rnn_conv_w",
             "rnn_conv_b", "rg_w_a", "rg_b_a", "rg_w_x", "rg_b_x", "rg_lambda", "rnn_w_proj", "conv_dw_w",
             "conv_dw_b", "conv_ln_g", "conv_ln_b", "conv_w_proj", "conv_b_proj", "w_out", "ffn2_norm",
             "ffn2_w_gu", "ffn2_w_down", "final_norm"]
    return (total[loss_row, 0], grad_x, *[res[nm][0] for nm in order], *[res[nm][1] for nm in order],
            *[res[nm][2] for nm in order], *[res[nm][3] for nm in order])
```

```python
import functools
import math

import jax
import jax.numpy as jnp
from jax import lax
from jax.experimental import pallas as pl
from jax.experimental.pallas import tpu as pltpu

F32 = jnp.float32
BF16 = jnp.bfloat16
MESH = pl.DeviceIdType.MESH
N_DEV = 8
N_HEADS = 4
RG_LRU_C = 8.0
EPS = 1e-6
FFN_RES = 0.5
ADAM_LR, ADAM_B1, ADAM_B2, ADAM_EPS, ADAM_WD, ADAM_STEP = 0.001, 0.9, 0.999, 1e-08, 0.01, 10
V7X_VMEM_LIMIT = 56 * 1024 * 1024
CONV4_HALO = 8
CONV31_HALO = 32
SUBLANES = 8
STAGE_ROWS = 512
TAIL_FFN1, TAIL_FINAL, TAIL_LOSS, TAIL_FFN2 = 0, 1, 2, 3
FFN_CHUNKS = 2
FFN_FWD_CHUNKS = 1
GELU_C = math.sqrt(2.0 / math.pi)
GELU_K = 0.044715


def _any():
    return pl.BlockSpec(memory_space=pl.ANY)


def _params(n_grid):
    return pltpu.CompilerParams(dimension_semantics=("arbitrary",) * n_grid, vmem_limit_bytes=V7X_VMEM_LIMIT)


def _nn(a, b):
    return jnp.dot(a, b, preferred_element_type=F32)


def _nt(a, b):
    return lax.dot_general(a, b, (((1,), (1,)), ((), ())), preferred_element_type=F32)


def _tn(a, b):
    return lax.dot_general(a, b, (((0,), (0,)), ((), ())), preferred_element_type=F32)


def _sigmoid(x):
    return 0.5 * jnp.tanh(0.5 * x) + 0.5


def _rowsum(x):
    return jnp.sum(x, axis=0, keepdims=True)


def _rms_fwd(x, g):
    r = lax.rsqrt(jnp.mean(x * x, axis=-1, keepdims=True) + EPS)
    return x * r * g, r


def _rms_bwd(dn, x, r, g):
    xr = x * r
    gy = dn * g
    dx = r * (gy - xr * jnp.mean(gy * xr, axis=-1, keepdims=True))
    return dx, _rowsum(dn * xr)


def _gelu(y):
    t = jnp.tanh(GELU_C * (y + GELU_K * y * y * y))
    return 0.5 * y * (1.0 + t), t


def _gelu_grad(y, t):
    return 0.5 * (1.0 + t) + 0.5 * y * (1.0 - t * t) * GELU_C * (1.0 + 3.0 * GELU_K * y * y)


def _softplus(x):
    return jnp.maximum(x, 0.0) + jnp.log(1.0 + jnp.exp(-jnp.abs(x)))


def _one_minus_exp(z):
    series = -z * (1.0 + 0.5 * z * (1.0 + z * (1.0 / 3.0) * (1.0 + 0.25 * z)))
    return jnp.where(z > -0.05, series, 1.0 - jnp.exp(z))


def _tiles(t_real):
    if t_real > 2048:
        tm = 416
        tp = -(-t_real // tm) * tm
        return tp, tm, tm // 2, tm // 2, tp, tp
    tm = 128
    tp = -(-t_real // tm) * tm
    return tp, tm, tm // 2, tm // 2, tm, tm


def _load_weights(copies, sems):
    cps = [pltpu.make_async_copy(s, d, sems.at[k]) for k, (s, d) in enumerate(copies)]
    for cp in cps:
        cp.start()
    for cp in cps:
        cp.wait()


def _position():
    x, y, c = lax.axis_index("x"), lax.axis_index("y"), lax.axis_index("c")
    chips = [(1 - x, y), (x, 1 - y), (1 - x, 1 - y)]
    return x, y, c, chips


def _slot(p):
    return 4 * p[0] + 2 * p[1] + p[2]


class _Lazy(dict):
    def __getitem__(self, key):
        val = dict.__getitem__(self, key)
        return val() if callable(val) else val


class _Gather:
    def __init__(self, shards, pass_on_at=None):
        self.shards = list(shards)
        self.n = len(self.shards)
        self.pass_on_at = pass_on_at

    def inputs(self):
        return self.shards

    def out_shape(self):
        return [jax.ShapeDtypeStruct((N_DEV,) + s.shape, s.dtype) for s in self.shards]

    N_SEMS = 9

    def scratch(self):
        return [pltpu.SemaphoreType.DMA((self.N_SEMS * self.n,)), pltpu.SemaphoreType.DMA((self.N_SEMS * self.n,)),
                pltpu.SemaphoreType.DMA((self.n,))]

    def _plan(self, ins, outs, sems):
        send_sems, recv_sems, local_sems = sems
        x, y, c, _ = _position()
        me, sib, xn, yn, dg = (x, y, c), (x, y, 1 - c), (1 - x, y, c), (x, 1 - y, c), (1 - x, 1 - y, c)
        other = lambda p: (p[0], p[1], 1 - c)

        def blk(a, p, half=None):
            ref = outs[a].at[_slot(p)]
            if half is None:
                return ref
            rows = self.shards[a].shape[0] // 2
            return ref.at[pl.ds(half * rows, rows)]

        def copy(a, k, dst, to, src=None):
            return pltpu.make_async_remote_copy(
                src_ref=dst if src is None else src, dst_ref=dst,
                send_sem=send_sems.at[self.N_SEMS * a + k], recv_sem=recv_sems.at[self.N_SEMS * a + k],
                device_id=to, device_id_type=MESH)

        cp = _Lazy(mine=lambda: [pltpu.make_async_copy(ins[a], blk(a, me), local_sems.at[a]) for a in range(self.n)])
        for a in range(self.n):
            cp[a] = _Lazy(
                own=lambda a=a: [copy(a, 0, blk(a, me), sib, src=ins[a]), copy(a, 1, blk(a, me), xn, src=ins[a]),
                                 copy(a, 2, blk(a, me), yn, src=ins[a])],
                from_x=lambda a=a: copy(a, 1, blk(a, xn), me), from_y=lambda a=a: copy(a, 2, blk(a, yn), me),
                relay_x=lambda a=a: copy(a, 3, blk(a, xn, 0), yn), relay_y=lambda a=a: copy(a, 4, blk(a, yn, 1), xn),
                diag0=lambda a=a: copy(a, 3, blk(a, dg, 0), me), diag1=lambda a=a: copy(a, 4, blk(a, dg, 1), me),
                pass_x=lambda a=a: copy(a, 5, blk(a, xn), sib), pass_y=lambda a=a: copy(a, 6, blk(a, yn), sib),
                pass_d0=lambda a=a: copy(a, 7, blk(a, dg, 0), sib), pass_d1=lambda a=a: copy(a, 8, blk(a, dg, 1), sib),
                from_sib=lambda a=a: [copy(a, 0, blk(a, sib), me), copy(a, 5, blk(a, other(xn)), me),
                                      copy(a, 6, blk(a, other(yn)), me), copy(a, 7, blk(a, other(dg), 0), me),
                                      copy(a, 8, blk(a, other(dg), 1), me)])
        return cp

    def start(self, ins, outs, sems):
        cp = self._plan(ins, outs, sems)
        for c in cp["mine"]:
            c.start()
        for a in range(self.n):
            for c in cp[a]["own"]:
                c.start()

    def pass_on(self, ins, outs, sems):
        cp = self._plan(ins, outs, sems)
        for a in range(self.n):
            cp[a]["from_x"].wait_recv()
            cp[a]["relay_x"].start()
            cp[a]["pass_x"].start()
        for a in range(self.n):
            cp[a]["from_y"].wait_recv()
            cp[a]["relay_y"].start()
            cp[a]["pass_y"].start()

    def pass_on_relayed(self, ins, outs, sems):
        cp = self._plan(ins, outs, sems)
        for a in range(self.n):
            cp[a]["diag0"].wait_recv()
            cp[a]["pass_d0"].start()
            cp[a]["diag1"].wait_recv()
            cp[a]["pass_d1"].start()

    def finish(self, ins, outs, sems):
        if self.pass_on_at is None:
            self.pass_on(ins, outs, sems)
            self.pass_on_relayed(ins, outs, sems)
        cp = self._plan(ins, outs, sems)
        for a in range(self.n):
            for c in cp[a]["from_sib"]:
                c.wait_recv()
            for c in cp[a]["own"] + [cp[a][k] for k in ("relay_x", "relay_y", "pass_x", "pass_y", "pass_d0", "pass_d1")]:
                c.wait_send()
        for c in cp["mine"]:
            c.wait()


class _Scatter:
    def __init__(self, grads):
        self.grads = list(grads)
        self.n = len(self.grads)

    def inputs(self):
        return self.grads

    def out_shape(self):
        return [jax.ShapeDtypeStruct((N_DEV - 1,) + g.shape[1:], g.dtype) for g in self.grads]

    def scratch(self):
        return [pltpu.SemaphoreType.DMA((7 * self.n,)), pltpu.SemaphoreType.DMA((7 * self.n,))]

    def _plan(self, ins, outs, sems):
        send_sems, recv_sems = sems
        x, y, c, _ = _position()
        cps = []
        for a in range(self.n):
            for k in range(1, N_DEV):
                peer = (x ^ (k >> 2), y ^ ((k >> 1) & 1), c ^ (k & 1))
                cps.append(pltpu.make_async_remote_copy(
                    src_ref=ins[a].at[_slot(peer)], dst_ref=outs[a].at[k - 1],
                    send_sem=send_sems.at[7 * a + k - 1], recv_sem=recv_sems.at[7 * a + k - 1],
                    device_id=peer, device_id_type=MESH))
        return cps

    def start(self, ins, outs, sems):
        for cp in self._plan(ins, outs, sems):
            cp.start()

    def finish(self, ins, outs, sems):
        for cp in self._plan(ins, outs, sems):
            cp.wait()


def _hosted(inner, n_in, n_out, comm, grid):
    if comm is None:
        return inner
    nc_in, nc_out, ns = len(comm.inputs()), len(comm.out_shape()), len(comm.scratch())

    def body(*refs):
        o0 = n_in + nc_in
        s0 = o0 + n_out + nc_out
        main = refs[:n_in] + refs[o0:o0 + n_out] + refs[s0:len(refs) - ns]
        c_in, c_out, c_sems = refs[n_in:o0], refs[o0 + n_out:s0], refs[len(refs) - ns:]
        ids = [pl.program_id(ax) for ax in range(len(grid))]
        first = functools.reduce(jnp.logical_and, [i == 0 for i in ids])
        last = functools.reduce(jnp.logical_and, [i == g - 1 for i, g in zip(ids, grid)])

        @pl.when(first)
        def _():
            comm.start(c_in, c_out, c_sems)

        inner(*main)

        if getattr(comm, "pass_on_at", None) is not None:
            assert len(grid) == 1
            first_at, second_at = (min(grid[0] - 1, int(frac * grid[0])) for frac in comm.pass_on_at)
            assert first_at < second_at

            @pl.when(ids[0] == first_at)
            def _():
                comm.pass_on(c_in, c_out, c_sems)

            @pl.when(ids[0] == second_at)
            def _():
                comm.pass_on_relayed(c_in, c_out, c_sems)

        @pl.when(last)
        def _():
            comm.finish(c_in, c_out, c_sems)

    return body


def _call(inner, name, grid, in_specs, out_specs, out_shape, scratch, args, comm=None):
    n_in, n_out = len(args), len(out_shape)
    body = _hosted(inner, n_in, n_out, comm, grid)
    if comm is not None:
        in_specs = list(in_specs) + [_any()] * len(comm.inputs())
        args = list(args) + comm.inputs()
        out_specs = list(out_specs) + [_any()] * len(comm.out_shape())
        out_shape = list(out_shape) + comm.out_shape()
        scratch = list(scratch) + comm.scratch()
    outs = pl.pallas_call(
        body, name=name, grid=grid, in_specs=list(in_specs), out_specs=list(out_specs), out_shape=list(out_shape),
        scratch_shapes=list(scratch), compiler_params=_params(len(grid)))(*args)
    return list(outs[:n_out]), list(outs[n_out:])


def _first_gather(shards, small_idx, x2, t2, n_meta, tp):
    comm = _Gather(shards)
    n = comm.n
    seq, d = x2.shape
    t_real = n_meta + seq
    n_pad = tp - t_real
    cw = d // N_DEV
    rows = STAGE_ROWS if seq % STAGE_ROWS == 0 else seq
    n_chunks = seq // rows

    def body(*refs):
        ins, (x_ref, t_ref) = refs[:n], refs[n:n + 2]
        outs, (h0_ref, tg_ref) = refs[n + 2:2 * n + 2], refs[2 * n + 2:2 * n + 4]
        sems = refs[2 * n + 4:2 * n + 7]
        buf, zeros, in_sems, out_sems, misc_sems = refs[2 * n + 7:]
        comm.start(ins, outs, sems)
        zeros[...] = jnp.zeros_like(zeros)
        fills = [pltpu.make_async_copy(zeros.at[pl.ds(0, n_pad)], h0_ref.at[pl.ds(t_real, n_pad)], misc_sems.at[0]),
                 pltpu.make_async_copy(zeros.at[pl.ds(0, n_pad)], tg_ref.at[pl.ds(t_real, n_pad)], misc_sems.at[1]),
                 pltpu.make_async_copy(zeros.at[pl.ds(0, n_meta)], tg_ref.at[pl.ds(0, n_meta)], misc_sems.at[2])]
        for cp in fills:
            cp.start()
        jobs = [(src, dst, c) for src, dst in ((x_ref, h0_ref), (t_ref, tg_ref)) for c in range(n_chunks)]

        def load(k):
            src, _, c = jobs[k]
            return pltpu.make_async_copy(src.at[pl.ds(c * rows, rows)], buf.at[k % 2], in_sems.at[k % 2])

        def store(k):
            _, dst, c = jobs[k]
            return pltpu.make_async_copy(buf.at[k % 2], dst.at[pl.ds(n_meta + c * rows, rows)], out_sems.at[k % 2])

        load(0).start()
        for k in range(len(jobs)):
            load(k).wait()
            if k + 1 < len(jobs):
                if k >= 1:
                    store(k - 1).wait()
                load(k + 1).start()
            store(k).start()
        for k in range(max(0, len(jobs) - 2), len(jobs)):
            store(k).wait()
        comm.finish(ins, outs, sems)
        meta = [pltpu.make_async_copy(outs[small_idx].at[k, pl.ds(0, n_meta)],
                                      h0_ref.at[pl.ds(0, n_meta), pl.ds(k * cw, cw)], misc_sems.at[3 + k])
                for k in range(N_DEV)]
        for cp in meta:
            cp.start()
        for cp in fills + meta:
            cp.wait()

    staged = [jax.ShapeDtypeStruct((tp, d), F32)] * 2
    outs = pl.pallas_call(
        body, name="weights_all_gather", out_shape=comm.out_shape() + staged,
        in_specs=[_any()] * (n + 2), out_specs=[_any()] * (n + 2),
        scratch_shapes=comm.scratch() + [
            pltpu.VMEM((2, rows, d), F32), pltpu.VMEM((max(n_pad, n_meta), d), F32),
            pltpu.SemaphoreType.DMA((2,)), pltpu.SemaphoreType.DMA((2,)), pltpu.SemaphoreType.DMA((3 + N_DEV,))],
        compiler_params=pltpu.CompilerParams(vmem_limit_bytes=V7X_VMEM_LIMIT),
    )(*shards, x2, t2)
    return outs[:n], outs[n], outs[n + 1]


def _chip_copies(c_ref, land_ref, sems):
    _, _, c, chips = _position()
    return [pltpu.make_async_remote_copy(
        src_ref=c_ref.at[2 * cx + cy], dst_ref=land_ref.at[j], send_sem=sems[j], recv_sem=sems[3 + j],
        device_id=(cx, cy, c), device_id_type=MESH) for j, (cx, cy) in enumerate(chips)]


def _scatter_copies(g_ref, land_ref, sems):
    x, y, c, _ = _position()
    cps = []
    for k in range(1, N_DEV):
        peer = (x ^ (k >> 2), y ^ ((k >> 1) & 1), c ^ (k & 1))
        cps.append(pltpu.make_async_remote_copy(
            src_ref=g_ref.at[_slot(peer)], dst_ref=land_ref.at[k - 1], send_sem=sems[k - 1],
            recv_sem=sems[N_DEV - 1 + k - 1], device_id=peer, device_id_type=MESH))
    return cps


def _bcast_copies(b_ref, land_ref, sems):
    x, y, c, _ = _position()
    mine = land_ref.at[_slot((x, y, c))]
    cps = []
    for k in range(1, N_DEV):
        peer = (x ^ (k >> 2), y ^ ((k >> 1) & 1), c ^ (k & 1))
        cps.append(pltpu.make_async_remote_copy(
            src_ref=b_ref, dst_ref=mine, send_sem=sems[k - 1], recv_sem=sems[N_DEV - 1 + k - 1],
            device_id=peer, device_id_type=MESH))
    return cps + [pltpu.make_async_copy(b_ref, mine, sems[2 * (N_DEV - 1)])]


def _exchanges_start(name, parts):
    hbm = pl.BlockSpec(memory_space=pltpu.HBM)
    sem = pl.BlockSpec(memory_space=pltpu.SEMAPHORE)
    n_parts = len(parts)
    total = sum(n for _, n, _, _ in parts)

    def body(*refs):
        ins, sems, token = refs[:2 * n_parts], refs[2 * n_parts:2 * n_parts + total], refs[4 * n_parts + total]
        at = 0
        for j, (copies, n, _, _) in enumerate(parts):
            for cp in copies(ins[2 * j], ins[2 * j + 1], sems[at:at + n]):
                cp.start()
            at += n
        token[...] = jnp.zeros_like(token)

    flat = []
    for _, _, src, land_shape in parts:
        flat += [pltpu.with_memory_space_constraint(src, pltpu.HBM),
                 pltpu.with_memory_space_constraint(lax.empty(land_shape, src.dtype), pltpu.HBM)]
    outs = pl.pallas_call(
        body, name=name + "_start",
        out_shape=(pltpu.SemaphoreType.DMA(()),) * total + tuple(pltpu.HBM(a.shape, a.dtype) for a in flat)
        + (jax.ShapeDtypeStruct((SUBLANES, 128), F32),),
        in_specs=(hbm,) * len(flat),
        out_specs=(sem,) * total + (hbm,) * len(flat) + (pl.BlockSpec(memory_space=pltpu.VMEM),),
        input_output_aliases={j: total + j for j in range(len(flat))},
        compiler_params=pltpu.CompilerParams(has_side_effects=pltpu.SideEffectType.DATAFLOW_SIDE_EFFECTING),
    )(*flat)
    res, at = [], 0
    for j, (_, n, _, _) in enumerate(parts):
        res.append((outs[at:at + n], outs[total + 2 * j], outs[total + 2 * j + 1]))
        at += n
    return res, outs[total + len(flat)]


def _exchange_start(name, copies, n_copies, src):
    (part,), token = _exchanges_start(name, [(copies, 2 * n_copies, src, (n_copies,) + src.shape[1:])])
    return (*part, token)


def _exchange_wait(name, copies, sems, src_thru, land_thru, after):
    hbm = pl.BlockSpec(memory_space=pltpu.HBM)
    sem = pl.BlockSpec(memory_space=pltpu.SEMAPHORE)
    n_sems = len(sems)

    def body(s_ref, land_ref, *refs):
        for cp in copies(s_ref, land_ref, refs[:n_sems]):
            cp.wait()

    return pl.pallas_call(
        body, name=name + "_wait",
        out_shape=(pltpu.HBM(src_thru.shape, src_thru.dtype), pltpu.HBM(land_thru.shape, land_thru.dtype)),
        in_specs=(hbm, hbm) + (sem,) * n_sems + (pl.BlockSpec(memory_space=pl.ANY),), out_specs=(hbm, hbm),
        input_output_aliases={0: 0, 1: 1},
        compiler_params=pltpu.CompilerParams(has_side_effects=pltpu.SideEffectType.DATAFLOW_SIDE_EFFECTING),
    )(src_thru, land_thru, *sems, after)


def _pair_reduce(grad, core):
    blk = grad.shape[2:]
    zeros = (0,) * len(blk)

    def body(core_ref, g_hbm, own_ref, o_ref, landed, send_sems, recv_sems):
        del core_ref
        i = pl.program_id(0)
        x, y, c, _ = _position()

        def copy(k):
            return pltpu.make_async_remote_copy(
                src_ref=g_hbm.at[k, 1 - c], dst_ref=landed.at[k], send_sem=send_sems.at[k],
                recv_sem=recv_sems.at[k], device_id=(x, y, 1 - c), device_id_type=MESH)

        @pl.when(i == 0)
        def _():
            for k in range(4):
                copy(k).start()

        for k in range(4):
            @pl.when(i == k)
            def _(k=k):
                copy(k).wait_recv()

        o_ref[...] = (own_ref[...].astype(F32) + landed[i].astype(F32)).astype(BF16)

        @pl.when(i == 3)
        def _():
            for k in range(4):
                copy(k).wait_send()

    return pl.pallas_call(
        body, name="grads_pair_reduce",
        out_shape=jax.ShapeDtypeStruct((4,) + blk, BF16),
        grid_spec=pltpu.PrefetchScalarGridSpec(
            num_scalar_prefetch=1, grid=(4,),
            in_specs=[_any(), pl.BlockSpec((None, None) + blk, lambda i, cr: (i, cr[0]) + zeros)],
            out_specs=pl.BlockSpec((None,) + blk, lambda i, cr: (i,) + zeros),
            scratch_shapes=[pltpu.VMEM((4,) + blk, BF16), pltpu.SemaphoreType.DMA((4,)),
                            pltpu.SemaphoreType.DMA((4,))]),
        compiler_params=_params(1),
    )(core, grad, grad)


def _adamw(w, g, m, v):
    m2 = ADAM_B1 * m + (1.0 - ADAM_B1) * g
    v2 = ADAM_B2 * v + (1.0 - ADAM_B2) * (g * g)
    m_hat = m2 / (1.0 - ADAM_B1 ** ADAM_STEP)
    v_hat = v2 / (1.0 - ADAM_B2 ** ADAM_STEP)
    delta = -ADAM_LR * (m_hat / (jnp.sqrt(v_hat) + ADAM_EPS) + ADAM_WD * w)
    return delta, m2, v2


def _final_adamw(own, recv, idx, parts, after):
    blk = own.shape[1:]
    n_recv = recv.shape[0]
    n_parts = len(parts)
    per = blk[0] // n_parts if n_parts > 1 else None
    rows = blk[-2]
    n_chunks = 1 if n_parts > 1 else (4 if rows % 64 == 0 and rows >= 512 else (2 if rows % 32 == 0 else 1))
    cblk = blk[:-2] + (rows // n_chunks, blk[-1])
    lead = (0,) * (len(blk) - 2)

    def body(idx_ref, c_ref, r_ref, after_ref, *refs):
        del idx_ref, after_ref
        ins, outs = refs[:3 * n_parts], refs[3 * n_parts:]
        g = c_ref[...].astype(F32)
        for k in range(n_recv):
            g = g + r_ref[k].astype(F32)
        for p in range(n_parts):
            w_ref, m_ref, v_ref = ins[3 * p:3 * p + 3]
            if n_parts == 1:
                gp = g
            elif per == 1:
                gp = g[p]
            else:
                gp = g[p * per:(p + 1) * per]
            delta, m2, v2 = _adamw(w_ref[0], gp, m_ref[0], v_ref[0])
            o = outs[4 * p:4 * p + 4]
            o[0][0] = gp
            o[1][0] = delta
            o[2][0] = m2
            o[3][0] = v2

    flat = [a for wmv in parts for a in wmv]

    def part_spec(a):
        shape = a.shape[:-2] + (a.shape[-2] // n_chunks, a.shape[-1])
        return pl.BlockSpec(shape, lambda i, cr, nd=a.ndim: (0,) * (nd - 2) + (i, 0))

    outs = pl.pallas_call(
        body, name="grads_sum_adamw",
        out_shape=[jax.ShapeDtypeStruct(wmv[0].shape, F32) for wmv in parts for _ in range(4)],
        grid_spec=pltpu.PrefetchScalarGridSpec(
            num_scalar_prefetch=1, grid=(n_chunks,),
            in_specs=[pl.BlockSpec((None,) + cblk, lambda i, cr: (cr[0],) + lead + (i, 0)),
                      pl.BlockSpec((n_recv,) + cblk, lambda i, cr: (0,) + lead + (i, 0))]
                     + [_any()] + [part_spec(a) for a in flat],
            out_specs=[part_spec(wmv[0]) for wmv in parts for _ in range(4)]),
        compiler_params=_params(1),
    )(idx, own, recv, after, *flat)
    return [tuple(outs[4 * p:4 * p + 4]) for p in range(n_parts)]


def _small_adamw(partials, layout, me_index, after):
    d = partials[0].shape[-1]
    rows = sum(p.shape[1] for p in partials)
    n = len(layout)
    n_p = len(partials)
    cw = d // N_DEV

    def body(me_ref, *refs):
        p_refs, refs = refs[:n_p], refs[n_p + 1:]
        ins, t_ref, outs = refs[:3 * n], refs[3 * n], refs[3 * n + 1:]
        me = me_ref[0]
        row = 0
        for p_ref in p_refs:
            total = p_ref[0]
            for j in range(1, N_DEV):
                total = total + p_ref[j]
            t_ref[row:row + p_ref.shape[1], :] = total
            row += p_ref.shape[1]
        for e, (kind, r0, nr, _, _, _) in enumerate(layout):
            w_ref, m_ref, v_ref = ins[3 * e:3 * e + 3]
            o = outs[4 * e:4 * e + 4]
            if kind == "rep":
                g = t_ref[r0:r0 + nr, :]
                delta, m2, v2 = _adamw(w_ref[...], g, m_ref[...], v_ref[...])
                for ref, val in zip(o, (g, delta, m2, v2)):
                    ref[...] = val
            elif kind == "wide":
                for q in range(nr):
                    sl = slice(q * d, (q + 1) * d)
                    g = t_ref[r0 + q:r0 + q + 1, :]
                    delta, m2, v2 = _adamw(w_ref[:, sl], g, m_ref[:, sl], v_ref[:, sl])
                    for ref, val in zip(o, (g, delta, m2, v2)):
                        ref[:, sl] = val
            else:
                for j in range(N_DEV):
                    @pl.when(me == j)
                    def _(j=j, o=o, w_ref=w_ref, m_ref=m_ref, v_ref=v_ref, r0=r0, nr=nr):
                        g = t_ref[r0:r0 + nr, j * cw:(j + 1) * cw]
                        delta, m2, v2 = _adamw(w_ref[...], g, m_ref[...], v_ref[...])
                        for ref, val in zip(o, (g, delta, m2, v2)):
                            ref[...] = val

    flat = [a for ent in layout for a in ent[3:]]
    vm = pl.BlockSpec(memory_space=pltpu.VMEM)
    outs = pl.pallas_call(
        body, name="small_adamw",
        out_shape=[jax.ShapeDtypeStruct((rows, d), F32)]
                  + [jax.ShapeDtypeStruct(ent[3].shape, F32) for ent in layout for _ in range(4)],
        in_specs=[pl.BlockSpec(memory_space=pltpu.SMEM)] + [vm] * n_p + [_any()] + [vm] * len(flat),
        out_specs=[vm] * (1 + 4 * n),
        compiler_params=pltpu.CompilerParams(vmem_limit_bytes=V7X_VMEM_LIMIT),
    )(me_index, *partials, after, *flat)
    return outs[0], [tuple(outs[1 + 4 * e:5 + 4 * e]) for e in range(n)]


def _ffn_fwd(h, g, wgu, wd, tm, loss=None, comm=None):
    tp, d = h.shape
    f = wd.shape[0]
    fc = f // FFN_FWD_CHUNKS
    nt = tp // tm
    with_loss = loss is not None
    if with_loss:
        tgt, gf, n_meta, t_real = loss

    def body(*refs):
        if with_loss:
            (h_ref, g_ref, wgu_hbm, wd_hbm, tgt_ref, gf_ref, out_ref, gu_ref, n_ref, tail_ref,
             wgu_v, wd_v, sems) = refs
        else:
            h_ref, g_ref, wgu_hbm, wd_hbm, out_ref, gu_ref, n_ref, wgu_v, wd_v, sems = refs
        i = pl.program_id(0)

        @pl.when(i == 0)
        def _():
            _load_weights([(wgu_hbm, wgu_v), (wd_hbm, wd_v)], sems)
            if with_loss:
                tail_ref[...] = jnp.zeros_like(tail_ref)

        x = h_ref[...]
        n, _ = _rms_fwd(x, g_ref[...])
        nb = n.astype(BF16)
        n_ref[...] = nb
        acc = jnp.zeros((tm, d), F32)
        for j in range(FFN_FWD_CHUNKS):
            cols = slice(j * fc, (j + 1) * fc)
            gate = _nt(nb, wgu_v[pl.ds(j * fc, fc), :])
            up = _nt(nb, wgu_v[pl.ds(f + j * fc, fc), :])
            gu_ref[0, :, cols] = gate.astype(BF16)
            gu_ref[1, :, cols] = up.astype(BF16)
            act = (gate * _sigmoid(gate) * up).astype(BF16)
            acc = acc + _nn(act, wd_v[pl.ds(j * fc, fc), :])
        hn = x + FFN_RES * acc
        if not with_loss:
            out_ref[...] = hn
        else:
            gfv = gf_ref[...]
            r = lax.rsqrt(jnp.mean(hn * hn, axis=-1, keepdims=True) + EPS)
            xr = hn * r
            rows = i * tm + lax.broadcasted_iota(jnp.int32, (tm, 1), 0)
            mask = jnp.logical_and(rows >= n_meta, rows < t_real)
            diff = jnp.where(mask, xr * gfv - tgt_ref[...], 0.0)
            tail_ref[TAIL_LOSS:TAIL_LOSS + 1, :] += jnp.zeros((1, d), F32) + 0.5 * jnp.sum(diff * diff) / d
            dy = diff / d
            gy = dy * gfv
            out_ref[...] = r * (gy - xr * jnp.mean(gy * xr, axis=-1, keepdims=True))
            tail_ref[TAIL_FINAL:TAIL_FINAL + 1, :] += _rowsum(dy * xr)

    row = pl.BlockSpec((tm, d), lambda i: (i, 0))
    vec = pl.BlockSpec((1, d), lambda i: (0, 0))
    in_specs = [row, vec, _any(), _any()]
    out_shape = [jax.ShapeDtypeStruct((tp, d), F32), jax.ShapeDtypeStruct((2, tp, f), BF16),
                 jax.ShapeDtypeStruct((tp, d), BF16)]
    out_specs = [row, pl.BlockSpec((2, tm, f), lambda i: (0, i, 0)), row]
    args = [h, g, wgu, wd]
    if with_loss:
        in_specs += [row, vec]
        out_shape += [jax.ShapeDtypeStruct((SUBLANES, d), F32)]
        out_specs += [pl.BlockSpec((SUBLANES, d), lambda i: (0, 0))]
        args += [tgt, gf]
    return _call(body, "ffn_fwd_loss" if with_loss else "ffn_fwd", (nt,), in_specs, out_specs, out_shape,
                 [pltpu.VMEM((2 * f, d), BF16), pltpu.VMEM((f, d), BF16), pltpu.SemaphoreType.DMA((2,))],
                 args, comm)


def _ffn_bwd(dh, h, gu, g, wgu, wd, tm, tail, tail_row, after):
    tp, d = h.shape
    f = wd.shape[0]
    fc = f // FFN_CHUNKS
    nt = tp // tm

    def body(dh_ref, h_ref, gu_ref, g_ref, tail_ref, wgu_hbm, wd_hbm, after_ref,
             dhin_ref, dgu_ref, act_ref, df_ref, dg_ref, wgu_v, wd_v, dn_v, sems):
        del after_ref
        i, j = pl.program_id(0), pl.program_id(1)

        @pl.when(jnp.logical_and(i == 0, j == 0))
        def _():
            _load_weights([(wgu_hbm, wgu_v), (wd_hbm, wd_v)], sems)
            dg_ref[...] = tail_ref[...]

        dfb = (FFN_RES * dh_ref[...]).astype(BF16)

        @pl.when(j == 0)
        def _():
            df_ref[...] = dfb
            dn_v[...] = jnp.zeros_like(dn_v)

        lo = pl.multiple_of(j * fc, 16)
        dact = _nt(dfb, wd_v[pl.ds(lo, fc), :])
        gate = gu_ref[0].astype(F32)
        up = gu_ref[1].astype(F32)
        sg = _sigmoid(gate)
        silu = gate * sg
        act_ref[...] = (silu * up).astype(BF16)
        dgate = (dact * up * (sg * (1.0 + gate * (1.0 - sg)))).astype(BF16)
        dup = (dact * silu).astype(BF16)
        dgu_ref[0] = dgate
        dgu_ref[1] = dup
        dn_v[...] += _nn(dgate, wgu_v[pl.ds(lo, fc), :]) + _nn(dup, wgu_v[pl.ds(pl.multiple_of(f + j * fc, 16), fc), :])

        @pl.when(j == FFN_CHUNKS - 1)
        def _():
            x = h_ref[...]
            r = lax.rsqrt(jnp.mean(x * x, axis=-1, keepdims=True) + EPS)
            dx, dgp = _rms_bwd(dn_v[...], x, r, g_ref[...])
            dhin_ref[...] = dh_ref[...] + dx
            dg_ref[tail_row:tail_row + 1, :] += dgp

    row = pl.BlockSpec((tm, d), lambda i, j: (i, 0))
    vec = pl.BlockSpec((1, d), lambda i, j: (0, 0))
    tile = pl.BlockSpec((SUBLANES, d), lambda i, j: (0, 0))
    hid2 = pl.BlockSpec((2, tm, fc), lambda i, j: (0, i, j))
    return _call(
        body, "ffn_bwd", (nt, FFN_CHUNKS),
        [row, row, hid2, vec, tile, _any(), _any(), _any()],
        [row, hid2, pl.BlockSpec((tm, fc), lambda i, j: (i, j)), row, tile],
        [jax.ShapeDtypeStruct((tp, d), F32), jax.ShapeDtypeStruct((2, tp, f), BF16),
         jax.ShapeDtypeStruct((tp, f), BF16), jax.ShapeDtypeStruct((tp, d), BF16),
         jax.ShapeDtypeStruct((SUBLANES, d), F32)],
        [pltpu.VMEM((2 * f, d), BF16), pltpu.VMEM((f, d), BF16), pltpu.VMEM((tm, d), F32),
         pltpu.SemaphoreType.DMA((2,))],
        [dh, h, gu, g, tail, wgu, wd, after])


def _piece_segments(q, d, nb_cols):
    segs = []
    for j in range(N_DEV):
        lo, hi = max(q * d, j * nb_cols), min((q + 1) * d, (j + 1) * nb_cols)
        if lo < hi:
            segs.append((j, lo - q * d, hi - q * d, lo - j * nb_cols, hi - j * nb_cols))
    return segs


def _w3_copies(w3_hbm, rows, w3_v):
    return [(w3_hbm.at[k, pl.ds(q * rows, rows)], w3_v.at[q, pl.ds(k * rows, rows)])
            for q in range(3) for k in range(N_DEV)]


def _gates(xrb, wg_ref, ba, bx, lam, hd):
    pre_r, pre_i = [], []
    for hh in range(N_HEADS):
        xh = xrb[:, hh * hd:(hh + 1) * hd]
        pre_r.append(_nn(xh, wg_ref[0, hh]))
        pre_i.append(_nn(xh, wg_ref[1, hh]))
    r = _sigmoid(jnp.concatenate(pre_r, axis=1) + ba)
    ig = _sigmoid(jnp.concatenate(pre_i, axis=1) + bx)
    sp = _softplus(-lam)
    log_a = -RG_LRU_C * r * sp
    a = jnp.exp(log_a)
    s = jnp.sqrt(_one_minus_exp(2.0 * log_a))
    return r, ig, sp, a, s


def _scan_fwd(a, u, h_prev):
    tm = a.shape[0]
    rows = lax.broadcasted_iota(jnp.int32, a.shape, 0)
    d = 1
    while d < tm:
        if d < SUBLANES:
            keep = rows >= d
            u = jnp.where(keep, a * pltpu.roll(u, d, 0) + u, u)
            a = jnp.where(keep, a * pltpu.roll(a, d, 0), a)
        else:
            u = jnp.concatenate([u[:d], a[d:] * u[:tm - d] + u[d:]], axis=0)
            a = jnp.concatenate([a[:d], a[d:] * a[:tm - d]], axis=0)
        d *= 2
    return u + a * h_prev


def _scan_bwd(b, v, g_next):
    tm = b.shape[0]
    rows = lax.broadcasted_iota(jnp.int32, b.shape, 0)
    d = 1
    while d < tm:
        if d < SUBLANES:
            keep = rows < tm - d
            v = jnp.where(keep, v + b * pltpu.roll(v, tm - d, 0), v)
            b = jnp.where(keep, b * pltpu.roll(b, tm - d, 0), b)
        else:
            v = jnp.concatenate([v[:tm - d] + b[:tm - d] * v[d:], v[tm - d:]], axis=0)
            b = jnp.concatenate([b[:tm - d] * b[d:], b[tm - d:]], axis=0)
        d *= 2
    return v + b * g_next


def _shifted_copies(ext_ref, es_ref, n_rows):
    for s in range(1, SUBLANES):
        es_ref[s, pl.ds(0, n_rows), :] = ext_ref[pl.ds(s, n_rows), :]


def _tap(ext_ref, es_ref, off, tm):
    q, s = divmod(off, SUBLANES)
    if s == 0:
        return ext_ref[pl.ds(SUBLANES * q, tm), :]
    return es_ref[s, pl.ds(SUBLANES * q, tm), :]


def _mixer_fwd(h, g, b_in, win_all, cw4, cb4, wg, ba, bx, lam, cw31, cb31, lng, lnb, bcp, w3_all, tm, comm=None):
    tp, d = h.shape
    nb_cols = win_all.shape[-1]
    n_in = N_DEV * nb_cols
    hd = wg.shape[-1]
    k4, k31 = cw4.shape[0], cw31.shape[0]
    w3_rows = d // N_DEV

    def body(h_ref, g_ref, b_ref, win_hbm, cw4_ref, cb4_ref, wg_ref, ba_ref, bx_ref, lam_ref, cw31_ref, cb31_ref,
             lng_ref, lnb_ref, bcp_ref, w3_hbm,
             h2_ref, p_ref, n_ref, xr_ref, hs_ref, v1_ref, ya_ref, yb_ref,
             win_v, w3_v, ext4, ext31, es31, hcar, sems):
        @pl.when(pl.program_id(0) == 0)
        def _():
            _load_weights([(win_hbm, win_v)] + _w3_copies(w3_hbm, w3_rows, w3_v), sems)
            ext4[pl.ds(0, CONV4_HALO), :] = jnp.zeros((CONV4_HALO, d), F32)
            ext31[pl.ds(0, CONV31_HALO), :] = jnp.zeros((CONV31_HALO, d), F32)
            hcar[...] = jnp.zeros_like(hcar)

        n, _ = _rms_fwd(h_ref[...], g_ref[...])
        nb = n.astype(BF16)
        n_ref[...] = nb

        def piece(q):
            parts = [_nn(nb, win_v[j, :, bl:bh]) for j, _, _, bl, bh in _piece_segments(q, d, nb_cols)]
            pq = (jnp.concatenate(parts, axis=1) + b_ref[:, q * d:(q + 1) * d]).astype(BF16)
            p_ref[:, q * d:(q + 1) * d] = pq
            return pq.astype(F32)

        x_rnn, y_rnn, glu_v, glu_g, gate_a, gate_b = [piece(q) for q in range(6)]

        ext4[pl.ds(CONV4_HALO, tm), :] = x_rnn
        xr = cb4_ref[...] + jnp.zeros((tm, d), F32)
        for k in range(k4):
            xr = xr + cw4_ref[k:k + 1, :] * ext4[pl.ds(CONV4_HALO - (k4 - 1) + k, tm), :]
        ext4[pl.ds(0, CONV4_HALO), :] = ext4[pl.ds(tm, CONV4_HALO), :]
        xrb = xr.astype(BF16)
        xr_ref[...] = xrb
        xr = xrb.astype(F32)
        _, ig, _, a, s = _gates(xrb, wg_ref, ba_ref[...], bx_ref[...], lam_ref[...], hd)
        hseq = _scan_fwd(a, s * (ig * xr), hcar[0:1, :])
        hcar[0:1, :] = hseq[tm - 1:tm, :]
        hs_ref[...] = hseq.astype(BF16)
        gl, _ = _gelu(y_rnn)
        ya = _nn((hseq * gl).astype(BF16), w3_v[0])
        ya_ref[...] = ya.astype(BF16)

        ext31[pl.ds(CONV31_HALO, tm), :] = glu_v * _sigmoid(glu_g)
        _shifted_copies(ext31, es31, tm + CONV31_HALO - SUBLANES)
        v1 = cb31_ref[...] + jnp.zeros((tm, d), F32)
        for k in range(k31):
            v1 = v1 + cw31_ref[k:k + 1, :] * _tap(ext31, es31, CONV31_HALO - (k31 - 1) + k, tm)
        ext31[pl.ds(0, CONV31_HALO), :] = ext31[pl.ds(tm, CONV31_HALO), :]
        v1b = v1.astype(BF16)
        v1_ref[...] = v1b
        v1 = v1b.astype(F32)
        xc = v1 - jnp.mean(v1, axis=-1, keepdims=True)
        rstd = lax.rsqrt(jnp.mean(xc * xc, axis=-1, keepdims=True) + EPS)
        v2 = xc * rstd * lng_ref[...] + lnb_ref[...]
        yb = _nn((v2 * _sigmoid(v2)).astype(BF16), w3_v[1]) + bcp_ref[...]
        yb_ref[...] = yb.astype(BF16)

        merged = _sigmoid(gate_a) * ya + _sigmoid(gate_b) * yb
        h2_ref[...] = h_ref[...] + _nn(merged.astype(BF16), w3_v[2])

    row = pl.BlockSpec((tm, d), lambda i: (i, 0))
    wide = pl.BlockSpec((tm, n_in), lambda i: (i, 0))
    full = lambda a: pl.BlockSpec(a.shape, lambda i, nd=a.ndim: (0,) * nd)
    smalls = [cw4, cb4, wg, ba, bx, lam, cw31, cb31, lng, lnb, bcp]
    return _call(
        body, "mixer_fwd", (tp // tm,),
        [row, full(g), full(b_in), _any()] + [full(a) for a in smalls] + [_any()],
        [row, wide] + [row] * 6,
        [jax.ShapeDtypeStruct((tp, d), F32), jax.ShapeDtypeStruct((tp, n_in), BF16)]
        + [jax.ShapeDtypeStruct((tp, d), BF16)] * 6,
        [pltpu.VMEM(win_all.shape, BF16),
         pltpu.VMEM((3, d, d), BF16),
         pltpu.VMEM((tm + CONV4_HALO, d), F32),
         pltpu.VMEM((tm + CONV31_HALO, d), F32),
         pltpu.VMEM((SUBLANES, tm + CONV31_HALO, d), F32),
         pltpu.VMEM((SUBLANES, d), F32),
         pltpu.SemaphoreType.DMA((1 + 3 * N_DEV,))],
        [h, g, b_in, win_all, *smalls, w3_all], comm)


SG_BIN, SG_CW4, SG_CB4, SG_BA, SG_BX, SG_LAM, SG_CB31, SG_LNG, SG_LNB, SG_BCP, SG_MIX, SG_CW31 = 0, 6, 10, 11, 12, 13, 14, 15, 16, 17, 18, 19


def _mixer_bwd(dh2, h, g, proj, xr_s, hs_s, v1_s, ya_s, yb_s, win_t, cw4, wg, ba, bx, lam, cw31, lng, lnb, w3_all, tm,
               comm=None):
    tp, d = dh2.shape
    n_in = proj.shape[1]
    hd = wg.shape[-1]
    k4, k31 = cw4.shape[0], cw31.shape[0]
    nt = tp // tm
    w3_rows = d // N_DEV
    sg_rows = -(-(SG_CW31 + k31) // SUBLANES) * SUBLANES
    halo_rows = 16
    per = tm // halo_rows

    def body(dh_ref, h_ref, g_ref, p_ref, xr_ref, hs_ref, hh_ref, v1_ref, ya_ref, yb_ref, win_hbm,
             cw4_ref, wg_ref, wgt_ref, ba_ref, bx_ref, lam_ref, cw31_ref, lng_ref, lnb_ref, w3_hbm,
             dh1_ref, dp_ref, x3_ref, y3_ref, yg_ref, sg_ref,
             win_v, w3_v, extd4, extd31, es31, gcar, sems):
        i = pl.program_id(0)
        tile = nt - 1 - i

        @pl.when(i == 0)
        def _():
            _load_weights([(win_hbm, win_v)] + _w3_copies(w3_hbm, w3_rows, w3_v), sems)
            for q in range(3):
                w3_v[q] = w3_v[q].T
            extd4[pl.ds(tm, CONV4_HALO), :] = jnp.zeros((CONV4_HALO, d), F32)
            extd31[pl.ds(tm, CONV31_HALO), :] = jnp.zeros((CONV31_HALO, d), F32)
            gcar[...] = jnp.zeros_like(gcar)
            sg_ref[...] = jnp.zeros_like(sg_ref)

        def acc(row, val):
            sg_ref[row:row + 1, :] += _rowsum(val)

        rows = lax.broadcasted_iota(jnp.int32, (tm, d), 0)
        x_rnn = p_ref[:, 0:d].astype(F32)
        y_rnn = p_ref[:, d:2 * d].astype(F32)
        glu_v = p_ref[:, 2 * d:3 * d].astype(F32)
        glu_g = p_ref[:, 3 * d:4 * d].astype(F32)
        sga = _sigmoid(p_ref[:, 4 * d:5 * d].astype(F32))
        sgb = _sigmoid(p_ref[:, 5 * d:6 * d].astype(F32))
        ya = ya_ref[...].astype(F32)
        yb = yb_ref[...].astype(F32)

        dmob = dh_ref[...].astype(BF16)
        dmerged = _nn(dmob, w3_v[2])
        x3_ref[:, 0:d] = (sga * ya + sgb * yb).astype(BF16)
        y3_ref[:, 0:d] = dmob
        dya = sga * dmerged
        dyb = sgb * dmerged
        dn_parts = []

        def emit(q, val):
            vb = val.astype(BF16)
            dp_ref[:, q * d:(q + 1) * d] = vb
            acc(SG_BIN + q, val)
            term = _nn(vb, win_v[pl.ds(q * d, d), :])
            dn_parts[:] = [term if not dn_parts else dn_parts[0] + term]

        emit(4, dmerged * ya * sga * (1.0 - sga))
        emit(5, dmerged * yb * sgb * (1.0 - sgb))

        dyab = dya.astype(BF16)
        y3_ref[:, d:2 * d] = dyab
        dza = _nn(dyab, w3_v[0])
        hsv = hs_ref[...].astype(F32)
        gl, th = _gelu(y_rnn)
        x3_ref[:, d:2 * d] = (hsv * gl).astype(BF16)
        emit(1, dza * hsv * _gelu_grad(y_rnn, th))
        dhs = dza * gl
        xrb = xr_ref[...]
        xr = xrb.astype(F32)
        lam_v = lam_ref[...]
        r, ig, sp, a, s = _gates(xrb, wg_ref, ba_ref[...], bx_ref[...], lam_v, hd)
        b = jnp.where(rows == tm - 1, gcar[1:2, :], pltpu.roll(a, tm - 1, 0))
        big_g = _scan_bwd(b, dhs, gcar[0:1, :])
        gcar[0:1, :] = big_g[0:1, :]
        gcar[1:2, :] = a[0:1, :]
        h_before = jnp.where(tile > 0, hh_ref[halo_rows - 1:halo_rows, :].astype(F32), 0.0)
        h_prev = jnp.where(rows == 0, h_before, pltpu.roll(hsv, 1, 0))
        ds = big_g * ig * xr
        dla = big_g * h_prev * a - ds * (a * a) / jnp.maximum(s, 1e-20)
        acc(SG_LAM, dla * r * (RG_LRU_C * _sigmoid(-lam_v)))
        dpr = dla * (-RG_LRU_C * sp) * r * (1.0 - r)
        dpi = big_g * s * xr * ig * (1.0 - ig)
        acc(SG_BA, dpr)
        acc(SG_BX, dpi)
        dprb = dpr.astype(BF16)
        dpib = dpi.astype(BF16)
        yg_ref[:, 0:d] = dprb
        yg_ref[:, d:2 * d] = dpib
        back = []
        for hh in range(N_HEADS):
            sl = slice(hh * hd, (hh + 1) * hd)
            back.append(_nn(dprb[:, sl], wgt_ref[0, hh]) + _nn(dpib[:, sl], wgt_ref[1, hh]))
        dxr = big_g * s * ig + jnp.concatenate(back, axis=1)
        acc(SG_CB4, dxr)
        extd4[pl.ds(0, tm), :] = dxr
        dx_rnn = jnp.zeros((tm, d), F32)
        for k in range(k4):
            term = extd4[pl.ds(k4 - 1 - k, tm), :]
            dx_rnn = dx_rnn + cw4_ref[k:k + 1, :] * term
            acc(SG_CW4 + k, x_rnn * term)
        extd4[pl.ds(tm, CONV4_HALO), :] = extd4[pl.ds(0, CONV4_HALO), :]
        emit(0, dx_rnn)

        dybb = dyb.astype(BF16)
        y3_ref[:, 2 * d:3 * d] = dybb
        acc(SG_BCP, dyb)
        dv3 = _nn(dybb, w3_v[1])
        v1 = v1_ref[...].astype(F32)
        xc = v1 - jnp.mean(v1, axis=-1, keepdims=True)
        rstd = lax.rsqrt(jnp.mean(xc * xc, axis=-1, keepdims=True) + EPS)
        xhat = xc * rstd
        lng_v = lng_ref[...]
        v2 = xhat * lng_v + lnb_ref[...]
        s2 = _sigmoid(v2)
        x3_ref[:, 2 * d:3 * d] = (v2 * s2).astype(BF16)
        dv2 = dv3 * (s2 * (1.0 + v2 * (1.0 - s2)))
        acc(SG_LNG, dv2 * xhat)
        acc(SG_LNB, dv2)
        dxh = dv2 * lng_v
        dv1 = rstd * (dxh - jnp.mean(dxh, axis=-1, keepdims=True)
                      - xhat * jnp.mean(dxh * xhat, axis=-1, keepdims=True))
        acc(SG_CB31, dv1)
        extd31[pl.ds(0, tm), :] = dv1
        _shifted_copies(extd31, es31, tm + CONV31_HALO - SUBLANES)
        sgg = _sigmoid(glu_g)
        v0 = glu_v * sgg
        dv0 = jnp.zeros((tm, d), F32)
        for k in range(k31):
            term = _tap(extd31, es31, k31 - 1 - k, tm)
            dv0 = dv0 + cw31_ref[k:k + 1, :] * term
            acc(SG_CW31 + k, v0 * term)
        extd31[pl.ds(tm, CONV31_HALO), :] = extd31[pl.ds(0, CONV31_HALO), :]
        emit(2, dv0 * sgg)
        emit(3, dv0 * glu_v * sgg * (1.0 - sgg))

        dn = dn_parts[0]
        x = h_ref[...]
        rr = lax.rsqrt(jnp.mean(x * x, axis=-1, keepdims=True) + EPS)
        dx, dgp = _rms_bwd(dn, x, rr, g_ref[...])
        dh1_ref[...] = dh_ref[...] + dx
        sg_ref[SG_MIX:SG_MIX + 1, :] += dgp

    rev = lambda i: (nt - 1 - i, 0)
    row = pl.BlockSpec((tm, d), rev)
    wide = pl.BlockSpec((tm, n_in), rev)
    full = lambda a: pl.BlockSpec(a.shape, lambda i, nd=a.ndim: (0,) * nd)
    halo = pl.BlockSpec((halo_rows, d), lambda i: (jnp.maximum((nt - 1 - i) * per - 1, 0), 0))
    smalls = [cw4, wg, jnp.swapaxes(wg, 2, 3), ba, bx, lam, cw31, lng, lnb]
    return _call(
        body, "mixer_bwd", (nt,),
        [row, row, full(g), wide, row, row, halo, row, row, row, _any()]
        + [full(a) for a in smalls] + [_any()],
        [row, wide, pl.BlockSpec((tm, 3 * d), rev), pl.BlockSpec((tm, 3 * d), rev),
         pl.BlockSpec((tm, 2 * d), rev), pl.BlockSpec((sg_rows, d), lambda i: (0, 0))],
        [jax.ShapeDtypeStruct((tp, d), F32), jax.ShapeDtypeStruct((tp, n_in), BF16),
         jax.ShapeDtypeStruct((tp, 3 * d), BF16), jax.ShapeDtypeStruct((tp, 3 * d), BF16),
         jax.ShapeDtypeStruct((tp, 2 * d), BF16), jax.ShapeDtypeStruct((sg_rows, d), F32)],
        [pltpu.VMEM(win_t.shape, BF16),
         pltpu.VMEM((3, d, d), BF16),
         pltpu.VMEM((tm + CONV4_HALO, d), F32),
         pltpu.VMEM((tm + CONV31_HALO, d), F32),
         pltpu.VMEM((SUBLANES, tm + CONV31_HALO, d), F32),
         pltpu.VMEM((SUBLANES, d), F32),
         pltpu.SemaphoreType.DMA((1 + 3 * N_DEV,))],
        [dh2, h, g, proj, xr_s, hs_s, hs_s, v1_s, ya_s, yb_s, win_t, *smalls, w3_all], comm)


def _tn_matmul(name, x, y, x_spec, y_spec, n_blocks, kb, nb, tm, tp, out_shape, out_spec, out_view, comm=None,
               after=None):
    nt = tp // tm

    def body(x_ref, y_ref, *refs):
        o_ref, acc = refs[-2:]
        i = pl.program_id(1)

        @pl.when(i == 0)
        def _():
            acc[...] = jnp.zeros_like(acc)

        acc[...] += _tn(x_ref[...], y_ref[...])

        @pl.when(i == nt - 1)
        def _():
            o_ref[...] = acc[...].astype(BF16).reshape(out_view)

    follows = [] if after is None else [after]
    outs, extra = _call(body, name, (n_blocks, nt), [x_spec, y_spec] + [_any()] * len(follows), [out_spec],
                        [jax.ShapeDtypeStruct(out_shape, BF16)], [pltpu.VMEM((kb, nb), F32)], [x, y] + follows,
                        comm)
    return outs[0], extra


def kernel(x, meta_tokens, ffn1_norm, ffn1_w_gu, ffn1_w_down, mix_norm, w_in, b_in, rnn_conv_w, rnn_conv_b, rg_w_a, rg_b_a, rg_w_x, rg_b_x, rg_lambda, rnn_w_proj, conv_dw_w, conv_dw_b, conv_ln_g, conv_ln_b, conv_w_proj, conv_b_proj, w_out, ffn2_norm, ffn2_w_gu, ffn2_w_down, final_norm, loss_target, m_meta_tokens, m_ffn1_norm, m_ffn1_w_gu, m_ffn1_w_down, m_mix_norm, m_w_in, m_b_in, m_rnn_conv_w, m_rnn_conv_b, m_rg_w_a, m_rg_b_a, m_rg_w_x, m_rg_b_x, m_rg_lambda, m_rnn_w_proj, m_conv_dw_w, m_conv_dw_b, m_conv_ln_g, m_conv_ln_b, m_conv_w_proj, m_conv_b_proj, m_w_out, m_ffn2_norm, m_ffn2_w_gu, m_ffn2_w_down, m_final_norm, v_meta_tokens, v_ffn1_norm, v_ffn1_w_gu, v_ffn1_w_down, v_mix_norm, v_w_in, v_b_in, v_rnn_conv_w, v_rnn_conv_b, v_rg_w_a, v_rg_b_a, v_rg_w_x, v_rg_b_x, v_rg_lambda, v_rnn_w_proj, v_conv_dw_w, v_conv_dw_b, v_conv_ln_g, v_conv_ln_b, v_conv_w_proj, v_conv_b_proj, v_w_out, v_ffn2_norm, v_ffn2_w_gu, v_ffn2_w_down, v_final_norm):
    w = dict(locals())
    seq, d = x.shape[1], x.shape[2]
    n_meta = meta_tokens.shape[0]
    t_real = n_meta + seq
    tp, tm, tmx_fwd, tmx, tmt, tmw = _tiles(t_real)
    fb = ffn1_w_gu.shape[-1]
    wr = ffn1_w_down.shape[1]
    f = N_DEV * wr
    fc = f // FFN_CHUNKS
    nbc = w_in.shape[-1]
    n_in = N_DEV * nbc
    pr = rnn_w_proj.shape[1]
    hd = rg_w_a.shape[-1]
    gr = rg_w_a.shape[2]
    cw = meta_tokens.shape[1]
    k4, k31 = rnn_conv_w.shape[1], conv_dw_w.shape[1]
    assert n_in == 6 * d and 2 * wr == fb and N_HEADS * hd == d and pr * N_DEV == d

    xi, yi, ci = lax.axis_index("x"), lax.axis_index("y"), lax.axis_index("c")
    core = ci.astype(jnp.int32).reshape(1)
    chip = (2 * xi + yi).astype(jnp.int32).reshape(1)
    me_index = (4 * xi + 2 * yi + ci).astype(jnp.int32).reshape(1)

    for nm in ("ffn1_w_gu", "ffn2_w_gu"):
        for pre in ("", "m_", "v_"):
            w[pre + nm] = jnp.swapaxes(w[pre + nm], 1, 2)

    wgut1 = w["ffn1_w_gu"][0].astype(BF16)
    wgut2 = w["ffn2_w_gu"][0].astype(BF16)
    wd1 = ffn1_w_down[0].astype(BF16)
    wd2 = ffn2_w_down[0].astype(BF16)
    win_loc = w_in[0].astype(BF16)
    win_t_loc = jnp.swapaxes(w_in[0], 0, 1).astype(BF16)
    w3_loc = jnp.concatenate([rnn_w_proj[0], conv_w_proj[0], w_out[0]], axis=0).astype(BF16)
    wg_loc = jnp.stack([rg_w_a[0], rg_w_x[0]]).astype(BF16)
    n_small = n_meta + k4 + k31
    small_rows = -(-n_small // SUBLANES) * SUBLANES
    small_loc = jnp.concatenate([meta_tokens, rnn_conv_w[0], conv_dw_w[0],
                                 jnp.zeros((small_rows - n_small, cw), F32)], axis=0)
    (wgut1_all, wd1_all, small_all), h0, tgt = _first_gather(
        [wgut1, wd1, small_loc], 2, x[0], loss_target[0], n_meta, tp)
    small_full = small_all.transpose(1, 0, 2).reshape(small_rows, d)
    cw4 = small_full[n_meta:n_meta + k4]
    cw31 = small_full[n_meta + k4:n_meta + k4 + k31]

    wgu1, wdn1 = wgut1_all.reshape(2 * f, d), wd1_all.reshape(f, d)
    (h1, gu1, n1), (win_all, w3_all, wg_all) = _ffn_fwd(
        h0, ffn1_norm, wgu1, wdn1, tm, comm=_Gather([win_loc, w3_loc, wg_loc], pass_on_at=(0.65, 0.95)))
    wg = wg_all.transpose(1, 2, 0, 3, 4).reshape(2, N_HEADS, hd, hd)
    (h2, proj, n2, xr_s, hs_s, v1_s, ya_s, yb_s), (wgut2_all, wd2_all, win_t_all) = _mixer_fwd(
        h1, mix_norm, b_in, win_all, cw4, rnn_conv_b, wg, rg_b_a, rg_b_x, rg_lambda, cw31, conv_dw_b, conv_ln_g,
        conv_ln_b, conv_b_proj, w3_all, tmx_fwd, comm=_Gather([wgut2, wd2, win_t_loc], pass_on_at=(0.45, 0.7)))
    wgu2, wdn2 = wgut2_all.reshape(2 * f, d), wd2_all.reshape(f, d)
    win_t = win_t_all.reshape(n_in, d)
    (dh3, gu2, n3, tail), _ = _ffn_fwd(
        h2, ffn2_norm, wgu2, wdn2, tm, loss=(tgt, final_norm.reshape(1, d), n_meta, t_real))

    def d_w_gu(tag, dgu, n_s, after=None):
        g, _ = _tn_matmul(
            "d_w_gu" + tag, dgu, n_s,
            pl.BlockSpec((None, tmt, fc), lambda b, i: (b // FFN_CHUNKS, i, b % FFN_CHUNKS)),
            pl.BlockSpec((tmt, d), lambda b, i: (i, 0)),
            2 * FFN_CHUNKS, fc, d, tmt, tp, (2 * FFN_CHUNKS, fc, d),
            pl.BlockSpec((None, fc, d), lambda b, i: (b, 0, 0)), (fc, d), after=after)
        return g.reshape(N_DEV, fb, d)

    def d_w_down(tag, act, df):
        g, _ = _tn_matmul(
            "d_w_down" + tag, act, df,
            pl.BlockSpec((tmt, fc), lambda b, i: (i, b)), pl.BlockSpec((tmt, d), lambda b, i: (i, 0)),
            FFN_CHUNKS, fc, d, tmt, tp, (FFN_CHUNKS, fc, d),
            pl.BlockSpec((None, fc, d), lambda b, i: (b, 0, 0)), (fc, d))
        return g.reshape(N_DEV, wr, d)

    (dh2, dgu2, act2, df2, tail), _ = _ffn_bwd(dh3, h2, gu2, ffn2_norm, wgu2, wdn2, tm, tail, TAIL_FFN2, n3)
    g_wgu2 = d_w_gu("2", dgu2, n3)
    g_wd2 = d_w_down("2", act2, df2)
    (dh1, dproj, x3, y3, yg, sg), (r_wd2, r_wgu2) = _mixer_bwd(
        dh2, h1, mix_norm, proj, xr_s, hs_s, v1_s, ya_s, yb_s, win_t, cw4, wg, rg_b_a, rg_b_x, rg_lambda, cw31,
        conv_ln_g, conv_ln_b, w3_all, tmx, comm=_Scatter([g_wd2, g_wgu2]))
    g_w3, _ = _tn_matmul(
        "d_w_proj3", x3, y3,
        pl.BlockSpec((tmw, d), lambda b, i: (i, b)), pl.BlockSpec((tmw, d), lambda b, i: (i, b)),
        3, d, d, tmw, tp, (N_DEV, 3, pr, d), pl.BlockSpec((N_DEV, None, pr, d), lambda b, i: (0, b, 0, 0)),
        (N_DEV, pr, d))
    g_wg, _ = _tn_matmul(
        "d_w_gates", xr_s, yg,
        pl.BlockSpec((tmw, hd), lambda b, i: (i, b % N_HEADS)), pl.BlockSpec((tmw, hd), lambda b, i: (i, b)),
        2 * N_HEADS, hd, hd, tmw, tp, (N_DEV, 2 * N_HEADS, gr, hd),
        pl.BlockSpec((N_DEV, None, gr, hd), lambda b, i: (0, b, 0, 0)), (N_DEV, gr, hd))
    ((sg_sems, sg_thru, sg_land), (w3_sems, g_w3_thru, w3_land)), w3_token = _exchanges_start(
        "grads_proj3_exchange",
        [(_bcast_copies, 2 * (N_DEV - 1) + 1, sg, (N_DEV,) + sg.shape),
         (_scatter_copies, 2 * (N_DEV - 1), g_w3, (N_DEV - 1,) + g_w3.shape[1:])])
    g_win, (r_wg,) = _tn_matmul(
        "d_w_in", n2, dproj,
        pl.BlockSpec((tmw, d), lambda b, i: (i, 0)), pl.BlockSpec((tmw, nbc), lambda b, i: (i, b)),
        N_DEV, d, nbc, tmw, tp, (N_DEV, d, nbc), pl.BlockSpec((None, d, nbc), lambda b, i: (b, 0, 0)), (d, nbc),
        comm=_Scatter([g_wg]), after=w3_token)
    win_sems, g_win_thru, win_land, win_token = _exchange_start("grads_w_in_exchange", _scatter_copies, N_DEV - 1, g_win)
    (dh0, dgu1, act1, df1, tail), _ = _ffn_bwd(dh1, h0, gu1, ffn1_norm, wgu1, wdn1, tm, tail, TAIL_FFN1, win_token)

    pieces = [sg, dh0[:n_meta], tail]
    assert all(p.shape[0] % SUBLANES == 0 for p in pieces)
    at = [0, sg.shape[0], sg.shape[0] + n_meta]
    loss_row = at[2] + TAIL_LOSS
    rep_rows = [("ffn1_norm", at[2] + TAIL_FFN1, 1), ("mix_norm", SG_MIX, 1), ("b_in", SG_BIN, 6),
                ("rnn_conv_b", SG_CB4, 1),
                ("rg_b_a", SG_BA, 1), ("rg_b_x", SG_BX, 1), ("rg_lambda", SG_LAM, 1), ("conv_dw_b", SG_CB31, 1),
                ("conv_ln_g", SG_LNG, 1), ("conv_ln_b", SG_LNB, 1), ("conv_b_proj", SG_BCP, 1),
                ("ffn2_norm", at[2] + TAIL_FFN2, 1), ("final_norm", at[2] + TAIL_FINAL, 1)]
    col_rows = [("meta_tokens", at[1], n_meta), ("rnn_conv_w", SG_CW4, k4), ("conv_dw_w", SG_CW31, k31)]
    layout = []
    for nm, row0, nr in rep_rows:
        kind = "wide" if nm == "b_in" else "rep"
        as2d = lambda a: a.reshape(1, -1) if a.ndim == 1 else a
        layout.append((kind, row0, nr, as2d(w[nm]), as2d(w["m_" + nm]), as2d(w["v_" + nm])))
    for nm, row0, nr in col_rows:
        sq = lambda a: a.reshape(a.shape[-2], a.shape[-1])
        layout.append(("col", row0, nr, sq(w[nm]), sq(w["m_" + nm]), sq(w["v_" + nm])))
    small_partial = jnp.concatenate(pieces[1:], axis=0)

    g_wd1 = d_w_down("1", act1, df1)
    ((small_sems, small_thru, small_land), (wd1_sems, g_wd1_thru, wd1_land)), wd1_token = _exchanges_start(
        "grads_w_down1_exchange",
        [(_bcast_copies, 2 * (N_DEV - 1) + 1, small_partial, (N_DEV,) + small_partial.shape),
         (_scatter_copies, 2 * (N_DEV - 1), g_wd1, (N_DEV - 1,) + g_wd1.shape[1:])])
    g_wgu1 = d_w_gu("1", dgu1, n1, after=wd1_token)

    g_last = g_wgu1.reshape((4, 2) + g_wgu1.shape[1:])
    comb_wgu1 = _pair_reduce(g_last, core)
    sems, comb_thru, land_thru, after = _exchange_start("grads_chip_exchange", _chip_copies, 3, comb_wgu1)
    g_win, r_win = _exchange_wait("grads_w_in_exchange", _scatter_copies, win_sems, g_win_thru, win_land, after)
    g_w3, r_w3 = _exchange_wait("grads_proj3_exchange", _scatter_copies, w3_sems, g_w3_thru, w3_land, after)
    g_wd1, r_wd1 = _exchange_wait("grads_w_down1_exchange", _scatter_copies, wd1_sems, g_wd1_thru, wd1_land, after)
    _, sg_all = _exchange_wait("grads_mixer_small_exchange", _bcast_copies, sg_sems, sg_thru, sg_land, after)
    _, rest_all = _exchange_wait("grads_small_exchange", _bcast_copies, small_sems, small_thru, small_land, after)

    grad_x = (dh0[n_meta:t_real] + after[0, 0])[None]
    total, small_out = _small_adamw([sg_all, rest_all], layout, me_index, grad_x)
    after = total

    groups = [(g_wd1, r_wd1, me_index, ["ffn1_w_down"]),
              (g_wd2, r_wd2, me_index, ["ffn2_w_down"]), (g_wgu2, r_wgu2, me_index, ["ffn2_w_gu"]),
              (g_win, r_win, me_index, ["w_in"]), (g_w3, r_w3, me_index, ["w_out", "rnn_w_proj", "conv_w_proj"]),
              (g_wg, r_wg, me_index, ["rg_w_a", "rg_w_x"]), (None, None, chip, ["ffn1_w_gu"])]
    res = {}
    for own, recv, idx, group in groups:
        if own is None:
            own, recv = _exchange_wait("grads_chip_exchange", _chip_copies, sems, comb_thru, land_thru, after)
        outs = _final_adamw(own, recv, idx, [(w[nm], w["m_" + nm], w["v_" + nm]) for nm in group], after)
        after = outs[-1][0]
        for nm, o in zip(group, outs):
            res[nm] = o
    for nm in ("ffn1_w_gu", "ffn2_w_gu"):
        res[nm] = tuple(jnp.swapaxes(a, 1, 2) for a in res[nm])
    for (nm, _, _), o in zip(rep_rows + col_rows, small_out):
        res[nm] = tuple(a.reshape(w[nm].shape) for a in o)


    order = ["meta_tokens", "ffn1_norm", "ffn1_w_gu", "ffn1_w_down", "mix_norm", "w_in", "b_in", "rnn_conv_w",
             "rnn_conv_b", "rg_w_a", "rg_b_a", "rg_w_x", "rg_b_x", "rg_lambda", "rnn_w_proj", "conv_dw_w",
             "conv_dw_b", "conv_ln_g", "conv_ln_b", "conv_w_proj", "conv_b_proj", "w_out", "ffn2_norm",
             "ffn2_w_gu", "ffn2_w_down", "final_norm"]
    return (total[loss_row, 0], grad_x, *[res[nm][0] for nm in order], *[res[nm][1] for nm in order],
            *[res[nm][2] for nm in order], *[res[nm][3] for nm in order])
```

```python
import functools
import math

import jax
import jax.numpy as jnp
from jax import lax
from jax.experimental import pallas as pl
from jax.experimental.pallas import tpu as pltpu

F32 = jnp.float32
BF16 = jnp.bfloat16
MESH = pl.DeviceIdType.MESH
N_DEV = 8
N_HEADS = 4
RG_LRU_C = 8.0
EPS = 1e-6
FFN_RES = 0.5
ADAM_LR, ADAM_B1, ADAM_B2, ADAM_EPS, ADAM_WD, ADAM_STEP = 0.001, 0.9, 0.999, 1e-08, 0.01, 10
V7X_VMEM_LIMIT = 56 * 1024 * 1024
CONV4_HALO = 8
CONV31_HALO = 32
SUBLANES = 8
STAGE_ROWS = 512
TAIL_FFN1, TAIL_FINAL, TAIL_LOSS, TAIL_FFN2 = 0, 1, 2, 3
FFN_CHUNKS = 2
FFN_FWD_CHUNKS = 1
GELU_C = math.sqrt(2.0 / math.pi)
GELU_K = 0.044715


def _any():
    return pl.BlockSpec(memory_space=pl.ANY)


def _params(n_grid):
    return pltpu.CompilerParams(dimension_semantics=("arbitrary",) * n_grid, vmem_limit_bytes=V7X_VMEM_LIMIT)


def _nn(a, b):
    return jnp.dot(a, b, preferred_element_type=F32)


def _nt(a, b):
    return lax.dot_general(a, b, (((1,), (1,)), ((), ())), preferred_element_type=F32)


def _tn(a, b):
    return lax.dot_general(a, b, (((0,), (0,)), ((), ())), preferred_element_type=F32)


def _sigmoid(x):
    return 0.5 * jnp.tanh(0.5 * x) + 0.5


def _rowsum(x):
    return jnp.sum(x, axis=0, keepdims=True)


def _rms_fwd(x, g):
    r = lax.rsqrt(jnp.mean(x * x, axis=-1, keepdims=True) + EPS)
    return x * r * g, r


def _rms_bwd(dn, x, r, g):
    xr = x * r
    gy = dn * g
    dx = r * (gy - xr * jnp.mean(gy * xr, axis=-1, keepdims=True))
    return dx, _rowsum(dn * xr)


def _gelu(y):
    t = jnp.tanh(GELU_C * (y + GELU_K * y * y * y))
    return 0.5 * y * (1.0 + t), t


def _gelu_grad(y, t):
    return 0.5 * (1.0 + t) + 0.5 * y * (1.0 - t * t) * GELU_C * (1.0 + 3.0 * GELU_K * y * y)


def _softplus(x):
    return jnp.maximum(x, 0.0) + jnp.log(1.0 + jnp.exp(-jnp.abs(x)))


def _one_minus_exp(z):
    series = -z * (1.0 + 0.5 * z * (1.0 + z * (1.0 / 3.0) * (1.0 + 0.25 * z)))
    return jnp.where(z > -0.05, series, 1.0 - jnp.exp(z))


def _tiles(t_real):
    if t_real > 2048:
        tm = 416
        tp = -(-t_real // tm) * tm
        return tp, tm, tm // 2, tm // 2, tp, tp
    tm = 128
    tp = -(-t_real // tm) * tm
    return tp, tm, tm // 2, tm // 2, tm, tm


def _load_weights(copies, sems):
    cps = [pltpu.make_async_copy(s, d, sems.at[k]) for k, (s, d) in enumerate(copies)]
    for cp in cps:
        cp.start()
    for cp in cps:
        cp.wait()


def _position():
    x, y, c = lax.axis_index("x"), lax.axis_index("y"), lax.axis_index("c")
    chips = [(1 - x, y), (x, 1 - y), (1 - x, 1 - y)]
    return x, y, c, chips


def _slot(p):
    return 4 * p[0] + 2 * p[1] + p[2]


class _Lazy(dict):
    def __getitem__(self, key):
        val = dict.__getitem__(self, key)
        return val() if callable(val) else val


class _Gather:
    def __init__(self, shards, pass_on_at=None):
        self.shards = list(shards)
        self.n = len(self.shards)
        self.pass_on_at = pass_on_at

    def inputs(self):
        return self.shards

    def out_shape(self):
        return [jax.ShapeDtypeStruct((N_DEV,) + s.shape, s.dtype) for s in self.shards]

    N_SEMS = 9

    def scratch(self):
        return [pltpu.SemaphoreType.DMA((self.N_SEMS * self.n,)), pltpu.SemaphoreType.DMA((self.N_SEMS * self.n,)),
                pltpu.SemaphoreType.DMA((self.n,))]

    def _plan(self, ins, outs, sems):
        send_sems, recv_sems, local_sems = sems
        x, y, c, _ = _position()
        me, sib, xn, yn, dg = (x, y, c), (x, y, 1 - c), (1 - x, y, c), (x, 1 - y, c), (1 - x, 1 - y, c)
        other = lambda p: (p[0], p[1], 1 - c)

        def blk(a, p, half=None):
            ref = outs[a].at[_slot(p)]
            if half is None:
                return ref
            rows = self.shards[a].shape[0] // 2
            return ref.at[pl.ds(half * rows, rows)]

        def copy(a, k, dst, to, src=None):
            return pltpu.make_async_remote_copy(
                src_ref=dst if src is None else src, dst_ref=dst,
                send_sem=send_sems.at[self.N_SEMS * a + k], recv_sem=recv_sems.at[self.N_SEMS * a + k],
                device_id=to, device_id_type=MESH)

        cp = _Lazy(mine=lambda: [pltpu.make_async_copy(ins[a], blk(a, me), local_sems.at[a]) for a in range(self.n)])
        for a in range(self.n):
            cp[a] = _Lazy(
                own=lambda a=a: [copy(a, 0, blk(a, me), sib, src=ins[a]), copy(a, 1, blk(a, me), xn, src=ins[a]),
                                 copy(a, 2, blk(a, me), yn, src=ins[a])],
                from_x=lambda a=a: copy(a, 1, blk(a, xn), me), from_y=lambda a=a: copy(a, 2, blk(a, yn), me),
                relay_x=lambda a=a: copy(a, 3, blk(a, xn, 0), yn), relay_y=lambda a=a: copy(a, 4, blk(a, yn, 1), xn),
                diag0=lambda a=a: copy(a, 3, blk(a, dg, 0), me), diag1=lambda a=a: copy(a, 4, blk(a, dg, 1), me),
                pass_x=lambda a=a: copy(a, 5, blk(a, xn), sib), pass_y=lambda a=a: copy(a, 6, blk(a, yn), sib),
                pass_d0=lambda a=a: copy(a, 7, blk(a, dg, 0), sib), pass_d1=lambda a=a: copy(a, 8, blk(a, dg, 1), sib),
                from_sib=lambda a=a: [copy(a, 0, blk(a, sib), me), copy(a, 5, blk(a, other(xn)), me),
                                      copy(a, 6, blk(a, other(yn)), me), copy(a, 7, blk(a, other(dg), 0), me),
                                      copy(a, 8, blk(a, other(dg), 1), me)])
        return cp

    def start(self, ins, outs, sems):
        cp = self._plan(ins, outs, sems)
        for c in cp["mine"]:
            c.start()
        for a in range(self.n):
            for c in cp[a]["own"]:
                c.start()

    def pass_on(self, ins, outs, sems):
        cp = self._plan(ins, outs, sems)
        for a in range(self.n):
            cp[a]["from_x"].wait_recv()
            cp[a]["relay_x"].start()
            cp[a]["pass_x"].start()
        for a in range(self.n):
            cp[a]["from_y"].wait_recv()
            cp[a]["relay_y"].start()
            cp[a]["pass_y"].start()

    def pass_on_relayed(self, ins, outs, sems):
        cp = self._plan(ins, outs, sems)
        for a in range(self.n):
            cp[a]["diag0"].wait_recv()
            cp[a]["pass_d0"].start()
            cp[a]["diag1"].wait_recv()
            cp[a]["pass_d1"].start()

    def finish(self, ins, outs, sems):
        if self.pass_on_at is None:
            self.pass_on(ins, outs, sems)
            self.pass_on_relayed(ins, outs, sems)
        cp = self._plan(ins, outs, sems)
        for a in range(self.n):
            for c in cp[a]["from_sib"]:
                c.wait_recv()
            for c in cp[a]["own"] + [cp[a][k] for k in ("relay_x", "relay_y", "pass_x", "pass_y", "pass_d0", "pass_d1")]:
                c.wait_send()
        for c in cp["mine"]:
            c.wait()


class _Scatter:
    def __init__(self, grads):
        self.grads = list(grads)
        self.n = len(self.grads)

    def inputs(self):
        return self.grads

    def out_shape(self):
        return [jax.ShapeDtypeStruct((N_DEV - 1,) + g.shape[1:], g.dtype) for g in self.grads]

    def scratch(self):
        return [pltpu.SemaphoreType.DMA((7 * self.n,)), pltpu.SemaphoreType.DMA((7 * self.n,))]

    def _plan(self, ins, outs, sems):
        send_sems, recv_sems = sems
        x, y, c, _ = _position()
        cps = []
        for a in range(self.n):
            for k in range(1, N_DEV):
                peer = (x ^ (k >> 2), y ^ ((k >> 1) & 1), c ^ (k & 1))
                cps.append(pltpu.make_async_remote_copy(
                    src_ref=ins[a].at[_slot(peer)], dst_ref=outs[a].at[k - 1],
                    send_sem=send_sems.at[7 * a + k - 1], recv_sem=recv_sems.at[7 * a + k - 1],
                    device_id=peer, device_id_type=MESH))
        return cps

    def start(self, ins, outs, sems):
        for cp in self._plan(ins, outs, sems):
            cp.start()

    def finish(self, ins, outs, sems):
        for cp in self._plan(ins, outs, sems):
            cp.wait()


def _hosted(inner, n_in, n_out, comm, grid):
    if comm is None:
        return inner
    nc_in, nc_out, ns = len(comm.inputs()), len(comm.out_shape()), len(comm.scratch())

    def body(*refs):
        o0 = n_in + nc_in
        s0 = o0 + n_out + nc_out
        main = refs[:n_in] + refs[o0:o0 + n_out] + refs[s0:len(refs) - ns]
        c_in, c_out, c_sems = refs[n_in:o0], refs[o0 + n_out:s0], refs[len(refs) - ns:]
        ids = [pl.program_id(ax) for ax in range(len(grid))]
        first = functools.reduce(jnp.logical_and, [i == 0 for i in ids])
        last = functools.reduce(jnp.logical_and, [i == g - 1 for i, g in zip(ids, grid)])

        @pl.when(first)
        def _():
            comm.start(c_in, c_out, c_sems)

        inner(*main)

        if getattr(comm, "pass_on_at", None) is not None:
            assert len(grid) == 1
            first_at, second_at = (min(grid[0] - 1, int(frac * grid[0])) for frac in comm.pass_on_at)
            assert first_at < second_at

            @pl.when(ids[0] == first_at)
            def _():
                comm.pass_on(c_in, c_out, c_sems)

            @pl.when(ids[0] == second_at)
            def _():
                comm.pass_on_relayed(c_in, c_out, c_sems)

        @pl.when(last)
        def _():
            comm.finish(c_in, c_out, c_sems)

    return body


def _call(inner, name, grid, in_specs, out_specs, out_shape, scratch, args, comm=None):
    n_in, n_out = len(args), len(out_shape)
    body = _hosted(inner, n_in, n_out, comm, grid)
    if comm is not None:
        in_specs = list(in_specs) + [_any()] * len(comm.inputs())
        args = list(args) + comm.inputs()
        out_specs = list(out_specs) + [_any()] * len(comm.out_shape())
        out_shape = list(out_shape) + comm.out_shape()
        scratch = list(scratch) + comm.scratch()
    outs = pl.pallas_call(
        body, name=name, grid=grid, in_specs=list(in_specs), out_specs=list(out_specs), out_shape=list(out_shape),
        scratch_shapes=list(scratch), compiler_params=_params(len(grid)))(*args)
    return list(outs[:n_out]), list(outs[n_out:])


def _first_gather(shards, small_idx, x2, t2, n_meta, tp):
    comm = _Gather(shards)
    n = comm.n
    seq, d = x2.shape
    t_real = n_meta + seq
    n_pad = tp - t_real
    cw = d // N_DEV
    rows = STAGE_ROWS if seq % STAGE_ROWS == 0 else seq
    n_chunks = seq // rows

    def body(*refs):
        ins, (x_ref, t_ref) = refs[:n], refs[n:n + 2]
        outs, (h0_ref, tg_ref) = refs[n + 2:2 * n + 2], refs[2 * n + 2:2 * n + 4]
        sems = refs[2 * n + 4:2 * n + 7]
        buf, zeros, in_sems, out_sems, misc_sems = refs[2 * n + 7:]
        comm.start(ins, outs, sems)
        zeros[...] = jnp.zeros_like(zeros)
        fills = [pltpu.make_async_copy(zeros.at[pl.ds(0, n_pad)], h0_ref.at[pl.ds(t_real, n_pad)], misc_sems.at[0]),
                 pltpu.make_async_copy(zeros.at[pl.ds(0, n_pad)], tg_ref.at[pl.ds(t_real, n_pad)], misc_sems.at[1]),
                 pltpu.make_async_copy(zeros.at[pl.ds(0, n_meta)], tg_ref.at[pl.ds(0, n_meta)], misc_sems.at[2])]
        for cp in fills:
            cp.start()
        jobs = [(src, dst, c) for src, dst in ((x_ref, h0_ref), (t_ref, tg_ref)) for c in range(n_chunks)]

        def load(k):
            src, _, c = jobs[k]
            return pltpu.make_async_copy(src.at[pl.ds(c * rows, rows)], buf.at[k % 2], in_sems.at[k % 2])

        def store(k):
            _, dst, c = jobs[k]
            return pltpu.make_async_copy(buf.at[k % 2], dst.at[pl.ds(n_meta + c * rows, rows)], out_sems.at[k % 2])

        load(0).start()
        for k in range(len(jobs)):
            load(k).wait()
            if k + 1 < len(jobs):
                if k >= 1:
                    store(k - 1).wait()
                load(k + 1).start()
            store(k).start()
        for k in range(max(0, len(jobs) - 2), len(jobs)):
            store(k).wait()
        comm.finish(ins, outs, sems)
        meta = [pltpu.make_async_copy(outs[small_idx].at[k, pl.ds(0, n_meta)],
                                      h0_ref.at[pl.ds(0, n_meta), pl.ds(k * cw, cw)], misc_sems.at[3 + k])
                for k in range(N_DEV)]
        for cp in meta:
            cp.start()
        for cp in fills + meta:
            cp.wait()

    staged = [jax.ShapeDtypeStruct((tp, d), F32)] * 2
    outs = pl.pallas_call(
        body, name="weights_all_gather", out_shape=comm.out_shape() + staged,
        in_specs=[_any()] * (n + 2), out_specs=[_any()] * (n + 2),
        scratch_shapes=comm.scratch() + [
            pltpu.VMEM((2, rows, d), F32), pltpu.VMEM((max(n_pad, n_meta), d), F32),
            pltpu.SemaphoreType.DMA((2,)), pltpu.SemaphoreType.DMA((2,)), pltpu.SemaphoreType.DMA((3 + N_DEV,))],
        compiler_params=pltpu.CompilerParams(vmem_limit_bytes=V7X_VMEM_LIMIT),
    )(*shards, x2, t2)
    return outs[:n], outs[n], outs[n + 1]


def _chip_copies(c_ref, land_ref, sems):
    _, _, c, chips = _position()
    return [pltpu.make_async_remote_copy(
        src_ref=c_ref.at[2 * cx + cy], dst_ref=land_ref.at[j], send_sem=sems[j], recv_sem=sems[3 + j],
        device_id=(cx, cy, c), device_id_type=MESH) for j, (cx, cy) in enumerate(chips)]


def _scatter_copies(g_ref, land_ref, sems):
    x, y, c, _ = _position()
    cps = []
    for k in range(1, N_DEV):
        peer = (x ^ (k >> 2), y ^ ((k >> 1) & 1), c ^ (k & 1))
        cps.append(pltpu.make_async_remote_copy(
            src_ref=g_ref.at[_slot(peer)], dst_ref=land_ref.at[k - 1], send_sem=sems[k - 1],
            recv_sem=sems[N_DEV - 1 + k - 1], device_id=peer, device_id_type=MESH))
    return cps


def _bcast_copies(b_ref, land_ref, sems):
    x, y, c, _ = _position()
    mine = land_ref.at[_slot((x, y, c))]
    cps = []
    for k in range(1, N_DEV):
        peer = (x ^ (k >> 2), y ^ ((k >> 1) & 1), c ^ (k & 1))
        cps.append(pltpu.make_async_remote_copy(
            src_ref=b_ref, dst_ref=mine, send_sem=sems[k - 1], recv_sem=sems[N_DEV - 1 + k - 1],
            device_id=peer, device_id_type=MESH))
    return cps + [pltpu.make_async_copy(b_ref, mine, sems[2 * (N_DEV - 1)])]


def _exchanges_start(name, parts):
    hbm = pl.BlockSpec(memory_space=pltpu.HBM)
    sem = pl.BlockSpec(memory_space=pltpu.SEMAPHORE)
    n_parts = len(parts)
    total = sum(n for _, n, _, _ in parts)

    def body(*refs):
        ins, sems, token = refs[:2 * n_parts], refs[2 * n_parts:2 * n_parts + total], refs[4 * n_parts + total]
        at = 0
        for j, (copies, n, _, _) in enumerate(parts):
            for cp in copies(ins[2 * j], ins[2 * j + 1], sems[at:at + n]):
                cp.start()
            at += n
        token[...] = jnp.zeros_like(token)

    flat = []
    for _, _, src, land_shape in parts:
        flat += [pltpu.with_memory_space_constraint(src, pltpu.HBM),
                 pltpu.with_memory_space_constraint(lax.empty(land_shape, src.dtype), pltpu.HBM)]
    outs = pl.pallas_call(
        body, name=name + "_start",
        out_shape=(pltpu.SemaphoreType.DMA(()),) * total + tuple(pltpu.HBM(a.shape, a.dtype) for a in flat)
        + (jax.ShapeDtypeStruct((SUBLANES, 128), F32),),
        in_specs=(hbm,) * len(flat),
        out_specs=(sem,) * total + (hbm,) * len(flat) + (pl.BlockSpec(memory_space=pltpu.VMEM),),
        input_output_aliases={j: total + j for j in range(len(flat))},
        compiler_params=pltpu.CompilerParams(has_side_effects=pltpu.SideEffectType.DATAFLOW_SIDE_EFFECTING),
    )(*flat)
    res, at = [], 0
    for j, (_, n, _, _) in enumerate(parts):
        res.append((outs[at:at + n], outs[total + 2 * j], outs[total + 2 * j + 1]))
        at += n
    return res, outs[total + len(flat)]


def _exchange_start(name, copies, n_copies, src):
    (part,), token = _exchanges_start(name, [(copies, 2 * n_copies, src, (n_copies,) + src.shape[1:])])
    return (*part, token)


def _exchange_wait(name, copies, sems, src_thru, land_thru, after):
    hbm = pl.BlockSpec(memory_space=pltpu.HBM)
    sem = pl.BlockSpec(memory_space=pltpu.SEMAPHORE)
    n_sems = len(sems)

    def body(s_ref, land_ref, *refs):
        for cp in copies(s_ref, land_ref, refs[:n_sems]):
            cp.wait()

    return pl.pallas_call(
        body, name=name + "_wait",
        out_shape=(pltpu.HBM(src_thru.shape, src_thru.dtype), pltpu.HBM(land_thru.shape, land_thru.dtype)),
        in_specs=(hbm, hbm) + (sem,) * n_sems + (pl.BlockSpec(memory_space=pl.ANY),), out_specs=(hbm, hbm),
        input_output_aliases={0: 0, 1: 1},
        compiler_params=pltpu.CompilerParams(has_side_effects=pltpu.SideEffectType.DATAFLOW_SIDE_EFFECTING),
    )(src_thru, land_thru, *sems, after)


def _pair_reduce(grad, core):
    blk = grad.shape[2:]
    zeros = (0,) * len(blk)

    def body(core_ref, g_hbm, own_ref, o_ref, landed, send_sems, recv_sems):
        del core_ref
        i = pl.program_id(0)
        x, y, c, _ = _position()

        def copy(k):
            return pltpu.make_async_remote_copy(
                src_ref=g_hbm.at[k, 1 - c], dst_ref=landed.at[k], send_sem=send_sems.at[k],
                recv_sem=recv_sems.at[k], device_id=(x, y, 1 - c), device_id_type=MESH)

        @pl.when(i == 0)
        def _():
            for k in range(4):
                copy(k).start()

        for k in range(4):
            @pl.when(i == k)
            def _(k=k):
                copy(k).wait_recv()

        o_ref[...] = (own_ref[...].astype(F32) + landed[i].astype(F32)).astype(BF16)

        @pl.when(i == 3)
        def _():
            for k in range(4):
                copy(k).wait_send()

    return pl.pallas_call(
        body, name="grads_pair_reduce",
        out_shape=jax.ShapeDtypeStruct((4,) + blk, BF16),
        grid_spec=pltpu.PrefetchScalarGridSpec(
            num_scalar_prefetch=1, grid=(4,),
            in_specs=[_any(), pl.BlockSpec((None, None) + blk, lambda i, cr: (i, cr[0]) + zeros)],
            out_specs=pl.BlockSpec((None,) + blk, lambda i, cr: (i,) + zeros),
            scratch_shapes=[pltpu.VMEM((4,) + blk, BF16), pltpu.SemaphoreType.DMA((4,)),
                            pltpu.SemaphoreType.DMA((4,))]),
        compiler_params=_params(1),
    )(core, grad, grad)


def _adamw(w, g, m, v):
    m2 = ADAM_B1 * m + (1.0 - ADAM_B1) * g
    v2 = ADAM_B2 * v + (1.0 - ADAM_B2) * (g * g)
    m_hat = m2 / (1.0 - ADAM_B1 ** ADAM_STEP)
    v_hat = v2 / (1.0 - ADAM_B2 ** ADAM_STEP)
    delta = -ADAM_LR * (m_hat / (jnp.sqrt(v_hat) + ADAM_EPS) + ADAM_WD * w)
    return delta, m2, v2


def _final_adamw(own, recv, idx, parts, after):
    blk = own.shape[1:]
    n_recv = recv.shape[0]
    n_parts = len(parts)
    per = blk[0] // n_parts if n_parts > 1 else None
    rows = blk[-2]
    n_chunks = 1 if n_parts > 1 else (4 if rows % 64 == 0 and rows >= 512 else (2 if rows % 32 == 0 else 1))
    cblk = blk[:-2] + (rows // n_chunks, blk[-1])
    lead = (0,) * (len(blk) - 2)

    def body(idx_ref, c_ref, r_ref, after_ref, *refs):
        del idx_ref, after_ref
        ins, outs = refs[:3 * n_parts], refs[3 * n_parts:]
        g = c_ref[...].astype(F32)
        for k in range(n_recv):
            g = g + r_ref[k].astype(F32)
        for p in range(n_parts):
            w_ref, m_ref, v_ref = ins[3 * p:3 * p + 3]
            if n_parts == 1:
                gp = g
            elif per == 1:
                gp = g[p]
            else:
                gp = g[p * per:(p + 1) * per]
            delta, m2, v2 = _adamw(w_ref[0], gp, m_ref[0], v_ref[0])
            o = outs[4 * p:4 * p + 4]
            o[0][0] = gp
            o[1][0] = delta
            o[2][0] = m2
            o[3][0] = v2

    flat = [a for wmv in parts for a in wmv]

    def part_spec(a):
        shape = a.shape[:-2] + (a.shape[-2] // n_chunks, a.shape[-1])
        return pl.BlockSpec(shape, lambda i, cr, nd=a.ndim: (0,) * (nd - 2) + (i, 0))

    outs = pl.pallas_call(
        body, name="grads_sum_adamw",
        out_shape=[jax.ShapeDtypeStruct(wmv[0].shape, F32) for wmv in parts for _ in range(4)],
        grid_spec=pltpu.PrefetchScalarGridSpec(
            num_scalar_prefetch=1, grid=(n_chunks,),
            in_specs=[pl.BlockSpec((None,) + cblk, lambda i, cr: (cr[0],) + lead + (i, 0)),
                      pl.BlockSpec((n_recv,) + cblk, lambda i, cr: (0,) + lead + (i, 0))]
                     + [_any()] + [part_spec(a) for a in flat],
            out_specs=[part_spec(wmv[0]) for wmv in parts for _ in range(4)]),
        compiler_params=_params(1),
    )(idx, own, recv, after, *flat)
    return [tuple(outs[4 * p:4 * p + 4]) for p in range(n_parts)]


def _small_adamw(partials, layout, me_index, after):
    d = partials[0].shape[-1]
    rows = sum(p.shape[1] for p in partials)
    n = len(layout)
    n_p = len(partials)
    cw = d // N_DEV

    def body(me_ref, *refs):
        p_refs, refs = refs[:n_p], refs[n_p + 1:]
        ins, t_ref, outs = refs[:3 * n], refs[3 * n], refs[3 * n + 1:]
        me = me_ref[0]
        row = 0
        for p_ref in p_refs:
            total = p_ref[0]
            for j in range(1, N_DEV):
                total = total + p_ref[j]
            t_ref[row:row + p_ref.shape[1], :] = total
            row += p_ref.shape[1]
        for e, (kind, r0, nr, _, _, _) in enumerate(layout):
            w_ref, m_ref, v_ref = ins[3 * e:3 * e + 3]
            o = outs[4 * e:4 * e + 4]
            if kind == "rep":
                g = t_ref[r0:r0 + nr, :]
                delta, m2, v2 = _adamw(w_ref[...], g, m_ref[...], v_ref[...])
                for ref, val in zip(o, (g, delta, m2, v2)):
                    ref[...] = val
            elif kind == "wide":
                for q in range(nr):
                    sl = slice(q * d, (q + 1) * d)
                    g = t_ref[r0 + q:r0 + q + 1, :]
                    delta, m2, v2 = _adamw(w_ref[:, sl], g, m_ref[:, sl], v_ref[:, sl])
                    for ref, val in zip(o, (g, delta, m2, v2)):
                        ref[:, sl] = val
            else:
                for j in range(N_DEV):
                    @pl.when(me == j)
                    def _(j=j, o=o, w_ref=w_ref, m_ref=m_ref, v_ref=v_ref, r0=r0, nr=nr):
                        g = t_ref[r0:r0 + nr, j * cw:(j + 1) * cw]
                        delta, m2, v2 = _adamw(w_ref[...], g, m_ref[...], v_ref[...])
                        for ref, val in zip(o, (g, delta, m2, v2)):
                            ref[...] = val

    flat = [a for ent in layout for a in ent[3:]]
    vm = pl.BlockSpec(memory_space=pltpu.VMEM)
    outs = pl.pallas_call(
        body, name="small_adamw",
        out_shape=[jax.ShapeDtypeStruct((rows, d), F32)]
                  + [jax.ShapeDtypeStruct(ent[3].shape, F32) for ent in layout for _ in range(4)],
        in_specs=[pl.BlockSpec(memory_space=pltpu.SMEM)] + [vm] * n_p + [_any()] + [vm] * len(flat),
        out_specs=[vm] * (1 + 4 * n),
        compiler_params=pltpu.CompilerParams(vmem_limit_bytes=V7X_VMEM_LIMIT),
    )(me_index, *partials, after, *flat)
    return outs[0], [tuple(outs[1 + 4 * e:5 + 4 * e]) for e in range(n)]


def _ffn_fwd(h, g, wgu, wd, tm, loss=None, comm=None):
    tp, d = h.shape
    f = wd.shape[0]
    fc = f // FFN_FWD_CHUNKS
    nt = tp // tm
    with_loss = loss is not None
    if with_loss:
        tgt, gf, n_meta, t_real = loss

    def body(*refs):
        if with_loss:
            (h_ref, g_ref, wgu_hbm, wd_hbm, tgt_ref, gf_ref, out_ref, gu_ref, n_ref, tail_ref,
             wgu_v, wd_v, sems) = refs
        else:
            h_ref, g_ref, wgu_hbm, wd_hbm, out_ref, gu_ref, n_ref, wgu_v, wd_v, sems = refs
        i = pl.program_id(0)

        @pl.when(i == 0)
        def _():
            _load_weights([(wgu_hbm, wgu_v), (wd_hbm, wd_v)], sems)
            if with_loss:
                tail_ref[...] = jnp.zeros_like(tail_ref)

        x = h_ref[...]
        n, _ = _rms_fwd(x, g_ref[...])
        nb = n.astype(BF16)
        n_ref[...] = nb
        acc = jnp.zeros((tm, d), F32)
        for j in range(FFN_FWD_CHUNKS):
            cols = slice(j * fc, (j + 1) * fc)
            gate = _nt(nb, wgu_v[pl.ds(j * fc, fc), :])
            up = _nt(nb, wgu_v[pl.ds(f + j * fc, fc), :])
            gu_ref[0, :, cols] = gate.astype(BF16)
            gu_ref[1, :, cols] = up.astype(BF16)
            act = (gate * _sigmoid(gate) * up).astype(BF16)
            acc = acc + _nn(act, wd_v[pl.ds(j * fc, fc), :])
        hn = x + FFN_RES * acc
        if not with_loss:
            out_ref[...] = hn
        else:
            gfv = gf_ref[...]
            r = lax.rsqrt(jnp.mean(hn * hn, axis=-1, keepdims=True) + EPS)
            xr = hn * r
            rows = i * tm + lax.broadcasted_iota(jnp.int32, (tm, 1), 0)
            mask = jnp.logical_and(rows >= n_meta, rows < t_real)
            diff = jnp.where(mask, xr * gfv - tgt_ref[...], 0.0)
            tail_ref[TAIL_LOSS:TAIL_LOSS + 1, :] += jnp.zeros((1, d), F32) + 0.5 * jnp.sum(diff * diff) / d
            dy = diff / d
            gy = dy * gfv
            out_ref[...] = r * (gy - xr * jnp.mean(gy * xr, axis=-1, keepdims=True))
            tail_ref[TAIL_FINAL:TAIL_FINAL + 1, :] += _rowsum(dy * xr)

    row = pl.BlockSpec((tm, d), lambda i: (i, 0))
    vec = pl.BlockSpec((1, d), lambda i: (0, 0))
    in_specs = [row, vec, _any(), _any()]
    out_shape = [jax.ShapeDtypeStruct((tp, d), F32), jax.ShapeDtypeStruct((2, tp, f), BF16),
                 jax.ShapeDtypeStruct((tp, d), BF16)]
    out_specs = [row, pl.BlockSpec((2, tm, f), lambda i: (0, i, 0)), row]
    args = [h, g, wgu, wd]
    if with_loss:
        in_specs += [row, vec]
        out_shape += [jax.ShapeDtypeStruct((SUBLANES, d), F32)]
        out_specs += [pl.BlockSpec((SUBLANES, d), lambda i: (0, 0))]
        args += [tgt, gf]
    return _call(body, "ffn_fwd_loss" if with_loss else "ffn_fwd", (nt,), in_specs, out_specs, out_shape,
                 [pltpu.VMEM((2 * f, d), BF16), pltpu.VMEM((f, d), BF16), pltpu.SemaphoreType.DMA((2,))],
                 args, comm)


def _ffn_bwd(dh, h, gu, g, wgu, wd, tm, tail, tail_row, after):
    tp, d = h.shape
    f = wd.shape[0]
    fc = f // FFN_CHUNKS
    nt = tp // tm

    def body(dh_ref, h_ref, gu_ref, g_ref, tail_ref, wgu_hbm, wd_hbm, after_ref,
             dhin_ref, dgu_ref, act_ref, df_ref, dg_ref, wgu_v, wd_v, dn_v, sems):
        del after_ref
        i, j = pl.program_id(0), pl.program_id(1)

        @pl.when(jnp.logical_and(i == 0, j == 0))
        def _():
            _load_weights([(wgu_hbm, wgu_v), (wd_hbm, wd_v)], sems)
            dg_ref[...] = tail_ref[...]

        dfb = (FFN_RES * dh_ref[...]).astype(BF16)

        @pl.when(j == 0)
        def _():
            df_ref[...] = dfb
            dn_v[...] = jnp.zeros_like(dn_v)

        lo = pl.multiple_of(j * fc, 16)
        dact = _nt(dfb, wd_v[pl.ds(lo, fc), :])
        gate = gu_ref[0].astype(F32)
        up = gu_ref[1].astype(F32)
        sg = _sigmoid(gate)
        silu = gate * sg
        act_ref[...] = (silu * up).astype(BF16)
        dgate = (dact * up * (sg * (1.0 + gate * (1.0 - sg)))).astype(BF16)
        dup = (dact * silu).astype(BF16)
        dgu_ref[0] = dgate
        dgu_ref[1] = dup
        dn_v[...] += _nn(dgate, wgu_v[pl.ds(lo, fc), :]) + _nn(dup, wgu_v[pl.ds(pl.multiple_of(f + j * fc, 16), fc), :])

        @pl.when(j == FFN_CHUNKS - 1)
        def _():
            x = h_ref[...]
            r = lax.rsqrt(jnp.mean(x * x, axis=-1, keepdims=True) + EPS)
            dx, dgp = _rms_bwd(dn_v[...], x, r, g_ref[...])
            dhin_ref[...] = dh_ref[...] + dx
            dg_ref[tail_row:tail_row + 1, :] += dgp

    row = pl.BlockSpec((tm, d), lambda i, j: (i, 0))
    vec = pl.BlockSpec((1, d), lambda i, j: (0, 0))
    tile = pl.BlockSpec((SUBLANES, d), lambda i, j: (0, 0))
    hid2 = pl.BlockSpec((2, tm, fc), lambda i, j: (0, i, j))
    return _call(
        body, "ffn_bwd", (nt, FFN_CHUNKS),
        [row, row, hid2, vec, tile, _any(), _any(), _any()],
        [row, hid2, pl.BlockSpec((tm, fc), lambda i, j: (i, j)), row, tile],
        [jax.ShapeDtypeStruct((tp, d), F32), jax.ShapeDtypeStruct((2, tp, f), BF16),
         jax.ShapeDtypeStruct((tp, f), BF16), jax.ShapeDtypeStruct((tp, d), BF16),
         jax.ShapeDtypeStruct((SUBLANES, d), F32)],
        [pltpu.VMEM((2 * f, d), BF16), pltpu.VMEM((f, d), BF16), pltpu.VMEM((tm, d), F32),
         pltpu.SemaphoreType.DMA((2,))],
        [dh, h, gu, g, tail, wgu, wd, after])


def _piece_segments(q, d, nb_cols):
    segs = []
    for j in range(N_DEV):
        lo, hi = max(q * d, j * nb_cols), min((q + 1) * d, (j + 1) * nb_cols)
        if lo < hi:
            segs.append((j, lo - q * d, hi - q * d, lo - j * nb_cols, hi - j * nb_cols))
    return segs


def _w3_copies(w3_hbm, rows, w3_v):
    return [(w3_hbm.at[k, pl.ds(q * rows, rows)], w3_v.at[q, pl.ds(k * rows, rows)])
            for q in range(3) for k in range(N_DEV)]


def _gates(xrb, wg_ref, ba, bx, lam, hd):
    pre_r, pre_i = [], []
    for hh in range(N_HEADS):
        xh = xrb[:, hh * hd:(hh + 1) * hd]
        pre_r.append(_nn(xh, wg_ref[0, hh]))
        pre_i.append(_nn(xh, wg_ref[1, hh]))
    r = _sigmoid(jnp.concatenate(pre_r, axis=1) + ba)
    ig = _sigmoid(jnp.concatenate(pre_i, axis=1) + bx)
    sp = _softplus(-lam)
    log_a = -RG_LRU_C * r * sp
    a = jnp.exp(log_a)
    s = jnp.sqrt(_one_minus_exp(2.0 * log_a))
    return r, ig, sp, a, s


def _scan_fwd(a, u, h_prev):
    tm = a.shape[0]
    rows = lax.broadcasted_iota(jnp.int32, a.shape, 0)
    d = 1
    while d < tm:
        if d < SUBLANES:
            keep = rows >= d
            u = jnp.where(keep, a * pltpu.roll(u, d, 0) + u, u)
            a = jnp.where(keep, a * pltpu.roll(a, d, 0), a)
        else:
            u = jnp.concatenate([u[:d], a[d:] * u[:tm - d] + u[d:]], axis=0)
            a = jnp.concatenate([a[:d], a[d:] * a[:tm - d]], axis=0)
        d *= 2
    return u + a * h_prev


def _scan_bwd(b, v, g_next):
    tm = b.shape[0]
    rows = lax.broadcasted_iota(jnp.int32, b.shape, 0)
    d = 1
    while d < tm:
        if d < SUBLANES:
            keep = rows < tm - d
            v = jnp.where(keep, v + b * pltpu.roll(v, tm - d, 0), v)
            b = jnp.where(keep, b * pltpu.roll(b, tm - d, 0), b)
        else:
            v = jnp.concatenate([v[:tm - d] + b[:tm - d] * v[d:], v[tm - d:]], axis=0)
            b = jnp.concatenate([b[:tm - d] * b[d:], b[tm - d:]], axis=0)
        d *= 2
    return v + b * g_next


def _shifted_copies(ext_ref, es_ref, n_rows):
    for s in range(1, SUBLANES):
        es_ref[s, pl.ds(0, n_rows), :] = ext_ref[pl.ds(s, n_rows), :]


def _tap(ext_ref, es_ref, off, tm):
    q, s = divmod(off, SUBLANES)
    if s == 0:
        return ext_ref[pl.ds(SUBLANES * q, tm), :]
    return es_ref[s, pl.ds(SUBLANES * q, tm), :]


def _mixer_fwd(h, g, b_in, win_all, cw4, cb4, wg, ba, bx, lam, cw31, cb31, lng, lnb, bcp, w3_all, tm, comm=None):
    tp, d = h.shape
    nb_cols = win_all.shape[-1]
    n_in = N_DEV * nb_cols
    hd = wg.shape[-1]
    k4, k31 = cw4.shape[0], cw31.shape[0]
    w3_rows = d // N_DEV

    def body(h_ref, g_ref, b_ref, win_hbm, cw4_ref, cb4_ref, wg_ref, ba_ref, bx_ref, lam_ref, cw31_ref, cb31_ref,
             lng_ref, lnb_ref, bcp_ref, w3_hbm,
             h2_ref, p_ref, n_ref, xr_ref, hs_ref, v1_ref, ya_ref, yb_ref,
             win_v, w3_v, ext4, ext31, es31, hcar, sems):
        @pl.when(pl.program_id(0) == 0)
        def _():
            _load_weights([(win_hbm, win_v)] + _w3_copies(w3_hbm, w3_rows, w3_v), sems)
            ext4[pl.ds(0, CONV4_HALO), :] = jnp.zeros((CONV4_HALO, d), F32)
            ext31[pl.ds(0, CONV31_HALO), :] = jnp.zeros((CONV31_HALO, d), F32)
            hcar[...] = jnp.zeros_like(hcar)

        n, _ = _rms_fwd(h_ref[...], g_ref[...])
        nb = n.astype(BF16)
        n_ref[...] = nb

        def piece(q):
            parts = [_nn(nb, win_v[j, :, bl:bh]) for j, _, _, bl, bh in _piece_segments(q, d, nb_cols)]
            pq = (jnp.concatenate(parts, axis=1) + b_ref[:, q * d:(q + 1) * d]).astype(BF16)
            p_ref[:, q * d:(q + 1) * d] = pq
            return pq.astype(F32)

        x_rnn, y_rnn, glu_v, glu_g, gate_a, gate_b = [piece(q) for q in range(6)]

        ext4[pl.ds(CONV4_HALO, tm), :] = x_rnn
        xr = cb4_ref[...] + jnp.zeros((tm, d), F32)
        for k in range(k4):
            xr = xr + cw4_ref[k:k + 1, :] * ext4[pl.ds(CONV4_HALO - (k4 - 1) + k, tm), :]
        ext4[pl.ds(0, CONV4_HALO), :] = ext4[pl.ds(tm, CONV4_HALO), :]
        xrb = xr.astype(BF16)
        xr_ref[...] = xrb
        xr = xrb.astype(F32)
        _, ig, _, a, s = _gates(xrb, wg_ref, ba_ref[...], bx_ref[...], lam_ref[...], hd)
        hseq = _scan_fwd(a, s * (ig * xr), hcar[0:1, :])
        hcar[0:1, :] = hseq[tm - 1:tm, :]
        hs_ref[...] = hseq.astype(BF16)
        gl, _ = _gelu(y_rnn)
        ya = _nn((hseq * gl).astype(BF16), w3_v[0])
        ya_ref[...] = ya.astype(BF16)

        ext31[pl.ds(CONV31_HALO, tm), :] = glu_v * _sigmoid(glu_g)
        _shifted_copies(ext31, es31, tm + CONV31_HALO - SUBLANES)
        v1 = cb31_ref[...] + jnp.zeros((tm, d), F32)
        for k in range(k31):
            v1 = v1 + cw31_ref[k:k + 1, :] * _tap(ext31, es31, CONV31_HALO - (k31 - 1) + k, tm)
        ext31[pl.ds(0, CONV31_HALO), :] = ext31[pl.ds(tm, CONV31_HALO), :]
        v1b = v1.astype(BF16)
        v1_ref[...] = v1b
        v1 = v1b.astype(F32)
        xc = v1 - jnp.mean(v1, axis=-1, keepdims=True)
        rstd = lax.rsqrt(jnp.mean(xc * xc, axis=-1, keepdims=True) + EPS)
        v2 = xc * rstd * lng_ref[...] + lnb_ref[...]
        yb = _nn((v2 * _sigmoid(v2)).astype(BF16), w3_v[1]) + bcp_ref[...]
        yb_ref[...] = yb.astype(BF16)

        merged = _sigmoid(gate_a) * ya + _sigmoid(gate_b) * yb
        h2_ref[...] = h_ref[...] + _nn(merged.astype(BF16), w3_v[2])

    row = pl.BlockSpec((tm, d), lambda i: (i, 0))
    wide = pl.BlockSpec((tm, n_in), lambda i: (i, 0))
    full = lambda a: pl.BlockSpec(a.shape, lambda i, nd=a.ndim: (0,) * nd)
    smalls = [cw4, cb4, wg, ba, bx, lam, cw31, cb31, lng, lnb, bcp]
    return _call(
        body, "mixer_fwd", (tp // tm,),
        [row, full(g), full(b_in), _any()] + [full(a) for a in smalls] + [_any()],
        [row, wide] + [row] * 6,
        [jax.ShapeDtypeStruct((tp, d), F32), jax.ShapeDtypeStruct((tp, n_in), BF16)]
        + [jax.ShapeDtypeStruct((tp, d), BF16)] * 6,
        [pltpu.VMEM(win_all.shape, BF16),
         pltpu.VMEM((3, d, d), BF16),
         pltpu.VMEM((tm + CONV4_HALO, d), F32),
         pltpu.VMEM((tm + CONV31_HALO, d), F32),
         pltpu.VMEM((SUBLANES, tm + CONV31_HALO, d), F32),
         pltpu.VMEM((SUBLANES, d), F32),
         pltpu.SemaphoreType.DMA((1 + 3 * N_DEV,))],
        [h, g, b_in, win_all, *smalls, w3_all], comm)


SG_BIN, SG_CW4, SG_CB4, SG_BA, SG_BX, SG_LAM, SG_CB31, SG_LNG, SG_LNB, SG_BCP, SG_MIX, SG_CW31 = 0, 6, 10, 11, 12, 13, 14, 15, 16, 17, 18, 19


def _mixer_bwd(dh2, h, g, proj, xr_s, hs_s, v1_s, ya_s, yb_s, win_t, cw4, wg, ba, bx, lam, cw31, lng, lnb, w3_all, tm,
               comm=None):
    tp, d = dh2.shape
    n_in = proj.shape[1]
    hd = wg.shape[-1]
    k4, k31 = cw4.shape[0], cw31.shape[0]
    nt = tp // tm
    w3_rows = d // N_DEV
    sg_rows = -(-(SG_CW31 + k31) // SUBLANES) * SUBLANES
    halo_rows = 16
    per = tm // halo_rows

    def body(dh_ref, h_ref, g_ref, p_ref, xr_ref, hs_ref, hh_ref, v1_ref, ya_ref, yb_ref, win_hbm,
             cw4_ref, wg_ref, wgt_ref, ba_ref, bx_ref, lam_ref, cw31_ref, lng_ref, lnb_ref, w3_hbm,
             dh1_ref, dp_ref, x3_ref, y3_ref, yg_ref, sg_ref,
             win_v, w3_v, extd4, extd31, es31, gcar, sems):
        i = pl.program_id(0)
        tile = nt - 1 - i

        @pl.when(i == 0)
        def _():
            _load_weights([(win_hbm, win_v)] + _w3_copies(w3_hbm, w3_rows, w3_v), sems)
            for q in range(3):
                w3_v[q] = w3_v[q].T
            extd4[pl.ds(tm, CONV4_HALO), :] = jnp.zeros((CONV4_HALO, d), F32)
            extd31[pl.ds(tm, CONV31_HALO), :] = jnp.zeros((CONV31_HALO, d), F32)
            gcar[...] = jnp.zeros_like(gcar)
            sg_ref[...] = jnp.zeros_like(sg_ref)

        def acc(row, val):
            sg_ref[row:row + 1, :] += _rowsum(val)

        rows = lax.broadcasted_iota(jnp.int32, (tm, d), 0)
        x_rnn = p_ref[:, 0:d].astype(F32)
        y_rnn = p_ref[:, d:2 * d].astype(F32)
        glu_v = p_ref[:, 2 * d:3 * d].astype(F32)
        glu_g = p_ref[:, 3 * d:4 * d].astype(F32)
        sga = _sigmoid(p_ref[:, 4 * d:5 * d].astype(F32))
        sgb = _sigmoid(p_ref[:, 5 * d:6 * d].astype(F32))
        ya = ya_ref[...].astype(F32)
        yb = yb_ref[...].astype(F32)

        dmob = dh_ref[...].astype(BF16)
        dmerged = _nn(dmob, w3_v[2])
        x3_ref[:, 0:d] = (sga * ya + sgb * yb).astype(BF16)
        y3_ref[:, 0:d] = dmob
        dya = sga * dmerged
        dyb = sgb * dmerged
        dn_parts = []

        def emit(q, val):
            vb = val.astype(BF16)
            dp_ref[:, q * d:(q + 1) * d] = vb
            acc(SG_BIN + q, val)
            term = _nn(vb, win_v[pl.ds(q * d, d), :])
            dn_parts[:] = [term if not dn_parts else dn_parts[0] + term]

        emit(4, dmerged * ya * sga * (1.0 - sga))
        emit(5, dmerged * yb * sgb * (1.0 - sgb))

        dyab = dya.astype(BF16)
        y3_ref[:, d:2 * d] = dyab
        dza = _nn(dyab, w3_v[0])
        hsv = hs_ref[...].astype(F32)
        gl, th = _gelu(y_rnn)
        x3_ref[:, d:2 * d] = (hsv * gl).astype(BF16)
        emit(1, dza * hsv * _gelu_grad(y_rnn, th))
        dhs = dza * gl
        xrb = xr_ref[...]
        xr = xrb.astype(F32)
        lam_v = lam_ref[...]
        r, ig, sp, a, s = _gates(xrb, wg_ref, ba_ref[...], bx_ref[...], lam_v, hd)
        b = jnp.where(rows == tm - 1, gcar[1:2, :], pltpu.roll(a, tm - 1, 0))
        big_g = _scan_bwd(b, dhs, gcar[0:1, :])
        gcar[0:1, :] = big_g[0:1, :]
        gcar[1:2, :] = a[0:1, :]
        h_before = jnp.where(tile > 0, hh_ref[halo_rows - 1:halo_rows, :].astype(F32), 0.0)
        h_prev = jnp.where(rows == 0, h_before, pltpu.roll(hsv, 1, 0))
        ds = big_g * ig * xr
        dla = big_g * h_prev * a - ds * (a * a) / jnp.maximum(s, 1e-20)
        acc(SG_LAM, dla * r * (RG_LRU_C * _sigmoid(-lam_v)))
        dpr = dla * (-RG_LRU_C * sp) * r * (1.0 - r)
        dpi = big_g * s * xr * ig * (1.0 - ig)
        acc(SG_BA, dpr)
        acc(SG_BX, dpi)
        dprb = dpr.astype(BF16)
        dpib = dpi.astype(BF16)
        yg_ref[:, 0:d] = dprb
        yg_ref[:, d:2 * d] = dpib
        back = []
        for hh in range(N_HEADS):
            sl = slice(hh * hd, (hh + 1) * hd)
            back.append(_nn(dprb[:, sl], wgt_ref[0, hh]) + _nn(dpib[:, sl], wgt_ref[1, hh]))
        dxr = big_g * s * ig + jnp.concatenate(back, axis=1)
        acc(SG_CB4, dxr)
        extd4[pl.ds(0, tm), :] = dxr
        dx_rnn = jnp.zeros((tm, d), F32)
        for k in range(k4):
            term = extd4[pl.ds(k4 - 1 - k, tm), :]
            dx_rnn = dx_rnn + cw4_ref[k:k + 1, :] * term
            acc(SG_CW4 + k, x_rnn * term)
        extd4[pl.ds(tm, CONV4_HALO), :] = extd4[pl.ds(0, CONV4_HALO), :]
        emit(0, dx_rnn)

        dybb = dyb.astype(BF16)
        y3_ref[:, 2 * d:3 * d] = dybb
        acc(SG_BCP, dyb)
        dv3 = _nn(dybb, w3_v[1])
        v1 = v1_ref[...].astype(F32)
        xc = v1 - jnp.mean(v1, axis=-1, keepdims=True)
        rstd = lax.rsqrt(jnp.mean(xc * xc, axis=-1, keepdims=True) + EPS)
        xhat = xc * rstd
        lng_v = lng_ref[...]
        v2 = xhat * lng_v + lnb_ref[...]
        s2 = _sigmoid(v2)
        x3_ref[:, 2 * d:3 * d] = (v2 * s2).astype(BF16)
        dv2 = dv3 * (s2 * (1.0 + v2 * (1.0 - s2)))
        acc(SG_LNG, dv2 * xhat)
        acc(SG_LNB, dv2)
        dxh = dv2 * lng_v
        dv1 = rstd * (dxh - jnp.mean(dxh, axis=-1, keepdims=True)
                      - xhat * jnp.mean(dxh * xhat, axis=-1, keepdims=True))
        acc(SG_CB31, dv1)
        extd31[pl.ds(0, tm), :] = dv1
        _shifted_copies(extd31, es31, tm + CONV31_HALO - SUBLANES)
        sgg = _sigmoid(glu_g)
        v0 = glu_v * sgg
        dv0 = jnp.zeros((tm, d), F32)
        for k in range(k31):
            term = _tap(extd31, es31, k31 - 1 - k, tm)
            dv0 = dv0 + cw31_ref[k:k + 1, :] * term
            acc(SG_CW31 + k, v0 * term)
        extd31[pl.ds(tm, CONV31_HALO), :] = extd31[pl.ds(0, CONV31_HALO), :]
        emit(2, dv0 * sgg)
        emit(3, dv0 * glu_v * sgg * (1.0 - sgg))

        dn = dn_parts[0]
        x = h_ref[...]
        rr = lax.rsqrt(jnp.mean(x * x, axis=-1, keepdims=True) + EPS)
        dx, dgp = _rms_bwd(dn, x, rr, g_ref[...])
        dh1_ref[...] = dh_ref[...] + dx
        sg_ref[SG_MIX:SG_MIX + 1, :] += dgp

    rev = lambda i: (nt - 1 - i, 0)
    row = pl.BlockSpec((tm, d), rev)
    wide = pl.BlockSpec((tm, n_in), rev)
    full = lambda a: pl.BlockSpec(a.shape, lambda i, nd=a.ndim: (0,) * nd)
    halo = pl.BlockSpec((halo_rows, d), lambda i: (jnp.maximum((nt - 1 - i) * per - 1, 0), 0))
    smalls = [cw4, wg, jnp.swapaxes(wg, 2, 3), ba, bx, lam, cw31, lng, lnb]
    return _call(
        body, "mixer_bwd", (nt,),
        [row, row, full(g), wide, row, row, halo, row, row, row, _any()]
        + [full(a) for a in smalls] + [_any()],
        [row, wide, pl.BlockSpec((tm, 3 * d), rev), pl.BlockSpec((tm, 3 * d), rev),
         pl.BlockSpec((tm, 2 * d), rev), pl.BlockSpec((sg_rows, d), lambda i: (0, 0))],
        [jax.ShapeDtypeStruct((tp, d), F32), jax.ShapeDtypeStruct((tp, n_in), BF16),
         jax.ShapeDtypeStruct((tp, 3 * d), BF16), jax.ShapeDtypeStruct((tp, 3 * d), BF16),
         jax.ShapeDtypeStruct((tp, 2 * d), BF16), jax.ShapeDtypeStruct((sg_rows, d), F32)],
        [pltpu.VMEM(win_t.shape, BF16),
         pltpu.VMEM((3, d, d), BF16),
         pltpu.VMEM((tm + CONV4_HALO, d), F32),
         pltpu.VMEM((tm + CONV31_HALO, d), F32),
         pltpu.VMEM((SUBLANES, tm + CONV31_HALO, d), F32),
         pltpu.VMEM((SUBLANES, d), F32),
         pltpu.SemaphoreType.DMA((1 + 3 * N_DEV,))],
        [dh2, h, g, proj, xr_s, hs_s, hs_s, v1_s, ya_s, yb_s, win_t, *smalls, w3_all], comm)


def _tn_matmul(name, x, y, x_spec, y_spec, n_blocks, kb, nb, tm, tp, out_shape, out_spec, out_view, comm=None,
               after=None):
    nt = tp // tm

    def body(x_ref, y_ref, *refs):
        o_ref, acc = refs[-2:]
        i = pl.program_id(1)

        @pl.when(i == 0)
        def _():
            acc[...] = jnp.zeros_like(acc)

        acc[...] += _tn(x_ref[...], y_ref[...])

        @pl.when(i == nt - 1)
        def _():
            o_ref[...] = acc[...].astype(BF16).reshape(out_view)

    follows = [] if after is None else [after]
    outs, extra = _call(body, name, (n_blocks, nt), [x_spec, y_spec] + [_any()] * len(follows), [out_spec],
                        [jax.ShapeDtypeStruct(out_shape, BF16)], [pltpu.VMEM((kb, nb), F32)], [x, y] + follows,
                        comm)
    return outs[0], extra


def kernel(x, meta_tokens, ffn1_norm, ffn1_w_gu, ffn1_w_down, mix_norm, w_in, b_in, rnn_conv_w, rnn_conv_b, rg_w_a, rg_b_a, rg_w_x, rg_b_x, rg_lambda, rnn_w_proj, conv_dw_w, conv_dw_b, conv_ln_g, conv_ln_b, conv_w_proj, conv_b_proj, w_out, ffn2_norm, ffn2_w_gu, ffn2_w_down, final_norm, loss_target, m_meta_tokens, m_ffn1_norm, m_ffn1_w_gu, m_ffn1_w_down, m_mix_norm, m_w_in, m_b_in, m_rnn_conv_w, m_rnn_conv_b, m_rg_w_a, m_rg_b_a, m_rg_w_x, m_rg_b_x, m_rg_lambda, m_rnn_w_proj, m_conv_dw_w, m_conv_dw_b, m_conv_ln_g, m_conv_ln_b, m_conv_w_proj, m_conv_b_proj, m_w_out, m_ffn2_norm, m_ffn2_w_gu, m_ffn2_w_down, m_final_norm, v_meta_tokens, v_ffn1_norm, v_ffn1_w_gu, v_ffn1_w_down, v_mix_norm, v_w_in, v_b_in, v_rnn_conv_w, v_rnn_conv_b, v_rg_w_a, v_rg_b_a, v_rg_w_x, v_rg_b_x, v_rg_lambda, v_rnn_w_proj, v_conv_dw_w, v_conv_dw_b, v_conv_ln_g, v_conv_ln_b, v_conv_w_proj, v_conv_b_proj, v_w_out, v_ffn2_norm, v_ffn2_w_gu, v_ffn2_w_down, v_final_norm):
    w = dict(locals())
    seq, d = x.shape[1], x.shape[2]
    n_meta = meta_tokens.shape[0]
    t_real = n_meta + seq
    tp, tm, tmx_fwd, tmx, tmt, tmw = _tiles(t_real)
    fb = ffn1_w_gu.shape[-1]
    wr = ffn1_w_down.shape[1]
    f = N_DEV * wr
    fc = f // FFN_CHUNKS
    nbc = w_in.shape[-1]
    n_in = N_DEV * nbc
    pr = rnn_w_proj.shape[1]
    hd = rg_w_a.shape[-1]
    gr = rg_w_a.shape[2]
    cw = meta_tokens.shape[1]
    k4, k31 = rnn_conv_w.shape[1], conv_dw_w.shape[1]
    assert n_in == 6 * d and 2 * wr == fb and N_HEADS * hd == d and pr * N_DEV == d

    xi, yi, ci = lax.axis_index("x"), lax.axis_index("y"), lax.axis_index("c")
    core = ci.astype(jnp.int32).reshape(1)
    chip = (2 * xi + yi).astype(jnp.int32).reshape(1)
    me_index = (4 * xi + 2 * yi + ci).astype(jnp.int32).reshape(1)

    for nm in ("ffn1_w_gu", "ffn2_w_gu"):
        for pre in ("", "m_", "v_"):
            w[pre + nm] = jnp.swapaxes(w[pre + nm], 1, 2)

    wgut1 = w["ffn1_w_gu"][0].astype(BF16)
    wgut2 = w["ffn2_w_gu"][0].astype(BF16)
    wd1 = ffn1_w_down[0].astype(BF16)
    wd2 = ffn2_w_down[0].astype(BF16)
    win_loc = w_in[0].astype(BF16)
    win_t_loc = jnp.swapaxes(w_in[0], 0, 1).astype(BF16)
    w3_loc = jnp.concatenate([rnn_w_proj[0], conv_w_proj[0], w_out[0]], axis=0).astype(BF16)
    wg_loc = jnp.stack([rg_w_a[0], rg_w_x[0]]).astype(BF16)
    n_small = n_meta + k4 + k31
    small_rows = -(-n_small // SUBLANES) * SUBLANES
    small_loc = jnp.concatenate([meta_tokens, rnn_conv_w[0], conv_dw_w[0],
                                 jnp.zeros((small_rows - n_small, cw), F32)], axis=0)
    (wgut1_all, wd1_all, small_all), h0, tgt = _first_gather(
        [wgut1, wd1, small_loc], 2, x[0], loss_target[0], n_meta, tp)
    small_full = small_all.transpose(1, 0, 2).reshape(small_rows, d)
    cw4 = small_full[n_meta:n_meta + k4]
    cw31 = small_full[n_meta + k4:n_meta + k4 + k31]

    wgu1, wdn1 = wgut1_all.reshape(2 * f, d), wd1_all.reshape(f, d)
    (h1, gu1, n1), (win_all, w3_all, wg_all) = _ffn_fwd(
        h0, ffn1_norm, wgu1, wdn1, tm, comm=_Gather([win_loc, w3_loc, wg_loc], pass_on_at=(0.65, 0.95)))
    wg = wg_all.transpose(1, 2, 0, 3, 4).reshape(2, N_HEADS, hd, hd)
    (h2, proj, n2, xr_s, hs_s, v1_s, ya_s, yb_s), (wgut2_all, wd2_all, win_t_all) = _mixer_fwd(
        h1, mix_norm, b_in, win_all, cw4, rnn_conv_b, wg, rg_b_a, rg_b_x, rg_lambda, cw31, conv_dw_b, conv_ln_g,
        conv_ln_b, conv_b_proj, w3_all, tmx_fwd, comm=_Gather([wgut2, wd2, win_t_loc], pass_on_at=(0.45, 0.7)))
    wgu2, wdn2 = wgut2_all.reshape(2 * f, d), wd2_all.reshape(f, d)
    win_t = win_t_all.reshape(n_in, d)
    (dh3, gu2, n3, tail), _ = _ffn_fwd(
        h2, ffn2_norm, wgu2, wdn2, tm, loss=(tgt, final_norm.reshape(1, d), n_meta, t_real))

    def d_w_gu(tag, dgu, n_s, after=None):
        g, _ = _tn_matmul(
            "d_w_gu" + tag, dgu, n_s,
            pl.BlockSpec((None, tmt, fc), lambda b, i: (b // FFN_CHUNKS, i, b % FFN_CHUNKS)),
            pl.BlockSpec((tmt, d), lambda b, i: (i, 0)),
            2 * FFN_CHUNKS, fc, d, tmt, tp, (2 * FFN_CHUNKS, fc, d),
            pl.BlockSpec((None, fc, d), lambda b, i: (b, 0, 0)), (fc, d), after=after)
        return g.reshape(N_DEV, fb, d)

    def d_w_down(tag, act, df):
        g, _ = _tn_matmul(
            "d_w_down" + tag, act, df,
            pl.BlockSpec((tmt, fc), lambda b, i: (i, b)), pl.BlockSpec((tmt, d), lambda b, i: (i, 0)),
            FFN_CHUNKS, fc, d, tmt, tp, (FFN_CHUNKS, fc, d),
            pl.BlockSpec((None, fc, d), lambda b, i: (b, 0, 0)), (fc, d))
        return g.reshape(N_DEV, wr, d)

    (dh2, dgu2, act2, df2, tail), _ = _ffn_bwd(dh3, h2, gu2, ffn2_norm, wgu2, wdn2, tm, tail, TAIL_FFN2, n3)
    g_wgu2 = d_w_gu("2", dgu2, n3)
    g_wd2 = d_w_down("2", act2, df2)
    (dh1, dproj, x3, y3, yg, sg), (r_wd2, r_wgu2) = _mixer_bwd(
        dh2, h1, mix_norm, proj, xr_s, hs_s, v1_s, ya_s, yb_s, win_t, cw4, wg, rg_b_a, rg_b_x, rg_lambda, cw31,
        conv_ln_g, conv_ln_b, w3_all, tmx, comm=_Scatter([g_wd2, g_wgu2]))
    g_w3, _ = _tn_matmul(
        "d_w_proj3", x3, y3,
        pl.BlockSpec((tmw, d), lambda b, i: (i, b)), pl.BlockSpec((tmw, d), lambda b, i: (i, b)),
        3, d, d, tmw, tp, (N_DEV, 3, pr, d), pl.BlockSpec((N_DEV, None, pr, d), lambda b, i: (0, b, 0, 0)),
        (N_DEV, pr, d))
    g_wg, _ = _tn_matmul(
        "d_w_gates", xr_s, yg,
        pl.BlockSpec((tmw, hd), lambda b, i: (i, b % N_HEADS)), pl.BlockSpec((tmw, hd), lambda b, i: (i, b)),
        2 * N_HEADS, hd, hd, tmw, tp, (N_DEV, 2 * N_HEADS, gr, hd),
        pl.BlockSpec((N_DEV, None, gr, hd), lambda b, i: (0, b, 0, 0)), (N_DEV, gr, hd))
    ((wg_sems, g_wg_thru, wg_land), (w3_sems, g_w3_thru, w3_land)), w3_token = _exchanges_start(
        "grads_proj3_exchange",
        [(_scatter_copies, 2 * (N_DEV - 1), g_wg, (N_DEV - 1,) + g_wg.shape[1:]),
         (_scatter_copies, 2 * (N_DEV - 1), g_w3, (N_DEV - 1,) + g_w3.shape[1:])])
    g_win, _ = _tn_matmul(
        "d_w_in", n2, dproj,
        pl.BlockSpec((tmw, d), lambda b, i: (i, 0)), pl.BlockSpec((tmw, nbc), lambda b, i: (i, b)),
        N_DEV, d, nbc, tmw, tp, (N_DEV, d, nbc), pl.BlockSpec((None, d, nbc), lambda b, i: (b, 0, 0)), (d, nbc),
        after=w3_token)
    win_sems, g_win_thru, win_land, win_token = _exchange_start("grads_w_in_exchange", _scatter_copies, N_DEV - 1, g_win)
    (dh0, dgu1, act1, df1, tail), _ = _ffn_bwd(dh1, h0, gu1, ffn1_norm, wgu1, wdn1, tm, tail, TAIL_FFN1, win_token)

    pieces = [sg, dh0[:n_meta], tail]
    assert all(p.shape[0] % SUBLANES == 0 for p in pieces)
    at = [0, sg.shape[0], sg.shape[0] + n_meta]
    loss_row = at[2] + TAIL_LOSS
    rep_rows = [("ffn1_norm", at[2] + TAIL_FFN1, 1), ("mix_norm", SG_MIX, 1), ("b_in", SG_BIN, 6),
                ("rnn_conv_b", SG_CB4, 1),
                ("rg_b_a", SG_BA, 1), ("rg_b_x", SG_BX, 1), ("rg_lambda", SG_LAM, 1), ("conv_dw_b", SG_CB31, 1),
                ("conv_ln_g", SG_LNG, 1), ("conv_ln_b", SG_LNB, 1), ("conv_b_proj", SG_BCP, 1),
                ("ffn2_norm", at[2] + TAIL_FFN2, 1), ("final_norm", at[2] + TAIL_FINAL, 1)]
    col_rows = [("meta_tokens", at[1], n_meta), ("rnn_conv_w", SG_CW4, k4), ("conv_dw_w", SG_CW31, k31)]
    layout = []
    for nm, row0, nr in rep_rows:
        kind = "wide" if nm == "b_in" else "rep"
        as2d = lambda a: a.reshape(1, -1) if a.ndim == 1 else a
        layout.append((kind, row0, nr, as2d(w[nm]), as2d(w["m_" + nm]), as2d(w["v_" + nm])))
    for nm, row0, nr in col_rows:
        sq = lambda a: a.reshape(a.shape[-2], a.shape[-1])
        layout.append(("col", row0, nr, sq(w[nm]), sq(w["m_" + nm]), sq(w["v_" + nm])))
    small_partial = jnp.concatenate(pieces, axis=0)

    g_wd1 = d_w_down("1", act1, df1)
    ((small_sems, small_thru, small_land), (wd1_sems, g_wd1_thru, wd1_land)), wd1_token = _exchanges_start(
        "grads_w_down1_exchange",
        [(_bcast_copies, 2 * (N_DEV - 1) + 1, small_partial, (N_DEV,) + small_partial.shape),
         (_scatter_copies, 2 * (N_DEV - 1), g_wd1, (N_DEV - 1,) + g_wd1.shape[1:])])
    g_wgu1 = d_w_gu("1", dgu1, n1, after=wd1_token)

    g_last = g_wgu1.reshape((4, 2) + g_wgu1.shape[1:])
    comb_wgu1 = _pair_reduce(g_last, core)
    sems, comb_thru, land_thru, after = _exchange_start("grads_chip_exchange", _chip_copies, 3, comb_wgu1)
    g_win, r_win = _exchange_wait("grads_w_in_exchange", _scatter_copies, win_sems, g_win_thru, win_land, after)
    g_w3, r_w3 = _exchange_wait("grads_proj3_exchange", _scatter_copies, w3_sems, g_w3_thru, w3_land, after)
    g_wd1, r_wd1 = _exchange_wait("grads_w_down1_exchange", _scatter_copies, wd1_sems, g_wd1_thru, wd1_land, after)
    g_wg, r_wg = _exchange_wait("grads_gates_exchange", _scatter_copies, wg_sems, g_wg_thru, wg_land, after)
    _, small_partials = _exchange_wait("grads_small_exchange", _bcast_copies, small_sems, small_thru, small_land, after)

    grad_x = (dh0[n_meta:t_real] + after[0, 0])[None]
    total, small_out = _small_adamw([small_partials], layout, me_index, grad_x)
    after = total

    groups = [(g_wd1, r_wd1, me_index, ["ffn1_w_down"]),
              (g_wd2, r_wd2, me_index, ["ffn2_w_down"]), (g_wgu2, r_wgu2, me_index, ["ffn2_w_gu"]),
              (g_win, r_win, me_index, ["w_in"]), (g_w3, r_w3, me_index, ["w_out", "rnn_w_proj", "conv_w_proj"]),
              (g_wg, r_wg, me_index, ["rg_w_a", "rg_w_x"]), (None, None, chip, ["ffn1_w_gu"])]
    res = {}
    for own, recv, idx, group in groups:
        if own is None:
            own, recv = _exchange_wait("grads_chip_exchange", _chip_copies, sems, comb_thru, land_thru, after)
        outs = _final_adamw(own, recv, idx, [(w[nm], w["m_" + nm], w["v_" + nm]) for nm in group], after)
        after = outs[-1][0]
        for nm, o in zip(group, outs):
            res[nm] = o
    for nm in ("ffn1_w_gu", "ffn2_w_gu"):
        res[nm] = tuple(jnp.swapaxes(a, 1, 2) for a in res[nm])
    for (nm, _, _), o in zip(rep_rows + col_rows, small_out):
        res[nm] = tuple(a.reshape(w[nm].shape) for a in o)


    order = ["meta_tokens", "ffn1_norm", "ffn1_w_gu", "ffn1_w_down", "mix_norm", "w_in", "b_in", "rnn_conv_w",
             "rnn_conv_b", "rg_w_a", "rg_b_a", "rg_w_x", "rg_b_x", "rg_lambda", "rnn_w_proj", "conv_dw_w",
             "conv_dw_b", "conv_ln_g", "conv_ln_b", "conv_w_proj", "conv_b_proj", "w_out", "ffn2_norm",
             "ffn2_w_gu", "ffn2_w_down", "final_norm"]
    return (total[loss_row, 0], grad_x, *[res[nm][0] for nm in order], *[res[nm][1] for nm in order],
            *[res[nm][2] for nm in order], *[res[nm][3] for nm in order])
```

```python
import functools
import math

import jax
import jax.numpy as jnp
from jax import lax
from jax.experimental import pallas as pl
from jax.experimental.pallas import tpu as pltpu

F32 = jnp.float32
BF16 = jnp.bfloat16
MESH = pl.DeviceIdType.MESH
N_DEV = 8
N_HEADS = 4
RG_LRU_C = 8.0
EPS = 1e-6
FFN_RES = 0.5
ADAM_LR, ADAM_B1, ADAM_B2, ADAM_EPS, ADAM_WD, ADAM_STEP = 0.001, 0.9, 0.999, 1e-08, 0.01, 10
V7X_VMEM_LIMIT = 56 * 1024 * 1024
CONV4_HALO = 8
CONV31_HALO = 32
SUBLANES = 8
STAGE_ROWS = 512
TAIL_FFN1, TAIL_FINAL, TAIL_LOSS, TAIL_FFN2 = 0, 1, 2, 3
FFN_CHUNKS = 2
FFN_FWD_CHUNKS = 1
GELU_C = math.sqrt(2.0 / math.pi)
GELU_K = 0.044715


def _any():
    return pl.BlockSpec(memory_space=pl.ANY)


def _params(n_grid):
    return pltpu.CompilerParams(dimension_semantics=("arbitrary",) * n_grid, vmem_limit_bytes=V7X_VMEM_LIMIT)


def _nn(a, b):
    return jnp.dot(a, b, preferred_element_type=F32)


def _nt(a, b):
    return lax.dot_general(a, b, (((1,), (1,)), ((), ())), preferred_element_type=F32)


def _tn(a, b):
    return lax.dot_general(a, b, (((0,), (0,)), ((), ())), preferred_element_type=F32)


def _sigmoid(x):
    return 0.5 * jnp.tanh(0.5 * x) + 0.5


def _rowsum(x):
    return jnp.sum(x, axis=0, keepdims=True)


def _rms_fwd(x, g):
    r = lax.rsqrt(jnp.mean(x * x, axis=-1, keepdims=True) + EPS)
    return x * r * g, r


def _rms_bwd(dn, x, r, g):
    xr = x * r
    gy = dn * g
    dx = r * (gy - xr * jnp.mean(gy * xr, axis=-1, keepdims=True))
    return dx, _rowsum(dn * xr)


def _gelu(y):
    t = jnp.tanh(GELU_C * (y + GELU_K * y * y * y))
    return 0.5 * y * (1.0 + t), t


def _gelu_grad(y, t):
    return 0.5 * (1.0 + t) + 0.5 * y * (1.0 - t * t) * GELU_C * (1.0 + 3.0 * GELU_K * y * y)


def _softplus(x):
    return jnp.maximum(x, 0.0) + jnp.log(1.0 + jnp.exp(-jnp.abs(x)))


def _one_minus_exp(z):
    series = -z * (1.0 + 0.5 * z * (1.0 + z * (1.0 / 3.0) * (1.0 + 0.25 * z)))
    return jnp.where(z > -0.05, series, 1.0 - jnp.exp(z))


def _tiles(t_real):
    if t_real > 2048:
        tm = 416
        tp = -(-t_real // tm) * tm
        return tp, tm, tm // 2, tm // 2, tp, tp
    tm = 128
    tp = -(-t_real // tm) * tm
    return tp, tm, tm // 2, tm // 2, tm, tm


def _load_weights(copies, sems):
    cps = [pltpu.make_async_copy(s, d, sems.at[k]) for k, (s, d) in enumerate(copies)]
    for cp in cps:
        cp.start()
    for cp in cps:
        cp.wait()


def _position():
    x, y, c = lax.axis_index("x"), lax.axis_index("y"), lax.axis_index("c")
    chips = [(1 - x, y), (x, 1 - y), (1 - x, 1 - y)]
    return x, y, c, chips


def _slot(p):
    return 4 * p[0] + 2 * p[1] + p[2]


class _Lazy(dict):
    def __getitem__(self, key):
        val = dict.__getitem__(self, key)
        return val() if callable(val) else val


class _Gather:
    def __init__(self, shards, pass_on_at=None):
        self.shards = list(shards)
        self.n = len(self.shards)
        self.pass_on_at = pass_on_at

    def inputs(self):
        return self.shards

    def out_shape(self):
        return [jax.ShapeDtypeStruct((N_DEV,) + s.shape, s.dtype) for s in self.shards]

    N_SEMS = 9

    def scratch(self):
        return [pltpu.SemaphoreType.DMA((self.N_SEMS * self.n,)), pltpu.SemaphoreType.DMA((self.N_SEMS * self.n,)),
                pltpu.SemaphoreType.DMA((self.n,))]

    def _plan(self, ins, outs, sems):
        send_sems, recv_sems, local_sems = sems
        x, y, c, _ = _position()
        me, sib, xn, yn, dg = (x, y, c), (x, y, 1 - c), (1 - x, y, c), (x, 1 - y, c), (1 - x, 1 - y, c)
        other = lambda p: (p[0], p[1], 1 - c)

        def blk(a, p, half=None):
            ref = outs[a].at[_slot(p)]
            if half is None:
                return ref
            rows = self.shards[a].shape[0] // 2
            return ref.at[pl.ds(half * rows, rows)]

        def copy(a, k, dst, to, src=None):
            return pltpu.make_async_remote_copy(
                src_ref=dst if src is None else src, dst_ref=dst,
                send_sem=send_sems.at[self.N_SEMS * a + k], recv_sem=recv_sems.at[self.N_SEMS * a + k],
                device_id=to, device_id_type=MESH)

        cp = _Lazy(mine=lambda: [pltpu.make_async_copy(ins[a], blk(a, me), local_sems.at[a]) for a in range(self.n)])
        for a in range(self.n):
            cp[a] = _Lazy(
                own=lambda a=a: [copy(a, 0, blk(a, me), sib, src=ins[a]), copy(a, 1, blk(a, me), xn, src=ins[a]),
                                 copy(a, 2, blk(a, me), yn, src=ins[a])],
                from_x=lambda a=a: copy(a, 1, blk(a, xn), me), from_y=lambda a=a: copy(a, 2, blk(a, yn), me),
                relay_x=lambda a=a: copy(a, 3, blk(a, xn, 0), yn), relay_y=lambda a=a: copy(a, 4, blk(a, yn, 1), xn),
                diag0=lambda a=a: copy(a, 3, blk(a, dg, 0), me), diag1=lambda a=a: copy(a, 4, blk(a, dg, 1), me),
                pass_x=lambda a=a: copy(a, 5, blk(a, xn), sib), pass_y=lambda a=a: copy(a, 6, blk(a, yn), sib),
                pass_d0=lambda a=a: copy(a, 7, blk(a, dg, 0), sib), pass_d1=lambda a=a: copy(a, 8, blk(a, dg, 1), sib),
                from_sib=lambda a=a: [copy(a, 0, blk(a, sib), me), copy(a, 5, blk(a, other(xn)), me),
                                      copy(a, 6, blk(a, other(yn)), me), copy(a, 7, blk(a, other(dg), 0), me),
                                      copy(a, 8, blk(a, other(dg), 1), me)])
        return cp

    def start(self, ins, outs, sems):
        cp = self._plan(ins, outs, sems)
        for c in cp["mine"]:
            c.start()
        for a in range(self.n):
            for c in cp[a]["own"]:
                c.start()

    def pass_on(self, ins, outs, sems):
        cp = self._plan(ins, outs, sems)
        for a in range(self.n):
            cp[a]["from_x"].wait_recv()
            cp[a]["relay_x"].start()
            cp[a]["pass_x"].start()
        for a in range(self.n):
            cp[a]["from_y"].wait_recv()
            cp[a]["relay_y"].start()
            cp[a]["pass_y"].start()

    def pass_on_relayed(self, ins, outs, sems):
        cp = self._plan(ins, outs, sems)
        for a in range(self.n):
            cp[a]["diag0"].wait_recv()
            cp[a]["pass_d0"].start()
            cp[a]["diag1"].wait_recv()
            cp[a]["pass_d1"].start()

    def finish(self, ins, outs, sems):
        if self.pass_on_at is None:
            self.pass_on(ins, outs, sems)
            self.pass_on_relayed(ins, outs, sems)
        cp = self._plan(ins, outs, sems)
        for a in range(self.n):
            for c in cp[a]["from_sib"]:
                c.wait_recv()
            for c in cp[a]["own"] + [cp[a][k] for k in ("relay_x", "relay_y", "pass_x", "pass_y", "pass_d0", "pass_d1")]:
                c.wait_send()
        for c in cp["mine"]:
            c.wait()


class _Scatter:
    def __init__(self, grads):
        self.grads = list(grads)
        self.n = len(self.grads)

    def inputs(self):
        return self.grads

    def out_shape(self):
        return [jax.ShapeDtypeStruct((N_DEV - 1,) + g.shape[1:], g.dtype) for g in self.grads]

    def scratch(self):
        return [pltpu.SemaphoreType.DMA((7 * self.n,)), pltpu.SemaphoreType.DMA((7 * self.n,))]

    def _plan(self, ins, outs, sems):
        send_sems, recv_sems = sems
        x, y, c, _ = _position()
        cps = []
        for a in range(self.n):
            for k in range(1, N_DEV):
                peer = (x ^ (k >> 2), y ^ ((k >> 1) & 1), c ^ (k & 1))
                cps.append(pltpu.make_async_remote_copy(
                    src_ref=ins[a].at[_slot(peer)], dst_ref=outs[a].at[k - 1],
                    send_sem=send_sems.at[7 * a + k - 1], recv_sem=recv_sems.at[7 * a + k - 1],
                    device_id=peer, device_id_type=MESH))
        return cps

    def start(self, ins, outs, sems):
        for cp in self._plan(ins, outs, sems):
            cp.start()

    def finish(self, ins, outs, sems):
        for cp in self._plan(ins, outs, sems):
            cp.wait()


def _hosted(inner, n_in, n_out, comm, grid):
    if comm is None:
        return inner
    nc_in, nc_out, ns = len(comm.inputs()), len(comm.out_shape()), len(comm.scratch())

    def body(*refs):
        o0 = n_in + nc_in
        s0 = o0 + n_out + nc_out
        main = refs[:n_in] + refs[o0:o0 + n_out] + refs[s0:len(refs) - ns]
        c_in, c_out, c_sems = refs[n_in:o0], refs[o0 + n_out:s0], refs[len(refs) - ns:]
        ids = [pl.program_id(ax) for ax in range(len(grid))]
        first = functools.reduce(jnp.logical_and, [i == 0 for i in ids])
        last = functools.reduce(jnp.logical_and, [i == g - 1 for i, g in zip(ids, grid)])

        @pl.when(first)
        def _():
            comm.start(c_in, c_out, c_sems)

        inner(*main)

        if getattr(comm, "pass_on_at", None) is not None:
            assert len(grid) == 1
            first_at, second_at = (min(grid[0] - 1, int(frac * grid[0])) for frac in comm.pass_on_at)
            assert first_at < second_at

            @pl.when(ids[0] == first_at)
            def _():
                comm.pass_on(c_in, c_out, c_sems)

            @pl.when(ids[0] == second_at)
            def _():
                comm.pass_on_relayed(c_in, c_out, c_sems)

        @pl.when(last)
        def _():
            comm.finish(c_in, c_out, c_sems)

    return body


def _call(inner, name, grid, in_specs, out_specs, out_shape, scratch, args, comm=None):
    n_in, n_out = len(args), len(out_shape)
    body = _hosted(inner, n_in, n_out, comm, grid)
    if comm is not None:
        in_specs = list(in_specs) + [_any()] * len(comm.inputs())
        args = list(args) + comm.inputs()
        out_specs = list(out_specs) + [_any()] * len(comm.out_shape())
        out_shape = list(out_shape) + comm.out_shape()
        scratch = list(scratch) + comm.scratch()
    outs = pl.pallas_call(
        body, name=name, grid=grid, in_specs=list(in_specs), out_specs=list(out_specs), out_shape=list(out_shape),
        scratch_shapes=list(scratch), compiler_params=_params(len(grid)))(*args)
    return list(outs[:n_out]), list(outs[n_out:])


def _first_gather(shards, small_idx, x2, t2, n_meta, tp):
    comm = _Gather(shards)
    n = comm.n
    seq, d = x2.shape
    t_real = n_meta + seq
    n_pad = tp - t_real
    cw = d // N_DEV
    rows = STAGE_ROWS if seq % STAGE_ROWS == 0 else seq
    n_chunks = seq // rows

    def body(*refs):
        ins, (x_ref, t_ref) = refs[:n], refs[n:n + 2]
        outs, (h0_ref, tg_ref) = refs[n + 2:2 * n + 2], refs[2 * n + 2:2 * n + 4]
        sems = refs[2 * n + 4:2 * n + 7]
        buf, zeros, in_sems, out_sems, misc_sems = refs[2 * n + 7:]
        comm.start(ins, outs, sems)
        zeros[...] = jnp.zeros_like(zeros)
        fills = [pltpu.make_async_copy(zeros.at[pl.ds(0, n_pad)], h0_ref.at[pl.ds(t_real, n_pad)], misc_sems.at[0]),
                 pltpu.make_async_copy(zeros.at[pl.ds(0, n_pad)], tg_ref.at[pl.ds(t_real, n_pad)], misc_sems.at[1]),
                 pltpu.make_async_copy(zeros.at[pl.ds(0, n_meta)], tg_ref.at[pl.ds(0, n_meta)], misc_sems.at[2])]
        for cp in fills:
            cp.start()
        jobs = [(src, dst, c) for src, dst in ((x_ref, h0_ref), (t_ref, tg_ref)) for c in range(n_chunks)]

        def load(k):
            src, _, c = jobs[k]
            return pltpu.make_async_copy(src.at[pl.ds(c * rows, rows)], buf.at[k % 2], in_sems.at[k % 2])

        def store(k):
            _, dst, c = jobs[k]
            return pltpu.make_async_copy(buf.at[k % 2], dst.at[pl.ds(n_meta + c * rows, rows)], out_sems.at[k % 2])

        load(0).start()
        for k in range(len(jobs)):
            load(k).wait()
            if k + 1 < len(jobs):
                if k >= 1:
                    store(k - 1).wait()
                load(k + 1).start()
            store(k).start()
        for k in range(max(0, len(jobs) - 2), len(jobs)):
            store(k).wait()
        comm.finish(ins, outs, sems)
        meta = [pltpu.make_async_copy(outs[small_idx].at[k, pl.ds(0, n_meta)],
                                      h0_ref.at[pl.ds(0, n_meta), pl.ds(k * cw, cw)], misc_sems.at[3 + k])
                for k in range(N_DEV)]
        for cp in meta:
            cp.start()
        for cp in fills + meta:
            cp.wait()

    staged = [jax.ShapeDtypeStruct((tp, d), F32)] * 2
    outs = pl.pallas_call(
        body, name="weights_all_gather", out_shape=comm.out_shape() + staged,
        in_specs=[_any()] * (n + 2), out_specs=[_any()] * (n + 2),
        scratch_shapes=comm.scratch() + [
            pltpu.VMEM((2, rows, d), F32), pltpu.VMEM((max(n_pad, n_meta), d), F32),
            pltpu.SemaphoreType.DMA((2,)), pltpu.SemaphoreType.DMA((2,)), pltpu.SemaphoreType.DMA((3 + N_DEV,))],
        compiler_params=pltpu.CompilerParams(vmem_limit_bytes=V7X_VMEM_LIMIT),
    )(*shards, x2, t2)
    return outs[:n], outs[n], outs[n + 1]


def _chip_copies(c_ref, land_ref, sems):
    _, _, c, chips = _position()
    return [pltpu.make_async_remote_copy(
        src_ref=c_ref.at[2 * cx + cy], dst_ref=land_ref.at[j], send_sem=sems[j], recv_sem=sems[3 + j],
        device_id=(cx, cy, c), device_id_type=MESH) for j, (cx, cy) in enumerate(chips)]


def _scatter_copies(g_ref, land_ref, sems):
    x, y, c, _ = _position()
    cps = []
    for k in range(1, N_DEV):
        peer = (x ^ (k >> 2), y ^ ((k >> 1) & 1), c ^ (k & 1))
        cps.append(pltpu.make_async_remote_copy(
            src_ref=g_ref.at[_slot(peer)], dst_ref=land_ref.at[k - 1], send_sem=sems[k - 1],
            recv_sem=sems[N_DEV - 1 + k - 1], device_id=peer, device_id_type=MESH))
    return cps


def _bcast_copies(b_ref, land_ref, sems):
    x, y, c, _ = _position()
    mine = land_ref.at[_slot((x, y, c))]
    cps = []
    for k in range(1, N_DEV):
        peer = (x ^ (k >> 2), y ^ ((k >> 1) & 1), c ^ (k & 1))
        cps.append(pltpu.make_async_remote_copy(
            src_ref=b_ref, dst_ref=mine, send_sem=sems[k - 1], recv_sem=sems[N_DEV - 1 + k - 1],
            device_id=peer, device_id_type=MESH))
    return cps + [pltpu.make_async_copy(b_ref, mine, sems[2 * (N_DEV - 1)])]


def _exchanges_start(name, parts):
    hbm = pl.BlockSpec(memory_space=pltpu.HBM)
    sem = pl.BlockSpec(memory_space=pltpu.SEMAPHORE)
    n_parts = len(parts)
    total = sum(n for _, n, _, _ in parts)

    def body(*refs):
        ins, sems, token = refs[:2 * n_parts], refs[2 * n_parts:2 * n_parts + total], refs[4 * n_parts + total]
        at = 0
        for j, (copies, n, _, _) in enumerate(parts):
            for cp in copies(ins[2 * j], ins[2 * j + 1], sems[at:at + n]):
                cp.start()
            at += n
        token[...] = jnp.zeros_like(token)

    flat = []
    for _, _, src, land_shape in parts:
        flat += [pltpu.with_memory_space_constraint(src, pltpu.HBM),
                 pltpu.with_memory_space_constraint(lax.empty(land_shape, src.dtype), pltpu.HBM)]
    outs = pl.pallas_call(
        body, name=name + "_start",
        out_shape=(pltpu.SemaphoreType.DMA(()),) * total + tuple(pltpu.HBM(a.shape, a.dtype) for a in flat)
        + (jax.ShapeDtypeStruct((SUBLANES, 128), F32),),
        in_specs=(hbm,) * len(flat),
        out_specs=(sem,) * total + (hbm,) * len(flat) + (pl.BlockSpec(memory_space=pltpu.VMEM),),
        input_output_aliases={j: total + j for j in range(len(flat))},
        compiler_params=pltpu.CompilerParams(has_side_effects=pltpu.SideEffectType.DATAFLOW_SIDE_EFFECTING),
    )(*flat)
    res, at = [], 0
    for j, (_, n, _, _) in enumerate(parts):
        res.append((outs[at:at + n], outs[total + 2 * j], outs[total + 2 * j + 1]))
        at += n
    return res, outs[total + len(flat)]


def _exchange_start(name, copies, n_copies, src):
    (part,), token = _exchanges_start(name, [(copies, 2 * n_copies, src, (n_copies,) + src.shape[1:])])
    return (*part, token)


def _exchange_wait(name, copies, sems, src_thru, land_thru, after):
    hbm = pl.BlockSpec(memory_space=pltpu.HBM)
    sem = pl.BlockSpec(memory_space=pltpu.SEMAPHORE)
    n_sems = len(sems)

    def body(s_ref, land_ref, *refs):
        for cp in copies(s_ref, land_ref, refs[:n_sems]):
            cp.wait()

    return pl.pallas_call(
        body, name=name + "_wait",
        out_shape=(pltpu.HBM(src_thru.shape, src_thru.dtype), pltpu.HBM(land_thru.shape, land_thru.dtype)),
        in_specs=(hbm, hbm) + (sem,) * n_sems + (pl.BlockSpec(memory_space=pl.ANY),), out_specs=(hbm, hbm),
        input_output_aliases={0: 0, 1: 1},
        compiler_params=pltpu.CompilerParams(has_side_effects=pltpu.SideEffectType.DATAFLOW_SIDE_EFFECTING),
    )(src_thru, land_thru, *sems, after)


def _pair_reduce(grad, core):
    blk = grad.shape[2:]
    zeros = (0,) * len(blk)

    def body(core_ref, g_hbm, own_ref, o_ref, landed, send_sems, recv_sems):
        del core_ref
        i = pl.program_id(0)
        x, y, c, _ = _position()

        def copy(k):
            return pltpu.make_async_remote_copy(
                src_ref=g_hbm.at[k, 1 - c], dst_ref=landed.at[k], send_sem=send_sems.at[k],
                recv_sem=recv_sems.at[k], device_id=(x, y, 1 - c), device_id_type=MESH)

        @pl.when(i == 0)
        def _():
            for k in range(4):
                copy(k).start()

        for k in range(4):
            @pl.when(i == k)
            def _(k=k):
                copy(k).wait_recv()

        o_ref[...] = (own_ref[...].astype(F32) + landed[i].astype(F32)).astype(BF16)

        @pl.when(i == 3)
        def _():
            for k in range(4):
                copy(k).wait_send()

    return pl.pallas_call(
        body, name="grads_pair_reduce",
        out_shape=jax.ShapeDtypeStruct((4,) + blk, BF16),
        grid_spec=pltpu.PrefetchScalarGridSpec(
            num_scalar_prefetch=1, grid=(4,),
            in_specs=[_any(), pl.BlockSpec((None, None) + blk, lambda i, cr: (i, cr[0]) + zeros)],
            out_specs=pl.BlockSpec((None,) + blk, lambda i, cr: (i,) + zeros),
            scratch_shapes=[pltpu.VMEM((4,) + blk, BF16), pltpu.SemaphoreType.DMA((4,)),
                            pltpu.SemaphoreType.DMA((4,))]),
        compiler_params=_params(1),
    )(core, grad, grad)


def _adamw(w, g, m, v):
    m2 = ADAM_B1 * m + (1.0 - ADAM_B1) * g
    v2 = ADAM_B2 * v + (1.0 - ADAM_B2) * (g * g)
    m_hat = m2 / (1.0 - ADAM_B1 ** ADAM_STEP)
    v_hat = v2 / (1.0 - ADAM_B2 ** ADAM_STEP)
    delta = -ADAM_LR * (m_hat / (jnp.sqrt(v_hat) + ADAM_EPS) + ADAM_WD * w)
    return delta, m2, v2


def _final_adamw(own, recv, idx, parts, after):
    blk = own.shape[1:]
    n_recv = recv.shape[0]
    n_parts = len(parts)
    per = blk[0] // n_parts if n_parts > 1 else None
    rows = blk[-2]
    n_chunks = 1 if n_parts > 1 else (4 if rows % 64 == 0 and rows >= 512 else (2 if rows % 32 == 0 else 1))
    cblk = blk[:-2] + (rows // n_chunks, blk[-1])
    lead = (0,) * (len(blk) - 2)

    def body(idx_ref, c_ref, r_ref, after_ref, *refs):
        del idx_ref, after_ref
        ins, outs = refs[:3 * n_parts], refs[3 * n_parts:]
        g = c_ref[...].astype(F32)
        for k in range(n_recv):
            g = g + r_ref[k].astype(F32)
        for p in range(n_parts):
            w_ref, m_ref, v_ref = ins[3 * p:3 * p + 3]
            if n_parts == 1:
                gp = g
            elif per == 1:
                gp = g[p]
            else:
                gp = g[p * per:(p + 1) * per]
            delta, m2, v2 = _adamw(w_ref[0], gp, m_ref[0], v_ref[0])
            o = outs[4 * p:4 * p + 4]
            o[0][0] = gp
            o[1][0] = delta
            o[2][0] = m2
            o[3][0] = v2

    flat = [a for wmv in parts for a in wmv]

    def part_spec(a):
        shape = a.shape[:-2] + (a.shape[-2] // n_chunks, a.shape[-1])
        return pl.BlockSpec(shape, lambda i, cr, nd=a.ndim: (0,) * (nd - 2) + (i, 0))

    outs = pl.pallas_call(
        body, name="grads_sum_adamw",
        out_shape=[jax.ShapeDtypeStruct(wmv[0].shape, F32) for wmv in parts for _ in range(4)],
        grid_spec=pltpu.PrefetchScalarGridSpec(
            num_scalar_prefetch=1, grid=(n_chunks,),
            in_specs=[pl.BlockSpec((None,) + cblk, lambda i, cr: (cr[0],) + lead + (i, 0)),
                      pl.BlockSpec((n_recv,) + cblk, lambda i, cr: (0,) + lead + (i, 0))]
                     + [_any()] + [part_spec(a) for a in flat],
            out_specs=[part_spec(wmv[0]) for wmv in parts for _ in range(4)]),
        compiler_params=_params(1),
    )(idx, own, recv, after, *flat)
    return [tuple(outs[4 * p:4 * p + 4]) for p in range(n_parts)]


def _small_adamw(partials, layout, me_index, after):
    d = partials[0].shape[-1]
    rows = sum(p.shape[1] for p in partials)
    n = len(layout)
    n_p = len(partials)
    cw = d // N_DEV

    def body(me_ref, *refs):
        p_refs, refs = refs[:n_p], refs[n_p + 1:]
        ins, t_ref, outs = refs[:3 * n], refs[3 * n], refs[3 * n + 1:]
        me = me_ref[0]
        row = 0
        for p_ref in p_refs:
            total = p_ref[0]
            for j in range(1, N_DEV):
                total = total + p_ref[j]
            t_ref[row:row + p_ref.shape[1], :] = total
            row += p_ref.shape[1]
        for e, (kind, r0, nr, _, _, _) in enumerate(layout):
            w_ref, m_ref, v_ref = ins[3 * e:3 * e + 3]
            o = outs[4 * e:4 * e + 4]
            if kind == "rep":
                g = t_ref[r0:r0 + nr, :]
                delta, m2, v2 = _adamw(w_ref[...], g, m_ref[...], v_ref[...])
                for ref, val in zip(o, (g, delta, m2, v2)):
                    ref[...] = val
            elif kind == "wide":
                for q in range(nr):
                    sl = slice(q * d, (q + 1) * d)
                    g = t_ref[r0 + q:r0 + q + 1, :]
                    delta, m2, v2 = _adamw(w_ref[:, sl], g, m_ref[:, sl], v_ref[:, sl])
                    for ref, val in zip(o, (g, delta, m2, v2)):
                        ref[:, sl] = val
            else:
                for j in range(N_DEV):
                    @pl.when(me == j)
                    def _(j=j, o=o, w_ref=w_ref, m_ref=m_ref, v_ref=v_ref, r0=r0, nr=nr):
                        g = t_ref[r0:r0 + nr, j * cw:(j + 1) * cw]
                        delta, m2, v2 = _adamw(w_ref[...], g, m_ref[...], v_ref[...])
                        for ref, val in zip(o, (g, delta, m2, v2)):
                            ref[...] = val

    flat = [a for ent in layout for a in ent[3:]]
    vm = pl.BlockSpec(memory_space=pltpu.VMEM)
    outs = pl.pallas_call(
        body, name="small_adamw",
        out_shape=[jax.ShapeDtypeStruct((rows, d), F32)]
                  + [jax.ShapeDtypeStruct(ent[3].shape, F32) for ent in layout for _ in range(4)],
        in_specs=[pl.BlockSpec(memory_space=pltpu.SMEM)] + [vm] * n_p + [_any()] + [vm] * len(flat),
        out_specs=[vm] * (1 + 4 * n),
        compiler_params=pltpu.CompilerParams(vmem_limit_bytes=V7X_VMEM_LIMIT),
    )(me_index, *partials, after, *flat)
    return outs[0], [tuple(outs[1 + 4 * e:5 + 4 * e]) for e in range(n)]


def _ffn_fwd(h, g, wgu, wd, tm, loss=None, comm=None):
    tp, d = h.shape
    f = wd.shape[0]
    fc = f // FFN_FWD_CHUNKS
    nt = tp // tm
    with_loss = loss is not None
    if with_loss:
        tgt, gf, n_meta, t_real = loss

    def body(*refs):
        if with_loss:
            (h_ref, g_ref, wgu_hbm, wd_hbm, tgt_ref, gf_ref, out_ref, gu_ref, n_ref, tail_ref,
             wgu_v, wd_v, sems) = refs
        else:
            h_ref, g_ref, wgu_hbm, wd_hbm, out_ref, gu_ref, n_ref, wgu_v, wd_v, sems = refs
        i = pl.program_id(0)

        @pl.when(i == 0)
        def _():
            _load_weights([(wgu_hbm, wgu_v), (wd_hbm, wd_v)], sems)
            if with_loss:
                tail_ref[...] = jnp.zeros_like(tail_ref)

        x = h_ref[...]
        n, _ = _rms_fwd(x, g_ref[...])
        nb = n.astype(BF16)
        n_ref[...] = nb
        acc = jnp.zeros((tm, d), F32)
        for j in range(FFN_FWD_CHUNKS):
            cols = slice(j * fc, (j + 1) * fc)
            gate = _nt(nb, wgu_v[pl.ds(j * fc, fc), :])
            up = _nt(nb, wgu_v[pl.ds(f + j * fc, fc), :])
            gu_ref[0, :, cols] = gate.astype(BF16)
            gu_ref[1, :, cols] = up.astype(BF16)
            act = (gate * _sigmoid(gate) * up).astype(BF16)
            acc = acc + _nn(act, wd_v[pl.ds(j * fc, fc), :])
        hn = x + FFN_RES * acc
        if not with_loss:
            out_ref[...] = hn
        else:
            gfv = gf_ref[...]
            r = lax.rsqrt(jnp.mean(hn * hn, axis=-1, keepdims=True) + EPS)
            xr = hn * r
            rows = i * tm + lax.broadcasted_iota(jnp.int32, (tm, 1), 0)
            mask = jnp.logical_and(rows >= n_meta, rows < t_real)
            diff = jnp.where(mask, xr * gfv - tgt_ref[...], 0.0)
            tail_ref[TAIL_LOSS:TAIL_LOSS + 1, :] += jnp.zeros((1, d), F32) + 0.5 * jnp.sum(diff * diff) / d
            dy = diff / d
            gy = dy * gfv
            out_ref[...] = r * (gy - xr * jnp.mean(gy * xr, axis=-1, keepdims=True))
            tail_ref[TAIL_FINAL:TAIL_FINAL + 1, :] += _rowsum(dy * xr)

    row = pl.BlockSpec((tm, d), lambda i: (i, 0))
    vec = pl.BlockSpec((1, d), lambda i: (0, 0))
    in_specs = [row, vec, _any(), _any()]
    out_shape = [jax.ShapeDtypeStruct((tp, d), F32), jax.ShapeDtypeStruct((2, tp, f), BF16),
                 jax.ShapeDtypeStruct((tp, d), BF16)]
    out_specs = [row, pl.BlockSpec((2, tm, f), lambda i: (0, i, 0)), row]
    args = [h, g, wgu, wd]
    if with_loss:
        in_specs += [row, vec]
        out_shape += [jax.ShapeDtypeStruct((SUBLANES, d), F32)]
        out_specs += [pl.BlockSpec((SUBLANES, d), lambda i: (0, 0))]
        args += [tgt, gf]
    return _call(body, "ffn_fwd_loss" if with_loss else "ffn_fwd", (nt,), in_specs, out_specs, out_shape,
                 [pltpu.VMEM((2 * f, d), BF16), pltpu.VMEM((f, d), BF16), pltpu.SemaphoreType.DMA((2,))],
                 args, comm)


def _ffn_bwd(dh, h, gu, g, wgu, wd, tm, tail, tail_row, after):
    tp, d = h.shape
    f = wd.shape[0]
    fc = f // FFN_CHUNKS
    nt = tp // tm

    def body(dh_ref, h_ref, gu_ref, g_ref, tail_ref, wgu_hbm, wd_hbm, after_ref,
             dhin_ref, dgu_ref, act_ref, df_ref, dg_ref, wgu_v, wd_v, dn_v, sems):
        del after_ref
        i, j = pl.program_id(0), pl.program_id(1)

        @pl.when(jnp.logical_and(i == 0, j == 0))
        def _():
            _load_weights([(wgu_hbm, wgu_v), (wd_hbm, wd_v)], sems)
            dg_ref[...] = tail_ref[...]

        dfb = (FFN_RES * dh_ref[...]).astype(BF16)

        @pl.when(j == 0)
        def _():
            df_ref[...] = dfb
            dn_v[...] = jnp.zeros_like(dn_v)

        lo = pl.multiple_of(j * fc, 16)
        dact = _nt(dfb, wd_v[pl.ds(lo, fc), :])
        gate = gu_ref[0].astype(F32)
        up = gu_ref[1].astype(F32)
        sg = _sigmoid(gate)
        silu = gate * sg
        act_ref[...] = (silu * up).astype(BF16)
        dgate = (dact * up * (sg * (1.0 + gate * (1.0 - sg)))).astype(BF16)
        dup = (dact * silu).astype(BF16)
        dgu_ref[0] = dgate
        dgu_ref[1] = dup
        dn_v[...] += _nn(dgate, wgu_v[pl.ds(lo, fc), :]) + _nn(dup, wgu_v[pl.ds(pl.multiple_of(f + j * fc, 16), fc), :])

        @pl.when(j == FFN_CHUNKS - 1)
        def _():
            x = h_ref[...]
            r = lax.rsqrt(jnp.mean(x * x, axis=-1, keepdims=True) + EPS)
            dx, dgp = _rms_bwd(dn_v[...], x, r, g_ref[...])
            dhin_ref[...] = dh_ref[...] + dx
            dg_ref[tail_row:tail_row + 1, :] += dgp

    row = pl.BlockSpec((tm, d), lambda i, j: (i, 0))
    vec = pl.BlockSpec((1, d), lambda i, j: (0, 0))
    tile = pl.BlockSpec((SUBLANES, d), lambda i, j: (0, 0))
    hid2 = pl.BlockSpec((2, tm, fc), lambda i, j: (0, i, j))
    return _call(
        body, "ffn_bwd", (nt, FFN_CHUNKS),
        [row, row, hid2, vec, tile, _any(), _any(), _any()],
        [row, hid2, pl.BlockSpec((tm, fc), lambda i, j: (i, j)), row, tile],
        [jax.ShapeDtypeStruct((tp, d), F32), jax.ShapeDtypeStruct((2, tp, f), BF16),
         jax.ShapeDtypeStruct((tp, f), BF16), jax.ShapeDtypeStruct((tp, d), BF16),
         jax.ShapeDtypeStruct((SUBLANES, d), F32)],
        [pltpu.VMEM((2 * f, d), BF16), pltpu.VMEM((f, d), BF16), pltpu.VMEM((tm, d), F32),
         pltpu.SemaphoreType.DMA((2,))],
        [dh, h, gu, g, tail, wgu, wd, after])


def _piece_segments(q, d, nb_cols):
    segs = []
    for j in range(N_DEV):
        lo, hi = max(q * d, j * nb_cols), min((q + 1) * d, (j + 1) * nb_cols)
        if lo < hi:
            segs.append((j, lo - q * d, hi - q * d, lo - j * nb_cols, hi - j * nb_cols))
    return segs


def _w3_copies(w3_hbm, rows, w3_v):
    return [(w3_hbm.at[k, pl.ds(q * rows, rows)], w3_v.at[q, pl.ds(k * rows, rows)])
            for q in range(3) for k in range(N_DEV)]


def _gates(xrb, wg_ref, ba, bx, lam, hd):
    pre_r, pre_i = [], []
    for hh in range(N_HEADS):
        xh = xrb[:, hh * hd:(hh + 1) * hd]
        pre_r.append(_nn(xh, wg_ref[0, hh]))
        pre_i.append(_nn(xh, wg_ref[1, hh]))
    r = _sigmoid(jnp.concatenate(pre_r, axis=1) + ba)
    ig = _sigmoid(jnp.concatenate(pre_i, axis=1) + bx)
    sp = _softplus(-lam)
    log_a = -RG_LRU_C * r * sp
    a = jnp.exp(log_a)
    s = jnp.sqrt(_one_minus_exp(2.0 * log_a))
    return r, ig, sp, a, s


def _scan_fwd(a, u, h_prev):
    tm = a.shape[0]
    rows = lax.broadcasted_iota(jnp.int32, a.shape, 0)
    d = 1
    while d < tm:
        if d < SUBLANES:
            keep = rows >= d
            u = jnp.where(keep, a * pltpu.roll(u, d, 0) + u, u)
            a = jnp.where(keep, a * pltpu.roll(a, d, 0), a)
        else:
            u = jnp.concatenate([u[:d], a[d:] * u[:tm - d] + u[d:]], axis=0)
            a = jnp.concatenate([a[:d], a[d:] * a[:tm - d]], axis=0)
        d *= 2
    return u + a * h_prev


def _scan_bwd(b, v, g_next):
    tm = b.shape[0]
    rows = lax.broadcasted_iota(jnp.int32, b.shape, 0)
    d = 1
    while d < tm:
        if d < SUBLANES:
            keep = rows < tm - d
            v = jnp.where(keep, v + b * pltpu.roll(v, tm - d, 0), v)
            b = jnp.where(keep, b * pltpu.roll(b, tm - d, 0), b)
        else:
            v = jnp.concatenate([v[:tm - d] + b[:tm - d] * v[d:], v[tm - d:]], axis=0)
            b = jnp.concatenate([b[:tm - d] * b[d:], b[tm - d:]], axis=0)
        d *= 2
    return v + b * g_next


def _shifted_copies(ext_ref, es_ref, n_rows):
    for s in range(1, SUBLANES):
        es_ref[s, pl.ds(0, n_rows), :] = ext_ref[pl.ds(s, n_rows), :]


def _tap(ext_ref, es_ref, off, tm):
    q, s = divmod(off, SUBLANES)
    if s == 0:
        return ext_ref[pl.ds(SUBLANES * q, tm), :]
    return es_ref[s, pl.ds(SUBLANES * q, tm), :]


def _mixer_fwd(h, g, b_in, win_all, cw4, cb4, wg, ba, bx, lam, cw31, cb31, lng, lnb, bcp, w3_all, tm, comm=None):
    tp, d = h.shape
    nb_cols = win_all.shape[-1]
    n_in = N_DEV * nb_cols
    hd = wg.shape[-1]
    k4, k31 = cw4.shape[0], cw31.shape[0]
    w3_rows = d // N_DEV

    def body(h_ref, g_ref, b_ref, win_hbm, cw4_ref, cb4_ref, wg_ref, ba_ref, bx_ref, lam_ref, cw31_ref, cb31_ref,
             lng_ref, lnb_ref, bcp_ref, w3_hbm,
             h2_ref, p_ref, n_ref, xr_ref, hs_ref, v1_ref, ya_ref, yb_ref,
             win_v, w3_v, ext4, ext31, es31, hcar, sems):
        @pl.when(pl.program_id(0) == 0)
        def _():
            _load_weights([(win_hbm, win_v)] + _w3_copies(w3_hbm, w3_rows, w3_v), sems)
            ext4[pl.ds(0, CONV4_HALO), :] = jnp.zeros((CONV4_HALO, d), F32)
            ext31[pl.ds(0, CONV31_HALO), :] = jnp.zeros((CONV31_HALO, d), F32)
            hcar[...] = jnp.zeros_like(hcar)

        n, _ = _rms_fwd(h_ref[...], g_ref[...])
        nb = n.astype(BF16)
        n_ref[...] = nb

        def piece(q):
            parts = [_nn(nb, win_v[j, :, bl:bh]) for j, _, _, bl, bh in _piece_segments(q, d, nb_cols)]
            pq = (jnp.concatenate(parts, axis=1) + b_ref[:, q * d:(q + 1) * d]).astype(BF16)
            p_ref[:, q * d:(q + 1) * d] = pq
            return pq.astype(F32)

        x_rnn, y_rnn, glu_v, glu_g, gate_a, gate_b = [piece(q) for q in range(6)]

        ext4[pl.ds(CONV4_HALO, tm), :] = x_rnn
        xr = cb4_ref[...] + jnp.zeros((tm, d), F32)
        for k in range(k4):
            xr = xr + cw4_ref[k:k + 1, :] * ext4[pl.ds(CONV4_HALO - (k4 - 1) + k, tm), :]
        ext4[pl.ds(0, CONV4_HALO), :] = ext4[pl.ds(tm, CONV4_HALO), :]
        xrb = xr.astype(BF16)
        xr_ref[...] = xrb
        xr = xrb.astype(F32)
        _, ig, _, a, s = _gates(xrb, wg_ref, ba_ref[...], bx_ref[...], lam_ref[...], hd)
        hseq = _scan_fwd(a, s * (ig * xr), hcar[0:1, :])
        hcar[0:1, :] = hseq[tm - 1:tm, :]
        hs_ref[...] = hseq.astype(BF16)
        gl, _ = _gelu(y_rnn)
        ya = _nn((hseq * gl).astype(BF16), w3_v[0])
        ya_ref[...] = ya.astype(BF16)

        ext31[pl.ds(CONV31_HALO, tm), :] = glu_v * _sigmoid(glu_g)
        _shifted_copies(ext31, es31, tm + CONV31_HALO - SUBLANES)
        v1 = cb31_ref[...] + jnp.zeros((tm, d), F32)
        for k in range(k31):
            v1 = v1 + cw31_ref[k:k + 1, :] * _tap(ext31, es31, CONV31_HALO - (k31 - 1) + k, tm)
        ext31[pl.ds(0, CONV31_HALO), :] = ext31[pl.ds(tm, CONV31_HALO), :]
        v1b = v1.astype(BF16)
        v1_ref[...] = v1b
        v1 = v1b.astype(F32)
        xc = v1 - jnp.mean(v1, axis=-1, keepdims=True)
        rstd = lax.rsqrt(jnp.mean(xc * xc, axis=-1, keepdims=True) + EPS)
        v2 = xc * rstd * lng_ref[...] + lnb_ref[...]
        yb = _nn((v2 * _sigmoid(v2)).astype(BF16), w3_v[1]) + bcp_ref[...]
        yb_ref[...] = yb.astype(BF16)

        merged = _sigmoid(gate_a) * ya + _sigmoid(gate_b) * yb
        h2_ref[...] = h_ref[...] + _nn(merged.astype(BF16), w3_v[2])

    row = pl.BlockSpec((tm, d), lambda i: (i, 0))
    wide = pl.BlockSpec((tm, n_in), lambda i: (i, 0))
    full = lambda a: pl.BlockSpec(a.shape, lambda i, nd=a.ndim: (0,) * nd)
    smalls = [cw4, cb4, wg, ba, bx, lam, cw31, cb31, lng, lnb, bcp]
    return _call(
        body, "mixer_fwd", (tp // tm,),
        [row, full(g), full(b_in), _any()] + [full(a) for a in smalls] + [_any()],
        [row, wide] + [row] * 6,
        [jax.ShapeDtypeStruct((tp, d), F32), jax.ShapeDtypeStruct((tp, n_in), BF16)]
        + [jax.ShapeDtypeStruct((tp, d), BF16)] * 6,
        [pltpu.VMEM(win_all.shape, BF16),
         pltpu.VMEM((3, d, d), BF16),
         pltpu.VMEM((tm + CONV4_HALO, d), F32),
         pltpu.VMEM((tm + CONV31_HALO, d), F32),
         pltpu.VMEM((SUBLANES, tm + CONV31_HALO, d), F32),
         pltpu.VMEM((SUBLANES, d), F32),
         pltpu.SemaphoreType.DMA((1 + 3 * N_DEV,))],
        [h, g, b_in, win_all, *smalls, w3_all], comm)


SG_BIN, SG_CW4, SG_CB4, SG_BA, SG_BX, SG_LAM, SG_CB31, SG_LNG, SG_LNB, SG_BCP, SG_MIX, SG_CW31 = 0, 6, 10, 11, 12, 13, 14, 15, 16, 17, 18, 19


def _mixer_bwd(dh2, h, g, proj, xr_s, hs_s, v1_s, ya_s, yb_s, win_t, cw4, wg, ba, bx, lam, cw31, lng, lnb, w3_all, tm,
               comm=None):
    tp, d = dh2.shape
    n_in = proj.shape[1]
    hd = wg.shape[-1]
    k4, k31 = cw4.shape[0], cw31.shape[0]
    nt = tp // tm
    w3_rows = d // N_DEV
    sg_rows = -(-(SG_CW31 + k31) // SUBLANES) * SUBLANES
    halo_rows = 16
    per = tm // halo_rows

    def body(dh_ref, h_ref, g_ref, p_ref, xr_ref, hs_ref, hh_ref, v1_ref, ya_ref, yb_ref, win_hbm,
             cw4_ref, wg_ref, wgt_ref, ba_ref, bx_ref, lam_ref, cw31_ref, lng_ref, lnb_ref, w3_hbm,
             dh1_ref, dp_ref, x3_ref, y3_ref, yg_ref, sg_ref,
             win_v, w3_v, extd4, extd31, es31, gcar, sems):
        i = pl.program_id(0)
        tile = nt - 1 - i

        @pl.when(i == 0)
        def _():
            _load_weights([(win_hbm, win_v)] + _w3_copies(w3_hbm, w3_rows, w3_v), sems)
            for q in range(3):
                w3_v[q] = w3_v[q].T
            extd4[pl.ds(tm, CONV4_HALO), :] = jnp.zeros((CONV4_HALO, d), F32)
            extd31[pl.ds(tm, CONV31_HALO), :] = jnp.zeros((CONV31_HALO, d), F32)
            gcar[...] = jnp.zeros_like(gcar)
            sg_ref[...] = jnp.zeros_like(sg_ref)

        def acc(row, val):
            sg_ref[row:row + 1, :] += _rowsum(val)

        rows = lax.broadcasted_iota(jnp.int32, (tm, d), 0)
        x_rnn = p_ref[:, 0:d].astype(F32)
        y_rnn = p_ref[:, d:2 * d].astype(F32)
        glu_v = p_ref[:, 2 * d:3 * d].astype(F32)
        glu_g = p_ref[:, 3 * d:4 * d].astype(F32)
        sga = _sigmoid(p_ref[:, 4 * d:5 * d].astype(F32))
        sgb = _sigmoid(p_ref[:, 5 * d:6 * d].astype(F32))
        ya = ya_ref[...].astype(F32)
        yb = yb_ref[...].astype(F32)

        dmob = dh_ref[...].astype(BF16)
        dmerged = _nn(dmob, w3_v[2])
        x3_ref[:, 0:d] = (sga * ya + sgb * yb).astype(BF16)
        y3_ref[:, 0:d] = dmob
        dya = sga * dmerged
        dyb = sgb * dmerged
        dn_parts = []

        def emit(q, val):
            vb = val.astype(BF16)
            dp_ref[:, q * d:(q + 1) * d] = vb
            acc(SG_BIN + q, val)
            term = _nn(vb, win_v[pl.ds(q * d, d), :])
            dn_parts[:] = [term if not dn_parts else dn_parts[0] + term]

        emit(4, dmerged * ya * sga * (1.0 - sga))
        emit(5, dmerged * yb * sgb * (1.0 - sgb))

        dyab = dya.astype(BF16)
        y3_ref[:, d:2 * d] = dyab
        dza = _nn(dyab, w3_v[0])
        hsv = hs_ref[...].astype(F32)
        gl, th = _gelu(y_rnn)
        x3_ref[:, d:2 * d] = (hsv * gl).astype(BF16)
        emit(1, dza * hsv * _gelu_grad(y_rnn, th))
        dhs = dza * gl
        xrb = xr_ref[...]
        xr = xrb.astype(F32)
        lam_v = lam_ref[...]
        r, ig, sp, a, s = _gates(xrb, wg_ref, ba_ref[...], bx_ref[...], lam_v, hd)
        b = jnp.where(rows == tm - 1, gcar[1:2, :], pltpu.roll(a, tm - 1, 0))
        big_g = _scan_bwd(b, dhs, gcar[0:1, :])
        gcar[0:1, :] = big_g[0:1, :]
        gcar[1:2, :] = a[0:1, :]
        h_before = jnp.where(tile > 0, hh_ref[halo_rows - 1:halo_rows, :].astype(F32), 0.0)
        h_prev = jnp.where(rows == 0, h_before, pltpu.roll(hsv, 1, 0))
        ds = big_g * ig * xr
        dla = big_g * h_prev * a - ds * (a * a) / jnp.maximum(s, 1e-20)
        acc(SG_LAM, dla * r * (RG_LRU_C * _sigmoid(-lam_v)))
        dpr = dla * (-RG_LRU_C * sp) * r * (1.0 - r)
        dpi = big_g * s * xr * ig * (1.0 - ig)
        acc(SG_BA, dpr)
        acc(SG_BX, dpi)
        dprb = dpr.astype(BF16)
        dpib = dpi.astype(BF16)
        yg_ref[:, 0:d] = dprb
        yg_ref[:, d:2 * d] = dpib
        back = []
        for hh in range(N_HEADS):
            sl = slice(hh * hd, (hh + 1) * hd)
            back.append(_nn(dprb[:, sl], wgt_ref[0, hh]) + _nn(dpib[:, sl], wgt_ref[1, hh]))
        dxr = big_g * s * ig + jnp.concatenate(back, axis=1)
        acc(SG_CB4, dxr)
        extd4[pl.ds(0, tm), :] = dxr
        dx_rnn = jnp.zeros((tm, d), F32)
        for k in range(k4):
            term = extd4[pl.ds(k4 - 1 - k, tm), :]
            dx_rnn = dx_rnn + cw4_ref[k:k + 1, :] * term
            acc(SG_CW4 + k, x_rnn * term)
        extd4[pl.ds(tm, CONV4_HALO), :] = extd4[pl.ds(0, CONV4_HALO), :]
        emit(0, dx_rnn)

        dybb = dyb.astype(BF16)
        y3_ref[:, 2 * d:3 * d] = dybb
        acc(SG_BCP, dyb)
        dv3 = _nn(dybb, w3_v[1])
        v1 = v1_ref[...].astype(F32)
        xc = v1 - jnp.mean(v1, axis=-1, keepdims=True)
        rstd = lax.rsqrt(jnp.mean(xc * xc, axis=-1, keepdims=True) + EPS)
        xhat = xc * rstd
        lng_v = lng_ref[...]
        v2 = xhat * lng_v + lnb_ref[...]
        s2 = _sigmoid(v2)
        x3_ref[:, 2 * d:3 * d] = (v2 * s2).astype(BF16)
        dv2 = dv3 * (s2 * (1.0 + v2 * (1.0 - s2)))
        acc(SG_LNG, dv2 * xhat)
        acc(SG_LNB, dv2)
        dxh = dv2 * lng_v
        dv1 = rstd * (dxh - jnp.mean(dxh, axis=-1, keepdims=True)
                      - xhat * jnp.mean(dxh * xhat, axis=-1, keepdims=True))
        acc(SG_CB31, dv1)
        extd31[pl.ds(0, tm), :] = dv1
        _shifted_copies(extd31, es31, tm + CONV31_HALO - SUBLANES)
        sgg = _sigmoid(glu_g)
        v0 = glu_v * sgg
        dv0 = jnp.zeros((tm, d), F32)
        for k in range(k31):
            term = _tap(extd31, es31, k31 - 1 - k, tm)
            dv0 = dv0 + cw31_ref[k:k + 1, :] * term
            acc(SG_CW31 + k, v0 * term)
        extd31[pl.ds(tm, CONV31_HALO), :] = extd31[pl.ds(0, CONV31_HALO), :]
        emit(2, dv0 * sgg)
        emit(3, dv0 * glu_v * sgg * (1.0 - sgg))

        dn = dn_parts[0]
        x = h_ref[...]
        rr = lax.rsqrt(jnp.mean(x * x, axis=-1, keepdims=True) + EPS)
        dx, dgp = _rms_bwd(dn, x, rr, g_ref[...])
        dh1_ref[...] = dh_ref[...] + dx
        sg_ref[SG_MIX:SG_MIX + 1, :] += dgp

    rev = lambda i: (nt - 1 - i, 0)
    row = pl.BlockSpec((tm, d), rev)
    wide = pl.BlockSpec((tm, n_in), rev)
    full = lambda a: pl.BlockSpec(a.shape, lambda i, nd=a.ndim: (0,) * nd)
    halo = pl.BlockSpec((halo_rows, d), lambda i: (jnp.maximum((nt - 1 - i) * per - 1, 0), 0))
    smalls = [cw4, wg, jnp.swapaxes(wg, 2, 3), ba, bx, lam, cw31, lng, lnb]
    return _call(
        body, "mixer_bwd", (nt,),
        [row, row, full(g), wide, row, row, halo, row, row, row, _any()]
        + [full(a) for a in smalls] + [_any()],
        [row, wide, pl.BlockSpec((tm, 3 * d), rev), pl.BlockSpec((tm, 3 * d), rev),
         pl.BlockSpec((tm, 2 * d), rev), pl.BlockSpec((sg_rows, d), lambda i: (0, 0))],
        [jax.ShapeDtypeStruct((tp, d), F32), jax.ShapeDtypeStruct((tp, n_in), BF16),
         jax.ShapeDtypeStruct((tp, 3 * d), BF16), jax.ShapeDtypeStruct((tp, 3 * d), BF16),
         jax.ShapeDtypeStruct((tp, 2 * d), BF16), jax.ShapeDtypeStruct((sg_rows, d), F32)],
        [pltpu.VMEM(win_t.shape, BF16),
         pltpu.VMEM((3, d, d), BF16),
         pltpu.VMEM((tm + CONV4_HALO, d), F32),
         pltpu.VMEM((tm + CONV31_HALO, d), F32),
         pltpu.VMEM((SUBLANES, tm + CONV31_HALO, d), F32),
         pltpu.VMEM((SUBLANES, d), F32),
         pltpu.SemaphoreType.DMA((1 + 3 * N_DEV,))],
        [dh2, h, g, proj, xr_s, hs_s, hs_s, v1_s, ya_s, yb_s, win_t, *smalls, w3_all], comm)


def _tn_matmul(name, x, y, x_spec, y_spec, n_blocks, kb, nb, tm, tp, out_shape, out_spec, out_view, comm=None,
               after=None):
    nt = tp // tm

    def body(x_ref, y_ref, *refs):
        o_ref, acc = refs[-2:]
        i = pl.program_id(1)

        @pl.when(i == 0)
        def _():
            acc[...] = jnp.zeros_like(acc)

        acc[...] += _tn(x_ref[...], y_ref[...])

        @pl.when(i == nt - 1)
        def _():
            o_ref[...] = acc[...].astype(BF16).reshape(out_view)

    follows = [] if after is None else [after]
    outs, extra = _call(body, name, (n_blocks, nt), [x_spec, y_spec] + [_any()] * len(follows), [out_spec],
                        [jax.ShapeDtypeStruct(out_shape, BF16)], [pltpu.VMEM((kb, nb), F32)], [x, y] + follows,
                        comm)
    return outs[0], extra


def kernel(x, meta_tokens, ffn1_norm, ffn1_w_gu, ffn1_w_down, mix_norm, w_in, b_in, rnn_conv_w, rnn_conv_b, rg_w_a, rg_b_a, rg_w_x, rg_b_x, rg_lambda, rnn_w_proj, conv_dw_w, conv_dw_b, conv_ln_g, conv_ln_b, conv_w_proj, conv_b_proj, w_out, ffn2_norm, ffn2_w_gu, ffn2_w_down, final_norm, loss_target, m_meta_tokens, m_ffn1_norm, m_ffn1_w_gu, m_ffn1_w_down, m_mix_norm, m_w_in, m_b_in, m_rnn_conv_w, m_rnn_conv_b, m_rg_w_a, m_rg_b_a, m_rg_w_x, m_rg_b_x, m_rg_lambda, m_rnn_w_proj, m_conv_dw_w, m_conv_dw_b, m_conv_ln_g, m_conv_ln_b, m_conv_w_proj, m_conv_b_proj, m_w_out, m_ffn2_norm, m_ffn2_w_gu, m_ffn2_w_down, m_final_norm, v_meta_tokens, v_ffn1_norm, v_ffn1_w_gu, v_ffn1_w_down, v_mix_norm, v_w_in, v_b_in, v_rnn_conv_w, v_rnn_conv_b, v_rg_w_a, v_rg_b_a, v_rg_w_x, v_rg_b_x, v_rg_lambda, v_rnn_w_proj, v_conv_dw_w, v_conv_dw_b, v_conv_ln_g, v_conv_ln_b, v_conv_w_proj, v_conv_b_proj, v_w_out, v_ffn2_norm, v_ffn2_w_gu, v_ffn2_w_down, v_final_norm):
    w = dict(locals())
    seq, d = x.shape[1], x.shape[2]
    n_meta = meta_tokens.shape[0]
    t_real = n_meta + seq
    tp, tm, tmx_fwd, tmx, tmt, tmw = _tiles(t_real)
    fb = ffn1_w_gu.shape[-1]
    wr = ffn1_w_down.shape[1]
    f = N_DEV * wr
    fc = f // FFN_CHUNKS
    nbc = w_in.shape[-1]
    n_in = N_DEV * nbc
    pr = rnn_w_proj.shape[1]
    hd = rg_w_a.shape[-1]
    gr = rg_w_a.shape[2]
    cw = meta_tokens.shape[1]
    k4, k31 = rnn_conv_w.shape[1], conv_dw_w.shape[1]
    assert n_in == 6 * d and 2 * wr == fb and N_HEADS * hd == d and pr * N_DEV == d

    xi, yi, ci = lax.axis_index("x"), lax.axis_index("y"), lax.axis_index("c")
    core = ci.astype(jnp.int32).reshape(1)
    chip = (2 * xi + yi).astype(jnp.int32).reshape(1)
    me_index = (4 * xi + 2 * yi + ci).astype(jnp.int32).reshape(1)

    for nm in ("ffn1_w_gu", "ffn2_w_gu"):
        for pre in ("", "m_", "v_"):
            w[pre + nm] = jnp.swapaxes(w[pre + nm], 1, 2)

    wgut1 = w["ffn1_w_gu"][0].astype(BF16)
    wgut2 = w["ffn2_w_gu"][0].astype(BF16)
    wd1 = ffn1_w_down[0].astype(BF16)
    wd2 = ffn2_w_down[0].astype(BF16)
    win_loc = w_in[0].astype(BF16)
    win_t_loc = jnp.swapaxes(w_in[0], 0, 1).astype(BF16)
    w3_loc = jnp.concatenate([rnn_w_proj[0], conv_w_proj[0], w_out[0]], axis=0).astype(BF16)
    wg_loc = jnp.stack([rg_w_a[0], rg_w_x[0]]).astype(BF16)
    n_small = n_meta + k4 + k31
    small_rows = -(-n_small // SUBLANES) * SUBLANES
    small_loc = jnp.concatenate([meta_tokens, rnn_conv_w[0], conv_dw_w[0],
                                 jnp.zeros((small_rows - n_small, cw), F32)], axis=0)
    (wgut1_all, wd1_all, small_all), h0, tgt = _first_gather(
        [wgut1, wd1, small_loc], 2, x[0], loss_target[0], n_meta, tp)
    small_full = small_all.transpose(1, 0, 2).reshape(small_rows, d)
    cw4 = small_full[n_meta:n_meta + k4]
    cw31 = small_full[n_meta + k4:n_meta + k4 + k31]

    wgu1, wdn1 = wgut1_all.reshape(2 * f, d), wd1_all.reshape(f, d)
    (h1, gu1, n1), (win_all, w3_all, wg_all) = _ffn_fwd(
        h0, ffn1_norm, wgu1, wdn1, tm, comm=_Gather([win_loc, w3_loc, wg_loc], pass_on_at=(0.65, 0.95)))
    wg = wg_all.transpose(1, 2, 0, 3, 4).reshape(2, N_HEADS, hd, hd)
    (h2, proj, n2, xr_s, hs_s, v1_s, ya_s, yb_s), (wgut2_all, wd2_all, win_t_all) = _mixer_fwd(
        h1, mix_norm, b_in, win_all, cw4, rnn_conv_b, wg, rg_b_a, rg_b_x, rg_lambda, cw31, conv_dw_b, conv_ln_g,
        conv_ln_b, conv_b_proj, w3_all, tmx_fwd, comm=_Gather([wgut2, wd2, win_t_loc], pass_on_at=(0.45, 0.7)))
    wgu2, wdn2 = wgut2_all.reshape(2 * f, d), wd2_all.reshape(f, d)
    win_t = win_t_all.reshape(n_in, d)
    (dh3, gu2, n3, tail), _ = _ffn_fwd(
        h2, ffn2_norm, wgu2, wdn2, tm, loss=(tgt, final_norm.reshape(1, d), n_meta, t_real))

    def d_w_gu(tag, dgu, n_s, after=None):
        g, _ = _tn_matmul(
            "d_w_gu" + tag, dgu, n_s,
            pl.BlockSpec((None, tmt, fc), lambda b, i: (b // FFN_CHUNKS, i, b % FFN_CHUNKS)),
            pl.BlockSpec((tmt, d), lambda b, i: (i, 0)),
            2 * FFN_CHUNKS, fc, d, tmt, tp, (2 * FFN_CHUNKS, fc, d),
            pl.BlockSpec((None, fc, d), lambda b, i: (b, 0, 0)), (fc, d), after=after)
        return g.reshape(N_DEV, fb, d)

    def d_w_down(tag, act, df):
        g, _ = _tn_matmul(
            "d_w_down" + tag, act, df,
            pl.BlockSpec((tmt, fc), lambda b, i: (i, b)), pl.BlockSpec((tmt, d), lambda b, i: (i, 0)),
            FFN_CHUNKS, fc, d, tmt, tp, (FFN_CHUNKS, fc, d),
            pl.BlockSpec((None, fc, d), lambda b, i: (b, 0, 0)), (fc, d))
        return g.reshape(N_DEV, wr, d)

    (dh2, dgu2, act2, df2, tail), _ = _ffn_bwd(dh3, h2, gu2, ffn2_norm, wgu2, wdn2, tm, tail, TAIL_FFN2, n3)
    g_wgu2 = d_w_gu("2", dgu2, n3)
    g_wd2 = d_w_down("2", act2, df2)
    (dh1, dproj, x3, y3, yg, sg), (r_wd2, r_wgu2) = _mixer_bwd(
        dh2, h1, mix_norm, proj, xr_s, hs_s, v1_s, ya_s, yb_s, win_t, cw4, wg, rg_b_a, rg_b_x, rg_lambda, cw31,
        conv_ln_g, conv_ln_b, w3_all, tmx, comm=_Scatter([g_wd2, g_wgu2]))
    g_w3, _ = _tn_matmul(
        "d_w_proj3", x3, y3,
        pl.BlockSpec((tmw, d), lambda b, i: (i, b)), pl.BlockSpec((tmw, d), lambda b, i: (i, b)),
        3, d, d, tmw, tp, (N_DEV, 3, pr, d), pl.BlockSpec((N_DEV, None, pr, d), lambda b, i: (0, b, 0, 0)),
        (N_DEV, pr, d))
    g_wg, _ = _tn_matmul(
        "d_w_gates", xr_s, yg,
        pl.BlockSpec((tmw, hd), lambda b, i: (i, b % N_HEADS)), pl.BlockSpec((tmw, hd), lambda b, i: (i, b)),
        2 * N_HEADS, hd, hd, tmw, tp, (N_DEV, 2 * N_HEADS, gr, hd),
        pl.BlockSpec((N_DEV, None, gr, hd), lambda b, i: (0, b, 0, 0)), (N_DEV, gr, hd))
    ((sg_sems, sg_thru, sg_land), (wg_sems, g_wg_thru, wg_land), (w3_sems, g_w3_thru, w3_land)), w3_token = _exchanges_start(
        "grads_proj3_exchange",
        [(_bcast_copies, 2 * (N_DEV - 1) + 1, sg, (N_DEV,) + sg.shape),
         (_scatter_copies, 2 * (N_DEV - 1), g_wg, (N_DEV - 1,) + g_wg.shape[1:]),
         (_scatter_copies, 2 * (N_DEV - 1), g_w3, (N_DEV - 1,) + g_w3.shape[1:])])
    g_win, _ = _tn_matmul(
        "d_w_in", n2, dproj,
        pl.BlockSpec((tmw, d), lambda b, i: (i, 0)), pl.BlockSpec((tmw, nbc), lambda b, i: (i, b)),
        N_DEV, d, nbc, tmw, tp, (N_DEV, d, nbc), pl.BlockSpec((None, d, nbc), lambda b, i: (b, 0, 0)), (d, nbc),
        after=w3_token)
    win_sems, g_win_thru, win_land, win_token = _exchange_start("grads_w_in_exchange", _scatter_copies, N_DEV - 1, g_win)
    (dh0, dgu1, act1, df1, tail), _ = _ffn_bwd(dh1, h0, gu1, ffn1_norm, wgu1, wdn1, tm, tail, TAIL_FFN1, win_token)

    pieces = [sg, dh0[:n_meta], tail]
    assert all(p.shape[0] % SUBLANES == 0 for p in pieces)
    at = [0, sg.shape[0], sg.shape[0] + n_meta]
    loss_row = at[2] + TAIL_LOSS
    rep_rows = [("ffn1_norm", at[2] + TAIL_FFN1, 1), ("mix_norm", SG_MIX, 1), ("b_in", SG_BIN, 6),
                ("rnn_conv_b", SG_CB4, 1),
                ("rg_b_a", SG_BA, 1), ("rg_b_x", SG_BX, 1), ("rg_lambda", SG_LAM, 1), ("conv_dw_b", SG_CB31, 1),
                ("conv_ln_g", SG_LNG, 1), ("conv_ln_b", SG_LNB, 1), ("conv_b_proj", SG_BCP, 1),
                ("ffn2_norm", at[2] + TAIL_FFN2, 1), ("final_norm", at[2] + TAIL_FINAL, 1)]
    col_rows = [("meta_tokens", at[1], n_meta), ("rnn_conv_w", SG_CW4, k4), ("conv_dw_w", SG_CW31, k31)]
    layout = []
    for nm, row0, nr in rep_rows:
        kind = "wide" if nm == "b_in" else "rep"
        as2d = lambda a: a.reshape(1, -1) if a.ndim == 1 else a
        layout.append((kind, row0, nr, as2d(w[nm]), as2d(w["m_" + nm]), as2d(w["v_" + nm])))
    for nm, row0, nr in col_rows:
        sq = lambda a: a.reshape(a.shape[-2], a.shape[-1])
        layout.append(("col", row0, nr, sq(w[nm]), sq(w["m_" + nm]), sq(w["v_" + nm])))
    small_partial = jnp.concatenate(pieces[1:], axis=0)

    g_wd1 = d_w_down("1", act1, df1)
    ((small_sems, small_thru, small_land), (wd1_sems, g_wd1_thru, wd1_land)), wd1_token = _exchanges_start(
        "grads_w_down1_exchange",
        [(_bcast_copies, 2 * (N_DEV - 1) + 1, small_partial, (N_DEV,) + small_partial.shape),
         (_scatter_copies, 2 * (N_DEV - 1), g_wd1, (N_DEV - 1,) + g_wd1.shape[1:])])
    g_wgu1 = d_w_gu("1", dgu1, n1, after=wd1_token)

    g_last = g_wgu1.reshape((4, 2) + g_wgu1.shape[1:])
    comb_wgu1 = _pair_reduce(g_last, core)
    sems, comb_thru, land_thru, after = _exchange_start("grads_chip_exchange", _chip_copies, 3, comb_wgu1)
    g_win, r_win = _exchange_wait("grads_w_in_exchange", _scatter_copies, win_sems, g_win_thru, win_land, after)
    g_w3, r_w3 = _exchange_wait("grads_proj3_exchange", _scatter_copies, w3_sems, g_w3_thru, w3_land, after)
    g_wd1, r_wd1 = _exchange_wait("grads_w_down1_exchange", _scatter_copies, wd1_sems, g_wd1_thru, wd1_land, after)
    g_wg, r_wg = _exchange_wait("grads_gates_exchange", _scatter_copies, wg_sems, g_wg_thru, wg_land, after)
    _, sg_all = _exchange_wait("grads_mixer_small_exchange", _bcast_copies, sg_sems, sg_thru, sg_land, after)
    _, rest_all = _exchange_wait("grads_small_exchange", _bcast_copies, small_sems, small_thru, small_land, after)

    grad_x = (dh0[n_meta:t_real] + after[0, 0])[None]
    total, small_out = _small_adamw([sg_all, rest_all], layout, me_index, grad_x)
    after = total

    groups = [(g_wd1, r_wd1, me_index, ["ffn1_w_down"]),
              (g_wd2, r_wd2, me_index, ["ffn2_w_down"]), (g_wgu2, r_wgu2, me_index, ["ffn2_w_gu"]),
              (g_win, r_win, me_index, ["w_in"]), (g_w3, r_w3, me_index, ["w_out", "rnn_w_proj", "conv_w_proj"]),
              (g_wg, r_wg, me_index, ["rg_w_a", "rg_w_x"]), (None, None, chip, ["ffn1_w_gu"])]
    res = {}
    for own, recv, idx, group in groups:
        if own is None:
            own, recv = _exchange_wait("grads_chip_exchange", _chip_copies, sems, comb_thru, land_thru, after)
        outs = _final_adamw(own, recv, idx, [(w[nm], w["m_" + nm], w["v_" + nm]) for nm in group], after)
        after = outs[-1][0]
        for nm, o in zip(group, outs):
            res[nm] = o
    for nm in ("ffn1_w_gu", "ffn2_w_gu"):
        res[nm] = tuple(jnp.swapaxes(a, 1, 2) for a in res[nm])
    for (nm, _, _), o in zip(rep_rows + col_rows, small_out):
        res[nm] = tuple(a.reshape(w[nm].shape) for a in o)


    order = ["meta_tokens", "ffn1_norm", "ffn1_w_gu", "ffn1_w_down", "mix_norm", "w_in", "b_in", "rnn_conv_w",
             "rnn_conv_b", "rg_w_a", "rg_b_a", "rg_w_x", "rg_b_x", "rg_lambda", "rnn_w_proj", "conv_dw_w",
             "conv_dw_b", "conv_ln_g", "conv_ln_b", "conv_w_proj", "conv_b_proj", "w_out", "ffn2_norm",
             "ffn2_w_gu", "ffn2_w_down", "final_norm"]
    return (total[loss_row, 0], grad_x, *[res[nm][0] for nm in order], *[res[nm][1] for nm in order],
            *[res[nm][2] for nm in order], *[res[nm][3] for nm in order])
```

```python
import functools
import math

import jax
import jax.numpy as jnp
from jax import lax
from jax.experimental import pallas as pl
from jax.experimental.pallas import tpu as pltpu

F32 = jnp.float32
BF16 = jnp.bfloat16
MESH = pl.DeviceIdType.MESH
N_DEV = 8
N_HEADS = 4
RG_LRU_C = 8.0
EPS = 1e-6
FFN_RES = 0.5
ADAM_LR, ADAM_B1, ADAM_B2, ADAM_EPS, ADAM_WD, ADAM_STEP = 0.001, 0.9, 0.999, 1e-08, 0.01, 10
V7X_VMEM_LIMIT = 56 * 1024 * 1024
CONV4_HALO = 8
CONV31_HALO = 32
SUBLANES = 8
STAGE_ROWS = 512
TAIL_FFN1, TAIL_FINAL, TAIL_LOSS, TAIL_FFN2 = 0, 1, 2, 3
FFN_CHUNKS = 2
FFN_FWD_CHUNKS = 1
GELU_C = math.sqrt(2.0 / math.pi)
GELU_K = 0.044715


def _any():
    return pl.BlockSpec(memory_space=pl.ANY)


def _params(n_grid):
    return pltpu.CompilerParams(dimension_semantics=("arbitrary",) * n_grid, vmem_limit_bytes=V7X_VMEM_LIMIT)


def _nn(a, b):
    return jnp.dot(a, b, preferred_element_type=F32)


def _nt(a, b):
    return lax.dot_general(a, b, (((1,), (1,)), ((), ())), preferred_element_type=F32)


def _tn(a, b):
    return lax.dot_general(a, b, (((0,), (0,)), ((), ())), preferred_element_type=F32)


def _sigmoid(x):
    return 0.5 * jnp.tanh(0.5 * x) + 0.5


def _rowsum(x):
    return jnp.sum(x, axis=0, keepdims=True)


def _rms_fwd(x, g):
    r = lax.rsqrt(jnp.mean(x * x, axis=-1, keepdims=True) + EPS)
    return x * r * g, r


def _rms_bwd(dn, x, r, g):
    xr = x * r
    gy = dn * g
    dx = r * (gy - xr * jnp.mean(gy * xr, axis=-1, keepdims=True))
    return dx, _rowsum(dn * xr)


def _gelu(y):
    t = jnp.tanh(GELU_C * (y + GELU_K * y * y * y))
    return 0.5 * y * (1.0 + t), t


def _gelu_grad(y, t):
    return 0.5 * (1.0 + t) + 0.5 * y * (1.0 - t * t) * GELU_C * (1.0 + 3.0 * GELU_K * y * y)


def _softplus(x):
    return jnp.maximum(x, 0.0) + jnp.log(1.0 + jnp.exp(-jnp.abs(x)))


def _one_minus_exp(z):
    series = -z * (1.0 + 0.5 * z * (1.0 + z * (1.0 / 3.0) * (1.0 + 0.25 * z)))
    return jnp.where(z > -0.05, series, 1.0 - jnp.exp(z))


def _tiles(t_real):
    if t_real > 2048:
        tm = 416
        tp = -(-t_real // tm) * tm
        return tp, tm, tm // 2, tm // 2, tp, tp
    tm = 128
    tp = -(-t_real // tm) * tm
    return tp, tm, tm // 2, tm // 2, tm, tm


def _load_weights(copies, sems):
    cps = [pltpu.make_async_copy(s, d, sems.at[k]) for k, (s, d) in enumerate(copies)]
    for cp in cps:
        cp.start()
    for cp in cps:
        cp.wait()


def _position():
    x, y, c = lax.axis_index("x"), lax.axis_index("y"), lax.axis_index("c")
    chips = [(1 - x, y), (x, 1 - y), (1 - x, 1 - y)]
    return x, y, c, chips


def _slot(p):
    return 4 * p[0] + 2 * p[1] + p[2]


class _Lazy(dict):
    def __getitem__(self, key):
        val = dict.__getitem__(self, key)
        return val() if callable(val) else val


class _Gather:
    def __init__(self, shards, pass_on_at=None):
        self.shards = list(shards)
        self.n = len(self.shards)
        self.pass_on_at = pass_on_at

    def inputs(self):
        return self.shards

    def out_shape(self):
        return [jax.ShapeDtypeStruct((N_DEV,) + s.shape, s.dtype) for s in self.shards]

    N_SEMS = 9

    def scratch(self):
        return [pltpu.SemaphoreType.DMA((self.N_SEMS * self.n,)), pltpu.SemaphoreType.DMA((self.N_SEMS * self.n,)),
                pltpu.SemaphoreType.DMA((self.n,))]

    def _plan(self, ins, outs, sems):
        send_sems, recv_sems, local_sems = sems
        x, y, c, _ = _position()
        me, sib, xn, yn, dg = (x, y, c), (x, y, 1 - c), (1 - x, y, c), (x, 1 - y, c), (1 - x, 1 - y, c)
        other = lambda p: (p[0], p[1], 1 - c)

        def blk(a, p, half=None):
            ref = outs[a].at[_slot(p)]
            if half is None:
                return ref
            rows = self.shards[a].shape[0] // 2
            return ref.at[pl.ds(half * rows, rows)]

        def copy(a, k, dst, to, src=None):
            return pltpu.make_async_remote_copy(
                src_ref=dst if src is None else src, dst_ref=dst,
                send_sem=send_sems.at[self.N_SEMS * a + k], recv_sem=recv_sems.at[self.N_SEMS * a + k],
                device_id=to, device_id_type=MESH)

        cp = _Lazy(mine=lambda: [pltpu.make_async_copy(ins[a], blk(a, me), local_sems.at[a]) for a in range(self.n)])
        for a in range(self.n):
            cp[a] = _Lazy(
                own=lambda a=a: [copy(a, 0, blk(a, me), sib, src=ins[a]), copy(a, 1, blk(a, me), xn, src=ins[a]),
                                 copy(a, 2, blk(a, me), yn, src=ins[a])],
                from_x=lambda a=a: copy(a, 1, blk(a, xn), me), from_y=lambda a=a: copy(a, 2, blk(a, yn), me),
                relay_x=lambda a=a: copy(a, 3, blk(a, xn, 0), yn), relay_y=lambda a=a: copy(a, 4, blk(a, yn, 1), xn),
                diag0=lambda a=a: copy(a, 3, blk(a, dg, 0), me), diag1=lambda a=a: copy(a, 4, blk(a, dg, 1), me),
                pass_x=lambda a=a: copy(a, 5, blk(a, xn), sib), pass_y=lambda a=a: copy(a, 6, blk(a, yn), sib),
                pass_d0=lambda a=a: copy(a, 7, blk(a, dg, 0), sib), pass_d1=lambda a=a: copy(a, 8, blk(a, dg, 1), sib),
                from_sib=lambda a=a: [copy(a, 0, blk(a, sib), me), copy(a, 5, blk(a, other(xn)), me),
                                      copy(a, 6, blk(a, other(yn)), me), copy(a, 7, blk(a, other(dg), 0), me),
                                      copy(a, 8, blk(a, other(dg), 1), me)])
        return cp

    def start(self, ins, outs, sems):
        cp = self._plan(ins, outs, sems)
        for c in cp["mine"]:
            c.start()
        for a in range(self.n):
            for c in cp[a]["own"]:
                c.start()

    def pass_on(self, ins, outs, sems):
        cp = self._plan(ins, outs, sems)
        for a in range(self.n):
            cp[a]["from_x"].wait_recv()
            cp[a]["relay_x"].start()
            cp[a]["pass_x"].start()
        for a in range(self.n):
            cp[a]["from_y"].wait_recv()
            cp[a]["relay_y"].start()
            cp[a]["pass_y"].start()

    def pass_on_relayed(self, ins, outs, sems):
        cp = self._plan(ins, outs, sems)
        for a in range(self.n):
            cp[a]["diag0"].wait_recv()
            cp[a]["pass_d0"].start()
            cp[a]["diag1"].wait_recv()
            cp[a]["pass_d1"].start()

    def finish(self, ins, outs, sems):
        if self.pass_on_at is None:
            self.pass_on(ins, outs, sems)
            self.pass_on_relayed(ins, outs, sems)
        cp = self._plan(ins, outs, sems)
        for a in range(self.n):
            for c in cp[a]["from_sib"]:
                c.wait_recv()
            for c in cp[a]["own"] + [cp[a][k] for k in ("relay_x", "relay_y", "pass_x", "pass_y", "pass_d0", "pass_d1")]:
                c.wait_send()
        for c in cp["mine"]:
            c.wait()


def _hosted(inner, n_in, n_out, comm, grid):
    if comm is None:
        return inner
    nc_in, nc_out, ns = len(comm.inputs()), len(comm.out_shape()), len(comm.scratch())

    def body(*refs):
        o0 = n_in + nc_in
        s0 = o0 + n_out + nc_out
        main = refs[:n_in] + refs[o0:o0 + n_out] + refs[s0:len(refs) - ns]
        c_in, c_out, c_sems = refs[n_in:o0], refs[o0 + n_out:s0], refs[len(refs) - ns:]
        ids = [pl.program_id(ax) for ax in range(len(grid))]
        first = functools.reduce(jnp.logical_and, [i == 0 for i in ids])
        last = functools.reduce(jnp.logical_and, [i == g - 1 for i, g in zip(ids, grid)])

        @pl.when(first)
        def _():
            comm.start(c_in, c_out, c_sems)

        inner(*main)

        if getattr(comm, "pass_on_at", None) is not None:
            assert len(grid) == 1
            first_at, second_at = (min(grid[0] - 1, int(frac * grid[0])) for frac in comm.pass_on_at)
            assert first_at < second_at

            @pl.when(ids[0] == first_at)
            def _():
                comm.pass_on(c_in, c_out, c_sems)

            @pl.when(ids[0] == second_at)
            def _():
                comm.pass_on_relayed(c_in, c_out, c_sems)

        @pl.when(last)
        def _():
            comm.finish(c_in, c_out, c_sems)

    return body


def _call(inner, name, grid, in_specs, out_specs, out_shape, scratch, args, comm=None):
    n_in, n_out = len(args), len(out_shape)
    body = _hosted(inner, n_in, n_out, comm, grid)
    if comm is not None:
        in_specs = list(in_specs) + [_any()] * len(comm.inputs())
        args = list(args) + comm.inputs()
        out_specs = list(out_specs) + [_any()] * len(comm.out_shape())
        out_shape = list(out_shape) + comm.out_shape()
        scratch = list(scratch) + comm.scratch()
    outs = pl.pallas_call(
        body, name=name, grid=grid, in_specs=list(in_specs), out_specs=list(out_specs), out_shape=list(out_shape),
        scratch_shapes=list(scratch), compiler_params=_params(len(grid)))(*args)
    return list(outs[:n_out]), list(outs[n_out:])


def _first_gather(shards, small_idx, x2, t2, n_meta, tp):
    comm = _Gather(shards)
    n = comm.n
    seq, d = x2.shape
    t_real = n_meta + seq
    n_pad = tp - t_real
    cw = d // N_DEV
    rows = STAGE_ROWS if seq % STAGE_ROWS == 0 else seq
    n_chunks = seq // rows

    def body(*refs):
        ins, (x_ref, t_ref) = refs[:n], refs[n:n + 2]
        outs, (h0_ref, tg_ref) = refs[n + 2:2 * n + 2], refs[2 * n + 2:2 * n + 4]
        sems = refs[2 * n + 4:2 * n + 7]
        buf, zeros, in_sems, out_sems, misc_sems = refs[2 * n + 7:]
        comm.start(ins, outs, sems)
        zeros[...] = jnp.zeros_like(zeros)
        fills = [pltpu.make_async_copy(zeros.at[pl.ds(0, n_pad)], h0_ref.at[pl.ds(t_real, n_pad)], misc_sems.at[0]),
                 pltpu.make_async_copy(zeros.at[pl.ds(0, n_pad)], tg_ref.at[pl.ds(t_real, n_pad)], misc_sems.at[1]),
                 pltpu.make_async_copy(zeros.at[pl.ds(0, n_meta)], tg_ref.at[pl.ds(0, n_meta)], misc_sems.at[2])]
        for cp in fills:
            cp.start()
        jobs = [(src, dst, c) for src, dst in ((x_ref, h0_ref), (t_ref, tg_ref)) for c in range(n_chunks)]

        def load(k):
            src, _, c = jobs[k]
            return pltpu.make_async_copy(src.at[pl.ds(c * rows, rows)], buf.at[k % 2], in_sems.at[k % 2])

        def store(k):
            _, dst, c = jobs[k]
            return pltpu.make_async_copy(buf.at[k % 2], dst.at[pl.ds(n_meta + c * rows, rows)], out_sems.at[k % 2])

        load(0).start()
        for k in range(len(jobs)):
            load(k).wait()
            if k + 1 < len(jobs):
                if k >= 1:
                    store(k - 1).wait()
                load(k + 1).start()
            store(k).start()
        for k in range(max(0, len(jobs) - 2), len(jobs)):
            store(k).wait()
        comm.finish(ins, outs, sems)
        meta = [pltpu.make_async_copy(outs[small_idx].at[k, pl.ds(0, n_meta)],
                                      h0_ref.at[pl.ds(0, n_meta), pl.ds(k * cw, cw)], misc_sems.at[3 + k])
                for k in range(N_DEV)]
        for cp in meta:
            cp.start()
        for cp in fills + meta:
            cp.wait()

    staged = [jax.ShapeDtypeStruct((tp, d), F32)] * 2
    outs = pl.pallas_call(
        body, name="weights_all_gather", out_shape=comm.out_shape() + staged,
        in_specs=[_any()] * (n + 2), out_specs=[_any()] * (n + 2),
        scratch_shapes=comm.scratch() + [
            pltpu.VMEM((2, rows, d), F32), pltpu.VMEM((max(n_pad, n_meta), d), F32),
            pltpu.SemaphoreType.DMA((2,)), pltpu.SemaphoreType.DMA((2,)), pltpu.SemaphoreType.DMA((3 + N_DEV,))],
        compiler_params=pltpu.CompilerParams(vmem_limit_bytes=V7X_VMEM_LIMIT),
    )(*shards, x2, t2)
    return outs[:n], outs[n], outs[n + 1]


def _chip_copies(c_ref, land_ref, sems):
    _, _, c, chips = _position()
    return [pltpu.make_async_remote_copy(
        src_ref=c_ref.at[2 * cx + cy], dst_ref=land_ref.at[j], send_sem=sems[j], recv_sem=sems[3 + j],
        device_id=(cx, cy, c), device_id_type=MESH) for j, (cx, cy) in enumerate(chips)]


def _scatter_copies(g_ref, land_ref, sems):
    x, y, c, _ = _position()
    cps = []
    for k in range(1, N_DEV):
        peer = (x ^ (k >> 2), y ^ ((k >> 1) & 1), c ^ (k & 1))
        cps.append(pltpu.make_async_remote_copy(
            src_ref=g_ref.at[_slot(peer)], dst_ref=land_ref.at[k - 1], send_sem=sems[k - 1],
            recv_sem=sems[N_DEV - 1 + k - 1], device_id=peer, device_id_type=MESH))
    return cps


def _bcast_copies(b_ref, land_ref, sems):
    x, y, c, _ = _position()
    mine = land_ref.at[_slot((x, y, c))]
    cps = []
    for k in range(1, N_DEV):
        peer = (x ^ (k >> 2), y ^ ((k >> 1) & 1), c ^ (k & 1))
        cps.append(pltpu.make_async_remote_copy(
            src_ref=b_ref, dst_ref=mine, send_sem=sems[k - 1], recv_sem=sems[N_DEV - 1 + k - 1],
            device_id=peer, device_id_type=MESH))
    return cps + [pltpu.make_async_copy(b_ref, mine, sems[2 * (N_DEV - 1)])]


def _exchanges_start(name, parts):
    hbm = pl.BlockSpec(memory_space=pltpu.HBM)
    sem = pl.BlockSpec(memory_space=pltpu.SEMAPHORE)
    n_parts = len(parts)
    total = sum(n for _, n, _, _ in parts)

    def body(*refs):
        ins, sems, token = refs[:2 * n_parts], refs[2 * n_parts:2 * n_parts + total], refs[4 * n_parts + total]
        at = 0
        for j, (copies, n, _, _) in enumerate(parts):
            for cp in copies(ins[2 * j], ins[2 * j + 1], sems[at:at + n]):
                cp.start()
            at += n
        token[...] = jnp.zeros_like(token)

    flat = []
    for _, _, src, land_shape in parts:
        flat += [pltpu.with_memory_space_constraint(src, pltpu.HBM),
                 pltpu.with_memory_space_constraint(lax.empty(land_shape, src.dtype), pltpu.HBM)]
    outs = pl.pallas_call(
        body, name=name + "_start",
        out_shape=(pltpu.SemaphoreType.DMA(()),) * total + tuple(pltpu.HBM(a.shape, a.dtype) for a in flat)
        + (jax.ShapeDtypeStruct((SUBLANES, 128), F32),),
        in_specs=(hbm,) * len(flat),
        out_specs=(sem,) * total + (hbm,) * len(flat) + (pl.BlockSpec(memory_space=pltpu.VMEM),),
        input_output_aliases={j: total + j for j in range(len(flat))},
        compiler_params=pltpu.CompilerParams(has_side_effects=pltpu.SideEffectType.DATAFLOW_SIDE_EFFECTING),
    )(*flat)
    res, at = [], 0
    for j, (_, n, _, _) in enumerate(parts):
        res.append((outs[at:at + n], outs[total + 2 * j], outs[total + 2 * j + 1]))
        at += n
    return res, outs[total + len(flat)]


def _exchange_start(name, copies, n_copies, src):
    (part,), token = _exchanges_start(name, [(copies, 2 * n_copies, src, (n_copies,) + src.shape[1:])])
    return (*part, token)


def _exchange_wait(name, copies, sems, src_thru, land_thru, after):
    hbm = pl.BlockSpec(memory_space=pltpu.HBM)
    sem = pl.BlockSpec(memory_space=pltpu.SEMAPHORE)
    n_sems = len(sems)

    def body(s_ref, land_ref, *refs):
        for cp in copies(s_ref, land_ref, refs[:n_sems]):
            cp.wait()

    return pl.pallas_call(
        body, name=name + "_wait",
        out_shape=(pltpu.HBM(src_thru.shape, src_thru.dtype), pltpu.HBM(land_thru.shape, land_thru.dtype)),
        in_specs=(hbm, hbm) + (sem,) * n_sems + (pl.BlockSpec(memory_space=pl.ANY),), out_specs=(hbm, hbm),
        input_output_aliases={0: 0, 1: 1},
        compiler_params=pltpu.CompilerParams(has_side_effects=pltpu.SideEffectType.DATAFLOW_SIDE_EFFECTING),
    )(src_thru, land_thru, *sems, after)


def _pair_reduce(grad, core):
    blk = grad.shape[2:]
    zeros = (0,) * len(blk)

    def body(core_ref, g_hbm, own_ref, o_ref, landed, send_sems, recv_sems):
        del core_ref
        i = pl.program_id(0)
        x, y, c, _ = _position()

        def copy(k):
            return pltpu.make_async_remote_copy(
                src_ref=g_hbm.at[k, 1 - c], dst_ref=landed.at[k], send_sem=send_sems.at[k],
                recv_sem=recv_sems.at[k], device_id=(x, y, 1 - c), device_id_type=MESH)

        @pl.when(i == 0)
        def _():
            for k in range(4):
                copy(k).start()

        for k in range(4):
            @pl.when(i == k)
            def _(k=k):
                copy(k).wait_recv()

        o_ref[...] = (own_ref[...].astype(F32) + landed[i].astype(F32)).astype(BF16)

        @pl.when(i == 3)
        def _():
            for k in range(4):
                copy(k).wait_send()

    return pl.pallas_call(
        body, name="grads_pair_reduce",
        out_shape=jax.ShapeDtypeStruct((4,) + blk, BF16),
        grid_spec=pltpu.PrefetchScalarGridSpec(
            num_scalar_prefetch=1, grid=(4,),
            in_specs=[_any(), pl.BlockSpec((None, None) + blk, lambda i, cr: (i, cr[0]) + zeros)],
            out_specs=pl.BlockSpec((None,) + blk, lambda i, cr: (i,) + zeros),
            scratch_shapes=[pltpu.VMEM((4,) + blk, BF16), pltpu.SemaphoreType.DMA((4,)),
                            pltpu.SemaphoreType.DMA((4,))]),
        compiler_params=_params(1),
    )(core, grad, grad)


def _adamw(w, g, m, v):
    m2 = ADAM_B1 * m + (1.0 - ADAM_B1) * g
    v2 = ADAM_B2 * v + (1.0 - ADAM_B2) * (g * g)
    m_hat = m2 / (1.0 - ADAM_B1 ** ADAM_STEP)
    v_hat = v2 / (1.0 - ADAM_B2 ** ADAM_STEP)
    delta = -ADAM_LR * (m_hat / (jnp.sqrt(v_hat) + ADAM_EPS) + ADAM_WD * w)
    return delta, m2, v2


def _final_adamw(own, recv, idx, parts, after):
    blk = own.shape[1:]
    n_recv = recv.shape[0]
    n_parts = len(parts)
    per = blk[0] // n_parts if n_parts > 1 else None
    rows = blk[-2]
    n_chunks = 1 if n_parts > 1 else (4 if rows % 64 == 0 and rows >= 512 else (2 if rows % 32 == 0 else 1))
    cblk = blk[:-2] + (rows // n_chunks, blk[-1])
    lead = (0,) * (len(blk) - 2)

    def body(idx_ref, c_ref, r_ref, after_ref, *refs):
        del idx_ref, after_ref
        ins, outs = refs[:3 * n_parts], refs[3 * n_parts:]
        g = c_ref[...].astype(F32)
        for k in range(n_recv):
            g = g + r_ref[k].astype(F32)
        for p in range(n_parts):
            w_ref, m_ref, v_ref = ins[3 * p:3 * p + 3]
            if n_parts == 1:
                gp = g
            elif per == 1:
                gp = g[p]
            else:
                gp = g[p * per:(p + 1) * per]
            delta, m2, v2 = _adamw(w_ref[0], gp, m_ref[0], v_ref[0])
            o = outs[4 * p:4 * p + 4]
            o[0][0] = gp
            o[1][0] = delta
            o[2][0] = m2
            o[3][0] = v2

    flat = [a for wmv in parts for a in wmv]

    def part_spec(a):
        shape = a.shape[:-2] + (a.shape[-2] // n_chunks, a.shape[-1])
        return pl.BlockSpec(shape, lambda i, cr, nd=a.ndim: (0,) * (nd - 2) + (i, 0))

    outs = pl.pallas_call(
        body, name="grads_sum_adamw",
        out_shape=[jax.ShapeDtypeStruct(wmv[0].shape, F32) for wmv in parts for _ in range(4)],
        grid_spec=pltpu.PrefetchScalarGridSpec(
            num_scalar_prefetch=1, grid=(n_chunks,),
            in_specs=[pl.BlockSpec((None,) + cblk, lambda i, cr: (cr[0],) + lead + (i, 0)),
                      pl.BlockSpec((n_recv,) + cblk, lambda i, cr: (0,) + lead + (i, 0))]
                     + [_any()] + [part_spec(a) for a in flat],
            out_specs=[part_spec(wmv[0]) for wmv in parts for _ in range(4)]),
        compiler_params=_params(1),
    )(idx, own, recv, after, *flat)
    return [tuple(outs[4 * p:4 * p + 4]) for p in range(n_parts)]


def _small_adamw(partials, layout, me_index, after):
    d = partials[0].shape[-1]
    rows = sum(p.shape[1] for p in partials)
    n = len(layout)
    n_p = len(partials)
    cw = d // N_DEV

    def body(me_ref, *refs):
        p_refs, refs = refs[:n_p], refs[n_p + 1:]
        ins, t_ref, outs = refs[:3 * n], refs[3 * n], refs[3 * n + 1:]
        me = me_ref[0]
        row = 0
        for p_ref in p_refs:
            total = p_ref[0]
            for j in range(1, N_DEV):
                total = total + p_ref[j]
            t_ref[row:row + p_ref.shape[1], :] = total
            row += p_ref.shape[1]
        for e, (kind, r0, nr, _, _, _) in enumerate(layout):
            w_ref, m_ref, v_ref = ins[3 * e:3 * e + 3]
            o = outs[4 * e:4 * e + 4]
            if kind == "rep":
                g = t_ref[r0:r0 + nr, :]
                delta, m2, v2 = _adamw(w_ref[...], g, m_ref[...], v_ref[...])
                for ref, val in zip(o, (g, delta, m2, v2)):
                    ref[...] = val
            elif kind == "wide":
                for q in range(nr):
                    sl = slice(q * d, (q + 1) * d)
                    g = t_ref[r0 + q:r0 + q + 1, :]
                    delta, m2, v2 = _adamw(w_ref[:, sl], g, m_ref[:, sl], v_ref[:, sl])
                    for ref, val in zip(o, (g, delta, m2, v2)):
                        ref[:, sl] = val
            else:
                for j in range(N_DEV):
                    @pl.when(me == j)
                    def _(j=j, o=o, w_ref=w_ref, m_ref=m_ref, v_ref=v_ref, r0=r0, nr=nr):
                        g = t_ref[r0:r0 + nr, j * cw:(j + 1) * cw]
                        delta, m2, v2 = _adamw(w_ref[...], g, m_ref[...], v_ref[...])
                        for ref, val in zip(o, (g, delta, m2, v2)):
                            ref[...] = val

    flat = [a for ent in layout for a in ent[3:]]
    vm = pl.BlockSpec(memory_space=pltpu.VMEM)
    outs = pl.pallas_call(
        body, name="small_adamw",
        out_shape=[jax.ShapeDtypeStruct((rows, d), F32)]
                  + [jax.ShapeDtypeStruct(ent[3].shape, F32) for ent in layout for _ in range(4)],
        in_specs=[pl.BlockSpec(memory_space=pltpu.SMEM)] + [vm] * n_p + [_any()] + [vm] * len(flat),
        out_specs=[vm] * (1 + 4 * n),
        compiler_params=pltpu.CompilerParams(vmem_limit_bytes=V7X_VMEM_LIMIT),
    )(me_index, *partials, after, *flat)
    return outs[0], [tuple(outs[1 + 4 * e:5 + 4 * e]) for e in range(n)]


def _ffn_fwd(h, g, wgu, wd, tm, loss=None, comm=None):
    tp, d = h.shape
    f = wd.shape[0]
    fc = f // FFN_FWD_CHUNKS
    nt = tp // tm
    with_loss = loss is not None
    if with_loss:
        tgt, gf, n_meta, t_real = loss

    def body(*refs):
        if with_loss:
            (h_ref, g_ref, wgu_hbm, wd_hbm, tgt_ref, gf_ref, out_ref, gu_ref, n_ref, tail_ref,
             wgu_v, wd_v, sems) = refs
        else:
            h_ref, g_ref, wgu_hbm, wd_hbm, out_ref, gu_ref, n_ref, wgu_v, wd_v, sems = refs
        i = pl.program_id(0)

        @pl.when(i == 0)
        def _():
            _load_weights([(wgu_hbm, wgu_v), (wd_hbm, wd_v)], sems)
            if with_loss:
                tail_ref[...] = jnp.zeros_like(tail_ref)

        x = h_ref[...]
        n, _ = _rms_fwd(x, g_ref[...])
        nb = n.astype(BF16)
        n_ref[...] = nb
        acc = jnp.zeros((tm, d), F32)
        for j in range(FFN_FWD_CHUNKS):
            cols = slice(j * fc, (j + 1) * fc)
            gate = _nt(nb, wgu_v[pl.ds(j * fc, fc), :])
            up = _nt(nb, wgu_v[pl.ds(f + j * fc, fc), :])
            gu_ref[0, :, cols] = gate.astype(BF16)
            gu_ref[1, :, cols] = up.astype(BF16)
            act = (gate * _sigmoid(gate) * up).astype(BF16)
            acc = acc + _nn(act, wd_v[pl.ds(j * fc, fc), :])
        hn = x + FFN_RES * acc
        if not with_loss:
            out_ref[...] = hn
        else:
            gfv = gf_ref[...]
            r = lax.rsqrt(jnp.mean(hn * hn, axis=-1, keepdims=True) + EPS)
            xr = hn * r
            rows = i * tm + lax.broadcasted_iota(jnp.int32, (tm, 1), 0)
            mask = jnp.logical_and(rows >= n_meta, rows < t_real)
            diff = jnp.where(mask, xr * gfv - tgt_ref[...], 0.0)
            tail_ref[TAIL_LOSS:TAIL_LOSS + 1, :] += jnp.zeros((1, d), F32) + 0.5 * jnp.sum(diff * diff) / d
            dy = diff / d
            gy = dy * gfv
            out_ref[...] = r * (gy - xr * jnp.mean(gy * xr, axis=-1, keepdims=True))
            tail_ref[TAIL_FINAL:TAIL_FINAL + 1, :] += _rowsum(dy * xr)

    row = pl.BlockSpec((tm, d), lambda i: (i, 0))
    vec = pl.BlockSpec((1, d), lambda i: (0, 0))
    in_specs = [row, vec, _any(), _any()]
    out_shape = [jax.ShapeDtypeStruct((tp, d), F32), jax.ShapeDtypeStruct((2, tp, f), BF16),
                 jax.ShapeDtypeStruct((tp, d), BF16)]
    out_specs = [row, pl.BlockSpec((2, tm, f), lambda i: (0, i, 0)), row]
    args = [h, g, wgu, wd]
    if with_loss:
        in_specs += [row, vec]
        out_shape += [jax.ShapeDtypeStruct((SUBLANES, d), F32)]
        out_specs += [pl.BlockSpec((SUBLANES, d), lambda i: (0, 0))]
        args += [tgt, gf]
    return _call(body, "ffn_fwd_loss" if with_loss else "ffn_fwd", (nt,), in_specs, out_specs, out_shape,
                 [pltpu.VMEM((2 * f, d), BF16), pltpu.VMEM((f, d), BF16), pltpu.SemaphoreType.DMA((2,))],
                 args, comm)


def _ffn_bwd(dh, h, gu, g, wgu, wd, tm, tail, tail_row, after):
    tp, d = h.shape
    f = wd.shape[0]
    fc = f // FFN_CHUNKS
    nt = tp // tm

    def body(dh_ref, h_ref, gu_ref, g_ref, tail_ref, wgu_hbm, wd_hbm, after_ref,
             dhin_ref, dgu_ref, act_ref, df_ref, dg_ref, wgu_v, wd_v, dn_v, sems):
        del after_ref
        i, j = pl.program_id(0), pl.program_id(1)

        @pl.when(jnp.logical_and(i == 0, j == 0))
        def _():
            _load_weights([(wgu_hbm, wgu_v), (wd_hbm, wd_v)], sems)
            dg_ref[...] = tail_ref[...]

        dfb = (FFN_RES * dh_ref[...]).astype(BF16)

        @pl.when(j == 0)
        def _():
            df_ref[...] = dfb
            dn_v[...] = jnp.zeros_like(dn_v)

        lo = pl.multiple_of(j * fc, 16)
        dact = _nt(dfb, wd_v[pl.ds(lo, fc), :])
        gate = gu_ref[0].astype(F32)
        up = gu_ref[1].astype(F32)
        sg = _sigmoid(gate)
        silu = gate * sg
        act_ref[...] = (silu * up).astype(BF16)
        dgate = (dact * up * (sg * (1.0 + gate * (1.0 - sg)))).astype(BF16)
        dup = (dact * silu).astype(BF16)
        dgu_ref[0] = dgate
        dgu_ref[1] = dup
        dn_v[...] += _nn(dgate, wgu_v[pl.ds(lo, fc), :]) + _nn(dup, wgu_v[pl.ds(pl.multiple_of(f + j * fc, 16), fc), :])

        @pl.when(j == FFN_CHUNKS - 1)
        def _():
            x = h_ref[...]
            r = lax.rsqrt(jnp.mean(x * x, axis=-1, keepdims=True) + EPS)
            dx, dgp = _rms_bwd(dn_v[...], x, r, g_ref[...])
            dhin_ref[...] = dh_ref[...] + dx
            dg_ref[tail_row:tail_row + 1, :] += dgp

    row = pl.BlockSpec((tm, d), lambda i, j: (i, 0))
    vec = pl.BlockSpec((1, d), lambda i, j: (0, 0))
    tile = pl.BlockSpec((SUBLANES, d), lambda i, j: (0, 0))
    hid2 = pl.BlockSpec((2, tm, fc), lambda i, j: (0, i, j))
    return _call(
        body, "ffn_bwd", (nt, FFN_CHUNKS),
        [row, row, hid2, vec, tile, _any(), _any(), _any()],
        [row, hid2, pl.BlockSpec((tm, fc), lambda i, j: (i, j)), row, tile],
        [jax.ShapeDtypeStruct((tp, d), F32), jax.ShapeDtypeStruct((2, tp, f), BF16),
         jax.ShapeDtypeStruct((tp, f), BF16), jax.ShapeDtypeStruct((tp, d), BF16),
         jax.ShapeDtypeStruct((SUBLANES, d), F32)],
        [pltpu.VMEM((2 * f, d), BF16), pltpu.VMEM((f, d), BF16), pltpu.VMEM((tm, d), F32),
         pltpu.SemaphoreType.DMA((2,))],
        [dh, h, gu, g, tail, wgu, wd, after])


def _piece_segments(q, d, nb_cols):
    segs = []
    for j in range(N_DEV):
        lo, hi = max(q * d, j * nb_cols), min((q + 1) * d, (j + 1) * nb_cols)
        if lo < hi:
            segs.append((j, lo - q * d, hi - q * d, lo - j * nb_cols, hi - j * nb_cols))
    return segs


def _w3_copies(w3_hbm, rows, w3_v):
    return [(w3_hbm.at[k, pl.ds(q * rows, rows)], w3_v.at[q, pl.ds(k * rows, rows)])
            for q in range(3) for k in range(N_DEV)]


def _gates(xrb, wg_ref, ba, bx, lam, hd):
    pre_r, pre_i = [], []
    for hh in range(N_HEADS):
        xh = xrb[:, hh * hd:(hh + 1) * hd]
        pre_r.append(_nn(xh, wg_ref[0, hh]))
        pre_i.append(_nn(xh, wg_ref[1, hh]))
    r = _sigmoid(jnp.concatenate(pre_r, axis=1) + ba)
    ig = _sigmoid(jnp.concatenate(pre_i, axis=1) + bx)
    sp = _softplus(-lam)
    log_a = -RG_LRU_C * r * sp
    a = jnp.exp(log_a)
    s = jnp.sqrt(_one_minus_exp(2.0 * log_a))
    return r, ig, sp, a, s


def _scan_fwd(a, u, h_prev):
    tm = a.shape[0]
    rows = lax.broadcasted_iota(jnp.int32, a.shape, 0)
    d = 1
    while d < tm:
        if d < SUBLANES:
            keep = rows >= d
            u = jnp.where(keep, a * pltpu.roll(u, d, 0) + u, u)
            a = jnp.where(keep, a * pltpu.roll(a, d, 0), a)
        else:
            u = jnp.concatenate([u[:d], a[d:] * u[:tm - d] + u[d:]], axis=0)
            a = jnp.concatenate([a[:d], a[d:] * a[:tm - d]], axis=0)
        d *= 2
    return u + a * h_prev


def _scan_bwd(b, v, g_next):
    tm = b.shape[0]
    rows = lax.broadcasted_iota(jnp.int32, b.shape, 0)
    d = 1
    while d < tm:
        if d < SUBLANES:
            keep = rows < tm - d
            v = jnp.where(keep, v + b * pltpu.roll(v, tm - d, 0), v)
            b = jnp.where(keep, b * pltpu.roll(b, tm - d, 0), b)
        else:
            v = jnp.concatenate([v[:tm - d] + b[:tm - d] * v[d:], v[tm - d:]], axis=0)
            b = jnp.concatenate([b[:tm - d] * b[d:], b[tm - d:]], axis=0)
        d *= 2
    return v + b * g_next


def _shifted_copies(ext_ref, es_ref, n_rows):
    for s in range(1, SUBLANES):
        es_ref[s, pl.ds(0, n_rows), :] = ext_ref[pl.ds(s, n_rows), :]


def _tap(ext_ref, es_ref, off, tm):
    q, s = divmod(off, SUBLANES)
    if s == 0:
        return ext_ref[pl.ds(SUBLANES * q, tm), :]
    return es_ref[s, pl.ds(SUBLANES * q, tm), :]


def _mixer_fwd(h, g, b_in, win_all, cw4, cb4, wg, ba, bx, lam, cw31, cb31, lng, lnb, bcp, w3_all, tm, comm=None):
    tp, d = h.shape
    nb_cols = win_all.shape[-1]
    n_in = N_DEV * nb_cols
    hd = wg.shape[-1]
    k4, k31 = cw4.shape[0], cw31.shape[0]
    w3_rows = d // N_DEV

    def body(h_ref, g_ref, b_ref, win_hbm, cw4_ref, cb4_ref, wg_ref, ba_ref, bx_ref, lam_ref, cw31_ref, cb31_ref,
             lng_ref, lnb_ref, bcp_ref, w3_hbm,
             h2_ref, p_ref, n_ref, xr_ref, hs_ref, v1_ref, ya_ref, yb_ref,
             win_v, w3_v, ext4, ext31, es31, hcar, sems):
        @pl.when(pl.program_id(0) == 0)
        def _():
            _load_weights([(win_hbm, win_v)] + _w3_copies(w3_hbm, w3_rows, w3_v), sems)
            ext4[pl.ds(0, CONV4_HALO), :] = jnp.zeros((CONV4_HALO, d), F32)
            ext31[pl.ds(0, CONV31_HALO), :] = jnp.zeros((CONV31_HALO, d), F32)
            hcar[...] = jnp.zeros_like(hcar)

        n, _ = _rms_fwd(h_ref[...], g_ref[...])
        nb = n.astype(BF16)
        n_ref[...] = nb

        def piece(q):
            parts = [_nn(nb, win_v[j, :, bl:bh]) for j, _, _, bl, bh in _piece_segments(q, d, nb_cols)]
            pq = (jnp.concatenate(parts, axis=1) + b_ref[:, q * d:(q + 1) * d]).astype(BF16)
            p_ref[:, q * d:(q + 1) * d] = pq
            return pq.astype(F32)

        x_rnn, y_rnn, glu_v, glu_g, gate_a, gate_b = [piece(q) for q in range(6)]

        ext4[pl.ds(CONV4_HALO, tm), :] = x_rnn
        xr = cb4_ref[...] + jnp.zeros((tm, d), F32)
        for k in range(k4):
            xr = xr + cw4_ref[k:k + 1, :] * ext4[pl.ds(CONV4_HALO - (k4 - 1) + k, tm), :]
        ext4[pl.ds(0, CONV4_HALO), :] = ext4[pl.ds(tm, CONV4_HALO), :]
        xrb = xr.astype(BF16)
        xr_ref[...] = xrb
        xr = xrb.astype(F32)
        _, ig, _, a, s = _gates(xrb, wg_ref, ba_ref[...], bx_ref[...], lam_ref[...], hd)
        hseq = _scan_fwd(a, s * (ig * xr), hcar[0:1, :])
        hcar[0:1, :] = hseq[tm - 1:tm, :]
        hs_ref[...] = hseq.astype(BF16)
        gl, _ = _gelu(y_rnn)
        ya = _nn((hseq * gl).astype(BF16), w3_v[0])
        ya_ref[...] = ya.astype(BF16)

        ext31[pl.ds(CONV31_HALO, tm), :] = glu_v * _sigmoid(glu_g)
        _shifted_copies(ext31, es31, tm + CONV31_HALO - SUBLANES)
        v1 = cb31_ref[...] + jnp.zeros((tm, d), F32)
        for k in range(k31):
            v1 = v1 + cw31_ref[k:k + 1, :] * _tap(ext31, es31, CONV31_HALO - (k31 - 1) + k, tm)
        ext31[pl.ds(0, CONV31_HALO), :] = ext31[pl.ds(tm, CONV31_HALO), :]
        v1b = v1.astype(BF16)
        v1_ref[...] = v1b
        v1 = v1b.astype(F32)
        xc = v1 - jnp.mean(v1, axis=-1, keepdims=True)
        rstd = lax.rsqrt(jnp.mean(xc * xc, axis=-1, keepdims=True) + EPS)
        v2 = xc * rstd * lng_ref[...] + lnb_ref[...]
        yb = _nn((v2 * _sigmoid(v2)).astype(BF16), w3_v[1]) + bcp_ref[...]
        yb_ref[...] = yb.astype(BF16)

        merged = _sigmoid(gate_a) * ya + _sigmoid(gate_b) * yb
        h2_ref[...] = h_ref[...] + _nn(merged.astype(BF16), w3_v[2])

    row = pl.BlockSpec((tm, d), lambda i: (i, 0))
    wide = pl.BlockSpec((tm, n_in), lambda i: (i, 0))
    full = lambda a: pl.BlockSpec(a.shape, lambda i, nd=a.ndim: (0,) * nd)
    smalls = [cw4, cb4, wg, ba, bx, lam, cw31, cb31, lng, lnb, bcp]
    return _call(
        body, "mixer_fwd", (tp // tm,),
        [row, full(g), full(b_in), _any()] + [full(a) for a in smalls] + [_any()],
        [row, wide] + [row] * 6,
        [jax.ShapeDtypeStruct((tp, d), F32), jax.ShapeDtypeStruct((tp, n_in), BF16)]
        + [jax.ShapeDtypeStruct((tp, d), BF16)] * 6,
        [pltpu.VMEM(win_all.shape, BF16),
         pltpu.VMEM((3, d, d), BF16),
         pltpu.VMEM((tm + CONV4_HALO, d), F32),
         pltpu.VMEM((tm + CONV31_HALO, d), F32),
         pltpu.VMEM((SUBLANES, tm + CONV31_HALO, d), F32),
         pltpu.VMEM((SUBLANES, d), F32),
         pltpu.SemaphoreType.DMA((1 + 3 * N_DEV,))],
        [h, g, b_in, win_all, *smalls, w3_all], comm)


SG_BIN, SG_CW4, SG_CB4, SG_BA, SG_BX, SG_LAM, SG_CB31, SG_LNG, SG_LNB, SG_BCP, SG_MIX, SG_CW31 = 0, 6, 10, 11, 12, 13, 14, 15, 16, 17, 18, 19


def _mixer_bwd(dh2, h, g, proj, xr_s, hs_s, v1_s, ya_s, yb_s, win_t, cw4, wg, ba, bx, lam, cw31, lng, lnb, w3_all, tm,
               after):
    tp, d = dh2.shape
    n_in = proj.shape[1]
    hd = wg.shape[-1]
    k4, k31 = cw4.shape[0], cw31.shape[0]
    nt = tp // tm
    w3_rows = d // N_DEV
    sg_rows = -(-(SG_CW31 + k31) // SUBLANES) * SUBLANES
    halo_rows = 16
    per = tm // halo_rows

    def body(dh_ref, h_ref, g_ref, p_ref, xr_ref, hs_ref, hh_ref, v1_ref, ya_ref, yb_ref, win_hbm,
             cw4_ref, wg_ref, wgt_ref, ba_ref, bx_ref, lam_ref, cw31_ref, lng_ref, lnb_ref, w3_hbm, after_ref,
             dh1_ref, dp_ref, x3_ref, y3_ref, yg_ref, sg_ref,
             win_v, w3_v, extd4, extd31, es31, gcar, sems):
        i = pl.program_id(0)
        tile = nt - 1 - i

        @pl.when(i == 0)
        def _():
            _load_weights([(win_hbm, win_v)] + _w3_copies(w3_hbm, w3_rows, w3_v), sems)
            for q in range(3):
                w3_v[q] = w3_v[q].T
            extd4[pl.ds(tm, CONV4_HALO), :] = jnp.zeros((CONV4_HALO, d), F32)
            extd31[pl.ds(tm, CONV31_HALO), :] = jnp.zeros((CONV31_HALO, d), F32)
            gcar[...] = jnp.zeros_like(gcar)
            sg_ref[...] = jnp.zeros_like(sg_ref)

        def acc(row, val):
            sg_ref[row:row + 1, :] += _rowsum(val)

        rows = lax.broadcasted_iota(jnp.int32, (tm, d), 0)
        x_rnn = p_ref[:, 0:d].astype(F32)
        y_rnn = p_ref[:, d:2 * d].astype(F32)
        glu_v = p_ref[:, 2 * d:3 * d].astype(F32)
        glu_g = p_ref[:, 3 * d:4 * d].astype(F32)
        sga = _sigmoid(p_ref[:, 4 * d:5 * d].astype(F32))
        sgb = _sigmoid(p_ref[:, 5 * d:6 * d].astype(F32))
        ya = ya_ref[...].astype(F32)
        yb = yb_ref[...].astype(F32)

        dmob = dh_ref[...].astype(BF16)
        dmerged = _nn(dmob, w3_v[2])
        x3_ref[:, 0:d] = (sga * ya + sgb * yb).astype(BF16)
        y3_ref[:, 0:d] = dmob
        dya = sga * dmerged
        dyb = sgb * dmerged
        dn_parts = []

        def emit(q, val):
            vb = val.astype(BF16)
            dp_ref[:, q * d:(q + 1) * d] = vb
            acc(SG_BIN + q, val)
            term = _nn(vb, win_v[pl.ds(q * d, d), :])
            dn_parts[:] = [term if not dn_parts else dn_parts[0] + term]

        emit(4, dmerged * ya * sga * (1.0 - sga))
        emit(5, dmerged * yb * sgb * (1.0 - sgb))

        dyab = dya.astype(BF16)
        y3_ref[:, d:2 * d] = dyab
        dza = _nn(dyab, w3_v[0])
        hsv = hs_ref[...].astype(F32)
        gl, th = _gelu(y_rnn)
        x3_ref[:, d:2 * d] = (hsv * gl).astype(BF16)
        emit(1, dza * hsv * _gelu_grad(y_rnn, th))
        dhs = dza * gl
        xrb = xr_ref[...]
        xr = xrb.astype(F32)
        lam_v = lam_ref[...]
        r, ig, sp, a, s = _gates(xrb, wg_ref, ba_ref[...], bx_ref[...], lam_v, hd)
        b = jnp.where(rows == tm - 1, gcar[1:2, :], pltpu.roll(a, tm - 1, 0))
        big_g = _scan_bwd(b, dhs, gcar[0:1, :])
        gcar[0:1, :] = big_g[0:1, :]
        gcar[1:2, :] = a[0:1, :]
        h_before = jnp.where(tile > 0, hh_ref[halo_rows - 1:halo_rows, :].astype(F32), 0.0)
        h_prev = jnp.where(rows == 0, h_before, pltpu.roll(hsv, 1, 0))
        ds = big_g * ig * xr
        dla = big_g * h_prev * a - ds * (a * a) / jnp.maximum(s, 1e-20)
        acc(SG_LAM, dla * r * (RG_LRU_C * _sigmoid(-lam_v)))
        dpr = dla * (-RG_LRU_C * sp) * r * (1.0 - r)
        dpi = big_g * s * xr * ig * (1.0 - ig)
        acc(SG_BA, dpr)
        acc(SG_BX, dpi)
        dprb = dpr.astype(BF16)
        dpib = dpi.astype(BF16)
        yg_ref[:, 0:d] = dprb
        yg_ref[:, d:2 * d] = dpib
        back = []
        for hh in range(N_HEADS):
            sl = slice(hh * hd, (hh + 1) * hd)
            back.append(_nn(dprb[:, sl], wgt_ref[0, hh]) + _nn(dpib[:, sl], wgt_ref[1, hh]))
        dxr = big_g * s * ig + jnp.concatenate(back, axis=1)
        acc(SG_CB4, dxr)
        extd4[pl.ds(0, tm), :] = dxr
        dx_rnn = jnp.zeros((tm, d), F32)
        for k in range(k4):
            term = extd4[pl.ds(k4 - 1 - k, tm), :]
            dx_rnn = dx_rnn + cw4_ref[k:k + 1, :] * term
            acc(SG_CW4 + k, x_rnn * term)
        extd4[pl.ds(tm, CONV4_HALO), :] = extd4[pl.ds(0, CONV4_HALO), :]
        emit(0, dx_rnn)

        dybb = dyb.astype(BF16)
        y3_ref[:, 2 * d:3 * d] = dybb
        acc(SG_BCP, dyb)
        dv3 = _nn(dybb, w3_v[1])
        v1 = v1_ref[...].astype(F32)
        xc = v1 - jnp.mean(v1, axis=-1, keepdims=True)
        rstd = lax.rsqrt(jnp.mean(xc * xc, axis=-1, keepdims=True) + EPS)
        xhat = xc * rstd
        lng_v = lng_ref[...]
        v2 = xhat * lng_v + lnb_ref[...]
        s2 = _sigmoid(v2)
        x3_ref[:, 2 * d:3 * d] = (v2 * s2).astype(BF16)
        dv2 = dv3 * (s2 * (1.0 + v2 * (1.0 - s2)))
        acc(SG_LNG, dv2 * xhat)
        acc(SG_LNB, dv2)
        dxh = dv2 * lng_v
        dv1 = rstd * (dxh - jnp.mean(dxh, axis=-1, keepdims=True)
                      - xhat * jnp.mean(dxh * xhat, axis=-1, keepdims=True))
        acc(SG_CB31, dv1)
        extd31[pl.ds(0, tm), :] = dv1
        _shifted_copies(extd31, es31, tm + CONV31_HALO - SUBLANES)
        sgg = _sigmoid(glu_g)
        v0 = glu_v * sgg
        dv0 = jnp.zeros((tm, d), F32)
        for k in range(k31):
            term = _tap(extd31, es31, k31 - 1 - k, tm)
            dv0 = dv0 + cw31_ref[k:k + 1, :] * term
            acc(SG_CW31 + k, v0 * term)
        extd31[pl.ds(tm, CONV31_HALO), :] = extd31[pl.ds(0, CONV31_HALO), :]
        emit(2, dv0 * sgg)
        emit(3, dv0 * glu_v * sgg * (1.0 - sgg))

        dn = dn_parts[0]
        x = h_ref[...]
        rr = lax.rsqrt(jnp.mean(x * x, axis=-1, keepdims=True) + EPS)
        dx, dgp = _rms_bwd(dn, x, rr, g_ref[...])
        dh1_ref[...] = dh_ref[...] + dx
        sg_ref[SG_MIX:SG_MIX + 1, :] += dgp

    rev = lambda i: (nt - 1 - i, 0)
    row = pl.BlockSpec((tm, d), rev)
    wide = pl.BlockSpec((tm, n_in), rev)
    full = lambda a: pl.BlockSpec(a.shape, lambda i, nd=a.ndim: (0,) * nd)
    halo = pl.BlockSpec((halo_rows, d), lambda i: (jnp.maximum((nt - 1 - i) * per - 1, 0), 0))
    smalls = [cw4, wg, jnp.swapaxes(wg, 2, 3), ba, bx, lam, cw31, lng, lnb]
    return _call(
        body, "mixer_bwd", (nt,),
        [row, row, full(g), wide, row, row, halo, row, row, row, _any()]
        + [full(a) for a in smalls] + [_any(), _any()],
        [row, wide, pl.BlockSpec((tm, 3 * d), rev), pl.BlockSpec((tm, 3 * d), rev),
         pl.BlockSpec((tm, 2 * d), rev), pl.BlockSpec((sg_rows, d), lambda i: (0, 0))],
        [jax.ShapeDtypeStruct((tp, d), F32), jax.ShapeDtypeStruct((tp, n_in), BF16),
         jax.ShapeDtypeStruct((tp, 3 * d), BF16), jax.ShapeDtypeStruct((tp, 3 * d), BF16),
         jax.ShapeDtypeStruct((tp, 2 * d), BF16), jax.ShapeDtypeStruct((sg_rows, d), F32)],
        [pltpu.VMEM(win_t.shape, BF16),
         pltpu.VMEM((3, d, d), BF16),
         pltpu.VMEM((tm + CONV4_HALO, d), F32),
         pltpu.VMEM((tm + CONV31_HALO, d), F32),
         pltpu.VMEM((SUBLANES, tm + CONV31_HALO, d), F32),
         pltpu.VMEM((SUBLANES, d), F32),
         pltpu.SemaphoreType.DMA((1 + 3 * N_DEV,))],
        [dh2, h, g, proj, xr_s, hs_s, hs_s, v1_s, ya_s, yb_s, win_t, *smalls, w3_all, after])[0]


def _tn_matmul(name, x, y, x_spec, y_spec, n_blocks, kb, nb, tm, tp, out_shape, out_spec, out_view, after=None):
    nt = tp // tm

    def body(x_ref, y_ref, *refs):
        o_ref, acc = refs[-2:]
        i = pl.program_id(1)

        @pl.when(i == 0)
        def _():
            acc[...] = jnp.zeros_like(acc)

        acc[...] += _tn(x_ref[...], y_ref[...])

        @pl.when(i == nt - 1)
        def _():
            o_ref[...] = acc[...].astype(BF16).reshape(out_view)

    follows = [] if after is None else [after]
    outs, extra = _call(body, name, (n_blocks, nt), [x_spec, y_spec] + [_any()] * len(follows), [out_spec],
                        [jax.ShapeDtypeStruct(out_shape, BF16)], [pltpu.VMEM((kb, nb), F32)], [x, y] + follows)
    return outs[0], extra


def kernel(x, meta_tokens, ffn1_norm, ffn1_w_gu, ffn1_w_down, mix_norm, w_in, b_in, rnn_conv_w, rnn_conv_b, rg_w_a, rg_b_a, rg_w_x, rg_b_x, rg_lambda, rnn_w_proj, conv_dw_w, conv_dw_b, conv_ln_g, conv_ln_b, conv_w_proj, conv_b_proj, w_out, ffn2_norm, ffn2_w_gu, ffn2_w_down, final_norm, loss_target, m_meta_tokens, m_ffn1_norm, m_ffn1_w_gu, m_ffn1_w_down, m_mix_norm, m_w_in, m_b_in, m_rnn_conv_w, m_rnn_conv_b, m_rg_w_a, m_rg_b_a, m_rg_w_x, m_rg_b_x, m_rg_lambda, m_rnn_w_proj, m_conv_dw_w, m_conv_dw_b, m_conv_ln_g, m_conv_ln_b, m_conv_w_proj, m_conv_b_proj, m_w_out, m_ffn2_norm, m_ffn2_w_gu, m_ffn2_w_down, m_final_norm, v_meta_tokens, v_ffn1_norm, v_ffn1_w_gu, v_ffn1_w_down, v_mix_norm, v_w_in, v_b_in, v_rnn_conv_w, v_rnn_conv_b, v_rg_w_a, v_rg_b_a, v_rg_w_x, v_rg_b_x, v_rg_lambda, v_rnn_w_proj, v_conv_dw_w, v_conv_dw_b, v_conv_ln_g, v_conv_ln_b, v_conv_w_proj, v_conv_b_proj, v_w_out, v_ffn2_norm, v_ffn2_w_gu, v_ffn2_w_down, v_final_norm):
    w = dict(locals())
    seq, d = x.shape[1], x.shape[2]
    n_meta = meta_tokens.shape[0]
    t_real = n_meta + seq
    tp, tm, tmx_fwd, tmx, tmt, tmw = _tiles(t_real)
    fb = ffn1_w_gu.shape[-1]
    wr = ffn1_w_down.shape[1]
    f = N_DEV * wr
    fc = f // FFN_CHUNKS
    nbc = w_in.shape[-1]
    n_in = N_DEV * nbc
    pr = rnn_w_proj.shape[1]
    hd = rg_w_a.shape[-1]
    gr = rg_w_a.shape[2]
    cw = meta_tokens.shape[1]
    k4, k31 = rnn_conv_w.shape[1], conv_dw_w.shape[1]
    assert n_in == 6 * d and 2 * wr == fb and N_HEADS * hd == d and pr * N_DEV == d

    xi, yi, ci = lax.axis_index("x"), lax.axis_index("y"), lax.axis_index("c")
    core = ci.astype(jnp.int32).reshape(1)
    chip = (2 * xi + yi).astype(jnp.int32).reshape(1)
    me_index = (4 * xi + 2 * yi + ci).astype(jnp.int32).reshape(1)

    for nm in ("ffn1_w_gu", "ffn2_w_gu"):
        for pre in ("", "m_", "v_"):
            w[pre + nm] = jnp.swapaxes(w[pre + nm], 1, 2)

    wgut1 = w["ffn1_w_gu"][0].astype(BF16)
    wgut2 = w["ffn2_w_gu"][0].astype(BF16)
    wd1 = ffn1_w_down[0].astype(BF16)
    wd2 = ffn2_w_down[0].astype(BF16)
    win_loc = w_in[0].astype(BF16)
    win_t_loc = jnp.swapaxes(w_in[0], 0, 1).astype(BF16)
    w3_loc = jnp.concatenate([rnn_w_proj[0], conv_w_proj[0], w_out[0]], axis=0).astype(BF16)
    wg_loc = jnp.stack([rg_w_a[0], rg_w_x[0]]).astype(BF16)
    n_small = n_meta + k4 + k31
    small_rows = -(-n_small // SUBLANES) * SUBLANES
    small_loc = jnp.concatenate([meta_tokens, rnn_conv_w[0], conv_dw_w[0],
                                 jnp.zeros((small_rows - n_small, cw), F32)], axis=0)
    (wgut1_all, wd1_all, small_all), h0, tgt = _first_gather(
        [wgut1, wd1, small_loc], 2, x[0], loss_target[0], n_meta, tp)
    small_full = small_all.transpose(1, 0, 2).reshape(small_rows, d)
    cw4 = small_full[n_meta:n_meta + k4]
    cw31 = small_full[n_meta + k4:n_meta + k4 + k31]

    wgu1, wdn1 = wgut1_all.reshape(2 * f, d), wd1_all.reshape(f, d)
    (h1, gu1, n1), (win_all, w3_all, wg_all) = _ffn_fwd(
        h0, ffn1_norm, wgu1, wdn1, tm, comm=_Gather([win_loc, w3_loc, wg_loc], pass_on_at=(0.65, 0.95)))
    wg = wg_all.transpose(1, 2, 0, 3, 4).reshape(2, N_HEADS, hd, hd)
    (h2, proj, n2, xr_s, hs_s, v1_s, ya_s, yb_s), (wgut2_all, wd2_all, win_t_all) = _mixer_fwd(
        h1, mix_norm, b_in, win_all, cw4, rnn_conv_b, wg, rg_b_a, rg_b_x, rg_lambda, cw31, conv_dw_b, conv_ln_g,
        conv_ln_b, conv_b_proj, w3_all, tmx_fwd, comm=_Gather([wgut2, wd2, win_t_loc], pass_on_at=(0.45, 0.7)))
    wgu2, wdn2 = wgut2_all.reshape(2 * f, d), wd2_all.reshape(f, d)
    win_t = win_t_all.reshape(n_in, d)
    (dh3, gu2, n3, tail), _ = _ffn_fwd(
        h2, ffn2_norm, wgu2, wdn2, tm, loss=(tgt, final_norm.reshape(1, d), n_meta, t_real))

    def d_w_gu(tag, dgu, n_s, after=None):
        g, _ = _tn_matmul(
            "d_w_gu" + tag, dgu, n_s,
            pl.BlockSpec((None, tmt, fc), lambda b, i: (b // FFN_CHUNKS, i, b % FFN_CHUNKS)),
            pl.BlockSpec((tmt, d), lambda b, i: (i, 0)),
            2 * FFN_CHUNKS, fc, d, tmt, tp, (2 * FFN_CHUNKS, fc, d),
            pl.BlockSpec((None, fc, d), lambda b, i: (b, 0, 0)), (fc, d), after=after)
        return g.reshape(N_DEV, fb, d)

    def d_w_down(tag, act, df):
        g, _ = _tn_matmul(
            "d_w_down" + tag, act, df,
            pl.BlockSpec((tmt, fc), lambda b, i: (i, b)), pl.BlockSpec((tmt, d), lambda b, i: (i, 0)),
            FFN_CHUNKS, fc, d, tmt, tp, (FFN_CHUNKS, fc, d),
            pl.BlockSpec((None, fc, d), lambda b, i: (b, 0, 0)), (fc, d))
        return g.reshape(N_DEV, wr, d)

    (dh2, dgu2, act2, df2, tail), _ = _ffn_bwd(dh3, h2, gu2, ffn2_norm, wgu2, wdn2, tm, tail, TAIL_FFN2, n3)
    g_wgu2 = d_w_gu("2", dgu2, n3)
    g_wd2 = d_w_down("2", act2, df2)
    ((wd2_sems, g_wd2_thru, wd2_land), (wgu2_sems, g_wgu2_thru, wgu2_land)), ffn2_token = _exchanges_start(
        "grads_ffn2_exchange",
        [(_scatter_copies, 2 * (N_DEV - 1), g_wd2, (N_DEV - 1,) + g_wd2.shape[1:]),
         (_scatter_copies, 2 * (N_DEV - 1), g_wgu2, (N_DEV - 1,) + g_wgu2.shape[1:])])
    dh1, dproj, x3, y3, yg, sg = _mixer_bwd(
        dh2, h1, mix_norm, proj, xr_s, hs_s, v1_s, ya_s, yb_s, win_t, cw4, wg, rg_b_a, rg_b_x, rg_lambda, cw31,
        conv_ln_g, conv_ln_b, w3_all, tmx, ffn2_token)
    g_w3, _ = _tn_matmul(
        "d_w_proj3", x3, y3,
        pl.BlockSpec((tmw, d), lambda b, i: (i, b)), pl.BlockSpec((tmw, d), lambda b, i: (i, b)),
        3, d, d, tmw, tp, (N_DEV, 3, pr, d), pl.BlockSpec((N_DEV, None, pr, d), lambda b, i: (0, b, 0, 0)),
        (N_DEV, pr, d))
    g_wg, _ = _tn_matmul(
        "d_w_gates", xr_s, yg,
        pl.BlockSpec((tmw, hd), lambda b, i: (i, b % N_HEADS)), pl.BlockSpec((tmw, hd), lambda b, i: (i, b)),
        2 * N_HEADS, hd, hd, tmw, tp, (N_DEV, 2 * N_HEADS, gr, hd),
        pl.BlockSpec((N_DEV, None, gr, hd), lambda b, i: (0, b, 0, 0)), (N_DEV, gr, hd))
    ((sg_sems, sg_thru, sg_land), (wg_sems, g_wg_thru, wg_land), (w3_sems, g_w3_thru, w3_land)), w3_token = _exchanges_start(
        "grads_proj3_exchange",
        [(_bcast_copies, 2 * (N_DEV - 1) + 1, sg, (N_DEV,) + sg.shape),
         (_scatter_copies, 2 * (N_DEV - 1), g_wg, (N_DEV - 1,) + g_wg.shape[1:]),
         (_scatter_copies, 2 * (N_DEV - 1), g_w3, (N_DEV - 1,) + g_w3.shape[1:])])
    g_win, _ = _tn_matmul(
        "d_w_in", n2, dproj,
        pl.BlockSpec((tmw, d), lambda b, i: (i, 0)), pl.BlockSpec((tmw, nbc), lambda b, i: (i, b)),
        N_DEV, d, nbc, tmw, tp, (N_DEV, d, nbc), pl.BlockSpec((None, d, nbc), lambda b, i: (b, 0, 0)), (d, nbc),
        after=w3_token)
    win_sems, g_win_thru, win_land, win_token = _exchange_start("grads_w_in_exchange", _scatter_copies, N_DEV - 1, g_win)
    (dh0, dgu1, act1, df1, tail), _ = _ffn_bwd(dh1, h0, gu1, ffn1_norm, wgu1, wdn1, tm, tail, TAIL_FFN1, win_token)

    pieces = [sg, dh0[:n_meta], tail]
    assert all(p.shape[0] % SUBLANES == 0 for p in pieces)
    at = [0, sg.shape[0], sg.shape[0] + n_meta]
    loss_row = at[2] + TAIL_LOSS
    rep_rows = [("ffn1_norm", at[2] + TAIL_FFN1, 1), ("mix_norm", SG_MIX, 1), ("b_in", SG_BIN, 6),
                ("rnn_conv_b", SG_CB4, 1),
                ("rg_b_a", SG_BA, 1), ("rg_b_x", SG_BX, 1), ("rg_lambda", SG_LAM, 1), ("conv_dw_b", SG_CB31, 1),
                ("conv_ln_g", SG_LNG, 1), ("conv_ln_b", SG_LNB, 1), ("conv_b_proj", SG_BCP, 1),
                ("ffn2_norm", at[2] + TAIL_FFN2, 1), ("final_norm", at[2] + TAIL_FINAL, 1)]
    col_rows = [("meta_tokens", at[1], n_meta), ("rnn_conv_w", SG_CW4, k4), ("conv_dw_w", SG_CW31, k31)]
    layout = []
    for nm, row0, nr in rep_rows:
        kind = "wide" if nm == "b_in" else "rep"
        as2d = lambda a: a.reshape(1, -1) if a.ndim == 1 else a
        layout.append((kind, row0, nr, as2d(w[nm]), as2d(w["m_" + nm]), as2d(w["v_" + nm])))
    for nm, row0, nr in col_rows:
        sq = lambda a: a.reshape(a.shape[-2], a.shape[-1])
        layout.append(("col", row0, nr, sq(w[nm]), sq(w["m_" + nm]), sq(w["v_" + nm])))
    small_partial = jnp.concatenate(pieces[1:], axis=0)

    g_wd1 = d_w_down("1", act1, df1)
    ((small_sems, small_thru, small_land), (wd1_sems, g_wd1_thru, wd1_land)), wd1_token = _exchanges_start(
        "grads_w_down1_exchange",
        [(_bcast_copies, 2 * (N_DEV - 1) + 1, small_partial, (N_DEV,) + small_partial.shape),
         (_scatter_copies, 2 * (N_DEV - 1), g_wd1, (N_DEV - 1,) + g_wd1.shape[1:])])
    g_wgu1 = d_w_gu("1", dgu1, n1, after=wd1_token)

    g_last = g_wgu1.reshape((4, 2) + g_wgu1.shape[1:])
    comb_wgu1 = _pair_reduce(g_last, core)
    sems, comb_thru, land_thru, after = _exchange_start("grads_chip_exchange", _chip_copies, 3, comb_wgu1)
    g_win, r_win = _exchange_wait("grads_w_in_exchange", _scatter_copies, win_sems, g_win_thru, win_land, after)
    g_w3, r_w3 = _exchange_wait("grads_proj3_exchange", _scatter_copies, w3_sems, g_w3_thru, w3_land, after)
    g_wd1, r_wd1 = _exchange_wait("grads_w_down1_exchange", _scatter_copies, wd1_sems, g_wd1_thru, wd1_land, after)
    g_wd2, r_wd2 = _exchange_wait("grads_w_down2_exchange", _scatter_copies, wd2_sems, g_wd2_thru, wd2_land, after)
    g_wgu2, r_wgu2 = _exchange_wait("grads_w_gu2_exchange", _scatter_copies, wgu2_sems, g_wgu2_thru, wgu2_land, after)
    g_wg, r_wg = _exchange_wait("grads_gates_exchange", _scatter_copies, wg_sems, g_wg_thru, wg_land, after)
    _, sg_all = _exchange_wait("grads_mixer_small_exchange", _bcast_copies, sg_sems, sg_thru, sg_land, after)
    _, rest_all = _exchange_wait("grads_small_exchange", _bcast_copies, small_sems, small_thru, small_land, after)

    grad_x = (dh0[n_meta:t_real] + after[0, 0])[None]
    total, small_out = _small_adamw([sg_all, rest_all], layout, me_index, grad_x)
    after = total

    groups = [(g_wd1, r_wd1, me_index, ["ffn1_w_down"]),
              (g_wd2, r_wd2, me_index, ["ffn2_w_down"]), (g_wgu2, r_wgu2, me_index, ["ffn2_w_gu"]),
              (g_win, r_win, me_index, ["w_in"]), (g_w3, r_w3, me_index, ["w_out", "rnn_w_proj", "conv_w_proj"]),
              (g_wg, r_wg, me_index, ["rg_w_a", "rg_w_x"]), (None, None, chip, ["ffn1_w_gu"])]
    res = {}
    for own, recv, idx, group in groups:
        if own is None:
            own, recv = _exchange_wait("grads_chip_exchange", _chip_copies, sems, comb_thru, land_thru, after)
        outs = _final_adamw(own, recv, idx, [(w[nm], w["m_" + nm], w["v_" + nm]) for nm in group], after)
        after = outs[-1][0]
        for nm, o in zip(group, outs):
            res[nm] = o
    for nm in ("ffn1_w_gu", "ffn2_w_gu"):
        res[nm] = tuple(jnp.swapaxes(a, 1, 2) for a in res[nm])
    for (nm, _, _), o in zip(rep_rows + col_rows, small_out):
        res[nm] = tuple(a.reshape(w[nm].shape) for a in o)


    order = ["meta_tokens", "ffn1_norm", "ffn1_w_gu", "ffn1_w_down", "mix_norm", "w_in", "b_in", "rnn_conv_w",
             "rnn_conv_b", "rg_w_a", "rg_b_a", "rg_w_x", "rg_b_x", "rg_lambda", "rnn_w_proj", "conv_dw_w",
             "conv_dw_b", "conv_ln_g", "conv_ln_b", "conv_w_proj", "conv_b_proj", "w_out", "ffn2_norm",
             "ffn2_w_gu", "ffn2_w_down", "final_norm"]
    return (total[loss_row, 0], grad_x, *[res[nm][0] for nm in order], *[res[nm][1] for nm in order],
            *[res[nm][2] for nm in order], *[res[nm][3] for nm in order])
```
